```python
import math
import jax
import jax.numpy as jnp
from jax import lax
import numpy as np

D_MODEL = 1024
BATCH = 8
SEQ = 4096
DEPTH = 2

N_META = 16
D_FF = 4 * D_MODEL
GDN_HEADS = 4
GDN_DK = 128
GDN_DV = 128
GDN_CONV = 4
GDN_CHUNK = 64
SB_HEADS = 8
SB_DH = 64
SB_BLOCK = 128
HG_HEADS = 8
HG_DK = D_MODEL // HG_HEADS
HG_DV = D_MODEL // HG_HEADS
HG_CHUNK = 64

N_EVEN = (DEPTH + 1) // 2
N_ODD = DEPTH // 2
DN_ALPHA = float((2 * DEPTH) ** 0.25)
DN_BETA = float((8 * DEPTH) ** -0.25)
LN_EPS = 1e-5
RMS_EPS = 1e-6
L2_EPS = 1e-6

GDN_QK_W = GDN_HEADS * GDN_DK
GDN_V_W = GDN_HEADS * GDN_DV
SB_W = SB_HEADS * SB_DH
AB_SPLITS = (GDN_QK_W, GDN_QK_W, GDN_V_W, GDN_V_W, GDN_HEADS, GDN_HEADS, SB_W, SB_W, SB_W)
AB_IN = sum(AB_SPLITS)
AB_OUT_IN = GDN_V_W + SB_W
HG_K_W = HG_HEADS * HG_DK
HG_V_W = HG_HEADS * HG_DV
C_SPLITS = (HG_K_W, HG_K_W, HG_V_W, HG_V_W)
C_IN = sum(C_SPLITS)

kernel_name = "hybrid_gdn_stickbreak_hgrn2_deepnorm"


def _split(a, sizes):
    idx = np.cumsum(sizes)[:-1].tolist()
    return jnp.split(a, idx, axis=-1)


def _front_pad(a, pad):
    return jnp.pad(a, [(0, 0), (pad, 0)] + [(0, 0)] * (a.ndim - 2))


def _to_chunks(a, n, c):
    bn, _, h, d = a.shape
    return jnp.transpose(a.reshape(bn, n, c, h, d), (1, 0, 3, 2, 4))


def _from_chunks(o):
    n, bn, h, c, d = o.shape
    return jnp.transpose(o, (1, 0, 3, 2, 4)).reshape(bn, n * c, h, d)


def layer_norm(x, g, b):
    xf = x.astype(jnp.float32)
    mu = jnp.mean(xf, axis=-1, keepdims=True)
    var = jnp.mean(jnp.square(xf - mu), axis=-1, keepdims=True)
    y = (xf - mu) * lax.rsqrt(var + LN_EPS)
    return (y * g.astype(jnp.float32) + b.astype(jnp.float32)).astype(x.dtype)


def rms_norm(x, g):
    xf = x.astype(jnp.float32)
    y = xf * lax.rsqrt(jnp.mean(xf * xf, axis=-1, keepdims=True) + RMS_EPS)
    return y * g.astype(jnp.float32)


def l2_normalize(x):
    xf = x.astype(jnp.float32)
    return xf * lax.rsqrt(jnp.sum(xf * xf, axis=-1, keepdims=True) + L2_EPS)


def causal_depthwise_conv(x, w):
    k_w, ch = w.shape
    return lax.conv_general_dilated(
        x, w[:, None, :].astype(x.dtype), window_strides=(1,), padding=[(k_w - 1, 0)],
        dimension_numbers=("NWC", "WIO", "NWC"), feature_group_count=ch)


def gated_delta_rule_chunked(q, k, v, g, beta):
    bn, t_len, h, dk = q.shape
    dv = v.shape[-1]
    c = GDN_CHUNK
    n = t_len // c
    q, k, v = (_to_chunks(a, n, c) for a in (q, k, v))
    g, beta = (_to_chunks(a[..., None], n, c)[..., 0] for a in (g, beta))
    gc = jnp.cumsum(g, axis=-1)
    causal = jnp.tril(jnp.ones((c, c), dtype=bool))
    strict = jnp.tril(jnp.ones((c, c), dtype=bool), -1)
    decay = jnp.exp(jnp.where(causal, gc[..., :, None] - gc[..., None, :], -jnp.inf))
    kb = k * beta[..., None]
    m = jnp.where(strict, jnp.einsum("nbhid,nbhjd->nbhij", kb, k) * decay, 0.0)
    eye = jnp.broadcast_to(jnp.eye(c, dtype=jnp.float32), m.shape)
    t_inv = lax.linalg.triangular_solve(eye + m, eye, left_side=True, lower=True, unit_diagonal=True)
    u = jnp.einsum("nbhij,nbhjd->nbhid", t_inv, v * beta[..., None])
    w = jnp.einsum("nbhij,nbhjd->nbhid", t_inv, kb * jnp.exp(gc)[..., None])
    a_intra = jnp.einsum("nbhid,nbhjd->nbhij", q, k) * decay
    q_dec = q * jnp.exp(gc)[..., None]
    k_dec = k * jnp.exp(gc[..., -1:] - gc)[..., None]
    g_tot = jnp.exp(gc[..., -1])

    def step(s, xs):
        u_n, w_n, qd_n, kd_n, a_n, gt_n = xs
        v_new = u_n - jnp.einsum("bhcd,bhde->bhce", w_n, s)
        o_n = jnp.einsum("bhcd,bhde->bhce", qd_n, s) + jnp.einsum("bhij,bhje->bhie", a_n, v_new)
        s = s * gt_n[..., None, None] + jnp.einsum("bhcd,bhce->bhde", kd_n, v_new)
        return s, o_n

    s0 = jnp.zeros((bn, h, dk, dv), jnp.float32)
    _, o = lax.scan(step, s0, (u, w, q_dec, k_dec, a_intra, g_tot))
    return _from_chunks(o)


def stick_breaking_attention(q, k, v):
    bn, l_len, h, d = q.shape
    pad = (-l_len) % SB_BLOCK
    t_len = l_len + pad
    nb = t_len // SB_BLOCK
    to_bhtd = lambda a: jnp.moveaxis(_front_pad(a.astype(jnp.float32), pad), 1, 2)
    q, k, v = (to_bhtd(a) for a in (q, k, v))
    q = q * (d ** -0.5)
    key_pos = jnp.arange(t_len)
    q_blocks = jnp.moveaxis(q.reshape(bn, h, nb, SB_BLOCK, d), 2, 0)
    q_pos = jnp.arange(t_len).reshape(nb, SB_BLOCK)

    def block(args):
        qb, qp = args
        z = jnp.einsum("bhqd,bhkd->bhqk", qb, k)
        allowed = (key_pos[None, :] < qp[:, None]) & (key_pos[None, :] >= pad)
        log_1m = jnp.where(allowed, jax.nn.log_sigmoid(-z), 0.0)
        after = lax.cumsum(log_1m, axis=3, reverse=True) - log_1m
        wgt = jnp.where(allowed, jnp.exp(jax.nn.log_sigmoid(z) + after), 0.0)
        return jnp.einsum("bhqk,bhkd->bhqd", wgt, v)

    o = lax.map(block, (q_blocks, q_pos))
    o = jnp.moveaxis(o, 0, 2).reshape(bn, h, t_len, d)[:, :, pad:]
    return jnp.moveaxis(o, 1, 2).reshape(bn, l_len, h * d)


def hgrn2_chunked(q, k, v, logf):
    bn, t_len, h, dk = q.shape
    dv = v.shape[-1]
    c = HG_CHUNK
    n = t_len // c
    q, k, v, logf = (_to_chunks(a, n, c) for a in (q, k, v, logf))
    b = jnp.cumsum(logf, axis=3)
    causal = jnp.tril(jnp.ones((c, c), dtype=bool))[:, :, None]

    def step(s, xs):
        q_n, k_n, v_n, b_n = xs
        pair = jnp.exp(jnp.where(causal, b_n[:, :, :, None, :] - b_n[:, :, None, :, :], -jnp.inf))
        a_n = jnp.einsum("bhtc,bhsc,bhtsc->bhts", q_n, k_n, pair)
        b_last = b_n[:, :, -1:, :]
        o_n = (jnp.einsum("bhtc,bhce->bhte", q_n * jnp.exp(b_n), s)
               + jnp.einsum("bhts,bhse->bhte", a_n, v_n))
        s = (jnp.exp(b_last)[:, :, 0, :, None] * s
             + jnp.einsum("bhsc,bhse->bhce", k_n * jnp.exp(b_last - b_n), v_n))
        return s, o_n

    s0 = jnp.zeros((bn, h, dk, dv), jnp.float32)
    _, o = lax.scan(step, s0, (q, k, v, b))
    return _from_chunks(o)


def mixer_gdn_sb(h, w_in, conv_w, a_log, dt_bias, gnorm_g, w_out):
    bn, l_len, _ = h.shape
    qa, ka, va, za, ba, aa, qb, kb, vb = _split(h @ w_in, AB_SPLITS)
    qkv = jax.nn.silu(causal_depthwise_conv(jnp.concatenate([qa, ka, va], axis=-1), conv_w))
    qa, ka, va = _split(qkv, (GDN_QK_W, GDN_QK_W, GDN_V_W))
    qa = l2_normalize(qa.reshape(bn, l_len, GDN_HEADS, GDN_DK)) * (GDN_DK ** -0.5)
    ka = l2_normalize(ka.reshape(bn, l_len, GDN_HEADS, GDN_DK))
    va = va.reshape(bn, l_len, GDN_HEADS, GDN_DV).astype(jnp.float32)
    beta = jax.nn.sigmoid(ba.astype(jnp.float32))
    g = -jnp.exp(a_log.astype(jnp.float32)) * jax.nn.softplus(aa.astype(jnp.float32) + dt_bias.astype(jnp.float32))
    pad = (-l_len) % GDN_CHUNK
    oa = gated_delta_rule_chunked(*(_front_pad(a, pad) for a in (qa, ka, va, g, beta)))[:, pad:]
    oa = rms_norm(oa, gnorm_g) * jax.nn.silu(za.reshape(bn, l_len, GDN_HEADS, GDN_DV).astype(jnp.float32))
    oa = oa.reshape(bn, l_len, GDN_V_W)
    shp = (bn, l_len, SB_HEADS, SB_DH)
    ob = stick_breaking_attention(qb.reshape(shp), kb.reshape(shp), vb.reshape(shp))
    return jnp.concatenate([oa.astype(h.dtype), ob.astype(h.dtype)], axis=-1) @ w_out


def mixer_hgrn2(h, w_in, lb, gnorm_g, w_out):
    bn, l_len, _ = h.shape
    q, f, i, z = _split(h @ w_in, C_SPLITS)
    shp_k = (bn, l_len, HG_HEADS, HG_DK)
    shp_v = (bn, l_len, HG_HEADS, HG_DV)
    lbf = lb.astype(jnp.float32).reshape(HG_HEADS, HG_DK)
    fgate = lbf + (1.0 - lbf) * jax.nn.sigmoid(f.astype(jnp.float32).reshape(shp_k))
    logf = jnp.log(fgate)
    k = 1.0 - fgate
    q = jax.nn.silu(q.astype(jnp.float32)).reshape(shp_k)
    v = i.astype(jnp.float32).reshape(shp_v)
    pad = (-l_len) % HG_CHUNK
    o = hgrn2_chunked(*(_front_pad(a, pad) for a in (q, k, v, logf)))[:, pad:]
    o = rms_norm(o, gnorm_g) * jax.nn.silu(z.astype(jnp.float32).reshape(shp_v))
    return o.reshape(bn, l_len, HG_V_W).astype(h.dtype) @ w_out


def squared_relu_mlp(h, w1, w2):
    return jnp.square(jax.nn.relu(h @ w1)) @ w2


def _fwd_setup_inputs(seed: int = 0) -> dict:
    key = jax.random.key(seed)
    ks = jax.random.split(key, 18)
    nrm = lambda k, shape, scale: jax.random.normal(k, shape, jnp.float32) * scale
    x = nrm(ks[0], (BATCH, SEQ, D_MODEL), 1.0)
    meta_tokens = nrm(ks[1], (N_META, D_MODEL), 1.0)
    ab_w_in = nrm(ks[2], (N_EVEN, D_MODEL, AB_IN), D_MODEL ** -0.5)
    ab_conv_w = nrm(ks[3], (N_EVEN, GDN_CONV, 2 * GDN_QK_W + GDN_V_W), GDN_CONV ** -0.5)
    ab_a_log = jnp.log(jax.random.uniform(ks[4], (N_EVEN, GDN_HEADS), jnp.float32, 1.0, 16.0))
    dt = jnp.exp(jax.random.uniform(ks[5], (N_EVEN, GDN_HEADS), jnp.float32, math.log(1e-3), math.log(1e-1)))
    ab_dt_bias = dt + jnp.log(-jnp.expm1(-dt))
    ab_gnorm_g = 1.0 + nrm(ks[6], (N_EVEN, GDN_DV), 0.02)
    ab_w_out = nrm(ks[7], (N_EVEN, AB_OUT_IN, D_MODEL), (AB_OUT_IN ** -0.5) * DN_BETA)
    c_w_in = nrm(ks[8], (N_ODD, D_MODEL, C_IN), D_MODEL ** -0.5)
    c_lb_raw = nrm(ks[9], (DEPTH, HG_K_W), 0.1)
    c_gnorm_g = 1.0 + nrm(ks[10], (N_ODD, HG_DV), 0.02)
    c_w_out = nrm(ks[11], (N_ODD, HG_V_W, D_MODEL), (HG_V_W ** -0.5) * DN_BETA)
    ln_mix_g = 1.0 + nrm(ks[12], (DEPTH, D_MODEL), 0.02)
    ln_mix_b = nrm(ks[13], (DEPTH, D_MODEL), 0.02)
    mlp_w1 = nrm(ks[14], (DEPTH, D_MODEL, D_FF), D_MODEL ** -0.5)
    mlp_w2 = nrm(ks[15], (DEPTH, D_FF, D_MODEL), (D_FF ** -0.5) * DN_BETA)
    ln_ffn_g = 1.0 + nrm(ks[16], (DEPTH, D_MODEL), 0.02)
    ln_ffn_b = nrm(ks[17], (DEPTH, D_MODEL), 0.02)
    return {"x": x, "meta_tokens": meta_tokens, "ab_w_in": ab_w_in, "ab_conv_w": ab_conv_w,
            "ab_a_log": ab_a_log, "ab_dt_bias": ab_dt_bias, "ab_gnorm_g": ab_gnorm_g, "ab_w_out": ab_w_out,
            "c_w_in": c_w_in, "c_lb_raw": c_lb_raw, "c_gnorm_g": c_gnorm_g, "c_w_out": c_w_out,
            "ln_mix_g": ln_mix_g, "ln_mix_b": ln_mix_b, "mlp_w1": mlp_w1, "mlp_w2": mlp_w2,
            "ln_ffn_g": ln_ffn_g, "ln_ffn_b": ln_ffn_b}


def _fwd_reference(x, meta_tokens, ab_w_in, ab_conv_w, ab_a_log, ab_dt_bias, ab_gnorm_g, ab_w_out,
              c_w_in, c_lb_raw, c_gnorm_g, c_w_out, ln_mix_g, ln_mix_b, mlp_w1, mlp_w2,
              ln_ffn_g, ln_ffn_b):
    lb_all = jnp.cumsum(jax.nn.softmax(c_lb_raw.astype(jnp.float32), axis=0), axis=0)
    lb_all = lb_all - lb_all[0:1]
    meta = jnp.broadcast_to(meta_tokens[None].astype(x.dtype), (x.shape[0], N_META, D_MODEL))
    h = jnp.concatenate([meta, x], axis=1)
    for layer in range(DEPTH):
        j = layer // 2
        if layer % 2 == 0:
            mix = mixer_gdn_sb(h, ab_w_in[j], ab_conv_w[j], ab_a_log[j], ab_dt_bias[j], ab_gnorm_g[j], ab_w_out[j])
        else:
            mix = mixer_hgrn2(h, c_w_in[j], lb_all[layer], c_gnorm_g[j], c_w_out[j])
        h = layer_norm(DN_ALPHA * h + mix, ln_mix_g[layer], ln_mix_b[layer])
        h = layer_norm(DN_ALPHA * h + squared_relu_mlp(h, mlp_w1[layer], mlp_w2[layer]),
                       ln_ffn_g[layer], ln_ffn_b[layer])
    return h[:, N_META:]


import jax as _jax
import jax.numpy as _jnp

TWIN_FORMAT = 'train_step'
FWD_PARAMS = ['x', 'meta_tokens', 'ab_w_in', 'ab_conv_w', 'ab_a_log', 'ab_dt_bias', 'ab_gnorm_g', 'ab_w_out', 'c_w_in', 'c_lb_raw', 'c_gnorm_g', 'c_w_out', 'ln_mix_g', 'ln_mix_b', 'mlp_w1', 'mlp_w2', 'ln_ffn_g', 'ln_ffn_b']
TWIN_WEIGHTS = ['meta_tokens', 'ab_w_in', 'ab_conv_w', 'ab_a_log', 'ab_dt_bias', 'ab_gnorm_g', 'ab_w_out', 'c_w_in', 'c_lb_raw', 'c_gnorm_g', 'c_w_out', 'ln_mix_g', 'ln_mix_b', 'mlp_w1', 'mlp_w2', 'ln_ffn_g', 'ln_ffn_b']
TWIN_DIFF_INPUT = 'x'
TWIN_INPUTS = ['x', 'meta_tokens', 'ab_w_in', 'ab_conv_w', 'ab_a_log', 'ab_dt_bias', 'ab_gnorm_g', 'ab_w_out', 'c_w_in', 'c_lb_raw', 'c_gnorm_g', 'c_w_out', 'ln_mix_g', 'ln_mix_b', 'mlp_w1', 'mlp_w2', 'ln_ffn_g', 'ln_ffn_b', 'loss_target', 'm_meta_tokens', 'm_ab_w_in', 'm_ab_conv_w', 'm_ab_a_log', 'm_ab_dt_bias', 'm_ab_gnorm_g', 'm_ab_w_out', 'm_c_w_in', 'm_c_lb_raw', 'm_c_gnorm_g', 'm_c_w_out', 'm_ln_mix_g', 'm_ln_mix_b', 'm_mlp_w1', 'm_mlp_w2', 'm_ln_ffn_g', 'm_ln_ffn_b', 'v_meta_tokens', 'v_ab_w_in', 'v_ab_conv_w', 'v_ab_a_log', 'v_ab_dt_bias', 'v_ab_gnorm_g', 'v_ab_w_out', 'v_c_w_in', 'v_c_lb_raw', 'v_c_gnorm_g', 'v_c_w_out', 'v_ln_mix_g', 'v_ln_mix_b', 'v_mlp_w1', 'v_mlp_w2', 'v_ln_ffn_g', 'v_ln_ffn_b']
TWIN_OUTPUTS = ['loss', 'grad_x', 'grad_meta_tokens', 'grad_ab_w_in', 'grad_ab_conv_w', 'grad_ab_a_log', 'grad_ab_dt_bias', 'grad_ab_gnorm_g', 'grad_ab_w_out', 'grad_c_w_in', 'grad_c_lb_raw', 'grad_c_gnorm_g', 'grad_c_w_out', 'grad_ln_mix_g', 'grad_ln_mix_b', 'grad_mlp_w1', 'grad_mlp_w2', 'grad_ln_ffn_g', 'grad_ln_ffn_b', 'delta_meta_tokens', 'delta_ab_w_in', 'delta_ab_conv_w', 'delta_ab_a_log', 'delta_ab_dt_bias', 'delta_ab_gnorm_g', 'delta_ab_w_out', 'delta_c_w_in', 'delta_c_lb_raw', 'delta_c_gnorm_g', 'delta_c_w_out', 'delta_ln_mix_g', 'delta_ln_mix_b', 'delta_mlp_w1', 'delta_mlp_w2', 'delta_ln_ffn_g', 'delta_ln_ffn_b', 'new_m_meta_tokens', 'new_m_ab_w_in', 'new_m_ab_conv_w', 'new_m_ab_a_log', 'new_m_ab_dt_bias', 'new_m_ab_gnorm_g', 'new_m_ab_w_out', 'new_m_c_w_in', 'new_m_c_lb_raw', 'new_m_c_gnorm_g', 'new_m_c_w_out', 'new_m_ln_mix_g', 'new_m_ln_mix_b', 'new_m_mlp_w1', 'new_m_mlp_w2', 'new_m_ln_ffn_g', 'new_m_ln_ffn_b', 'new_v_meta_tokens', 'new_v_ab_w_in', 'new_v_ab_conv_w', 'new_v_ab_a_log', 'new_v_ab_dt_bias', 'new_v_ab_gnorm_g', 'new_v_ab_w_out', 'new_v_c_w_in', 'new_v_c_lb_raw', 'new_v_c_gnorm_g', 'new_v_c_w_out', 'new_v_ln_mix_g', 'new_v_ln_mix_b', 'new_v_mlp_w1', 'new_v_mlp_w2', 'new_v_ln_ffn_g', 'new_v_ln_ffn_b']
TWIN_LEAF_KINDS = {'loss': 'loss', 'grad_x': 'grad_x', 'grad_meta_tokens': 'grad_w', 'grad_ab_w_in': 'grad_w', 'grad_ab_conv_w': 'grad_w', 'grad_ab_a_log': 'grad_w', 'grad_ab_dt_bias': 'grad_w', 'grad_ab_gnorm_g': 'grad_w', 'grad_ab_w_out': 'grad_w', 'grad_c_w_in': 'grad_w', 'grad_c_lb_raw': 'grad_w', 'grad_c_gnorm_g': 'grad_w', 'grad_c_w_out': 'grad_w', 'grad_ln_mix_g': 'grad_w', 'grad_ln_mix_b': 'grad_w', 'grad_mlp_w1': 'grad_w', 'grad_mlp_w2': 'grad_w', 'grad_ln_ffn_g': 'grad_w', 'grad_ln_ffn_b': 'grad_w', 'delta_meta_tokens': 'delta_w', 'delta_ab_w_in': 'delta_w', 'delta_ab_conv_w': 'delta_w', 'delta_ab_a_log': 'delta_w', 'delta_ab_dt_bias': 'delta_w', 'delta_ab_gnorm_g': 'delta_w', 'delta_ab_w_out': 'delta_w', 'delta_c_w_in': 'delta_w', 'delta_c_lb_raw': 'delta_w', 'delta_c_gnorm_g': 'delta_w', 'delta_c_w_out': 'delta_w', 'delta_ln_mix_g': 'delta_w', 'delta_ln_mix_b': 'delta_w', 'delta_mlp_w1': 'delta_w', 'delta_mlp_w2': 'delta_w', 'delta_ln_ffn_g': 'delta_w', 'delta_ln_ffn_b': 'delta_w', 'new_m_meta_tokens': 'new_m', 'new_m_ab_w_in': 'new_m', 'new_m_ab_conv_w': 'new_m', 'new_m_ab_a_log': 'new_m', 'new_m_ab_dt_bias': 'new_m', 'new_m_ab_gnorm_g': 'new_m', 'new_m_ab_w_out': 'new_m', 'new_m_c_w_in': 'new_m', 'new_m_c_lb_raw': 'new_m', 'new_m_c_gnorm_g': 'new_m', 'new_m_c_w_out': 'new_m', 'new_m_ln_mix_g': 'new_m', 'new_m_ln_mix_b': 'new_m', 'new_m_mlp_w1': 'new_m', 'new_m_mlp_w2': 'new_m', 'new_m_ln_ffn_g': 'new_m', 'new_m_ln_ffn_b': 'new_m', 'new_v_meta_tokens': 'new_v', 'new_v_ab_w_in': 'new_v', 'new_v_ab_conv_w': 'new_v', 'new_v_ab_a_log': 'new_v', 'new_v_ab_dt_bias': 'new_v', 'new_v_ab_gnorm_g': 'new_v', 'new_v_ab_w_out': 'new_v', 'new_v_c_w_in': 'new_v', 'new_v_c_lb_raw': 'new_v', 'new_v_c_gnorm_g': 'new_v', 'new_v_c_w_out': 'new_v', 'new_v_ln_mix_g': 'new_v', 'new_v_ln_mix_b': 'new_v', 'new_v_mlp_w1': 'new_v', 'new_v_mlp_w2': 'new_v', 'new_v_ln_ffn_g': 'new_v', 'new_v_ln_ffn_b': 'new_v'}


def _forward(args):
    return _fwd_reference(*[args[k] for k in FWD_PARAMS])


def _output_shape():
    out = _jax.eval_shape(lambda: _forward(_fwd_setup_inputs(0)))
    return out.shape, out.dtype

N_MICROBATCH = 1
ADAM_LR = 0.001
ADAM_B1 = 0.9
ADAM_B2 = 0.999
ADAM_EPS = 1e-08
ADAM_WD = 0.01
ADAM_STEP = 10
PER_EXAMPLE_BATCH_AXIS = {'x': 0, 'loss_target': 0}
SHARED_INPUTS = []
_WEIGHT_DTYPES = {'meta_tokens': _jnp.float32, 'ab_w_in': _jnp.float32, 'ab_conv_w': _jnp.float32, 'ab_a_log': _jnp.float32, 'ab_dt_bias': _jnp.float32, 'ab_gnorm_g': _jnp.float32, 'ab_w_out': _jnp.float32, 'c_w_in': _jnp.float32, 'c_lb_raw': _jnp.float32, 'c_gnorm_g': _jnp.float32, 'c_w_out': _jnp.float32, 'ln_mix_g': _jnp.float32, 'ln_mix_b': _jnp.float32, 'mlp_w1': _jnp.float32, 'mlp_w2': _jnp.float32, 'ln_ffn_g': _jnp.float32, 'ln_ffn_b': _jnp.float32}
MOMENT_SCALE = {'meta_tokens': 1.248824e-03, 'ab_w_in': 3.118205e-02, 'ab_conv_w': 3.019386e-02, 'ab_a_log': 1.627507e-01, 'ab_dt_bias': 1.534142e-01, 'ab_gnorm_g': 8.202142e-02, 'ab_w_out': 8.276610e-02, 'c_w_in': 2.784585e-02, 'c_lb_raw': 3.160158e-03, 'c_gnorm_g': 1.156196e-01, 'c_w_out': 7.813284e-02, 'ln_mix_g': 7.248545e-01, 'ln_mix_b': 4.563655e-01, 'mlp_w1': 4.184097e-02, 'mlp_w2': 1.906929e-01, 'ln_ffn_g': 2.269723e+01, 'ln_ffn_b': 5.063288e+00}


def _to_microbatches(a, axis):
    t = _jnp.moveaxis(a, axis, 0)
    t = t.reshape((N_MICROBATCH, t.shape[0] // N_MICROBATCH) + t.shape[1:])
    return _jnp.moveaxis(t, 1, axis + 1)


def setup_inputs(seed: int = 0) -> dict:
    inp = _fwd_setup_inputs(seed)
    key = _jax.random.fold_in(_jax.random.key(seed), 7919)
    shape, _ = _output_shape()
    out = dict(inp)
    out["loss_target"] = _jax.random.normal(_jax.random.fold_in(key, 0), shape, _jnp.float32)
    for i, name in enumerate(TWIN_WEIGHTS):
        w = inp[name].astype(_jnp.float32)
        if MOMENT_SCALE is None:
            s = _jnp.sqrt(_jnp.mean(_jnp.square(w)) + 1e-30)
        else:
            s = MOMENT_SCALE[name]
        km, kv = _jax.random.split(_jax.random.fold_in(key, i + 1))
        out[name] = w
        out["m_" + name] = s * _jax.random.normal(km, w.shape, _jnp.float32)
        out["v_" + name] = (s * s) * _jax.random.uniform(kv, w.shape, _jnp.float32, 0.5, 1.5)
    if N_MICROBATCH > 1:
        for name, axis in PER_EXAMPLE_BATCH_AXIS.items():
            out[name] = _to_microbatches(out[name], axis)
    return {'x': out['x'], 'meta_tokens': out['meta_tokens'], 'ab_w_in': out['ab_w_in'], 'ab_conv_w': out['ab_conv_w'], 'ab_a_log': out['ab_a_log'], 'ab_dt_bias': out['ab_dt_bias'], 'ab_gnorm_g': out['ab_gnorm_g'], 'ab_w_out': out['ab_w_out'], 'c_w_in': out['c_w_in'], 'c_lb_raw': out['c_lb_raw'], 'c_gnorm_g': out['c_gnorm_g'], 'c_w_out': out['c_w_out'], 'ln_mix_g': out['ln_mix_g'], 'ln_mix_b': out['ln_mix_b'], 'mlp_w1': out['mlp_w1'], 'mlp_w2': out['mlp_w2'], 'ln_ffn_g': out['ln_ffn_g'], 'ln_ffn_b': out['ln_ffn_b'], 'loss_target': out['loss_target'], 'm_meta_tokens': out['m_meta_tokens'], 'm_ab_w_in': out['m_ab_w_in'], 'm_ab_conv_w': out['m_ab_conv_w'], 'm_ab_a_log': out['m_ab_a_log'], 'm_ab_dt_bias': out['m_ab_dt_bias'], 'm_ab_gnorm_g': out['m_ab_gnorm_g'], 'm_ab_w_out': out['m_ab_w_out'], 'm_c_w_in': out['m_c_w_in'], 'm_c_lb_raw': out['m_c_lb_raw'], 'm_c_gnorm_g': out['m_c_gnorm_g'], 'm_c_w_out': out['m_c_w_out'], 'm_ln_mix_g': out['m_ln_mix_g'], 'm_ln_mix_b': out['m_ln_mix_b'], 'm_mlp_w1': out['m_mlp_w1'], 'm_mlp_w2': out['m_mlp_w2'], 'm_ln_ffn_g': out['m_ln_ffn_g'], 'm_ln_ffn_b': out['m_ln_ffn_b'], 'v_meta_tokens': out['v_meta_tokens'], 'v_ab_w_in': out['v_ab_w_in'], 'v_ab_conv_w': out['v_ab_conv_w'], 'v_ab_a_log': out['v_ab_a_log'], 'v_ab_dt_bias': out['v_ab_dt_bias'], 'v_ab_gnorm_g': out['v_ab_gnorm_g'], 'v_ab_w_out': out['v_ab_w_out'], 'v_c_w_in': out['v_c_w_in'], 'v_c_lb_raw': out['v_c_lb_raw'], 'v_c_gnorm_g': out['v_c_gnorm_g'], 'v_c_w_out': out['v_c_w_out'], 'v_ln_mix_g': out['v_ln_mix_g'], 'v_ln_mix_b': out['v_ln_mix_b'], 'v_mlp_w1': out['v_mlp_w1'], 'v_mlp_w2': out['v_mlp_w2'], 'v_ln_ffn_g': out['v_ln_ffn_g'], 'v_ln_ffn_b': out['v_ln_ffn_b']}


def _loss(weights, diff, rest, loss_target):
    with _jax.named_scope("forward"):
        args = {**rest, TWIN_DIFF_INPUT: diff, **{k: w.astype(_WEIGHT_DTYPES[k]) for k, w in weights.items()}}
        y = _forward(args)
    with _jax.named_scope("loss_head"):
        err = _jnp.square(y.astype(_jnp.float32) - loss_target)
        return 0.5 * _jnp.sum(_jnp.mean(err, axis=-1)) if err.ndim else 0.5 * err


def _adamw(w, g, m, v):
    m = ADAM_B1 * m + (1.0 - ADAM_B1) * g
    v = ADAM_B2 * v + (1.0 - ADAM_B2) * _jnp.square(g)
    m_hat = m / (1.0 - ADAM_B1 ** ADAM_STEP)
    v_hat = v / (1.0 - ADAM_B2 ** ADAM_STEP)
    delta = -ADAM_LR * (m_hat / (_jnp.sqrt(v_hat) + ADAM_EPS) + ADAM_WD * w)
    return delta, m, v


def reference(x, meta_tokens, ab_w_in, ab_conv_w, ab_a_log, ab_dt_bias, ab_gnorm_g, ab_w_out, c_w_in, c_lb_raw, c_gnorm_g, c_w_out, ln_mix_g, ln_mix_b, mlp_w1, mlp_w2, ln_ffn_g, ln_ffn_b, loss_target, m_meta_tokens, m_ab_w_in, m_ab_conv_w, m_ab_a_log, m_ab_dt_bias, m_ab_gnorm_g, m_ab_w_out, m_c_w_in, m_c_lb_raw, m_c_gnorm_g, m_c_w_out, m_ln_mix_g, m_ln_mix_b, m_mlp_w1, m_mlp_w2, m_ln_ffn_g, m_ln_ffn_b, v_meta_tokens, v_ab_w_in, v_ab_conv_w, v_ab_a_log, v_ab_dt_bias, v_ab_gnorm_g, v_ab_w_out, v_c_w_in, v_c_lb_raw, v_c_gnorm_g, v_c_w_out, v_ln_mix_g, v_ln_mix_b, v_mlp_w1, v_mlp_w2, v_ln_ffn_g, v_ln_ffn_b):
    given = dict(x=x, meta_tokens=meta_tokens, ab_w_in=ab_w_in, ab_conv_w=ab_conv_w, ab_a_log=ab_a_log, ab_dt_bias=ab_dt_bias, ab_gnorm_g=ab_gnorm_g, ab_w_out=ab_w_out, c_w_in=c_w_in, c_lb_raw=c_lb_raw, c_gnorm_g=c_gnorm_g, c_w_out=c_w_out, ln_mix_g=ln_mix_g, ln_mix_b=ln_mix_b, mlp_w1=mlp_w1, mlp_w2=mlp_w2, ln_ffn_g=ln_ffn_g, ln_ffn_b=ln_ffn_b, loss_target=loss_target, m_meta_tokens=m_meta_tokens, m_ab_w_in=m_ab_w_in, m_ab_conv_w=m_ab_conv_w, m_ab_a_log=m_ab_a_log, m_ab_dt_bias=m_ab_dt_bias, m_ab_gnorm_g=m_ab_gnorm_g, m_ab_w_out=m_ab_w_out, m_c_w_in=m_c_w_in, m_c_lb_raw=m_c_lb_raw, m_c_gnorm_g=m_c_gnorm_g, m_c_w_out=m_c_w_out, m_ln_mix_g=m_ln_mix_g, m_ln_mix_b=m_ln_mix_b, m_mlp_w1=m_mlp_w1, m_mlp_w2=m_mlp_w2, m_ln_ffn_g=m_ln_ffn_g, m_ln_ffn_b=m_ln_ffn_b, v_meta_tokens=v_meta_tokens, v_ab_w_in=v_ab_w_in, v_ab_conv_w=v_ab_conv_w, v_ab_a_log=v_ab_a_log, v_ab_dt_bias=v_ab_dt_bias, v_ab_gnorm_g=v_ab_gnorm_g, v_ab_w_out=v_ab_w_out, v_c_w_in=v_c_w_in, v_c_lb_raw=v_c_lb_raw, v_c_gnorm_g=v_c_gnorm_g, v_c_w_out=v_c_w_out, v_ln_mix_g=v_ln_mix_g, v_ln_mix_b=v_ln_mix_b, v_mlp_w1=v_mlp_w1, v_mlp_w2=v_mlp_w2, v_ln_ffn_g=v_ln_ffn_g, v_ln_ffn_b=v_ln_ffn_b)
    weights = {n: given[n] for n in TWIN_WEIGHTS}
    shared = {n: given[n] for n in SHARED_INPUTS}
    per_example = {n: given[n] for n in ['x']}
    grad_fn = _jax.value_and_grad(_loss, argnums=(0, 1))

    def one_microbatch(ex, loss_target):
        ex = dict(ex)
        diff = ex.pop(TWIN_DIFF_INPUT)
        return grad_fn(weights, diff, {**shared, **ex}, loss_target)

    if N_MICROBATCH == 1:
        loss, (grad_w, grad_x) = one_microbatch(per_example, given["loss_target"])
    else:
        def body(carry, xs):
            loss_sum, grad_sum = carry
            l_k, (gw_k, gx_k) = one_microbatch(xs[0], xs[1])
            with _jax.named_scope("update"):
                return (loss_sum + l_k, _jax.tree.map(_jnp.add, grad_sum, gw_k)), gx_k

        init = (_jnp.zeros((), _jnp.float32), _jax.tree.map(_jnp.zeros_like, weights))
        (loss, grad_w), grad_x = _jax.lax.scan(body, init, (per_example, given["loss_target"]))
    with _jax.named_scope("update"):
        delta_w, new_m, new_v = {}, {}, {}
        for n in TWIN_WEIGHTS:
            delta_w[n], new_m[n], new_v[n] = _adamw(weights[n], grad_w[n], given["m_" + n], given["v_" + n])
    return (loss, grad_x, *[grad_w[n] for n in TWIN_WEIGHTS], *[delta_w[n] for n in TWIN_WEIGHTS],
            *[new_m[n] for n in TWIN_WEIGHTS], *[new_v[n] for n in TWIN_WEIGHTS])
```

```python
import functools
import math

import numpy as np
import jax
import jax.numpy as jnp
from jax import lax
from jax.experimental import pallas as pl
from jax.experimental.pallas import tpu as pltpu

F32 = jnp.float32
BF16 = jnp.bfloat16

D = 1024
N_META = 16
D_FF = 4 * D
DEPTH = 2
GDN_H = 4
SB_H = 8
SB_DH = 64
HG_H = 8
HD = 128
CH = 64
QB = 128
ALPHA = float((2 * DEPTH) ** 0.25)
LN_EPS = 1e-5
RMS_EPS = 1e-6
L2_EPS = 1e-6
NEG = -1e30

ADAM_LR = 0.001
ADAM_B1 = 0.9
ADAM_B2 = 0.999
ADAM_EPS = 1e-08
ADAM_WD = 0.01
ADAM_STEP = 10

AB_W = 30 * HD
AB_TRUE = 3592
VMEM_LIMIT = 56 * 1024 * 1024

NN = ((1,), (0,))
NT = ((1,), (1,))
TN = ((0,), (0,))


def _cparams(**kw):
    return pltpu.CompilerParams(vmem_limit_bytes=VMEM_LIMIT, **kw)


def _dg(a, b, dims, hi):
    if hi:
        return lax.dot_general(a, b, (dims, ((), ())), precision=lax.Precision.HIGHEST,
                               preferred_element_type=F32)
    return lax.dot_general(a.astype(BF16), b.astype(BF16), (dims, ((), ())), preferred_element_type=F32)


def _make_dots(hi):
    @jax.custom_vjp
    def nn(a, b):
        return _dg(a, b, NN, hi)

    @jax.custom_vjp
    def nt(a, b):
        return _dg(a, b, NT, hi)

    @jax.custom_vjp
    def tn(a, b):
        return _dg(a, b, TN, hi)

    nn.defvjp(lambda a, b: (nn(a, b), (a, b)), lambda r, g: (nt(g, r[1]), tn(r[0], g)))
    nt.defvjp(lambda a, b: (nt(a, b), (a, b)), lambda r, g: (nn(g, r[1]), tn(g, r[0])))
    tn.defvjp(lambda a, b: (tn(a, b), (a, b)), lambda r, g: (nt(r[1], g), nn(r[0], g)))
    return nn, nt, tn


bnn, bnt, btn = _make_dots(False)
hnn, hnt, htn = _make_dots(True)


def _split2(a, b16, dims):
    hi = a.astype(BF16)
    lo = (a - hi.astype(F32)).astype(BF16)
    return (lax.dot_general(hi, b16, (dims, ((), ())), preferred_element_type=F32)
            + lax.dot_general(lo, b16, (dims, ((), ())), preferred_element_type=F32))


def _sigmoid(x):
    return jax.nn.sigmoid(x)


def _silu(x):
    return x * jax.nn.sigmoid(x)


def _softplus(x):
    return jnp.maximum(x, 0.0) + jnp.log(1.0 + jnp.exp(-jnp.abs(x)))


def _iota(shape, dim):
    return lax.broadcasted_iota(jnp.int32, shape, dim)


def _pick(n, prefs):
    for p in prefs:
        if n % p == 0:
            return p
    return n


def _mm(a, b, *, ta=False, tb=False, out_dtype=F32, name):
    if ta:
        k_dim, m_dim = a.shape
    else:
        m_dim, k_dim = a.shape
    n_dim = b.shape[0] if tb else b.shape[1]
    assert (b.shape[1] if tb else b.shape[0]) == k_dim
    tm = _pick(m_dim, (1024, 1056, 704, 640, 512, 384, 256, 128))
    tn = _pick(n_dim, (1024, 1056, 704, 640, 512, 384, 256, 128))
    tk = _pick(k_dim, (1024, 1056, 704, 512, 384, 256, 128))
    nk = k_dim // tk
    a_spec = pl.BlockSpec((tk, tm), lambda i, j, k: (k, i)) if ta else pl.BlockSpec((tm, tk), lambda i, j, k: (i, k))
    b_spec = pl.BlockSpec((tn, tk), lambda i, j, k: (j, k)) if tb else pl.BlockSpec((tk, tn), lambda i, j, k: (k, j))
    dims = (((0 if ta else 1,), (1 if tb else 0,)), ((), ()))

    def body(a_ref, b_ref, o_ref, acc_ref):
        part = lax.dot_general(a_ref[...], b_ref[...], dims, preferred_element_type=F32)
        if nk == 1:
            o_ref[...] = part.astype(o_ref.dtype)
        else:
            k = pl.program_id(2)

            @pl.when(k == 0)
            def _():
                acc_ref[...] = part

            @pl.when(k > 0)
            def _():
                acc_ref[...] += part

            @pl.when(k == nk - 1)
            def _():
                o_ref[...] = acc_ref[...].astype(o_ref.dtype)

    return pl.pallas_call(
        body, name=name, grid=(m_dim // tm, n_dim // tn, nk),
        in_specs=[a_spec, b_spec],
        out_specs=pl.BlockSpec((tm, tn), lambda i, j, k: (i, j)),
        out_shape=jax.ShapeDtypeStruct((m_dim, n_dim), out_dtype),
        scratch_shapes=[pltpu.VMEM((tm, tn) if nk > 1 else (8, 128), F32)],
        compiler_params=_cparams(dimension_semantics=("parallel", "parallel", "arbitrary")),
    )(a, b)


def _row_tile(t_pad, width):
    for tr in (528, 352, 176, 128, 64):
        if t_pad % tr == 0 and tr * width * 4 <= (3 << 19) and tr % 16 == 0:
            return tr
    return 64 if t_pad % 64 == 0 else t_pad


def _ln_res_fn(h, m, g, b):
    x = ALPHA * h + m
    mu = jnp.mean(x, axis=-1, keepdims=True)
    xc = x - mu
    var = jnp.mean(xc * xc, axis=-1, keepdims=True)
    return xc * lax.rsqrt(var + LN_EPS) * g + b


def _ln_res_fwd(h, m, g, b, name):
    t_pad = h.shape[0]
    tr = _row_tile(t_pad, D)

    def body(h_ref, m_ref, g_ref, b_ref, y_ref, yb_ref):
        y = _ln_res_fn(h_ref[...], m_ref[...], g_ref[...], b_ref[...])
        y_ref[...] = y
        yb_ref[...] = y.astype(BF16)

    row = pl.BlockSpec((tr, D), lambda i: (i, 0))
    par = pl.BlockSpec((1, D), lambda i: (0, 0))
    return pl.pallas_call(
        body, name=name, grid=(t_pad // tr,), in_specs=[row, row, par, par], out_specs=[row, row],
        out_shape=[jax.ShapeDtypeStruct((t_pad, D), F32), jax.ShapeDtypeStruct((t_pad, D), BF16)],
        compiler_params=_cparams(),
    )(h, m, g, b)


def _ln_res_bwd(h, m, g, b, dy1, dy2, name):
    t_pad = h.shape[0]
    tr = _row_tile(t_pad, D)

    def body(h_ref, m_ref, g_ref, b_ref, d1_ref, d2_ref, dh_ref, dm_ref, dg_ref, db_ref):
        _, vjp = jax.vjp(_ln_res_fn, h_ref[...], m_ref[...], g_ref[...], b_ref[...])
        dh, dm, dg, db = vjp(d1_ref[...] + d2_ref[...])
        dh_ref[...] = dh
        dm_ref[...] = dm.astype(BF16)

        @pl.when(pl.program_id(0) == 0)
        def _():
            dg_ref[...] = jnp.zeros_like(dg_ref)
            db_ref[...] = jnp.zeros_like(db_ref)

        dg_ref[...] += dg
        db_ref[...] += db

    row = pl.BlockSpec((tr, D), lambda i: (i, 0))
    par = pl.BlockSpec((1, D), lambda i: (0, 0))
    return pl.pallas_call(
        body, name=name, grid=(t_pad // tr,), in_specs=[row, row, par, par, row, row],
        out_specs=[row, row, par, par],
        out_shape=[jax.ShapeDtypeStruct((t_pad, D), F32), jax.ShapeDtypeStruct((t_pad, D), BF16),
                   jax.ShapeDtypeStruct((1, D), F32), jax.ShapeDtypeStruct((1, D), F32)],
        compiler_params=_cparams(),
    )(h, m, g, b, dy1, dy2)


def _relu2_fwd(a, name):
    t_pad, w = a.shape
    tr = _row_tile(t_pad, w)

    def body(a_ref, r_ref):
        r = jnp.maximum(a_ref[...], 0.0)
        r_ref[...] = (r * r).astype(BF16)

    row = pl.BlockSpec((tr, w), lambda i: (i, 0))
    return pl.pallas_call(body, name=name, grid=(t_pad // tr,), in_specs=[row], out_specs=row,
                          out_shape=jax.ShapeDtypeStruct((t_pad, w), BF16), compiler_params=_cparams())(a)


def _relu2_bwd(a, dr, name):
    t_pad, w = a.shape
    tr = _row_tile(t_pad, w)

    def body(a_ref, dr_ref, da_ref):
        da_ref[...] = (dr_ref[...] * (2.0 * jnp.maximum(a_ref[...], 0.0))).astype(BF16)

    row = pl.BlockSpec((tr, w), lambda i: (i, 0))
    return pl.pallas_call(body, name=name, grid=(t_pad // tr,), in_specs=[row, row], out_specs=row,
                          out_shape=jax.ShapeDtypeStruct((t_pad, w), BF16), compiler_params=_cparams())(a, dr)


def _grms_fn(o, z, g):
    y = o * lax.rsqrt(jnp.mean(o * o, axis=-1, keepdims=True) + RMS_EPS) * g
    return y * _silu(z)


def _grms_fwd(o, z_arr, z_blk0, g, name):
    t_pad, w = o.shape
    nh = w // HD
    tr = _row_tile(t_pad, HD * 4)

    def body(o_ref, z_ref, g_ref, y_ref):
        y_ref[...] = _grms_fn(o_ref[...], z_ref[...], g_ref[...]).astype(BF16)

    return pl.pallas_call(
        body, name=name, grid=(t_pad // tr, nh),
        in_specs=[pl.BlockSpec((tr, HD), lambda i, h: (i, h)),
                  pl.BlockSpec((tr, HD), lambda i, h: (i, z_blk0 + h)),
                  pl.BlockSpec((1, HD), lambda i, h: (0, 0))],
        out_specs=pl.BlockSpec((tr, HD), lambda i, h: (i, h)),
        out_shape=jax.ShapeDtypeStruct((t_pad, w), BF16), compiler_params=_cparams(),
    )(o, z_arr, g)


def _grms_bwd(o, z_arr, z_blk0, g, dy_arr, dy_blk0, name):
    t_pad, w = o.shape
    nh = w // HD
    tr = _row_tile(t_pad, HD * 4)

    def body(o_ref, z_ref, g_ref, dy_ref, do_ref, dz_ref, dg_ref):
        _, vjp = jax.vjp(_grms_fn, o_ref[...], z_ref[...], g_ref[...])
        do, dz, dg = vjp(dy_ref[...])
        do_ref[...] = do
        dz_ref[...] = dz

        @pl.when((pl.program_id(0) == 0) & (pl.program_id(1) == 0))
        def _():
            dg_ref[...] = jnp.zeros_like(dg_ref)

        dg_ref[...] += dg

    blk = pl.BlockSpec((tr, HD), lambda i, h: (i, h))
    return pl.pallas_call(
        body, name=name, grid=(t_pad // tr, nh),
        in_specs=[blk, pl.BlockSpec((tr, HD), lambda i, h: (i, z_blk0 + h)),
                  pl.BlockSpec((1, HD), lambda i, h: (0, 0)),
                  pl.BlockSpec((tr, HD), lambda i, h: (i, dy_blk0 + h))],
        out_specs=[blk, blk, pl.BlockSpec((1, HD), lambda i, h: (0, 0))],
        out_shape=[jax.ShapeDtypeStruct((t_pad, w), F32), jax.ShapeDtypeStruct((t_pad, w), F32),
                   jax.ShapeDtypeStruct((1, HD), F32)],
        compiler_params=_cparams(),
    )(o, z_arr, g, dy_arr)


def _loss_fwd(y, tgt, first_row, name):
    t_pad = y.shape[0]
    tr = _row_tile(t_pad, D)

    def body(y_ref, t_ref, l_ref, dy_ref):
        rows = pl.program_id(0) * tr + _iota((tr, 1), 0)
        err = jnp.where(rows >= first_row, y_ref[...] - t_ref[...], 0.0)
        dy_ref[...] = err * (1.0 / D)

        @pl.when(pl.program_id(0) == 0)
        def _():
            l_ref[...] = jnp.zeros_like(l_ref)

        part = jnp.sum(jnp.sum(err * err, axis=1, keepdims=True), axis=0, keepdims=True)
        l_ref[...] += jnp.broadcast_to(part * (0.5 / D), l_ref.shape)

    row = pl.BlockSpec((tr, D), lambda i: (i, 0))
    return pl.pallas_call(
        body, name=name, grid=(t_pad // tr,), in_specs=[row, row],
        out_specs=[pl.BlockSpec((8, 128), lambda i: (0, 0)), row],
        out_shape=[jax.ShapeDtypeStruct((8, 128), F32), jax.ShapeDtypeStruct((t_pad, D), F32)],
        compiler_params=_cparams(),
    )(y, tgt)


def _add2(a, b, name):
    t_pad, w = a.shape
    tr = _row_tile(t_pad, w)

    def body(a_ref, b_ref, o_ref):
        o_ref[...] = a_ref[...] + b_ref[...]

    row = pl.BlockSpec((tr, w), lambda i: (i, 0))
    return pl.pallas_call(body, name=name, grid=(t_pad // tr,), in_specs=[row, row], out_specs=row,
                          out_shape=jax.ShapeDtypeStruct((t_pad, w), F32), compiler_params=_cparams())(a, b)


def _assemble_bf16(parts, name):
    t_pad = parts[0][0].shape[0] if parts[0][1] == "cols" else parts[0][0].shape[1]
    widths = [p.shape[1] if kind == "cols" else HD for p, kind in parts]
    total = sum(widths)
    tr = _row_tile(t_pad, total)

    def body(*refs):
        o_ref = refs[-1]
        off = 0
        for ref, (p, kind), w in zip(refs[:-1], parts, widths):
            if kind == "cols":
                o_ref[:, off:off + w] = ref[...].astype(BF16)
            else:
                acc = ref[0]
                for hh in range(1, p.shape[0]):
                    acc = acc + ref[hh]
                o_ref[:, off:off + w] = acc.astype(BF16)
            off += w

    specs = []
    for p, kind in parts:
        if kind == "cols":
            specs.append(pl.BlockSpec((tr, p.shape[1]), lambda i: (i, 0)))
        else:
            specs.append(pl.BlockSpec((p.shape[0], tr, HD), lambda i: (0, i, 0)))
    return pl.pallas_call(
        body, name=name, grid=(t_pad // tr,), in_specs=specs,
        out_specs=pl.BlockSpec((tr, total), lambda i: (i, 0)),
        out_shape=jax.ShapeDtypeStruct((t_pad, total), BF16), compiler_params=_cparams(),
    )(*[p for p, _ in parts])


CONV_K = 4
HALO = 8
RT = 128


def _conv_fwd(p, blk0, w, mode, pad, name):
    t_pad = p.shape[0]
    nt = t_pad // RT
    scale = HD ** -0.5 if mode == "q" else 1.0

    def body(x_ref, w_ref, y_ref, xs_ref):
        xs_ref[0:HALO, :] = jnp.zeros((HALO, HD), F32)
        rows = _iota((t_pad, 1), 0)
        xs_ref[HALO:HALO + t_pad, :] = jnp.where(rows >= pad, x_ref[...], 0.0)
        wv = w_ref[...]

        def tile(i, carry):
            r0 = pl.multiple_of(i * RT, RT)
            ext = xs_ref[pl.ds(r0, RT + HALO), :]
            acc = ext[HALO:, :] * wv[3:4, :]
            for s in (1, 2, 3):
                acc = acc + pltpu.roll(ext, s, 0)[HALO:, :] * wv[3 - s:4 - s, :]
            y = _silu(acc)
            if mode != "v":
                y = y * lax.rsqrt(jnp.sum(y * y, axis=-1, keepdims=True) + L2_EPS) * scale
            y_ref[pl.ds(r0, RT), :] = y
            return carry

        lax.fori_loop(0, nt, tile, 0)

    return pl.pallas_call(
        body, name=name, grid=(GDN_H,),
        in_specs=[pl.BlockSpec((t_pad, HD), lambda h: (0, blk0 + h)), pl.BlockSpec((CONV_K, HD), lambda h: (0, h))],
        out_specs=pl.BlockSpec((t_pad, HD), lambda h: (0, h)),
        out_shape=jax.ShapeDtypeStruct((t_pad, GDN_H * HD), F32),
        scratch_shapes=[pltpu.VMEM((t_pad + HALO, HD), F32)],
        compiler_params=_cparams(),
    )(p, w)


def _conv_bwd(p, blk0, w, dn, mode, pad, name):
    t_pad = p.shape[0]
    nt = t_pad // RT
    scale = HD ** -0.5 if mode == "q" else 1.0

    def body(x_ref, w_ref, dn_ref, dx_ref, dw_ref, xs_ref, ds_ref):
        xs_ref[0:HALO, :] = jnp.zeros((HALO, HD), F32)
        xs_ref[HALO + t_pad:HALO + t_pad + 2 * HALO, :] = jnp.zeros((2 * HALO, HD), F32)
        ds_ref[t_pad:t_pad + HALO, :] = jnp.zeros((HALO, HD), F32)
        rows = _iota((t_pad, 1), 0)
        xs_ref[HALO:HALO + t_pad, :] = jnp.where(rows >= pad, x_ref[...], 0.0)
        ds_ref[0:t_pad, :] = dn_ref[...]
        wv = w_ref[...]

        def tile(i, dw):
            r0 = pl.multiple_of(i * RT, RT)
            ext = xs_ref[pl.ds(r0, RT + 2 * HALO), :]
            dn_e = ds_ref[pl.ds(r0, RT + HALO), :]
            xsh = [ext[HALO:, :]] + [pltpu.roll(ext, s, 0)[HALO:, :] for s in (1, 2, 3)]
            pre = xsh[0] * wv[3:4, :]
            for s in (1, 2, 3):
                pre = pre + xsh[s] * wv[3 - s:4 - s, :]
            sg = _sigmoid(pre)
            y = pre * sg
            if mode != "v":
                ss = jnp.sum(y * y, axis=-1, keepdims=True) + L2_EPS
                r = lax.rsqrt(ss)
                dy = scale * (dn_e * r - y * (r * r * r) * jnp.sum(dn_e * y, axis=-1, keepdims=True))
            else:
                dy = dn_e
            dpre = dy * (sg * (1.0 + pre * (1.0 - sg)))
            dx = dpre[:RT, :] * wv[3:4, :]
            for s in (1, 2, 3):
                dx = dx + pltpu.roll(dpre, RT + HALO - s, 0)[:RT, :] * wv[3 - s:4 - s, :]
            trow = r0 + _iota((RT, 1), 0)
            dx_ref[pl.ds(r0, RT), :] = jnp.where(trow >= pad, dx, 0.0)
            new = []
            for s in (0, 1, 2, 3):
                new.append(dw[s] + jnp.sum(dpre[:RT, :] * xsh[s][:RT, :], axis=0, keepdims=True))
            return tuple(new)

        z = jnp.zeros((1, HD), F32)
        dw = lax.fori_loop(0, nt, tile, (z, z, z, z))
        for s in (0, 1, 2, 3):
            dw_ref[3 - s:4 - s, :] = dw[s]

    return pl.pallas_call(
        body, name=name, grid=(GDN_H,),
        in_specs=[pl.BlockSpec((t_pad, HD), lambda h: (0, blk0 + h)), pl.BlockSpec((CONV_K, HD), lambda h: (0, h)),
                  pl.BlockSpec((t_pad, HD), lambda h: (0, h))],
        out_specs=[pl.BlockSpec((t_pad, HD), lambda h: (0, h)), pl.BlockSpec((CONV_K, HD), lambda h: (0, h))],
        out_shape=[jax.ShapeDtypeStruct((t_pad, GDN_H * HD), F32), jax.ShapeDtypeStruct((CONV_K, GDN_H * HD), F32)],
        scratch_shapes=[pltpu.VMEM((t_pad + 3 * HALO, HD), F32), pltpu.VMEM((t_pad + HALO, HD), F32)],
        compiler_params=_cparams(),
    )(p, w, dn)


def _unit_lower_inv(m, bd, eye):
    md = m * bd
    low = m - md
    p2 = hnn(md, md)
    p4 = hnn(p2, p2)
    p8 = hnn(p4, p4)
    dinv = hnn(hnn(hnn(eye - md, eye + p2), eye + p4), eye + p8)
    n = hnn(dinv, low)
    n2 = hnn(n, n)
    return hnn(hnn(eye - n, eye + n2), dinv)


def _gdn_chunk(q, k, v, bb, aa, alog, dtb, s, sel, valid):
    ri = _iota((CH, CH), 0)
    ci = _iota((CH, CH), 1)
    causal = ri >= ci
    strict = ri > ci
    ltri = causal.astype(F32)
    eye = (ri == ci).astype(F32)
    bd = ((ri >> 4) == (ci >> 4)).astype(F32)

    beta_all = jnp.where(valid, _sigmoid(bb), 0.0)
    g_all = jnp.where(valid, -jnp.exp(alog) * _softplus(aa + dtb), 0.0)
    gc_all = hnn(ltri, g_all)
    beta = jnp.sum(beta_all * sel, axis=1, keepdims=True)
    gc = jnp.sum(gc_all * sel, axis=1, keepdims=True)
    gc_rows = hnt(jnp.broadcast_to(sel, (CH, HD)), gc_all)
    last = _iota((CH, 1), 0) == CH - 1
    gc_last = jnp.sum(jnp.where(last, gc, 0.0), axis=0, keepdims=True)
    decay = jnp.exp(jnp.where(causal, gc - gc_rows, NEG))
    egc = jnp.exp(gc)

    kb = k * beta
    m = jnp.where(strict, bnt(kb, k) * decay, 0.0)
    t_inv = _unit_lower_inv(m, bd, eye)
    u = bnn(t_inv, v * beta)
    w = bnn(t_inv, kb * egc)
    a_intra = bnt(q, k) * decay
    q_dec = q * egc
    k_dec = k * jnp.exp(gc_last - gc)
    v_new = u - bnn(w, s)
    o = bnn(q_dec, s) + bnn(a_intra, v_new)
    s_new = s * jnp.exp(gc_last) + btn(k_dec, v_new)
    return o, s_new


def _gdn_specs(nc, rev):
    cc = (lambda c: nc - 1 - c) if rev else (lambda c: c)
    blk = lambda off: pl.BlockSpec((CH, HD), lambda h, c: (cc(c), off + h))
    fix = lambda off: pl.BlockSpec((CH, HD), lambda h, c: (cc(c), off))
    par = pl.BlockSpec((1, HD), lambda h, c: (0, 0))
    return cc, blk, fix, par


def _gdn_fwd(qn, kn, vn, p, alog, dtb, pad, name):
    t_pad = qn.shape[0]
    nc = t_pad // CH
    cc, blk, fix, par = _gdn_specs(nc, False)

    def body(q_ref, k_ref, v_ref, bb_ref, aa_ref, al_ref, dt_ref, o_ref, ss_ref, s_ref):
        h, c = pl.program_id(0), pl.program_id(1)

        @pl.when(c == 0)
        def _():
            s_ref[...] = jnp.zeros_like(s_ref)

        s = s_ref[...]
        ss_ref[0, 0] = s
        sel = (_iota((1, HD), 1) == h).astype(F32)
        valid = (c * CH + _iota((CH, 1), 0)) >= pad
        o, s_new = _gdn_chunk(q_ref[...], k_ref[...], v_ref[...], bb_ref[...], aa_ref[...], al_ref[...], dt_ref[...],
                              s, sel, valid)
        o_ref[...] = o
        s_ref[...] = s_new

    return pl.pallas_call(
        body, name=name, grid=(GDN_H, nc),
        in_specs=[blk(0), blk(0), blk(0), fix(16), fix(17), par, par],
        out_specs=[blk(0), pl.BlockSpec((1, 1, HD, HD), lambda h, c: (c, h, 0, 0))],
        out_shape=[jax.ShapeDtypeStruct((t_pad, GDN_H * HD), F32), jax.ShapeDtypeStruct((nc, GDN_H, HD, HD), F32)],
        scratch_shapes=[pltpu.VMEM((HD, HD), F32)],
        compiler_params=_cparams(),
    )(qn, kn, vn, p, p, alog, dtb)


def _gdn_bwd(qn, kn, vn, p, alog, dtb, ssave, do, pad, name):
    t_pad = qn.shape[0]
    nc = t_pad // CH
    cc, blk, fix, par = _gdn_specs(nc, True)

    def body(q_ref, k_ref, v_ref, bb_ref, aa_ref, al_ref, dt_ref, ss_ref, do_ref,
             dq_ref, dk_ref, dv_ref, dbb_ref, daa_ref, dal_ref, ddt_ref, ds_ref):
        h, c = pl.program_id(0), pl.program_id(1)

        @pl.when(c == 0)
        def _():
            ds_ref[...] = jnp.zeros_like(ds_ref)
            dal_ref[...] = jnp.zeros_like(dal_ref)
            ddt_ref[...] = jnp.zeros_like(ddt_ref)

        sel = (_iota((1, HD), 1) == h).astype(F32)
        valid = ((nc - 1 - c) * CH + _iota((CH, 1), 0)) >= pad
        fn = lambda q, k, v, bb, aa, al, dt, s: _gdn_chunk(q, k, v, bb, aa, al, dt, s, sel, valid)
        _, vjp = jax.vjp(fn, q_ref[...], k_ref[...], v_ref[...], bb_ref[...], aa_ref[...], al_ref[...], dt_ref[...],
                         ss_ref[0, 0])
        dq, dk, dv, dbb, daa, dal, ddt, ds = vjp((do_ref[...], ds_ref[...]))
        dq_ref[...] = dq
        dk_ref[...] = dk
        dv_ref[...] = dv
        dbb_ref[0] = dbb
        daa_ref[0] = daa
        dal_ref[0] += dal
        ddt_ref[0] += ddt
        ds_ref[...] = ds

    hblk = pl.BlockSpec((1, CH, HD), lambda h, c: (h, cc(c), 0))
    hpar = pl.BlockSpec((1, 1, HD), lambda h, c: (h, 0, 0))
    sds = jax.ShapeDtypeStruct
    return pl.pallas_call(
        body, name=name, grid=(GDN_H, nc),
        in_specs=[blk(0), blk(0), blk(0), fix(16), fix(17), par, par,
                  pl.BlockSpec((1, 1, HD, HD), lambda h, c: (cc(c), h, 0, 0)), blk(0)],
        out_specs=[blk(0), blk(0), blk(0), hblk, hblk, hpar, hpar],
        out_shape=[sds((t_pad, GDN_H * HD), F32)] * 3 + [sds((GDN_H, t_pad, HD), F32)] * 2 + [sds((GDN_H, 1, HD), F32)] * 2,
        scratch_shapes=[pltpu.VMEM((HD, HD), F32)],
        compiler_params=_cparams(),
    )(qn, kn, vn, p, p, alog, dtb, ssave, do)


SB_Q0, SB_K0, SB_V0 = 18, 22, 26
SB_SCALE = SB_DH ** -0.5


def _sb_terms(z, allowed):
    sp = jnp.log(1.0 + jnp.exp(-jnp.abs(z)))
    l1m = jnp.where(allowed, -jnp.maximum(z, 0.0) - sp, 0.0)
    ls = jnp.minimum(z, 0.0) - sp
    return l1m, ls


def _sb_fwd(p, pad, name):
    t_pad = p.shape[0]
    nq = t_pad // QB

    def body(q_ref, k_ref, v_ref, o_ref, r_ref):
        i = pl.program_id(1)
        q = q_ref[...] * SB_SCALE
        lane = _iota((1, HD), 1)
        rowq = i * QB + _iota((QB, 1), 0)
        upper = (_iota((QB, QB), 0) > _iota((QB, QB), 1)).astype(BF16)
        o_tot = jnp.zeros((QB, HD), F32)
        r_tot = jnp.zeros((QB, HD), F32)
        for hh in (0, 1):
            lm = (lane < SB_DH) if hh == 0 else (lane >= SB_DH)
            qm = jnp.where(lm, q, 0.0).astype(BF16)

            def step(j, carry):
                o_acc, run = carry
                kb = i - j
                start = pl.multiple_of(kb * QB, QB)
                kblk = k_ref[pl.ds(start, QB), :].astype(BF16)
                vblk = v_ref[pl.ds(start, QB), :].astype(BF16)
                z = lax.dot_general(qm, kblk, (NT, ((), ())), preferred_element_type=F32)
                colk = kb * QB + _iota((1, QB), 1)
                allowed = (colk < rowq) & (colk >= pad)
                l1m, ls = _sb_terms(z, allowed)
                after = _split2(l1m, upper, NN) + run
                wgt = jnp.where(allowed, jnp.exp(ls + after), 0.0)
                o_acc = o_acc + lax.dot_general(wgt.astype(BF16), vblk, (NN, ((), ())), preferred_element_type=F32)
                run = run + jnp.sum(l1m, axis=1, keepdims=True)
                return o_acc, run

            o_acc, run = lax.fori_loop(0, i + 1, step, (jnp.zeros((QB, HD), F32), jnp.zeros((QB, 1), F32)))
            o_tot = jnp.where(lm, o_acc, o_tot)
            r_tot = jnp.where(lm, run, r_tot)
        o_ref[...] = o_tot.astype(BF16)
        r_ref[...] = r_tot

    full = lambda off: pl.BlockSpec((t_pad, HD), lambda pr, i: (0, off + pr))
    blk = pl.BlockSpec((QB, HD), lambda pr, i: (i, pr))
    return pl.pallas_call(
        body, name=name, grid=(SB_H // 2, nq),
        in_specs=[pl.BlockSpec((QB, HD), lambda pr, i: (i, SB_Q0 + pr)), full(SB_K0), full(SB_V0)],
        out_specs=[blk, blk],
        out_shape=[jax.ShapeDtypeStruct((t_pad, SB_H * SB_DH), BF16), jax.ShapeDtypeStruct((t_pad, SB_H * SB_DH), F32)],
        compiler_params=_cparams(),
    )(p, p, p)


def _sb_bwd(p, rtot, dy, dy_blk0, pad, name):
    t_pad = p.shape[0]
    nq = t_pad // QB

    def body(q_ref, k_ref, v_ref, r_ref, do_ref, dq_ref, dk_ref, dv_ref):
        i = pl.program_id(1)

        @pl.when(i == 0)
        def _():
            dk_ref[...] = jnp.zeros_like(dk_ref)
            dv_ref[...] = jnp.zeros_like(dv_ref)

        q = q_ref[...] * SB_SCALE
        do = do_ref[...]
        rt = r_ref[...]
        lane = _iota((1, HD), 1)
        rowq = i * QB + _iota((QB, 1), 0)
        rj = _iota((QB, QB), 0)
        cs = _iota((QB, QB), 1)
        upper = (rj > cs).astype(BF16)
        lower = (rj < cs).astype(BF16)
        dq_tot = jnp.zeros((QB, HD), F32)
        for hh in (0, 1):
            lm = (lane < SB_DH) if hh == 0 else (lane >= SB_DH)
            qm = jnp.where(lm, q, 0.0).astype(BF16)
            dom = jnp.where(lm, do, 0.0).astype(BF16)
            rcol = jnp.sum(jnp.where(lane == hh * SB_DH, rt, 0.0), axis=1, keepdims=True)

            def step(kb, carry):
                dq_acc, seen, gseen = carry
                start = pl.multiple_of(kb * QB, QB)
                kblk = k_ref[pl.ds(start, QB), :].astype(BF16)
                vblk = v_ref[pl.ds(start, QB), :].astype(BF16)
                z = lax.dot_general(qm, kblk, (NT, ((), ())), preferred_element_type=F32)
                colk = kb * QB + _iota((1, QB), 1)
                allowed = (colk < rowq) & (colk >= pad)
                l1m, ls = _sb_terms(z, allowed)
                rs = jnp.sum(l1m, axis=1, keepdims=True)
                after = _split2(l1m, upper, NN) + (rcol - seen - rs)
                wgt = jnp.where(allowed, jnp.exp(ls + after), 0.0)
                dwgt = lax.dot_general(dom, vblk, (NT, ((), ())), preferred_element_type=F32)
                dl = dwgt * wgt
                gpre = gseen + _split2(dl, lower, NN)
                sg = _sigmoid(z)
                dz = jnp.where(allowed, dl * (1.0 - sg) - gpre * sg, 0.0).astype(BF16)
                dq_acc = dq_acc + lax.dot_general(dz, kblk, (NN, ((), ())), preferred_element_type=F32)
                dk_ref[pl.ds(start, QB), :] += lax.dot_general(dz, qm, (TN, ((), ())), preferred_element_type=F32)
                dv_ref[pl.ds(start, QB), :] += lax.dot_general(wgt.astype(BF16), dom, (TN, ((), ())),
                                                               preferred_element_type=F32)
                return dq_acc, seen + rs, gseen + jnp.sum(dl, axis=1, keepdims=True)

            zc = jnp.zeros((QB, 1), F32)
            dq_acc, _, _ = lax.fori_loop(0, i + 1, step, (jnp.zeros((QB, HD), F32), zc, zc))
            dq_tot = jnp.where(lm, dq_acc * SB_SCALE, dq_tot)
        dq_ref[...] = dq_tot

    full_in = lambda off: pl.BlockSpec((t_pad, HD), lambda pr, i: (0, off + pr))
    full_out = pl.BlockSpec((t_pad, HD), lambda pr, i: (0, pr))
    blk = pl.BlockSpec((QB, HD), lambda pr, i: (i, pr))
    sds = jax.ShapeDtypeStruct((t_pad, SB_H * SB_DH), F32)
    return pl.pallas_call(
        body, name=name, grid=(SB_H // 2, nq),
        in_specs=[pl.BlockSpec((QB, HD), lambda pr, i: (i, SB_Q0 + pr)), full_in(SB_K0), full_in(SB_V0), blk,
                  pl.BlockSpec((QB, HD), lambda pr, i: (i, dy_blk0 + pr))],
        out_specs=[blk, full_out, full_out],
        out_shape=[sds, sds, sds],
        compiler_params=_cparams(),
    )(p, p, p, rtot, dy)


HG_LEVELS = 6


def _hg_prefix_matrix():
    t = np.arange(CH)[:, None]
    j = np.arange(CH)[None, :]
    groups = [(j <= t)]
    for lvl in range(1, HG_LEVELS + 1):
        half = CH >> lvl
        e = (t // (2 * half)) * (2 * half) + half - 1
        groups.append(j <= e)
    groups.append(np.ones((2 * CH, CH), bool))
    return np.concatenate(groups, axis=0).astype(np.float32)


def _hg_chunk(qr, fr, iv, r0, r1, st, valid, ecat):
    mx = jnp.maximum(r0, r1)
    e0 = jnp.exp(r0 - mx)
    e1 = jnp.exp(r1 - mx)
    lb = e1 / (e0 + e1)
    fg = lb + (1.0 - lb) * _sigmoid(fr)
    logf = jnp.where(valid, jnp.log(fg), 0.0)
    kk = jnp.where(valid, 1.0 - fg, 0.0)
    q = jnp.where(valid, _silu(qr), 0.0)
    v = jnp.where(valid, iv, 0.0)

    pre = hnn(ecat, logf)
    b = pre[0:CH]
    b_last = pre[(HG_LEVELS + 1) * CH:]
    row = _iota((CH, 1), 0)
    ri = _iota((CH, CH), 0)
    ci = _iota((CH, CH), 1)
    a = jnp.where(ri == ci, jnp.sum(q * kk, axis=1, keepdims=True), 0.0)
    for lvl in range(1, HG_LEVELS + 1):
        half = CH >> lvl
        m = pre[lvl * CH:(lvl + 1) * CH]
        low = (row & half) != 0
        qt = jnp.where(low, q * jnp.exp(jnp.where(low, b - m, 0.0)), 0.0)
        kt = jnp.where(low, 0.0, kk * jnp.exp(jnp.where(low, 0.0, m - b)))
        same = (ri >> (7 - lvl)) == (ci >> (7 - lvl))
        a = a + jnp.where(same, bnt(qt, kt), 0.0)
    o = bnt(q * jnp.exp(b), st) + bnn(a, v)
    kd = kk * jnp.exp(b_last[0:CH] - b)
    st_new = st * jnp.exp(b_last) + btn(v, kd)
    return o, st_new


def _hg_specs(nc, rev):
    cc = (lambda c: nc - 1 - c) if rev else (lambda c: c)
    blk = lambda off: pl.BlockSpec((CH, HD), lambda h, c: (cc(c), off + h))
    return cc, blk


def _hg_fwd(p, lbraw, ecat, pad, name):
    t_pad = p.shape[0]
    nc = t_pad // CH
    cc, blk = _hg_specs(nc, False)

    def body(q_ref, f_ref, i_ref, lb_ref, e_ref, o_ref, ss_ref, s_ref):
        c = pl.program_id(1)

        @pl.when(c == 0)
        def _():
            s_ref[...] = jnp.zeros_like(s_ref)

        st = s_ref[...]
        ss_ref[0, 0] = st
        valid = (c * CH + _iota((CH, 1), 0)) >= pad
        o, st_new = _hg_chunk(q_ref[...], f_ref[...], i_ref[...], lb_ref[0:1, :], lb_ref[1:2, :], st, valid, e_ref[...])
        o_ref[...] = o
        s_ref[...] = st_new

    return pl.pallas_call(
        body, name=name, grid=(HG_H, nc),
        in_specs=[blk(0), blk(HG_H), blk(2 * HG_H), pl.BlockSpec((2, HD), lambda h, c: (0, h)),
                  pl.BlockSpec(ecat.shape, lambda h, c: (0, 0))],
        out_specs=[blk(0), pl.BlockSpec((1, 1, HD, HD), lambda h, c: (c, h, 0, 0))],
        out_shape=[jax.ShapeDtypeStruct((t_pad, HG_H * HD), F32), jax.ShapeDtypeStruct((nc, HG_H, HD, HD), F32)],
        scratch_shapes=[pltpu.VMEM((HD, HD), F32)],
        compiler_params=_cparams(),
    )(p, p, p, lbraw, ecat)


def _hg_bwd(p, lbraw, ecat, ssave, do, pad, name):
    t_pad = p.shape[0]
    nc = t_pad // CH
    cc, blk = _hg_specs(nc, True)

    def body(q_ref, f_ref, i_ref, lb_ref, e_ref, ss_ref, do_ref, dq_ref, df_ref, di_ref, dlb_ref, ds_ref):
        c = pl.program_id(1)

        @pl.when(c == 0)
        def _():
            ds_ref[...] = jnp.zeros_like(ds_ref)
            dlb_ref[...] = jnp.zeros_like(dlb_ref)

        valid = ((nc - 1 - c) * CH + _iota((CH, 1), 0)) >= pad
        ecv = e_ref[...]
        fn = lambda qr, fr, iv, r0, r1, st: _hg_chunk(qr, fr, iv, r0, r1, st, valid, ecv)
        _, vjp = jax.vjp(fn, q_ref[...], f_ref[...], i_ref[...], lb_ref[0:1, :], lb_ref[1:2, :], ss_ref[0, 0])
        dq, df, di, d0, d1, ds = vjp((do_ref[...], ds_ref[...]))
        dq_ref[...] = dq
        df_ref[...] = df
        di_ref[...] = di
        dlb_ref[0:1, :] += d0
        dlb_ref[1:2, :] += d1
        ds_ref[...] = ds

    sds = jax.ShapeDtypeStruct((t_pad, HG_H * HD), F32)
    return pl.pallas_call(
        body, name=name, grid=(HG_H, nc),
        in_specs=[blk(0), blk(HG_H), blk(2 * HG_H), pl.BlockSpec((2, HD), lambda h, c: (0, h)),
                  pl.BlockSpec(ecat.shape, lambda h, c: (0, 0)),
                  pl.BlockSpec((1, 1, HD, HD), lambda h, c: (cc(c), h, 0, 0)), blk(0)],
        out_specs=[blk(0), blk(0), blk(0), pl.BlockSpec((2, HD), lambda h, c: (0, h))],
        out_shape=[sds, sds, sds, jax.ShapeDtypeStruct((2, HG_H * HD), F32)],
        scratch_shapes=[pltpu.VMEM((HD, HD), F32)],
        compiler_params=_cparams(),
    )(p, p, p, lbraw, ecat, ssave, do)


def _pad_ab_cols(w):
    z = jnp.zeros((w.shape[0], HD - GDN_H), w.dtype)
    return jnp.concatenate([w[:, :2048], w[:, 2048:2052], z, w[:, 2052:2056], z, w[:, 2056:]], axis=1)


def _unpad_ab_cols(w):
    return jnp.concatenate([w[:, :2048], w[:, 2048:2052], w[:, 2176:2180], w[:, 2304:]], axis=1)


def _lane_pad(v):
    return jnp.pad(v, ((0, 0), (0, HD - v.shape[1])))


def _mlp_fwd(hb, w1, w2, tag):
    a = _mm(hb, w1, name=f"mlp_up_{tag}")
    r = _relu2_fwd(a, f"relu2_{tag}")
    m = _mm(r, w2, name=f"mlp_down_{tag}")
    return a, r, m


def _mlp_bwd(hb, a, r, dmb, w1, w2, tag):
    dr = _mm(dmb, w2, tb=True, name=f"mlp_down_dx_{tag}")
    dw2 = _mm(r, dmb, ta=True, name=f"mlp_down_dw_{tag}")
    da = _relu2_bwd(a, dr, f"relu2_bwd_{tag}")
    dh = _mm(da, w1, tb=True, name=f"mlp_up_dx_{tag}")
    dw1 = _mm(hb, da, ta=True, name=f"mlp_up_dw_{tag}")
    return dh, dw1, dw2


def _local_step(h0, tgt, w, pad):
    row = lambda a, i: a[i:i + 1]
    ecat = jnp.asarray(_hg_prefix_matrix())
    cw = [w["conv_w"][:, i * 512:(i + 1) * 512] for i in range(3)]
    alog, dtb = _lane_pad(w["a_log"]), _lane_pad(w["dt_bias"])

    h0b = h0.astype(BF16)
    p0 = _mm(h0b, w["ab_w_in"], name="ab_in")
    qn = _conv_fwd(p0, 0, cw[0], "q", pad, "conv_q")
    kn = _conv_fwd(p0, 4, cw[1], "k", pad, "conv_k")
    vn = _conv_fwd(p0, 8, cw[2], "v", pad, "conv_v")
    oa_raw, ss0 = _gdn_fwd(qn, kn, vn, p0, alog, dtb, pad, "gdn_fwd")
    oa = _grms_fwd(oa_raw, p0, 12, w["ab_gnorm_g"], "gdn_gate")
    ob, rtot = _sb_fwd(p0, pad, "sb_fwd")
    ycat = jnp.concatenate([oa, ob], axis=1)
    mix0 = _mm(ycat, w["ab_w_out"], name="ab_out")
    h1, h1b = _ln_res_fwd(h0, mix0, row(w["ln_mix_g"], 0), row(w["ln_mix_b"], 0), "ln_mix_0")
    a0, r0, m0 = _mlp_fwd(h1b, w["mlp_w1"][0], w["mlp_w2"][0], "0")
    h2, h2b = _ln_res_fwd(h1, m0, row(w["ln_ffn_g"], 0), row(w["ln_ffn_b"], 0), "ln_ffn_0")
    p1 = _mm(h2b, w["c_w_in"], name="c_in")
    oc_raw, ss1 = _hg_fwd(p1, w["c_lb_raw"], ecat, pad, "hg_fwd")
    yc = _grms_fwd(oc_raw, p1, 3 * HG_H, w["c_gnorm_g"], "hg_gate")
    mix1 = _mm(yc, w["c_w_out"], name="c_out")
    h3, h3b = _ln_res_fwd(h2, mix1, row(w["ln_mix_g"], 1), row(w["ln_mix_b"], 1), "ln_mix_1")
    a1, r1, m1 = _mlp_fwd(h3b, w["mlp_w1"][1], w["mlp_w2"][1], "1")
    h4, _ = _ln_res_fwd(h3, m1, row(w["ln_ffn_g"], 1), row(w["ln_ffn_b"], 1), "ln_ffn_1")
    loss, dh4 = _loss_fwd(h4, tgt, pad + N_META, "loss")

    zero = jnp.zeros_like(dh4)
    dh3a, dm1b, dfg1, dfb1 = _ln_res_bwd(h3, m1, row(w["ln_ffn_g"], 1), row(w["ln_ffn_b"], 1), dh4, zero, "ln_ffn_bwd_1")
    dh3b, dw1_1, dw2_1 = _mlp_bwd(h3b, a1, r1, dm1b, w["mlp_w1"][1], w["mlp_w2"][1], "1")
    dh2a, dmix1b, dmg1, dmb1 = _ln_res_bwd(h2, mix1, row(w["ln_mix_g"], 1), row(w["ln_mix_b"], 1), dh3a, dh3b, "ln_mix_bwd_1")
    dyc = _mm(dmix1b, w["c_w_out"], tb=True, name="c_out_dx")
    dwco = _mm(yc, dmix1b, ta=True, name="c_out_dw")
    doc, dzc, dcg = _grms_bwd(oc_raw, p1, 3 * HG_H, w["c_gnorm_g"], dyc, 0, "hg_gate_bwd")
    dq1, df1, di1, dlb = _hg_bwd(p1, w["c_lb_raw"], ecat, ss1, doc, pad, "hg_bwd")
    dp1 = _assemble_bf16([(dq1, "cols"), (df1, "cols"), (di1, "cols"), (dzc, "cols")], "c_in_dy")
    dh2b = _mm(dp1, w["c_w_in"], tb=True, name="c_in_dx")
    dwc = _mm(h2b, dp1, ta=True, name="c_in_dw")
    dh1a, dm0b, dfg0, dfb0 = _ln_res_bwd(h1, m0, row(w["ln_ffn_g"], 0), row(w["ln_ffn_b"], 0), dh2a, dh2b, "ln_ffn_bwd_0")
    dh1b, dw1_0, dw2_0 = _mlp_bwd(h1b, a0, r0, dm0b, w["mlp_w1"][0], w["mlp_w2"][0], "0")
    dh0a, dmix0b, dmg0, dmb0 = _ln_res_bwd(h0, mix0, row(w["ln_mix_g"], 0), row(w["ln_mix_b"], 0), dh1a, dh1b, "ln_mix_bwd_0")
    dycat = _mm(dmix0b, w["ab_w_out"], tb=True, name="ab_out_dx")
    dwabo = _mm(ycat, dmix0b, ta=True, name="ab_out_dw")
    doa, dza, dag = _grms_bwd(oa_raw, p0, 12, w["ab_gnorm_g"], dycat, 0, "gdn_gate_bwd")
    dqn, dkn, dvn, dbb, daa, dal, ddt = _gdn_bwd(qn, kn, vn, p0, alog, dtb, ss0, doa, pad, "gdn_bwd")
    dpq, dcq = _conv_bwd(p0, 0, cw[0], dqn, "q", pad, "conv_q_bwd")
    dpk, dck = _conv_bwd(p0, 4, cw[1], dkn, "k", pad, "conv_k_bwd")
    dpv, dcv = _conv_bwd(p0, 8, cw[2], dvn, "v", pad, "conv_v_bwd")
    dqb, dkb, dvb = _sb_bwd(p0, rtot, dycat, 4, pad, "sb_bwd")
    dp0 = _assemble_bf16([(dpq, "cols"), (dpk, "cols"), (dpv, "cols"), (dza, "cols"), (dbb, "heads"), (daa, "heads"),
                          (dqb, "cols"), (dkb, "cols"), (dvb, "cols")], "ab_in_dy")
    dh0b = _mm(dp0, w["ab_w_in"], tb=True, name="ab_in_dx")
    dwab = _mm(h0b, dp0, ta=True, name="ab_in_dw")
    dh0 = _add2(dh0a, dh0b, "dh0")

    grads = {
        "ab_w_in": dwab, "conv_w": jnp.concatenate([dcq, dck, dcv], axis=1),
        "a_log": jnp.sum(dal, axis=0)[:, :GDN_H], "dt_bias": jnp.sum(ddt, axis=0)[:, :GDN_H],
        "ab_gnorm_g": dag, "ab_w_out": dwabo, "c_w_in": dwc, "c_lb_raw": dlb, "c_gnorm_g": dcg, "c_w_out": dwco,
        "ln_mix_g": jnp.concatenate([dmg0, dmg1], 0), "ln_mix_b": jnp.concatenate([dmb0, dmb1], 0),
        "mlp_w1": jnp.stack([dw1_0, dw1_1]), "mlp_w2": jnp.stack([dw2_0, dw2_1]),
        "ln_ffn_g": jnp.concatenate([dfg0, dfg1], 0), "ln_ffn_b": jnp.concatenate([dfb0, dfb1], 0),
    }
    return loss, dh0, grads


MESH = pl.DeviceIdType.MESH
ANY = pl.BlockSpec(memory_space=pl.ANY)
N_CHIP = 4
N_DEV = 8
CHIP_REL = ((1, 0), (0, 1), (1, 1))
DEV_REL = tuple((dx, dy, dc) for dx in (0, 1) for dy in (0, 1) for dc in (0, 1))[1:]


def _pos():
    return lax.axis_index("x"), lax.axis_index("y"), lax.axis_index("c")


def _flip(a, d):
    return a + d - 2 * a * d


def _chip_allgather(buf, name):
    def body(x_ref, o_ref, send_sems, recv_sems, local_sem):
        x, y, c = _pos()
        mine = pltpu.make_async_copy(x_ref, o_ref.at[2 * x + y], local_sem)
        mine.start()

        def copy(k):
            tx, ty = _flip(x, CHIP_REL[k][0]), _flip(y, CHIP_REL[k][1])
            return tx, ty, (lambda slot: pltpu.make_async_remote_copy(
                src_ref=x_ref, dst_ref=o_ref.at[slot], send_sem=send_sems.at[k], recv_sem=recv_sems.at[k],
                device_id=(tx, ty, c), device_id_type=MESH))

        sends = []
        for k in range(3):
            _, _, mk = copy(k)
            cp = mk(2 * x + y)
            cp.start()
            sends.append(cp)
        for k in range(3):
            tx, ty, mk = copy(k)
            mk(2 * tx + ty).wait_recv()
        for cp in sends:
            cp.wait_send()
        mine.wait()

    return pl.pallas_call(
        body, name=name, in_specs=[ANY], out_specs=ANY,
        out_shape=jax.ShapeDtypeStruct((N_CHIP,) + buf.shape, buf.dtype),
        scratch_shapes=[pltpu.SemaphoreType.DMA((3,)), pltpu.SemaphoreType.DMA((3,)), pltpu.SemaphoreType.DMA],
        compiler_params=pltpu.CompilerParams(has_side_effects=True),
    )(buf)


def _grad_alltoall(g, name):
    def body(g_ref, o_ref, send_sems, recv_sems, local_sem):
        x, y, c = _pos()
        me = 4 * x + 2 * y + c
        mine = pltpu.make_async_copy(g_ref.at[2 * x + y, c], o_ref.at[me], local_sem)
        mine.start()

        def target(k):
            dx, dy, dc = DEV_REL[k]
            return _flip(x, dx), _flip(y, dy), _flip(c, dc)

        sends = []
        for k in range(N_DEV - 1):
            tx, ty, tc = target(k)
            cp = pltpu.make_async_remote_copy(
                src_ref=g_ref.at[2 * tx + ty, tc], dst_ref=o_ref.at[me], send_sem=send_sems.at[k],
                recv_sem=recv_sems.at[k], device_id=(tx, ty, tc), device_id_type=MESH)
            cp.start()
            sends.append(cp)
        for k in range(N_DEV - 1):
            tx, ty, tc = target(k)
            pltpu.make_async_remote_copy(
                src_ref=g_ref.at[0, 0], dst_ref=o_ref.at[4 * tx + 2 * ty + tc], send_sem=send_sems.at[k],
                recv_sem=recv_sems.at[k], device_id=(tx, ty, tc), device_id_type=MESH).wait_recv()
        for cp in sends:
            cp.wait_send()
        mine.wait()

    return pl.pallas_call(
        body, name=name, in_specs=[ANY], out_specs=ANY,
        out_shape=jax.ShapeDtypeStruct((N_DEV,) + g.shape[2:], g.dtype),
        scratch_shapes=[pltpu.SemaphoreType.DMA((N_DEV - 1,)), pltpu.SemaphoreType.DMA((N_DEV - 1,)),
                        pltpu.SemaphoreType.DMA],
        compiler_params=pltpu.CompilerParams(has_side_effects=True),
    )(g)


def _sum_slots(r, name):
    n, rh, w = r.shape
    tr = _pick(rh, (656, 328, 80, 16))

    def body(r_ref, o_ref):
        acc = r_ref[0].astype(F32)
        for s in range(1, n):
            acc = acc + r_ref[s].astype(F32)
        o_ref[...] = acc

    return pl.pallas_call(
        body, name=name, grid=(rh // tr,), in_specs=[pl.BlockSpec((n, tr, w), lambda i: (0, i, 0))],
        out_specs=pl.BlockSpec((tr, w), lambda i: (i, 0)), out_shape=jax.ShapeDtypeStruct((rh, w), F32),
        compiler_params=_cparams(),
    )(r)


def _sibling_exchange(half, name):
    def body(h_ref, o_ref, send_sem, recv_sem, local_sem):
        x, y, c = _pos()
        mine = pltpu.make_async_copy(h_ref, o_ref.at[c], local_sem)
        mine.start()
        cp = pltpu.make_async_remote_copy(src_ref=h_ref, dst_ref=o_ref.at[c], send_sem=send_sem, recv_sem=recv_sem,
                                          device_id=(x, y, 1 - c), device_id_type=MESH)
        cp.start()
        pltpu.make_async_remote_copy(src_ref=h_ref, dst_ref=o_ref.at[1 - c], send_sem=send_sem, recv_sem=recv_sem,
                                     device_id=(x, y, 1 - c), device_id_type=MESH).wait_recv()
        cp.wait_send()
        mine.wait()

    return pl.pallas_call(
        body, name=name, in_specs=[ANY], out_specs=ANY,
        out_shape=jax.ShapeDtypeStruct((2,) + half.shape, half.dtype),
        scratch_shapes=[pltpu.SemaphoreType.DMA, pltpu.SemaphoreType.DMA, pltpu.SemaphoreType.DMA],
        compiler_params=pltpu.CompilerParams(has_side_effects=True),
    )(half)


def _small_allreduce(buf, name):
    r, w = buf.shape

    def body(b_ref, o_ref, land_ref, send_sems, recv_sems):
        x, y, c = _pos()
        me = 4 * x + 2 * y + c
        land_ref[me] = b_ref[...]

        def target(k):
            dx, dy, dc = DEV_REL[k]
            return _flip(x, dx), _flip(y, dy), _flip(c, dc)

        sends = []
        for k in range(N_DEV - 1):
            tx, ty, tc = target(k)
            cp = pltpu.make_async_remote_copy(
                src_ref=b_ref, dst_ref=land_ref.at[me], send_sem=send_sems.at[k], recv_sem=recv_sems.at[k],
                device_id=(tx, ty, tc), device_id_type=MESH)
            cp.start()
            sends.append(cp)
        for k in range(N_DEV - 1):
            tx, ty, tc = target(k)
            pltpu.make_async_remote_copy(
                src_ref=b_ref, dst_ref=land_ref.at[4 * tx + 2 * ty + tc], send_sem=send_sems.at[k],
                recv_sem=recv_sems.at[k], device_id=(tx, ty, tc), device_id_type=MESH).wait_recv()
        for cp in sends:
            cp.wait_send()
        acc = land_ref[0]
        for s in range(1, N_DEV):
            acc = acc + land_ref[s]
        o_ref[...] = acc

    vm = pl.BlockSpec(memory_space=pltpu.VMEM)
    return pl.pallas_call(
        body, name=name, in_specs=[vm], out_specs=vm, out_shape=jax.ShapeDtypeStruct((r, w), F32),
        scratch_shapes=[pltpu.VMEM((N_DEV, r, w), F32), pltpu.SemaphoreType.DMA((N_DEV - 1,)),
                        pltpu.SemaphoreType.DMA((N_DEV - 1,))],
        compiler_params=pltpu.CompilerParams(has_side_effects=True),
    )(buf)


def _adamw(w, g, m, v, name):
    r, c = w.shape
    tr = _pick(r, (256, 128, 64, 8)) if r * c > (1 << 18) else r

    def body(w_ref, g_ref, m_ref, v_ref, d_ref, m2_ref, v2_ref):
        gg = g_ref[...]
        m2 = ADAM_B1 * m_ref[...] + (1.0 - ADAM_B1) * gg
        v2 = ADAM_B2 * v_ref[...] + (1.0 - ADAM_B2) * (gg * gg)
        m_hat = m2 / (1.0 - ADAM_B1 ** ADAM_STEP)
        v_hat = v2 / (1.0 - ADAM_B2 ** ADAM_STEP)
        d_ref[...] = -ADAM_LR * (m_hat / (jnp.sqrt(v_hat) + ADAM_EPS) + ADAM_WD * w_ref[...])
        m2_ref[...] = m2
        v2_ref[...] = v2

    blk = pl.BlockSpec((tr, c), lambda i: (i, 0))
    sds = jax.ShapeDtypeStruct((r, c), F32)
    return pl.pallas_call(body, name=name, grid=(r // tr,), in_specs=[blk] * 4, out_specs=[blk] * 3,
                          out_shape=[sds] * 3, compiler_params=_cparams())(w, g, m, v)


BIG = ("ab_w_in", "ab_conv_w", "ab_w_out", "c_w_in", "c_w_out", "mlp_w1", "mlp_w2")
BIG_AXIS = {"ab_w_in": 2, "ab_conv_w": 2, "ab_w_out": 1, "c_w_in": 2, "c_w_out": 1, "mlp_w1": 2, "mlp_w2": 1}
SMALL = ("ln_mix_g", "ln_mix_b", "ln_ffn_g", "ln_ffn_b", "c_lb_raw", "ab_a_log", "ab_dt_bias", "ab_gnorm_g", "c_gnorm_g")
SMALL_ROWS = 16
PACK_ROWS = 6560


def _pack_flat(pieces, rows):
    flat = jnp.concatenate([p.reshape(-1) for p in pieces])
    return jnp.pad(flat, (0, rows * D - flat.shape[0])).reshape(rows, D)


def _unpack_flat(buf, shapes):
    flat = buf.reshape(-1)
    out, off = [], 0
    for s in shapes:
        n = math.prod(s)
        out.append(flat[off:off + n].reshape(s))
        off += n
    return out


def _pack_small(d):
    rows = [jnp.pad(d[n], ((0, 0), (0, D - d[n].shape[1]))) for n in SMALL]
    buf = jnp.concatenate(rows, axis=0)
    return jnp.pad(buf, ((0, SMALL_ROWS - buf.shape[0]), (0, 0)))


def _unpack_small(buf, like):
    out, r = {}, 0
    for n in SMALL:
        nr, nc = like[n].shape
        out[n] = buf[r:r + nr, :nc]
        r += nr
    return out


def kernel(x, meta_tokens, ab_w_in, ab_conv_w, ab_a_log, ab_dt_bias, ab_gnorm_g, ab_w_out, c_w_in, c_lb_raw, c_gnorm_g, c_w_out, ln_mix_g, ln_mix_b, mlp_w1, mlp_w2, ln_ffn_g, ln_ffn_b, loss_target, m_meta_tokens, m_ab_w_in, m_ab_conv_w, m_ab_a_log, m_ab_dt_bias, m_ab_gnorm_g, m_ab_w_out, m_c_w_in, m_c_lb_raw, m_c_gnorm_g, m_c_w_out, m_ln_mix_g, m_ln_mix_b, m_mlp_w1, m_mlp_w2, m_ln_ffn_g, m_ln_ffn_b, v_meta_tokens, v_ab_w_in, v_ab_conv_w, v_ab_a_log, v_ab_dt_bias, v_ab_gnorm_g, v_ab_w_out, v_c_w_in, v_c_lb_raw, v_c_gnorm_g, v_c_w_out, v_ln_mix_g, v_ln_mix_b, v_mlp_w1, v_mlp_w2, v_ln_ffn_g, v_ln_ffn_b):
    names = ("meta_tokens", "ab_w_in", "ab_conv_w", "ab_a_log", "ab_dt_bias", "ab_gnorm_g", "ab_w_out", "c_w_in",
             "c_lb_raw", "c_gnorm_g", "c_w_out", "ln_mix_g", "ln_mix_b", "mlp_w1", "mlp_w2", "ln_ffn_g", "ln_ffn_b")
    wts = dict(zip(names, (meta_tokens, ab_w_in, ab_conv_w, ab_a_log, ab_dt_bias, ab_gnorm_g, ab_w_out, c_w_in, c_lb_raw,
                           c_gnorm_g, c_w_out, ln_mix_g, ln_mix_b, mlp_w1, mlp_w2, ln_ffn_g, ln_ffn_b)))
    mom_m = dict(zip(names, (m_meta_tokens, m_ab_w_in, m_ab_conv_w, m_ab_a_log, m_ab_dt_bias, m_ab_gnorm_g, m_ab_w_out,
                             m_c_w_in, m_c_lb_raw, m_c_gnorm_g, m_c_w_out, m_ln_mix_g, m_ln_mix_b, m_mlp_w1, m_mlp_w2,
                             m_ln_ffn_g, m_ln_ffn_b)))
    mom_v = dict(zip(names, (v_meta_tokens, v_ab_w_in, v_ab_conv_w, v_ab_a_log, v_ab_dt_bias, v_ab_gnorm_g, v_ab_w_out,
                             v_c_w_in, v_c_lb_raw, v_c_gnorm_g, v_c_w_out, v_ln_mix_g, v_ln_mix_b, v_mlp_w1, v_mlp_w2,
                             v_ln_ffn_g, v_ln_ffn_b)))
    seq = x.shape[1]
    pad = (-(N_META + seq)) % QB
    xi, yi, ci = _pos()
    chip = 2 * xi + yi

    shard_shapes = [wts[n].shape for n in BIG]
    gathered = _chip_allgather(_pack_flat([wts[n].astype(BF16) for n in BIG], PACK_ROWS), "gather_weights")
    parts = [_unpack_flat(gathered[j], shard_shapes) for j in range(N_CHIP)]
    full = {n: jnp.concatenate([parts[j][i] for j in range(N_CHIP)], axis=BIG_AXIS[n]) for i, n in enumerate(BIG)}
    mcols = meta_tokens.shape[1]
    meta_place = lax.dynamic_update_slice(jnp.zeros((N_META, D), F32), 0.5 * meta_tokens, (0, chip * mcols))
    meta_full = _small_allreduce(meta_place, "gather_meta")

    w = {
        "ab_w_in": _pad_ab_cols(full["ab_w_in"][0]), "conv_w": None, "a_log": ab_a_log, "dt_bias": ab_dt_bias,
        "ab_gnorm_g": ab_gnorm_g, "ab_w_out": full["ab_w_out"][0], "c_w_in": full["c_w_in"][0], "c_lb_raw": c_lb_raw,
        "c_gnorm_g": c_gnorm_g, "c_w_out": full["c_w_out"][0], "ln_mix_g": ln_mix_g, "ln_mix_b": ln_mix_b,
        "ln_ffn_g": ln_ffn_g, "ln_ffn_b": ln_ffn_b, "mlp_w1": full["mlp_w1"], "mlp_w2": full["mlp_w2"],
    }
    w["conv_w"] = full["ab_conv_w"][0].astype(F32)

    h0 = jnp.concatenate([jnp.zeros((pad, D), F32), meta_full, x[0]], axis=0)
    tgt = jnp.concatenate([jnp.zeros((pad + N_META, D), F32), loss_target[0]], axis=0)
    loss8, dh0, g = _local_step(h0, tgt, w, pad)
    loss = lax.psum(loss8[0, 0], ("x", "y", "c"))
    grad_x = dh0[pad + N_META:][None]

    gsmall = {"ln_mix_g": g["ln_mix_g"], "ln_mix_b": g["ln_mix_b"], "ln_ffn_g": g["ln_ffn_g"], "ln_ffn_b": g["ln_ffn_b"],
              "c_lb_raw": g["c_lb_raw"], "ab_a_log": g["a_log"], "ab_dt_bias": g["dt_bias"], "ab_gnorm_g": g["ab_gnorm_g"],
              "c_gnorm_g": g["c_gnorm_g"]}
    sbuf = jnp.concatenate([_pack_small(gsmall), dh0[pad:pad + N_META]], axis=0)
    ssum = _small_allreduce(sbuf, "allreduce_small")
    grads = _unpack_small(ssum[:SMALL_ROWS], wts)
    grads["meta_tokens"] = lax.dynamic_slice(ssum[SMALL_ROWS:], (0, chip * mcols), (N_META, mcols))

    gfull = {"ab_w_in": _unpad_ab_cols(g["ab_w_in"])[None], "ab_conv_w": g["conv_w"][None], "ab_w_out": g["ab_w_out"][None],
             "c_w_in": g["c_w_in"][None], "c_w_out": g["c_w_out"][None], "mlp_w1": g["mlp_w1"], "mlp_w2": g["mlp_w2"]}
    packed = []
    for j in range(N_CHIP):
        pieces = []
        for n in BIG:
            ax, size = BIG_AXIS[n], wts[n].shape[BIG_AXIS[n]]
            pieces.append(lax.slice_in_dim(gfull[n], j * size, (j + 1) * size, axis=ax).astype(BF16))
        packed.append(_pack_flat(pieces, PACK_ROWS))
    gpack = jnp.stack(packed).reshape(N_CHIP, 2, PACK_ROWS // 2, D)
    landed = _grad_alltoall(gpack, "grad_alltoall")
    half = _sum_slots(landed, "grad_sum")
    gshard = _sibling_exchange(half, "grad_swap").reshape(PACK_ROWS, D)
    for n, gs in zip(BIG, _unpack_flat(gshard, shard_shapes)):
        grads[n] = gs

    delta, new_m, new_v = {}, {}, {}
    for n in ("meta_tokens",) + BIG:
        shp = wts[n].shape
        to2 = lambda a: a.reshape(-1, shp[-1])
        d2, m2, v2 = _adamw(to2(wts[n]), to2(grads[n]), to2(mom_m[n]), to2(mom_v[n]), f"adamw_{n}")
        delta[n], new_m[n], new_v[n] = d2.reshape(shp), m2.reshape(shp), v2.reshape(shp)
    d2, m2, v2 = _adamw(_pack_small(wts), ssum[:SMALL_ROWS], _pack_small(mom_m), _pack_small(mom_v), "adamw_small")
    delta.update(_unpack_small(d2, wts))
    new_m.update(_unpack_small(m2, wts))
    new_v.update(_unpack_small(v2, wts))

    return (loss, grad_x, *[grads[n] for n in names], *[delta[n] for n in names], *[new_m[n] for n in names],
            *[new_v[n] for n in names])
```

```python
import functools
import math

import numpy as np
import jax
import jax.numpy as jnp
from jax import lax
from jax.experimental import pallas as pl
from jax.experimental.pallas import tpu as pltpu

F32 = jnp.float32
BF16 = jnp.bfloat16

D = 1024
N_META = 16
D_FF = 4 * D
DEPTH = 2
GDN_H = 4
SB_H = 8
SB_DH = 64
HG_H = 8
HD = 128
CH = 64
QB = 128
ALPHA = float((2 * DEPTH) ** 0.25)
LN_EPS = 1e-5
RMS_EPS = 1e-6
L2_EPS = 1e-6
NEG = -1e30

ADAM_LR = 0.001
ADAM_B1 = 0.9
ADAM_B2 = 0.999
ADAM_EPS = 1e-08
ADAM_WD = 0.01
ADAM_STEP = 10

AB_W = 30 * HD
AB_TRUE = 3592
VMEM_LIMIT = 56 * 1024 * 1024

NN = ((1,), (0,))
NT = ((1,), (1,))
TN = ((0,), (0,))


def _cparams(**kw):
    return pltpu.CompilerParams(vmem_limit_bytes=VMEM_LIMIT, **kw)


def _dg(a, b, dims, mode):
    if mode == "h":
        return lax.dot_general(a, b, dims, precision=lax.Precision.HIGHEST, preferred_element_type=F32)
    if mode == "b":
        return lax.dot_general(a.astype(BF16), b.astype(BF16), dims, preferred_element_type=F32)
    ah, bh = a.astype(BF16), b.astype(BF16)
    al, bl = (a - ah.astype(F32)).astype(BF16), (b - bh.astype(F32)).astype(BF16)
    d = lambda x, y: lax.dot_general(x, y, dims, preferred_element_type=F32)
    return d(ah, bh) + (d(ah, bl) + d(al, bh))


def _make_dots(mode, batched=False):
    if batched:
        nn_d, nt_d, tn_d = (((2,), (1,)), ((0,), (0,))), (((2,), (2,)), ((0,), (0,))), (((1,), (1,)), ((0,), (0,)))
    else:
        nn_d, nt_d, tn_d = (NN, ((), ())), (NT, ((), ())), (TN, ((), ()))

    @jax.custom_vjp
    def nn(a, b):
        return _dg(a, b, nn_d, mode)

    @jax.custom_vjp
    def nt(a, b):
        return _dg(a, b, nt_d, mode)

    @jax.custom_vjp
    def tn(a, b):
        return _dg(a, b, tn_d, mode)

    nn.defvjp(lambda a, b: (nn(a, b), (a, b)), lambda r, g: (nt(g, r[1]), tn(r[0], g)))
    nt.defvjp(lambda a, b: (nt(a, b), (a, b)), lambda r, g: (nn(g, r[1]), tn(g, r[0])))
    tn.defvjp(lambda a, b: (tn(a, b), (a, b)), lambda r, g: (nt(r[1], g), nn(r[0], g)))
    return nn, nt, tn


hnn, hnt, htn = _make_dots("h")
bbnn, bbnt, bbtn = _make_dots("b", True)
mbnn, mbnt, mbtn = _make_dots("m", True)
hbnn, hbnt, hbtn = _make_dots("h", True)


def _heads(a, n):
    return jnp.stack([a[:, h * HD:(h + 1) * HD] for h in range(n)])


def _sigmoid(x):
    return jax.nn.sigmoid(x)


def _silu(x):
    return x * jax.nn.sigmoid(x)


def _softplus(x):
    return jnp.maximum(x, 0.0) + jnp.log(1.0 + jnp.exp(-jnp.abs(x)))


def _iota(shape, dim):
    return lax.broadcasted_iota(jnp.int32, shape, dim)


def _pick(n, prefs):
    for p in prefs:
        if n % p == 0:
            return p
    return n


def _mm(a, b, *, ta=False, tb=False, out_dtype=F32, name):
    if ta:
        k_dim, m_dim = a.shape
    else:
        m_dim, k_dim = a.shape
    n_dim = b.shape[0] if tb else b.shape[1]
    assert (b.shape[1] if tb else b.shape[0]) == k_dim
    tm = _pick(m_dim, (1024, 1056, 704, 640, 512, 384, 256, 128))
    tn = _pick(n_dim, (1024, 1056, 704, 640, 512, 384, 256, 128))
    tk = _pick(k_dim, (1024, 1056, 704, 512, 384, 256, 128))
    nk = k_dim // tk
    a_spec = pl.BlockSpec((tk, tm), lambda i, j, k: (k, i)) if ta else pl.BlockSpec((tm, tk), lambda i, j, k: (i, k))
    b_spec = pl.BlockSpec((tn, tk), lambda i, j, k: (j, k)) if tb else pl.BlockSpec((tk, tn), lambda i, j, k: (k, j))
    dims = (((0 if ta else 1,), (1 if tb else 0,)), ((), ()))

    def body(a_ref, b_ref, o_ref, acc_ref):
        part = lax.dot_general(a_ref[...], b_ref[...], dims, preferred_element_type=F32)
        if nk == 1:
            o_ref[...] = part.astype(o_ref.dtype)
        else:
            k = pl.program_id(2)

            @pl.when(k == 0)
            def _():
                acc_ref[...] = part

            @pl.when(k > 0)
            def _():
                acc_ref[...] += part

            @pl.when(k == nk - 1)
            def _():
                o_ref[...] = acc_ref[...].astype(o_ref.dtype)

    return pl.pallas_call(
        body, name=name, grid=(m_dim // tm, n_dim // tn, nk),
        in_specs=[a_spec, b_spec],
        out_specs=pl.BlockSpec((tm, tn), lambda i, j, k: (i, j)),
        out_shape=jax.ShapeDtypeStruct((m_dim, n_dim), out_dtype),
        scratch_shapes=[pltpu.VMEM((tm, tn) if nk > 1 else (8, 128), F32)],
        compiler_params=_cparams(dimension_semantics=("parallel", "parallel", "arbitrary")),
    )(a, b)


def _row_tile(t_pad, width):
    for tr in (528, 352, 176, 128, 64):
        if t_pad % tr == 0 and tr * width * 4 <= (3 << 19) and tr % 16 == 0:
            return tr
    return 64 if t_pad % 64 == 0 else t_pad


def _ln_res_fn(h, m, g, b):
    x = ALPHA * h + m
    mu = jnp.mean(x, axis=-1, keepdims=True)
    xc = x - mu
    var = jnp.mean(xc * xc, axis=-1, keepdims=True)
    return xc * lax.rsqrt(var + LN_EPS) * g + b


def _ln_res_fwd(h, m, g, b, name):
    t_pad = h.shape[0]
    tr = _row_tile(t_pad, D)

    def body(h_ref, m_ref, g_ref, b_ref, y_ref, yb_ref):
        y = _ln_res_fn(h_ref[...], m_ref[...], g_ref[...], b_ref[...])
        y_ref[...] = y
        yb_ref[...] = y.astype(BF16)

    row = pl.BlockSpec((tr, D), lambda i: (i, 0))
    par = pl.BlockSpec((1, D), lambda i: (0, 0))
    return pl.pallas_call(
        body, name=name, grid=(t_pad // tr,), in_specs=[row, row, par, par], out_specs=[row, row],
        out_shape=[jax.ShapeDtypeStruct((t_pad, D), F32), jax.ShapeDtypeStruct((t_pad, D), BF16)],
        compiler_params=_cparams(),
    )(h, m, g, b)


def _ln_res_bwd(h, m, g, b, dy1, dy2, name):
    t_pad = h.shape[0]
    tr = _row_tile(t_pad, D)

    def body(h_ref, m_ref, g_ref, b_ref, d1_ref, d2_ref, dh_ref, dm_ref, dg_ref, db_ref):
        _, vjp = jax.vjp(_ln_res_fn, h_ref[...], m_ref[...], g_ref[...], b_ref[...])
        dh, dm, dg, db = vjp(d1_ref[...] + d2_ref[...])
        dh_ref[...] = dh
        dm_ref[...] = dm.astype(BF16)

        @pl.when(pl.program_id(0) == 0)
        def _():
            dg_ref[...] = jnp.zeros_like(dg_ref)
            db_ref[...] = jnp.zeros_like(db_ref)

        dg_ref[...] += dg
        db_ref[...] += db

    row = pl.BlockSpec((tr, D), lambda i: (i, 0))
    par = pl.BlockSpec((1, D), lambda i: (0, 0))
    return pl.pallas_call(
        body, name=name, grid=(t_pad // tr,), in_specs=[row, row, par, par, row, row],
        out_specs=[row, row, par, par],
        out_shape=[jax.ShapeDtypeStruct((t_pad, D), F32), jax.ShapeDtypeStruct((t_pad, D), BF16),
                   jax.ShapeDtypeStruct((1, D), F32), jax.ShapeDtypeStruct((1, D), F32)],
        compiler_params=_cparams(),
    )(h, m, g, b, dy1, dy2)


def _relu2_fwd(a, name):
    t_pad, w = a.shape
    tr = _row_tile(t_pad, w)

    def body(a_ref, r_ref):
        r = jnp.maximum(a_ref[...], 0.0)
        r_ref[...] = (r * r).astype(BF16)

    row = pl.BlockSpec((tr, w), lambda i: (i, 0))
    return pl.pallas_call(body, name=name, grid=(t_pad // tr,), in_specs=[row], out_specs=row,
                          out_shape=jax.ShapeDtypeStruct((t_pad, w), BF16), compiler_params=_cparams())(a)


def _relu2_bwd(a, dr, name):
    t_pad, w = a.shape
    tr = _row_tile(t_pad, w)

    def body(a_ref, dr_ref, da_ref):
        da_ref[...] = (dr_ref[...] * (2.0 * jnp.maximum(a_ref[...], 0.0))).astype(BF16)

    row = pl.BlockSpec((tr, w), lambda i: (i, 0))
    return pl.pallas_call(body, name=name, grid=(t_pad // tr,), in_specs=[row, row], out_specs=row,
                          out_shape=jax.ShapeDtypeStruct((t_pad, w), BF16), compiler_params=_cparams())(a, dr)


def _grms_fn(o, z, g):
    y = o * lax.rsqrt(jnp.mean(o * o, axis=-1, keepdims=True) + RMS_EPS) * g
    return y * _silu(z)


def _grms_fwd(o, z_arr, z_blk0, g, name):
    t_pad, w = o.shape
    nh = w // HD
    tr = _row_tile(t_pad, HD * 4)

    def body(o_ref, z_ref, g_ref, y_ref):
        y_ref[...] = _grms_fn(o_ref[...], z_ref[...], g_ref[...]).astype(BF16)

    return pl.pallas_call(
        body, name=name, grid=(t_pad // tr, nh),
        in_specs=[pl.BlockSpec((tr, HD), lambda i, h: (i, h)),
                  pl.BlockSpec((tr, HD), lambda i, h: (i, z_blk0 + h)),
                  pl.BlockSpec((1, HD), lambda i, h: (0, 0))],
        out_specs=pl.BlockSpec((tr, HD), lambda i, h: (i, h)),
        out_shape=jax.ShapeDtypeStruct((t_pad, w), BF16), compiler_params=_cparams(),
    )(o, z_arr, g)


def _grms_bwd(o, z_arr, z_blk0, g, dy_arr, dy_blk0, name):
    t_pad, w = o.shape
    nh = w // HD
    tr = _row_tile(t_pad, HD * 4)

    def body(o_ref, z_ref, g_ref, dy_ref, do_ref, dz_ref, dg_ref):
        _, vjp = jax.vjp(_grms_fn, o_ref[...], z_ref[...], g_ref[...])
        do, dz, dg = vjp(dy_ref[...])
        do_ref[...] = do
        dz_ref[...] = dz

        @pl.when((pl.program_id(0) == 0) & (pl.program_id(1) == 0))
        def _():
            dg_ref[...] = jnp.zeros_like(dg_ref)

        dg_ref[...] += dg

    blk = pl.BlockSpec((tr, HD), lambda i, h: (i, h))
    return pl.pallas_call(
        body, name=name, grid=(t_pad // tr, nh),
        in_specs=[blk, pl.BlockSpec((tr, HD), lambda i, h: (i, z_blk0 + h)),
                  pl.BlockSpec((1, HD), lambda i, h: (0, 0)),
                  pl.BlockSpec((tr, HD), lambda i, h: (i, dy_blk0 + h))],
        out_specs=[blk, blk, pl.BlockSpec((1, HD), lambda i, h: (0, 0))],
        out_shape=[jax.ShapeDtypeStruct((t_pad, w), F32), jax.ShapeDtypeStruct((t_pad, w), F32),
                   jax.ShapeDtypeStruct((1, HD), F32)],
        compiler_params=_cparams(),
    )(o, z_arr, g, dy_arr)


def _loss_fwd(y, tgt, first_row, name):
    t_pad = y.shape[0]
    tr = _row_tile(t_pad, D)

    def body(y_ref, t_ref, l_ref, dy_ref):
        rows = pl.program_id(0) * tr + _iota((tr, 1), 0)
        err = jnp.where(rows >= first_row, y_ref[...] - t_ref[...], 0.0)
        dy_ref[...] = err * (1.0 / D)

        @pl.when(pl.program_id(0) == 0)
        def _():
            l_ref[...] = jnp.zeros_like(l_ref)

        part = jnp.sum(jnp.sum(err * err, axis=1, keepdims=True), axis=0, keepdims=True)
        l_ref[...] += jnp.broadcast_to(part * (0.5 / D), l_ref.shape)

    row = pl.BlockSpec((tr, D), lambda i: (i, 0))
    return pl.pallas_call(
        body, name=name, grid=(t_pad // tr,), in_specs=[row, row],
        out_specs=[pl.BlockSpec((8, 128), lambda i: (0, 0)), row],
        out_shape=[jax.ShapeDtypeStruct((8, 128), F32), jax.ShapeDtypeStruct((t_pad, D), F32)],
        compiler_params=_cparams(),
    )(y, tgt)


def _add2(a, b, name):
    t_pad, w = a.shape
    tr = _row_tile(t_pad, w)

    def body(a_ref, b_ref, o_ref):
        o_ref[...] = a_ref[...] + b_ref[...]

    row = pl.BlockSpec((tr, w), lambda i: (i, 0))
    return pl.pallas_call(body, name=name, grid=(t_pad // tr,), in_specs=[row, row], out_specs=row,
                          out_shape=jax.ShapeDtypeStruct((t_pad, w), F32), compiler_params=_cparams())(a, b)


def _assemble_bf16(parts, name):
    t_pad = parts[0][0].shape[0] if parts[0][1] == "cols" else parts[0][0].shape[1]
    widths = [p.shape[1] if kind == "cols" else HD for p, kind in parts]
    total = sum(widths)
    tr = _row_tile(t_pad, total)

    def body(*refs):
        o_ref = refs[-1]
        off = 0
        for ref, (p, kind), w in zip(refs[:-1], parts, widths):
            if kind == "cols":
                o_ref[:, off:off + w] = ref[...].astype(BF16)
            else:
                acc = ref[0]
                for hh in range(1, p.shape[0]):
                    acc = acc + ref[hh]
                o_ref[:, off:off + w] = acc.astype(BF16)
            off += w

    specs = []
    for p, kind in parts:
        if kind == "cols":
            specs.append(pl.BlockSpec((tr, p.shape[1]), lambda i: (i, 0)))
        else:
            specs.append(pl.BlockSpec((p.shape[0], tr, HD), lambda i: (0, i, 0)))
    return pl.pallas_call(
        body, name=name, grid=(t_pad // tr,), in_specs=specs,
        out_specs=pl.BlockSpec((tr, total), lambda i: (i, 0)),
        out_shape=jax.ShapeDtypeStruct((t_pad, total), BF16), compiler_params=_cparams(),
    )(*[p for p, _ in parts])


CONV_K = 4
HALO = 8
RT = 128


def _conv_fwd(p, blk0, w, mode, pad, name):
    t_pad = p.shape[0]
    nt = t_pad // RT
    scale = HD ** -0.5 if mode == "q" else 1.0

    def body(x_ref, w_ref, y_ref, xs_ref):
        xs_ref[0:HALO, :] = jnp.zeros((HALO, HD), F32)
        rows = _iota((t_pad, 1), 0)
        xs_ref[HALO:HALO + t_pad, :] = jnp.where(rows >= pad, x_ref[...], 0.0)
        wv = w_ref[...]

        def tile(i, carry):
            r0 = pl.multiple_of(i * RT, RT)
            ext = xs_ref[pl.ds(r0, RT + HALO), :]
            acc = ext[HALO:, :] * wv[3:4, :]
            for s in (1, 2, 3):
                acc = acc + pltpu.roll(ext, s, 0)[HALO:, :] * wv[3 - s:4 - s, :]
            y = _silu(acc)
            if mode != "v":
                y = y * lax.rsqrt(jnp.sum(y * y, axis=-1, keepdims=True) + L2_EPS) * scale
            y_ref[pl.ds(r0, RT), :] = y
            return carry

        lax.fori_loop(0, nt, tile, 0)

    return pl.pallas_call(
        body, name=name, grid=(GDN_H,),
        in_specs=[pl.BlockSpec((t_pad, HD), lambda h: (0, blk0 + h)), pl.BlockSpec((CONV_K, HD), lambda h: (0, h))],
        out_specs=pl.BlockSpec((t_pad, HD), lambda h: (0, h)),
        out_shape=jax.ShapeDtypeStruct((t_pad, GDN_H * HD), F32),
        scratch_shapes=[pltpu.VMEM((t_pad + HALO, HD), F32)],
        compiler_params=_cparams(),
    )(p, w)


def _conv_bwd(p, blk0, w, dn, mode, pad, name):
    t_pad = p.shape[0]
    nt = t_pad // RT
    scale = HD ** -0.5 if mode == "q" else 1.0

    def body(x_ref, w_ref, dn_ref, dx_ref, dw_ref, xs_ref, ds_ref):
        xs_ref[0:HALO, :] = jnp.zeros((HALO, HD), F32)
        xs_ref[HALO + t_pad:HALO + t_pad + 2 * HALO, :] = jnp.zeros((2 * HALO, HD), F32)
        ds_ref[t_pad:t_pad + HALO, :] = jnp.zeros((HALO, HD), F32)
        rows = _iota((t_pad, 1), 0)
        xs_ref[HALO:HALO + t_pad, :] = jnp.where(rows >= pad, x_ref[...], 0.0)
        ds_ref[0:t_pad, :] = dn_ref[...]
        wv = w_ref[...]

        def tile(i, dw):
            r0 = pl.multiple_of(i * RT, RT)
            ext = xs_ref[pl.ds(r0, RT + 2 * HALO), :]
            dn_e = ds_ref[pl.ds(r0, RT + HALO), :]
            xsh = [ext[HALO:, :]] + [pltpu.roll(ext, s, 0)[HALO:, :] for s in (1, 2, 3)]
            pre = xsh[0] * wv[3:4, :]
            for s in (1, 2, 3):
                pre = pre + xsh[s] * wv[3 - s:4 - s, :]
            sg = _sigmoid(pre)
            y = pre * sg
            if mode != "v":
                ss = jnp.sum(y * y, axis=-1, keepdims=True) + L2_EPS
                r = lax.rsqrt(ss)
                dy = scale * (dn_e * r - y * (r * r * r) * jnp.sum(dn_e * y, axis=-1, keepdims=True))
            else:
                dy = dn_e
            dpre = dy * (sg * (1.0 + pre * (1.0 - sg)))
            dx = dpre[:RT, :] * wv[3:4, :]
            for s in (1, 2, 3):
                dx = dx + pltpu.roll(dpre, RT + HALO - s, 0)[:RT, :] * wv[3 - s:4 - s, :]
            trow = r0 + _iota((RT, 1), 0)
            dx_ref[pl.ds(r0, RT), :] = jnp.where(trow >= pad, dx, 0.0)
            new = []
            for s in (0, 1, 2, 3):
                new.append(dw[s] + jnp.sum(dpre[:RT, :] * xsh[s][:RT, :], axis=0, keepdims=True))
            return tuple(new)

        z = jnp.zeros((1, HD), F32)
        dw = lax.fori_loop(0, nt, tile, (z, z, z, z))
        for s in (0, 1, 2, 3):
            dw_ref[3 - s:4 - s, :] = dw[s]

    return pl.pallas_call(
        body, name=name, grid=(GDN_H,),
        in_specs=[pl.BlockSpec((t_pad, HD), lambda h: (0, blk0 + h)), pl.BlockSpec((CONV_K, HD), lambda h: (0, h)),
                  pl.BlockSpec((t_pad, HD), lambda h: (0, h))],
        out_specs=[pl.BlockSpec((t_pad, HD), lambda h: (0, h)), pl.BlockSpec((CONV_K, HD), lambda h: (0, h))],
        out_shape=[jax.ShapeDtypeStruct((t_pad, GDN_H * HD), F32), jax.ShapeDtypeStruct((CONV_K, GDN_H * HD), F32)],
        scratch_shapes=[pltpu.VMEM((t_pad + 3 * HALO, HD), F32), pltpu.VMEM((t_pad + HALO, HD), F32)],
        compiler_params=_cparams(),
    )(p, w, dn)


def _unit_lower_inv(m, bd, eye):
    md = m * bd
    low = m - md
    p2 = mbnn(md, md)
    p4 = mbnn(p2, p2)
    p8 = mbnn(p4, p4)
    dinv = mbnn(mbnn(mbnn(eye - md, eye + p2), eye + p4), eye + p8)
    n = mbnn(dinv, low)
    n2 = mbnn(n, n)
    return mbnn(mbnn(eye - n, eye + n2), dinv)


def _gdn_chunk(q, k, v, bb, aa, alog, dtb, s, valid):
    nh = q.shape[0]
    ri = _iota((1, CH, CH), 1)
    ci = _iota((1, CH, CH), 2)
    causal = ri >= ci
    strict = ri > ci
    eye = (ri == ci).astype(F32)
    bd = ((ri >> 4) == (ci >> 4)).astype(F32)
    ltri = (_iota((CH, CH), 0) >= _iota((CH, CH), 1)).astype(F32)
    sel = (_iota((nh, 1, HD), 2) == _iota((nh, 1, HD), 0)).astype(F32)

    beta_all = jnp.where(valid, _sigmoid(bb), 0.0)
    g_all = jnp.where(valid, -jnp.exp(alog) * _softplus(aa + dtb), 0.0)
    gc_all = hnn(ltri, g_all)
    beta = jnp.sum(beta_all[None] * sel, axis=2, keepdims=True)
    gc = jnp.sum(gc_all[None] * sel, axis=2, keepdims=True)
    gc_rows = hbnt(jnp.broadcast_to(sel, (nh, CH, HD)), jnp.broadcast_to(gc_all[None], (nh, CH, HD)))
    last = _iota((1, CH, 1), 1) == CH - 1
    gc_last = jnp.sum(jnp.where(last, gc, 0.0), axis=1, keepdims=True)
    decay = jnp.exp(jnp.where(causal, gc - gc_rows, NEG))
    egc = jnp.exp(gc)

    kb = k * beta
    m = jnp.where(strict, bbnt(kb, k) * decay, 0.0)
    t_inv = _unit_lower_inv(m, bd, eye)
    u = bbnn(t_inv, v * beta)
    w = bbnn(t_inv, kb * egc)
    a_intra = bbnt(q, k) * decay
    q_dec = q * egc
    k_dec = k * jnp.exp(gc_last - gc)
    v_new = u - bbnn(w, s)
    o = bbnn(q_dec, s) + bbnn(a_intra, v_new)
    s_new = s * jnp.exp(gc_last) + bbtn(k_dec, v_new)
    return o, s_new


def _gdn_specs(nc, rev):
    cc = (lambda c: nc - 1 - c) if rev else (lambda c: c)
    wide = pl.BlockSpec((CH, GDN_H * HD), lambda c: (cc(c), 0))
    fix = lambda off: pl.BlockSpec((CH, HD), lambda c: (cc(c), off))
    par = pl.BlockSpec((1, HD), lambda c: (0, 0))
    state = pl.BlockSpec((1, GDN_H, HD, HD), lambda c: (cc(c), 0, 0, 0))
    return wide, fix, par, state


def _store_heads(ref, a):
    for h in range(a.shape[0]):
        ref[:, h * HD:(h + 1) * HD] = a[h]


def _gdn_fwd(qn, kn, vn, p, alog, dtb, pad, name):
    t_pad = qn.shape[0]
    nc = t_pad // CH
    wide, fix, par, state = _gdn_specs(nc, False)

    def body(q_ref, k_ref, v_ref, bb_ref, aa_ref, al_ref, dt_ref, o_ref, ss_ref, s_ref):
        c = pl.program_id(0)

        @pl.when(c == 0)
        def _():
            s_ref[...] = jnp.zeros_like(s_ref)

        s = s_ref[...]
        ss_ref[0] = s
        valid = (c * CH + _iota((CH, 1), 0)) >= pad
        o, s_new = _gdn_chunk(_heads(q_ref[...], GDN_H), _heads(k_ref[...], GDN_H), _heads(v_ref[...], GDN_H),
                              bb_ref[...], aa_ref[...], al_ref[...], dt_ref[...], s, valid)
        _store_heads(o_ref, o)
        s_ref[...] = s_new

    return pl.pallas_call(
        body, name=name, grid=(nc,),
        in_specs=[wide, wide, wide, fix(16), fix(17), par, par],
        out_specs=[wide, state],
        out_shape=[jax.ShapeDtypeStruct((t_pad, GDN_H * HD), F32), jax.ShapeDtypeStruct((nc, GDN_H, HD, HD), F32)],
        scratch_shapes=[pltpu.VMEM((GDN_H, HD, HD), F32)],
        compiler_params=_cparams(),
    )(qn, kn, vn, p, p, alog, dtb)


def _gdn_bwd(qn, kn, vn, p, alog, dtb, ssave, do, pad, name):
    t_pad = qn.shape[0]
    nc = t_pad // CH
    wide, fix, par, state = _gdn_specs(nc, True)

    def body(q_ref, k_ref, v_ref, bb_ref, aa_ref, al_ref, dt_ref, ss_ref, do_ref,
             dq_ref, dk_ref, dv_ref, dbb_ref, daa_ref, dal_ref, ddt_ref, ds_ref):
        c = pl.program_id(0)

        @pl.when(c == 0)
        def _():
            ds_ref[...] = jnp.zeros_like(ds_ref)
            dal_ref[...] = jnp.zeros_like(dal_ref)
            ddt_ref[...] = jnp.zeros_like(ddt_ref)

        valid = ((nc - 1 - c) * CH + _iota((CH, 1), 0)) >= pad
        fn = lambda q, k, v, bb, aa, al, dt, s: _gdn_chunk(q, k, v, bb, aa, al, dt, s, valid)
        _, vjp = jax.vjp(fn, _heads(q_ref[...], GDN_H), _heads(k_ref[...], GDN_H), _heads(v_ref[...], GDN_H),
                         bb_ref[...], aa_ref[...], al_ref[...], dt_ref[...], ss_ref[0])
        dq, dk, dv, dbb, daa, dal, ddt, ds = vjp((_heads(do_ref[...], GDN_H), ds_ref[...]))
        _store_heads(dq_ref, dq)
        _store_heads(dk_ref, dk)
        _store_heads(dv_ref, dv)
        dbb_ref[...] = dbb
        daa_ref[...] = daa
        dal_ref[...] += dal
        ddt_ref[...] += ddt
        ds_ref[...] = ds

    sds = jax.ShapeDtypeStruct
    return pl.pallas_call(
        body, name=name, grid=(nc,),
        in_specs=[wide, wide, wide, fix(16), fix(17), par, par, state, wide],
        out_specs=[wide, wide, wide, fix(0), fix(0), par, par],
        out_shape=[sds((t_pad, GDN_H * HD), F32)] * 3 + [sds((t_pad, HD), F32)] * 2 + [sds((1, HD), F32)] * 2,
        scratch_shapes=[pltpu.VMEM((GDN_H, HD, HD), F32)],
        compiler_params=_cparams(),
    )(qn, kn, vn, p, p, alog, dtb, ssave, do)


SB_Q0, SB_K0, SB_V0 = 18, 22, 26
SB_SCALE = SB_DH ** -0.5


def _sb_terms(z, allowed):
    sp = jnp.log(1.0 + jnp.exp(-jnp.abs(z)))
    l1m = jnp.where(allowed, -jnp.maximum(z, 0.0) - sp, 0.0)
    ls = jnp.minimum(z, 0.0) - sp
    return l1m, ls


def _sb_stack(a, i):
    first = _iota((1, HD), 1) < SB_DH
    a2 = jnp.concatenate([jnp.where(first, a, 0.0), jnp.where(first, 0.0, a)], axis=0).astype(BF16)
    rq = i * QB + _iota((QB, 1), 0)
    return a2, jnp.concatenate([rq, rq], axis=0), first


def _dot_hi_lo(a, b2):
    hi = a.astype(BF16)
    lo = (a - hi.astype(F32)).astype(BF16)
    return lax.dot_general(jnp.concatenate([hi, lo], axis=1), b2, (NN, ((), ())), preferred_element_type=F32)


def _sb_fwd(p, pad, name):
    t_pad = p.shape[0]
    nq = t_pad // QB

    def body(q_ref, k_ref, v_ref, o_ref, r_ref):
        i = pl.program_id(1)
        q2, rowq, first = _sb_stack(q_ref[...] * SB_SCALE, i)
        tri = (_iota((QB, QB), 0) > _iota((QB, QB), 1)).astype(BF16)
        upper2 = jnp.concatenate([tri, tri], axis=0)

        def chain(kb, live):
            start = pl.multiple_of(kb * QB, QB)
            kblk = k_ref[pl.ds(start, QB), :].astype(BF16)
            vblk = v_ref[pl.ds(start, QB), :].astype(BF16)
            z = lax.dot_general(q2, kblk, (NT, ((), ())), preferred_element_type=F32)
            colk = kb * QB + _iota((1, QB), 1)
            allowed = (colk < rowq) & (colk >= pad) & live
            l1m, ls = _sb_terms(z, allowed)
            return allowed, ls, _dot_hi_lo(l1m, upper2), jnp.sum(l1m, axis=1, keepdims=True), vblk

        def step(j, carry):
            o_acc, run = carry
            kb_a = i - 2 * j
            kb_b = kb_a - 1
            al_a, ls_a, suf_a, rs_a, v_a = chain(kb_a, True)
            al_b, ls_b, suf_b, rs_b, v_b = chain(jnp.maximum(kb_b, 0), kb_b >= 0)
            w_a = jnp.where(al_a, jnp.exp(ls_a + suf_a + run), 0.0).astype(BF16)
            w_b = jnp.where(al_b, jnp.exp(ls_b + suf_b + (run + rs_a)), 0.0).astype(BF16)
            o_acc = o_acc + lax.dot_general(jnp.concatenate([w_a, w_b], axis=1), jnp.concatenate([v_a, v_b], axis=0),
                                            (NN, ((), ())), preferred_element_type=F32)
            return o_acc, run + rs_a + rs_b

        o_acc, run = lax.fori_loop(0, (i + 2) // 2, step,
                                   (jnp.zeros((2 * QB, HD), F32), jnp.zeros((2 * QB, 1), F32)))
        o_ref[...] = jnp.where(first, o_acc[:QB], o_acc[QB:]).astype(BF16)
        r_ref[...] = jnp.where(first, run[:QB], run[QB:])

    full = lambda off: pl.BlockSpec((t_pad, HD), lambda pr, i: (0, off + pr))
    blk = pl.BlockSpec((QB, HD), lambda pr, i: (i, pr))
    return pl.pallas_call(
        body, name=name, grid=(SB_H // 2, nq),
        in_specs=[pl.BlockSpec((QB, HD), lambda pr, i: (i, SB_Q0 + pr)), full(SB_K0), full(SB_V0)],
        out_specs=[blk, blk],
        out_shape=[jax.ShapeDtypeStruct((t_pad, SB_H * SB_DH), BF16), jax.ShapeDtypeStruct((t_pad, SB_H * SB_DH), F32)],
        compiler_params=_cparams(),
    )(p, p, p)


def _sb_bwd(p, rtot, dy, dy_blk0, pad, name):
    t_pad = p.shape[0]
    nq = t_pad // QB

    def body(q_ref, k_ref, v_ref, r_ref, do_ref, dq_ref, dk_ref, dv_ref):
        i = pl.program_id(1)

        @pl.when(i == 0)
        def _():
            dk_ref[...] = jnp.zeros_like(dk_ref)
            dv_ref[...] = jnp.zeros_like(dv_ref)

        q2, rowq, first = _sb_stack(q_ref[...] * SB_SCALE, i)
        do2, _, _ = _sb_stack(do_ref[...], i)
        rt = r_ref[...]
        lane = _iota((1, HD), 1)
        rcol = jnp.concatenate([jnp.sum(jnp.where(lane == 0, rt, 0.0), axis=1, keepdims=True),
                                jnp.sum(jnp.where(lane == SB_DH, rt, 0.0), axis=1, keepdims=True)], axis=0)
        rj = _iota((QB, QB), 0)
        cs = _iota((QB, QB), 1)
        tri_u = (rj > cs).astype(BF16)
        tri_l = (rj < cs).astype(BF16)
        upper2 = jnp.concatenate([tri_u, tri_u], axis=0)
        lower2 = jnp.concatenate([tri_l, tri_l], axis=0)

        def chain(kb, live):
            start = pl.multiple_of(kb * QB, QB)
            kblk = k_ref[pl.ds(start, QB), :].astype(BF16)
            vblk = v_ref[pl.ds(start, QB), :].astype(BF16)
            z = lax.dot_general(q2, kblk, (NT, ((), ())), preferred_element_type=F32)
            colk = kb * QB + _iota((1, QB), 1)
            allowed = (colk < rowq) & (colk >= pad) & live
            l1m, ls = _sb_terms(z, allowed)
            dwgt = lax.dot_general(do2, vblk, (NT, ((), ())), preferred_element_type=F32)
            return (start, kblk, allowed, ls, _dot_hi_lo(l1m, upper2), jnp.sum(l1m, axis=1, keepdims=True), dwgt,
                    _sigmoid(z))

        def finish(c, seen, gseen):
            start, kblk, allowed, ls, suf, rs, dwgt, sg = c
            wgt = jnp.where(allowed, jnp.exp(ls + suf + (rcol - seen - rs)), 0.0)
            dl = dwgt * wgt
            gpre = gseen + _dot_hi_lo(dl, lower2)
            dz = jnp.where(allowed, dl * (1.0 - sg) - gpre * sg, 0.0).astype(BF16)
            dk_ref[pl.ds(start, QB), :] += lax.dot_general(dz, q2, (TN, ((), ())), preferred_element_type=F32)
            dv_ref[pl.ds(start, QB), :] += lax.dot_general(wgt.astype(BF16), do2, (TN, ((), ())),
                                                           preferred_element_type=F32)
            return dz, seen + rs, gseen + jnp.sum(dl, axis=1, keepdims=True)

        def step(j, carry):
            dq_acc, seen, gseen = carry
            kb_b = 2 * j + 1
            c_a = chain(2 * j, True)
            c_b = chain(jnp.minimum(kb_b, i), kb_b <= i)
            dz_a, seen, gseen = finish(c_a, seen, gseen)
            dz_b, seen, gseen = finish(c_b, seen, gseen)
            dq_acc = dq_acc + lax.dot_general(jnp.concatenate([dz_a, dz_b], axis=1),
                                              jnp.concatenate([c_a[1], c_b[1]], axis=0),
                                              (NN, ((), ())), preferred_element_type=F32)
            return dq_acc, seen, gseen

        zc = jnp.zeros((2 * QB, 1), F32)
        dq_acc, _, _ = lax.fori_loop(0, (i + 2) // 2, step, (jnp.zeros((2 * QB, HD), F32), zc, zc))
        dq_ref[...] = jnp.where(first, dq_acc[:QB], dq_acc[QB:]) * SB_SCALE

    full_in = lambda off: pl.BlockSpec((t_pad, HD), lambda pr, i: (0, off + pr))
    full_out = pl.BlockSpec((t_pad, HD), lambda pr, i: (0, pr))
    blk = pl.BlockSpec((QB, HD), lambda pr, i: (i, pr))
    sds = jax.ShapeDtypeStruct((t_pad, SB_H * SB_DH), F32)
    return pl.pallas_call(
        body, name=name, grid=(SB_H // 2, nq),
        in_specs=[pl.BlockSpec((QB, HD), lambda pr, i: (i, SB_Q0 + pr)), full_in(SB_K0), full_in(SB_V0), blk,
                  pl.BlockSpec((QB, HD), lambda pr, i: (i, dy_blk0 + pr))],
        out_specs=[blk, full_out, full_out],
        out_shape=[sds, sds, sds],
        compiler_params=_cparams(),
    )(p, p, p, rtot, dy)


HG_LEVELS = 6


def _hg_prefix_matrix():
    t = np.arange(CH)[:, None]
    j = np.arange(CH)[None, :]
    groups = [(j <= t)]
    for lvl in range(1, HG_LEVELS + 1):
        half = CH >> lvl
        e = (t // (2 * half)) * (2 * half) + half - 1
        groups.append(j <= e)
    groups.append(np.ones((2 * CH, CH), bool))
    return np.concatenate(groups, axis=0).astype(np.float32)


HG_G = 4


def _hg_chunk(qr, fr, iv, r0, r1, st, valid, ecat):
    g = st.shape[0]
    mx = jnp.maximum(r0, r1)
    e0 = jnp.exp(r0 - mx)
    e1 = jnp.exp(r1 - mx)
    lb = e1 / (e0 + e1)
    fg = lb + (1.0 - lb) * _sigmoid(fr)
    logf = jnp.where(valid, jnp.log(fg), 0.0)
    kk = jnp.where(valid, 1.0 - fg, 0.0)
    q = jnp.where(valid, _silu(qr), 0.0)
    v = _heads(jnp.where(valid, iv, 0.0), g)

    pre = hnn(ecat, logf)
    b = pre[0:CH]
    b_last = pre[(HG_LEVELS + 1) * CH:]
    row = _iota((CH, 1), 0)
    ri = _iota((1, CH, CH), 1)
    ci = _iota((1, CH, CH), 2)
    a = jnp.where(ri == ci, jnp.sum(_heads(q * kk, g), axis=2, keepdims=True), 0.0)
    for lvl in range(1, HG_LEVELS + 1):
        half = CH >> lvl
        m = pre[lvl * CH:(lvl + 1) * CH]
        low = (row & half) != 0
        qt = jnp.where(low, q * jnp.exp(jnp.where(low, b - m, 0.0)), 0.0)
        kt = jnp.where(low, 0.0, kk * jnp.exp(jnp.where(low, 0.0, m - b)))
        same = (ri >> (7 - lvl)) == (ci >> (7 - lvl))
        a = a + jnp.where(same, bbnt(_heads(qt, g), _heads(kt, g)), 0.0)
    o = bbnt(_heads(q * jnp.exp(b), g), st) + bbnn(a, v)
    kd = kk * jnp.exp(b_last[0:CH] - b)
    st_new = st * _heads(jnp.exp(b_last), g) + bbtn(v, _heads(kd, g))
    return o, st_new


def _hg_specs(nc, rev):
    cc = (lambda c: nc - 1 - c) if rev else (lambda c: c)
    ng = HG_H // HG_G
    blk = lambda off: pl.BlockSpec((CH, HG_G * HD), lambda h, c: (cc(c), off * ng + h))
    lbs = pl.BlockSpec((2, HG_G * HD), lambda h, c: (0, h))
    state = pl.BlockSpec((1, HG_G, HD, HD), lambda h, c: (cc(c), h, 0, 0))
    return ng, blk, lbs, state


def _hg_fwd(p, lbraw, ecat, pad, name):
    t_pad = p.shape[0]
    nc = t_pad // CH
    ng, blk, lbs, state = _hg_specs(nc, False)

    def body(q_ref, f_ref, i_ref, lb_ref, e_ref, o_ref, ss_ref, s_ref):
        c = pl.program_id(1)

        @pl.when(c == 0)
        def _():
            s_ref[...] = jnp.zeros_like(s_ref)

        st = s_ref[...]
        ss_ref[0] = st
        valid = (c * CH + _iota((CH, 1), 0)) >= pad
        o, st_new = _hg_chunk(q_ref[...], f_ref[...], i_ref[...], lb_ref[0:1, :], lb_ref[1:2, :], st, valid, e_ref[...])
        _store_heads(o_ref, o)
        s_ref[...] = st_new

    return pl.pallas_call(
        body, name=name, grid=(ng, nc),
        in_specs=[blk(0), blk(1), blk(2), lbs, pl.BlockSpec(ecat.shape, lambda h, c: (0, 0))],
        out_specs=[blk(0), state],
        out_shape=[jax.ShapeDtypeStruct((t_pad, HG_H * HD), F32), jax.ShapeDtypeStruct((nc, HG_H, HD, HD), F32)],
        scratch_shapes=[pltpu.VMEM((HG_G, HD, HD), F32)],
        compiler_params=_cparams(),
    )(p, p, p, lbraw, ecat)


def _hg_bwd(p, lbraw, ecat, ssave, do, pad, name):
    t_pad = p.shape[0]
    nc = t_pad // CH
    ng, blk, lbs, state = _hg_specs(nc, True)

    def body(q_ref, f_ref, i_ref, lb_ref, e_ref, ss_ref, do_ref, dq_ref, df_ref, di_ref, dlb_ref, ds_ref):
        c = pl.program_id(1)

        @pl.when(c == 0)
        def _():
            ds_ref[...] = jnp.zeros_like(ds_ref)
            dlb_ref[...] = jnp.zeros_like(dlb_ref)

        valid = ((nc - 1 - c) * CH + _iota((CH, 1), 0)) >= pad
        ecv = e_ref[...]
        fn = lambda qr, fr, iv, r0, r1, st: _hg_chunk(qr, fr, iv, r0, r1, st, valid, ecv)
        _, vjp = jax.vjp(fn, q_ref[...], f_ref[...], i_ref[...], lb_ref[0:1, :], lb_ref[1:2, :], ss_ref[0])
        dq, df, di, d0, d1, ds = vjp((_heads(do_ref[...], HG_G), ds_ref[...]))
        dq_ref[...] = dq
        df_ref[...] = df
        di_ref[...] = di
        dlb_ref[0:1, :] += d0
        dlb_ref[1:2, :] += d1
        ds_ref[...] = ds

    sds = jax.ShapeDtypeStruct((t_pad, HG_H * HD), F32)
    return pl.pallas_call(
        body, name=name, grid=(ng, nc),
        in_specs=[blk(0), blk(1), blk(2), lbs, pl.BlockSpec(ecat.shape, lambda h, c: (0, 0)), state, blk(0)],
        out_specs=[blk(0), blk(0), blk(0), lbs],
        out_shape=[sds, sds, sds, jax.ShapeDtypeStruct((2, HG_H * HD), F32)],
        scratch_shapes=[pltpu.VMEM((HG_G, HD, HD), F32)],
        compiler_params=_cparams(),
    )(p, p, p, lbraw, ecat, ssave, do)


def _pad_ab_cols(w):
    z = jnp.zeros((w.shape[0], HD - GDN_H), w.dtype)
    return jnp.concatenate([w[:, :2048], w[:, 2048:2052], z, w[:, 2052:2056], z, w[:, 2056:]], axis=1)


def _unpad_ab_cols(w):
    return jnp.concatenate([w[:, :2048], w[:, 2048:2052], w[:, 2176:2180], w[:, 2304:]], axis=1)


def _lane_pad(v):
    return jnp.pad(v, ((0, 0), (0, HD - v.shape[1])))


def _mlp_fwd(hb, w1, w2, tag):
    a = _mm(hb, w1, name=f"mlp_up_{tag}")
    r = _relu2_fwd(a, f"relu2_{tag}")
    m = _mm(r, w2, name=f"mlp_down_{tag}")
    return a, r, m


def _mlp_bwd(hb, a, r, dmb, w1, w2, tag):
    dr = _mm(dmb, w2, tb=True, name=f"mlp_down_dx_{tag}")
    dw2 = _mm(r, dmb, ta=True, name=f"mlp_down_dw_{tag}")
    da = _relu2_bwd(a, dr, f"relu2_bwd_{tag}")
    dh = _mm(da, w1, tb=True, name=f"mlp_up_dx_{tag}")
    dw1 = _mm(hb, da, ta=True, name=f"mlp_up_dw_{tag}")
    return dh, dw1, dw2


def _local_step(h0, tgt, w, pad):
    row = lambda a, i: a[i:i + 1]
    ecat = jnp.asarray(_hg_prefix_matrix())
    cw = [w["conv_w"][:, i * 512:(i + 1) * 512] for i in range(3)]
    alog, dtb = _lane_pad(w["a_log"]), _lane_pad(w["dt_bias"])

    h0b = h0.astype(BF16)
    p0 = _mm(h0b, w["ab_w_in"], name="ab_in")
    qn = _conv_fwd(p0, 0, cw[0], "q", pad, "conv_q")
    kn = _conv_fwd(p0, 4, cw[1], "k", pad, "conv_k")
    vn = _conv_fwd(p0, 8, cw[2], "v", pad, "conv_v")
    oa_raw, ss0 = _gdn_fwd(qn, kn, vn, p0, alog, dtb, pad, "gdn_fwd")
    oa = _grms_fwd(oa_raw, p0, 12, w["ab_gnorm_g"], "gdn_gate")
    ob, rtot = _sb_fwd(p0, pad, "sb_fwd")
    ycat = jnp.concatenate([oa, ob], axis=1)
    mix0 = _mm(ycat, w["ab_w_out"], name="ab_out")
    h1, h1b = _ln_res_fwd(h0, mix0, row(w["ln_mix_g"], 0), row(w["ln_mix_b"], 0), "ln_mix_0")
    a0, r0, m0 = _mlp_fwd(h1b, w["mlp_w1"][0], w["mlp_w2"][0], "0")
    h2, h2b = _ln_res_fwd(h1, m0, row(w["ln_ffn_g"], 0), row(w["ln_ffn_b"], 0), "ln_ffn_0")
    p1 = _mm(h2b, w["c_w_in"], name="c_in")
    oc_raw, ss1 = _hg_fwd(p1, w["c_lb_raw"], ecat, pad, "hg_fwd")
    yc = _grms_fwd(oc_raw, p1, 3 * HG_H, w["c_gnorm_g"], "hg_gate")
    mix1 = _mm(yc, w["c_w_out"], name="c_out")
    h3, h3b = _ln_res_fwd(h2, mix1, row(w["ln_mix_g"], 1), row(w["ln_mix_b"], 1), "ln_mix_1")
    a1, r1, m1 = _mlp_fwd(h3b, w["mlp_w1"][1], w["mlp_w2"][1], "1")
    h4, _ = _ln_res_fwd(h3, m1, row(w["ln_ffn_g"], 1), row(w["ln_ffn_b"], 1), "ln_ffn_1")
    loss, dh4 = _loss_fwd(h4, tgt, pad + N_META, "loss")

    zero = jnp.zeros_like(dh4)
    dh3a, dm1b, dfg1, dfb1 = _ln_res_bwd(h3, m1, row(w["ln_ffn_g"], 1), row(w["ln_ffn_b"], 1), dh4, zero, "ln_ffn_bwd_1")
    dh3b, dw1_1, dw2_1 = _mlp_bwd(h3b, a1, r1, dm1b, w["mlp_w1"][1], w["mlp_w2"][1], "1")
    dh2a, dmix1b, dmg1, dmb1 = _ln_res_bwd(h2, mix1, row(w["ln_mix_g"], 1), row(w["ln_mix_b"], 1), dh3a, dh3b, "ln_mix_bwd_1")
    dyc = _mm(dmix1b, w["c_w_out"], tb=True, name="c_out_dx")
    dwco = _mm(yc, dmix1b, ta=True, name="c_out_dw")
    doc, dzc, dcg = _grms_bwd(oc_raw, p1, 3 * HG_H, w["c_gnorm_g"], dyc, 0, "hg_gate_bwd")
    dq1, df1, di1, dlb = _hg_bwd(p1, w["c_lb_raw"], ecat, ss1, doc, pad, "hg_bwd")
    dp1 = _assemble_bf16([(dq1, "cols"), (df1, "cols"), (di1, "cols"), (dzc, "cols")], "c_in_dy")
    dh2b = _mm(dp1, w["c_w_in"], tb=True, name="c_in_dx")
    dwc = _mm(h2b, dp1, ta=True, name="c_in_dw")
    dh1a, dm0b, dfg0, dfb0 = _ln_res_bwd(h1, m0, row(w["ln_ffn_g"], 0), row(w["ln_ffn_b"], 0), dh2a, dh2b, "ln_ffn_bwd_0")
    dh1b, dw1_0, dw2_0 = _mlp_bwd(h1b, a0, r0, dm0b, w["mlp_w1"][0], w["mlp_w2"][0], "0")
    dh0a, dmix0b, dmg0, dmb0 = _ln_res_bwd(h0, mix0, row(w["ln_mix_g"], 0), row(w["ln_mix_b"], 0), dh1a, dh1b, "ln_mix_bwd_0")
    dycat = _mm(dmix0b, w["ab_w_out"], tb=True, name="ab_out_dx")
    dwabo = _mm(ycat, dmix0b, ta=True, name="ab_out_dw")
    doa, dza, dag = _grms_bwd(oa_raw, p0, 12, w["ab_gnorm_g"], dycat, 0, "gdn_gate_bwd")
    dqn, dkn, dvn, dbb, daa, dal, ddt = _gdn_bwd(qn, kn, vn, p0, alog, dtb, ss0, doa, pad, "gdn_bwd")
    dpq, dcq = _conv_bwd(p0, 0, cw[0], dqn, "q", pad, "conv_q_bwd")
    dpk, dck = _conv_bwd(p0, 4, cw[1], dkn, "k", pad, "conv_k_bwd")
    dpv, dcv = _conv_bwd(p0, 8, cw[2], dvn, "v", pad, "conv_v_bwd")
    dqb, dkb, dvb = _sb_bwd(p0, rtot, dycat, 4, pad, "sb_bwd")
    dp0 = _assemble_bf16([(dpq, "cols"), (dpk, "cols"), (dpv, "cols"), (dza, "cols"), (dbb, "cols"), (daa, "cols"),
                          (dqb, "cols"), (dkb, "cols"), (dvb, "cols")], "ab_in_dy")
    dh0b = _mm(dp0, w["ab_w_in"], tb=True, name="ab_in_dx")
    dwab = _mm(h0b, dp0, ta=True, name="ab_in_dw")
    dh0 = _add2(dh0a, dh0b, "dh0")

    grads = {
        "ab_w_in": dwab, "conv_w": jnp.concatenate([dcq, dck, dcv], axis=1),
        "a_log": dal[:, :GDN_H], "dt_bias": ddt[:, :GDN_H],
        "ab_gnorm_g": dag, "ab_w_out": dwabo, "c_w_in": dwc, "c_lb_raw": dlb, "c_gnorm_g": dcg, "c_w_out": dwco,
        "ln_mix_g": jnp.concatenate([dmg0, dmg1], 0), "ln_mix_b": jnp.concatenate([dmb0, dmb1], 0),
        "mlp_w1": jnp.stack([dw1_0, dw1_1]), "mlp_w2": jnp.stack([dw2_0, dw2_1]),
        "ln_ffn_g": jnp.concatenate([dfg0, dfg1], 0), "ln_ffn_b": jnp.concatenate([dfb0, dfb1], 0),
    }
    return loss, dh0, grads


MESH = pl.DeviceIdType.MESH
ANY = pl.BlockSpec(memory_space=pl.ANY)
N_CHIP = 4
N_DEV = 8
CHIP_REL = ((1, 0), (0, 1), (1, 1))
DEV_REL = tuple((dx, dy, dc) for dx in (0, 1) for dy in (0, 1) for dc in (0, 1))[1:]


def _pos():
    return lax.axis_index("x"), lax.axis_index("y"), lax.axis_index("c")


def _flip(a, d):
    return a + d - 2 * a * d


def _chip_allgather(buf, name):
    def body(x_ref, o_ref, send_sems, recv_sems, local_sem):
        x, y, c = _pos()
        mine = pltpu.make_async_copy(x_ref, o_ref.at[2 * x + y], local_sem)
        mine.start()

        def copy(k):
            tx, ty = _flip(x, CHIP_REL[k][0]), _flip(y, CHIP_REL[k][1])
            return tx, ty, (lambda slot: pltpu.make_async_remote_copy(
                src_ref=x_ref, dst_ref=o_ref.at[slot], send_sem=send_sems.at[k], recv_sem=recv_sems.at[k],
                device_id=(tx, ty, c), device_id_type=MESH))

        sends = []
        for k in range(3):
            _, _, mk = copy(k)
            cp = mk(2 * x + y)
            cp.start()
            sends.append(cp)
        for k in range(3):
            tx, ty, mk = copy(k)
            mk(2 * tx + ty).wait_recv()
        for cp in sends:
            cp.wait_send()
        mine.wait()

    return pl.pallas_call(
        body, name=name, in_specs=[ANY], out_specs=ANY,
        out_shape=jax.ShapeDtypeStruct((N_CHIP,) + buf.shape, buf.dtype),
        scratch_shapes=[pltpu.SemaphoreType.DMA((3,)), pltpu.SemaphoreType.DMA((3,)), pltpu.SemaphoreType.DMA],
        compiler_params=pltpu.CompilerParams(has_side_effects=True),
    )(buf)


def _grad_alltoall(g, name):
    def body(g_ref, o_ref, send_sems, recv_sems, local_sem):
        x, y, c = _pos()
        me = 4 * x + 2 * y + c
        mine = pltpu.make_async_copy(g_ref.at[2 * x + y, c], o_ref.at[me], local_sem)
        mine.start()

        def target(k):
            dx, dy, dc = DEV_REL[k]
            return _flip(x, dx), _flip(y, dy), _flip(c, dc)

        sends = []
        for k in range(N_DEV - 1):
            tx, ty, tc = target(k)
            cp = pltpu.make_async_remote_copy(
                src_ref=g_ref.at[2 * tx + ty, tc], dst_ref=o_ref.at[me], send_sem=send_sems.at[k],
                recv_sem=recv_sems.at[k], device_id=(tx, ty, tc), device_id_type=MESH)
            cp.start()
            sends.append(cp)
        for k in range(N_DEV - 1):
            tx, ty, tc = target(k)
            pltpu.make_async_remote_copy(
                src_ref=g_ref.at[0, 0], dst_ref=o_ref.at[4 * tx + 2 * ty + tc], send_sem=send_sems.at[k],
                recv_sem=recv_sems.at[k], device_id=(tx, ty, tc), device_id_type=MESH).wait_recv()
        for cp in sends:
            cp.wait_send()
        mine.wait()

    return pl.pallas_call(
        body, name=name, in_specs=[ANY], out_specs=ANY,
        out_shape=jax.ShapeDtypeStruct((N_DEV,) + g.shape[2:], g.dtype),
        scratch_shapes=[pltpu.SemaphoreType.DMA((N_DEV - 1,)), pltpu.SemaphoreType.DMA((N_DEV - 1,)),
                        pltpu.SemaphoreType.DMA],
        compiler_params=pltpu.CompilerParams(has_side_effects=True),
    )(g)


def _sum_slots(r, name):
    n, rh, w = r.shape
    tr = _pick(rh, (656, 328, 80, 16))

    def body(r_ref, o_ref):
        acc = r_ref[0].astype(F32)
        for s in range(1, n):
            acc = acc + r_ref[s].astype(F32)
        o_ref[...] = acc

    return pl.pallas_call(
        body, name=name, grid=(rh // tr,), in_specs=[pl.BlockSpec((n, tr, w), lambda i: (0, i, 0))],
        out_specs=pl.BlockSpec((tr, w), lambda i: (i, 0)), out_shape=jax.ShapeDtypeStruct((rh, w), F32),
        compiler_params=_cparams(),
    )(r)


def _sibling_exchange(half, name):
    def body(h_ref, o_ref, send_sem, recv_sem, local_sem):
        x, y, c = _pos()
        mine = pltpu.make_async_copy(h_ref, o_ref.at[c], local_sem)
        mine.start()
        cp = pltpu.make_async_remote_copy(src_ref=h_ref, dst_ref=o_ref.at[c], send_sem=send_sem, recv_sem=recv_sem,
                                          device_id=(x, y, 1 - c), device_id_type=MESH)
        cp.start()
        pltpu.make_async_remote_copy(src_ref=h_ref, dst_ref=o_ref.at[1 - c], send_sem=send_sem, recv_sem=recv_sem,
                                     device_id=(x, y, 1 - c), device_id_type=MESH).wait_recv()
        cp.wait_send()
        mine.wait()

    return pl.pallas_call(
        body, name=name, in_specs=[ANY], out_specs=ANY,
        out_shape=jax.ShapeDtypeStruct((2,) + half.shape, half.dtype),
        scratch_shapes=[pltpu.SemaphoreType.DMA, pltpu.SemaphoreType.DMA, pltpu.SemaphoreType.DMA],
        compiler_params=pltpu.CompilerParams(has_side_effects=True),
    )(half)


def _small_allreduce(buf, name):
    r, w = buf.shape

    def body(b_ref, o_ref, land_ref, send_sems, recv_sems):
        x, y, c = _pos()
        me = 4 * x + 2 * y + c
        land_ref[me] = b_ref[...]

        def target(k):
            dx, dy, dc = DEV_REL[k]
            return _flip(x, dx), _flip(y, dy), _flip(c, dc)

        sends = []
        for k in range(N_DEV - 1):
            tx, ty, tc = target(k)
            cp = pltpu.make_async_remote_copy(
                src_ref=b_ref, dst_ref=land_ref.at[me], send_sem=send_sems.at[k], recv_sem=recv_sems.at[k],
                device_id=(tx, ty, tc), device_id_type=MESH)
            cp.start()
            sends.append(cp)
        for k in range(N_DEV - 1):
            tx, ty, tc = target(k)
            pltpu.make_async_remote_copy(
                src_ref=b_ref, dst_ref=land_ref.at[4 * tx + 2 * ty + tc], send_sem=send_sems.at[k],
                recv_sem=recv_sems.at[k], device_id=(tx, ty, tc), device_id_type=MESH).wait_recv()
        for cp in sends:
            cp.wait_send()
        acc = land_ref[0]
        for s in range(1, N_DEV):
            acc = acc + land_ref[s]
        o_ref[...] = acc

    vm = pl.BlockSpec(memory_space=pltpu.VMEM)
    return pl.pallas_call(
        body, name=name, in_specs=[vm], out_specs=vm, out_shape=jax.ShapeDtypeStruct((r, w), F32),
        scratch_shapes=[pltpu.VMEM((N_DEV, r, w), F32), pltpu.SemaphoreType.DMA((N_DEV - 1,)),
                        pltpu.SemaphoreType.DMA((N_DEV - 1,))],
        compiler_params=pltpu.CompilerParams(has_side_effects=True),
    )(buf)


def _adamw(w, g, m, v, name):
    r, c = w.shape
    tr = _pick(r, (256, 128, 64, 8)) if r * c > (1 << 18) else r

    def body(w_ref, g_ref, m_ref, v_ref, d_ref, m2_ref, v2_ref):
        gg = g_ref[...]
        m2 = ADAM_B1 * m_ref[...] + (1.0 - ADAM_B1) * gg
        v2 = ADAM_B2 * v_ref[...] + (1.0 - ADAM_B2) * (gg * gg)
        m_hat = m2 / (1.0 - ADAM_B1 ** ADAM_STEP)
        v_hat = v2 / (1.0 - ADAM_B2 ** ADAM_STEP)
        d_ref[...] = -ADAM_LR * (m_hat / (jnp.sqrt(v_hat) + ADAM_EPS) + ADAM_WD * w_ref[...])
        m2_ref[...] = m2
        v2_ref[...] = v2

    blk = pl.BlockSpec((tr, c), lambda i: (i, 0))
    sds = jax.ShapeDtypeStruct((r, c), F32)
    return pl.pallas_call(body, name=name, grid=(r // tr,), in_specs=[blk] * 4, out_specs=[blk] * 3,
                          out_shape=[sds] * 3, compiler_params=_cparams())(w, g, m, v)


BIG = ("ab_w_in", "ab_conv_w", "ab_w_out", "c_w_in", "c_w_out", "mlp_w1", "mlp_w2")
BIG_AXIS = {"ab_w_in": 2, "ab_conv_w": 2, "ab_w_out": 1, "c_w_in": 2, "c_w_out": 1, "mlp_w1": 2, "mlp_w2": 1}
SMALL = ("ln_mix_g", "ln_mix_b", "ln_ffn_g", "ln_ffn_b", "c_lb_raw", "ab_a_log", "ab_dt_bias", "ab_gnorm_g", "c_gnorm_g")
SMALL_ROWS = 16
PACK_ROWS = 6560


def _pack_flat(pieces, rows):
    flat = jnp.concatenate([p.reshape(-1) for p in pieces])
    return jnp.pad(flat, (0, rows * D - flat.shape[0])).reshape(rows, D)


def _unpack_flat(buf, shapes):
    flat = buf.reshape(-1)
    out, off = [], 0
    for s in shapes:
        n = math.prod(s)
        out.append(flat[off:off + n].reshape(s))
        off += n
    return out


def _pack_small(d):
    rows = [jnp.pad(d[n], ((0, 0), (0, D - d[n].shape[1]))) for n in SMALL]
    buf = jnp.concatenate(rows, axis=0)
    return jnp.pad(buf, ((0, SMALL_ROWS - buf.shape[0]), (0, 0)))


def _unpack_small(buf, like):
    out, r = {}, 0
    for n in SMALL:
        nr, nc = like[n].shape
        out[n] = buf[r:r + nr, :nc]
        r += nr
    return out


def kernel(x, meta_tokens, ab_w_in, ab_conv_w, ab_a_log, ab_dt_bias, ab_gnorm_g, ab_w_out, c_w_in, c_lb_raw, c_gnorm_g, c_w_out, ln_mix_g, ln_mix_b, mlp_w1, mlp_w2, ln_ffn_g, ln_ffn_b, loss_target, m_meta_tokens, m_ab_w_in, m_ab_conv_w, m_ab_a_log, m_ab_dt_bias, m_ab_gnorm_g, m_ab_w_out, m_c_w_in, m_c_lb_raw, m_c_gnorm_g, m_c_w_out, m_ln_mix_g, m_ln_mix_b, m_mlp_w1, m_mlp_w2, m_ln_ffn_g, m_ln_ffn_b, v_meta_tokens, v_ab_w_in, v_ab_conv_w, v_ab_a_log, v_ab_dt_bias, v_ab_gnorm_g, v_ab_w_out, v_c_w_in, v_c_lb_raw, v_c_gnorm_g, v_c_w_out, v_ln_mix_g, v_ln_mix_b, v_mlp_w1, v_mlp_w2, v_ln_ffn_g, v_ln_ffn_b):
    names = ("meta_tokens", "ab_w_in", "ab_conv_w", "ab_a_log", "ab_dt_bias", "ab_gnorm_g", "ab_w_out", "c_w_in",
             "c_lb_raw", "c_gnorm_g", "c_w_out", "ln_mix_g", "ln_mix_b", "mlp_w1", "mlp_w2", "ln_ffn_g", "ln_ffn_b")
    wts = dict(zip(names, (meta_tokens, ab_w_in, ab_conv_w, ab_a_log, ab_dt_bias, ab_gnorm_g, ab_w_out, c_w_in, c_lb_raw,
                           c_gnorm_g, c_w_out, ln_mix_g, ln_mix_b, mlp_w1, mlp_w2, ln_ffn_g, ln_ffn_b)))
    mom_m = dict(zip(names, (m_meta_tokens, m_ab_w_in, m_ab_conv_w, m_ab_a_log, m_ab_dt_bias, m_ab_gnorm_g, m_ab_w_out,
                             m_c_w_in, m_c_lb_raw, m_c_gnorm_g, m_c_w_out, m_ln_mix_g, m_ln_mix_b, m_mlp_w1, m_mlp_w2,
                             m_ln_ffn_g, m_ln_ffn_b)))
    mom_v = dict(zip(names, (v_meta_tokens, v_ab_w_in, v_ab_conv_w, v_ab_a_log, v_ab_dt_bias, v_ab_gnorm_g, v_ab_w_out,
                             v_c_w_in, v_c_lb_raw, v_c_gnorm_g, v_c_w_out, v_ln_mix_g, v_ln_mix_b, v_mlp_w1, v_mlp_w2,
                             v_ln_ffn_g, v_ln_ffn_b)))
    seq = x.shape[1]
    pad = (-(N_META + seq)) % QB
    xi, yi, ci = _pos()
    chip = 2 * xi + yi

    shard_shapes = [wts[n].shape for n in BIG]
    gathered = _chip_allgather(_pack_flat([wts[n].astype(BF16) for n in BIG], PACK_ROWS), "gather_weights")
    parts = [_unpack_flat(gathered[j], shard_shapes) for j in range(N_CHIP)]
    full = {n: jnp.concatenate([parts[j][i] for j in range(N_CHIP)], axis=BIG_AXIS[n]) for i, n in enumerate(BIG)}
    mcols = meta_tokens.shape[1]
    meta_place = lax.dynamic_update_slice(jnp.zeros((N_META, D), F32), 0.5 * meta_tokens, (0, chip * mcols))
    meta_full = _small_allreduce(meta_place, "gather_meta")

    w = {
        "ab_w_in": _pad_ab_cols(full["ab_w_in"][0]), "conv_w": None, "a_log": ab_a_log, "dt_bias": ab_dt_bias,
        "ab_gnorm_g": ab_gnorm_g, "ab_w_out": full["ab_w_out"][0], "c_w_in": full["c_w_in"][0], "c_lb_raw": c_lb_raw,
        "c_gnorm_g": c_gnorm_g, "c_w_out": full["c_w_out"][0], "ln_mix_g": ln_mix_g, "ln_mix_b": ln_mix_b,
        "ln_ffn_g": ln_ffn_g, "ln_ffn_b": ln_ffn_b, "mlp_w1": full["mlp_w1"], "mlp_w2": full["mlp_w2"],
    }
    w["conv_w"] = full["ab_conv_w"][0].astype(F32)

    h0 = jnp.concatenate([jnp.zeros((pad, D), F32), meta_full, x[0]], axis=0)
    tgt = jnp.concatenate([jnp.zeros((pad + N_META, D), F32), loss_target[0]], axis=0)
    loss8, dh0, g = _local_step(h0, tgt, w, pad)
    loss = lax.psum(loss8[0, 0], ("x", "y", "c"))
    grad_x = dh0[pad + N_META:][None]

    gsmall = {"ln_mix_g": g["ln_mix_g"], "ln_mix_b": g["ln_mix_b"], "ln_ffn_g": g["ln_ffn_g"], "ln_ffn_b": g["ln_ffn_b"],
              "c_lb_raw": g["c_lb_raw"], "ab_a_log": g["a_log"], "ab_dt_bias": g["dt_bias"], "ab_gnorm_g": g["ab_gnorm_g"],
              "c_gnorm_g": g["c_gnorm_g"]}
    sbuf = jnp.concatenate([_pack_small(gsmall), dh0[pad:pad + N_META]], axis=0)
    ssum = _small_allreduce(sbuf, "allreduce_small")
    grads = _unpack_small(ssum[:SMALL_ROWS], wts)
    grads["meta_tokens"] = lax.dynamic_slice(ssum[SMALL_ROWS:], (0, chip * mcols), (N_META, mcols))

    gfull = {"ab_w_in": _unpad_ab_cols(g["ab_w_in"])[None], "ab_conv_w": g["conv_w"][None], "ab_w_out": g["ab_w_out"][None],
             "c_w_in": g["c_w_in"][None], "c_w_out": g["c_w_out"][None], "mlp_w1": g["mlp_w1"], "mlp_w2": g["mlp_w2"]}
    packed = []
    for j in range(N_CHIP):
        pieces = []
        for n in BIG:
            ax, size = BIG_AXIS[n], wts[n].shape[BIG_AXIS[n]]
            pieces.append(lax.slice_in_dim(gfull[n], j * size, (j + 1) * size, axis=ax).astype(BF16))
        packed.append(_pack_flat(pieces, PACK_ROWS))
    gpack = jnp.stack(packed).reshape(N_CHIP, 2, PACK_ROWS // 2, D)
    landed = _grad_alltoall(gpack, "grad_alltoall")
    half = _sum_slots(landed, "grad_sum")
    gshard = _sibling_exchange(half, "grad_swap").reshape(PACK_ROWS, D)
    for n, gs in zip(BIG, _unpack_flat(gshard, shard_shapes)):
        grads[n] = gs

    delta, new_m, new_v = {}, {}, {}
    for n in ("meta_tokens",) + BIG:
        shp = wts[n].shape
        to2 = lambda a: a.reshape(-1, shp[-1])
        d2, m2, v2 = _adamw(to2(wts[n]), to2(grads[n]), to2(mom_m[n]), to2(mom_v[n]), f"adamw_{n}")
        delta[n], new_m[n], new_v[n] = d2.reshape(shp), m2.reshape(shp), v2.reshape(shp)
    d2, m2, v2 = _adamw(_pack_small(wts), ssum[:SMALL_ROWS], _pack_small(mom_m), _pack_small(mom_v), "adamw_small")
    delta.update(_unpack_small(d2, wts))
    new_m.update(_unpack_small(m2, wts))
    new_v.update(_unpack_small(v2, wts))

    return (loss, grad_x, *[grads[n] for n in names], *[delta[n] for n in names], *[new_m[n] for n in names],
            *[new_v[n] for n in names])
```

```python
import functools
import math

import numpy as np
import jax
import jax.numpy as jnp
from jax import lax
from jax.experimental import pallas as pl
from jax.experimental.pallas import tpu as pltpu

F32 = jnp.float32
BF16 = jnp.bfloat16

D = 1024
N_META = 16
D_FF = 4 * D
DEPTH = 2
GDN_H = 4
SB_H = 8
SB_DH = 64
HG_H = 8
HD = 128
CH = 64
QB = 128
ALPHA = float((2 * DEPTH) ** 0.25)
LN_EPS = 1e-5
RMS_EPS = 1e-6
L2_EPS = 1e-6
NEG = -1e30

ADAM_LR = 0.001
ADAM_B1 = 0.9
ADAM_B2 = 0.999
ADAM_EPS = 1e-08
ADAM_WD = 0.01
ADAM_STEP = 10

AB_W = 30 * HD
AB_TRUE = 3592
VMEM_LIMIT = 56 * 1024 * 1024

NN = ((1,), (0,))
NT = ((1,), (1,))
TN = ((0,), (0,))


def _cparams(**kw):
    return pltpu.CompilerParams(vmem_limit_bytes=VMEM_LIMIT, **kw)


def _dg(a, b, dims, mode):
    if mode == "h":
        return lax.dot_general(a, b, dims, precision=lax.Precision.HIGHEST, preferred_element_type=F32)
    if mode == "b":
        return lax.dot_general(a.astype(BF16), b.astype(BF16), dims, preferred_element_type=F32)
    ah, bh = a.astype(BF16), b.astype(BF16)
    al, bl = (a - ah.astype(F32)).astype(BF16), (b - bh.astype(F32)).astype(BF16)
    d = lambda x, y: lax.dot_general(x, y, dims, preferred_element_type=F32)
    return d(ah, bh) + (d(ah, bl) + d(al, bh))


def _make_dots(mode, batched=False):
    if batched:
        nn_d, nt_d, tn_d = (((2,), (1,)), ((0,), (0,))), (((2,), (2,)), ((0,), (0,))), (((1,), (1,)), ((0,), (0,)))
    else:
        nn_d, nt_d, tn_d = (NN, ((), ())), (NT, ((), ())), (TN, ((), ()))

    @jax.custom_vjp
    def nn(a, b):
        return _dg(a, b, nn_d, mode)

    @jax.custom_vjp
    def nt(a, b):
        return _dg(a, b, nt_d, mode)

    @jax.custom_vjp
    def tn(a, b):
        return _dg(a, b, tn_d, mode)

    nn.defvjp(lambda a, b: (nn(a, b), (a, b)), lambda r, g: (nt(g, r[1]), tn(r[0], g)))
    nt.defvjp(lambda a, b: (nt(a, b), (a, b)), lambda r, g: (nn(g, r[1]), tn(g, r[0])))
    tn.defvjp(lambda a, b: (tn(a, b), (a, b)), lambda r, g: (nt(r[1], g), nn(r[0], g)))
    return nn, nt, tn


hnn, hnt, htn = _make_dots("h")
bbnn, bbnt, bbtn = _make_dots("b", True)
mbnn, mbnt, mbtn = _make_dots("m", True)
hbnn, hbnt, hbtn = _make_dots("h", True)


def _heads(a, n):
    return jnp.stack([a[:, h * HD:(h + 1) * HD] for h in range(n)])


def _sigmoid(x):
    return jax.nn.sigmoid(x)


def _silu(x):
    return x * jax.nn.sigmoid(x)


def _softplus(x):
    return jnp.maximum(x, 0.0) + jnp.log(1.0 + jnp.exp(-jnp.abs(x)))


def _iota(shape, dim):
    return lax.broadcasted_iota(jnp.int32, shape, dim)


def _pick(n, prefs):
    for p in prefs:
        if n % p == 0:
            return p
    return n


def _mm(a, b, *, ta=False, tb=False, out_dtype=F32, name, b_view=None, out_split=0):
    if ta:
        k_dim, m_dim = a.shape
    else:
        m_dim, k_dim = a.shape
    if b_view is None:
        w_rows, w_cols = b.shape
    else:
        kind, layer = b_view
        nj, _, blk_r, blk_c = b.shape
        w_rows, w_cols = (blk_r, nj * blk_c) if kind == "cols" else (nj * blk_r, blk_c)
    n_dim = w_rows if tb else w_cols
    assert (w_cols if tb else w_rows) == k_dim
    tm = _pick(m_dim, (1024, 1056, 704, 640, 512, 384, 256, 128))
    tn = _pick(n_dim, (1024, 1056, 704, 640, 512, 384, 256, 128))
    tk = _pick(k_dim, (1024, 1056, 704, 512, 384, 256, 128))
    nk = k_dim // tk
    a_spec = pl.BlockSpec((tk, tm), lambda i, j, k: (k, i)) if ta else pl.BlockSpec((tm, tk), lambda i, j, k: (i, k))
    wb = (tn, tk) if tb else (tk, tn)
    w_idx = (lambda i, j, k: (j, k)) if tb else (lambda i, j, k: (k, j))
    if b_view is None:
        b_spec = pl.BlockSpec(wb, w_idx)
    elif kind == "cols":
        per = blk_c // wb[1]
        b_spec = pl.BlockSpec((None, None) + wb,
                              lambda i, j, k: (w_idx(i, j, k)[1] // per, layer, w_idx(i, j, k)[0], w_idx(i, j, k)[1] % per))
    else:
        per = blk_r // wb[0]
        b_spec = pl.BlockSpec((None, None) + wb,
                              lambda i, j, k: (w_idx(i, j, k)[0] // per, layer, w_idx(i, j, k)[0] % per, w_idx(i, j, k)[1]))
    if out_split:
        per_o = (n_dim // out_split) // tn
        out_spec = pl.BlockSpec((None, tm, tn), lambda i, j, k: (j // per_o, i, j % per_o))
        out_sds = jax.ShapeDtypeStruct((out_split, m_dim, n_dim // out_split), out_dtype)
    else:
        out_spec = pl.BlockSpec((tm, tn), lambda i, j, k: (i, j))
        out_sds = jax.ShapeDtypeStruct((m_dim, n_dim), out_dtype)
    dims = (((0 if ta else 1,), (1 if tb else 0,)), ((), ()))

    def body(a_ref, b_ref, o_ref, acc_ref):
        part = lax.dot_general(a_ref[...], b_ref[...], dims, preferred_element_type=F32)
        if nk == 1:
            o_ref[...] = part.astype(o_ref.dtype)
        else:
            k = pl.program_id(2)

            @pl.when(k == 0)
            def _():
                acc_ref[...] = part

            @pl.when(k > 0)
            def _():
                acc_ref[...] += part

            @pl.when(k == nk - 1)
            def _():
                o_ref[...] = acc_ref[...].astype(o_ref.dtype)

    return pl.pallas_call(
        body, name=name, grid=(m_dim // tm, n_dim // tn, nk),
        in_specs=[a_spec, b_spec],
        out_specs=out_spec,
        out_shape=out_sds,
        scratch_shapes=[pltpu.VMEM((tm, tn) if nk > 1 else (8, 128), F32)],
        compiler_params=_cparams(dimension_semantics=("parallel", "parallel", "arbitrary")),
    )(a, b)


def _row_tile(t_pad, width):
    for tr in (528, 352, 176, 128, 64):
        if t_pad % tr == 0 and tr * width * 4 <= (3 << 19) and tr % 16 == 0:
            return tr
    return 64 if t_pad % 64 == 0 else t_pad


def _ln_res_fn(h, m, g, b):
    x = ALPHA * h + m
    mu = jnp.mean(x, axis=-1, keepdims=True)
    xc = x - mu
    var = jnp.mean(xc * xc, axis=-1, keepdims=True)
    return xc * lax.rsqrt(var + LN_EPS) * g + b


def _ln_res_fwd(h, m, g, b, name):
    t_pad = h.shape[0]
    tr = _row_tile(t_pad, D)

    def body(h_ref, m_ref, g_ref, b_ref, y_ref, yb_ref):
        y = _ln_res_fn(h_ref[...], m_ref[...], g_ref[...], b_ref[...])
        y_ref[...] = y
        yb_ref[...] = y.astype(BF16)

    row = pl.BlockSpec((tr, D), lambda i: (i, 0))
    par = pl.BlockSpec((1, D), lambda i: (0, 0))
    return pl.pallas_call(
        body, name=name, grid=(t_pad // tr,), in_specs=[row, row, par, par], out_specs=[row, row],
        out_shape=[jax.ShapeDtypeStruct((t_pad, D), F32), jax.ShapeDtypeStruct((t_pad, D), BF16)],
        compiler_params=_cparams(),
    )(h, m, g, b)


def _ln_res_bwd(h, m, g, b, dy1, dy2, name):
    t_pad = h.shape[0]
    tr = _row_tile(t_pad, D)

    def body(h_ref, m_ref, g_ref, b_ref, d1_ref, d2_ref, dh_ref, dm_ref, dg_ref, db_ref):
        _, vjp = jax.vjp(_ln_res_fn, h_ref[...], m_ref[...], g_ref[...], b_ref[...])
        dh, dm, dg, db = vjp(d1_ref[...] + d2_ref[...])
        dh_ref[...] = dh
        dm_ref[...] = dm.astype(BF16)

        @pl.when(pl.program_id(0) == 0)
        def _():
            dg_ref[...] = jnp.zeros_like(dg_ref)
            db_ref[...] = jnp.zeros_like(db_ref)

        dg_ref[...] += dg
        db_ref[...] += db

    row = pl.BlockSpec((tr, D), lambda i: (i, 0))
    par = pl.BlockSpec((1, D), lambda i: (0, 0))
    return pl.pallas_call(
        body, name=name, grid=(t_pad // tr,), in_specs=[row, row, par, par, row, row],
        out_specs=[row, row, par, par],
        out_shape=[jax.ShapeDtypeStruct((t_pad, D), F32), jax.ShapeDtypeStruct((t_pad, D), BF16),
                   jax.ShapeDtypeStruct((1, D), F32), jax.ShapeDtypeStruct((1, D), F32)],
        compiler_params=_cparams(),
    )(h, m, g, b, dy1, dy2)


def _relu2_fwd(a, name):
    t_pad, w = a.shape
    tr = _row_tile(t_pad, w)

    def body(a_ref, r_ref):
        r = jnp.maximum(a_ref[...], 0.0)
        r_ref[...] = (r * r).astype(BF16)

    row = pl.BlockSpec((tr, w), lambda i: (i, 0))
    return pl.pallas_call(body, name=name, grid=(t_pad // tr,), in_specs=[row], out_specs=row,
                          out_shape=jax.ShapeDtypeStruct((t_pad, w), BF16), compiler_params=_cparams())(a)


def _relu2_bwd(a, dr, name):
    t_pad, w = a.shape
    tr = _row_tile(t_pad, w)

    def body(a_ref, dr_ref, da_ref):
        da_ref[...] = (dr_ref[...] * (2.0 * jnp.maximum(a_ref[...], 0.0))).astype(BF16)

    row = pl.BlockSpec((tr, w), lambda i: (i, 0))
    return pl.pallas_call(body, name=name, grid=(t_pad // tr,), in_specs=[row, row], out_specs=row,
                          out_shape=jax.ShapeDtypeStruct((t_pad, w), BF16), compiler_params=_cparams())(a, dr)


def _grms_fn(o, z, g):
    y = o * lax.rsqrt(jnp.mean(o * o, axis=-1, keepdims=True) + RMS_EPS) * g
    return y * _silu(z)


def _grms_fwd(o, z_arr, z_blk0, g, name):
    t_pad, w = o.shape
    nh = w // HD
    tr = _row_tile(t_pad, HD * 4)

    def body(o_ref, z_ref, g_ref, y_ref):
        y_ref[...] = _grms_fn(o_ref[...], z_ref[...], g_ref[...]).astype(BF16)

    return pl.pallas_call(
        body, name=name, grid=(t_pad // tr, nh),
        in_specs=[pl.BlockSpec((tr, HD), lambda i, h: (i, h)),
                  pl.BlockSpec((tr, HD), lambda i, h: (i, z_blk0 + h)),
                  pl.BlockSpec((1, HD), lambda i, h: (0, 0))],
        out_specs=pl.BlockSpec((tr, HD), lambda i, h: (i, h)),
        out_shape=jax.ShapeDtypeStruct((t_pad, w), BF16), compiler_params=_cparams(),
    )(o, z_arr, g)


def _grms_bwd(o, z_arr, z_blk0, g, dy_arr, dy_blk0, name):
    t_pad, w = o.shape
    nh = w // HD
    tr = _row_tile(t_pad, HD * 4)

    def body(o_ref, z_ref, g_ref, dy_ref, do_ref, dz_ref, dg_ref):
        _, vjp = jax.vjp(_grms_fn, o_ref[...], z_ref[...], g_ref[...])
        do, dz, dg = vjp(dy_ref[...])
        do_ref[...] = do
        dz_ref[...] = dz

        @pl.when((pl.program_id(0) == 0) & (pl.program_id(1) == 0))
        def _():
            dg_ref[...] = jnp.zeros_like(dg_ref)

        dg_ref[...] += dg

    blk = pl.BlockSpec((tr, HD), lambda i, h: (i, h))
    return pl.pallas_call(
        body, name=name, grid=(t_pad // tr, nh),
        in_specs=[blk, pl.BlockSpec((tr, HD), lambda i, h: (i, z_blk0 + h)),
                  pl.BlockSpec((1, HD), lambda i, h: (0, 0)),
                  pl.BlockSpec((tr, HD), lambda i, h: (i, dy_blk0 + h))],
        out_specs=[blk, blk, pl.BlockSpec((1, HD), lambda i, h: (0, 0))],
        out_shape=[jax.ShapeDtypeStruct((t_pad, w), F32), jax.ShapeDtypeStruct((t_pad, w), F32),
                   jax.ShapeDtypeStruct((1, HD), F32)],
        compiler_params=_cparams(),
    )(o, z_arr, g, dy_arr)


def _loss_fwd(y, tgt, first_row, name):
    t_pad = y.shape[0]
    tr = _row_tile(t_pad, D)

    def body(y_ref, t_ref, l_ref, dy_ref):
        rows = pl.program_id(0) * tr + _iota((tr, 1), 0)
        err = jnp.where(rows >= first_row, y_ref[...] - t_ref[...], 0.0)
        dy_ref[...] = err * (1.0 / D)

        @pl.when(pl.program_id(0) == 0)
        def _():
            l_ref[...] = jnp.zeros_like(l_ref)

        part = jnp.sum(jnp.sum(err * err, axis=1, keepdims=True), axis=0, keepdims=True)
        l_ref[...] += jnp.broadcast_to(part * (0.5 / D), l_ref.shape)

    row = pl.BlockSpec((tr, D), lambda i: (i, 0))
    return pl.pallas_call(
        body, name=name, grid=(t_pad // tr,), in_specs=[row, row],
        out_specs=[pl.BlockSpec((8, 128), lambda i: (0, 0)), row],
        out_shape=[jax.ShapeDtypeStruct((8, 128), F32), jax.ShapeDtypeStruct((t_pad, D), F32)],
        compiler_params=_cparams(),
    )(y, tgt)


def _add2(a, b, name):
    t_pad, w = a.shape
    tr = _row_tile(t_pad, w)

    def body(a_ref, b_ref, o_ref):
        o_ref[...] = a_ref[...] + b_ref[...]

    row = pl.BlockSpec((tr, w), lambda i: (i, 0))
    return pl.pallas_call(body, name=name, grid=(t_pad // tr,), in_specs=[row, row], out_specs=row,
                          out_shape=jax.ShapeDtypeStruct((t_pad, w), F32), compiler_params=_cparams())(a, b)


def _assemble_bf16(parts, name):
    t_pad = parts[0][0].shape[0] if parts[0][1] == "cols" else parts[0][0].shape[1]
    widths = [p.shape[1] if kind == "cols" else HD for p, kind in parts]
    total = sum(widths)
    tr = _row_tile(t_pad, total)

    def body(*refs):
        o_ref = refs[-1]
        off = 0
        for ref, (p, kind), w in zip(refs[:-1], parts, widths):
            if kind == "cols":
                o_ref[:, off:off + w] = ref[...].astype(BF16)
            else:
                acc = ref[0]
                for hh in range(1, p.shape[0]):
                    acc = acc + ref[hh]
                o_ref[:, off:off + w] = acc.astype(BF16)
            off += w

    specs = []
    for p, kind in parts:
        if kind == "cols":
            specs.append(pl.BlockSpec((tr, p.shape[1]), lambda i: (i, 0)))
        else:
            specs.append(pl.BlockSpec((p.shape[0], tr, HD), lambda i: (0, i, 0)))
    return pl.pallas_call(
        body, name=name, grid=(t_pad // tr,), in_specs=specs,
        out_specs=pl.BlockSpec((tr, total), lambda i: (i, 0)),
        out_shape=jax.ShapeDtypeStruct((t_pad, total), BF16), compiler_params=_cparams(),
    )(*[p for p, _ in parts])


CONV_K = 4
HALO = 8
RT = 128


def _conv_fwd(p, blk0, w, mode, pad, name):
    t_pad = p.shape[0]
    nt = t_pad // RT
    scale = HD ** -0.5 if mode == "q" else 1.0

    def body(x_ref, w_ref, y_ref, xs_ref):
        xs_ref[0:HALO, :] = jnp.zeros((HALO, HD), F32)
        rows = _iota((t_pad, 1), 0)
        xs_ref[HALO:HALO + t_pad, :] = jnp.where(rows >= pad, x_ref[...], 0.0)
        wv = w_ref[...]

        def tile(i, carry):
            r0 = pl.multiple_of(i * RT, RT)
            ext = xs_ref[pl.ds(r0, RT + HALO), :]
            acc = ext[HALO:, :] * wv[3:4, :]
            for s in (1, 2, 3):
                acc = acc + pltpu.roll(ext, s, 0)[HALO:, :] * wv[3 - s:4 - s, :]
            y = _silu(acc)
            if mode != "v":
                y = y * lax.rsqrt(jnp.sum(y * y, axis=-1, keepdims=True) + L2_EPS) * scale
            y_ref[pl.ds(r0, RT), :] = y
            return carry

        lax.fori_loop(0, nt, tile, 0)

    return pl.pallas_call(
        body, name=name, grid=(GDN_H,),
        in_specs=[pl.BlockSpec((t_pad, HD), lambda h: (0, blk0 + h)), pl.BlockSpec((CONV_K, HD), lambda h: (0, h))],
        out_specs=pl.BlockSpec((t_pad, HD), lambda h: (0, h)),
        out_shape=jax.ShapeDtypeStruct((t_pad, GDN_H * HD), F32),
        scratch_shapes=[pltpu.VMEM((t_pad + HALO, HD), F32)],
        compiler_params=_cparams(),
    )(p, w)


def _conv_bwd(p, blk0, w, dn, mode, pad, name):
    t_pad = p.shape[0]
    nt = t_pad // RT
    scale = HD ** -0.5 if mode == "q" else 1.0

    def body(x_ref, w_ref, dn_ref, dx_ref, dw_ref, xs_ref, ds_ref):
        xs_ref[0:HALO, :] = jnp.zeros((HALO, HD), F32)
        xs_ref[HALO + t_pad:HALO + t_pad + 2 * HALO, :] = jnp.zeros((2 * HALO, HD), F32)
        ds_ref[t_pad:t_pad + HALO, :] = jnp.zeros((HALO, HD), F32)
        rows = _iota((t_pad, 1), 0)
        xs_ref[HALO:HALO + t_pad, :] = jnp.where(rows >= pad, x_ref[...], 0.0)
        ds_ref[0:t_pad, :] = dn_ref[...]
        wv = w_ref[...]

        def tile(i, dw):
            r0 = pl.multiple_of(i * RT, RT)
            ext = xs_ref[pl.ds(r0, RT + 2 * HALO), :]
            dn_e = ds_ref[pl.ds(r0, RT + HALO), :]
            xsh = [ext[HALO:, :]] + [pltpu.roll(ext, s, 0)[HALO:, :] for s in (1, 2, 3)]
            pre = xsh[0] * wv[3:4, :]
            for s in (1, 2, 3):
                pre = pre + xsh[s] * wv[3 - s:4 - s, :]
            sg = _sigmoid(pre)
            y = pre * sg
            if mode != "v":
                ss = jnp.sum(y * y, axis=-1, keepdims=True) + L2_EPS
                r = lax.rsqrt(ss)
                dy = scale * (dn_e * r - y * (r * r * r) * jnp.sum(dn_e * y, axis=-1, keepdims=True))
            else:
                dy = dn_e
            dpre = dy * (sg * (1.0 + pre * (1.0 - sg)))
            dx = dpre[:RT, :] * wv[3:4, :]
            for s in (1, 2, 3):
                dx = dx + pltpu.roll(dpre, RT + HALO - s, 0)[:RT, :] * wv[3 - s:4 - s, :]
            trow = r0 + _iota((RT, 1), 0)
            dx_ref[pl.ds(r0, RT), :] = jnp.where(trow >= pad, dx, 0.0)
            new = []
            for s in (0, 1, 2, 3):
                new.append(dw[s] + jnp.sum(dpre[:RT, :] * xsh[s][:RT, :], axis=0, keepdims=True))
            return tuple(new)

        z = jnp.zeros((1, HD), F32)
        dw = lax.fori_loop(0, nt, tile, (z, z, z, z))
        for s in (0, 1, 2, 3):
            dw_ref[3 - s:4 - s, :] = dw[s]

    return pl.pallas_call(
        body, name=name, grid=(GDN_H,),
        in_specs=[pl.BlockSpec((t_pad, HD), lambda h: (0, blk0 + h)), pl.BlockSpec((CONV_K, HD), lambda h: (0, h)),
                  pl.BlockSpec((t_pad, HD), lambda h: (0, h))],
        out_specs=[pl.BlockSpec((t_pad, HD), lambda h: (0, h)), pl.BlockSpec((CONV_K, HD), lambda h: (0, h))],
        out_shape=[jax.ShapeDtypeStruct((t_pad, GDN_H * HD), F32), jax.ShapeDtypeStruct((CONV_K, GDN_H * HD), F32)],
        scratch_shapes=[pltpu.VMEM((t_pad + 3 * HALO, HD), F32), pltpu.VMEM((t_pad + HALO, HD), F32)],
        compiler_params=_cparams(),
    )(p, w, dn)


def _unit_lower_inv(m, bd, eye):
    md = m * bd
    low = m - md
    p2 = mbnn(md, md)
    p4 = mbnn(p2, p2)
    p8 = mbnn(p4, p4)
    dinv = mbnn(mbnn(mbnn(eye - md, eye + p2), eye + p4), eye + p8)
    n = mbnn(dinv, low)
    n2 = mbnn(n, n)
    return mbnn(mbnn(eye - n, eye + n2), dinv)


def _gdn_chunk(q, k, v, bb, aa, alog, dtb, s, valid):
    nh = q.shape[0]
    ri = _iota((1, CH, CH), 1)
    ci = _iota((1, CH, CH), 2)
    causal = ri >= ci
    strict = ri > ci
    eye = (ri == ci).astype(F32)
    bd = ((ri >> 4) == (ci >> 4)).astype(F32)
    ltri = (_iota((CH, CH), 0) >= _iota((CH, CH), 1)).astype(F32)
    sel = (_iota((nh, 1, HD), 2) == _iota((nh, 1, HD), 0)).astype(F32)

    beta_all = jnp.where(valid, _sigmoid(bb), 0.0)
    g_all = jnp.where(valid, -jnp.exp(alog) * _softplus(aa + dtb), 0.0)
    gc_all = hnn(ltri, g_all)
    beta = jnp.sum(beta_all[None] * sel, axis=2, keepdims=True)
    gc = jnp.sum(gc_all[None] * sel, axis=2, keepdims=True)
    gc_rows = hbnt(jnp.broadcast_to(sel, (nh, CH, HD)), jnp.broadcast_to(gc_all[None], (nh, CH, HD)))
    last = _iota((1, CH, 1), 1) == CH - 1
    gc_last = jnp.sum(jnp.where(last, gc, 0.0), axis=1, keepdims=True)
    decay = jnp.exp(jnp.where(causal, gc - gc_rows, NEG))
    egc = jnp.exp(gc)

    kb = k * beta
    m = jnp.where(strict, bbnt(kb, k) * decay, 0.0)
    t_inv = _unit_lower_inv(m, bd, eye)
    u = bbnn(t_inv, v * beta)
    w = bbnn(t_inv, kb * egc)
    a_intra = bbnt(q, k) * decay
    q_dec = q * egc
    k_dec = k * jnp.exp(gc_last - gc)
    v_new = u - bbnn(w, s)
    o = bbnn(q_dec, s) + bbnn(a_intra, v_new)
    s_new = s * jnp.exp(gc_last) + bbtn(k_dec, v_new)
    return o, s_new


def _gdn_specs(nc, rev):
    cc = (lambda c: nc - 1 - c) if rev else (lambda c: c)
    wide = pl.BlockSpec((CH, GDN_H * HD), lambda c: (cc(c), 0))
    fix = lambda off: pl.BlockSpec((CH, HD), lambda c: (cc(c), off))
    par = pl.BlockSpec((1, HD), lambda c: (0, 0))
    state = pl.BlockSpec((1, GDN_H, HD, HD), lambda c: (cc(c), 0, 0, 0))
    return wide, fix, par, state


def _store_heads(ref, a):
    for h in range(a.shape[0]):
        ref[:, h * HD:(h + 1) * HD] = a[h]


def _gdn_fwd(qn, kn, vn, p, alog, dtb, pad, name):
    t_pad = qn.shape[0]
    nc = t_pad // CH
    wide, fix, par, state = _gdn_specs(nc, False)

    def body(q_ref, k_ref, v_ref, bb_ref, aa_ref, al_ref, dt_ref, o_ref, ss_ref, s_ref):
        c = pl.program_id(0)

        @pl.when(c == 0)
        def _():
            s_ref[...] = jnp.zeros_like(s_ref)

        s = s_ref[...]
        ss_ref[0] = s
        valid = (c * CH + _iota((CH, 1), 0)) >= pad
        o, s_new = _gdn_chunk(_heads(q_ref[...], GDN_H), _heads(k_ref[...], GDN_H), _heads(v_ref[...], GDN_H),
                              bb_ref[...], aa_ref[...], al_ref[...], dt_ref[...], s, valid)
        _store_heads(o_ref, o)
        s_ref[...] = s_new

    return pl.pallas_call(
        body, name=name, grid=(nc,),
        in_specs=[wide, wide, wide, fix(16), fix(17), par, par],
        out_specs=[wide, state],
        out_shape=[jax.ShapeDtypeStruct((t_pad, GDN_H * HD), F32), jax.ShapeDtypeStruct((nc, GDN_H, HD, HD), F32)],
        scratch_shapes=[pltpu.VMEM((GDN_H, HD, HD), F32)],
        compiler_params=_cparams(),
    )(qn, kn, vn, p, p, alog, dtb)


def _gdn_bwd(qn, kn, vn, p, alog, dtb, ssave, do, pad, name):
    t_pad = qn.shape[0]
    nc = t_pad // CH
    wide, fix, par, state = _gdn_specs(nc, True)

    def body(q_ref, k_ref, v_ref, bb_ref, aa_ref, al_ref, dt_ref, ss_ref, do_ref,
             dq_ref, dk_ref, dv_ref, dbb_ref, daa_ref, dal_ref, ddt_ref, ds_ref):
        c = pl.program_id(0)

        @pl.when(c == 0)
        def _():
            ds_ref[...] = jnp.zeros_like(ds_ref)
            dal_ref[...] = jnp.zeros_like(dal_ref)
            ddt_ref[...] = jnp.zeros_like(ddt_ref)

        valid = ((nc - 1 - c) * CH + _iota((CH, 1), 0)) >= pad
        fn = lambda q, k, v, bb, aa, al, dt, s: _gdn_chunk(q, k, v, bb, aa, al, dt, s, valid)
        _, vjp = jax.vjp(fn, _heads(q_ref[...], GDN_H), _heads(k_ref[...], GDN_H), _heads(v_ref[...], GDN_H),
                         bb_ref[...], aa_ref[...], al_ref[...], dt_ref[...], ss_ref[0])
        dq, dk, dv, dbb, daa, dal, ddt, ds = vjp((_heads(do_ref[...], GDN_H), ds_ref[...]))
        _store_heads(dq_ref, dq)
        _store_heads(dk_ref, dk)
        _store_heads(dv_ref, dv)
        dbb_ref[...] = dbb
        daa_ref[...] = daa
        dal_ref[...] += dal
        ddt_ref[...] += ddt
        ds_ref[...] = ds

    sds = jax.ShapeDtypeStruct
    return pl.pallas_call(
        body, name=name, grid=(nc,),
        in_specs=[wide, wide, wide, fix(16), fix(17), par, par, state, wide],
        out_specs=[wide, wide, wide, fix(0), fix(0), par, par],
        out_shape=[sds((t_pad, GDN_H * HD), F32)] * 3 + [sds((t_pad, HD), F32)] * 2 + [sds((1, HD), F32)] * 2,
        scratch_shapes=[pltpu.VMEM((GDN_H, HD, HD), F32)],
        compiler_params=_cparams(),
    )(qn, kn, vn, p, p, alog, dtb, ssave, do)


SB_Q0, SB_K0, SB_V0 = 18, 22, 26
SB_SCALE = SB_DH ** -0.5


def _sb_terms(z, allowed):
    sp = jnp.log(1.0 + jnp.exp(-jnp.abs(z)))
    l1m = jnp.where(allowed, -jnp.maximum(z, 0.0) - sp, 0.0)
    ls = jnp.minimum(z, 0.0) - sp
    return l1m, ls


def _sb_stack(a, i):
    first = _iota((1, HD), 1) < SB_DH
    a2 = jnp.concatenate([jnp.where(first, a, 0.0), jnp.where(first, 0.0, a)], axis=0).astype(BF16)
    rq = i * QB + _iota((QB, 1), 0)
    return a2, jnp.concatenate([rq, rq], axis=0), first


def _dot_hi_lo(a, b2):
    hi = a.astype(BF16)
    lo = (a - hi.astype(F32)).astype(BF16)
    return lax.dot_general(jnp.concatenate([hi, lo], axis=1), b2, (NN, ((), ())), preferred_element_type=F32)


def _sb_fwd(p, pad, name):
    t_pad = p.shape[0]
    nq = t_pad // QB

    def body(q_ref, k_ref, v_ref, o_ref, r_ref):
        i = pl.program_id(1)
        q2, rowq, first = _sb_stack(q_ref[...] * SB_SCALE, i)
        tri = (_iota((QB, QB), 0) > _iota((QB, QB), 1)).astype(BF16)
        upper2 = jnp.concatenate([tri, tri], axis=0)

        def chain(kb, live):
            start = pl.multiple_of(kb * QB, QB)
            kblk = k_ref[pl.ds(start, QB), :].astype(BF16)
            vblk = v_ref[pl.ds(start, QB), :].astype(BF16)
            z = lax.dot_general(q2, kblk, (NT, ((), ())), preferred_element_type=F32)
            colk = kb * QB + _iota((1, QB), 1)
            allowed = (colk < rowq) & (colk >= pad) & live
            l1m, ls = _sb_terms(z, allowed)
            return allowed, ls, _dot_hi_lo(l1m, upper2), jnp.sum(l1m, axis=1, keepdims=True), vblk

        def step(j, carry):
            o_acc, run = carry
            kb_a = i - 2 * j
            kb_b = kb_a - 1
            al_a, ls_a, suf_a, rs_a, v_a = chain(kb_a, True)
            al_b, ls_b, suf_b, rs_b, v_b = chain(jnp.maximum(kb_b, 0), kb_b >= 0)
            w_a = jnp.where(al_a, jnp.exp(ls_a + suf_a + run), 0.0).astype(BF16)
            w_b = jnp.where(al_b, jnp.exp(ls_b + suf_b + (run + rs_a)), 0.0).astype(BF16)
            o_acc = o_acc + lax.dot_general(jnp.concatenate([w_a, w_b], axis=1), jnp.concatenate([v_a, v_b], axis=0),
                                            (NN, ((), ())), preferred_element_type=F32)
            return o_acc, run + rs_a + rs_b

        o_acc, run = lax.fori_loop(0, (i + 2) // 2, step,
                                   (jnp.zeros((2 * QB, HD), F32), jnp.zeros((2 * QB, 1), F32)))
        o_ref[...] = jnp.where(first, o_acc[:QB], o_acc[QB:]).astype(BF16)
        r_ref[...] = jnp.where(first, run[:QB], run[QB:])

    full = lambda off: pl.BlockSpec((t_pad, HD), lambda pr, i: (0, off + pr))
    blk = pl.BlockSpec((QB, HD), lambda pr, i: (i, pr))
    return pl.pallas_call(
        body, name=name, grid=(SB_H // 2, nq),
        in_specs=[pl.BlockSpec((QB, HD), lambda pr, i: (i, SB_Q0 + pr)), full(SB_K0), full(SB_V0)],
        out_specs=[blk, blk],
        out_shape=[jax.ShapeDtypeStruct((t_pad, SB_H * SB_DH), BF16), jax.ShapeDtypeStruct((t_pad, SB_H * SB_DH), F32)],
        compiler_params=_cparams(),
    )(p, p, p)


def _sb_bwd(p, rtot, dy, dy_blk0, pad, name):
    t_pad = p.shape[0]
    nq = t_pad // QB

    def body(q_ref, k_ref, v_ref, r_ref, do_ref, dq_ref, dk_ref, dv_ref):
        i = pl.program_id(1)

        @pl.when(i == 0)
        def _():
            dk_ref[...] = jnp.zeros_like(dk_ref)
            dv_ref[...] = jnp.zeros_like(dv_ref)

        q2, rowq, first = _sb_stack(q_ref[...] * SB_SCALE, i)
        do2, _, _ = _sb_stack(do_ref[...], i)
        rt = r_ref[...]
        lane = _iota((1, HD), 1)
        rcol = jnp.concatenate([jnp.sum(jnp.where(lane == 0, rt, 0.0), axis=1, keepdims=True),
                                jnp.sum(jnp.where(lane == SB_DH, rt, 0.0), axis=1, keepdims=True)], axis=0)
        rj = _iota((QB, QB), 0)
        cs = _iota((QB, QB), 1)
        tri_u = (rj > cs).astype(BF16)
        tri_l = (rj < cs).astype(BF16)
        upper2 = jnp.concatenate([tri_u, tri_u], axis=0)
        lower2 = jnp.concatenate([tri_l, tri_l], axis=0)

        def chain(kb, live):
            start = pl.multiple_of(kb * QB, QB)
            kblk = k_ref[pl.ds(start, QB), :].astype(BF16)
            vblk = v_ref[pl.ds(start, QB), :].astype(BF16)
            z = lax.dot_general(q2, kblk, (NT, ((), ())), preferred_element_type=F32)
            colk = kb * QB + _iota((1, QB), 1)
            allowed = (colk < rowq) & (colk >= pad) & live
            l1m, ls = _sb_terms(z, allowed)
            dwgt = lax.dot_general(do2, vblk, (NT, ((), ())), preferred_element_type=F32)
            return (start, kblk, allowed, ls, _dot_hi_lo(l1m, upper2), jnp.sum(l1m, axis=1, keepdims=True), dwgt,
                    _sigmoid(z))

        def finish(c, seen, gseen):
            start, kblk, allowed, ls, suf, rs, dwgt, sg = c
            wgt = jnp.where(allowed, jnp.exp(ls + suf + (rcol - seen - rs)), 0.0)
            dl = dwgt * wgt
            gpre = gseen + _dot_hi_lo(dl, lower2)
            dz = jnp.where(allowed, dl * (1.0 - sg) - gpre * sg, 0.0).astype(BF16)
            dk_ref[pl.ds(start, QB), :] += lax.dot_general(dz, q2, (TN, ((), ())), preferred_element_type=F32)
            dv_ref[pl.ds(start, QB), :] += lax.dot_general(wgt.astype(BF16), do2, (TN, ((), ())),
                                                           preferred_element_type=F32)
            return dz, seen + rs, gseen + jnp.sum(dl, axis=1, keepdims=True)

        def step(j, carry):
            dq_acc, seen, gseen = carry
            kb_b = 2 * j + 1
            c_a = chain(2 * j, True)
            c_b = chain(jnp.minimum(kb_b, i), kb_b <= i)
            dz_a, seen, gseen = finish(c_a, seen, gseen)
            dz_b, seen, gseen = finish(c_b, seen, gseen)
            dq_acc = dq_acc + lax.dot_general(jnp.concatenate([dz_a, dz_b], axis=1),
                                              jnp.concatenate([c_a[1], c_b[1]], axis=0),
                                              (NN, ((), ())), preferred_element_type=F32)
            return dq_acc, seen, gseen

        zc = jnp.zeros((2 * QB, 1), F32)
        dq_acc, _, _ = lax.fori_loop(0, (i + 2) // 2, step, (jnp.zeros((2 * QB, HD), F32), zc, zc))
        dq_ref[...] = jnp.where(first, dq_acc[:QB], dq_acc[QB:]) * SB_SCALE

    full_in = lambda off: pl.BlockSpec((t_pad, HD), lambda pr, i: (0, off + pr))
    full_out = pl.BlockSpec((t_pad, HD), lambda pr, i: (0, pr))
    blk = pl.BlockSpec((QB, HD), lambda pr, i: (i, pr))
    sds = jax.ShapeDtypeStruct((t_pad, SB_H * SB_DH), F32)
    return pl.pallas_call(
        body, name=name, grid=(SB_H // 2, nq),
        in_specs=[pl.BlockSpec((QB, HD), lambda pr, i: (i, SB_Q0 + pr)), full_in(SB_K0), full_in(SB_V0), blk,
                  pl.BlockSpec((QB, HD), lambda pr, i: (i, dy_blk0 + pr))],
        out_specs=[blk, full_out, full_out],
        out_shape=[sds, sds, sds],
        compiler_params=_cparams(),
    )(p, p, p, rtot, dy)


HG_LEVELS = 6


def _hg_prefix_matrix():
    t = np.arange(CH)[:, None]
    j = np.arange(CH)[None, :]
    groups = [(j <= t)]
    for lvl in range(1, HG_LEVELS + 1):
        half = CH >> lvl
        e = (t // (2 * half)) * (2 * half) + half - 1
        groups.append(j <= e)
    groups.append(np.ones((2 * CH, CH), bool))
    return np.concatenate(groups, axis=0).astype(np.float32)


HG_G = 4


def _hg_chunk(qr, fr, iv, r0, r1, st, valid, ecat):
    g = st.shape[0]
    mx = jnp.maximum(r0, r1)
    e0 = jnp.exp(r0 - mx)
    e1 = jnp.exp(r1 - mx)
    lb = e1 / (e0 + e1)
    fg = lb + (1.0 - lb) * _sigmoid(fr)
    logf = jnp.where(valid, jnp.log(fg), 0.0)
    kk = jnp.where(valid, 1.0 - fg, 0.0)
    q = jnp.where(valid, _silu(qr), 0.0)
    v = _heads(jnp.where(valid, iv, 0.0), g)

    pre = hnn(ecat, logf)
    b = pre[0:CH]
    b_last = pre[(HG_LEVELS + 1) * CH:]
    row = _iota((CH, 1), 0)
    ri = _iota((1, CH, CH), 1)
    ci = _iota((1, CH, CH), 2)
    a = jnp.where(ri == ci, jnp.sum(_heads(q * kk, g), axis=2, keepdims=True), 0.0)
    for lvl in range(1, HG_LEVELS + 1):
        half = CH >> lvl
        m = pre[lvl * CH:(lvl + 1) * CH]
        low = (row & half) != 0
        qt = jnp.where(low, q * jnp.exp(jnp.where(low, b - m, 0.0)), 0.0)
        kt = jnp.where(low, 0.0, kk * jnp.exp(jnp.where(low, 0.0, m - b)))
        same = (ri >> (7 - lvl)) == (ci >> (7 - lvl))
        a = a + jnp.where(same, bbnt(_heads(qt, g), _heads(kt, g)), 0.0)
    o = bbnt(_heads(q * jnp.exp(b), g), st) + bbnn(a, v)
    kd = kk * jnp.exp(b_last[0:CH] - b)
    st_new = st * _heads(jnp.exp(b_last), g) + bbtn(v, _heads(kd, g))
    return o, st_new


def _hg_specs(nc, rev):
    cc = (lambda c: nc - 1 - c) if rev else (lambda c: c)
    ng = HG_H // HG_G
    blk = lambda off: pl.BlockSpec((CH, HG_G * HD), lambda h, c: (cc(c), off * ng + h))
    lbs = pl.BlockSpec((2, HG_G * HD), lambda h, c: (0, h))
    state = pl.BlockSpec((1, HG_G, HD, HD), lambda h, c: (cc(c), h, 0, 0))
    return ng, blk, lbs, state


def _hg_fwd(p, lbraw, ecat, pad, name):
    t_pad = p.shape[0]
    nc = t_pad // CH
    ng, blk, lbs, state = _hg_specs(nc, False)

    def body(q_ref, f_ref, i_ref, lb_ref, e_ref, o_ref, ss_ref, s_ref):
        c = pl.program_id(1)

        @pl.when(c == 0)
        def _():
            s_ref[...] = jnp.zeros_like(s_ref)

        st = s_ref[...]
        ss_ref[0] = st
        valid = (c * CH + _iota((CH, 1), 0)) >= pad
        o, st_new = _hg_chunk(q_ref[...], f_ref[...], i_ref[...], lb_ref[0:1, :], lb_ref[1:2, :], st, valid, e_ref[...])
        _store_heads(o_ref, o)
        s_ref[...] = st_new

    return pl.pallas_call(
        body, name=name, grid=(ng, nc),
        in_specs=[blk(0), blk(1), blk(2), lbs, pl.BlockSpec(ecat.shape, lambda h, c: (0, 0))],
        out_specs=[blk(0), state],
        out_shape=[jax.ShapeDtypeStruct((t_pad, HG_H * HD), F32), jax.ShapeDtypeStruct((nc, HG_H, HD, HD), F32)],
        scratch_shapes=[pltpu.VMEM((HG_G, HD, HD), F32)],
        compiler_params=_cparams(),
    )(p, p, p, lbraw, ecat)


def _hg_bwd(p, lbraw, ecat, ssave, do, pad, name):
    t_pad = p.shape[0]
    nc = t_pad // CH
    ng, blk, lbs, state = _hg_specs(nc, True)

    def body(q_ref, f_ref, i_ref, lb_ref, e_ref, ss_ref, do_ref, dq_ref, df_ref, di_ref, dlb_ref, ds_ref):
        c = pl.program_id(1)

        @pl.when(c == 0)
        def _():
            ds_ref[...] = jnp.zeros_like(ds_ref)
            dlb_ref[...] = jnp.zeros_like(dlb_ref)

        valid = ((nc - 1 - c) * CH + _iota((CH, 1), 0)) >= pad
        ecv = e_ref[...]
        fn = lambda qr, fr, iv, r0, r1, st: _hg_chunk(qr, fr, iv, r0, r1, st, valid, ecv)
        _, vjp = jax.vjp(fn, q_ref[...], f_ref[...], i_ref[...], lb_ref[0:1, :], lb_ref[1:2, :], ss_ref[0])
        dq, df, di, d0, d1, ds = vjp((_heads(do_ref[...], HG_G), ds_ref[...]))
        dq_ref[...] = dq
        df_ref[...] = df
        di_ref[...] = di
        dlb_ref[0:1, :] += d0
        dlb_ref[1:2, :] += d1
        ds_ref[...] = ds

    sds = jax.ShapeDtypeStruct((t_pad, HG_H * HD), F32)
    return pl.pallas_call(
        body, name=name, grid=(ng, nc),
        in_specs=[blk(0), blk(1), blk(2), lbs, pl.BlockSpec(ecat.shape, lambda h, c: (0, 0)), state, blk(0)],
        out_specs=[blk(0), blk(0), blk(0), lbs],
        out_shape=[sds, sds, sds, jax.ShapeDtypeStruct((2, HG_H * HD), F32)],
        scratch_shapes=[pltpu.VMEM((HG_G, HD, HD), F32)],
        compiler_params=_cparams(),
    )(p, p, p, lbraw, ecat, ssave, do)


def _pad_ab_cols(w):
    z = jnp.zeros((w.shape[0], HD - GDN_H), w.dtype)
    return jnp.concatenate([w[:, :2048], w[:, 2048:2052], z, w[:, 2052:2056], z, w[:, 2056:]], axis=1)


def _unpad_ab_cols(w):
    return jnp.concatenate([w[:, :2048], w[:, 2048:2052], w[:, 2176:2180], w[:, 2304:]], axis=1)


def _lane_pad(v):
    return jnp.pad(v, ((0, 0), (0, HD - v.shape[1])))


def _mlp_fwd(hb, w1, w2, layer):
    a = _mm(hb, w1, b_view=("cols", layer), name=f"mlp_up_{layer}")
    r = _relu2_fwd(a, f"relu2_{layer}")
    m = _mm(r, w2, b_view=("rows", layer), name=f"mlp_down_{layer}")
    return a, r, m


def _mlp_bwd(hb, a, r, dmb, w1, w2, layer):
    dr = _mm(dmb, w2, tb=True, b_view=("rows", layer), name=f"mlp_down_dx_{layer}")
    dw2 = _mm(r, dmb, ta=True, out_dtype=BF16, name=f"mlp_down_dw_{layer}")
    da = _relu2_bwd(a, dr, f"relu2_bwd_{layer}")
    dh = _mm(da, w1, tb=True, b_view=("cols", layer), name=f"mlp_up_dx_{layer}")
    dw1 = _mm(hb, da, ta=True, out_dtype=BF16, out_split=N_CHIP, name=f"mlp_up_dw_{layer}")
    return dh, dw1, dw2


def _local_step(h0, tgt, w, pad):
    row = lambda a, i: a[i:i + 1]
    ecat = jnp.asarray(_hg_prefix_matrix())
    cw = [w["conv_w"][:, i * 512:(i + 1) * 512] for i in range(3)]
    alog, dtb = _lane_pad(w["a_log"]), _lane_pad(w["dt_bias"])

    h0b = h0.astype(BF16)
    p0 = _mm(h0b, w["ab_w_in"], name="ab_in")
    qn = _conv_fwd(p0, 0, cw[0], "q", pad, "conv_q")
    kn = _conv_fwd(p0, 4, cw[1], "k", pad, "conv_k")
    vn = _conv_fwd(p0, 8, cw[2], "v", pad, "conv_v")
    oa_raw, ss0 = _gdn_fwd(qn, kn, vn, p0, alog, dtb, pad, "gdn_fwd")
    oa = _grms_fwd(oa_raw, p0, 12, w["ab_gnorm_g"], "gdn_gate")
    ob, rtot = _sb_fwd(p0, pad, "sb_fwd")
    ycat = jnp.concatenate([oa, ob], axis=1)
    mix0 = _mm(ycat, w["ab_w_out"], name="ab_out")
    h1, h1b = _ln_res_fwd(h0, mix0, row(w["ln_mix_g"], 0), row(w["ln_mix_b"], 0), "ln_mix_0")
    a0, r0, m0 = _mlp_fwd(h1b, w["mlp_w1"], w["mlp_w2"], 0)
    h2, h2b = _ln_res_fwd(h1, m0, row(w["ln_ffn_g"], 0), row(w["ln_ffn_b"], 0), "ln_ffn_0")
    p1 = _mm(h2b, w["c_w_in"], b_view=("cols", 0), name="c_in")
    oc_raw, ss1 = _hg_fwd(p1, w["c_lb_raw"], ecat, pad, "hg_fwd")
    yc = _grms_fwd(oc_raw, p1, 3 * HG_H, w["c_gnorm_g"], "hg_gate")
    mix1 = _mm(yc, w["c_w_out"], name="c_out")
    h3, h3b = _ln_res_fwd(h2, mix1, row(w["ln_mix_g"], 1), row(w["ln_mix_b"], 1), "ln_mix_1")
    a1, r1, m1 = _mlp_fwd(h3b, w["mlp_w1"], w["mlp_w2"], 1)
    h4, _ = _ln_res_fwd(h3, m1, row(w["ln_ffn_g"], 1), row(w["ln_ffn_b"], 1), "ln_ffn_1")
    loss, dh4 = _loss_fwd(h4, tgt, pad + N_META, "loss")

    zero = jnp.zeros_like(dh4)
    dh3a, dm1b, dfg1, dfb1 = _ln_res_bwd(h3, m1, row(w["ln_ffn_g"], 1), row(w["ln_ffn_b"], 1), dh4, zero, "ln_ffn_bwd_1")
    dh3b, dw1_1, dw2_1 = _mlp_bwd(h3b, a1, r1, dm1b, w["mlp_w1"], w["mlp_w2"], 1)
    dh2a, dmix1b, dmg1, dmb1 = _ln_res_bwd(h2, mix1, row(w["ln_mix_g"], 1), row(w["ln_mix_b"], 1), dh3a, dh3b, "ln_mix_bwd_1")
    dyc = _mm(dmix1b, w["c_w_out"], tb=True, name="c_out_dx")
    dwco = _mm(yc, dmix1b, ta=True, out_dtype=BF16, name="c_out_dw")
    doc, dzc, dcg = _grms_bwd(oc_raw, p1, 3 * HG_H, w["c_gnorm_g"], dyc, 0, "hg_gate_bwd")
    dq1, df1, di1, dlb = _hg_bwd(p1, w["c_lb_raw"], ecat, ss1, doc, pad, "hg_bwd")
    dp1 = _assemble_bf16([(dq1, "cols"), (df1, "cols"), (di1, "cols"), (dzc, "cols")], "c_in_dy")
    dh2b = _mm(dp1, w["c_w_in"], tb=True, b_view=("cols", 0), name="c_in_dx")
    dwc = _mm(h2b, dp1, ta=True, out_dtype=BF16, out_split=N_CHIP, name="c_in_dw")
    dh1a, dm0b, dfg0, dfb0 = _ln_res_bwd(h1, m0, row(w["ln_ffn_g"], 0), row(w["ln_ffn_b"], 0), dh2a, dh2b, "ln_ffn_bwd_0")
    dh1b, dw1_0, dw2_0 = _mlp_bwd(h1b, a0, r0, dm0b, w["mlp_w1"], w["mlp_w2"], 0)
    dh0a, dmix0b, dmg0, dmb0 = _ln_res_bwd(h0, mix0, row(w["ln_mix_g"], 0), row(w["ln_mix_b"], 0), dh1a, dh1b, "ln_mix_bwd_0")
    dycat = _mm(dmix0b, w["ab_w_out"], tb=True, name="ab_out_dx")
    dwabo = _mm(ycat, dmix0b, ta=True, out_dtype=BF16, name="ab_out_dw")
    doa, dza, dag = _grms_bwd(oa_raw, p0, 12, w["ab_gnorm_g"], dycat, 0, "gdn_gate_bwd")
    dqn, dkn, dvn, dbb, daa, dal, ddt = _gdn_bwd(qn, kn, vn, p0, alog, dtb, ss0, doa, pad, "gdn_bwd")
    dpq, dcq = _conv_bwd(p0, 0, cw[0], dqn, "q", pad, "conv_q_bwd")
    dpk, dck = _conv_bwd(p0, 4, cw[1], dkn, "k", pad, "conv_k_bwd")
    dpv, dcv = _conv_bwd(p0, 8, cw[2], dvn, "v", pad, "conv_v_bwd")
    dqb, dkb, dvb = _sb_bwd(p0, rtot, dycat, 4, pad, "sb_bwd")
    dp0 = _assemble_bf16([(dpq, "cols"), (dpk, "cols"), (dpv, "cols"), (dza, "cols"), (dbb, "cols"), (daa, "cols"),
                          (dqb, "cols"), (dkb, "cols"), (dvb, "cols")], "ab_in_dy")
    dh0b = _mm(dp0, w["ab_w_in"], tb=True, name="ab_in_dx")
    dwab = _mm(h0b, dp0, ta=True, out_dtype=BF16, name="ab_in_dw")
    dh0 = _add2(dh0a, dh0b, "dh0")

    grads = {
        "ab_w_in": dwab, "conv_w": jnp.concatenate([dcq, dck, dcv], axis=1),
        "a_log": dal[:, :GDN_H], "dt_bias": ddt[:, :GDN_H],
        "ab_gnorm_g": dag, "ab_w_out": dwabo, "c_w_in": dwc, "c_lb_raw": dlb, "c_gnorm_g": dcg, "c_w_out": dwco,
        "ln_mix_g": jnp.concatenate([dmg0, dmg1], 0), "ln_mix_b": jnp.concatenate([dmb0, dmb1], 0),
        "w1_0": dw1_0, "w1_1": dw1_1, "w2_0": dw2_0, "w2_1": dw2_1,
        "ln_ffn_g": jnp.concatenate([dfg0, dfg1], 0), "ln_ffn_b": jnp.concatenate([dfb0, dfb1], 0),
    }
    return loss, dh0, grads


MESH = pl.DeviceIdType.MESH
ANY = pl.BlockSpec(memory_space=pl.ANY)
N_CHIP = 4
N_DEV = 8
CHIP_REL = ((1, 0), (0, 1), (1, 1))
DEV_REL = tuple((dx, dy, dc) for dx in (0, 1) for dy in (0, 1) for dc in (0, 1))[1:]


def _pos():
    return lax.axis_index("x"), lax.axis_index("y"), lax.axis_index("c")


def _flip(a, d):
    return a + d - 2 * a * d


def _chip_allgather(bufs, name):
    n = len(bufs)

    def body(*refs):
        x_refs, o_refs = refs[:n], refs[n:2 * n]
        send_sems, recv_sems, local_sems = refs[2 * n:]
        x, y, c = _pos()
        local = [pltpu.make_async_copy(x_refs[a], o_refs[a].at[2 * x + y], local_sems.at[a]) for a in range(n)]
        for cp in local:
            cp.start()

        def copy(a, k, slot):
            tx, ty = _flip(x, CHIP_REL[k][0]), _flip(y, CHIP_REL[k][1])
            return pltpu.make_async_remote_copy(
                src_ref=x_refs[a], dst_ref=o_refs[a].at[slot(tx, ty)], send_sem=send_sems.at[3 * a + k],
                recv_sem=recv_sems.at[3 * a + k], device_id=(tx, ty, c), device_id_type=MESH)

        sends = [copy(a, k, lambda tx, ty: 2 * x + y) for a in range(n) for k in range(3)]
        for cp in sends:
            cp.start()
        for a in range(n):
            for k in range(3):
                copy(a, k, lambda tx, ty: 2 * tx + ty).wait_recv()
        for cp in sends:
            cp.wait_send()
        for cp in local:
            cp.wait()

    return pl.pallas_call(
        body, name=name, in_specs=[ANY] * n, out_specs=[ANY] * n,
        out_shape=[jax.ShapeDtypeStruct((N_CHIP,) + b.shape, b.dtype) for b in bufs],
        scratch_shapes=[pltpu.SemaphoreType.DMA((3 * n,)), pltpu.SemaphoreType.DMA((3 * n,)),
                        pltpu.SemaphoreType.DMA((n,))],
        compiler_params=pltpu.CompilerParams(has_side_effects=True),
    )(*bufs)


def _grad_alltoall(gs, name):
    n = len(gs)
    nr = N_DEV - 1

    def body(*refs):
        g_refs, o_refs = refs[:n], refs[n:2 * n]
        send_sems, recv_sems, local_sems = refs[2 * n:]
        x, y, c = _pos()
        me = 4 * x + 2 * y + c
        local = [pltpu.make_async_copy(g_refs[a].at[2 * x + y, c], o_refs[a].at[me], local_sems.at[a]) for a in range(n)]
        for cp in local:
            cp.start()

        def target(k):
            dx, dy, dc = DEV_REL[k]
            return _flip(x, dx), _flip(y, dy), _flip(c, dc)

        def copy(a, k, sending):
            tx, ty, tc = target(k)
            return pltpu.make_async_remote_copy(
                src_ref=g_refs[a].at[2 * tx + ty, tc],
                dst_ref=o_refs[a].at[me if sending else 4 * tx + 2 * ty + tc],
                send_sem=send_sems.at[nr * a + k], recv_sem=recv_sems.at[nr * a + k],
                device_id=(tx, ty, tc), device_id_type=MESH)

        sends = [copy(a, k, True) for a in range(n) for k in range(nr)]
        for cp in sends:
            cp.start()
        for a in range(n):
            for k in range(nr):
                copy(a, k, False).wait_recv()
        for cp in sends:
            cp.wait_send()
        for cp in local:
            cp.wait()

    return pl.pallas_call(
        body, name=name, in_specs=[ANY] * n, out_specs=[ANY] * n,
        out_shape=[jax.ShapeDtypeStruct((N_DEV,) + g.shape[2:], g.dtype) for g in gs],
        scratch_shapes=[pltpu.SemaphoreType.DMA((nr * n,)), pltpu.SemaphoreType.DMA((nr * n,)),
                        pltpu.SemaphoreType.DMA((n,))],
        compiler_params=pltpu.CompilerParams(has_side_effects=True),
    )(*gs)


def _sum_slots(r, name):
    n, rh, w = r.shape
    tr = _pick(rh, (256, 128, 64, 16))

    def body(r_ref, o_ref):
        acc = r_ref[0].astype(F32)
        for s in range(1, n):
            acc = acc + r_ref[s].astype(F32)
        o_ref[...] = acc

    return pl.pallas_call(
        body, name=name, grid=(rh // tr,), in_specs=[pl.BlockSpec((n, tr, w), lambda i: (0, i, 0))],
        out_specs=pl.BlockSpec((tr, w), lambda i: (i, 0)), out_shape=jax.ShapeDtypeStruct((rh, w), F32),
        compiler_params=_cparams(),
    )(r)


def _sibling_exchange(halves, dests, out_shapes, name):
    n = len(halves)
    no = len(out_shapes)

    def body(*refs):
        h_refs, o_refs = refs[:n], refs[n:n + no]
        send_sems, recv_sems, local_sems = refs[n + no:]
        x, y, c = _pos()

        def slot(a, cc):
            o, lead = dests[a]
            return o_refs[o].at[(*lead, cc)]

        local = [pltpu.make_async_copy(h_refs[a], slot(a, c), local_sems.at[a]) for a in range(n)]
        sends = [pltpu.make_async_remote_copy(src_ref=h_refs[a], dst_ref=slot(a, c), send_sem=send_sems.at[a],
                                              recv_sem=recv_sems.at[a], device_id=(x, y, 1 - c), device_id_type=MESH)
                 for a in range(n)]
        for cp in local + sends:
            cp.start()
        for a in range(n):
            pltpu.make_async_remote_copy(src_ref=h_refs[a], dst_ref=slot(a, 1 - c), send_sem=send_sems.at[a],
                                         recv_sem=recv_sems.at[a], device_id=(x, y, 1 - c), device_id_type=MESH).wait_recv()
        for cp in sends:
            cp.wait_send()
        for cp in local:
            cp.wait()

    return pl.pallas_call(
        body, name=name, in_specs=[ANY] * n, out_specs=[ANY] * no,
        out_shape=[jax.ShapeDtypeStruct(s, F32) for s in out_shapes],
        scratch_shapes=[pltpu.SemaphoreType.DMA((n,)), pltpu.SemaphoreType.DMA((n,)), pltpu.SemaphoreType.DMA((n,))],
        compiler_params=pltpu.CompilerParams(has_side_effects=True),
    )(*halves)


def _small_allreduce(buf, name):
    r, w = buf.shape

    def body(b_ref, o_ref, land_ref, send_sems, recv_sems):
        x, y, c = _pos()
        me = 4 * x + 2 * y + c
        land_ref[me] = b_ref[...]

        def target(k):
            dx, dy, dc = DEV_REL[k]
            return _flip(x, dx), _flip(y, dy), _flip(c, dc)

        sends = []
        for k in range(N_DEV - 1):
            tx, ty, tc = target(k)
            cp = pltpu.make_async_remote_copy(
                src_ref=b_ref, dst_ref=land_ref.at[me], send_sem=send_sems.at[k], recv_sem=recv_sems.at[k],
                device_id=(tx, ty, tc), device_id_type=MESH)
            cp.start()
            sends.append(cp)
        for k in range(N_DEV - 1):
            tx, ty, tc = target(k)
            pltpu.make_async_remote_copy(
                src_ref=b_ref, dst_ref=land_ref.at[4 * tx + 2 * ty + tc], send_sem=send_sems.at[k],
                recv_sem=recv_sems.at[k], device_id=(tx, ty, tc), device_id_type=MESH).wait_recv()
        for cp in sends:
            cp.wait_send()
        acc = land_ref[0]
        for s in range(1, N_DEV):
            acc = acc + land_ref[s]
        o_ref[...] = acc

    vm = pl.BlockSpec(memory_space=pltpu.VMEM)
    return pl.pallas_call(
        body, name=name, in_specs=[vm], out_specs=vm, out_shape=jax.ShapeDtypeStruct((r, w), F32),
        scratch_shapes=[pltpu.VMEM((N_DEV, r, w), F32), pltpu.SemaphoreType.DMA((N_DEV - 1,)),
                        pltpu.SemaphoreType.DMA((N_DEV - 1,))],
        compiler_params=pltpu.CompilerParams(has_side_effects=True),
    )(buf)


def _adamw(w, g, m, v, name):
    r, c = w.shape
    tr = _pick(r, (256, 128, 64, 8)) if r * c > (1 << 18) else r

    def body(w_ref, g_ref, m_ref, v_ref, d_ref, m2_ref, v2_ref):
        gg = g_ref[...]
        m2 = ADAM_B1 * m_ref[...] + (1.0 - ADAM_B1) * gg
        v2 = ADAM_B2 * v_ref[...] + (1.0 - ADAM_B2) * (gg * gg)
        m_hat = m2 / (1.0 - ADAM_B1 ** ADAM_STEP)
        v_hat = v2 / (1.0 - ADAM_B2 ** ADAM_STEP)
        d_ref[...] = -ADAM_LR * (m_hat / (jnp.sqrt(v_hat) + ADAM_EPS) + ADAM_WD * w_ref[...])
        m2_ref[...] = m2
        v2_ref[...] = v2

    blk = pl.BlockSpec((tr, c), lambda i: (i, 0))
    sds = jax.ShapeDtypeStruct((r, c), F32)
    return pl.pallas_call(body, name=name, grid=(r // tr,), in_specs=[blk] * 4, out_specs=[blk] * 3,
                          out_shape=[sds] * 3, compiler_params=_cparams())(w, g, m, v)


BIG = ("ab_w_in", "ab_w_out", "c_w_in", "c_w_out", "mlp_w1", "mlp_w2")
SMALL = ("ln_mix_g", "ln_mix_b", "ln_ffn_g", "ln_ffn_b", "c_lb_raw", "ab_a_log", "ab_dt_bias", "ab_gnorm_g", "c_gnorm_g")
SMALL_ROWS = 16
CONV_ROWS = 8
CONV_W = 3 * GDN_H * HD


def _conv_to_rows(cw):
    return jnp.pad(cw, ((0, 0), (0, 2 * D - CONV_W))).reshape(CONV_ROWS, D)


def _rows_to_conv(rows):
    return rows.reshape(CONV_K, 2 * D)[:, :CONV_W]


def _pack_small(d):
    rows = [jnp.pad(d[n], ((0, 0), (0, D - d[n].shape[1]))) for n in SMALL]
    buf = jnp.concatenate(rows, axis=0)
    return jnp.pad(buf, ((0, SMALL_ROWS - buf.shape[0]), (0, 0)))


def _unpack_small(buf, like):
    out, r = {}, 0
    for n in SMALL:
        nr, nc = like[n].shape
        out[n] = buf[r:r + nr, :nc]
        r += nr
    return out


def kernel(x, meta_tokens, ab_w_in, ab_conv_w, ab_a_log, ab_dt_bias, ab_gnorm_g, ab_w_out, c_w_in, c_lb_raw, c_gnorm_g, c_w_out, ln_mix_g, ln_mix_b, mlp_w1, mlp_w2, ln_ffn_g, ln_ffn_b, loss_target, m_meta_tokens, m_ab_w_in, m_ab_conv_w, m_ab_a_log, m_ab_dt_bias, m_ab_gnorm_g, m_ab_w_out, m_c_w_in, m_c_lb_raw, m_c_gnorm_g, m_c_w_out, m_ln_mix_g, m_ln_mix_b, m_mlp_w1, m_mlp_w2, m_ln_ffn_g, m_ln_ffn_b, v_meta_tokens, v_ab_w_in, v_ab_conv_w, v_ab_a_log, v_ab_dt_bias, v_ab_gnorm_g, v_ab_w_out, v_c_w_in, v_c_lb_raw, v_c_gnorm_g, v_c_w_out, v_ln_mix_g, v_ln_mix_b, v_mlp_w1, v_mlp_w2, v_ln_ffn_g, v_ln_ffn_b):
    names = ("meta_tokens", "ab_w_in", "ab_conv_w", "ab_a_log", "ab_dt_bias", "ab_gnorm_g", "ab_w_out", "c_w_in",
             "c_lb_raw", "c_gnorm_g", "c_w_out", "ln_mix_g", "ln_mix_b", "mlp_w1", "mlp_w2", "ln_ffn_g", "ln_ffn_b")
    wts = dict(zip(names, (meta_tokens, ab_w_in, ab_conv_w, ab_a_log, ab_dt_bias, ab_gnorm_g, ab_w_out, c_w_in, c_lb_raw,
                           c_gnorm_g, c_w_out, ln_mix_g, ln_mix_b, mlp_w1, mlp_w2, ln_ffn_g, ln_ffn_b)))
    mom_m = dict(zip(names, (m_meta_tokens, m_ab_w_in, m_ab_conv_w, m_ab_a_log, m_ab_dt_bias, m_ab_gnorm_g, m_ab_w_out,
                             m_c_w_in, m_c_lb_raw, m_c_gnorm_g, m_c_w_out, m_ln_mix_g, m_ln_mix_b, m_mlp_w1, m_mlp_w2,
                             m_ln_ffn_g, m_ln_ffn_b)))
    mom_v = dict(zip(names, (v_meta_tokens, v_ab_w_in, v_ab_conv_w, v_ab_a_log, v_ab_dt_bias, v_ab_gnorm_g, v_ab_w_out,
                             v_c_w_in, v_c_lb_raw, v_c_gnorm_g, v_c_w_out, v_ln_mix_g, v_ln_mix_b, v_mlp_w1, v_mlp_w2,
                             v_ln_ffn_g, v_ln_ffn_b)))
    seq = x.shape[1]
    pad = (-(N_META + seq)) % QB
    xi, yi, ci = _pos()
    chip = 2 * xi + yi

    gat = dict(zip(BIG, _chip_allgather(
        [ab_w_in[0].astype(BF16), ab_w_out[0].astype(BF16), c_w_in.astype(BF16), c_w_out[0].astype(BF16),
         mlp_w1.astype(BF16), mlp_w2.astype(BF16)], "gather_weights")))
    mcols, ccols = meta_tokens.shape[1], ab_conv_w.shape[2]
    place = jnp.concatenate([
        lax.dynamic_update_slice(jnp.zeros((N_META, D), F32), 0.5 * meta_tokens, (0, chip * mcols)),
        _conv_to_rows(lax.dynamic_update_slice(jnp.zeros((CONV_K, CONV_W), F32), 0.5 * ab_conv_w[0], (0, chip * ccols)))],
        axis=0)
    placed = _small_allreduce(place, "gather_meta")
    meta_full = placed[:N_META]

    w = {
        "ab_w_in": _pad_ab_cols(jnp.transpose(gat["ab_w_in"], (1, 0, 2)).reshape(D, AB_TRUE)),
        "conv_w": _rows_to_conv(placed[N_META:]), "a_log": ab_a_log, "dt_bias": ab_dt_bias,
        "ab_gnorm_g": ab_gnorm_g, "ab_w_out": gat["ab_w_out"].reshape(D, D), "c_w_in": gat["c_w_in"], "c_lb_raw": c_lb_raw,
        "c_gnorm_g": c_gnorm_g, "c_w_out": gat["c_w_out"].reshape(D, D), "ln_mix_g": ln_mix_g, "ln_mix_b": ln_mix_b,
        "ln_ffn_g": ln_ffn_g, "ln_ffn_b": ln_ffn_b, "mlp_w1": gat["mlp_w1"], "mlp_w2": gat["mlp_w2"],
    }

    h0 = jnp.concatenate([jnp.zeros((pad, D), F32), meta_full, x[0]], axis=0)
    tgt = jnp.concatenate([jnp.zeros((pad + N_META, D), F32), loss_target[0]], axis=0)
    loss8, dh0, g = _local_step(h0, tgt, w, pad)
    loss = lax.psum(loss8[0, 0], ("x", "y", "c"))
    grad_x = dh0[pad + N_META:][None]

    gsmall = {"ln_mix_g": g["ln_mix_g"], "ln_mix_b": g["ln_mix_b"], "ln_ffn_g": g["ln_ffn_g"], "ln_ffn_b": g["ln_ffn_b"],
              "c_lb_raw": g["c_lb_raw"], "ab_a_log": g["a_log"], "ab_dt_bias": g["dt_bias"], "ab_gnorm_g": g["ab_gnorm_g"],
              "c_gnorm_g": g["c_gnorm_g"]}
    sbuf = jnp.concatenate([_pack_small(gsmall), dh0[pad:pad + N_META], _conv_to_rows(g["conv_w"])], axis=0)
    ssum = _small_allreduce(sbuf, "allreduce_small")
    grads = _unpack_small(ssum[:SMALL_ROWS], wts)
    grads["meta_tokens"] = lax.dynamic_slice(ssum[SMALL_ROWS:SMALL_ROWS + N_META], (0, chip * mcols), (N_META, mcols))
    grads["ab_conv_w"] = lax.dynamic_slice(_rows_to_conv(ssum[SMALL_ROWS + N_META:]), (0, chip * ccols), (CONV_K, ccols))[None]

    dab = _unpad_ab_cols(g["ab_w_in"])
    items = [jnp.transpose(dab.reshape(D, N_CHIP, AB_TRUE // N_CHIP), (1, 0, 2)), g["ab_w_out"].reshape(N_CHIP, -1, D),
             g["c_w_in"], g["c_w_out"].reshape(N_CHIP, -1, D), g["w1_0"], g["w1_1"],
             g["w2_0"].reshape(N_CHIP, -1, D), g["w2_1"].reshape(N_CHIP, -1, D)]
    landed = _grad_alltoall([a.reshape(N_CHIP, 2, a.shape[1] // 2, a.shape[2]) for a in items], "grad_alltoall")
    halves = [_sum_slots(l, f"grad_sum_{i}") for i, l in enumerate(landed)]
    dests = [(0, ()), (1, ()), (2, ()), (3, ()), (4, (0,)), (4, (1,)), (5, (0,)), (5, (1,))]
    out_shapes = [(2,) + halves[i].shape for i in range(4)] + [(2, 2) + halves[4].shape, (2, 2) + halves[6].shape]
    for n, o in zip(BIG, _sibling_exchange(halves, dests, out_shapes, "grad_swap")):
        grads[n] = o.reshape(wts[n].shape)

    delta, new_m, new_v = {}, {}, {}
    for n in ("meta_tokens", "ab_conv_w") + BIG:
        shp = wts[n].shape
        to2 = lambda a: a.reshape(-1, shp[-1])
        d2, m2, v2 = _adamw(to2(wts[n]), to2(grads[n]), to2(mom_m[n]), to2(mom_v[n]), f"adamw_{n}")
        delta[n], new_m[n], new_v[n] = d2.reshape(shp), m2.reshape(shp), v2.reshape(shp)
    d2, m2, v2 = _adamw(_pack_small(wts), ssum[:SMALL_ROWS], _pack_small(mom_m), _pack_small(mom_v), "adamw_small")
    delta.update(_unpack_small(d2, wts))
    new_m.update(_unpack_small(m2, wts))
    new_v.update(_unpack_small(v2, wts))

    return (loss, grad_x, *[grads[n] for n in names], *[delta[n] for n in names], *[new_m[n] for n in names],
            *[new_v[n] for n in names])
```

```python
import functools
import math

import numpy as np
import jax
import jax.numpy as jnp
from jax import lax
from jax.experimental import pallas as pl
from jax.experimental.pallas import tpu as pltpu

F32 = jnp.float32
BF16 = jnp.bfloat16

D = 1024
N_META = 16
D_FF = 4 * D
DEPTH = 2
GDN_H = 4
SB_H = 8
SB_DH = 64
HG_H = 8
HD = 128
CH = 64
QB = 128
ALPHA = float((2 * DEPTH) ** 0.25)
LN_EPS = 1e-5
RMS_EPS = 1e-6
L2_EPS = 1e-6
NEG = -1e30

ADAM_LR = 0.001
ADAM_B1 = 0.9
ADAM_B2 = 0.999
ADAM_EPS = 1e-08
ADAM_WD = 0.01
ADAM_STEP = 10

AB_W = 30 * HD
AB_TRUE = 3592
VMEM_LIMIT = 56 * 1024 * 1024

NN = ((1,), (0,))
NT = ((1,), (1,))
TN = ((0,), (0,))


def _cparams(**kw):
    return pltpu.CompilerParams(vmem_limit_bytes=VMEM_LIMIT, **kw)


def _dg(a, b, dims, mode):
    if mode == "h":
        return lax.dot_general(a, b, dims, precision=lax.Precision.HIGHEST, preferred_element_type=F32)
    if mode == "b":
        return lax.dot_general(a.astype(BF16), b.astype(BF16), dims, preferred_element_type=F32)
    ah, bh = a.astype(BF16), b.astype(BF16)
    al, bl = (a - ah.astype(F32)).astype(BF16), (b - bh.astype(F32)).astype(BF16)
    d = lambda x, y: lax.dot_general(x, y, dims, preferred_element_type=F32)
    return d(ah, bh) + (d(ah, bl) + d(al, bh))


def _make_dots(mode, batched=False):
    if batched:
        nn_d, nt_d, tn_d = (((2,), (1,)), ((0,), (0,))), (((2,), (2,)), ((0,), (0,))), (((1,), (1,)), ((0,), (0,)))
    else:
        nn_d, nt_d, tn_d = (NN, ((), ())), (NT, ((), ())), (TN, ((), ()))

    @jax.custom_vjp
    def nn(a, b):
        return _dg(a, b, nn_d, mode)

    @jax.custom_vjp
    def nt(a, b):
        return _dg(a, b, nt_d, mode)

    @jax.custom_vjp
    def tn(a, b):
        return _dg(a, b, tn_d, mode)

    nn.defvjp(lambda a, b: (nn(a, b), (a, b)), lambda r, g: (nt(g, r[1]), tn(r[0], g)))
    nt.defvjp(lambda a, b: (nt(a, b), (a, b)), lambda r, g: (nn(g, r[1]), tn(g, r[0])))
    tn.defvjp(lambda a, b: (tn(a, b), (a, b)), lambda r, g: (nt(r[1], g), nn(r[0], g)))
    return nn, nt, tn


hnn, hnt, htn = _make_dots("h")
bbnn, bbnt, bbtn = _make_dots("b", True)
mbnn, mbnt, mbtn = _make_dots("m", True)
hbnn, hbnt, hbtn = _make_dots("h", True)


def _heads(a, n):
    return jnp.stack([a[:, h * HD:(h + 1) * HD] for h in range(n)])


def _sigmoid(x):
    return jax.nn.sigmoid(x)


def _silu(x):
    return x * jax.nn.sigmoid(x)


def _softplus(x):
    return jnp.maximum(x, 0.0) + jnp.log(1.0 + jnp.exp(-jnp.abs(x)))


def _iota(shape, dim):
    return lax.broadcasted_iota(jnp.int32, shape, dim)


def _pick(n, prefs):
    for p in prefs:
        if n % p == 0:
            return p
    return n


def _mm(a, b, *, ta=False, tb=False, out_dtype=F32, name, b_view=None, out_split=0, act=False, gate=None):
    if ta:
        k_dim, m_dim = a.shape
    else:
        m_dim, k_dim = a.shape
    if b_view is None:
        w_rows, w_cols = b.shape
    else:
        kind, layer = b_view
        nj, _, blk_r, blk_c = b.shape
        w_rows, w_cols = (blk_r, nj * blk_c) if kind == "cols" else (nj * blk_r, blk_c)
    n_dim = w_rows if tb else w_cols
    assert (w_cols if tb else w_rows) == k_dim
    tm = _pick(m_dim, (1024, 1056, 704, 640, 512, 384, 256, 128))
    tn = _pick(n_dim, (1024, 1056, 704, 640, 512, 384, 256, 128))
    tk = _pick(k_dim, (1024, 1056, 704, 512, 384, 256, 128))
    nk = k_dim // tk
    a_spec = pl.BlockSpec((tk, tm), lambda i, j, k: (k, i)) if ta else pl.BlockSpec((tm, tk), lambda i, j, k: (i, k))
    wb = (tn, tk) if tb else (tk, tn)
    w_idx = (lambda i, j, k: (j, k)) if tb else (lambda i, j, k: (k, j))
    if b_view is None:
        b_spec = pl.BlockSpec(wb, w_idx)
    elif kind == "cols":
        per = blk_c // wb[1]
        b_spec = pl.BlockSpec((None, None) + wb,
                              lambda i, j, k: (w_idx(i, j, k)[1] // per, layer, w_idx(i, j, k)[0], w_idx(i, j, k)[1] % per))
    else:
        per = blk_r // wb[0]
        b_spec = pl.BlockSpec((None, None) + wb,
                              lambda i, j, k: (w_idx(i, j, k)[0] // per, layer, w_idx(i, j, k)[0] % per, w_idx(i, j, k)[1]))
    if out_split:
        per_o = (n_dim // out_split) // tn
        out_spec = pl.BlockSpec((None, tm, tn), lambda i, j, k: (j // per_o, i, j % per_o))
        out_sds = jax.ShapeDtypeStruct((out_split, m_dim, n_dim // out_split), out_dtype)
    else:
        out_spec = pl.BlockSpec((tm, tn), lambda i, j, k: (i, j))
        out_sds = jax.ShapeDtypeStruct((m_dim, n_dim), out_dtype)
    dims = (((0 if ta else 1,), (1 if tb else 0,)), ((), ()))
    extra = [] if gate is None else [gate]
    n_out = 2 if act else 1

    def finish(acc, refs):
        if act:
            refs[0][...] = acc.astype(refs[0].dtype)
            r = jnp.maximum(acc, 0.0)
            refs[1][...] = (r * r).astype(refs[1].dtype)
        elif gate is not None:
            refs[1][...] = (acc * (2.0 * jnp.maximum(refs[0][...].astype(F32), 0.0))).astype(refs[1].dtype)
        else:
            refs[0][...] = acc.astype(refs[0].dtype)

    def body(a_ref, b_ref, *rest):
        refs, acc_ref = rest[:-1], rest[-1]
        part = lax.dot_general(a_ref[...], b_ref[...], dims, preferred_element_type=F32)
        if nk == 1:
            finish(part, refs)
        else:
            k = pl.program_id(2)

            @pl.when(k == 0)
            def _():
                acc_ref[...] = part

            @pl.when(k > 0)
            def _():
                acc_ref[...] += part

            @pl.when(k == nk - 1)
            def _():
                finish(acc_ref[...], refs)

    out = pl.pallas_call(
        body, name=name, grid=(m_dim // tm, n_dim // tn, nk),
        in_specs=[a_spec, b_spec] + [pl.BlockSpec((tm, tn), lambda i, j, k: (i, j))] * len(extra),
        out_specs=[out_spec] * n_out,
        out_shape=[out_sds] * n_out,
        scratch_shapes=[pltpu.VMEM((tm, tn) if nk > 1 else (8, 128), F32)],
        compiler_params=_cparams(dimension_semantics=("parallel", "parallel", "arbitrary")),
    )(a, b, *extra)
    return out if act else out[0]


def _row_tile(t_pad, width):
    for tr in (528, 352, 176, 128, 64):
        if t_pad % tr == 0 and tr * width * 4 <= (3 << 19) and tr % 16 == 0:
            return tr
    return 64 if t_pad % 64 == 0 else t_pad


def _ln_res_fn(h, m, g, b):
    x = ALPHA * h + m
    mu = jnp.mean(x, axis=-1, keepdims=True)
    xc = x - mu
    var = jnp.mean(xc * xc, axis=-1, keepdims=True)
    return xc * lax.rsqrt(var + LN_EPS) * g + b


def _ln_res_fwd(h, m, g, b, name):
    t_pad = h.shape[0]
    tr = _row_tile(t_pad, D)

    def body(h_ref, m_ref, g_ref, b_ref, y_ref, yb_ref):
        y = _ln_res_fn(h_ref[...], m_ref[...], g_ref[...], b_ref[...])
        y_ref[...] = y
        yb_ref[...] = y.astype(BF16)

    row = pl.BlockSpec((tr, D), lambda i: (i, 0))
    par = pl.BlockSpec((1, D), lambda i: (0, 0))
    return pl.pallas_call(
        body, name=name, grid=(t_pad // tr,), in_specs=[row, row, par, par], out_specs=[row, row],
        out_shape=[jax.ShapeDtypeStruct((t_pad, D), F32), jax.ShapeDtypeStruct((t_pad, D), BF16)],
        compiler_params=_cparams(),
    )(h, m, g, b)


def _ln_res_bwd(h, m, g, b, dy1, dy2, name):
    t_pad = h.shape[0]
    tr = _row_tile(t_pad, D)

    def body(h_ref, m_ref, g_ref, b_ref, d1_ref, d2_ref, dh_ref, dm_ref, dg_ref, db_ref):
        _, vjp = jax.vjp(_ln_res_fn, h_ref[...], m_ref[...], g_ref[...], b_ref[...])
        dh, dm, dg, db = vjp(d1_ref[...] + d2_ref[...])
        dh_ref[...] = dh
        dm_ref[...] = dm.astype(BF16)

        @pl.when(pl.program_id(0) == 0)
        def _():
            dg_ref[...] = jnp.zeros_like(dg_ref)
            db_ref[...] = jnp.zeros_like(db_ref)

        dg_ref[...] += dg
        db_ref[...] += db

    row = pl.BlockSpec((tr, D), lambda i: (i, 0))
    par = pl.BlockSpec((1, D), lambda i: (0, 0))
    return pl.pallas_call(
        body, name=name, grid=(t_pad // tr,), in_specs=[row, row, par, par, row, row],
        out_specs=[row, row, par, par],
        out_shape=[jax.ShapeDtypeStruct((t_pad, D), F32), jax.ShapeDtypeStruct((t_pad, D), BF16),
                   jax.ShapeDtypeStruct((1, D), F32), jax.ShapeDtypeStruct((1, D), F32)],
        compiler_params=_cparams(),
    )(h, m, g, b, dy1, dy2)


def _grms_fn(o, z, g):
    y = o * lax.rsqrt(jnp.mean(o * o, axis=-1, keepdims=True) + RMS_EPS) * g
    return y * _silu(z)


def _grms_fwd(o, z_arr, z_blk0, g, name):
    t_pad, w = o.shape
    nh = w // HD
    tr = _row_tile(t_pad, HD * 4)

    def body(o_ref, z_ref, g_ref, y_ref):
        y_ref[...] = _grms_fn(o_ref[...], z_ref[...], g_ref[...]).astype(BF16)

    return pl.pallas_call(
        body, name=name, grid=(t_pad // tr, nh),
        in_specs=[pl.BlockSpec((tr, HD), lambda i, h: (i, h)),
                  pl.BlockSpec((tr, HD), lambda i, h: (i, z_blk0 + h)),
                  pl.BlockSpec((1, HD), lambda i, h: (0, 0))],
        out_specs=pl.BlockSpec((tr, HD), lambda i, h: (i, h)),
        out_shape=jax.ShapeDtypeStruct((t_pad, w), BF16), compiler_params=_cparams(),
    )(o, z_arr, g)


def _grms_bwd(o, z_arr, z_blk0, g, dy_arr, dy_blk0, name):
    t_pad, w = o.shape
    nh = w // HD
    tr = _row_tile(t_pad, HD * 4)

    def body(o_ref, z_ref, g_ref, dy_ref, do_ref, dz_ref, dg_ref):
        _, vjp = jax.vjp(_grms_fn, o_ref[...], z_ref[...], g_ref[...])
        do, dz, dg = vjp(dy_ref[...])
        do_ref[...] = do
        dz_ref[...] = dz

        @pl.when((pl.program_id(0) == 0) & (pl.program_id(1) == 0))
        def _():
            dg_ref[...] = jnp.zeros_like(dg_ref)

        dg_ref[...] += dg

    blk = pl.BlockSpec((tr, HD), lambda i, h: (i, h))
    return pl.pallas_call(
        body, name=name, grid=(t_pad // tr, nh),
        in_specs=[blk, pl.BlockSpec((tr, HD), lambda i, h: (i, z_blk0 + h)),
                  pl.BlockSpec((1, HD), lambda i, h: (0, 0)),
                  pl.BlockSpec((tr, HD), lambda i, h: (i, dy_blk0 + h))],
        out_specs=[blk, blk, pl.BlockSpec((1, HD), lambda i, h: (0, 0))],
        out_shape=[jax.ShapeDtypeStruct((t_pad, w), F32), jax.ShapeDtypeStruct((t_pad, w), F32),
                   jax.ShapeDtypeStruct((1, HD), F32)],
        compiler_params=_cparams(),
    )(o, z_arr, g, dy_arr)


def _loss_fwd(y, tgt, first_row, name):
    t_pad = y.shape[0]
    tr = _row_tile(t_pad, D)

    def body(y_ref, t_ref, l_ref, dy_ref):
        rows = pl.program_id(0) * tr + _iota((tr, 1), 0)
        err = jnp.where(rows >= first_row, y_ref[...] - t_ref[...], 0.0)
        dy_ref[...] = err * (1.0 / D)

        @pl.when(pl.program_id(0) == 0)
        def _():
            l_ref[...] = jnp.zeros_like(l_ref)

        part = jnp.sum(jnp.sum(err * err, axis=1, keepdims=True), axis=0, keepdims=True)
        l_ref[...] += jnp.broadcast_to(part * (0.5 / D), l_ref.shape)

    row = pl.BlockSpec((tr, D), lambda i: (i, 0))
    return pl.pallas_call(
        body, name=name, grid=(t_pad // tr,), in_specs=[row, row],
        out_specs=[pl.BlockSpec((8, 128), lambda i: (0, 0)), row],
        out_shape=[jax.ShapeDtypeStruct((8, 128), F32), jax.ShapeDtypeStruct((t_pad, D), F32)],
        compiler_params=_cparams(),
    )(y, tgt)


def _add2(a, b, name):
    t_pad, w = a.shape
    tr = _row_tile(t_pad, w)

    def body(a_ref, b_ref, o_ref):
        o_ref[...] = a_ref[...] + b_ref[...]

    row = pl.BlockSpec((tr, w), lambda i: (i, 0))
    return pl.pallas_call(body, name=name, grid=(t_pad // tr,), in_specs=[row, row], out_specs=row,
                          out_shape=jax.ShapeDtypeStruct((t_pad, w), F32), compiler_params=_cparams())(a, b)


def _assemble_bf16(parts, name):
    t_pad = parts[0][0].shape[0] if parts[0][1] == "cols" else parts[0][0].shape[1]
    widths = [p.shape[1] if kind == "cols" else HD for p, kind in parts]
    total = sum(widths)
    tr = _row_tile(t_pad, total)

    def body(*refs):
        o_ref = refs[-1]
        off = 0
        for ref, (p, kind), w in zip(refs[:-1], parts, widths):
            if kind == "cols":
                o_ref[:, off:off + w] = ref[...].astype(BF16)
            else:
                acc = ref[0]
                for hh in range(1, p.shape[0]):
                    acc = acc + ref[hh]
                o_ref[:, off:off + w] = acc.astype(BF16)
            off += w

    specs = []
    for p, kind in parts:
        if kind == "cols":
            specs.append(pl.BlockSpec((tr, p.shape[1]), lambda i: (i, 0)))
        else:
            specs.append(pl.BlockSpec((p.shape[0], tr, HD), lambda i: (0, i, 0)))
    return pl.pallas_call(
        body, name=name, grid=(t_pad // tr,), in_specs=specs,
        out_specs=pl.BlockSpec((tr, total), lambda i: (i, 0)),
        out_shape=jax.ShapeDtypeStruct((t_pad, total), BF16), compiler_params=_cparams(),
    )(*[p for p, _ in parts])


CONV_K = 4
HALO = 8
RT = 128


def _conv_fwd(p, blk0, w, mode, pad, name):
    t_pad = p.shape[0]
    nt = t_pad // RT
    scale = HD ** -0.5 if mode == "q" else 1.0

    def body(x_ref, w_ref, y_ref, xs_ref):
        xs_ref[0:HALO, :] = jnp.zeros((HALO, HD), F32)
        rows = _iota((t_pad, 1), 0)
        xs_ref[HALO:HALO + t_pad, :] = jnp.where(rows >= pad, x_ref[...], 0.0)
        wv = w_ref[...]

        def tile(i, carry):
            r0 = pl.multiple_of(i * RT, RT)
            ext = xs_ref[pl.ds(r0, RT + HALO), :]
            acc = ext[HALO:, :] * wv[3:4, :]
            for s in (1, 2, 3):
                acc = acc + pltpu.roll(ext, s, 0)[HALO:, :] * wv[3 - s:4 - s, :]
            y = _silu(acc)
            if mode != "v":
                y = y * lax.rsqrt(jnp.sum(y * y, axis=-1, keepdims=True) + L2_EPS) * scale
            y_ref[pl.ds(r0, RT), :] = y
            return carry

        lax.fori_loop(0, nt, tile, 0)

    return pl.pallas_call(
        body, name=name, grid=(GDN_H,),
        in_specs=[pl.BlockSpec((t_pad, HD), lambda h: (0, blk0 + h)), pl.BlockSpec((CONV_K, HD), lambda h: (0, h))],
        out_specs=pl.BlockSpec((t_pad, HD), lambda h: (0, h)),
        out_shape=jax.ShapeDtypeStruct((t_pad, GDN_H * HD), F32),
        scratch_shapes=[pltpu.VMEM((t_pad + HALO, HD), F32)],
        compiler_params=_cparams(),
    )(p, w)


def _conv_bwd(p, blk0, w, dn, mode, pad, name):
    t_pad = p.shape[0]
    nt = t_pad // RT
    scale = HD ** -0.5 if mode == "q" else 1.0

    def body(x_ref, w_ref, dn_ref, dx_ref, dw_ref, xs_ref, ds_ref):
        xs_ref[0:HALO, :] = jnp.zeros((HALO, HD), F32)
        xs_ref[HALO + t_pad:HALO + t_pad + 2 * HALO, :] = jnp.zeros((2 * HALO, HD), F32)
        ds_ref[t_pad:t_pad + HALO, :] = jnp.zeros((HALO, HD), F32)
        rows = _iota((t_pad, 1), 0)
        xs_ref[HALO:HALO + t_pad, :] = jnp.where(rows >= pad, x_ref[...], 0.0)
        ds_ref[0:t_pad, :] = dn_ref[...]
        wv = w_ref[...]

        def tile(i, dw):
            r0 = pl.multiple_of(i * RT, RT)
            ext = xs_ref[pl.ds(r0, RT + 2 * HALO), :]
            dn_e = ds_ref[pl.ds(r0, RT + HALO), :]
            xsh = [ext[HALO:, :]] + [pltpu.roll(ext, s, 0)[HALO:, :] for s in (1, 2, 3)]
            pre = xsh[0] * wv[3:4, :]
            for s in (1, 2, 3):
                pre = pre + xsh[s] * wv[3 - s:4 - s, :]
            sg = _sigmoid(pre)
            y = pre * sg
            if mode != "v":
                ss = jnp.sum(y * y, axis=-1, keepdims=True) + L2_EPS
                r = lax.rsqrt(ss)
                dy = scale * (dn_e * r - y * (r * r * r) * jnp.sum(dn_e * y, axis=-1, keepdims=True))
            else:
                dy = dn_e
            dpre = dy * (sg * (1.0 + pre * (1.0 - sg)))
            dx = dpre[:RT, :] * wv[3:4, :]
            for s in (1, 2, 3):
                dx = dx + pltpu.roll(dpre, RT + HALO - s, 0)[:RT, :] * wv[3 - s:4 - s, :]
            trow = r0 + _iota((RT, 1), 0)
            dx_ref[pl.ds(r0, RT), :] = jnp.where(trow >= pad, dx, 0.0)
            new = []
            for s in (0, 1, 2, 3):
                new.append(dw[s] + jnp.sum(dpre[:RT, :] * xsh[s][:RT, :], axis=0, keepdims=True))
            return tuple(new)

        z = jnp.zeros((1, HD), F32)
        dw = lax.fori_loop(0, nt, tile, (z, z, z, z))
        for s in (0, 1, 2, 3):
            dw_ref[3 - s:4 - s, :] = dw[s]

    return pl.pallas_call(
        body, name=name, grid=(GDN_H,),
        in_specs=[pl.BlockSpec((t_pad, HD), lambda h: (0, blk0 + h)), pl.BlockSpec((CONV_K, HD), lambda h: (0, h)),
                  pl.BlockSpec((t_pad, HD), lambda h: (0, h))],
        out_specs=[pl.BlockSpec((t_pad, HD), lambda h: (0, h)), pl.BlockSpec((CONV_K, HD), lambda h: (0, h))],
        out_shape=[jax.ShapeDtypeStruct((t_pad, GDN_H * HD), F32), jax.ShapeDtypeStruct((CONV_K, GDN_H * HD), F32)],
        scratch_shapes=[pltpu.VMEM((t_pad + 3 * HALO, HD), F32), pltpu.VMEM((t_pad + HALO, HD), F32)],
        compiler_params=_cparams(),
    )(p, w, dn)


def _unit_lower_inv(m, bd, eye):
    md = m * bd
    low = m - md
    p2 = mbnn(md, md)
    p4 = mbnn(p2, p2)
    p8 = mbnn(p4, p4)
    dinv = mbnn(mbnn(mbnn(eye - md, eye + p2), eye + p4), eye + p8)
    n = mbnn(dinv, low)
    n2 = mbnn(n, n)
    return mbnn(mbnn(eye - n, eye + n2), dinv)


def _gdn_chunk(q, k, v, bb, aa, alog, dtb, s, valid):
    nh = q.shape[0]
    ri = _iota((1, CH, CH), 1)
    ci = _iota((1, CH, CH), 2)
    causal = ri >= ci
    strict = ri > ci
    eye = (ri == ci).astype(F32)
    bd = ((ri >> 4) == (ci >> 4)).astype(F32)
    ltri = (_iota((CH, CH), 0) >= _iota((CH, CH), 1)).astype(F32)
    sel = (_iota((nh, 1, HD), 2) == _iota((nh, 1, HD), 0)).astype(F32)

    beta_all = jnp.where(valid, _sigmoid(bb), 0.0)
    g_all = jnp.where(valid, -jnp.exp(alog) * _softplus(aa + dtb), 0.0)
    gc_all = hnn(ltri, g_all)
    beta = jnp.sum(beta_all[None] * sel, axis=2, keepdims=True)
    gc = jnp.sum(gc_all[None] * sel, axis=2, keepdims=True)
    gc_rows = hbnt(jnp.broadcast_to(sel, (nh, CH, HD)), jnp.broadcast_to(gc_all[None], (nh, CH, HD)))
    last = _iota((1, CH, 1), 1) == CH - 1
    gc_last = jnp.sum(jnp.where(last, gc, 0.0), axis=1, keepdims=True)
    decay = jnp.exp(jnp.where(causal, gc - gc_rows, NEG))
    egc = jnp.exp(gc)

    kb = k * beta
    m = jnp.where(strict, bbnt(kb, k) * decay, 0.0)
    t_inv = _unit_lower_inv(m, bd, eye)
    u = bbnn(t_inv, v * beta)
    w = bbnn(t_inv, kb * egc)
    a_intra = bbnt(q, k) * decay
    q_dec = q * egc
    k_dec = k * jnp.exp(gc_last - gc)
    v_new = u - bbnn(w, s)
    o = bbnn(q_dec, s) + bbnn(a_intra, v_new)
    s_new = s * jnp.exp(gc_last) + bbtn(k_dec, v_new)
    return o, s_new


def _gdn_specs(nc, rev):
    cc = (lambda c: nc - 1 - c) if rev else (lambda c: c)
    wide = pl.BlockSpec((CH, GDN_H * HD), lambda c: (cc(c), 0))
    fix = lambda off: pl.BlockSpec((CH, HD), lambda c: (cc(c), off))
    par = pl.BlockSpec((1, HD), lambda c: (0, 0))
    state = pl.BlockSpec((1, GDN_H, HD, HD), lambda c: (cc(c), 0, 0, 0))
    return wide, fix, par, state


def _store_heads(ref, a):
    for h in range(a.shape[0]):
        ref[:, h * HD:(h + 1) * HD] = a[h]


def _gdn_fwd(qn, kn, vn, p, alog, dtb, pad, name):
    t_pad = qn.shape[0]
    nc = t_pad // CH
    wide, fix, par, state = _gdn_specs(nc, False)

    def body(q_ref, k_ref, v_ref, bb_ref, aa_ref, al_ref, dt_ref, o_ref, ss_ref, s_ref):
        c = pl.program_id(0)

        @pl.when(c == 0)
        def _():
            s_ref[...] = jnp.zeros_like(s_ref)

        s = s_ref[...]
        ss_ref[0] = s
        valid = (c * CH + _iota((CH, 1), 0)) >= pad
        o, s_new = _gdn_chunk(_heads(q_ref[...], GDN_H), _heads(k_ref[...], GDN_H), _heads(v_ref[...], GDN_H),
                              bb_ref[...], aa_ref[...], al_ref[...], dt_ref[...], s, valid)
        _store_heads(o_ref, o)
        s_ref[...] = s_new

    return pl.pallas_call(
        body, name=name, grid=(nc,),
        in_specs=[wide, wide, wide, fix(16), fix(17), par, par],
        out_specs=[wide, state],
        out_shape=[jax.ShapeDtypeStruct((t_pad, GDN_H * HD), F32), jax.ShapeDtypeStruct((nc, GDN_H, HD, HD), F32)],
        scratch_shapes=[pltpu.VMEM((GDN_H, HD, HD), F32)],
        compiler_params=_cparams(),
    )(qn, kn, vn, p, p, alog, dtb)


def _gdn_bwd(qn, kn, vn, p, alog, dtb, ssave, do, pad, name):
    t_pad = qn.shape[0]
    nc = t_pad // CH
    wide, fix, par, state = _gdn_specs(nc, True)

    def body(q_ref, k_ref, v_ref, bb_ref, aa_ref, al_ref, dt_ref, ss_ref, do_ref,
             dq_ref, dk_ref, dv_ref, dbb_ref, daa_ref, dal_ref, ddt_ref, ds_ref):
        c = pl.program_id(0)

        @pl.when(c == 0)
        def _():
            ds_ref[...] = jnp.zeros_like(ds_ref)
            dal_ref[...] = jnp.zeros_like(dal_ref)
            ddt_ref[...] = jnp.zeros_like(ddt_ref)

        valid = ((nc - 1 - c) * CH + _iota((CH, 1), 0)) >= pad
        fn = lambda q, k, v, bb, aa, al, dt, s: _gdn_chunk(q, k, v, bb, aa, al, dt, s, valid)
        _, vjp = jax.vjp(fn, _heads(q_ref[...], GDN_H), _heads(k_ref[...], GDN_H), _heads(v_ref[...], GDN_H),
                         bb_ref[...], aa_ref[...], al_ref[...], dt_ref[...], ss_ref[0])
        dq, dk, dv, dbb, daa, dal, ddt, ds = vjp((_heads(do_ref[...], GDN_H), ds_ref[...]))
        _store_heads(dq_ref, dq)
        _store_heads(dk_ref, dk)
        _store_heads(dv_ref, dv)
        dbb_ref[...] = dbb
        daa_ref[...] = daa
        dal_ref[...] += dal
        ddt_ref[...] += ddt
        ds_ref[...] = ds

    sds = jax.ShapeDtypeStruct
    return pl.pallas_call(
        body, name=name, grid=(nc,),
        in_specs=[wide, wide, wide, fix(16), fix(17), par, par, state, wide],
        out_specs=[wide, wide, wide, fix(0), fix(0), par, par],
        out_shape=[sds((t_pad, GDN_H * HD), F32)] * 3 + [sds((t_pad, HD), F32)] * 2 + [sds((1, HD), F32)] * 2,
        scratch_shapes=[pltpu.VMEM((GDN_H, HD, HD), F32)],
        compiler_params=_cparams(),
    )(qn, kn, vn, p, p, alog, dtb, ssave, do)


SB_Q0, SB_K0, SB_V0 = 18, 22, 26
SB_SCALE = SB_DH ** -0.5
SB_NB = 4


def _sb_terms(z, allowed):
    sp = jnp.log(1.0 + jnp.exp(-jnp.abs(z)))
    l1m = jnp.where(allowed, -jnp.maximum(z, 0.0) - sp, 0.0)
    ls = jnp.minimum(z, 0.0) - sp
    return l1m, ls


def _sb_stack(a, i):
    first = _iota((1, HD), 1) < SB_DH
    a2 = jnp.concatenate([jnp.where(first, a, 0.0), jnp.where(first, 0.0, a)], axis=0).astype(BF16)
    rq = i * QB + _iota((QB, 1), 0)
    return a2, jnp.concatenate([rq, rq], axis=0), first


def _dot_hi_lo(a, b2):
    hi = a.astype(BF16)
    lo = (a - hi.astype(F32)).astype(BF16)
    return lax.dot_general(jnp.concatenate([hi, lo], axis=1), b2, (NN, ((), ())), preferred_element_type=F32)


def _sb_fwd(p, pad, name):
    t_pad = p.shape[0]
    nq = t_pad // QB

    def body(q_ref, k_ref, v_ref, o_ref, r_ref):
        i = pl.program_id(1)
        q2, rowq, first = _sb_stack(q_ref[...] * SB_SCALE, i)
        tri = (_iota((QB, QB), 0) > _iota((QB, QB), 1)).astype(BF16)
        upper2 = jnp.concatenate([tri, tri], axis=0)

        def chain(kb, live):
            start = pl.multiple_of(kb * QB, QB)
            kblk = k_ref[pl.ds(start, QB), :].astype(BF16)
            vblk = v_ref[pl.ds(start, QB), :].astype(BF16)
            z = lax.dot_general(q2, kblk, (NT, ((), ())), preferred_element_type=F32)
            colk = kb * QB + _iota((1, QB), 1)
            allowed = (colk < rowq) & (colk >= pad) & live
            l1m, ls = _sb_terms(z, allowed)
            return allowed, ls, _dot_hi_lo(l1m, upper2), jnp.sum(l1m, axis=1, keepdims=True), vblk

        def step(j, carry):
            o_acc, run = carry
            ws, vs = [], []
            for n in range(SB_NB):
                kb = i - SB_NB * j - n
                allowed, ls, suf, rs, vblk = chain(jnp.maximum(kb, 0), kb >= 0)
                ws.append(jnp.where(allowed, jnp.exp(ls + suf + run), 0.0).astype(BF16))
                vs.append(vblk)
                run = run + rs
            o_acc = o_acc + lax.dot_general(jnp.concatenate(ws, axis=1), jnp.concatenate(vs, axis=0),
                                            (NN, ((), ())), preferred_element_type=F32)
            return o_acc, run

        o_acc, run = lax.fori_loop(0, (i + SB_NB) // SB_NB, step,
                                   (jnp.zeros((2 * QB, HD), F32), jnp.zeros((2 * QB, 1), F32)))
        o_ref[...] = jnp.where(first, o_acc[:QB], o_acc[QB:]).astype(BF16)
        r_ref[...] = jnp.where(first, run[:QB], run[QB:])

    full = lambda off: pl.BlockSpec((t_pad, HD), lambda pr, i: (0, off + pr))
    blk = pl.BlockSpec((QB, HD), lambda pr, i: (i, pr))
    return pl.pallas_call(
        body, name=name, grid=(SB_H // 2, nq),
        in_specs=[pl.BlockSpec((QB, HD), lambda pr, i: (i, SB_Q0 + pr)), full(SB_K0), full(SB_V0)],
        out_specs=[blk, blk],
        out_shape=[jax.ShapeDtypeStruct((t_pad, SB_H * SB_DH), BF16), jax.ShapeDtypeStruct((t_pad, SB_H * SB_DH), F32)],
        compiler_params=_cparams(),
    )(p, p, p)


def _sb_bwd(p, rtot, dy, dy_blk0, pad, name):
    t_pad = p.shape[0]
    nq = t_pad // QB

    def body(q_ref, k_ref, v_ref, r_ref, do_ref, dq_ref, dk_ref, dv_ref):
        i = pl.program_id(1)

        @pl.when(i == 0)
        def _():
            dk_ref[...] = jnp.zeros_like(dk_ref)
            dv_ref[...] = jnp.zeros_like(dv_ref)

        q2, rowq, first = _sb_stack(q_ref[...] * SB_SCALE, i)
        do2, _, _ = _sb_stack(do_ref[...], i)
        rt = r_ref[...]
        lane = _iota((1, HD), 1)
        rcol = jnp.concatenate([jnp.sum(jnp.where(lane == 0, rt, 0.0), axis=1, keepdims=True),
                                jnp.sum(jnp.where(lane == SB_DH, rt, 0.0), axis=1, keepdims=True)], axis=0)
        rj = _iota((QB, QB), 0)
        cs = _iota((QB, QB), 1)
        tri_u = (rj > cs).astype(BF16)
        tri_l = (rj < cs).astype(BF16)
        upper2 = jnp.concatenate([tri_u, tri_u], axis=0)
        lower2 = jnp.concatenate([tri_l, tri_l], axis=0)

        def chain(kb, live):
            start = pl.multiple_of(kb * QB, QB)
            kblk = k_ref[pl.ds(start, QB), :].astype(BF16)
            vblk = v_ref[pl.ds(start, QB), :].astype(BF16)
            z = lax.dot_general(q2, kblk, (NT, ((), ())), preferred_element_type=F32)
            colk = kb * QB + _iota((1, QB), 1)
            allowed = (colk < rowq) & (colk >= pad) & live
            l1m, ls = _sb_terms(z, allowed)
            dwgt = lax.dot_general(do2, vblk, (NT, ((), ())), preferred_element_type=F32)
            return (start, kblk, allowed, ls, _dot_hi_lo(l1m, upper2), jnp.sum(l1m, axis=1, keepdims=True), dwgt,
                    _sigmoid(z))

        def finish(c, seen, gseen):
            start, kblk, allowed, ls, suf, rs, dwgt, sg = c
            wgt = jnp.where(allowed, jnp.exp(ls + suf + (rcol - seen - rs)), 0.0)
            dl = dwgt * wgt
            gpre = gseen + _dot_hi_lo(dl, lower2)
            dz = jnp.where(allowed, dl * (1.0 - sg) - gpre * sg, 0.0).astype(BF16)
            dk_ref[pl.ds(start, QB), :] += lax.dot_general(dz, q2, (TN, ((), ())), preferred_element_type=F32)
            dv_ref[pl.ds(start, QB), :] += lax.dot_general(wgt.astype(BF16), do2, (TN, ((), ())),
                                                           preferred_element_type=F32)
            return dz, seen + rs, gseen + jnp.sum(dl, axis=1, keepdims=True)

        def step(j, carry):
            dq_acc, seen, gseen = carry
            cs_ = [chain(jnp.minimum(SB_NB * j + n, i), SB_NB * j + n <= i) for n in range(SB_NB)]
            dzs = []
            for c in cs_:
                dz, seen, gseen = finish(c, seen, gseen)
                dzs.append(dz)
            dq_acc = dq_acc + lax.dot_general(jnp.concatenate(dzs, axis=1), jnp.concatenate([c[1] for c in cs_], axis=0),
                                              (NN, ((), ())), preferred_element_type=F32)
            return dq_acc, seen, gseen

        zc = jnp.zeros((2 * QB, 1), F32)
        dq_acc, _, _ = lax.fori_loop(0, (i + SB_NB) // SB_NB, step, (jnp.zeros((2 * QB, HD), F32), zc, zc))
        dq_ref[...] = jnp.where(first, dq_acc[:QB], dq_acc[QB:]) * SB_SCALE

    full_in = lambda off: pl.BlockSpec((t_pad, HD), lambda pr, i: (0, off + pr))
    full_out = pl.BlockSpec((t_pad, HD), lambda pr, i: (0, pr))
    blk = pl.BlockSpec((QB, HD), lambda pr, i: (i, pr))
    sds = jax.ShapeDtypeStruct((t_pad, SB_H * SB_DH), F32)
    return pl.pallas_call(
        body, name=name, grid=(SB_H // 2, nq),
        in_specs=[pl.BlockSpec((QB, HD), lambda pr, i: (i, SB_Q0 + pr)), full_in(SB_K0), full_in(SB_V0), blk,
                  pl.BlockSpec((QB, HD), lambda pr, i: (i, dy_blk0 + pr))],
        out_specs=[blk, full_out, full_out],
        out_shape=[sds, sds, sds],
        compiler_params=_cparams(),
    )(p, p, p, rtot, dy)


HG_LEVELS = 6


def _hg_prefix_matrix():
    t = np.arange(CH)[:, None]
    j = np.arange(CH)[None, :]
    groups = [(j <= t)]
    for lvl in range(1, HG_LEVELS + 1):
        half = CH >> lvl
        e = (t // (2 * half)) * (2 * half) + half - 1
        groups.append(j <= e)
    groups.append(np.ones((2 * CH, CH), bool))
    return np.concatenate(groups, axis=0).astype(np.float32)


HG_G = 4


def _hg_chunk(qr, fr, iv, r0, r1, st, valid, ecat):
    g = st.shape[0]
    mx = jnp.maximum(r0, r1)
    e0 = jnp.exp(r0 - mx)
    e1 = jnp.exp(r1 - mx)
    lb = e1 / (e0 + e1)
    fg = lb + (1.0 - lb) * _sigmoid(fr)
    logf = jnp.where(valid, jnp.log(fg), 0.0)
    kk = jnp.where(valid, 1.0 - fg, 0.0)
    q = jnp.where(valid, _silu(qr), 0.0)
    v = _heads(jnp.where(valid, iv, 0.0), g)

    pre = hnn(ecat, logf)
    b = pre[0:CH]
    b_last = pre[(HG_LEVELS + 1) * CH:]
    row = _iota((CH, 1), 0)
    ri = _iota((1, CH, CH), 1)
    ci = _iota((1, CH, CH), 2)
    a = jnp.where(ri == ci, jnp.sum(_heads(q * kk, g), axis=2, keepdims=True), 0.0)
    for lvl in range(1, HG_LEVELS + 1):
        half = CH >> lvl
        m = pre[lvl * CH:(lvl + 1) * CH]
        low = (row & half) != 0
        qt = jnp.where(low, q * jnp.exp(jnp.where(low, b - m, 0.0)), 0.0)
        kt = jnp.where(low, 0.0, kk * jnp.exp(jnp.where(low, 0.0, m - b)))
        same = (ri >> (7 - lvl)) == (ci >> (7 - lvl))
        a = a + jnp.where(same, bbnt(_heads(qt, g), _heads(kt, g)), 0.0)
    o = bbnt(_heads(q * jnp.exp(b), g), st) + bbnn(a, v)
    kd = kk * jnp.exp(b_last[0:CH] - b)
    st_new = st * _heads(jnp.exp(b_last), g) + bbtn(v, _heads(kd, g))
    return o, st_new


def _hg_specs(nc, rev):
    cc = (lambda c: nc - 1 - c) if rev else (lambda c: c)
    ng = HG_H // HG_G
    blk = lambda off: pl.BlockSpec((CH, HG_G * HD), lambda h, c: (cc(c), off * ng + h))
    lbs = pl.BlockSpec((2, HG_G * HD), lambda h, c: (0, h))
    state = pl.BlockSpec((1, HG_G, HD, HD), lambda h, c: (cc(c), h, 0, 0))
    return ng, blk, lbs, state


def _hg_fwd(p, lbraw, ecat, pad, name):
    t_pad = p.shape[0]
    nc = t_pad // CH
    ng, blk, lbs, state = _hg_specs(nc, False)

    def body(q_ref, f_ref, i_ref, lb_ref, e_ref, o_ref, ss_ref, s_ref):
        c = pl.program_id(1)

        @pl.when(c == 0)
        def _():
            s_ref[...] = jnp.zeros_like(s_ref)

        st = s_ref[...]
        ss_ref[0] = st
        valid = (c * CH + _iota((CH, 1), 0)) >= pad
        o, st_new = _hg_chunk(q_ref[...], f_ref[...], i_ref[...], lb_ref[0:1, :], lb_ref[1:2, :], st, valid, e_ref[...])
        _store_heads(o_ref, o)
        s_ref[...] = st_new

    return pl.pallas_call(
        body, name=name, grid=(ng, nc),
        in_specs=[blk(0), blk(1), blk(2), lbs, pl.BlockSpec(ecat.shape, lambda h, c: (0, 0))],
        out_specs=[blk(0), state],
        out_shape=[jax.ShapeDtypeStruct((t_pad, HG_H * HD), F32), jax.ShapeDtypeStruct((nc, HG_H, HD, HD), F32)],
        scratch_shapes=[pltpu.VMEM((HG_G, HD, HD), F32)],
        compiler_params=_cparams(),
    )(p, p, p, lbraw, ecat)


def _hg_bwd(p, lbraw, ecat, ssave, do, pad, name):
    t_pad = p.shape[0]
    nc = t_pad // CH
    ng, blk, lbs, state = _hg_specs(nc, True)

    def body(q_ref, f_ref, i_ref, lb_ref, e_ref, ss_ref, do_ref, dq_ref, df_ref, di_ref, dlb_ref, ds_ref):
        c = pl.program_id(1)

        @pl.when(c == 0)
        def _():
            ds_ref[...] = jnp.zeros_like(ds_ref)
            dlb_ref[...] = jnp.zeros_like(dlb_ref)

        valid = ((nc - 1 - c) * CH + _iota((CH, 1), 0)) >= pad
        ecv = e_ref[...]
        fn = lambda qr, fr, iv, r0, r1, st: _hg_chunk(qr, fr, iv, r0, r1, st, valid, ecv)
        _, vjp = jax.vjp(fn, q_ref[...], f_ref[...], i_ref[...], lb_ref[0:1, :], lb_ref[1:2, :], ss_ref[0])
        dq, df, di, d0, d1, ds = vjp((_heads(do_ref[...], HG_G), ds_ref[...]))
        dq_ref[...] = dq
        df_ref[...] = df
        di_ref[...] = di
        dlb_ref[0:1, :] += d0
        dlb_ref[1:2, :] += d1
        ds_ref[...] = ds

    sds = jax.ShapeDtypeStruct((t_pad, HG_H * HD), F32)
    return pl.pallas_call(
        body, name=name, grid=(ng, nc),
        in_specs=[blk(0), blk(1), blk(2), lbs, pl.BlockSpec(ecat.shape, lambda h, c: (0, 0)), state, blk(0)],
        out_specs=[blk(0), blk(0), blk(0), lbs],
        out_shape=[sds, sds, sds, jax.ShapeDtypeStruct((2, HG_H * HD), F32)],
        scratch_shapes=[pltpu.VMEM((HG_G, HD, HD), F32)],
        compiler_params=_cparams(),
    )(p, p, p, lbraw, ecat, ssave, do)


def _pad_ab_cols(w):
    z = jnp.zeros((w.shape[0], HD - GDN_H), w.dtype)
    return jnp.concatenate([w[:, :2048], w[:, 2048:2052], z, w[:, 2052:2056], z, w[:, 2056:]], axis=1)


def _unpad_ab_cols(w):
    return jnp.concatenate([w[:, :2048], w[:, 2048:2052], w[:, 2176:2180], w[:, 2304:]], axis=1)


def _lane_pad(v):
    return jnp.pad(v, ((0, 0), (0, HD - v.shape[1])))


def _mlp_fwd(hb, w1, w2, layer):
    a, r = _mm(hb, w1, b_view=("cols", layer), out_dtype=BF16, act=True, name=f"mlp_up_{layer}")
    m = _mm(r, w2, b_view=("rows", layer), name=f"mlp_down_{layer}")
    return a, r, m


def _mlp_bwd(hb, a, r, dmb, w1, w2, layer):
    da = _mm(dmb, w2, tb=True, b_view=("rows", layer), out_dtype=BF16, gate=a, name=f"mlp_down_dx_{layer}")
    dw2 = _mm(r, dmb, ta=True, out_dtype=BF16, name=f"mlp_down_dw_{layer}")
    dh = _mm(da, w1, tb=True, b_view=("cols", layer), name=f"mlp_up_dx_{layer}")
    dw1 = _mm(hb, da, ta=True, out_dtype=BF16, out_split=N_CHIP, name=f"mlp_up_dw_{layer}")
    return dh, dw1, dw2


def _local_step(h0, tgt, w, pad):
    row = lambda a, i: a[i:i + 1]
    ecat = jnp.asarray(_hg_prefix_matrix())
    cw = [w["conv_w"][:, i * 512:(i + 1) * 512] for i in range(3)]
    alog, dtb = _lane_pad(w["a_log"]), _lane_pad(w["dt_bias"])

    h0b = h0.astype(BF16)
    p0 = _mm(h0b, w["ab_w_in"], name="ab_in")
    qn = _conv_fwd(p0, 0, cw[0], "q", pad, "conv_q")
    kn = _conv_fwd(p0, 4, cw[1], "k", pad, "conv_k")
    vn = _conv_fwd(p0, 8, cw[2], "v", pad, "conv_v")
    oa_raw, ss0 = _gdn_fwd(qn, kn, vn, p0, alog, dtb, pad, "gdn_fwd")
    oa = _grms_fwd(oa_raw, p0, 12, w["ab_gnorm_g"], "gdn_gate")
    ob, rtot = _sb_fwd(p0, pad, "sb_fwd")
    ycat = jnp.concatenate([oa, ob], axis=1)
    mix0 = _mm(ycat, w["ab_w_out"], name="ab_out")
    h1, h1b = _ln_res_fwd(h0, mix0, row(w["ln_mix_g"], 0), row(w["ln_mix_b"], 0), "ln_mix_0")
    a0, r0, m0 = _mlp_fwd(h1b, w["mlp_w1"], w["mlp_w2"], 0)
    h2, h2b = _ln_res_fwd(h1, m0, row(w["ln_ffn_g"], 0), row(w["ln_ffn_b"], 0), "ln_ffn_0")
    p1 = _mm(h2b, w["c_w_in"], b_view=("cols", 0), name="c_in")
    oc_raw, ss1 = _hg_fwd(p1, w["c_lb_raw"], ecat, pad, "hg_fwd")
    yc = _grms_fwd(oc_raw, p1, 3 * HG_H, w["c_gnorm_g"], "hg_gate")
    mix1 = _mm(yc, w["c_w_out"], name="c_out")
    h3, h3b = _ln_res_fwd(h2, mix1, row(w["ln_mix_g"], 1), row(w["ln_mix_b"], 1), "ln_mix_1")
    a1, r1, m1 = _mlp_fwd(h3b, w["mlp_w1"], w["mlp_w2"], 1)
    h4, _ = _ln_res_fwd(h3, m1, row(w["ln_ffn_g"], 1), row(w["ln_ffn_b"], 1), "ln_ffn_1")
    loss, dh4 = _loss_fwd(h4, tgt, pad + N_META, "loss")

    zero = jnp.zeros_like(dh4)
    dh3a, dm1b, dfg1, dfb1 = _ln_res_bwd(h3, m1, row(w["ln_ffn_g"], 1), row(w["ln_ffn_b"], 1), dh4, zero, "ln_ffn_bwd_1")
    dh3b, dw1_1, dw2_1 = _mlp_bwd(h3b, a1, r1, dm1b, w["mlp_w1"], w["mlp_w2"], 1)
    dh2a, dmix1b, dmg1, dmb1 = _ln_res_bwd(h2, mix1, row(w["ln_mix_g"], 1), row(w["ln_mix_b"], 1), dh3a, dh3b, "ln_mix_bwd_1")
    dyc = _mm(dmix1b, w["c_w_out"], tb=True, name="c_out_dx")
    dwco = _mm(yc, dmix1b, ta=True, out_dtype=BF16, name="c_out_dw")
    doc, dzc, dcg = _grms_bwd(oc_raw, p1, 3 * HG_H, w["c_gnorm_g"], dyc, 0, "hg_gate_bwd")
    dq1, df1, di1, dlb = _hg_bwd(p1, w["c_lb_raw"], ecat, ss1, doc, pad, "hg_bwd")
    dp1 = _assemble_bf16([(dq1, "cols"), (df1, "cols"), (di1, "cols"), (dzc, "cols")], "c_in_dy")
    dh2b = _mm(dp1, w["c_w_in"], tb=True, b_view=("cols", 0), name="c_in_dx")
    dwc = _mm(h2b, dp1, ta=True, out_dtype=BF16, out_split=N_CHIP, name="c_in_dw")
    dh1a, dm0b, dfg0, dfb0 = _ln_res_bwd(h1, m0, row(w["ln_ffn_g"], 0), row(w["ln_ffn_b"], 0), dh2a, dh2b, "ln_ffn_bwd_0")
    dh1b, dw1_0, dw2_0 = _mlp_bwd(h1b, a0, r0, dm0b, w["mlp_w1"], w["mlp_w2"], 0)
    dh0a, dmix0b, dmg0, dmb0 = _ln_res_bwd(h0, mix0, row(w["ln_mix_g"], 0), row(w["ln_mix_b"], 0), dh1a, dh1b, "ln_mix_bwd_0")
    dycat = _mm(dmix0b, w["ab_w_out"], tb=True, name="ab_out_dx")
    dwabo = _mm(ycat, dmix0b, ta=True, out_dtype=BF16, name="ab_out_dw")
    doa, dza, dag = _grms_bwd(oa_raw, p0, 12, w["ab_gnorm_g"], dycat, 0, "gdn_gate_bwd")
    dqn, dkn, dvn, dbb, daa, dal, ddt = _gdn_bwd(qn, kn, vn, p0, alog, dtb, ss0, doa, pad, "gdn_bwd")
    dpq, dcq = _conv_bwd(p0, 0, cw[0], dqn, "q", pad, "conv_q_bwd")
    dpk, dck = _conv_bwd(p0, 4, cw[1], dkn, "k", pad, "conv_k_bwd")
    dpv, dcv = _conv_bwd(p0, 8, cw[2], dvn, "v", pad, "conv_v_bwd")
    dqb, dkb, dvb = _sb_bwd(p0, rtot, dycat, 4, pad, "sb_bwd")
    dp0 = _assemble_bf16([(dpq, "cols"), (dpk, "cols"), (dpv, "cols"), (dza, "cols"), (dbb, "cols"), (daa, "cols"),
                          (dqb, "cols"), (dkb, "cols"), (dvb, "cols")], "ab_in_dy")
    dh0b = _mm(dp0, w["ab_w_in"], tb=True, name="ab_in_dx")
    dwab = _mm(h0b, dp0, ta=True, out_dtype=BF16, name="ab_in_dw")
    dh0 = _add2(dh0a, dh0b, "dh0")

    grads = {
        "ab_w_in": dwab, "conv_w": jnp.concatenate([dcq, dck, dcv], axis=1),
        "a_log": dal[:, :GDN_H], "dt_bias": ddt[:, :GDN_H],
        "ab_gnorm_g": dag, "ab_w_out": dwabo, "c_w_in": dwc, "c_lb_raw": dlb, "c_gnorm_g": dcg, "c_w_out": dwco,
        "ln_mix_g": jnp.concatenate([dmg0, dmg1], 0), "ln_mix_b": jnp.concatenate([dmb0, dmb1], 0),
        "w1_0": dw1_0, "w1_1": dw1_1, "w2_0": dw2_0, "w2_1": dw2_1,
        "ln_ffn_g": jnp.concatenate([dfg0, dfg1], 0), "ln_ffn_b": jnp.concatenate([dfb0, dfb1], 0),
    }
    return loss, dh0, grads


MESH = pl.DeviceIdType.MESH
ANY = pl.BlockSpec(memory_space=pl.ANY)
N_CHIP = 4
N_DEV = 8
CHIP_REL = ((1, 0), (0, 1), (1, 1))
DEV_REL = tuple((dx, dy, dc) for dx in (0, 1) for dy in (0, 1) for dc in (0, 1))[1:]
SWAP_CHUNKS = 4


def _pos():
    return lax.axis_index("x"), lax.axis_index("y"), lax.axis_index("c")


def _flip(a, d):
    return a + d - 2 * a * d


def _chip_allgather(bufs, name):
    n = len(bufs)

    def body(*refs):
        x_refs, o_refs = refs[:n], refs[n:2 * n]
        send_sems, recv_sems, local_sems = refs[2 * n:]
        x, y, c = _pos()
        local = [pltpu.make_async_copy(x_refs[a], o_refs[a].at[2 * x + y], local_sems.at[a]) for a in range(n)]
        for cp in local:
            cp.start()

        def copy(a, k, slot):
            tx, ty = _flip(x, CHIP_REL[k][0]), _flip(y, CHIP_REL[k][1])
            return pltpu.make_async_remote_copy(
                src_ref=x_refs[a], dst_ref=o_refs[a].at[slot(tx, ty)], send_sem=send_sems.at[3 * a + k],
                recv_sem=recv_sems.at[3 * a + k], device_id=(tx, ty, c), device_id_type=MESH)

        sends = [copy(a, k, lambda tx, ty: 2 * x + y) for a in range(n) for k in range(3)]
        for cp in sends:
            cp.start()
        for a in range(n):
            for k in range(3):
                copy(a, k, lambda tx, ty: 2 * tx + ty).wait_recv()
        for cp in sends:
            cp.wait_send()
        for cp in local:
            cp.wait()

    return pl.pallas_call(
        body, name=name, in_specs=[ANY] * n, out_specs=[ANY] * n,
        out_shape=[jax.ShapeDtypeStruct((N_CHIP,) + b.shape, b.dtype) for b in bufs],
        scratch_shapes=[pltpu.SemaphoreType.DMA((3 * n,)), pltpu.SemaphoreType.DMA((3 * n,)),
                        pltpu.SemaphoreType.DMA((n,))],
        compiler_params=pltpu.CompilerParams(has_side_effects=True),
    )(*bufs)


def _grad_alltoall(gs, name):
    n = len(gs)
    nr = N_DEV - 1

    def body(*refs):
        g_refs, o_refs = refs[:n], refs[n:2 * n]
        send_sems, recv_sems, local_sems = refs[2 * n:]
        x, y, c = _pos()
        me = 4 * x + 2 * y + c
        local = [pltpu.make_async_copy(g_refs[a].at[2 * x + y, c], o_refs[a].at[me], local_sems.at[a]) for a in range(n)]
        for cp in local:
            cp.start()

        def target(k):
            dx, dy, dc = DEV_REL[k]
            return _flip(x, dx), _flip(y, dy), _flip(c, dc)

        def copy(a, k, sending):
            tx, ty, tc = target(k)
            return pltpu.make_async_remote_copy(
                src_ref=g_refs[a].at[2 * tx + ty, tc],
                dst_ref=o_refs[a].at[me if sending else 4 * tx + 2 * ty + tc],
                send_sem=send_sems.at[nr * a + k], recv_sem=recv_sems.at[nr * a + k],
                device_id=(tx, ty, tc), device_id_type=MESH)

        sends = [copy(a, k, True) for a in range(n) for k in range(nr)]
        for cp in sends:
            cp.start()
        for a in range(n):
            for k in range(nr):
                copy(a, k, False).wait_recv()
        for cp in sends:
            cp.wait_send()
        for cp in local:
            cp.wait()

    return pl.pallas_call(
        body, name=name, in_specs=[ANY] * n, out_specs=[ANY] * n,
        out_shape=[jax.ShapeDtypeStruct((N_DEV,) + g.shape[2:], g.dtype) for g in gs],
        scratch_shapes=[pltpu.SemaphoreType.DMA((nr * n,)), pltpu.SemaphoreType.DMA((nr * n,)),
                        pltpu.SemaphoreType.DMA((n,))],
        compiler_params=pltpu.CompilerParams(has_side_effects=True),
    )(*gs)


def _sum_slots(r, name):
    n, rh, w = r.shape
    tr = _pick(rh, (256, 128, 64, 16))

    def body(r_ref, o_ref):
        acc = r_ref[0].astype(F32)
        for s in range(1, n):
            acc = acc + r_ref[s].astype(F32)
        o_ref[...] = acc

    return pl.pallas_call(
        body, name=name, grid=(rh // tr,), in_specs=[pl.BlockSpec((n, tr, w), lambda i: (0, i, 0))],
        out_specs=pl.BlockSpec((tr, w), lambda i: (i, 0)), out_shape=jax.ShapeDtypeStruct((rh, w), F32),
        compiler_params=_cparams(),
    )(r)


def _sibling_exchange(halves, dests, out_shapes, name):
    n = len(halves)
    no = len(out_shapes)

    def body(*refs):
        h_refs, o_refs = refs[:n], refs[n:n + no]
        send_sems, recv_sems, local_sems = refs[n + no:]
        x, y, c = _pos()

        def src(a, q):
            rows = halves[a].shape[0] // SWAP_CHUNKS
            return h_refs[a].at[pl.ds(q * rows, rows)]

        def slot(a, q, cc):
            o, lead = dests[a]
            rows = halves[a].shape[0] // SWAP_CHUNKS
            return o_refs[o].at[(*lead, cc, pl.ds(q * rows, rows))]

        def remote(a, q, cc):
            return pltpu.make_async_remote_copy(
                src_ref=src(a, q), dst_ref=slot(a, q, cc), send_sem=send_sems.at[SWAP_CHUNKS * a + q],
                recv_sem=recv_sems.at[SWAP_CHUNKS * a + q], device_id=(x, y, 1 - c), device_id_type=MESH)

        pairs = [(a, q) for a in range(n) for q in range(SWAP_CHUNKS)]
        local = [pltpu.make_async_copy(src(a, q), slot(a, q, c), local_sems.at[SWAP_CHUNKS * a + q]) for a, q in pairs]
        sends = [remote(a, q, c) for a, q in pairs]
        for cp in sends + local:
            cp.start()
        for a, q in pairs:
            remote(a, q, 1 - c).wait_recv()
        for cp in sends:
            cp.wait_send()
        for cp in local:
            cp.wait()

    ns = n * SWAP_CHUNKS
    return pl.pallas_call(
        body, name=name, in_specs=[ANY] * n, out_specs=[ANY] * no,
        out_shape=[jax.ShapeDtypeStruct(s, F32) for s in out_shapes],
        scratch_shapes=[pltpu.SemaphoreType.DMA((ns,)), pltpu.SemaphoreType.DMA((ns,)), pltpu.SemaphoreType.DMA((ns,))],
        compiler_params=pltpu.CompilerParams(has_side_effects=True),
    )(*halves)


def _small_allreduce(buf, name):
    r, w = buf.shape

    def body(b_ref, o_ref, land_ref, send_sems, recv_sems):
        x, y, c = _pos()
        me = 4 * x + 2 * y + c
        land_ref[me] = b_ref[...]

        def target(k):
            dx, dy, dc = DEV_REL[k]
            return _flip(x, dx), _flip(y, dy), _flip(c, dc)

        sends = []
        for k in range(N_DEV - 1):
            tx, ty, tc = target(k)
            cp = pltpu.make_async_remote_copy(
                src_ref=b_ref, dst_ref=land_ref.at[me], send_sem=send_sems.at[k], recv_sem=recv_sems.at[k],
                device_id=(tx, ty, tc), device_id_type=MESH)
            cp.start()
            sends.append(cp)
        for k in range(N_DEV - 1):
            tx, ty, tc = target(k)
            pltpu.make_async_remote_copy(
                src_ref=b_ref, dst_ref=land_ref.at[4 * tx + 2 * ty + tc], send_sem=send_sems.at[k],
                recv_sem=recv_sems.at[k], device_id=(tx, ty, tc), device_id_type=MESH).wait_recv()
        for cp in sends:
            cp.wait_send()
        acc = land_ref[0]
        for s in range(1, N_DEV):
            acc = acc + land_ref[s]
        o_ref[...] = acc

    vm = pl.BlockSpec(memory_space=pltpu.VMEM)
    return pl.pallas_call(
        body, name=name, in_specs=[vm], out_specs=vm, out_shape=jax.ShapeDtypeStruct((r, w), F32),
        scratch_shapes=[pltpu.VMEM((N_DEV, r, w), F32), pltpu.SemaphoreType.DMA((N_DEV - 1,)),
                        pltpu.SemaphoreType.DMA((N_DEV - 1,))],
        compiler_params=pltpu.CompilerParams(has_side_effects=True),
    )(buf)


def _adamw(w, g, m, v, name):
    r, c = w.shape
    tr = _pick(r, (256, 128, 64, 8)) if r * c > (1 << 18) else r

    def body(w_ref, g_ref, m_ref, v_ref, d_ref, m2_ref, v2_ref):
        gg = g_ref[...]
        m2 = ADAM_B1 * m_ref[...] + (1.0 - ADAM_B1) * gg
        v2 = ADAM_B2 * v_ref[...] + (1.0 - ADAM_B2) * (gg * gg)
        m_hat = m2 / (1.0 - ADAM_B1 ** ADAM_STEP)
        v_hat = v2 / (1.0 - ADAM_B2 ** ADAM_STEP)
        d_ref[...] = -ADAM_LR * (m_hat / (jnp.sqrt(v_hat) + ADAM_EPS) + ADAM_WD * w_ref[...])
        m2_ref[...] = m2
        v2_ref[...] = v2

    blk = pl.BlockSpec((tr, c), lambda i: (i, 0))
    sds = jax.ShapeDtypeStruct((r, c), F32)
    return pl.pallas_call(body, name=name, grid=(r // tr,), in_specs=[blk] * 4, out_specs=[blk] * 3,
                          out_shape=[sds] * 3, compiler_params=_cparams())(w, g, m, v)


BIG = ("ab_w_in", "ab_w_out", "c_w_in", "c_w_out", "mlp_w1", "mlp_w2")
SMALL = ("ln_mix_g", "ln_mix_b", "ln_ffn_g", "ln_ffn_b", "c_lb_raw", "ab_a_log", "ab_dt_bias", "ab_gnorm_g", "c_gnorm_g")
SMALL_ROWS = 16
CONV_ROWS = 8
CONV_W = 3 * GDN_H * HD


def _conv_to_rows(cw):
    return jnp.pad(cw, ((0, 0), (0, 2 * D - CONV_W))).reshape(CONV_ROWS, D)


def _rows_to_conv(rows):
    return rows.reshape(CONV_K, 2 * D)[:, :CONV_W]


def _pack_small(d):
    rows = [jnp.pad(d[n], ((0, 0), (0, D - d[n].shape[1]))) for n in SMALL]
    buf = jnp.concatenate(rows, axis=0)
    return jnp.pad(buf, ((0, SMALL_ROWS - buf.shape[0]), (0, 0)))


def _unpack_small(buf, like):
    out, r = {}, 0
    for n in SMALL:
        nr, nc = like[n].shape
        out[n] = buf[r:r + nr, :nc]
        r += nr
    return out


def kernel(x, meta_tokens, ab_w_in, ab_conv_w, ab_a_log, ab_dt_bias, ab_gnorm_g, ab_w_out, c_w_in, c_lb_raw, c_gnorm_g, c_w_out, ln_mix_g, ln_mix_b, mlp_w1, mlp_w2, ln_ffn_g, ln_ffn_b, loss_target, m_meta_tokens, m_ab_w_in, m_ab_conv_w, m_ab_a_log, m_ab_dt_bias, m_ab_gnorm_g, m_ab_w_out, m_c_w_in, m_c_lb_raw, m_c_gnorm_g, m_c_w_out, m_ln_mix_g, m_ln_mix_b, m_mlp_w1, m_mlp_w2, m_ln_ffn_g, m_ln_ffn_b, v_meta_tokens, v_ab_w_in, v_ab_conv_w, v_ab_a_log, v_ab_dt_bias, v_ab_gnorm_g, v_ab_w_out, v_c_w_in, v_c_lb_raw, v_c_gnorm_g, v_c_w_out, v_ln_mix_g, v_ln_mix_b, v_mlp_w1, v_mlp_w2, v_ln_ffn_g, v_ln_ffn_b):
    names = ("meta_tokens", "ab_w_in", "ab_conv_w", "ab_a_log", "ab_dt_bias", "ab_gnorm_g", "ab_w_out", "c_w_in",
             "c_lb_raw", "c_gnorm_g", "c_w_out", "ln_mix_g", "ln_mix_b", "mlp_w1", "mlp_w2", "ln_ffn_g", "ln_ffn_b")
    wts = dict(zip(names, (meta_tokens, ab_w_in, ab_conv_w, ab_a_log, ab_dt_bias, ab_gnorm_g, ab_w_out, c_w_in, c_lb_raw,
                           c_gnorm_g, c_w_out, ln_mix_g, ln_mix_b, mlp_w1, mlp_w2, ln_ffn_g, ln_ffn_b)))
    mom_m = dict(zip(names, (m_meta_tokens, m_ab_w_in, m_ab_conv_w, m_ab_a_log, m_ab_dt_bias, m_ab_gnorm_g, m_ab_w_out,
                             m_c_w_in, m_c_lb_raw, m_c_gnorm_g, m_c_w_out, m_ln_mix_g, m_ln_mix_b, m_mlp_w1, m_mlp_w2,
                             m_ln_ffn_g, m_ln_ffn_b)))
    mom_v = dict(zip(names, (v_meta_tokens, v_ab_w_in, v_ab_conv_w, v_ab_a_log, v_ab_dt_bias, v_ab_gnorm_g, v_ab_w_out,
                             v_c_w_in, v_c_lb_raw, v_c_gnorm_g, v_c_w_out, v_ln_mix_g, v_ln_mix_b, v_mlp_w1, v_mlp_w2,
                             v_ln_ffn_g, v_ln_ffn_b)))
    seq = x.shape[1]
    pad = (-(N_META + seq)) % QB
    xi, yi, ci = _pos()
    chip = 2 * xi + yi

    gat = dict(zip(BIG, _chip_allgather(
        [ab_w_in[0].astype(BF16), ab_w_out[0].astype(BF16), c_w_in.astype(BF16), c_w_out[0].astype(BF16),
         mlp_w1.astype(BF16), mlp_w2.astype(BF16)], "gather_weights")))
    mcols, ccols = meta_tokens.shape[1], ab_conv_w.shape[2]
    place = jnp.concatenate([
        lax.dynamic_update_slice(jnp.zeros((N_META, D), F32), 0.5 * meta_tokens, (0, chip * mcols)),
        _conv_to_rows(lax.dynamic_update_slice(jnp.zeros((CONV_K, CONV_W), F32), 0.5 * ab_conv_w[0], (0, chip * ccols)))],
        axis=0)
    placed = _small_allreduce(place, "gather_meta")
    meta_full = placed[:N_META]

    w = {
        "ab_w_in": _pad_ab_cols(jnp.transpose(gat["ab_w_in"], (1, 0, 2)).reshape(D, AB_TRUE)),
        "conv_w": _rows_to_conv(placed[N_META:]), "a_log": ab_a_log, "dt_bias": ab_dt_bias,
        "ab_gnorm_g": ab_gnorm_g, "ab_w_out": gat["ab_w_out"].reshape(D, D), "c_w_in": gat["c_w_in"], "c_lb_raw": c_lb_raw,
        "c_gnorm_g": c_gnorm_g, "c_w_out": gat["c_w_out"].reshape(D, D), "ln_mix_g": ln_mix_g, "ln_mix_b": ln_mix_b,
        "ln_ffn_g": ln_ffn_g, "ln_ffn_b": ln_ffn_b, "mlp_w1": gat["mlp_w1"], "mlp_w2": gat["mlp_w2"],
    }

    h0 = jnp.concatenate([jnp.zeros((pad, D), F32), meta_full, x[0]], axis=0)
    tgt = jnp.concatenate([jnp.zeros((pad + N_META, D), F32), loss_target[0]], axis=0)
    loss8, dh0, g = _local_step(h0, tgt, w, pad)
    loss = lax.psum(loss8[0, 0], ("x", "y", "c"))
    grad_x = dh0[pad + N_META:][None]

    gsmall = {"ln_mix_g": g["ln_mix_g"], "ln_mix_b": g["ln_mix_b"], "ln_ffn_g": g["ln_ffn_g"], "ln_ffn_b": g["ln_ffn_b"],
              "c_lb_raw": g["c_lb_raw"], "ab_a_log": g["a_log"], "ab_dt_bias": g["dt_bias"], "ab_gnorm_g": g["ab_gnorm_g"],
              "c_gnorm_g": g["c_gnorm_g"]}
    sbuf = jnp.concatenate([_pack_small(gsmall), dh0[pad:pad + N_META], _conv_to_rows(g["conv_w"])], axis=0)
    ssum = _small_allreduce(sbuf, "allreduce_small")
    grads = _unpack_small(ssum[:SMALL_ROWS], wts)
    grads["meta_tokens"] = lax.dynamic_slice(ssum[SMALL_ROWS:SMALL_ROWS + N_META], (0, chip * mcols), (N_META, mcols))
    grads["ab_conv_w"] = lax.dynamic_slice(_rows_to_conv(ssum[SMALL_ROWS + N_META:]), (0, chip * ccols), (CONV_K, ccols))[None]

    dab = _unpad_ab_cols(g["ab_w_in"])
    items = [jnp.transpose(dab.reshape(D, N_CHIP, AB_TRUE // N_CHIP), (1, 0, 2)), g["ab_w_out"].reshape(N_CHIP, -1, D),
             g["c_w_in"], g["c_w_out"].reshape(N_CHIP, -1, D), g["w1_0"], g["w1_1"],
             g["w2_0"].reshape(N_CHIP, -1, D), g["w2_1"].reshape(N_CHIP, -1, D)]
    landed = _grad_alltoall([a.reshape(N_CHIP, 2, a.shape[1] // 2, a.shape[2]) for a in items], "grad_alltoall")
    halves = [_sum_slots(l, f"grad_sum_{i}") for i, l in enumerate(landed)]
    dests = [(0, ()), (1, ()), (2, ()), (3, ()), (4, (0,)), (4, (1,)), (5, (0,)), (5, (1,))]
    out_shapes = [(2,) + halves[i].shape for i in range(4)] + [(2, 2) + halves[4].shape, (2, 2) + halves[6].shape]
    for n, o in zip(BIG, _sibling_exchange(halves, dests, out_shapes, "grad_swap")):
        grads[n] = o.reshape(wts[n].shape)

    delta, new_m, new_v = {}, {}, {}
    for n in ("meta_tokens", "ab_conv_w") + BIG:
        shp = wts[n].shape
        to2 = lambda a: a.reshape(-1, shp[-1])
        d2, m2, v2 = _adamw(to2(wts[n]), to2(grads[n]), to2(mom_m[n]), to2(mom_v[n]), f"adamw_{n}")
        delta[n], new_m[n], new_v[n] = d2.reshape(shp), m2.reshape(shp), v2.reshape(shp)
    d2, m2, v2 = _adamw(_pack_small(wts), ssum[:SMALL_ROWS], _pack_small(mom_m), _pack_small(mom_v), "adamw_small")
    delta.update(_unpack_small(d2, wts))
    new_m.update(_unpack_small(m2, wts))
    new_v.update(_unpack_small(v2, wts))

    return (loss, grad_x, *[grads[n] for n in names], *[delta[n] for n in names], *[new_m[n] for n in names],
            *[new_v[n] for n in names])
```

```python
import functools
import math

import numpy as np
import jax
import jax.numpy as jnp
from jax import lax
from jax.experimental import pallas as pl
from jax.experimental.pallas import tpu as pltpu

F32 = jnp.float32
BF16 = jnp.bfloat16

D = 1024
N_META = 16
D_FF = 4 * D
DEPTH = 2
GDN_H = 4
SB_H = 8
SB_DH = 64
HG_H = 8
HD = 128
CH = 64
QB = 128
ALPHA = float((2 * DEPTH) ** 0.25)
LN_EPS = 1e-5
RMS_EPS = 1e-6
L2_EPS = 1e-6
NEG = -1e30

ADAM_LR = 0.001
ADAM_B1 = 0.9
ADAM_B2 = 0.999
ADAM_EPS = 1e-08
ADAM_WD = 0.01
ADAM_STEP = 10

AB_W = 30 * HD
AB_TRUE = 3592
VMEM_LIMIT = 56 * 1024 * 1024

NN = ((1,), (0,))
NT = ((1,), (1,))
TN = ((0,), (0,))


def _cparams(**kw):
    return pltpu.CompilerParams(vmem_limit_bytes=VMEM_LIMIT, **kw)


def _dg(a, b, dims, mode):
    if mode == "h":
        return lax.dot_general(a, b, dims, precision=lax.Precision.HIGHEST, preferred_element_type=F32)
    if mode == "b":
        return lax.dot_general(a.astype(BF16), b.astype(BF16), dims, preferred_element_type=F32)
    ah, bh = a.astype(BF16), b.astype(BF16)
    al, bl = (a - ah.astype(F32)).astype(BF16), (b - bh.astype(F32)).astype(BF16)
    d = lambda x, y: lax.dot_general(x, y, dims, preferred_element_type=F32)
    return d(ah, bh) + (d(ah, bl) + d(al, bh))


def _make_dots(mode, batched=False):
    if batched:
        nn_d, nt_d, tn_d = (((2,), (1,)), ((0,), (0,))), (((2,), (2,)), ((0,), (0,))), (((1,), (1,)), ((0,), (0,)))
    else:
        nn_d, nt_d, tn_d = (NN, ((), ())), (NT, ((), ())), (TN, ((), ()))

    @jax.custom_vjp
    def nn(a, b):
        return _dg(a, b, nn_d, mode)

    @jax.custom_vjp
    def nt(a, b):
        return _dg(a, b, nt_d, mode)

    @jax.custom_vjp
    def tn(a, b):
        return _dg(a, b, tn_d, mode)

    nn.defvjp(lambda a, b: (nn(a, b), (a, b)), lambda r, g: (nt(g, r[1]), tn(r[0], g)))
    nt.defvjp(lambda a, b: (nt(a, b), (a, b)), lambda r, g: (nn(g, r[1]), tn(g, r[0])))
    tn.defvjp(lambda a, b: (tn(a, b), (a, b)), lambda r, g: (nt(r[1], g), nn(r[0], g)))
    return nn, nt, tn


hnn, hnt, htn = _make_dots("h")
bbnn, bbnt, bbtn = _make_dots("b", True)
mbnn, mbnt, mbtn = _make_dots("m", True)
hbnn, hbnt, hbtn = _make_dots("h", True)


def _heads(a, n):
    return jnp.stack([a[:, h * HD:(h + 1) * HD] for h in range(n)])


def _sigmoid(x):
    return jax.nn.sigmoid(x)


def _silu(x):
    return x * jax.nn.sigmoid(x)


def _softplus(x):
    return jnp.maximum(x, 0.0) + jnp.log(1.0 + jnp.exp(-jnp.abs(x)))


def _iota(shape, dim):
    return lax.broadcasted_iota(jnp.int32, shape, dim)


def _pick(n, prefs):
    for p in prefs:
        if n % p == 0:
            return p
    return n


def _mm(a, b, *, ta=False, tb=False, out_dtype=F32, name, b_view=None, out_split=0, act=False, gate=None):
    if ta:
        k_dim, m_dim = a.shape
    else:
        m_dim, k_dim = a.shape
    if b_view is None:
        w_rows, w_cols = b.shape
    else:
        kind, layer = b_view
        nj, _, blk_r, blk_c = b.shape
        w_rows, w_cols = (blk_r, nj * blk_c) if kind == "cols" else (nj * blk_r, blk_c)
    n_dim = w_rows if tb else w_cols
    assert (w_cols if tb else w_rows) == k_dim
    tm = _pick(m_dim, (1024, 1056, 704, 640, 512, 384, 256, 128))
    tn = _pick(n_dim, (1024, 1056, 704, 640, 512, 384, 256, 128))
    tk = _pick(k_dim, (1024, 1056, 704, 512, 384, 256, 128))
    nk = k_dim // tk
    a_spec = pl.BlockSpec((tk, tm), lambda i, j, k: (k, i)) if ta else pl.BlockSpec((tm, tk), lambda i, j, k: (i, k))
    wb = (tn, tk) if tb else (tk, tn)
    w_idx = (lambda i, j, k: (j, k)) if tb else (lambda i, j, k: (k, j))
    if b_view is None:
        b_spec = pl.BlockSpec(wb, w_idx)
    elif kind == "cols":
        per = blk_c // wb[1]
        b_spec = pl.BlockSpec((None, None) + wb,
                              lambda i, j, k: (w_idx(i, j, k)[1] // per, layer, w_idx(i, j, k)[0], w_idx(i, j, k)[1] % per))
    else:
        per = blk_r // wb[0]
        b_spec = pl.BlockSpec((None, None) + wb,
                              lambda i, j, k: (w_idx(i, j, k)[0] // per, layer, w_idx(i, j, k)[0] % per, w_idx(i, j, k)[1]))
    if out_split:
        per_o = (n_dim // out_split) // tn
        out_spec = pl.BlockSpec((None, tm, tn), lambda i, j, k: (j // per_o, i, j % per_o))
        out_sds = jax.ShapeDtypeStruct((out_split, m_dim, n_dim // out_split), out_dtype)
    else:
        out_spec = pl.BlockSpec((tm, tn), lambda i, j, k: (i, j))
        out_sds = jax.ShapeDtypeStruct((m_dim, n_dim), out_dtype)
    dims = (((0 if ta else 1,), (1 if tb else 0,)), ((), ()))
    extra = [] if gate is None else [gate]
    n_out = 2 if act else 1

    def finish(acc, refs):
        if act:
            refs[0][...] = acc.astype(refs[0].dtype)
            r = jnp.maximum(acc, 0.0)
            refs[1][...] = (r * r).astype(refs[1].dtype)
        elif gate is not None:
            refs[1][...] = (acc * (2.0 * jnp.maximum(refs[0][...].astype(F32), 0.0))).astype(refs[1].dtype)
        else:
            refs[0][...] = acc.astype(refs[0].dtype)

    def body(a_ref, b_ref, *rest):
        refs, acc_ref = rest[:-1], rest[-1]
        part = lax.dot_general(a_ref[...], b_ref[...], dims, preferred_element_type=F32)
        if nk == 1:
            finish(part, refs)
        else:
            k = pl.program_id(2)

            @pl.when(k == 0)
            def _():
                acc_ref[...] = part

            @pl.when(k > 0)
            def _():
                acc_ref[...] += part

            @pl.when(k == nk - 1)
            def _():
                finish(acc_ref[...], refs)

    out = pl.pallas_call(
        body, name=name, grid=(m_dim // tm, n_dim // tn, nk),
        in_specs=[a_spec, b_spec] + [pl.BlockSpec((tm, tn), lambda i, j, k: (i, j))] * len(extra),
        out_specs=[out_spec] * n_out,
        out_shape=[out_sds] * n_out,
        scratch_shapes=[pltpu.VMEM((tm, tn) if nk > 1 else (8, 128), F32)],
        compiler_params=_cparams(dimension_semantics=("parallel", "parallel", "arbitrary")),
    )(a, b, *extra)
    return out if act else out[0]


def _row_tile(t_pad, width):
    for tr in (528, 352, 176, 128, 64):
        if t_pad % tr == 0 and tr * width * 4 <= (3 << 19) and tr % 16 == 0:
            return tr
    return 64 if t_pad % 64 == 0 else t_pad


def _ln_res_fn(h, m, g, b):
    x = ALPHA * h + m
    mu = jnp.mean(x, axis=-1, keepdims=True)
    xc = x - mu
    var = jnp.mean(xc * xc, axis=-1, keepdims=True)
    return xc * lax.rsqrt(var + LN_EPS) * g + b


def _ln_res_fwd(h, m, g, b, name):
    t_pad = h.shape[0]
    tr = _row_tile(t_pad, D)

    def body(h_ref, m_ref, g_ref, b_ref, y_ref, yb_ref):
        y = _ln_res_fn(h_ref[...], m_ref[...], g_ref[...], b_ref[...])
        y_ref[...] = y
        yb_ref[...] = y.astype(BF16)

    row = pl.BlockSpec((tr, D), lambda i: (i, 0))
    par = pl.BlockSpec((1, D), lambda i: (0, 0))
    return pl.pallas_call(
        body, name=name, grid=(t_pad // tr,), in_specs=[row, row, par, par], out_specs=[row, row],
        out_shape=[jax.ShapeDtypeStruct((t_pad, D), F32), jax.ShapeDtypeStruct((t_pad, D), BF16)],
        compiler_params=_cparams(),
    )(h, m, g, b)


def _ln_res_bwd(h, m, g, b, dy1, dy2, name):
    t_pad = h.shape[0]
    tr = _row_tile(t_pad, D)

    def body(h_ref, m_ref, g_ref, b_ref, d1_ref, d2_ref, dh_ref, dm_ref, dg_ref, db_ref):
        _, vjp = jax.vjp(_ln_res_fn, h_ref[...], m_ref[...], g_ref[...], b_ref[...])
        dh, dm, dg, db = vjp(d1_ref[...] + d2_ref[...])
        dh_ref[...] = dh
        dm_ref[...] = dm.astype(BF16)

        @pl.when(pl.program_id(0) == 0)
        def _():
            dg_ref[...] = jnp.zeros_like(dg_ref)
            db_ref[...] = jnp.zeros_like(db_ref)

        dg_ref[...] += dg
        db_ref[...] += db

    row = pl.BlockSpec((tr, D), lambda i: (i, 0))
    par = pl.BlockSpec((1, D), lambda i: (0, 0))
    return pl.pallas_call(
        body, name=name, grid=(t_pad // tr,), in_specs=[row, row, par, par, row, row],
        out_specs=[row, row, par, par],
        out_shape=[jax.ShapeDtypeStruct((t_pad, D), F32), jax.ShapeDtypeStruct((t_pad, D), BF16),
                   jax.ShapeDtypeStruct((1, D), F32), jax.ShapeDtypeStruct((1, D), F32)],
        compiler_params=_cparams(),
    )(h, m, g, b, dy1, dy2)


def _grms_fn(o, z, g):
    y = o * lax.rsqrt(jnp.mean(o * o, axis=-1, keepdims=True) + RMS_EPS) * g
    return y * _silu(z)


def _grms_fwd(o, z_arr, z_blk0, g, name):
    t_pad, w = o.shape
    nh = w // HD
    tr = _row_tile(t_pad, HD * 4)

    def body(o_ref, z_ref, g_ref, y_ref):
        y_ref[...] = _grms_fn(o_ref[...], z_ref[...], g_ref[...]).astype(BF16)

    return pl.pallas_call(
        body, name=name, grid=(t_pad // tr, nh),
        in_specs=[pl.BlockSpec((tr, HD), lambda i, h: (i, h)),
                  pl.BlockSpec((tr, HD), lambda i, h: (i, z_blk0 + h)),
                  pl.BlockSpec((1, HD), lambda i, h: (0, 0))],
        out_specs=pl.BlockSpec((tr, HD), lambda i, h: (i, h)),
        out_shape=jax.ShapeDtypeStruct((t_pad, w), BF16), compiler_params=_cparams(),
    )(o, z_arr, g)


def _grms_bwd(o, z_arr, z_blk0, g, dy_arr, dy_blk0, name):
    t_pad, w = o.shape
    nh = w // HD
    tr = _row_tile(t_pad, HD * 4)

    def body(o_ref, z_ref, g_ref, dy_ref, do_ref, dz_ref, dg_ref):
        _, vjp = jax.vjp(_grms_fn, o_ref[...], z_ref[...], g_ref[...])
        do, dz, dg = vjp(dy_ref[...])
        do_ref[...] = do
        dz_ref[...] = dz

        @pl.when((pl.program_id(0) == 0) & (pl.program_id(1) == 0))
        def _():
            dg_ref[...] = jnp.zeros_like(dg_ref)

        dg_ref[...] += dg

    blk = pl.BlockSpec((tr, HD), lambda i, h: (i, h))
    return pl.pallas_call(
        body, name=name, grid=(t_pad // tr, nh),
        in_specs=[blk, pl.BlockSpec((tr, HD), lambda i, h: (i, z_blk0 + h)),
                  pl.BlockSpec((1, HD), lambda i, h: (0, 0)),
                  pl.BlockSpec((tr, HD), lambda i, h: (i, dy_blk0 + h))],
        out_specs=[blk, blk, pl.BlockSpec((1, HD), lambda i, h: (0, 0))],
        out_shape=[jax.ShapeDtypeStruct((t_pad, w), F32), jax.ShapeDtypeStruct((t_pad, w), F32),
                   jax.ShapeDtypeStruct((1, HD), F32)],
        compiler_params=_cparams(),
    )(o, z_arr, g, dy_arr)


def _loss_fwd(y, tgt, first_row, name):
    t_pad = y.shape[0]
    tr = _row_tile(t_pad, D)

    def body(y_ref, t_ref, l_ref, dy_ref):
        rows = pl.program_id(0) * tr + _iota((tr, 1), 0)
        err = jnp.where(rows >= first_row, y_ref[...] - t_ref[...], 0.0)
        dy_ref[...] = err * (1.0 / D)

        @pl.when(pl.program_id(0) == 0)
        def _():
            l_ref[...] = jnp.zeros_like(l_ref)

        part = jnp.sum(jnp.sum(err * err, axis=1, keepdims=True), axis=0, keepdims=True)
        l_ref[...] += jnp.broadcast_to(part * (0.5 / D), l_ref.shape)

    row = pl.BlockSpec((tr, D), lambda i: (i, 0))
    return pl.pallas_call(
        body, name=name, grid=(t_pad // tr,), in_specs=[row, row],
        out_specs=[pl.BlockSpec((8, 128), lambda i: (0, 0)), row],
        out_shape=[jax.ShapeDtypeStruct((8, 128), F32), jax.ShapeDtypeStruct((t_pad, D), F32)],
        compiler_params=_cparams(),
    )(y, tgt)


def _add2(a, b, name):
    t_pad, w = a.shape
    tr = _row_tile(t_pad, w)

    def body(a_ref, b_ref, o_ref):
        o_ref[...] = a_ref[...] + b_ref[...]

    row = pl.BlockSpec((tr, w), lambda i: (i, 0))
    return pl.pallas_call(body, name=name, grid=(t_pad // tr,), in_specs=[row, row], out_specs=row,
                          out_shape=jax.ShapeDtypeStruct((t_pad, w), F32), compiler_params=_cparams())(a, b)


def _assemble_bf16(parts, name):
    t_pad = parts[0][0].shape[0] if parts[0][1] == "cols" else parts[0][0].shape[1]
    widths = [p.shape[1] if kind == "cols" else HD for p, kind in parts]
    total = sum(widths)
    tr = _row_tile(t_pad, total)

    def body(*refs):
        o_ref = refs[-1]
        off = 0
        for ref, (p, kind), w in zip(refs[:-1], parts, widths):
            if kind == "cols":
                o_ref[:, off:off + w] = ref[...].astype(BF16)
            else:
                acc = ref[0]
                for hh in range(1, p.shape[0]):
                    acc = acc + ref[hh]
                o_ref[:, off:off + w] = acc.astype(BF16)
            off += w

    specs = []
    for p, kind in parts:
        if kind == "cols":
            specs.append(pl.BlockSpec((tr, p.shape[1]), lambda i: (i, 0)))
        else:
            specs.append(pl.BlockSpec((p.shape[0], tr, HD), lambda i: (0, i, 0)))
    return pl.pallas_call(
        body, name=name, grid=(t_pad // tr,), in_specs=specs,
        out_specs=pl.BlockSpec((tr, total), lambda i: (i, 0)),
        out_shape=jax.ShapeDtypeStruct((t_pad, total), BF16), compiler_params=_cparams(),
    )(*[p for p, _ in parts])


CONV_K = 4
HALO = 8
RT = 128


def _conv_fwd(p, blk0, w, mode, pad, name):
    t_pad = p.shape[0]
    nt = t_pad // RT
    scale = HD ** -0.5 if mode == "q" else 1.0

    def body(x_ref, w_ref, y_ref, xs_ref):
        xs_ref[0:HALO, :] = jnp.zeros((HALO, HD), F32)
        rows = _iota((t_pad, 1), 0)
        xs_ref[HALO:HALO + t_pad, :] = jnp.where(rows >= pad, x_ref[...], 0.0)
        wv = w_ref[...]

        def tile(i, carry):
            r0 = pl.multiple_of(i * RT, RT)
            ext = xs_ref[pl.ds(r0, RT + HALO), :]
            acc = ext[HALO:, :] * wv[3:4, :]
            for s in (1, 2, 3):
                acc = acc + pltpu.roll(ext, s, 0)[HALO:, :] * wv[3 - s:4 - s, :]
            y = _silu(acc)
            if mode != "v":
                y = y * lax.rsqrt(jnp.sum(y * y, axis=-1, keepdims=True) + L2_EPS) * scale
            y_ref[pl.ds(r0, RT), :] = y
            return carry

        lax.fori_loop(0, nt, tile, 0)

    return pl.pallas_call(
        body, name=name, grid=(GDN_H,),
        in_specs=[pl.BlockSpec((t_pad, HD), lambda h: (0, blk0 + h)), pl.BlockSpec((CONV_K, HD), lambda h: (0, h))],
        out_specs=pl.BlockSpec((t_pad, HD), lambda h: (0, h)),
        out_shape=jax.ShapeDtypeStruct((t_pad, GDN_H * HD), F32),
        scratch_shapes=[pltpu.VMEM((t_pad + HALO, HD), F32)],
        compiler_params=_cparams(),
    )(p, w)


def _conv_bwd(p, blk0, w, dn, mode, pad, name):
    t_pad = p.shape[0]
    nt = t_pad // RT
    scale = HD ** -0.5 if mode == "q" else 1.0

    def body(x_ref, w_ref, dn_ref, dx_ref, dw_ref, xs_ref, ds_ref):
        xs_ref[0:HALO, :] = jnp.zeros((HALO, HD), F32)
        xs_ref[HALO + t_pad:HALO + t_pad + 2 * HALO, :] = jnp.zeros((2 * HALO, HD), F32)
        ds_ref[t_pad:t_pad + HALO, :] = jnp.zeros((HALO, HD), F32)
        rows = _iota((t_pad, 1), 0)
        xs_ref[HALO:HALO + t_pad, :] = jnp.where(rows >= pad, x_ref[...], 0.0)
        ds_ref[0:t_pad, :] = dn_ref[...]
        wv = w_ref[...]

        def tile(i, dw):
            r0 = pl.multiple_of(i * RT, RT)
            ext = xs_ref[pl.ds(r0, RT + 2 * HALO), :]
            dn_e = ds_ref[pl.ds(r0, RT + HALO), :]
            xsh = [ext[HALO:, :]] + [pltpu.roll(ext, s, 0)[HALO:, :] for s in (1, 2, 3)]
            pre = xsh[0] * wv[3:4, :]
            for s in (1, 2, 3):
                pre = pre + xsh[s] * wv[3 - s:4 - s, :]
            sg = _sigmoid(pre)
            y = pre * sg
            if mode != "v":
                ss = jnp.sum(y * y, axis=-1, keepdims=True) + L2_EPS
                r = lax.rsqrt(ss)
                dy = scale * (dn_e * r - y * (r * r * r) * jnp.sum(dn_e * y, axis=-1, keepdims=True))
            else:
                dy = dn_e
            dpre = dy * (sg * (1.0 + pre * (1.0 - sg)))
            dx = dpre[:RT, :] * wv[3:4, :]
            for s in (1, 2, 3):
                dx = dx + pltpu.roll(dpre, RT + HALO - s, 0)[:RT, :] * wv[3 - s:4 - s, :]
            trow = r0 + _iota((RT, 1), 0)
            dx_ref[pl.ds(r0, RT), :] = jnp.where(trow >= pad, dx, 0.0)
            new = []
            for s in (0, 1, 2, 3):
                new.append(dw[s] + jnp.sum(dpre[:RT, :] * xsh[s][:RT, :], axis=0, keepdims=True))
            return tuple(new)

        z = jnp.zeros((1, HD), F32)
        dw = lax.fori_loop(0, nt, tile, (z, z, z, z))
        for s in (0, 1, 2, 3):
            dw_ref[3 - s:4 - s, :] = dw[s]

    return pl.pallas_call(
        body, name=name, grid=(GDN_H,),
        in_specs=[pl.BlockSpec((t_pad, HD), lambda h: (0, blk0 + h)), pl.BlockSpec((CONV_K, HD), lambda h: (0, h)),
                  pl.BlockSpec((t_pad, HD), lambda h: (0, h))],
        out_specs=[pl.BlockSpec((t_pad, HD), lambda h: (0, h)), pl.BlockSpec((CONV_K, HD), lambda h: (0, h))],
        out_shape=[jax.ShapeDtypeStruct((t_pad, GDN_H * HD), F32), jax.ShapeDtypeStruct((CONV_K, GDN_H * HD), F32)],
        scratch_shapes=[pltpu.VMEM((t_pad + 3 * HALO, HD), F32), pltpu.VMEM((t_pad + HALO, HD), F32)],
        compiler_params=_cparams(),
    )(p, w, dn)


def _unit_lower_inv(m, bd, eye):
    md = m * bd
    low = m - md
    p2 = mbnn(md, md)
    p4 = mbnn(p2, p2)
    p8 = mbnn(p4, p4)
    dinv = mbnn(mbnn(mbnn(eye - md, eye + p2), eye + p4), eye + p8)
    n = mbnn(dinv, low)
    n2 = mbnn(n, n)
    return mbnn(mbnn(eye - n, eye + n2), dinv)


def _gdn_chunk(q, k, v, bb, aa, alog, dtb, s, valid):
    nh = q.shape[0]
    ri = _iota((1, CH, CH), 1)
    ci = _iota((1, CH, CH), 2)
    causal = ri >= ci
    strict = ri > ci
    eye = (ri == ci).astype(F32)
    bd = ((ri >> 4) == (ci >> 4)).astype(F32)
    ltri = (_iota((CH, CH), 0) >= _iota((CH, CH), 1)).astype(F32)
    sel = (_iota((nh, 1, HD), 2) == _iota((nh, 1, HD), 0)).astype(F32)

    beta_all = jnp.where(valid, _sigmoid(bb), 0.0)
    g_all = jnp.where(valid, -jnp.exp(alog) * _softplus(aa + dtb), 0.0)
    gc_all = hnn(ltri, g_all)
    beta = jnp.sum(beta_all[None] * sel, axis=2, keepdims=True)
    gc = jnp.sum(gc_all[None] * sel, axis=2, keepdims=True)
    gc_rows = hbnt(jnp.broadcast_to(sel, (nh, CH, HD)), jnp.broadcast_to(gc_all[None], (nh, CH, HD)))
    last = _iota((1, CH, 1), 1) == CH - 1
    gc_last = jnp.sum(jnp.where(last, gc, 0.0), axis=1, keepdims=True)
    decay = jnp.exp(jnp.where(causal, gc - gc_rows, NEG))
    egc = jnp.exp(gc)

    kb = k * beta
    m = jnp.where(strict, bbnt(kb, k) * decay, 0.0)
    t_inv = _unit_lower_inv(m, bd, eye)
    u = bbnn(t_inv, v * beta)
    w = bbnn(t_inv, kb * egc)
    a_intra = bbnt(q, k) * decay
    q_dec = q * egc
    k_dec = k * jnp.exp(gc_last - gc)
    v_new = u - bbnn(w, s)
    o = bbnn(q_dec, s) + bbnn(a_intra, v_new)
    s_new = s * jnp.exp(gc_last) + bbtn(k_dec, v_new)
    return o, s_new


def _gdn_specs(nc, rev):
    cc = (lambda c: nc - 1 - c) if rev else (lambda c: c)
    wide = pl.BlockSpec((CH, GDN_H * HD), lambda c: (cc(c), 0))
    fix = lambda off: pl.BlockSpec((CH, HD), lambda c: (cc(c), off))
    par = pl.BlockSpec((1, HD), lambda c: (0, 0))
    state = pl.BlockSpec((1, GDN_H, HD, HD), lambda c: (cc(c), 0, 0, 0))
    return wide, fix, par, state


def _store_heads(ref, a):
    for h in range(a.shape[0]):
        ref[:, h * HD:(h + 1) * HD] = a[h]


def _gdn_fwd(qn, kn, vn, p, alog, dtb, pad, name):
    t_pad = qn.shape[0]
    nc = t_pad // CH
    wide, fix, par, state = _gdn_specs(nc, False)

    def body(q_ref, k_ref, v_ref, bb_ref, aa_ref, al_ref, dt_ref, o_ref, ss_ref, s_ref):
        c = pl.program_id(0)

        @pl.when(c == 0)
        def _():
            s_ref[...] = jnp.zeros_like(s_ref)

        s = s_ref[...]
        ss_ref[0] = s
        valid = (c * CH + _iota((CH, 1), 0)) >= pad
        o, s_new = _gdn_chunk(_heads(q_ref[...], GDN_H), _heads(k_ref[...], GDN_H), _heads(v_ref[...], GDN_H),
                              bb_ref[...], aa_ref[...], al_ref[...], dt_ref[...], s, valid)
        _store_heads(o_ref, o)
        s_ref[...] = s_new

    return pl.pallas_call(
        body, name=name, grid=(nc,),
        in_specs=[wide, wide, wide, fix(16), fix(17), par, par],
        out_specs=[wide, state],
        out_shape=[jax.ShapeDtypeStruct((t_pad, GDN_H * HD), F32), jax.ShapeDtypeStruct((nc, GDN_H, HD, HD), F32)],
        scratch_shapes=[pltpu.VMEM((GDN_H, HD, HD), F32)],
        compiler_params=_cparams(),
    )(qn, kn, vn, p, p, alog, dtb)


def _gdn_bwd(qn, kn, vn, p, alog, dtb, ssave, do, pad, name):
    t_pad = qn.shape[0]
    nc = t_pad // CH
    wide, fix, par, state = _gdn_specs(nc, True)

    def body(q_ref, k_ref, v_ref, bb_ref, aa_ref, al_ref, dt_ref, ss_ref, do_ref,
             dq_ref, dk_ref, dv_ref, dbb_ref, daa_ref, dal_ref, ddt_ref, ds_ref):
        c = pl.program_id(0)

        @pl.when(c == 0)
        def _():
            ds_ref[...] = jnp.zeros_like(ds_ref)
            dal_ref[...] = jnp.zeros_like(dal_ref)
            ddt_ref[...] = jnp.zeros_like(ddt_ref)

        valid = ((nc - 1 - c) * CH + _iota((CH, 1), 0)) >= pad
        fn = lambda q, k, v, bb, aa, al, dt, s: _gdn_chunk(q, k, v, bb, aa, al, dt, s, valid)
        _, vjp = jax.vjp(fn, _heads(q_ref[...], GDN_H), _heads(k_ref[...], GDN_H), _heads(v_ref[...], GDN_H),
                         bb_ref[...], aa_ref[...], al_ref[...], dt_ref[...], ss_ref[0])
        dq, dk, dv, dbb, daa, dal, ddt, ds = vjp((_heads(do_ref[...], GDN_H), ds_ref[...]))
        _store_heads(dq_ref, dq)
        _store_heads(dk_ref, dk)
        _store_heads(dv_ref, dv)
        dbb_ref[...] = dbb
        daa_ref[...] = daa
        dal_ref[...] += dal
        ddt_ref[...] += ddt
        ds_ref[...] = ds

    sds = jax.ShapeDtypeStruct
    return pl.pallas_call(
        body, name=name, grid=(nc,),
        in_specs=[wide, wide, wide, fix(16), fix(17), par, par, state, wide],
        out_specs=[wide, wide, wide, fix(0), fix(0), par, par],
        out_shape=[sds((t_pad, GDN_H * HD), F32)] * 3 + [sds((t_pad, HD), F32)] * 2 + [sds((1, HD), F32)] * 2,
        scratch_shapes=[pltpu.VMEM((GDN_H, HD, HD), F32)],
        compiler_params=_cparams(),
    )(qn, kn, vn, p, p, alog, dtb, ssave, do)


SB_Q0, SB_K0, SB_V0 = 18, 22, 26
SB_SCALE = SB_DH ** -0.5
SB_NB = 4


def _sb_terms(z, allowed):
    sp = jnp.log(1.0 + jnp.exp(-jnp.abs(z)))
    l1m = jnp.where(allowed, -jnp.maximum(z, 0.0) - sp, 0.0)
    ls = jnp.minimum(z, 0.0) - sp
    return l1m, ls


def _sb_stack(a, i):
    first = _iota((1, HD), 1) < SB_DH
    a2 = jnp.concatenate([jnp.where(first, a, 0.0), jnp.where(first, 0.0, a)], axis=0).astype(BF16)
    rq = i * QB + _iota((QB, 1), 0)
    return a2, jnp.concatenate([rq, rq], axis=0), first


def _dot_hi_lo(a, b2):
    hi = a.astype(BF16)
    lo = (a - hi.astype(F32)).astype(BF16)
    return lax.dot_general(jnp.concatenate([hi, lo], axis=1), b2, (NN, ((), ())), preferred_element_type=F32)


def _cargo_bounds(refs, n, n_out, exchange, first, last):
    outs = refs[n:n + n_out]
    if not n:
        return outs, lambda: None
    ex = exchange[0](refs[:n], refs[n + n_out:2 * n + n_out], *refs[2 * n + n_out:])

    @pl.when(first)
    def _():
        ex.start()

    def finish():
        @pl.when(last)
        def _():
            ex.wait()

    return outs, finish


def _sb_fwd(p, pad, name, cargo=(), exchange=None):
    t_pad = p.shape[0]
    nq = t_pad // QB
    n = len(cargo)

    def body(q_ref, k_ref, v_ref, *rest):
        i = pl.program_id(1)
        pr = pl.program_id(0)
        (o_ref, r_ref), end_cargo = _cargo_bounds(rest, n, 2, exchange, (pr == 0) & (i == 0),
                                                  (pr == SB_H // 2 - 1) & (i == nq - 1))
        q2, rowq, first = _sb_stack(q_ref[...] * SB_SCALE, i)
        tri = (_iota((QB, QB), 0) > _iota((QB, QB), 1)).astype(BF16)
        upper2 = jnp.concatenate([tri, tri], axis=0)

        def chain(kb, live):
            start = pl.multiple_of(kb * QB, QB)
            kblk = k_ref[pl.ds(start, QB), :].astype(BF16)
            vblk = v_ref[pl.ds(start, QB), :].astype(BF16)
            z = lax.dot_general(q2, kblk, (NT, ((), ())), preferred_element_type=F32)
            colk = kb * QB + _iota((1, QB), 1)
            allowed = (colk < rowq) & (colk >= pad) & live
            l1m, ls = _sb_terms(z, allowed)
            return allowed, ls, _dot_hi_lo(l1m, upper2), jnp.sum(l1m, axis=1, keepdims=True), vblk

        def step(j, carry):
            o_acc, run = carry
            ws, vs = [], []
            for n in range(SB_NB):
                kb = i - SB_NB * j - n
                allowed, ls, suf, rs, vblk = chain(jnp.maximum(kb, 0), kb >= 0)
                ws.append(jnp.where(allowed, jnp.exp(ls + suf + run), 0.0).astype(BF16))
                vs.append(vblk)
                run = run + rs
            o_acc = o_acc + lax.dot_general(jnp.concatenate(ws, axis=1), jnp.concatenate(vs, axis=0),
                                            (NN, ((), ())), preferred_element_type=F32)
            return o_acc, run

        o_acc, run = lax.fori_loop(0, (i + SB_NB) // SB_NB, step,
                                   (jnp.zeros((2 * QB, HD), F32), jnp.zeros((2 * QB, 1), F32)))
        o_ref[...] = jnp.where(first, o_acc[:QB], o_acc[QB:]).astype(BF16)
        r_ref[...] = jnp.where(first, run[:QB], run[QB:])
        end_cargo()

    full = lambda off: pl.BlockSpec((t_pad, HD), lambda pr, i: (0, off + pr))
    blk = pl.BlockSpec((QB, HD), lambda pr, i: (i, pr))
    return pl.pallas_call(
        body, name=name, grid=(SB_H // 2, nq),
        in_specs=[pl.BlockSpec((QB, HD), lambda pr, i: (i, SB_Q0 + pr)), full(SB_K0), full(SB_V0)] + [ANY] * n,
        out_specs=[blk, blk] + [ANY] * n,
        out_shape=[jax.ShapeDtypeStruct((t_pad, SB_H * SB_DH), BF16), jax.ShapeDtypeStruct((t_pad, SB_H * SB_DH), F32)]
        + (exchange[1](cargo) if n else []),
        scratch_shapes=exchange[2](n) if n else [],
        compiler_params=_cparams(),
    )(p, p, p, *cargo)


def _sb_bwd(p, rtot, dy, dy_blk0, pad, name, cargo=(), exchange=None):
    t_pad = p.shape[0]
    nq = t_pad // QB
    n = len(cargo)

    def body(q_ref, k_ref, v_ref, r_ref, do_ref, *rest):
        i = pl.program_id(1)
        pr = pl.program_id(0)
        (dq_ref, dk_ref, dv_ref), end_cargo = _cargo_bounds(rest, n, 3, exchange, (pr == 0) & (i == 0),
                                                            (pr == SB_H // 2 - 1) & (i == nq - 1))

        @pl.when(i == 0)
        def _():
            dk_ref[...] = jnp.zeros_like(dk_ref)
            dv_ref[...] = jnp.zeros_like(dv_ref)

        q2, rowq, first = _sb_stack(q_ref[...] * SB_SCALE, i)
        do2, _, _ = _sb_stack(do_ref[...], i)
        rt = r_ref[...]
        lane = _iota((1, HD), 1)
        rcol = jnp.concatenate([jnp.sum(jnp.where(lane == 0, rt, 0.0), axis=1, keepdims=True),
                                jnp.sum(jnp.where(lane == SB_DH, rt, 0.0), axis=1, keepdims=True)], axis=0)
        rj = _iota((QB, QB), 0)
        cs = _iota((QB, QB), 1)
        tri_u = (rj > cs).astype(BF16)
        tri_l = (rj < cs).astype(BF16)
        upper2 = jnp.concatenate([tri_u, tri_u], axis=0)
        lower2 = jnp.concatenate([tri_l, tri_l], axis=0)

        def chain(kb, live):
            start = pl.multiple_of(kb * QB, QB)
            kblk = k_ref[pl.ds(start, QB), :].astype(BF16)
            vblk = v_ref[pl.ds(start, QB), :].astype(BF16)
            z = lax.dot_general(q2, kblk, (NT, ((), ())), preferred_element_type=F32)
            colk = kb * QB + _iota((1, QB), 1)
            allowed = (colk < rowq) & (colk >= pad) & live
            l1m, ls = _sb_terms(z, allowed)
            dwgt = lax.dot_general(do2, vblk, (NT, ((), ())), preferred_element_type=F32)
            return (start, kblk, allowed, ls, _dot_hi_lo(l1m, upper2), jnp.sum(l1m, axis=1, keepdims=True), dwgt,
                    _sigmoid(z))

        def finish(c, seen, gseen):
            start, kblk, allowed, ls, suf, rs, dwgt, sg = c
            wgt = jnp.where(allowed, jnp.exp(ls + suf + (rcol - seen - rs)), 0.0)
            dl = dwgt * wgt
            gpre = gseen + _dot_hi_lo(dl, lower2)
            dz = jnp.where(allowed, dl * (1.0 - sg) - gpre * sg, 0.0).astype(BF16)
            dk_ref[pl.ds(start, QB), :] += lax.dot_general(dz, q2, (TN, ((), ())), preferred_element_type=F32)
            dv_ref[pl.ds(start, QB), :] += lax.dot_general(wgt.astype(BF16), do2, (TN, ((), ())),
                                                           preferred_element_type=F32)
            return dz, seen + rs, gseen + jnp.sum(dl, axis=1, keepdims=True)

        def step(j, carry):
            dq_acc, seen, gseen = carry
            cs_ = [chain(jnp.minimum(SB_NB * j + n, i), SB_NB * j + n <= i) for n in range(SB_NB)]
            dzs = []
            for c in cs_:
                dz, seen, gseen = finish(c, seen, gseen)
                dzs.append(dz)
            dq_acc = dq_acc + lax.dot_general(jnp.concatenate(dzs, axis=1), jnp.concatenate([c[1] for c in cs_], axis=0),
                                              (NN, ((), ())), preferred_element_type=F32)
            return dq_acc, seen, gseen

        zc = jnp.zeros((2 * QB, 1), F32)
        dq_acc, _, _ = lax.fori_loop(0, (i + SB_NB) // SB_NB, step, (jnp.zeros((2 * QB, HD), F32), zc, zc))
        dq_ref[...] = jnp.where(first, dq_acc[:QB], dq_acc[QB:]) * SB_SCALE
        end_cargo()

    full_in = lambda off: pl.BlockSpec((t_pad, HD), lambda pr, i: (0, off + pr))
    full_out = pl.BlockSpec((t_pad, HD), lambda pr, i: (0, pr))
    blk = pl.BlockSpec((QB, HD), lambda pr, i: (i, pr))
    sds = jax.ShapeDtypeStruct((t_pad, SB_H * SB_DH), F32)
    return pl.pallas_call(
        body, name=name, grid=(SB_H // 2, nq),
        in_specs=[pl.BlockSpec((QB, HD), lambda pr, i: (i, SB_Q0 + pr)), full_in(SB_K0), full_in(SB_V0), blk,
                  pl.BlockSpec((QB, HD), lambda pr, i: (i, dy_blk0 + pr))] + [ANY] * n,
        out_specs=[blk, full_out, full_out] + [ANY] * n,
        out_shape=[sds, sds, sds] + (exchange[1](cargo) if n else []),
        scratch_shapes=exchange[2](n) if n else [],
        compiler_params=_cparams(),
    )(p, p, p, rtot, dy, *cargo)


HG_LEVELS = 6


def _hg_prefix_matrix():
    t = np.arange(CH)[:, None]
    j = np.arange(CH)[None, :]
    groups = [(j <= t)]
    for lvl in range(1, HG_LEVELS + 1):
        half = CH >> lvl
        e = (t // (2 * half)) * (2 * half) + half - 1
        groups.append(j <= e)
    groups.append(np.ones((2 * CH, CH), bool))
    return np.concatenate(groups, axis=0).astype(np.float32)


HG_G = 4


def _hg_chunk(qr, fr, iv, r0, r1, st, valid, ecat):
    g = st.shape[0]
    mx = jnp.maximum(r0, r1)
    e0 = jnp.exp(r0 - mx)
    e1 = jnp.exp(r1 - mx)
    lb = e1 / (e0 + e1)
    fg = lb + (1.0 - lb) * _sigmoid(fr)
    logf = jnp.where(valid, jnp.log(fg), 0.0)
    kk = jnp.where(valid, 1.0 - fg, 0.0)
    q = jnp.where(valid, _silu(qr), 0.0)
    v = _heads(jnp.where(valid, iv, 0.0), g)

    pre = hnn(ecat, logf)
    b = pre[0:CH]
    b_last = pre[(HG_LEVELS + 1) * CH:]
    row = _iota((CH, 1), 0)
    ri = _iota((1, CH, CH), 1)
    ci = _iota((1, CH, CH), 2)
    a = jnp.where(ri == ci, jnp.sum(_heads(q * kk, g), axis=2, keepdims=True), 0.0)
    for lvl in range(1, HG_LEVELS + 1):
        half = CH >> lvl
        m = pre[lvl * CH:(lvl + 1) * CH]
        low = (row & half) != 0
        qt = jnp.where(low, q * jnp.exp(jnp.where(low, b - m, 0.0)), 0.0)
        kt = jnp.where(low, 0.0, kk * jnp.exp(jnp.where(low, 0.0, m - b)))
        same = (ri >> (7 - lvl)) == (ci >> (7 - lvl))
        a = a + jnp.where(same, bbnt(_heads(qt, g), _heads(kt, g)), 0.0)
    o = bbnt(_heads(q * jnp.exp(b), g), st) + bbnn(a, v)
    kd = kk * jnp.exp(b_last[0:CH] - b)
    st_new = st * _heads(jnp.exp(b_last), g) + bbtn(v, _heads(kd, g))
    return o, st_new


def _hg_specs(nc, rev):
    cc = (lambda c: nc - 1 - c) if rev else (lambda c: c)
    ng = HG_H // HG_G
    blk = lambda off: pl.BlockSpec((CH, HG_G * HD), lambda h, c: (cc(c), off * ng + h))
    lbs = pl.BlockSpec((2, HG_G * HD), lambda h, c: (0, h))
    state = pl.BlockSpec((1, HG_G, HD, HD), lambda h, c: (cc(c), h, 0, 0))
    return ng, blk, lbs, state


def _hg_fwd(p, lbraw, ecat, pad, name):
    t_pad = p.shape[0]
    nc = t_pad // CH
    ng, blk, lbs, state = _hg_specs(nc, False)

    def body(q_ref, f_ref, i_ref, lb_ref, e_ref, o_ref, ss_ref, s_ref):
        c = pl.program_id(1)

        @pl.when(c == 0)
        def _():
            s_ref[...] = jnp.zeros_like(s_ref)

        st = s_ref[...]
        ss_ref[0] = st
        valid = (c * CH + _iota((CH, 1), 0)) >= pad
        o, st_new = _hg_chunk(q_ref[...], f_ref[...], i_ref[...], lb_ref[0:1, :], lb_ref[1:2, :], st, valid, e_ref[...])
        _store_heads(o_ref, o)
        s_ref[...] = st_new

    return pl.pallas_call(
        body, name=name, grid=(ng, nc),
        in_specs=[blk(0), blk(1), blk(2), lbs, pl.BlockSpec(ecat.shape, lambda h, c: (0, 0))],
        out_specs=[blk(0), state],
        out_shape=[jax.ShapeDtypeStruct((t_pad, HG_H * HD), F32), jax.ShapeDtypeStruct((nc, HG_H, HD, HD), F32)],
        scratch_shapes=[pltpu.VMEM((HG_G, HD, HD), F32)],
        compiler_params=_cparams(),
    )(p, p, p, lbraw, ecat)


def _hg_bwd(p, lbraw, ecat, ssave, do, pad, name):
    t_pad = p.shape[0]
    nc = t_pad // CH
    ng, blk, lbs, state = _hg_specs(nc, True)

    def body(q_ref, f_ref, i_ref, lb_ref, e_ref, ss_ref, do_ref, dq_ref, df_ref, di_ref, dlb_ref, ds_ref):
        c = pl.program_id(1)

        @pl.when(c == 0)
        def _():
            ds_ref[...] = jnp.zeros_like(ds_ref)
            dlb_ref[...] = jnp.zeros_like(dlb_ref)

        valid = ((nc - 1 - c) * CH + _iota((CH, 1), 0)) >= pad
        ecv = e_ref[...]
        fn = lambda qr, fr, iv, r0, r1, st: _hg_chunk(qr, fr, iv, r0, r1, st, valid, ecv)
        _, vjp = jax.vjp(fn, q_ref[...], f_ref[...], i_ref[...], lb_ref[0:1, :], lb_ref[1:2, :], ss_ref[0])
        dq, df, di, d0, d1, ds = vjp((_heads(do_ref[...], HG_G), ds_ref[...]))
        dq_ref[...] = dq
        df_ref[...] = df
        di_ref[...] = di
        dlb_ref[0:1, :] += d0
        dlb_ref[1:2, :] += d1
        ds_ref[...] = ds

    sds = jax.ShapeDtypeStruct((t_pad, HG_H * HD), F32)
    return pl.pallas_call(
        body, name=name, grid=(ng, nc),
        in_specs=[blk(0), blk(1), blk(2), lbs, pl.BlockSpec(ecat.shape, lambda h, c: (0, 0)), state, blk(0)],
        out_specs=[blk(0), blk(0), blk(0), lbs],
        out_shape=[sds, sds, sds, jax.ShapeDtypeStruct((2, HG_H * HD), F32)],
        scratch_shapes=[pltpu.VMEM((HG_G, HD, HD), F32)],
        compiler_params=_cparams(),
    )(p, p, p, lbraw, ecat, ssave, do)


def _pad_ab_cols(w):
    z = jnp.zeros((w.shape[0], HD - GDN_H), w.dtype)
    return jnp.concatenate([w[:, :2048], w[:, 2048:2052], z, w[:, 2052:2056], z, w[:, 2056:]], axis=1)


def _unpad_ab_cols(w):
    return jnp.concatenate([w[:, :2048], w[:, 2048:2052], w[:, 2176:2180], w[:, 2304:]], axis=1)


def _lane_pad(v):
    return jnp.pad(v, ((0, 0), (0, HD - v.shape[1])))


def _mlp_fwd(hb, w1, w2, layer):
    a, r = _mm(hb, w1, b_view=("cols", layer), out_dtype=BF16, act=True, name=f"mlp_up_{layer}")
    m = _mm(r, w2, b_view=("rows", layer), name=f"mlp_down_{layer}")
    return a, r, m


def _mlp_bwd(hb, a, r, dmb, w1, w2, layer):
    da = _mm(dmb, w2, tb=True, b_view=("rows", layer), out_dtype=BF16, gate=a, name=f"mlp_down_dx_{layer}")
    dw2 = _mm(r, dmb, ta=True, out_dtype=BF16, name=f"mlp_down_dw_{layer}")
    dh = _mm(da, w1, tb=True, b_view=("cols", layer), name=f"mlp_up_dx_{layer}")
    dw1 = _mm(hb, da, ta=True, out_dtype=BF16, out_split=N_CHIP, name=f"mlp_up_dw_{layer}")
    return dh, dw1, dw2


def _halves(a):
    return a.reshape(a.shape[0], 2, a.shape[1] // 2, a.shape[2])


def _local_step(h0, tgt, w, pad, late=None):
    row = lambda a, i: a[i:i + 1]
    ecat = jnp.asarray(_hg_prefix_matrix())
    cw = [w["conv_w"][:, i * 512:(i + 1) * 512] for i in range(3)]
    alog, dtb = _lane_pad(w["a_log"]), _lane_pad(w["dt_bias"])

    h0b = h0.astype(BF16)
    p0 = _mm(h0b, w["ab_w_in"], name="ab_in")
    qn = _conv_fwd(p0, 0, cw[0], "q", pad, "conv_q")
    kn = _conv_fwd(p0, 4, cw[1], "k", pad, "conv_k")
    vn = _conv_fwd(p0, 8, cw[2], "v", pad, "conv_v")
    oa_raw, ss0 = _gdn_fwd(qn, kn, vn, p0, alog, dtb, pad, "gdn_fwd")
    oa = _grms_fwd(oa_raw, p0, 12, w["ab_gnorm_g"], "gdn_gate")
    if late is None:
        ob, rtot = _sb_fwd(p0, pad, "sb_fwd")
    else:
        ob, rtot, g_cin, g_cout, g_w1, g_w2 = _sb_fwd(p0, pad, "sb_fwd", cargo=late, exchange=GATHER)
        w = dict(w, c_w_in=g_cin, c_w_out=g_cout.reshape(D, D), mlp_w1=g_w1, mlp_w2=g_w2)
    ycat = jnp.concatenate([oa, ob], axis=1)
    mix0 = _mm(ycat, w["ab_w_out"], name="ab_out")
    h1, h1b = _ln_res_fwd(h0, mix0, row(w["ln_mix_g"], 0), row(w["ln_mix_b"], 0), "ln_mix_0")
    a0, r0, m0 = _mlp_fwd(h1b, w["mlp_w1"], w["mlp_w2"], 0)
    h2, h2b = _ln_res_fwd(h1, m0, row(w["ln_ffn_g"], 0), row(w["ln_ffn_b"], 0), "ln_ffn_0")
    p1 = _mm(h2b, w["c_w_in"], b_view=("cols", 0), name="c_in")
    oc_raw, ss1 = _hg_fwd(p1, w["c_lb_raw"], ecat, pad, "hg_fwd")
    yc = _grms_fwd(oc_raw, p1, 3 * HG_H, w["c_gnorm_g"], "hg_gate")
    mix1 = _mm(yc, w["c_w_out"], name="c_out")
    h3, h3b = _ln_res_fwd(h2, mix1, row(w["ln_mix_g"], 1), row(w["ln_mix_b"], 1), "ln_mix_1")
    a1, r1, m1 = _mlp_fwd(h3b, w["mlp_w1"], w["mlp_w2"], 1)
    h4, _ = _ln_res_fwd(h3, m1, row(w["ln_ffn_g"], 1), row(w["ln_ffn_b"], 1), "ln_ffn_1")
    loss, dh4 = _loss_fwd(h4, tgt, pad + N_META, "loss")

    zero = jnp.zeros_like(dh4)
    dh3a, dm1b, dfg1, dfb1 = _ln_res_bwd(h3, m1, row(w["ln_ffn_g"], 1), row(w["ln_ffn_b"], 1), dh4, zero, "ln_ffn_bwd_1")
    dh3b, dw1_1, dw2_1 = _mlp_bwd(h3b, a1, r1, dm1b, w["mlp_w1"], w["mlp_w2"], 1)
    dh2a, dmix1b, dmg1, dmb1 = _ln_res_bwd(h2, mix1, row(w["ln_mix_g"], 1), row(w["ln_mix_b"], 1), dh3a, dh3b, "ln_mix_bwd_1")
    dyc = _mm(dmix1b, w["c_w_out"], tb=True, name="c_out_dx")
    dwco = _mm(yc, dmix1b, ta=True, out_dtype=BF16, name="c_out_dw")
    doc, dzc, dcg = _grms_bwd(oc_raw, p1, 3 * HG_H, w["c_gnorm_g"], dyc, 0, "hg_gate_bwd")
    dq1, df1, di1, dlb = _hg_bwd(p1, w["c_lb_raw"], ecat, ss1, doc, pad, "hg_bwd")
    dp1 = _assemble_bf16([(dq1, "cols"), (df1, "cols"), (di1, "cols"), (dzc, "cols")], "c_in_dy")
    dh2b = _mm(dp1, w["c_w_in"], tb=True, b_view=("cols", 0), name="c_in_dx")
    dwc = _mm(h2b, dp1, ta=True, out_dtype=BF16, out_split=N_CHIP, name="c_in_dw")
    dh1a, dm0b, dfg0, dfb0 = _ln_res_bwd(h1, m0, row(w["ln_ffn_g"], 0), row(w["ln_ffn_b"], 0), dh2a, dh2b, "ln_ffn_bwd_0")
    dh1b, dw1_0, dw2_0 = _mlp_bwd(h1b, a0, r0, dm0b, w["mlp_w1"], w["mlp_w2"], 0)
    dh0a, dmix0b, dmg0, dmb0 = _ln_res_bwd(h0, mix0, row(w["ln_mix_g"], 0), row(w["ln_mix_b"], 0), dh1a, dh1b, "ln_mix_bwd_0")
    dycat = _mm(dmix0b, w["ab_w_out"], tb=True, name="ab_out_dx")
    dwabo = _mm(ycat, dmix0b, ta=True, out_dtype=BF16, name="ab_out_dw")
    doa, dza, dag = _grms_bwd(oa_raw, p0, 12, w["ab_gnorm_g"], dycat, 0, "gdn_gate_bwd")
    dqn, dkn, dvn, dbb, daa, dal, ddt = _gdn_bwd(qn, kn, vn, p0, alog, dtb, ss0, doa, pad, "gdn_bwd")
    dpq, dcq = _conv_bwd(p0, 0, cw[0], dqn, "q", pad, "conv_q_bwd")
    dpk, dck = _conv_bwd(p0, 4, cw[1], dkn, "k", pad, "conv_k_bwd")
    dpv, dcv = _conv_bwd(p0, 8, cw[2], dvn, "v", pad, "conv_v_bwd")
    landed = None
    if late is None:
        dqb, dkb, dvb = _sb_bwd(p0, rtot, dycat, 4, pad, "sb_bwd")
    else:
        ready = [dwabo.reshape(N_CHIP, -1, D), dwc, dwco.reshape(N_CHIP, -1, D), dw1_0, dw1_1,
                 dw2_0.reshape(N_CHIP, -1, D), dw2_1.reshape(N_CHIP, -1, D)]
        dqb, dkb, dvb, *landed = _sb_bwd(p0, rtot, dycat, 4, pad, "sb_bwd", cargo=[_halves(a) for a in ready],
                                         exchange=SCATTER)
    dp0 = _assemble_bf16([(dpq, "cols"), (dpk, "cols"), (dpv, "cols"), (dza, "cols"), (dbb, "cols"), (daa, "cols"),
                          (dqb, "cols"), (dkb, "cols"), (dvb, "cols")], "ab_in_dy")
    dh0b = _mm(dp0, w["ab_w_in"], tb=True, name="ab_in_dx")
    dwab = _mm(h0b, dp0, ta=True, out_dtype=BF16, name="ab_in_dw")
    dh0 = _add2(dh0a, dh0b, "dh0")

    grads = {
        "ab_w_in": dwab, "conv_w": jnp.concatenate([dcq, dck, dcv], axis=1),
        "a_log": dal[:, :GDN_H], "dt_bias": ddt[:, :GDN_H],
        "ab_gnorm_g": dag, "ab_w_out": dwabo, "c_w_in": dwc, "c_lb_raw": dlb, "c_gnorm_g": dcg, "c_w_out": dwco,
        "ln_mix_g": jnp.concatenate([dmg0, dmg1], 0), "ln_mix_b": jnp.concatenate([dmb0, dmb1], 0),
        "w1_0": dw1_0, "w1_1": dw1_1, "w2_0": dw2_0, "w2_1": dw2_1,
        "ln_ffn_g": jnp.concatenate([dfg0, dfg1], 0), "ln_ffn_b": jnp.concatenate([dfb0, dfb1], 0),
        "landed": landed,
    }
    return loss, dh0, grads


MESH = pl.DeviceIdType.MESH
ANY = pl.BlockSpec(memory_space=pl.ANY)
N_CHIP = 4
N_DEV = 8
CHIP_REL = ((1, 0), (0, 1), (1, 1))
DEV_REL = tuple((dx, dy, dc) for dx in (0, 1) for dy in (0, 1) for dc in (0, 1))[1:]
SWAP_CHUNKS = 4


def _pos():
    return lax.axis_index("x"), lax.axis_index("y"), lax.axis_index("c")


def _flip(a, d):
    return a + d - 2 * a * d


class _Exchange:
    def __init__(self, local, sends, recvs):
        self.local, self.sends, self.recvs = local, sends, recvs

    def start(self):
        for cp in self.local + self.sends:
            cp.start()

    def wait(self):
        for cp in self.recvs:
            cp.wait_recv()
        for cp in self.sends:
            cp.wait_send()
        for cp in self.local:
            cp.wait()


def _gather_sems(n):
    return [pltpu.SemaphoreType.DMA((3 * n,)), pltpu.SemaphoreType.DMA((3 * n,)), pltpu.SemaphoreType.DMA((n,))]


def _gather_copies(x_refs, o_refs, send_sems, recv_sems, local_sems):
    n = len(x_refs)
    x, y, c = _pos()
    local = [pltpu.make_async_copy(x_refs[a], o_refs[a].at[2 * x + y], local_sems.at[a]) for a in range(n)]

    def copy(a, k, sending):
        tx, ty = _flip(x, CHIP_REL[k][0]), _flip(y, CHIP_REL[k][1])
        return pltpu.make_async_remote_copy(
            src_ref=x_refs[a], dst_ref=o_refs[a].at[2 * x + y if sending else 2 * tx + ty],
            send_sem=send_sems.at[3 * a + k], recv_sem=recv_sems.at[3 * a + k], device_id=(tx, ty, c), device_id_type=MESH)

    pairs = [(a, k) for a in range(n) for k in range(3)]
    return _Exchange(local, [copy(a, k, True) for a, k in pairs], [copy(a, k, False) for a, k in pairs])


def _gather_shapes(bufs):
    return [jax.ShapeDtypeStruct((N_CHIP,) + b.shape, b.dtype) for b in bufs]


def _chip_allgather(bufs, name):
    n = len(bufs)

    def body(*refs):
        ex = _gather_copies(refs[:n], refs[n:2 * n], *refs[2 * n:])
        ex.start()
        ex.wait()

    return pl.pallas_call(
        body, name=name, in_specs=[ANY] * n, out_specs=[ANY] * n, out_shape=_gather_shapes(bufs),
        scratch_shapes=_gather_sems(n), compiler_params=pltpu.CompilerParams(has_side_effects=True),
    )(*bufs)


def _scatter_sems(n):
    nr = N_DEV - 1
    return [pltpu.SemaphoreType.DMA((nr * n,)), pltpu.SemaphoreType.DMA((nr * n,)), pltpu.SemaphoreType.DMA((n,))]


def _scatter_copies(g_refs, o_refs, send_sems, recv_sems, local_sems):
    n = len(g_refs)
    nr = N_DEV - 1
    x, y, c = _pos()
    me = 4 * x + 2 * y + c
    local = [pltpu.make_async_copy(g_refs[a].at[2 * x + y, c], o_refs[a].at[me], local_sems.at[a]) for a in range(n)]

    def copy(a, k, sending):
        dx, dy, dc = DEV_REL[k]
        tx, ty, tc = _flip(x, dx), _flip(y, dy), _flip(c, dc)
        return pltpu.make_async_remote_copy(
            src_ref=g_refs[a].at[2 * tx + ty, tc], dst_ref=o_refs[a].at[me if sending else 4 * tx + 2 * ty + tc],
            send_sem=send_sems.at[nr * a + k], recv_sem=recv_sems.at[nr * a + k],
            device_id=(tx, ty, tc), device_id_type=MESH)

    pairs = [(a, k) for a in range(n) for k in range(nr)]
    return _Exchange(local, [copy(a, k, True) for a, k in pairs], [copy(a, k, False) for a, k in pairs])


def _scatter_shapes(gs):
    return [jax.ShapeDtypeStruct((N_DEV,) + g.shape[2:], g.dtype) for g in gs]


def _grad_alltoall(gs, name):
    n = len(gs)

    def body(*refs):
        ex = _scatter_copies(refs[:n], refs[n:2 * n], *refs[2 * n:])
        ex.start()
        ex.wait()

    return pl.pallas_call(
        body, name=name, in_specs=[ANY] * n, out_specs=[ANY] * n, out_shape=_scatter_shapes(gs),
        scratch_shapes=_scatter_sems(n), compiler_params=pltpu.CompilerParams(has_side_effects=True),
    )(*gs)


GATHER = (_gather_copies, _gather_shapes, _gather_sems)
SCATTER = (_scatter_copies, _scatter_shapes, _scatter_sems)


def _sum_slots(r, name):
    n, rh, w = r.shape
    tr = _pick(rh, (256, 128, 64, 16))

    def body(r_ref, o_ref):
        acc = r_ref[0].astype(F32)
        for s in range(1, n):
            acc = acc + r_ref[s].astype(F32)
        o_ref[...] = acc

    return pl.pallas_call(
        body, name=name, grid=(rh // tr,), in_specs=[pl.BlockSpec((n, tr, w), lambda i: (0, i, 0))],
        out_specs=pl.BlockSpec((tr, w), lambda i: (i, 0)), out_shape=jax.ShapeDtypeStruct((rh, w), F32),
        compiler_params=_cparams(),
    )(r)


def _sibling_exchange(halves, dests, out_shapes, name):
    n = len(halves)
    no = len(out_shapes)

    def body(*refs):
        h_refs, o_refs = refs[:n], refs[n:n + no]
        send_sems, recv_sems, local_sems = refs[n + no:]
        x, y, c = _pos()

        def src(a, q):
            rows = halves[a].shape[0] // SWAP_CHUNKS
            return h_refs[a].at[pl.ds(q * rows, rows)]

        def slot(a, q, cc):
            o, lead = dests[a]
            rows = halves[a].shape[0] // SWAP_CHUNKS
            return o_refs[o].at[(*lead, cc, pl.ds(q * rows, rows))]

        def remote(a, q, cc):
            return pltpu.make_async_remote_copy(
                src_ref=src(a, q), dst_ref=slot(a, q, cc), send_sem=send_sems.at[SWAP_CHUNKS * a + q],
                recv_sem=recv_sems.at[SWAP_CHUNKS * a + q], device_id=(x, y, 1 - c), device_id_type=MESH)

        pairs = [(a, q) for a in range(n) for q in range(SWAP_CHUNKS)]
        local = [pltpu.make_async_copy(src(a, q), slot(a, q, c), local_sems.at[SWAP_CHUNKS * a + q]) for a, q in pairs]
        sends = [remote(a, q, c) for a, q in pairs]
        for cp in sends + local:
            cp.start()
        for a, q in pairs:
            remote(a, q, 1 - c).wait_recv()
        for cp in sends:
            cp.wait_send()
        for cp in local:
            cp.wait()

    ns = n * SWAP_CHUNKS
    return pl.pallas_call(
        body, name=name, in_specs=[ANY] * n, out_specs=[ANY] * no,
        out_shape=[jax.ShapeDtypeStruct(s, F32) for s in out_shapes],
        scratch_shapes=[pltpu.SemaphoreType.DMA((ns,)), pltpu.SemaphoreType.DMA((ns,)), pltpu.SemaphoreType.DMA((ns,))],
        compiler_params=pltpu.CompilerParams(has_side_effects=True),
    )(*halves)


def _small_allreduce(buf, name):
    r, w = buf.shape

    def body(b_ref, o_ref, land_ref, send_sems, recv_sems):
        x, y, c = _pos()
        me = 4 * x + 2 * y + c
        land_ref[me] = b_ref[...]

        def target(k):
            dx, dy, dc = DEV_REL[k]
            return _flip(x, dx), _flip(y, dy), _flip(c, dc)

        sends = []
        for k in range(N_DEV - 1):
            tx, ty, tc = target(k)
            cp = pltpu.make_async_remote_copy(
                src_ref=b_ref, dst_ref=land_ref.at[me], send_sem=send_sems.at[k], recv_sem=recv_sems.at[k],
                device_id=(tx, ty, tc), device_id_type=MESH)
            cp.start()
            sends.append(cp)
        for k in range(N_DEV - 1):
            tx, ty, tc = target(k)
            pltpu.make_async_remote_copy(
                src_ref=b_ref, dst_ref=land_ref.at[4 * tx + 2 * ty + tc], send_sem=send_sems.at[k],
                recv_sem=recv_sems.at[k], device_id=(tx, ty, tc), device_id_type=MESH).wait_recv()
        for cp in sends:
            cp.wait_send()
        acc = land_ref[0]
        for s in range(1, N_DEV):
            acc = acc + land_ref[s]
        o_ref[...] = acc

    vm = pl.BlockSpec(memory_space=pltpu.VMEM)
    return pl.pallas_call(
        body, name=name, in_specs=[vm], out_specs=vm, out_shape=jax.ShapeDtypeStruct((r, w), F32),
        scratch_shapes=[pltpu.VMEM((N_DEV, r, w), F32), pltpu.SemaphoreType.DMA((N_DEV - 1,)),
                        pltpu.SemaphoreType.DMA((N_DEV - 1,))],
        compiler_params=pltpu.CompilerParams(has_side_effects=True),
    )(buf)


def _adamw(w, g, m, v, name):
    r, c = w.shape
    tr = _pick(r, (256, 128, 64, 8)) if r * c > (1 << 18) else r

    def body(w_ref, g_ref, m_ref, v_ref, d_ref, m2_ref, v2_ref):
        gg = g_ref[...]
        m2 = ADAM_B1 * m_ref[...] + (1.0 - ADAM_B1) * gg
        v2 = ADAM_B2 * v_ref[...] + (1.0 - ADAM_B2) * (gg * gg)
        m_hat = m2 / (1.0 - ADAM_B1 ** ADAM_STEP)
        v_hat = v2 / (1.0 - ADAM_B2 ** ADAM_STEP)
        d_ref[...] = -ADAM_LR * (m_hat / (jnp.sqrt(v_hat) + ADAM_EPS) + ADAM_WD * w_ref[...])
        m2_ref[...] = m2
        v2_ref[...] = v2

    blk = pl.BlockSpec((tr, c), lambda i: (i, 0))
    sds = jax.ShapeDtypeStruct((r, c), F32)
    return pl.pallas_call(body, name=name, grid=(r // tr,), in_specs=[blk] * 4, out_specs=[blk] * 3,
                          out_shape=[sds] * 3, compiler_params=_cparams())(w, g, m, v)


BIG = ("ab_w_in", "ab_w_out", "c_w_in", "c_w_out", "mlp_w1", "mlp_w2")
SMALL = ("ln_mix_g", "ln_mix_b", "ln_ffn_g", "ln_ffn_b", "c_lb_raw", "ab_a_log", "ab_dt_bias", "ab_gnorm_g", "c_gnorm_g")
SMALL_ROWS = 16
CONV_ROWS = 8
CONV_W = 3 * GDN_H * HD


def _conv_to_rows(cw):
    return jnp.pad(cw, ((0, 0), (0, 2 * D - CONV_W))).reshape(CONV_ROWS, D)


def _rows_to_conv(rows):
    return rows.reshape(CONV_K, 2 * D)[:, :CONV_W]


def _pack_small(d):
    rows = [jnp.pad(d[n], ((0, 0), (0, D - d[n].shape[1]))) for n in SMALL]
    buf = jnp.concatenate(rows, axis=0)
    return jnp.pad(buf, ((0, SMALL_ROWS - buf.shape[0]), (0, 0)))


def _unpack_small(buf, like):
    out, r = {}, 0
    for n in SMALL:
        nr, nc = like[n].shape
        out[n] = buf[r:r + nr, :nc]
        r += nr
    return out


def kernel(x, meta_tokens, ab_w_in, ab_conv_w, ab_a_log, ab_dt_bias, ab_gnorm_g, ab_w_out, c_w_in, c_lb_raw, c_gnorm_g, c_w_out, ln_mix_g, ln_mix_b, mlp_w1, mlp_w2, ln_ffn_g, ln_ffn_b, loss_target, m_meta_tokens, m_ab_w_in, m_ab_conv_w, m_ab_a_log, m_ab_dt_bias, m_ab_gnorm_g, m_ab_w_out, m_c_w_in, m_c_lb_raw, m_c_gnorm_g, m_c_w_out, m_ln_mix_g, m_ln_mix_b, m_mlp_w1, m_mlp_w2, m_ln_ffn_g, m_ln_ffn_b, v_meta_tokens, v_ab_w_in, v_ab_conv_w, v_ab_a_log, v_ab_dt_bias, v_ab_gnorm_g, v_ab_w_out, v_c_w_in, v_c_lb_raw, v_c_gnorm_g, v_c_w_out, v_ln_mix_g, v_ln_mix_b, v_mlp_w1, v_mlp_w2, v_ln_ffn_g, v_ln_ffn_b):
    names = ("meta_tokens", "ab_w_in", "ab_conv_w", "ab_a_log", "ab_dt_bias", "ab_gnorm_g", "ab_w_out", "c_w_in",
             "c_lb_raw", "c_gnorm_g", "c_w_out", "ln_mix_g", "ln_mix_b", "mlp_w1", "mlp_w2", "ln_ffn_g", "ln_ffn_b")
    wts = dict(zip(names, (meta_tokens, ab_w_in, ab_conv_w, ab_a_log, ab_dt_bias, ab_gnorm_g, ab_w_out, c_w_in, c_lb_raw,
                           c_gnorm_g, c_w_out, ln_mix_g, ln_mix_b, mlp_w1, mlp_w2, ln_ffn_g, ln_ffn_b)))
    mom_m = dict(zip(names, (m_meta_tokens, m_ab_w_in, m_ab_conv_w, m_ab_a_log, m_ab_dt_bias, m_ab_gnorm_g, m_ab_w_out,
                             m_c_w_in, m_c_lb_raw, m_c_gnorm_g, m_c_w_out, m_ln_mix_g, m_ln_mix_b, m_mlp_w1, m_mlp_w2,
                             m_ln_ffn_g, m_ln_ffn_b)))
    mom_v = dict(zip(names, (v_meta_tokens, v_ab_w_in, v_ab_conv_w, v_ab_a_log, v_ab_dt_bias, v_ab_gnorm_g, v_ab_w_out,
                             v_c_w_in, v_c_lb_raw, v_c_gnorm_g, v_c_w_out, v_ln_mix_g, v_ln_mix_b, v_mlp_w1, v_mlp_w2,
                             v_ln_ffn_g, v_ln_ffn_b)))
    seq = x.shape[1]
    pad = (-(N_META + seq)) % QB
    xi, yi, ci = _pos()
    chip = 2 * xi + yi

    gat_ab_in, gat_ab_out = _chip_allgather([ab_w_in[0].astype(BF16), ab_w_out[0].astype(BF16)], "gather_weights")
    late = [c_w_in.astype(BF16), c_w_out[0].astype(BF16), mlp_w1.astype(BF16), mlp_w2.astype(BF16)]
    mcols, ccols = meta_tokens.shape[1], ab_conv_w.shape[2]
    place = jnp.concatenate([
        lax.dynamic_update_slice(jnp.zeros((N_META, D), F32), 0.5 * meta_tokens, (0, chip * mcols)),
        _conv_to_rows(lax.dynamic_update_slice(jnp.zeros((CONV_K, CONV_W), F32), 0.5 * ab_conv_w[0], (0, chip * ccols)))],
        axis=0)
    placed = _small_allreduce(place, "gather_meta")
    meta_full = placed[:N_META]

    w = {
        "ab_w_in": _pad_ab_cols(jnp.transpose(gat_ab_in, (1, 0, 2)).reshape(D, AB_TRUE)),
        "conv_w": _rows_to_conv(placed[N_META:]), "a_log": ab_a_log, "dt_bias": ab_dt_bias,
        "ab_gnorm_g": ab_gnorm_g, "ab_w_out": gat_ab_out.reshape(D, D), "c_lb_raw": c_lb_raw,
        "c_gnorm_g": c_gnorm_g, "ln_mix_g": ln_mix_g, "ln_mix_b": ln_mix_b, "ln_ffn_g": ln_ffn_g, "ln_ffn_b": ln_ffn_b,
    }

    h0 = jnp.concatenate([jnp.zeros((pad, D), F32), meta_full, x[0]], axis=0)
    tgt = jnp.concatenate([jnp.zeros((pad + N_META, D), F32), loss_target[0]], axis=0)
    loss8, dh0, g = _local_step(h0, tgt, w, pad, late)
    loss = lax.psum(loss8[0, 0], ("x", "y", "c"))
    grad_x = dh0[pad + N_META:][None]

    gsmall = {"ln_mix_g": g["ln_mix_g"], "ln_mix_b": g["ln_mix_b"], "ln_ffn_g": g["ln_ffn_g"], "ln_ffn_b": g["ln_ffn_b"],
              "c_lb_raw": g["c_lb_raw"], "ab_a_log": g["a_log"], "ab_dt_bias": g["dt_bias"], "ab_gnorm_g": g["ab_gnorm_g"],
              "c_gnorm_g": g["c_gnorm_g"]}
    sbuf = jnp.concatenate([_pack_small(gsmall), dh0[pad:pad + N_META], _conv_to_rows(g["conv_w"])], axis=0)
    ssum = _small_allreduce(sbuf, "allreduce_small")
    grads = _unpack_small(ssum[:SMALL_ROWS], wts)
    grads["meta_tokens"] = lax.dynamic_slice(ssum[SMALL_ROWS:SMALL_ROWS + N_META], (0, chip * mcols), (N_META, mcols))
    grads["ab_conv_w"] = lax.dynamic_slice(_rows_to_conv(ssum[SMALL_ROWS + N_META:]), (0, chip * ccols), (CONV_K, ccols))[None]

    dab = jnp.transpose(_unpad_ab_cols(g["ab_w_in"]).reshape(D, N_CHIP, AB_TRUE // N_CHIP), (1, 0, 2))
    landed = list(_grad_alltoall([_halves(dab)], "grad_alltoall")) + list(g["landed"])
    halves = [_sum_slots(l, f"grad_sum_{i}") for i, l in enumerate(landed)]
    dests = [(0, ()), (1, ()), (2, ()), (3, ()), (4, (0,)), (4, (1,)), (5, (0,)), (5, (1,))]
    out_shapes = [(2,) + halves[i].shape for i in range(4)] + [(2, 2) + halves[4].shape, (2, 2) + halves[6].shape]
    for n, o in zip(BIG, _sibling_exchange(halves, dests, out_shapes, "grad_swap")):
        grads[n] = o.reshape(wts[n].shape)

    delta, new_m, new_v = {}, {}, {}
    for n in ("meta_tokens", "ab_conv_w") + BIG:
        shp = wts[n].shape
        to2 = lambda a: a.reshape(-1, shp[-1])
        d2, m2, v2 = _adamw(to2(wts[n]), to2(grads[n]), to2(mom_m[n]), to2(mom_v[n]), f"adamw_{n}")
        delta[n], new_m[n], new_v[n] = d2.reshape(shp), m2.reshape(shp), v2.reshape(shp)
    d2, m2, v2 = _adamw(_pack_small(wts), ssum[:SMALL_ROWS], _pack_small(mom_m), _pack_small(mom_v), "adamw_small")
    delta.update(_unpack_small(d2, wts))
    new_m.update(_unpack_small(m2, wts))
    new_v.update(_unpack_small(v2, wts))

    return (loss, grad_x, *[grads[n] for n in names], *[delta[n] for n in names], *[new_m[n] for n in names],
            *[new_v[n] for n in names])
```

```python
import functools
import math

import numpy as np
import jax
import jax.numpy as jnp
from jax import lax
from jax.experimental import pallas as pl
from jax.experimental.pallas import tpu as pltpu

F32 = jnp.float32
BF16 = jnp.bfloat16

D = 1024
N_META = 16
D_FF = 4 * D
DEPTH = 2
GDN_H = 4
SB_H = 8
SB_DH = 64
HG_H = 8
HD = 128
CH = 64
QB = 128
ALPHA = float((2 * DEPTH) ** 0.25)
LN_EPS = 1e-5
RMS_EPS = 1e-6
L2_EPS = 1e-6
NEG = -1e30

ADAM_LR = 0.001
ADAM_B1 = 0.9
ADAM_B2 = 0.999
ADAM_EPS = 1e-08
ADAM_WD = 0.01
ADAM_STEP = 10

AB_W = 30 * HD
AB_TRUE = 3592
VMEM_LIMIT = 56 * 1024 * 1024

NN = ((1,), (0,))
NT = ((1,), (1,))
TN = ((0,), (0,))


def _cparams(**kw):
    return pltpu.CompilerParams(vmem_limit_bytes=VMEM_LIMIT, **kw)


def _dg(a, b, dims, mode):
    if mode == "h":
        return lax.dot_general(a, b, dims, precision=lax.Precision.HIGHEST, preferred_element_type=F32)
    if mode == "b":
        return lax.dot_general(a.astype(BF16), b.astype(BF16), dims, preferred_element_type=F32)
    ah, bh = a.astype(BF16), b.astype(BF16)
    al, bl = (a - ah.astype(F32)).astype(BF16), (b - bh.astype(F32)).astype(BF16)
    d = lambda x, y: lax.dot_general(x, y, dims, preferred_element_type=F32)
    return d(ah, bh) + (d(ah, bl) + d(al, bh))


def _make_dots(mode, batched=False):
    if batched:
        nn_d, nt_d, tn_d = (((2,), (1,)), ((0,), (0,))), (((2,), (2,)), ((0,), (0,))), (((1,), (1,)), ((0,), (0,)))
    else:
        nn_d, nt_d, tn_d = (NN, ((), ())), (NT, ((), ())), (TN, ((), ()))

    @jax.custom_vjp
    def nn(a, b):
        return _dg(a, b, nn_d, mode)

    @jax.custom_vjp
    def nt(a, b):
        return _dg(a, b, nt_d, mode)

    @jax.custom_vjp
    def tn(a, b):
        return _dg(a, b, tn_d, mode)

    nn.defvjp(lambda a, b: (nn(a, b), (a, b)), lambda r, g: (nt(g, r[1]), tn(r[0], g)))
    nt.defvjp(lambda a, b: (nt(a, b), (a, b)), lambda r, g: (nn(g, r[1]), tn(g, r[0])))
    tn.defvjp(lambda a, b: (tn(a, b), (a, b)), lambda r, g: (nt(r[1], g), nn(r[0], g)))
    return nn, nt, tn


hnn, hnt, htn = _make_dots("h")
bbnn, bbnt, bbtn = _make_dots("b", True)
mbnn, mbnt, mbtn = _make_dots("m", True)
hbnn, hbnt, hbtn = _make_dots("h", True)


def _heads(a, n):
    return jnp.concatenate([a[None, :, h * HD:(h + 1) * HD] for h in range(n)], axis=0)


def _sigmoid(x):
    return jax.nn.sigmoid(x)


def _silu(x):
    return x * jax.nn.sigmoid(x)


def _softplus(x):
    return jnp.maximum(x, 0.0) + jnp.log(1.0 + jnp.exp(-jnp.abs(x)))


def _iota(shape, dim):
    return lax.broadcasted_iota(jnp.int32, shape, dim)


def _pick(n, prefs):
    for p in prefs:
        if n % p == 0:
            return p
    return n


def _mm(a, b, *, ta=False, tb=False, out_dtype=F32, name, b_view=None, out_split=0, act=False, gate=None):
    if ta:
        k_dim, m_dim = a.shape
    else:
        m_dim, k_dim = a.shape
    if b_view is None:
        w_rows, w_cols = b.shape
    else:
        kind, layer = b_view
        nj, _, blk_r, blk_c = b.shape
        w_rows, w_cols = (blk_r, nj * blk_c) if kind == "cols" else (nj * blk_r, blk_c)
    n_dim = w_rows if tb else w_cols
    assert (w_cols if tb else w_rows) == k_dim
    tm = _pick(m_dim, (1024, 1056, 704, 640, 512, 384, 256, 128))
    tn = _pick(n_dim, (1024, 1056, 704, 640, 512, 384, 256, 128))
    tk = _pick(k_dim, (1024, 1056, 704, 512, 384, 256, 128))
    nk = k_dim // tk
    a_spec = pl.BlockSpec((tk, tm), lambda i, j, k: (k, i)) if ta else pl.BlockSpec((tm, tk), lambda i, j, k: (i, k))
    wb = (tn, tk) if tb else (tk, tn)
    w_idx = (lambda i, j, k: (j, k)) if tb else (lambda i, j, k: (k, j))
    if b_view is None:
        b_spec = pl.BlockSpec(wb, w_idx)
    elif kind == "cols":
        per = blk_c // wb[1]
        b_spec = pl.BlockSpec((None, None) + wb,
                              lambda i, j, k: (w_idx(i, j, k)[1] // per, layer, w_idx(i, j, k)[0], w_idx(i, j, k)[1] % per))
    else:
        per = blk_r // wb[0]
        b_spec = pl.BlockSpec((None, None) + wb,
                              lambda i, j, k: (w_idx(i, j, k)[0] // per, layer, w_idx(i, j, k)[0] % per, w_idx(i, j, k)[1]))
    if out_split:
        per_o = (n_dim // out_split) // tn
        out_spec = pl.BlockSpec((None, tm, tn), lambda i, j, k: (j // per_o, i, j % per_o))
        out_sds = jax.ShapeDtypeStruct((out_split, m_dim, n_dim // out_split), out_dtype)
    else:
        out_spec = pl.BlockSpec((tm, tn), lambda i, j, k: (i, j))
        out_sds = jax.ShapeDtypeStruct((m_dim, n_dim), out_dtype)
    dims = (((0 if ta else 1,), (1 if tb else 0,)), ((), ()))
    extra = [] if gate is None else [gate]
    n_out = 2 if act else 1

    def finish(acc, refs):
        if act:
            refs[0][...] = acc.astype(refs[0].dtype)
            r = jnp.maximum(acc, 0.0)
            refs[1][...] = (r * r).astype(refs[1].dtype)
        elif gate is not None:
            refs[1][...] = (acc * (2.0 * jnp.maximum(refs[0][...].astype(F32), 0.0))).astype(refs[1].dtype)
        else:
            refs[0][...] = acc.astype(refs[0].dtype)

    def body(a_ref, b_ref, *rest):
        refs, acc_ref = rest[:-1], rest[-1]
        part = lax.dot_general(a_ref[...], b_ref[...], dims, preferred_element_type=F32)
        if nk == 1:
            finish(part, refs)
        else:
            k = pl.program_id(2)

            @pl.when(k == 0)
            def _():
                acc_ref[...] = part

            @pl.when(k > 0)
            def _():
                acc_ref[...] += part

            @pl.when(k == nk - 1)
            def _():
                finish(acc_ref[...], refs)

    out = pl.pallas_call(
        body, name=name, grid=(m_dim // tm, n_dim // tn, nk),
        in_specs=[a_spec, b_spec] + [pl.BlockSpec((tm, tn), lambda i, j, k: (i, j))] * len(extra),
        out_specs=[out_spec] * n_out,
        out_shape=[out_sds] * n_out,
        scratch_shapes=[pltpu.VMEM((tm, tn) if nk > 1 else (8, 128), F32)],
        compiler_params=_cparams(dimension_semantics=("parallel", "parallel", "arbitrary")),
    )(a, b, *extra)
    return out if act else out[0]


def _row_tile(t_pad, width):
    for tr in (528, 352, 176, 128, 64):
        if t_pad % tr == 0 and tr * width * 4 <= (3 << 19) and tr % 16 == 0:
            return tr
    return 64 if t_pad % 64 == 0 else t_pad


def _ln_res_fn(h, m, g, b):
    x = ALPHA * h + m
    mu = jnp.mean(x, axis=-1, keepdims=True)
    xc = x - mu
    var = jnp.mean(xc * xc, axis=-1, keepdims=True)
    return xc * lax.rsqrt(var + LN_EPS) * g + b


def _ln_res_fwd(h, m, g, b, name):
    t_pad = h.shape[0]
    tr = _row_tile(t_pad, D)

    def body(h_ref, m_ref, g_ref, b_ref, y_ref, yb_ref):
        y = _ln_res_fn(h_ref[...], m_ref[...], g_ref[...], b_ref[...])
        y_ref[...] = y
        yb_ref[...] = y.astype(BF16)

    row = pl.BlockSpec((tr, D), lambda i: (i, 0))
    par = pl.BlockSpec((1, D), lambda i: (0, 0))
    return pl.pallas_call(
        body, name=name, grid=(t_pad // tr,), in_specs=[row, row, par, par], out_specs=[row, row],
        out_shape=[jax.ShapeDtypeStruct((t_pad, D), F32), jax.ShapeDtypeStruct((t_pad, D), BF16)],
        compiler_params=_cparams(),
    )(h, m, g, b)


def _ln_res_bwd(h, m, g, b, dy1, dy2, name):
    t_pad = h.shape[0]
    tr = _row_tile(t_pad, D)

    def body(h_ref, m_ref, g_ref, b_ref, d1_ref, d2_ref, dh_ref, dm_ref, dg_ref, db_ref):
        _, vjp = jax.vjp(_ln_res_fn, h_ref[...], m_ref[...], g_ref[...], b_ref[...])
        dh, dm, dg, db = vjp(d1_ref[...] + d2_ref[...])
        dh_ref[...] = dh
        dm_ref[...] = dm.astype(BF16)

        @pl.when(pl.program_id(0) == 0)
        def _():
            dg_ref[...] = jnp.zeros_like(dg_ref)
            db_ref[...] = jnp.zeros_like(db_ref)

        dg_ref[...] += dg
        db_ref[...] += db

    row = pl.BlockSpec((tr, D), lambda i: (i, 0))
    par = pl.BlockSpec((1, D), lambda i: (0, 0))
    return pl.pallas_call(
        body, name=name, grid=(t_pad // tr,), in_specs=[row, row, par, par, row, row],
        out_specs=[row, row, par, par],
        out_shape=[jax.ShapeDtypeStruct((t_pad, D), F32), jax.ShapeDtypeStruct((t_pad, D), BF16),
                   jax.ShapeDtypeStruct((1, D), F32), jax.ShapeDtypeStruct((1, D), F32)],
        compiler_params=_cparams(),
    )(h, m, g, b, dy1, dy2)


def _grms_fn(o, z, g):
    y = o * lax.rsqrt(jnp.mean(o * o, axis=-1, keepdims=True) + RMS_EPS) * g
    return y * _silu(z)


def _grms_fwd(o, z_arr, z_blk0, g, name):
    t_pad, w = o.shape
    nh = w // HD
    tr = _row_tile(t_pad, HD * 4)

    def body(o_ref, z_ref, g_ref, y_ref):
        y_ref[...] = _grms_fn(o_ref[...], z_ref[...], g_ref[...]).astype(BF16)

    return pl.pallas_call(
        body, name=name, grid=(t_pad // tr, nh),
        in_specs=[pl.BlockSpec((tr, HD), lambda i, h: (i, h)),
                  pl.BlockSpec((tr, HD), lambda i, h: (i, z_blk0 + h)),
                  pl.BlockSpec((1, HD), lambda i, h: (0, 0))],
        out_specs=pl.BlockSpec((tr, HD), lambda i, h: (i, h)),
        out_shape=jax.ShapeDtypeStruct((t_pad, w), BF16), compiler_params=_cparams(),
    )(o, z_arr, g)


def _grms_bwd(o, z_arr, z_blk0, g, dy_arr, dy_blk0, name):
    t_pad, w = o.shape
    nh = w // HD
    tr = _row_tile(t_pad, HD * 4)

    def body(o_ref, z_ref, g_ref, dy_ref, do_ref, dz_ref, dg_ref):
        _, vjp = jax.vjp(_grms_fn, o_ref[...], z_ref[...], g_ref[...])
        do, dz, dg = vjp(dy_ref[...])
        do_ref[...] = do
        dz_ref[...] = dz

        @pl.when((pl.program_id(0) == 0) & (pl.program_id(1) == 0))
        def _():
            dg_ref[...] = jnp.zeros_like(dg_ref)

        dg_ref[...] += dg

    blk = pl.BlockSpec((tr, HD), lambda i, h: (i, h))
    return pl.pallas_call(
        body, name=name, grid=(t_pad // tr, nh),
        in_specs=[blk, pl.BlockSpec((tr, HD), lambda i, h: (i, z_blk0 + h)),
                  pl.BlockSpec((1, HD), lambda i, h: (0, 0)),
                  pl.BlockSpec((tr, HD), lambda i, h: (i, dy_blk0 + h))],
        out_specs=[blk, blk, pl.BlockSpec((1, HD), lambda i, h: (0, 0))],
        out_shape=[jax.ShapeDtypeStruct((t_pad, w), F32), jax.ShapeDtypeStruct((t_pad, w), F32),
                   jax.ShapeDtypeStruct((1, HD), F32)],
        compiler_params=_cparams(),
    )(o, z_arr, g, dy_arr)


def _loss_fwd(y, tgt, first_row, name):
    t_pad = y.shape[0]
    tr = _row_tile(t_pad, D)

    def body(y_ref, t_ref, l_ref, dy_ref):
        rows = pl.program_id(0) * tr + _iota((tr, 1), 0)
        err = jnp.where(rows >= first_row, y_ref[...] - t_ref[...], 0.0)
        dy_ref[...] = err * (1.0 / D)

        @pl.when(pl.program_id(0) == 0)
        def _():
            l_ref[...] = jnp.zeros_like(l_ref)

        part = jnp.sum(jnp.sum(err * err, axis=1, keepdims=True), axis=0, keepdims=True)
        l_ref[...] += jnp.broadcast_to(part * (0.5 / D), l_ref.shape)

    row = pl.BlockSpec((tr, D), lambda i: (i, 0))
    return pl.pallas_call(
        body, name=name, grid=(t_pad // tr,), in_specs=[row, row],
        out_specs=[pl.BlockSpec((8, 128), lambda i: (0, 0)), row],
        out_shape=[jax.ShapeDtypeStruct((8, 128), F32), jax.ShapeDtypeStruct((t_pad, D), F32)],
        compiler_params=_cparams(),
    )(y, tgt)


def _add2(a, b, name):
    t_pad, w = a.shape
    tr = _row_tile(t_pad, w)

    def body(a_ref, b_ref, o_ref):
        o_ref[...] = a_ref[...] + b_ref[...]

    row = pl.BlockSpec((tr, w), lambda i: (i, 0))
    return pl.pallas_call(body, name=name, grid=(t_pad // tr,), in_specs=[row, row], out_specs=row,
                          out_shape=jax.ShapeDtypeStruct((t_pad, w), F32), compiler_params=_cparams())(a, b)


def _assemble_bf16(parts, name):
    t_pad = parts[0][0].shape[0] if parts[0][1] == "cols" else parts[0][0].shape[1]
    widths = [p.shape[1] if kind == "cols" else HD for p, kind in parts]
    total = sum(widths)
    tr = _row_tile(t_pad, total)

    def body(*refs):
        o_ref = refs[-1]
        off = 0
        for ref, (p, kind), w in zip(refs[:-1], parts, widths):
            if kind == "cols":
                o_ref[:, off:off + w] = ref[...].astype(BF16)
            else:
                acc = ref[0]
                for hh in range(1, p.shape[0]):
                    acc = acc + ref[hh]
                o_ref[:, off:off + w] = acc.astype(BF16)
            off += w

    specs = []
    for p, kind in parts:
        if kind == "cols":
            specs.append(pl.BlockSpec((tr, p.shape[1]), lambda i: (i, 0)))
        else:
            specs.append(pl.BlockSpec((p.shape[0], tr, HD), lambda i: (0, i, 0)))
    return pl.pallas_call(
        body, name=name, grid=(t_pad // tr,), in_specs=specs,
        out_specs=pl.BlockSpec((tr, total), lambda i: (i, 0)),
        out_shape=jax.ShapeDtypeStruct((t_pad, total), BF16), compiler_params=_cparams(),
    )(*[p for p, _ in parts])


CONV_K = 4
HALO = 8
RT = 128


def _conv_fwd(p, blk0, w, mode, pad, name):
    t_pad = p.shape[0]
    nt = t_pad // RT
    scale = HD ** -0.5 if mode == "q" else 1.0

    def body(x_ref, w_ref, y_ref, xs_ref):
        xs_ref[0:HALO, :] = jnp.zeros((HALO, HD), F32)
        rows = _iota((t_pad, 1), 0)
        xs_ref[HALO:HALO + t_pad, :] = jnp.where(rows >= pad, x_ref[...], 0.0)
        wv = w_ref[...]

        def tile(i, carry):
            r0 = pl.multiple_of(i * RT, RT)
            ext = xs_ref[pl.ds(r0, RT + HALO), :]
            acc = ext[HALO:, :] * wv[3:4, :]
            for s in (1, 2, 3):
                acc = acc + pltpu.roll(ext, s, 0)[HALO:, :] * wv[3 - s:4 - s, :]
            y = _silu(acc)
            if mode != "v":
                y = y * lax.rsqrt(jnp.sum(y * y, axis=-1, keepdims=True) + L2_EPS) * scale
            y_ref[pl.ds(r0, RT), :] = y
            return carry

        lax.fori_loop(0, nt, tile, 0)

    return pl.pallas_call(
        body, name=name, grid=(GDN_H,),
        in_specs=[pl.BlockSpec((t_pad, HD), lambda h: (0, blk0 + h)), pl.BlockSpec((CONV_K, HD), lambda h: (0, h))],
        out_specs=pl.BlockSpec((t_pad, HD), lambda h: (0, h)),
        out_shape=jax.ShapeDtypeStruct((t_pad, GDN_H * HD), F32),
        scratch_shapes=[pltpu.VMEM((t_pad + HALO, HD), F32)],
        compiler_params=_cparams(),
    )(p, w)


def _conv_bwd(p, blk0, w, dn, mode, pad, name):
    t_pad = p.shape[0]
    nt = t_pad // RT
    scale = HD ** -0.5 if mode == "q" else 1.0

    def body(x_ref, w_ref, dn_ref, dx_ref, dw_ref, xs_ref, ds_ref):
        xs_ref[0:HALO, :] = jnp.zeros((HALO, HD), F32)
        xs_ref[HALO + t_pad:HALO + t_pad + 2 * HALO, :] = jnp.zeros((2 * HALO, HD), F32)
        ds_ref[t_pad:t_pad + HALO, :] = jnp.zeros((HALO, HD), F32)
        rows = _iota((t_pad, 1), 0)
        xs_ref[HALO:HALO + t_pad, :] = jnp.where(rows >= pad, x_ref[...], 0.0)
        ds_ref[0:t_pad, :] = dn_ref[...]
        wv = w_ref[...]

        def tile(i, dw):
            r0 = pl.multiple_of(i * RT, RT)
            ext = xs_ref[pl.ds(r0, RT + 2 * HALO), :]
            dn_e = ds_ref[pl.ds(r0, RT + HALO), :]
            xsh = [ext[HALO:, :]] + [pltpu.roll(ext, s, 0)[HALO:, :] for s in (1, 2, 3)]
            pre = xsh[0] * wv[3:4, :]
            for s in (1, 2, 3):
                pre = pre + xsh[s] * wv[3 - s:4 - s, :]
            sg = _sigmoid(pre)
            y = pre * sg
            if mode != "v":
                ss = jnp.sum(y * y, axis=-1, keepdims=True) + L2_EPS
                r = lax.rsqrt(ss)
                dy = scale * (dn_e * r - y * (r * r * r) * jnp.sum(dn_e * y, axis=-1, keepdims=True))
            else:
                dy = dn_e
            dpre = dy * (sg * (1.0 + pre * (1.0 - sg)))
            dx = dpre[:RT, :] * wv[3:4, :]
            for s in (1, 2, 3):
                dx = dx + pltpu.roll(dpre, RT + HALO - s, 0)[:RT, :] * wv[3 - s:4 - s, :]
            trow = r0 + _iota((RT, 1), 0)
            dx_ref[pl.ds(r0, RT), :] = jnp.where(trow >= pad, dx, 0.0)
            new = []
            for s in (0, 1, 2, 3):
                new.append(dw[s] + jnp.sum(dpre[:RT, :] * xsh[s][:RT, :], axis=0, keepdims=True))
            return tuple(new)

        z = jnp.zeros((1, HD), F32)
        dw = lax.fori_loop(0, nt, tile, (z, z, z, z))
        for s in (0, 1, 2, 3):
            dw_ref[3 - s:4 - s, :] = dw[s]

    return pl.pallas_call(
        body, name=name, grid=(GDN_H,),
        in_specs=[pl.BlockSpec((t_pad, HD), lambda h: (0, blk0 + h)), pl.BlockSpec((CONV_K, HD), lambda h: (0, h)),
                  pl.BlockSpec((t_pad, HD), lambda h: (0, h))],
        out_specs=[pl.BlockSpec((t_pad, HD), lambda h: (0, h)), pl.BlockSpec((CONV_K, HD), lambda h: (0, h))],
        out_shape=[jax.ShapeDtypeStruct((t_pad, GDN_H * HD), F32), jax.ShapeDtypeStruct((CONV_K, GDN_H * HD), F32)],
        scratch_shapes=[pltpu.VMEM((t_pad + 3 * HALO, HD), F32), pltpu.VMEM((t_pad + HALO, HD), F32)],
        compiler_params=_cparams(),
    )(p, w, dn)


def _unit_lower_inv(m, bd, eye):
    md = m * bd
    low = m - md
    p2 = mbnn(md, md)
    p4 = mbnn(p2, p2)
    p8 = mbnn(p4, p4)
    dinv = mbnn(mbnn(mbnn(eye - md, eye + p2), eye + p4), eye + p8)
    n = mbnn(dinv, low)
    n2 = mbnn(n, n)
    return mbnn(mbnn(eye - n, eye + n2), dinv)


def _gdn_chunk(q, k, v, bb, aa, alog, dtb, s, valid):
    nh = q.shape[0]
    ri = _iota((1, CH, CH), 1)
    ci = _iota((1, CH, CH), 2)
    causal = ri >= ci
    strict = ri > ci
    eye = (ri == ci).astype(F32)
    bd = ((ri >> 4) == (ci >> 4)).astype(F32)
    ltri = (_iota((CH, CH), 0) >= _iota((CH, CH), 1)).astype(F32)
    sel = (_iota((nh, 1, HD), 2) == _iota((nh, 1, HD), 0)).astype(F32)

    beta_all = jnp.where(valid, _sigmoid(bb), 0.0)
    g_all = jnp.where(valid, -jnp.exp(alog) * _softplus(aa + dtb), 0.0)
    gc_all = hnn(ltri, g_all)
    beta = jnp.sum(beta_all[None] * sel, axis=2, keepdims=True)
    gc = jnp.sum(gc_all[None] * sel, axis=2, keepdims=True)
    gc_rows = hbnt(jnp.broadcast_to(sel, (nh, CH, HD)), jnp.broadcast_to(gc_all[None], (nh, CH, HD)))
    last = _iota((1, CH, 1), 1) == CH - 1
    gc_last = jnp.sum(jnp.where(last, gc, 0.0), axis=1, keepdims=True)
    decay = jnp.exp(jnp.where(causal, gc - gc_rows, NEG))
    egc = jnp.exp(gc)

    kb = k * beta
    m = jnp.where(strict, bbnt(kb, k) * decay, 0.0)
    t_inv = _unit_lower_inv(m, bd, eye)
    u = bbnn(t_inv, v * beta)
    w = bbnn(t_inv, kb * egc)
    a_intra = bbnt(q, k) * decay
    q_dec = q * egc
    k_dec = k * jnp.exp(gc_last - gc)
    v_new = u - bbnn(w, s)
    o = bbnn(q_dec, s) + bbnn(a_intra, v_new)
    s_new = s * jnp.exp(gc_last) + bbtn(k_dec, v_new)
    return o, s_new


def _gdn_specs(nc, rev):
    cc = (lambda c: nc - 1 - c) if rev else (lambda c: c)
    wide = pl.BlockSpec((CH, GDN_H * HD), lambda c: (cc(c), 0))
    fix = lambda off: pl.BlockSpec((CH, HD), lambda c: (cc(c), off))
    par = pl.BlockSpec((1, HD), lambda c: (0, 0))
    state = pl.BlockSpec((1, GDN_H, HD, HD), lambda c: (cc(c), 0, 0, 0))
    return wide, fix, par, state


def _store_heads(ref, a):
    for h in range(a.shape[0]):
        ref[:, h * HD:(h + 1) * HD] = a[h]


def _gdn_fwd(qn, kn, vn, p, alog, dtb, pad, name):
    t_pad = qn.shape[0]
    nc = t_pad // CH
    wide, fix, par, state = _gdn_specs(nc, False)

    def body(q_ref, k_ref, v_ref, bb_ref, aa_ref, al_ref, dt_ref, o_ref, ss_ref, s_ref):
        c = pl.program_id(0)

        @pl.when(c == 0)
        def _():
            s_ref[...] = jnp.zeros_like(s_ref)

        s = s_ref[...]
        ss_ref[0] = s
        valid = (c * CH + _iota((CH, 1), 0)) >= pad
        o, s_new = _gdn_chunk(_heads(q_ref[...], GDN_H), _heads(k_ref[...], GDN_H), _heads(v_ref[...], GDN_H),
                              bb_ref[...], aa_ref[...], al_ref[...], dt_ref[...], s, valid)
        _store_heads(o_ref, o)
        s_ref[...] = s_new

    return pl.pallas_call(
        body, name=name, grid=(nc,),
        in_specs=[wide, wide, wide, fix(16), fix(17), par, par],
        out_specs=[wide, state],
        out_shape=[jax.ShapeDtypeStruct((t_pad, GDN_H * HD), F32), jax.ShapeDtypeStruct((nc, GDN_H, HD, HD), F32)],
        scratch_shapes=[pltpu.VMEM((GDN_H, HD, HD), F32)],
        compiler_params=_cparams(),
    )(qn, kn, vn, p, p, alog, dtb)


def _gdn_bwd(qn, kn, vn, p, alog, dtb, ssave, do, pad, name, cargo=(), exchange=None):
    t_pad = qn.shape[0]
    nc = t_pad // CH
    wide, fix, par, state = _gdn_specs(nc, True)
    n = len(cargo)

    def body(q_ref, k_ref, v_ref, bb_ref, aa_ref, al_ref, dt_ref, ss_ref, do_ref, *rest):
        c = pl.program_id(0)
        ds_ref = rest[-1]
        (dq_ref, dk_ref, dv_ref, dbb_ref, daa_ref, dal_ref, ddt_ref), end_cargo = _cargo_bounds(
            rest[:-1], n, 7, exchange, c == 0, c == nc - 1)

        @pl.when(c == 0)
        def _():
            ds_ref[...] = jnp.zeros_like(ds_ref)
            dal_ref[...] = jnp.zeros_like(dal_ref)
            ddt_ref[...] = jnp.zeros_like(ddt_ref)

        valid = ((nc - 1 - c) * CH + _iota((CH, 1), 0)) >= pad
        fn = lambda q, k, v, bb, aa, al, dt, s: _gdn_chunk(q, k, v, bb, aa, al, dt, s, valid)
        _, vjp = jax.vjp(fn, _heads(q_ref[...], GDN_H), _heads(k_ref[...], GDN_H), _heads(v_ref[...], GDN_H),
                         bb_ref[...], aa_ref[...], al_ref[...], dt_ref[...], ss_ref[0])
        dq, dk, dv, dbb, daa, dal, ddt, ds = vjp((_heads(do_ref[...], GDN_H), ds_ref[...]))
        _store_heads(dq_ref, dq)
        _store_heads(dk_ref, dk)
        _store_heads(dv_ref, dv)
        dbb_ref[...] = dbb
        daa_ref[...] = daa
        dal_ref[...] += dal
        ddt_ref[...] += ddt
        ds_ref[...] = ds
        end_cargo()

    sds = jax.ShapeDtypeStruct
    return pl.pallas_call(
        body, name=name, grid=(nc,),
        in_specs=[wide, wide, wide, fix(16), fix(17), par, par, state, wide] + [ANY] * n,
        out_specs=[wide, wide, wide, fix(0), fix(0), par, par] + [ANY] * n,
        out_shape=[sds((t_pad, GDN_H * HD), F32)] * 3 + [sds((t_pad, HD), F32)] * 2 + [sds((1, HD), F32)] * 2
        + (exchange[1](cargo) if n else []),
        scratch_shapes=(exchange[2](n) if n else []) + [pltpu.VMEM((GDN_H, HD, HD), F32)],
        compiler_params=_cparams(),
    )(qn, kn, vn, p, p, alog, dtb, ssave, do, *cargo)


SB_Q0, SB_K0, SB_V0 = 18, 22, 26
SB_SCALE = SB_DH ** -0.5
SB_NB = 4


def _sb_terms(z, allowed):
    sp = jnp.log(1.0 + jnp.exp(-jnp.abs(z)))
    l1m = jnp.where(allowed, -jnp.maximum(z, 0.0) - sp, 0.0)
    ls = jnp.minimum(z, 0.0) - sp
    return l1m, ls


def _sb_stack(a, i):
    first = _iota((1, HD), 1) < SB_DH
    a2 = jnp.concatenate([jnp.where(first, a, 0.0), jnp.where(first, 0.0, a)], axis=0).astype(BF16)
    rq = i * QB + _iota((QB, 1), 0)
    return a2, jnp.concatenate([rq, rq], axis=0), first


def _dot_hi_lo(a, b2):
    hi = a.astype(BF16)
    lo = (a - hi.astype(F32)).astype(BF16)
    return lax.dot_general(jnp.concatenate([hi, lo], axis=1), b2, (NN, ((), ())), preferred_element_type=F32)


def _cargo_bounds(refs, n, n_out, exchange, first, last):
    outs = refs[n:n + n_out]
    if not n:
        return outs, lambda: None
    ex = exchange[0](refs[:n], refs[n + n_out:2 * n + n_out], *refs[2 * n + n_out:])

    @pl.when(first)
    def _():
        ex.start()

    def finish():
        @pl.when(last)
        def _():
            ex.wait()

    return outs, finish


def _sb_fwd(p, pad, name, cargo=(), exchange=None):
    t_pad = p.shape[0]
    nq = t_pad // QB
    n = len(cargo)

    def body(q_ref, k_ref, v_ref, *rest):
        i = pl.program_id(1)
        pr = pl.program_id(0)
        (o_ref, r_ref), end_cargo = _cargo_bounds(rest, n, 2, exchange, (pr == 0) & (i == 0),
                                                  (pr == SB_H // 2 - 1) & (i == nq - 1))
        q2, rowq, first = _sb_stack(q_ref[...] * SB_SCALE, i)
        tri = (_iota((QB, QB), 0) > _iota((QB, QB), 1)).astype(BF16)
        upper2 = jnp.concatenate([tri, tri], axis=0)

        def chain(kb, live):
            start = pl.multiple_of(kb * QB, QB)
            kblk = k_ref[pl.ds(start, QB), :].astype(BF16)
            vblk = v_ref[pl.ds(start, QB), :].astype(BF16)
            z = lax.dot_general(q2, kblk, (NT, ((), ())), preferred_element_type=F32)
            colk = kb * QB + _iota((1, QB), 1)
            allowed = (colk < rowq) & (colk >= pad) & live
            l1m, ls = _sb_terms(z, allowed)
            return allowed, ls, _dot_hi_lo(l1m, upper2), jnp.sum(l1m, axis=1, keepdims=True), vblk

        def step(j, carry):
            o_acc, run = carry
            ws, vs = [], []
            for n in range(SB_NB):
                kb = i - SB_NB * j - n
                allowed, ls, suf, rs, vblk = chain(jnp.maximum(kb, 0), kb >= 0)
                ws.append(jnp.where(allowed, jnp.exp(ls + suf + run), 0.0).astype(BF16))
                vs.append(vblk)
                run = run + rs
            o_acc = o_acc + lax.dot_general(jnp.concatenate(ws, axis=1), jnp.concatenate(vs, axis=0),
                                            (NN, ((), ())), preferred_element_type=F32)
            return o_acc, run

        o_acc, run = lax.fori_loop(0, (i + SB_NB) // SB_NB, step,
                                   (jnp.zeros((2 * QB, HD), F32), jnp.zeros((2 * QB, 1), F32)))
        o_ref[...] = jnp.where(first, o_acc[:QB], o_acc[QB:]).astype(BF16)
        r_ref[...] = jnp.where(first, run[:QB], run[QB:])
        end_cargo()

    full = lambda off: pl.BlockSpec((t_pad, HD), lambda pr, i: (0, off + pr))
    blk = pl.BlockSpec((QB, HD), lambda pr, i: (i, pr))
    return pl.pallas_call(
        body, name=name, grid=(SB_H // 2, nq),
        in_specs=[pl.BlockSpec((QB, HD), lambda pr, i: (i, SB_Q0 + pr)), full(SB_K0), full(SB_V0)] + [ANY] * n,
        out_specs=[blk, blk] + [ANY] * n,
        out_shape=[jax.ShapeDtypeStruct((t_pad, SB_H * SB_DH), BF16), jax.ShapeDtypeStruct((t_pad, SB_H * SB_DH), F32)]
        + (exchange[1](cargo) if n else []),
        scratch_shapes=exchange[2](n) if n else [],
        compiler_params=_cparams(),
    )(p, p, p, *cargo)


def _sb_bwd(p, rtot, dy, dy_blk0, pad, name, cargo=(), exchange=None):
    t_pad = p.shape[0]
    nq = t_pad // QB
    n = len(cargo)

    def body(q_ref, k_ref, v_ref, r_ref, do_ref, *rest):
        i = pl.program_id(1)
        pr = pl.program_id(0)
        (dq_ref, dk_ref, dv_ref), end_cargo = _cargo_bounds(rest, n, 3, exchange, (pr == 0) & (i == 0),
                                                            (pr == SB_H // 2 - 1) & (i == nq - 1))

        @pl.when(i == 0)
        def _():
            dk_ref[...] = jnp.zeros_like(dk_ref)
            dv_ref[...] = jnp.zeros_like(dv_ref)

        q2, rowq, first = _sb_stack(q_ref[...] * SB_SCALE, i)
        do2, _, _ = _sb_stack(do_ref[...], i)
        rt = r_ref[...]
        lane = _iota((1, HD), 1)
        rcol = jnp.concatenate([jnp.sum(jnp.where(lane == 0, rt, 0.0), axis=1, keepdims=True),
                                jnp.sum(jnp.where(lane == SB_DH, rt, 0.0), axis=1, keepdims=True)], axis=0)
        rj = _iota((QB, QB), 0)
        cs = _iota((QB, QB), 1)
        tri_u = (rj > cs).astype(BF16)
        tri_l = (rj < cs).astype(BF16)
        upper2 = jnp.concatenate([tri_u, tri_u], axis=0)
        lower2 = jnp.concatenate([tri_l, tri_l], axis=0)

        def chain(kb, live):
            start = pl.multiple_of(kb * QB, QB)
            kblk = k_ref[pl.ds(start, QB), :].astype(BF16)
            vblk = v_ref[pl.ds(start, QB), :].astype(BF16)
            z = lax.dot_general(q2, kblk, (NT, ((), ())), preferred_element_type=F32)
            colk = kb * QB + _iota((1, QB), 1)
            allowed = (colk < rowq) & (colk >= pad) & live
            l1m, ls = _sb_terms(z, allowed)
            dwgt = lax.dot_general(do2, vblk, (NT, ((), ())), preferred_element_type=F32)
            return (start, kblk, allowed, ls, _dot_hi_lo(l1m, upper2), jnp.sum(l1m, axis=1, keepdims=True), dwgt,
                    _sigmoid(z))

        def finish(c, seen, gseen):
            start, kblk, allowed, ls, suf, rs, dwgt, sg = c
            wgt = jnp.where(allowed, jnp.exp(ls + suf + (rcol - seen - rs)), 0.0)
            dl = dwgt * wgt
            gpre = gseen + _dot_hi_lo(dl, lower2)
            dz = jnp.where(allowed, dl * (1.0 - sg) - gpre * sg, 0.0).astype(BF16)
            dk_ref[pl.ds(start, QB), :] += lax.dot_general(dz, q2, (TN, ((), ())), preferred_element_type=F32)
            dv_ref[pl.ds(start, QB), :] += lax.dot_general(wgt.astype(BF16), do2, (TN, ((), ())),
                                                           preferred_element_type=F32)
            return dz, seen + rs, gseen + jnp.sum(dl, axis=1, keepdims=True)

        def step(j, carry):
            dq_acc, seen, gseen = carry
            cs_ = [chain(jnp.minimum(SB_NB * j + n, i), SB_NB * j + n <= i) for n in range(SB_NB)]
            dzs = []
            for c in cs_:
                dz, seen, gseen = finish(c, seen, gseen)
                dzs.append(dz)
            dq_acc = dq_acc + lax.dot_general(jnp.concatenate(dzs, axis=1), jnp.concatenate([c[1] for c in cs_], axis=0),
                                              (NN, ((), ())), preferred_element_type=F32)
            return dq_acc, seen, gseen

        zc = jnp.zeros((2 * QB, 1), F32)
        dq_acc, _, _ = lax.fori_loop(0, (i + SB_NB) // SB_NB, step, (jnp.zeros((2 * QB, HD), F32), zc, zc))
        dq_ref[...] = jnp.where(first, dq_acc[:QB], dq_acc[QB:]) * SB_SCALE
        end_cargo()

    full_in = lambda off: pl.BlockSpec((t_pad, HD), lambda pr, i: (0, off + pr))
    full_out = pl.BlockSpec((t_pad, HD), lambda pr, i: (0, pr))
    blk = pl.BlockSpec((QB, HD), lambda pr, i: (i, pr))
    sds = jax.ShapeDtypeStruct((t_pad, SB_H * SB_DH), F32)
    return pl.pallas_call(
        body, name=name, grid=(SB_H // 2, nq),
        in_specs=[pl.BlockSpec((QB, HD), lambda pr, i: (i, SB_Q0 + pr)), full_in(SB_K0), full_in(SB_V0), blk,
                  pl.BlockSpec((QB, HD), lambda pr, i: (i, dy_blk0 + pr))] + [ANY] * n,
        out_specs=[blk, full_out, full_out] + [ANY] * n,
        out_shape=[sds, sds, sds] + (exchange[1](cargo) if n else []),
        scratch_shapes=exchange[2](n) if n else [],
        compiler_params=_cparams(),
    )(p, p, p, rtot, dy, *cargo)


HG_LEVELS = 6


def _hg_prefix_matrix():
    t = np.arange(CH)[:, None]
    j = np.arange(CH)[None, :]
    groups = [(j <= t)]
    for lvl in range(1, HG_LEVELS + 1):
        half = CH >> lvl
        e = (t // (2 * half)) * (2 * half) + half - 1
        groups.append(j <= e)
    groups.append(np.ones((2 * CH, CH), bool))
    return np.concatenate(groups, axis=0).astype(np.float32)


HG_G = 4


def _hg_chunk(qr, fr, iv, r0, r1, st, valid, ecat):
    g = st.shape[0]
    mx = jnp.maximum(r0, r1)
    e0 = jnp.exp(r0 - mx)
    e1 = jnp.exp(r1 - mx)
    lb = e1 / (e0 + e1)
    fg = lb + (1.0 - lb) * _sigmoid(fr)
    logf = jnp.where(valid, jnp.log(fg), 0.0)
    kk = jnp.where(valid, 1.0 - fg, 0.0)
    q = jnp.where(valid, _silu(qr), 0.0)
    v = _heads(jnp.where(valid, iv, 0.0), g)

    pre = hnn(ecat, logf)
    b = pre[0:CH]
    b_last = pre[(HG_LEVELS + 1) * CH:]
    row = _iota((CH, 1), 0)
    ri = _iota((1, CH, CH), 1)
    ci = _iota((1, CH, CH), 2)
    a = jnp.where(ri == ci, jnp.sum(_heads(q * kk, g), axis=2, keepdims=True), 0.0)
    for lvl in range(1, HG_LEVELS + 1):
        half = CH >> lvl
        m = pre[lvl * CH:(lvl + 1) * CH]
        low = (row & half) != 0
        qt = jnp.where(low, q * jnp.exp(jnp.where(low, b - m, 0.0)), 0.0)
        kt = jnp.where(low, 0.0, kk * jnp.exp(jnp.where(low, 0.0, m - b)))
        same = (ri >> (7 - lvl)) == (ci >> (7 - lvl))
        a = a + jnp.where(same, bbnt(_heads(qt, g), _heads(kt, g)), 0.0)
    o = bbnt(_heads(q * jnp.exp(b), g), st) + bbnn(a, v)
    kd = kk * jnp.exp(b_last[0:CH] - b)
    st_new = st * _heads(jnp.exp(b_last), g) + bbtn(v, _heads(kd, g))
    return o, st_new


def _hg_specs(nc, rev):
    cc = (lambda c: nc - 1 - c) if rev else (lambda c: c)
    ng = HG_H // HG_G
    blk = lambda off: pl.BlockSpec((CH, HG_G * HD), lambda h, c: (cc(c), off * ng + h))
    lbs = pl.BlockSpec((2, HG_G * HD), lambda h, c: (0, h))
    state = pl.BlockSpec((1, HG_G, HD, HD), lambda h, c: (cc(c), h, 0, 0))
    return ng, blk, lbs, state


def _hg_fwd(p, lbraw, ecat, pad, name):
    t_pad = p.shape[0]
    nc = t_pad // CH
    ng, blk, lbs, state = _hg_specs(nc, False)

    def body(q_ref, f_ref, i_ref, lb_ref, e_ref, o_ref, ss_ref, s_ref):
        c = pl.program_id(1)

        @pl.when(c == 0)
        def _():
            s_ref[...] = jnp.zeros_like(s_ref)

        st = s_ref[...]
        ss_ref[0] = st
        valid = (c * CH + _iota((CH, 1), 0)) >= pad
        o, st_new = _hg_chunk(q_ref[...], f_ref[...], i_ref[...], lb_ref[0:1, :], lb_ref[1:2, :], st, valid, e_ref[...])
        _store_heads(o_ref, o)
        s_ref[...] = st_new

    return pl.pallas_call(
        body, name=name, grid=(ng, nc),
        in_specs=[blk(0), blk(1), blk(2), lbs, pl.BlockSpec(ecat.shape, lambda h, c: (0, 0))],
        out_specs=[blk(0), state],
        out_shape=[jax.ShapeDtypeStruct((t_pad, HG_H * HD), F32), jax.ShapeDtypeStruct((nc, HG_H, HD, HD), F32)],
        scratch_shapes=[pltpu.VMEM((HG_G, HD, HD), F32)],
        compiler_params=_cparams(),
    )(p, p, p, lbraw, ecat)


def _hg_bwd(p, lbraw, ecat, ssave, do, pad, name, cargo=(), exchange=None):
    t_pad = p.shape[0]
    nc = t_pad // CH
    ng, blk, lbs, state = _hg_specs(nc, True)
    n = len(cargo)

    def body(q_ref, f_ref, i_ref, lb_ref, e_ref, ss_ref, do_ref, *rest):
        c = pl.program_id(1)
        hg = pl.program_id(0)
        ds_ref = rest[-1]
        (dq_ref, df_ref, di_ref, dlb_ref), end_cargo = _cargo_bounds(
            rest[:-1], n, 4, exchange, (hg == 0) & (c == 0), (hg == ng - 1) & (c == nc - 1))

        @pl.when(c == 0)
        def _():
            ds_ref[...] = jnp.zeros_like(ds_ref)
            dlb_ref[...] = jnp.zeros_like(dlb_ref)

        valid = ((nc - 1 - c) * CH + _iota((CH, 1), 0)) >= pad
        ecv = e_ref[...]
        fn = lambda qr, fr, iv, r0, r1, st: _hg_chunk(qr, fr, iv, r0, r1, st, valid, ecv)
        _, vjp = jax.vjp(fn, q_ref[...], f_ref[...], i_ref[...], lb_ref[0:1, :], lb_ref[1:2, :], ss_ref[0])
        dq, df, di, d0, d1, ds = vjp((_heads(do_ref[...], HG_G), ds_ref[...]))
        dq_ref[...] = dq
        df_ref[...] = df
        di_ref[...] = di
        dlb_ref[0:1, :] += d0
        dlb_ref[1:2, :] += d1
        ds_ref[...] = ds
        end_cargo()

    sds = jax.ShapeDtypeStruct((t_pad, HG_H * HD), F32)
    return pl.pallas_call(
        body, name=name, grid=(ng, nc),
        in_specs=[blk(0), blk(1), blk(2), lbs, pl.BlockSpec(ecat.shape, lambda h, c: (0, 0)), state, blk(0)] + [ANY] * n,
        out_specs=[blk(0), blk(0), blk(0), lbs] + [ANY] * n,
        out_shape=[sds, sds, sds, jax.ShapeDtypeStruct((2, HG_H * HD), F32)] + (exchange[1](cargo) if n else []),
        scratch_shapes=(exchange[2](n) if n else []) + [pltpu.VMEM((HG_G, HD, HD), F32)],
        compiler_params=_cparams(),
    )(p, p, p, lbraw, ecat, ssave, do, *cargo)


def _pad_ab_cols(w):
    z = jnp.zeros((w.shape[0], HD - GDN_H), w.dtype)
    return jnp.concatenate([w[:, :2048], w[:, 2048:2052], z, w[:, 2052:2056], z, w[:, 2056:]], axis=1)


def _unpad_ab_cols(w):
    return jnp.concatenate([w[:, :2048], w[:, 2048:2052], w[:, 2176:2180], w[:, 2304:]], axis=1)


def _lane_pad(v):
    return jnp.pad(v, ((0, 0), (0, HD - v.shape[1])))


def _mlp_fwd(hb, w1, w2, layer):
    a, r = _mm(hb, w1, b_view=("cols", layer), out_dtype=BF16, act=True, name=f"mlp_up_{layer}")
    m = _mm(r, w2, b_view=("rows", layer), name=f"mlp_down_{layer}")
    return a, r, m


def _mlp_bwd(hb, a, r, dmb, w1, w2, layer):
    da = _mm(dmb, w2, tb=True, b_view=("rows", layer), out_dtype=BF16, gate=a, name=f"mlp_down_dx_{layer}")
    dw2 = _mm(r, dmb, ta=True, out_dtype=BF16, name=f"mlp_down_dw_{layer}")
    dh = _mm(da, w1, tb=True, b_view=("cols", layer), name=f"mlp_up_dx_{layer}")
    dw1 = _mm(hb, da, ta=True, out_dtype=BF16, out_split=N_CHIP, name=f"mlp_up_dw_{layer}")
    return dh, dw1, dw2


def _local_step(h0, tgt, w, pad, late=None):
    row = lambda a, i: a[i:i + 1]
    ecat = jnp.asarray(_hg_prefix_matrix())
    cw = [w["conv_w"][:, i * 512:(i + 1) * 512] for i in range(3)]
    alog, dtb = _lane_pad(w["a_log"]), _lane_pad(w["dt_bias"])

    h0b = h0.astype(BF16)
    p0 = _mm(h0b, w["ab_w_in"], name="ab_in")
    qn = _conv_fwd(p0, 0, cw[0], "q", pad, "conv_q")
    kn = _conv_fwd(p0, 4, cw[1], "k", pad, "conv_k")
    vn = _conv_fwd(p0, 8, cw[2], "v", pad, "conv_v")
    oa_raw, ss0 = _gdn_fwd(qn, kn, vn, p0, alog, dtb, pad, "gdn_fwd")
    oa = _grms_fwd(oa_raw, p0, 12, w["ab_gnorm_g"], "gdn_gate")
    if late is None:
        ob, rtot = _sb_fwd(p0, pad, "sb_fwd")
    else:
        ob, rtot, g_about, g_cin, g_cout, g_w1, g_w2 = _sb_fwd(p0, pad, "sb_fwd", cargo=late, exchange=GATHER)
        w = dict(w, ab_w_out=g_about.reshape(D, D), c_w_in=g_cin, c_w_out=g_cout.reshape(D, D), mlp_w1=g_w1, mlp_w2=g_w2)
    ycat = jnp.concatenate([oa, ob], axis=1)
    mix0 = _mm(ycat, w["ab_w_out"], name="ab_out")
    h1, h1b = _ln_res_fwd(h0, mix0, row(w["ln_mix_g"], 0), row(w["ln_mix_b"], 0), "ln_mix_0")
    a0, r0, m0 = _mlp_fwd(h1b, w["mlp_w1"], w["mlp_w2"], 0)
    h2, h2b = _ln_res_fwd(h1, m0, row(w["ln_ffn_g"], 0), row(w["ln_ffn_b"], 0), "ln_ffn_0")
    p1 = _mm(h2b, w["c_w_in"], b_view=("cols", 0), name="c_in")
    oc_raw, ss1 = _hg_fwd(p1, w["c_lb_raw"], ecat, pad, "hg_fwd")
    yc = _grms_fwd(oc_raw, p1, 3 * HG_H, w["c_gnorm_g"], "hg_gate")
    mix1 = _mm(yc, w["c_w_out"], name="c_out")
    h3, h3b = _ln_res_fwd(h2, mix1, row(w["ln_mix_g"], 1), row(w["ln_mix_b"], 1), "ln_mix_1")
    a1, r1, m1 = _mlp_fwd(h3b, w["mlp_w1"], w["mlp_w2"], 1)
    h4, _ = _ln_res_fwd(h3, m1, row(w["ln_ffn_g"], 1), row(w["ln_ffn_b"], 1), "ln_ffn_1")
    loss, dh4 = _loss_fwd(h4, tgt, pad + N_META, "loss")

    zero = jnp.zeros_like(dh4)
    dh3a, dm1b, dfg1, dfb1 = _ln_res_bwd(h3, m1, row(w["ln_ffn_g"], 1), row(w["ln_ffn_b"], 1), dh4, zero, "ln_ffn_bwd_1")
    dh3b, dw1_1, dw2_1 = _mlp_bwd(h3b, a1, r1, dm1b, w["mlp_w1"], w["mlp_w2"], 1)
    dh2a, dmix1b, dmg1, dmb1 = _ln_res_bwd(h2, mix1, row(w["ln_mix_g"], 1), row(w["ln_mix_b"], 1), dh3a, dh3b, "ln_mix_bwd_1")
    dyc = _mm(dmix1b, w["c_w_out"], tb=True, name="c_out_dx")
    dwco = _mm(yc, dmix1b, ta=True, out_dtype=BF16, name="c_out_dw")
    doc, dzc, dcg = _grms_bwd(oc_raw, p1, 3 * HG_H, w["c_gnorm_g"], dyc, 0, "hg_gate_bwd")
    landed = {}
    rows4 = lambda a: a.reshape(N_CHIP, -1, D)
    if late is None:
        dq1, df1, di1, dlb = _hg_bwd(p1, w["c_lb_raw"], ecat, ss1, doc, pad, "hg_bwd")
    else:
        dq1, df1, di1, dlb, landed["w1_1"], landed["w2_1"], landed["c_w_out"] = _hg_bwd(
            p1, w["c_lb_raw"], ecat, ss1, doc, pad, "hg_bwd", cargo=[dw1_1, rows4(dw2_1), rows4(dwco)], exchange=SCATTER)
    dp1 = _assemble_bf16([(dq1, "cols"), (df1, "cols"), (di1, "cols"), (dzc, "cols")], "c_in_dy")
    dh2b = _mm(dp1, w["c_w_in"], tb=True, b_view=("cols", 0), name="c_in_dx")
    dwc = _mm(h2b, dp1, ta=True, out_dtype=BF16, out_split=N_CHIP, name="c_in_dw")
    dh1a, dm0b, dfg0, dfb0 = _ln_res_bwd(h1, m0, row(w["ln_ffn_g"], 0), row(w["ln_ffn_b"], 0), dh2a, dh2b, "ln_ffn_bwd_0")
    dh1b, dw1_0, dw2_0 = _mlp_bwd(h1b, a0, r0, dm0b, w["mlp_w1"], w["mlp_w2"], 0)
    dh0a, dmix0b, dmg0, dmb0 = _ln_res_bwd(h0, mix0, row(w["ln_mix_g"], 0), row(w["ln_mix_b"], 0), dh1a, dh1b, "ln_mix_bwd_0")
    dycat = _mm(dmix0b, w["ab_w_out"], tb=True, name="ab_out_dx")
    dwabo = _mm(ycat, dmix0b, ta=True, out_dtype=BF16, name="ab_out_dw")
    doa, dza, dag = _grms_bwd(oa_raw, p0, 12, w["ab_gnorm_g"], dycat, 0, "gdn_gate_bwd")
    if late is None:
        dqn, dkn, dvn, dbb, daa, dal, ddt = _gdn_bwd(qn, kn, vn, p0, alog, dtb, ss0, doa, pad, "gdn_bwd")
        dqb, dkb, dvb = _sb_bwd(p0, rtot, dycat, 4, pad, "sb_bwd")
    else:
        dqn, dkn, dvn, dbb, daa, dal, ddt, landed["c_w_in"], landed["w2_0"] = _gdn_bwd(
            qn, kn, vn, p0, alog, dtb, ss0, doa, pad, "gdn_bwd", cargo=[dwc, rows4(dw2_0)], exchange=SCATTER)
        dqb, dkb, dvb, landed["w1_0"], landed["ab_w_out"] = _sb_bwd(
            p0, rtot, dycat, 4, pad, "sb_bwd", cargo=[dw1_0, rows4(dwabo)], exchange=SCATTER)
    dpq, dcq = _conv_bwd(p0, 0, cw[0], dqn, "q", pad, "conv_q_bwd")
    dpk, dck = _conv_bwd(p0, 4, cw[1], dkn, "k", pad, "conv_k_bwd")
    dpv, dcv = _conv_bwd(p0, 8, cw[2], dvn, "v", pad, "conv_v_bwd")
    dp0 = _assemble_bf16([(dpq, "cols"), (dpk, "cols"), (dpv, "cols"), (dza, "cols"), (dbb, "cols"), (daa, "cols"),
                          (dqb, "cols"), (dkb, "cols"), (dvb, "cols")], "ab_in_dy")
    dh0b = _mm(dp0, w["ab_w_in"], tb=True, name="ab_in_dx")
    dwab = _mm(h0b, dp0, ta=True, out_dtype=BF16, name="ab_in_dw")
    dh0 = _add2(dh0a, dh0b, "dh0")

    grads = {
        "ab_w_in": dwab, "conv_w": jnp.concatenate([dcq, dck, dcv], axis=1),
        "a_log": dal[:, :GDN_H], "dt_bias": ddt[:, :GDN_H],
        "ab_gnorm_g": dag, "ab_w_out": dwabo, "c_w_in": dwc, "c_lb_raw": dlb, "c_gnorm_g": dcg, "c_w_out": dwco,
        "ln_mix_g": jnp.concatenate([dmg0, dmg1], 0), "ln_mix_b": jnp.concatenate([dmb0, dmb1], 0),
        "w1_0": dw1_0, "w1_1": dw1_1, "w2_0": dw2_0, "w2_1": dw2_1,
        "ln_ffn_g": jnp.concatenate([dfg0, dfg1], 0), "ln_ffn_b": jnp.concatenate([dfb0, dfb1], 0),
        "landed": landed,
    }
    return loss, dh0, grads


MESH = pl.DeviceIdType.MESH
ANY = pl.BlockSpec(memory_space=pl.ANY)
N_CHIP = 4
N_DEV = 8
CHIP_REL = ((1, 0), (0, 1), (1, 1))
DEV_REL = tuple((dx, dy, dc) for dx in (0, 1) for dy in (0, 1) for dc in (0, 1))[1:]

def _pos():
    return lax.axis_index("x"), lax.axis_index("y"), lax.axis_index("c")


def _flip(a, d):
    return a + d - 2 * a * d


class _Exchange:
    def __init__(self, local, sends, recvs):
        self.local, self.sends, self.recvs = local, sends, recvs

    def start(self):
        for cp in self.local + self.sends:
            cp.start()

    def wait(self):
        for cp in self.recvs:
            cp.wait_recv()
        for cp in self.sends:
            cp.wait_send()
        for cp in self.local:
            cp.wait()


def _gather_sems(n):
    return [pltpu.SemaphoreType.DMA((3 * n,)), pltpu.SemaphoreType.DMA((3 * n,)), pltpu.SemaphoreType.DMA((n,))]


def _gather_copies(x_refs, o_refs, send_sems, recv_sems, local_sems):
    n = len(x_refs)
    x, y, c = _pos()
    local = [pltpu.make_async_copy(x_refs[a], o_refs[a].at[2 * x + y], local_sems.at[a]) for a in range(n)]

    def copy(a, k, sending):
        tx, ty = _flip(x, CHIP_REL[k][0]), _flip(y, CHIP_REL[k][1])
        return pltpu.make_async_remote_copy(
            src_ref=x_refs[a], dst_ref=o_refs[a].at[2 * x + y if sending else 2 * tx + ty],
            send_sem=send_sems.at[3 * a + k], recv_sem=recv_sems.at[3 * a + k], device_id=(tx, ty, c), device_id_type=MESH)

    pairs = [(a, k) for a in range(n) for k in range(3)]
    return _Exchange(local, [copy(a, k, True) for a, k in pairs], [copy(a, k, False) for a, k in pairs])


def _gather_shapes(bufs):
    return [jax.ShapeDtypeStruct((N_CHIP,) + b.shape, b.dtype) for b in bufs]


def _chip_allgather(bufs, name):
    n = len(bufs)

    def body(*refs):
        ex = _gather_copies(refs[:n], refs[n:2 * n], *refs[2 * n:])
        ex.start()
        ex.wait()

    return pl.pallas_call(
        body, name=name, in_specs=[ANY] * n, out_specs=[ANY] * n, out_shape=_gather_shapes(bufs),
        scratch_shapes=_gather_sems(n), compiler_params=pltpu.CompilerParams(has_side_effects=True),
    )(*bufs)


def _scatter_sems(n):
    nr = N_DEV - 1
    return [pltpu.SemaphoreType.DMA((nr * n,)), pltpu.SemaphoreType.DMA((nr * n,)), pltpu.SemaphoreType.DMA((n,))]


def _scatter_copies(g_refs, o_refs, send_sems, recv_sems, local_sems):
    n = len(g_refs)
    nr = N_DEV - 1
    x, y, c = _pos()
    me = 4 * x + 2 * y + c
    local = [pltpu.make_async_copy(g_refs[a].at[2 * x + y], o_refs[a].at[me], local_sems.at[a]) for a in range(n)]

    def copy(a, k, sending):
        dx, dy, dc = DEV_REL[k]
        tx, ty, tc = _flip(x, dx), _flip(y, dy), _flip(c, dc)
        return pltpu.make_async_remote_copy(
            src_ref=g_refs[a].at[2 * tx + ty], dst_ref=o_refs[a].at[me if sending else 4 * tx + 2 * ty + tc],
            send_sem=send_sems.at[nr * a + k], recv_sem=recv_sems.at[nr * a + k],
            device_id=(tx, ty, tc), device_id_type=MESH)

    pairs = [(a, k) for a in range(n) for k in range(nr)]
    return _Exchange(local, [copy(a, k, True) for a, k in pairs], [copy(a, k, False) for a, k in pairs])


def _scatter_shapes(gs):
    return [jax.ShapeDtypeStruct((N_DEV,) + g.shape[1:], g.dtype) for g in gs]


def _grad_alltoall(gs, name):
    n = len(gs)

    def body(*refs):
        ex = _scatter_copies(refs[:n], refs[n:2 * n], *refs[2 * n:])
        ex.start()
        ex.wait()

    return pl.pallas_call(
        body, name=name, in_specs=[ANY] * n, out_specs=[ANY] * n, out_shape=_scatter_shapes(gs),
        scratch_shapes=_scatter_sems(n), compiler_params=pltpu.CompilerParams(has_side_effects=True),
    )(*gs)


GATHER = (_gather_copies, _gather_shapes, _gather_sems)
SCATTER = (_scatter_copies, _scatter_shapes, _scatter_sems)


def _sum_slots(r, name):
    n, rh, w = r.shape
    tr = _pick(rh, (256, 128, 64, 16))

    def body(r_ref, o_ref):
        acc = r_ref[0].astype(F32)
        for s in range(1, n):
            acc = acc + r_ref[s].astype(F32)
        o_ref[...] = acc

    return pl.pallas_call(
        body, name=name, grid=(rh // tr,), in_specs=[pl.BlockSpec((n, tr, w), lambda i: (0, i, 0))],
        out_specs=pl.BlockSpec((tr, w), lambda i: (i, 0)), out_shape=jax.ShapeDtypeStruct((rh, w), F32),
        compiler_params=_cparams(),
    )(r)


def _small_allreduce(buf, name):
    r, w = buf.shape

    def body(b_ref, o_ref, land_ref, send_sems, recv_sems):
        x, y, c = _pos()
        me = 4 * x + 2 * y + c
        land_ref[me] = b_ref[...]

        def target(k):
            dx, dy, dc = DEV_REL[k]
            return _flip(x, dx), _flip(y, dy), _flip(c, dc)

        sends = []
        for k in range(N_DEV - 1):
            tx, ty, tc = target(k)
            cp = pltpu.make_async_remote_copy(
                src_ref=b_ref, dst_ref=land_ref.at[me], send_sem=send_sems.at[k], recv_sem=recv_sems.at[k],
                device_id=(tx, ty, tc), device_id_type=MESH)
            cp.start()
            sends.append(cp)
        for k in range(N_DEV - 1):
            tx, ty, tc = target(k)
            pltpu.make_async_remote_copy(
                src_ref=b_ref, dst_ref=land_ref.at[4 * tx + 2 * ty + tc], send_sem=send_sems.at[k],
                recv_sem=recv_sems.at[k], device_id=(tx, ty, tc), device_id_type=MESH).wait_recv()
        for cp in sends:
            cp.wait_send()
        acc = land_ref[0]
        for s in range(1, N_DEV):
            acc = acc + land_ref[s]
        o_ref[...] = acc

    vm = pl.BlockSpec(memory_space=pltpu.VMEM)
    return pl.pallas_call(
        body, name=name, in_specs=[vm], out_specs=vm, out_shape=jax.ShapeDtypeStruct((r, w), F32),
        scratch_shapes=[pltpu.VMEM((N_DEV, r, w), F32), pltpu.SemaphoreType.DMA((N_DEV - 1,)),
                        pltpu.SemaphoreType.DMA((N_DEV - 1,))],
        compiler_params=pltpu.CompilerParams(has_side_effects=True),
    )(buf)


def _adamw(w, g, m, v, name):
    r, c = w.shape
    tr = _pick(r, (256, 128, 64, 8)) if r * c > (1 << 18) else r

    def body(w_ref, g_ref, m_ref, v_ref, d_ref, m2_ref, v2_ref):
        gg = g_ref[...]
        m2 = ADAM_B1 * m_ref[...] + (1.0 - ADAM_B1) * gg
        v2 = ADAM_B2 * v_ref[...] + (1.0 - ADAM_B2) * (gg * gg)
        m_hat = m2 / (1.0 - ADAM_B1 ** ADAM_STEP)
        v_hat = v2 / (1.0 - ADAM_B2 ** ADAM_STEP)
        d_ref[...] = -ADAM_LR * (m_hat / (jnp.sqrt(v_hat) + ADAM_EPS) + ADAM_WD * w_ref[...])
        m2_ref[...] = m2
        v2_ref[...] = v2

    blk = pl.BlockSpec((tr, c), lambda i: (i, 0))
    sds = jax.ShapeDtypeStruct((r, c), F32)
    return pl.pallas_call(body, name=name, grid=(r // tr,), in_specs=[blk] * 4, out_specs=[blk] * 3,
                          out_shape=[sds] * 3, compiler_params=_cparams())(w, g, m, v)


BIG = ("ab_w_in", "ab_w_out", "c_w_in", "c_w_out", "mlp_w1", "mlp_w2")
SMALL = ("ln_mix_g", "ln_mix_b", "ln_ffn_g", "ln_ffn_b", "c_lb_raw", "ab_a_log", "ab_dt_bias", "ab_gnorm_g", "c_gnorm_g")
SMALL_ROWS = 16
CONV_ROWS = 8
CONV_W = 3 * GDN_H * HD


def _conv_to_rows(cw):
    return jnp.pad(cw, ((0, 0), (0, 2 * D - CONV_W))).reshape(CONV_ROWS, D)


def _rows_to_conv(rows):
    return rows.reshape(CONV_K, 2 * D)[:, :CONV_W]


def _pack_small(d):
    rows = [jnp.pad(d[n], ((0, 0), (0, D - d[n].shape[1]))) for n in SMALL]
    buf = jnp.concatenate(rows, axis=0)
    return jnp.pad(buf, ((0, SMALL_ROWS - buf.shape[0]), (0, 0)))


def _unpack_small(buf, like):
    out, r = {}, 0
    for n in SMALL:
        nr, nc = like[n].shape
        out[n] = buf[r:r + nr, :nc]
        r += nr
    return out


def kernel(x, meta_tokens, ab_w_in, ab_conv_w, ab_a_log, ab_dt_bias, ab_gnorm_g, ab_w_out, c_w_in, c_lb_raw, c_gnorm_g, c_w_out, ln_mix_g, ln_mix_b, mlp_w1, mlp_w2, ln_ffn_g, ln_ffn_b, loss_target, m_meta_tokens, m_ab_w_in, m_ab_conv_w, m_ab_a_log, m_ab_dt_bias, m_ab_gnorm_g, m_ab_w_out, m_c_w_in, m_c_lb_raw, m_c_gnorm_g, m_c_w_out, m_ln_mix_g, m_ln_mix_b, m_mlp_w1, m_mlp_w2, m_ln_ffn_g, m_ln_ffn_b, v_meta_tokens, v_ab_w_in, v_ab_conv_w, v_ab_a_log, v_ab_dt_bias, v_ab_gnorm_g, v_ab_w_out, v_c_w_in, v_c_lb_raw, v_c_gnorm_g, v_c_w_out, v_ln_mix_g, v_ln_mix_b, v_mlp_w1, v_mlp_w2, v_ln_ffn_g, v_ln_ffn_b):
    names = ("meta_tokens", "ab_w_in", "ab_conv_w", "ab_a_log", "ab_dt_bias", "ab_gnorm_g", "ab_w_out", "c_w_in",
             "c_lb_raw", "c_gnorm_g", "c_w_out", "ln_mix_g", "ln_mix_b", "mlp_w1", "mlp_w2", "ln_ffn_g", "ln_ffn_b")
    wts = dict(zip(names, (meta_tokens, ab_w_in, ab_conv_w, ab_a_log, ab_dt_bias, ab_gnorm_g, ab_w_out, c_w_in, c_lb_raw,
                           c_gnorm_g, c_w_out, ln_mix_g, ln_mix_b, mlp_w1, mlp_w2, ln_ffn_g, ln_ffn_b)))
    mom_m = dict(zip(names, (m_meta_tokens, m_ab_w_in, m_ab_conv_w, m_ab_a_log, m_ab_dt_bias, m_ab_gnorm_g, m_ab_w_out,
                             m_c_w_in, m_c_lb_raw, m_c_gnorm_g, m_c_w_out, m_ln_mix_g, m_ln_mix_b, m_mlp_w1, m_mlp_w2,
                             m_ln_ffn_g, m_ln_ffn_b)))
    mom_v = dict(zip(names, (v_meta_tokens, v_ab_w_in, v_ab_conv_w, v_ab_a_log, v_ab_dt_bias, v_ab_gnorm_g, v_ab_w_out,
                             v_c_w_in, v_c_lb_raw, v_c_gnorm_g, v_c_w_out, v_ln_mix_g, v_ln_mix_b, v_mlp_w1, v_mlp_w2,
                             v_ln_ffn_g, v_ln_ffn_b)))
    seq = x.shape[1]
    pad = (-(N_META + seq)) % QB
    xi, yi, ci = _pos()
    chip = 2 * xi + yi

    gat_ab_in, = _chip_allgather([ab_w_in[0].astype(BF16)], "gather_weights")
    late = [ab_w_out[0].astype(BF16), c_w_in.astype(BF16), c_w_out[0].astype(BF16), mlp_w1.astype(BF16),
            mlp_w2.astype(BF16)]
    mcols, ccols = meta_tokens.shape[1], ab_conv_w.shape[2]
    place = jnp.concatenate([
        lax.dynamic_update_slice(jnp.zeros((N_META, D), F32), 0.5 * meta_tokens, (0, chip * mcols)),
        _conv_to_rows(lax.dynamic_update_slice(jnp.zeros((CONV_K, CONV_W), F32), 0.5 * ab_conv_w[0], (0, chip * ccols)))],
        axis=0)
    placed = _small_allreduce(place, "gather_meta")
    meta_full = placed[:N_META]

    w = {
        "ab_w_in": _pad_ab_cols(jnp.transpose(gat_ab_in, (1, 0, 2)).reshape(D, AB_TRUE)),
        "conv_w": _rows_to_conv(placed[N_META:]), "a_log": ab_a_log, "dt_bias": ab_dt_bias,
        "ab_gnorm_g": ab_gnorm_g, "c_lb_raw": c_lb_raw,
        "c_gnorm_g": c_gnorm_g, "ln_mix_g": ln_mix_g, "ln_mix_b": ln_mix_b, "ln_ffn_g": ln_ffn_g, "ln_ffn_b": ln_ffn_b,
    }

    h0 = jnp.concatenate([jnp.zeros((pad, D), F32), meta_full, x[0]], axis=0)
    tgt = jnp.concatenate([jnp.zeros((pad + N_META, D), F32), loss_target[0]], axis=0)
    loss8, dh0, g = _local_step(h0, tgt, w, pad, late)
    loss = lax.psum(loss8[0, 0], ("x", "y", "c"))
    grad_x = dh0[pad + N_META:][None]

    gsmall = {"ln_mix_g": g["ln_mix_g"], "ln_mix_b": g["ln_mix_b"], "ln_ffn_g": g["ln_ffn_g"], "ln_ffn_b": g["ln_ffn_b"],
              "c_lb_raw": g["c_lb_raw"], "ab_a_log": g["a_log"], "ab_dt_bias": g["dt_bias"], "ab_gnorm_g": g["ab_gnorm_g"],
              "c_gnorm_g": g["c_gnorm_g"]}
    sbuf = jnp.concatenate([_pack_small(gsmall), dh0[pad:pad + N_META], _conv_to_rows(g["conv_w"])], axis=0)
    ssum = _small_allreduce(sbuf, "allreduce_small")
    grads = _unpack_small(ssum[:SMALL_ROWS], wts)
    grads["meta_tokens"] = lax.dynamic_slice(ssum[SMALL_ROWS:SMALL_ROWS + N_META], (0, chip * mcols), (N_META, mcols))
    grads["ab_conv_w"] = lax.dynamic_slice(_rows_to_conv(ssum[SMALL_ROWS + N_META:]), (0, chip * ccols), (CONV_K, ccols))[None]

    dab = jnp.transpose(_unpad_ab_cols(g["ab_w_in"]).reshape(D, N_CHIP, AB_TRUE // N_CHIP), (1, 0, 2))
    landed = dict(g["landed"])
    landed["ab_w_in"], = _grad_alltoall([dab], "grad_alltoall")
    sums = {k: _sum_slots(v, f"grad_sum_{k}") for k, v in landed.items()}
    for n in ("ab_w_in", "ab_w_out", "c_w_in", "c_w_out"):
        grads[n] = sums[n][None]
    grads["mlp_w1"] = jnp.stack([sums["w1_0"], sums["w1_1"]])
    grads["mlp_w2"] = jnp.stack([sums["w2_0"], sums["w2_1"]])

    delta, new_m, new_v = {}, {}, {}
    for n in ("meta_tokens", "ab_conv_w") + BIG:
        shp = wts[n].shape
        to2 = lambda a: a.reshape(-1, shp[-1])
        d2, m2, v2 = _adamw(to2(wts[n]), to2(grads[n]), to2(mom_m[n]), to2(mom_v[n]), f"adamw_{n}")
        delta[n], new_m[n], new_v[n] = d2.reshape(shp), m2.reshape(shp), v2.reshape(shp)
    d2, m2, v2 = _adamw(_pack_small(wts), ssum[:SMALL_ROWS], _pack_small(mom_m), _pack_small(mom_v), "adamw_small")
    delta.update(_unpack_small(d2, wts))
    new_m.update(_unpack_small(m2, wts))
    new_v.update(_unpack_small(v2, wts))

    return (loss, grad_x, *[grads[n] for n in names], *[delta[n] for n in names], *[new_m[n] for n in names],
            *[new_v[n] for n in names])
```

```python
import functools
import math

import numpy as np
import jax
import jax.numpy as jnp
from jax import lax
from jax.experimental import pallas as pl
from jax.experimental.pallas import tpu as pltpu

F32 = jnp.float32
BF16 = jnp.bfloat16

D = 1024
N_META = 16
D_FF = 4 * D
DEPTH = 2
GDN_H = 4
SB_H = 8
SB_DH = 64
HG_H = 8
HD = 128
CH = 64
QB = 128
ALPHA = float((2 * DEPTH) ** 0.25)
LN_EPS = 1e-5
RMS_EPS = 1e-6
L2_EPS = 1e-6
NEG = -1e30

ADAM_LR = 0.001
ADAM_B1 = 0.9
ADAM_B2 = 0.999
ADAM_EPS = 1e-08
ADAM_WD = 0.01
ADAM_STEP = 10

AB_W = 30 * HD
AB_TRUE = 3592
VMEM_LIMIT = 56 * 1024 * 1024

NN = ((1,), (0,))
NT = ((1,), (1,))
TN = ((0,), (0,))


def _cparams(**kw):
    return pltpu.CompilerParams(vmem_limit_bytes=VMEM_LIMIT, **kw)


def _dg(a, b, dims, mode):
    if mode == "h":
        return lax.dot_general(a, b, dims, precision=lax.Precision.HIGHEST, preferred_element_type=F32)
    if mode == "b":
        return lax.dot_general(a.astype(BF16), b.astype(BF16), dims, preferred_element_type=F32)
    ah, bh = a.astype(BF16), b.astype(BF16)
    al, bl = (a - ah.astype(F32)).astype(BF16), (b - bh.astype(F32)).astype(BF16)
    d = lambda x, y: lax.dot_general(x, y, dims, preferred_element_type=F32)
    return d(ah, bh) + (d(ah, bl) + d(al, bh))


def _make_dots(mode, batched=False):
    if batched:
        nn_d, nt_d, tn_d = (((2,), (1,)), ((0,), (0,))), (((2,), (2,)), ((0,), (0,))), (((1,), (1,)), ((0,), (0,)))
    else:
        nn_d, nt_d, tn_d = (NN, ((), ())), (NT, ((), ())), (TN, ((), ()))

    @jax.custom_vjp
    def nn(a, b):
        return _dg(a, b, nn_d, mode)

    @jax.custom_vjp
    def nt(a, b):
        return _dg(a, b, nt_d, mode)

    @jax.custom_vjp
    def tn(a, b):
        return _dg(a, b, tn_d, mode)

    nn.defvjp(lambda a, b: (nn(a, b), (a, b)), lambda r, g: (nt(g, r[1]), tn(r[0], g)))
    nt.defvjp(lambda a, b: (nt(a, b), (a, b)), lambda r, g: (nn(g, r[1]), tn(g, r[0])))
    tn.defvjp(lambda a, b: (tn(a, b), (a, b)), lambda r, g: (nt(r[1], g), nn(r[0], g)))
    return nn, nt, tn


hnn, hnt, htn = _make_dots("h")
bbnn, bbnt, bbtn = _make_dots("b", True)
mbnn, mbnt, mbtn = _make_dots("m", True)
hbnn, hbnt, hbtn = _make_dots("h", True)


def _heads(a, n):
    return jnp.concatenate([a[None, :, h * HD:(h + 1) * HD] for h in range(n)], axis=0)


def _sigmoid(x):
    return jax.nn.sigmoid(x)


def _silu(x):
    return x * jax.nn.sigmoid(x)


def _softplus(x):
    return jnp.maximum(x, 0.0) + jnp.log(1.0 + jnp.exp(-jnp.abs(x)))


def _iota(shape, dim):
    return lax.broadcasted_iota(jnp.int32, shape, dim)


def _pick(n, prefs):
    for p in prefs:
        if n % p == 0:
            return p
    return n


def _mm(a, b, *, ta=False, tb=False, out_dtype=F32, name, b_view=None, out_split=0, act=False, gate=None):
    if ta:
        k_dim, m_dim = a.shape
    else:
        m_dim, k_dim = a.shape
    if b_view is None:
        w_rows, w_cols = b.shape
    else:
        kind, layer = b_view
        nj, _, blk_r, blk_c = b.shape
        w_rows, w_cols = (blk_r, nj * blk_c) if kind == "cols" else (nj * blk_r, blk_c)
    n_dim = w_rows if tb else w_cols
    assert (w_cols if tb else w_rows) == k_dim
    tm = _pick(m_dim, (1024, 1056, 704, 640, 512, 384, 256, 128))
    tn = _pick(n_dim, (1024, 1056, 704, 640, 512, 384, 256, 128))
    tk = _pick(k_dim, (1024, 1056, 704, 512, 384, 256, 128))
    nk = k_dim // tk
    a_spec = pl.BlockSpec((tk, tm), lambda i, j, k: (k, i)) if ta else pl.BlockSpec((tm, tk), lambda i, j, k: (i, k))
    wb = (tn, tk) if tb else (tk, tn)
    w_idx = (lambda i, j, k: (j, k)) if tb else (lambda i, j, k: (k, j))
    if b_view is None:
        b_spec = pl.BlockSpec(wb, w_idx)
    elif kind == "cols":
        per = blk_c // wb[1]
        b_spec = pl.BlockSpec((None, None) + wb,
                              lambda i, j, k: (w_idx(i, j, k)[1] // per, layer, w_idx(i, j, k)[0], w_idx(i, j, k)[1] % per))
    else:
        per = blk_r // wb[0]
        b_spec = pl.BlockSpec((None, None) + wb,
                              lambda i, j, k: (w_idx(i, j, k)[0] // per, layer, w_idx(i, j, k)[0] % per, w_idx(i, j, k)[1]))
    if out_split:
        per_o = (n_dim // out_split) // tn
        out_spec = pl.BlockSpec((None, tm, tn), lambda i, j, k: (j // per_o, i, j % per_o))
        out_sds = jax.ShapeDtypeStruct((out_split, m_dim, n_dim // out_split), out_dtype)
    else:
        out_spec = pl.BlockSpec((tm, tn), lambda i, j, k: (i, j))
        out_sds = jax.ShapeDtypeStruct((m_dim, n_dim), out_dtype)
    dims = (((0 if ta else 1,), (1 if tb else 0,)), ((), ()))
    extra = [] if gate is None else [gate]
    n_out = 2 if act else 1

    def finish(acc, refs):
        if act:
            refs[0][...] = acc.astype(refs[0].dtype)
            r = jnp.maximum(acc, 0.0)
            refs[1][...] = (r * r).astype(refs[1].dtype)
        elif gate is not None:
            refs[1][...] = (acc * (2.0 * jnp.maximum(refs[0][...].astype(F32), 0.0))).astype(refs[1].dtype)
        else:
            refs[0][...] = acc.astype(refs[0].dtype)

    def body(a_ref, b_ref, *rest):
        refs, acc_ref = rest[:-1], rest[-1]
        part = lax.dot_general(a_ref[...], b_ref[...], dims, preferred_element_type=F32)
        if nk == 1:
            finish(part, refs)
        else:
            k = pl.program_id(2)

            @pl.when(k == 0)
            def _():
                acc_ref[...] = part

            @pl.when(k > 0)
            def _():
                acc_ref[...] += part

            @pl.when(k == nk - 1)
            def _():
                finish(acc_ref[...], refs)

    out = pl.pallas_call(
        body, name=name, grid=(m_dim // tm, n_dim // tn, nk),
        in_specs=[a_spec, b_spec] + [pl.BlockSpec((tm, tn), lambda i, j, k: (i, j))] * len(extra),
        out_specs=[out_spec] * n_out,
        out_shape=[out_sds] * n_out,
        scratch_shapes=[pltpu.VMEM((tm, tn) if nk > 1 else (8, 128), F32)],
        compiler_params=_cparams(dimension_semantics=("parallel", "parallel", "arbitrary")),
    )(a, b, *extra)
    return out if act else out[0]


def _row_tile(t_pad, width):
    for tr in (528, 352, 176, 128, 64):
        if t_pad % tr == 0 and tr * width * 4 <= (3 << 19) and tr % 16 == 0:
            return tr
    return 64 if t_pad % 64 == 0 else t_pad


def _ln_res_fn(h, m, g, b):
    x = ALPHA * h + m
    mu = jnp.mean(x, axis=-1, keepdims=True)
    xc = x - mu
    var = jnp.mean(xc * xc, axis=-1, keepdims=True)
    return xc * lax.rsqrt(var + LN_EPS) * g + b


def _ln_res_fwd(h, m, g, b, name):
    t_pad = h.shape[0]
    tr = _row_tile(t_pad, D)

    def body(h_ref, m_ref, g_ref, b_ref, y_ref, yb_ref):
        y = _ln_res_fn(h_ref[...], m_ref[...], g_ref[...], b_ref[...])
        y_ref[...] = y
        yb_ref[...] = y.astype(BF16)

    row = pl.BlockSpec((tr, D), lambda i: (i, 0))
    par = pl.BlockSpec((1, D), lambda i: (0, 0))
    return pl.pallas_call(
        body, name=name, grid=(t_pad // tr,), in_specs=[row, row, par, par], out_specs=[row, row],
        out_shape=[jax.ShapeDtypeStruct((t_pad, D), F32), jax.ShapeDtypeStruct((t_pad, D), BF16)],
        compiler_params=_cparams(),
    )(h, m, g, b)


def _ln_res_bwd(h, m, g, b, dy1, dy2, name):
    t_pad = h.shape[0]
    tr = _row_tile(t_pad, D)

    def body(h_ref, m_ref, g_ref, b_ref, d1_ref, d2_ref, dh_ref, dm_ref, dg_ref, db_ref):
        _, vjp = jax.vjp(_ln_res_fn, h_ref[...], m_ref[...], g_ref[...], b_ref[...])
        dh, dm, dg, db = vjp(d1_ref[...] + d2_ref[...])
        dh_ref[...] = dh
        dm_ref[...] = dm.astype(BF16)

        @pl.when(pl.program_id(0) == 0)
        def _():
            dg_ref[...] = jnp.zeros_like(dg_ref)
            db_ref[...] = jnp.zeros_like(db_ref)

        dg_ref[...] += dg
        db_ref[...] += db

    row = pl.BlockSpec((tr, D), lambda i: (i, 0))
    par = pl.BlockSpec((1, D), lambda i: (0, 0))
    return pl.pallas_call(
        body, name=name, grid=(t_pad // tr,), in_specs=[row, row, par, par, row, row],
        out_specs=[row, row, par, par],
        out_shape=[jax.ShapeDtypeStruct((t_pad, D), F32), jax.ShapeDtypeStruct((t_pad, D), BF16),
                   jax.ShapeDtypeStruct((1, D), F32), jax.ShapeDtypeStruct((1, D), F32)],
        compiler_params=_cparams(),
    )(h, m, g, b, dy1, dy2)


def _grms_fn(o, z, g):
    y = o * lax.rsqrt(jnp.mean(o * o, axis=-1, keepdims=True) + RMS_EPS) * g
    return y * _silu(z)


def _grms_fwd(o, z_arr, z_blk0, g, name):
    t_pad, w = o.shape
    nh = w // HD
    tr = _row_tile(t_pad, HD * 4)

    def body(o_ref, z_ref, g_ref, y_ref):
        y_ref[...] = _grms_fn(o_ref[...], z_ref[...], g_ref[...]).astype(BF16)

    return pl.pallas_call(
        body, name=name, grid=(t_pad // tr, nh),
        in_specs=[pl.BlockSpec((tr, HD), lambda i, h: (i, h)),
                  pl.BlockSpec((tr, HD), lambda i, h: (i, z_blk0 + h)),
                  pl.BlockSpec((1, HD), lambda i, h: (0, 0))],
        out_specs=pl.BlockSpec((tr, HD), lambda i, h: (i, h)),
        out_shape=jax.ShapeDtypeStruct((t_pad, w), BF16), compiler_params=_cparams(),
    )(o, z_arr, g)


def _grms_bwd(o, z_arr, z_blk0, g, dy_arr, dy_blk0, name):
    t_pad, w = o.shape
    nh = w // HD
    tr = _row_tile(t_pad, HD * 4)

    def body(o_ref, z_ref, g_ref, dy_ref, do_ref, dz_ref, dg_ref):
        _, vjp = jax.vjp(_grms_fn, o_ref[...], z_ref[...], g_ref[...])
        do, dz, dg = vjp(dy_ref[...])
        do_ref[...] = do
        dz_ref[...] = dz

        @pl.when((pl.program_id(0) == 0) & (pl.program_id(1) == 0))
        def _():
            dg_ref[...] = jnp.zeros_like(dg_ref)

        dg_ref[...] += dg

    blk = pl.BlockSpec((tr, HD), lambda i, h: (i, h))
    return pl.pallas_call(
        body, name=name, grid=(t_pad // tr, nh),
        in_specs=[blk, pl.BlockSpec((tr, HD), lambda i, h: (i, z_blk0 + h)),
                  pl.BlockSpec((1, HD), lambda i, h: (0, 0)),
                  pl.BlockSpec((tr, HD), lambda i, h: (i, dy_blk0 + h))],
        out_specs=[blk, blk, pl.BlockSpec((1, HD), lambda i, h: (0, 0))],
        out_shape=[jax.ShapeDtypeStruct((t_pad, w), F32), jax.ShapeDtypeStruct((t_pad, w), F32),
                   jax.ShapeDtypeStruct((1, HD), F32)],
        compiler_params=_cparams(),
    )(o, z_arr, g, dy_arr)


def _loss_fwd(y, tgt, first_row, name):
    t_pad = y.shape[0]
    tr = _row_tile(t_pad, D)

    def body(y_ref, t_ref, l_ref, dy_ref):
        rows = pl.program_id(0) * tr + _iota((tr, 1), 0)
        err = jnp.where(rows >= first_row, y_ref[...] - t_ref[...], 0.0)
        dy_ref[...] = err * (1.0 / D)

        @pl.when(pl.program_id(0) == 0)
        def _():
            l_ref[...] = jnp.zeros_like(l_ref)

        part = jnp.sum(jnp.sum(err * err, axis=1, keepdims=True), axis=0, keepdims=True)
        l_ref[...] += jnp.broadcast_to(part * (0.5 / D), l_ref.shape)

    row = pl.BlockSpec((tr, D), lambda i: (i, 0))
    return pl.pallas_call(
        body, name=name, grid=(t_pad // tr,), in_specs=[row, row],
        out_specs=[pl.BlockSpec((8, 128), lambda i: (0, 0)), row],
        out_shape=[jax.ShapeDtypeStruct((8, 128), F32), jax.ShapeDtypeStruct((t_pad, D), F32)],
        compiler_params=_cparams(),
    )(y, tgt)


def _add2(a, b, name):
    t_pad, w = a.shape
    tr = _row_tile(t_pad, w)

    def body(a_ref, b_ref, o_ref):
        o_ref[...] = a_ref[...] + b_ref[...]

    row = pl.BlockSpec((tr, w), lambda i: (i, 0))
    return pl.pallas_call(body, name=name, grid=(t_pad // tr,), in_specs=[row, row], out_specs=row,
                          out_shape=jax.ShapeDtypeStruct((t_pad, w), F32), compiler_params=_cparams())(a, b)


def _assemble_bf16(parts, name):
    t_pad = parts[0][0].shape[0] if parts[0][1] == "cols" else parts[0][0].shape[1]
    widths = [p.shape[1] if kind == "cols" else HD for p, kind in parts]
    total = sum(widths)
    tr = _row_tile(t_pad, total)

    def body(*refs):
        o_ref = refs[-1]
        off = 0
        for ref, (p, kind), w in zip(refs[:-1], parts, widths):
            if kind == "cols":
                o_ref[:, off:off + w] = ref[...].astype(BF16)
            else:
                acc = ref[0]
                for hh in range(1, p.shape[0]):
                    acc = acc + ref[hh]
                o_ref[:, off:off + w] = acc.astype(BF16)
            off += w

    specs = []
    for p, kind in parts:
        if kind == "cols":
            specs.append(pl.BlockSpec((tr, p.shape[1]), lambda i: (i, 0)))
        else:
            specs.append(pl.BlockSpec((p.shape[0], tr, HD), lambda i: (0, i, 0)))
    return pl.pallas_call(
        body, name=name, grid=(t_pad // tr,), in_specs=specs,
        out_specs=pl.BlockSpec((tr, total), lambda i: (i, 0)),
        out_shape=jax.ShapeDtypeStruct((t_pad, total), BF16), compiler_params=_cparams(),
    )(*[p for p, _ in parts])


CONV_K = 4
HALO = 8
RT = 128


def _conv_fwd(p, blk0, w, mode, pad, name):
    t_pad = p.shape[0]
    nt = t_pad // RT
    scale = HD ** -0.5 if mode == "q" else 1.0

    def body(x_ref, w_ref, y_ref, xs_ref):
        xs_ref[0:HALO, :] = jnp.zeros((HALO, HD), F32)
        rows = _iota((t_pad, 1), 0)
        xs_ref[HALO:HALO + t_pad, :] = jnp.where(rows >= pad, x_ref[...], 0.0)
        wv = w_ref[...]

        def tile(i, carry):
            r0 = pl.multiple_of(i * RT, RT)
            ext = xs_ref[pl.ds(r0, RT + HALO), :]
            acc = ext[HALO:, :] * wv[3:4, :]
            for s in (1, 2, 3):
                acc = acc + pltpu.roll(ext, s, 0)[HALO:, :] * wv[3 - s:4 - s, :]
            y = _silu(acc)
            if mode != "v":
                y = y * lax.rsqrt(jnp.sum(y * y, axis=-1, keepdims=True) + L2_EPS) * scale
            y_ref[pl.ds(r0, RT), :] = y
            return carry

        lax.fori_loop(0, nt, tile, 0)

    return pl.pallas_call(
        body, name=name, grid=(GDN_H,),
        in_specs=[pl.BlockSpec((t_pad, HD), lambda h: (0, blk0 + h)), pl.BlockSpec((CONV_K, HD), lambda h: (0, h))],
        out_specs=pl.BlockSpec((t_pad, HD), lambda h: (0, h)),
        out_shape=jax.ShapeDtypeStruct((t_pad, GDN_H * HD), F32),
        scratch_shapes=[pltpu.VMEM((t_pad + HALO, HD), F32)],
        compiler_params=_cparams(),
    )(p, w)


def _conv_bwd(p, blk0, w, dn, mode, pad, name):
    t_pad = p.shape[0]
    nt = t_pad // RT
    scale = HD ** -0.5 if mode == "q" else 1.0

    def body(x_ref, w_ref, dn_ref, dx_ref, dw_ref, xs_ref, ds_ref):
        xs_ref[0:HALO, :] = jnp.zeros((HALO, HD), F32)
        xs_ref[HALO + t_pad:HALO + t_pad + 2 * HALO, :] = jnp.zeros((2 * HALO, HD), F32)
        ds_ref[t_pad:t_pad + HALO, :] = jnp.zeros((HALO, HD), F32)
        rows = _iota((t_pad, 1), 0)
        xs_ref[HALO:HALO + t_pad, :] = jnp.where(rows >= pad, x_ref[...], 0.0)
        ds_ref[0:t_pad, :] = dn_ref[...]
        wv = w_ref[...]

        def tile(i, dw):
            r0 = pl.multiple_of(i * RT, RT)
            ext = xs_ref[pl.ds(r0, RT + 2 * HALO), :]
            dn_e = ds_ref[pl.ds(r0, RT + HALO), :]
            xsh = [ext[HALO:, :]] + [pltpu.roll(ext, s, 0)[HALO:, :] for s in (1, 2, 3)]
            pre = xsh[0] * wv[3:4, :]
            for s in (1, 2, 3):
                pre = pre + xsh[s] * wv[3 - s:4 - s, :]
            sg = _sigmoid(pre)
            y = pre * sg
            if mode != "v":
                ss = jnp.sum(y * y, axis=-1, keepdims=True) + L2_EPS
                r = lax.rsqrt(ss)
                dy = scale * (dn_e * r - y * (r * r * r) * jnp.sum(dn_e * y, axis=-1, keepdims=True))
            else:
                dy = dn_e
            dpre = dy * (sg * (1.0 + pre * (1.0 - sg)))
            dx = dpre[:RT, :] * wv[3:4, :]
            for s in (1, 2, 3):
                dx = dx + pltpu.roll(dpre, RT + HALO - s, 0)[:RT, :] * wv[3 - s:4 - s, :]
            trow = r0 + _iota((RT, 1), 0)
            dx_ref[pl.ds(r0, RT), :] = jnp.where(trow >= pad, dx, 0.0)
            new = []
            for s in (0, 1, 2, 3):
                new.append(dw[s] + jnp.sum(dpre[:RT, :] * xsh[s][:RT, :], axis=0, keepdims=True))
            return tuple(new)

        z = jnp.zeros((1, HD), F32)
        dw = lax.fori_loop(0, nt, tile, (z, z, z, z))
        for s in (0, 1, 2, 3):
            dw_ref[3 - s:4 - s, :] = dw[s]

    return pl.pallas_call(
        body, name=name, grid=(GDN_H,),
        in_specs=[pl.BlockSpec((t_pad, HD), lambda h: (0, blk0 + h)), pl.BlockSpec((CONV_K, HD), lambda h: (0, h)),
                  pl.BlockSpec((t_pad, HD), lambda h: (0, h))],
        out_specs=[pl.BlockSpec((t_pad, HD), lambda h: (0, h)), pl.BlockSpec((CONV_K, HD), lambda h: (0, h))],
        out_shape=[jax.ShapeDtypeStruct((t_pad, GDN_H * HD), F32), jax.ShapeDtypeStruct((CONV_K, GDN_H * HD), F32)],
        scratch_shapes=[pltpu.VMEM((t_pad + 3 * HALO, HD), F32), pltpu.VMEM((t_pad + HALO, HD), F32)],
        compiler_params=_cparams(),
    )(p, w, dn)


def _unit_lower_inv(m, bd, eye):
    md = m * bd
    low = m - md
    p2 = mbnn(md, md)
    p4 = mbnn(p2, p2)
    dinv = mbnn(mbnn(eye - md, eye + p2), eye + p4)
    n = mbnn(dinv, low)
    n2 = mbnn(n, n)
    n4 = mbnn(n2, n2)
    return mbnn(mbnn(mbnn(eye - n, eye + n2), eye + n4), dinv)


def _gdn_chunk(q, k, v, bb, aa, alog, dtb, s, valid):
    nh = q.shape[0]
    ri = _iota((1, CH, CH), 1)
    ci = _iota((1, CH, CH), 2)
    causal = ri >= ci
    strict = ri > ci
    eye = (ri == ci).astype(F32)
    bd = ((ri >> 3) == (ci >> 3)).astype(F32)
    ltri = (_iota((CH, CH), 0) >= _iota((CH, CH), 1)).astype(F32)
    sel = (_iota((nh, 1, HD), 2) == _iota((nh, 1, HD), 0)).astype(F32)

    beta_all = jnp.where(valid, _sigmoid(bb), 0.0)
    g_all = jnp.where(valid, -jnp.exp(alog) * _softplus(aa + dtb), 0.0)
    gc_all = hnn(ltri, g_all)
    beta = jnp.sum(beta_all[None] * sel, axis=2, keepdims=True)
    gc = jnp.sum(gc_all[None] * sel, axis=2, keepdims=True)
    gc_rows = hbnt(jnp.broadcast_to(sel, (nh, CH, HD)), jnp.broadcast_to(gc_all[None], (nh, CH, HD)))
    last = _iota((1, CH, 1), 1) == CH - 1
    gc_last = jnp.sum(jnp.where(last, gc, 0.0), axis=1, keepdims=True)
    decay = jnp.exp(jnp.where(causal, gc - gc_rows, NEG))
    egc = jnp.exp(gc)

    kb = k * beta
    m = jnp.where(strict, bbnt(kb, k) * decay, 0.0)
    t_inv = _unit_lower_inv(m, bd, eye)
    u = bbnn(t_inv, v * beta)
    w = bbnn(t_inv, kb * egc)
    a_intra = bbnt(q, k) * decay
    q_dec = q * egc
    k_dec = k * jnp.exp(gc_last - gc)
    v_new = u - bbnn(w, s)
    o = bbnn(q_dec, s) + bbnn(a_intra, v_new)
    s_new = s * jnp.exp(gc_last) + bbtn(k_dec, v_new)
    return o, s_new


PAIR = 2 * CH


def _gdn_specs(npair, rev):
    cc = (lambda c: npair - 1 - c) if rev else (lambda c: c)
    wide = pl.BlockSpec((PAIR, GDN_H * HD), lambda c: (cc(c), 0))
    fix = lambda off: pl.BlockSpec((PAIR, HD), lambda c: (cc(c), off))
    par = pl.BlockSpec((1, HD), lambda c: (0, 0))
    state = pl.BlockSpec((1, GDN_H, HD, HD), lambda c: (cc(c), 0, 0, 0))
    return wide, fix, par, state


def _store_heads(ref, a, rows=slice(None)):
    for h in range(a.shape[0]):
        ref[rows, h * HD:(h + 1) * HD] = a[h]


def _chunk_rows(half):
    return slice(half * CH, (half + 1) * CH)


def _chunk_valid(pair, half, pad):
    return ((2 * pair + half) * CH + _iota((CH, 1), 0)) >= pad


def _gdn_fwd(qn, kn, vn, p, alog, dtb, pad, name):
    t_pad = qn.shape[0]
    npair = t_pad // PAIR
    wide, fix, par, state = _gdn_specs(npair, False)

    def body(q_ref, k_ref, v_ref, bb_ref, aa_ref, al_ref, dt_ref, o_ref, ss_ref, s_ref):
        c = pl.program_id(0)

        @pl.when(c == 0)
        def _():
            s_ref[...] = jnp.zeros_like(s_ref)

        s = s_ref[...]
        ss_ref[0] = s
        for half in (0, 1):
            r = _chunk_rows(half)
            o, s = _gdn_chunk(_heads(q_ref[r, :], GDN_H), _heads(k_ref[r, :], GDN_H), _heads(v_ref[r, :], GDN_H),
                              bb_ref[r, :], aa_ref[r, :], al_ref[...], dt_ref[...], s, _chunk_valid(c, half, pad))
            _store_heads(o_ref, o, r)
        s_ref[...] = s

    return pl.pallas_call(
        body, name=name, grid=(npair,),
        in_specs=[wide, wide, wide, fix(16), fix(17), par, par],
        out_specs=[wide, state],
        out_shape=[jax.ShapeDtypeStruct((t_pad, GDN_H * HD), F32), jax.ShapeDtypeStruct((npair, GDN_H, HD, HD), F32)],
        scratch_shapes=[pltpu.VMEM((GDN_H, HD, HD), F32)],
        compiler_params=_cparams(),
    )(qn, kn, vn, p, p, alog, dtb)


def _gdn_bwd(qn, kn, vn, p, alog, dtb, ssave, do, pad, name, cargo=(), exchange=None):
    t_pad = qn.shape[0]
    npair = t_pad // PAIR
    wide, fix, par, state = _gdn_specs(npair, True)
    n = len(cargo)

    def body(q_ref, k_ref, v_ref, bb_ref, aa_ref, al_ref, dt_ref, ss_ref, do_ref, *rest):
        c = pl.program_id(0)
        ds_ref = rest[-1]
        (dq_ref, dk_ref, dv_ref, dbb_ref, daa_ref, dal_ref, ddt_ref), end_cargo = _cargo_bounds(
            rest[:-1], n, 7, exchange, c == 0, c == npair - 1)

        @pl.when(c == 0)
        def _():
            ds_ref[...] = jnp.zeros_like(ds_ref)
            dal_ref[...] = jnp.zeros_like(dal_ref)
            ddt_ref[...] = jnp.zeros_like(ddt_ref)

        ra, rb = _chunk_rows(0), _chunk_rows(1)
        va, vb = _chunk_valid(npair - 1 - c, 0, pad), _chunk_valid(npair - 1 - c, 1, pad)

        def pair(qa, ka, va_, ba, aa, qb, kb, vb_, bb, ab, al, dt, s):
            oa, s = _gdn_chunk(qa, ka, va_, ba, aa, al, dt, s, va)
            ob, s = _gdn_chunk(qb, kb, vb_, bb, ab, al, dt, s, vb)
            return oa, ob, s

        ins = [f(ref[r, :]) for r in (ra, rb)
               for ref, f in ((q_ref, lambda a: _heads(a, GDN_H)), (k_ref, lambda a: _heads(a, GDN_H)),
                              (v_ref, lambda a: _heads(a, GDN_H)), (bb_ref, lambda a: a), (aa_ref, lambda a: a))]
        _, vjp = jax.vjp(pair, *ins, al_ref[...], dt_ref[...], ss_ref[0])
        g = vjp((_heads(do_ref[ra, :], GDN_H), _heads(do_ref[rb, :], GDN_H), ds_ref[...]))
        for r, (dq, dk, dv, dbb, daa) in ((ra, g[0:5]), (rb, g[5:10])):
            _store_heads(dq_ref, dq, r)
            _store_heads(dk_ref, dk, r)
            _store_heads(dv_ref, dv, r)
            dbb_ref[r, :] = dbb
            daa_ref[r, :] = daa
        dal_ref[...] += g[10]
        ddt_ref[...] += g[11]
        ds_ref[...] = g[12]
        end_cargo()

    sds = jax.ShapeDtypeStruct
    return pl.pallas_call(
        body, name=name, grid=(npair,),
        in_specs=[wide, wide, wide, fix(16), fix(17), par, par, state, wide] + [ANY] * n,
        out_specs=[wide, wide, wide, fix(0), fix(0), par, par] + [ANY] * n,
        out_shape=[sds((t_pad, GDN_H * HD), F32)] * 3 + [sds((t_pad, HD), F32)] * 2 + [sds((1, HD), F32)] * 2
        + (exchange[1](cargo) if n else []),
        scratch_shapes=(exchange[2](n) if n else []) + [pltpu.VMEM((GDN_H, HD, HD), F32)],
        compiler_params=_cparams(),
    )(qn, kn, vn, p, p, alog, dtb, ssave, do, *cargo)


SB_Q0, SB_K0, SB_V0 = 18, 22, 26
SB_SCALE = SB_DH ** -0.5
SB_NB = 4


def _sb_terms(z, allowed):
    e = jnp.exp(-jnp.abs(z))
    den = 1.0 + e
    raw = -jnp.maximum(z, 0.0) - jnp.log(den)
    l1m = raw if allowed is None else jnp.where(allowed, raw, 0.0)
    return l1m, z + raw, jnp.where(z >= 0.0, 1.0, e) / den


def _sb_passes(i, first_pass, inner_pass, last_pass, carry):
    n_pass = (i + SB_NB) // SB_NB
    carry = first_pass(0, carry)
    carry = lax.fori_loop(1, n_pass - 1, inner_pass, carry)
    return lax.cond(n_pass > 1, lambda c: last_pass(n_pass - 1, c), lambda c: c, carry)


def _sb_stack(a, i):
    first = _iota((1, HD), 1) < SB_DH
    a2 = jnp.concatenate([jnp.where(first, a, 0.0), jnp.where(first, 0.0, a)], axis=0).astype(BF16)
    rq = i * QB + _iota((QB, 1), 0)
    return a2, jnp.concatenate([rq, rq], axis=0), first


def _dot_hi_lo(a, b2):
    hi = a.astype(BF16)
    lo = (a - hi.astype(F32)).astype(BF16)
    return lax.dot_general(jnp.concatenate([hi, lo], axis=1), b2, (NN, ((), ())), preferred_element_type=F32)


def _cargo_bounds(refs, n, n_out, exchange, first, last):
    outs = refs[n:n + n_out]
    if not n:
        return outs, lambda: None
    ex = exchange[0](refs[:n], refs[n + n_out:2 * n + n_out], *refs[2 * n + n_out:])

    @pl.when(first)
    def _():
        ex.start()

    def finish():
        @pl.when(last)
        def _():
            ex.wait()

    return outs, finish


def _sb_fwd(p, pad, name, cargo=(), exchange=None):
    t_pad = p.shape[0]
    nq = t_pad // QB
    n = len(cargo)

    def body(q_ref, k_ref, v_ref, *rest):
        i = pl.program_id(1)
        pr = pl.program_id(0)
        (o_ref, r_ref), end_cargo = _cargo_bounds(rest, n, 2, exchange, (pr == 0) & (i == 0),
                                                  (pr == SB_H // 2 - 1) & (i == nq - 1))
        q2, rowq, first = _sb_stack(q_ref[...] * SB_SCALE, i)
        tri = (_iota((QB, QB), 0) > _iota((QB, QB), 1)).astype(BF16)
        upper2 = jnp.concatenate([tri, tri], axis=0)

        def chain(kb, masked):
            live = kb >= 0
            kb = jnp.maximum(kb, 0)
            start = pl.multiple_of(kb * QB, QB)
            kblk = k_ref[pl.ds(start, QB), :].astype(BF16)
            vblk = v_ref[pl.ds(start, QB), :].astype(BF16)
            z = lax.dot_general(q2, kblk, (NT, ((), ())), preferred_element_type=F32)
            allowed = None
            if masked:
                colk = kb * QB + _iota((1, QB), 1)
                allowed = (colk < rowq) & (colk >= pad) & live
            l1m, ls, _ = _sb_terms(z, allowed)
            return allowed, ls, _dot_hi_lo(l1m, upper2), jnp.sum(l1m, axis=1, keepdims=True), vblk

        def step(j, carry, masked):
            o_acc, run = carry
            ws, vs = [], []
            for n in range(SB_NB):
                allowed, ls, suf, rs, vblk = chain(i - SB_NB * j - n, masked)
                wgt = jnp.exp(ls + suf + run)
                ws.append((wgt if allowed is None else jnp.where(allowed, wgt, 0.0)).astype(BF16))
                vs.append(vblk)
                run = run + rs
            o_acc = o_acc + lax.dot_general(jnp.concatenate(ws, axis=1), jnp.concatenate(vs, axis=0),
                                            (NN, ((), ())), preferred_element_type=F32)
            return o_acc, run

        edge = lambda j, c: step(j, c, True)
        o_acc, run = _sb_passes(i, edge, lambda j, c: step(j, c, False), edge,
                                (jnp.zeros((2 * QB, HD), F32), jnp.zeros((2 * QB, 1), F32)))
        o_ref[...] = jnp.where(first, o_acc[:QB], o_acc[QB:]).astype(BF16)
        r_ref[...] = jnp.where(first, run[:QB], run[QB:])
        end_cargo()

    full = lambda off: pl.BlockSpec((t_pad, HD), lambda pr, i: (0, off + pr))
    blk = pl.BlockSpec((QB, HD), lambda pr, i: (i, pr))
    return pl.pallas_call(
        body, name=name, grid=(SB_H // 2, nq),
        in_specs=[pl.BlockSpec((QB, HD), lambda pr, i: (i, SB_Q0 + pr)), full(SB_K0), full(SB_V0)] + [ANY] * n,
        out_specs=[blk, blk] + [ANY] * n,
        out_shape=[jax.ShapeDtypeStruct((t_pad, SB_H * SB_DH), BF16), jax.ShapeDtypeStruct((t_pad, SB_H * SB_DH), F32)]
        + (exchange[1](cargo) if n else []),
        scratch_shapes=exchange[2](n) if n else [],
        compiler_params=_cparams(),
    )(p, p, p, *cargo)


def _sb_bwd(p, rtot, dy, dy_blk0, pad, name, cargo=(), exchange=None):
    t_pad = p.shape[0]
    nq = t_pad // QB
    n = len(cargo)

    def body(q_ref, k_ref, v_ref, r_ref, do_ref, *rest):
        i = pl.program_id(1)
        pr = pl.program_id(0)
        (dq_ref, dk_ref, dv_ref), end_cargo = _cargo_bounds(rest, n, 3, exchange, (pr == 0) & (i == 0),
                                                            (pr == SB_H // 2 - 1) & (i == nq - 1))

        @pl.when(i == 0)
        def _():
            dk_ref[...] = jnp.zeros_like(dk_ref)
            dv_ref[...] = jnp.zeros_like(dv_ref)

        q2, rowq, first = _sb_stack(q_ref[...] * SB_SCALE, i)
        do2, _, _ = _sb_stack(do_ref[...], i)
        rt = r_ref[...]
        lane = _iota((1, HD), 1)
        rcol = jnp.concatenate([jnp.sum(jnp.where(lane == 0, rt, 0.0), axis=1, keepdims=True),
                                jnp.sum(jnp.where(lane == SB_DH, rt, 0.0), axis=1, keepdims=True)], axis=0)
        rj = _iota((QB, QB), 0)
        cs = _iota((QB, QB), 1)
        tri_u = (rj > cs).astype(BF16)
        tri_l = (rj < cs).astype(BF16)
        upper2 = jnp.concatenate([tri_u, tri_u], axis=0)
        lower2 = jnp.concatenate([tri_l, tri_l], axis=0)

        def chain(kb, masked):
            live = kb <= i
            kb = jnp.minimum(kb, i)
            start = pl.multiple_of(kb * QB, QB)
            kblk = k_ref[pl.ds(start, QB), :].astype(BF16)
            vblk = v_ref[pl.ds(start, QB), :].astype(BF16)
            z = lax.dot_general(q2, kblk, (NT, ((), ())), preferred_element_type=F32)
            allowed = None
            if masked:
                colk = kb * QB + _iota((1, QB), 1)
                allowed = (colk < rowq) & (colk >= pad) & live
            l1m, ls, sg = _sb_terms(z, allowed)
            dwgt = lax.dot_general(do2, vblk, (NT, ((), ())), preferred_element_type=F32)
            return start, kblk, allowed, ls, _dot_hi_lo(l1m, upper2), jnp.sum(l1m, axis=1, keepdims=True), dwgt, sg

        def finish(c, seen, gseen):
            start, kblk, allowed, ls, suf, rs, dwgt, sg = c
            wgt = jnp.exp(ls + suf + (rcol - seen - rs))
            if allowed is not None:
                wgt = jnp.where(allowed, wgt, 0.0)
            dl = dwgt * wgt
            gpre = gseen + _dot_hi_lo(dl, lower2)
            dz = dl - sg * (dl + gpre)
            if allowed is not None:
                dz = jnp.where(allowed, dz, 0.0)
            dz = dz.astype(BF16)
            dk_ref[pl.ds(start, QB), :] += lax.dot_general(dz, q2, (TN, ((), ())), preferred_element_type=F32)
            dv_ref[pl.ds(start, QB), :] += lax.dot_general(wgt.astype(BF16), do2, (TN, ((), ())),
                                                           preferred_element_type=F32)
            return dz, seen + rs, gseen + jnp.sum(dl, axis=1, keepdims=True)

        def step(j, carry, masked):
            dq_acc, seen, gseen = carry
            cs_ = [chain(SB_NB * j + n, masked) for n in range(SB_NB)]
            dzs = []
            for c in cs_:
                dz, seen, gseen = finish(c, seen, gseen)
                dzs.append(dz)
            dq_acc = dq_acc + lax.dot_general(jnp.concatenate(dzs, axis=1), jnp.concatenate([c[1] for c in cs_], axis=0),
                                              (NN, ((), ())), preferred_element_type=F32)
            return dq_acc, seen, gseen

        zc = jnp.zeros((2 * QB, 1), F32)
        edge = lambda j, c: step(j, c, True)
        dq_acc, _, _ = _sb_passes(i, edge, lambda j, c: step(j, c, False), edge, (jnp.zeros((2 * QB, HD), F32), zc, zc))
        dq_ref[...] = jnp.where(first, dq_acc[:QB], dq_acc[QB:]) * SB_SCALE
        end_cargo()

    full_in = lambda off: pl.BlockSpec((t_pad, HD), lambda pr, i: (0, off + pr))
    full_out = pl.BlockSpec((t_pad, HD), lambda pr, i: (0, pr))
    blk = pl.BlockSpec((QB, HD), lambda pr, i: (i, pr))
    sds = jax.ShapeDtypeStruct((t_pad, SB_H * SB_DH), F32)
    return pl.pallas_call(
        body, name=name, grid=(SB_H // 2, nq),
        in_specs=[pl.BlockSpec((QB, HD), lambda pr, i: (i, SB_Q0 + pr)), full_in(SB_K0), full_in(SB_V0), blk,
                  pl.BlockSpec((QB, HD), lambda pr, i: (i, dy_blk0 + pr))] + [ANY] * n,
        out_specs=[blk, full_out, full_out] + [ANY] * n,
        out_shape=[sds, sds, sds] + (exchange[1](cargo) if n else []),
        scratch_shapes=exchange[2](n) if n else [],
        compiler_params=_cparams(),
    )(p, p, p, rtot, dy, *cargo)


HG_LEVELS = 6


def _hg_prefix_matrix():
    t = np.arange(CH)[:, None]
    j = np.arange(CH)[None, :]
    groups = [(j <= t)]
    for lvl in range(1, HG_LEVELS + 1):
        half = CH >> lvl
        e = (t // (2 * half)) * (2 * half) + half - 1
        groups.append(j <= e)
    groups.append(np.ones((2 * CH, CH), bool))
    return np.concatenate(groups, axis=0).astype(np.float32)


HG_G = 4


def _hg_chunk(qr, fr, iv, r0, r1, st, valid, ecat):
    g = st.shape[0]
    mx = jnp.maximum(r0, r1)
    e0 = jnp.exp(r0 - mx)
    e1 = jnp.exp(r1 - mx)
    lb = e1 / (e0 + e1)
    fg = lb + (1.0 - lb) * _sigmoid(fr)
    logf = jnp.where(valid, jnp.log(fg), 0.0)
    kk = jnp.where(valid, 1.0 - fg, 0.0)
    q = jnp.where(valid, _silu(qr), 0.0)
    v = _heads(jnp.where(valid, iv, 0.0), g)

    pre = hnn(ecat, logf)
    b = pre[0:CH]
    b_last = pre[(HG_LEVELS + 1) * CH:]
    row = _iota((CH, 1), 0)
    ri = _iota((1, CH, CH), 1)
    ci = _iota((1, CH, CH), 2)
    a = jnp.where(ri == ci, jnp.sum(_heads(q * kk, g), axis=2, keepdims=True), 0.0)
    for lvl in range(1, HG_LEVELS + 1):
        half = CH >> lvl
        m = pre[lvl * CH:(lvl + 1) * CH]
        low = (row & half) != 0
        qt = jnp.where(low, q * jnp.exp(jnp.where(low, b - m, 0.0)), 0.0)
        kt = jnp.where(low, 0.0, kk * jnp.exp(jnp.where(low, 0.0, m - b)))
        same = (ri >> (7 - lvl)) == (ci >> (7 - lvl))
        a = a + jnp.where(same, bbnt(_heads(qt, g), _heads(kt, g)), 0.0)
    o = bbnt(_heads(q * jnp.exp(b), g), st) + bbnn(a, v)
    kd = kk * jnp.exp(b_last[0:CH] - b)
    st_new = st * _heads(jnp.exp(b_last), g) + bbtn(v, _heads(kd, g))
    return o, st_new


def _hg_specs(npair, rev):
    cc = (lambda c: npair - 1 - c) if rev else (lambda c: c)
    ng = HG_H // HG_G
    blk = lambda off: pl.BlockSpec((PAIR, HG_G * HD), lambda h, c: (cc(c), off * ng + h))
    lbs = pl.BlockSpec((2, HG_G * HD), lambda h, c: (0, h))
    state = pl.BlockSpec((1, HG_G, HD, HD), lambda h, c: (cc(c), h, 0, 0))
    return ng, blk, lbs, state


def _hg_fwd(p, lbraw, ecat, pad, name):
    t_pad = p.shape[0]
    npair = t_pad // PAIR
    ng, blk, lbs, state = _hg_specs(npair, False)

    def body(q_ref, f_ref, i_ref, lb_ref, e_ref, o_ref, ss_ref, s_ref):
        c = pl.program_id(1)

        @pl.when(c == 0)
        def _():
            s_ref[...] = jnp.zeros_like(s_ref)

        st = s_ref[...]
        ss_ref[0] = st
        for half in (0, 1):
            r = _chunk_rows(half)
            o, st = _hg_chunk(q_ref[r, :], f_ref[r, :], i_ref[r, :], lb_ref[0:1, :], lb_ref[1:2, :], st,
                              _chunk_valid(c, half, pad), e_ref[...])
            _store_heads(o_ref, o, r)
        s_ref[...] = st

    return pl.pallas_call(
        body, name=name, grid=(ng, npair),
        in_specs=[blk(0), blk(1), blk(2), lbs, pl.BlockSpec(ecat.shape, lambda h, c: (0, 0))],
        out_specs=[blk(0), state],
        out_shape=[jax.ShapeDtypeStruct((t_pad, HG_H * HD), F32), jax.ShapeDtypeStruct((npair, HG_H, HD, HD), F32)],
        scratch_shapes=[pltpu.VMEM((HG_G, HD, HD), F32)],
        compiler_params=_cparams(),
    )(p, p, p, lbraw, ecat)


def _hg_bwd(p, lbraw, ecat, ssave, do, pad, name, cargo=(), exchange=None):
    t_pad = p.shape[0]
    npair = t_pad // PAIR
    ng, blk, lbs, state = _hg_specs(npair, True)
    n = len(cargo)

    def body(q_ref, f_ref, i_ref, lb_ref, e_ref, ss_ref, do_ref, *rest):
        c = pl.program_id(1)
        hg = pl.program_id(0)
        ds_ref = rest[-1]
        (dq_ref, df_ref, di_ref, dlb_ref), end_cargo = _cargo_bounds(
            rest[:-1], n, 4, exchange, (hg == 0) & (c == 0), (hg == ng - 1) & (c == npair - 1))

        @pl.when(c == 0)
        def _():
            ds_ref[...] = jnp.zeros_like(ds_ref)
            dlb_ref[...] = jnp.zeros_like(dlb_ref)

        ra, rb = _chunk_rows(0), _chunk_rows(1)
        va, vb = _chunk_valid(npair - 1 - c, 0, pad), _chunk_valid(npair - 1 - c, 1, pad)
        ecv = e_ref[...]

        def pair(qa, fa, ia, qb, fb, ib, r0, r1, st):
            oa, st = _hg_chunk(qa, fa, ia, r0, r1, st, va, ecv)
            ob, st = _hg_chunk(qb, fb, ib, r0, r1, st, vb, ecv)
            return oa, ob, st

        ins = [ref[r, :] for r in (ra, rb) for ref in (q_ref, f_ref, i_ref)]
        _, vjp = jax.vjp(pair, *ins, lb_ref[0:1, :], lb_ref[1:2, :], ss_ref[0])
        g = vjp((_heads(do_ref[ra, :], HG_G), _heads(do_ref[rb, :], HG_G), ds_ref[...]))
        for r, (dq, df, di) in ((ra, g[0:3]), (rb, g[3:6])):
            dq_ref[r, :] = dq
            df_ref[r, :] = df
            di_ref[r, :] = di
        dlb_ref[0:1, :] += g[6]
        dlb_ref[1:2, :] += g[7]
        ds_ref[...] = g[8]
        end_cargo()

    sds = jax.ShapeDtypeStruct((t_pad, HG_H * HD), F32)
    return pl.pallas_call(
        body, name=name, grid=(ng, npair),
        in_specs=[blk(0), blk(1), blk(2), lbs, pl.BlockSpec(ecat.shape, lambda h, c: (0, 0)), state, blk(0)] + [ANY] * n,
        out_specs=[blk(0), blk(0), blk(0), lbs] + [ANY] * n,
        out_shape=[sds, sds, sds, jax.ShapeDtypeStruct((2, HG_H * HD), F32)] + (exchange[1](cargo) if n else []),
        scratch_shapes=(exchange[2](n) if n else []) + [pltpu.VMEM((HG_G, HD, HD), F32)],
        compiler_params=_cparams(),
    )(p, p, p, lbraw, ecat, ssave, do, *cargo)


def _pad_ab_cols(w):
    z = jnp.zeros((w.shape[0], HD - GDN_H), w.dtype)
    return jnp.concatenate([w[:, :2048], w[:, 2048:2052], z, w[:, 2052:2056], z, w[:, 2056:]], axis=1)


def _unpad_ab_cols(w):
    return jnp.concatenate([w[:, :2048], w[:, 2048:2052], w[:, 2176:2180], w[:, 2304:]], axis=1)


def _lane_pad(v):
    return jnp.pad(v, ((0, 0), (0, HD - v.shape[1])))


def _mlp_fwd(hb, w1, w2, layer):
    a, r = _mm(hb, w1, b_view=("cols", layer), out_dtype=BF16, act=True, name=f"mlp_up_{layer}")
    m = _mm(r, w2, b_view=("rows", layer), name=f"mlp_down_{layer}")
    return a, r, m


def _mlp_bwd(hb, a, r, dmb, w1, w2, layer):
    da = _mm(dmb, w2, tb=True, b_view=("rows", layer), out_dtype=BF16, gate=a, name=f"mlp_down_dx_{layer}")
    dw2 = _mm(r, dmb, ta=True, out_dtype=BF16, name=f"mlp_down_dw_{layer}")
    dh = _mm(da, w1, tb=True, b_view=("cols", layer), name=f"mlp_up_dx_{layer}")
    dw1 = _mm(hb, da, ta=True, out_dtype=BF16, out_split=N_CHIP, name=f"mlp_up_dw_{layer}")
    return dh, dw1, dw2


def _local_step(h0, tgt, w, pad, late=None):
    row = lambda a, i: a[i:i + 1]
    ecat = jnp.asarray(_hg_prefix_matrix())
    cw = [w["conv_w"][:, i * 512:(i + 1) * 512] for i in range(3)]
    alog, dtb = _lane_pad(w["a_log"]), _lane_pad(w["dt_bias"])

    h0b = h0.astype(BF16)
    p0 = _mm(h0b, w["ab_w_in"], name="ab_in")
    qn = _conv_fwd(p0, 0, cw[0], "q", pad, "conv_q")
    kn = _conv_fwd(p0, 4, cw[1], "k", pad, "conv_k")
    vn = _conv_fwd(p0, 8, cw[2], "v", pad, "conv_v")
    oa_raw, ss0 = _gdn_fwd(qn, kn, vn, p0, alog, dtb, pad, "gdn_fwd")
    oa = _grms_fwd(oa_raw, p0, 12, w["ab_gnorm_g"], "gdn_gate")
    if late is None:
        ob, rtot = _sb_fwd(p0, pad, "sb_fwd")
    else:
        ob, rtot, g_about, g_cin, g_cout, g_w1, g_w2 = _sb_fwd(p0, pad, "sb_fwd", cargo=late, exchange=GATHER)
        w = dict(w, ab_w_out=g_about.reshape(D, D), c_w_in=g_cin, c_w_out=g_cout.reshape(D, D), mlp_w1=g_w1, mlp_w2=g_w2)
    ycat = jnp.concatenate([oa, ob], axis=1)
    mix0 = _mm(ycat, w["ab_w_out"], name="ab_out")
    h1, h1b = _ln_res_fwd(h0, mix0, row(w["ln_mix_g"], 0), row(w["ln_mix_b"], 0), "ln_mix_0")
    a0, r0, m0 = _mlp_fwd(h1b, w["mlp_w1"], w["mlp_w2"], 0)
    h2, h2b = _ln_res_fwd(h1, m0, row(w["ln_ffn_g"], 0), row(w["ln_ffn_b"], 0), "ln_ffn_0")
    p1 = _mm(h2b, w["c_w_in"], b_view=("cols", 0), name="c_in")
    oc_raw, ss1 = _hg_fwd(p1, w["c_lb_raw"], ecat, pad, "hg_fwd")
    yc = _grms_fwd(oc_raw, p1, 3 * HG_H, w["c_gnorm_g"], "hg_gate")
    mix1 = _mm(yc, w["c_w_out"], name="c_out")
    h3, h3b = _ln_res_fwd(h2, mix1, row(w["ln_mix_g"], 1), row(w["ln_mix_b"], 1), "ln_mix_1")
    a1, r1, m1 = _mlp_fwd(h3b, w["mlp_w1"], w["mlp_w2"], 1)
    h4, _ = _ln_res_fwd(h3, m1, row(w["ln_ffn_g"], 1), row(w["ln_ffn_b"], 1), "ln_ffn_1")
    loss, dh4 = _loss_fwd(h4, tgt, pad + N_META, "loss")

    zero = jnp.zeros_like(dh4)
    dh3a, dm1b, dfg1, dfb1 = _ln_res_bwd(h3, m1, row(w["ln_ffn_g"], 1), row(w["ln_ffn_b"], 1), dh4, zero, "ln_ffn_bwd_1")
    dh3b, dw1_1, dw2_1 = _mlp_bwd(h3b, a1, r1, dm1b, w["mlp_w1"], w["mlp_w2"], 1)
    dh2a, dmix1b, dmg1, dmb1 = _ln_res_bwd(h2, mix1, row(w["ln_mix_g"], 1), row(w["ln_mix_b"], 1), dh3a, dh3b, "ln_mix_bwd_1")
    dyc = _mm(dmix1b, w["c_w_out"], tb=True, name="c_out_dx")
    dwco = _mm(yc, dmix1b, ta=True, out_dtype=BF16, name="c_out_dw")
    doc, dzc, dcg = _grms_bwd(oc_raw, p1, 3 * HG_H, w["c_gnorm_g"], dyc, 0, "hg_gate_bwd")
    landed = {}
    rows4 = lambda a: a.reshape(N_CHIP, -1, D)
    if late is None:
        dq1, df1, di1, dlb = _hg_bwd(p1, w["c_lb_raw"], ecat, ss1, doc, pad, "hg_bwd")
    else:
        dq1, df1, di1, dlb, landed["w1_1"], landed["w2_1"], landed["c_w_out"] = _hg_bwd(
            p1, w["c_lb_raw"], ecat, ss1, doc, pad, "hg_bwd", cargo=[dw1_1, rows4(dw2_1), rows4(dwco)], exchange=SCATTER)
    dp1 = _assemble_bf16([(dq1, "cols"), (df1, "cols"), (di1, "cols"), (dzc, "cols")], "c_in_dy")
    dh2b = _mm(dp1, w["c_w_in"], tb=True, b_view=("cols", 0), name="c_in_dx")
    dwc = _mm(h2b, dp1, ta=True, out_dtype=BF16, out_split=N_CHIP, name="c_in_dw")
    dh1a, dm0b, dfg0, dfb0 = _ln_res_bwd(h1, m0, row(w["ln_ffn_g"], 0), row(w["ln_ffn_b"], 0), dh2a, dh2b, "ln_ffn_bwd_0")
    dh1b, dw1_0, dw2_0 = _mlp_bwd(h1b, a0, r0, dm0b, w["mlp_w1"], w["mlp_w2"], 0)
    dh0a, dmix0b, dmg0, dmb0 = _ln_res_bwd(h0, mix0, row(w["ln_mix_g"], 0), row(w["ln_mix_b"], 0), dh1a, dh1b, "ln_mix_bwd_0")
    dycat = _mm(dmix0b, w["ab_w_out"], tb=True, name="ab_out_dx")
    dwabo = _mm(ycat, dmix0b, ta=True, out_dtype=BF16, name="ab_out_dw")
    doa, dza, dag = _grms_bwd(oa_raw, p0, 12, w["ab_gnorm_g"], dycat, 0, "gdn_gate_bwd")
    if late is None:
        dqn, dkn, dvn, dbb, daa, dal, ddt = _gdn_bwd(qn, kn, vn, p0, alog, dtb, ss0, doa, pad, "gdn_bwd")
        dqb, dkb, dvb = _sb_bwd(p0, rtot, dycat, 4, pad, "sb_bwd")
    else:
        dqn, dkn, dvn, dbb, daa, dal, ddt, landed["c_w_in"], landed["w2_0"] = _gdn_bwd(
            qn, kn, vn, p0, alog, dtb, ss0, doa, pad, "gdn_bwd", cargo=[dwc, rows4(dw2_0)], exchange=SCATTER)
        dqb, dkb, dvb, landed["w1_0"], landed["ab_w_out"] = _sb_bwd(
            p0, rtot, dycat, 4, pad, "sb_bwd", cargo=[dw1_0, rows4(dwabo)], exchange=SCATTER)
    dpq, dcq = _conv_bwd(p0, 0, cw[0], dqn, "q", pad, "conv_q_bwd")
    dpk, dck = _conv_bwd(p0, 4, cw[1], dkn, "k", pad, "conv_k_bwd")
    dpv, dcv = _conv_bwd(p0, 8, cw[2], dvn, "v", pad, "conv_v_bwd")
    dp0 = _assemble_bf16([(dpq, "cols"), (dpk, "cols"), (dpv, "cols"), (dza, "cols"), (dbb, "cols"), (daa, "cols"),
                          (dqb, "cols"), (dkb, "cols"), (dvb, "cols")], "ab_in_dy")
    dh0b = _mm(dp0, w["ab_w_in"], tb=True, name="ab_in_dx")
    dwab = _mm(h0b, dp0, ta=True, out_dtype=BF16, name="ab_in_dw")
    dh0 = _add2(dh0a, dh0b, "dh0")

    grads = {
        "ab_w_in": dwab, "conv_w": jnp.concatenate([dcq, dck, dcv], axis=1),
        "a_log": dal[:, :GDN_H], "dt_bias": ddt[:, :GDN_H],
        "ab_gnorm_g": dag, "ab_w_out": dwabo, "c_w_in": dwc, "c_lb_raw": dlb, "c_gnorm_g": dcg, "c_w_out": dwco,
        "ln_mix_g": jnp.concatenate([dmg0, dmg1], 0), "ln_mix_b": jnp.concatenate([dmb0, dmb1], 0),
        "w1_0": dw1_0, "w1_1": dw1_1, "w2_0": dw2_0, "w2_1": dw2_1,
        "ln_ffn_g": jnp.concatenate([dfg0, dfg1], 0), "ln_ffn_b": jnp.concatenate([dfb0, dfb1], 0),
        "landed": landed,
    }
    return loss, dh0, grads


MESH = pl.DeviceIdType.MESH
ANY = pl.BlockSpec(memory_space=pl.ANY)
N_CHIP = 4
N_DEV = 8
CHIP_REL = ((1, 0), (0, 1), (1, 1))
DEV_REL = tuple((dx, dy, dc) for dx in (0, 1) for dy in (0, 1) for dc in (0, 1))[1:]

def _pos():
    return lax.axis_index("x"), lax.axis_index("y"), lax.axis_index("c")


def _flip(a, d):
    return a + d - 2 * a * d


class _Exchange:
    def __init__(self, local, sends, recvs):
        self.local, self.sends, self.recvs = local, sends, recvs

    def start(self):
        for cp in self.local + self.sends:
            cp.start()

    def wait(self):
        for cp in self.recvs:
            cp.wait_recv()
        for cp in self.sends:
            cp.wait_send()
        for cp in self.local:
            cp.wait()


def _gather_sems(n):
    return [pltpu.SemaphoreType.DMA((3 * n,)), pltpu.SemaphoreType.DMA((3 * n,)), pltpu.SemaphoreType.DMA((n,))]


def _gather_copies(x_refs, o_refs, send_sems, recv_sems, local_sems):
    n = len(x_refs)
    x, y, c = _pos()
    local = [pltpu.make_async_copy(x_refs[a], o_refs[a].at[2 * x + y], local_sems.at[a]) for a in range(n)]

    def copy(a, k, sending):
        tx, ty = _flip(x, CHIP_REL[k][0]), _flip(y, CHIP_REL[k][1])
        return pltpu.make_async_remote_copy(
            src_ref=x_refs[a], dst_ref=o_refs[a].at[2 * x + y if sending else 2 * tx + ty],
            send_sem=send_sems.at[3 * a + k], recv_sem=recv_sems.at[3 * a + k], device_id=(tx, ty, c), device_id_type=MESH)

    pairs = [(a, k) for a in range(n) for k in range(3)]
    return _Exchange(local, [copy(a, k, True) for a, k in pairs], [copy(a, k, False) for a, k in pairs])


def _gather_shapes(bufs):
    return [jax.ShapeDtypeStruct((N_CHIP,) + b.shape, b.dtype) for b in bufs]


def _chip_allgather(bufs, name):
    n = len(bufs)

    def body(*refs):
        ex = _gather_copies(refs[:n], refs[n:2 * n], *refs[2 * n:])
        ex.start()
        ex.wait()

    return pl.pallas_call(
        body, name=name, in_specs=[ANY] * n, out_specs=[ANY] * n, out_shape=_gather_shapes(bufs),
        scratch_shapes=_gather_sems(n), compiler_params=pltpu.CompilerParams(has_side_effects=True),
    )(*bufs)


def _scatter_sems(n):
    nr = N_DEV - 1
    return [pltpu.SemaphoreType.DMA((nr * n,)), pltpu.SemaphoreType.DMA((nr * n,)), pltpu.SemaphoreType.DMA((n,))]


def _scatter_copies(g_refs, o_refs, send_sems, recv_sems, local_sems):
    n = len(g_refs)
    nr = N_DEV - 1
    x, y, c = _pos()
    me = 4 * x + 2 * y + c
    local = [pltpu.make_async_copy(g_refs[a].at[2 * x + y], o_refs[a].at[me], local_sems.at[a]) for a in range(n)]

    def copy(a, k, sending):
        dx, dy, dc = DEV_REL[k]
        tx, ty, tc = _flip(x, dx), _flip(y, dy), _flip(c, dc)
        return pltpu.make_async_remote_copy(
            src_ref=g_refs[a].at[2 * tx + ty], dst_ref=o_refs[a].at[me if sending else 4 * tx + 2 * ty + tc],
            send_sem=send_sems.at[nr * a + k], recv_sem=recv_sems.at[nr * a + k],
            device_id=(tx, ty, tc), device_id_type=MESH)

    pairs = [(a, k) for a in range(n) for k in range(nr)]
    return _Exchange(local, [copy(a, k, True) for a, k in pairs], [copy(a, k, False) for a, k in pairs])


def _scatter_shapes(gs):
    return [jax.ShapeDtypeStruct((N_DEV,) + g.shape[1:], g.dtype) for g in gs]


def _grad_alltoall(gs, name):
    n = len(gs)

    def body(*refs):
        ex = _scatter_copies(refs[:n], refs[n:2 * n], *refs[2 * n:])
        ex.start()
        ex.wait()

    return pl.pallas_call(
        body, name=name, in_specs=[ANY] * n, out_specs=[ANY] * n, out_shape=_scatter_shapes(gs),
        scratch_shapes=_scatter_sems(n), compiler_params=pltpu.CompilerParams(has_side_effects=True),
    )(*gs)


GATHER = (_gather_copies, _gather_shapes, _gather_sems)
SCATTER = (_scatter_copies, _scatter_shapes, _scatter_sems)


def _sum_slots(r, name):
    n, rh, w = r.shape
    tr = _pick(rh, (256, 128, 64, 16))

    def body(r_ref, o_ref):
        acc = r_ref[0].astype(F32)
        for s in range(1, n):
            acc = acc + r_ref[s].astype(F32)
        o_ref[...] = acc

    return pl.pallas_call(
        body, name=name, grid=(rh // tr,), in_specs=[pl.BlockSpec((n, tr, w), lambda i: (0, i, 0))],
        out_specs=pl.BlockSpec((tr, w), lambda i: (i, 0)), out_shape=jax.ShapeDtypeStruct((rh, w), F32),
        compiler_params=_cparams(),
    )(r)


def _small_allreduce(buf, name):
    r, w = buf.shape

    def body(b_ref, o_ref, land_ref, send_sems, recv_sems):
        x, y, c = _pos()
        me = 4 * x + 2 * y + c
        land_ref[me] = b_ref[...]

        def target(k):
            dx, dy, dc = DEV_REL[k]
            return _flip(x, dx), _flip(y, dy), _flip(c, dc)

        sends = []
        for k in range(N_DEV - 1):
            tx, ty, tc = target(k)
            cp = pltpu.make_async_remote_copy(
                src_ref=b_ref, dst_ref=land_ref.at[me], send_sem=send_sems.at[k], recv_sem=recv_sems.at[k],
                device_id=(tx, ty, tc), device_id_type=MESH)
            cp.start()
            sends.append(cp)
        for k in range(N_DEV - 1):
            tx, ty, tc = target(k)
            pltpu.make_async_remote_copy(
                src_ref=b_ref, dst_ref=land_ref.at[4 * tx + 2 * ty + tc], send_sem=send_sems.at[k],
                recv_sem=recv_sems.at[k], device_id=(tx, ty, tc), device_id_type=MESH).wait_recv()
        for cp in sends:
            cp.wait_send()
        acc = land_ref[0]
        for s in range(1, N_DEV):
            acc = acc + land_ref[s]
        o_ref[...] = acc

    vm = pl.BlockSpec(memory_space=pltpu.VMEM)
    return pl.pallas_call(
        body, name=name, in_specs=[vm], out_specs=vm, out_shape=jax.ShapeDtypeStruct((r, w), F32),
        scratch_shapes=[pltpu.VMEM((N_DEV, r, w), F32), pltpu.SemaphoreType.DMA((N_DEV - 1,)),
                        pltpu.SemaphoreType.DMA((N_DEV - 1,))],
        compiler_params=pltpu.CompilerParams(has_side_effects=True),
    )(buf)


def _adamw(w, g, m, v, name):
    r, c = w.shape
    tr = _pick(r, (256, 128, 64, 8)) if r * c > (1 << 18) else r

    def body(w_ref, g_ref, m_ref, v_ref, d_ref, m2_ref, v2_ref):
        gg = g_ref[...]
        m2 = ADAM_B1 * m_ref[...] + (1.0 - ADAM_B1) * gg
        v2 = ADAM_B2 * v_ref[...] + (1.0 - ADAM_B2) * (gg * gg)
        m_hat = m2 / (1.0 - ADAM_B1 ** ADAM_STEP)
        v_hat = v2 / (1.0 - ADAM_B2 ** ADAM_STEP)
        d_ref[...] = -ADAM_LR * (m_hat / (jnp.sqrt(v_hat) + ADAM_EPS) + ADAM_WD * w_ref[...])
        m2_ref[...] = m2
        v2_ref[...] = v2

    blk = pl.BlockSpec((tr, c), lambda i: (i, 0))
    sds = jax.ShapeDtypeStruct((r, c), F32)
    return pl.pallas_call(body, name=name, grid=(r // tr,), in_specs=[blk] * 4, out_specs=[blk] * 3,
                          out_shape=[sds] * 3, compiler_params=_cparams())(w, g, m, v)


BIG = ("ab_w_in", "ab_w_out", "c_w_in", "c_w_out", "mlp_w1", "mlp_w2")
SMALL = ("ln_mix_g", "ln_mix_b", "ln_ffn_g", "ln_ffn_b", "c_lb_raw", "ab_a_log", "ab_dt_bias", "ab_gnorm_g", "c_gnorm_g")
SMALL_ROWS = 16
CONV_ROWS = 8
CONV_W = 3 * GDN_H * HD


def _conv_to_rows(cw):
    return jnp.pad(cw, ((0, 0), (0, 2 * D - CONV_W))).reshape(CONV_ROWS, D)


def _rows_to_conv(rows):
    return rows.reshape(CONV_K, 2 * D)[:, :CONV_W]


def _pack_small(d):
    rows = [jnp.pad(d[n], ((0, 0), (0, D - d[n].shape[1]))) for n in SMALL]
    buf = jnp.concatenate(rows, axis=0)
    return jnp.pad(buf, ((0, SMALL_ROWS - buf.shape[0]), (0, 0)))


def _unpack_small(buf, like):
    out, r = {}, 0
    for n in SMALL:
        nr, nc = like[n].shape
        out[n] = buf[r:r + nr, :nc]
        r += nr
    return out


def kernel(x, meta_tokens, ab_w_in, ab_conv_w, ab_a_log, ab_dt_bias, ab_gnorm_g, ab_w_out, c_w_in, c_lb_raw, c_gnorm_g, c_w_out, ln_mix_g, ln_mix_b, mlp_w1, mlp_w2, ln_ffn_g, ln_ffn_b, loss_target, m_meta_tokens, m_ab_w_in, m_ab_conv_w, m_ab_a_log, m_ab_dt_bias, m_ab_gnorm_g, m_ab_w_out, m_c_w_in, m_c_lb_raw, m_c_gnorm_g, m_c_w_out, m_ln_mix_g, m_ln_mix_b, m_mlp_w1, m_mlp_w2, m_ln_ffn_g, m_ln_ffn_b, v_meta_tokens, v_ab_w_in, v_ab_conv_w, v_ab_a_log, v_ab_dt_bias, v_ab_gnorm_g, v_ab_w_out, v_c_w_in, v_c_lb_raw, v_c_gnorm_g, v_c_w_out, v_ln_mix_g, v_ln_mix_b, v_mlp_w1, v_mlp_w2, v_ln_ffn_g, v_ln_ffn_b):
    names = ("meta_tokens", "ab_w_in", "ab_conv_w", "ab_a_log", "ab_dt_bias", "ab_gnorm_g", "ab_w_out", "c_w_in",
             "c_lb_raw", "c_gnorm_g", "c_w_out", "ln_mix_g", "ln_mix_b", "mlp_w1", "mlp_w2", "ln_ffn_g", "ln_ffn_b")
    wts = dict(zip(names, (meta_tokens, ab_w_in, ab_conv_w, ab_a_log, ab_dt_bias, ab_gnorm_g, ab_w_out, c_w_in, c_lb_raw,
                           c_gnorm_g, c_w_out, ln_mix_g, ln_mix_b, mlp_w1, mlp_w2, ln_ffn_g, ln_ffn_b)))
    mom_m = dict(zip(names, (m_meta_tokens, m_ab_w_in, m_ab_conv_w, m_ab_a_log, m_ab_dt_bias, m_ab_gnorm_g, m_ab_w_out,
                             m_c_w_in, m_c_lb_raw, m_c_gnorm_g, m_c_w_out, m_ln_mix_g, m_ln_mix_b, m_mlp_w1, m_mlp_w2,
                             m_ln_ffn_g, m_ln_ffn_b)))
    mom_v = dict(zip(names, (v_meta_tokens, v_ab_w_in, v_ab_conv_w, v_ab_a_log, v_ab_dt_bias, v_ab_gnorm_g, v_ab_w_out,
                             v_c_w_in, v_c_lb_raw, v_c_gnorm_g, v_c_w_out, v_ln_mix_g, v_ln_mix_b, v_mlp_w1, v_mlp_w2,
                             v_ln_ffn_g, v_ln_ffn_b)))
    seq = x.shape[1]
    pad = (-(N_META + seq)) % QB
    xi, yi, ci = _pos()
    chip = 2 * xi + yi

    gat_ab_in, = _chip_allgather([ab_w_in[0].astype(BF16)], "gather_weights")
    late = [ab_w_out[0].astype(BF16), c_w_in.astype(BF16), c_w_out[0].astype(BF16), mlp_w1.astype(BF16),
            mlp_w2.astype(BF16)]
    mcols, ccols = meta_tokens.shape[1], ab_conv_w.shape[2]
    place = jnp.concatenate([
        lax.dynamic_update_slice(jnp.zeros((N_META, D), F32), 0.5 * meta_tokens, (0, chip * mcols)),
        _conv_to_rows(lax.dynamic_update_slice(jnp.zeros((CONV_K, CONV_W), F32), 0.5 * ab_conv_w[0], (0, chip * ccols)))],
        axis=0)
    placed = _small_allreduce(place, "gather_meta")
    meta_full = placed[:N_META]

    w = {
        "ab_w_in": _pad_ab_cols(jnp.transpose(gat_ab_in, (1, 0, 2)).reshape(D, AB_TRUE)),
        "conv_w": _rows_to_conv(placed[N_META:]), "a_log": ab_a_log, "dt_bias": ab_dt_bias,
        "ab_gnorm_g": ab_gnorm_g, "c_lb_raw": c_lb_raw,
        "c_gnorm_g": c_gnorm_g, "ln_mix_g": ln_mix_g, "ln_mix_b": ln_mix_b, "ln_ffn_g": ln_ffn_g, "ln_ffn_b": ln_ffn_b,
    }

    h0 = jnp.concatenate([jnp.zeros((pad, D), F32), meta_full, x[0]], axis=0)
    tgt = jnp.concatenate([jnp.zeros((pad + N_META, D), F32), loss_target[0]], axis=0)
    loss8, dh0, g = _local_step(h0, tgt, w, pad, late)
    loss = lax.psum(loss8[0, 0], ("x", "y", "c"))
    grad_x = dh0[pad + N_META:][None]

    gsmall = {"ln_mix_g": g["ln_mix_g"], "ln_mix_b": g["ln_mix_b"], "ln_ffn_g": g["ln_ffn_g"], "ln_ffn_b": g["ln_ffn_b"],
              "c_lb_raw": g["c_lb_raw"], "ab_a_log": g["a_log"], "ab_dt_bias": g["dt_bias"], "ab_gnorm_g": g["ab_gnorm_g"],
              "c_gnorm_g": g["c_gnorm_g"]}
    sbuf = jnp.concatenate([_pack_small(gsmall), dh0[pad:pad + N_META], _conv_to_rows(g["conv_w"])], axis=0)
    ssum = _small_allreduce(sbuf, "allreduce_small")
    grads = _unpack_small(ssum[:SMALL_ROWS], wts)
    grads["meta_tokens"] = lax.dynamic_slice(ssum[SMALL_ROWS:SMALL_ROWS + N_META], (0, chip * mcols), (N_META, mcols))
    grads["ab_conv_w"] = lax.dynamic_slice(_rows_to_conv(ssum[SMALL_ROWS + N_META:]), (0, chip * ccols), (CONV_K, ccols))[None]

    dab = jnp.transpose(_unpad_ab_cols(g["ab_w_in"]).reshape(D, N_CHIP, AB_TRUE // N_CHIP), (1, 0, 2))
    landed = dict(g["landed"])
    landed["ab_w_in"], = _grad_alltoall([dab], "grad_alltoall")
    sums = {k: _sum_slots(v, f"grad_sum_{k}") for k, v in landed.items()}
    for n in ("ab_w_in", "ab_w_out", "c_w_in", "c_w_out"):
        grads[n] = sums[n][None]
    grads["mlp_w1"] = jnp.stack([sums["w1_0"], sums["w1_1"]])
    grads["mlp_w2"] = jnp.stack([sums["w2_0"], sums["w2_1"]])

    delta, new_m, new_v = {}, {}, {}
    for n in ("meta_tokens", "ab_conv_w") + BIG:
        shp = wts[n].shape
        to2 = lambda a: a.reshape(-1, shp[-1])
        d2, m2, v2 = _adamw(to2(wts[n]), to2(grads[n]), to2(mom_m[n]), to2(mom_v[n]), f"adamw_{n}")
        delta[n], new_m[n], new_v[n] = d2.reshape(shp), m2.reshape(shp), v2.reshape(shp)
    d2, m2, v2 = _adamw(_pack_small(wts), ssum[:SMALL_ROWS], _pack_small(mom_m), _pack_small(mom_v), "adamw_small")
    delta.update(_unpack_small(d2, wts))
    new_m.update(_unpack_small(m2, wts))
    new_v.update(_unpack_small(v2, wts))

    return (loss, grad_x, *[grads[n] for n in names], *[delta[n] for n in names], *[new_m[n] for n in names],
            *[new_v[n] for n in names])
```

```python
import functools
import math

import numpy as np
import jax
import jax.numpy as jnp
from jax import lax
from jax.experimental import pallas as pl
from jax.experimental.pallas import tpu as pltpu

F32 = jnp.float32
BF16 = jnp.bfloat16

D = 1024
N_META = 16
D_FF = 4 * D
DEPTH = 2
GDN_H = 4
SB_H = 8
SB_DH = 64
HG_H = 8
HD = 128
CH = 64
QB = 128
ALPHA = float((2 * DEPTH) ** 0.25)
LN_EPS = 1e-5
RMS_EPS = 1e-6
L2_EPS = 1e-6
NEG = -1e30

ADAM_LR = 0.001
ADAM_B1 = 0.9
ADAM_B2 = 0.999
ADAM_EPS = 1e-08
ADAM_WD = 0.01
ADAM_STEP = 10

AB_W = 30 * HD
AB_TRUE = 3592
VMEM_LIMIT = 56 * 1024 * 1024

NN = ((1,), (0,))
NT = ((1,), (1,))
TN = ((0,), (0,))


def _cparams(**kw):
    return pltpu.CompilerParams(vmem_limit_bytes=VMEM_LIMIT, **kw)


def _dg(a, b, dims, mode):
    if mode == "h":
        return lax.dot_general(a, b, dims, precision=lax.Precision.HIGHEST, preferred_element_type=F32)
    if mode == "b":
        return lax.dot_general(a.astype(BF16), b.astype(BF16), dims, preferred_element_type=F32)
    ah, bh = a.astype(BF16), b.astype(BF16)
    al, bl = (a - ah.astype(F32)).astype(BF16), (b - bh.astype(F32)).astype(BF16)
    d = lambda x, y: lax.dot_general(x, y, dims, preferred_element_type=F32)
    return d(ah, bh) + (d(ah, bl) + d(al, bh))


def _make_dots(mode, batched=False):
    if batched:
        nn_d, nt_d, tn_d = (((2,), (1,)), ((0,), (0,))), (((2,), (2,)), ((0,), (0,))), (((1,), (1,)), ((0,), (0,)))
    else:
        nn_d, nt_d, tn_d = (NN, ((), ())), (NT, ((), ())), (TN, ((), ()))

    @jax.custom_vjp
    def nn(a, b):
        return _dg(a, b, nn_d, mode)

    @jax.custom_vjp
    def nt(a, b):
        return _dg(a, b, nt_d, mode)

    @jax.custom_vjp
    def tn(a, b):
        return _dg(a, b, tn_d, mode)

    nn.defvjp(lambda a, b: (nn(a, b), (a, b)), lambda r, g: (nt(g, r[1]), tn(r[0], g)))
    nt.defvjp(lambda a, b: (nt(a, b), (a, b)), lambda r, g: (nn(g, r[1]), tn(g, r[0])))
    tn.defvjp(lambda a, b: (tn(a, b), (a, b)), lambda r, g: (nt(r[1], g), nn(r[0], g)))
    return nn, nt, tn


hnn, hnt, htn = _make_dots("h")
bbnn, bbnt, bbtn = _make_dots("b", True)
mbnn, mbnt, mbtn = _make_dots("m", True)
hbnn, hbnt, hbtn = _make_dots("h", True)


def _split3(x, axis):
    x1 = x.astype(BF16)
    r1 = x - x1.astype(F32)
    x2 = r1.astype(BF16)
    x3 = (r1 - x2.astype(F32)).astype(BF16)
    return jnp.concatenate([x1, x2, x3], axis=axis)


@jax.custom_vjp
def _mask_dot(e3, x):
    return lax.dot_general(e3[0], _split3(x, 0), (NN, ((), ())), preferred_element_type=F32)


def _mask_dot_bwd(e3, g):
    dx = lax.dot_general(e3[1], _split3(g, 0), (TN, ((), ())), preferred_element_type=F32)
    return (jnp.zeros_like(e3[0]), jnp.zeros_like(e3[1])), dx


_mask_dot.defvjp(lambda e3, x: (_mask_dot(e3, x), e3), _mask_dot_bwd)


def _heads(a, n):
    return jnp.concatenate([a[None, :, h * HD:(h + 1) * HD] for h in range(n)], axis=0)


def _sigmoid(x):
    return jax.nn.sigmoid(x)


def _silu(x):
    return x * jax.nn.sigmoid(x)


def _softplus(x):
    return jnp.maximum(x, 0.0) + jnp.log(1.0 + jnp.exp(-jnp.abs(x)))


def _iota(shape, dim):
    return lax.broadcasted_iota(jnp.int32, shape, dim)


def _pick(n, prefs):
    for p in prefs:
        if n % p == 0:
            return p
    return n


def _mm(a, b, *, ta=False, tb=False, out_dtype=F32, name, b_view=None, out_split=0, act=False, gate=None):
    if ta:
        k_dim, m_dim = a.shape
    else:
        m_dim, k_dim = a.shape
    if b_view is None:
        w_rows, w_cols = b.shape
    else:
        kind, layer = b_view
        nj, _, blk_r, blk_c = b.shape
        w_rows, w_cols = (blk_r, nj * blk_c) if kind == "cols" else (nj * blk_r, blk_c)
    n_dim = w_rows if tb else w_cols
    assert (w_cols if tb else w_rows) == k_dim
    tm = _pick(m_dim, (1024, 1056, 704, 640, 512, 384, 256, 128))
    tn = _pick(n_dim, (1024, 1056, 704, 640, 512, 384, 256, 128))
    tk = _pick(k_dim, (1024, 1056, 704, 512, 384, 256, 128))
    nk = k_dim // tk
    a_spec = pl.BlockSpec((tk, tm), lambda i, j, k: (k, i)) if ta else pl.BlockSpec((tm, tk), lambda i, j, k: (i, k))
    wb = (tn, tk) if tb else (tk, tn)
    w_idx = (lambda i, j, k: (j, k)) if tb else (lambda i, j, k: (k, j))
    if b_view is None:
        b_spec = pl.BlockSpec(wb, w_idx)
    elif kind == "cols":
        per = blk_c // wb[1]
        b_spec = pl.BlockSpec((None, None) + wb,
                              lambda i, j, k: (w_idx(i, j, k)[1] // per, layer, w_idx(i, j, k)[0], w_idx(i, j, k)[1] % per))
    else:
        per = blk_r // wb[0]
        b_spec = pl.BlockSpec((None, None) + wb,
                              lambda i, j, k: (w_idx(i, j, k)[0] // per, layer, w_idx(i, j, k)[0] % per, w_idx(i, j, k)[1]))
    if out_split:
        per_o = (n_dim // out_split) // tn
        out_spec = pl.BlockSpec((None, tm, tn), lambda i, j, k: (j // per_o, i, j % per_o))
        out_sds = jax.ShapeDtypeStruct((out_split, m_dim, n_dim // out_split), out_dtype)
    else:
        out_spec = pl.BlockSpec((tm, tn), lambda i, j, k: (i, j))
        out_sds = jax.ShapeDtypeStruct((m_dim, n_dim), out_dtype)
    dims = (((0 if ta else 1,), (1 if tb else 0,)), ((), ()))
    extra = [] if gate is None else [gate]
    n_out = 2 if act else 1

    def finish(acc, refs):
        if act:
            refs[0][...] = acc.astype(refs[0].dtype)
            r = jnp.maximum(acc, 0.0)
            refs[1][...] = (r * r).astype(refs[1].dtype)
        elif gate is not None:
            refs[1][...] = (acc * (2.0 * jnp.maximum(refs[0][...].astype(F32), 0.0))).astype(refs[1].dtype)
        else:
            refs[0][...] = acc.astype(refs[0].dtype)

    def body(a_ref, b_ref, *rest):
        refs, acc_ref = rest[:-1], rest[-1]
        part = lax.dot_general(a_ref[...], b_ref[...], dims, preferred_element_type=F32)
        if nk == 1:
            finish(part, refs)
        else:
            k = pl.program_id(2)

            @pl.when(k == 0)
            def _():
                acc_ref[...] = part

            @pl.when(k > 0)
            def _():
                acc_ref[...] += part

            @pl.when(k == nk - 1)
            def _():
                finish(acc_ref[...], refs)

    out = pl.pallas_call(
        body, name=name, grid=(m_dim // tm, n_dim // tn, nk),
        in_specs=[a_spec, b_spec] + [pl.BlockSpec((tm, tn), lambda i, j, k: (i, j))] * len(extra),
        out_specs=[out_spec] * n_out,
        out_shape=[out_sds] * n_out,
        scratch_shapes=[pltpu.VMEM((tm, tn) if nk > 1 else (8, 128), F32)],
        compiler_params=_cparams(dimension_semantics=("parallel", "parallel", "arbitrary")),
    )(a, b, *extra)
    return out if act else out[0]


def _row_tile(t_pad, width):
    for tr in (528, 352, 176, 128, 64):
        if t_pad % tr == 0 and tr * width * 4 <= (3 << 19) and tr % 16 == 0:
            return tr
    return 64 if t_pad % 64 == 0 else t_pad


def _ln_res_fn(h, m, g, b):
    x = ALPHA * h + m
    mu = jnp.mean(x, axis=-1, keepdims=True)
    xc = x - mu
    var = jnp.mean(xc * xc, axis=-1, keepdims=True)
    return xc * lax.rsqrt(var + LN_EPS) * g + b


def _ln_res_fwd(h, m, g, b, name):
    t_pad = h.shape[0]
    tr = _row_tile(t_pad, D)

    def body(h_ref, m_ref, g_ref, b_ref, y_ref, yb_ref):
        y = _ln_res_fn(h_ref[...], m_ref[...], g_ref[...], b_ref[...])
        y_ref[...] = y
        yb_ref[...] = y.astype(BF16)

    row = pl.BlockSpec((tr, D), lambda i: (i, 0))
    par = pl.BlockSpec((1, D), lambda i: (0, 0))
    return pl.pallas_call(
        body, name=name, grid=(t_pad // tr,), in_specs=[row, row, par, par], out_specs=[row, row],
        out_shape=[jax.ShapeDtypeStruct((t_pad, D), F32), jax.ShapeDtypeStruct((t_pad, D), BF16)],
        compiler_params=_cparams(),
    )(h, m, g, b)


def _ln_res_bwd(h, m, g, b, dy1, dy2, name):
    t_pad = h.shape[0]
    tr = _row_tile(t_pad, D)

    def body(h_ref, m_ref, g_ref, b_ref, d1_ref, d2_ref, dh_ref, dm_ref, dg_ref, db_ref):
        _, vjp = jax.vjp(_ln_res_fn, h_ref[...], m_ref[...], g_ref[...], b_ref[...])
        dh, dm, dg, db = vjp(d1_ref[...] + d2_ref[...])
        dh_ref[...] = dh
        dm_ref[...] = dm.astype(BF16)

        @pl.when(pl.program_id(0) == 0)
        def _():
            dg_ref[...] = jnp.zeros_like(dg_ref)
            db_ref[...] = jnp.zeros_like(db_ref)

        dg_ref[...] += dg
        db_ref[...] += db

    row = pl.BlockSpec((tr, D), lambda i: (i, 0))
    par = pl.BlockSpec((1, D), lambda i: (0, 0))
    return pl.pallas_call(
        body, name=name, grid=(t_pad // tr,), in_specs=[row, row, par, par, row, row],
        out_specs=[row, row, par, par],
        out_shape=[jax.ShapeDtypeStruct((t_pad, D), F32), jax.ShapeDtypeStruct((t_pad, D), BF16),
                   jax.ShapeDtypeStruct((1, D), F32), jax.ShapeDtypeStruct((1, D), F32)],
        compiler_params=_cparams(),
    )(h, m, g, b, dy1, dy2)


def _grms_fn(o, z, g):
    y = o * lax.rsqrt(jnp.mean(o * o, axis=-1, keepdims=True) + RMS_EPS) * g
    return y * _silu(z)


def _grms_fwd(o, z_arr, z_blk0, g, name):
    t_pad, w = o.shape
    nh = w // HD
    tr = _row_tile(t_pad, HD * 4)

    def body(o_ref, z_ref, g_ref, y_ref):
        y_ref[...] = _grms_fn(o_ref[...], z_ref[...], g_ref[...]).astype(BF16)

    return pl.pallas_call(
        body, name=name, grid=(t_pad // tr, nh),
        in_specs=[pl.BlockSpec((tr, HD), lambda i, h: (i, h)),
                  pl.BlockSpec((tr, HD), lambda i, h: (i, z_blk0 + h)),
                  pl.BlockSpec((1, HD), lambda i, h: (0, 0))],
        out_specs=pl.BlockSpec((tr, HD), lambda i, h: (i, h)),
        out_shape=jax.ShapeDtypeStruct((t_pad, w), BF16), compiler_params=_cparams(),
    )(o, z_arr, g)


def _grms_bwd(o, z_arr, z_blk0, g, dy_arr, dy_blk0, name):
    t_pad, w = o.shape
    nh = w // HD
    tr = _row_tile(t_pad, HD * 4)

    def body(o_ref, z_ref, g_ref, dy_ref, do_ref, dz_ref, dg_ref):
        _, vjp = jax.vjp(_grms_fn, o_ref[...], z_ref[...], g_ref[...])
        do, dz, dg = vjp(dy_ref[...])
        do_ref[...] = do
        dz_ref[...] = dz

        @pl.when((pl.program_id(0) == 0) & (pl.program_id(1) == 0))
        def _():
            dg_ref[...] = jnp.zeros_like(dg_ref)

        dg_ref[...] += dg

    blk = pl.BlockSpec((tr, HD), lambda i, h: (i, h))
    return pl.pallas_call(
        body, name=name, grid=(t_pad // tr, nh),
        in_specs=[blk, pl.BlockSpec((tr, HD), lambda i, h: (i, z_blk0 + h)),
                  pl.BlockSpec((1, HD), lambda i, h: (0, 0)),
                  pl.BlockSpec((tr, HD), lambda i, h: (i, dy_blk0 + h))],
        out_specs=[blk, blk, pl.BlockSpec((1, HD), lambda i, h: (0, 0))],
        out_shape=[jax.ShapeDtypeStruct((t_pad, w), F32), jax.ShapeDtypeStruct((t_pad, w), F32),
                   jax.ShapeDtypeStruct((1, HD), F32)],
        compiler_params=_cparams(),
    )(o, z_arr, g, dy_arr)


def _loss_fwd(y, tgt, first_row, name):
    t_pad = y.shape[0]
    tr = _row_tile(t_pad, D)

    def body(y_ref, t_ref, l_ref, dy_ref):
        rows = pl.program_id(0) * tr + _iota((tr, 1), 0)
        err = jnp.where(rows >= first_row, y_ref[...] - t_ref[...], 0.0)
        dy_ref[...] = err * (1.0 / D)

        @pl.when(pl.program_id(0) == 0)
        def _():
            l_ref[...] = jnp.zeros_like(l_ref)

        part = jnp.sum(jnp.sum(err * err, axis=1, keepdims=True), axis=0, keepdims=True)
        l_ref[...] += jnp.broadcast_to(part * (0.5 / D), l_ref.shape)

    row = pl.BlockSpec((tr, D), lambda i: (i, 0))
    return pl.pallas_call(
        body, name=name, grid=(t_pad // tr,), in_specs=[row, row],
        out_specs=[pl.BlockSpec((8, 128), lambda i: (0, 0)), row],
        out_shape=[jax.ShapeDtypeStruct((8, 128), F32), jax.ShapeDtypeStruct((t_pad, D), F32)],
        compiler_params=_cparams(),
    )(y, tgt)


def _add2(a, b, name):
    t_pad, w = a.shape
    tr = _row_tile(t_pad, w)

    def body(a_ref, b_ref, o_ref):
        o_ref[...] = a_ref[...] + b_ref[...]

    row = pl.BlockSpec((tr, w), lambda i: (i, 0))
    return pl.pallas_call(body, name=name, grid=(t_pad // tr,), in_specs=[row, row], out_specs=row,
                          out_shape=jax.ShapeDtypeStruct((t_pad, w), F32), compiler_params=_cparams())(a, b)


def _assemble_bf16(parts, name):
    t_pad = parts[0][0].shape[0] if parts[0][1] == "cols" else parts[0][0].shape[1]
    widths = [p.shape[1] if kind == "cols" else HD for p, kind in parts]
    total = sum(widths)
    tr = _row_tile(t_pad, total)

    def body(*refs):
        o_ref = refs[-1]
        off = 0
        for ref, (p, kind), w in zip(refs[:-1], parts, widths):
            if kind == "cols":
                o_ref[:, off:off + w] = ref[...].astype(BF16)
            else:
                acc = ref[0]
                for hh in range(1, p.shape[0]):
                    acc = acc + ref[hh]
                o_ref[:, off:off + w] = acc.astype(BF16)
            off += w

    specs = []
    for p, kind in parts:
        if kind == "cols":
            specs.append(pl.BlockSpec((tr, p.shape[1]), lambda i: (i, 0)))
        else:
            specs.append(pl.BlockSpec((p.shape[0], tr, HD), lambda i: (0, i, 0)))
    return pl.pallas_call(
        body, name=name, grid=(t_pad // tr,), in_specs=specs,
        out_specs=pl.BlockSpec((tr, total), lambda i: (i, 0)),
        out_shape=jax.ShapeDtypeStruct((t_pad, total), BF16), compiler_params=_cparams(),
    )(*[p for p, _ in parts])


CONV_K = 4
HALO = 8
RT = 128


def _conv_fwd(p, blk0, w, mode, pad, name):
    t_pad = p.shape[0]
    nt = t_pad // RT
    scale = HD ** -0.5 if mode == "q" else 1.0

    def body(x_ref, w_ref, y_ref, xs_ref):
        xs_ref[0:HALO, :] = jnp.zeros((HALO, HD), F32)
        rows = _iota((t_pad, 1), 0)
        xs_ref[HALO:HALO + t_pad, :] = jnp.where(rows >= pad, x_ref[...], 0.0)
        wv = w_ref[...]

        def tile(i, carry):
            r0 = pl.multiple_of(i * RT, RT)
            ext = xs_ref[pl.ds(r0, RT + HALO), :]
            acc = ext[HALO:, :] * wv[3:4, :]
            for s in (1, 2, 3):
                acc = acc + pltpu.roll(ext, s, 0)[HALO:, :] * wv[3 - s:4 - s, :]
            y = _silu(acc)
            if mode != "v":
                y = y * lax.rsqrt(jnp.sum(y * y, axis=-1, keepdims=True) + L2_EPS) * scale
            y_ref[pl.ds(r0, RT), :] = y
            return carry

        lax.fori_loop(0, nt, tile, 0)

    return pl.pallas_call(
        body, name=name, grid=(GDN_H,),
        in_specs=[pl.BlockSpec((t_pad, HD), lambda h: (0, blk0 + h)), pl.BlockSpec((CONV_K, HD), lambda h: (0, h))],
        out_specs=pl.BlockSpec((t_pad, HD), lambda h: (0, h)),
        out_shape=jax.ShapeDtypeStruct((t_pad, GDN_H * HD), F32),
        scratch_shapes=[pltpu.VMEM((t_pad + HALO, HD), F32)],
        compiler_params=_cparams(),
    )(p, w)


def _conv_bwd(p, blk0, w, dn, mode, pad, name):
    t_pad = p.shape[0]
    nt = t_pad // RT
    scale = HD ** -0.5 if mode == "q" else 1.0

    def body(x_ref, w_ref, dn_ref, dx_ref, dw_ref, xs_ref, ds_ref):
        xs_ref[0:HALO, :] = jnp.zeros((HALO, HD), F32)
        xs_ref[HALO + t_pad:HALO + t_pad + 2 * HALO, :] = jnp.zeros((2 * HALO, HD), F32)
        ds_ref[t_pad:t_pad + HALO, :] = jnp.zeros((HALO, HD), F32)
        rows = _iota((t_pad, 1), 0)
        xs_ref[HALO:HALO + t_pad, :] = jnp.where(rows >= pad, x_ref[...], 0.0)
        ds_ref[0:t_pad, :] = dn_ref[...]
        wv = w_ref[...]

        def tile(i, dw):
            r0 = pl.multiple_of(i * RT, RT)
            ext = xs_ref[pl.ds(r0, RT + 2 * HALO), :]
            dn_e = ds_ref[pl.ds(r0, RT + HALO), :]
            xsh = [ext[HALO:, :]] + [pltpu.roll(ext, s, 0)[HALO:, :] for s in (1, 2, 3)]
            pre = xsh[0] * wv[3:4, :]
            for s in (1, 2, 3):
                pre = pre + xsh[s] * wv[3 - s:4 - s, :]
            sg = _sigmoid(pre)
            y = pre * sg
            if mode != "v":
                ss = jnp.sum(y * y, axis=-1, keepdims=True) + L2_EPS
                r = lax.rsqrt(ss)
                dy = scale * (dn_e * r - y * (r * r * r) * jnp.sum(dn_e * y, axis=-1, keepdims=True))
            else:
                dy = dn_e
            dpre = dy * (sg * (1.0 + pre * (1.0 - sg)))
            dx = dpre[:RT, :] * wv[3:4, :]
            for s in (1, 2, 3):
                dx = dx + pltpu.roll(dpre, RT + HALO - s, 0)[:RT, :] * wv[3 - s:4 - s, :]
            trow = r0 + _iota((RT, 1), 0)
            dx_ref[pl.ds(r0, RT), :] = jnp.where(trow >= pad, dx, 0.0)
            new = []
            for s in (0, 1, 2, 3):
                new.append(dw[s] + jnp.sum(dpre[:RT, :] * xsh[s][:RT, :], axis=0, keepdims=True))
            return tuple(new)

        z = jnp.zeros((1, HD), F32)
        dw = lax.fori_loop(0, nt, tile, (z, z, z, z))
        for s in (0, 1, 2, 3):
            dw_ref[3 - s:4 - s, :] = dw[s]

    return pl.pallas_call(
        body, name=name, grid=(GDN_H,),
        in_specs=[pl.BlockSpec((t_pad, HD), lambda h: (0, blk0 + h)), pl.BlockSpec((CONV_K, HD), lambda h: (0, h)),
                  pl.BlockSpec((t_pad, HD), lambda h: (0, h))],
        out_specs=[pl.BlockSpec((t_pad, HD), lambda h: (0, h)), pl.BlockSpec((CONV_K, HD), lambda h: (0, h))],
        out_shape=[jax.ShapeDtypeStruct((t_pad, GDN_H * HD), F32), jax.ShapeDtypeStruct((CONV_K, GDN_H * HD), F32)],
        scratch_shapes=[pltpu.VMEM((t_pad + 3 * HALO, HD), F32), pltpu.VMEM((t_pad + HALO, HD), F32)],
        compiler_params=_cparams(),
    )(p, w, dn)


@jax.custom_vjp
def _unit_lower_inv(m, bd, eye):
    md = m * bd
    low = m - md
    p2 = mbnn(md, md)
    p4 = mbnn(p2, p2)
    dinv = mbnn(mbnn(eye - md, eye + p2), eye + p4)
    n = mbnn(dinv, low)
    n2 = mbnn(n, n)
    n4 = mbnn(n2, n2)
    return mbnn(mbnn(mbnn(eye - n, eye + n2), eye + n4), dinv)


def _unit_lower_inv_bwd(res, g):
    t, bd, eye = res
    return -mbtn(t, mbnt(g, t)), jnp.zeros_like(bd), jnp.zeros_like(eye)


def _unit_lower_inv_fwd(m, bd, eye):
    t = _unit_lower_inv(m, bd, eye)
    return t, (t, bd, eye)


_unit_lower_inv.defvjp(_unit_lower_inv_fwd, _unit_lower_inv_bwd)


def _gdn_chunk(q, k, v, bb, aa, alog, dtb, s, valid):
    nh = q.shape[0]
    ri = _iota((1, CH, CH), 1)
    ci = _iota((1, CH, CH), 2)
    causal = ri >= ci
    strict = ri > ci
    eye = (ri == ci).astype(F32)
    bd = ((ri >> 3) == (ci >> 3)).astype(F32)
    ltri = (_iota((CH, CH), 0) >= _iota((CH, CH), 1)).astype(F32)
    sel = (_iota((nh, 1, HD), 2) == _iota((nh, 1, HD), 0)).astype(F32)

    beta_all = jnp.where(valid, _sigmoid(bb), 0.0)
    g_all = jnp.where(valid, -jnp.exp(alog) * _softplus(aa + dtb), 0.0)
    gc_all = hnn(ltri, g_all)
    beta = jnp.sum(beta_all[None] * sel, axis=2, keepdims=True)
    gc = jnp.sum(gc_all[None] * sel, axis=2, keepdims=True)
    gc_rows = hbnt(jnp.broadcast_to(sel, (nh, CH, HD)), jnp.broadcast_to(gc_all[None], (nh, CH, HD)))
    last = _iota((1, CH, 1), 1) == CH - 1
    gc_last = jnp.sum(jnp.where(last, gc, 0.0), axis=1, keepdims=True)
    decay = jnp.exp(jnp.where(causal, gc - gc_rows, NEG))
    egc = jnp.exp(gc)

    kb = k * beta
    m = jnp.where(strict, bbnt(kb, k) * decay, 0.0)
    t_inv = _unit_lower_inv(m, bd, eye)
    u = bbnn(t_inv, v * beta)
    w = bbnn(t_inv, kb * egc)
    a_intra = bbnt(q, k) * decay
    q_dec = q * egc
    k_dec = k * jnp.exp(gc_last - gc)
    v_new = u - bbnn(w, s)
    o = bbnn(q_dec, s) + bbnn(a_intra, v_new)
    s_new = s * jnp.exp(gc_last) + bbtn(k_dec, v_new)
    return o, s_new


PAIR = 2 * CH


def _gdn_specs(npair, rev):
    cc = (lambda c: npair - 1 - c) if rev else (lambda c: c)
    wide = pl.BlockSpec((PAIR, GDN_H * HD), lambda c: (cc(c), 0))
    fix = lambda off: pl.BlockSpec((PAIR, HD), lambda c: (cc(c), off))
    par = pl.BlockSpec((1, HD), lambda c: (0, 0))
    state = pl.BlockSpec((1, GDN_H, HD, HD), lambda c: (cc(c), 0, 0, 0))
    return wide, fix, par, state


def _store_heads(ref, a, rows=slice(None)):
    for h in range(a.shape[0]):
        ref[rows, h * HD:(h + 1) * HD] = a[h]


def _chunk_rows(half):
    return slice(half * CH, (half + 1) * CH)


def _chunk_valid(pair, half, pad):
    return ((2 * pair + half) * CH + _iota((CH, 1), 0)) >= pad


def _gdn_fwd(qn, kn, vn, p, alog, dtb, pad, name):
    t_pad = qn.shape[0]
    npair = t_pad // PAIR
    wide, fix, par, state = _gdn_specs(npair, False)

    def body(q_ref, k_ref, v_ref, bb_ref, aa_ref, al_ref, dt_ref, o_ref, ss_ref, s_ref):
        c = pl.program_id(0)

        @pl.when(c == 0)
        def _():
            s_ref[...] = jnp.zeros_like(s_ref)

        s = s_ref[...]
        ss_ref[0] = s
        for half in (0, 1):
            r = _chunk_rows(half)
            o, s = _gdn_chunk(_heads(q_ref[r, :], GDN_H), _heads(k_ref[r, :], GDN_H), _heads(v_ref[r, :], GDN_H),
                              bb_ref[r, :], aa_ref[r, :], al_ref[...], dt_ref[...], s, _chunk_valid(c, half, pad))
            _store_heads(o_ref, o, r)
        s_ref[...] = s

    return pl.pallas_call(
        body, name=name, grid=(npair,),
        in_specs=[wide, wide, wide, fix(16), fix(17), par, par],
        out_specs=[wide, state],
        out_shape=[jax.ShapeDtypeStruct((t_pad, GDN_H * HD), F32), jax.ShapeDtypeStruct((npair, GDN_H, HD, HD), F32)],
        scratch_shapes=[pltpu.VMEM((GDN_H, HD, HD), F32)],
        compiler_params=_cparams(),
    )(qn, kn, vn, p, p, alog, dtb)


def _gdn_bwd(qn, kn, vn, p, alog, dtb, ssave, do, pad, name, cargo=(), exchange=None):
    t_pad = qn.shape[0]
    npair = t_pad // PAIR
    wide, fix, par, state = _gdn_specs(npair, True)
    n = len(cargo)

    def body(q_ref, k_ref, v_ref, bb_ref, aa_ref, al_ref, dt_ref, ss_ref, do_ref, *rest):
        c = pl.program_id(0)
        ds_ref = rest[-1]
        (dq_ref, dk_ref, dv_ref, dbb_ref, daa_ref, dal_ref, ddt_ref), end_cargo = _cargo_bounds(
            rest[:-1], n, 7, exchange, c == 0, c == npair - 1)

        @pl.when(c == 0)
        def _():
            ds_ref[...] = jnp.zeros_like(ds_ref)
            dal_ref[...] = jnp.zeros_like(dal_ref)
            ddt_ref[...] = jnp.zeros_like(ddt_ref)

        ra, rb = _chunk_rows(0), _chunk_rows(1)
        va, vb = _chunk_valid(npair - 1 - c, 0, pad), _chunk_valid(npair - 1 - c, 1, pad)

        def pair(qa, ka, va_, ba, aa, qb, kb, vb_, bb, ab, al, dt, s):
            oa, s = _gdn_chunk(qa, ka, va_, ba, aa, al, dt, s, va)
            ob, s = _gdn_chunk(qb, kb, vb_, bb, ab, al, dt, s, vb)
            return oa, ob, s

        ins = [f(ref[r, :]) for r in (ra, rb)
               for ref, f in ((q_ref, lambda a: _heads(a, GDN_H)), (k_ref, lambda a: _heads(a, GDN_H)),
                              (v_ref, lambda a: _heads(a, GDN_H)), (bb_ref, lambda a: a), (aa_ref, lambda a: a))]
        _, vjp = jax.vjp(pair, *ins, al_ref[...], dt_ref[...], ss_ref[0])
        g = vjp((_heads(do_ref[ra, :], GDN_H), _heads(do_ref[rb, :], GDN_H), ds_ref[...]))
        for r, (dq, dk, dv, dbb, daa) in ((ra, g[0:5]), (rb, g[5:10])):
            _store_heads(dq_ref, dq, r)
            _store_heads(dk_ref, dk, r)
            _store_heads(dv_ref, dv, r)
            dbb_ref[r, :] = dbb
            daa_ref[r, :] = daa
        dal_ref[...] += g[10]
        ddt_ref[...] += g[11]
        ds_ref[...] = g[12]
        end_cargo()

    sds = jax.ShapeDtypeStruct
    return pl.pallas_call(
        body, name=name, grid=(npair,),
        in_specs=[wide, wide, wide, fix(16), fix(17), par, par, state, wide] + [ANY] * n,
        out_specs=[wide, wide, wide, fix(0), fix(0), par, par] + [ANY] * n,
        out_shape=[sds((t_pad, GDN_H * HD), F32)] * 3 + [sds((t_pad, HD), F32)] * 2 + [sds((1, HD), F32)] * 2
        + (exchange[1](cargo) if n else []),
        scratch_shapes=(exchange[2](n) if n else []) + [pltpu.VMEM((GDN_H, HD, HD), F32)],
        compiler_params=_cparams(),
    )(qn, kn, vn, p, p, alog, dtb, ssave, do, *cargo)


SB_Q0, SB_K0, SB_V0 = 18, 22, 26
SB_SCALE = SB_DH ** -0.5
SB_NB = 4


def _sb_terms(z, allowed):
    e = jnp.exp(-jnp.abs(z))
    den = 1.0 + e
    raw = -jnp.maximum(z, 0.0) - jnp.log(den)
    l1m = raw if allowed is None else jnp.where(allowed, raw, 0.0)
    return l1m, z + raw, jnp.where(z >= 0.0, 1.0, e) / den


def _sb_passes(i, first_pass, inner_pass, last_pass, carry):
    n_pass = (i + SB_NB) // SB_NB
    carry = first_pass(0, carry)
    carry = lax.fori_loop(1, n_pass - 1, inner_pass, carry)
    return lax.cond(n_pass > 1, lambda c: last_pass(n_pass - 1, c), lambda c: c, carry)


def _sb_stack(a, i):
    first = _iota((1, HD), 1) < SB_DH
    a2 = jnp.concatenate([jnp.where(first, a, 0.0), jnp.where(first, 0.0, a)], axis=0).astype(BF16)
    rq = i * QB + _iota((QB, 1), 0)
    return a2, jnp.concatenate([rq, rq], axis=0), first


def _dot_hi_lo(a, b2):
    hi = a.astype(BF16)
    lo = (a - hi.astype(F32)).astype(BF16)
    return lax.dot_general(jnp.concatenate([hi, lo], axis=1), b2, (NN, ((), ())), preferred_element_type=F32)


def _cargo_bounds(refs, n, n_out, exchange, first, last):
    outs = refs[n:n + n_out]
    if not n:
        return outs, lambda: None
    ex = exchange[0](refs[:n], refs[n + n_out:2 * n + n_out], *refs[2 * n + n_out:])

    @pl.when(first)
    def _():
        ex.start()

    def finish():
        @pl.when(last)
        def _():
            ex.wait()

    return outs, finish


def _sb_fwd(p, pad, name, cargo=(), exchange=None):
    t_pad = p.shape[0]
    nq = t_pad // QB
    n = len(cargo)

    def body(q_ref, k_ref, v_ref, *rest):
        i = pl.program_id(1)
        pr = pl.program_id(0)
        (o_ref, r_ref), end_cargo = _cargo_bounds(rest, n, 2, exchange, (pr == 0) & (i == 0),
                                                  (pr == SB_H // 2 - 1) & (i == nq - 1))
        q2, rowq, first = _sb_stack(q_ref[...] * SB_SCALE, i)
        tri = (_iota((QB, QB), 0) > _iota((QB, QB), 1)).astype(BF16)
        upper2 = jnp.concatenate([tri, tri], axis=0)

        def chain(kb, masked):
            live = kb >= 0
            kb = jnp.maximum(kb, 0)
            start = pl.multiple_of(kb * QB, QB)
            kblk = k_ref[pl.ds(start, QB), :].astype(BF16)
            vblk = v_ref[pl.ds(start, QB), :].astype(BF16)
            z = lax.dot_general(q2, kblk, (NT, ((), ())), preferred_element_type=F32)
            allowed = None
            if masked:
                colk = kb * QB + _iota((1, QB), 1)
                allowed = (colk < rowq) & (colk >= pad) & live
            l1m, ls, _ = _sb_terms(z, allowed)
            return allowed, ls, _dot_hi_lo(l1m, upper2), jnp.sum(l1m, axis=1, keepdims=True), vblk

        def step(j, carry, masked):
            o_acc, run = carry
            ws, vs = [], []
            for n in range(SB_NB):
                allowed, ls, suf, rs, vblk = chain(i - SB_NB * j - n, masked)
                wgt = jnp.exp(ls + suf + run)
                ws.append((wgt if allowed is None else jnp.where(allowed, wgt, 0.0)).astype(BF16))
                vs.append(vblk)
                run = run + rs
            o_acc = o_acc + lax.dot_general(jnp.concatenate(ws, axis=1), jnp.concatenate(vs, axis=0),
                                            (NN, ((), ())), preferred_element_type=F32)
            return o_acc, run

        edge = lambda j, c: step(j, c, True)
        o_acc, run = _sb_passes(i, edge, lambda j, c: step(j, c, False), edge,
                                (jnp.zeros((2 * QB, HD), F32), jnp.zeros((2 * QB, 1), F32)))
        o_ref[...] = jnp.where(first, o_acc[:QB], o_acc[QB:]).astype(BF16)
        r_ref[...] = jnp.where(first, run[:QB], run[QB:])
        end_cargo()

    full = lambda off: pl.BlockSpec((t_pad, HD), lambda pr, i: (0, off + pr))
    blk = pl.BlockSpec((QB, HD), lambda pr, i: (i, pr))
    return pl.pallas_call(
        body, name=name, grid=(SB_H // 2, nq),
        in_specs=[pl.BlockSpec((QB, HD), lambda pr, i: (i, SB_Q0 + pr)), full(SB_K0), full(SB_V0)] + [ANY] * n,
        out_specs=[blk, blk] + [ANY] * n,
        out_shape=[jax.ShapeDtypeStruct((t_pad, SB_H * SB_DH), BF16), jax.ShapeDtypeStruct((t_pad, SB_H * SB_DH), F32)]
        + (exchange[1](cargo) if n else []),
        scratch_shapes=exchange[2](n) if n else [],
        compiler_params=_cparams(),
    )(p, p, p, *cargo)


def _sb_bwd(p, rtot, dy, dy_blk0, pad, name, cargo=(), exchange=None):
    t_pad = p.shape[0]
    nq = t_pad // QB
    n = len(cargo)

    def body(q_ref, k_ref, v_ref, r_ref, do_ref, *rest):
        i = pl.program_id(1)
        pr = pl.program_id(0)
        (dq_ref, dk_ref, dv_ref), end_cargo = _cargo_bounds(rest, n, 3, exchange, (pr == 0) & (i == 0),
                                                            (pr == SB_H // 2 - 1) & (i == nq - 1))

        @pl.when(i == 0)
        def _():
            dk_ref[...] = jnp.zeros_like(dk_ref)
            dv_ref[...] = jnp.zeros_like(dv_ref)

        q2, rowq, first = _sb_stack(q_ref[...] * SB_SCALE, i)
        do2, _, _ = _sb_stack(do_ref[...], i)
        rt = r_ref[...]
        lane = _iota((1, HD), 1)
        rcol = jnp.concatenate([jnp.sum(jnp.where(lane == 0, rt, 0.0), axis=1, keepdims=True),
                                jnp.sum(jnp.where(lane == SB_DH, rt, 0.0), axis=1, keepdims=True)], axis=0)
        rj = _iota((QB, QB), 0)
        cs = _iota((QB, QB), 1)
        tri_u = (rj > cs).astype(BF16)
        tri_l = (rj < cs).astype(BF16)
        upper2 = jnp.concatenate([tri_u, tri_u], axis=0)
        lower2 = jnp.concatenate([tri_l, tri_l], axis=0)

        def chain(kb, masked):
            live = kb <= i
            kb = jnp.minimum(kb, i)
            start = pl.multiple_of(kb * QB, QB)
            kblk = k_ref[pl.ds(start, QB), :].astype(BF16)
            vblk = v_ref[pl.ds(start, QB), :].astype(BF16)
            z = lax.dot_general(q2, kblk, (NT, ((), ())), preferred_element_type=F32)
            allowed = None
            if masked:
                colk = kb * QB + _iota((1, QB), 1)
                allowed = (colk < rowq) & (colk >= pad) & live
            l1m, ls, sg = _sb_terms(z, allowed)
            dwgt = lax.dot_general(do2, vblk, (NT, ((), ())), preferred_element_type=F32)
            return start, kblk, allowed, ls, _dot_hi_lo(l1m, upper2), jnp.sum(l1m, axis=1, keepdims=True), dwgt, sg

        def finish(c, seen, gseen):
            start, kblk, allowed, ls, suf, rs, dwgt, sg = c
            wgt = jnp.exp(ls + suf + (rcol - seen - rs))
            if allowed is not None:
                wgt = jnp.where(allowed, wgt, 0.0)
            dl = dwgt * wgt
            gpre = gseen + _dot_hi_lo(dl, lower2)
            dz = dl - sg * (dl + gpre)
            if allowed is not None:
                dz = jnp.where(allowed, dz, 0.0)
            dz = dz.astype(BF16)
            dk_ref[pl.ds(start, QB), :] += lax.dot_general(dz, q2, (TN, ((), ())), preferred_element_type=F32)
            dv_ref[pl.ds(start, QB), :] += lax.dot_general(wgt.astype(BF16), do2, (TN, ((), ())),
                                                           preferred_element_type=F32)
            return dz, seen + rs, gseen + jnp.sum(dl, axis=1, keepdims=True)

        def step(j, carry, masked):
            dq_acc, seen, gseen = carry
            cs_ = [chain(SB_NB * j + n, masked) for n in range(SB_NB)]
            dzs = []
            for c in cs_:
                dz, seen, gseen = finish(c, seen, gseen)
                dzs.append(dz)
            dq_acc = dq_acc + lax.dot_general(jnp.concatenate(dzs, axis=1), jnp.concatenate([c[1] for c in cs_], axis=0),
                                              (NN, ((), ())), preferred_element_type=F32)
            return dq_acc, seen, gseen

        zc = jnp.zeros((2 * QB, 1), F32)
        edge = lambda j, c: step(j, c, True)
        dq_acc, _, _ = _sb_passes(i, edge, lambda j, c: step(j, c, False), edge, (jnp.zeros((2 * QB, HD), F32), zc, zc))
        dq_ref[...] = jnp.where(first, dq_acc[:QB], dq_acc[QB:]) * SB_SCALE
        end_cargo()

    full_in = lambda off: pl.BlockSpec((t_pad, HD), lambda pr, i: (0, off + pr))
    full_out = pl.BlockSpec((t_pad, HD), lambda pr, i: (0, pr))
    blk = pl.BlockSpec((QB, HD), lambda pr, i: (i, pr))
    sds = jax.ShapeDtypeStruct((t_pad, SB_H * SB_DH), F32)
    return pl.pallas_call(
        body, name=name, grid=(SB_H // 2, nq),
        in_specs=[pl.BlockSpec((QB, HD), lambda pr, i: (i, SB_Q0 + pr)), full_in(SB_K0), full_in(SB_V0), blk,
                  pl.BlockSpec((QB, HD), lambda pr, i: (i, dy_blk0 + pr))] + [ANY] * n,
        out_specs=[blk, full_out, full_out] + [ANY] * n,
        out_shape=[sds, sds, sds] + (exchange[1](cargo) if n else []),
        scratch_shapes=exchange[2](n) if n else [],
        compiler_params=_cparams(),
    )(p, p, p, rtot, dy, *cargo)


HG_LEVELS = 6


def _hg_prefix_matrix():
    t = np.arange(CH)[:, None]
    j = np.arange(CH)[None, :]
    groups = [(j <= t)]
    for lvl in range(1, HG_LEVELS + 1):
        half = CH >> lvl
        e = (t // (2 * half)) * (2 * half) + half - 1
        groups.append(j <= e)
    groups.append(np.ones((8, CH), bool))
    e = np.concatenate(groups, axis=0).astype(np.float32)
    return np.concatenate([e, e, e], axis=1), np.concatenate([e, e, e], axis=0)


HG_G = 4


def _hg_chunk(qr, fr, iv, r0, r1, st, valid, ecat):
    g = st.shape[0]
    mx = jnp.maximum(r0, r1)
    e0 = jnp.exp(r0 - mx)
    e1 = jnp.exp(r1 - mx)
    lb = e1 / (e0 + e1)
    fg = lb + (1.0 - lb) * _sigmoid(fr)
    logf = jnp.where(valid, jnp.log(fg), 0.0)
    kk = jnp.where(valid, 1.0 - fg, 0.0)
    q = jnp.where(valid, _silu(qr), 0.0)
    v = _heads(jnp.where(valid, iv, 0.0), g)

    pre = _mask_dot(ecat, logf)
    b = pre[0:CH]
    b_last = jnp.max(pre[(HG_LEVELS + 1) * CH:], axis=0, keepdims=True)
    row = _iota((CH, 1), 0)
    ri = _iota((1, CH, CH), 1)
    ci = _iota((1, CH, CH), 2)
    a = jnp.where(ri == ci, jnp.sum(_heads(q * kk, g), axis=2, keepdims=True), 0.0)
    for lvl in range(1, HG_LEVELS + 1):
        half = CH >> lvl
        m = pre[lvl * CH:(lvl + 1) * CH]
        low = (row & half) != 0
        qt = jnp.where(low, q * jnp.exp(jnp.where(low, b - m, 0.0)), 0.0)
        kt = jnp.where(low, 0.0, kk * jnp.exp(jnp.where(low, 0.0, m - b)))
        same = (ri >> (7 - lvl)) == (ci >> (7 - lvl))
        a = a + jnp.where(same, bbnt(_heads(qt, g), _heads(kt, g)), 0.0)
    o = bbnt(_heads(q * jnp.exp(b), g), st) + bbnn(a, v)
    kd = kk * jnp.exp(b_last - b)
    st_new = st * _heads(jnp.exp(b_last), g) + bbtn(v, _heads(kd, g))
    return o, st_new


def _hg_specs(npair, rev):
    cc = (lambda c: npair - 1 - c) if rev else (lambda c: c)
    ng = HG_H // HG_G
    blk = lambda off: pl.BlockSpec((PAIR, HG_G * HD), lambda h, c: (cc(c), off * ng + h))
    lbs = pl.BlockSpec((2, HG_G * HD), lambda h, c: (0, h))
    state = pl.BlockSpec((1, HG_G, HD, HD), lambda h, c: (cc(c), h, 0, 0))
    return ng, blk, lbs, state


def _hg_fwd(p, lbraw, ecat, pad, name):
    t_pad = p.shape[0]
    npair = t_pad // PAIR
    ng, blk, lbs, state = _hg_specs(npair, False)

    def body(q_ref, f_ref, i_ref, lb_ref, e_ref, et_ref, o_ref, ss_ref, s_ref):
        c = pl.program_id(1)

        @pl.when(c == 0)
        def _():
            s_ref[...] = jnp.zeros_like(s_ref)

        st = s_ref[...]
        ss_ref[0] = st
        for half in (0, 1):
            r = _chunk_rows(half)
            o, st = _hg_chunk(q_ref[r, :], f_ref[r, :], i_ref[r, :], lb_ref[0:1, :], lb_ref[1:2, :], st,
                              _chunk_valid(c, half, pad), (e_ref[...], et_ref[...]))
            _store_heads(o_ref, o, r)
        s_ref[...] = st

    return pl.pallas_call(
        body, name=name, grid=(ng, npair),
        in_specs=[blk(0), blk(1), blk(2), lbs] + [pl.BlockSpec(e.shape, lambda h, c: (0, 0)) for e in ecat],
        out_specs=[blk(0), state],
        out_shape=[jax.ShapeDtypeStruct((t_pad, HG_H * HD), F32), jax.ShapeDtypeStruct((npair, HG_H, HD, HD), F32)],
        scratch_shapes=[pltpu.VMEM((HG_G, HD, HD), F32)],
        compiler_params=_cparams(),
    )(p, p, p, lbraw, *ecat)


def _hg_bwd(p, lbraw, ecat, ssave, do, pad, name, cargo=(), exchange=None):
    t_pad = p.shape[0]
    npair = t_pad // PAIR
    ng, blk, lbs, state = _hg_specs(npair, True)
    n = len(cargo)

    def body(q_ref, f_ref, i_ref, lb_ref, e_ref, et_ref, ss_ref, do_ref, *rest):
        c = pl.program_id(1)
        hg = pl.program_id(0)
        ds_ref = rest[-1]
        (dq_ref, df_ref, di_ref, dlb_ref), end_cargo = _cargo_bounds(
            rest[:-1], n, 4, exchange, (hg == 0) & (c == 0), (hg == ng - 1) & (c == npair - 1))

        @pl.when(c == 0)
        def _():
            ds_ref[...] = jnp.zeros_like(ds_ref)
            dlb_ref[...] = jnp.zeros_like(dlb_ref)

        ra, rb = _chunk_rows(0), _chunk_rows(1)
        va, vb = _chunk_valid(npair - 1 - c, 0, pad), _chunk_valid(npair - 1 - c, 1, pad)
        ecv = (e_ref[...], et_ref[...])

        def pair(qa, fa, ia, qb, fb, ib, r0, r1, st):
            oa, st = _hg_chunk(qa, fa, ia, r0, r1, st, va, ecv)
            ob, st = _hg_chunk(qb, fb, ib, r0, r1, st, vb, ecv)
            return oa, ob, st

        ins = [ref[r, :] for r in (ra, rb) for ref in (q_ref, f_ref, i_ref)]
        _, vjp = jax.vjp(pair, *ins, lb_ref[0:1, :], lb_ref[1:2, :], ss_ref[0])
        g = vjp((_heads(do_ref[ra, :], HG_G), _heads(do_ref[rb, :], HG_G), ds_ref[...]))
        for r, (dq, df, di) in ((ra, g[0:3]), (rb, g[3:6])):
            dq_ref[r, :] = dq
            df_ref[r, :] = df
            di_ref[r, :] = di
        dlb_ref[0:1, :] += g[6]
        dlb_ref[1:2, :] += g[7]
        ds_ref[...] = g[8]
        end_cargo()

    sds = jax.ShapeDtypeStruct((t_pad, HG_H * HD), F32)
    return pl.pallas_call(
        body, name=name, grid=(ng, npair),
        in_specs=[blk(0), blk(1), blk(2), lbs] + [pl.BlockSpec(e.shape, lambda h, c: (0, 0)) for e in ecat]
        + [state, blk(0)] + [ANY] * n,
        out_specs=[blk(0), blk(0), blk(0), lbs] + [ANY] * n,
        out_shape=[sds, sds, sds, jax.ShapeDtypeStruct((2, HG_H * HD), F32)] + (exchange[1](cargo) if n else []),
        scratch_shapes=(exchange[2](n) if n else []) + [pltpu.VMEM((HG_G, HD, HD), F32)],
        compiler_params=_cparams(),
    )(p, p, p, lbraw, *ecat, ssave, do, *cargo)


def _pad_ab_cols(w):
    z = jnp.zeros((w.shape[0], HD - GDN_H), w.dtype)
    return jnp.concatenate([w[:, :2048], w[:, 2048:2052], z, w[:, 2052:2056], z, w[:, 2056:]], axis=1)


def _unpad_ab_cols(w):
    return jnp.concatenate([w[:, :2048], w[:, 2048:2052], w[:, 2176:2180], w[:, 2304:]], axis=1)


def _lane_pad(v):
    return jnp.pad(v, ((0, 0), (0, HD - v.shape[1])))


def _mlp_fwd(hb, w1, w2, layer):
    a, r = _mm(hb, w1, b_view=("cols", layer), out_dtype=BF16, act=True, name=f"mlp_up_{layer}")
    m = _mm(r, w2, b_view=("rows", layer), name=f"mlp_down_{layer}")
    return a, r, m


def _mlp_bwd(hb, a, r, dmb, w1, w2, layer):
    da = _mm(dmb, w2, tb=True, b_view=("rows", layer), out_dtype=BF16, gate=a, name=f"mlp_down_dx_{layer}")
    dw2 = _mm(r, dmb, ta=True, out_dtype=BF16, name=f"mlp_down_dw_{layer}")
    dh = _mm(da, w1, tb=True, b_view=("cols", layer), name=f"mlp_up_dx_{layer}")
    dw1 = _mm(hb, da, ta=True, out_dtype=BF16, out_split=N_CHIP, name=f"mlp_up_dw_{layer}")
    return dh, dw1, dw2


def _local_step(h0, tgt, w, pad, late=None):
    row = lambda a, i: a[i:i + 1]
    ecat = tuple(jnp.asarray(e, dtype=BF16) for e in _hg_prefix_matrix())
    cw = [w["conv_w"][:, i * 512:(i + 1) * 512] for i in range(3)]
    alog, dtb = _lane_pad(w["a_log"]), _lane_pad(w["dt_bias"])

    h0b = h0.astype(BF16)
    p0 = _mm(h0b, w["ab_w_in"], name="ab_in")
    qn = _conv_fwd(p0, 0, cw[0], "q", pad, "conv_q")
    kn = _conv_fwd(p0, 4, cw[1], "k", pad, "conv_k")
    vn = _conv_fwd(p0, 8, cw[2], "v", pad, "conv_v")
    oa_raw, ss0 = _gdn_fwd(qn, kn, vn, p0, alog, dtb, pad, "gdn_fwd")
    oa = _grms_fwd(oa_raw, p0, 12, w["ab_gnorm_g"], "gdn_gate")
    if late is None:
        ob, rtot = _sb_fwd(p0, pad, "sb_fwd")
    else:
        ob, rtot, g_about, g_cin, g_cout, g_w1, g_w2 = _sb_fwd(p0, pad, "sb_fwd", cargo=late, exchange=GATHER)
        w = dict(w, ab_w_out=g_about.reshape(D, D), c_w_in=g_cin, c_w_out=g_cout.reshape(D, D), mlp_w1=g_w1, mlp_w2=g_w2)
    ycat = jnp.concatenate([oa, ob], axis=1)
    mix0 = _mm(ycat, w["ab_w_out"], name="ab_out")
    h1, h1b = _ln_res_fwd(h0, mix0, row(w["ln_mix_g"], 0), row(w["ln_mix_b"], 0), "ln_mix_0")
    a0, r0, m0 = _mlp_fwd(h1b, w["mlp_w1"], w["mlp_w2"], 0)
    h2, h2b = _ln_res_fwd(h1, m0, row(w["ln_ffn_g"], 0), row(w["ln_ffn_b"], 0), "ln_ffn_0")
    p1 = _mm(h2b, w["c_w_in"], b_view=("cols", 0), name="c_in")
    oc_raw, ss1 = _hg_fwd(p1, w["c_lb_raw"], ecat, pad, "hg_fwd")
    yc = _grms_fwd(oc_raw, p1, 3 * HG_H, w["c_gnorm_g"], "hg_gate")
    mix1 = _mm(yc, w["c_w_out"], name="c_out")
    h3, h3b = _ln_res_fwd(h2, mix1, row(w["ln_mix_g"], 1), row(w["ln_mix_b"], 1), "ln_mix_1")
    a1, r1, m1 = _mlp_fwd(h3b, w["mlp_w1"], w["mlp_w2"], 1)
    h4, _ = _ln_res_fwd(h3, m1, row(w["ln_ffn_g"], 1), row(w["ln_ffn_b"], 1), "ln_ffn_1")
    loss, dh4 = _loss_fwd(h4, tgt, pad + N_META, "loss")

    zero = jnp.zeros_like(dh4)
    dh3a, dm1b, dfg1, dfb1 = _ln_res_bwd(h3, m1, row(w["ln_ffn_g"], 1), row(w["ln_ffn_b"], 1), dh4, zero, "ln_ffn_bwd_1")
    dh3b, dw1_1, dw2_1 = _mlp_bwd(h3b, a1, r1, dm1b, w["mlp_w1"], w["mlp_w2"], 1)
    dh2a, dmix1b, dmg1, dmb1 = _ln_res_bwd(h2, mix1, row(w["ln_mix_g"], 1), row(w["ln_mix_b"], 1), dh3a, dh3b, "ln_mix_bwd_1")
    dyc = _mm(dmix1b, w["c_w_out"], tb=True, name="c_out_dx")
    dwco = _mm(yc, dmix1b, ta=True, out_dtype=BF16, name="c_out_dw")
    doc, dzc, dcg = _grms_bwd(oc_raw, p1, 3 * HG_H, w["c_gnorm_g"], dyc, 0, "hg_gate_bwd")
    landed = {}
    rows4 = lambda a: a.reshape(N_CHIP, -1, D)
    if late is None:
        dq1, df1, di1, dlb = _hg_bwd(p1, w["c_lb_raw"], ecat, ss1, doc, pad, "hg_bwd")
    else:
        dq1, df1, di1, dlb, landed["w1_1"], landed["w2_1"], landed["c_w_out"] = _hg_bwd(
            p1, w["c_lb_raw"], ecat, ss1, doc, pad, "hg_bwd", cargo=[dw1_1, rows4(dw2_1), rows4(dwco)], exchange=SCATTER)
    dp1 = _assemble_bf16([(dq1, "cols"), (df1, "cols"), (di1, "cols"), (dzc, "cols")], "c_in_dy")
    dh2b = _mm(dp1, w["c_w_in"], tb=True, b_view=("cols", 0), name="c_in_dx")
    dwc = _mm(h2b, dp1, ta=True, out_dtype=BF16, out_split=N_CHIP, name="c_in_dw")
    dh1a, dm0b, dfg0, dfb0 = _ln_res_bwd(h1, m0, row(w["ln_ffn_g"], 0), row(w["ln_ffn_b"], 0), dh2a, dh2b, "ln_ffn_bwd_0")
    dh1b, dw1_0, dw2_0 = _mlp_bwd(h1b, a0, r0, dm0b, w["mlp_w1"], w["mlp_w2"], 0)
    dh0a, dmix0b, dmg0, dmb0 = _ln_res_bwd(h0, mix0, row(w["ln_mix_g"], 0), row(w["ln_mix_b"], 0), dh1a, dh1b, "ln_mix_bwd_0")
    dycat = _mm(dmix0b, w["ab_w_out"], tb=True, name="ab_out_dx")
    dwabo = _mm(ycat, dmix0b, ta=True, out_dtype=BF16, name="ab_out_dw")
    doa, dza, dag = _grms_bwd(oa_raw, p0, 12, w["ab_gnorm_g"], dycat, 0, "gdn_gate_bwd")
    if late is None:
        dqn, dkn, dvn, dbb, daa, dal, ddt = _gdn_bwd(qn, kn, vn, p0, alog, dtb, ss0, doa, pad, "gdn_bwd")
        dqb, dkb, dvb = _sb_bwd(p0, rtot, dycat, 4, pad, "sb_bwd")
    else:
        dqn, dkn, dvn, dbb, daa, dal, ddt, landed["c_w_in"], landed["w2_0"] = _gdn_bwd(
            qn, kn, vn, p0, alog, dtb, ss0, doa, pad, "gdn_bwd", cargo=[dwc, rows4(dw2_0)], exchange=SCATTER)
        dqb, dkb, dvb, landed["w1_0"], landed["ab_w_out"] = _sb_bwd(
            p0, rtot, dycat, 4, pad, "sb_bwd", cargo=[dw1_0, rows4(dwabo)], exchange=SCATTER)
    dpq, dcq = _conv_bwd(p0, 0, cw[0], dqn, "q", pad, "conv_q_bwd")
    dpk, dck = _conv_bwd(p0, 4, cw[1], dkn, "k", pad, "conv_k_bwd")
    dpv, dcv = _conv_bwd(p0, 8, cw[2], dvn, "v", pad, "conv_v_bwd")
    dp0 = _assemble_bf16([(dpq, "cols"), (dpk, "cols"), (dpv, "cols"), (dza, "cols"), (dbb, "cols"), (daa, "cols"),
                          (dqb, "cols"), (dkb, "cols"), (dvb, "cols")], "ab_in_dy")
    dh0b = _mm(dp0, w["ab_w_in"], tb=True, name="ab_in_dx")
    dwab = _mm(h0b, dp0, ta=True, out_dtype=BF16, name="ab_in_dw")
    dh0 = _add2(dh0a, dh0b, "dh0")

    grads = {
        "ab_w_in": dwab, "conv_w": jnp.concatenate([dcq, dck, dcv], axis=1),
        "a_log": dal[:, :GDN_H], "dt_bias": ddt[:, :GDN_H],
        "ab_gnorm_g": dag, "ab_w_out": dwabo, "c_w_in": dwc, "c_lb_raw": dlb, "c_gnorm_g": dcg, "c_w_out": dwco,
        "ln_mix_g": jnp.concatenate([dmg0, dmg1], 0), "ln_mix_b": jnp.concatenate([dmb0, dmb1], 0),
        "w1_0": dw1_0, "w1_1": dw1_1, "w2_0": dw2_0, "w2_1": dw2_1,
        "ln_ffn_g": jnp.concatenate([dfg0, dfg1], 0), "ln_ffn_b": jnp.concatenate([dfb0, dfb1], 0),
        "landed": landed,
    }
    return loss, dh0, grads


MESH = pl.DeviceIdType.MESH
ANY = pl.BlockSpec(memory_space=pl.ANY)
N_CHIP = 4
N_DEV = 8
CHIP_REL = ((1, 0), (0, 1), (1, 1))
DEV_REL = tuple((dx, dy, dc) for dx in (0, 1) for dy in (0, 1) for dc in (0, 1))[1:]

def _pos():
    return lax.axis_index("x"), lax.axis_index("y"), lax.axis_index("c")


def _flip(a, d):
    return a + d - 2 * a * d


class _Exchange:
    def __init__(self, local, sends, recvs):
        self.local, self.sends, self.recvs = local, sends, recvs

    def start(self):
        for cp in self.local + self.sends:
            cp.start()

    def wait(self):
        for cp in self.recvs:
            cp.wait_recv()
        for cp in self.sends:
            cp.wait_send()
        for cp in self.local:
            cp.wait()


def _gather_sems(n):
    return [pltpu.SemaphoreType.DMA((3 * n,)), pltpu.SemaphoreType.DMA((3 * n,)), pltpu.SemaphoreType.DMA((n,))]


def _gather_copies(x_refs, o_refs, send_sems, recv_sems, local_sems):
    n = len(x_refs)
    x, y, c = _pos()
    local = [pltpu.make_async_copy(x_refs[a], o_refs[a].at[2 * x + y], local_sems.at[a]) for a in range(n)]

    def copy(a, k, sending):
        tx, ty = _flip(x, CHIP_REL[k][0]), _flip(y, CHIP_REL[k][1])
        return pltpu.make_async_remote_copy(
            src_ref=x_refs[a], dst_ref=o_refs[a].at[2 * x + y if sending else 2 * tx + ty],
            send_sem=send_sems.at[3 * a + k], recv_sem=recv_sems.at[3 * a + k], device_id=(tx, ty, c), device_id_type=MESH)

    pairs = [(a, k) for a in range(n) for k in range(3)]
    return _Exchange(local, [copy(a, k, True) for a, k in pairs], [copy(a, k, False) for a, k in pairs])


def _gather_shapes(bufs):
    return [jax.ShapeDtypeStruct((N_CHIP,) + b.shape, b.dtype) for b in bufs]


def _chip_allgather(bufs, name):
    n = len(bufs)

    def body(*refs):
        ex = _gather_copies(refs[:n], refs[n:2 * n], *refs[2 * n:])
        ex.start()
        ex.wait()

    return pl.pallas_call(
        body, name=name, in_specs=[ANY] * n, out_specs=[ANY] * n, out_shape=_gather_shapes(bufs),
        scratch_shapes=_gather_sems(n), compiler_params=pltpu.CompilerParams(has_side_effects=True),
    )(*bufs)


def _scatter_sems(n):
    nr = N_DEV - 1
    return [pltpu.SemaphoreType.DMA((nr * n,)), pltpu.SemaphoreType.DMA((nr * n,)), pltpu.SemaphoreType.DMA((n,))]


def _scatter_copies(g_refs, o_refs, send_sems, recv_sems, local_sems):
    n = len(g_refs)
    nr = N_DEV - 1
    x, y, c = _pos()
    me = 4 * x + 2 * y + c
    local = [pltpu.make_async_copy(g_refs[a].at[2 * x + y], o_refs[a].at[me], local_sems.at[a]) for a in range(n)]

    def copy(a, k, sending):
        dx, dy, dc = DEV_REL[k]
        tx, ty, tc = _flip(x, dx), _flip(y, dy), _flip(c, dc)
        return pltpu.make_async_remote_copy(
            src_ref=g_refs[a].at[2 * tx + ty], dst_ref=o_refs[a].at[me if sending else 4 * tx + 2 * ty + tc],
            send_sem=send_sems.at[nr * a + k], recv_sem=recv_sems.at[nr * a + k],
            device_id=(tx, ty, tc), device_id_type=MESH)

    pairs = [(a, k) for a in range(n) for k in range(nr)]
    return _Exchange(local, [copy(a, k, True) for a, k in pairs], [copy(a, k, False) for a, k in pairs])


def _scatter_shapes(gs):
    return [jax.ShapeDtypeStruct((N_DEV,) + g.shape[1:], g.dtype) for g in gs]


def _grad_alltoall(gs, name):
    n = len(gs)

    def body(*refs):
        ex = _scatter_copies(refs[:n], refs[n:2 * n], *refs[2 * n:])
        ex.start()
        ex.wait()

    return pl.pallas_call(
        body, name=name, in_specs=[ANY] * n, out_specs=[ANY] * n, out_shape=_scatter_shapes(gs),
        scratch_shapes=_scatter_sems(n), compiler_params=pltpu.CompilerParams(has_side_effects=True),
    )(*gs)


GATHER = (_gather_copies, _gather_shapes, _gather_sems)
SCATTER = (_scatter_copies, _scatter_shapes, _scatter_sems)


def _sum_slots(r, name):
    n, rh, w = r.shape
    tr = _pick(rh, (256, 128, 64, 16))

    def body(r_ref, o_ref):
        acc = r_ref[0].astype(F32)
        for s in range(1, n):
            acc = acc + r_ref[s].astype(F32)
        o_ref[...] = acc

    return pl.pallas_call(
        body, name=name, grid=(rh // tr,), in_specs=[pl.BlockSpec((n, tr, w), lambda i: (0, i, 0))],
        out_specs=pl.BlockSpec((tr, w), lambda i: (i, 0)), out_shape=jax.ShapeDtypeStruct((rh, w), F32),
        compiler_params=_cparams(),
    )(r)


def _small_allreduce(buf, name):
    r, w = buf.shape

    def body(b_ref, o_ref, land_ref, send_sems, recv_sems):
        x, y, c = _pos()
        me = 4 * x + 2 * y + c
        land_ref[me] = b_ref[...]

        def target(k):
            dx, dy, dc = DEV_REL[k]
            return _flip(x, dx), _flip(y, dy), _flip(c, dc)

        sends = []
        for k in range(N_DEV - 1):
            tx, ty, tc = target(k)
            cp = pltpu.make_async_remote_copy(
                src_ref=b_ref, dst_ref=land_ref.at[me], send_sem=send_sems.at[k], recv_sem=recv_sems.at[k],
                device_id=(tx, ty, tc), device_id_type=MESH)
            cp.start()
            sends.append(cp)
        for k in range(N_DEV - 1):
            tx, ty, tc = target(k)
            pltpu.make_async_remote_copy(
                src_ref=b_ref, dst_ref=land_ref.at[4 * tx + 2 * ty + tc], send_sem=send_sems.at[k],
                recv_sem=recv_sems.at[k], device_id=(tx, ty, tc), device_id_type=MESH).wait_recv()
        for cp in sends:
            cp.wait_send()
        acc = land_ref[0]
        for s in range(1, N_DEV):
            acc = acc + land_ref[s]
        o_ref[...] = acc

    vm = pl.BlockSpec(memory_space=pltpu.VMEM)
    return pl.pallas_call(
        body, name=name, in_specs=[vm], out_specs=vm, out_shape=jax.ShapeDtypeStruct((r, w), F32),
        scratch_shapes=[pltpu.VMEM((N_DEV, r, w), F32), pltpu.SemaphoreType.DMA((N_DEV - 1,)),
                        pltpu.SemaphoreType.DMA((N_DEV - 1,))],
        compiler_params=pltpu.CompilerParams(has_side_effects=True),
    )(buf)


def _adamw(w, g, m, v, name):
    r, c = w.shape
    tr = _pick(r, (256, 128, 64, 8)) if r * c > (1 << 18) else r

    def body(w_ref, g_ref, m_ref, v_ref, d_ref, m2_ref, v2_ref):
        gg = g_ref[...]
        m2 = ADAM_B1 * m_ref[...] + (1.0 - ADAM_B1) * gg
        v2 = ADAM_B2 * v_ref[...] + (1.0 - ADAM_B2) * (gg * gg)
        m_hat = m2 / (1.0 - ADAM_B1 ** ADAM_STEP)
        v_hat = v2 / (1.0 - ADAM_B2 ** ADAM_STEP)
        d_ref[...] = -ADAM_LR * (m_hat / (jnp.sqrt(v_hat) + ADAM_EPS) + ADAM_WD * w_ref[...])
        m2_ref[...] = m2
        v2_ref[...] = v2

    blk = pl.BlockSpec((tr, c), lambda i: (i, 0))
    sds = jax.ShapeDtypeStruct((r, c), F32)
    return pl.pallas_call(body, name=name, grid=(r // tr,), in_specs=[blk] * 4, out_specs=[blk] * 3,
                          out_shape=[sds] * 3, compiler_params=_cparams())(w, g, m, v)


BIG = ("ab_w_in", "ab_w_out", "c_w_in", "c_w_out", "mlp_w1", "mlp_w2")
SMALL = ("ln_mix_g", "ln_mix_b", "ln_ffn_g", "ln_ffn_b", "c_lb_raw", "ab_a_log", "ab_dt_bias", "ab_gnorm_g", "c_gnorm_g")
SMALL_ROWS = 16
CONV_ROWS = 8
CONV_W = 3 * GDN_H * HD


def _conv_to_rows(cw):
    return jnp.pad(cw, ((0, 0), (0, 2 * D - CONV_W))).reshape(CONV_ROWS, D)


def _rows_to_conv(rows):
    return rows.reshape(CONV_K, 2 * D)[:, :CONV_W]


def _pack_small(d):
    rows = [jnp.pad(d[n], ((0, 0), (0, D - d[n].shape[1]))) for n in SMALL]
    buf = jnp.concatenate(rows, axis=0)
    return jnp.pad(buf, ((0, SMALL_ROWS - buf.shape[0]), (0, 0)))


def _unpack_small(buf, like):
    out, r = {}, 0
    for n in SMALL:
        nr, nc = like[n].shape
        out[n] = buf[r:r + nr, :nc]
        r += nr
    return out


def kernel(x, meta_tokens, ab_w_in, ab_conv_w, ab_a_log, ab_dt_bias, ab_gnorm_g, ab_w_out, c_w_in, c_lb_raw, c_gnorm_g, c_w_out, ln_mix_g, ln_mix_b, mlp_w1, mlp_w2, ln_ffn_g, ln_ffn_b, loss_target, m_meta_tokens, m_ab_w_in, m_ab_conv_w, m_ab_a_log, m_ab_dt_bias, m_ab_gnorm_g, m_ab_w_out, m_c_w_in, m_c_lb_raw, m_c_gnorm_g, m_c_w_out, m_ln_mix_g, m_ln_mix_b, m_mlp_w1, m_mlp_w2, m_ln_ffn_g, m_ln_ffn_b, v_meta_tokens, v_ab_w_in, v_ab_conv_w, v_ab_a_log, v_ab_dt_bias, v_ab_gnorm_g, v_ab_w_out, v_c_w_in, v_c_lb_raw, v_c_gnorm_g, v_c_w_out, v_ln_mix_g, v_ln_mix_b, v_mlp_w1, v_mlp_w2, v_ln_ffn_g, v_ln_ffn_b):
    names = ("meta_tokens", "ab_w_in", "ab_conv_w", "ab_a_log", "ab_dt_bias", "ab_gnorm_g", "ab_w_out", "c_w_in",
             "c_lb_raw", "c_gnorm_g", "c_w_out", "ln_mix_g", "ln_mix_b", "mlp_w1", "mlp_w2", "ln_ffn_g", "ln_ffn_b")
    wts = dict(zip(names, (meta_tokens, ab_w_in, ab_conv_w, ab_a_log, ab_dt_bias, ab_gnorm_g, ab_w_out, c_w_in, c_lb_raw,
                           c_gnorm_g, c_w_out, ln_mix_g, ln_mix_b, mlp_w1, mlp_w2, ln_ffn_g, ln_ffn_b)))
    mom_m = dict(zip(names, (m_meta_tokens, m_ab_w_in, m_ab_conv_w, m_ab_a_log, m_ab_dt_bias, m_ab_gnorm_g, m_ab_w_out,
                             m_c_w_in, m_c_lb_raw, m_c_gnorm_g, m_c_w_out, m_ln_mix_g, m_ln_mix_b, m_mlp_w1, m_mlp_w2,
                             m_ln_ffn_g, m_ln_ffn_b)))
    mom_v = dict(zip(names, (v_meta_tokens, v_ab_w_in, v_ab_conv_w, v_ab_a_log, v_ab_dt_bias, v_ab_gnorm_g, v_ab_w_out,
                             v_c_w_in, v_c_lb_raw, v_c_gnorm_g, v_c_w_out, v_ln_mix_g, v_ln_mix_b, v_mlp_w1, v_mlp_w2,
                             v_ln_ffn_g, v_ln_ffn_b)))
    seq = x.shape[1]
    pad = (-(N_META + seq)) % QB
    xi, yi, ci = _pos()
    chip = 2 * xi + yi

    gat_ab_in, = _chip_allgather([ab_w_in[0].astype(BF16)], "gather_weights")
    late = [ab_w_out[0].astype(BF16), c_w_in.astype(BF16), c_w_out[0].astype(BF16), mlp_w1.astype(BF16),
            mlp_w2.astype(BF16)]
    mcols, ccols = meta_tokens.shape[1], ab_conv_w.shape[2]
    place = jnp.concatenate([
        lax.dynamic_update_slice(jnp.zeros((N_META, D), F32), 0.5 * meta_tokens, (0, chip * mcols)),
        _conv_to_rows(lax.dynamic_update_slice(jnp.zeros((CONV_K, CONV_W), F32), 0.5 * ab_conv_w[0], (0, chip * ccols)))],
        axis=0)
    placed = _small_allreduce(place, "gather_meta")
    meta_full = placed[:N_META]

    w = {
        "ab_w_in": _pad_ab_cols(jnp.transpose(gat_ab_in, (1, 0, 2)).reshape(D, AB_TRUE)),
        "conv_w": _rows_to_conv(placed[N_META:]), "a_log": ab_a_log, "dt_bias": ab_dt_bias,
        "ab_gnorm_g": ab_gnorm_g, "c_lb_raw": c_lb_raw,
        "c_gnorm_g": c_gnorm_g, "ln_mix_g": ln_mix_g, "ln_mix_b": ln_mix_b, "ln_ffn_g": ln_ffn_g, "ln_ffn_b": ln_ffn_b,
    }

    h0 = jnp.concatenate([jnp.zeros((pad, D), F32), meta_full, x[0]], axis=0)
    tgt = jnp.concatenate([jnp.zeros((pad + N_META, D), F32), loss_target[0]], axis=0)
    loss8, dh0, g = _local_step(h0, tgt, w, pad, late)
    loss = lax.psum(loss8[0, 0], ("x", "y", "c"))
    grad_x = dh0[pad + N_META:][None]

    gsmall = {"ln_mix_g": g["ln_mix_g"], "ln_mix_b": g["ln_mix_b"], "ln_ffn_g": g["ln_ffn_g"], "ln_ffn_b": g["ln_ffn_b"],
              "c_lb_raw": g["c_lb_raw"], "ab_a_log": g["a_log"], "ab_dt_bias": g["dt_bias"], "ab_gnorm_g": g["ab_gnorm_g"],
              "c_gnorm_g": g["c_gnorm_g"]}
    sbuf = jnp.concatenate([_pack_small(gsmall), dh0[pad:pad + N_META], _conv_to_rows(g["conv_w"])], axis=0)
    ssum = _small_allreduce(sbuf, "allreduce_small")
    grads = _unpack_small(ssum[:SMALL_ROWS], wts)
    grads["meta_tokens"] = lax.dynamic_slice(ssum[SMALL_ROWS:SMALL_ROWS + N_META], (0, chip * mcols), (N_META, mcols))
    grads["ab_conv_w"] = lax.dynamic_slice(_rows_to_conv(ssum[SMALL_ROWS + N_META:]), (0, chip * ccols), (CONV_K, ccols))[None]

    dab = jnp.transpose(_unpad_ab_cols(g["ab_w_in"]).reshape(D, N_CHIP, AB_TRUE // N_CHIP), (1, 0, 2))
    landed = dict(g["landed"])
    landed["ab_w_in"], = _grad_alltoall([dab], "grad_alltoall")
    sums = {k: _sum_slots(v, f"grad_sum_{k}") for k, v in landed.items()}
    for n in ("ab_w_in", "ab_w_out", "c_w_in", "c_w_out"):
        grads[n] = sums[n][None]
    grads["mlp_w1"] = jnp.stack([sums["w1_0"], sums["w1_1"]])
    grads["mlp_w2"] = jnp.stack([sums["w2_0"], sums["w2_1"]])

    delta, new_m, new_v = {}, {}, {}
    for n in ("meta_tokens", "ab_conv_w") + BIG:
        shp = wts[n].shape
        to2 = lambda a: a.reshape(-1, shp[-1])
        d2, m2, v2 = _adamw(to2(wts[n]), to2(grads[n]), to2(mom_m[n]), to2(mom_v[n]), f"adamw_{n}")
        delta[n], new_m[n], new_v[n] = d2.reshape(shp), m2.reshape(shp), v2.reshape(shp)
    d2, m2, v2 = _adamw(_pack_small(wts), ssum[:SMALL_ROWS], _pack_small(mom_m), _pack_small(mom_v), "adamw_small")
    delta.update(_unpack_small(d2, wts))
    new_m.update(_unpack_small(m2, wts))
    new_v.update(_unpack_small(v2, wts))

    return (loss, grad_x, *[grads[n] for n in names], *[delta[n] for n in names], *[new_m[n] for n in names],
            *[new_v[n] for n in names])
```

```python
import functools
import math

import numpy as np
import jax
import jax.numpy as jnp
from jax import lax
from jax.experimental import pallas as pl
from jax.experimental.pallas import tpu as pltpu

F32 = jnp.float32
BF16 = jnp.bfloat16

D = 1024
N_META = 16
D_FF = 4 * D
DEPTH = 2
GDN_H = 4
SB_H = 8
SB_DH = 64
HG_H = 8
HD = 128
CH = 64
QB = 128
ALPHA = float((2 * DEPTH) ** 0.25)
LN_EPS = 1e-5
RMS_EPS = 1e-6
L2_EPS = 1e-6
NEG = -1e30

ADAM_LR = 0.001
ADAM_B1 = 0.9
ADAM_B2 = 0.999
ADAM_EPS = 1e-08
ADAM_WD = 0.01
ADAM_STEP = 10

AB_W = 30 * HD
AB_TRUE = 3592
VMEM_LIMIT = 56 * 1024 * 1024

NN = ((1,), (0,))
NT = ((1,), (1,))
TN = ((0,), (0,))


def _cparams(**kw):
    return pltpu.CompilerParams(vmem_limit_bytes=VMEM_LIMIT, **kw)


def _dg(a, b, dims, mode):
    if mode == "h":
        return lax.dot_general(a, b, dims, precision=lax.Precision.HIGHEST, preferred_element_type=F32)
    if mode == "b":
        return lax.dot_general(a.astype(BF16), b.astype(BF16), dims, preferred_element_type=F32)
    ah, bh = a.astype(BF16), b.astype(BF16)
    al, bl = (a - ah.astype(F32)).astype(BF16), (b - bh.astype(F32)).astype(BF16)
    d = lambda x, y: lax.dot_general(x, y, dims, preferred_element_type=F32)
    return d(ah, bh) + (d(ah, bl) + d(al, bh))


def _make_dots(mode, batched=False):
    if batched:
        nn_d, nt_d, tn_d = (((2,), (1,)), ((0,), (0,))), (((2,), (2,)), ((0,), (0,))), (((1,), (1,)), ((0,), (0,)))
    else:
        nn_d, nt_d, tn_d = (NN, ((), ())), (NT, ((), ())), (TN, ((), ()))

    @jax.custom_vjp
    def nn(a, b):
        return _dg(a, b, nn_d, mode)

    @jax.custom_vjp
    def nt(a, b):
        return _dg(a, b, nt_d, mode)

    @jax.custom_vjp
    def tn(a, b):
        return _dg(a, b, tn_d, mode)

    nn.defvjp(lambda a, b: (nn(a, b), (a, b)), lambda r, g: (nt(g, r[1]), tn(r[0], g)))
    nt.defvjp(lambda a, b: (nt(a, b), (a, b)), lambda r, g: (nn(g, r[1]), tn(g, r[0])))
    tn.defvjp(lambda a, b: (tn(a, b), (a, b)), lambda r, g: (nt(r[1], g), nn(r[0], g)))
    return nn, nt, tn


hnn, hnt, htn = _make_dots("h")
bbnn, bbnt, bbtn = _make_dots("b", True)
mbnn, mbnt, mbtn = _make_dots("m", True)
hbnn, hbnt, hbtn = _make_dots("h", True)


def _split3(x, axis):
    x1 = x.astype(BF16)
    r1 = x - x1.astype(F32)
    x2 = r1.astype(BF16)
    x3 = (r1 - x2.astype(F32)).astype(BF16)
    return jnp.concatenate([x1, x2, x3], axis=axis)


@jax.custom_vjp
def _mask_dot(e3, x):
    return lax.dot_general(e3[0], _split3(x, 0), (NN, ((), ())), preferred_element_type=F32)


def _mask_dot_bwd(e3, g):
    dx = lax.dot_general(e3[1], _split3(g, 0), (TN, ((), ())), preferred_element_type=F32)
    return (jnp.zeros_like(e3[0]), jnp.zeros_like(e3[1])), dx


_mask_dot.defvjp(lambda e3, x: (_mask_dot(e3, x), e3), _mask_dot_bwd)


def _heads(a, n):
    return jnp.concatenate([a[None, :, h * HD:(h + 1) * HD] for h in range(n)], axis=0)


def _sigmoid(x):
    return jax.nn.sigmoid(x)


def _silu(x):
    return x * jax.nn.sigmoid(x)


def _softplus(x):
    return jnp.maximum(x, 0.0) + jnp.log(1.0 + jnp.exp(-jnp.abs(x)))


def _iota(shape, dim):
    return lax.broadcasted_iota(jnp.int32, shape, dim)


def _pick(n, prefs):
    for p in prefs:
        if n % p == 0:
            return p
    return n


def _mm(a, b, *, ta=False, tb=False, out_dtype=F32, name, b_view=None, out_split=0, act=False, gate=None,
        cargo=(), exchange=None):
    if ta:
        k_dim, m_dim = a.shape
    else:
        m_dim, k_dim = a.shape
    if b_view is None:
        w_rows, w_cols = b.shape
    else:
        kind, layer = b_view
        nj, _, blk_r, blk_c = b.shape
        w_rows, w_cols = (blk_r, nj * blk_c) if kind == "cols" else (nj * blk_r, blk_c)
    n_dim = w_rows if tb else w_cols
    assert (w_cols if tb else w_rows) == k_dim
    tm = _pick(m_dim, (1024, 1056, 704, 640, 512, 384, 256, 128))
    tn = _pick(n_dim, (1024, 1056, 704, 640, 512, 384, 256, 128))
    tk = _pick(k_dim, (1024, 1056, 704, 512, 384, 256, 128))
    nk = k_dim // tk
    a_spec = pl.BlockSpec((tk, tm), lambda i, j, k: (k, i)) if ta else pl.BlockSpec((tm, tk), lambda i, j, k: (i, k))
    wb = (tn, tk) if tb else (tk, tn)
    w_idx = (lambda i, j, k: (j, k)) if tb else (lambda i, j, k: (k, j))
    if b_view is None:
        b_spec = pl.BlockSpec(wb, w_idx)
    elif kind == "cols":
        per = blk_c // wb[1]
        b_spec = pl.BlockSpec((None, None) + wb,
                              lambda i, j, k: (w_idx(i, j, k)[1] // per, layer, w_idx(i, j, k)[0], w_idx(i, j, k)[1] % per))
    else:
        per = blk_r // wb[0]
        b_spec = pl.BlockSpec((None, None) + wb,
                              lambda i, j, k: (w_idx(i, j, k)[0] // per, layer, w_idx(i, j, k)[0] % per, w_idx(i, j, k)[1]))
    if out_split:
        per_o = (n_dim // out_split) // tn
        out_spec = pl.BlockSpec((None, tm, tn), lambda i, j, k: (j // per_o, i, j % per_o))
        out_sds = jax.ShapeDtypeStruct((out_split, m_dim, n_dim // out_split), out_dtype)
    else:
        out_spec = pl.BlockSpec((tm, tn), lambda i, j, k: (i, j))
        out_sds = jax.ShapeDtypeStruct((m_dim, n_dim), out_dtype)
    dims = (((0 if ta else 1,), (1 if tb else 0,)), ((), ()))
    extra = [] if gate is None else [gate]
    n_out = 2 if act else 1

    def finish(acc, refs):
        if act:
            refs[0][...] = acc.astype(refs[0].dtype)
            r = jnp.maximum(acc, 0.0)
            refs[1][...] = (r * r).astype(refs[1].dtype)
        elif gate is not None:
            refs[1][...] = (acc * (2.0 * jnp.maximum(refs[0][...].astype(F32), 0.0))).astype(refs[1].dtype)
        else:
            refs[0][...] = acc.astype(refs[0].dtype)

    grid = (m_dim // tm, n_dim // tn, nk)
    nc = len(cargo)

    def body(a_ref, b_ref, *rest):
        acc_ref = rest[-1]
        ids = [pl.program_id(d) for d in range(3)]
        outs, end_cargo = _cargo_bounds(
            rest[len(extra):-1], nc, n_out, exchange, (ids[0] == 0) & (ids[1] == 0) & (ids[2] == 0),
            (ids[0] == grid[0] - 1) & (ids[1] == grid[1] - 1) & (ids[2] == grid[2] - 1))
        refs = tuple(rest[:len(extra)]) + tuple(outs)
        part = lax.dot_general(a_ref[...], b_ref[...], dims, preferred_element_type=F32)
        if nk == 1:
            finish(part, refs)
        else:
            k = ids[2]

            @pl.when(k == 0)
            def _():
                acc_ref[...] = part

            @pl.when(k > 0)
            def _():
                acc_ref[...] += part

            @pl.when(k == nk - 1)
            def _():
                finish(acc_ref[...], refs)
        end_cargo()

    out = pl.pallas_call(
        body, name=name, grid=grid,
        in_specs=[a_spec, b_spec] + [pl.BlockSpec((tm, tn), lambda i, j, k: (i, j))] * len(extra) + [ANY] * nc,
        out_specs=[out_spec] * n_out + [ANY] * nc,
        out_shape=[out_sds] * n_out + (exchange[1](cargo) if nc else []),
        scratch_shapes=(exchange[2](nc) if nc else []) + [pltpu.VMEM((tm, tn) if nk > 1 else (8, 128), F32)],
        compiler_params=_cparams(dimension_semantics=("arbitrary",) * 3 if nc else ("parallel", "parallel", "arbitrary")),
    )(a, b, *extra, *cargo)
    if nc:
        return out
    return out if act else out[0]


def _row_tile(t_pad, width):
    for tr in (528, 352, 176, 128, 64):
        if t_pad % tr == 0 and tr * width * 4 <= (3 << 19) and tr % 16 == 0:
            return tr
    return 64 if t_pad % 64 == 0 else t_pad


def _ln_res_fn(h, m, g, b):
    x = ALPHA * h + m
    mu = jnp.mean(x, axis=-1, keepdims=True)
    xc = x - mu
    var = jnp.mean(xc * xc, axis=-1, keepdims=True)
    return xc * lax.rsqrt(var + LN_EPS) * g + b


def _ln_res_fwd(h, m, g, b, name):
    t_pad = h.shape[0]
    tr = _row_tile(t_pad, D)

    def body(h_ref, m_ref, g_ref, b_ref, y_ref, yb_ref):
        y = _ln_res_fn(h_ref[...], m_ref[...], g_ref[...], b_ref[...])
        y_ref[...] = y
        yb_ref[...] = y.astype(BF16)

    row = pl.BlockSpec((tr, D), lambda i: (i, 0))
    par = pl.BlockSpec((1, D), lambda i: (0, 0))
    return pl.pallas_call(
        body, name=name, grid=(t_pad // tr,), in_specs=[row, row, par, par], out_specs=[row, row],
        out_shape=[jax.ShapeDtypeStruct((t_pad, D), F32), jax.ShapeDtypeStruct((t_pad, D), BF16)],
        compiler_params=_cparams(),
    )(h, m, g, b)


def _ln_res_bwd(h, m, g, b, dy1, dy2, name):
    t_pad = h.shape[0]
    tr = _row_tile(t_pad, D)

    def body(h_ref, m_ref, g_ref, b_ref, d1_ref, d2_ref, dh_ref, dm_ref, dg_ref, db_ref):
        _, vjp = jax.vjp(_ln_res_fn, h_ref[...], m_ref[...], g_ref[...], b_ref[...])
        dh, dm, dg, db = vjp(d1_ref[...] + d2_ref[...])
        dh_ref[...] = dh
        dm_ref[...] = dm.astype(BF16)

        @pl.when(pl.program_id(0) == 0)
        def _():
            dg_ref[...] = jnp.zeros_like(dg_ref)
            db_ref[...] = jnp.zeros_like(db_ref)

        dg_ref[...] += dg
        db_ref[...] += db

    row = pl.BlockSpec((tr, D), lambda i: (i, 0))
    par = pl.BlockSpec((1, D), lambda i: (0, 0))
    return pl.pallas_call(
        body, name=name, grid=(t_pad // tr,), in_specs=[row, row, par, par, row, row],
        out_specs=[row, row, par, par],
        out_shape=[jax.ShapeDtypeStruct((t_pad, D), F32), jax.ShapeDtypeStruct((t_pad, D), BF16),
                   jax.ShapeDtypeStruct((1, D), F32), jax.ShapeDtypeStruct((1, D), F32)],
        compiler_params=_cparams(),
    )(h, m, g, b, dy1, dy2)


def _grms_fn(o, z, g):
    y = o * lax.rsqrt(jnp.mean(o * o, axis=-1, keepdims=True) + RMS_EPS) * g
    return y * _silu(z)


def _grms_fwd(o, z_arr, z_blk0, g, name):
    t_pad, w = o.shape
    nh = w // HD
    tr = _row_tile(t_pad, HD * 4)

    def body(o_ref, z_ref, g_ref, y_ref):
        y_ref[...] = _grms_fn(o_ref[...], z_ref[...], g_ref[...]).astype(BF16)

    return pl.pallas_call(
        body, name=name, grid=(t_pad // tr, nh),
        in_specs=[pl.BlockSpec((tr, HD), lambda i, h: (i, h)),
                  pl.BlockSpec((tr, HD), lambda i, h: (i, z_blk0 + h)),
                  pl.BlockSpec((1, HD), lambda i, h: (0, 0))],
        out_specs=pl.BlockSpec((tr, HD), lambda i, h: (i, h)),
        out_shape=jax.ShapeDtypeStruct((t_pad, w), BF16), compiler_params=_cparams(),
    )(o, z_arr, g)


def _grms_bwd(o, z_arr, z_blk0, g, dy_arr, dy_blk0, name):
    t_pad, w = o.shape
    nh = w // HD
    tr = _row_tile(t_pad, HD * 4)

    def body(o_ref, z_ref, g_ref, dy_ref, do_ref, dz_ref, dg_ref):
        _, vjp = jax.vjp(_grms_fn, o_ref[...], z_ref[...], g_ref[...])
        do, dz, dg = vjp(dy_ref[...])
        do_ref[...] = do
        dz_ref[...] = dz

        @pl.when((pl.program_id(0) == 0) & (pl.program_id(1) == 0))
        def _():
            dg_ref[...] = jnp.zeros_like(dg_ref)

        dg_ref[...] += dg

    blk = pl.BlockSpec((tr, HD), lambda i, h: (i, h))
    return pl.pallas_call(
        body, name=name, grid=(t_pad // tr, nh),
        in_specs=[blk, pl.BlockSpec((tr, HD), lambda i, h: (i, z_blk0 + h)),
                  pl.BlockSpec((1, HD), lambda i, h: (0, 0)),
                  pl.BlockSpec((tr, HD), lambda i, h: (i, dy_blk0 + h))],
        out_specs=[blk, blk, pl.BlockSpec((1, HD), lambda i, h: (0, 0))],
        out_shape=[jax.ShapeDtypeStruct((t_pad, w), F32), jax.ShapeDtypeStruct((t_pad, w), F32),
                   jax.ShapeDtypeStruct((1, HD), F32)],
        compiler_params=_cparams(),
    )(o, z_arr, g, dy_arr)


def _loss_fwd(y, tgt, first_row, name):
    t_pad = y.shape[0]
    tr = _row_tile(t_pad, D)

    def body(y_ref, t_ref, l_ref, dy_ref):
        rows = pl.program_id(0) * tr + _iota((tr, 1), 0)
        err = jnp.where(rows >= first_row, y_ref[...] - t_ref[...], 0.0)
        dy_ref[...] = err * (1.0 / D)

        @pl.when(pl.program_id(0) == 0)
        def _():
            l_ref[...] = jnp.zeros_like(l_ref)

        part = jnp.sum(jnp.sum(err * err, axis=1, keepdims=True), axis=0, keepdims=True)
        l_ref[...] += jnp.broadcast_to(part * (0.5 / D), l_ref.shape)

    row = pl.BlockSpec((tr, D), lambda i: (i, 0))
    return pl.pallas_call(
        body, name=name, grid=(t_pad // tr,), in_specs=[row, row],
        out_specs=[pl.BlockSpec((8, 128), lambda i: (0, 0)), row],
        out_shape=[jax.ShapeDtypeStruct((8, 128), F32), jax.ShapeDtypeStruct((t_pad, D), F32)],
        compiler_params=_cparams(),
    )(y, tgt)


def _add2(a, b, name):
    t_pad, w = a.shape
    tr = _row_tile(t_pad, w)

    def body(a_ref, b_ref, o_ref):
        o_ref[...] = a_ref[...] + b_ref[...]

    row = pl.BlockSpec((tr, w), lambda i: (i, 0))
    return pl.pallas_call(body, name=name, grid=(t_pad // tr,), in_specs=[row, row], out_specs=row,
                          out_shape=jax.ShapeDtypeStruct((t_pad, w), F32), compiler_params=_cparams())(a, b)


def _assemble_bf16(parts, name):
    t_pad = parts[0][0].shape[0] if parts[0][1] == "cols" else parts[0][0].shape[1]
    widths = [p.shape[1] if kind == "cols" else HD for p, kind in parts]
    total = sum(widths)
    tr = _row_tile(t_pad, total)

    def body(*refs):
        o_ref = refs[-1]
        off = 0
        for ref, (p, kind), w in zip(refs[:-1], parts, widths):
            if kind == "cols":
                o_ref[:, off:off + w] = ref[...].astype(BF16)
            else:
                acc = ref[0]
                for hh in range(1, p.shape[0]):
                    acc = acc + ref[hh]
                o_ref[:, off:off + w] = acc.astype(BF16)
            off += w

    specs = []
    for p, kind in parts:
        if kind == "cols":
            specs.append(pl.BlockSpec((tr, p.shape[1]), lambda i: (i, 0)))
        else:
            specs.append(pl.BlockSpec((p.shape[0], tr, HD), lambda i: (0, i, 0)))
    return pl.pallas_call(
        body, name=name, grid=(t_pad // tr,), in_specs=specs,
        out_specs=pl.BlockSpec((tr, total), lambda i: (i, 0)),
        out_shape=jax.ShapeDtypeStruct((t_pad, total), BF16), compiler_params=_cparams(),
    )(*[p for p, _ in parts])


CONV_K = 4
HALO = 8
RT = 128


def _conv_fwd(p, blk0, w, mode, pad, name):
    t_pad = p.shape[0]
    nt = t_pad // RT
    scale = HD ** -0.5 if mode == "q" else 1.0

    def body(x_ref, w_ref, y_ref, xs_ref):
        xs_ref[0:HALO, :] = jnp.zeros((HALO, HD), F32)
        rows = _iota((t_pad, 1), 0)
        xs_ref[HALO:HALO + t_pad, :] = jnp.where(rows >= pad, x_ref[...], 0.0)
        wv = w_ref[...]

        def tile(i, carry):
            r0 = pl.multiple_of(i * RT, RT)
            ext = xs_ref[pl.ds(r0, RT + HALO), :]
            acc = ext[HALO:, :] * wv[3:4, :]
            for s in (1, 2, 3):
                acc = acc + pltpu.roll(ext, s, 0)[HALO:, :] * wv[3 - s:4 - s, :]
            y = _silu(acc)
            if mode != "v":
                y = y * lax.rsqrt(jnp.sum(y * y, axis=-1, keepdims=True) + L2_EPS) * scale
            y_ref[pl.ds(r0, RT), :] = y
            return carry

        lax.fori_loop(0, nt, tile, 0)

    return pl.pallas_call(
        body, name=name, grid=(GDN_H,),
        in_specs=[pl.BlockSpec((t_pad, HD), lambda h: (0, blk0 + h)), pl.BlockSpec((CONV_K, HD), lambda h: (0, h))],
        out_specs=pl.BlockSpec((t_pad, HD), lambda h: (0, h)),
        out_shape=jax.ShapeDtypeStruct((t_pad, GDN_H * HD), F32),
        scratch_shapes=[pltpu.VMEM((t_pad + HALO, HD), F32)],
        compiler_params=_cparams(),
    )(p, w)


def _conv_bwd(p, blk0, w, dn, mode, pad, name):
    t_pad = p.shape[0]
    nt = t_pad // RT
    scale = HD ** -0.5 if mode == "q" else 1.0

    def body(x_ref, w_ref, dn_ref, dx_ref, dw_ref, xs_ref, ds_ref):
        xs_ref[0:HALO, :] = jnp.zeros((HALO, HD), F32)
        xs_ref[HALO + t_pad:HALO + t_pad + 2 * HALO, :] = jnp.zeros((2 * HALO, HD), F32)
        ds_ref[t_pad:t_pad + HALO, :] = jnp.zeros((HALO, HD), F32)
        rows = _iota((t_pad, 1), 0)
        xs_ref[HALO:HALO + t_pad, :] = jnp.where(rows >= pad, x_ref[...], 0.0)
        ds_ref[0:t_pad, :] = dn_ref[...]
        wv = w_ref[...]

        def tile(i, dw):
            r0 = pl.multiple_of(i * RT, RT)
            ext = xs_ref[pl.ds(r0, RT + 2 * HALO), :]
            dn_e = ds_ref[pl.ds(r0, RT + HALO), :]
            xsh = [ext[HALO:, :]] + [pltpu.roll(ext, s, 0)[HALO:, :] for s in (1, 2, 3)]
            pre = xsh[0] * wv[3:4, :]
            for s in (1, 2, 3):
                pre = pre + xsh[s] * wv[3 - s:4 - s, :]
            sg = _sigmoid(pre)
            y = pre * sg
            if mode != "v":
                ss = jnp.sum(y * y, axis=-1, keepdims=True) + L2_EPS
                r = lax.rsqrt(ss)
                dy = scale * (dn_e * r - y * (r * r * r) * jnp.sum(dn_e * y, axis=-1, keepdims=True))
            else:
                dy = dn_e
            dpre = dy * (sg * (1.0 + pre * (1.0 - sg)))
            dx = dpre[:RT, :] * wv[3:4, :]
            for s in (1, 2, 3):
                dx = dx + pltpu.roll(dpre, RT + HALO - s, 0)[:RT, :] * wv[3 - s:4 - s, :]
            trow = r0 + _iota((RT, 1), 0)
            dx_ref[pl.ds(r0, RT), :] = jnp.where(trow >= pad, dx, 0.0)
            new = []
            for s in (0, 1, 2, 3):
                new.append(dw[s] + jnp.sum(dpre[:RT, :] * xsh[s][:RT, :], axis=0, keepdims=True))
            return tuple(new)

        z = jnp.zeros((1, HD), F32)
        dw = lax.fori_loop(0, nt, tile, (z, z, z, z))
        for s in (0, 1, 2, 3):
            dw_ref[3 - s:4 - s, :] = dw[s]

    return pl.pallas_call(
        body, name=name, grid=(GDN_H,),
        in_specs=[pl.BlockSpec((t_pad, HD), lambda h: (0, blk0 + h)), pl.BlockSpec((CONV_K, HD), lambda h: (0, h)),
                  pl.BlockSpec((t_pad, HD), lambda h: (0, h))],
        out_specs=[pl.BlockSpec((t_pad, HD), lambda h: (0, h)), pl.BlockSpec((CONV_K, HD), lambda h: (0, h))],
        out_shape=[jax.ShapeDtypeStruct((t_pad, GDN_H * HD), F32), jax.ShapeDtypeStruct((CONV_K, GDN_H * HD), F32)],
        scratch_shapes=[pltpu.VMEM((t_pad + 3 * HALO, HD), F32), pltpu.VMEM((t_pad + HALO, HD), F32)],
        compiler_params=_cparams(),
    )(p, w, dn)


@jax.custom_vjp
def _unit_lower_inv(m, bd, eye):
    md = m * bd
    low = m - md
    p2 = mbnn(md, md)
    p4 = mbnn(p2, p2)
    dinv = mbnn(mbnn(eye - md, eye + p2), eye + p4)
    n = mbnn(dinv, low)
    n2 = mbnn(n, n)
    n4 = mbnn(n2, n2)
    return mbnn(mbnn(mbnn(eye - n, eye + n2), eye + n4), dinv)


def _unit_lower_inv_bwd(res, g):
    t, bd, eye = res
    return -mbtn(t, mbnt(g, t)), jnp.zeros_like(bd), jnp.zeros_like(eye)


def _unit_lower_inv_fwd(m, bd, eye):
    t = _unit_lower_inv(m, bd, eye)
    return t, (t, bd, eye)


_unit_lower_inv.defvjp(_unit_lower_inv_fwd, _unit_lower_inv_bwd)


def _gdn_chunk(q, k, v, bb, aa, alog, dtb, s, valid):
    nh = q.shape[0]
    ri = _iota((1, CH, CH), 1)
    ci = _iota((1, CH, CH), 2)
    causal = ri >= ci
    strict = ri > ci
    eye = (ri == ci).astype(F32)
    bd = ((ri >> 3) == (ci >> 3)).astype(F32)
    ltri = (_iota((CH, CH), 0) >= _iota((CH, CH), 1)).astype(F32)
    sel = (_iota((nh, 1, HD), 2) == _iota((nh, 1, HD), 0)).astype(F32)

    beta_all = jnp.where(valid, _sigmoid(bb), 0.0)
    g_all = jnp.where(valid, -jnp.exp(alog) * _softplus(aa + dtb), 0.0)
    gc_all = hnn(ltri, g_all)
    beta = jnp.sum(beta_all[None] * sel, axis=2, keepdims=True)
    gc = jnp.sum(gc_all[None] * sel, axis=2, keepdims=True)
    gc_rows = hbnt(jnp.broadcast_to(sel, (nh, CH, HD)), jnp.broadcast_to(gc_all[None], (nh, CH, HD)))
    last = _iota((1, CH, 1), 1) == CH - 1
    gc_last = jnp.sum(jnp.where(last, gc, 0.0), axis=1, keepdims=True)
    decay = jnp.exp(jnp.where(causal, gc - gc_rows, NEG))
    egc = jnp.exp(gc)

    kb = k * beta
    m = jnp.where(strict, bbnt(kb, k) * decay, 0.0)
    t_inv = _unit_lower_inv(m, bd, eye)
    u = bbnn(t_inv, v * beta)
    w = bbnn(t_inv, kb * egc)
    a_intra = bbnt(q, k) * decay
    q_dec = q * egc
    k_dec = k * jnp.exp(gc_last - gc)
    v_new = u - bbnn(w, s)
    o = bbnn(q_dec, s) + bbnn(a_intra, v_new)
    s_new = s * jnp.exp(gc_last) + bbtn(k_dec, v_new)
    return o, s_new


PAIR = 2 * CH


def _gdn_specs(npair, rev):
    cc = (lambda c: npair - 1 - c) if rev else (lambda c: c)
    wide = pl.BlockSpec((PAIR, GDN_H * HD), lambda c: (cc(c), 0))
    fix = lambda off: pl.BlockSpec((PAIR, HD), lambda c: (cc(c), off))
    par = pl.BlockSpec((1, HD), lambda c: (0, 0))
    state = pl.BlockSpec((1, GDN_H, HD, HD), lambda c: (cc(c), 0, 0, 0))
    return wide, fix, par, state


def _store_heads(ref, a, rows=slice(None)):
    for h in range(a.shape[0]):
        ref[rows, h * HD:(h + 1) * HD] = a[h]


def _chunk_rows(half):
    return slice(half * CH, (half + 1) * CH)


def _chunk_valid(pair, half, pad):
    return ((2 * pair + half) * CH + _iota((CH, 1), 0)) >= pad


def _gdn_fwd(qn, kn, vn, p, alog, dtb, pad, name):
    t_pad = qn.shape[0]
    npair = t_pad // PAIR
    wide, fix, par, state = _gdn_specs(npair, False)

    def body(q_ref, k_ref, v_ref, bb_ref, aa_ref, al_ref, dt_ref, o_ref, ss_ref, s_ref):
        c = pl.program_id(0)

        @pl.when(c == 0)
        def _():
            s_ref[...] = jnp.zeros_like(s_ref)

        s = s_ref[...]
        ss_ref[0] = s
        for half in (0, 1):
            r = _chunk_rows(half)
            o, s = _gdn_chunk(_heads(q_ref[r, :], GDN_H), _heads(k_ref[r, :], GDN_H), _heads(v_ref[r, :], GDN_H),
                              bb_ref[r, :], aa_ref[r, :], al_ref[...], dt_ref[...], s, _chunk_valid(c, half, pad))
            _store_heads(o_ref, o, r)
        s_ref[...] = s

    return pl.pallas_call(
        body, name=name, grid=(npair,),
        in_specs=[wide, wide, wide, fix(16), fix(17), par, par],
        out_specs=[wide, state],
        out_shape=[jax.ShapeDtypeStruct((t_pad, GDN_H * HD), F32), jax.ShapeDtypeStruct((npair, GDN_H, HD, HD), F32)],
        scratch_shapes=[pltpu.VMEM((GDN_H, HD, HD), F32)],
        compiler_params=_cparams(),
    )(qn, kn, vn, p, p, alog, dtb)


def _gdn_bwd(qn, kn, vn, p, alog, dtb, ssave, do, pad, name, cargo=(), exchange=None):
    t_pad = qn.shape[0]
    npair = t_pad // PAIR
    wide, fix, par, state = _gdn_specs(npair, True)
    n = len(cargo)

    def body(q_ref, k_ref, v_ref, bb_ref, aa_ref, al_ref, dt_ref, ss_ref, do_ref, *rest):
        c = pl.program_id(0)
        ds_ref = rest[-1]
        (dq_ref, dk_ref, dv_ref, dbb_ref, daa_ref, dal_ref, ddt_ref), end_cargo = _cargo_bounds(
            rest[:-1], n, 7, exchange, c == 0, c == npair - 1)

        @pl.when(c == 0)
        def _():
            ds_ref[...] = jnp.zeros_like(ds_ref)
            dal_ref[...] = jnp.zeros_like(dal_ref)
            ddt_ref[...] = jnp.zeros_like(ddt_ref)

        ra, rb = _chunk_rows(0), _chunk_rows(1)
        va, vb = _chunk_valid(npair - 1 - c, 0, pad), _chunk_valid(npair - 1 - c, 1, pad)

        def pair(qa, ka, va_, ba, aa, qb, kb, vb_, bb, ab, al, dt, s):
            oa, s = _gdn_chunk(qa, ka, va_, ba, aa, al, dt, s, va)
            ob, s = _gdn_chunk(qb, kb, vb_, bb, ab, al, dt, s, vb)
            return oa, ob, s

        ins = [f(ref[r, :]) for r in (ra, rb)
               for ref, f in ((q_ref, lambda a: _heads(a, GDN_H)), (k_ref, lambda a: _heads(a, GDN_H)),
                              (v_ref, lambda a: _heads(a, GDN_H)), (bb_ref, lambda a: a), (aa_ref, lambda a: a))]
        _, vjp = jax.vjp(pair, *ins, al_ref[...], dt_ref[...], ss_ref[0])
        g = vjp((_heads(do_ref[ra, :], GDN_H), _heads(do_ref[rb, :], GDN_H), ds_ref[...]))
        for r, (dq, dk, dv, dbb, daa) in ((ra, g[0:5]), (rb, g[5:10])):
            _store_heads(dq_ref, dq, r)
            _store_heads(dk_ref, dk, r)
            _store_heads(dv_ref, dv, r)
            dbb_ref[r, :] = dbb
            daa_ref[r, :] = daa
        dal_ref[...] += g[10]
        ddt_ref[...] += g[11]
        ds_ref[...] = g[12]
        end_cargo()

    sds = jax.ShapeDtypeStruct
    return pl.pallas_call(
        body, name=name, grid=(npair,),
        in_specs=[wide, wide, wide, fix(16), fix(17), par, par, state, wide] + [ANY] * n,
        out_specs=[wide, wide, wide, fix(0), fix(0), par, par] + [ANY] * n,
        out_shape=[sds((t_pad, GDN_H * HD), F32)] * 3 + [sds((t_pad, HD), F32)] * 2 + [sds((1, HD), F32)] * 2
        + (exchange[1](cargo) if n else []),
        scratch_shapes=(exchange[2](n) if n else []) + [pltpu.VMEM((GDN_H, HD, HD), F32)],
        compiler_params=_cparams(),
    )(qn, kn, vn, p, p, alog, dtb, ssave, do, *cargo)


SB_Q0, SB_K0, SB_V0 = 18, 22, 26
SB_SCALE = SB_DH ** -0.5
SB_NB = 4


def _sb_terms(z, allowed):
    e = jnp.exp(-jnp.abs(z))
    den = 1.0 + e
    raw = -jnp.maximum(z, 0.0) - jnp.log(den)
    l1m = raw if allowed is None else jnp.where(allowed, raw, 0.0)
    return l1m, z + raw, jnp.where(z >= 0.0, 1.0, e) / den


def _sb_passes(i, first_pass, inner_pass, last_pass, carry):
    n_pass = (i + SB_NB) // SB_NB
    carry = first_pass(0, carry)
    carry = lax.fori_loop(1, n_pass - 1, inner_pass, carry)
    return lax.cond(n_pass > 1, lambda c: last_pass(n_pass - 1, c), lambda c: c, carry)


def _sb_stack(a, i):
    first = _iota((1, HD), 1) < SB_DH
    a2 = jnp.concatenate([jnp.where(first, a, 0.0), jnp.where(first, 0.0, a)], axis=0).astype(BF16)
    rq = i * QB + _iota((QB, 1), 0)
    return a2, jnp.concatenate([rq, rq], axis=0), first


def _hi_lo(a):
    hi = a.astype(BF16)
    lo = (a - hi.astype(F32)).astype(BF16)
    return jnp.concatenate([hi, lo], axis=1)


def _cargo_bounds(refs, n, n_out, exchange, first, last):
    outs = refs[n:n + n_out]
    if not n:
        return outs, lambda: None
    ex = exchange[0](refs[:n], refs[n + n_out:2 * n + n_out], *refs[2 * n + n_out:])

    @pl.when(first)
    def _():
        ex.start()

    def finish():
        @pl.when(last)
        def _():
            ex.wait()

    return outs, finish


def _sb_fwd(p, pad, name, cargo=(), exchange=None):
    t_pad = p.shape[0]
    nq = t_pad // QB
    n = len(cargo)

    def body(q_ref, k_ref, v_ref, *rest):
        i = pl.program_id(1)
        pr = pl.program_id(0)
        (o_ref, r_ref), end_cargo = _cargo_bounds(rest, n, 2, exchange, (pr == 0) & (i == 0),
                                                  (pr == SB_H // 2 - 1) & (i == nq - 1))
        q2, rowq, first = _sb_stack(q_ref[...] * SB_SCALE, i)
        tri = (_iota((QB, QB), 0) > _iota((QB, QB), 1)).astype(BF16)
        upper2 = jnp.concatenate([tri, tri], axis=0)

        def chain(kb, masked):
            live = kb >= 0
            kb = jnp.maximum(kb, 0)
            start = pl.multiple_of(kb * QB, QB)
            kblk = k_ref[pl.ds(start, QB), :].astype(BF16)
            vblk = v_ref[pl.ds(start, QB), :].astype(BF16)
            z = lax.dot_general(q2, kblk, (NT, ((), ())), preferred_element_type=F32)
            colk = kb * QB + _iota((1, QB), 1)
            allowed = (lambda rq: (colk < rq) & (colk >= pad) & live) if masked else (lambda rq: None)

            def terms(z_s, rq_s):
                l1m, ls, _ = _sb_terms(z_s, allowed(rq_s))
                return _hi_lo(l1m), ls, jnp.sum(l1m, axis=1, keepdims=True)

            cat, ls, rs = terms(z, rowq)
            suf = lax.dot_general(cat, upper2, (NN, ((), ())), preferred_element_type=F32)
            return allowed, ls, suf, rs, vblk

        def step(j, carry, masked):
            o_acc, run = carry
            ws, vs = [], []
            for n in range(SB_NB):
                allowed, ls, suf, rs, vblk = chain(i - SB_NB * j - n, masked)

                def weights(ls_s, suf_s, run_s, rq_s):
                    wgt = jnp.exp(ls_s + suf_s + run_s)
                    al = allowed(rq_s)
                    return ((wgt if al is None else jnp.where(al, wgt, 0.0)).astype(BF16),)

                ws.append(weights(ls, suf, run, rowq)[0])
                vs.append(vblk)
                run = run + rs
            o_acc = o_acc + lax.dot_general(jnp.concatenate(ws, axis=1), jnp.concatenate(vs, axis=0),
                                            (NN, ((), ())), preferred_element_type=F32)
            return o_acc, run

        edge = lambda j, c: step(j, c, True)
        o_acc, run = _sb_passes(i, edge, lambda j, c: step(j, c, False), edge,
                                (jnp.zeros((2 * QB, HD), F32), jnp.zeros((2 * QB, 1), F32)))
        o_ref[...] = jnp.where(first, o_acc[:QB], o_acc[QB:]).astype(BF16)
        r_ref[...] = jnp.where(first, run[:QB], run[QB:])
        end_cargo()

    full = lambda off: pl.BlockSpec((t_pad, HD), lambda pr, i: (0, off + pr))
    blk = pl.BlockSpec((QB, HD), lambda pr, i: (i, pr))
    return pl.pallas_call(
        body, name=name, grid=(SB_H // 2, nq),
        in_specs=[pl.BlockSpec((QB, HD), lambda pr, i: (i, SB_Q0 + pr)), full(SB_K0), full(SB_V0)] + [ANY] * n,
        out_specs=[blk, blk] + [ANY] * n,
        out_shape=[jax.ShapeDtypeStruct((t_pad, SB_H * SB_DH), BF16), jax.ShapeDtypeStruct((t_pad, SB_H * SB_DH), F32)]
        + (exchange[1](cargo) if n else []),
        scratch_shapes=exchange[2](n) if n else [],
        compiler_params=_cparams(),
    )(p, p, p, *cargo)


def _sb_bwd(p, rtot, dy, dy_blk0, pad, name, cargo=(), exchange=None):
    t_pad = p.shape[0]
    nq = t_pad // QB
    n = len(cargo)

    def body(q_ref, k_ref, v_ref, r_ref, do_ref, *rest):
        i = pl.program_id(1)
        pr = pl.program_id(0)
        (dq_ref, dk_ref, dv_ref), end_cargo = _cargo_bounds(rest, n, 3, exchange, (pr == 0) & (i == 0),
                                                            (pr == SB_H // 2 - 1) & (i == nq - 1))

        @pl.when(i == 0)
        def _():
            dk_ref[...] = jnp.zeros_like(dk_ref)
            dv_ref[...] = jnp.zeros_like(dv_ref)

        q2, rowq, first = _sb_stack(q_ref[...] * SB_SCALE, i)
        do2, _, _ = _sb_stack(do_ref[...], i)
        rt = r_ref[...]
        lane = _iota((1, HD), 1)
        rcol = jnp.concatenate([jnp.sum(jnp.where(lane == 0, rt, 0.0), axis=1, keepdims=True),
                                jnp.sum(jnp.where(lane == SB_DH, rt, 0.0), axis=1, keepdims=True)], axis=0)
        rj = _iota((QB, QB), 0)
        cs = _iota((QB, QB), 1)
        tri_u = (rj > cs).astype(BF16)
        tri_l = (rj < cs).astype(BF16)
        upper2 = jnp.concatenate([tri_u, tri_u], axis=0)
        lower2 = jnp.concatenate([tri_l, tri_l], axis=0)

        def chain(kb, masked):
            live = kb <= i
            kb = jnp.minimum(kb, i)
            start = pl.multiple_of(kb * QB, QB)
            kblk = k_ref[pl.ds(start, QB), :].astype(BF16)
            vblk = v_ref[pl.ds(start, QB), :].astype(BF16)
            z = lax.dot_general(q2, kblk, (NT, ((), ())), preferred_element_type=F32)
            colk = kb * QB + _iota((1, QB), 1)
            allowed = (lambda rq: (colk < rq) & (colk >= pad) & live) if masked else (lambda rq: None)

            def terms(z_s, rq_s):
                l1m, ls, sg = _sb_terms(z_s, allowed(rq_s))
                return _hi_lo(l1m), ls, sg, jnp.sum(l1m, axis=1, keepdims=True)

            cat, ls, sg, rs = terms(z, rowq)
            suf = lax.dot_general(cat, upper2, (NN, ((), ())), preferred_element_type=F32)
            dwgt = lax.dot_general(do2, vblk, (NT, ((), ())), preferred_element_type=F32)
            return start, kblk, allowed, ls, suf, rs, dwgt, sg

        def finish(c, seen, gseen):
            start, kblk, allowed, ls, suf, rs, dwgt, sg = c

            def weights(ls_s, suf_s, base_s, dwgt_s, rq_s):
                wgt = jnp.exp(ls_s + suf_s + base_s)
                al = allowed(rq_s)
                if al is not None:
                    wgt = jnp.where(al, wgt, 0.0)
                dl = dwgt_s * wgt
                return wgt.astype(BF16), dl, _hi_lo(dl), jnp.sum(dl, axis=1, keepdims=True)

            w16, dl, cat, rsd = weights(ls, suf, rcol - seen - rs, dwgt, rowq)
            gpre = lax.dot_general(cat, lower2, (NN, ((), ())), preferred_element_type=F32)

            def logits(dl_s, sg_s, gpre_s, gseen_s, rq_s):
                dz = dl_s - sg_s * (dl_s + (gseen_s + gpre_s))
                al = allowed(rq_s)
                return ((dz if al is None else jnp.where(al, dz, 0.0)).astype(BF16),)

            dz, = logits(dl, sg, gpre, gseen, rowq)
            dk_ref[pl.ds(start, QB), :] += lax.dot_general(dz, q2, (TN, ((), ())), preferred_element_type=F32)
            dv_ref[pl.ds(start, QB), :] += lax.dot_general(w16, do2, (TN, ((), ())), preferred_element_type=F32)
            return dz, seen + rs, gseen + rsd

        def step(j, carry, masked):
            dq_acc, seen, gseen = carry
            cs_ = [chain(SB_NB * j + n, masked) for n in range(SB_NB)]
            dzs = []
            for c in cs_:
                dz, seen, gseen = finish(c, seen, gseen)
                dzs.append(dz)
            dq_acc = dq_acc + lax.dot_general(jnp.concatenate(dzs, axis=1), jnp.concatenate([c[1] for c in cs_], axis=0),
                                              (NN, ((), ())), preferred_element_type=F32)
            return dq_acc, seen, gseen

        zc = jnp.zeros((2 * QB, 1), F32)
        edge = lambda j, c: step(j, c, True)
        dq_acc, _, _ = _sb_passes(i, edge, lambda j, c: step(j, c, False), edge, (jnp.zeros((2 * QB, HD), F32), zc, zc))
        dq_ref[...] = jnp.where(first, dq_acc[:QB], dq_acc[QB:]) * SB_SCALE
        end_cargo()

    full_in = lambda off: pl.BlockSpec((t_pad, HD), lambda pr, i: (0, off + pr))
    full_out = pl.BlockSpec((t_pad, HD), lambda pr, i: (0, pr))
    blk = pl.BlockSpec((QB, HD), lambda pr, i: (i, pr))
    sds = jax.ShapeDtypeStruct((t_pad, SB_H * SB_DH), F32)
    return pl.pallas_call(
        body, name=name, grid=(SB_H // 2, nq),
        in_specs=[pl.BlockSpec((QB, HD), lambda pr, i: (i, SB_Q0 + pr)), full_in(SB_K0), full_in(SB_V0), blk,
                  pl.BlockSpec((QB, HD), lambda pr, i: (i, dy_blk0 + pr))] + [ANY] * n,
        out_specs=[blk, full_out, full_out] + [ANY] * n,
        out_shape=[sds, sds, sds] + (exchange[1](cargo) if n else []),
        scratch_shapes=exchange[2](n) if n else [],
        compiler_params=_cparams(),
    )(p, p, p, rtot, dy, *cargo)


HG_LEVELS = 6


def _hg_prefix_matrix():
    t = np.arange(CH)[:, None]
    j = np.arange(CH)[None, :]
    groups = [(j <= t)]
    for lvl in range(1, HG_LEVELS + 1):
        half = CH >> lvl
        e = (t // (2 * half)) * (2 * half) + half - 1
        groups.append(j <= e)
    groups.append(np.ones((8, CH), bool))
    e = np.concatenate(groups, axis=0).astype(np.float32)
    return np.concatenate([e, e, e], axis=1), np.concatenate([e, e, e], axis=0)


HG_G = 4


def _hg_chunk(qr, fr, iv, r0, r1, st, valid, ecat):
    g = st.shape[0]
    mx = jnp.maximum(r0, r1)
    e0 = jnp.exp(r0 - mx)
    e1 = jnp.exp(r1 - mx)
    lb = e1 / (e0 + e1)
    fg = lb + (1.0 - lb) * _sigmoid(fr)
    logf = jnp.where(valid, jnp.log(fg), 0.0)
    kk = jnp.where(valid, 1.0 - fg, 0.0)
    q = jnp.where(valid, _silu(qr), 0.0)
    v = _heads(jnp.where(valid, iv, 0.0), g)

    pre = _mask_dot(ecat, logf)
    b = pre[0:CH]
    b_last = jnp.max(pre[(HG_LEVELS + 1) * CH:], axis=0, keepdims=True)
    row = _iota((CH, 1), 0)
    ri = _iota((1, CH, CH), 1)
    ci = _iota((1, CH, CH), 2)
    a = jnp.where(ri == ci, jnp.sum(_heads(q * kk, g), axis=2, keepdims=True), 0.0)
    for lvl in range(1, HG_LEVELS + 1):
        half = CH >> lvl
        m = pre[lvl * CH:(lvl + 1) * CH]
        low = (row & half) != 0
        dec = jnp.exp(jnp.where(low, b - m, m - b))
        qt = jnp.where(low, q * dec, 0.0)
        kt = jnp.where(low, 0.0, kk * dec)
        same = (ri >> (7 - lvl)) == (ci >> (7 - lvl))
        a = a + jnp.where(same, bbnt(_heads(qt, g), _heads(kt, g)), 0.0)
    o = bbnt(_heads(q * jnp.exp(b), g), st) + bbnn(a, v)
    kd = kk * jnp.exp(b_last - b)
    st_new = st * _heads(jnp.exp(b_last), g) + bbtn(v, _heads(kd, g))
    return o, st_new


def _hg_specs(npair, rev):
    cc = (lambda c: npair - 1 - c) if rev else (lambda c: c)
    ng = HG_H // HG_G
    blk = lambda off: pl.BlockSpec((PAIR, HG_G * HD), lambda h, c: (cc(c), off * ng + h))
    lbs = pl.BlockSpec((2, HG_G * HD), lambda h, c: (0, h))
    state = pl.BlockSpec((1, HG_G, HD, HD), lambda h, c: (cc(c), h, 0, 0))
    return ng, blk, lbs, state


def _hg_fwd(p, lbraw, ecat, pad, name):
    t_pad = p.shape[0]
    npair = t_pad // PAIR
    ng, blk, lbs, state = _hg_specs(npair, False)

    def body(q_ref, f_ref, i_ref, lb_ref, e_ref, et_ref, o_ref, ss_ref, s_ref):
        c = pl.program_id(1)

        @pl.when(c == 0)
        def _():
            s_ref[...] = jnp.zeros_like(s_ref)

        st = s_ref[...]
        ss_ref[0] = st
        for half in (0, 1):
            r = _chunk_rows(half)
            o, st = _hg_chunk(q_ref[r, :], f_ref[r, :], i_ref[r, :], lb_ref[0:1, :], lb_ref[1:2, :], st,
                              _chunk_valid(c, half, pad), (e_ref[...], et_ref[...]))
            _store_heads(o_ref, o, r)
        s_ref[...] = st

    return pl.pallas_call(
        body, name=name, grid=(ng, npair),
        in_specs=[blk(0), blk(1), blk(2), lbs] + [pl.BlockSpec(e.shape, lambda h, c: (0, 0)) for e in ecat],
        out_specs=[blk(0), state],
        out_shape=[jax.ShapeDtypeStruct((t_pad, HG_H * HD), F32), jax.ShapeDtypeStruct((npair, HG_H, HD, HD), F32)],
        scratch_shapes=[pltpu.VMEM((HG_G, HD, HD), F32)],
        compiler_params=_cparams(),
    )(p, p, p, lbraw, *ecat)


def _hg_bwd(p, lbraw, ecat, ssave, do, pad, name, cargo=(), exchange=None):
    t_pad = p.shape[0]
    npair = t_pad // PAIR
    ng, blk, lbs, state = _hg_specs(npair, True)
    n = len(cargo)

    def body(q_ref, f_ref, i_ref, lb_ref, e_ref, et_ref, ss_ref, do_ref, *rest):
        c = pl.program_id(1)
        hg = pl.program_id(0)
        ds_ref = rest[-1]
        (dq_ref, df_ref, di_ref, dlb_ref), end_cargo = _cargo_bounds(
            rest[:-1], n, 4, exchange, (hg == 0) & (c == 0), (hg == ng - 1) & (c == npair - 1))

        @pl.when(c == 0)
        def _():
            ds_ref[...] = jnp.zeros_like(ds_ref)
            dlb_ref[...] = jnp.zeros_like(dlb_ref)

        ra, rb = _chunk_rows(0), _chunk_rows(1)
        va, vb = _chunk_valid(npair - 1 - c, 0, pad), _chunk_valid(npair - 1 - c, 1, pad)
        ecv = (e_ref[...], et_ref[...])

        def pair(qa, fa, ia, qb, fb, ib, r0, r1, st):
            oa, st = _hg_chunk(qa, fa, ia, r0, r1, st, va, ecv)
            ob, st = _hg_chunk(qb, fb, ib, r0, r1, st, vb, ecv)
            return oa, ob, st

        ins = [ref[r, :] for r in (ra, rb) for ref in (q_ref, f_ref, i_ref)]
        _, vjp = jax.vjp(pair, *ins, lb_ref[0:1, :], lb_ref[1:2, :], ss_ref[0])
        g = vjp((_heads(do_ref[ra, :], HG_G), _heads(do_ref[rb, :], HG_G), ds_ref[...]))
        for r, (dq, df, di) in ((ra, g[0:3]), (rb, g[3:6])):
            dq_ref[r, :] = dq
            df_ref[r, :] = df
            di_ref[r, :] = di
        dlb_ref[0:1, :] += g[6]
        dlb_ref[1:2, :] += g[7]
        ds_ref[...] = g[8]
        end_cargo()

    sds = jax.ShapeDtypeStruct((t_pad, HG_H * HD), F32)
    return pl.pallas_call(
        body, name=name, grid=(ng, npair),
        in_specs=[blk(0), blk(1), blk(2), lbs] + [pl.BlockSpec(e.shape, lambda h, c: (0, 0)) for e in ecat]
        + [state, blk(0)] + [ANY] * n,
        out_specs=[blk(0), blk(0), blk(0), lbs] + [ANY] * n,
        out_shape=[sds, sds, sds, jax.ShapeDtypeStruct((2, HG_H * HD), F32)] + (exchange[1](cargo) if n else []),
        scratch_shapes=(exchange[2](n) if n else []) + [pltpu.VMEM((HG_G, HD, HD), F32)],
        compiler_params=_cparams(),
    )(p, p, p, lbraw, *ecat, ssave, do, *cargo)


def _pad_ab_cols(w):
    z = jnp.zeros((w.shape[0], HD - GDN_H), w.dtype)
    return jnp.concatenate([w[:, :2048], w[:, 2048:2052], z, w[:, 2052:2056], z, w[:, 2056:]], axis=1)


def _unpad_ab_cols(w):
    return jnp.concatenate([w[:, :2048], w[:, 2048:2052], w[:, 2176:2180], w[:, 2304:]], axis=1)


def _lane_pad(v):
    return jnp.pad(v, ((0, 0), (0, HD - v.shape[1])))


def _mlp_fwd(hb, w1, w2, layer):
    a, r = _mm(hb, w1, b_view=("cols", layer), out_dtype=BF16, act=True, name=f"mlp_up_{layer}")
    m = _mm(r, w2, b_view=("rows", layer), name=f"mlp_down_{layer}")
    return a, r, m


def _mlp_bwd(hb, a, r, dmb, w1, w2, layer):
    da = _mm(dmb, w2, tb=True, b_view=("rows", layer), out_dtype=BF16, gate=a, name=f"mlp_down_dx_{layer}")
    dw2 = _mm(r, dmb, ta=True, out_dtype=BF16, name=f"mlp_down_dw_{layer}")
    dh = _mm(da, w1, tb=True, b_view=("cols", layer), name=f"mlp_up_dx_{layer}")
    dw1 = _mm(hb, da, ta=True, out_dtype=BF16, out_split=N_CHIP, name=f"mlp_up_dw_{layer}")
    return dh, dw1, dw2


def _local_step(h0, tgt, w, pad, late=None):
    row = lambda a, i: a[i:i + 1]
    ecat = tuple(jnp.asarray(e, dtype=BF16) for e in _hg_prefix_matrix())
    cw = [w["conv_w"][:, i * 512:(i + 1) * 512] for i in range(3)]
    alog, dtb = _lane_pad(w["a_log"]), _lane_pad(w["dt_bias"])

    h0b = h0.astype(BF16)
    p0 = _mm(h0b, w["ab_w_in"], name="ab_in")
    qn = _conv_fwd(p0, 0, cw[0], "q", pad, "conv_q")
    kn = _conv_fwd(p0, 4, cw[1], "k", pad, "conv_k")
    vn = _conv_fwd(p0, 8, cw[2], "v", pad, "conv_v")
    oa_raw, ss0 = _gdn_fwd(qn, kn, vn, p0, alog, dtb, pad, "gdn_fwd")
    oa = _grms_fwd(oa_raw, p0, 12, w["ab_gnorm_g"], "gdn_gate")
    if late is None:
        ob, rtot = _sb_fwd(p0, pad, "sb_fwd")
    else:
        ob, rtot, g_about, g_cin, g_cout, g_w1, g_w2 = _sb_fwd(p0, pad, "sb_fwd", cargo=late, exchange=GATHER)
        w = dict(w, ab_w_out=g_about.reshape(D, D), c_w_in=g_cin, c_w_out=g_cout.reshape(D, D), mlp_w1=g_w1, mlp_w2=g_w2)
    ycat = jnp.concatenate([oa, ob], axis=1)
    mix0 = _mm(ycat, w["ab_w_out"], name="ab_out")
    h1, h1b = _ln_res_fwd(h0, mix0, row(w["ln_mix_g"], 0), row(w["ln_mix_b"], 0), "ln_mix_0")
    a0, r0, m0 = _mlp_fwd(h1b, w["mlp_w1"], w["mlp_w2"], 0)
    h2, h2b = _ln_res_fwd(h1, m0, row(w["ln_ffn_g"], 0), row(w["ln_ffn_b"], 0), "ln_ffn_0")
    p1 = _mm(h2b, w["c_w_in"], b_view=("cols", 0), name="c_in")
    oc_raw, ss1 = _hg_fwd(p1, w["c_lb_raw"], ecat, pad, "hg_fwd")
    yc = _grms_fwd(oc_raw, p1, 3 * HG_H, w["c_gnorm_g"], "hg_gate")
    mix1 = _mm(yc, w["c_w_out"], name="c_out")
    h3, h3b = _ln_res_fwd(h2, mix1, row(w["ln_mix_g"], 1), row(w["ln_mix_b"], 1), "ln_mix_1")
    a1, r1, m1 = _mlp_fwd(h3b, w["mlp_w1"], w["mlp_w2"], 1)
    h4, _ = _ln_res_fwd(h3, m1, row(w["ln_ffn_g"], 1), row(w["ln_ffn_b"], 1), "ln_ffn_1")
    loss, dh4 = _loss_fwd(h4, tgt, pad + N_META, "loss")

    zero = jnp.zeros_like(dh4)
    dh3a, dm1b, dfg1, dfb1 = _ln_res_bwd(h3, m1, row(w["ln_ffn_g"], 1), row(w["ln_ffn_b"], 1), dh4, zero, "ln_ffn_bwd_1")
    dh3b, dw1_1, dw2_1 = _mlp_bwd(h3b, a1, r1, dm1b, w["mlp_w1"], w["mlp_w2"], 1)
    dh2a, dmix1b, dmg1, dmb1 = _ln_res_bwd(h2, mix1, row(w["ln_mix_g"], 1), row(w["ln_mix_b"], 1), dh3a, dh3b, "ln_mix_bwd_1")
    dyc = _mm(dmix1b, w["c_w_out"], tb=True, name="c_out_dx")
    dwco = _mm(yc, dmix1b, ta=True, out_dtype=BF16, name="c_out_dw")
    doc, dzc, dcg = _grms_bwd(oc_raw, p1, 3 * HG_H, w["c_gnorm_g"], dyc, 0, "hg_gate_bwd")
    landed = {}
    rows4 = lambda a: a.reshape(N_CHIP, -1, D)
    if late is None:
        dq1, df1, di1, dlb = _hg_bwd(p1, w["c_lb_raw"], ecat, ss1, doc, pad, "hg_bwd")
    else:
        dq1, df1, di1, dlb, landed["w1_1"], landed["w2_1"], landed["c_w_out"] = _hg_bwd(
            p1, w["c_lb_raw"], ecat, ss1, doc, pad, "hg_bwd", cargo=[dw1_1, rows4(dw2_1), rows4(dwco)], exchange=SCATTER)
    dp1 = _assemble_bf16([(dq1, "cols"), (df1, "cols"), (di1, "cols"), (dzc, "cols")], "c_in_dy")
    dh2b = _mm(dp1, w["c_w_in"], tb=True, b_view=("cols", 0), name="c_in_dx")
    dwc = _mm(h2b, dp1, ta=True, out_dtype=BF16, out_split=N_CHIP, name="c_in_dw")
    dh1a, dm0b, dfg0, dfb0 = _ln_res_bwd(h1, m0, row(w["ln_ffn_g"], 0), row(w["ln_ffn_b"], 0), dh2a, dh2b, "ln_ffn_bwd_0")
    dh1b, dw1_0, dw2_0 = _mlp_bwd(h1b, a0, r0, dm0b, w["mlp_w1"], w["mlp_w2"], 0)
    dh0a, dmix0b, dmg0, dmb0 = _ln_res_bwd(h0, mix0, row(w["ln_mix_g"], 0), row(w["ln_mix_b"], 0), dh1a, dh1b, "ln_mix_bwd_0")
    dycat = _mm(dmix0b, w["ab_w_out"], tb=True, name="ab_out_dx")
    dwabo = _mm(ycat, dmix0b, ta=True, out_dtype=BF16, name="ab_out_dw")
    doa, dza, dag = _grms_bwd(oa_raw, p0, 12, w["ab_gnorm_g"], dycat, 0, "gdn_gate_bwd")
    if late is None:
        dqn, dkn, dvn, dbb, daa, dal, ddt = _gdn_bwd(qn, kn, vn, p0, alog, dtb, ss0, doa, pad, "gdn_bwd")
        dqb, dkb, dvb = _sb_bwd(p0, rtot, dycat, 4, pad, "sb_bwd")
    else:
        dqn, dkn, dvn, dbb, daa, dal, ddt, landed["c_w_in"], landed["w2_0"] = _gdn_bwd(
            qn, kn, vn, p0, alog, dtb, ss0, doa, pad, "gdn_bwd", cargo=[dwc, rows4(dw2_0)], exchange=SCATTER)
        dqb, dkb, dvb, landed["w1_0"], landed["ab_w_out"] = _sb_bwd(
            p0, rtot, dycat, 4, pad, "sb_bwd", cargo=[dw1_0, rows4(dwabo)], exchange=SCATTER)
    dpq, dcq = _conv_bwd(p0, 0, cw[0], dqn, "q", pad, "conv_q_bwd")
    dpk, dck = _conv_bwd(p0, 4, cw[1], dkn, "k", pad, "conv_k_bwd")
    dpv, dcv = _conv_bwd(p0, 8, cw[2], dvn, "v", pad, "conv_v_bwd")
    dp0 = _assemble_bf16([(dpq, "cols"), (dpk, "cols"), (dpv, "cols"), (dza, "cols"), (dbb, "cols"), (daa, "cols"),
                          (dqb, "cols"), (dkb, "cols"), (dvb, "cols")], "ab_in_dy")
    dwab = _mm(h0b, dp0, ta=True, out_dtype=BF16, name="ab_in_dw")
    if late is None:
        dh0b = _mm(dp0, w["ab_w_in"], tb=True, name="ab_in_dx")
    else:
        dab = jnp.transpose(_unpad_ab_cols(dwab).reshape(D, N_CHIP, AB_TRUE // N_CHIP), (1, 0, 2))
        dh0b, landed["ab_w_in"] = _mm(dp0, w["ab_w_in"], tb=True, name="ab_in_dx", cargo=[dab], exchange=SCATTER)
    dh0 = _add2(dh0a, dh0b, "dh0")

    grads = {
        "ab_w_in": dwab, "conv_w": jnp.concatenate([dcq, dck, dcv], axis=1),
        "a_log": dal[:, :GDN_H], "dt_bias": ddt[:, :GDN_H],
        "ab_gnorm_g": dag, "ab_w_out": dwabo, "c_w_in": dwc, "c_lb_raw": dlb, "c_gnorm_g": dcg, "c_w_out": dwco,
        "ln_mix_g": jnp.concatenate([dmg0, dmg1], 0), "ln_mix_b": jnp.concatenate([dmb0, dmb1], 0),
        "w1_0": dw1_0, "w1_1": dw1_1, "w2_0": dw2_0, "w2_1": dw2_1,
        "ln_ffn_g": jnp.concatenate([dfg0, dfg1], 0), "ln_ffn_b": jnp.concatenate([dfb0, dfb1], 0),
        "landed": landed,
    }
    return loss, dh0, grads


MESH = pl.DeviceIdType.MESH
ANY = pl.BlockSpec(memory_space=pl.ANY)
N_CHIP = 4
N_DEV = 8
CHIP_REL = ((1, 0), (0, 1), (1, 1))
DEV_REL = tuple((dx, dy, dc) for dx in (0, 1) for dy in (0, 1) for dc in (0, 1))[1:]

def _pos():
    return lax.axis_index("x"), lax.axis_index("y"), lax.axis_index("c")


def _flip(a, d):
    return a + d - 2 * a * d


class _Exchange:
    def __init__(self, local, sends, recvs):
        self.local, self.sends, self.recvs = local, sends, recvs

    def start(self):
        for cp in self.local + self.sends:
            cp.start()

    def wait(self):
        for cp in self.recvs:
            cp.wait_recv()
        for cp in self.sends:
            cp.wait_send()
        for cp in self.local:
            cp.wait()


def _gather_sems(n):
    return [pltpu.SemaphoreType.DMA((3 * n,)), pltpu.SemaphoreType.DMA((3 * n,)), pltpu.SemaphoreType.DMA((n,))]


def _gather_copies(x_refs, o_refs, send_sems, recv_sems, local_sems):
    n = len(x_refs)
    x, y, c = _pos()
    local = [pltpu.make_async_copy(x_refs[a], o_refs[a].at[2 * x + y], local_sems.at[a]) for a in range(n)]

    def copy(a, k, sending):
        tx, ty = _flip(x, CHIP_REL[k][0]), _flip(y, CHIP_REL[k][1])
        return pltpu.make_async_remote_copy(
            src_ref=x_refs[a], dst_ref=o_refs[a].at[2 * x + y if sending else 2 * tx + ty],
            send_sem=send_sems.at[3 * a + k], recv_sem=recv_sems.at[3 * a + k], device_id=(tx, ty, c), device_id_type=MESH)

    pairs = [(a, k) for a in range(n) for k in range(3)]
    return _Exchange(local, [copy(a, k, True) for a, k in pairs], [copy(a, k, False) for a, k in pairs])


def _gather_shapes(bufs):
    return [jax.ShapeDtypeStruct((N_CHIP,) + b.shape, b.dtype) for b in bufs]


def _chip_allgather(bufs, name):
    n = len(bufs)

    def body(*refs):
        ex = _gather_copies(refs[:n], refs[n:2 * n], *refs[2 * n:])
        ex.start()
        ex.wait()

    return pl.pallas_call(
        body, name=name, in_specs=[ANY] * n, out_specs=[ANY] * n, out_shape=_gather_shapes(bufs),
        scratch_shapes=_gather_sems(n), compiler_params=pltpu.CompilerParams(has_side_effects=True),
    )(*bufs)


def _scatter_sems(n):
    nr = N_DEV - 1
    return [pltpu.SemaphoreType.DMA((nr * n,)), pltpu.SemaphoreType.DMA((nr * n,)), pltpu.SemaphoreType.DMA((n,))]


def _scatter_copies(g_refs, o_refs, send_sems, recv_sems, local_sems):
    n = len(g_refs)
    nr = N_DEV - 1
    x, y, c = _pos()
    me = 4 * x + 2 * y + c
    local = [pltpu.make_async_copy(g_refs[a].at[2 * x + y], o_refs[a].at[me], local_sems.at[a]) for a in range(n)]

    def copy(a, k, sending):
        dx, dy, dc = DEV_REL[k]
        tx, ty, tc = _flip(x, dx), _flip(y, dy), _flip(c, dc)
        return pltpu.make_async_remote_copy(
            src_ref=g_refs[a].at[2 * tx + ty], dst_ref=o_refs[a].at[me if sending else 4 * tx + 2 * ty + tc],
            send_sem=send_sems.at[nr * a + k], recv_sem=recv_sems.at[nr * a + k],
            device_id=(tx, ty, tc), device_id_type=MESH)

    pairs = [(a, k) for a in range(n) for k in range(nr)]
    return _Exchange(local, [copy(a, k, True) for a, k in pairs], [copy(a, k, False) for a, k in pairs])


def _scatter_shapes(gs):
    return [jax.ShapeDtypeStruct((N_DEV,) + g.shape[1:], g.dtype) for g in gs]


GATHER = (_gather_copies, _gather_shapes, _gather_sems)
SCATTER = (_scatter_copies, _scatter_shapes, _scatter_sems)


def _sum_slots(r, name):
    n, rh, w = r.shape
    tr = _pick(rh, (256, 128, 64, 16))

    def body(r_ref, o_ref):
        acc = r_ref[0].astype(F32)
        for s in range(1, n):
            acc = acc + r_ref[s].astype(F32)
        o_ref[...] = acc

    return pl.pallas_call(
        body, name=name, grid=(rh // tr,), in_specs=[pl.BlockSpec((n, tr, w), lambda i: (0, i, 0))],
        out_specs=pl.BlockSpec((tr, w), lambda i: (i, 0)), out_shape=jax.ShapeDtypeStruct((rh, w), F32),
        compiler_params=_cparams(),
    )(r)


def _small_allreduce(buf, name):
    r, w = buf.shape

    def body(b_ref, o_ref, land_ref, send_sems, recv_sems):
        x, y, c = _pos()
        me = 4 * x + 2 * y + c
        land_ref[me] = b_ref[...]

        def target(k):
            dx, dy, dc = DEV_REL[k]
            return _flip(x, dx), _flip(y, dy), _flip(c, dc)

        sends = []
        for k in range(N_DEV - 1):
            tx, ty, tc = target(k)
            cp = pltpu.make_async_remote_copy(
                src_ref=b_ref, dst_ref=land_ref.at[me], send_sem=send_sems.at[k], recv_sem=recv_sems.at[k],
                device_id=(tx, ty, tc), device_id_type=MESH)
            cp.start()
            sends.append(cp)
        for k in range(N_DEV - 1):
            tx, ty, tc = target(k)
            pltpu.make_async_remote_copy(
                src_ref=b_ref, dst_ref=land_ref.at[4 * tx + 2 * ty + tc], send_sem=send_sems.at[k],
                recv_sem=recv_sems.at[k], device_id=(tx, ty, tc), device_id_type=MESH).wait_recv()
        for cp in sends:
            cp.wait_send()
        acc = land_ref[0]
        for s in range(1, N_DEV):
            acc = acc + land_ref[s]
        o_ref[...] = acc

    vm = pl.BlockSpec(memory_space=pltpu.VMEM)
    return pl.pallas_call(
        body, name=name, in_specs=[vm], out_specs=vm, out_shape=jax.ShapeDtypeStruct((r, w), F32),
        scratch_shapes=[pltpu.VMEM((N_DEV, r, w), F32), pltpu.SemaphoreType.DMA((N_DEV - 1,)),
                        pltpu.SemaphoreType.DMA((N_DEV - 1,))],
        compiler_params=pltpu.CompilerParams(has_side_effects=True),
    )(buf)


def _adamw(w, g, m, v, name):
    r, c = w.shape
    tr = _pick(r, (256, 128, 64, 8)) if r * c > (1 << 18) else r

    def body(w_ref, g_ref, m_ref, v_ref, d_ref, m2_ref, v2_ref):
        gg = g_ref[...]
        m2 = ADAM_B1 * m_ref[...] + (1.0 - ADAM_B1) * gg
        v2 = ADAM_B2 * v_ref[...] + (1.0 - ADAM_B2) * (gg * gg)
        m_hat = m2 / (1.0 - ADAM_B1 ** ADAM_STEP)
        v_hat = v2 / (1.0 - ADAM_B2 ** ADAM_STEP)
        d_ref[...] = -ADAM_LR * (m_hat / (jnp.sqrt(v_hat) + ADAM_EPS) + ADAM_WD * w_ref[...])
        m2_ref[...] = m2
        v2_ref[...] = v2

    blk = pl.BlockSpec((tr, c), lambda i: (i, 0))
    sds = jax.ShapeDtypeStruct((r, c), F32)
    return pl.pallas_call(body, name=name, grid=(r // tr,), in_specs=[blk] * 4, out_specs=[blk] * 3,
                          out_shape=[sds] * 3, compiler_params=_cparams())(w, g, m, v)


BIG = ("ab_w_in", "ab_w_out", "c_w_in", "c_w_out", "mlp_w1", "mlp_w2")
SMALL = ("ln_mix_g", "ln_mix_b", "ln_ffn_g", "ln_ffn_b", "c_lb_raw", "ab_a_log", "ab_dt_bias", "ab_gnorm_g", "c_gnorm_g")
SMALL_ROWS = 16
CONV_ROWS = 8
CONV_W = 3 * GDN_H * HD


def _conv_to_rows(cw):
    return jnp.pad(cw, ((0, 0), (0, 2 * D - CONV_W))).reshape(CONV_ROWS, D)


def _rows_to_conv(rows):
    return rows.reshape(CONV_K, 2 * D)[:, :CONV_W]


def _pack_small(d):
    rows = [jnp.pad(d[n], ((0, 0), (0, D - d[n].shape[1]))) for n in SMALL]
    buf = jnp.concatenate(rows, axis=0)
    return jnp.pad(buf, ((0, SMALL_ROWS - buf.shape[0]), (0, 0)))


def _unpack_small(buf, like):
    out, r = {}, 0
    for n in SMALL:
        nr, nc = like[n].shape
        out[n] = buf[r:r + nr, :nc]
        r += nr
    return out


def kernel(x, meta_tokens, ab_w_in, ab_conv_w, ab_a_log, ab_dt_bias, ab_gnorm_g, ab_w_out, c_w_in, c_lb_raw, c_gnorm_g, c_w_out, ln_mix_g, ln_mix_b, mlp_w1, mlp_w2, ln_ffn_g, ln_ffn_b, loss_target, m_meta_tokens, m_ab_w_in, m_ab_conv_w, m_ab_a_log, m_ab_dt_bias, m_ab_gnorm_g, m_ab_w_out, m_c_w_in, m_c_lb_raw, m_c_gnorm_g, m_c_w_out, m_ln_mix_g, m_ln_mix_b, m_mlp_w1, m_mlp_w2, m_ln_ffn_g, m_ln_ffn_b, v_meta_tokens, v_ab_w_in, v_ab_conv_w, v_ab_a_log, v_ab_dt_bias, v_ab_gnorm_g, v_ab_w_out, v_c_w_in, v_c_lb_raw, v_c_gnorm_g, v_c_w_out, v_ln_mix_g, v_ln_mix_b, v_mlp_w1, v_mlp_w2, v_ln_ffn_g, v_ln_ffn_b):
    names = ("meta_tokens", "ab_w_in", "ab_conv_w", "ab_a_log", "ab_dt_bias", "ab_gnorm_g", "ab_w_out", "c_w_in",
             "c_lb_raw", "c_gnorm_g", "c_w_out", "ln_mix_g", "ln_mix_b", "mlp_w1", "mlp_w2", "ln_ffn_g", "ln_ffn_b")
    wts = dict(zip(names, (meta_tokens, ab_w_in, ab_conv_w, ab_a_log, ab_dt_bias, ab_gnorm_g, ab_w_out, c_w_in, c_lb_raw,
                           c_gnorm_g, c_w_out, ln_mix_g, ln_mix_b, mlp_w1, mlp_w2, ln_ffn_g, ln_ffn_b)))
    mom_m = dict(zip(names, (m_meta_tokens, m_ab_w_in, m_ab_conv_w, m_ab_a_log, m_ab_dt_bias, m_ab_gnorm_g, m_ab_w_out,
                             m_c_w_in, m_c_lb_raw, m_c_gnorm_g, m_c_w_out, m_ln_mix_g, m_ln_mix_b, m_mlp_w1, m_mlp_w2,
                             m_ln_ffn_g, m_ln_ffn_b)))
    mom_v = dict(zip(names, (v_meta_tokens, v_ab_w_in, v_ab_conv_w, v_ab_a_log, v_ab_dt_bias, v_ab_gnorm_g, v_ab_w_out,
                             v_c_w_in, v_c_lb_raw, v_c_gnorm_g, v_c_w_out, v_ln_mix_g, v_ln_mix_b, v_mlp_w1, v_mlp_w2,
                             v_ln_ffn_g, v_ln_ffn_b)))
    seq = x.shape[1]
    pad = (-(N_META + seq)) % QB
    xi, yi, ci = _pos()
    chip = 2 * xi + yi

    gat_ab_in, = _chip_allgather([ab_w_in[0].astype(BF16)], "gather_weights")
    late = [ab_w_out[0].astype(BF16), c_w_in.astype(BF16), c_w_out[0].astype(BF16), mlp_w1.astype(BF16),
            mlp_w2.astype(BF16)]
    mcols, ccols = meta_tokens.shape[1], ab_conv_w.shape[2]
    place = jnp.concatenate([
        lax.dynamic_update_slice(jnp.zeros((N_META, D), F32), 0.5 * meta_tokens, (0, chip * mcols)),
        _conv_to_rows(lax.dynamic_update_slice(jnp.zeros((CONV_K, CONV_W), F32), 0.5 * ab_conv_w[0], (0, chip * ccols)))],
        axis=0)
    placed = _small_allreduce(place, "gather_meta")
    meta_full = placed[:N_META]

    w = {
        "ab_w_in": _pad_ab_cols(jnp.transpose(gat_ab_in, (1, 0, 2)).reshape(D, AB_TRUE)),
        "conv_w": _rows_to_conv(placed[N_META:]), "a_log": ab_a_log, "dt_bias": ab_dt_bias,
        "ab_gnorm_g": ab_gnorm_g, "c_lb_raw": c_lb_raw,
        "c_gnorm_g": c_gnorm_g, "ln_mix_g": ln_mix_g, "ln_mix_b": ln_mix_b, "ln_ffn_g": ln_ffn_g, "ln_ffn_b": ln_ffn_b,
    }

    h0 = jnp.concatenate([jnp.zeros((pad, D), F32), meta_full, x[0]], axis=0)
    tgt = jnp.concatenate([jnp.zeros((pad + N_META, D), F32), loss_target[0]], axis=0)
    loss8, dh0, g = _local_step(h0, tgt, w, pad, late)
    loss = lax.psum(loss8[0, 0], ("x", "y", "c"))
    grad_x = dh0[pad + N_META:][None]

    gsmall = {"ln_mix_g": g["ln_mix_g"], "ln_mix_b": g["ln_mix_b"], "ln_ffn_g": g["ln_ffn_g"], "ln_ffn_b": g["ln_ffn_b"],
              "c_lb_raw": g["c_lb_raw"], "ab_a_log": g["a_log"], "ab_dt_bias": g["dt_bias"], "ab_gnorm_g": g["ab_gnorm_g"],
              "c_gnorm_g": g["c_gnorm_g"]}
    sbuf = jnp.concatenate([_pack_small(gsmall), dh0[pad:pad + N_META], _conv_to_rows(g["conv_w"])], axis=0)
    ssum = _small_allreduce(sbuf, "allreduce_small")
    grads = _unpack_small(ssum[:SMALL_ROWS], wts)
    grads["meta_tokens"] = lax.dynamic_slice(ssum[SMALL_ROWS:SMALL_ROWS + N_META], (0, chip * mcols), (N_META, mcols))
    grads["ab_conv_w"] = lax.dynamic_slice(_rows_to_conv(ssum[SMALL_ROWS + N_META:]), (0, chip * ccols), (CONV_K, ccols))[None]

    sums = {k: _sum_slots(v, f"grad_sum_{k}") for k, v in g["landed"].items()}
    for n in ("ab_w_in", "ab_w_out", "c_w_in", "c_w_out"):
        grads[n] = sums[n][None]
    grads["mlp_w1"] = jnp.stack([sums["w1_0"], sums["w1_1"]])
    grads["mlp_w2"] = jnp.stack([sums["w2_0"], sums["w2_1"]])

    delta, new_m, new_v = {}, {}, {}
    for n in ("meta_tokens", "ab_conv_w") + BIG:
        shp = wts[n].shape
        to2 = lambda a: a.reshape(-1, shp[-1])
        d2, m2, v2 = _adamw(to2(wts[n]), to2(grads[n]), to2(mom_m[n]), to2(mom_v[n]), f"adamw_{n}")
        delta[n], new_m[n], new_v[n] = d2.reshape(shp), m2.reshape(shp), v2.reshape(shp)
    d2, m2, v2 = _adamw(_pack_small(wts), ssum[:SMALL_ROWS], _pack_small(mom_m), _pack_small(mom_v), "adamw_small")
    delta.update(_unpack_small(d2, wts))
    new_m.update(_unpack_small(m2, wts))
    new_v.update(_unpack_small(v2, wts))

    return (loss, grad_x, *[grads[n] for n in names], *[delta[n] for n in names], *[new_m[n] for n in names],
            *[new_v[n] for n in names])
```

```python
import functools
import math

import numpy as np
import jax
import jax.numpy as jnp
from jax import lax
from jax.experimental import pallas as pl
from jax.experimental.pallas import tpu as pltpu

F32 = jnp.float32
BF16 = jnp.bfloat16

D = 1024
N_META = 16
D_FF = 4 * D
DEPTH = 2
GDN_H = 4
SB_H = 8
SB_DH = 64
HG_H = 8
HD = 128
CH = 64
QB = 128
ALPHA = float((2 * DEPTH) ** 0.25)
LN_EPS = 1e-5
RMS_EPS = 1e-6
L2_EPS = 1e-6
NEG = -1e30

ADAM_LR = 0.001
ADAM_B1 = 0.9
ADAM_B2 = 0.999
ADAM_EPS = 1e-08
ADAM_WD = 0.01
ADAM_STEP = 10

AB_W = 30 * HD
AB_TRUE = 3592
VMEM_LIMIT = 56 * 1024 * 1024

NN = ((1,), (0,))
NT = ((1,), (1,))
TN = ((0,), (0,))


def _cparams(**kw):
    return pltpu.CompilerParams(vmem_limit_bytes=VMEM_LIMIT, **kw)


def _dg(a, b, dims, mode):
    if mode == "h":
        return lax.dot_general(a, b, dims, precision=lax.Precision.HIGHEST, preferred_element_type=F32)
    if mode == "b":
        return lax.dot_general(a.astype(BF16), b.astype(BF16), dims, preferred_element_type=F32)
    ah, bh = a.astype(BF16), b.astype(BF16)
    al, bl = (a - ah.astype(F32)).astype(BF16), (b - bh.astype(F32)).astype(BF16)
    d = lambda x, y: lax.dot_general(x, y, dims, preferred_element_type=F32)
    return d(ah, bh) + (d(ah, bl) + d(al, bh))


def _make_dots(mode, batched=False):
    if batched:
        nn_d, nt_d, tn_d = (((2,), (1,)), ((0,), (0,))), (((2,), (2,)), ((0,), (0,))), (((1,), (1,)), ((0,), (0,)))
    else:
        nn_d, nt_d, tn_d = (NN, ((), ())), (NT, ((), ())), (TN, ((), ()))

    @jax.custom_vjp
    def nn(a, b):
        return _dg(a, b, nn_d, mode)

    @jax.custom_vjp
    def nt(a, b):
        return _dg(a, b, nt_d, mode)

    @jax.custom_vjp
    def tn(a, b):
        return _dg(a, b, tn_d, mode)

    nn.defvjp(lambda a, b: (nn(a, b), (a, b)), lambda r, g: (nt(g, r[1]), tn(r[0], g)))
    nt.defvjp(lambda a, b: (nt(a, b), (a, b)), lambda r, g: (nn(g, r[1]), tn(g, r[0])))
    tn.defvjp(lambda a, b: (tn(a, b), (a, b)), lambda r, g: (nt(r[1], g), nn(r[0], g)))
    return nn, nt, tn


hnn, hnt, htn = _make_dots("h")
bbnn, bbnt, bbtn = _make_dots("b", True)
mbnn, mbnt, mbtn = _make_dots("m", True)
hbnn, hbnt, hbtn = _make_dots("h", True)


def _split3(x, axis):
    x1 = x.astype(BF16)
    r1 = x - x1.astype(F32)
    x2 = r1.astype(BF16)
    x3 = (r1 - x2.astype(F32)).astype(BF16)
    return jnp.concatenate([x1, x2, x3], axis=axis)


@jax.custom_vjp
def _mask_dot(e3, x):
    return lax.dot_general(e3[0], _split3(x, 0), (NN, ((), ())), preferred_element_type=F32)


def _mask_dot_bwd(e3, g):
    dx = lax.dot_general(e3[1], _split3(g, 0), (TN, ((), ())), preferred_element_type=F32)
    return (jnp.zeros_like(e3[0]), jnp.zeros_like(e3[1])), dx


_mask_dot.defvjp(lambda e3, x: (_mask_dot(e3, x), e3), _mask_dot_bwd)


def _heads(a, n):
    return jnp.concatenate([a[None, :, h * HD:(h + 1) * HD] for h in range(n)], axis=0)


def _sigmoid(x):
    return jax.nn.sigmoid(x)


def _silu(x):
    return x * jax.nn.sigmoid(x)


def _softplus(x):
    return jnp.maximum(x, 0.0) + jnp.log(1.0 + jnp.exp(-jnp.abs(x)))


def _iota(shape, dim):
    return lax.broadcasted_iota(jnp.int32, shape, dim)


def _pick(n, prefs):
    for p in prefs:
        if n % p == 0:
            return p
    return n


def _mm(a, b, *, ta=False, tb=False, out_dtype=F32, name, b_view=None, out_split=0, act=False, gate=None,
        cargo=(), exchange=None):
    if ta:
        k_dim, m_dim = a.shape
    else:
        m_dim, k_dim = a.shape
    if b_view is None:
        w_rows, w_cols = b.shape
    else:
        kind, layer = b_view
        nj, _, blk_r, blk_c = b.shape
        w_rows, w_cols = (blk_r, nj * blk_c) if kind == "cols" else (nj * blk_r, blk_c)
    n_dim = w_rows if tb else w_cols
    assert (w_cols if tb else w_rows) == k_dim
    tm = _pick(m_dim, (1024, 1056, 704, 640, 512, 384, 256, 128))
    tn = _pick(n_dim, (1024, 1056, 704, 640, 512, 384, 256, 128))
    tk = _pick(k_dim, (1024, 1056, 704, 512, 384, 256, 128))
    nk = k_dim // tk
    a_spec = pl.BlockSpec((tk, tm), lambda i, j, k: (k, i)) if ta else pl.BlockSpec((tm, tk), lambda i, j, k: (i, k))
    wb = (tn, tk) if tb else (tk, tn)
    w_idx = (lambda i, j, k: (j, k)) if tb else (lambda i, j, k: (k, j))
    if b_view is None:
        b_spec = pl.BlockSpec(wb, w_idx)
    elif kind == "cols":
        per = blk_c // wb[1]
        b_spec = pl.BlockSpec((None, None) + wb,
                              lambda i, j, k: (w_idx(i, j, k)[1] // per, layer, w_idx(i, j, k)[0], w_idx(i, j, k)[1] % per))
    else:
        per = blk_r // wb[0]
        b_spec = pl.BlockSpec((None, None) + wb,
                              lambda i, j, k: (w_idx(i, j, k)[0] // per, layer, w_idx(i, j, k)[0] % per, w_idx(i, j, k)[1]))
    if out_split:
        per_o = (n_dim // out_split) // tn
        out_spec = pl.BlockSpec((None, tm, tn), lambda i, j, k: (j // per_o, i, j % per_o))
        out_sds = jax.ShapeDtypeStruct((out_split, m_dim, n_dim // out_split), out_dtype)
    else:
        out_spec = pl.BlockSpec((tm, tn), lambda i, j, k: (i, j))
        out_sds = jax.ShapeDtypeStruct((m_dim, n_dim), out_dtype)
    dims = (((0 if ta else 1,), (1 if tb else 0,)), ((), ()))
    extra = [] if gate is None else [gate]
    n_out = 2 if act else 1

    def finish(acc, refs):
        if act:
            refs[0][...] = acc.astype(refs[0].dtype)
            r = jnp.maximum(acc, 0.0)
            refs[1][...] = (r * r).astype(refs[1].dtype)
        elif gate is not None:
            refs[1][...] = (acc * (2.0 * jnp.maximum(refs[0][...].astype(F32), 0.0))).astype(refs[1].dtype)
        else:
            refs[0][...] = acc.astype(refs[0].dtype)

    grid = (m_dim // tm, n_dim // tn, nk)
    nc = len(cargo)

    def body(a_ref, b_ref, *rest):
        acc_ref = rest[-1]
        ids = [pl.program_id(d) for d in range(3)]
        outs, end_cargo = _cargo_bounds(
            rest[len(extra):-1], nc, n_out, exchange, (ids[0] == 0) & (ids[1] == 0) & (ids[2] == 0),
            (ids[0] == grid[0] - 1) & (ids[1] == grid[1] - 1) & (ids[2] == grid[2] - 1))
        refs = tuple(rest[:len(extra)]) + tuple(outs)
        part = lax.dot_general(a_ref[...], b_ref[...], dims, preferred_element_type=F32)
        if nk == 1:
            finish(part, refs)
        else:
            k = ids[2]

            @pl.when(k == 0)
            def _():
                acc_ref[...] = part

            @pl.when(k > 0)
            def _():
                acc_ref[...] += part

            @pl.when(k == nk - 1)
            def _():
                finish(acc_ref[...], refs)
        end_cargo()

    out = pl.pallas_call(
        body, name=name, grid=grid,
        in_specs=[a_spec, b_spec] + [pl.BlockSpec((tm, tn), lambda i, j, k: (i, j))] * len(extra) + [ANY] * nc,
        out_specs=[out_spec] * n_out + [ANY] * nc,
        out_shape=[out_sds] * n_out + (exchange[1](cargo) if nc else []),
        scratch_shapes=(exchange[2](nc) if nc else []) + [pltpu.VMEM((tm, tn) if nk > 1 else (8, 128), F32)],
        compiler_params=_cparams(dimension_semantics=("arbitrary",) * 3 if nc else ("parallel", "parallel", "arbitrary")),
    )(a, b, *extra, *cargo)
    if nc:
        return out
    return out if act else out[0]


def _row_tile(t_pad, width):
    for tr in (528, 352, 176, 128, 64):
        if t_pad % tr == 0 and tr * width * 4 <= (3 << 19) and tr % 16 == 0:
            return tr
    return 64 if t_pad % 64 == 0 else t_pad


def _ln_res_fn(h, m, g, b):
    x = ALPHA * h + m
    mu = jnp.mean(x, axis=-1, keepdims=True)
    xc = x - mu
    var = jnp.mean(xc * xc, axis=-1, keepdims=True)
    return xc * lax.rsqrt(var + LN_EPS) * g + b


def _ln_res_fwd(h, m, g, b, name):
    t_pad = h.shape[0]
    tr = _row_tile(t_pad, D)

    def body(h_ref, m_ref, g_ref, b_ref, y_ref, yb_ref):
        y = _ln_res_fn(h_ref[...], m_ref[...], g_ref[...], b_ref[...])
        y_ref[...] = y
        yb_ref[...] = y.astype(BF16)

    row = pl.BlockSpec((tr, D), lambda i: (i, 0))
    par = pl.BlockSpec((1, D), lambda i: (0, 0))
    return pl.pallas_call(
        body, name=name, grid=(t_pad // tr,), in_specs=[row, row, par, par], out_specs=[row, row],
        out_shape=[jax.ShapeDtypeStruct((t_pad, D), F32), jax.ShapeDtypeStruct((t_pad, D), BF16)],
        compiler_params=_cparams(),
    )(h, m, g, b)


def _ln_res_bwd(h, m, g, b, dy1, dy2, name):
    t_pad = h.shape[0]
    tr = _row_tile(t_pad, D)

    def body(h_ref, m_ref, g_ref, b_ref, d1_ref, d2_ref, dh_ref, dm_ref, dg_ref, db_ref):
        _, vjp = jax.vjp(_ln_res_fn, h_ref[...], m_ref[...], g_ref[...], b_ref[...])
        dh, dm, dg, db = vjp(d1_ref[...] + d2_ref[...])
        dh_ref[...] = dh
        dm_ref[...] = dm.astype(BF16)

        @pl.when(pl.program_id(0) == 0)
        def _():
            dg_ref[...] = jnp.zeros_like(dg_ref)
            db_ref[...] = jnp.zeros_like(db_ref)

        dg_ref[...] += dg
        db_ref[...] += db

    row = pl.BlockSpec((tr, D), lambda i: (i, 0))
    par = pl.BlockSpec((1, D), lambda i: (0, 0))
    return pl.pallas_call(
        body, name=name, grid=(t_pad // tr,), in_specs=[row, row, par, par, row, row],
        out_specs=[row, row, par, par],
        out_shape=[jax.ShapeDtypeStruct((t_pad, D), F32), jax.ShapeDtypeStruct((t_pad, D), BF16),
                   jax.ShapeDtypeStruct((1, D), F32), jax.ShapeDtypeStruct((1, D), F32)],
        compiler_params=_cparams(),
    )(h, m, g, b, dy1, dy2)


def _grms_fn(o, z, g):
    y = o * lax.rsqrt(jnp.mean(o * o, axis=-1, keepdims=True) + RMS_EPS) * g
    return y * _silu(z)


def _grms_fwd(o, z_arr, z_blk0, g, name):
    t_pad, w = o.shape
    nh = w // HD
    tr = _row_tile(t_pad, HD * 4)

    def body(o_ref, z_ref, g_ref, y_ref):
        y_ref[...] = _grms_fn(o_ref[...], z_ref[...], g_ref[...]).astype(BF16)

    return pl.pallas_call(
        body, name=name, grid=(t_pad // tr, nh),
        in_specs=[pl.BlockSpec((tr, HD), lambda i, h: (i, h)),
                  pl.BlockSpec((tr, HD), lambda i, h: (i, z_blk0 + h)),
                  pl.BlockSpec((1, HD), lambda i, h: (0, 0))],
        out_specs=pl.BlockSpec((tr, HD), lambda i, h: (i, h)),
        out_shape=jax.ShapeDtypeStruct((t_pad, w), BF16), compiler_params=_cparams(),
    )(o, z_arr, g)


def _grms_bwd(o, z_arr, z_blk0, g, dy_arr, dy_blk0, name):
    t_pad, w = o.shape
    nh = w // HD
    tr = _row_tile(t_pad, HD * 4)

    def body(o_ref, z_ref, g_ref, dy_ref, do_ref, dz_ref, dg_ref):
        _, vjp = jax.vjp(_grms_fn, o_ref[...], z_ref[...], g_ref[...])
        do, dz, dg = vjp(dy_ref[...])
        do_ref[...] = do
        dz_ref[...] = dz

        @pl.when((pl.program_id(0) == 0) & (pl.program_id(1) == 0))
        def _():
            dg_ref[...] = jnp.zeros_like(dg_ref)

        dg_ref[...] += dg

    blk = pl.BlockSpec((tr, HD), lambda i, h: (i, h))
    return pl.pallas_call(
        body, name=name, grid=(t_pad // tr, nh),
        in_specs=[blk, pl.BlockSpec((tr, HD), lambda i, h: (i, z_blk0 + h)),
                  pl.BlockSpec((1, HD), lambda i, h: (0, 0)),
                  pl.BlockSpec((tr, HD), lambda i, h: (i, dy_blk0 + h))],
        out_specs=[blk, blk, pl.BlockSpec((1, HD), lambda i, h: (0, 0))],
        out_shape=[jax.ShapeDtypeStruct((t_pad, w), F32), jax.ShapeDtypeStruct((t_pad, w), F32),
                   jax.ShapeDtypeStruct((1, HD), F32)],
        compiler_params=_cparams(),
    )(o, z_arr, g, dy_arr)


def _loss_fwd(y, tgt, first_row, name):
    t_pad = y.shape[0]
    tr = _row_tile(t_pad, D)

    def body(y_ref, t_ref, l_ref, dy_ref):
        rows = pl.program_id(0) * tr + _iota((tr, 1), 0)
        err = jnp.where(rows >= first_row, y_ref[...] - t_ref[...], 0.0)
        dy_ref[...] = err * (1.0 / D)

        @pl.when(pl.program_id(0) == 0)
        def _():
            l_ref[...] = jnp.zeros_like(l_ref)

        part = jnp.sum(jnp.sum(err * err, axis=1, keepdims=True), axis=0, keepdims=True)
        l_ref[...] += jnp.broadcast_to(part * (0.5 / D), l_ref.shape)

    row = pl.BlockSpec((tr, D), lambda i: (i, 0))
    return pl.pallas_call(
        body, name=name, grid=(t_pad // tr,), in_specs=[row, row],
        out_specs=[pl.BlockSpec((8, 128), lambda i: (0, 0)), row],
        out_shape=[jax.ShapeDtypeStruct((8, 128), F32), jax.ShapeDtypeStruct((t_pad, D), F32)],
        compiler_params=_cparams(),
    )(y, tgt)


def _add2(a, b, name):
    t_pad, w = a.shape
    tr = _row_tile(t_pad, w)

    def body(a_ref, b_ref, o_ref):
        o_ref[...] = a_ref[...] + b_ref[...]

    row = pl.BlockSpec((tr, w), lambda i: (i, 0))
    return pl.pallas_call(body, name=name, grid=(t_pad // tr,), in_specs=[row, row], out_specs=row,
                          out_shape=jax.ShapeDtypeStruct((t_pad, w), F32), compiler_params=_cparams())(a, b)


def _assemble_bf16(parts, name):
    t_pad = parts[0][0].shape[0] if parts[0][1] == "cols" else parts[0][0].shape[1]
    widths = [p.shape[1] if kind == "cols" else HD for p, kind in parts]
    total = sum(widths)
    tr = _row_tile(t_pad, total)

    def body(*refs):
        o_ref = refs[-1]
        off = 0
        for ref, (p, kind), w in zip(refs[:-1], parts, widths):
            if kind == "cols":
                o_ref[:, off:off + w] = ref[...].astype(BF16)
            else:
                acc = ref[0]
                for hh in range(1, p.shape[0]):
                    acc = acc + ref[hh]
                o_ref[:, off:off + w] = acc.astype(BF16)
            off += w

    specs = []
    for p, kind in parts:
        if kind == "cols":
            specs.append(pl.BlockSpec((tr, p.shape[1]), lambda i: (i, 0)))
        else:
            specs.append(pl.BlockSpec((p.shape[0], tr, HD), lambda i: (0, i, 0)))
    return pl.pallas_call(
        body, name=name, grid=(t_pad // tr,), in_specs=specs,
        out_specs=pl.BlockSpec((tr, total), lambda i: (i, 0)),
        out_shape=jax.ShapeDtypeStruct((t_pad, total), BF16), compiler_params=_cparams(),
    )(*[p for p, _ in parts])


CONV_K = 4
HALO = 8
RT = 128


def _conv_fwd(p, blk0, w, mode, pad, name):
    t_pad = p.shape[0]
    nt = t_pad // RT
    scale = HD ** -0.5 if mode == "q" else 1.0

    def body(x_ref, w_ref, y_ref, xs_ref):
        xs_ref[0:HALO, :] = jnp.zeros((HALO, HD), F32)
        rows = _iota((t_pad, 1), 0)
        xs_ref[HALO:HALO + t_pad, :] = jnp.where(rows >= pad, x_ref[...], 0.0)
        wv = w_ref[...]

        def tile(i, carry):
            r0 = pl.multiple_of(i * RT, RT)
            ext = xs_ref[pl.ds(r0, RT + HALO), :]
            acc = ext[HALO:, :] * wv[3:4, :]
            for s in (1, 2, 3):
                acc = acc + pltpu.roll(ext, s, 0)[HALO:, :] * wv[3 - s:4 - s, :]
            y = _silu(acc)
            if mode != "v":
                y = y * lax.rsqrt(jnp.sum(y * y, axis=-1, keepdims=True) + L2_EPS) * scale
            y_ref[pl.ds(r0, RT), :] = y
            return carry

        lax.fori_loop(0, nt, tile, 0)

    return pl.pallas_call(
        body, name=name, grid=(GDN_H,),
        in_specs=[pl.BlockSpec((t_pad, HD), lambda h: (0, blk0 + h)), pl.BlockSpec((CONV_K, HD), lambda h: (0, h))],
        out_specs=pl.BlockSpec((t_pad, HD), lambda h: (0, h)),
        out_shape=jax.ShapeDtypeStruct((t_pad, GDN_H * HD), F32),
        scratch_shapes=[pltpu.VMEM((t_pad + HALO, HD), F32)],
        compiler_params=_cparams(),
    )(p, w)


def _conv_bwd(p, blk0, w, dn, mode, pad, name):
    t_pad = p.shape[0]
    nt = t_pad // RT
    scale = HD ** -0.5 if mode == "q" else 1.0

    def body(x_ref, w_ref, dn_ref, dx_ref, dw_ref, xs_ref, ds_ref):
        xs_ref[0:HALO, :] = jnp.zeros((HALO, HD), F32)
        xs_ref[HALO + t_pad:HALO + t_pad + 2 * HALO, :] = jnp.zeros((2 * HALO, HD), F32)
        ds_ref[t_pad:t_pad + HALO, :] = jnp.zeros((HALO, HD), F32)
        rows = _iota((t_pad, 1), 0)
        xs_ref[HALO:HALO + t_pad, :] = jnp.where(rows >= pad, x_ref[...], 0.0)
        ds_ref[0:t_pad, :] = dn_ref[...]
        wv = w_ref[...]

        def tile(i, dw):
            r0 = pl.multiple_of(i * RT, RT)
            ext = xs_ref[pl.ds(r0, RT + 2 * HALO), :]
            dn_e = ds_ref[pl.ds(r0, RT + HALO), :]
            xsh = [ext[HALO:, :]] + [pltpu.roll(ext, s, 0)[HALO:, :] for s in (1, 2, 3)]
            pre = xsh[0] * wv[3:4, :]
            for s in (1, 2, 3):
                pre = pre + xsh[s] * wv[3 - s:4 - s, :]
            sg = _sigmoid(pre)
            y = pre * sg
            if mode != "v":
                ss = jnp.sum(y * y, axis=-1, keepdims=True) + L2_EPS
                r = lax.rsqrt(ss)
                dy = scale * (dn_e * r - y * (r * r * r) * jnp.sum(dn_e * y, axis=-1, keepdims=True))
            else:
                dy = dn_e
            dpre = dy * (sg * (1.0 + pre * (1.0 - sg)))
            dx = dpre[:RT, :] * wv[3:4, :]
            for s in (1, 2, 3):
                dx = dx + pltpu.roll(dpre, RT + HALO - s, 0)[:RT, :] * wv[3 - s:4 - s, :]
            trow = r0 + _iota((RT, 1), 0)
            dx_ref[pl.ds(r0, RT), :] = jnp.where(trow >= pad, dx, 0.0)
            new = []
            for s in (0, 1, 2, 3):
                new.append(dw[s] + jnp.sum(dpre[:RT, :] * xsh[s][:RT, :], axis=0, keepdims=True))
            return tuple(new)

        z = jnp.zeros((1, HD), F32)
        dw = lax.fori_loop(0, nt, tile, (z, z, z, z))
        for s in (0, 1, 2, 3):
            dw_ref[3 - s:4 - s, :] = dw[s]

    return pl.pallas_call(
        body, name=name, grid=(GDN_H,),
        in_specs=[pl.BlockSpec((t_pad, HD), lambda h: (0, blk0 + h)), pl.BlockSpec((CONV_K, HD), lambda h: (0, h)),
                  pl.BlockSpec((t_pad, HD), lambda h: (0, h))],
        out_specs=[pl.BlockSpec((t_pad, HD), lambda h: (0, h)), pl.BlockSpec((CONV_K, HD), lambda h: (0, h))],
        out_shape=[jax.ShapeDtypeStruct((t_pad, GDN_H * HD), F32), jax.ShapeDtypeStruct((CONV_K, GDN_H * HD), F32)],
        scratch_shapes=[pltpu.VMEM((t_pad + 3 * HALO, HD), F32), pltpu.VMEM((t_pad + HALO, HD), F32)],
        compiler_params=_cparams(),
    )(p, w, dn)


@jax.custom_vjp
def _unit_lower_inv(m, bd, eye):
    md = m * bd
    low = m - md
    p2 = mbnn(md, md)
    p4 = mbnn(p2, p2)
    dinv = mbnn(mbnn(eye - md, eye + p2), eye + p4)
    n = mbnn(dinv, low)
    n2 = mbnn(n, n)
    n4 = mbnn(n2, n2)
    return mbnn(mbnn(mbnn(eye - n, eye + n2), eye + n4), dinv)


def _unit_lower_inv_bwd(res, g):
    t, bd, eye = res
    return -mbtn(t, mbnt(g, t)), jnp.zeros_like(bd), jnp.zeros_like(eye)


def _unit_lower_inv_fwd(m, bd, eye):
    t = _unit_lower_inv(m, bd, eye)
    return t, (t, bd, eye)


_unit_lower_inv.defvjp(_unit_lower_inv_fwd, _unit_lower_inv_bwd)


def _gdn_chunk(q, k, v, bb, aa, alog, dtb, s, valid):
    nh = q.shape[0]
    ri = _iota((1, CH, CH), 1)
    ci = _iota((1, CH, CH), 2)
    causal = ri >= ci
    strict = ri > ci
    eye = (ri == ci).astype(F32)
    bd = ((ri >> 3) == (ci >> 3)).astype(F32)
    ltri = (_iota((CH, CH), 0) >= _iota((CH, CH), 1)).astype(F32)
    sel = (_iota((nh, 1, HD), 2) == _iota((nh, 1, HD), 0)).astype(F32)

    beta_all = jnp.where(valid, _sigmoid(bb), 0.0)
    g_all = jnp.where(valid, -jnp.exp(alog) * _softplus(aa + dtb), 0.0)
    gc_all = hnn(ltri, g_all)
    beta = jnp.sum(beta_all[None] * sel, axis=2, keepdims=True)
    gc = jnp.sum(gc_all[None] * sel, axis=2, keepdims=True)
    gc_rows = hbnt(jnp.broadcast_to(sel, (nh, CH, HD)), jnp.broadcast_to(gc_all[None], (nh, CH, HD)))
    last = _iota((1, CH, 1), 1) == CH - 1
    gc_last = jnp.sum(jnp.where(last, gc, 0.0), axis=1, keepdims=True)
    decay = jnp.exp(jnp.where(causal, gc - gc_rows, NEG))
    egc = jnp.exp(gc)

    kb = k * beta
    m = jnp.where(strict, bbnt(kb, k) * decay, 0.0)
    t_inv = _unit_lower_inv(m, bd, eye)
    u = bbnn(t_inv, v * beta)
    w = bbnn(t_inv, kb * egc)
    a_intra = bbnt(q, k) * decay
    q_dec = q * egc
    k_dec = k * jnp.exp(gc_last - gc)
    v_new = u - bbnn(w, s)
    o = bbnn(q_dec, s) + bbnn(a_intra, v_new)
    s_new = s * jnp.exp(gc_last) + bbtn(k_dec, v_new)
    return o, s_new


PAIR = 2 * CH


def _gdn_specs(npair, rev):
    cc = (lambda c: npair - 1 - c) if rev else (lambda c: c)
    wide = pl.BlockSpec((PAIR, GDN_H * HD), lambda c: (cc(c), 0))
    fix = lambda off: pl.BlockSpec((PAIR, HD), lambda c: (cc(c), off))
    par = pl.BlockSpec((1, HD), lambda c: (0, 0))
    state = pl.BlockSpec((1, GDN_H, HD, HD), lambda c: (cc(c), 0, 0, 0))
    return wide, fix, par, state


def _store_heads(ref, a, rows=slice(None)):
    for h in range(a.shape[0]):
        ref[rows, h * HD:(h + 1) * HD] = a[h]


def _chunk_rows(half):
    return slice(half * CH, (half + 1) * CH)


def _chunk_valid(pair, half, pad):
    return ((2 * pair + half) * CH + _iota((CH, 1), 0)) >= pad


def _gdn_fwd(qn, kn, vn, p, alog, dtb, pad, name):
    t_pad = qn.shape[0]
    npair = t_pad // PAIR
    wide, fix, par, state = _gdn_specs(npair, False)

    def body(q_ref, k_ref, v_ref, bb_ref, aa_ref, al_ref, dt_ref, o_ref, ss_ref, s_ref):
        c = pl.program_id(0)

        @pl.when(c == 0)
        def _():
            s_ref[...] = jnp.zeros_like(s_ref)

        s = s_ref[...]
        ss_ref[0] = s
        for half in (0, 1):
            r = _chunk_rows(half)
            o, s = _gdn_chunk(_heads(q_ref[r, :], GDN_H), _heads(k_ref[r, :], GDN_H), _heads(v_ref[r, :], GDN_H),
                              bb_ref[r, :], aa_ref[r, :], al_ref[...], dt_ref[...], s, _chunk_valid(c, half, pad))
            _store_heads(o_ref, o, r)
        s_ref[...] = s

    return pl.pallas_call(
        body, name=name, grid=(npair,),
        in_specs=[wide, wide, wide, fix(16), fix(17), par, par],
        out_specs=[wide, state],
        out_shape=[jax.ShapeDtypeStruct((t_pad, GDN_H * HD), F32), jax.ShapeDtypeStruct((npair, GDN_H, HD, HD), F32)],
        scratch_shapes=[pltpu.VMEM((GDN_H, HD, HD), F32)],
        compiler_params=_cparams(),
    )(qn, kn, vn, p, p, alog, dtb)


def _gdn_bwd(qn, kn, vn, p, alog, dtb, ssave, do, pad, name, cargo=(), exchange=None):
    t_pad = qn.shape[0]
    npair = t_pad // PAIR
    wide, fix, par, state = _gdn_specs(npair, True)
    n = len(cargo)

    def body(q_ref, k_ref, v_ref, bb_ref, aa_ref, al_ref, dt_ref, ss_ref, do_ref, *rest):
        c = pl.program_id(0)
        ds_ref = rest[-1]
        (dq_ref, dk_ref, dv_ref, dbb_ref, daa_ref, dal_ref, ddt_ref), end_cargo = _cargo_bounds(
            rest[:-1], n, 7, exchange, c == 0, c == npair - 1)

        @pl.when(c == 0)
        def _():
            ds_ref[...] = jnp.zeros_like(ds_ref)
            dal_ref[...] = jnp.zeros_like(dal_ref)
            ddt_ref[...] = jnp.zeros_like(ddt_ref)

        ra, rb = _chunk_rows(0), _chunk_rows(1)
        va, vb = _chunk_valid(npair - 1 - c, 0, pad), _chunk_valid(npair - 1 - c, 1, pad)

        def pair(qa, ka, va_, ba, aa, qb, kb, vb_, bb, ab, al, dt, s):
            oa, s = _gdn_chunk(qa, ka, va_, ba, aa, al, dt, s, va)
            ob, s = _gdn_chunk(qb, kb, vb_, bb, ab, al, dt, s, vb)
            return oa, ob, s

        ins = [f(ref[r, :]) for r in (ra, rb)
               for ref, f in ((q_ref, lambda a: _heads(a, GDN_H)), (k_ref, lambda a: _heads(a, GDN_H)),
                              (v_ref, lambda a: _heads(a, GDN_H)), (bb_ref, lambda a: a), (aa_ref, lambda a: a))]
        _, vjp = jax.vjp(pair, *ins, al_ref[...], dt_ref[...], ss_ref[0])
        g = vjp((_heads(do_ref[ra, :], GDN_H), _heads(do_ref[rb, :], GDN_H), ds_ref[...]))
        for r, (dq, dk, dv, dbb, daa) in ((ra, g[0:5]), (rb, g[5:10])):
            _store_heads(dq_ref, dq, r)
            _store_heads(dk_ref, dk, r)
            _store_heads(dv_ref, dv, r)
            dbb_ref[r, :] = dbb
            daa_ref[r, :] = daa
        dal_ref[...] += g[10]
        ddt_ref[...] += g[11]
        ds_ref[...] = g[12]
        end_cargo()

    sds = jax.ShapeDtypeStruct
    return pl.pallas_call(
        body, name=name, grid=(npair,),
        in_specs=[wide, wide, wide, fix(16), fix(17), par, par, state, wide] + [ANY] * n,
        out_specs=[wide, wide, wide, fix(0), fix(0), par, par] + [ANY] * n,
        out_shape=[sds((t_pad, GDN_H * HD), F32)] * 3 + [sds((t_pad, HD), F32)] * 2 + [sds((1, HD), F32)] * 2
        + (exchange[1](cargo) if n else []),
        scratch_shapes=(exchange[2](n) if n else []) + [pltpu.VMEM((GDN_H, HD, HD), F32)],
        compiler_params=_cparams(),
    )(qn, kn, vn, p, p, alog, dtb, ssave, do, *cargo)


SB_Q0, SB_K0, SB_V0 = 18, 22, 26
SB_SCALE = SB_DH ** -0.5
SB_NB = 4


def _sb_terms(z, allowed):
    e = jnp.exp(-jnp.abs(z))
    den = 1.0 + e
    raw = -jnp.maximum(z, 0.0) - jnp.log(den)
    l1m = raw if allowed is None else jnp.where(allowed, raw, 0.0)
    return l1m, z + raw, jnp.where(z >= 0.0, 1.0, e) / den


def _sb_passes(i, step, carry):
    total = i + 1
    sized = lambda done: [functools.partial(step, done, masked=True, nb=nb) for nb in range(1, SB_NB + 1)]

    def several(c):
        n_mid = (total - SB_NB - 1) // SB_NB
        c = step(0, c, masked=True, nb=SB_NB)
        c = lax.fori_loop(0, n_mid, lambda t, cc: step(SB_NB * (1 + t), cc, masked=False, nb=SB_NB), c)
        done = SB_NB * (1 + n_mid)
        return lax.switch(total - done - 1, sized(done), c)

    return lax.cond(total <= SB_NB, lambda c: lax.switch(total - 1, sized(0), c), several, carry)


def _sb_stack(a, i):
    first = _iota((1, HD), 1) < SB_DH
    a2 = jnp.concatenate([jnp.where(first, a, 0.0), jnp.where(first, 0.0, a)], axis=0).astype(BF16)
    rq = i * QB + _iota((QB, 1), 0)
    return a2, jnp.concatenate([rq, rq], axis=0), first


def _hi_lo(a):
    hi = a.astype(BF16)
    lo = (a - hi.astype(F32)).astype(BF16)
    return jnp.concatenate([hi, lo], axis=1)


def _cargo_bounds(refs, n, n_out, exchange, first, last):
    outs = refs[n:n + n_out]
    if not n:
        return outs, lambda: None
    ex = exchange[0](refs[:n], refs[n + n_out:2 * n + n_out], *refs[2 * n + n_out:])

    @pl.when(first)
    def _():
        ex.start()

    def finish():
        @pl.when(last)
        def _():
            ex.wait()

    return outs, finish


def _sb_fwd(p, pad, name, cargo=(), exchange=None):
    t_pad = p.shape[0]
    nq = t_pad // QB
    n = len(cargo)

    def body(q_ref, k_ref, v_ref, *rest):
        i = pl.program_id(1)
        pr = pl.program_id(0)
        (o_ref, r_ref), end_cargo = _cargo_bounds(rest, n, 2, exchange, (pr == 0) & (i == 0),
                                                  (pr == SB_H // 2 - 1) & (i == nq - 1))
        q2, rowq, first = _sb_stack(q_ref[...] * SB_SCALE, i)
        tri = (_iota((QB, QB), 0) > _iota((QB, QB), 1)).astype(BF16)
        upper2 = jnp.concatenate([tri, tri], axis=0)

        def chain(kb, masked):
            start = pl.multiple_of(kb * QB, QB)
            kblk = k_ref[pl.ds(start, QB), :].astype(BF16)
            vblk = v_ref[pl.ds(start, QB), :].astype(BF16)
            z = lax.dot_general(q2, kblk, (NT, ((), ())), preferred_element_type=F32)
            colk = kb * QB + _iota((1, QB), 1)
            al = ((colk < rowq) & (colk >= pad)) if masked else None
            l1m, ls, _ = _sb_terms(z, al)
            suf = lax.dot_general(_hi_lo(l1m), upper2, (NN, ((), ())), preferred_element_type=F32)
            return al, ls, suf, jnp.sum(l1m, axis=1, keepdims=True), vblk

        def step(done, carry, masked, nb):
            o_acc, run = carry
            ws, vs = [], []
            for n in range(nb):
                al, ls, suf, rs, vblk = chain(i - done - n, masked)
                wgt = jnp.exp(ls + suf + run)
                ws.append((wgt if al is None else jnp.where(al, wgt, 0.0)).astype(BF16))
                vs.append(vblk)
                run = run + rs
            o_acc = o_acc + lax.dot_general(jnp.concatenate(ws, axis=1), jnp.concatenate(vs, axis=0),
                                            (NN, ((), ())), preferred_element_type=F32)
            return o_acc, run

        o_acc, run = _sb_passes(i, step, (jnp.zeros((2 * QB, HD), F32), jnp.zeros((2 * QB, 1), F32)))
        o_ref[...] = jnp.where(first, o_acc[:QB], o_acc[QB:]).astype(BF16)
        r_ref[...] = jnp.where(first, run[:QB], run[QB:])
        end_cargo()

    full = lambda off: pl.BlockSpec((t_pad, HD), lambda pr, i: (0, off + pr))
    blk = pl.BlockSpec((QB, HD), lambda pr, i: (i, pr))
    return pl.pallas_call(
        body, name=name, grid=(SB_H // 2, nq),
        in_specs=[pl.BlockSpec((QB, HD), lambda pr, i: (i, SB_Q0 + pr)), full(SB_K0), full(SB_V0)] + [ANY] * n,
        out_specs=[blk, blk] + [ANY] * n,
        out_shape=[jax.ShapeDtypeStruct((t_pad, SB_H * SB_DH), BF16), jax.ShapeDtypeStruct((t_pad, SB_H * SB_DH), F32)]
        + (exchange[1](cargo) if n else []),
        scratch_shapes=exchange[2](n) if n else [],
        compiler_params=_cparams(),
    )(p, p, p, *cargo)


def _sb_bwd(p, rtot, dy, dy_blk0, pad, name, cargo=(), exchange=None):
    t_pad = p.shape[0]
    nq = t_pad // QB
    n = len(cargo)

    def body(q_ref, k_ref, v_ref, r_ref, do_ref, *rest):
        i = pl.program_id(1)
        pr = pl.program_id(0)
        (dq_ref, dk_ref, dv_ref), end_cargo = _cargo_bounds(rest, n, 3, exchange, (pr == 0) & (i == 0),
                                                            (pr == SB_H // 2 - 1) & (i == nq - 1))

        @pl.when(i == 0)
        def _():
            dk_ref[...] = jnp.zeros_like(dk_ref)
            dv_ref[...] = jnp.zeros_like(dv_ref)

        q2, rowq, first = _sb_stack(q_ref[...] * SB_SCALE, i)
        do2, _, _ = _sb_stack(do_ref[...], i)
        rt = r_ref[...]
        lane = _iota((1, HD), 1)
        rcol = jnp.concatenate([jnp.sum(jnp.where(lane == 0, rt, 0.0), axis=1, keepdims=True),
                                jnp.sum(jnp.where(lane == SB_DH, rt, 0.0), axis=1, keepdims=True)], axis=0)
        rj = _iota((QB, QB), 0)
        cs = _iota((QB, QB), 1)
        tri_u = (rj > cs).astype(BF16)
        tri_l = (rj < cs).astype(BF16)
        upper2 = jnp.concatenate([tri_u, tri_u], axis=0)
        lower2 = jnp.concatenate([tri_l, tri_l], axis=0)

        def chain(kb, masked):
            start = pl.multiple_of(kb * QB, QB)
            kblk = k_ref[pl.ds(start, QB), :].astype(BF16)
            vblk = v_ref[pl.ds(start, QB), :].astype(BF16)
            z = lax.dot_general(q2, kblk, (NT, ((), ())), preferred_element_type=F32)
            colk = kb * QB + _iota((1, QB), 1)
            al = ((colk < rowq) & (colk >= pad)) if masked else None
            l1m, ls, sg = _sb_terms(z, al)
            dwgt = lax.dot_general(do2, vblk, (NT, ((), ())), preferred_element_type=F32)
            suf = lax.dot_general(_hi_lo(l1m), upper2, (NN, ((), ())), preferred_element_type=F32)
            return start, kblk, al, ls, suf, jnp.sum(l1m, axis=1, keepdims=True), dwgt, sg

        def finish(c, seen, gseen):
            start, kblk, al, ls, suf, rs, dwgt, sg = c
            wgt = jnp.exp(ls + suf + (rcol - seen - rs))
            if al is not None:
                wgt = jnp.where(al, wgt, 0.0)
            dl = dwgt * wgt
            gpre = gseen + lax.dot_general(_hi_lo(dl), lower2, (NN, ((), ())), preferred_element_type=F32)
            dz = dl - sg * (dl + gpre)
            if al is not None:
                dz = jnp.where(al, dz, 0.0)
            dz = dz.astype(BF16)
            dk_ref[pl.ds(start, QB), :] += lax.dot_general(dz, q2, (TN, ((), ())), preferred_element_type=F32)
            dv_ref[pl.ds(start, QB), :] += lax.dot_general(wgt.astype(BF16), do2, (TN, ((), ())),
                                                           preferred_element_type=F32)
            return dz, seen + rs, gseen + jnp.sum(dl, axis=1, keepdims=True)

        def step(done, carry, masked, nb):
            dq_acc, seen, gseen = carry
            cs_ = [chain(done + n, masked) for n in range(nb)]
            dzs = []
            for c in cs_:
                dz, seen, gseen = finish(c, seen, gseen)
                dzs.append(dz)
            dq_acc = dq_acc + lax.dot_general(jnp.concatenate(dzs, axis=1), jnp.concatenate([c[1] for c in cs_], axis=0),
                                              (NN, ((), ())), preferred_element_type=F32)
            return dq_acc, seen, gseen

        zc = jnp.zeros((2 * QB, 1), F32)
        dq_acc, _, _ = _sb_passes(i, step, (jnp.zeros((2 * QB, HD), F32), zc, zc))
        dq_ref[...] = jnp.where(first, dq_acc[:QB], dq_acc[QB:]) * SB_SCALE
        end_cargo()

    full_in = lambda off: pl.BlockSpec((t_pad, HD), lambda pr, i: (0, off + pr))
    full_out = pl.BlockSpec((t_pad, HD), lambda pr, i: (0, pr))
    blk = pl.BlockSpec((QB, HD), lambda pr, i: (i, pr))
    sds = jax.ShapeDtypeStruct((t_pad, SB_H * SB_DH), F32)
    return pl.pallas_call(
        body, name=name, grid=(SB_H // 2, nq),
        in_specs=[pl.BlockSpec((QB, HD), lambda pr, i: (i, SB_Q0 + pr)), full_in(SB_K0), full_in(SB_V0), blk,
                  pl.BlockSpec((QB, HD), lambda pr, i: (i, dy_blk0 + pr))] + [ANY] * n,
        out_specs=[blk, full_out, full_out] + [ANY] * n,
        out_shape=[sds, sds, sds] + (exchange[1](cargo) if n else []),
        scratch_shapes=exchange[2](n) if n else [],
        compiler_params=_cparams(),
    )(p, p, p, rtot, dy, *cargo)


HG_LEVELS = 6


def _hg_prefix_matrix():
    t = np.arange(CH)[:, None]
    j = np.arange(CH)[None, :]
    groups = [(j <= t)]
    for lvl in range(1, HG_LEVELS + 1):
        half = CH >> lvl
        e = (t // (2 * half)) * (2 * half) + half - 1
        groups.append(j <= e)
    groups.append(np.ones((8, CH), bool))
    e = np.concatenate(groups, axis=0).astype(np.float32)
    return np.concatenate([e, e, e], axis=1), np.concatenate([e, e, e], axis=0)


HG_G = 4


def _hg_chunk(qr, fr, iv, r0, r1, st, valid, ecat):
    g = st.shape[0]
    mx = jnp.maximum(r0, r1)
    e0 = jnp.exp(r0 - mx)
    e1 = jnp.exp(r1 - mx)
    lb = e1 / (e0 + e1)
    fg = lb + (1.0 - lb) * _sigmoid(fr)
    logf = jnp.where(valid, jnp.log(fg), 0.0)
    kk = jnp.where(valid, 1.0 - fg, 0.0)
    q = jnp.where(valid, _silu(qr), 0.0)
    v = _heads(jnp.where(valid, iv, 0.0), g)

    pre = _mask_dot(ecat, logf)
    b = pre[0:CH]
    b_last = jnp.max(pre[(HG_LEVELS + 1) * CH:], axis=0, keepdims=True)
    row = _iota((CH, 1), 0)
    ri = _iota((1, CH, CH), 1)
    ci = _iota((1, CH, CH), 2)
    a = jnp.where(ri == ci, jnp.sum(_heads(q * kk, g), axis=2, keepdims=True), 0.0)
    for lvl in range(1, HG_LEVELS + 1):
        half = CH >> lvl
        m = pre[lvl * CH:(lvl + 1) * CH]
        low = (row & half) != 0
        dec = jnp.exp(jnp.where(low, b - m, m - b))
        qt = jnp.where(low, q * dec, 0.0)
        kt = jnp.where(low, 0.0, kk * dec)
        same = (ri >> (7 - lvl)) == (ci >> (7 - lvl))
        a = a + jnp.where(same, bbnt(_heads(qt, g), _heads(kt, g)), 0.0)
    o = bbnt(_heads(q * jnp.exp(b), g), st) + bbnn(a, v)
    kd = kk * jnp.exp(b_last - b)
    st_new = st * _heads(jnp.exp(b_last), g) + bbtn(v, _heads(kd, g))
    return o, st_new


def _hg_specs(npair, rev):
    cc = (lambda c: npair - 1 - c) if rev else (lambda c: c)
    ng = HG_H // HG_G
    blk = lambda off: pl.BlockSpec((PAIR, HG_G * HD), lambda h, c: (cc(c), off * ng + h))
    lbs = pl.BlockSpec((2, HG_G * HD), lambda h, c: (0, h))
    state = pl.BlockSpec((1, HG_G, HD, HD), lambda h, c: (cc(c), h, 0, 0))
    return ng, blk, lbs, state


def _hg_fwd(p, lbraw, ecat, pad, name):
    t_pad = p.shape[0]
    npair = t_pad // PAIR
    ng, blk, lbs, state = _hg_specs(npair, False)

    def body(q_ref, f_ref, i_ref, lb_ref, e_ref, et_ref, o_ref, ss_ref, s_ref):
        c = pl.program_id(1)

        @pl.when(c == 0)
        def _():
            s_ref[...] = jnp.zeros_like(s_ref)

        st = s_ref[...]
        ss_ref[0] = st
        for half in (0, 1):
            r = _chunk_rows(half)
            o, st = _hg_chunk(q_ref[r, :], f_ref[r, :], i_ref[r, :], lb_ref[0:1, :], lb_ref[1:2, :], st,
                              _chunk_valid(c, half, pad), (e_ref[...], et_ref[...]))
            _store_heads(o_ref, o, r)
        s_ref[...] = st

    return pl.pallas_call(
        body, name=name, grid=(ng, npair),
        in_specs=[blk(0), blk(1), blk(2), lbs] + [pl.BlockSpec(e.shape, lambda h, c: (0, 0)) for e in ecat],
        out_specs=[blk(0), state],
        out_shape=[jax.ShapeDtypeStruct((t_pad, HG_H * HD), F32), jax.ShapeDtypeStruct((npair, HG_H, HD, HD), F32)],
        scratch_shapes=[pltpu.VMEM((HG_G, HD, HD), F32)],
        compiler_params=_cparams(),
    )(p, p, p, lbraw, *ecat)


def _hg_bwd(p, lbraw, ecat, ssave, do, pad, name, cargo=(), exchange=None):
    t_pad = p.shape[0]
    npair = t_pad // PAIR
    ng, blk, lbs, state = _hg_specs(npair, True)
    n = len(cargo)

    def body(q_ref, f_ref, i_ref, lb_ref, e_ref, et_ref, ss_ref, do_ref, *rest):
        c = pl.program_id(1)
        hg = pl.program_id(0)
        ds_ref = rest[-1]
        (dq_ref, df_ref, di_ref, dlb_ref), end_cargo = _cargo_bounds(
            rest[:-1], n, 4, exchange, (hg == 0) & (c == 0), (hg == ng - 1) & (c == npair - 1))

        @pl.when(c == 0)
        def _():
            ds_ref[...] = jnp.zeros_like(ds_ref)
            dlb_ref[...] = jnp.zeros_like(dlb_ref)

        ra, rb = _chunk_rows(0), _chunk_rows(1)
        va, vb = _chunk_valid(npair - 1 - c, 0, pad), _chunk_valid(npair - 1 - c, 1, pad)
        ecv = (e_ref[...], et_ref[...])

        def pair(qa, fa, ia, qb, fb, ib, r0, r1, st):
            oa, st = _hg_chunk(qa, fa, ia, r0, r1, st, va, ecv)
            ob, st = _hg_chunk(qb, fb, ib, r0, r1, st, vb, ecv)
            return oa, ob, st

        ins = [ref[r, :] for r in (ra, rb) for ref in (q_ref, f_ref, i_ref)]
        _, vjp = jax.vjp(pair, *ins, lb_ref[0:1, :], lb_ref[1:2, :], ss_ref[0])
        g = vjp((_heads(do_ref[ra, :], HG_G), _heads(do_ref[rb, :], HG_G), ds_ref[...]))
        for r, (dq, df, di) in ((ra, g[0:3]), (rb, g[3:6])):
            dq_ref[r, :] = dq
            df_ref[r, :] = df
            di_ref[r, :] = di
        dlb_ref[0:1, :] += g[6]
        dlb_ref[1:2, :] += g[7]
        ds_ref[...] = g[8]
        end_cargo()

    sds = jax.ShapeDtypeStruct((t_pad, HG_H * HD), F32)
    return pl.pallas_call(
        body, name=name, grid=(ng, npair),
        in_specs=[blk(0), blk(1), blk(2), lbs] + [pl.BlockSpec(e.shape, lambda h, c: (0, 0)) for e in ecat]
        + [state, blk(0)] + [ANY] * n,
        out_specs=[blk(0), blk(0), blk(0), lbs] + [ANY] * n,
        out_shape=[sds, sds, sds, jax.ShapeDtypeStruct((2, HG_H * HD), F32)] + (exchange[1](cargo) if n else []),
        scratch_shapes=(exchange[2](n) if n else []) + [pltpu.VMEM((HG_G, HD, HD), F32)],
        compiler_params=_cparams(),
    )(p, p, p, lbraw, *ecat, ssave, do, *cargo)


def _pad_ab_cols(w):
    z = jnp.zeros((w.shape[0], HD - GDN_H), w.dtype)
    return jnp.concatenate([w[:, :2048], w[:, 2048:2052], z, w[:, 2052:2056], z, w[:, 2056:]], axis=1)


def _unpad_ab_cols(w):
    return jnp.concatenate([w[:, :2048], w[:, 2048:2052], w[:, 2176:2180], w[:, 2304:]], axis=1)


def _lane_pad(v):
    return jnp.pad(v, ((0, 0), (0, HD - v.shape[1])))


def _mlp_fwd(hb, w1, w2, layer):
    a, r = _mm(hb, w1, b_view=("cols", layer), out_dtype=BF16, act=True, name=f"mlp_up_{layer}")
    m = _mm(r, w2, b_view=("rows", layer), name=f"mlp_down_{layer}")
    return a, r, m


def _mlp_bwd(hb, a, r, dmb, w1, w2, layer):
    da = _mm(dmb, w2, tb=True, b_view=("rows", layer), out_dtype=BF16, gate=a, name=f"mlp_down_dx_{layer}")
    dw2 = _mm(r, dmb, ta=True, out_dtype=BF16, name=f"mlp_down_dw_{layer}")
    dh = _mm(da, w1, tb=True, b_view=("cols", layer), name=f"mlp_up_dx_{layer}")
    dw1 = _mm(hb, da, ta=True, out_dtype=BF16, out_split=N_CHIP, name=f"mlp_up_dw_{layer}")
    return dh, dw1, dw2


def _local_step(h0, tgt, w, pad, late=None):
    row = lambda a, i: a[i:i + 1]
    ecat = tuple(jnp.asarray(e, dtype=BF16) for e in _hg_prefix_matrix())
    cw = [w["conv_w"][:, i * 512:(i + 1) * 512] for i in range(3)]
    alog, dtb = _lane_pad(w["a_log"]), _lane_pad(w["dt_bias"])

    h0b = h0.astype(BF16)
    p0 = _mm(h0b, w["ab_w_in"], name="ab_in")
    qn = _conv_fwd(p0, 0, cw[0], "q", pad, "conv_q")
    kn = _conv_fwd(p0, 4, cw[1], "k", pad, "conv_k")
    vn = _conv_fwd(p0, 8, cw[2], "v", pad, "conv_v")
    oa_raw, ss0 = _gdn_fwd(qn, kn, vn, p0, alog, dtb, pad, "gdn_fwd")
    oa = _grms_fwd(oa_raw, p0, 12, w["ab_gnorm_g"], "gdn_gate")
    if late is None:
        ob, rtot = _sb_fwd(p0, pad, "sb_fwd")
    else:
        ob, rtot, g_about, g_cin, g_cout, g_w1, g_w2 = _sb_fwd(p0, pad, "sb_fwd", cargo=late, exchange=GATHER)
        w = dict(w, ab_w_out=g_about.reshape(D, D), c_w_in=g_cin, c_w_out=g_cout.reshape(D, D), mlp_w1=g_w1, mlp_w2=g_w2)
    ycat = jnp.concatenate([oa, ob], axis=1)
    mix0 = _mm(ycat, w["ab_w_out"], name="ab_out")
    h1, h1b = _ln_res_fwd(h0, mix0, row(w["ln_mix_g"], 0), row(w["ln_mix_b"], 0), "ln_mix_0")
    a0, r0, m0 = _mlp_fwd(h1b, w["mlp_w1"], w["mlp_w2"], 0)
    h2, h2b = _ln_res_fwd(h1, m0, row(w["ln_ffn_g"], 0), row(w["ln_ffn_b"], 0), "ln_ffn_0")
    p1 = _mm(h2b, w["c_w_in"], b_view=("cols", 0), name="c_in")
    oc_raw, ss1 = _hg_fwd(p1, w["c_lb_raw"], ecat, pad, "hg_fwd")
    yc = _grms_fwd(oc_raw, p1, 3 * HG_H, w["c_gnorm_g"], "hg_gate")
    mix1 = _mm(yc, w["c_w_out"], name="c_out")
    h3, h3b = _ln_res_fwd(h2, mix1, row(w["ln_mix_g"], 1), row(w["ln_mix_b"], 1), "ln_mix_1")
    a1, r1, m1 = _mlp_fwd(h3b, w["mlp_w1"], w["mlp_w2"], 1)
    h4, _ = _ln_res_fwd(h3, m1, row(w["ln_ffn_g"], 1), row(w["ln_ffn_b"], 1), "ln_ffn_1")
    loss, dh4 = _loss_fwd(h4, tgt, pad + N_META, "loss")

    zero = jnp.zeros_like(dh4)
    dh3a, dm1b, dfg1, dfb1 = _ln_res_bwd(h3, m1, row(w["ln_ffn_g"], 1), row(w["ln_ffn_b"], 1), dh4, zero, "ln_ffn_bwd_1")
    dh3b, dw1_1, dw2_1 = _mlp_bwd(h3b, a1, r1, dm1b, w["mlp_w1"], w["mlp_w2"], 1)
    dh2a, dmix1b, dmg1, dmb1 = _ln_res_bwd(h2, mix1, row(w["ln_mix_g"], 1), row(w["ln_mix_b"], 1), dh3a, dh3b, "ln_mix_bwd_1")
    dyc = _mm(dmix1b, w["c_w_out"], tb=True, name="c_out_dx")
    dwco = _mm(yc, dmix1b, ta=True, out_dtype=BF16, name="c_out_dw")
    doc, dzc, dcg = _grms_bwd(oc_raw, p1, 3 * HG_H, w["c_gnorm_g"], dyc, 0, "hg_gate_bwd")
    landed = {}
    rows4 = lambda a: a.reshape(N_CHIP, -1, D)
    if late is None:
        dq1, df1, di1, dlb = _hg_bwd(p1, w["c_lb_raw"], ecat, ss1, doc, pad, "hg_bwd")
    else:
        dq1, df1, di1, dlb, landed["w1_1"], landed["w2_1"], landed["c_w_out"] = _hg_bwd(
            p1, w["c_lb_raw"], ecat, ss1, doc, pad, "hg_bwd", cargo=[dw1_1, rows4(dw2_1), rows4(dwco)], exchange=SCATTER)
    dp1 = _assemble_bf16([(dq1, "cols"), (df1, "cols"), (di1, "cols"), (dzc, "cols")], "c_in_dy")
    dh2b = _mm(dp1, w["c_w_in"], tb=True, b_view=("cols", 0), name="c_in_dx")
    dwc = _mm(h2b, dp1, ta=True, out_dtype=BF16, out_split=N_CHIP, name="c_in_dw")
    dh1a, dm0b, dfg0, dfb0 = _ln_res_bwd(h1, m0, row(w["ln_ffn_g"], 0), row(w["ln_ffn_b"], 0), dh2a, dh2b, "ln_ffn_bwd_0")
    dh1b, dw1_0, dw2_0 = _mlp_bwd(h1b, a0, r0, dm0b, w["mlp_w1"], w["mlp_w2"], 0)
    dh0a, dmix0b, dmg0, dmb0 = _ln_res_bwd(h0, mix0, row(w["ln_mix_g"], 0), row(w["ln_mix_b"], 0), dh1a, dh1b, "ln_mix_bwd_0")
    dycat = _mm(dmix0b, w["ab_w_out"], tb=True, name="ab_out_dx")
    dwabo = _mm(ycat, dmix0b, ta=True, out_dtype=BF16, name="ab_out_dw")
    doa, dza, dag = _grms_bwd(oa_raw, p0, 12, w["ab_gnorm_g"], dycat, 0, "gdn_gate_bwd")
    if late is None:
        dqn, dkn, dvn, dbb, daa, dal, ddt = _gdn_bwd(qn, kn, vn, p0, alog, dtb, ss0, doa, pad, "gdn_bwd")
        dqb, dkb, dvb = _sb_bwd(p0, rtot, dycat, 4, pad, "sb_bwd")
    else:
        dqn, dkn, dvn, dbb, daa, dal, ddt, landed["c_w_in"], landed["w2_0"] = _gdn_bwd(
            qn, kn, vn, p0, alog, dtb, ss0, doa, pad, "gdn_bwd", cargo=[dwc, rows4(dw2_0)], exchange=SCATTER)
        dqb, dkb, dvb, landed["w1_0"], landed["ab_w_out"] = _sb_bwd(
            p0, rtot, dycat, 4, pad, "sb_bwd", cargo=[dw1_0, rows4(dwabo)], exchange=SCATTER)
    dpq, dcq = _conv_bwd(p0, 0, cw[0], dqn, "q", pad, "conv_q_bwd")
    dpk, dck = _conv_bwd(p0, 4, cw[1], dkn, "k", pad, "conv_k_bwd")
    dpv, dcv = _conv_bwd(p0, 8, cw[2], dvn, "v", pad, "conv_v_bwd")
    dp0 = _assemble_bf16([(dpq, "cols"), (dpk, "cols"), (dpv, "cols"), (dza, "cols"), (dbb, "cols"), (daa, "cols"),
                          (dqb, "cols"), (dkb, "cols"), (dvb, "cols")], "ab_in_dy")
    dwab = _mm(h0b, dp0, ta=True, out_dtype=BF16, name="ab_in_dw")
    if late is None:
        dh0b = _mm(dp0, w["ab_w_in"], tb=True, name="ab_in_dx")
    else:
        dab = jnp.transpose(_unpad_ab_cols(dwab).reshape(D, N_CHIP, AB_TRUE // N_CHIP), (1, 0, 2))
        dh0b, landed["ab_w_in"] = _mm(dp0, w["ab_w_in"], tb=True, name="ab_in_dx", cargo=[dab], exchange=SCATTER)
    dh0 = _add2(dh0a, dh0b, "dh0")

    grads = {
        "ab_w_in": dwab, "conv_w": jnp.concatenate([dcq, dck, dcv], axis=1),
        "a_log": dal[:, :GDN_H], "dt_bias": ddt[:, :GDN_H],
        "ab_gnorm_g": dag, "ab_w_out": dwabo, "c_w_in": dwc, "c_lb_raw": dlb, "c_gnorm_g": dcg, "c_w_out": dwco,
        "ln_mix_g": jnp.concatenate([dmg0, dmg1], 0), "ln_mix_b": jnp.concatenate([dmb0, dmb1], 0),
        "w1_0": dw1_0, "w1_1": dw1_1, "w2_0": dw2_0, "w2_1": dw2_1,
        "ln_ffn_g": jnp.concatenate([dfg0, dfg1], 0), "ln_ffn_b": jnp.concatenate([dfb0, dfb1], 0),
        "landed": landed,
    }
    return loss, dh0, grads


MESH = pl.DeviceIdType.MESH
ANY = pl.BlockSpec(memory_space=pl.ANY)
N_CHIP = 4
N_DEV = 8
CHIP_REL = ((1, 0), (0, 1), (1, 1))
DEV_REL = tuple((dx, dy, dc) for dx in (0, 1) for dy in (0, 1) for dc in (0, 1))[1:]

def _pos():
    return lax.axis_index("x"), lax.axis_index("y"), lax.axis_index("c")


def _flip(a, d):
    return a + d - 2 * a * d


class _Exchange:
    def __init__(self, local, sends, recvs):
        self.local, self.sends, self.recvs = local, sends, recvs

    def start(self):
        for cp in self.local + self.sends:
            cp.start()

    def wait(self):
        for cp in self.recvs:
            cp.wait_recv()
        for cp in self.sends:
            cp.wait_send()
        for cp in self.local:
            cp.wait()


def _gather_sems(n):
    return [pltpu.SemaphoreType.DMA((3 * n,)), pltpu.SemaphoreType.DMA((3 * n,)), pltpu.SemaphoreType.DMA((n,))]


def _gather_copies(x_refs, o_refs, send_sems, recv_sems, local_sems):
    n = len(x_refs)
    x, y, c = _pos()
    local = [pltpu.make_async_copy(x_refs[a], o_refs[a].at[2 * x + y], local_sems.at[a]) for a in range(n)]

    def copy(a, k, sending):
        tx, ty = _flip(x, CHIP_REL[k][0]), _flip(y, CHIP_REL[k][1])
        return pltpu.make_async_remote_copy(
            src_ref=x_refs[a], dst_ref=o_refs[a].at[2 * x + y if sending else 2 * tx + ty],
            send_sem=send_sems.at[3 * a + k], recv_sem=recv_sems.at[3 * a + k], device_id=(tx, ty, c), device_id_type=MESH)

    pairs = [(a, k) for a in range(n) for k in range(3)]
    return _Exchange(local, [copy(a, k, True) for a, k in pairs], [copy(a, k, False) for a, k in pairs])


def _gather_shapes(bufs):
    return [jax.ShapeDtypeStruct((N_CHIP,) + b.shape, b.dtype) for b in bufs]


def _chip_allgather(bufs, name):
    n = len(bufs)

    def body(*refs):
        ex = _gather_copies(refs[:n], refs[n:2 * n], *refs[2 * n:])
        ex.start()
        ex.wait()

    return pl.pallas_call(
        body, name=name, in_specs=[ANY] * n, out_specs=[ANY] * n, out_shape=_gather_shapes(bufs),
        scratch_shapes=_gather_sems(n), compiler_params=pltpu.CompilerParams(has_side_effects=True),
    )(*bufs)


def _scatter_sems(n):
    nr = N_DEV - 1
    return [pltpu.SemaphoreType.DMA((nr * n,)), pltpu.SemaphoreType.DMA((nr * n,)), pltpu.SemaphoreType.DMA((n,))]


def _scatter_copies(g_refs, o_refs, send_sems, recv_sems, local_sems):
    n = len(g_refs)
    nr = N_DEV - 1
    x, y, c = _pos()
    me = 4 * x + 2 * y + c
    local = [pltpu.make_async_copy(g_refs[a].at[2 * x + y], o_refs[a].at[me], local_sems.at[a]) for a in range(n)]

    def copy(a, k, sending):
        dx, dy, dc = DEV_REL[k]
        tx, ty, tc = _flip(x, dx), _flip(y, dy), _flip(c, dc)
        return pltpu.make_async_remote_copy(
            src_ref=g_refs[a].at[2 * tx + ty], dst_ref=o_refs[a].at[me if sending else 4 * tx + 2 * ty + tc],
            send_sem=send_sems.at[nr * a + k], recv_sem=recv_sems.at[nr * a + k],
            device_id=(tx, ty, tc), device_id_type=MESH)

    pairs = [(a, k) for a in range(n) for k in range(nr)]
    return _Exchange(local, [copy(a, k, True) for a, k in pairs], [copy(a, k, False) for a, k in pairs])


def _scatter_shapes(gs):
    return [jax.ShapeDtypeStruct((N_DEV,) + g.shape[1:], g.dtype) for g in gs]


GATHER = (_gather_copies, _gather_shapes, _gather_sems)
SCATTER = (_scatter_copies, _scatter_shapes, _scatter_sems)


def _sum_slots(r, name):
    n, rh, w = r.shape
    tr = _pick(rh, (256, 128, 64, 16))

    def body(r_ref, o_ref):
        acc = r_ref[0].astype(F32)
        for s in range(1, n):
            acc = acc + r_ref[s].astype(F32)
        o_ref[...] = acc

    return pl.pallas_call(
        body, name=name, grid=(rh // tr,), in_specs=[pl.BlockSpec((n, tr, w), lambda i: (0, i, 0))],
        out_specs=pl.BlockSpec((tr, w), lambda i: (i, 0)), out_shape=jax.ShapeDtypeStruct((rh, w), F32),
        compiler_params=_cparams(),
    )(r)


def _small_allreduce(buf, name):
    r, w = buf.shape

    def body(b_ref, o_ref, land_ref, send_sems, recv_sems):
        x, y, c = _pos()
        me = 4 * x + 2 * y + c
        land_ref[me] = b_ref[...]

        def target(k):
            dx, dy, dc = DEV_REL[k]
            return _flip(x, dx), _flip(y, dy), _flip(c, dc)

        sends = []
        for k in range(N_DEV - 1):
            tx, ty, tc = target(k)
            cp = pltpu.make_async_remote_copy(
                src_ref=b_ref, dst_ref=land_ref.at[me], send_sem=send_sems.at[k], recv_sem=recv_sems.at[k],
                device_id=(tx, ty, tc), device_id_type=MESH)
            cp.start()
            sends.append(cp)
        for k in range(N_DEV - 1):
            tx, ty, tc = target(k)
            pltpu.make_async_remote_copy(
                src_ref=b_ref, dst_ref=land_ref.at[4 * tx + 2 * ty + tc], send_sem=send_sems.at[k],
                recv_sem=recv_sems.at[k], device_id=(tx, ty, tc), device_id_type=MESH).wait_recv()
        for cp in sends:
            cp.wait_send()
        acc = land_ref[0]
        for s in range(1, N_DEV):
            acc = acc + land_ref[s]
        o_ref[...] = acc

    vm = pl.BlockSpec(memory_space=pltpu.VMEM)
    return pl.pallas_call(
        body, name=name, in_specs=[vm], out_specs=vm, out_shape=jax.ShapeDtypeStruct((r, w), F32),
        scratch_shapes=[pltpu.VMEM((N_DEV, r, w), F32), pltpu.SemaphoreType.DMA((N_DEV - 1,)),
                        pltpu.SemaphoreType.DMA((N_DEV - 1,))],
        compiler_params=pltpu.CompilerParams(has_side_effects=True),
    )(buf)


def _adamw(w, g, m, v, name):
    r, c = w.shape
    tr = _pick(r, (256, 128, 64, 8)) if r * c > (1 << 18) else r

    def body(w_ref, g_ref, m_ref, v_ref, d_ref, m2_ref, v2_ref):
        gg = g_ref[...]
        m2 = ADAM_B1 * m_ref[...] + (1.0 - ADAM_B1) * gg
        v2 = ADAM_B2 * v_ref[...] + (1.0 - ADAM_B2) * (gg * gg)
        m_hat = m2 / (1.0 - ADAM_B1 ** ADAM_STEP)
        v_hat = v2 / (1.0 - ADAM_B2 ** ADAM_STEP)
        d_ref[...] = -ADAM_LR * (m_hat / (jnp.sqrt(v_hat) + ADAM_EPS) + ADAM_WD * w_ref[...])
        m2_ref[...] = m2
        v2_ref[...] = v2

    blk = pl.BlockSpec((tr, c), lambda i: (i, 0))
    sds = jax.ShapeDtypeStruct((r, c), F32)
    return pl.pallas_call(body, name=name, grid=(r // tr,), in_specs=[blk] * 4, out_specs=[blk] * 3,
                          out_shape=[sds] * 3, compiler_params=_cparams())(w, g, m, v)


BIG = ("ab_w_in", "ab_w_out", "c_w_in", "c_w_out", "mlp_w1", "mlp_w2")
SMALL = ("ln_mix_g", "ln_mix_b", "ln_ffn_g", "ln_ffn_b", "c_lb_raw", "ab_a_log", "ab_dt_bias", "ab_gnorm_g", "c_gnorm_g")
SMALL_ROWS = 16
CONV_ROWS = 8
CONV_W = 3 * GDN_H * HD


def _conv_to_rows(cw):
    return jnp.pad(cw, ((0, 0), (0, 2 * D - CONV_W))).reshape(CONV_ROWS, D)


def _rows_to_conv(rows):
    return rows.reshape(CONV_K, 2 * D)[:, :CONV_W]


def _pack_small(d):
    rows = [jnp.pad(d[n], ((0, 0), (0, D - d[n].shape[1]))) for n in SMALL]
    buf = jnp.concatenate(rows, axis=0)
    return jnp.pad(buf, ((0, SMALL_ROWS - buf.shape[0]), (0, 0)))


def _unpack_small(buf, like):
    out, r = {}, 0
    for n in SMALL:
        nr, nc = like[n].shape
        out[n] = buf[r:r + nr, :nc]
        r += nr
    return out


def kernel(x, meta_tokens, ab_w_in, ab_conv_w, ab_a_log, ab_dt_bias, ab_gnorm_g, ab_w_out, c_w_in, c_lb_raw, c_gnorm_g, c_w_out, ln_mix_g, ln_mix_b, mlp_w1, mlp_w2, ln_ffn_g, ln_ffn_b, loss_target, m_meta_tokens, m_ab_w_in, m_ab_conv_w, m_ab_a_log, m_ab_dt_bias, m_ab_gnorm_g, m_ab_w_out, m_c_w_in, m_c_lb_raw, m_c_gnorm_g, m_c_w_out, m_ln_mix_g, m_ln_mix_b, m_mlp_w1, m_mlp_w2, m_ln_ffn_g, m_ln_ffn_b, v_meta_tokens, v_ab_w_in, v_ab_conv_w, v_ab_a_log, v_ab_dt_bias, v_ab_gnorm_g, v_ab_w_out, v_c_w_in, v_c_lb_raw, v_c_gnorm_g, v_c_w_out, v_ln_mix_g, v_ln_mix_b, v_mlp_w1, v_mlp_w2, v_ln_ffn_g, v_ln_ffn_b):
    names = ("meta_tokens", "ab_w_in", "ab_conv_w", "ab_a_log", "ab_dt_bias", "ab_gnorm_g", "ab_w_out", "c_w_in",
             "c_lb_raw", "c_gnorm_g", "c_w_out", "ln_mix_g", "ln_mix_b", "mlp_w1", "mlp_w2", "ln_ffn_g", "ln_ffn_b")
    wts = dict(zip(names, (meta_tokens, ab_w_in, ab_conv_w, ab_a_log, ab_dt_bias, ab_gnorm_g, ab_w_out, c_w_in, c_lb_raw,
                           c_gnorm_g, c_w_out, ln_mix_g, ln_mix_b, mlp_w1, mlp_w2, ln_ffn_g, ln_ffn_b)))
    mom_m = dict(zip(names, (m_meta_tokens, m_ab_w_in, m_ab_conv_w, m_ab_a_log, m_ab_dt_bias, m_ab_gnorm_g, m_ab_w_out,
                             m_c_w_in, m_c_lb_raw, m_c_gnorm_g, m_c_w_out, m_ln_mix_g, m_ln_mix_b, m_mlp_w1, m_mlp_w2,
                             m_ln_ffn_g, m_ln_ffn_b)))
    mom_v = dict(zip(names, (v_meta_tokens, v_ab_w_in, v_ab_conv_w, v_ab_a_log, v_ab_dt_bias, v_ab_gnorm_g, v_ab_w_out,
                             v_c_w_in, v_c_lb_raw, v_c_gnorm_g, v_c_w_out, v_ln_mix_g, v_ln_mix_b, v_mlp_w1, v_mlp_w2,
                             v_ln_ffn_g, v_ln_ffn_b)))
    seq = x.shape[1]
    pad = (-(N_META + seq)) % QB
    xi, yi, ci = _pos()
    chip = 2 * xi + yi

    gat_ab_in, = _chip_allgather([ab_w_in[0].astype(BF16)], "gather_weights")
    late = [ab_w_out[0].astype(BF16), c_w_in.astype(BF16), c_w_out[0].astype(BF16), mlp_w1.astype(BF16),
            mlp_w2.astype(BF16)]
    mcols, ccols = meta_tokens.shape[1], ab_conv_w.shape[2]
    place = jnp.concatenate([
        lax.dynamic_update_slice(jnp.zeros((N_META, D), F32), 0.5 * meta_tokens, (0, chip * mcols)),
        _conv_to_rows(lax.dynamic_update_slice(jnp.zeros((CONV_K, CONV_W), F32), 0.5 * ab_conv_w[0], (0, chip * ccols)))],
        axis=0)
    placed = _small_allreduce(place, "gather_meta")
    meta_full = placed[:N_META]

    w = {
        "ab_w_in": _pad_ab_cols(jnp.transpose(gat_ab_in, (1, 0, 2)).reshape(D, AB_TRUE)),
        "conv_w": _rows_to_conv(placed[N_META:]), "a_log": ab_a_log, "dt_bias": ab_dt_bias,
        "ab_gnorm_g": ab_gnorm_g, "c_lb_raw": c_lb_raw,
        "c_gnorm_g": c_gnorm_g, "ln_mix_g": ln_mix_g, "ln_mix_b": ln_mix_b, "ln_ffn_g": ln_ffn_g, "ln_ffn_b": ln_ffn_b,
    }

    h0 = jnp.concatenate([jnp.zeros((pad, D), F32), meta_full, x[0]], axis=0)
    tgt = jnp.concatenate([jnp.zeros((pad + N_META, D), F32), loss_target[0]], axis=0)
    loss8, dh0, g = _local_step(h0, tgt, w, pad, late)
    loss = lax.psum(loss8[0, 0], ("x", "y", "c"))
    grad_x = dh0[pad + N_META:][None]

    gsmall = {"ln_mix_g": g["ln_mix_g"], "ln_mix_b": g["ln_mix_b"], "ln_ffn_g": g["ln_ffn_g"], "ln_ffn_b": g["ln_ffn_b"],
              "c_lb_raw": g["c_lb_raw"], "ab_a_log": g["a_log"], "ab_dt_bias": g["dt_bias"], "ab_gnorm_g": g["ab_gnorm_g"],
              "c_gnorm_g": g["c_gnorm_g"]}
    sbuf = jnp.concatenate([_pack_small(gsmall), dh0[pad:pad + N_META], _conv_to_rows(g["conv_w"])], axis=0)
    ssum = _small_allreduce(sbuf, "allreduce_small")
    grads = _unpack_small(ssum[:SMALL_ROWS], wts)
    grads["meta_tokens"] = lax.dynamic_slice(ssum[SMALL_ROWS:SMALL_ROWS + N_META], (0, chip * mcols), (N_META, mcols))
    grads["ab_conv_w"] = lax.dynamic_slice(_rows_to_conv(ssum[SMALL_ROWS + N_META:]), (0, chip * ccols), (CONV_K, ccols))[None]

    sums = {k: _sum_slots(v, f"grad_sum_{k}") for k, v in g["landed"].items()}
    for n in ("ab_w_in", "ab_w_out", "c_w_in", "c_w_out"):
        grads[n] = sums[n][None]
    grads["mlp_w1"] = jnp.stack([sums["w1_0"], sums["w1_1"]])
    grads["mlp_w2"] = jnp.stack([sums["w2_0"], sums["w2_1"]])

    delta, new_m, new_v = {}, {}, {}
    for n in ("meta_tokens", "ab_conv_w") + BIG:
        shp = wts[n].shape
        to2 = lambda a: a.reshape(-1, shp[-1])
        d2, m2, v2 = _adamw(to2(wts[n]), to2(grads[n]), to2(mom_m[n]), to2(mom_v[n]), f"adamw_{n}")
        delta[n], new_m[n], new_v[n] = d2.reshape(shp), m2.reshape(shp), v2.reshape(shp)
    d2, m2, v2 = _adamw(_pack_small(wts), ssum[:SMALL_ROWS], _pack_small(mom_m), _pack_small(mom_v), "adamw_small")
    delta.update(_unpack_small(d2, wts))
    new_m.update(_unpack_small(m2, wts))
    new_v.update(_unpack_small(v2, wts))

    return (loss, grad_x, *[grads[n] for n in names], *[delta[n] for n in names], *[new_m[n] for n in names],
            *[new_v[n] for n in names])
```

```python
import functools
import math

import numpy as np
import jax
import jax.numpy as jnp
from jax import lax
from jax.experimental import pallas as pl
from jax.experimental.pallas import tpu as pltpu

F32 = jnp.float32
BF16 = jnp.bfloat16

D = 1024
N_META = 16
D_FF = 4 * D
DEPTH = 2
GDN_H = 4
SB_H = 8
SB_DH = 64
HG_H = 8
HD = 128
CH = 64
QB = 128
ALPHA = float((2 * DEPTH) ** 0.25)
LN_EPS = 1e-5
RMS_EPS = 1e-6
L2_EPS = 1e-6
NEG = -1e30

ADAM_LR = 0.001
ADAM_B1 = 0.9
ADAM_B2 = 0.999
ADAM_EPS = 1e-08
ADAM_WD = 0.01
ADAM_STEP = 10

AB_W = 30 * HD
AB_TRUE = 3592
VMEM_LIMIT = 56 * 1024 * 1024

NN = ((1,), (0,))
NT = ((1,), (1,))
TN = ((0,), (0,))


def _cparams(**kw):
    return pltpu.CompilerParams(vmem_limit_bytes=VMEM_LIMIT, **kw)


def _dg(a, b, dims, mode):
    if mode == "h":
        return lax.dot_general(a, b, dims, precision=lax.Precision.HIGHEST, preferred_element_type=F32)
    if mode == "b":
        return lax.dot_general(a.astype(BF16), b.astype(BF16), dims, preferred_element_type=F32)
    ah, bh = a.astype(BF16), b.astype(BF16)
    al, bl = (a - ah.astype(F32)).astype(BF16), (b - bh.astype(F32)).astype(BF16)
    d = lambda x, y: lax.dot_general(x, y, dims, preferred_element_type=F32)
    return d(ah, bh) + (d(ah, bl) + d(al, bh))


def _make_dots(mode, batched=False):
    if batched:
        nn_d, nt_d, tn_d = (((2,), (1,)), ((0,), (0,))), (((2,), (2,)), ((0,), (0,))), (((1,), (1,)), ((0,), (0,)))
    else:
        nn_d, nt_d, tn_d = (NN, ((), ())), (NT, ((), ())), (TN, ((), ()))

    @jax.custom_vjp
    def nn(a, b):
        return _dg(a, b, nn_d, mode)

    @jax.custom_vjp
    def nt(a, b):
        return _dg(a, b, nt_d, mode)

    @jax.custom_vjp
    def tn(a, b):
        return _dg(a, b, tn_d, mode)

    nn.defvjp(lambda a, b: (nn(a, b), (a, b)), lambda r, g: (nt(g, r[1]), tn(r[0], g)))
    nt.defvjp(lambda a, b: (nt(a, b), (a, b)), lambda r, g: (nn(g, r[1]), tn(g, r[0])))
    tn.defvjp(lambda a, b: (tn(a, b), (a, b)), lambda r, g: (nt(r[1], g), nn(r[0], g)))
    return nn, nt, tn


hnn, hnt, htn = _make_dots("h")
bbnn, bbnt, bbtn = _make_dots("b", True)
mbnn, mbnt, mbtn = _make_dots("m", True)
hbnn, hbnt, hbtn = _make_dots("h", True)


def _split3(x, axis):
    x1 = x.astype(BF16)
    r1 = x - x1.astype(F32)
    x2 = r1.astype(BF16)
    x3 = (r1 - x2.astype(F32)).astype(BF16)
    return jnp.concatenate([x1, x2, x3], axis=axis)


@jax.custom_vjp
def _mask_dot(e3, x):
    return lax.dot_general(e3[0], _split3(x, 0), (NN, ((), ())), preferred_element_type=F32)


def _mask_dot_bwd(e3, g):
    dx = lax.dot_general(e3[1], _split3(g, 0), (TN, ((), ())), preferred_element_type=F32)
    return (jnp.zeros_like(e3[0]), jnp.zeros_like(e3[1])), dx


_mask_dot.defvjp(lambda e3, x: (_mask_dot(e3, x), e3), _mask_dot_bwd)


def _heads(a, n):
    return jnp.concatenate([a[None, :, h * HD:(h + 1) * HD] for h in range(n)], axis=0)


def _sigmoid(x):
    return jax.nn.sigmoid(x)


def _silu(x):
    return x * jax.nn.sigmoid(x)


def _softplus(x):
    return jnp.maximum(x, 0.0) + jnp.log(1.0 + jnp.exp(-jnp.abs(x)))


def _iota(shape, dim):
    return lax.broadcasted_iota(jnp.int32, shape, dim)


def _pick(n, prefs):
    for p in prefs:
        if n % p == 0:
            return p
    return n


def _mm(a, b, *, ta=False, tb=False, out_dtype=F32, name, b_view=None, out_split=0, act=False, gate=None,
        cargo=(), exchange=None):
    if ta:
        k_dim, m_dim = a.shape
    else:
        m_dim, k_dim = a.shape
    if b_view is None:
        w_rows, w_cols = b.shape
    else:
        kind, layer = b_view
        nj, _, blk_r, blk_c = b.shape
        w_rows, w_cols = (blk_r, nj * blk_c) if kind == "cols" else (nj * blk_r, blk_c)
    n_dim = w_rows if tb else w_cols
    assert (w_cols if tb else w_rows) == k_dim
    tm = _pick(m_dim, (1024, 1056, 704, 640, 512, 384, 256, 128))
    tn = _pick(n_dim, (1024, 1056, 704, 640, 512, 384, 256, 128))
    tk = _pick(k_dim, (1024, 1056, 704, 512, 384, 256, 128))
    nk = k_dim // tk
    a_spec = pl.BlockSpec((tk, tm), lambda i, j, k: (k, i)) if ta else pl.BlockSpec((tm, tk), lambda i, j, k: (i, k))
    wb = (tn, tk) if tb else (tk, tn)
    w_idx = (lambda i, j, k: (j, k)) if tb else (lambda i, j, k: (k, j))
    if b_view is None:
        b_spec = pl.BlockSpec(wb, w_idx)
    elif kind == "cols":
        per = blk_c // wb[1]
        b_spec = pl.BlockSpec((None, None) + wb,
                              lambda i, j, k: (w_idx(i, j, k)[1] // per, layer, w_idx(i, j, k)[0], w_idx(i, j, k)[1] % per))
    else:
        per = blk_r // wb[0]
        b_spec = pl.BlockSpec((None, None) + wb,
                              lambda i, j, k: (w_idx(i, j, k)[0] // per, layer, w_idx(i, j, k)[0] % per, w_idx(i, j, k)[1]))
    if out_split:
        per_o = (n_dim // out_split) // tn
        out_spec = pl.BlockSpec((None, tm, tn), lambda i, j, k: (j // per_o, i, j % per_o))
        out_sds = jax.ShapeDtypeStruct((out_split, m_dim, n_dim // out_split), out_dtype)
    else:
        out_spec = pl.BlockSpec((tm, tn), lambda i, j, k: (i, j))
        out_sds = jax.ShapeDtypeStruct((m_dim, n_dim), out_dtype)
    dims = (((0 if ta else 1,), (1 if tb else 0,)), ((), ()))
    extra = [] if gate is None else [gate]
    n_out = 2 if act else 1

    def finish(acc, refs):
        if act:
            refs[0][...] = acc.astype(refs[0].dtype)
            r = jnp.maximum(acc, 0.0)
            refs[1][...] = (r * r).astype(refs[1].dtype)
        elif gate is not None:
            refs[1][...] = (acc * (2.0 * jnp.maximum(refs[0][...].astype(F32), 0.0))).astype(refs[1].dtype)
        else:
            refs[0][...] = acc.astype(refs[0].dtype)

    grid = (m_dim // tm, n_dim // tn, nk)
    nc = len(cargo)

    def body(a_ref, b_ref, *rest):
        acc_ref = rest[-1]
        ids = [pl.program_id(d) for d in range(3)]
        outs, end_cargo = _cargo_bounds(
            rest[len(extra):-1], nc, n_out, exchange, (ids[0] == 0) & (ids[1] == 0) & (ids[2] == 0),
            (ids[0] == grid[0] - 1) & (ids[1] == grid[1] - 1) & (ids[2] == grid[2] - 1))
        refs = tuple(rest[:len(extra)]) + tuple(outs)
        part = lax.dot_general(a_ref[...], b_ref[...], dims, preferred_element_type=F32)
        if nk == 1:
            finish(part, refs)
        else:
            k = ids[2]

            @pl.when(k == 0)
            def _():
                acc_ref[...] = part

            @pl.when(k > 0)
            def _():
                acc_ref[...] += part

            @pl.when(k == nk - 1)
            def _():
                finish(acc_ref[...], refs)
        end_cargo()

    out = pl.pallas_call(
        body, name=name, grid=grid,
        in_specs=[a_spec, b_spec] + [pl.BlockSpec((tm, tn), lambda i, j, k: (i, j))] * len(extra) + [ANY] * nc,
        out_specs=[out_spec] * n_out + [ANY] * nc,
        out_shape=[out_sds] * n_out + (exchange[1](cargo) if nc else []),
        scratch_shapes=(exchange[2](nc) if nc else []) + [pltpu.VMEM((tm, tn) if nk > 1 else (8, 128), F32)],
        compiler_params=_cparams(dimension_semantics=("arbitrary",) * 3 if nc else ("parallel", "parallel", "arbitrary")),
    )(a, b, *extra, *cargo)
    if nc:
        return out
    return out if act else out[0]


def _mm_groups_nt(parts, b, name):
    m_dim, k_dim = parts[0].shape
    ng, _, n_dim, _ = b.shape
    assert len(parts) == ng and b.shape[3] == k_dim
    tm = _pick(m_dim, (1056, 704, 512, 384, 256, 128))

    def body(*refs):
        a_refs, b_ref, o_ref, acc_ref = refs[:ng], refs[ng], refs[ng + 1], refs[ng + 2]
        k = pl.program_id(1)
        for g in range(ng):
            @pl.when(k == g)
            def _(g=g):
                part = lax.dot_general(a_refs[g][...], b_ref[...], (NT, ((), ())), preferred_element_type=F32)
                if g == 0:
                    acc_ref[...] = part
                elif g < ng - 1:
                    acc_ref[...] += part
                else:
                    o_ref[...] = acc_ref[...] + part

    return pl.pallas_call(
        body, name=name, grid=(m_dim // tm, ng),
        in_specs=[pl.BlockSpec((tm, k_dim), lambda i, k: (i, 0))] * ng
        + [pl.BlockSpec((None, None, n_dim, k_dim), lambda i, k: (k, 0, 0, 0))],
        out_specs=pl.BlockSpec((tm, n_dim), lambda i, k: (i, 0)),
        out_shape=jax.ShapeDtypeStruct((m_dim, n_dim), F32),
        scratch_shapes=[pltpu.VMEM((tm, n_dim), F32)],
        compiler_params=_cparams(dimension_semantics=("parallel", "arbitrary")),
    )(*parts, b)


def _row_tile(t_pad, width):
    for tr in (528, 352, 176, 128, 64):
        if t_pad % tr == 0 and tr * width * 4 <= (3 << 19) and tr % 16 == 0:
            return tr
    return 64 if t_pad % 64 == 0 else t_pad


def _ln_res_fn(h, m, g, b):
    x = ALPHA * h + m
    mu = jnp.mean(x, axis=-1, keepdims=True)
    xc = x - mu
    var = jnp.mean(xc * xc, axis=-1, keepdims=True)
    return xc * lax.rsqrt(var + LN_EPS) * g + b


def _ln_res_fwd(h, m, g, b, name):
    t_pad = h.shape[0]
    tr = _row_tile(t_pad, D)

    def body(h_ref, m_ref, g_ref, b_ref, y_ref, yb_ref):
        y = _ln_res_fn(h_ref[...], m_ref[...], g_ref[...], b_ref[...])
        y_ref[...] = y
        yb_ref[...] = y.astype(BF16)

    row = pl.BlockSpec((tr, D), lambda i: (i, 0))
    par = pl.BlockSpec((1, D), lambda i: (0, 0))
    return pl.pallas_call(
        body, name=name, grid=(t_pad // tr,), in_specs=[row, row, par, par], out_specs=[row, row],
        out_shape=[jax.ShapeDtypeStruct((t_pad, D), F32), jax.ShapeDtypeStruct((t_pad, D), BF16)],
        compiler_params=_cparams(),
    )(h, m, g, b)


def _ln_res_bwd(h, m, g, b, dy1, dy2, name):
    t_pad = h.shape[0]
    tr = _row_tile(t_pad, D)

    def body(h_ref, m_ref, g_ref, b_ref, d1_ref, d2_ref, dh_ref, dm_ref, dg_ref, db_ref):
        _, vjp = jax.vjp(_ln_res_fn, h_ref[...], m_ref[...], g_ref[...], b_ref[...])
        dh, dm, dg, db = vjp(d1_ref[...] + d2_ref[...])
        dh_ref[...] = dh
        dm_ref[...] = dm.astype(BF16)

        @pl.when(pl.program_id(0) == 0)
        def _():
            dg_ref[...] = jnp.zeros_like(dg_ref)
            db_ref[...] = jnp.zeros_like(db_ref)

        dg_ref[...] += dg
        db_ref[...] += db

    row = pl.BlockSpec((tr, D), lambda i: (i, 0))
    par = pl.BlockSpec((1, D), lambda i: (0, 0))
    return pl.pallas_call(
        body, name=name, grid=(t_pad // tr,), in_specs=[row, row, par, par, row, row],
        out_specs=[row, row, par, par],
        out_shape=[jax.ShapeDtypeStruct((t_pad, D), F32), jax.ShapeDtypeStruct((t_pad, D), BF16),
                   jax.ShapeDtypeStruct((1, D), F32), jax.ShapeDtypeStruct((1, D), F32)],
        compiler_params=_cparams(),
    )(h, m, g, b, dy1, dy2)


def _grms_fn(o, z, g):
    y = o * lax.rsqrt(jnp.mean(o * o, axis=-1, keepdims=True) + RMS_EPS) * g
    return y * _silu(z)


def _grms_fwd(o, z_arr, z_blk0, g, name):
    t_pad, w = o.shape
    tr = _row_tile(t_pad, w)
    assert (z_blk0 * HD) % w == 0

    def body(o_ref, z_ref, g_ref, y_ref):
        for h in range(w // HD):
            c = slice(h * HD, (h + 1) * HD)
            y_ref[:, c] = _grms_fn(o_ref[:, c], z_ref[:, c], g_ref[...]).astype(BF16)

    return pl.pallas_call(
        body, name=name, grid=(t_pad // tr,),
        in_specs=[pl.BlockSpec((tr, w), lambda i: (i, 0)), pl.BlockSpec((tr, w), lambda i: (i, z_blk0 * HD // w)),
                  pl.BlockSpec((1, HD), lambda i: (0, 0))],
        out_specs=pl.BlockSpec((tr, w), lambda i: (i, 0)),
        out_shape=jax.ShapeDtypeStruct((t_pad, w), BF16), compiler_params=_cparams(),
    )(o, z_arr, g)


def _grms_bwd(o, z_arr, z_blk0, g, dy_arr, dy_blk0, name):
    t_pad, w = o.shape
    tr = _row_tile(t_pad, w)
    assert (z_blk0 * HD) % w == 0 and (dy_blk0 * HD) % w == 0

    def body(o_ref, z_ref, g_ref, dy_ref, do_ref, dz_ref, dg_ref):
        @pl.when(pl.program_id(0) == 0)
        def _():
            dg_ref[...] = jnp.zeros_like(dg_ref)

        for h in range(w // HD):
            c = slice(h * HD, (h + 1) * HD)
            _, vjp = jax.vjp(_grms_fn, o_ref[:, c], z_ref[:, c], g_ref[...])
            do, dz, dg = vjp(dy_ref[:, c])
            do_ref[:, c] = do
            dz_ref[:, c] = dz.astype(BF16)
            dg_ref[...] += dg

    blk = pl.BlockSpec((tr, w), lambda i: (i, 0))
    return pl.pallas_call(
        body, name=name, grid=(t_pad // tr,),
        in_specs=[blk, pl.BlockSpec((tr, w), lambda i: (i, z_blk0 * HD // w)), pl.BlockSpec((1, HD), lambda i: (0, 0)),
                  pl.BlockSpec((tr, w), lambda i: (i, dy_blk0 * HD // w))],
        out_specs=[blk, blk, pl.BlockSpec((1, HD), lambda i: (0, 0))],
        out_shape=[jax.ShapeDtypeStruct((t_pad, w), F32), jax.ShapeDtypeStruct((t_pad, w), BF16),
                   jax.ShapeDtypeStruct((1, HD), F32)],
        compiler_params=_cparams(),
    )(o, z_arr, g, dy_arr)


def _loss_fwd(y, tgt, first_row, name):
    t_pad = y.shape[0]
    tr = _row_tile(t_pad, D)

    def body(y_ref, t_ref, l_ref, dy_ref):
        rows = pl.program_id(0) * tr + _iota((tr, 1), 0)
        err = jnp.where(rows >= first_row, y_ref[...] - t_ref[...], 0.0)
        dy_ref[...] = err * (1.0 / D)

        @pl.when(pl.program_id(0) == 0)
        def _():
            l_ref[...] = jnp.zeros_like(l_ref)

        part = jnp.sum(jnp.sum(err * err, axis=1, keepdims=True), axis=0, keepdims=True)
        l_ref[...] += jnp.broadcast_to(part * (0.5 / D), l_ref.shape)

    row = pl.BlockSpec((tr, D), lambda i: (i, 0))
    return pl.pallas_call(
        body, name=name, grid=(t_pad // tr,), in_specs=[row, row],
        out_specs=[pl.BlockSpec((8, 128), lambda i: (0, 0)), row],
        out_shape=[jax.ShapeDtypeStruct((8, 128), F32), jax.ShapeDtypeStruct((t_pad, D), F32)],
        compiler_params=_cparams(),
    )(y, tgt)


def _add2(a, b, name):
    t_pad, w = a.shape
    tr = _row_tile(t_pad, w)

    def body(a_ref, b_ref, o_ref):
        o_ref[...] = a_ref[...] + b_ref[...]

    row = pl.BlockSpec((tr, w), lambda i: (i, 0))
    return pl.pallas_call(body, name=name, grid=(t_pad // tr,), in_specs=[row, row], out_specs=row,
                          out_shape=jax.ShapeDtypeStruct((t_pad, w), F32), compiler_params=_cparams())(a, b)


def _assemble_bf16(parts, name):
    t_pad = parts[0][0].shape[0] if parts[0][1] == "cols" else parts[0][0].shape[1]
    widths = [p.shape[1] if kind == "cols" else HD for p, kind in parts]
    total = sum(widths)
    tr = _row_tile(t_pad, total)

    def body(*refs):
        o_ref = refs[-1]
        off = 0
        for ref, (p, kind), w in zip(refs[:-1], parts, widths):
            if kind == "cols":
                o_ref[:, off:off + w] = ref[...].astype(BF16)
            else:
                acc = ref[0]
                for hh in range(1, p.shape[0]):
                    acc = acc + ref[hh]
                o_ref[:, off:off + w] = acc.astype(BF16)
            off += w

    specs = []
    for p, kind in parts:
        if kind == "cols":
            specs.append(pl.BlockSpec((tr, p.shape[1]), lambda i: (i, 0)))
        else:
            specs.append(pl.BlockSpec((p.shape[0], tr, HD), lambda i: (0, i, 0)))
    return pl.pallas_call(
        body, name=name, grid=(t_pad // tr,), in_specs=specs,
        out_specs=pl.BlockSpec((tr, total), lambda i: (i, 0)),
        out_shape=jax.ShapeDtypeStruct((t_pad, total), BF16), compiler_params=_cparams(),
    )(*[p for p, _ in parts])


CONV_K = 4
HALO = 8
RT = 128


def _conv_fwd(p, blk0, w, mode, pad, name):
    t_pad = p.shape[0]
    nt = t_pad // RT
    scale = HD ** -0.5 if mode == "q" else 1.0

    def body(x_ref, w_ref, y_ref, xs_ref):
        xs_ref[0:HALO, :] = jnp.zeros((HALO, HD), F32)
        rows = _iota((t_pad, 1), 0)
        xs_ref[HALO:HALO + t_pad, :] = jnp.where(rows >= pad, x_ref[...], 0.0)
        wv = w_ref[...]

        def tile(i, carry):
            r0 = pl.multiple_of(i * RT, RT)
            ext = xs_ref[pl.ds(r0, RT + HALO), :]
            acc = ext[HALO:, :] * wv[3:4, :]
            for s in (1, 2, 3):
                acc = acc + pltpu.roll(ext, s, 0)[HALO:, :] * wv[3 - s:4 - s, :]
            y = _silu(acc)
            if mode != "v":
                y = y * lax.rsqrt(jnp.sum(y * y, axis=-1, keepdims=True) + L2_EPS) * scale
            y_ref[pl.ds(r0, RT), :] = y
            return carry

        lax.fori_loop(0, nt, tile, 0)

    return pl.pallas_call(
        body, name=name, grid=(GDN_H,),
        in_specs=[pl.BlockSpec((t_pad, HD), lambda h: (0, blk0 + h)), pl.BlockSpec((CONV_K, HD), lambda h: (0, h))],
        out_specs=pl.BlockSpec((t_pad, HD), lambda h: (0, h)),
        out_shape=jax.ShapeDtypeStruct((t_pad, GDN_H * HD), F32),
        scratch_shapes=[pltpu.VMEM((t_pad + HALO, HD), F32)],
        compiler_params=_cparams(),
    )(p, w)


def _conv_bwd(p, blk0, w, dn, mode, pad, name):
    t_pad = p.shape[0]
    nt = t_pad // RT
    scale = HD ** -0.5 if mode == "q" else 1.0

    def body(x_ref, w_ref, dn_ref, dx_ref, dw_ref, xs_ref, ds_ref):
        xs_ref[0:HALO, :] = jnp.zeros((HALO, HD), F32)
        xs_ref[HALO + t_pad:HALO + t_pad + 2 * HALO, :] = jnp.zeros((2 * HALO, HD), F32)
        ds_ref[t_pad:t_pad + HALO, :] = jnp.zeros((HALO, HD), F32)
        rows = _iota((t_pad, 1), 0)
        xs_ref[HALO:HALO + t_pad, :] = jnp.where(rows >= pad, x_ref[...], 0.0)
        ds_ref[0:t_pad, :] = dn_ref[...]
        wv = w_ref[...]

        def tile(i, dw):
            r0 = pl.multiple_of(i * RT, RT)
            ext = xs_ref[pl.ds(r0, RT + 2 * HALO), :]
            dn_e = ds_ref[pl.ds(r0, RT + HALO), :]
            xsh = [ext[HALO:, :]] + [pltpu.roll(ext, s, 0)[HALO:, :] for s in (1, 2, 3)]
            pre = xsh[0] * wv[3:4, :]
            for s in (1, 2, 3):
                pre = pre + xsh[s] * wv[3 - s:4 - s, :]
            sg = _sigmoid(pre)
            y = pre * sg
            if mode != "v":
                ss = jnp.sum(y * y, axis=-1, keepdims=True) + L2_EPS
                r = lax.rsqrt(ss)
                dy = scale * (dn_e * r - y * (r * r * r) * jnp.sum(dn_e * y, axis=-1, keepdims=True))
            else:
                dy = dn_e
            dpre = dy * (sg * (1.0 + pre * (1.0 - sg)))
            dx = dpre[:RT, :] * wv[3:4, :]
            for s in (1, 2, 3):
                dx = dx + pltpu.roll(dpre, RT + HALO - s, 0)[:RT, :] * wv[3 - s:4 - s, :]
            trow = r0 + _iota((RT, 1), 0)
            dx_ref[pl.ds(r0, RT), :] = jnp.where(trow >= pad, dx, 0.0)
            new = []
            for s in (0, 1, 2, 3):
                new.append(dw[s] + jnp.sum(dpre[:RT, :] * xsh[s][:RT, :], axis=0, keepdims=True))
            return tuple(new)

        z = jnp.zeros((1, HD), F32)
        dw = lax.fori_loop(0, nt, tile, (z, z, z, z))
        for s in (0, 1, 2, 3):
            dw_ref[3 - s:4 - s, :] = dw[s]

    return pl.pallas_call(
        body, name=name, grid=(GDN_H,),
        in_specs=[pl.BlockSpec((t_pad, HD), lambda h: (0, blk0 + h)), pl.BlockSpec((CONV_K, HD), lambda h: (0, h)),
                  pl.BlockSpec((t_pad, HD), lambda h: (0, h))],
        out_specs=[pl.BlockSpec((t_pad, HD), lambda h: (0, h)), pl.BlockSpec((CONV_K, HD), lambda h: (0, h))],
        out_shape=[jax.ShapeDtypeStruct((t_pad, GDN_H * HD), F32), jax.ShapeDtypeStruct((CONV_K, GDN_H * HD), F32)],
        scratch_shapes=[pltpu.VMEM((t_pad + 3 * HALO, HD), F32), pltpu.VMEM((t_pad + HALO, HD), F32)],
        compiler_params=_cparams(),
    )(p, w, dn)


@jax.custom_vjp
def _unit_lower_inv(m, bd, eye):
    md = m * bd
    low = m - md
    p2 = mbnn(md, md)
    p4 = mbnn(p2, p2)
    dinv = mbnn(mbnn(eye - md, eye + p2), eye + p4)
    n = mbnn(dinv, low)
    n2 = mbnn(n, n)
    n4 = mbnn(n2, n2)
    return mbnn(mbnn(mbnn(eye - n, eye + n2), eye + n4), dinv)


def _unit_lower_inv_bwd(res, g):
    t, bd, eye = res
    return -mbtn(t, mbnt(g, t)), jnp.zeros_like(bd), jnp.zeros_like(eye)


def _unit_lower_inv_fwd(m, bd, eye):
    t = _unit_lower_inv(m, bd, eye)
    return t, (t, bd, eye)


_unit_lower_inv.defvjp(_unit_lower_inv_fwd, _unit_lower_inv_bwd)


def _gdn_chunk(q, k, v, bb, aa, alog, dtb, s, valid):
    nh = q.shape[0]
    ri = _iota((1, CH, CH), 1)
    ci = _iota((1, CH, CH), 2)
    causal = ri >= ci
    strict = ri > ci
    eye = (ri == ci).astype(F32)
    bd = ((ri >> 3) == (ci >> 3)).astype(F32)
    ltri = (_iota((CH, CH), 0) >= _iota((CH, CH), 1)).astype(F32)
    sel = (_iota((nh, 1, HD), 2) == _iota((nh, 1, HD), 0)).astype(F32)

    beta_all = jnp.where(valid, _sigmoid(bb), 0.0)
    g_all = jnp.where(valid, -jnp.exp(alog) * _softplus(aa + dtb), 0.0)
    gc_all = hnn(ltri, g_all)
    beta = jnp.sum(beta_all[None] * sel, axis=2, keepdims=True)
    gc = jnp.sum(gc_all[None] * sel, axis=2, keepdims=True)
    gc_rows = hbnt(jnp.broadcast_to(sel, (nh, CH, HD)), jnp.broadcast_to(gc_all[None], (nh, CH, HD)))
    last = _iota((1, CH, 1), 1) == CH - 1
    gc_last = jnp.sum(jnp.where(last, gc, 0.0), axis=1, keepdims=True)
    decay = jnp.exp(jnp.where(causal, gc - gc_rows, NEG))
    egc = jnp.exp(gc)

    kb = k * beta
    m = jnp.where(strict, bbnt(kb, k) * decay, 0.0)
    t_inv = _unit_lower_inv(m, bd, eye)
    u = bbnn(t_inv, v * beta)
    w = bbnn(t_inv, kb * egc)
    a_intra = bbnt(q, k) * decay
    q_dec = q * egc
    k_dec = k * jnp.exp(gc_last - gc)
    v_new = u - bbnn(w, s)
    o = bbnn(q_dec, s) + bbnn(a_intra, v_new)
    s_new = s * jnp.exp(gc_last) + bbtn(k_dec, v_new)
    return o, s_new


PAIR = 2 * CH


def _gdn_specs(npair, rev):
    cc = (lambda c: npair - 1 - c) if rev else (lambda c: c)
    wide = pl.BlockSpec((PAIR, GDN_H * HD), lambda c: (cc(c), 0))
    fix = lambda off: pl.BlockSpec((PAIR, HD), lambda c: (cc(c), off))
    par = pl.BlockSpec((1, HD), lambda c: (0, 0))
    state = pl.BlockSpec((1, GDN_H, HD, HD), lambda c: (cc(c), 0, 0, 0))
    return wide, fix, par, state


def _store_heads(ref, a, rows=slice(None)):
    for h in range(a.shape[0]):
        ref[rows, h * HD:(h + 1) * HD] = a[h]


def _chunk_rows(half):
    return slice(half * CH, (half + 1) * CH)


def _chunk_valid(pair, half, pad):
    return ((2 * pair + half) * CH + _iota((CH, 1), 0)) >= pad


def _gdn_fwd(qn, kn, vn, p, alog, dtb, pad, name):
    t_pad = qn.shape[0]
    npair = t_pad // PAIR
    wide, fix, par, state = _gdn_specs(npair, False)

    def body(q_ref, k_ref, v_ref, bb_ref, aa_ref, al_ref, dt_ref, o_ref, ss_ref, s_ref):
        c = pl.program_id(0)

        @pl.when(c == 0)
        def _():
            s_ref[...] = jnp.zeros_like(s_ref)

        s = s_ref[...]
        ss_ref[0] = s
        for half in (0, 1):
            r = _chunk_rows(half)
            o, s = _gdn_chunk(_heads(q_ref[r, :], GDN_H), _heads(k_ref[r, :], GDN_H), _heads(v_ref[r, :], GDN_H),
                              bb_ref[r, :], aa_ref[r, :], al_ref[...], dt_ref[...], s, _chunk_valid(c, half, pad))
            _store_heads(o_ref, o, r)
        s_ref[...] = s

    return pl.pallas_call(
        body, name=name, grid=(npair,),
        in_specs=[wide, wide, wide, fix(16), fix(17), par, par],
        out_specs=[wide, state],
        out_shape=[jax.ShapeDtypeStruct((t_pad, GDN_H * HD), F32), jax.ShapeDtypeStruct((npair, GDN_H, HD, HD), F32)],
        scratch_shapes=[pltpu.VMEM((GDN_H, HD, HD), F32)],
        compiler_params=_cparams(),
    )(qn, kn, vn, p, p, alog, dtb)


def _gdn_bwd(qn, kn, vn, p, alog, dtb, ssave, do, pad, name, cargo=(), exchange=None):
    t_pad = qn.shape[0]
    npair = t_pad // PAIR
    wide, fix, par, state = _gdn_specs(npair, True)
    n = len(cargo)

    def body(q_ref, k_ref, v_ref, bb_ref, aa_ref, al_ref, dt_ref, ss_ref, do_ref, *rest):
        c = pl.program_id(0)
        ds_ref = rest[-1]
        (dq_ref, dk_ref, dv_ref, dbb_ref, daa_ref, dal_ref, ddt_ref), end_cargo = _cargo_bounds(
            rest[:-1], n, 7, exchange, c == 0, c == npair - 1)

        @pl.when(c == 0)
        def _():
            ds_ref[...] = jnp.zeros_like(ds_ref)
            dal_ref[...] = jnp.zeros_like(dal_ref)
            ddt_ref[...] = jnp.zeros_like(ddt_ref)

        ra, rb = _chunk_rows(0), _chunk_rows(1)
        va, vb = _chunk_valid(npair - 1 - c, 0, pad), _chunk_valid(npair - 1 - c, 1, pad)

        def pair(qa, ka, va_, ba, aa, qb, kb, vb_, bb, ab, al, dt, s):
            oa, s = _gdn_chunk(qa, ka, va_, ba, aa, al, dt, s, va)
            ob, s = _gdn_chunk(qb, kb, vb_, bb, ab, al, dt, s, vb)
            return oa, ob, s

        ins = [f(ref[r, :]) for r in (ra, rb)
               for ref, f in ((q_ref, lambda a: _heads(a, GDN_H)), (k_ref, lambda a: _heads(a, GDN_H)),
                              (v_ref, lambda a: _heads(a, GDN_H)), (bb_ref, lambda a: a), (aa_ref, lambda a: a))]
        _, vjp = jax.vjp(pair, *ins, al_ref[...], dt_ref[...], ss_ref[0])
        g = vjp((_heads(do_ref[ra, :], GDN_H), _heads(do_ref[rb, :], GDN_H), ds_ref[...]))
        for r, (dq, dk, dv, dbb, daa) in ((ra, g[0:5]), (rb, g[5:10])):
            _store_heads(dq_ref, dq, r)
            _store_heads(dk_ref, dk, r)
            _store_heads(dv_ref, dv, r)
            dbb_ref[r, :] = dbb
            daa_ref[r, :] = daa
        dal_ref[...] += g[10]
        ddt_ref[...] += g[11]
        ds_ref[...] = g[12]
        end_cargo()

    sds = jax.ShapeDtypeStruct
    return pl.pallas_call(
        body, name=name, grid=(npair,),
        in_specs=[wide, wide, wide, fix(16), fix(17), par, par, state, wide] + [ANY] * n,
        out_specs=[wide, wide, wide, fix(0), fix(0), par, par] + [ANY] * n,
        out_shape=[sds((t_pad, GDN_H * HD), F32)] * 3 + [sds((t_pad, HD), F32)] * 2 + [sds((1, HD), F32)] * 2
        + (exchange[1](cargo) if n else []),
        scratch_shapes=(exchange[2](n) if n else []) + [pltpu.VMEM((GDN_H, HD, HD), F32)],
        compiler_params=_cparams(),
    )(qn, kn, vn, p, p, alog, dtb, ssave, do, *cargo)


SB_Q0, SB_K0, SB_V0 = 18, 22, 26
SB_SCALE = SB_DH ** -0.5
SB_NB = 4


def _sb_terms(z, allowed):
    e = jnp.exp(-jnp.abs(z))
    den = 1.0 + e
    raw = -jnp.maximum(z, 0.0) - jnp.log(den)
    l1m = raw if allowed is None else jnp.where(allowed, raw, 0.0)
    return l1m, z + raw, jnp.where(z >= 0.0, 1.0, e) / den


def _sb_passes(i, step, carry):
    total = i + 1
    sized = lambda done: [functools.partial(step, done, masked=True, nb=nb) for nb in range(1, SB_NB + 1)]

    def several(c):
        n_mid = (total - SB_NB - 1) // SB_NB
        c = step(0, c, masked=True, nb=SB_NB)
        c = lax.fori_loop(0, n_mid, lambda t, cc: step(SB_NB * (1 + t), cc, masked=False, nb=SB_NB), c)
        done = SB_NB * (1 + n_mid)
        return lax.switch(total - done - 1, sized(done), c)

    return lax.cond(total <= SB_NB, lambda c: lax.switch(total - 1, sized(0), c), several, carry)


def _sb_stack(a, i):
    first = _iota((1, HD), 1) < SB_DH
    a2 = jnp.concatenate([jnp.where(first, a, 0.0), jnp.where(first, 0.0, a)], axis=0).astype(BF16)
    rq = i * QB + _iota((QB, 1), 0)
    return a2, jnp.concatenate([rq, rq], axis=0), first


def _hi_lo(a):
    hi = a.astype(BF16)
    lo = (a - hi.astype(F32)).astype(BF16)
    return jnp.concatenate([hi, lo], axis=1)


def _cargo_bounds(refs, n, n_out, exchange, first, last):
    outs = refs[n:n + n_out]
    if not n:
        return outs, lambda: None
    ex = exchange[0](refs[:n], refs[n + n_out:2 * n + n_out], *refs[2 * n + n_out:])

    @pl.when(first)
    def _():
        ex.start()

    def finish():
        @pl.when(last)
        def _():
            ex.wait()

    return outs, finish


def _sb_fwd(p, pad, name, cargo=(), exchange=None):
    t_pad = p.shape[0]
    nq = t_pad // QB
    n = len(cargo)

    def body(q_ref, k_ref, v_ref, *rest):
        i = pl.program_id(1)
        pr = pl.program_id(0)
        (o_ref, r_ref), end_cargo = _cargo_bounds(rest, n, 2, exchange, (pr == 0) & (i == 0),
                                                  (pr == SB_H // 2 - 1) & (i == nq - 1))
        q2, rowq, first = _sb_stack(q_ref[...] * SB_SCALE, i)
        tri = (_iota((QB, QB), 0) > _iota((QB, QB), 1)).astype(BF16)
        upper2 = jnp.concatenate([tri, tri], axis=0)

        def chain(kb, masked):
            start = pl.multiple_of(kb * QB, QB)
            kblk = k_ref[pl.ds(start, QB), :].astype(BF16)
            vblk = v_ref[pl.ds(start, QB), :].astype(BF16)
            z = lax.dot_general(q2, kblk, (NT, ((), ())), preferred_element_type=F32)
            colk = kb * QB + _iota((1, QB), 1)
            al = ((colk < rowq) & (colk >= pad)) if masked else None
            l1m, ls, _ = _sb_terms(z, al)
            suf = lax.dot_general(_hi_lo(l1m), upper2, (NN, ((), ())), preferred_element_type=F32)
            return al, ls, suf, jnp.sum(l1m, axis=1, keepdims=True), vblk

        def step(done, carry, masked, nb):
            o_acc, run = carry
            ws, vs = [], []
            for n in range(nb):
                al, ls, suf, rs, vblk = chain(i - done - n, masked)
                wgt = jnp.exp(ls + suf + run)
                ws.append((wgt if al is None else jnp.where(al, wgt, 0.0)).astype(BF16))
                vs.append(vblk)
                run = run + rs
            o_acc = o_acc + lax.dot_general(jnp.concatenate(ws, axis=1), jnp.concatenate(vs, axis=0),
                                            (NN, ((), ())), preferred_element_type=F32)
            return o_acc, run

        o_acc, run = _sb_passes(i, step, (jnp.zeros((2 * QB, HD), F32), jnp.zeros((2 * QB, 1), F32)))
        o_ref[...] = jnp.where(first, o_acc[:QB], o_acc[QB:]).astype(BF16)
        r_ref[...] = jnp.where(first, run[:QB], run[QB:])
        end_cargo()

    full = lambda off: pl.BlockSpec((t_pad, HD), lambda pr, i: (0, off + pr))
    blk = pl.BlockSpec((QB, HD), lambda pr, i: (i, pr))
    return pl.pallas_call(
        body, name=name, grid=(SB_H // 2, nq),
        in_specs=[pl.BlockSpec((QB, HD), lambda pr, i: (i, SB_Q0 + pr)), full(SB_K0), full(SB_V0)] + [ANY] * n,
        out_specs=[blk, blk] + [ANY] * n,
        out_shape=[jax.ShapeDtypeStruct((t_pad, SB_H * SB_DH), BF16), jax.ShapeDtypeStruct((t_pad, SB_H * SB_DH), F32)]
        + (exchange[1](cargo) if n else []),
        scratch_shapes=exchange[2](n) if n else [],
        compiler_params=_cparams(),
    )(p, p, p, *cargo)


def _sb_bwd(p, rtot, dy, dy_blk0, pad, name, cargo=(), exchange=None):
    t_pad = p.shape[0]
    nq = t_pad // QB
    n = len(cargo)

    def body(q_ref, k_ref, v_ref, r_ref, do_ref, *rest):
        i = pl.program_id(1)
        pr = pl.program_id(0)
        (dq_ref, dk_ref, dv_ref), end_cargo = _cargo_bounds(rest, n, 3, exchange, (pr == 0) & (i == 0),
                                                            (pr == SB_H // 2 - 1) & (i == nq - 1))

        @pl.when(i == 0)
        def _():
            dk_ref[...] = jnp.zeros_like(dk_ref)
            dv_ref[...] = jnp.zeros_like(dv_ref)

        q2, rowq, first = _sb_stack(q_ref[...] * SB_SCALE, i)
        do2, _, _ = _sb_stack(do_ref[...], i)
        rt = r_ref[...]
        lane = _iota((1, HD), 1)
        rcol = jnp.concatenate([jnp.sum(jnp.where(lane == 0, rt, 0.0), axis=1, keepdims=True),
                                jnp.sum(jnp.where(lane == SB_DH, rt, 0.0), axis=1, keepdims=True)], axis=0)
        rj = _iota((QB, QB), 0)
        cs = _iota((QB, QB), 1)
        tri_u = (rj > cs).astype(BF16)
        tri_l = (rj < cs).astype(BF16)
        upper2 = jnp.concatenate([tri_u, tri_u], axis=0)
        lower2 = jnp.concatenate([tri_l, tri_l], axis=0)

        def chain(kb, masked):
            start = pl.multiple_of(kb * QB, QB)
            kblk = k_ref[pl.ds(start, QB), :].astype(BF16)
            vblk = v_ref[pl.ds(start, QB), :].astype(BF16)
            z = lax.dot_general(q2, kblk, (NT, ((), ())), preferred_element_type=F32)
            colk = kb * QB + _iota((1, QB), 1)
            al = ((colk < rowq) & (colk >= pad)) if masked else None
            l1m, ls, sg = _sb_terms(z, al)
            dwgt = lax.dot_general(do2, vblk, (NT, ((), ())), preferred_element_type=F32)
            suf = lax.dot_general(_hi_lo(l1m), upper2, (NN, ((), ())), preferred_element_type=F32)
            return start, kblk, al, ls, suf, jnp.sum(l1m, axis=1, keepdims=True), dwgt, sg

        def finish(c, seen, gseen):
            start, kblk, al, ls, suf, rs, dwgt, sg = c
            wgt = jnp.exp(ls + suf + (rcol - seen - rs))
            if al is not None:
                wgt = jnp.where(al, wgt, 0.0)
            dl = dwgt * wgt
            gpre = gseen + lax.dot_general(_hi_lo(dl), lower2, (NN, ((), ())), preferred_element_type=F32)
            dz = dl - sg * (dl + gpre)
            if al is not None:
                dz = jnp.where(al, dz, 0.0)
            dz = dz.astype(BF16)
            dk_ref[pl.ds(start, QB), :] += lax.dot_general(dz, q2, (TN, ((), ())), preferred_element_type=F32)
            dv_ref[pl.ds(start, QB), :] += lax.dot_general(wgt.astype(BF16), do2, (TN, ((), ())),
                                                           preferred_element_type=F32)
            return dz, seen + rs, gseen + jnp.sum(dl, axis=1, keepdims=True)

        def step(done, carry, masked, nb):
            dq_acc, seen, gseen = carry
            cs_ = [chain(done + n, masked) for n in range(nb)]
            dzs = []
            for c in cs_:
                dz, seen, gseen = finish(c, seen, gseen)
                dzs.append(dz)
            dq_acc = dq_acc + lax.dot_general(jnp.concatenate(dzs, axis=1), jnp.concatenate([c[1] for c in cs_], axis=0),
                                              (NN, ((), ())), preferred_element_type=F32)
            return dq_acc, seen, gseen

        zc = jnp.zeros((2 * QB, 1), F32)
        dq_acc, _, _ = _sb_passes(i, step, (jnp.zeros((2 * QB, HD), F32), zc, zc))
        dq_ref[...] = jnp.where(first, dq_acc[:QB], dq_acc[QB:]) * SB_SCALE
        end_cargo()

    full_in = lambda off: pl.BlockSpec((t_pad, HD), lambda pr, i: (0, off + pr))
    full_out = pl.BlockSpec((t_pad, HD), lambda pr, i: (0, pr))
    blk = pl.BlockSpec((QB, HD), lambda pr, i: (i, pr))
    sds = jax.ShapeDtypeStruct((t_pad, SB_H * SB_DH), F32)
    return pl.pallas_call(
        body, name=name, grid=(SB_H // 2, nq),
        in_specs=[pl.BlockSpec((QB, HD), lambda pr, i: (i, SB_Q0 + pr)), full_in(SB_K0), full_in(SB_V0), blk,
                  pl.BlockSpec((QB, HD), lambda pr, i: (i, dy_blk0 + pr))] + [ANY] * n,
        out_specs=[blk, full_out, full_out] + [ANY] * n,
        out_shape=[sds, sds, sds] + (exchange[1](cargo) if n else []),
        scratch_shapes=exchange[2](n) if n else [],
        compiler_params=_cparams(),
    )(p, p, p, rtot, dy, *cargo)


HG_LEVELS = 6


def _hg_prefix_matrix():
    t = np.arange(CH)[:, None]
    j = np.arange(CH)[None, :]
    groups = [(j <= t)]
    for lvl in range(1, HG_LEVELS + 1):
        half = CH >> lvl
        e = (t // (2 * half)) * (2 * half) + half - 1
        groups.append(j <= e)
    groups.append(np.ones((8, CH), bool))
    e = np.concatenate(groups, axis=0).astype(np.float32)
    return np.concatenate([e, e, e], axis=1), np.concatenate([e, e, e], axis=0)


HG_G = 4


def _hg_chunk(qr, fr, iv, r0, r1, st, valid, ecat):
    g = st.shape[0]
    mx = jnp.maximum(r0, r1)
    e0 = jnp.exp(r0 - mx)
    e1 = jnp.exp(r1 - mx)
    lb = e1 / (e0 + e1)
    fg = lb + (1.0 - lb) * _sigmoid(fr)
    logf = jnp.where(valid, jnp.log(fg), 0.0)
    kk = jnp.where(valid, 1.0 - fg, 0.0)
    q = jnp.where(valid, _silu(qr), 0.0)
    v = _heads(jnp.where(valid, iv, 0.0), g)

    pre = _mask_dot(ecat, logf)
    b = pre[0:CH]
    b_last = jnp.max(pre[(HG_LEVELS + 1) * CH:], axis=0, keepdims=True)
    row = _iota((CH, 1), 0)
    ri = _iota((1, CH, CH), 1)
    ci = _iota((1, CH, CH), 2)
    a = jnp.where(ri == ci, jnp.sum(_heads(q * kk, g), axis=2, keepdims=True), 0.0)
    for lvl in range(1, HG_LEVELS + 1):
        half = CH >> lvl
        m = pre[lvl * CH:(lvl + 1) * CH]
        low = (row & half) != 0
        dec = jnp.exp(jnp.where(low, b - m, m - b))
        qt = jnp.where(low, q * dec, 0.0)
        kt = jnp.where(low, 0.0, kk * dec)
        same = (ri >> (7 - lvl)) == (ci >> (7 - lvl))
        a = a + jnp.where(same, bbnt(_heads(qt, g), _heads(kt, g)), 0.0)
    o = bbnt(_heads(q * jnp.exp(b), g), st) + bbnn(a, v)
    kd = kk * jnp.exp(b_last - b)
    st_new = st * _heads(jnp.exp(b_last), g) + bbtn(v, _heads(kd, g))
    return o, st_new


def _hg_specs(npair, rev):
    cc = (lambda c: npair - 1 - c) if rev else (lambda c: c)
    ng = HG_H // HG_G
    blk = lambda off: pl.BlockSpec((PAIR, HG_G * HD), lambda h, c: (cc(c), off * ng + h))
    lbs = pl.BlockSpec((2, HG_G * HD), lambda h, c: (0, h))
    state = pl.BlockSpec((1, HG_G, HD, HD), lambda h, c: (cc(c), h, 0, 0))
    return ng, blk, lbs, state


def _hg_fwd(p, lbraw, ecat, pad, name):
    t_pad = p.shape[0]
    npair = t_pad // PAIR
    ng, blk, lbs, state = _hg_specs(npair, False)

    def body(q_ref, f_ref, i_ref, lb_ref, e_ref, et_ref, o_ref, ss_ref, s_ref):
        c = pl.program_id(1)

        @pl.when(c == 0)
        def _():
            s_ref[...] = jnp.zeros_like(s_ref)

        st = s_ref[...]
        ss_ref[0] = st
        for half in (0, 1):
            r = _chunk_rows(half)
            o, st = _hg_chunk(q_ref[r, :], f_ref[r, :], i_ref[r, :], lb_ref[0:1, :], lb_ref[1:2, :], st,
                              _chunk_valid(c, half, pad), (e_ref[...], et_ref[...]))
            _store_heads(o_ref, o, r)
        s_ref[...] = st

    return pl.pallas_call(
        body, name=name, grid=(ng, npair),
        in_specs=[blk(0), blk(1), blk(2), lbs] + [pl.BlockSpec(e.shape, lambda h, c: (0, 0)) for e in ecat],
        out_specs=[blk(0), state],
        out_shape=[jax.ShapeDtypeStruct((t_pad, HG_H * HD), F32), jax.ShapeDtypeStruct((npair, HG_H, HD, HD), F32)],
        scratch_shapes=[pltpu.VMEM((HG_G, HD, HD), F32)],
        compiler_params=_cparams(),
    )(p, p, p, lbraw, *ecat)


def _hg_bwd(p, lbraw, ecat, ssave, do, pad, name, cargo=(), exchange=None):
    t_pad = p.shape[0]
    npair = t_pad // PAIR
    ng, blk, lbs, state = _hg_specs(npair, True)
    n = len(cargo)

    def body(q_ref, f_ref, i_ref, lb_ref, e_ref, et_ref, ss_ref, do_ref, *rest):
        c = pl.program_id(1)
        hg = pl.program_id(0)
        ds_ref = rest[-1]
        (dq_ref, df_ref, di_ref, dlb_ref), end_cargo = _cargo_bounds(
            rest[:-1], n, 4, exchange, (hg == 0) & (c == 0), (hg == ng - 1) & (c == npair - 1))

        @pl.when(c == 0)
        def _():
            ds_ref[...] = jnp.zeros_like(ds_ref)
            dlb_ref[...] = jnp.zeros_like(dlb_ref)

        ra, rb = _chunk_rows(0), _chunk_rows(1)
        va, vb = _chunk_valid(npair - 1 - c, 0, pad), _chunk_valid(npair - 1 - c, 1, pad)
        ecv = (e_ref[...], et_ref[...])

        def pair(qa, fa, ia, qb, fb, ib, r0, r1, st):
            oa, st = _hg_chunk(qa, fa, ia, r0, r1, st, va, ecv)
            ob, st = _hg_chunk(qb, fb, ib, r0, r1, st, vb, ecv)
            return oa, ob, st

        ins = [ref[r, :] for r in (ra, rb) for ref in (q_ref, f_ref, i_ref)]
        _, vjp = jax.vjp(pair, *ins, lb_ref[0:1, :], lb_ref[1:2, :], ss_ref[0])
        g = vjp((_heads(do_ref[ra, :], HG_G), _heads(do_ref[rb, :], HG_G), ds_ref[...]))
        for r, (dq, df, di) in ((ra, g[0:3]), (rb, g[3:6])):
            dq_ref[r, :] = dq.astype(BF16)
            df_ref[r, :] = df.astype(BF16)
            di_ref[r, :] = di.astype(BF16)
        dlb_ref[0:1, :] += g[6]
        dlb_ref[1:2, :] += g[7]
        ds_ref[...] = g[8]
        end_cargo()

    sds = jax.ShapeDtypeStruct((t_pad, HG_H * HD), BF16)
    return pl.pallas_call(
        body, name=name, grid=(ng, npair),
        in_specs=[blk(0), blk(1), blk(2), lbs] + [pl.BlockSpec(e.shape, lambda h, c: (0, 0)) for e in ecat]
        + [state, blk(0)] + [ANY] * n,
        out_specs=[blk(0), blk(0), blk(0), lbs] + [ANY] * n,
        out_shape=[sds, sds, sds, jax.ShapeDtypeStruct((2, HG_H * HD), F32)] + (exchange[1](cargo) if n else []),
        scratch_shapes=(exchange[2](n) if n else []) + [pltpu.VMEM((HG_G, HD, HD), F32)],
        compiler_params=_cparams(),
    )(p, p, p, lbraw, *ecat, ssave, do, *cargo)


def _pad_ab_cols(w):
    z = jnp.zeros((w.shape[0], HD - GDN_H), w.dtype)
    return jnp.concatenate([w[:, :2048], w[:, 2048:2052], z, w[:, 2052:2056], z, w[:, 2056:]], axis=1)


def _unpad_ab_cols(w):
    return jnp.concatenate([w[:, :2048], w[:, 2048:2052], w[:, 2176:2180], w[:, 2304:]], axis=1)


def _lane_pad(v):
    return jnp.pad(v, ((0, 0), (0, HD - v.shape[1])))


def _mlp_fwd(hb, w1, w2, layer):
    a, r = _mm(hb, w1, b_view=("cols", layer), out_dtype=BF16, act=True, name=f"mlp_up_{layer}")
    m = _mm(r, w2, b_view=("rows", layer), name=f"mlp_down_{layer}")
    return a, r, m


def _mlp_bwd(hb, a, r, dmb, w1, w2, layer):
    da = _mm(dmb, w2, tb=True, b_view=("rows", layer), out_dtype=BF16, gate=a, name=f"mlp_down_dx_{layer}")
    dw2 = _mm(r, dmb, ta=True, out_dtype=BF16, name=f"mlp_down_dw_{layer}")
    dh = _mm(da, w1, tb=True, b_view=("cols", layer), name=f"mlp_up_dx_{layer}")
    dw1 = _mm(hb, da, ta=True, out_dtype=BF16, out_split=N_CHIP, name=f"mlp_up_dw_{layer}")
    return dh, dw1, dw2


def _local_step(h0, tgt, w, pad, late=None):
    row = lambda a, i: a[i:i + 1]
    ecat = tuple(jnp.asarray(e, dtype=BF16) for e in _hg_prefix_matrix())
    cw = [w["conv_w"][:, i * 512:(i + 1) * 512] for i in range(3)]
    alog, dtb = _lane_pad(w["a_log"]), _lane_pad(w["dt_bias"])

    h0b = h0.astype(BF16)
    p0 = _mm(h0b, w["ab_w_in"], name="ab_in")
    qn = _conv_fwd(p0, 0, cw[0], "q", pad, "conv_q")
    kn = _conv_fwd(p0, 4, cw[1], "k", pad, "conv_k")
    vn = _conv_fwd(p0, 8, cw[2], "v", pad, "conv_v")
    oa_raw, ss0 = _gdn_fwd(qn, kn, vn, p0, alog, dtb, pad, "gdn_fwd")
    oa = _grms_fwd(oa_raw, p0, 12, w["ab_gnorm_g"], "gdn_gate")
    if late is None:
        ob, rtot = _sb_fwd(p0, pad, "sb_fwd")
    else:
        ob, rtot, g_about, g_cin, g_cout, g_w1, g_w2 = _sb_fwd(p0, pad, "sb_fwd", cargo=late, exchange=GATHER)
        w = dict(w, ab_w_out=g_about.reshape(D, D), c_w_in=g_cin, c_w_out=g_cout.reshape(D, D), mlp_w1=g_w1, mlp_w2=g_w2)
    ycat = jnp.concatenate([oa, ob], axis=1)
    mix0 = _mm(ycat, w["ab_w_out"], name="ab_out")
    h1, h1b = _ln_res_fwd(h0, mix0, row(w["ln_mix_g"], 0), row(w["ln_mix_b"], 0), "ln_mix_0")
    a0, r0, m0 = _mlp_fwd(h1b, w["mlp_w1"], w["mlp_w2"], 0)
    h2, h2b = _ln_res_fwd(h1, m0, row(w["ln_ffn_g"], 0), row(w["ln_ffn_b"], 0), "ln_ffn_0")
    p1 = _mm(h2b, w["c_w_in"], b_view=("cols", 0), name="c_in")
    oc_raw, ss1 = _hg_fwd(p1, w["c_lb_raw"], ecat, pad, "hg_fwd")
    yc = _grms_fwd(oc_raw, p1, 3 * HG_H, w["c_gnorm_g"], "hg_gate")
    mix1 = _mm(yc, w["c_w_out"], name="c_out")
    h3, h3b = _ln_res_fwd(h2, mix1, row(w["ln_mix_g"], 1), row(w["ln_mix_b"], 1), "ln_mix_1")
    a1, r1, m1 = _mlp_fwd(h3b, w["mlp_w1"], w["mlp_w2"], 1)
    h4, _ = _ln_res_fwd(h3, m1, row(w["ln_ffn_g"], 1), row(w["ln_ffn_b"], 1), "ln_ffn_1")
    loss, dh4 = _loss_fwd(h4, tgt, pad + N_META, "loss")

    zero = jnp.zeros_like(dh4)
    dh3a, dm1b, dfg1, dfb1 = _ln_res_bwd(h3, m1, row(w["ln_ffn_g"], 1), row(w["ln_ffn_b"], 1), dh4, zero, "ln_ffn_bwd_1")
    dh3b, dw1_1, dw2_1 = _mlp_bwd(h3b, a1, r1, dm1b, w["mlp_w1"], w["mlp_w2"], 1)
    dh2a, dmix1b, dmg1, dmb1 = _ln_res_bwd(h2, mix1, row(w["ln_mix_g"], 1), row(w["ln_mix_b"], 1), dh3a, dh3b, "ln_mix_bwd_1")
    dyc = _mm(dmix1b, w["c_w_out"], tb=True, name="c_out_dx")
    dwco = _mm(yc, dmix1b, ta=True, out_dtype=BF16, name="c_out_dw")
    doc, dzc, dcg = _grms_bwd(oc_raw, p1, 3 * HG_H, w["c_gnorm_g"], dyc, 0, "hg_gate_bwd")
    landed = {}
    rows4 = lambda a: a.reshape(N_CHIP, -1, D)
    if late is None:
        dq1, df1, di1, dlb = _hg_bwd(p1, w["c_lb_raw"], ecat, ss1, doc, pad, "hg_bwd")
    else:
        dq1, df1, di1, dlb, landed["w1_1"], landed["w2_1"], landed["c_w_out"] = _hg_bwd(
            p1, w["c_lb_raw"], ecat, ss1, doc, pad, "hg_bwd", cargo=[dw1_1, rows4(dw2_1), rows4(dwco)], exchange=SCATTER)
    dp1 = [dq1, df1, di1, dzc]
    dh2b = _mm_groups_nt(dp1, w["c_w_in"], "c_in_dx")
    dwc = jnp.stack([_mm(h2b, d, ta=True, out_dtype=BF16, name=f"c_in_dw_{i}") for i, d in enumerate(dp1)])
    dh1a, dm0b, dfg0, dfb0 = _ln_res_bwd(h1, m0, row(w["ln_ffn_g"], 0), row(w["ln_ffn_b"], 0), dh2a, dh2b, "ln_ffn_bwd_0")
    dh1b, dw1_0, dw2_0 = _mlp_bwd(h1b, a0, r0, dm0b, w["mlp_w1"], w["mlp_w2"], 0)
    dh0a, dmix0b, dmg0, dmb0 = _ln_res_bwd(h0, mix0, row(w["ln_mix_g"], 0), row(w["ln_mix_b"], 0), dh1a, dh1b, "ln_mix_bwd_0")
    dycat = _mm(dmix0b, w["ab_w_out"], tb=True, name="ab_out_dx")
    dwabo = _mm(ycat, dmix0b, ta=True, out_dtype=BF16, name="ab_out_dw")
    doa, dza, dag = _grms_bwd(oa_raw, p0, 12, w["ab_gnorm_g"], dycat, 0, "gdn_gate_bwd")
    if late is None:
        dqn, dkn, dvn, dbb, daa, dal, ddt = _gdn_bwd(qn, kn, vn, p0, alog, dtb, ss0, doa, pad, "gdn_bwd")
        dqb, dkb, dvb = _sb_bwd(p0, rtot, dycat, 4, pad, "sb_bwd")
    else:
        dqn, dkn, dvn, dbb, daa, dal, ddt, landed["c_w_in"], landed["w2_0"] = _gdn_bwd(
            qn, kn, vn, p0, alog, dtb, ss0, doa, pad, "gdn_bwd", cargo=[dwc, rows4(dw2_0)], exchange=SCATTER)
        dqb, dkb, dvb, landed["w1_0"], landed["ab_w_out"] = _sb_bwd(
            p0, rtot, dycat, 4, pad, "sb_bwd", cargo=[dw1_0, rows4(dwabo)], exchange=SCATTER)
    dpq, dcq = _conv_bwd(p0, 0, cw[0], dqn, "q", pad, "conv_q_bwd")
    dpk, dck = _conv_bwd(p0, 4, cw[1], dkn, "k", pad, "conv_k_bwd")
    dpv, dcv = _conv_bwd(p0, 8, cw[2], dvn, "v", pad, "conv_v_bwd")
    dp0 = _assemble_bf16([(dpq, "cols"), (dpk, "cols"), (dpv, "cols"), (dza, "cols"), (dbb, "cols"), (daa, "cols"),
                          (dqb, "cols"), (dkb, "cols"), (dvb, "cols")], "ab_in_dy")
    dwab = _mm(h0b, dp0, ta=True, out_dtype=BF16, name="ab_in_dw")
    if late is None:
        dh0b = _mm(dp0, w["ab_w_in"], tb=True, name="ab_in_dx")
    else:
        dab = jnp.transpose(_unpad_ab_cols(dwab).reshape(D, N_CHIP, AB_TRUE // N_CHIP), (1, 0, 2))
        dh0b, landed["ab_w_in"] = _mm(dp0, w["ab_w_in"], tb=True, name="ab_in_dx", cargo=[dab], exchange=SCATTER)
    dh0 = _add2(dh0a, dh0b, "dh0")

    grads = {
        "ab_w_in": dwab, "conv_w": jnp.concatenate([dcq, dck, dcv], axis=1),
        "a_log": dal[:, :GDN_H], "dt_bias": ddt[:, :GDN_H],
        "ab_gnorm_g": dag, "ab_w_out": dwabo, "c_w_in": dwc, "c_lb_raw": dlb, "c_gnorm_g": dcg, "c_w_out": dwco,
        "ln_mix_g": jnp.concatenate([dmg0, dmg1], 0), "ln_mix_b": jnp.concatenate([dmb0, dmb1], 0),
        "w1_0": dw1_0, "w1_1": dw1_1, "w2_0": dw2_0, "w2_1": dw2_1,
        "ln_ffn_g": jnp.concatenate([dfg0, dfg1], 0), "ln_ffn_b": jnp.concatenate([dfb0, dfb1], 0),
        "landed": landed,
    }
    return loss, dh0, grads


MESH = pl.DeviceIdType.MESH
ANY = pl.BlockSpec(memory_space=pl.ANY)
N_CHIP = 4
N_DEV = 8
CHIP_REL = ((1, 0), (0, 1), (1, 1))
DEV_REL = tuple((dx, dy, dc) for dx in (0, 1) for dy in (0, 1) for dc in (0, 1))[1:]

def _pos():
    return lax.axis_index("x"), lax.axis_index("y"), lax.axis_index("c")


def _flip(a, d):
    return a + d - 2 * a * d


class _Exchange:
    def __init__(self, local, sends, recvs):
        self.local, self.sends, self.recvs = local, sends, recvs

    def start(self):
        for cp in self.local + self.sends:
            cp.start()

    def wait(self):
        for cp in self.recvs:
            cp.wait_recv()
        for cp in self.sends:
            cp.wait_send()
        for cp in self.local:
            cp.wait()


def _gather_sems(n):
    return [pltpu.SemaphoreType.DMA((3 * n,)), pltpu.SemaphoreType.DMA((3 * n,)), pltpu.SemaphoreType.DMA((n,))]


def _gather_copies(x_refs, o_refs, send_sems, recv_sems, local_sems):
    n = len(x_refs)
    x, y, c = _pos()
    local = [pltpu.make_async_copy(x_refs[a], o_refs[a].at[2 * x + y], local_sems.at[a]) for a in range(n)]

    def copy(a, k, sending):
        tx, ty = _flip(x, CHIP_REL[k][0]), _flip(y, CHIP_REL[k][1])
        return pltpu.make_async_remote_copy(
            src_ref=x_refs[a], dst_ref=o_refs[a].at[2 * x + y if sending else 2 * tx + ty],
            send_sem=send_sems.at[3 * a + k], recv_sem=recv_sems.at[3 * a + k], device_id=(tx, ty, c), device_id_type=MESH)

    pairs = [(a, k) for a in range(n) for k in range(3)]
    return _Exchange(local, [copy(a, k, True) for a, k in pairs], [copy(a, k, False) for a, k in pairs])


def _gather_shapes(bufs):
    return [jax.ShapeDtypeStruct((N_CHIP,) + b.shape, b.dtype) for b in bufs]


def _chip_allgather(bufs, name):
    n = len(bufs)

    def body(*refs):
        ex = _gather_copies(refs[:n], refs[n:2 * n], *refs[2 * n:])
        ex.start()
        ex.wait()

    return pl.pallas_call(
        body, name=name, in_specs=[ANY] * n, out_specs=[ANY] * n, out_shape=_gather_shapes(bufs),
        scratch_shapes=_gather_sems(n), compiler_params=pltpu.CompilerParams(has_side_effects=True),
    )(*bufs)


def _scatter_sems(n):
    nr = N_DEV - 1
    return [pltpu.SemaphoreType.DMA((nr * n,)), pltpu.SemaphoreType.DMA((nr * n,)), pltpu.SemaphoreType.DMA((n,))]


def _scatter_copies(g_refs, o_refs, send_sems, recv_sems, local_sems):
    n = len(g_refs)
    nr = N_DEV - 1
    x, y, c = _pos()
    me = 4 * x + 2 * y + c
    local = [pltpu.make_async_copy(g_refs[a].at[2 * x + y], o_refs[a].at[me], local_sems.at[a]) for a in range(n)]

    def copy(a, k, sending):
        dx, dy, dc = DEV_REL[k]
        tx, ty, tc = _flip(x, dx), _flip(y, dy), _flip(c, dc)
        return pltpu.make_async_remote_copy(
            src_ref=g_refs[a].at[2 * tx + ty], dst_ref=o_refs[a].at[me if sending else 4 * tx + 2 * ty + tc],
            send_sem=send_sems.at[nr * a + k], recv_sem=recv_sems.at[nr * a + k],
            device_id=(tx, ty, tc), device_id_type=MESH)

    pairs = [(a, k) for a in range(n) for k in range(nr)]
    return _Exchange(local, [copy(a, k, True) for a, k in pairs], [copy(a, k, False) for a, k in pairs])


def _scatter_shapes(gs):
    return [jax.ShapeDtypeStruct((N_DEV,) + g.shape[1:], g.dtype) for g in gs]


GATHER = (_gather_copies, _gather_shapes, _gather_sems)
SCATTER = (_scatter_copies, _scatter_shapes, _scatter_sems)


def _sum_slots(r, name):
    n, rh, w = r.shape
    tr = _pick(rh, (256, 128, 64, 16))

    def body(r_ref, o_ref):
        acc = r_ref[0].astype(F32)
        for s in range(1, n):
            acc = acc + r_ref[s].astype(F32)
        o_ref[...] = acc

    return pl.pallas_call(
        body, name=name, grid=(rh // tr,), in_specs=[pl.BlockSpec((n, tr, w), lambda i: (0, i, 0))],
        out_specs=pl.BlockSpec((tr, w), lambda i: (i, 0)), out_shape=jax.ShapeDtypeStruct((rh, w), F32),
        compiler_params=_cparams(),
    )(r)


def _small_allreduce(buf, name):
    r, w = buf.shape

    def body(b_ref, o_ref, land_ref, send_sems, recv_sems):
        x, y, c = _pos()
        me = 4 * x + 2 * y + c
        land_ref[me] = b_ref[...]

        def target(k):
            dx, dy, dc = DEV_REL[k]
            return _flip(x, dx), _flip(y, dy), _flip(c, dc)

        sends = []
        for k in range(N_DEV - 1):
            tx, ty, tc = target(k)
            cp = pltpu.make_async_remote_copy(
                src_ref=b_ref, dst_ref=land_ref.at[me], send_sem=send_sems.at[k], recv_sem=recv_sems.at[k],
                device_id=(tx, ty, tc), device_id_type=MESH)
            cp.start()
            sends.append(cp)
        for k in range(N_DEV - 1):
            tx, ty, tc = target(k)
            pltpu.make_async_remote_copy(
                src_ref=b_ref, dst_ref=land_ref.at[4 * tx + 2 * ty + tc], send_sem=send_sems.at[k],
                recv_sem=recv_sems.at[k], device_id=(tx, ty, tc), device_id_type=MESH).wait_recv()
        for cp in sends:
            cp.wait_send()
        acc = land_ref[0]
        for s in range(1, N_DEV):
            acc = acc + land_ref[s]
        o_ref[...] = acc

    vm = pl.BlockSpec(memory_space=pltpu.VMEM)
    return pl.pallas_call(
        body, name=name, in_specs=[vm], out_specs=vm, out_shape=jax.ShapeDtypeStruct((r, w), F32),
        scratch_shapes=[pltpu.VMEM((N_DEV, r, w), F32), pltpu.SemaphoreType.DMA((N_DEV - 1,)),
                        pltpu.SemaphoreType.DMA((N_DEV - 1,))],
        compiler_params=pltpu.CompilerParams(has_side_effects=True),
    )(buf)


def _adamw(w, g, m, v, name):
    r, c = w.shape
    tr = _pick(r, (256, 128, 64, 8)) if r * c > (1 << 18) else r

    def body(w_ref, g_ref, m_ref, v_ref, d_ref, m2_ref, v2_ref):
        gg = g_ref[...]
        m2 = ADAM_B1 * m_ref[...] + (1.0 - ADAM_B1) * gg
        v2 = ADAM_B2 * v_ref[...] + (1.0 - ADAM_B2) * (gg * gg)
        m_hat = m2 / (1.0 - ADAM_B1 ** ADAM_STEP)
        v_hat = v2 / (1.0 - ADAM_B2 ** ADAM_STEP)
        d_ref[...] = -ADAM_LR * (m_hat / (jnp.sqrt(v_hat) + ADAM_EPS) + ADAM_WD * w_ref[...])
        m2_ref[...] = m2
        v2_ref[...] = v2

    blk = pl.BlockSpec((tr, c), lambda i: (i, 0))
    sds = jax.ShapeDtypeStruct((r, c), F32)
    return pl.pallas_call(body, name=name, grid=(r // tr,), in_specs=[blk] * 4, out_specs=[blk] * 3,
                          out_shape=[sds] * 3, compiler_params=_cparams())(w, g, m, v)


BIG = ("ab_w_in", "ab_w_out", "c_w_in", "c_w_out", "mlp_w1", "mlp_w2")
SMALL = ("ln_mix_g", "ln_mix_b", "ln_ffn_g", "ln_ffn_b", "c_lb_raw", "ab_a_log", "ab_dt_bias", "ab_gnorm_g", "c_gnorm_g")
SMALL_ROWS = 16
CONV_ROWS = 8
CONV_W = 3 * GDN_H * HD


def _conv_to_rows(cw):
    return jnp.pad(cw, ((0, 0), (0, 2 * D - CONV_W))).reshape(CONV_ROWS, D)


def _rows_to_conv(rows):
    return rows.reshape(CONV_K, 2 * D)[:, :CONV_W]


def _pack_small(d):
    rows = [jnp.pad(d[n], ((0, 0), (0, D - d[n].shape[1]))) for n in SMALL]
    buf = jnp.concatenate(rows, axis=0)
    return jnp.pad(buf, ((0, SMALL_ROWS - buf.shape[0]), (0, 0)))


def _unpack_small(buf, like):
    out, r = {}, 0
    for n in SMALL:
        nr, nc = like[n].shape
        out[n] = buf[r:r + nr, :nc]
        r += nr
    return out


def kernel(x, meta_tokens, ab_w_in, ab_conv_w, ab_a_log, ab_dt_bias, ab_gnorm_g, ab_w_out, c_w_in, c_lb_raw, c_gnorm_g, c_w_out, ln_mix_g, ln_mix_b, mlp_w1, mlp_w2, ln_ffn_g, ln_ffn_b, loss_target, m_meta_tokens, m_ab_w_in, m_ab_conv_w, m_ab_a_log, m_ab_dt_bias, m_ab_gnorm_g, m_ab_w_out, m_c_w_in, m_c_lb_raw, m_c_gnorm_g, m_c_w_out, m_ln_mix_g, m_ln_mix_b, m_mlp_w1, m_mlp_w2, m_ln_ffn_g, m_ln_ffn_b, v_meta_tokens, v_ab_w_in, v_ab_conv_w, v_ab_a_log, v_ab_dt_bias, v_ab_gnorm_g, v_ab_w_out, v_c_w_in, v_c_lb_raw, v_c_gnorm_g, v_c_w_out, v_ln_mix_g, v_ln_mix_b, v_mlp_w1, v_mlp_w2, v_ln_ffn_g, v_ln_ffn_b):
    names = ("meta_tokens", "ab_w_in", "ab_conv_w", "ab_a_log", "ab_dt_bias", "ab_gnorm_g", "ab_w_out", "c_w_in",
             "c_lb_raw", "c_gnorm_g", "c_w_out", "ln_mix_g", "ln_mix_b", "mlp_w1", "mlp_w2", "ln_ffn_g", "ln_ffn_b")
    wts = dict(zip(names, (meta_tokens, ab_w_in, ab_conv_w, ab_a_log, ab_dt_bias, ab_gnorm_g, ab_w_out, c_w_in, c_lb_raw,
                           c_gnorm_g, c_w_out, ln_mix_g, ln_mix_b, mlp_w1, mlp_w2, ln_ffn_g, ln_ffn_b)))
    mom_m = dict(zip(names, (m_meta_tokens, m_ab_w_in, m_ab_conv_w, m_ab_a_log, m_ab_dt_bias, m_ab_gnorm_g, m_ab_w_out,
                             m_c_w_in, m_c_lb_raw, m_c_gnorm_g, m_c_w_out, m_ln_mix_g, m_ln_mix_b, m_mlp_w1, m_mlp_w2,
                             m_ln_ffn_g, m_ln_ffn_b)))
    mom_v = dict(zip(names, (v_meta_tokens, v_ab_w_in, v_ab_conv_w, v_ab_a_log, v_ab_dt_bias, v_ab_gnorm_g, v_ab_w_out,
                             v_c_w_in, v_c_lb_raw, v_c_gnorm_g, v_c_w_out, v_ln_mix_g, v_ln_mix_b, v_mlp_w1, v_mlp_w2,
                             v_ln_ffn_g, v_ln_ffn_b)))
    seq = x.shape[1]
    pad = (-(N_META + seq)) % QB
    xi, yi, ci = _pos()
    chip = 2 * xi + yi

    gat_ab_in, = _chip_allgather([ab_w_in[0].astype(BF16)], "gather_weights")
    late = [ab_w_out[0].astype(BF16), c_w_in.astype(BF16), c_w_out[0].astype(BF16), mlp_w1.astype(BF16),
            mlp_w2.astype(BF16)]
    mcols, ccols = meta_tokens.shape[1], ab_conv_w.shape[2]
    place = jnp.concatenate([
        lax.dynamic_update_slice(jnp.zeros((N_META, D), F32), 0.5 * meta_tokens, (0, chip * mcols)),
        _conv_to_rows(lax.dynamic_update_slice(jnp.zeros((CONV_K, CONV_W), F32), 0.5 * ab_conv_w[0], (0, chip * ccols)))],
        axis=0)
    placed = _small_allreduce(place, "gather_meta")
    meta_full = placed[:N_META]

    w = {
        "ab_w_in": _pad_ab_cols(jnp.transpose(gat_ab_in, (1, 0, 2)).reshape(D, AB_TRUE)),
        "conv_w": _rows_to_conv(placed[N_META:]), "a_log": ab_a_log, "dt_bias": ab_dt_bias,
        "ab_gnorm_g": ab_gnorm_g, "c_lb_raw": c_lb_raw,
        "c_gnorm_g": c_gnorm_g, "ln_mix_g": ln_mix_g, "ln_mix_b": ln_mix_b, "ln_ffn_g": ln_ffn_g, "ln_ffn_b": ln_ffn_b,
    }

    h0 = jnp.concatenate([jnp.zeros((pad, D), F32), meta_full, x[0]], axis=0)
    tgt = jnp.concatenate([jnp.zeros((pad + N_META, D), F32), loss_target[0]], axis=0)
    loss8, dh0, g = _local_step(h0, tgt, w, pad, late)
    loss = lax.psum(loss8[0, 0], ("x", "y", "c"))
    grad_x = dh0[pad + N_META:][None]

    gsmall = {"ln_mix_g": g["ln_mix_g"], "ln_mix_b": g["ln_mix_b"], "ln_ffn_g": g["ln_ffn_g"], "ln_ffn_b": g["ln_ffn_b"],
              "c_lb_raw": g["c_lb_raw"], "ab_a_log": g["a_log"], "ab_dt_bias": g["dt_bias"], "ab_gnorm_g": g["ab_gnorm_g"],
              "c_gnorm_g": g["c_gnorm_g"]}
    sbuf = jnp.concatenate([_pack_small(gsmall), dh0[pad:pad + N_META], _conv_to_rows(g["conv_w"])], axis=0)
    ssum = _small_allreduce(sbuf, "allreduce_small")
    grads = _unpack_small(ssum[:SMALL_ROWS], wts)
    grads["meta_tokens"] = lax.dynamic_slice(ssum[SMALL_ROWS:SMALL_ROWS + N_META], (0, chip * mcols), (N_META, mcols))
    grads["ab_conv_w"] = lax.dynamic_slice(_rows_to_conv(ssum[SMALL_ROWS + N_META:]), (0, chip * ccols), (CONV_K, ccols))[None]

    sums = {k: _sum_slots(v, f"grad_sum_{k}") for k, v in g["landed"].items()}
    for n in ("ab_w_in", "ab_w_out", "c_w_in", "c_w_out"):
        grads[n] = sums[n][None]
    grads["mlp_w1"] = jnp.stack([sums["w1_0"], sums["w1_1"]])
    grads["mlp_w2"] = jnp.stack([sums["w2_0"], sums["w2_1"]])

    delta, new_m, new_v = {}, {}, {}
    for n in ("meta_tokens", "ab_conv_w") + BIG:
        shp = wts[n].shape
        to2 = lambda a: a.reshape(-1, shp[-1])
        d2, m2, v2 = _adamw(to2(wts[n]), to2(grads[n]), to2(mom_m[n]), to2(mom_v[n]), f"adamw_{n}")
        delta[n], new_m[n], new_v[n] = d2.reshape(shp), m2.reshape(shp), v2.reshape(shp)
    d2, m2, v2 = _adamw(_pack_small(wts), ssum[:SMALL_ROWS], _pack_small(mom_m), _pack_small(mom_v), "adamw_small")
    delta.update(_unpack_small(d2, wts))
    new_m.update(_unpack_small(m2, wts))
    new_v.update(_unpack_small(v2, wts))

    return (loss, grad_x, *[grads[n] for n in names], *[delta[n] for n in names], *[new_m[n] for n in names],
            *[new_v[n] for n in names])
```

```python
import functools
import math

import numpy as np
import jax
import jax.numpy as jnp
from jax import lax
from jax.experimental import pallas as pl
from jax.experimental.pallas import tpu as pltpu

F32 = jnp.float32
BF16 = jnp.bfloat16

D = 1024
N_META = 16
D_FF = 4 * D
DEPTH = 2
GDN_H = 4
SB_H = 8
SB_DH = 64
HG_H = 8
HD = 128
CH = 64
QB = 128
ALPHA = float((2 * DEPTH) ** 0.25)
LN_EPS = 1e-5
RMS_EPS = 1e-6
L2_EPS = 1e-6
NEG = -1e30

ADAM_LR = 0.001
ADAM_B1 = 0.9
ADAM_B2 = 0.999
ADAM_EPS = 1e-08
ADAM_WD = 0.01
ADAM_STEP = 10

AB_W = 30 * HD
AB_TRUE = 3592
VMEM_LIMIT = 56 * 1024 * 1024

NN = ((1,), (0,))
NT = ((1,), (1,))
TN = ((0,), (0,))


def _cparams(**kw):
    return pltpu.CompilerParams(vmem_limit_bytes=VMEM_LIMIT, **kw)


def _dg(a, b, dims, mode):
    if mode == "h":
        return lax.dot_general(a, b, dims, precision=lax.Precision.HIGHEST, preferred_element_type=F32)
    if mode == "b":
        return lax.dot_general(a.astype(BF16), b.astype(BF16), dims, preferred_element_type=F32)
    ah, bh = a.astype(BF16), b.astype(BF16)
    al, bl = (a - ah.astype(F32)).astype(BF16), (b - bh.astype(F32)).astype(BF16)
    d = lambda x, y: lax.dot_general(x, y, dims, preferred_element_type=F32)
    return d(ah, bh) + (d(ah, bl) + d(al, bh))


def _make_dots(mode, batched=False):
    if batched:
        nn_d, nt_d, tn_d = (((2,), (1,)), ((0,), (0,))), (((2,), (2,)), ((0,), (0,))), (((1,), (1,)), ((0,), (0,)))
    else:
        nn_d, nt_d, tn_d = (NN, ((), ())), (NT, ((), ())), (TN, ((), ()))

    @jax.custom_vjp
    def nn(a, b):
        return _dg(a, b, nn_d, mode)

    @jax.custom_vjp
    def nt(a, b):
        return _dg(a, b, nt_d, mode)

    @jax.custom_vjp
    def tn(a, b):
        return _dg(a, b, tn_d, mode)

    nn.defvjp(lambda a, b: (nn(a, b), (a, b)), lambda r, g: (nt(g, r[1]), tn(r[0], g)))
    nt.defvjp(lambda a, b: (nt(a, b), (a, b)), lambda r, g: (nn(g, r[1]), tn(g, r[0])))
    tn.defvjp(lambda a, b: (tn(a, b), (a, b)), lambda r, g: (nt(r[1], g), nn(r[0], g)))
    return nn, nt, tn


hnn, hnt, htn = _make_dots("h")
bbnn, bbnt, bbtn = _make_dots("b", True)
mbnn, mbnt, mbtn = _make_dots("m", True)
hbnn, hbnt, hbtn = _make_dots("h", True)


def _split3(x, axis):
    x1 = x.astype(BF16)
    r1 = x - x1.astype(F32)
    x2 = r1.astype(BF16)
    x3 = (r1 - x2.astype(F32)).astype(BF16)
    return jnp.concatenate([x1, x2, x3], axis=axis)


@jax.custom_vjp
def _mask_dot(e3, x):
    return lax.dot_general(e3[0], _split3(x, 0), (NN, ((), ())), preferred_element_type=F32)


def _mask_dot_bwd(e3, g):
    dx = lax.dot_general(e3[1], _split3(g, 0), (TN, ((), ())), preferred_element_type=F32)
    return (jnp.zeros_like(e3[0]), jnp.zeros_like(e3[1])), dx


_mask_dot.defvjp(lambda e3, x: (_mask_dot(e3, x), e3), _mask_dot_bwd)


def _heads(a, n):
    return jnp.concatenate([a[None, :, h * HD:(h + 1) * HD] for h in range(n)], axis=0)


def _sigmoid(x):
    return jax.nn.sigmoid(x)


def _silu(x):
    return x * jax.nn.sigmoid(x)


def _softplus(x):
    return jnp.maximum(x, 0.0) + jnp.log(1.0 + jnp.exp(-jnp.abs(x)))


def _iota(shape, dim):
    return lax.broadcasted_iota(jnp.int32, shape, dim)


def _pick(n, prefs):
    for p in prefs:
        if n % p == 0:
            return p
    return n


def _mm(a, b, *, ta=False, tb=False, out_dtype=F32, name, b_view=None, out_split=0, act=False, gate=None,
        cargo=(), exchange=None):
    if ta:
        k_dim, m_dim = a.shape
    else:
        m_dim, k_dim = a.shape
    if b_view is None:
        w_rows, w_cols = b.shape
    else:
        kind, layer = b_view
        nj, _, blk_r, blk_c = b.shape
        w_rows, w_cols = (blk_r, nj * blk_c) if kind == "cols" else (nj * blk_r, blk_c)
    n_dim = w_rows if tb else w_cols
    assert (w_cols if tb else w_rows) == k_dim
    tm = _pick(m_dim, (1024, 1056, 704, 640, 512, 384, 256, 128))
    tn = _pick(n_dim, (1024, 1056, 704, 640, 512, 384, 256, 128))
    tk = _pick(k_dim, (1024, 1056, 704, 512, 384, 256, 128))
    nk = k_dim // tk
    a_spec = pl.BlockSpec((tk, tm), lambda i, j, k: (k, i)) if ta else pl.BlockSpec((tm, tk), lambda i, j, k: (i, k))
    wb = (tn, tk) if tb else (tk, tn)
    w_idx = (lambda i, j, k: (j, k)) if tb else (lambda i, j, k: (k, j))
    if b_view is None:
        b_spec = pl.BlockSpec(wb, w_idx)
    elif kind == "cols":
        per = blk_c // wb[1]
        b_spec = pl.BlockSpec((None, None) + wb,
                              lambda i, j, k: (w_idx(i, j, k)[1] // per, layer, w_idx(i, j, k)[0], w_idx(i, j, k)[1] % per))
    else:
        per = blk_r // wb[0]
        b_spec = pl.BlockSpec((None, None) + wb,
                              lambda i, j, k: (w_idx(i, j, k)[0] // per, layer, w_idx(i, j, k)[0] % per, w_idx(i, j, k)[1]))
    if out_split:
        per_o = (n_dim // out_split) // tn
        out_spec = pl.BlockSpec((None, tm, tn), lambda i, j, k: (j // per_o, i, j % per_o))
        out_sds = jax.ShapeDtypeStruct((out_split, m_dim, n_dim // out_split), out_dtype)
    else:
        out_spec = pl.BlockSpec((tm, tn), lambda i, j, k: (i, j))
        out_sds = jax.ShapeDtypeStruct((m_dim, n_dim), out_dtype)
    dims = (((0 if ta else 1,), (1 if tb else 0,)), ((), ()))
    extra = [] if gate is None else [gate]
    n_out = 2 if act else 1

    def finish(acc, refs):
        if act:
            refs[0][...] = acc.astype(refs[0].dtype)
            r = jnp.maximum(acc, 0.0)
            refs[1][...] = (r * r).astype(refs[1].dtype)
        elif gate is not None:
            refs[1][...] = (acc * (2.0 * jnp.maximum(refs[0][...].astype(F32), 0.0))).astype(refs[1].dtype)
        else:
            refs[0][...] = acc.astype(refs[0].dtype)

    grid = (m_dim // tm, n_dim // tn, nk)
    nc = len(cargo)

    def body(a_ref, b_ref, *rest):
        acc_ref = rest[-1]
        ids = [pl.program_id(d) for d in range(3)]
        outs, end_cargo = _cargo_bounds(
            rest[len(extra):-1], nc, n_out, exchange, (ids[0] == 0) & (ids[1] == 0) & (ids[2] == 0),
            (ids[0] == grid[0] - 1) & (ids[1] == grid[1] - 1) & (ids[2] == grid[2] - 1))
        refs = tuple(rest[:len(extra)]) + tuple(outs)
        part = lax.dot_general(a_ref[...], b_ref[...], dims, preferred_element_type=F32)
        if nk == 1:
            finish(part, refs)
        else:
            k = ids[2]

            @pl.when(k == 0)
            def _():
                acc_ref[...] = part

            @pl.when(k > 0)
            def _():
                acc_ref[...] += part

            @pl.when(k == nk - 1)
            def _():
                finish(acc_ref[...], refs)
        end_cargo()

    out = pl.pallas_call(
        body, name=name, grid=grid,
        in_specs=[a_spec, b_spec] + [pl.BlockSpec((tm, tn), lambda i, j, k: (i, j))] * len(extra) + [ANY] * nc,
        out_specs=[out_spec] * n_out + [ANY] * nc,
        out_shape=[out_sds] * n_out + (exchange[1](cargo) if nc else []),
        scratch_shapes=(exchange[2](nc) if nc else []) + [pltpu.VMEM((tm, tn) if nk > 1 else (8, 128), F32)],
        compiler_params=_cparams(dimension_semantics=("arbitrary",) * 3 if nc else ("parallel", "parallel", "arbitrary")),
    )(a, b, *extra, *cargo)
    if nc:
        return out
    return out if act else out[0]


def _mm_groups_nt(parts, b, name):
    m_dim, k_dim = parts[0].shape
    ng, _, n_dim, _ = b.shape
    assert len(parts) == ng and b.shape[3] == k_dim
    tm = _pick(m_dim, (1056, 704, 512, 384, 256, 128))

    def body(*refs):
        a_refs, b_ref, o_ref, acc_ref = refs[:ng], refs[ng], refs[ng + 1], refs[ng + 2]
        k = pl.program_id(1)
        for g in range(ng):
            @pl.when(k == g)
            def _(g=g):
                part = lax.dot_general(a_refs[g][...], b_ref[...], (NT, ((), ())), preferred_element_type=F32)
                if g == 0:
                    acc_ref[...] = part
                elif g < ng - 1:
                    acc_ref[...] += part
                else:
                    o_ref[...] = acc_ref[...] + part

    return pl.pallas_call(
        body, name=name, grid=(m_dim // tm, ng),
        in_specs=[pl.BlockSpec((tm, k_dim), lambda i, k: (i, 0))] * ng
        + [pl.BlockSpec((None, None, n_dim, k_dim), lambda i, k: (k, 0, 0, 0))],
        out_specs=pl.BlockSpec((tm, n_dim), lambda i, k: (i, 0)),
        out_shape=jax.ShapeDtypeStruct((m_dim, n_dim), F32),
        scratch_shapes=[pltpu.VMEM((tm, n_dim), F32)],
        compiler_params=_cparams(dimension_semantics=("parallel", "arbitrary")),
    )(*parts, b)


def _row_tile(t_pad, width):
    for tr in (528, 352, 176, 128, 64):
        if t_pad % tr == 0 and tr * width * 4 <= (3 << 19) and tr % 16 == 0:
            return tr
    return 64 if t_pad % 64 == 0 else t_pad


def _ln_res_fn(h, m, g, b):
    x = ALPHA * h + m
    mu = jnp.mean(x, axis=-1, keepdims=True)
    xc = x - mu
    var = jnp.mean(xc * xc, axis=-1, keepdims=True)
    return xc * lax.rsqrt(var + LN_EPS) * g + b


def _ln_res_fwd(h, m, g, b, name):
    t_pad = h.shape[0]
    tr = _row_tile(t_pad, D)

    def body(h_ref, m_ref, g_ref, b_ref, y_ref, yb_ref):
        y = _ln_res_fn(h_ref[...], m_ref[...], g_ref[...], b_ref[...])
        y_ref[...] = y
        yb_ref[...] = y.astype(BF16)

    row = pl.BlockSpec((tr, D), lambda i: (i, 0))
    par = pl.BlockSpec((1, D), lambda i: (0, 0))
    return pl.pallas_call(
        body, name=name, grid=(t_pad // tr,), in_specs=[row, row, par, par], out_specs=[row, row],
        out_shape=[jax.ShapeDtypeStruct((t_pad, D), F32), jax.ShapeDtypeStruct((t_pad, D), BF16)],
        compiler_params=_cparams(),
    )(h, m, g, b)


def _ln_res_bwd(h, m, g, b, dy1, dy2, name):
    t_pad = h.shape[0]
    tr = _row_tile(t_pad, D)

    def body(h_ref, m_ref, g_ref, b_ref, d1_ref, d2_ref, dh_ref, dm_ref, dg_ref, db_ref):
        _, vjp = jax.vjp(_ln_res_fn, h_ref[...], m_ref[...], g_ref[...], b_ref[...])
        dh, dm, dg, db = vjp(d1_ref[...] + d2_ref[...])
        dh_ref[...] = dh
        dm_ref[...] = dm.astype(BF16)

        @pl.when(pl.program_id(0) == 0)
        def _():
            dg_ref[...] = jnp.zeros_like(dg_ref)
            db_ref[...] = jnp.zeros_like(db_ref)

        dg_ref[...] += dg
        db_ref[...] += db

    row = pl.BlockSpec((tr, D), lambda i: (i, 0))
    par = pl.BlockSpec((1, D), lambda i: (0, 0))
    return pl.pallas_call(
        body, name=name, grid=(t_pad // tr,), in_specs=[row, row, par, par, row, row],
        out_specs=[row, row, par, par],
        out_shape=[jax.ShapeDtypeStruct((t_pad, D), F32), jax.ShapeDtypeStruct((t_pad, D), BF16),
                   jax.ShapeDtypeStruct((1, D), F32), jax.ShapeDtypeStruct((1, D), F32)],
        compiler_params=_cparams(),
    )(h, m, g, b, dy1, dy2)


def _grms_fn(o, z, g):
    y = o * lax.rsqrt(jnp.mean(o * o, axis=-1, keepdims=True) + RMS_EPS) * g
    return y * _silu(z)


def _grms_fwd(o, z_arr, z_blk0, g, name):
    t_pad, w = o.shape
    tr = _row_tile(t_pad, w)
    assert (z_blk0 * HD) % w == 0

    def body(o_ref, z_ref, g_ref, y_ref):
        for h in range(w // HD):
            c = slice(h * HD, (h + 1) * HD)
            y_ref[:, c] = _grms_fn(o_ref[:, c], z_ref[:, c], g_ref[...]).astype(BF16)

    return pl.pallas_call(
        body, name=name, grid=(t_pad // tr,),
        in_specs=[pl.BlockSpec((tr, w), lambda i: (i, 0)), pl.BlockSpec((tr, w), lambda i: (i, z_blk0 * HD // w)),
                  pl.BlockSpec((1, HD), lambda i: (0, 0))],
        out_specs=pl.BlockSpec((tr, w), lambda i: (i, 0)),
        out_shape=jax.ShapeDtypeStruct((t_pad, w), BF16), compiler_params=_cparams(),
    )(o, z_arr, g)


def _grms_bwd(o, z_arr, z_blk0, g, dy_arr, dy_blk0, name):
    t_pad, w = o.shape
    tr = _row_tile(t_pad, w)
    assert (z_blk0 * HD) % w == 0 and (dy_blk0 * HD) % w == 0

    def body(o_ref, z_ref, g_ref, dy_ref, do_ref, dz_ref, dg_ref):
        @pl.when(pl.program_id(0) == 0)
        def _():
            dg_ref[...] = jnp.zeros_like(dg_ref)

        for h in range(w // HD):
            c = slice(h * HD, (h + 1) * HD)
            _, vjp = jax.vjp(_grms_fn, o_ref[:, c], z_ref[:, c], g_ref[...])
            do, dz, dg = vjp(dy_ref[:, c])
            do_ref[:, c] = do
            dz_ref[:, c] = dz.astype(BF16)
            dg_ref[...] += dg

    blk = pl.BlockSpec((tr, w), lambda i: (i, 0))
    return pl.pallas_call(
        body, name=name, grid=(t_pad // tr,),
        in_specs=[blk, pl.BlockSpec((tr, w), lambda i: (i, z_blk0 * HD // w)), pl.BlockSpec((1, HD), lambda i: (0, 0)),
                  pl.BlockSpec((tr, w), lambda i: (i, dy_blk0 * HD // w))],
        out_specs=[blk, blk, pl.BlockSpec((1, HD), lambda i: (0, 0))],
        out_shape=[jax.ShapeDtypeStruct((t_pad, w), F32), jax.ShapeDtypeStruct((t_pad, w), BF16),
                   jax.ShapeDtypeStruct((1, HD), F32)],
        compiler_params=_cparams(),
    )(o, z_arr, g, dy_arr)


def _loss_fwd(y, tgt, first_row, name):
    t_pad = y.shape[0]
    tr = _row_tile(t_pad, D)

    def body(y_ref, t_ref, l_ref, dy_ref):
        rows = pl.program_id(0) * tr + _iota((tr, 1), 0)
        err = jnp.where(rows >= first_row, y_ref[...] - t_ref[...], 0.0)
        dy_ref[...] = err * (1.0 / D)

        @pl.when(pl.program_id(0) == 0)
        def _():
            l_ref[...] = jnp.zeros_like(l_ref)

        part = jnp.sum(jnp.sum(err * err, axis=1, keepdims=True), axis=0, keepdims=True)
        l_ref[...] += jnp.broadcast_to(part * (0.5 / D), l_ref.shape)

    row = pl.BlockSpec((tr, D), lambda i: (i, 0))
    return pl.pallas_call(
        body, name=name, grid=(t_pad // tr,), in_specs=[row, row],
        out_specs=[pl.BlockSpec((8, 128), lambda i: (0, 0)), row],
        out_shape=[jax.ShapeDtypeStruct((8, 128), F32), jax.ShapeDtypeStruct((t_pad, D), F32)],
        compiler_params=_cparams(),
    )(y, tgt)


def _add2(a, b, name):
    t_pad, w = a.shape
    tr = _row_tile(t_pad, w)

    def body(a_ref, b_ref, o_ref):
        o_ref[...] = a_ref[...] + b_ref[...]

    row = pl.BlockSpec((tr, w), lambda i: (i, 0))
    return pl.pallas_call(body, name=name, grid=(t_pad // tr,), in_specs=[row, row], out_specs=row,
                          out_shape=jax.ShapeDtypeStruct((t_pad, w), F32), compiler_params=_cparams())(a, b)


def _assemble_bf16(parts, name):
    t_pad = parts[0][0].shape[0] if parts[0][1] == "cols" else parts[0][0].shape[1]
    widths = [p.shape[1] if kind == "cols" else HD for p, kind in parts]
    total = sum(widths)
    tr = _row_tile(t_pad, total)

    def body(*refs):
        o_ref = refs[-1]
        off = 0
        for ref, (p, kind), w in zip(refs[:-1], parts, widths):
            if kind == "cols":
                o_ref[:, off:off + w] = ref[...].astype(BF16)
            else:
                acc = ref[0]
                for hh in range(1, p.shape[0]):
                    acc = acc + ref[hh]
                o_ref[:, off:off + w] = acc.astype(BF16)
            off += w

    specs = []
    for p, kind in parts:
        if kind == "cols":
            specs.append(pl.BlockSpec((tr, p.shape[1]), lambda i: (i, 0)))
        else:
            specs.append(pl.BlockSpec((p.shape[0], tr, HD), lambda i: (0, i, 0)))
    return pl.pallas_call(
        body, name=name, grid=(t_pad // tr,), in_specs=specs,
        out_specs=pl.BlockSpec((tr, total), lambda i: (i, 0)),
        out_shape=jax.ShapeDtypeStruct((t_pad, total), BF16), compiler_params=_cparams(),
    )(*[p for p, _ in parts])


CONV_K = 4
HALO = 8
RT = 128


def _conv_fwd(p, blk0, w, mode, pad, name):
    t_pad = p.shape[0]
    nt = t_pad // RT
    scale = HD ** -0.5 if mode == "q" else 1.0

    def body(x_ref, w_ref, y_ref, xs_ref):
        xs_ref[0:HALO, :] = jnp.zeros((HALO, HD), F32)
        rows = _iota((t_pad, 1), 0)
        xs_ref[HALO:HALO + t_pad, :] = jnp.where(rows >= pad, x_ref[...], 0.0)
        wv = w_ref[...]

        def tile(i, carry):
            r0 = pl.multiple_of(i * RT, RT)
            ext = xs_ref[pl.ds(r0, RT + HALO), :]
            acc = ext[HALO:, :] * wv[3:4, :]
            for s in (1, 2, 3):
                acc = acc + pltpu.roll(ext, s, 0)[HALO:, :] * wv[3 - s:4 - s, :]
            y = _silu(acc)
            if mode != "v":
                y = y * lax.rsqrt(jnp.sum(y * y, axis=-1, keepdims=True) + L2_EPS) * scale
            y_ref[pl.ds(r0, RT), :] = y
            return carry

        lax.fori_loop(0, nt, tile, 0)

    return pl.pallas_call(
        body, name=name, grid=(GDN_H,),
        in_specs=[pl.BlockSpec((t_pad, HD), lambda h: (0, blk0 + h)), pl.BlockSpec((CONV_K, HD), lambda h: (0, h))],
        out_specs=pl.BlockSpec((t_pad, HD), lambda h: (0, h)),
        out_shape=jax.ShapeDtypeStruct((t_pad, GDN_H * HD), F32),
        scratch_shapes=[pltpu.VMEM((t_pad + HALO, HD), F32)],
        compiler_params=_cparams(),
    )(p, w)


def _conv_bwd(p, blk0, w, dn, mode, pad, name):
    t_pad = p.shape[0]
    nt = t_pad // RT
    scale = HD ** -0.5 if mode == "q" else 1.0

    def body(x_ref, w_ref, dn_ref, dx_ref, dw_ref, xs_ref, ds_ref):
        xs_ref[0:HALO, :] = jnp.zeros((HALO, HD), F32)
        xs_ref[HALO + t_pad:HALO + t_pad + 2 * HALO, :] = jnp.zeros((2 * HALO, HD), F32)
        ds_ref[t_pad:t_pad + HALO, :] = jnp.zeros((HALO, HD), F32)
        rows = _iota((t_pad, 1), 0)
        xs_ref[HALO:HALO + t_pad, :] = jnp.where(rows >= pad, x_ref[...], 0.0)
        ds_ref[0:t_pad, :] = dn_ref[...]
        wv = w_ref[...]

        def tile(i, dw):
            r0 = pl.multiple_of(i * RT, RT)
            ext = xs_ref[pl.ds(r0, RT + 2 * HALO), :]
            dn_e = ds_ref[pl.ds(r0, RT + HALO), :]
            xsh = [ext[HALO:, :]] + [pltpu.roll(ext, s, 0)[HALO:, :] for s in (1, 2, 3)]
            pre = xsh[0] * wv[3:4, :]
            for s in (1, 2, 3):
                pre = pre + xsh[s] * wv[3 - s:4 - s, :]
            sg = _sigmoid(pre)
            y = pre * sg
            if mode != "v":
                ss = jnp.sum(y * y, axis=-1, keepdims=True) + L2_EPS
                r = lax.rsqrt(ss)
                dy = scale * (dn_e * r - y * (r * r * r) * jnp.sum(dn_e * y, axis=-1, keepdims=True))
            else:
                dy = dn_e
            dpre = dy * (sg * (1.0 + pre * (1.0 - sg)))
            dx = dpre[:RT, :] * wv[3:4, :]
            for s in (1, 2, 3):
                dx = dx + pltpu.roll(dpre, RT + HALO - s, 0)[:RT, :] * wv[3 - s:4 - s, :]
            trow = r0 + _iota((RT, 1), 0)
            dx_ref[pl.ds(r0, RT), :] = jnp.where(trow >= pad, dx, 0.0)
            new = []
            for s in (0, 1, 2, 3):
                new.append(dw[s] + jnp.sum(dpre[:RT, :] * xsh[s][:RT, :], axis=0, keepdims=True))
            return tuple(new)

        z = jnp.zeros((1, HD), F32)
        dw = lax.fori_loop(0, nt, tile, (z, z, z, z))
        for s in (0, 1, 2, 3):
            dw_ref[3 - s:4 - s, :] = dw[s]

    return pl.pallas_call(
        body, name=name, grid=(GDN_H,),
        in_specs=[pl.BlockSpec((t_pad, HD), lambda h: (0, blk0 + h)), pl.BlockSpec((CONV_K, HD), lambda h: (0, h)),
                  pl.BlockSpec((t_pad, HD), lambda h: (0, h))],
        out_specs=[pl.BlockSpec((t_pad, HD), lambda h: (0, h)), pl.BlockSpec((CONV_K, HD), lambda h: (0, h))],
        out_shape=[jax.ShapeDtypeStruct((t_pad, GDN_H * HD), F32), jax.ShapeDtypeStruct((CONV_K, GDN_H * HD), F32)],
        scratch_shapes=[pltpu.VMEM((t_pad + 3 * HALO, HD), F32), pltpu.VMEM((t_pad + HALO, HD), F32)],
        compiler_params=_cparams(),
    )(p, w, dn)


@jax.custom_vjp
def _unit_lower_inv(m, bd, eye):
    md = m * bd
    low = m - md
    p2 = mbnn(md, md)
    p4 = mbnn(p2, p2)
    dinv = mbnn(mbnn(eye - md, eye + p2), eye + p4)
    n = mbnn(dinv, low)
    n2 = mbnn(n, n)
    n4 = mbnn(n2, n2)
    return mbnn(mbnn(mbnn(eye - n, eye + n2), eye + n4), dinv)


def _unit_lower_inv_bwd(res, g):
    t, bd, eye = res
    return -mbtn(t, mbnt(g, t)), jnp.zeros_like(bd), jnp.zeros_like(eye)


def _unit_lower_inv_fwd(m, bd, eye):
    t = _unit_lower_inv(m, bd, eye)
    return t, (t, bd, eye)


_unit_lower_inv.defvjp(_unit_lower_inv_fwd, _unit_lower_inv_bwd)


def _gdn_chunks(chunks, alog, dtb, s):
    nh = chunks[0][0].shape[0]
    ri = _iota((1, CH, CH), 1)
    ci = _iota((1, CH, CH), 2)
    causal = ri >= ci
    strict = ri > ci
    eye = (ri == ci).astype(F32)
    bd = ((ri >> 3) == (ci >> 3)).astype(F32)
    ltri = (_iota((CH, CH), 0) >= _iota((CH, CH), 1)).astype(F32)
    sel = (_iota((nh, 1, HD), 2) == _iota((nh, 1, HD), 0)).astype(F32)
    last = _iota((1, CH, 1), 1) == CH - 1

    beta, gc, gc_rows = [], [], []
    for _, _, _, bb, aa, valid in chunks:
        beta_all = jnp.where(valid, _sigmoid(bb), 0.0)
        g_all = jnp.where(valid, -jnp.exp(alog) * _softplus(aa + dtb), 0.0)
        gc_all = hnn(ltri, g_all)
        beta.append(jnp.sum(beta_all[None] * sel, axis=2, keepdims=True))
        gc.append(jnp.sum(gc_all[None] * sel, axis=2, keepdims=True))
        gc_rows.append(hbnt(jnp.broadcast_to(sel, (nh, CH, HD)), jnp.broadcast_to(gc_all[None], (nh, CH, HD))))
    cat = lambda xs: jnp.concatenate(xs, axis=0)
    q, k, v = (cat([c[j] for c in chunks]) for j in range(3))
    beta, gc, gc_rows = cat(beta), cat(gc), cat(gc_rows)
    gc_last = jnp.sum(jnp.where(last, gc, 0.0), axis=1, keepdims=True)
    decay = jnp.exp(jnp.where(causal, gc - gc_rows, NEG))
    egc = jnp.exp(gc)

    kb = k * beta
    m = jnp.where(strict, bbnt(kb, k) * decay, 0.0)
    t_inv = _unit_lower_inv(m, bd, eye)
    u = bbnn(t_inv, v * beta)
    w = bbnn(t_inv, kb * egc)
    a_intra = bbnt(q, k) * decay
    q_dec = q * egc
    k_dec = k * jnp.exp(gc_last - gc)
    g_tot = jnp.exp(gc_last)

    outs = []
    for n in range(len(chunks)):
        part = lambda a: a[n * nh:(n + 1) * nh]
        v_new = part(u) - bbnn(part(w), s)
        outs.append(bbnn(part(q_dec), s) + bbnn(part(a_intra), v_new))
        s = s * part(g_tot) + bbtn(part(k_dec), v_new)
    return outs, s


PAIR = 2 * CH


def _gdn_specs(npair, rev):
    cc = (lambda c: npair - 1 - c) if rev else (lambda c: c)
    wide = pl.BlockSpec((PAIR, GDN_H * HD), lambda c: (cc(c), 0))
    fix = lambda off: pl.BlockSpec((PAIR, HD), lambda c: (cc(c), off))
    par = pl.BlockSpec((1, HD), lambda c: (0, 0))
    state = pl.BlockSpec((1, GDN_H, HD, HD), lambda c: (cc(c), 0, 0, 0))
    return wide, fix, par, state


def _store_heads(ref, a, rows=slice(None)):
    for h in range(a.shape[0]):
        ref[rows, h * HD:(h + 1) * HD] = a[h]


def _chunk_rows(half):
    return slice(half * CH, (half + 1) * CH)


def _chunk_valid(pair, half, pad):
    return ((2 * pair + half) * CH + _iota((CH, 1), 0)) >= pad


def _gdn_fwd(qn, kn, vn, p, alog, dtb, pad, name):
    t_pad = qn.shape[0]
    npair = t_pad // PAIR
    wide, fix, par, state = _gdn_specs(npair, False)

    def body(q_ref, k_ref, v_ref, bb_ref, aa_ref, al_ref, dt_ref, o_ref, ss_ref, s_ref):
        c = pl.program_id(0)

        @pl.when(c == 0)
        def _():
            s_ref[...] = jnp.zeros_like(s_ref)

        s = s_ref[...]
        ss_ref[0] = s
        rows = [_chunk_rows(half) for half in (0, 1)]
        chunks = [(_heads(q_ref[r, :], GDN_H), _heads(k_ref[r, :], GDN_H), _heads(v_ref[r, :], GDN_H),
                   bb_ref[r, :], aa_ref[r, :], _chunk_valid(c, half, pad)) for half, r in enumerate(rows)]
        outs, s = _gdn_chunks(chunks, al_ref[...], dt_ref[...], s)
        for r, o in zip(rows, outs):
            _store_heads(o_ref, o, r)
        s_ref[...] = s

    return pl.pallas_call(
        body, name=name, grid=(npair,),
        in_specs=[wide, wide, wide, fix(16), fix(17), par, par],
        out_specs=[wide, state],
        out_shape=[jax.ShapeDtypeStruct((t_pad, GDN_H * HD), F32), jax.ShapeDtypeStruct((npair, GDN_H, HD, HD), F32)],
        scratch_shapes=[pltpu.VMEM((GDN_H, HD, HD), F32)],
        compiler_params=_cparams(),
    )(qn, kn, vn, p, p, alog, dtb)


def _gdn_bwd(qn, kn, vn, p, alog, dtb, ssave, do, pad, name, cargo=(), exchange=None):
    t_pad = qn.shape[0]
    npair = t_pad // PAIR
    wide, fix, par, state = _gdn_specs(npair, True)
    n = len(cargo)

    def body(q_ref, k_ref, v_ref, bb_ref, aa_ref, al_ref, dt_ref, ss_ref, do_ref, *rest):
        c = pl.program_id(0)
        ds_ref = rest[-1]
        (dq_ref, dk_ref, dv_ref, dbb_ref, daa_ref, dal_ref, ddt_ref), end_cargo = _cargo_bounds(
            rest[:-1], n, 7, exchange, c == 0, c == npair - 1)

        @pl.when(c == 0)
        def _():
            ds_ref[...] = jnp.zeros_like(ds_ref)
            dal_ref[...] = jnp.zeros_like(dal_ref)
            ddt_ref[...] = jnp.zeros_like(ddt_ref)

        ra, rb = _chunk_rows(0), _chunk_rows(1)
        va, vb = _chunk_valid(npair - 1 - c, 0, pad), _chunk_valid(npair - 1 - c, 1, pad)

        def pair(qa, ka, va_, ba, aa, qb, kb, vb_, bb, ab, al, dt, s):
            (oa, ob), s = _gdn_chunks([(qa, ka, va_, ba, aa, va), (qb, kb, vb_, bb, ab, vb)], al, dt, s)
            return oa, ob, s

        ins = [f(ref[r, :]) for r in (ra, rb)
               for ref, f in ((q_ref, lambda a: _heads(a, GDN_H)), (k_ref, lambda a: _heads(a, GDN_H)),
                              (v_ref, lambda a: _heads(a, GDN_H)), (bb_ref, lambda a: a), (aa_ref, lambda a: a))]
        _, vjp = jax.vjp(pair, *ins, al_ref[...], dt_ref[...], ss_ref[0])
        g = vjp((_heads(do_ref[ra, :], GDN_H), _heads(do_ref[rb, :], GDN_H), ds_ref[...]))
        for r, (dq, dk, dv, dbb, daa) in ((ra, g[0:5]), (rb, g[5:10])):
            _store_heads(dq_ref, dq, r)
            _store_heads(dk_ref, dk, r)
            _store_heads(dv_ref, dv, r)
            dbb_ref[r, :] = dbb
            daa_ref[r, :] = daa
        dal_ref[...] += g[10]
        ddt_ref[...] += g[11]
        ds_ref[...] = g[12]
        end_cargo()

    sds = jax.ShapeDtypeStruct
    return pl.pallas_call(
        body, name=name, grid=(npair,),
        in_specs=[wide, wide, wide, fix(16), fix(17), par, par, state, wide] + [ANY] * n,
        out_specs=[wide, wide, wide, fix(0), fix(0), par, par] + [ANY] * n,
        out_shape=[sds((t_pad, GDN_H * HD), F32)] * 3 + [sds((t_pad, HD), F32)] * 2 + [sds((1, HD), F32)] * 2
        + (exchange[1](cargo) if n else []),
        scratch_shapes=(exchange[2](n) if n else []) + [pltpu.VMEM((GDN_H, HD, HD), F32)],
        compiler_params=_cparams(),
    )(qn, kn, vn, p, p, alog, dtb, ssave, do, *cargo)


SB_Q0, SB_K0, SB_V0 = 18, 22, 26
SB_SCALE = SB_DH ** -0.5
SB_NB = 4


def _sb_terms(z, allowed):
    e = jnp.exp(-jnp.abs(z))
    den = 1.0 + e
    raw = -jnp.maximum(z, 0.0) - jnp.log(den)
    l1m = raw if allowed is None else jnp.where(allowed, raw, 0.0)
    return l1m, z + raw, jnp.where(z >= 0.0, 1.0, e) / den


def _sb_passes(i, step, carry):
    total = i + 1
    sized = lambda done: [functools.partial(step, done, masked=True, nb=nb) for nb in range(1, SB_NB + 1)]

    def several(c):
        n_mid = (total - SB_NB - 1) // SB_NB
        c = step(0, c, masked=True, nb=SB_NB)
        c = lax.fori_loop(0, n_mid, lambda t, cc: step(SB_NB * (1 + t), cc, masked=False, nb=SB_NB), c)
        done = SB_NB * (1 + n_mid)
        return lax.switch(total - done - 1, sized(done), c)

    return lax.cond(total <= SB_NB, lambda c: lax.switch(total - 1, sized(0), c), several, carry)


def _sb_stack(a, i):
    first = _iota((1, HD), 1) < SB_DH
    a2 = jnp.concatenate([jnp.where(first, a, 0.0), jnp.where(first, 0.0, a)], axis=0).astype(BF16)
    rq = i * QB + _iota((QB, 1), 0)
    return a2, jnp.concatenate([rq, rq], axis=0), first


def _hi_lo(a):
    hi = a.astype(BF16)
    lo = (a - hi.astype(F32)).astype(BF16)
    return jnp.concatenate([hi, lo], axis=1)


def _cargo_bounds(refs, n, n_out, exchange, first, last):
    outs = refs[n:n + n_out]
    if not n:
        return outs, lambda: None
    ex = exchange[0](refs[:n], refs[n + n_out:2 * n + n_out], *refs[2 * n + n_out:])

    @pl.when(first)
    def _():
        ex.start()

    def finish():
        @pl.when(last)
        def _():
            ex.wait()

    return outs, finish


def _sb_fwd(p, pad, name, cargo=(), exchange=None):
    t_pad = p.shape[0]
    nq = t_pad // QB
    n = len(cargo)

    def body(q_ref, k_ref, v_ref, *rest):
        i = pl.program_id(1)
        pr = pl.program_id(0)
        (o_ref, r_ref), end_cargo = _cargo_bounds(rest, n, 2, exchange, (pr == 0) & (i == 0),
                                                  (pr == SB_H // 2 - 1) & (i == nq - 1))
        q2, rowq, first = _sb_stack(q_ref[...] * SB_SCALE, i)
        tri = (_iota((QB, QB), 0) > _iota((QB, QB), 1)).astype(BF16)
        upper2 = jnp.concatenate([tri, tri], axis=0)

        def chain(kb, masked):
            start = pl.multiple_of(kb * QB, QB)
            kblk = k_ref[pl.ds(start, QB), :].astype(BF16)
            vblk = v_ref[pl.ds(start, QB), :].astype(BF16)
            z = lax.dot_general(q2, kblk, (NT, ((), ())), preferred_element_type=F32)
            colk = kb * QB + _iota((1, QB), 1)
            al = ((colk < rowq) & (colk >= pad)) if masked else None
            l1m, ls, _ = _sb_terms(z, al)
            suf = lax.dot_general(_hi_lo(l1m), upper2, (NN, ((), ())), preferred_element_type=F32)
            return al, ls, suf, jnp.sum(l1m, axis=1, keepdims=True), vblk

        def step(done, carry, masked, nb):
            o_acc, run = carry
            ws, vs = [], []
            for n in range(nb):
                al, ls, suf, rs, vblk = chain(i - done - n, masked)
                wgt = jnp.exp(ls + suf + run)
                ws.append((wgt if al is None else jnp.where(al, wgt, 0.0)).astype(BF16))
                vs.append(vblk)
                run = run + rs
            o_acc = o_acc + lax.dot_general(jnp.concatenate(ws, axis=1), jnp.concatenate(vs, axis=0),
                                            (NN, ((), ())), preferred_element_type=F32)
            return o_acc, run

        o_acc, run = _sb_passes(i, step, (jnp.zeros((2 * QB, HD), F32), jnp.zeros((2 * QB, 1), F32)))
        o_ref[...] = jnp.where(first, o_acc[:QB], o_acc[QB:]).astype(BF16)
        r_ref[...] = jnp.where(first, run[:QB], run[QB:])
        end_cargo()

    full = lambda off: pl.BlockSpec((t_pad, HD), lambda pr, i: (0, off + pr))
    blk = pl.BlockSpec((QB, HD), lambda pr, i: (i, pr))
    return pl.pallas_call(
        body, name=name, grid=(SB_H // 2, nq),
        in_specs=[pl.BlockSpec((QB, HD), lambda pr, i: (i, SB_Q0 + pr)), full(SB_K0), full(SB_V0)] + [ANY] * n,
        out_specs=[blk, blk] + [ANY] * n,
        out_shape=[jax.ShapeDtypeStruct((t_pad, SB_H * SB_DH), BF16), jax.ShapeDtypeStruct((t_pad, SB_H * SB_DH), F32)]
        + (exchange[1](cargo) if n else []),
        scratch_shapes=exchange[2](n) if n else [],
        compiler_params=_cparams(),
    )(p, p, p, *cargo)


def _sb_bwd(p, rtot, dy, dy_blk0, pad, name, cargo=(), exchange=None):
    t_pad = p.shape[0]
    nq = t_pad // QB
    n = len(cargo)

    def body(q_ref, k_ref, v_ref, r_ref, do_ref, *rest):
        i = pl.program_id(1)
        pr = pl.program_id(0)
        (dq_ref, dk_ref, dv_ref), end_cargo = _cargo_bounds(rest, n, 3, exchange, (pr == 0) & (i == 0),
                                                            (pr == SB_H // 2 - 1) & (i == nq - 1))

        @pl.when(i == 0)
        def _():
            dk_ref[...] = jnp.zeros_like(dk_ref)
            dv_ref[...] = jnp.zeros_like(dv_ref)

        q2, rowq, first = _sb_stack(q_ref[...] * SB_SCALE, i)
        do2, _, _ = _sb_stack(do_ref[...], i)
        rt = r_ref[...]
        lane = _iota((1, HD), 1)
        rcol = jnp.concatenate([jnp.sum(jnp.where(lane == 0, rt, 0.0), axis=1, keepdims=True),
                                jnp.sum(jnp.where(lane == SB_DH, rt, 0.0), axis=1, keepdims=True)], axis=0)
        rj = _iota((QB, QB), 0)
        cs = _iota((QB, QB), 1)
        tri_u = (rj > cs).astype(BF16)
        tri_l = (rj < cs).astype(BF16)
        upper2 = jnp.concatenate([tri_u, tri_u], axis=0)
        lower2 = jnp.concatenate([tri_l, tri_l], axis=0)

        def chain(kb, masked):
            start = pl.multiple_of(kb * QB, QB)
            kblk = k_ref[pl.ds(start, QB), :].astype(BF16)
            vblk = v_ref[pl.ds(start, QB), :].astype(BF16)
            z = lax.dot_general(q2, kblk, (NT, ((), ())), preferred_element_type=F32)
            colk = kb * QB + _iota((1, QB), 1)
            al = ((colk < rowq) & (colk >= pad)) if masked else None
            l1m, ls, sg = _sb_terms(z, al)
            dwgt = lax.dot_general(do2, vblk, (NT, ((), ())), preferred_element_type=F32)
            suf = lax.dot_general(_hi_lo(l1m), upper2, (NN, ((), ())), preferred_element_type=F32)
            return start, kblk, al, ls, suf, jnp.sum(l1m, axis=1, keepdims=True), dwgt, sg

        def finish(c, seen, gseen):
            start, kblk, al, ls, suf, rs, dwgt, sg = c
            wgt = jnp.exp(ls + suf + (rcol - seen - rs))
            if al is not None:
                wgt = jnp.where(al, wgt, 0.0)
            dl = dwgt * wgt
            gpre = gseen + lax.dot_general(_hi_lo(dl), lower2, (NN, ((), ())), preferred_element_type=F32)
            dz = dl - sg * (dl + gpre)
            if al is not None:
                dz = jnp.where(al, dz, 0.0)
            dz = dz.astype(BF16)
            dk_ref[pl.ds(start, QB), :] += lax.dot_general(dz, q2, (TN, ((), ())), preferred_element_type=F32)
            dv_ref[pl.ds(start, QB), :] += lax.dot_general(wgt.astype(BF16), do2, (TN, ((), ())),
                                                           preferred_element_type=F32)
            return dz, seen + rs, gseen + jnp.sum(dl, axis=1, keepdims=True)

        def step(done, carry, masked, nb):
            dq_acc, seen, gseen = carry
            cs_ = [chain(done + n, masked) for n in range(nb)]
            dzs = []
            for c in cs_:
                dz, seen, gseen = finish(c, seen, gseen)
                dzs.append(dz)
            dq_acc = dq_acc + lax.dot_general(jnp.concatenate(dzs, axis=1), jnp.concatenate([c[1] for c in cs_], axis=0),
                                              (NN, ((), ())), preferred_element_type=F32)
            return dq_acc, seen, gseen

        zc = jnp.zeros((2 * QB, 1), F32)
        dq_acc, _, _ = _sb_passes(i, step, (jnp.zeros((2 * QB, HD), F32), zc, zc))
        dq_ref[...] = jnp.where(first, dq_acc[:QB], dq_acc[QB:]) * SB_SCALE
        end_cargo()

    full_in = lambda off: pl.BlockSpec((t_pad, HD), lambda pr, i: (0, off + pr))
    full_out = pl.BlockSpec((t_pad, HD), lambda pr, i: (0, pr))
    blk = pl.BlockSpec((QB, HD), lambda pr, i: (i, pr))
    sds = jax.ShapeDtypeStruct((t_pad, SB_H * SB_DH), F32)
    return pl.pallas_call(
        body, name=name, grid=(SB_H // 2, nq),
        in_specs=[pl.BlockSpec((QB, HD), lambda pr, i: (i, SB_Q0 + pr)), full_in(SB_K0), full_in(SB_V0), blk,
                  pl.BlockSpec((QB, HD), lambda pr, i: (i, dy_blk0 + pr))] + [ANY] * n,
        out_specs=[blk, full_out, full_out] + [ANY] * n,
        out_shape=[sds, sds, sds] + (exchange[1](cargo) if n else []),
        scratch_shapes=exchange[2](n) if n else [],
        compiler_params=_cparams(),
    )(p, p, p, rtot, dy, *cargo)


HG_LEVELS = 6


def _hg_prefix_matrix():
    t = np.arange(CH)[:, None]
    j = np.arange(CH)[None, :]
    groups = [(j <= t)]
    for lvl in range(1, HG_LEVELS + 1):
        half = CH >> lvl
        e = (t // (2 * half)) * (2 * half) + half - 1
        groups.append(j <= e)
    groups.append(np.ones((8, CH), bool))
    e = np.concatenate(groups, axis=0).astype(np.float32)
    return np.concatenate([e, e, e], axis=1), np.concatenate([e, e, e], axis=0)


HG_G = 4


def _hg_chunk(qr, fr, iv, r0, r1, st, valid, ecat):
    g = st.shape[0]
    mx = jnp.maximum(r0, r1)
    e0 = jnp.exp(r0 - mx)
    e1 = jnp.exp(r1 - mx)
    lb = e1 / (e0 + e1)
    fg = lb + (1.0 - lb) * _sigmoid(fr)
    logf = jnp.where(valid, jnp.log(fg), 0.0)
    kk = jnp.where(valid, 1.0 - fg, 0.0)
    q = jnp.where(valid, _silu(qr), 0.0)
    v = _heads(jnp.where(valid, iv, 0.0), g)

    pre = _mask_dot(ecat, logf)
    b = pre[0:CH]
    b_last = jnp.max(pre[(HG_LEVELS + 1) * CH:], axis=0, keepdims=True)
    row = _iota((CH, 1), 0)
    ri = _iota((1, CH, CH), 1)
    ci = _iota((1, CH, CH), 2)
    a = jnp.where(ri == ci, jnp.sum(_heads(q * kk, g), axis=2, keepdims=True), 0.0)
    for lvl in range(1, HG_LEVELS + 1):
        half = CH >> lvl
        m = pre[lvl * CH:(lvl + 1) * CH]
        low = (row & half) != 0
        dec = jnp.exp(jnp.where(low, b - m, m - b))
        qt = jnp.where(low, q * dec, 0.0)
        kt = jnp.where(low, 0.0, kk * dec)
        same = (ri >> (7 - lvl)) == (ci >> (7 - lvl))
        a = a + jnp.where(same, bbnt(_heads(qt, g), _heads(kt, g)), 0.0)
    o = bbnt(_heads(q * jnp.exp(b), g), st) + bbnn(a, v)
    kd = kk * jnp.exp(b_last - b)
    st_new = st * _heads(jnp.exp(b_last), g) + bbtn(v, _heads(kd, g))
    return o, st_new


def _hg_specs(npair, rev):
    cc = (lambda c: npair - 1 - c) if rev else (lambda c: c)
    ng = HG_H // HG_G
    blk = lambda off: pl.BlockSpec((PAIR, HG_G * HD), lambda h, c: (cc(c), off * ng + h))
    lbs = pl.BlockSpec((2, HG_G * HD), lambda h, c: (0, h))
    state = pl.BlockSpec((1, HG_G, HD, HD), lambda h, c: (cc(c), h, 0, 0))
    return ng, blk, lbs, state


def _hg_fwd(p, lbraw, ecat, pad, name):
    t_pad = p.shape[0]
    npair = t_pad // PAIR
    ng, blk, lbs, state = _hg_specs(npair, False)

    def body(q_ref, f_ref, i_ref, lb_ref, e_ref, et_ref, o_ref, ss_ref, s_ref):
        c = pl.program_id(1)

        @pl.when(c == 0)
        def _():
            s_ref[...] = jnp.zeros_like(s_ref)

        st = s_ref[...]
        ss_ref[0] = st
        for half in (0, 1):
            r = _chunk_rows(half)
            o, st = _hg_chunk(q_ref[r, :], f_ref[r, :], i_ref[r, :], lb_ref[0:1, :], lb_ref[1:2, :], st,
                              _chunk_valid(c, half, pad), (e_ref[...], et_ref[...]))
            _store_heads(o_ref, o, r)
        s_ref[...] = st

    return pl.pallas_call(
        body, name=name, grid=(ng, npair),
        in_specs=[blk(0), blk(1), blk(2), lbs] + [pl.BlockSpec(e.shape, lambda h, c: (0, 0)) for e in ecat],
        out_specs=[blk(0), state],
        out_shape=[jax.ShapeDtypeStruct((t_pad, HG_H * HD), F32), jax.ShapeDtypeStruct((npair, HG_H, HD, HD), F32)],
        scratch_shapes=[pltpu.VMEM((HG_G, HD, HD), F32)],
        compiler_params=_cparams(),
    )(p, p, p, lbraw, *ecat)


def _hg_bwd(p, lbraw, ecat, ssave, do, pad, name, cargo=(), exchange=None):
    t_pad = p.shape[0]
    npair = t_pad // PAIR
    ng, blk, lbs, state = _hg_specs(npair, True)
    n = len(cargo)

    def body(q_ref, f_ref, i_ref, lb_ref, e_ref, et_ref, ss_ref, do_ref, *rest):
        c = pl.program_id(1)
        hg = pl.program_id(0)
        ds_ref = rest[-1]
        (dq_ref, df_ref, di_ref, dlb_ref), end_cargo = _cargo_bounds(
            rest[:-1], n, 4, exchange, (hg == 0) & (c == 0), (hg == ng - 1) & (c == npair - 1))

        @pl.when(c == 0)
        def _():
            ds_ref[...] = jnp.zeros_like(ds_ref)
            dlb_ref[...] = jnp.zeros_like(dlb_ref)

        ra, rb = _chunk_rows(0), _chunk_rows(1)
        va, vb = _chunk_valid(npair - 1 - c, 0, pad), _chunk_valid(npair - 1 - c, 1, pad)
        ecv = (e_ref[...], et_ref[...])

        def pair(qa, fa, ia, qb, fb, ib, r0, r1, st):
            oa, st = _hg_chunk(qa, fa, ia, r0, r1, st, va, ecv)
            ob, st = _hg_chunk(qb, fb, ib, r0, r1, st, vb, ecv)
            return oa, ob, st

        ins = [ref[r, :] for r in (ra, rb) for ref in (q_ref, f_ref, i_ref)]
        _, vjp = jax.vjp(pair, *ins, lb_ref[0:1, :], lb_ref[1:2, :], ss_ref[0])
        g = vjp((_heads(do_ref[ra, :], HG_G), _heads(do_ref[rb, :], HG_G), ds_ref[...]))
        for r, (dq, df, di) in ((ra, g[0:3]), (rb, g[3:6])):
            dq_ref[r, :] = dq.astype(BF16)
            df_ref[r, :] = df.astype(BF16)
            di_ref[r, :] = di.astype(BF16)
        dlb_ref[0:1, :] += g[6]
        dlb_ref[1:2, :] += g[7]
        ds_ref[...] = g[8]
        end_cargo()

    sds = jax.ShapeDtypeStruct((t_pad, HG_H * HD), BF16)
    return pl.pallas_call(
        body, name=name, grid=(ng, npair),
        in_specs=[blk(0), blk(1), blk(2), lbs] + [pl.BlockSpec(e.shape, lambda h, c: (0, 0)) for e in ecat]
        + [state, blk(0)] + [ANY] * n,
        out_specs=[blk(0), blk(0), blk(0), lbs] + [ANY] * n,
        out_shape=[sds, sds, sds, jax.ShapeDtypeStruct((2, HG_H * HD), F32)] + (exchange[1](cargo) if n else []),
        scratch_shapes=(exchange[2](n) if n else []) + [pltpu.VMEM((HG_G, HD, HD), F32)],
        compiler_params=_cparams(),
    )(p, p, p, lbraw, *ecat, ssave, do, *cargo)


def _pad_ab_cols(w):
    z = jnp.zeros((w.shape[0], HD - GDN_H), w.dtype)
    return jnp.concatenate([w[:, :2048], w[:, 2048:2052], z, w[:, 2052:2056], z, w[:, 2056:]], axis=1)


def _unpad_ab_cols(w):
    return jnp.concatenate([w[:, :2048], w[:, 2048:2052], w[:, 2176:2180], w[:, 2304:]], axis=1)


def _lane_pad(v):
    return jnp.pad(v, ((0, 0), (0, HD - v.shape[1])))


def _mlp_fwd(hb, w1, w2, layer):
    a, r = _mm(hb, w1, b_view=("cols", layer), out_dtype=BF16, act=True, name=f"mlp_up_{layer}")
    m = _mm(r, w2, b_view=("rows", layer), name=f"mlp_down_{layer}")
    return a, r, m


def _mlp_bwd(hb, a, r, dmb, w1, w2, layer):
    da = _mm(dmb, w2, tb=True, b_view=("rows", layer), out_dtype=BF16, gate=a, name=f"mlp_down_dx_{layer}")
    dw2 = _mm(r, dmb, ta=True, out_dtype=BF16, name=f"mlp_down_dw_{layer}")
    dh = _mm(da, w1, tb=True, b_view=("cols", layer), name=f"mlp_up_dx_{layer}")
    dw1 = _mm(hb, da, ta=True, out_dtype=BF16, out_split=N_CHIP, name=f"mlp_up_dw_{layer}")
    return dh, dw1, dw2


def _local_step(h0, tgt, w, pad, late=None):
    row = lambda a, i: a[i:i + 1]
    ecat = tuple(jnp.asarray(e, dtype=BF16) for e in _hg_prefix_matrix())
    cw = [w["conv_w"][:, i * 512:(i + 1) * 512] for i in range(3)]
    alog, dtb = _lane_pad(w["a_log"]), _lane_pad(w["dt_bias"])

    h0b = h0.astype(BF16)
    p0 = _mm(h0b, w["ab_w_in"], name="ab_in")
    qn = _conv_fwd(p0, 0, cw[0], "q", pad, "conv_q")
    kn = _conv_fwd(p0, 4, cw[1], "k", pad, "conv_k")
    vn = _conv_fwd(p0, 8, cw[2], "v", pad, "conv_v")
    oa_raw, ss0 = _gdn_fwd(qn, kn, vn, p0, alog, dtb, pad, "gdn_fwd")
    oa = _grms_fwd(oa_raw, p0, 12, w["ab_gnorm_g"], "gdn_gate")
    if late is None:
        ob, rtot = _sb_fwd(p0, pad, "sb_fwd")
    else:
        ob, rtot, g_about, g_cin, g_cout, g_w1, g_w2 = _sb_fwd(p0, pad, "sb_fwd", cargo=late, exchange=GATHER)
        w = dict(w, ab_w_out=g_about.reshape(D, D), c_w_in=g_cin, c_w_out=g_cout.reshape(D, D), mlp_w1=g_w1, mlp_w2=g_w2)
    ycat = jnp.concatenate([oa, ob], axis=1)
    mix0 = _mm(ycat, w["ab_w_out"], name="ab_out")
    h1, h1b = _ln_res_fwd(h0, mix0, row(w["ln_mix_g"], 0), row(w["ln_mix_b"], 0), "ln_mix_0")
    a0, r0, m0 = _mlp_fwd(h1b, w["mlp_w1"], w["mlp_w2"], 0)
    h2, h2b = _ln_res_fwd(h1, m0, row(w["ln_ffn_g"], 0), row(w["ln_ffn_b"], 0), "ln_ffn_0")
    p1 = _mm(h2b, w["c_w_in"], b_view=("cols", 0), name="c_in")
    oc_raw, ss1 = _hg_fwd(p1, w["c_lb_raw"], ecat, pad, "hg_fwd")
    yc = _grms_fwd(oc_raw, p1, 3 * HG_H, w["c_gnorm_g"], "hg_gate")
    mix1 = _mm(yc, w["c_w_out"], name="c_out")
    h3, h3b = _ln_res_fwd(h2, mix1, row(w["ln_mix_g"], 1), row(w["ln_mix_b"], 1), "ln_mix_1")
    a1, r1, m1 = _mlp_fwd(h3b, w["mlp_w1"], w["mlp_w2"], 1)
    h4, _ = _ln_res_fwd(h3, m1, row(w["ln_ffn_g"], 1), row(w["ln_ffn_b"], 1), "ln_ffn_1")
    loss, dh4 = _loss_fwd(h4, tgt, pad + N_META, "loss")

    zero = jnp.zeros_like(dh4)
    dh3a, dm1b, dfg1, dfb1 = _ln_res_bwd(h3, m1, row(w["ln_ffn_g"], 1), row(w["ln_ffn_b"], 1), dh4, zero, "ln_ffn_bwd_1")
    dh3b, dw1_1, dw2_1 = _mlp_bwd(h3b, a1, r1, dm1b, w["mlp_w1"], w["mlp_w2"], 1)
    dh2a, dmix1b, dmg1, dmb1 = _ln_res_bwd(h2, mix1, row(w["ln_mix_g"], 1), row(w["ln_mix_b"], 1), dh3a, dh3b, "ln_mix_bwd_1")
    dyc = _mm(dmix1b, w["c_w_out"], tb=True, name="c_out_dx")
    dwco = _mm(yc, dmix1b, ta=True, out_dtype=BF16, name="c_out_dw")
    doc, dzc, dcg = _grms_bwd(oc_raw, p1, 3 * HG_H, w["c_gnorm_g"], dyc, 0, "hg_gate_bwd")
    landed = {}
    rows4 = lambda a: a.reshape(N_CHIP, -1, D)
    if late is None:
        dq1, df1, di1, dlb = _hg_bwd(p1, w["c_lb_raw"], ecat, ss1, doc, pad, "hg_bwd")
    else:
        dq1, df1, di1, dlb, landed["w1_1"], landed["w2_1"], landed["c_w_out"] = _hg_bwd(
            p1, w["c_lb_raw"], ecat, ss1, doc, pad, "hg_bwd", cargo=[dw1_1, rows4(dw2_1), rows4(dwco)], exchange=SCATTER)
    dp1 = [dq1, df1, di1, dzc]
    dh2b = _mm_groups_nt(dp1, w["c_w_in"], "c_in_dx")
    dwc = jnp.stack([_mm(h2b, d, ta=True, out_dtype=BF16, name=f"c_in_dw_{i}") for i, d in enumerate(dp1)])
    dh1a, dm0b, dfg0, dfb0 = _ln_res_bwd(h1, m0, row(w["ln_ffn_g"], 0), row(w["ln_ffn_b"], 0), dh2a, dh2b, "ln_ffn_bwd_0")
    dh1b, dw1_0, dw2_0 = _mlp_bwd(h1b, a0, r0, dm0b, w["mlp_w1"], w["mlp_w2"], 0)
    dh0a, dmix0b, dmg0, dmb0 = _ln_res_bwd(h0, mix0, row(w["ln_mix_g"], 0), row(w["ln_mix_b"], 0), dh1a, dh1b, "ln_mix_bwd_0")
    dycat = _mm(dmix0b, w["ab_w_out"], tb=True, name="ab_out_dx")
    dwabo = _mm(ycat, dmix0b, ta=True, out_dtype=BF16, name="ab_out_dw")
    doa, dza, dag = _grms_bwd(oa_raw, p0, 12, w["ab_gnorm_g"], dycat, 0, "gdn_gate_bwd")
    if late is None:
        dqn, dkn, dvn, dbb, daa, dal, ddt = _gdn_bwd(qn, kn, vn, p0, alog, dtb, ss0, doa, pad, "gdn_bwd")
        dqb, dkb, dvb = _sb_bwd(p0, rtot, dycat, 4, pad, "sb_bwd")
    else:
        dqn, dkn, dvn, dbb, daa, dal, ddt, landed["c_w_in"], landed["w2_0"] = _gdn_bwd(
            qn, kn, vn, p0, alog, dtb, ss0, doa, pad, "gdn_bwd", cargo=[dwc, rows4(dw2_0)], exchange=SCATTER)
        dqb, dkb, dvb, landed["w1_0"], landed["ab_w_out"] = _sb_bwd(
            p0, rtot, dycat, 4, pad, "sb_bwd", cargo=[dw1_0, rows4(dwabo)], exchange=SCATTER)
    dpq, dcq = _conv_bwd(p0, 0, cw[0], dqn, "q", pad, "conv_q_bwd")
    dpk, dck = _conv_bwd(p0, 4, cw[1], dkn, "k", pad, "conv_k_bwd")
    dpv, dcv = _conv_bwd(p0, 8, cw[2], dvn, "v", pad, "conv_v_bwd")
    dp0 = _assemble_bf16([(dpq, "cols"), (dpk, "cols"), (dpv, "cols"), (dza, "cols"), (dbb, "cols"), (daa, "cols"),
                          (dqb, "cols"), (dkb, "cols"), (dvb, "cols")], "ab_in_dy")
    dwab = _mm(h0b, dp0, ta=True, out_dtype=BF16, name="ab_in_dw")
    if late is None:
        dh0b = _mm(dp0, w["ab_w_in"], tb=True, name="ab_in_dx")
    else:
        dab = jnp.transpose(_unpad_ab_cols(dwab).reshape(D, N_CHIP, AB_TRUE // N_CHIP), (1, 0, 2))
        dh0b, landed["ab_w_in"] = _mm(dp0, w["ab_w_in"], tb=True, name="ab_in_dx", cargo=[dab], exchange=SCATTER)
    dh0 = _add2(dh0a, dh0b, "dh0")

    grads = {
        "ab_w_in": dwab, "conv_w": jnp.concatenate([dcq, dck, dcv], axis=1),
        "a_log": dal[:, :GDN_H], "dt_bias": ddt[:, :GDN_H],
        "ab_gnorm_g": dag, "ab_w_out": dwabo, "c_w_in": dwc, "c_lb_raw": dlb, "c_gnorm_g": dcg, "c_w_out": dwco,
        "ln_mix_g": jnp.concatenate([dmg0, dmg1], 0), "ln_mix_b": jnp.concatenate([dmb0, dmb1], 0),
        "w1_0": dw1_0, "w1_1": dw1_1, "w2_0": dw2_0, "w2_1": dw2_1,
        "ln_ffn_g": jnp.concatenate([dfg0, dfg1], 0), "ln_ffn_b": jnp.concatenate([dfb0, dfb1], 0),
        "landed": landed,
    }
    return loss, dh0, grads


MESH = pl.DeviceIdType.MESH
ANY = pl.BlockSpec(memory_space=pl.ANY)
N_CHIP = 4
N_DEV = 8
CHIP_REL = ((1, 0), (0, 1), (1, 1))
DEV_REL = tuple((dx, dy, dc) for dx in (0, 1) for dy in (0, 1) for dc in (0, 1))[1:]

def _pos():
    return lax.axis_index("x"), lax.axis_index("y"), lax.axis_index("c")


def _flip(a, d):
    return a + d - 2 * a * d


class _Exchange:
    def __init__(self, local, sends, recvs):
        self.local, self.sends, self.recvs = local, sends, recvs

    def start(self):
        for cp in self.local + self.sends:
            cp.start()

    def wait(self):
        for cp in self.recvs:
            cp.wait_recv()
        for cp in self.sends:
            cp.wait_send()
        for cp in self.local:
            cp.wait()


def _gather_sems(n):
    return [pltpu.SemaphoreType.DMA((3 * n,)), pltpu.SemaphoreType.DMA((3 * n,)), pltpu.SemaphoreType.DMA((n,))]


def _gather_copies(x_refs, o_refs, send_sems, recv_sems, local_sems):
    n = len(x_refs)
    x, y, c = _pos()
    local = [pltpu.make_async_copy(x_refs[a], o_refs[a].at[2 * x + y], local_sems.at[a]) for a in range(n)]

    def copy(a, k, sending):
        tx, ty = _flip(x, CHIP_REL[k][0]), _flip(y, CHIP_REL[k][1])
        return pltpu.make_async_remote_copy(
            src_ref=x_refs[a], dst_ref=o_refs[a].at[2 * x + y if sending else 2 * tx + ty],
            send_sem=send_sems.at[3 * a + k], recv_sem=recv_sems.at[3 * a + k], device_id=(tx, ty, c), device_id_type=MESH)

    pairs = [(a, k) for a in range(n) for k in range(3)]
    return _Exchange(local, [copy(a, k, True) for a, k in pairs], [copy(a, k, False) for a, k in pairs])


def _gather_shapes(bufs):
    return [jax.ShapeDtypeStruct((N_CHIP,) + b.shape, b.dtype) for b in bufs]


def _chip_allgather(bufs, name):
    n = len(bufs)

    def body(*refs):
        ex = _gather_copies(refs[:n], refs[n:2 * n], *refs[2 * n:])
        ex.start()
        ex.wait()

    return pl.pallas_call(
        body, name=name, in_specs=[ANY] * n, out_specs=[ANY] * n, out_shape=_gather_shapes(bufs),
        scratch_shapes=_gather_sems(n), compiler_params=pltpu.CompilerParams(has_side_effects=True),
    )(*bufs)


def _scatter_sems(n):
    nr = N_DEV - 1
    return [pltpu.SemaphoreType.DMA((nr * n,)), pltpu.SemaphoreType.DMA((nr * n,)), pltpu.SemaphoreType.DMA((n,))]


def _scatter_copies(g_refs, o_refs, send_sems, recv_sems, local_sems):
    n = len(g_refs)
    nr = N_DEV - 1
    x, y, c = _pos()
    me = 4 * x + 2 * y + c
    local = [pltpu.make_async_copy(g_refs[a].at[2 * x + y], o_refs[a].at[me], local_sems.at[a]) for a in range(n)]

    def copy(a, k, sending):
        dx, dy, dc = DEV_REL[k]
        tx, ty, tc = _flip(x, dx), _flip(y, dy), _flip(c, dc)
        return pltpu.make_async_remote_copy(
            src_ref=g_refs[a].at[2 * tx + ty], dst_ref=o_refs[a].at[me if sending else 4 * tx + 2 * ty + tc],
            send_sem=send_sems.at[nr * a + k], recv_sem=recv_sems.at[nr * a + k],
            device_id=(tx, ty, tc), device_id_type=MESH)

    pairs = [(a, k) for a in range(n) for k in range(nr)]
    return _Exchange(local, [copy(a, k, True) for a, k in pairs], [copy(a, k, False) for a, k in pairs])


def _scatter_shapes(gs):
    return [jax.ShapeDtypeStruct((N_DEV,) + g.shape[1:], g.dtype) for g in gs]


GATHER = (_gather_copies, _gather_shapes, _gather_sems)
SCATTER = (_scatter_copies, _scatter_shapes, _scatter_sems)


def _sum_slots(r, name):
    n, rh, w = r.shape
    tr = _pick(rh, (256, 128, 64, 16))

    def body(r_ref, o_ref):
        acc = r_ref[0].astype(F32)
        for s in range(1, n):
            acc = acc + r_ref[s].astype(F32)
        o_ref[...] = acc

    return pl.pallas_call(
        body, name=name, grid=(rh // tr,), in_specs=[pl.BlockSpec((n, tr, w), lambda i: (0, i, 0))],
        out_specs=pl.BlockSpec((tr, w), lambda i: (i, 0)), out_shape=jax.ShapeDtypeStruct((rh, w), F32),
        compiler_params=_cparams(),
    )(r)


def _small_allreduce(buf, name):
    r, w = buf.shape

    def body(b_ref, o_ref, land_ref, send_sems, recv_sems):
        x, y, c = _pos()
        me = 4 * x + 2 * y + c
        land_ref[me] = b_ref[...]

        def target(k):
            dx, dy, dc = DEV_REL[k]
            return _flip(x, dx), _flip(y, dy), _flip(c, dc)

        sends = []
        for k in range(N_DEV - 1):
            tx, ty, tc = target(k)
            cp = pltpu.make_async_remote_copy(
                src_ref=b_ref, dst_ref=land_ref.at[me], send_sem=send_sems.at[k], recv_sem=recv_sems.at[k],
                device_id=(tx, ty, tc), device_id_type=MESH)
            cp.start()
            sends.append(cp)
        for k in range(N_DEV - 1):
            tx, ty, tc = target(k)
            pltpu.make_async_remote_copy(
                src_ref=b_ref, dst_ref=land_ref.at[4 * tx + 2 * ty + tc], send_sem=send_sems.at[k],
                recv_sem=recv_sems.at[k], device_id=(tx, ty, tc), device_id_type=MESH).wait_recv()
        for cp in sends:
            cp.wait_send()
        acc = land_ref[0]
        for s in range(1, N_DEV):
            acc = acc + land_ref[s]
        o_ref[...] = acc

    vm = pl.BlockSpec(memory_space=pltpu.VMEM)
    return pl.pallas_call(
        body, name=name, in_specs=[vm], out_specs=vm, out_shape=jax.ShapeDtypeStruct((r, w), F32),
        scratch_shapes=[pltpu.VMEM((N_DEV, r, w), F32), pltpu.SemaphoreType.DMA((N_DEV - 1,)),
                        pltpu.SemaphoreType.DMA((N_DEV - 1,))],
        compiler_params=pltpu.CompilerParams(has_side_effects=True),
    )(buf)


def _adamw(w, g, m, v, name):
    r, c = w.shape
    tr = _pick(r, (256, 128, 64, 8)) if r * c > (1 << 18) else r

    def body(w_ref, g_ref, m_ref, v_ref, d_ref, m2_ref, v2_ref):
        gg = g_ref[...]
        m2 = ADAM_B1 * m_ref[...] + (1.0 - ADAM_B1) * gg
        v2 = ADAM_B2 * v_ref[...] + (1.0 - ADAM_B2) * (gg * gg)
        m_hat = m2 / (1.0 - ADAM_B1 ** ADAM_STEP)
        v_hat = v2 / (1.0 - ADAM_B2 ** ADAM_STEP)
        d_ref[...] = -ADAM_LR * (m_hat / (jnp.sqrt(v_hat) + ADAM_EPS) + ADAM_WD * w_ref[...])
        m2_ref[...] = m2
        v2_ref[...] = v2

    blk = pl.BlockSpec((tr, c), lambda i: (i, 0))
    sds = jax.ShapeDtypeStruct((r, c), F32)
    return pl.pallas_call(body, name=name, grid=(r // tr,), in_specs=[blk] * 4, out_specs=[blk] * 3,
                          out_shape=[sds] * 3, compiler_params=_cparams())(w, g, m, v)


BIG = ("ab_w_in", "ab_w_out", "c_w_in", "c_w_out", "mlp_w1", "mlp_w2")
SMALL = ("ln_mix_g", "ln_mix_b", "ln_ffn_g", "ln_ffn_b", "c_lb_raw", "ab_a_log", "ab_dt_bias", "ab_gnorm_g", "c_gnorm_g")
SMALL_ROWS = 16
CONV_ROWS = 8
CONV_W = 3 * GDN_H * HD


def _conv_to_rows(cw):
    return jnp.pad(cw, ((0, 0), (0, 2 * D - CONV_W))).reshape(CONV_ROWS, D)


def _rows_to_conv(rows):
    return rows.reshape(CONV_K, 2 * D)[:, :CONV_W]


def _pack_small(d):
    rows = [jnp.pad(d[n], ((0, 0), (0, D - d[n].shape[1]))) for n in SMALL]
    buf = jnp.concatenate(rows, axis=0)
    return jnp.pad(buf, ((0, SMALL_ROWS - buf.shape[0]), (0, 0)))


def _unpack_small(buf, like):
    out, r = {}, 0
    for n in SMALL:
        nr, nc = like[n].shape
        out[n] = buf[r:r + nr, :nc]
        r += nr
    return out


def kernel(x, meta_tokens, ab_w_in, ab_conv_w, ab_a_log, ab_dt_bias, ab_gnorm_g, ab_w_out, c_w_in, c_lb_raw, c_gnorm_g, c_w_out, ln_mix_g, ln_mix_b, mlp_w1, mlp_w2, ln_ffn_g, ln_ffn_b, loss_target, m_meta_tokens, m_ab_w_in, m_ab_conv_w, m_ab_a_log, m_ab_dt_bias, m_ab_gnorm_g, m_ab_w_out, m_c_w_in, m_c_lb_raw, m_c_gnorm_g, m_c_w_out, m_ln_mix_g, m_ln_mix_b, m_mlp_w1, m_mlp_w2, m_ln_ffn_g, m_ln_ffn_b, v_meta_tokens, v_ab_w_in, v_ab_conv_w, v_ab_a_log, v_ab_dt_bias, v_ab_gnorm_g, v_ab_w_out, v_c_w_in, v_c_lb_raw, v_c_gnorm_g, v_c_w_out, v_ln_mix_g, v_ln_mix_b, v_mlp_w1, v_mlp_w2, v_ln_ffn_g, v_ln_ffn_b):
    names = ("meta_tokens", "ab_w_in", "ab_conv_w", "ab_a_log", "ab_dt_bias", "ab_gnorm_g", "ab_w_out", "c_w_in",
             "c_lb_raw", "c_gnorm_g", "c_w_out", "ln_mix_g", "ln_mix_b", "mlp_w1", "mlp_w2", "ln_ffn_g", "ln_ffn_b")
    wts = dict(zip(names, (meta_tokens, ab_w_in, ab_conv_w, ab_a_log, ab_dt_bias, ab_gnorm_g, ab_w_out, c_w_in, c_lb_raw,
                           c_gnorm_g, c_w_out, ln_mix_g, ln_mix_b, mlp_w1, mlp_w2, ln_ffn_g, ln_ffn_b)))
    mom_m = dict(zip(names, (m_meta_tokens, m_ab_w_in, m_ab_conv_w, m_ab_a_log, m_ab_dt_bias, m_ab_gnorm_g, m_ab_w_out,
                             m_c_w_in, m_c_lb_raw, m_c_gnorm_g, m_c_w_out, m_ln_mix_g, m_ln_mix_b, m_mlp_w1, m_mlp_w2,
                             m_ln_ffn_g, m_ln_ffn_b)))
    mom_v = dict(zip(names, (v_meta_tokens, v_ab_w_in, v_ab_conv_w, v_ab_a_log, v_ab_dt_bias, v_ab_gnorm_g, v_ab_w_out,
                             v_c_w_in, v_c_lb_raw, v_c_gnorm_g, v_c_w_out, v_ln_mix_g, v_ln_mix_b, v_mlp_w1, v_mlp_w2,
                             v_ln_ffn_g, v_ln_ffn_b)))
    seq = x.shape[1]
    pad = (-(N_META + seq)) % QB
    xi, yi, ci = _pos()
    chip = 2 * xi + yi

    gat_ab_in, = _chip_allgather([ab_w_in[0].astype(BF16)], "gather_weights")
    late = [ab_w_out[0].astype(BF16), c_w_in.astype(BF16), c_w_out[0].astype(BF16), mlp_w1.astype(BF16),
            mlp_w2.astype(BF16)]
    mcols, ccols = meta_tokens.shape[1], ab_conv_w.shape[2]
    place = jnp.concatenate([
        lax.dynamic_update_slice(jnp.zeros((N_META, D), F32), 0.5 * meta_tokens, (0, chip * mcols)),
        _conv_to_rows(lax.dynamic_update_slice(jnp.zeros((CONV_K, CONV_W), F32), 0.5 * ab_conv_w[0], (0, chip * ccols)))],
        axis=0)
    placed = _small_allreduce(place, "gather_meta")
    meta_full = placed[:N_META]

    w = {
        "ab_w_in": _pad_ab_cols(jnp.transpose(gat_ab_in, (1, 0, 2)).reshape(D, AB_TRUE)),
        "conv_w": _rows_to_conv(placed[N_META:]), "a_log": ab_a_log, "dt_bias": ab_dt_bias,
        "ab_gnorm_g": ab_gnorm_g, "c_lb_raw": c_lb_raw,
        "c_gnorm_g": c_gnorm_g, "ln_mix_g": ln_mix_g, "ln_mix_b": ln_mix_b, "ln_ffn_g": ln_ffn_g, "ln_ffn_b": ln_ffn_b,
    }

    h0 = jnp.concatenate([jnp.zeros((pad, D), F32), meta_full, x[0]], axis=0)
    tgt = jnp.concatenate([jnp.zeros((pad + N_META, D), F32), loss_target[0]], axis=0)
    loss8, dh0, g = _local_step(h0, tgt, w, pad, late)
    loss = lax.psum(loss8[0, 0], ("x", "y", "c"))
    grad_x = dh0[pad + N_META:][None]

    gsmall = {"ln_mix_g": g["ln_mix_g"], "ln_mix_b": g["ln_mix_b"], "ln_ffn_g": g["ln_ffn_g"], "ln_ffn_b": g["ln_ffn_b"],
              "c_lb_raw": g["c_lb_raw"], "ab_a_log": g["a_log"], "ab_dt_bias": g["dt_bias"], "ab_gnorm_g": g["ab_gnorm_g"],
              "c_gnorm_g": g["c_gnorm_g"]}
    sbuf = jnp.concatenate([_pack_small(gsmall), dh0[pad:pad + N_META], _conv_to_rows(g["conv_w"])], axis=0)
    ssum = _small_allreduce(sbuf, "allreduce_small")
    grads = _unpack_small(ssum[:SMALL_ROWS], wts)
    grads["meta_tokens"] = lax.dynamic_slice(ssum[SMALL_ROWS:SMALL_ROWS + N_META], (0, chip * mcols), (N_META, mcols))
    grads["ab_conv_w"] = lax.dynamic_slice(_rows_to_conv(ssum[SMALL_ROWS + N_META:]), (0, chip * ccols), (CONV_K, ccols))[None]

    sums = {k: _sum_slots(v, f"grad_sum_{k}") for k, v in g["landed"].items()}
    for n in ("ab_w_in", "ab_w_out", "c_w_in", "c_w_out"):
        grads[n] = sums[n][None]
    grads["mlp_w1"] = jnp.stack([sums["w1_0"], sums["w1_1"]])
    grads["mlp_w2"] = jnp.stack([sums["w2_0"], sums["w2_1"]])

    delta, new_m, new_v = {}, {}, {}
    for n in ("meta_tokens", "ab_conv_w") + BIG:
        shp = wts[n].shape
        to2 = lambda a: a.reshape(-1, shp[-1])
        d2, m2, v2 = _adamw(to2(wts[n]), to2(grads[n]), to2(mom_m[n]), to2(mom_v[n]), f"adamw_{n}")
        delta[n], new_m[n], new_v[n] = d2.reshape(shp), m2.reshape(shp), v2.reshape(shp)
    d2, m2, v2 = _adamw(_pack_small(wts), ssum[:SMALL_ROWS], _pack_small(mom_m), _pack_small(mom_v), "adamw_small")
    delta.update(_unpack_small(d2, wts))
    new_m.update(_unpack_small(m2, wts))
    new_v.update(_unpack_small(v2, wts))

    return (loss, grad_x, *[grads[n] for n in names], *[delta[n] for n in names], *[new_m[n] for n in names],
            *[new_v[n] for n in names])
```

```python
import functools
import math

import numpy as np
import jax
import jax.numpy as jnp
from jax import lax
from jax.experimental import pallas as pl
from jax.experimental.pallas import tpu as pltpu

F32 = jnp.float32
BF16 = jnp.bfloat16

D = 1024
N_META = 16
D_FF = 4 * D
DEPTH = 2
GDN_H = 4
SB_H = 8
SB_DH = 64
HG_H = 8
HD = 128
CH = 64
QB = 128
ALPHA = float((2 * DEPTH) ** 0.25)
LN_EPS = 1e-5
RMS_EPS = 1e-6
L2_EPS = 1e-6
NEG = -1e30

ADAM_LR = 0.001
ADAM_B1 = 0.9
ADAM_B2 = 0.999
ADAM_EPS = 1e-08
ADAM_WD = 0.01
ADAM_STEP = 10

AB_W = 30 * HD
AB_TRUE = 3592
VMEM_LIMIT = 56 * 1024 * 1024

NN = ((1,), (0,))
NT = ((1,), (1,))
TN = ((0,), (0,))


def _cparams(**kw):
    return pltpu.CompilerParams(vmem_limit_bytes=VMEM_LIMIT, **kw)


def _dg(a, b, dims, mode):
    if mode == "h":
        return lax.dot_general(a, b, dims, precision=lax.Precision.HIGHEST, preferred_element_type=F32)
    if mode == "b":
        return lax.dot_general(a.astype(BF16), b.astype(BF16), dims, preferred_element_type=F32)
    ah, bh = a.astype(BF16), b.astype(BF16)
    al, bl = (a - ah.astype(F32)).astype(BF16), (b - bh.astype(F32)).astype(BF16)
    d = lambda x, y: lax.dot_general(x, y, dims, preferred_element_type=F32)
    return d(ah, bh) + (d(ah, bl) + d(al, bh))


def _make_dots(mode, batched=False):
    if batched:
        nn_d, nt_d, tn_d = (((2,), (1,)), ((0,), (0,))), (((2,), (2,)), ((0,), (0,))), (((1,), (1,)), ((0,), (0,)))
    else:
        nn_d, nt_d, tn_d = (NN, ((), ())), (NT, ((), ())), (TN, ((), ()))

    @jax.custom_vjp
    def nn(a, b):
        return _dg(a, b, nn_d, mode)

    @jax.custom_vjp
    def nt(a, b):
        return _dg(a, b, nt_d, mode)

    @jax.custom_vjp
    def tn(a, b):
        return _dg(a, b, tn_d, mode)

    nn.defvjp(lambda a, b: (nn(a, b), (a, b)), lambda r, g: (nt(g, r[1]), tn(r[0], g)))
    nt.defvjp(lambda a, b: (nt(a, b), (a, b)), lambda r, g: (nn(g, r[1]), tn(g, r[0])))
    tn.defvjp(lambda a, b: (tn(a, b), (a, b)), lambda r, g: (nt(r[1], g), nn(r[0], g)))
    return nn, nt, tn


hnn, hnt, htn = _make_dots("h")
bbnn, bbnt, bbtn = _make_dots("b", True)
mbnn, mbnt, mbtn = _make_dots("m", True)
hbnn, hbnt, hbtn = _make_dots("h", True)


def _split3(x, axis):
    x1 = x.astype(BF16)
    r1 = x - x1.astype(F32)
    x2 = r1.astype(BF16)
    x3 = (r1 - x2.astype(F32)).astype(BF16)
    return jnp.concatenate([x1, x2, x3], axis=axis)


@jax.custom_vjp
def _mask_dot(e3, x):
    return lax.dot_general(e3[0], _split3(x, 0), (NN, ((), ())), preferred_element_type=F32)


def _mask_dot_bwd(e3, g):
    dx = lax.dot_general(e3[1], _split3(g, 0), (TN, ((), ())), preferred_element_type=F32)
    return (jnp.zeros_like(e3[0]), jnp.zeros_like(e3[1])), dx


_mask_dot.defvjp(lambda e3, x: (_mask_dot(e3, x), e3), _mask_dot_bwd)


def _heads(a, n):
    return jnp.concatenate([a[None, :, h * HD:(h + 1) * HD] for h in range(n)], axis=0)


def _sigmoid(x):
    return jax.nn.sigmoid(x)


def _silu(x):
    return x * jax.nn.sigmoid(x)


def _softplus(x):
    return jnp.maximum(x, 0.0) + jnp.log(1.0 + jnp.exp(-jnp.abs(x)))


def _iota(shape, dim):
    return lax.broadcasted_iota(jnp.int32, shape, dim)


def _pick(n, prefs):
    for p in prefs:
        if n % p == 0:
            return p
    return n


def _mm(a, b, *, ta=False, tb=False, out_dtype=F32, name, b_view=None, out_split=0, act=False, gate=None,
        cargo=(), exchange=None):
    if ta:
        k_dim, m_dim = a.shape
    else:
        m_dim, k_dim = a.shape
    if b_view is None:
        w_rows, w_cols = b.shape
    else:
        kind, layer = b_view
        nj, _, blk_r, blk_c = b.shape
        w_rows, w_cols = (blk_r, nj * blk_c) if kind == "cols" else (nj * blk_r, blk_c)
    n_dim = w_rows if tb else w_cols
    assert (w_cols if tb else w_rows) == k_dim
    tm = _pick(m_dim, (1024, 1056, 704, 640, 512, 384, 256, 128))
    tn = _pick(n_dim, (1024, 1056, 704, 640, 512, 384, 256, 128))
    tk = _pick(k_dim, (1024, 1056, 704, 512, 384, 256, 128))
    nk = k_dim // tk
    a_spec = pl.BlockSpec((tk, tm), lambda i, j, k: (k, i)) if ta else pl.BlockSpec((tm, tk), lambda i, j, k: (i, k))
    wb = (tn, tk) if tb else (tk, tn)
    w_idx = (lambda i, j, k: (j, k)) if tb else (lambda i, j, k: (k, j))
    if b_view is None:
        b_spec = pl.BlockSpec(wb, w_idx)
    elif kind == "cols":
        per = blk_c // wb[1]
        b_spec = pl.BlockSpec((None, None) + wb,
                              lambda i, j, k: (w_idx(i, j, k)[1] // per, layer, w_idx(i, j, k)[0], w_idx(i, j, k)[1] % per))
    else:
        per = blk_r // wb[0]
        b_spec = pl.BlockSpec((None, None) + wb,
                              lambda i, j, k: (w_idx(i, j, k)[0] // per, layer, w_idx(i, j, k)[0] % per, w_idx(i, j, k)[1]))
    if out_split:
        per_o = (n_dim // out_split) // tn
        out_spec = pl.BlockSpec((None, tm, tn), lambda i, j, k: (j // per_o, i, j % per_o))
        out_sds = jax.ShapeDtypeStruct((out_split, m_dim, n_dim // out_split), out_dtype)
    else:
        out_spec = pl.BlockSpec((tm, tn), lambda i, j, k: (i, j))
        out_sds = jax.ShapeDtypeStruct((m_dim, n_dim), out_dtype)
    dims = (((0 if ta else 1,), (1 if tb else 0,)), ((), ()))
    extra = [] if gate is None else [gate]
    n_out = 2 if act else 1

    def finish(acc, refs):
        if act:
            refs[0][...] = acc.astype(refs[0].dtype)
            r = jnp.maximum(acc, 0.0)
            refs[1][...] = (r * r).astype(refs[1].dtype)
        elif gate is not None:
            refs[1][...] = (acc * (2.0 * jnp.maximum(refs[0][...].astype(F32), 0.0))).astype(refs[1].dtype)
        else:
            refs[0][...] = acc.astype(refs[0].dtype)

    grid = (m_dim // tm, n_dim // tn, nk)
    nc = len(cargo)

    def body(a_ref, b_ref, *rest):
        acc_ref = rest[-1]
        ids = [pl.program_id(d) for d in range(3)]
        outs, end_cargo = _cargo_bounds(
            rest[len(extra):-1], nc, n_out, exchange, (ids[0] == 0) & (ids[1] == 0) & (ids[2] == 0),
            (ids[0] == grid[0] - 1) & (ids[1] == grid[1] - 1) & (ids[2] == grid[2] - 1))
        refs = tuple(rest[:len(extra)]) + tuple(outs)
        part = lax.dot_general(a_ref[...], b_ref[...], dims, preferred_element_type=F32)
        if nk == 1:
            finish(part, refs)
        else:
            k = ids[2]

            @pl.when(k == 0)
            def _():
                acc_ref[...] = part

            @pl.when(k > 0)
            def _():
                acc_ref[...] += part

            @pl.when(k == nk - 1)
            def _():
                finish(acc_ref[...], refs)
        end_cargo()

    out = pl.pallas_call(
        body, name=name, grid=grid,
        in_specs=[a_spec, b_spec] + [pl.BlockSpec((tm, tn), lambda i, j, k: (i, j))] * len(extra) + [ANY] * nc,
        out_specs=[out_spec] * n_out + [ANY] * nc,
        out_shape=[out_sds] * n_out + (exchange[1](cargo) if nc else []),
        scratch_shapes=(exchange[2](nc) if nc else []) + [pltpu.VMEM((tm, tn) if nk > 1 else (8, 128), F32)],
        compiler_params=_cparams(dimension_semantics=("arbitrary",) * 3 if nc else ("parallel", "parallel", "arbitrary")),
    )(a, b, *extra, *cargo)
    if nc:
        return out
    return out if act else out[0]


def _mm_groups_nt(parts, b, name):
    m_dim, k_dim = parts[0].shape
    ng, _, n_dim, _ = b.shape
    assert len(parts) == ng and b.shape[3] == k_dim
    tm = _pick(m_dim, (1056, 704, 512, 384, 256, 128))

    def body(*refs):
        a_refs, b_ref, o_ref, acc_ref = refs[:ng], refs[ng], refs[ng + 1], refs[ng + 2]
        k = pl.program_id(1)
        for g in range(ng):
            @pl.when(k == g)
            def _(g=g):
                part = lax.dot_general(a_refs[g][...], b_ref[...], (NT, ((), ())), preferred_element_type=F32)
                if g == 0:
                    acc_ref[...] = part
                elif g < ng - 1:
                    acc_ref[...] += part
                else:
                    o_ref[...] = acc_ref[...] + part

    return pl.pallas_call(
        body, name=name, grid=(m_dim // tm, ng),
        in_specs=[pl.BlockSpec((tm, k_dim), lambda i, k: (i, 0))] * ng
        + [pl.BlockSpec((None, None, n_dim, k_dim), lambda i, k: (k, 0, 0, 0))],
        out_specs=pl.BlockSpec((tm, n_dim), lambda i, k: (i, 0)),
        out_shape=jax.ShapeDtypeStruct((m_dim, n_dim), F32),
        scratch_shapes=[pltpu.VMEM((tm, n_dim), F32)],
        compiler_params=_cparams(dimension_semantics=("parallel", "arbitrary")),
    )(*parts, b)


def _row_tile(t_pad, width):
    for tr in (528, 352, 176, 128, 64):
        if t_pad % tr == 0 and tr * width * 4 <= (3 << 19) and tr % 16 == 0:
            return tr
    return 64 if t_pad % 64 == 0 else t_pad


def _ln_res_fn(h, m, g, b):
    x = ALPHA * h + m
    mu = jnp.mean(x, axis=-1, keepdims=True)
    xc = x - mu
    var = jnp.mean(xc * xc, axis=-1, keepdims=True)
    return xc * lax.rsqrt(var + LN_EPS) * g + b


def _ln_res_fwd(h, m, g, b, name):
    t_pad = h.shape[0]
    tr = _row_tile(t_pad, D)

    def body(h_ref, m_ref, g_ref, b_ref, y_ref, yb_ref):
        y = _ln_res_fn(h_ref[...], m_ref[...], g_ref[...], b_ref[...])
        y_ref[...] = y
        yb_ref[...] = y.astype(BF16)

    row = pl.BlockSpec((tr, D), lambda i: (i, 0))
    par = pl.BlockSpec((1, D), lambda i: (0, 0))
    return pl.pallas_call(
        body, name=name, grid=(t_pad // tr,), in_specs=[row, row, par, par], out_specs=[row, row],
        out_shape=[jax.ShapeDtypeStruct((t_pad, D), F32), jax.ShapeDtypeStruct((t_pad, D), BF16)],
        compiler_params=_cparams(),
    )(h, m, g, b)


def _ln_res_bwd(h, m, g, b, dy1, dy2, name):
    t_pad = h.shape[0]
    tr = _row_tile(t_pad, D)

    def body(h_ref, m_ref, g_ref, b_ref, d1_ref, d2_ref, dh_ref, dm_ref, dg_ref, db_ref):
        _, vjp = jax.vjp(_ln_res_fn, h_ref[...], m_ref[...], g_ref[...], b_ref[...])
        dh, dm, dg, db = vjp(d1_ref[...] + d2_ref[...])
        dh_ref[...] = dh
        dm_ref[...] = dm.astype(BF16)

        @pl.when(pl.program_id(0) == 0)
        def _():
            dg_ref[...] = jnp.zeros_like(dg_ref)
            db_ref[...] = jnp.zeros_like(db_ref)

        dg_ref[...] += dg
        db_ref[...] += db

    row = pl.BlockSpec((tr, D), lambda i: (i, 0))
    par = pl.BlockSpec((1, D), lambda i: (0, 0))
    return pl.pallas_call(
        body, name=name, grid=(t_pad // tr,), in_specs=[row, row, par, par, row, row],
        out_specs=[row, row, par, par],
        out_shape=[jax.ShapeDtypeStruct((t_pad, D), F32), jax.ShapeDtypeStruct((t_pad, D), BF16),
                   jax.ShapeDtypeStruct((1, D), F32), jax.ShapeDtypeStruct((1, D), F32)],
        compiler_params=_cparams(),
    )(h, m, g, b, dy1, dy2)


def _grms_fn(o, z, g):
    y = o * lax.rsqrt(jnp.mean(o * o, axis=-1, keepdims=True) + RMS_EPS) * g
    return y * _silu(z)


def _grms_fwd(o, z_arr, z_blk0, g, name):
    t_pad, w = o.shape
    tr = _row_tile(t_pad, w)
    assert (z_blk0 * HD) % w == 0

    def body(o_ref, z_ref, g_ref, y_ref):
        for h in range(w // HD):
            c = slice(h * HD, (h + 1) * HD)
            y_ref[:, c] = _grms_fn(o_ref[:, c], z_ref[:, c], g_ref[...]).astype(BF16)

    return pl.pallas_call(
        body, name=name, grid=(t_pad // tr,),
        in_specs=[pl.BlockSpec((tr, w), lambda i: (i, 0)), pl.BlockSpec((tr, w), lambda i: (i, z_blk0 * HD // w)),
                  pl.BlockSpec((1, HD), lambda i: (0, 0))],
        out_specs=pl.BlockSpec((tr, w), lambda i: (i, 0)),
        out_shape=jax.ShapeDtypeStruct((t_pad, w), BF16), compiler_params=_cparams(),
    )(o, z_arr, g)


def _grms_bwd(o, z_arr, z_blk0, g, dy_arr, dy_blk0, name):
    t_pad, w = o.shape
    tr = _row_tile(t_pad, w)
    assert (z_blk0 * HD) % w == 0 and (dy_blk0 * HD) % w == 0

    def body(o_ref, z_ref, g_ref, dy_ref, do_ref, dz_ref, dg_ref):
        @pl.when(pl.program_id(0) == 0)
        def _():
            dg_ref[...] = jnp.zeros_like(dg_ref)

        for h in range(w // HD):
            c = slice(h * HD, (h + 1) * HD)
            _, vjp = jax.vjp(_grms_fn, o_ref[:, c], z_ref[:, c], g_ref[...])
            do, dz, dg = vjp(dy_ref[:, c])
            do_ref[:, c] = do
            dz_ref[:, c] = dz.astype(BF16)
            dg_ref[...] += dg

    blk = pl.BlockSpec((tr, w), lambda i: (i, 0))
    return pl.pallas_call(
        body, name=name, grid=(t_pad // tr,),
        in_specs=[blk, pl.BlockSpec((tr, w), lambda i: (i, z_blk0 * HD // w)), pl.BlockSpec((1, HD), lambda i: (0, 0)),
                  pl.BlockSpec((tr, w), lambda i: (i, dy_blk0 * HD // w))],
        out_specs=[blk, blk, pl.BlockSpec((1, HD), lambda i: (0, 0))],
        out_shape=[jax.ShapeDtypeStruct((t_pad, w), F32), jax.ShapeDtypeStruct((t_pad, w), BF16),
                   jax.ShapeDtypeStruct((1, HD), F32)],
        compiler_params=_cparams(),
    )(o, z_arr, g, dy_arr)


def _loss_fwd(y, tgt, first_row, name):
    t_pad = y.shape[0]
    tr = _row_tile(t_pad, D)

    def body(y_ref, t_ref, l_ref, dy_ref):
        rows = pl.program_id(0) * tr + _iota((tr, 1), 0)
        err = jnp.where(rows >= first_row, y_ref[...] - t_ref[...], 0.0)
        dy_ref[...] = err * (1.0 / D)

        @pl.when(pl.program_id(0) == 0)
        def _():
            l_ref[...] = jnp.zeros_like(l_ref)

        part = jnp.sum(jnp.sum(err * err, axis=1, keepdims=True), axis=0, keepdims=True)
        l_ref[...] += jnp.broadcast_to(part * (0.5 / D), l_ref.shape)

    row = pl.BlockSpec((tr, D), lambda i: (i, 0))
    return pl.pallas_call(
        body, name=name, grid=(t_pad // tr,), in_specs=[row, row],
        out_specs=[pl.BlockSpec((8, 128), lambda i: (0, 0)), row],
        out_shape=[jax.ShapeDtypeStruct((8, 128), F32), jax.ShapeDtypeStruct((t_pad, D), F32)],
        compiler_params=_cparams(),
    )(y, tgt)


def _add2(a, b, name):
    t_pad, w = a.shape
    tr = _row_tile(t_pad, w)

    def body(a_ref, b_ref, o_ref):
        o_ref[...] = a_ref[...] + b_ref[...]

    row = pl.BlockSpec((tr, w), lambda i: (i, 0))
    return pl.pallas_call(body, name=name, grid=(t_pad // tr,), in_specs=[row, row], out_specs=row,
                          out_shape=jax.ShapeDtypeStruct((t_pad, w), F32), compiler_params=_cparams())(a, b)


def _assemble_bf16(parts, name):
    t_pad = parts[0][0].shape[0] if parts[0][1] == "cols" else parts[0][0].shape[1]
    widths = [p.shape[1] if kind == "cols" else HD for p, kind in parts]
    total = sum(widths)
    tr = _row_tile(t_pad, total)

    def body(*refs):
        o_ref = refs[-1]
        off = 0
        for ref, (p, kind), w in zip(refs[:-1], parts, widths):
            if kind == "cols":
                o_ref[:, off:off + w] = ref[...].astype(BF16)
            else:
                acc = ref[0]
                for hh in range(1, p.shape[0]):
                    acc = acc + ref[hh]
                o_ref[:, off:off + w] = acc.astype(BF16)
            off += w

    specs = []
    for p, kind in parts:
        if kind == "cols":
            specs.append(pl.BlockSpec((tr, p.shape[1]), lambda i: (i, 0)))
        else:
            specs.append(pl.BlockSpec((p.shape[0], tr, HD), lambda i: (0, i, 0)))
    return pl.pallas_call(
        body, name=name, grid=(t_pad // tr,), in_specs=specs,
        out_specs=pl.BlockSpec((tr, total), lambda i: (i, 0)),
        out_shape=jax.ShapeDtypeStruct((t_pad, total), BF16), compiler_params=_cparams(),
    )(*[p for p, _ in parts])


CONV_K = 4
HALO = 8
RT = 128


def _conv_fwd(p, blk0, w, mode, pad, name):
    t_pad = p.shape[0]
    nt = t_pad // RT
    scale = HD ** -0.5 if mode == "q" else 1.0

    def body(x_ref, w_ref, y_ref, xs_ref):
        xs_ref[0:HALO, :] = jnp.zeros((HALO, HD), F32)
        rows = _iota((t_pad, 1), 0)
        xs_ref[HALO:HALO + t_pad, :] = jnp.where(rows >= pad, x_ref[...], 0.0)
        wv = w_ref[...]

        def tile(i, carry):
            r0 = pl.multiple_of(i * RT, RT)
            ext = xs_ref[pl.ds(r0, RT + HALO), :]
            acc = ext[HALO:, :] * wv[3:4, :]
            for s in (1, 2, 3):
                acc = acc + pltpu.roll(ext, s, 0)[HALO:, :] * wv[3 - s:4 - s, :]
            y = _silu(acc)
            if mode != "v":
                y = y * lax.rsqrt(jnp.sum(y * y, axis=-1, keepdims=True) + L2_EPS) * scale
            y_ref[pl.ds(r0, RT), :] = y
            return carry

        lax.fori_loop(0, nt, tile, 0)

    return pl.pallas_call(
        body, name=name, grid=(GDN_H,),
        in_specs=[pl.BlockSpec((t_pad, HD), lambda h: (0, blk0 + h)), pl.BlockSpec((CONV_K, HD), lambda h: (0, h))],
        out_specs=pl.BlockSpec((t_pad, HD), lambda h: (0, h)),
        out_shape=jax.ShapeDtypeStruct((t_pad, GDN_H * HD), F32),
        scratch_shapes=[pltpu.VMEM((t_pad + HALO, HD), F32)],
        compiler_params=_cparams(),
    )(p, w)


def _conv_bwd(p, blk0, w, dn, mode, pad, name):
    t_pad = p.shape[0]
    nt = t_pad // RT
    scale = HD ** -0.5 if mode == "q" else 1.0

    def body(x_ref, w_ref, dn_ref, dx_ref, dw_ref, xs_ref, ds_ref):
        xs_ref[0:HALO, :] = jnp.zeros((HALO, HD), F32)
        xs_ref[HALO + t_pad:HALO + t_pad + 2 * HALO, :] = jnp.zeros((2 * HALO, HD), F32)
        ds_ref[t_pad:t_pad + HALO, :] = jnp.zeros((HALO, HD), F32)
        rows = _iota((t_pad, 1), 0)
        xs_ref[HALO:HALO + t_pad, :] = jnp.where(rows >= pad, x_ref[...], 0.0)
        ds_ref[0:t_pad, :] = dn_ref[...]
        wv = w_ref[...]

        def tile(i, dw):
            r0 = pl.multiple_of(i * RT, RT)
            ext = xs_ref[pl.ds(r0, RT + 2 * HALO), :]
            dn_e = ds_ref[pl.ds(r0, RT + HALO), :]
            xsh = [ext[HALO:, :]] + [pltpu.roll(ext, s, 0)[HALO:, :] for s in (1, 2, 3)]
            pre = xsh[0] * wv[3:4, :]
            for s in (1, 2, 3):
                pre = pre + xsh[s] * wv[3 - s:4 - s, :]
            sg = _sigmoid(pre)
            y = pre * sg
            if mode != "v":
                ss = jnp.sum(y * y, axis=-1, keepdims=True) + L2_EPS
                r = lax.rsqrt(ss)
                dy = scale * (dn_e * r - y * (r * r * r) * jnp.sum(dn_e * y, axis=-1, keepdims=True))
            else:
                dy = dn_e
            dpre = dy * (sg * (1.0 + pre * (1.0 - sg)))
            dx = dpre[:RT, :] * wv[3:4, :]
            for s in (1, 2, 3):
                dx = dx + pltpu.roll(dpre, RT + HALO - s, 0)[:RT, :] * wv[3 - s:4 - s, :]
            trow = r0 + _iota((RT, 1), 0)
            dx_ref[pl.ds(r0, RT), :] = jnp.where(trow >= pad, dx, 0.0)
            new = []
            for s in (0, 1, 2, 3):
                new.append(dw[s] + jnp.sum(dpre[:RT, :] * xsh[s][:RT, :], axis=0, keepdims=True))
            return tuple(new)

        z = jnp.zeros((1, HD), F32)
        dw = lax.fori_loop(0, nt, tile, (z, z, z, z))
        for s in (0, 1, 2, 3):
            dw_ref[3 - s:4 - s, :] = dw[s]

    return pl.pallas_call(
        body, name=name, grid=(GDN_H,),
        in_specs=[pl.BlockSpec((t_pad, HD), lambda h: (0, blk0 + h)), pl.BlockSpec((CONV_K, HD), lambda h: (0, h)),
                  pl.BlockSpec((t_pad, HD), lambda h: (0, h))],
        out_specs=[pl.BlockSpec((t_pad, HD), lambda h: (0, h)), pl.BlockSpec((CONV_K, HD), lambda h: (0, h))],
        out_shape=[jax.ShapeDtypeStruct((t_pad, GDN_H * HD), F32), jax.ShapeDtypeStruct((CONV_K, GDN_H * HD), F32)],
        scratch_shapes=[pltpu.VMEM((t_pad + 3 * HALO, HD), F32), pltpu.VMEM((t_pad + HALO, HD), F32)],
        compiler_params=_cparams(),
    )(p, w, dn)


@jax.custom_vjp
def _unit_lower_inv(m, bd, eye):
    md = m * bd
    low = m - md
    p2 = mbnn(md, md)
    p4 = mbnn(p2, p2)
    dinv = mbnn(mbnn(eye - md, eye + p2), eye + p4)
    n = mbnn(dinv, low)
    n2 = mbnn(n, n)
    n4 = mbnn(n2, n2)
    return mbnn(mbnn(mbnn(eye - n, eye + n2), eye + n4), dinv)


def _unit_lower_inv_bwd(res, g):
    t, bd, eye = res
    return -mbtn(t, mbnt(g, t)), jnp.zeros_like(bd), jnp.zeros_like(eye)


def _unit_lower_inv_fwd(m, bd, eye):
    t = _unit_lower_inv(m, bd, eye)
    return t, (t, bd, eye)


_unit_lower_inv.defvjp(_unit_lower_inv_fwd, _unit_lower_inv_bwd)


def _gdn_chunks(chunks, alog, dtb, s):
    nh = chunks[0][0].shape[0]
    ri = _iota((1, CH, CH), 1)
    ci = _iota((1, CH, CH), 2)
    causal = ri >= ci
    strict = ri > ci
    eye = (ri == ci).astype(F32)
    bd = ((ri >> 3) == (ci >> 3)).astype(F32)
    ltri = (_iota((CH, CH), 0) >= _iota((CH, CH), 1)).astype(F32)
    sel = (_iota((nh, 1, HD), 2) == _iota((nh, 1, HD), 0)).astype(F32)
    last = _iota((1, CH, 1), 1) == CH - 1

    beta, gc, gc_rows = [], [], []
    for _, _, _, bb, aa, valid in chunks:
        beta_all = jnp.where(valid, _sigmoid(bb), 0.0)
        g_all = jnp.where(valid, -jnp.exp(alog) * _softplus(aa + dtb), 0.0)
        gc_all = hnn(ltri, g_all)
        beta.append(jnp.sum(beta_all[None] * sel, axis=2, keepdims=True))
        gc.append(jnp.sum(gc_all[None] * sel, axis=2, keepdims=True))
        gc_rows.append(hbnt(jnp.broadcast_to(sel, (nh, CH, HD)), jnp.broadcast_to(gc_all[None], (nh, CH, HD))))
    cat = lambda xs: jnp.concatenate(xs, axis=0)
    q, k, v = (cat([c[j] for c in chunks]) for j in range(3))
    beta, gc, gc_rows = cat(beta), cat(gc), cat(gc_rows)
    gc_last = jnp.sum(jnp.where(last, gc, 0.0), axis=1, keepdims=True)
    decay = jnp.exp(jnp.where(causal, gc - gc_rows, NEG))
    egc = jnp.exp(gc)

    kb = k * beta
    m = jnp.where(strict, bbnt(kb, k) * decay, 0.0)
    t_inv = _unit_lower_inv(m, bd, eye)
    u = bbnn(t_inv, v * beta)
    w = bbnn(t_inv, kb * egc)
    a_intra = bbnt(q, k) * decay
    q_dec = q * egc
    k_dec = k * jnp.exp(gc_last - gc)
    g_tot = jnp.exp(gc_last)

    outs = []
    for n in range(len(chunks)):
        part = lambda a: a[n * nh:(n + 1) * nh]
        v_new = part(u) - bbnn(part(w), s)
        outs.append(bbnn(part(q_dec), s) + bbnn(part(a_intra), v_new))
        s = s * part(g_tot) + bbtn(part(k_dec), v_new)
    return outs, s


PAIR = 2 * CH


def _gdn_specs(npair, rev):
    cc = (lambda c: npair - 1 - c) if rev else (lambda c: c)
    wide = pl.BlockSpec((PAIR, GDN_H * HD), lambda c: (cc(c), 0))
    fix = lambda off: pl.BlockSpec((PAIR, HD), lambda c: (cc(c), off))
    par = pl.BlockSpec((1, HD), lambda c: (0, 0))
    state = pl.BlockSpec((1, GDN_H, HD, HD), lambda c: (cc(c), 0, 0, 0))
    return wide, fix, par, state


def _store_heads(ref, a, rows=slice(None)):
    for h in range(a.shape[0]):
        ref[rows, h * HD:(h + 1) * HD] = a[h]


def _chunk_rows(half):
    return slice(half * CH, (half + 1) * CH)


def _chunk_valid(pair, half, pad):
    return ((2 * pair + half) * CH + _iota((CH, 1), 0)) >= pad


def _gdn_fwd(qn, kn, vn, p, alog, dtb, pad, name):
    t_pad = qn.shape[0]
    npair = t_pad // PAIR
    wide, fix, par, state = _gdn_specs(npair, False)

    def body(q_ref, k_ref, v_ref, bb_ref, aa_ref, al_ref, dt_ref, o_ref, ss_ref, s_ref):
        c = pl.program_id(0)

        @pl.when(c == 0)
        def _():
            s_ref[...] = jnp.zeros_like(s_ref)

        s = s_ref[...]
        ss_ref[0] = s
        rows = [_chunk_rows(half) for half in (0, 1)]
        chunks = [(_heads(q_ref[r, :], GDN_H), _heads(k_ref[r, :], GDN_H), _heads(v_ref[r, :], GDN_H),
                   bb_ref[r, :], aa_ref[r, :], _chunk_valid(c, half, pad)) for half, r in enumerate(rows)]
        outs, s = _gdn_chunks(chunks, al_ref[...], dt_ref[...], s)
        for r, o in zip(rows, outs):
            _store_heads(o_ref, o, r)
        s_ref[...] = s

    return pl.pallas_call(
        body, name=name, grid=(npair,),
        in_specs=[wide, wide, wide, fix(16), fix(17), par, par],
        out_specs=[wide, state],
        out_shape=[jax.ShapeDtypeStruct((t_pad, GDN_H * HD), F32), jax.ShapeDtypeStruct((npair, GDN_H, HD, HD), F32)],
        scratch_shapes=[pltpu.VMEM((GDN_H, HD, HD), F32)],
        compiler_params=_cparams(),
    )(qn, kn, vn, p, p, alog, dtb)


def _gdn_bwd(qn, kn, vn, p, alog, dtb, ssave, do, pad, name, cargo=(), exchange=None):
    t_pad = qn.shape[0]
    npair = t_pad // PAIR
    wide, fix, par, state = _gdn_specs(npair, True)
    n = len(cargo)

    def body(q_ref, k_ref, v_ref, bb_ref, aa_ref, al_ref, dt_ref, ss_ref, do_ref, *rest):
        c = pl.program_id(0)
        ds_ref = rest[-1]
        (dq_ref, dk_ref, dv_ref, dbb_ref, daa_ref, dal_ref, ddt_ref), end_cargo = _cargo_bounds(
            rest[:-1], n, 7, exchange, c == 0, c == npair - 1)

        @pl.when(c == 0)
        def _():
            ds_ref[...] = jnp.zeros_like(ds_ref)
            dal_ref[...] = jnp.zeros_like(dal_ref)
            ddt_ref[...] = jnp.zeros_like(ddt_ref)

        ra, rb = _chunk_rows(0), _chunk_rows(1)
        va, vb = _chunk_valid(npair - 1 - c, 0, pad), _chunk_valid(npair - 1 - c, 1, pad)

        def pair(qa, ka, va_, ba, aa, qb, kb, vb_, bb, ab, al, dt, s):
            (oa, ob), s = _gdn_chunks([(qa, ka, va_, ba, aa, va), (qb, kb, vb_, bb, ab, vb)], al, dt, s)
            return oa, ob, s

        ins = [f(ref[r, :]) for r in (ra, rb)
               for ref, f in ((q_ref, lambda a: _heads(a, GDN_H)), (k_ref, lambda a: _heads(a, GDN_H)),
                              (v_ref, lambda a: _heads(a, GDN_H)), (bb_ref, lambda a: a), (aa_ref, lambda a: a))]
        _, vjp = jax.vjp(pair, *ins, al_ref[...], dt_ref[...], ss_ref[0])
        g = vjp((_heads(do_ref[ra, :], GDN_H), _heads(do_ref[rb, :], GDN_H), ds_ref[...]))
        for r, (dq, dk, dv, dbb, daa) in ((ra, g[0:5]), (rb, g[5:10])):
            _store_heads(dq_ref, dq, r)
            _store_heads(dk_ref, dk, r)
            _store_heads(dv_ref, dv, r)
            dbb_ref[r, :] = dbb
            daa_ref[r, :] = daa
        dal_ref[...] += g[10]
        ddt_ref[...] += g[11]
        ds_ref[...] = g[12]
        end_cargo()

    sds = jax.ShapeDtypeStruct
    return pl.pallas_call(
        body, name=name, grid=(npair,),
        in_specs=[wide, wide, wide, fix(16), fix(17), par, par, state, wide] + [ANY] * n,
        out_specs=[wide, wide, wide, fix(0), fix(0), par, par] + [ANY] * n,
        out_shape=[sds((t_pad, GDN_H * HD), F32)] * 3 + [sds((t_pad, HD), F32)] * 2 + [sds((1, HD), F32)] * 2
        + (exchange[1](cargo) if n else []),
        scratch_shapes=(exchange[2](n) if n else []) + [pltpu.VMEM((GDN_H, HD, HD), F32)],
        compiler_params=_cparams(),
    )(qn, kn, vn, p, p, alog, dtb, ssave, do, *cargo)


SB_Q0, SB_K0, SB_V0 = 18, 22, 26
SB_SCALE = SB_DH ** -0.5
SB_NB = 4


def _sb_terms(z, allowed):
    e = jnp.exp(-jnp.abs(z))
    den = 1.0 + e
    raw = -jnp.maximum(z, 0.0) - jnp.log(den)
    l1m = raw if allowed is None else jnp.where(allowed, raw, 0.0)
    return l1m, z + raw, jnp.where(z >= 0.0, 1.0, e) / den


def _sb_passes(i, step, carry):
    total = i + 1
    sized = lambda done: [functools.partial(step, done, masked=True, nb=nb) for nb in range(1, SB_NB + 1)]

    def several(c):
        n_mid = (total - SB_NB - 1) // SB_NB
        c = step(0, c, masked=True, nb=SB_NB)
        c = lax.fori_loop(0, n_mid, lambda t, cc: step(SB_NB * (1 + t), cc, masked=False, nb=SB_NB), c)
        done = SB_NB * (1 + n_mid)
        return lax.switch(total - done - 1, sized(done), c)

    return lax.cond(total <= SB_NB, lambda c: lax.switch(total - 1, sized(0), c), several, carry)


def _sb_stack(a, i):
    first = _iota((1, HD), 1) < SB_DH
    a2 = jnp.concatenate([jnp.where(first, a, 0.0), jnp.where(first, 0.0, a)], axis=0).astype(BF16)
    rq = i * QB + _iota((QB, 1), 0)
    return a2, jnp.concatenate([rq, rq], axis=0), first


def _hi_lo(a):
    hi = a.astype(BF16)
    lo = (a - hi.astype(F32)).astype(BF16)
    return jnp.concatenate([hi, lo], axis=1)


def _cargo_bounds(refs, n, n_out, exchange, first, last):
    outs = refs[n:n + n_out]
    if not n:
        return outs, lambda: None
    ex = exchange[0](refs[:n], refs[n + n_out:2 * n + n_out], *refs[2 * n + n_out:])

    @pl.when(first)
    def _():
        ex.start()

    def finish():
        @pl.when(last)
        def _():
            ex.wait()

    return outs, finish


def _sb_fwd(p, pad, name, cargo=(), exchange=None):
    t_pad = p.shape[0]
    nq = t_pad // QB
    n = len(cargo)

    def body(q_ref, k_ref, v_ref, *rest):
        i = pl.program_id(1)
        pr = pl.program_id(0)
        (o_ref, r_ref), end_cargo = _cargo_bounds(rest, n, 2, exchange, (pr == 0) & (i == 0),
                                                  (pr == SB_H // 2 - 1) & (i == nq - 1))
        q2, rowq, first = _sb_stack(q_ref[...] * SB_SCALE, i)
        tri = (_iota((QB, QB), 0) > _iota((QB, QB), 1)).astype(BF16)
        upper2 = jnp.concatenate([tri, tri], axis=0)

        def chain(kb, masked):
            start = pl.multiple_of(kb * QB, QB)
            kblk = k_ref[pl.ds(start, QB), :].astype(BF16)
            vblk = v_ref[pl.ds(start, QB), :].astype(BF16)
            z = lax.dot_general(q2, kblk, (NT, ((), ())), preferred_element_type=F32)
            colk = kb * QB + _iota((1, QB), 1)
            al = ((colk < rowq) & (colk >= pad)) if masked else None
            l1m, ls, _ = _sb_terms(z, al)
            suf = lax.dot_general(_hi_lo(l1m), upper2, (NN, ((), ())), preferred_element_type=F32)
            return al, ls, suf, jnp.sum(l1m, axis=1, keepdims=True), vblk

        def step(done, carry, masked, nb):
            o_acc, run = carry
            ws, vs = [], []
            for n in range(nb):
                al, ls, suf, rs, vblk = chain(i - done - n, masked)
                wgt = jnp.exp(ls + suf + run)
                ws.append((wgt if al is None else jnp.where(al, wgt, 0.0)).astype(BF16))
                vs.append(vblk)
                run = run + rs
            o_acc = o_acc + lax.dot_general(jnp.concatenate(ws, axis=1), jnp.concatenate(vs, axis=0),
                                            (NN, ((), ())), preferred_element_type=F32)
            return o_acc, run

        o_acc, run = _sb_passes(i, step, (jnp.zeros((2 * QB, HD), F32), jnp.zeros((2 * QB, 1), F32)))
        o_ref[...] = jnp.where(first, o_acc[:QB], o_acc[QB:]).astype(BF16)
        r_ref[...] = jnp.where(first, run[:QB], run[QB:])
        end_cargo()

    full = lambda off: pl.BlockSpec((t_pad, HD), lambda pr, i: (0, off + pr))
    blk = pl.BlockSpec((QB, HD), lambda pr, i: (i, pr))
    return pl.pallas_call(
        body, name=name, grid=(SB_H // 2, nq),
        in_specs=[pl.BlockSpec((QB, HD), lambda pr, i: (i, SB_Q0 + pr)), full(SB_K0), full(SB_V0)] + [ANY] * n,
        out_specs=[blk, blk] + [ANY] * n,
        out_shape=[jax.ShapeDtypeStruct((t_pad, SB_H * SB_DH), BF16), jax.ShapeDtypeStruct((t_pad, SB_H * SB_DH), F32)]
        + (exchange[1](cargo) if n else []),
        scratch_shapes=exchange[2](n) if n else [],
        compiler_params=_cparams(),
    )(p, p, p, *cargo)


def _sb_bwd(p, rtot, dy, dy_blk0, pad, name, cargo=(), exchange=None):
    t_pad = p.shape[0]
    nq = t_pad // QB
    n = len(cargo)

    def body(q_ref, k_ref, v_ref, r_ref, do_ref, *rest):
        i = pl.program_id(1)
        pr = pl.program_id(0)
        (dq_ref, dk_ref, dv_ref), end_cargo = _cargo_bounds(rest, n, 3, exchange, (pr == 0) & (i == 0),
                                                            (pr == SB_H // 2 - 1) & (i == nq - 1))

        @pl.when(i == 0)
        def _():
            dk_ref[...] = jnp.zeros_like(dk_ref)
            dv_ref[...] = jnp.zeros_like(dv_ref)

        q2, rowq, first = _sb_stack(q_ref[...] * SB_SCALE, i)
        do2, _, _ = _sb_stack(do_ref[...], i)
        rt = r_ref[...]
        lane = _iota((1, HD), 1)
        rcol = jnp.concatenate([jnp.sum(jnp.where(lane == 0, rt, 0.0), axis=1, keepdims=True),
                                jnp.sum(jnp.where(lane == SB_DH, rt, 0.0), axis=1, keepdims=True)], axis=0)
        rj = _iota((QB, QB), 0)
        cs = _iota((QB, QB), 1)
        tri_u = (rj > cs).astype(BF16)
        tri_l = (rj < cs).astype(BF16)
        upper2 = jnp.concatenate([tri_u, tri_u], axis=0)
        lower2 = jnp.concatenate([tri_l, tri_l], axis=0)

        def chain(kb, masked):
            start = pl.multiple_of(kb * QB, QB)
            kblk = k_ref[pl.ds(start, QB), :].astype(BF16)
            vblk = v_ref[pl.ds(start, QB), :].astype(BF16)
            z = lax.dot_general(q2, kblk, (NT, ((), ())), preferred_element_type=F32)
            colk = kb * QB + _iota((1, QB), 1)
            al = ((colk < rowq) & (colk >= pad)) if masked else None
            l1m, ls, sg = _sb_terms(z, al)
            dwgt = lax.dot_general(do2, vblk, (NT, ((), ())), preferred_element_type=F32)
            suf = lax.dot_general(_hi_lo(l1m), upper2, (NN, ((), ())), preferred_element_type=F32)
            return start, kblk, al, ls, suf, jnp.sum(l1m, axis=1, keepdims=True), dwgt, sg

        def finish(c, seen, gseen):
            start, kblk, al, ls, suf, rs, dwgt, sg = c
            wgt = jnp.exp(ls + suf + (rcol - seen - rs))
            if al is not None:
                wgt = jnp.where(al, wgt, 0.0)
            dl = dwgt * wgt
            gpre = gseen + lax.dot_general(_hi_lo(dl), lower2, (NN, ((), ())), preferred_element_type=F32)
            dz = dl - sg * (dl + gpre)
            if al is not None:
                dz = jnp.where(al, dz, 0.0)
            dz = dz.astype(BF16)
            dk_ref[pl.ds(start, QB), :] += lax.dot_general(dz, q2, (TN, ((), ())), preferred_element_type=F32)
            dv_ref[pl.ds(start, QB), :] += lax.dot_general(wgt.astype(BF16), do2, (TN, ((), ())),
                                                           preferred_element_type=F32)
            return dz, seen + rs, gseen + jnp.sum(dl, axis=1, keepdims=True)

        def step(done, carry, masked, nb):
            dq_acc, seen, gseen = carry
            cs_ = [chain(done + n, masked) for n in range(nb)]
            dzs = []
            for c in cs_:
                dz, seen, gseen = finish(c, seen, gseen)
                dzs.append(dz)
            dq_acc = dq_acc + lax.dot_general(jnp.concatenate(dzs, axis=1), jnp.concatenate([c[1] for c in cs_], axis=0),
                                              (NN, ((), ())), preferred_element_type=F32)
            return dq_acc, seen, gseen

        zc = jnp.zeros((2 * QB, 1), F32)
        dq_acc, _, _ = _sb_passes(i, step, (jnp.zeros((2 * QB, HD), F32), zc, zc))
        dq_ref[...] = jnp.where(first, dq_acc[:QB], dq_acc[QB:]) * SB_SCALE
        end_cargo()

    full_in = lambda off: pl.BlockSpec((t_pad, HD), lambda pr, i: (0, off + pr))
    full_out = pl.BlockSpec((t_pad, HD), lambda pr, i: (0, pr))
    blk = pl.BlockSpec((QB, HD), lambda pr, i: (i, pr))
    sds = jax.ShapeDtypeStruct((t_pad, SB_H * SB_DH), F32)
    return pl.pallas_call(
        body, name=name, grid=(SB_H // 2, nq),
        in_specs=[pl.BlockSpec((QB, HD), lambda pr, i: (i, SB_Q0 + pr)), full_in(SB_K0), full_in(SB_V0), blk,
                  pl.BlockSpec((QB, HD), lambda pr, i: (i, dy_blk0 + pr))] + [ANY] * n,
        out_specs=[blk, full_out, full_out] + [ANY] * n,
        out_shape=[sds, sds, sds] + (exchange[1](cargo) if n else []),
        scratch_shapes=exchange[2](n) if n else [],
        compiler_params=_cparams(),
    )(p, p, p, rtot, dy, *cargo)


HG_LEVELS = 6


def _hg_prefix_matrix():
    t = np.arange(CH)[:, None]
    j = np.arange(CH)[None, :]
    groups = [(j <= t)]
    for lvl in range(1, HG_LEVELS + 1):
        half = CH >> lvl
        e = (t // (2 * half)) * (2 * half) + half - 1
        groups.append(j <= e)
    groups.append(np.ones((8, CH), bool))
    e = np.concatenate(groups, axis=0).astype(np.float32)
    return np.concatenate([e, e, e], axis=1), np.concatenate([e, e, e], axis=0)


HG_G = 4


def _hg_chunk(qr, fr, iv, r0, r1, st, valid, ecat):
    g = st.shape[0]
    mx = jnp.maximum(r0, r1)
    e0 = jnp.exp(r0 - mx)
    e1 = jnp.exp(r1 - mx)
    lb = e1 / (e0 + e1)
    fg = lb + (1.0 - lb) * _sigmoid(fr)
    logf = jnp.where(valid, jnp.log(fg), 0.0)
    kk = jnp.where(valid, 1.0 - fg, 0.0)
    q = jnp.where(valid, _silu(qr), 0.0)
    v = _heads(jnp.where(valid, iv, 0.0), g)

    pre = _mask_dot(ecat, logf)
    b = pre[0:CH]
    b_last = jnp.max(pre[(HG_LEVELS + 1) * CH:], axis=0, keepdims=True)
    row = _iota((CH, 1), 0)
    ri = _iota((1, CH, CH), 1)
    ci = _iota((1, CH, CH), 2)
    a = jnp.where(ri == ci, jnp.sum(_heads(q * kk, g), axis=2, keepdims=True), 0.0)
    for lvl in range(1, HG_LEVELS + 1):
        half = CH >> lvl
        m = pre[lvl * CH:(lvl + 1) * CH]
        low = (row & half) != 0
        dec = jnp.exp(jnp.where(low, b - m, m - b))
        qt = jnp.where(low, q * dec, 0.0)
        kt = jnp.where(low, 0.0, kk * dec)
        same = (ri >> (7 - lvl)) == (ci >> (7 - lvl))
        a = a + jnp.where(same, bbnt(_heads(qt, g), _heads(kt, g)), 0.0)
    o = bbnt(_heads(q * jnp.exp(b), g), st) + bbnn(a, v)
    kd = kk * jnp.exp(b_last - b)
    st_new = st * _heads(jnp.exp(b_last), g) + bbtn(v, _heads(kd, g))
    return o, st_new


def _hg_specs(npair, rev):
    cc = (lambda c: npair - 1 - c) if rev else (lambda c: c)
    ng = HG_H // HG_G
    blk = lambda off: pl.BlockSpec((PAIR, HG_G * HD), lambda h, c: (cc(c), off * ng + h))
    lbs = pl.BlockSpec((2, HG_G * HD), lambda h, c: (0, h))
    state = pl.BlockSpec((1, HG_G, HD, HD), lambda h, c: (cc(c), h, 0, 0))
    return ng, blk, lbs, state


def _hg_fwd(p, lbraw, ecat, pad, name):
    t_pad = p.shape[0]
    npair = t_pad // PAIR
    ng, blk, lbs, state = _hg_specs(npair, False)

    def body(q_ref, f_ref, i_ref, lb_ref, e_ref, et_ref, o_ref, ss_ref, s_ref):
        c = pl.program_id(1)

        @pl.when(c == 0)
        def _():
            s_ref[...] = jnp.zeros_like(s_ref)

        st = s_ref[...]
        ss_ref[0] = st
        for half in (0, 1):
            r = _chunk_rows(half)
            o, st = _hg_chunk(q_ref[r, :], f_ref[r, :], i_ref[r, :], lb_ref[0:1, :], lb_ref[1:2, :], st,
                              _chunk_valid(c, half, pad), (e_ref[...], et_ref[...]))
            _store_heads(o_ref, o, r)
        s_ref[...] = st

    return pl.pallas_call(
        body, name=name, grid=(ng, npair),
        in_specs=[blk(0), blk(1), blk(2), lbs] + [pl.BlockSpec(e.shape, lambda h, c: (0, 0)) for e in ecat],
        out_specs=[blk(0), state],
        out_shape=[jax.ShapeDtypeStruct((t_pad, HG_H * HD), F32), jax.ShapeDtypeStruct((npair, HG_H, HD, HD), F32)],
        scratch_shapes=[pltpu.VMEM((HG_G, HD, HD), F32)],
        compiler_params=_cparams(),
    )(p, p, p, lbraw, *ecat)


def _hg_bwd(p, lbraw, ecat, ssave, do, pad, name, cargo=(), exchange=None):
    t_pad = p.shape[0]
    npair = t_pad // PAIR
    ng, blk, lbs, state = _hg_specs(npair, True)
    n = len(cargo)

    def body(q_ref, f_ref, i_ref, lb_ref, e_ref, et_ref, ss_ref, do_ref, *rest):
        c = pl.program_id(1)
        hg = pl.program_id(0)
        ds_ref = rest[-1]
        (dq_ref, df_ref, di_ref, dlb_ref), end_cargo = _cargo_bounds(
            rest[:-1], n, 4, exchange, (hg == 0) & (c == 0), (hg == ng - 1) & (c == npair - 1))

        @pl.when(c == 0)
        def _():
            ds_ref[...] = jnp.zeros_like(ds_ref)
            dlb_ref[...] = jnp.zeros_like(dlb_ref)

        ra, rb = _chunk_rows(0), _chunk_rows(1)
        va, vb = _chunk_valid(npair - 1 - c, 0, pad), _chunk_valid(npair - 1 - c, 1, pad)
        ecv = (e_ref[...], et_ref[...])

        def pair(qa, fa, ia, qb, fb, ib, r0, r1, st):
            oa, st = _hg_chunk(qa, fa, ia, r0, r1, st, va, ecv)
            ob, st = _hg_chunk(qb, fb, ib, r0, r1, st, vb, ecv)
            return oa, ob, st

        ins = [ref[r, :] for r in (ra, rb) for ref in (q_ref, f_ref, i_ref)]
        _, vjp = jax.vjp(pair, *ins, lb_ref[0:1, :], lb_ref[1:2, :], ss_ref[0])
        g = vjp((_heads(do_ref[ra, :], HG_G), _heads(do_ref[rb, :], HG_G), ds_ref[...]))
        for r, (dq, df, di) in ((ra, g[0:3]), (rb, g[3:6])):
            dq_ref[r, :] = dq.astype(BF16)
            df_ref[r, :] = df.astype(BF16)
            di_ref[r, :] = di.astype(BF16)
        dlb_ref[0:1, :] += g[6]
        dlb_ref[1:2, :] += g[7]
        ds_ref[...] = g[8]
        end_cargo()

    sds = jax.ShapeDtypeStruct((t_pad, HG_H * HD), BF16)
    return pl.pallas_call(
        body, name=name, grid=(ng, npair),
        in_specs=[blk(0), blk(1), blk(2), lbs] + [pl.BlockSpec(e.shape, lambda h, c: (0, 0)) for e in ecat]
        + [state, blk(0)] + [ANY] * n,
        out_specs=[blk(0), blk(0), blk(0), lbs] + [ANY] * n,
        out_shape=[sds, sds, sds, jax.ShapeDtypeStruct((2, HG_H * HD), F32)] + (exchange[1](cargo) if n else []),
        scratch_shapes=(exchange[2](n) if n else []) + [pltpu.VMEM((HG_G, HD, HD), F32)],
        compiler_params=_cparams(),
    )(p, p, p, lbraw, *ecat, ssave, do, *cargo)


def _pad_ab_cols(w):
    z = jnp.zeros((w.shape[0], HD - GDN_H), w.dtype)
    return jnp.concatenate([w[:, :2048], w[:, 2048:2052], z, w[:, 2052:2056], z, w[:, 2056:]], axis=1)


def _unpad_ab_cols(w):
    return jnp.concatenate([w[:, :2048], w[:, 2048:2052], w[:, 2176:2180], w[:, 2304:]], axis=1)


def _lane_pad(v):
    return jnp.pad(v, ((0, 0), (0, HD - v.shape[1])))


def _mlp_fwd(hb, w1, w2, layer):
    a, r = _mm(hb, w1, b_view=("cols", layer), out_dtype=BF16, act=True, name=f"mlp_up_{layer}")
    m = _mm(r, w2, b_view=("rows", layer), name=f"mlp_down_{layer}")
    return a, r, m


def _mlp_bwd(hb, a, r, dmb, w1, w2, layer):
    da = _mm(dmb, w2, tb=True, b_view=("rows", layer), out_dtype=BF16, gate=a, name=f"mlp_down_dx_{layer}")
    dw2 = _mm(r, dmb, ta=True, out_dtype=BF16, name=f"mlp_down_dw_{layer}")
    dh = _mm(da, w1, tb=True, b_view=("cols", layer), name=f"mlp_up_dx_{layer}")
    dw1 = _mm(hb, da, ta=True, out_dtype=BF16, out_split=N_CHIP, name=f"mlp_up_dw_{layer}")
    return dh, dw1, dw2


def _local_step(h0, tgt, w, pad, late=None):
    row = lambda a, i: a[i:i + 1]
    ecat = tuple(jnp.asarray(e, dtype=BF16) for e in _hg_prefix_matrix())
    cw = [w["conv_w"][:, i * 512:(i + 1) * 512] for i in range(3)]
    alog, dtb = _lane_pad(w["a_log"]), _lane_pad(w["dt_bias"])

    h0b = h0.astype(BF16)
    p0 = _mm(h0b, w["ab_w_in"], name="ab_in")
    qn = _conv_fwd(p0, 0, cw[0], "q", pad, "conv_q")
    kn = _conv_fwd(p0, 4, cw[1], "k", pad, "conv_k")
    vn = _conv_fwd(p0, 8, cw[2], "v", pad, "conv_v")
    oa_raw, ss0 = _gdn_fwd(qn, kn, vn, p0, alog, dtb, pad, "gdn_fwd")
    oa = _grms_fwd(oa_raw, p0, 12, w["ab_gnorm_g"], "gdn_gate")
    if late is None:
        ob, rtot = _sb_fwd(p0, pad, "sb_fwd")
    else:
        ob, rtot, g_about, g_cin, g_cout, g_w1, g_w2 = _sb_fwd(p0, pad, "sb_fwd", cargo=late, exchange=GATHER)
        w = dict(w, ab_w_out=g_about.reshape(D, D), c_w_in=g_cin, c_w_out=g_cout.reshape(D, D), mlp_w1=g_w1, mlp_w2=g_w2)
    ycat = jnp.concatenate([oa, ob], axis=1)
    mix0 = _mm(ycat, w["ab_w_out"], name="ab_out")
    h1, h1b = _ln_res_fwd(h0, mix0, row(w["ln_mix_g"], 0), row(w["ln_mix_b"], 0), "ln_mix_0")
    a0, r0, m0 = _mlp_fwd(h1b, w["mlp_w1"], w["mlp_w2"], 0)
    h2, h2b = _ln_res_fwd(h1, m0, row(w["ln_ffn_g"], 0), row(w["ln_ffn_b"], 0), "ln_ffn_0")
    p1 = _mm(h2b, w["c_w_in"], b_view=("cols", 0), name="c_in")
    oc_raw, ss1 = _hg_fwd(p1, w["c_lb_raw"], ecat, pad, "hg_fwd")
    yc = _grms_fwd(oc_raw, p1, 3 * HG_H, w["c_gnorm_g"], "hg_gate")
    mix1 = _mm(yc, w["c_w_out"], name="c_out")
    h3, h3b = _ln_res_fwd(h2, mix1, row(w["ln_mix_g"], 1), row(w["ln_mix_b"], 1), "ln_mix_1")
    a1, r1, m1 = _mlp_fwd(h3b, w["mlp_w1"], w["mlp_w2"], 1)
    h4, _ = _ln_res_fwd(h3, m1, row(w["ln_ffn_g"], 1), row(w["ln_ffn_b"], 1), "ln_ffn_1")
    loss, dh4 = _loss_fwd(h4, tgt, pad + N_META, "loss")

    zero = jnp.zeros_like(dh4)
    dh3a, dm1b, dfg1, dfb1 = _ln_res_bwd(h3, m1, row(w["ln_ffn_g"], 1), row(w["ln_ffn_b"], 1), dh4, zero, "ln_ffn_bwd_1")
    dh3b, dw1_1, dw2_1 = _mlp_bwd(h3b, a1, r1, dm1b, w["mlp_w1"], w["mlp_w2"], 1)
    dh2a, dmix1b, dmg1, dmb1 = _ln_res_bwd(h2, mix1, row(w["ln_mix_g"], 1), row(w["ln_mix_b"], 1), dh3a, dh3b, "ln_mix_bwd_1")
    dyc = _mm(dmix1b, w["c_w_out"], tb=True, name="c_out_dx")
    dwco = _mm(yc, dmix1b, ta=True, out_dtype=BF16, name="c_out_dw")
    doc, dzc, dcg = _grms_bwd(oc_raw, p1, 3 * HG_H, w["c_gnorm_g"], dyc, 0, "hg_gate_bwd")
    landed = {}
    rows4 = lambda a: a.reshape(N_CHIP, -1, D)
    if late is None:
        dq1, df1, di1, dlb = _hg_bwd(p1, w["c_lb_raw"], ecat, ss1, doc, pad, "hg_bwd")
    else:
        dq1, df1, di1, dlb, landed["w1_1"] = _hg_bwd(
            p1, w["c_lb_raw"], ecat, ss1, doc, pad, "hg_bwd", cargo=[dw1_1], exchange=SCATTER)
    dp1 = [dq1, df1, di1, dzc]
    dh2b = _mm_groups_nt(dp1, w["c_w_in"], "c_in_dx")
    dwc = jnp.stack([_mm(h2b, d, ta=True, out_dtype=BF16, name=f"c_in_dw_{i}") for i, d in enumerate(dp1)])
    dh1a, dm0b, dfg0, dfb0 = _ln_res_bwd(h1, m0, row(w["ln_ffn_g"], 0), row(w["ln_ffn_b"], 0), dh2a, dh2b, "ln_ffn_bwd_0")
    dh1b, dw1_0, dw2_0 = _mlp_bwd(h1b, a0, r0, dm0b, w["mlp_w1"], w["mlp_w2"], 0)
    dh0a, dmix0b, dmg0, dmb0 = _ln_res_bwd(h0, mix0, row(w["ln_mix_g"], 0), row(w["ln_mix_b"], 0), dh1a, dh1b, "ln_mix_bwd_0")
    dycat = _mm(dmix0b, w["ab_w_out"], tb=True, name="ab_out_dx")
    dwabo = _mm(ycat, dmix0b, ta=True, out_dtype=BF16, name="ab_out_dw")
    doa, dza, dag = _grms_bwd(oa_raw, p0, 12, w["ab_gnorm_g"], dycat, 0, "gdn_gate_bwd")
    if late is None:
        dqn, dkn, dvn, dbb, daa, dal, ddt = _gdn_bwd(qn, kn, vn, p0, alog, dtb, ss0, doa, pad, "gdn_bwd")
        dqb, dkb, dvb = _sb_bwd(p0, rtot, dycat, 4, pad, "sb_bwd")
    else:
        dqn, dkn, dvn, dbb, daa, dal, ddt, landed["c_w_in"] = _gdn_bwd(
            qn, kn, vn, p0, alog, dtb, ss0, doa, pad, "gdn_bwd", cargo=[dwc], exchange=SCATTER)
        (dqb, dkb, dvb, landed["w1_0"], landed["w2_0"], landed["w2_1"], landed["ab_w_out"],
         landed["c_w_out"]) = _sb_bwd(
            p0, rtot, dycat, 4, pad, "sb_bwd",
            cargo=[dw1_0, rows4(dw2_0), rows4(dw2_1), rows4(dwabo), rows4(dwco)], exchange=SCATTER)
    dpq, dcq = _conv_bwd(p0, 0, cw[0], dqn, "q", pad, "conv_q_bwd")
    dpk, dck = _conv_bwd(p0, 4, cw[1], dkn, "k", pad, "conv_k_bwd")
    dpv, dcv = _conv_bwd(p0, 8, cw[2], dvn, "v", pad, "conv_v_bwd")
    dp0 = _assemble_bf16([(dpq, "cols"), (dpk, "cols"), (dpv, "cols"), (dza, "cols"), (dbb, "cols"), (daa, "cols"),
                          (dqb, "cols"), (dkb, "cols"), (dvb, "cols")], "ab_in_dy")
    dwab = _mm(h0b, dp0, ta=True, out_dtype=BF16, name="ab_in_dw")
    if late is None:
        dh0b = _mm(dp0, w["ab_w_in"], tb=True, name="ab_in_dx")
    else:
        dab = jnp.transpose(_unpad_ab_cols(dwab).reshape(D, N_CHIP, AB_TRUE // N_CHIP), (1, 0, 2))
        dh0b, landed["ab_w_in"] = _mm(dp0, w["ab_w_in"], tb=True, name="ab_in_dx", cargo=[dab], exchange=SCATTER)
    dh0 = _add2(dh0a, dh0b, "dh0")

    grads = {
        "ab_w_in": dwab, "conv_w": jnp.concatenate([dcq, dck, dcv], axis=1),
        "a_log": dal[:, :GDN_H], "dt_bias": ddt[:, :GDN_H],
        "ab_gnorm_g": dag, "ab_w_out": dwabo, "c_w_in": dwc, "c_lb_raw": dlb, "c_gnorm_g": dcg, "c_w_out": dwco,
        "ln_mix_g": jnp.concatenate([dmg0, dmg1], 0), "ln_mix_b": jnp.concatenate([dmb0, dmb1], 0),
        "w1_0": dw1_0, "w1_1": dw1_1, "w2_0": dw2_0, "w2_1": dw2_1,
        "ln_ffn_g": jnp.concatenate([dfg0, dfg1], 0), "ln_ffn_b": jnp.concatenate([dfb0, dfb1], 0),
        "landed": landed,
    }
    return loss, dh0, grads


MESH = pl.DeviceIdType.MESH
ANY = pl.BlockSpec(memory_space=pl.ANY)
N_CHIP = 4
N_DEV = 8
CHIP_REL = ((1, 0), (0, 1), (1, 1))
DEV_REL = tuple((dx, dy, dc) for dx in (0, 1) for dy in (0, 1) for dc in (0, 1))[1:]

def _pos():
    return lax.axis_index("x"), lax.axis_index("y"), lax.axis_index("c")


def _flip(a, d):
    return a + d - 2 * a * d


class _Exchange:
    def __init__(self, local, sends, recvs):
        self.local, self.sends, self.recvs = local, sends, recvs

    def start(self):
        for cp in self.local + self.sends:
            cp.start()

    def wait(self):
        for cp in self.recvs:
            cp.wait_recv()
        for cp in self.sends:
            cp.wait_send()
        for cp in self.local:
            cp.wait()


def _gather_sems(n):
    return [pltpu.SemaphoreType.DMA((3 * n,)), pltpu.SemaphoreType.DMA((3 * n,)), pltpu.SemaphoreType.DMA((n,))]


def _gather_copies(x_refs, o_refs, send_sems, recv_sems, local_sems):
    n = len(x_refs)
    x, y, c = _pos()
    local = [pltpu.make_async_copy(x_refs[a], o_refs[a].at[2 * x + y], local_sems.at[a]) for a in range(n)]

    def copy(a, k, sending):
        tx, ty = _flip(x, CHIP_REL[k][0]), _flip(y, CHIP_REL[k][1])
        return pltpu.make_async_remote_copy(
            src_ref=x_refs[a], dst_ref=o_refs[a].at[2 * x + y if sending else 2 * tx + ty],
            send_sem=send_sems.at[3 * a + k], recv_sem=recv_sems.at[3 * a + k], device_id=(tx, ty, c), device_id_type=MESH)

    pairs = [(a, k) for a in range(n) for k in range(3)]
    return _Exchange(local, [copy(a, k, True) for a, k in pairs], [copy(a, k, False) for a, k in pairs])


def _gather_shapes(bufs):
    return [jax.ShapeDtypeStruct((N_CHIP,) + b.shape, b.dtype) for b in bufs]


def _chip_allgather(bufs, name):
    n = len(bufs)

    def body(*refs):
        ex = _gather_copies(refs[:n], refs[n:2 * n], *refs[2 * n:])
        ex.start()
        ex.wait()

    return pl.pallas_call(
        body, name=name, in_specs=[ANY] * n, out_specs=[ANY] * n, out_shape=_gather_shapes(bufs),
        scratch_shapes=_gather_sems(n), compiler_params=pltpu.CompilerParams(has_side_effects=True),
    )(*bufs)


def _scatter_sems(n):
    nr = N_DEV - 1
    return [pltpu.SemaphoreType.DMA((nr * n,)), pltpu.SemaphoreType.DMA((nr * n,)), pltpu.SemaphoreType.DMA((n,))]


def _scatter_copies(g_refs, o_refs, send_sems, recv_sems, local_sems):
    n = len(g_refs)
    nr = N_DEV - 1
    x, y, c = _pos()
    me = 4 * x + 2 * y + c
    local = [pltpu.make_async_copy(g_refs[a].at[2 * x + y], o_refs[a].at[me], local_sems.at[a]) for a in range(n)]

    def copy(a, k, sending):
        dx, dy, dc = DEV_REL[k]
        tx, ty, tc = _flip(x, dx), _flip(y, dy), _flip(c, dc)
        return pltpu.make_async_remote_copy(
            src_ref=g_refs[a].at[2 * tx + ty], dst_ref=o_refs[a].at[me if sending else 4 * tx + 2 * ty + tc],
            send_sem=send_sems.at[nr * a + k], recv_sem=recv_sems.at[nr * a + k],
            device_id=(tx, ty, tc), device_id_type=MESH)

    pairs = [(a, k) for a in range(n) for k in range(nr)]
    return _Exchange(local, [copy(a, k, True) for a, k in pairs], [copy(a, k, False) for a, k in pairs])


def _scatter_shapes(gs):
    return [jax.ShapeDtypeStruct((N_DEV,) + g.shape[1:], g.dtype) for g in gs]


GATHER = (_gather_copies, _gather_shapes, _gather_sems)
SCATTER = (_scatter_copies, _scatter_shapes, _scatter_sems)


def _sum_slots(r, name):
    n, rh, w = r.shape
    tr = _pick(rh, (256, 128, 64, 16))

    def body(r_ref, o_ref):
        acc = r_ref[0].astype(F32)
        for s in range(1, n):
            acc = acc + r_ref[s].astype(F32)
        o_ref[...] = acc

    return pl.pallas_call(
        body, name=name, grid=(rh // tr,), in_specs=[pl.BlockSpec((n, tr, w), lambda i: (0, i, 0))],
        out_specs=pl.BlockSpec((tr, w), lambda i: (i, 0)), out_shape=jax.ShapeDtypeStruct((rh, w), F32),
        compiler_params=_cparams(),
    )(r)


def _small_allreduce(buf, name):
    r, w = buf.shape

    def body(b_ref, o_ref, land_ref, send_sems, recv_sems):
        x, y, c = _pos()
        me = 4 * x + 2 * y + c
        land_ref[me] = b_ref[...]

        def target(k):
            dx, dy, dc = DEV_REL[k]
            return _flip(x, dx), _flip(y, dy), _flip(c, dc)

        sends = []
        for k in range(N_DEV - 1):
            tx, ty, tc = target(k)
            cp = pltpu.make_async_remote_copy(
                src_ref=b_ref, dst_ref=land_ref.at[me], send_sem=send_sems.at[k], recv_sem=recv_sems.at[k],
                device_id=(tx, ty, tc), device_id_type=MESH)
            cp.start()
            sends.append(cp)
        for k in range(N_DEV - 1):
            tx, ty, tc = target(k)
            pltpu.make_async_remote_copy(
                src_ref=b_ref, dst_ref=land_ref.at[4 * tx + 2 * ty + tc], send_sem=send_sems.at[k],
                recv_sem=recv_sems.at[k], device_id=(tx, ty, tc), device_id_type=MESH).wait_recv()
        for cp in sends:
            cp.wait_send()
        acc = land_ref[0]
        for s in range(1, N_DEV):
            acc = acc + land_ref[s]
        o_ref[...] = acc

    vm = pl.BlockSpec(memory_space=pltpu.VMEM)
    return pl.pallas_call(
        body, name=name, in_specs=[vm], out_specs=vm, out_shape=jax.ShapeDtypeStruct((r, w), F32),
        scratch_shapes=[pltpu.VMEM((N_DEV, r, w), F32), pltpu.SemaphoreType.DMA((N_DEV - 1,)),
                        pltpu.SemaphoreType.DMA((N_DEV - 1,))],
        compiler_params=pltpu.CompilerParams(has_side_effects=True),
    )(buf)


def _adamw(w, g, m, v, name):
    r, c = w.shape
    tr = _pick(r, (256, 128, 64, 8)) if r * c > (1 << 18) else r

    def body(w_ref, g_ref, m_ref, v_ref, d_ref, m2_ref, v2_ref):
        gg = g_ref[...]
        m2 = ADAM_B1 * m_ref[...] + (1.0 - ADAM_B1) * gg
        v2 = ADAM_B2 * v_ref[...] + (1.0 - ADAM_B2) * (gg * gg)
        m_hat = m2 / (1.0 - ADAM_B1 ** ADAM_STEP)
        v_hat = v2 / (1.0 - ADAM_B2 ** ADAM_STEP)
        d_ref[...] = -ADAM_LR * (m_hat / (jnp.sqrt(v_hat) + ADAM_EPS) + ADAM_WD * w_ref[...])
        m2_ref[...] = m2
        v2_ref[...] = v2

    blk = pl.BlockSpec((tr, c), lambda i: (i, 0))
    sds = jax.ShapeDtypeStruct((r, c), F32)
    return pl.pallas_call(body, name=name, grid=(r // tr,), in_specs=[blk] * 4, out_specs=[blk] * 3,
                          out_shape=[sds] * 3, compiler_params=_cparams())(w, g, m, v)


BIG = ("ab_w_in", "ab_w_out", "c_w_in", "c_w_out", "mlp_w1", "mlp_w2")
SMALL = ("ln_mix_g", "ln_mix_b", "ln_ffn_g", "ln_ffn_b", "c_lb_raw", "ab_a_log", "ab_dt_bias", "ab_gnorm_g", "c_gnorm_g")
SMALL_ROWS = 16
CONV_ROWS = 8
CONV_W = 3 * GDN_H * HD


def _conv_to_rows(cw):
    return jnp.pad(cw, ((0, 0), (0, 2 * D - CONV_W))).reshape(CONV_ROWS, D)


def _rows_to_conv(rows):
    return rows.reshape(CONV_K, 2 * D)[:, :CONV_W]


def _pack_small(d):
    rows = [jnp.pad(d[n], ((0, 0), (0, D - d[n].shape[1]))) for n in SMALL]
    buf = jnp.concatenate(rows, axis=0)
    return jnp.pad(buf, ((0, SMALL_ROWS - buf.shape[0]), (0, 0)))


def _unpack_small(buf, like):
    out, r = {}, 0
    for n in SMALL:
        nr, nc = like[n].shape
        out[n] = buf[r:r + nr, :nc]
        r += nr
    return out


def kernel(x, meta_tokens, ab_w_in, ab_conv_w, ab_a_log, ab_dt_bias, ab_gnorm_g, ab_w_out, c_w_in, c_lb_raw, c_gnorm_g, c_w_out, ln_mix_g, ln_mix_b, mlp_w1, mlp_w2, ln_ffn_g, ln_ffn_b, loss_target, m_meta_tokens, m_ab_w_in, m_ab_conv_w, m_ab_a_log, m_ab_dt_bias, m_ab_gnorm_g, m_ab_w_out, m_c_w_in, m_c_lb_raw, m_c_gnorm_g, m_c_w_out, m_ln_mix_g, m_ln_mix_b, m_mlp_w1, m_mlp_w2, m_ln_ffn_g, m_ln_ffn_b, v_meta_tokens, v_ab_w_in, v_ab_conv_w, v_ab_a_log, v_ab_dt_bias, v_ab_gnorm_g, v_ab_w_out, v_c_w_in, v_c_lb_raw, v_c_gnorm_g, v_c_w_out, v_ln_mix_g, v_ln_mix_b, v_mlp_w1, v_mlp_w2, v_ln_ffn_g, v_ln_ffn_b):
    names = ("meta_tokens", "ab_w_in", "ab_conv_w", "ab_a_log", "ab_dt_bias", "ab_gnorm_g", "ab_w_out", "c_w_in",
             "c_lb_raw", "c_gnorm_g", "c_w_out", "ln_mix_g", "ln_mix_b", "mlp_w1", "mlp_w2", "ln_ffn_g", "ln_ffn_b")
    wts = dict(zip(names, (meta_tokens, ab_w_in, ab_conv_w, ab_a_log, ab_dt_bias, ab_gnorm_g, ab_w_out, c_w_in, c_lb_raw,
                           c_gnorm_g, c_w_out, ln_mix_g, ln_mix_b, mlp_w1, mlp_w2, ln_ffn_g, ln_ffn_b)))
    mom_m = dict(zip(names, (m_meta_tokens, m_ab_w_in, m_ab_conv_w, m_ab_a_log, m_ab_dt_bias, m_ab_gnorm_g, m_ab_w_out,
                             m_c_w_in, m_c_lb_raw, m_c_gnorm_g, m_c_w_out, m_ln_mix_g, m_ln_mix_b, m_mlp_w1, m_mlp_w2,
                             m_ln_ffn_g, m_ln_ffn_b)))
    mom_v = dict(zip(names, (v_meta_tokens, v_ab_w_in, v_ab_conv_w, v_ab_a_log, v_ab_dt_bias, v_ab_gnorm_g, v_ab_w_out,
                             v_c_w_in, v_c_lb_raw, v_c_gnorm_g, v_c_w_out, v_ln_mix_g, v_ln_mix_b, v_mlp_w1, v_mlp_w2,
                             v_ln_ffn_g, v_ln_ffn_b)))
    seq = x.shape[1]
    pad = (-(N_META + seq)) % QB
    xi, yi, ci = _pos()
    chip = 2 * xi + yi

    gat_ab_in, = _chip_allgather([ab_w_in[0].astype(BF16)], "gather_weights")
    late = [ab_w_out[0].astype(BF16), c_w_in.astype(BF16), c_w_out[0].astype(BF16), mlp_w1.astype(BF16),
            mlp_w2.astype(BF16)]
    mcols, ccols = meta_tokens.shape[1], ab_conv_w.shape[2]
    place = jnp.concatenate([
        lax.dynamic_update_slice(jnp.zeros((N_META, D), F32), 0.5 * meta_tokens, (0, chip * mcols)),
        _conv_to_rows(lax.dynamic_update_slice(jnp.zeros((CONV_K, CONV_W), F32), 0.5 * ab_conv_w[0], (0, chip * ccols)))],
        axis=0)
    placed = _small_allreduce(place, "gather_meta")
    meta_full = placed[:N_META]

    w = {
        "ab_w_in": _pad_ab_cols(jnp.transpose(gat_ab_in, (1, 0, 2)).reshape(D, AB_TRUE)),
        "conv_w": _rows_to_conv(placed[N_META:]), "a_log": ab_a_log, "dt_bias": ab_dt_bias,
        "ab_gnorm_g": ab_gnorm_g, "c_lb_raw": c_lb_raw,
        "c_gnorm_g": c_gnorm_g, "ln_mix_g": ln_mix_g, "ln_mix_b": ln_mix_b, "ln_ffn_g": ln_ffn_g, "ln_ffn_b": ln_ffn_b,
    }

    h0 = jnp.concatenate([jnp.zeros((pad, D), F32), meta_full, x[0]], axis=0)
    tgt = jnp.concatenate([jnp.zeros((pad + N_META, D), F32), loss_target[0]], axis=0)
    loss8, dh0, g = _local_step(h0, tgt, w, pad, late)
    loss = lax.psum(loss8[0, 0], ("x", "y", "c"))
    grad_x = dh0[pad + N_META:][None]

    gsmall = {"ln_mix_g": g["ln_mix_g"], "ln_mix_b": g["ln_mix_b"], "ln_ffn_g": g["ln_ffn_g"], "ln_ffn_b": g["ln_ffn_b"],
              "c_lb_raw": g["c_lb_raw"], "ab_a_log": g["a_log"], "ab_dt_bias": g["dt_bias"], "ab_gnorm_g": g["ab_gnorm_g"],
              "c_gnorm_g": g["c_gnorm_g"]}
    sbuf = jnp.concatenate([_pack_small(gsmall), dh0[pad:pad + N_META], _conv_to_rows(g["conv_w"])], axis=0)
    ssum = _small_allreduce(sbuf, "allreduce_small")
    grads = _unpack_small(ssum[:SMALL_ROWS], wts)
    grads["meta_tokens"] = lax.dynamic_slice(ssum[SMALL_ROWS:SMALL_ROWS + N_META], (0, chip * mcols), (N_META, mcols))
    grads["ab_conv_w"] = lax.dynamic_slice(_rows_to_conv(ssum[SMALL_ROWS + N_META:]), (0, chip * ccols), (CONV_K, ccols))[None]

    sums = {k: _sum_slots(v, f"grad_sum_{k}") for k, v in g["landed"].items()}
    for n in ("ab_w_in", "ab_w_out", "c_w_in", "c_w_out"):
        grads[n] = sums[n][None]
    grads["mlp_w1"] = jnp.stack([sums["w1_0"], sums["w1_1"]])
    grads["mlp_w2"] = jnp.stack([sums["w2_0"], sums["w2_1"]])

    delta, new_m, new_v = {}, {}, {}
    for n in ("meta_tokens", "ab_conv_w") + BIG:
        shp = wts[n].shape
        to2 = lambda a: a.reshape(-1, shp[-1])
        d2, m2, v2 = _adamw(to2(wts[n]), to2(grads[n]), to2(mom_m[n]), to2(mom_v[n]), f"adamw_{n}")
        delta[n], new_m[n], new_v[n] = d2.reshape(shp), m2.reshape(shp), v2.reshape(shp)
    d2, m2, v2 = _adamw(_pack_small(wts), ssum[:SMALL_ROWS], _pack_small(mom_m), _pack_small(mom_v), "adamw_small")
    delta.update(_unpack_small(d2, wts))
    new_m.update(_unpack_small(m2, wts))
    new_v.update(_unpack_small(v2, wts))

    return (loss, grad_x, *[grads[n] for n in names], *[delta[n] for n in names], *[new_m[n] for n in names],
            *[new_v[n] for n in names])
```

```python
import functools
import math

import numpy as np
import jax
import jax.numpy as jnp
from jax import lax
from jax.experimental import pallas as pl
from jax.experimental.pallas import tpu as pltpu

F32 = jnp.float32
BF16 = jnp.bfloat16

D = 1024
N_META = 16
D_FF = 4 * D
DEPTH = 2
GDN_H = 4
SB_H = 8
SB_DH = 64
HG_H = 8
HD = 128
CH = 64
QB = 128
ALPHA = float((2 * DEPTH) ** 0.25)
LN_EPS = 1e-5
RMS_EPS = 1e-6
L2_EPS = 1e-6
NEG = -1e30

ADAM_LR = 0.001
ADAM_B1 = 0.9
ADAM_B2 = 0.999
ADAM_EPS = 1e-08
ADAM_WD = 0.01
ADAM_STEP = 10

AB_W = 30 * HD
AB_TRUE = 3592
VMEM_LIMIT = 56 * 1024 * 1024

NN = ((1,), (0,))
NT = ((1,), (1,))
TN = ((0,), (0,))


def _cparams(**kw):
    return pltpu.CompilerParams(vmem_limit_bytes=VMEM_LIMIT, **kw)


def _dg(a, b, dims, mode):
    if mode == "h":
        return lax.dot_general(a, b, dims, precision=lax.Precision.HIGHEST, preferred_element_type=F32)
    if mode == "b":
        return lax.dot_general(a.astype(BF16), b.astype(BF16), dims, preferred_element_type=F32)
    ah, bh = a.astype(BF16), b.astype(BF16)
    al, bl = (a - ah.astype(F32)).astype(BF16), (b - bh.astype(F32)).astype(BF16)
    d = lambda x, y: lax.dot_general(x, y, dims, preferred_element_type=F32)
    return d(ah, bh) + (d(ah, bl) + d(al, bh))


def _make_dots(mode, batched=False):
    if batched:
        nn_d, nt_d, tn_d = (((2,), (1,)), ((0,), (0,))), (((2,), (2,)), ((0,), (0,))), (((1,), (1,)), ((0,), (0,)))
    else:
        nn_d, nt_d, tn_d = (NN, ((), ())), (NT, ((), ())), (TN, ((), ()))

    @jax.custom_vjp
    def nn(a, b):
        return _dg(a, b, nn_d, mode)

    @jax.custom_vjp
    def nt(a, b):
        return _dg(a, b, nt_d, mode)

    @jax.custom_vjp
    def tn(a, b):
        return _dg(a, b, tn_d, mode)

    nn.defvjp(lambda a, b: (nn(a, b), (a, b)), lambda r, g: (nt(g, r[1]), tn(r[0], g)))
    nt.defvjp(lambda a, b: (nt(a, b), (a, b)), lambda r, g: (nn(g, r[1]), tn(g, r[0])))
    tn.defvjp(lambda a, b: (tn(a, b), (a, b)), lambda r, g: (nt(r[1], g), nn(r[0], g)))
    return nn, nt, tn


hnn, hnt, htn = _make_dots("h")
bbnn, bbnt, bbtn = _make_dots("b", True)
mbnn, mbnt, mbtn = _make_dots("m", True)
hbnn, hbnt, hbtn = _make_dots("h", True)


def _split3(x, axis):
    x1 = x.astype(BF16)
    r1 = x - x1.astype(F32)
    x2 = r1.astype(BF16)
    x3 = (r1 - x2.astype(F32)).astype(BF16)
    return jnp.concatenate([x1, x2, x3], axis=axis)


@jax.custom_vjp
def _mask_dot(e3, x):
    return lax.dot_general(e3[0], _split3(x, 0), (NN, ((), ())), preferred_element_type=F32)


def _mask_dot_bwd(e3, g):
    dx = lax.dot_general(e3[1], _split3(g, 0), (TN, ((), ())), preferred_element_type=F32)
    return (jnp.zeros_like(e3[0]), jnp.zeros_like(e3[1])), dx


_mask_dot.defvjp(lambda e3, x: (_mask_dot(e3, x), e3), _mask_dot_bwd)


def _heads(a, n):
    return jnp.concatenate([a[None, :, h * HD:(h + 1) * HD] for h in range(n)], axis=0)


def _sigmoid(x):
    return jax.nn.sigmoid(x)


def _silu(x):
    return x * jax.nn.sigmoid(x)


def _softplus(x):
    return jnp.maximum(x, 0.0) + jnp.log(1.0 + jnp.exp(-jnp.abs(x)))


def _iota(shape, dim):
    return lax.broadcasted_iota(jnp.int32, shape, dim)


def _pick(n, prefs):
    for p in prefs:
        if n % p == 0:
            return p
    return n


def _mm(a, b, *, ta=False, tb=False, out_dtype=F32, name, b_view=None, out_split=0, act=False, gate=None, plus=None,
        cargo=(), exchange=None):
    if ta:
        k_dim, m_dim = a.shape
    else:
        m_dim, k_dim = a.shape
    if b_view is None:
        w_rows, w_cols = b.shape
    else:
        kind, layer = b_view
        nj, _, blk_r, blk_c = b.shape
        w_rows, w_cols = (blk_r, nj * blk_c) if kind == "cols" else (nj * blk_r, blk_c)
    n_dim = w_rows if tb else w_cols
    assert (w_cols if tb else w_rows) == k_dim
    tm = _pick(m_dim, (1024, 1056, 704, 640, 512, 384, 256, 128))
    tn = _pick(n_dim, (1024, 1056, 704, 640, 512, 384, 256, 128))
    tk = _pick(k_dim, (1024, 1056, 704, 512, 384, 256, 128))
    nk = k_dim // tk
    a_spec = pl.BlockSpec((tk, tm), lambda i, j, k: (k, i)) if ta else pl.BlockSpec((tm, tk), lambda i, j, k: (i, k))
    wb = (tn, tk) if tb else (tk, tn)
    w_idx = (lambda i, j, k: (j, k)) if tb else (lambda i, j, k: (k, j))
    if b_view is None:
        b_spec = pl.BlockSpec(wb, w_idx)
    elif kind == "cols":
        per = blk_c // wb[1]
        b_spec = pl.BlockSpec((None, None) + wb,
                              lambda i, j, k: (w_idx(i, j, k)[1] // per, layer, w_idx(i, j, k)[0], w_idx(i, j, k)[1] % per))
    else:
        per = blk_r // wb[0]
        b_spec = pl.BlockSpec((None, None) + wb,
                              lambda i, j, k: (w_idx(i, j, k)[0] // per, layer, w_idx(i, j, k)[0] % per, w_idx(i, j, k)[1]))
    if out_split:
        per_o = (n_dim // out_split) // tn
        out_spec = pl.BlockSpec((None, tm, tn), lambda i, j, k: (j // per_o, i, j % per_o))
        out_sds = jax.ShapeDtypeStruct((out_split, m_dim, n_dim // out_split), out_dtype)
    else:
        out_spec = pl.BlockSpec((tm, tn), lambda i, j, k: (i, j))
        out_sds = jax.ShapeDtypeStruct((m_dim, n_dim), out_dtype)
    dims = (((0 if ta else 1,), (1 if tb else 0,)), ((), ()))
    assert gate is None or plus is None
    extra = [e for e in (gate, plus) if e is not None]
    n_out = 2 if act else 1

    def finish(acc, refs):
        if act:
            refs[0][...] = acc.astype(refs[0].dtype)
            r = jnp.maximum(acc, 0.0)
            refs[1][...] = (r * r).astype(refs[1].dtype)
        elif gate is not None:
            refs[1][...] = (acc * (2.0 * jnp.maximum(refs[0][...].astype(F32), 0.0))).astype(refs[1].dtype)
        elif plus is not None:
            refs[1][...] = (refs[0][...] + acc).astype(refs[1].dtype)
        else:
            refs[0][...] = acc.astype(refs[0].dtype)

    grid = (m_dim // tm, n_dim // tn, nk)
    nc = len(cargo)

    def body(a_ref, b_ref, *rest):
        acc_ref = rest[-1]
        ids = [pl.program_id(d) for d in range(3)]
        outs, end_cargo = _cargo_bounds(
            rest[len(extra):-1], nc, n_out, exchange, (ids[0] == 0) & (ids[1] == 0) & (ids[2] == 0),
            (ids[0] == grid[0] - 1) & (ids[1] == grid[1] - 1) & (ids[2] == grid[2] - 1))
        refs = tuple(rest[:len(extra)]) + tuple(outs)
        part = lax.dot_general(a_ref[...], b_ref[...], dims, preferred_element_type=F32)
        if nk == 1:
            finish(part, refs)
        else:
            k = ids[2]

            @pl.when(k == 0)
            def _():
                acc_ref[...] = part

            @pl.when(k > 0)
            def _():
                acc_ref[...] += part

            @pl.when(k == nk - 1)
            def _():
                finish(acc_ref[...], refs)
        end_cargo()

    out = pl.pallas_call(
        body, name=name, grid=grid,
        in_specs=[a_spec, b_spec] + [pl.BlockSpec((tm, tn), lambda i, j, k: (i, j))] * len(extra) + [ANY] * nc,
        out_specs=[out_spec] * n_out + [ANY] * nc,
        out_shape=[out_sds] * n_out + (exchange[1](cargo) if nc else []),
        scratch_shapes=(exchange[2](nc) if nc else []) + [pltpu.VMEM((tm, tn) if nk > 1 else (8, 128), F32)],
        compiler_params=_cparams(dimension_semantics=("arbitrary",) * 3 if nc else ("parallel", "parallel", "arbitrary")),
    )(a, b, *extra, *cargo)
    if nc:
        return out
    return out if act else out[0]


def _mm_groups_nt(parts, b, name):
    m_dim, k_dim = parts[0].shape
    ng, _, n_dim, _ = b.shape
    assert len(parts) == ng and b.shape[3] == k_dim
    tm = _pick(m_dim, (1056, 704, 512, 384, 256, 128))

    def body(*refs):
        a_refs, b_ref, o_ref, acc_ref = refs[:ng], refs[ng], refs[ng + 1], refs[ng + 2]
        k = pl.program_id(1)
        for g in range(ng):
            @pl.when(k == g)
            def _(g=g):
                part = lax.dot_general(a_refs[g][...], b_ref[...], (NT, ((), ())), preferred_element_type=F32)
                if g == 0:
                    acc_ref[...] = part
                elif g < ng - 1:
                    acc_ref[...] += part
                else:
                    o_ref[...] = acc_ref[...] + part

    return pl.pallas_call(
        body, name=name, grid=(m_dim // tm, ng),
        in_specs=[pl.BlockSpec((tm, k_dim), lambda i, k: (i, 0))] * ng
        + [pl.BlockSpec((None, None, n_dim, k_dim), lambda i, k: (k, 0, 0, 0))],
        out_specs=pl.BlockSpec((tm, n_dim), lambda i, k: (i, 0)),
        out_shape=jax.ShapeDtypeStruct((m_dim, n_dim), F32),
        scratch_shapes=[pltpu.VMEM((tm, n_dim), F32)],
        compiler_params=_cparams(dimension_semantics=("parallel", "arbitrary")),
    )(*parts, b)


def _row_tile(t_pad, width):
    for tr in (528, 352, 176, 128, 64):
        if t_pad % tr == 0 and tr * width * 4 <= (3 << 19) and tr % 16 == 0:
            return tr
    return 64 if t_pad % 64 == 0 else t_pad


def _ln_res_fn(h, m, g, b):
    x = ALPHA * h + m
    mu = jnp.mean(x, axis=-1, keepdims=True)
    xc = x - mu
    var = jnp.mean(xc * xc, axis=-1, keepdims=True)
    return xc * lax.rsqrt(var + LN_EPS) * g + b


def _ln_res_fwd(h, m, g, b, name):
    t_pad = h.shape[0]
    tr = _row_tile(t_pad, D)

    def body(h_ref, m_ref, g_ref, b_ref, y_ref, yb_ref):
        y = _ln_res_fn(h_ref[...], m_ref[...], g_ref[...], b_ref[...])
        y_ref[...] = y
        yb_ref[...] = y.astype(BF16)

    row = pl.BlockSpec((tr, D), lambda i: (i, 0))
    par = pl.BlockSpec((1, D), lambda i: (0, 0))
    return pl.pallas_call(
        body, name=name, grid=(t_pad // tr,), in_specs=[row, row, par, par], out_specs=[row, row],
        out_shape=[jax.ShapeDtypeStruct((t_pad, D), F32), jax.ShapeDtypeStruct((t_pad, D), BF16)],
        compiler_params=_cparams(),
    )(h, m, g, b)


def _ln_res_bwd(h, m, g, b, dys, name):
    t_pad = h.shape[0]
    tr = _row_tile(t_pad, D)
    nd = len(dys)

    def body(h_ref, m_ref, g_ref, b_ref, *rest):
        d_refs, (dh_ref, dm_ref, dg_ref, db_ref) = rest[:nd], rest[nd:]
        _, vjp = jax.vjp(_ln_res_fn, h_ref[...], m_ref[...], g_ref[...], b_ref[...])
        dy = d_refs[0][...]
        for d_ref in d_refs[1:]:
            dy = dy + d_ref[...]
        dh, dm, dg, db = vjp(dy)
        dh_ref[...] = dh
        dm_ref[...] = dm.astype(BF16)

        @pl.when(pl.program_id(0) == 0)
        def _():
            dg_ref[...] = jnp.zeros_like(dg_ref)
            db_ref[...] = jnp.zeros_like(db_ref)

        dg_ref[...] += dg
        db_ref[...] += db

    row = pl.BlockSpec((tr, D), lambda i: (i, 0))
    par = pl.BlockSpec((1, D), lambda i: (0, 0))
    return pl.pallas_call(
        body, name=name, grid=(t_pad // tr,), in_specs=[row, row, par, par] + [row] * nd,
        out_specs=[row, row, par, par],
        out_shape=[jax.ShapeDtypeStruct((t_pad, D), F32), jax.ShapeDtypeStruct((t_pad, D), BF16),
                   jax.ShapeDtypeStruct((1, D), F32), jax.ShapeDtypeStruct((1, D), F32)],
        compiler_params=_cparams(),
    )(h, m, g, b, *dys)


def _grms_fn(o, z, g):
    y = o * lax.rsqrt(jnp.mean(o * o, axis=-1, keepdims=True) + RMS_EPS) * g
    return y * _silu(z)


def _grms_fwd(o, z_arr, z_blk0, g, name):
    t_pad, w = o.shape
    tr = _row_tile(t_pad, w)
    assert (z_blk0 * HD) % w == 0

    def body(o_ref, z_ref, g_ref, y_ref):
        for h in range(w // HD):
            c = slice(h * HD, (h + 1) * HD)
            y_ref[:, c] = _grms_fn(o_ref[:, c], z_ref[:, c], g_ref[...]).astype(BF16)

    return pl.pallas_call(
        body, name=name, grid=(t_pad // tr,),
        in_specs=[pl.BlockSpec((tr, w), lambda i: (i, 0)), pl.BlockSpec((tr, w), lambda i: (i, z_blk0 * HD // w)),
                  pl.BlockSpec((1, HD), lambda i: (0, 0))],
        out_specs=pl.BlockSpec((tr, w), lambda i: (i, 0)),
        out_shape=jax.ShapeDtypeStruct((t_pad, w), BF16), compiler_params=_cparams(),
    )(o, z_arr, g)


def _grms_bwd(o, z_arr, z_blk0, g, dy_arr, dy_blk0, name):
    t_pad, w = o.shape
    tr = _row_tile(t_pad, w)
    assert (z_blk0 * HD) % w == 0 and (dy_blk0 * HD) % w == 0

    def body(o_ref, z_ref, g_ref, dy_ref, do_ref, dz_ref, dg_ref):
        @pl.when(pl.program_id(0) == 0)
        def _():
            dg_ref[...] = jnp.zeros_like(dg_ref)

        for h in range(w // HD):
            c = slice(h * HD, (h + 1) * HD)
            _, vjp = jax.vjp(_grms_fn, o_ref[:, c], z_ref[:, c], g_ref[...])
            do, dz, dg = vjp(dy_ref[:, c])
            do_ref[:, c] = do
            dz_ref[:, c] = dz.astype(BF16)
            dg_ref[...] += dg

    blk = pl.BlockSpec((tr, w), lambda i: (i, 0))
    return pl.pallas_call(
        body, name=name, grid=(t_pad // tr,),
        in_specs=[blk, pl.BlockSpec((tr, w), lambda i: (i, z_blk0 * HD // w)), pl.BlockSpec((1, HD), lambda i: (0, 0)),
                  pl.BlockSpec((tr, w), lambda i: (i, dy_blk0 * HD // w))],
        out_specs=[blk, blk, pl.BlockSpec((1, HD), lambda i: (0, 0))],
        out_shape=[jax.ShapeDtypeStruct((t_pad, w), F32), jax.ShapeDtypeStruct((t_pad, w), BF16),
                   jax.ShapeDtypeStruct((1, HD), F32)],
        compiler_params=_cparams(),
    )(o, z_arr, g, dy_arr)


def _loss_fwd(y, tgt, first_row, name):
    t_pad = y.shape[0]
    tr = _row_tile(t_pad, D)

    def body(y_ref, t_ref, l_ref, dy_ref):
        rows = pl.program_id(0) * tr + _iota((tr, 1), 0)
        err = jnp.where(rows >= first_row, y_ref[...] - t_ref[...], 0.0)
        dy_ref[...] = err * (1.0 / D)

        @pl.when(pl.program_id(0) == 0)
        def _():
            l_ref[...] = jnp.zeros_like(l_ref)

        part = jnp.sum(jnp.sum(err * err, axis=1, keepdims=True), axis=0, keepdims=True)
        l_ref[...] += jnp.broadcast_to(part * (0.5 / D), l_ref.shape)

    row = pl.BlockSpec((tr, D), lambda i: (i, 0))
    return pl.pallas_call(
        body, name=name, grid=(t_pad // tr,), in_specs=[row, row],
        out_specs=[pl.BlockSpec((8, 128), lambda i: (0, 0)), row],
        out_shape=[jax.ShapeDtypeStruct((8, 128), F32), jax.ShapeDtypeStruct((t_pad, D), F32)],
        compiler_params=_cparams(),
    )(y, tgt)


def _assemble_bf16(parts, name):
    t_pad = parts[0][0].shape[0] if parts[0][1] == "cols" else parts[0][0].shape[1]
    widths = [p.shape[1] if kind == "cols" else HD for p, kind in parts]
    total = sum(widths)
    tr = _row_tile(t_pad, total)

    def body(*refs):
        o_ref = refs[-1]
        off = 0
        for ref, (p, kind), w in zip(refs[:-1], parts, widths):
            if kind == "cols":
                o_ref[:, off:off + w] = ref[...].astype(BF16)
            else:
                acc = ref[0]
                for hh in range(1, p.shape[0]):
                    acc = acc + ref[hh]
                o_ref[:, off:off + w] = acc.astype(BF16)
            off += w

    specs = []
    for p, kind in parts:
        if kind == "cols":
            specs.append(pl.BlockSpec((tr, p.shape[1]), lambda i: (i, 0)))
        else:
            specs.append(pl.BlockSpec((p.shape[0], tr, HD), lambda i: (0, i, 0)))
    return pl.pallas_call(
        body, name=name, grid=(t_pad // tr,), in_specs=specs,
        out_specs=pl.BlockSpec((tr, total), lambda i: (i, 0)),
        out_shape=jax.ShapeDtypeStruct((t_pad, total), BF16), compiler_params=_cparams(),
    )(*[p for p, _ in parts])


CONV_K = 4
HALO = 8
RT = 128


def _conv_fwd(p, blk0, w, mode, pad, name):
    t_pad = p.shape[0]
    nt = t_pad // RT
    scale = HD ** -0.5 if mode == "q" else 1.0

    def body(x_ref, w_ref, y_ref, xs_ref):
        xs_ref[0:HALO, :] = jnp.zeros((HALO, HD), F32)
        rows = _iota((t_pad, 1), 0)
        xs_ref[HALO:HALO + t_pad, :] = jnp.where(rows >= pad, x_ref[...], 0.0)
        wv = w_ref[...]

        def tile(i, carry):
            r0 = pl.multiple_of(i * RT, RT)
            ext = xs_ref[pl.ds(r0, RT + HALO), :]
            acc = ext[HALO:, :] * wv[3:4, :]
            for s in (1, 2, 3):
                acc = acc + pltpu.roll(ext, s, 0)[HALO:, :] * wv[3 - s:4 - s, :]
            y = _silu(acc)
            if mode != "v":
                y = y * lax.rsqrt(jnp.sum(y * y, axis=-1, keepdims=True) + L2_EPS) * scale
            y_ref[pl.ds(r0, RT), :] = y
            return carry

        lax.fori_loop(0, nt, tile, 0)

    return pl.pallas_call(
        body, name=name, grid=(GDN_H,),
        in_specs=[pl.BlockSpec((t_pad, HD), lambda h: (0, blk0 + h)), pl.BlockSpec((CONV_K, HD), lambda h: (0, h))],
        out_specs=pl.BlockSpec((t_pad, HD), lambda h: (0, h)),
        out_shape=jax.ShapeDtypeStruct((t_pad, GDN_H * HD), F32),
        scratch_shapes=[pltpu.VMEM((t_pad + HALO, HD), F32)],
        compiler_params=_cparams(),
    )(p, w)


def _conv_bwd(p, blk0, w, dn, mode, pad, name):
    t_pad = p.shape[0]
    nt = t_pad // RT
    scale = HD ** -0.5 if mode == "q" else 1.0

    def body(x_ref, w_ref, dn_ref, dx_ref, dw_ref, xs_ref, ds_ref):
        xs_ref[0:HALO, :] = jnp.zeros((HALO, HD), F32)
        xs_ref[HALO + t_pad:HALO + t_pad + 2 * HALO, :] = jnp.zeros((2 * HALO, HD), F32)
        ds_ref[t_pad:t_pad + HALO, :] = jnp.zeros((HALO, HD), F32)
        rows = _iota((t_pad, 1), 0)
        xs_ref[HALO:HALO + t_pad, :] = jnp.where(rows >= pad, x_ref[...], 0.0)
        ds_ref[0:t_pad, :] = dn_ref[...]
        wv = w_ref[...]

        def tile(i, dw):
            r0 = pl.multiple_of(i * RT, RT)
            ext = xs_ref[pl.ds(r0, RT + 2 * HALO), :]
            dn_e = ds_ref[pl.ds(r0, RT + HALO), :]
            xsh = [ext[HALO:, :]] + [pltpu.roll(ext, s, 0)[HALO:, :] for s in (1, 2, 3)]
            pre = xsh[0] * wv[3:4, :]
            for s in (1, 2, 3):
                pre = pre + xsh[s] * wv[3 - s:4 - s, :]
            sg = _sigmoid(pre)
            y = pre * sg
            if mode != "v":
                ss = jnp.sum(y * y, axis=-1, keepdims=True) + L2_EPS
                r = lax.rsqrt(ss)
                dy = scale * (dn_e * r - y * (r * r * r) * jnp.sum(dn_e * y, axis=-1, keepdims=True))
            else:
                dy = dn_e
            dpre = dy * (sg * (1.0 + pre * (1.0 - sg)))
            dx = dpre[:RT, :] * wv[3:4, :]
            for s in (1, 2, 3):
                dx = dx + pltpu.roll(dpre, RT + HALO - s, 0)[:RT, :] * wv[3 - s:4 - s, :]
            trow = r0 + _iota((RT, 1), 0)
            dx_ref[pl.ds(r0, RT), :] = jnp.where(trow >= pad, dx, 0.0)
            new = []
            for s in (0, 1, 2, 3):
                new.append(dw[s] + jnp.sum(dpre[:RT, :] * xsh[s][:RT, :], axis=0, keepdims=True))
            return tuple(new)

        z = jnp.zeros((1, HD), F32)
        dw = lax.fori_loop(0, nt, tile, (z, z, z, z))
        for s in (0, 1, 2, 3):
            dw_ref[3 - s:4 - s, :] = dw[s]

    return pl.pallas_call(
        body, name=name, grid=(GDN_H,),
        in_specs=[pl.BlockSpec((t_pad, HD), lambda h: (0, blk0 + h)), pl.BlockSpec((CONV_K, HD), lambda h: (0, h)),
                  pl.BlockSpec((t_pad, HD), lambda h: (0, h))],
        out_specs=[pl.BlockSpec((t_pad, HD), lambda h: (0, h)), pl.BlockSpec((CONV_K, HD), lambda h: (0, h))],
        out_shape=[jax.ShapeDtypeStruct((t_pad, GDN_H * HD), F32), jax.ShapeDtypeStruct((CONV_K, GDN_H * HD), F32)],
        scratch_shapes=[pltpu.VMEM((t_pad + 3 * HALO, HD), F32), pltpu.VMEM((t_pad + HALO, HD), F32)],
        compiler_params=_cparams(),
    )(p, w, dn)


@jax.custom_vjp
def _unit_lower_inv(m, bd, eye):
    md = m * bd
    low = m - md
    p2 = mbnn(md, md)
    p4 = mbnn(p2, p2)
    dinv = mbnn(mbnn(eye - md, eye + p2), eye + p4)
    n = mbnn(dinv, low)
    n2 = mbnn(n, n)
    n4 = mbnn(n2, n2)
    return mbnn(mbnn(mbnn(eye - n, eye + n2), eye + n4), dinv)


def _unit_lower_inv_bwd(res, g):
    t, bd, eye = res
    return -mbtn(t, mbnt(g, t)), jnp.zeros_like(bd), jnp.zeros_like(eye)


def _unit_lower_inv_fwd(m, bd, eye):
    t = _unit_lower_inv(m, bd, eye)
    return t, (t, bd, eye)


_unit_lower_inv.defvjp(_unit_lower_inv_fwd, _unit_lower_inv_bwd)


def _gdn_chunks(chunks, alog, dtb, s):
    nh = chunks[0][0].shape[0]
    ri = _iota((1, CH, CH), 1)
    ci = _iota((1, CH, CH), 2)
    causal = ri >= ci
    strict = ri > ci
    eye = (ri == ci).astype(F32)
    bd = ((ri >> 3) == (ci >> 3)).astype(F32)
    ltri = (_iota((CH, CH), 0) >= _iota((CH, CH), 1)).astype(F32)
    sel = (_iota((nh, 1, HD), 2) == _iota((nh, 1, HD), 0)).astype(F32)
    last = _iota((1, CH, 1), 1) == CH - 1

    beta, gc, gc_rows = [], [], []
    for _, _, _, bb, aa, valid in chunks:
        beta_all = jnp.where(valid, _sigmoid(bb), 0.0)
        g_all = jnp.where(valid, -jnp.exp(alog) * _softplus(aa + dtb), 0.0)
        gc_all = hnn(ltri, g_all)
        beta.append(jnp.sum(beta_all[None] * sel, axis=2, keepdims=True))
        gc.append(jnp.sum(gc_all[None] * sel, axis=2, keepdims=True))
        gc_rows.append(hbnt(jnp.broadcast_to(sel, (nh, CH, HD)), jnp.broadcast_to(gc_all[None], (nh, CH, HD))))
    cat = lambda xs: jnp.concatenate(xs, axis=0)
    q, k, v = (cat([c[j] for c in chunks]) for j in range(3))
    beta, gc, gc_rows = cat(beta), cat(gc), cat(gc_rows)
    gc_last = jnp.sum(jnp.where(last, gc, 0.0), axis=1, keepdims=True)
    decay = jnp.exp(jnp.where(causal, gc - gc_rows, NEG))
    egc = jnp.exp(gc)

    kb = k * beta
    m = jnp.where(strict, bbnt(kb, k) * decay, 0.0)
    t_inv = _unit_lower_inv(m, bd, eye)
    u = bbnn(t_inv, v * beta)
    w = bbnn(t_inv, kb * egc)
    a_intra = bbnt(q, k) * decay
    q_dec = q * egc
    k_dec = k * jnp.exp(gc_last - gc)
    g_tot = jnp.exp(gc_last)

    outs = []
    for n in range(len(chunks)):
        part = lambda a: a[n * nh:(n + 1) * nh]
        v_new = part(u) - bbnn(part(w), s)
        outs.append(bbnn(part(q_dec), s) + bbnn(part(a_intra), v_new))
        s = s * part(g_tot) + bbtn(part(k_dec), v_new)
    return outs, s


PAIR = 2 * CH


def _gdn_specs(npair, rev):
    cc = (lambda c: npair - 1 - c) if rev else (lambda c: c)
    wide = pl.BlockSpec((PAIR, GDN_H * HD), lambda c: (cc(c), 0))
    fix = lambda off: pl.BlockSpec((PAIR, HD), lambda c: (cc(c), off))
    par = pl.BlockSpec((1, HD), lambda c: (0, 0))
    state = pl.BlockSpec((1, GDN_H, HD, HD), lambda c: (cc(c), 0, 0, 0))
    return wide, fix, par, state


def _store_heads(ref, a, rows=slice(None)):
    for h in range(a.shape[0]):
        ref[rows, h * HD:(h + 1) * HD] = a[h]


def _chunk_rows(half):
    return slice(half * CH, (half + 1) * CH)


def _chunk_valid(pair, half, pad):
    return ((2 * pair + half) * CH + _iota((CH, 1), 0)) >= pad


def _gdn_fwd(qn, kn, vn, p, alog, dtb, pad, name, cargo=(), exchange=None):
    t_pad = qn.shape[0]
    npair = t_pad // PAIR
    wide, fix, par, state = _gdn_specs(npair, False)
    n = len(cargo)

    def body(q_ref, k_ref, v_ref, bb_ref, aa_ref, al_ref, dt_ref, *rest):
        c = pl.program_id(0)
        s_ref = rest[-1]
        (o_ref, ss_ref), end_cargo = _cargo_bounds(rest[:-1], n, 2, exchange, c == 0, c == npair - 1)

        @pl.when(c == 0)
        def _():
            s_ref[...] = jnp.zeros_like(s_ref)

        s = s_ref[...]
        ss_ref[0] = s
        rows = [_chunk_rows(half) for half in (0, 1)]
        chunks = [(_heads(q_ref[r, :], GDN_H), _heads(k_ref[r, :], GDN_H), _heads(v_ref[r, :], GDN_H),
                   bb_ref[r, :], aa_ref[r, :], _chunk_valid(c, half, pad)) for half, r in enumerate(rows)]
        outs, s = _gdn_chunks(chunks, al_ref[...], dt_ref[...], s)
        for r, o in zip(rows, outs):
            _store_heads(o_ref, o, r)
        s_ref[...] = s
        end_cargo()

    return pl.pallas_call(
        body, name=name, grid=(npair,),
        in_specs=[wide, wide, wide, fix(16), fix(17), par, par] + [ANY] * n,
        out_specs=[wide, state] + [ANY] * n,
        out_shape=[jax.ShapeDtypeStruct((t_pad, GDN_H * HD), F32), jax.ShapeDtypeStruct((npair, GDN_H, HD, HD), F32)]
        + (exchange[1](cargo) if n else []),
        scratch_shapes=(exchange[2](n) if n else []) + [pltpu.VMEM((GDN_H, HD, HD), F32)],
        compiler_params=_cparams(),
    )(qn, kn, vn, p, p, alog, dtb, *cargo)


def _gdn_bwd(qn, kn, vn, p, alog, dtb, ssave, do, pad, name, cargo=(), exchange=None):
    t_pad = qn.shape[0]
    npair = t_pad // PAIR
    wide, fix, par, state = _gdn_specs(npair, True)
    n = len(cargo)

    def body(q_ref, k_ref, v_ref, bb_ref, aa_ref, al_ref, dt_ref, ss_ref, do_ref, *rest):
        c = pl.program_id(0)
        ds_ref = rest[-1]
        (dq_ref, dk_ref, dv_ref, dbb_ref, daa_ref, dal_ref, ddt_ref), end_cargo = _cargo_bounds(
            rest[:-1], n, 7, exchange, c == 0, c == npair - 1)

        @pl.when(c == 0)
        def _():
            ds_ref[...] = jnp.zeros_like(ds_ref)
            dal_ref[...] = jnp.zeros_like(dal_ref)
            ddt_ref[...] = jnp.zeros_like(ddt_ref)

        ra, rb = _chunk_rows(0), _chunk_rows(1)
        va, vb = _chunk_valid(npair - 1 - c, 0, pad), _chunk_valid(npair - 1 - c, 1, pad)

        def pair(qa, ka, va_, ba, aa, qb, kb, vb_, bb, ab, al, dt, s):
            (oa, ob), s = _gdn_chunks([(qa, ka, va_, ba, aa, va), (qb, kb, vb_, bb, ab, vb)], al, dt, s)
            return oa, ob, s

        ins = [f(ref[r, :]) for r in (ra, rb)
               for ref, f in ((q_ref, lambda a: _heads(a, GDN_H)), (k_ref, lambda a: _heads(a, GDN_H)),
                              (v_ref, lambda a: _heads(a, GDN_H)), (bb_ref, lambda a: a), (aa_ref, lambda a: a))]
        _, vjp = jax.vjp(pair, *ins, al_ref[...], dt_ref[...], ss_ref[0])
        g = vjp((_heads(do_ref[ra, :], GDN_H), _heads(do_ref[rb, :], GDN_H), ds_ref[...]))
        for r, (dq, dk, dv, dbb, daa) in ((ra, g[0:5]), (rb, g[5:10])):
            _store_heads(dq_ref, dq, r)
            _store_heads(dk_ref, dk, r)
            _store_heads(dv_ref, dv, r)
            dbb_ref[r, :] = dbb
            daa_ref[r, :] = daa
        dal_ref[...] += g[10]
        ddt_ref[...] += g[11]
        ds_ref[...] = g[12]
        end_cargo()

    sds = jax.ShapeDtypeStruct
    return pl.pallas_call(
        body, name=name, grid=(npair,),
        in_specs=[wide, wide, wide, fix(16), fix(17), par, par, state, wide] + [ANY] * n,
        out_specs=[wide, wide, wide, fix(0), fix(0), par, par] + [ANY] * n,
        out_shape=[sds((t_pad, GDN_H * HD), F32)] * 3 + [sds((t_pad, HD), F32)] * 2 + [sds((1, HD), F32)] * 2
        + (exchange[1](cargo) if n else []),
        scratch_shapes=(exchange[2](n) if n else []) + [pltpu.VMEM((GDN_H, HD, HD), F32)],
        compiler_params=_cparams(),
    )(qn, kn, vn, p, p, alog, dtb, ssave, do, *cargo)


SB_Q0, SB_K0, SB_V0 = 18, 22, 26
SB_SCALE = SB_DH ** -0.5
SB_NB = 4


def _sb_terms(z, allowed):
    e = jnp.exp(-jnp.abs(z))
    den = 1.0 + e
    raw = -jnp.maximum(z, 0.0) - jnp.log(den)
    l1m = raw if allowed is None else jnp.where(allowed, raw, 0.0)
    return l1m, z + raw, jnp.where(z >= 0.0, 1.0, e) / den


def _sb_passes(i, step, carry):
    total = i + 1
    sized = lambda done: [functools.partial(step, done, masked=True, nb=nb) for nb in range(1, SB_NB + 1)]

    def several(c):
        n_mid = (total - SB_NB - 1) // SB_NB
        c = step(0, c, masked=True, nb=SB_NB)
        c = lax.fori_loop(0, n_mid, lambda t, cc: step(SB_NB * (1 + t), cc, masked=False, nb=SB_NB), c)
        done = SB_NB * (1 + n_mid)
        return lax.switch(total - done - 1, sized(done), c)

    return lax.cond(total <= SB_NB, lambda c: lax.switch(total - 1, sized(0), c), several, carry)


def _sb_stack(a, i):
    first = _iota((1, HD), 1) < SB_DH
    a2 = jnp.concatenate([jnp.where(first, a, 0.0), jnp.where(first, 0.0, a)], axis=0).astype(BF16)
    rq = i * QB + _iota((QB, 1), 0)
    return a2, jnp.concatenate([rq, rq], axis=0), first


def _hi_lo(a):
    hi = a.astype(BF16)
    lo = (a - hi.astype(F32)).astype(BF16)
    return jnp.concatenate([hi, lo], axis=1)


def _cargo_bounds(refs, n, n_out, exchange, first, last):
    outs = refs[n:n + n_out]
    if not n:
        return outs, lambda: None
    ex = exchange[0](refs[:n], refs[n + n_out:2 * n + n_out], *refs[2 * n + n_out:])

    @pl.when(first)
    def _():
        ex.start()

    def finish():
        @pl.when(last)
        def _():
            ex.wait()

    return outs, finish


def _sb_fwd(p, pad, name, cargo=(), exchange=None):
    t_pad = p.shape[0]
    nq = t_pad // QB
    n = len(cargo)

    def body(q_ref, k_ref, v_ref, *rest):
        i = pl.program_id(1)
        pr = pl.program_id(0)
        (o_ref, r_ref), end_cargo = _cargo_bounds(rest, n, 2, exchange, (pr == 0) & (i == 0),
                                                  (pr == SB_H // 2 - 1) & (i == nq - 1))
        q2, rowq, first = _sb_stack(q_ref[...] * SB_SCALE, i)
        tri = (_iota((QB, QB), 0) > _iota((QB, QB), 1)).astype(BF16)
        upper2 = jnp.concatenate([tri, tri], axis=0)

        def chain(kb, masked):
            start = pl.multiple_of(kb * QB, QB)
            kblk = k_ref[pl.ds(start, QB), :].astype(BF16)
            vblk = v_ref[pl.ds(start, QB), :].astype(BF16)
            z = lax.dot_general(q2, kblk, (NT, ((), ())), preferred_element_type=F32)
            colk = kb * QB + _iota((1, QB), 1)
            al = ((colk < rowq) & (colk >= pad)) if masked else None
            l1m, ls, _ = _sb_terms(z, al)
            suf = lax.dot_general(_hi_lo(l1m), upper2, (NN, ((), ())), preferred_element_type=F32)
            return al, ls, suf, jnp.sum(l1m, axis=1, keepdims=True), vblk

        def step(done, carry, masked, nb):
            o_acc, run = carry
            ws, vs = [], []
            for n in range(nb):
                al, ls, suf, rs, vblk = chain(i - done - n, masked)
                wgt = jnp.exp(ls + suf + run)
                ws.append((wgt if al is None else jnp.where(al, wgt, 0.0)).astype(BF16))
                vs.append(vblk)
                run = run + rs
            o_acc = o_acc + lax.dot_general(jnp.concatenate(ws, axis=1), jnp.concatenate(vs, axis=0),
                                            (NN, ((), ())), preferred_element_type=F32)
            return o_acc, run

        o_acc, run = _sb_passes(i, step, (jnp.zeros((2 * QB, HD), F32), jnp.zeros((2 * QB, 1), F32)))
        o_ref[...] = jnp.where(first, o_acc[:QB], o_acc[QB:]).astype(BF16)
        r_ref[...] = jnp.where(first, run[:QB], run[QB:])
        end_cargo()

    full = lambda off: pl.BlockSpec((t_pad, HD), lambda pr, i: (0, off + pr))
    blk = pl.BlockSpec((QB, HD), lambda pr, i: (i, pr))
    return pl.pallas_call(
        body, name=name, grid=(SB_H // 2, nq),
        in_specs=[pl.BlockSpec((QB, HD), lambda pr, i: (i, SB_Q0 + pr)), full(SB_K0), full(SB_V0)] + [ANY] * n,
        out_specs=[blk, blk] + [ANY] * n,
        out_shape=[jax.ShapeDtypeStruct((t_pad, SB_H * SB_DH), BF16), jax.ShapeDtypeStruct((t_pad, SB_H * SB_DH), F32)]
        + (exchange[1](cargo) if n else []),
        scratch_shapes=exchange[2](n) if n else [],
        compiler_params=_cparams(),
    )(p, p, p, *cargo)


def _sb_bwd(p, rtot, dy, dy_blk0, pad, name, cargo=(), exchange=None):
    t_pad = p.shape[0]
    nq = t_pad // QB
    n = len(cargo)

    def body(q_ref, k_ref, v_ref, r_ref, do_ref, *rest):
        i = pl.program_id(1)
        pr = pl.program_id(0)
        (dq_ref, dk_ref, dv_ref), end_cargo = _cargo_bounds(rest, n, 3, exchange, (pr == 0) & (i == 0),
                                                            (pr == SB_H // 2 - 1) & (i == nq - 1))

        @pl.when(i == 0)
        def _():
            dk_ref[...] = jnp.zeros_like(dk_ref)
            dv_ref[...] = jnp.zeros_like(dv_ref)

        q2, rowq, first = _sb_stack(q_ref[...] * SB_SCALE, i)
        do2, _, _ = _sb_stack(do_ref[...], i)
        rt = r_ref[...]
        lane = _iota((1, HD), 1)
        rcol = jnp.concatenate([jnp.sum(jnp.where(lane == 0, rt, 0.0), axis=1, keepdims=True),
                                jnp.sum(jnp.where(lane == SB_DH, rt, 0.0), axis=1, keepdims=True)], axis=0)
        rj = _iota((QB, QB), 0)
        cs = _iota((QB, QB), 1)
        tri_u = (rj > cs).astype(BF16)
        tri_l = (rj < cs).astype(BF16)
        upper2 = jnp.concatenate([tri_u, tri_u], axis=0)
        lower2 = jnp.concatenate([tri_l, tri_l], axis=0)

        def chain(kb, masked):
            start = pl.multiple_of(kb * QB, QB)
            kblk = k_ref[pl.ds(start, QB), :].astype(BF16)
            vblk = v_ref[pl.ds(start, QB), :].astype(BF16)
            z = lax.dot_general(q2, kblk, (NT, ((), ())), preferred_element_type=F32)
            colk = kb * QB + _iota((1, QB), 1)
            al = ((colk < rowq) & (colk >= pad)) if masked else None
            l1m, ls, sg = _sb_terms(z, al)
            dwgt = lax.dot_general(do2, vblk, (NT, ((), ())), preferred_element_type=F32)
            suf = lax.dot_general(_hi_lo(l1m), upper2, (NN, ((), ())), preferred_element_type=F32)
            return start, kblk, al, ls, suf, jnp.sum(l1m, axis=1, keepdims=True), dwgt, sg

        def finish(c, seen, gseen):
            start, kblk, al, ls, suf, rs, dwgt, sg = c
            wgt = jnp.exp(ls + suf + (rcol - seen - rs))
            if al is not None:
                wgt = jnp.where(al, wgt, 0.0)
            dl = dwgt * wgt
            gpre = gseen + lax.dot_general(_hi_lo(dl), lower2, (NN, ((), ())), preferred_element_type=F32)
            dz = dl - sg * (dl + gpre)
            if al is not None:
                dz = jnp.where(al, dz, 0.0)
            dz = dz.astype(BF16)
            dk_ref[pl.ds(start, QB), :] += lax.dot_general(dz, q2, (TN, ((), ())), preferred_element_type=F32)
            dv_ref[pl.ds(start, QB), :] += lax.dot_general(wgt.astype(BF16), do2, (TN, ((), ())),
                                                           preferred_element_type=F32)
            return dz, seen + rs, gseen + jnp.sum(dl, axis=1, keepdims=True)

        def step(done, carry, masked, nb):
            dq_acc, seen, gseen = carry
            cs_ = [chain(done + n, masked) for n in range(nb)]
            dzs = []
            for c in cs_:
                dz, seen, gseen = finish(c, seen, gseen)
                dzs.append(dz)
            dq_acc = dq_acc + lax.dot_general(jnp.concatenate(dzs, axis=1), jnp.concatenate([c[1] for c in cs_], axis=0),
                                              (NN, ((), ())), preferred_element_type=F32)
            return dq_acc, seen, gseen

        zc = jnp.zeros((2 * QB, 1), F32)
        dq_acc, _, _ = _sb_passes(i, step, (jnp.zeros((2 * QB, HD), F32), zc, zc))
        dq_ref[...] = jnp.where(first, dq_acc[:QB], dq_acc[QB:]) * SB_SCALE
        end_cargo()

    full_in = lambda off: pl.BlockSpec((t_pad, HD), lambda pr, i: (0, off + pr))
    full_out = pl.BlockSpec((t_pad, HD), lambda pr, i: (0, pr))
    blk = pl.BlockSpec((QB, HD), lambda pr, i: (i, pr))
    sds = jax.ShapeDtypeStruct((t_pad, SB_H * SB_DH), F32)
    return pl.pallas_call(
        body, name=name, grid=(SB_H // 2, nq),
        in_specs=[pl.BlockSpec((QB, HD), lambda pr, i: (i, SB_Q0 + pr)), full_in(SB_K0), full_in(SB_V0), blk,
                  pl.BlockSpec((QB, HD), lambda pr, i: (i, dy_blk0 + pr))] + [ANY] * n,
        out_specs=[blk, full_out, full_out] + [ANY] * n,
        out_shape=[sds, sds, sds] + (exchange[1](cargo) if n else []),
        scratch_shapes=exchange[2](n) if n else [],
        compiler_params=_cparams(),
    )(p, p, p, rtot, dy, *cargo)


HG_LEVELS = 6


def _hg_prefix_matrix():
    t = np.arange(CH)[:, None]
    j = np.arange(CH)[None, :]
    groups = [(j <= t)]
    for lvl in range(1, HG_LEVELS + 1):
        half = CH >> lvl
        e = (t // (2 * half)) * (2 * half) + half - 1
        groups.append(j <= e)
    groups.append(np.ones((8, CH), bool))
    e = np.concatenate(groups, axis=0).astype(np.float32)
    return np.concatenate([e, e, e], axis=1), np.concatenate([e, e, e], axis=0)


HG_G = 4


def _hg_chunk(qr, fr, iv, r0, r1, st, valid, ecat):
    g = st.shape[0]
    mx = jnp.maximum(r0, r1)
    e0 = jnp.exp(r0 - mx)
    e1 = jnp.exp(r1 - mx)
    lb = e1 / (e0 + e1)
    fg = lb + (1.0 - lb) * _sigmoid(fr)
    logf = jnp.where(valid, jnp.log(fg), 0.0)
    kk = jnp.where(valid, 1.0 - fg, 0.0)
    q = jnp.where(valid, _silu(qr), 0.0)
    v = _heads(jnp.where(valid, iv, 0.0), g)

    pre = _mask_dot(ecat, logf)
    b = pre[0:CH]
    b_last = jnp.max(pre[(HG_LEVELS + 1) * CH:], axis=0, keepdims=True)
    row = _iota((CH, 1), 0)
    ri = _iota((1, CH, CH), 1)
    ci = _iota((1, CH, CH), 2)
    a = jnp.where(ri == ci, jnp.sum(_heads(q * kk, g), axis=2, keepdims=True), 0.0)
    for lvl in range(1, HG_LEVELS + 1):
        half = CH >> lvl
        m = pre[lvl * CH:(lvl + 1) * CH]
        low = (row & half) != 0
        dec = jnp.exp(jnp.where(low, b - m, m - b))
        qt = jnp.where(low, q * dec, 0.0)
        kt = jnp.where(low, 0.0, kk * dec)
        same = (ri >> (7 - lvl)) == (ci >> (7 - lvl))
        a = a + jnp.where(same, bbnt(_heads(qt, g), _heads(kt, g)), 0.0)
    o = bbnt(_heads(q * jnp.exp(b), g), st) + bbnn(a, v)
    kd = kk * jnp.exp(b_last - b)
    st_new = st * _heads(jnp.exp(b_last), g) + bbtn(v, _heads(kd, g))
    return o, st_new


def _hg_specs(npair, rev):
    cc = (lambda c: npair - 1 - c) if rev else (lambda c: c)
    ng = HG_H // HG_G
    blk = lambda off: pl.BlockSpec((PAIR, HG_G * HD), lambda h, c: (cc(c), off * ng + h))
    lbs = pl.BlockSpec((2, HG_G * HD), lambda h, c: (0, h))
    state = pl.BlockSpec((1, HG_G, HD, HD), lambda h, c: (cc(c), h, 0, 0))
    return ng, blk, lbs, state


def _hg_fwd(p, lbraw, ecat, pad, name):
    t_pad = p.shape[0]
    npair = t_pad // PAIR
    ng, blk, lbs, state = _hg_specs(npair, False)

    def body(q_ref, f_ref, i_ref, lb_ref, e_ref, et_ref, o_ref, ss_ref, s_ref):
        c = pl.program_id(1)

        @pl.when(c == 0)
        def _():
            s_ref[...] = jnp.zeros_like(s_ref)

        st = s_ref[...]
        ss_ref[0] = st
        for half in (0, 1):
            r = _chunk_rows(half)
            o, st = _hg_chunk(q_ref[r, :], f_ref[r, :], i_ref[r, :], lb_ref[0:1, :], lb_ref[1:2, :], st,
                              _chunk_valid(c, half, pad), (e_ref[...], et_ref[...]))
            _store_heads(o_ref, o, r)
        s_ref[...] = st

    return pl.pallas_call(
        body, name=name, grid=(ng, npair),
        in_specs=[blk(0), blk(1), blk(2), lbs] + [pl.BlockSpec(e.shape, lambda h, c: (0, 0)) for e in ecat],
        out_specs=[blk(0), state],
        out_shape=[jax.ShapeDtypeStruct((t_pad, HG_H * HD), F32), jax.ShapeDtypeStruct((npair, HG_H, HD, HD), F32)],
        scratch_shapes=[pltpu.VMEM((HG_G, HD, HD), F32)],
        compiler_params=_cparams(),
    )(p, p, p, lbraw, *ecat)


def _hg_bwd(p, lbraw, ecat, ssave, do, pad, name, cargo=(), exchange=None):
    t_pad = p.shape[0]
    npair = t_pad // PAIR
    ng, blk, lbs, state = _hg_specs(npair, True)
    n = len(cargo)

    def body(q_ref, f_ref, i_ref, lb_ref, e_ref, et_ref, ss_ref, do_ref, *rest):
        c = pl.program_id(1)
        hg = pl.program_id(0)
        ds_ref = rest[-1]
        (dq_ref, df_ref, di_ref, dlb_ref), end_cargo = _cargo_bounds(
            rest[:-1], n, 4, exchange, (hg == 0) & (c == 0), (hg == ng - 1) & (c == npair - 1))

        @pl.when(c == 0)
        def _():
            ds_ref[...] = jnp.zeros_like(ds_ref)
            dlb_ref[...] = jnp.zeros_like(dlb_ref)

        ra, rb = _chunk_rows(0), _chunk_rows(1)
        va, vb = _chunk_valid(npair - 1 - c, 0, pad), _chunk_valid(npair - 1 - c, 1, pad)
        ecv = (e_ref[...], et_ref[...])

        def pair(qa, fa, ia, qb, fb, ib, r0, r1, st):
            oa, st = _hg_chunk(qa, fa, ia, r0, r1, st, va, ecv)
            ob, st = _hg_chunk(qb, fb, ib, r0, r1, st, vb, ecv)
            return oa, ob, st

        ins = [ref[r, :] for r in (ra, rb) for ref in (q_ref, f_ref, i_ref)]
        _, vjp = jax.vjp(pair, *ins, lb_ref[0:1, :], lb_ref[1:2, :], ss_ref[0])
        g = vjp((_heads(do_ref[ra, :], HG_G), _heads(do_ref[rb, :], HG_G), ds_ref[...]))
        for r, (dq, df, di) in ((ra, g[0:3]), (rb, g[3:6])):
            dq_ref[r, :] = dq.astype(BF16)
            df_ref[r, :] = df.astype(BF16)
            di_ref[r, :] = di.astype(BF16)
        dlb_ref[0:1, :] += g[6]
        dlb_ref[1:2, :] += g[7]
        ds_ref[...] = g[8]
        end_cargo()

    sds = jax.ShapeDtypeStruct((t_pad, HG_H * HD), BF16)
    return pl.pallas_call(
        body, name=name, grid=(ng, npair),
        in_specs=[blk(0), blk(1), blk(2), lbs] + [pl.BlockSpec(e.shape, lambda h, c: (0, 0)) for e in ecat]
        + [state, blk(0)] + [ANY] * n,
        out_specs=[blk(0), blk(0), blk(0), lbs] + [ANY] * n,
        out_shape=[sds, sds, sds, jax.ShapeDtypeStruct((2, HG_H * HD), F32)] + (exchange[1](cargo) if n else []),
        scratch_shapes=(exchange[2](n) if n else []) + [pltpu.VMEM((HG_G, HD, HD), F32)],
        compiler_params=_cparams(),
    )(p, p, p, lbraw, *ecat, ssave, do, *cargo)


def _pad_ab_cols(w):
    z = jnp.zeros((w.shape[0], HD - GDN_H), w.dtype)
    return jnp.concatenate([w[:, :2048], w[:, 2048:2052], z, w[:, 2052:2056], z, w[:, 2056:]], axis=1)


def _unpad_ab_cols(w):
    return jnp.concatenate([w[:, :2048], w[:, 2048:2052], w[:, 2176:2180], w[:, 2304:]], axis=1)


def _lane_pad(v):
    return jnp.pad(v, ((0, 0), (0, HD - v.shape[1])))


def _mlp_fwd(hb, w1, w2, layer):
    a, r = _mm(hb, w1, b_view=("cols", layer), out_dtype=BF16, act=True, name=f"mlp_up_{layer}")
    m = _mm(r, w2, b_view=("rows", layer), name=f"mlp_down_{layer}")
    return a, r, m


def _mlp_bwd(hb, a, r, dmb, w1, w2, layer):
    da = _mm(dmb, w2, tb=True, b_view=("rows", layer), out_dtype=BF16, gate=a, name=f"mlp_down_dx_{layer}")
    dw2 = _mm(r, dmb, ta=True, out_dtype=BF16, name=f"mlp_down_dw_{layer}")
    dh = _mm(da, w1, tb=True, b_view=("cols", layer), name=f"mlp_up_dx_{layer}")
    dw1 = _mm(hb, da, ta=True, out_dtype=BF16, out_split=N_CHIP, name=f"mlp_up_dw_{layer}")
    return dh, dw1, dw2


def _local_step(h0, tgt, w, pad, late=None):
    row = lambda a, i: a[i:i + 1]
    ecat = tuple(jnp.asarray(e, dtype=BF16) for e in _hg_prefix_matrix())
    cw = [w["conv_w"][:, i * 512:(i + 1) * 512] for i in range(3)]
    alog, dtb = _lane_pad(w["a_log"]), _lane_pad(w["dt_bias"])

    h0b = h0.astype(BF16)
    p0 = _mm(h0b, w["ab_w_in"], name="ab_in")
    qn = _conv_fwd(p0, 0, cw[0], "q", pad, "conv_q")
    kn = _conv_fwd(p0, 4, cw[1], "k", pad, "conv_k")
    vn = _conv_fwd(p0, 8, cw[2], "v", pad, "conv_v")
    if late is None:
        oa_raw, ss0 = _gdn_fwd(qn, kn, vn, p0, alog, dtb, pad, "gdn_fwd")
        ob, rtot = _sb_fwd(p0, pad, "sb_fwd")
    else:
        oa_raw, ss0, g_cin, g_cout = _gdn_fwd(qn, kn, vn, p0, alog, dtb, pad, "gdn_fwd",
                                              cargo=[late["c_w_in"], late["c_w_out"]], exchange=GATHER)
        ob, rtot, g_about, g_w1, g_w2 = _sb_fwd(p0, pad, "sb_fwd", exchange=GATHER,
                                                cargo=[late["ab_w_out"], late["mlp_w1"], late["mlp_w2"]])
        w = dict(w, ab_w_out=g_about.reshape(D, D), c_w_in=g_cin, c_w_out=g_cout.reshape(D, D), mlp_w1=g_w1, mlp_w2=g_w2)
    oa = _grms_fwd(oa_raw, p0, 12, w["ab_gnorm_g"], "gdn_gate")
    ycat = jnp.concatenate([oa, ob], axis=1)
    mix0 = _mm(ycat, w["ab_w_out"], name="ab_out")
    h1, h1b = _ln_res_fwd(h0, mix0, row(w["ln_mix_g"], 0), row(w["ln_mix_b"], 0), "ln_mix_0")
    a0, r0, m0 = _mlp_fwd(h1b, w["mlp_w1"], w["mlp_w2"], 0)
    h2, h2b = _ln_res_fwd(h1, m0, row(w["ln_ffn_g"], 0), row(w["ln_ffn_b"], 0), "ln_ffn_0")
    p1 = _mm(h2b, w["c_w_in"], b_view=("cols", 0), name="c_in")
    oc_raw, ss1 = _hg_fwd(p1, w["c_lb_raw"], ecat, pad, "hg_fwd")
    yc = _grms_fwd(oc_raw, p1, 3 * HG_H, w["c_gnorm_g"], "hg_gate")
    mix1 = _mm(yc, w["c_w_out"], name="c_out")
    h3, h3b = _ln_res_fwd(h2, mix1, row(w["ln_mix_g"], 1), row(w["ln_mix_b"], 1), "ln_mix_1")
    a1, r1, m1 = _mlp_fwd(h3b, w["mlp_w1"], w["mlp_w2"], 1)
    h4, _ = _ln_res_fwd(h3, m1, row(w["ln_ffn_g"], 1), row(w["ln_ffn_b"], 1), "ln_ffn_1")
    loss, dh4 = _loss_fwd(h4, tgt, pad + N_META, "loss")

    dh3a, dm1b, dfg1, dfb1 = _ln_res_bwd(h3, m1, row(w["ln_ffn_g"], 1), row(w["ln_ffn_b"], 1), [dh4], "ln_ffn_bwd_1")
    dh3b, dw1_1, dw2_1 = _mlp_bwd(h3b, a1, r1, dm1b, w["mlp_w1"], w["mlp_w2"], 1)
    dh2a, dmix1b, dmg1, dmb1 = _ln_res_bwd(h2, mix1, row(w["ln_mix_g"], 1), row(w["ln_mix_b"], 1), [dh3a, dh3b], "ln_mix_bwd_1")
    dyc = _mm(dmix1b, w["c_w_out"], tb=True, name="c_out_dx")
    dwco = _mm(yc, dmix1b, ta=True, out_dtype=BF16, name="c_out_dw")
    doc, dzc, dcg = _grms_bwd(oc_raw, p1, 3 * HG_H, w["c_gnorm_g"], dyc, 0, "hg_gate_bwd")
    landed = {}
    rows4 = lambda a: a.reshape(N_CHIP, -1, D)
    if late is None:
        dq1, df1, di1, dlb = _hg_bwd(p1, w["c_lb_raw"], ecat, ss1, doc, pad, "hg_bwd")
    else:
        dq1, df1, di1, dlb, landed["w1_1"] = _hg_bwd(
            p1, w["c_lb_raw"], ecat, ss1, doc, pad, "hg_bwd", cargo=[dw1_1], exchange=SCATTER)
    dp1 = [dq1, df1, di1, dzc]
    dh2b = _mm_groups_nt(dp1, w["c_w_in"], "c_in_dx")
    dwc = jnp.stack([_mm(h2b, d, ta=True, out_dtype=BF16, name=f"c_in_dw_{i}") for i, d in enumerate(dp1)])
    dh1a, dm0b, dfg0, dfb0 = _ln_res_bwd(h1, m0, row(w["ln_ffn_g"], 0), row(w["ln_ffn_b"], 0), [dh2a, dh2b], "ln_ffn_bwd_0")
    dh1b, dw1_0, dw2_0 = _mlp_bwd(h1b, a0, r0, dm0b, w["mlp_w1"], w["mlp_w2"], 0)
    dh0a, dmix0b, dmg0, dmb0 = _ln_res_bwd(h0, mix0, row(w["ln_mix_g"], 0), row(w["ln_mix_b"], 0), [dh1a, dh1b], "ln_mix_bwd_0")
    dycat = _mm(dmix0b, w["ab_w_out"], tb=True, name="ab_out_dx")
    dwabo = _mm(ycat, dmix0b, ta=True, out_dtype=BF16, name="ab_out_dw")
    doa, dza, dag = _grms_bwd(oa_raw, p0, 12, w["ab_gnorm_g"], dycat, 0, "gdn_gate_bwd")
    if late is None:
        dqn, dkn, dvn, dbb, daa, dal, ddt = _gdn_bwd(qn, kn, vn, p0, alog, dtb, ss0, doa, pad, "gdn_bwd")
        dqb, dkb, dvb = _sb_bwd(p0, rtot, dycat, 4, pad, "sb_bwd")
    else:
        dqn, dkn, dvn, dbb, daa, dal, ddt, landed["c_w_in"] = _gdn_bwd(
            qn, kn, vn, p0, alog, dtb, ss0, doa, pad, "gdn_bwd", cargo=[dwc], exchange=SCATTER)
        (dqb, dkb, dvb, landed["w1_0"], landed["w2_0"], landed["w2_1"], landed["ab_w_out"],
         landed["c_w_out"]) = _sb_bwd(
            p0, rtot, dycat, 4, pad, "sb_bwd",
            cargo=[dw1_0, rows4(dw2_0), rows4(dw2_1), rows4(dwabo), rows4(dwco)], exchange=SCATTER)
    dpq, dcq = _conv_bwd(p0, 0, cw[0], dqn, "q", pad, "conv_q_bwd")
    dpk, dck = _conv_bwd(p0, 4, cw[1], dkn, "k", pad, "conv_k_bwd")
    dpv, dcv = _conv_bwd(p0, 8, cw[2], dvn, "v", pad, "conv_v_bwd")
    dp0 = _assemble_bf16([(dpq, "cols"), (dpk, "cols"), (dpv, "cols"), (dza, "cols"), (dbb, "cols"), (daa, "cols"),
                          (dqb, "cols"), (dkb, "cols"), (dvb, "cols")], "ab_in_dy")
    dwab = _mm(h0b, dp0, ta=True, out_dtype=BF16, name="ab_in_dw")
    if late is None:
        dh0 = _mm(dp0, w["ab_w_in"], tb=True, plus=dh0a, name="ab_in_dx")
    else:
        dab = jnp.transpose(_unpad_ab_cols(dwab).reshape(D, N_CHIP, AB_TRUE // N_CHIP), (1, 0, 2))
        dh0, landed["ab_w_in"] = _mm(dp0, w["ab_w_in"], tb=True, plus=dh0a, name="ab_in_dx", cargo=[dab],
                                     exchange=SCATTER)

    grads = {
        "ab_w_in": dwab, "conv_w": jnp.concatenate([dcq, dck, dcv], axis=1),
        "a_log": dal[:, :GDN_H], "dt_bias": ddt[:, :GDN_H],
        "ab_gnorm_g": dag, "ab_w_out": dwabo, "c_w_in": dwc, "c_lb_raw": dlb, "c_gnorm_g": dcg, "c_w_out": dwco,
        "ln_mix_g": jnp.concatenate([dmg0, dmg1], 0), "ln_mix_b": jnp.concatenate([dmb0, dmb1], 0),
        "w1_0": dw1_0, "w1_1": dw1_1, "w2_0": dw2_0, "w2_1": dw2_1,
        "ln_ffn_g": jnp.concatenate([dfg0, dfg1], 0), "ln_ffn_b": jnp.concatenate([dfb0, dfb1], 0),
        "landed": landed,
    }
    return loss, dh0, grads


MESH = pl.DeviceIdType.MESH
ANY = pl.BlockSpec(memory_space=pl.ANY)
N_CHIP = 4
N_DEV = 8
CHIP_REL = ((1, 0), (0, 1), (1, 1))
DEV_REL = tuple((dx, dy, dc) for dx in (0, 1) for dy in (0, 1) for dc in (0, 1))[1:]

def _pos():
    return lax.axis_index("x"), lax.axis_index("y"), lax.axis_index("c")


def _flip(a, d):
    return a + d - 2 * a * d


class _Exchange:
    def __init__(self, local, sends, recvs):
        self.local, self.sends, self.recvs = local, sends, recvs

    def start(self):
        for cp in self.local + self.sends:
            cp.start()

    def wait(self):
        for cp in self.recvs:
            cp.wait_recv()
        for cp in self.sends:
            cp.wait_send()
        for cp in self.local:
            cp.wait()


def _gather_sems(n):
    return [pltpu.SemaphoreType.DMA((3 * n,)), pltpu.SemaphoreType.DMA((3 * n,)), pltpu.SemaphoreType.DMA((n,))]


def _gather_copies(x_refs, o_refs, send_sems, recv_sems, local_sems):
    n = len(x_refs)
    x, y, c = _pos()
    local = [pltpu.make_async_copy(x_refs[a], o_refs[a].at[2 * x + y], local_sems.at[a]) for a in range(n)]

    def copy(a, k, sending):
        tx, ty = _flip(x, CHIP_REL[k][0]), _flip(y, CHIP_REL[k][1])
        return pltpu.make_async_remote_copy(
            src_ref=x_refs[a], dst_ref=o_refs[a].at[2 * x + y if sending else 2 * tx + ty],
            send_sem=send_sems.at[3 * a + k], recv_sem=recv_sems.at[3 * a + k], device_id=(tx, ty, c), device_id_type=MESH)

    pairs = [(a, k) for a in range(n) for k in range(3)]
    return _Exchange(local, [copy(a, k, True) for a, k in pairs], [copy(a, k, False) for a, k in pairs])


def _gather_shapes(bufs):
    return [jax.ShapeDtypeStruct((N_CHIP,) + b.shape, b.dtype) for b in bufs]


def _chip_allgather(bufs, name):
    n = len(bufs)

    def body(*refs):
        ex = _gather_copies(refs[:n], refs[n:2 * n], *refs[2 * n:])
        ex.start()
        ex.wait()

    return pl.pallas_call(
        body, name=name, in_specs=[ANY] * n, out_specs=[ANY] * n, out_shape=_gather_shapes(bufs),
        scratch_shapes=_gather_sems(n), compiler_params=pltpu.CompilerParams(has_side_effects=True),
    )(*bufs)


def _scatter_sems(n):
    nr = N_DEV - 1
    return [pltpu.SemaphoreType.DMA((nr * n,)), pltpu.SemaphoreType.DMA((nr * n,)), pltpu.SemaphoreType.DMA((n,))]


def _scatter_copies(g_refs, o_refs, send_sems, recv_sems, local_sems):
    n = len(g_refs)
    nr = N_DEV - 1
    x, y, c = _pos()
    me = 4 * x + 2 * y + c
    local = [pltpu.make_async_copy(g_refs[a].at[2 * x + y], o_refs[a].at[me], local_sems.at[a]) for a in range(n)]

    def copy(a, k, sending):
        dx, dy, dc = DEV_REL[k]
        tx, ty, tc = _flip(x, dx), _flip(y, dy), _flip(c, dc)
        return pltpu.make_async_remote_copy(
            src_ref=g_refs[a].at[2 * tx + ty], dst_ref=o_refs[a].at[me if sending else 4 * tx + 2 * ty + tc],
            send_sem=send_sems.at[nr * a + k], recv_sem=recv_sems.at[nr * a + k],
            device_id=(tx, ty, tc), device_id_type=MESH)

    pairs = [(a, k) for a in range(n) for k in range(nr)]
    return _Exchange(local, [copy(a, k, True) for a, k in pairs], [copy(a, k, False) for a, k in pairs])


def _scatter_shapes(gs):
    return [jax.ShapeDtypeStruct((N_DEV,) + g.shape[1:], g.dtype) for g in gs]


GATHER = (_gather_copies, _gather_shapes, _gather_sems)
SCATTER = (_scatter_copies, _scatter_shapes, _scatter_sems)


def _sum_slots(r, name):
    n, rh, w = r.shape
    tr = _pick(rh, (256, 128, 64, 16))

    def body(r_ref, o_ref):
        acc = r_ref[0].astype(F32)
        for s in range(1, n):
            acc = acc + r_ref[s].astype(F32)
        o_ref[...] = acc

    return pl.pallas_call(
        body, name=name, grid=(rh // tr,), in_specs=[pl.BlockSpec((n, tr, w), lambda i: (0, i, 0))],
        out_specs=pl.BlockSpec((tr, w), lambda i: (i, 0)), out_shape=jax.ShapeDtypeStruct((rh, w), F32),
        compiler_params=_cparams(),
    )(r)


def _small_allreduce(buf, name):
    r, w = buf.shape

    def body(b_ref, o_ref, land_ref, send_sems, recv_sems):
        x, y, c = _pos()
        me = 4 * x + 2 * y + c
        land_ref[me] = b_ref[...]

        def target(k):
            dx, dy, dc = DEV_REL[k]
            return _flip(x, dx), _flip(y, dy), _flip(c, dc)

        sends = []
        for k in range(N_DEV - 1):
            tx, ty, tc = target(k)
            cp = pltpu.make_async_remote_copy(
                src_ref=b_ref, dst_ref=land_ref.at[me], send_sem=send_sems.at[k], recv_sem=recv_sems.at[k],
                device_id=(tx, ty, tc), device_id_type=MESH)
            cp.start()
            sends.append(cp)
        for k in range(N_DEV - 1):
            tx, ty, tc = target(k)
            pltpu.make_async_remote_copy(
                src_ref=b_ref, dst_ref=land_ref.at[4 * tx + 2 * ty + tc], send_sem=send_sems.at[k],
                recv_sem=recv_sems.at[k], device_id=(tx, ty, tc), device_id_type=MESH).wait_recv()
        for cp in sends:
            cp.wait_send()
        acc = land_ref[0]
        for s in range(1, N_DEV):
            acc = acc + land_ref[s]
        o_ref[...] = acc

    vm = pl.BlockSpec(memory_space=pltpu.VMEM)
    return pl.pallas_call(
        body, name=name, in_specs=[vm], out_specs=vm, out_shape=jax.ShapeDtypeStruct((r, w), F32),
        scratch_shapes=[pltpu.VMEM((N_DEV, r, w), F32), pltpu.SemaphoreType.DMA((N_DEV - 1,)),
                        pltpu.SemaphoreType.DMA((N_DEV - 1,))],
        compiler_params=pltpu.CompilerParams(has_side_effects=True),
    )(buf)


def _adamw(w, g, m, v, name):
    r, c = w.shape
    tr = _pick(r, (256, 128, 64, 8)) if r * c > (1 << 18) else r

    def body(w_ref, g_ref, m_ref, v_ref, d_ref, m2_ref, v2_ref):
        gg = g_ref[...]
        m2 = ADAM_B1 * m_ref[...] + (1.0 - ADAM_B1) * gg
        v2 = ADAM_B2 * v_ref[...] + (1.0 - ADAM_B2) * (gg * gg)
        m_hat = m2 / (1.0 - ADAM_B1 ** ADAM_STEP)
        v_hat = v2 / (1.0 - ADAM_B2 ** ADAM_STEP)
        d_ref[...] = -ADAM_LR * (m_hat / (jnp.sqrt(v_hat) + ADAM_EPS) + ADAM_WD * w_ref[...])
        m2_ref[...] = m2
        v2_ref[...] = v2

    blk = pl.BlockSpec((tr, c), lambda i: (i, 0))
    sds = jax.ShapeDtypeStruct((r, c), F32)
    return pl.pallas_call(body, name=name, grid=(r // tr,), in_specs=[blk] * 4, out_specs=[blk] * 3,
                          out_shape=[sds] * 3, compiler_params=_cparams())(w, g, m, v)


BIG = ("ab_w_in", "ab_w_out", "c_w_in", "c_w_out", "mlp_w1", "mlp_w2")
SMALL = ("ln_mix_g", "ln_mix_b", "ln_ffn_g", "ln_ffn_b", "c_lb_raw", "ab_a_log", "ab_dt_bias", "ab_gnorm_g", "c_gnorm_g")
SMALL_ROWS = 16
CONV_ROWS = 8
CONV_W = 3 * GDN_H * HD


def _conv_to_rows(cw):
    return jnp.pad(cw, ((0, 0), (0, 2 * D - CONV_W))).reshape(CONV_ROWS, D)


def _rows_to_conv(rows):
    return rows.reshape(CONV_K, 2 * D)[:, :CONV_W]


def _pack_small(d):
    rows = [jnp.pad(d[n], ((0, 0), (0, D - d[n].shape[1]))) for n in SMALL]
    buf = jnp.concatenate(rows, axis=0)
    return jnp.pad(buf, ((0, SMALL_ROWS - buf.shape[0]), (0, 0)))


def _unpack_small(buf, like):
    out, r = {}, 0
    for n in SMALL:
        nr, nc = like[n].shape
        out[n] = buf[r:r + nr, :nc]
        r += nr
    return out


def kernel(x, meta_tokens, ab_w_in, ab_conv_w, ab_a_log, ab_dt_bias, ab_gnorm_g, ab_w_out, c_w_in, c_lb_raw, c_gnorm_g, c_w_out, ln_mix_g, ln_mix_b, mlp_w1, mlp_w2, ln_ffn_g, ln_ffn_b, loss_target, m_meta_tokens, m_ab_w_in, m_ab_conv_w, m_ab_a_log, m_ab_dt_bias, m_ab_gnorm_g, m_ab_w_out, m_c_w_in, m_c_lb_raw, m_c_gnorm_g, m_c_w_out, m_ln_mix_g, m_ln_mix_b, m_mlp_w1, m_mlp_w2, m_ln_ffn_g, m_ln_ffn_b, v_meta_tokens, v_ab_w_in, v_ab_conv_w, v_ab_a_log, v_ab_dt_bias, v_ab_gnorm_g, v_ab_w_out, v_c_w_in, v_c_lb_raw, v_c_gnorm_g, v_c_w_out, v_ln_mix_g, v_ln_mix_b, v_mlp_w1, v_mlp_w2, v_ln_ffn_g, v_ln_ffn_b):
    names = ("meta_tokens", "ab_w_in", "ab_conv_w", "ab_a_log", "ab_dt_bias", "ab_gnorm_g", "ab_w_out", "c_w_in",
             "c_lb_raw", "c_gnorm_g", "c_w_out", "ln_mix_g", "ln_mix_b", "mlp_w1", "mlp_w2", "ln_ffn_g", "ln_ffn_b")
    wts = dict(zip(names, (meta_tokens, ab_w_in, ab_conv_w, ab_a_log, ab_dt_bias, ab_gnorm_g, ab_w_out, c_w_in, c_lb_raw,
                           c_gnorm_g, c_w_out, ln_mix_g, ln_mix_b, mlp_w1, mlp_w2, ln_ffn_g, ln_ffn_b)))
    mom_m = dict(zip(names, (m_meta_tokens, m_ab_w_in, m_ab_conv_w, m_ab_a_log, m_ab_dt_bias, m_ab_gnorm_g, m_ab_w_out,
                             m_c_w_in, m_c_lb_raw, m_c_gnorm_g, m_c_w_out, m_ln_mix_g, m_ln_mix_b, m_mlp_w1, m_mlp_w2,
                             m_ln_ffn_g, m_ln_ffn_b)))
    mom_v = dict(zip(names, (v_meta_tokens, v_ab_w_in, v_ab_conv_w, v_ab_a_log, v_ab_dt_bias, v_ab_gnorm_g, v_ab_w_out,
                             v_c_w_in, v_c_lb_raw, v_c_gnorm_g, v_c_w_out, v_ln_mix_g, v_ln_mix_b, v_mlp_w1, v_mlp_w2,
                             v_ln_ffn_g, v_ln_ffn_b)))
    seq = x.shape[1]
    pad = (-(N_META + seq)) % QB
    xi, yi, ci = _pos()
    chip = 2 * xi + yi

    gat_ab_in, = _chip_allgather([ab_w_in[0].astype(BF16)], "gather_weights")
    late = {"ab_w_out": ab_w_out[0].astype(BF16), "c_w_in": c_w_in.astype(BF16), "c_w_out": c_w_out[0].astype(BF16),
            "mlp_w1": mlp_w1.astype(BF16), "mlp_w2": mlp_w2.astype(BF16)}
    mcols, ccols = meta_tokens.shape[1], ab_conv_w.shape[2]
    place = jnp.concatenate([
        lax.dynamic_update_slice(jnp.zeros((N_META, D), F32), 0.5 * meta_tokens, (0, chip * mcols)),
        _conv_to_rows(lax.dynamic_update_slice(jnp.zeros((CONV_K, CONV_W), F32), 0.5 * ab_conv_w[0], (0, chip * ccols)))],
        axis=0)
    placed = _small_allreduce(place, "gather_meta")
    meta_full = placed[:N_META]

    w = {
        "ab_w_in": _pad_ab_cols(jnp.transpose(gat_ab_in, (1, 0, 2)).reshape(D, AB_TRUE)),
        "conv_w": _rows_to_conv(placed[N_META:]), "a_log": ab_a_log, "dt_bias": ab_dt_bias,
        "ab_gnorm_g": ab_gnorm_g, "c_lb_raw": c_lb_raw,
        "c_gnorm_g": c_gnorm_g, "ln_mix_g": ln_mix_g, "ln_mix_b": ln_mix_b, "ln_ffn_g": ln_ffn_g, "ln_ffn_b": ln_ffn_b,
    }

    h0 = jnp.concatenate([jnp.zeros((pad, D), F32), meta_full, x[0]], axis=0)
    tgt = jnp.concatenate([jnp.zeros((pad + N_META, D), F32), loss_target[0]], axis=0)
    loss8, dh0, g = _local_step(h0, tgt, w, pad, late)
    loss = lax.psum(loss8[0, 0], ("x", "y", "c"))
    grad_x = dh0[pad + N_META:][None]

    gsmall = {"ln_mix_g": g["ln_mix_g"], "ln_mix_b": g["ln_mix_b"], "ln_ffn_g": g["ln_ffn_g"], "ln_ffn_b": g["ln_ffn_b"],
              "c_lb_raw": g["c_lb_raw"], "ab_a_log": g["a_log"], "ab_dt_bias": g["dt_bias"], "ab_gnorm_g": g["ab_gnorm_g"],
              "c_gnorm_g": g["c_gnorm_g"]}
    sbuf = jnp.concatenate([_pack_small(gsmall), dh0[pad:pad + N_META], _conv_to_rows(g["conv_w"])], axis=0)
    ssum = _small_allreduce(sbuf, "allreduce_small")
    grads = _unpack_small(ssum[:SMALL_ROWS], wts)
    grads["meta_tokens"] = lax.dynamic_slice(ssum[SMALL_ROWS:SMALL_ROWS + N_META], (0, chip * mcols), (N_META, mcols))
    grads["ab_conv_w"] = lax.dynamic_slice(_rows_to_conv(ssum[SMALL_ROWS + N_META:]), (0, chip * ccols), (CONV_K, ccols))[None]

    sums = {k: _sum_slots(v, f"grad_sum_{k}") for k, v in g["landed"].items()}
    for n in ("ab_w_in", "ab_w_out", "c_w_in", "c_w_out"):
        grads[n] = sums[n][None]
    grads["mlp_w1"] = jnp.stack([sums["w1_0"], sums["w1_1"]])
    grads["mlp_w2"] = jnp.stack([sums["w2_0"], sums["w2_1"]])

    delta, new_m, new_v = {}, {}, {}
    for n in ("meta_tokens", "ab_conv_w") + BIG:
        shp = wts[n].shape
        to2 = lambda a: a.reshape(-1, shp[-1])
        d2, m2, v2 = _adamw(to2(wts[n]), to2(grads[n]), to2(mom_m[n]), to2(mom_v[n]), f"adamw_{n}")
        delta[n], new_m[n], new_v[n] = d2.reshape(shp), m2.reshape(shp), v2.reshape(shp)
    d2, m2, v2 = _adamw(_pack_small(wts), ssum[:SMALL_ROWS], _pack_small(mom_m), _pack_small(mom_v), "adamw_small")
    delta.update(_unpack_small(d2, wts))
    new_m.update(_unpack_small(m2, wts))
    new_v.update(_unpack_small(v2, wts))

    return (loss, grad_x, *[grads[n] for n in names], *[delta[n] for n in names], *[new_m[n] for n in names],
            *[new_v[n] for n in names])
```

```python
import functools

import numpy as np
import jax
import jax.numpy as jnp
from jax import lax
from jax.experimental import pallas as pl
from jax.experimental.pallas import tpu as pltpu

F32 = jnp.float32
BF16 = jnp.bfloat16

D = 1024
N_META = 16
DEPTH = 2
GDN_H = 4
SB_H = 8
SB_DH = 64
HG_H = 8
HD = 128
CH = 64
QB = 128
ALPHA = float((2 * DEPTH) ** 0.25)
LN_EPS = 1e-5
RMS_EPS = 1e-6
L2_EPS = 1e-6
NEG = -1e30

ADAM_LR = 0.001
ADAM_B1 = 0.9
ADAM_B2 = 0.999
ADAM_EPS = 1e-08
ADAM_WD = 0.01
ADAM_STEP = 10

AB_TRUE = 3592
V7X_VMEM_BYTES = 64 * 1024 * 1024
VMEM_LIMIT = V7X_VMEM_BYTES - 8 * 1024 * 1024

NN = ((1,), (0,))
NT = ((1,), (1,))
TN = ((0,), (0,))


def _cparams(**kw):
    return pltpu.CompilerParams(vmem_limit_bytes=VMEM_LIMIT, **kw)


def _dg(a, b, dims, mode):
    if mode == "h":
        return lax.dot_general(a, b, dims, precision=lax.Precision.HIGHEST, preferred_element_type=F32)
    if mode == "b":
        return lax.dot_general(a.astype(BF16), b.astype(BF16), dims, preferred_element_type=F32)
    ah, bh = a.astype(BF16), b.astype(BF16)
    al, bl = (a - ah.astype(F32)).astype(BF16), (b - bh.astype(F32)).astype(BF16)
    d = lambda x, y: lax.dot_general(x, y, dims, preferred_element_type=F32)
    return d(ah, bh) + (d(ah, bl) + d(al, bh))


def _make_dots(mode, batched=False):
    if batched:
        nn_d, nt_d, tn_d = (((2,), (1,)), ((0,), (0,))), (((2,), (2,)), ((0,), (0,))), (((1,), (1,)), ((0,), (0,)))
    else:
        nn_d, nt_d, tn_d = (NN, ((), ())), (NT, ((), ())), (TN, ((), ()))

    @jax.custom_vjp
    def nn(a, b):
        return _dg(a, b, nn_d, mode)

    @jax.custom_vjp
    def nt(a, b):
        return _dg(a, b, nt_d, mode)

    @jax.custom_vjp
    def tn(a, b):
        return _dg(a, b, tn_d, mode)

    nn.defvjp(lambda a, b: (nn(a, b), (a, b)), lambda r, g: (nt(g, r[1]), tn(r[0], g)))
    nt.defvjp(lambda a, b: (nt(a, b), (a, b)), lambda r, g: (nn(g, r[1]), tn(g, r[0])))
    tn.defvjp(lambda a, b: (tn(a, b), (a, b)), lambda r, g: (nt(r[1], g), nn(r[0], g)))
    return nn, nt, tn


hnn = _make_dots("h")[0]
bbnn, bbnt, bbtn = _make_dots("b", True)
mbnn, mbnt, mbtn = _make_dots("m", True)
hbnt = _make_dots("h", True)[1]


def _split3(x, axis):
    x1 = x.astype(BF16)
    r1 = x - x1.astype(F32)
    x2 = r1.astype(BF16)
    x3 = (r1 - x2.astype(F32)).astype(BF16)
    return jnp.concatenate([x1, x2, x3], axis=axis)


@jax.custom_vjp
def _mask_dot(e3, x):
    return lax.dot_general(e3[0], _split3(x, 0), (NN, ((), ())), preferred_element_type=F32)


def _mask_dot_bwd(e3, g):
    dx = lax.dot_general(e3[1], _split3(g, 0), (TN, ((), ())), preferred_element_type=F32)
    return (jnp.zeros_like(e3[0]), jnp.zeros_like(e3[1])), dx


_mask_dot.defvjp(lambda e3, x: (_mask_dot(e3, x), e3), _mask_dot_bwd)


def _heads(a, n):
    return jnp.concatenate([a[None, :, h * HD:(h + 1) * HD] for h in range(n)], axis=0)


def _sigmoid(x):
    return jax.nn.sigmoid(x)


def _silu(x):
    return x * jax.nn.sigmoid(x)


def _softplus(x):
    return jnp.maximum(x, 0.0) + jnp.log(1.0 + jnp.exp(-jnp.abs(x)))


def _iota(shape, dim):
    return lax.broadcasted_iota(jnp.int32, shape, dim)


def _pick(n, prefs):
    for p in prefs:
        if n % p == 0:
            return p
    return n


def _mm(a, b, *, ta=False, tb=False, out_dtype=F32, name, b_view=None, out_split=0, act=False, gate=None, plus=None,
        cargo=(), exchange=None):
    if ta:
        k_dim, m_dim = a.shape
    else:
        m_dim, k_dim = a.shape
    if b_view is None:
        w_rows, w_cols = b.shape
    else:
        kind, layer = b_view
        nj, _, blk_r, blk_c = b.shape
        w_rows, w_cols = (blk_r, nj * blk_c) if kind == "cols" else (nj * blk_r, blk_c)
    n_dim = w_rows if tb else w_cols
    assert (w_cols if tb else w_rows) == k_dim
    tm = _pick(m_dim, (1024, 1056, 704, 640, 512, 384, 256, 128))
    tn = _pick(n_dim, (1024, 1056, 704, 640, 512, 384, 256, 128))
    tk = _pick(k_dim, (1024, 1056, 704, 512, 384, 256, 128))
    nk = k_dim // tk
    a_spec = pl.BlockSpec((tk, tm), lambda i, j, k: (k, i)) if ta else pl.BlockSpec((tm, tk), lambda i, j, k: (i, k))
    wb = (tn, tk) if tb else (tk, tn)
    w_idx = (lambda i, j, k: (j, k)) if tb else (lambda i, j, k: (k, j))
    if b_view is None:
        b_spec = pl.BlockSpec(wb, w_idx)
    elif kind == "cols":
        per = blk_c // wb[1]
        b_spec = pl.BlockSpec((None, None) + wb,
                              lambda i, j, k: (w_idx(i, j, k)[1] // per, layer, w_idx(i, j, k)[0], w_idx(i, j, k)[1] % per))
    else:
        per = blk_r // wb[0]
        b_spec = pl.BlockSpec((None, None) + wb,
                              lambda i, j, k: (w_idx(i, j, k)[0] // per, layer, w_idx(i, j, k)[0] % per, w_idx(i, j, k)[1]))
    if out_split:
        per_o = (n_dim // out_split) // tn
        out_spec = pl.BlockSpec((None, tm, tn), lambda i, j, k: (j // per_o, i, j % per_o))
        out_sds = jax.ShapeDtypeStruct((out_split, m_dim, n_dim // out_split), out_dtype)
    else:
        out_spec = pl.BlockSpec((tm, tn), lambda i, j, k: (i, j))
        out_sds = jax.ShapeDtypeStruct((m_dim, n_dim), out_dtype)
    dims = (((0 if ta else 1,), (1 if tb else 0,)), ((), ()))
    assert gate is None or plus is None
    extra = [e for e in (gate, plus) if e is not None]
    n_out = 2 if act else 1

    def finish(acc, refs):
        if act:
            refs[0][...] = acc.astype(refs[0].dtype)
            r = jnp.maximum(acc, 0.0)
            refs[1][...] = (r * r).astype(refs[1].dtype)
        elif gate is not None:
            refs[1][...] = (acc * (2.0 * jnp.maximum(refs[0][...].astype(F32), 0.0))).astype(refs[1].dtype)
        elif plus is not None:
            refs[1][...] = (refs[0][...] + acc).astype(refs[1].dtype)
        else:
            refs[0][...] = acc.astype(refs[0].dtype)

    grid = (m_dim // tm, n_dim // tn, nk)
    nc = len(cargo)

    def body(a_ref, b_ref, *rest):
        acc_ref = rest[-1]
        ids = [pl.program_id(d) for d in range(3)]
        outs, end_cargo = _cargo_bounds(
            rest[len(extra):-1], nc, n_out, exchange, (ids[0] == 0) & (ids[1] == 0) & (ids[2] == 0),
            (ids[0] == grid[0] - 1) & (ids[1] == grid[1] - 1) & (ids[2] == grid[2] - 1))
        refs = tuple(rest[:len(extra)]) + tuple(outs)
        part = lax.dot_general(a_ref[...], b_ref[...], dims, preferred_element_type=F32)
        if nk == 1:
            finish(part, refs)
        else:
            k = ids[2]

            @pl.when(k == 0)
            def _():
                acc_ref[...] = part

            @pl.when(k > 0)
            def _():
                acc_ref[...] += part

            @pl.when(k == nk - 1)
            def _():
                finish(acc_ref[...], refs)
        end_cargo()

    out = pl.pallas_call(
        body, name=name, grid=grid,
        in_specs=[a_spec, b_spec] + [pl.BlockSpec((tm, tn), lambda i, j, k: (i, j))] * len(extra) + [ANY] * nc,
        out_specs=[out_spec] * n_out + [ANY] * nc,
        out_shape=[out_sds] * n_out + (exchange[1](cargo) if nc else []),
        scratch_shapes=(exchange[2](nc) if nc else []) + [pltpu.VMEM((tm, tn) if nk > 1 else (8, 128), F32)],
        compiler_params=_cparams(dimension_semantics=("arbitrary",) * 3 if nc else ("parallel", "parallel", "arbitrary")),
    )(a, b, *extra, *cargo)
    if nc:
        return out
    return out if act else out[0]


def _mm_groups_nt(parts, b, name):
    m_dim, k_dim = parts[0].shape
    ng, _, n_dim, _ = b.shape
    assert len(parts) == ng and b.shape[3] == k_dim
    tm = _pick(m_dim, (1056, 704, 512, 384, 256, 128))

    def body(*refs):
        a_refs, b_ref, o_ref, acc_ref = refs[:ng], refs[ng], refs[ng + 1], refs[ng + 2]
        k = pl.program_id(1)
        for g in range(ng):
            @pl.when(k == g)
            def _(g=g):
                part = lax.dot_general(a_refs[g][...], b_ref[...], (NT, ((), ())), preferred_element_type=F32)
                if g == 0:
                    acc_ref[...] = part
                elif g < ng - 1:
                    acc_ref[...] += part
                else:
                    o_ref[...] = acc_ref[...] + part

    return pl.pallas_call(
        body, name=name, grid=(m_dim // tm, ng),
        in_specs=[pl.BlockSpec((tm, k_dim), lambda i, k: (i, 0))] * ng
        + [pl.BlockSpec((None, None, n_dim, k_dim), lambda i, k: (k, 0, 0, 0))],
        out_specs=pl.BlockSpec((tm, n_dim), lambda i, k: (i, 0)),
        out_shape=jax.ShapeDtypeStruct((m_dim, n_dim), F32),
        scratch_shapes=[pltpu.VMEM((tm, n_dim), F32)],
        compiler_params=_cparams(dimension_semantics=("parallel", "arbitrary")),
    )(*parts, b)


def _row_tile(t_pad, width):
    for tr in (528, 352, 176, 128, 64):
        if t_pad % tr == 0 and tr * width * 4 <= (3 << 19) and tr % 16 == 0:
            return tr
    return 64 if t_pad % 64 == 0 else t_pad


def _ln_res_fn(h, m, g, b):
    x = ALPHA * h + m
    mu = jnp.mean(x, axis=-1, keepdims=True)
    xc = x - mu
    var = jnp.mean(xc * xc, axis=-1, keepdims=True)
    return xc * lax.rsqrt(var + LN_EPS) * g + b


def _ln_res_fwd(h, m, g, b, name):
    t_pad = h.shape[0]
    tr = _row_tile(t_pad, D)

    def body(h_ref, m_ref, g_ref, b_ref, y_ref, yb_ref):
        y = _ln_res_fn(h_ref[...], m_ref[...], g_ref[...], b_ref[...])
        y_ref[...] = y
        yb_ref[...] = y.astype(BF16)

    row = pl.BlockSpec((tr, D), lambda i: (i, 0))
    par = pl.BlockSpec((1, D), lambda i: (0, 0))
    return pl.pallas_call(
        body, name=name, grid=(t_pad // tr,), in_specs=[row, row, par, par], out_specs=[row, row],
        out_shape=[jax.ShapeDtypeStruct((t_pad, D), F32), jax.ShapeDtypeStruct((t_pad, D), BF16)],
        compiler_params=_cparams(),
    )(h, m, g, b)


def _ln_res_bwd(h, m, g, b, dys, name):
    t_pad = h.shape[0]
    tr = _row_tile(t_pad, D)
    nd = len(dys)

    def body(h_ref, m_ref, g_ref, b_ref, *rest):
        d_refs, (dh_ref, dm_ref, dg_ref, db_ref) = rest[:nd], rest[nd:]
        _, vjp = jax.vjp(_ln_res_fn, h_ref[...], m_ref[...], g_ref[...], b_ref[...])
        dy = d_refs[0][...]
        for d_ref in d_refs[1:]:
            dy = dy + d_ref[...]
        dh, dm, dg, db = vjp(dy)
        dh_ref[...] = dh
        dm_ref[...] = dm.astype(BF16)

        @pl.when(pl.program_id(0) == 0)
        def _():
            dg_ref[...] = jnp.zeros_like(dg_ref)
            db_ref[...] = jnp.zeros_like(db_ref)

        dg_ref[...] += dg
        db_ref[...] += db

    row = pl.BlockSpec((tr, D), lambda i: (i, 0))
    par = pl.BlockSpec((1, D), lambda i: (0, 0))
    return pl.pallas_call(
        body, name=name, grid=(t_pad // tr,), in_specs=[row, row, par, par] + [row] * nd,
        out_specs=[row, row, par, par],
        out_shape=[jax.ShapeDtypeStruct((t_pad, D), F32), jax.ShapeDtypeStruct((t_pad, D), BF16),
                   jax.ShapeDtypeStruct((1, D), F32), jax.ShapeDtypeStruct((1, D), F32)],
        compiler_params=_cparams(),
    )(h, m, g, b, *dys)


def _grms_fn(o, z, g):
    y = o * lax.rsqrt(jnp.mean(o * o, axis=-1, keepdims=True) + RMS_EPS) * g
    return y * _silu(z)


def _grms_fwd(o, z_arr, z_blk0, g, name):
    t_pad, w = o.shape
    tr = _row_tile(t_pad, w)
    assert (z_blk0 * HD) % w == 0

    def body(o_ref, z_ref, g_ref, y_ref):
        for h in range(w // HD):
            c = slice(h * HD, (h + 1) * HD)
            y_ref[:, c] = _grms_fn(o_ref[:, c], z_ref[:, c], g_ref[...]).astype(BF16)

    return pl.pallas_call(
        body, name=name, grid=(t_pad // tr,),
        in_specs=[pl.BlockSpec((tr, w), lambda i: (i, 0)), pl.BlockSpec((tr, w), lambda i: (i, z_blk0 * HD // w)),
                  pl.BlockSpec((1, HD), lambda i: (0, 0))],
        out_specs=pl.BlockSpec((tr, w), lambda i: (i, 0)),
        out_shape=jax.ShapeDtypeStruct((t_pad, w), BF16), compiler_params=_cparams(),
    )(o, z_arr, g)


def _grms_bwd(o, z_arr, z_blk0, g, dy_arr, dy_blk0, name):
    t_pad, w = o.shape
    tr = _row_tile(t_pad, w)
    assert (z_blk0 * HD) % w == 0 and (dy_blk0 * HD) % w == 0

    def body(o_ref, z_ref, g_ref, dy_ref, do_ref, dz_ref, dg_ref):
        @pl.when(pl.program_id(0) == 0)
        def _():
            dg_ref[...] = jnp.zeros_like(dg_ref)

        for h in range(w // HD):
            c = slice(h * HD, (h + 1) * HD)
            _, vjp = jax.vjp(_grms_fn, o_ref[:, c], z_ref[:, c], g_ref[...])
            do, dz, dg = vjp(dy_ref[:, c])
            do_ref[:, c] = do
            dz_ref[:, c] = dz.astype(BF16)
            dg_ref[...] += dg

    blk = pl.BlockSpec((tr, w), lambda i: (i, 0))
    return pl.pallas_call(
        body, name=name, grid=(t_pad // tr,),
        in_specs=[blk, pl.BlockSpec((tr, w), lambda i: (i, z_blk0 * HD // w)), pl.BlockSpec((1, HD), lambda i: (0, 0)),
                  pl.BlockSpec((tr, w), lambda i: (i, dy_blk0 * HD // w))],
        out_specs=[blk, blk, pl.BlockSpec((1, HD), lambda i: (0, 0))],
        out_shape=[jax.ShapeDtypeStruct((t_pad, w), F32), jax.ShapeDtypeStruct((t_pad, w), BF16),
                   jax.ShapeDtypeStruct((1, HD), F32)],
        compiler_params=_cparams(),
    )(o, z_arr, g, dy_arr)


def _loss_fwd(y, tgt, first_row, name):
    t_pad = y.shape[0]
    tr = _row_tile(t_pad, D)

    def body(y_ref, t_ref, l_ref, dy_ref):
        rows = pl.program_id(0) * tr + _iota((tr, 1), 0)
        err = jnp.where(rows >= first_row, y_ref[...] - t_ref[...], 0.0)
        dy_ref[...] = err * (1.0 / D)

        @pl.when(pl.program_id(0) == 0)
        def _():
            l_ref[...] = jnp.zeros_like(l_ref)

        part = jnp.sum(jnp.sum(err * err, axis=1, keepdims=True), axis=0, keepdims=True)
        l_ref[...] += jnp.broadcast_to(part * (0.5 / D), l_ref.shape)

    row = pl.BlockSpec((tr, D), lambda i: (i, 0))
    return pl.pallas_call(
        body, name=name, grid=(t_pad // tr,), in_specs=[row, row],
        out_specs=[pl.BlockSpec((8, 128), lambda i: (0, 0)), row],
        out_shape=[jax.ShapeDtypeStruct((8, 128), F32), jax.ShapeDtypeStruct((t_pad, D), F32)],
        compiler_params=_cparams(),
    )(y, tgt)


def _assemble_bf16(parts, name):
    t_pad = parts[0].shape[0]
    widths = [p.shape[1] for p in parts]
    total = sum(widths)
    tr = _row_tile(t_pad, total)

    def body(*refs):
        o_ref = refs[-1]
        off = 0
        for ref, w in zip(refs[:-1], widths):
            o_ref[:, off:off + w] = ref[...].astype(BF16)
            off += w

    return pl.pallas_call(
        body, name=name, grid=(t_pad // tr,), in_specs=[pl.BlockSpec((tr, w), lambda i: (i, 0)) for w in widths],
        out_specs=pl.BlockSpec((tr, total), lambda i: (i, 0)),
        out_shape=jax.ShapeDtypeStruct((t_pad, total), BF16), compiler_params=_cparams(),
    )(*parts)


CONV_K = 4
HALO = 8
RT = 128


def _conv_fwd(p, blk0, w, mode, pad, name):
    t_pad = p.shape[0]
    nt = t_pad // RT
    scale = HD ** -0.5 if mode == "q" else 1.0

    def body(x_ref, w_ref, y_ref, xs_ref):
        xs_ref[0:HALO, :] = jnp.zeros((HALO, HD), F32)
        rows = _iota((t_pad, 1), 0)
        xs_ref[HALO:HALO + t_pad, :] = jnp.where(rows >= pad, x_ref[...], 0.0)
        wv = w_ref[...]

        def tile(i, carry):
            r0 = pl.multiple_of(i * RT, RT)
            ext = xs_ref[pl.ds(r0, RT + HALO), :]
            acc = ext[HALO:, :] * wv[3:4, :]
            for s in (1, 2, 3):
                acc = acc + pltpu.roll(ext, s, 0)[HALO:, :] * wv[3 - s:4 - s, :]
            y = _silu(acc)
            if mode != "v":
                y = y * lax.rsqrt(jnp.sum(y * y, axis=-1, keepdims=True) + L2_EPS) * scale
            y_ref[pl.ds(r0, RT), :] = y
            return carry

        lax.fori_loop(0, nt, tile, 0)

    return pl.pallas_call(
        body, name=name, grid=(GDN_H,),
        in_specs=[pl.BlockSpec((t_pad, HD), lambda h: (0, blk0 + h)), pl.BlockSpec((CONV_K, HD), lambda h: (0, h))],
        out_specs=pl.BlockSpec((t_pad, HD), lambda h: (0, h)),
        out_shape=jax.ShapeDtypeStruct((t_pad, GDN_H * HD), F32),
        scratch_shapes=[pltpu.VMEM((t_pad + HALO, HD), F32)],
        compiler_params=_cparams(),
    )(p, w)


def _conv_bwd(p, blk0, w, dn, mode, pad, name):
    t_pad = p.shape[0]
    nt = t_pad // RT
    scale = HD ** -0.5 if mode == "q" else 1.0

    def body(x_ref, w_ref, dn_ref, dx_ref, dw_ref, xs_ref, ds_ref):
        xs_ref[0:HALO, :] = jnp.zeros((HALO, HD), F32)
        xs_ref[HALO + t_pad:HALO + t_pad + 2 * HALO, :] = jnp.zeros((2 * HALO, HD), F32)
        ds_ref[t_pad:t_pad + HALO, :] = jnp.zeros((HALO, HD), F32)
        rows = _iota((t_pad, 1), 0)
        xs_ref[HALO:HALO + t_pad, :] = jnp.where(rows >= pad, x_ref[...], 0.0)
        ds_ref[0:t_pad, :] = dn_ref[...]
        wv = w_ref[...]

        def tile(i, dw):
            r0 = pl.multiple_of(i * RT, RT)
            ext = xs_ref[pl.ds(r0, RT + 2 * HALO), :]
            dn_e = ds_ref[pl.ds(r0, RT + HALO), :]
            xsh = [ext[HALO:, :]] + [pltpu.roll(ext, s, 0)[HALO:, :] for s in (1, 2, 3)]
            pre = xsh[0] * wv[3:4, :]
            for s in (1, 2, 3):
                pre = pre + xsh[s] * wv[3 - s:4 - s, :]
            sg = _sigmoid(pre)
            y = pre * sg
            if mode != "v":
                ss = jnp.sum(y * y, axis=-1, keepdims=True) + L2_EPS
                r = lax.rsqrt(ss)
                dy = scale * (dn_e * r - y * (r * r * r) * jnp.sum(dn_e * y, axis=-1, keepdims=True))
            else:
                dy = dn_e
            dpre = dy * (sg * (1.0 + pre * (1.0 - sg)))
            dx = dpre[:RT, :] * wv[3:4, :]
            for s in (1, 2, 3):
                dx = dx + pltpu.roll(dpre, RT + HALO - s, 0)[:RT, :] * wv[3 - s:4 - s, :]
            trow = r0 + _iota((RT, 1), 0)
            dx_ref[pl.ds(r0, RT), :] = jnp.where(trow >= pad, dx, 0.0)
            new = []
            for s in (0, 1, 2, 3):
                new.append(dw[s] + jnp.sum(dpre[:RT, :] * xsh[s][:RT, :], axis=0, keepdims=True))
            return tuple(new)

        z = jnp.zeros((1, HD), F32)
        dw = lax.fori_loop(0, nt, tile, (z, z, z, z))
        for s in (0, 1, 2, 3):
            dw_ref[3 - s:4 - s, :] = dw[s]

    return pl.pallas_call(
        body, name=name, grid=(GDN_H,),
        in_specs=[pl.BlockSpec((t_pad, HD), lambda h: (0, blk0 + h)), pl.BlockSpec((CONV_K, HD), lambda h: (0, h)),
                  pl.BlockSpec((t_pad, HD), lambda h: (0, h))],
        out_specs=[pl.BlockSpec((t_pad, HD), lambda h: (0, h)), pl.BlockSpec((CONV_K, HD), lambda h: (0, h))],
        out_shape=[jax.ShapeDtypeStruct((t_pad, GDN_H * HD), F32), jax.ShapeDtypeStruct((CONV_K, GDN_H * HD), F32)],
        scratch_shapes=[pltpu.VMEM((t_pad + 3 * HALO, HD), F32), pltpu.VMEM((t_pad + HALO, HD), F32)],
        compiler_params=_cparams(),
    )(p, w, dn)


@jax.custom_vjp
def _unit_lower_inv(m, bd, eye):
    md = m * bd
    low = m - md
    p2 = mbnn(md, md)
    p4 = mbnn(p2, p2)
    dinv = mbnn(mbnn(eye - md, eye + p2), eye + p4)
    n = mbnn(dinv, low)
    n2 = mbnn(n, n)
    n4 = mbnn(n2, n2)
    return mbnn(mbnn(mbnn(eye - n, eye + n2), eye + n4), dinv)


def _unit_lower_inv_bwd(res, g):
    t, bd, eye = res
    return -mbtn(t, mbnt(g, t)), jnp.zeros_like(bd), jnp.zeros_like(eye)


def _unit_lower_inv_fwd(m, bd, eye):
    t = _unit_lower_inv(m, bd, eye)
    return t, (t, bd, eye)


_unit_lower_inv.defvjp(_unit_lower_inv_fwd, _unit_lower_inv_bwd)


def _gdn_chunks(chunks, alog, dtb, s):
    nh = chunks[0][0].shape[0]
    ri = _iota((1, CH, CH), 1)
    ci = _iota((1, CH, CH), 2)
    causal = ri >= ci
    strict = ri > ci
    eye = (ri == ci).astype(F32)
    bd = ((ri >> 3) == (ci >> 3)).astype(F32)
    ltri = (_iota((CH, CH), 0) >= _iota((CH, CH), 1)).astype(F32)
    sel = (_iota((nh, 1, HD), 2) == _iota((nh, 1, HD), 0)).astype(F32)
    last = _iota((1, CH, 1), 1) == CH - 1

    beta, gc, gc_rows = [], [], []
    for _, _, _, bb, aa, valid in chunks:
        beta_all = jnp.where(valid, _sigmoid(bb), 0.0)
        g_all = jnp.where(valid, -jnp.exp(alog) * _softplus(aa + dtb), 0.0)
        gc_all = hnn(ltri, g_all)
        beta.append(jnp.sum(beta_all[None] * sel, axis=2, keepdims=True))
        gc.append(jnp.sum(gc_all[None] * sel, axis=2, keepdims=True))
        gc_rows.append(hbnt(jnp.broadcast_to(sel, (nh, CH, HD)), jnp.broadcast_to(gc_all[None], (nh, CH, HD))))
    cat = lambda xs: jnp.concatenate(xs, axis=0)
    q, k, v = (cat([c[j] for c in chunks]) for j in range(3))
    beta, gc, gc_rows = cat(beta), cat(gc), cat(gc_rows)
    gc_last = jnp.sum(jnp.where(last, gc, 0.0), axis=1, keepdims=True)
    decay = jnp.exp(jnp.where(causal, gc - gc_rows, NEG))
    egc = jnp.exp(gc)

    kb = k * beta
    m = jnp.where(strict, bbnt(kb, k) * decay, 0.0)
    t_inv = _unit_lower_inv(m, bd, eye)
    u = bbnn(t_inv, v * beta)
    w = bbnn(t_inv, kb * egc)
    a_intra = bbnt(q, k) * decay
    q_dec = q * egc
    k_dec = k * jnp.exp(gc_last - gc)
    g_tot = jnp.exp(gc_last)

    outs = []
    for n in range(len(chunks)):
        part = lambda a: a[n * nh:(n + 1) * nh]
        v_new = part(u) - bbnn(part(w), s)
        outs.append(bbnn(part(q_dec), s) + bbnn(part(a_intra), v_new))
        s = s * part(g_tot) + bbtn(part(k_dec), v_new)
    return outs, s


PAIR = 2 * CH


def _gdn_specs(npair, rev):
    cc = (lambda c: npair - 1 - c) if rev else (lambda c: c)
    wide = pl.BlockSpec((PAIR, GDN_H * HD), lambda c: (cc(c), 0))
    fix = lambda off: pl.BlockSpec((PAIR, HD), lambda c: (cc(c), off))
    par = pl.BlockSpec((1, HD), lambda c: (0, 0))
    state = pl.BlockSpec((1, GDN_H, HD, HD), lambda c: (cc(c), 0, 0, 0))
    return wide, fix, par, state


def _store_heads(ref, a, rows=slice(None)):
    for h in range(a.shape[0]):
        ref[rows, h * HD:(h + 1) * HD] = a[h]


def _chunk_rows(half):
    return slice(half * CH, (half + 1) * CH)


def _chunk_valid(pair, half, pad):
    return ((2 * pair + half) * CH + _iota((CH, 1), 0)) >= pad


def _gdn_fwd(qn, kn, vn, p, alog, dtb, pad, name, cargo=(), exchange=None):
    t_pad = qn.shape[0]
    npair = t_pad // PAIR
    wide, fix, par, state = _gdn_specs(npair, False)
    n = len(cargo)

    def body(q_ref, k_ref, v_ref, bb_ref, aa_ref, al_ref, dt_ref, *rest):
        c = pl.program_id(0)
        s_ref = rest[-1]
        (o_ref, ss_ref), end_cargo = _cargo_bounds(rest[:-1], n, 2, exchange, c == 0, c == npair - 1)

        @pl.when(c == 0)
        def _():
            s_ref[...] = jnp.zeros_like(s_ref)

        s = s_ref[...]
        ss_ref[0] = s
        rows = [_chunk_rows(half) for half in (0, 1)]
        chunks = [(_heads(q_ref[r, :], GDN_H), _heads(k_ref[r, :], GDN_H), _heads(v_ref[r, :], GDN_H),
                   bb_ref[r, :], aa_ref[r, :], _chunk_valid(c, half, pad)) for half, r in enumerate(rows)]
        outs, s = _gdn_chunks(chunks, al_ref[...], dt_ref[...], s)
        for r, o in zip(rows, outs):
            _store_heads(o_ref, o, r)
        s_ref[...] = s
        end_cargo()

    return pl.pallas_call(
        body, name=name, grid=(npair,),
        in_specs=[wide, wide, wide, fix(16), fix(17), par, par] + [ANY] * n,
        out_specs=[wide, state] + [ANY] * n,
        out_shape=[jax.ShapeDtypeStruct((t_pad, GDN_H * HD), F32), jax.ShapeDtypeStruct((npair, GDN_H, HD, HD), F32)]
        + (exchange[1](cargo) if n else []),
        scratch_shapes=(exchange[2](n) if n else []) + [pltpu.VMEM((GDN_H, HD, HD), F32)],
        compiler_params=_cparams(),
    )(qn, kn, vn, p, p, alog, dtb, *cargo)


def _gdn_bwd(qn, kn, vn, p, alog, dtb, ssave, do, pad, name, cargo=(), exchange=None):
    t_pad = qn.shape[0]
    npair = t_pad // PAIR
    wide, fix, par, state = _gdn_specs(npair, True)
    n = len(cargo)

    def body(q_ref, k_ref, v_ref, bb_ref, aa_ref, al_ref, dt_ref, ss_ref, do_ref, *rest):
        c = pl.program_id(0)
        ds_ref = rest[-1]
        (dq_ref, dk_ref, dv_ref, dbb_ref, daa_ref, dal_ref, ddt_ref), end_cargo = _cargo_bounds(
            rest[:-1], n, 7, exchange, c == 0, c == npair - 1)

        @pl.when(c == 0)
        def _():
            ds_ref[...] = jnp.zeros_like(ds_ref)
            dal_ref[...] = jnp.zeros_like(dal_ref)
            ddt_ref[...] = jnp.zeros_like(ddt_ref)

        ra, rb = _chunk_rows(0), _chunk_rows(1)
        va, vb = _chunk_valid(npair - 1 - c, 0, pad), _chunk_valid(npair - 1 - c, 1, pad)

        def pair(qa, ka, va_, ba, aa, qb, kb, vb_, bb, ab, al, dt, s):
            (oa, ob), s = _gdn_chunks([(qa, ka, va_, ba, aa, va), (qb, kb, vb_, bb, ab, vb)], al, dt, s)
            return oa, ob, s

        ins = [f(ref[r, :]) for r in (ra, rb)
               for ref, f in ((q_ref, lambda a: _heads(a, GDN_H)), (k_ref, lambda a: _heads(a, GDN_H)),
                              (v_ref, lambda a: _heads(a, GDN_H)), (bb_ref, lambda a: a), (aa_ref, lambda a: a))]
        _, vjp = jax.vjp(pair, *ins, al_ref[...], dt_ref[...], ss_ref[0])
        g = vjp((_heads(do_ref[ra, :], GDN_H), _heads(do_ref[rb, :], GDN_H), ds_ref[...]))
        for r, (dq, dk, dv, dbb, daa) in ((ra, g[0:5]), (rb, g[5:10])):
            _store_heads(dq_ref, dq, r)
            _store_heads(dk_ref, dk, r)
            _store_heads(dv_ref, dv, r)
            dbb_ref[r, :] = dbb
            daa_ref[r, :] = daa
        dal_ref[...] += g[10]
        ddt_ref[...] += g[11]
        ds_ref[...] = g[12]
        end_cargo()

    sds = jax.ShapeDtypeStruct
    return pl.pallas_call(
        body, name=name, grid=(npair,),
        in_specs=[wide, wide, wide, fix(16), fix(17), par, par, state, wide] + [ANY] * n,
        out_specs=[wide, wide, wide, fix(0), fix(0), par, par] + [ANY] * n,
        out_shape=[sds((t_pad, GDN_H * HD), F32)] * 3 + [sds((t_pad, HD), F32)] * 2 + [sds((1, HD), F32)] * 2
        + (exchange[1](cargo) if n else []),
        scratch_shapes=(exchange[2](n) if n else []) + [pltpu.VMEM((GDN_H, HD, HD), F32)],
        compiler_params=_cparams(),
    )(qn, kn, vn, p, p, alog, dtb, ssave, do, *cargo)


SB_Q0, SB_K0, SB_V0 = 18, 22, 26
SB_SCALE = SB_DH ** -0.5
SB_NB = 8


def _sb_terms(z, allowed):
    e = jnp.exp(-jnp.abs(z))
    den = 1.0 + e
    raw = -jnp.maximum(z, 0.0) - jnp.log(den)
    l1m = raw if allowed is None else jnp.where(allowed, raw, 0.0)
    return l1m, z + raw, jnp.where(z >= 0.0, 1.0, e) / den


def _sb_passes(i, step, carry):
    total = i + 1
    sized = lambda done: [functools.partial(step, done, masked=True, nb=nb) for nb in range(1, SB_NB + 1)]

    def several(c):
        n_mid = (total - SB_NB - 1) // SB_NB
        c = step(0, c, masked=True, nb=SB_NB)
        c = lax.fori_loop(0, n_mid, lambda t, cc: step(SB_NB * (1 + t), cc, masked=False, nb=SB_NB), c)
        done = SB_NB * (1 + n_mid)
        return lax.switch(total - done - 1, sized(done), c)

    return lax.cond(total <= SB_NB, lambda c: lax.switch(total - 1, sized(0), c), several, carry)


def _sb_stack(a, i):
    first = _iota((1, HD), 1) < SB_DH
    a2 = jnp.concatenate([jnp.where(first, a, 0.0), jnp.where(first, 0.0, a)], axis=0).astype(BF16)
    rq = i * QB + _iota((QB, 1), 0)
    return a2, jnp.concatenate([rq, rq], axis=0), first


def _hi_lo(a):
    hi = a.astype(BF16)
    lo = (a - hi.astype(F32)).astype(BF16)
    return jnp.concatenate([hi, lo], axis=1)


def _cargo_bounds(refs, n, n_out, exchange, first, last):
    outs = refs[n:n + n_out]
    if not n:
        return outs, lambda: None
    ex = exchange[0](refs[:n], refs[n + n_out:2 * n + n_out], *refs[2 * n + n_out:])

    @pl.when(first)
    def _():
        ex.start()

    def finish():
        @pl.when(last)
        def _():
            ex.wait()

    return outs, finish


def _sb_fwd(p, pad, name, cargo=(), exchange=None):
    t_pad = p.shape[0]
    nq = t_pad // QB
    n = len(cargo)

    def body(q_ref, k_ref, v_ref, *rest):
        i = pl.program_id(1)
        pr = pl.program_id(0)
        (o_ref, r_ref), end_cargo = _cargo_bounds(rest, n, 2, exchange, (pr == 0) & (i == 0),
                                                  (pr == SB_H // 2 - 1) & (i == nq - 1))
        q2, rowq, first = _sb_stack(q_ref[...] * SB_SCALE, i)
        tri = (_iota((QB, QB), 0) > _iota((QB, QB), 1)).astype(BF16)
        upper2 = jnp.concatenate([tri, tri], axis=0)

        def chain(kb, masked):
            start = pl.multiple_of(kb * QB, QB)
            kblk = k_ref[pl.ds(start, QB), :].astype(BF16)
            vblk = v_ref[pl.ds(start, QB), :].astype(BF16)
            z = lax.dot_general(q2, kblk, (NT, ((), ())), preferred_element_type=F32)
            colk = kb * QB + _iota((1, QB), 1)
            al = ((colk < rowq) & (colk >= pad)) if masked else None
            l1m, ls, _ = _sb_terms(z, al)
            suf = lax.dot_general(_hi_lo(l1m), upper2, (NN, ((), ())), preferred_element_type=F32)
            return al, ls, suf, jnp.sum(l1m, axis=1, keepdims=True), vblk

        def step(done, carry, masked, nb):
            o_acc, run = carry
            ws, vs = [], []
            for n in range(nb):
                al, ls, suf, rs, vblk = chain(i - done - n, masked)
                wgt = jnp.exp(ls + suf + run)
                ws.append((wgt if al is None else jnp.where(al, wgt, 0.0)).astype(BF16))
                vs.append(vblk)
                run = run + rs
            o_acc = o_acc + lax.dot_general(jnp.concatenate(ws, axis=1), jnp.concatenate(vs, axis=0),
                                            (NN, ((), ())), preferred_element_type=F32)
            return o_acc, run

        o_acc, run = _sb_passes(i, step, (jnp.zeros((2 * QB, HD), F32), jnp.zeros((2 * QB, 1), F32)))
        o_ref[...] = jnp.where(first, o_acc[:QB], o_acc[QB:]).astype(BF16)
        r_ref[...] = jnp.where(first, run[:QB], run[QB:])
        end_cargo()

    full = lambda off: pl.BlockSpec((t_pad, HD), lambda pr, i: (0, off + pr))
    blk = pl.BlockSpec((QB, HD), lambda pr, i: (i, pr))
    return pl.pallas_call(
        body, name=name, grid=(SB_H // 2, nq),
        in_specs=[pl.BlockSpec((QB, HD), lambda pr, i: (i, SB_Q0 + pr)), full(SB_K0), full(SB_V0)] + [ANY] * n,
        out_specs=[blk, blk] + [ANY] * n,
        out_shape=[jax.ShapeDtypeStruct((t_pad, SB_H * SB_DH), BF16), jax.ShapeDtypeStruct((t_pad, SB_H * SB_DH), F32)]
        + (exchange[1](cargo) if n else []),
        scratch_shapes=exchange[2](n) if n else [],
        compiler_params=_cparams(),
    )(p, p, p, *cargo)


def _sb_bwd(p, rtot, dy, dy_blk0, pad, name, cargo=(), exchange=None):
    t_pad = p.shape[0]
    nq = t_pad // QB
    n = len(cargo)

    def body(q_ref, k_ref, v_ref, r_ref, do_ref, *rest):
        i = pl.program_id(1)
        pr = pl.program_id(0)
        (dq_ref, dk_ref, dv_ref), end_cargo = _cargo_bounds(rest, n, 3, exchange, (pr == 0) & (i == 0),
                                                            (pr == SB_H // 2 - 1) & (i == nq - 1))

        @pl.when(i == 0)
        def _():
            dk_ref[...] = jnp.zeros_like(dk_ref)
            dv_ref[...] = jnp.zeros_like(dv_ref)

        q2, rowq, first = _sb_stack(q_ref[...] * SB_SCALE, i)
        do2, _, _ = _sb_stack(do_ref[...], i)
        rt = r_ref[...]
        lane = _iota((1, HD), 1)
        rcol = jnp.concatenate([jnp.sum(jnp.where(lane == 0, rt, 0.0), axis=1, keepdims=True),
                                jnp.sum(jnp.where(lane == SB_DH, rt, 0.0), axis=1, keepdims=True)], axis=0)
        rj = _iota((QB, QB), 0)
        cs = _iota((QB, QB), 1)
        tri_u = (rj > cs).astype(BF16)
        tri_l = (rj < cs).astype(BF16)
        upper2 = jnp.concatenate([tri_u, tri_u], axis=0)
        lower2 = jnp.concatenate([tri_l, tri_l], axis=0)

        def chain(kb, masked):
            start = pl.multiple_of(kb * QB, QB)
            kblk = k_ref[pl.ds(start, QB), :].astype(BF16)
            vblk = v_ref[pl.ds(start, QB), :].astype(BF16)
            z = lax.dot_general(q2, kblk, (NT, ((), ())), preferred_element_type=F32)
            colk = kb * QB + _iota((1, QB), 1)
            al = ((colk < rowq) & (colk >= pad)) if masked else None
            l1m, ls, sg = _sb_terms(z, al)
            dwgt = lax.dot_general(do2, vblk, (NT, ((), ())), preferred_element_type=F32)
            suf = lax.dot_general(_hi_lo(l1m), upper2, (NN, ((), ())), preferred_element_type=F32)
            return start, kblk, al, ls, suf, jnp.sum(l1m, axis=1, keepdims=True), dwgt, sg

        def finish(c, seen, gseen):
            start, kblk, al, ls, suf, rs, dwgt, sg = c
            wgt = jnp.exp(ls + suf + (rcol - seen - rs))
            if al is not None:
                wgt = jnp.where(al, wgt, 0.0)
            dl = dwgt * wgt
            gpre = gseen + lax.dot_general(_hi_lo(dl), lower2, (NN, ((), ())), preferred_element_type=F32)
            dz = dl - sg * (dl + gpre)
            if al is not None:
                dz = jnp.where(al, dz, 0.0)
            dz = dz.astype(BF16)
            dk_ref[pl.ds(start, QB), :] += lax.dot_general(dz, q2, (TN, ((), ())), preferred_element_type=F32)
            dv_ref[pl.ds(start, QB), :] += lax.dot_general(wgt.astype(BF16), do2, (TN, ((), ())),
                                                           preferred_element_type=F32)
            return dz, seen + rs, gseen + jnp.sum(dl, axis=1, keepdims=True)

        def step(done, carry, masked, nb):
            dq_acc, seen, gseen = carry
            cs_ = [chain(done + n, masked) for n in range(nb)]
            dzs = []
            for c in cs_:
                dz, seen, gseen = finish(c, seen, gseen)
                dzs.append(dz)
            dq_acc = dq_acc + lax.dot_general(jnp.concatenate(dzs, axis=1), jnp.concatenate([c[1] for c in cs_], axis=0),
                                              (NN, ((), ())), preferred_element_type=F32)
            return dq_acc, seen, gseen

        zc = jnp.zeros((2 * QB, 1), F32)
        dq_acc, _, _ = _sb_passes(i, step, (jnp.zeros((2 * QB, HD), F32), zc, zc))
        dq_ref[...] = jnp.where(first, dq_acc[:QB], dq_acc[QB:]) * SB_SCALE
        end_cargo()

    full_in = lambda off: pl.BlockSpec((t_pad, HD), lambda pr, i: (0, off + pr))
    full_out = pl.BlockSpec((t_pad, HD), lambda pr, i: (0, pr))
    blk = pl.BlockSpec((QB, HD), lambda pr, i: (i, pr))
    sds = jax.ShapeDtypeStruct((t_pad, SB_H * SB_DH), F32)
    return pl.pallas_call(
        body, name=name, grid=(SB_H // 2, nq),
        in_specs=[pl.BlockSpec((QB, HD), lambda pr, i: (i, SB_Q0 + pr)), full_in(SB_K0), full_in(SB_V0), blk,
                  pl.BlockSpec((QB, HD), lambda pr, i: (i, dy_blk0 + pr))] + [ANY] * n,
        out_specs=[blk, full_out, full_out] + [ANY] * n,
        out_shape=[sds, sds, sds] + (exchange[1](cargo) if n else []),
        scratch_shapes=exchange[2](n) if n else [],
        compiler_params=_cparams(),
    )(p, p, p, rtot, dy, *cargo)


HG_LEVELS = 6


def _hg_prefix_matrix():
    t = np.arange(CH)[:, None]
    j = np.arange(CH)[None, :]
    groups = [(j <= t)]
    for lvl in range(1, HG_LEVELS + 1):
        half = CH >> lvl
        e = (t // (2 * half)) * (2 * half) + half - 1
        groups.append(j <= e)
    groups.append(np.ones((8, CH), bool))
    e = np.concatenate(groups, axis=0).astype(np.float32)
    return np.concatenate([e, e, e], axis=1), np.concatenate([e, e, e], axis=0)


HG_G = 4


def _hg_chunk(qr, fr, iv, r0, r1, st, valid, ecat):
    g = st.shape[0]
    mx = jnp.maximum(r0, r1)
    e0 = jnp.exp(r0 - mx)
    e1 = jnp.exp(r1 - mx)
    lb = e1 / (e0 + e1)
    fg = lb + (1.0 - lb) * _sigmoid(fr)
    logf = jnp.where(valid, jnp.log(fg), 0.0)
    kk = jnp.where(valid, 1.0 - fg, 0.0)
    q = jnp.where(valid, _silu(qr), 0.0)
    v = _heads(jnp.where(valid, iv, 0.0), g)

    pre = _mask_dot(ecat, logf)
    b = pre[0:CH]
    b_last = jnp.max(pre[(HG_LEVELS + 1) * CH:], axis=0, keepdims=True)
    row = _iota((CH, 1), 0)
    ri = _iota((1, CH, CH), 1)
    ci = _iota((1, CH, CH), 2)
    a = jnp.where(ri == ci, jnp.sum(_heads(q * kk, g), axis=2, keepdims=True), 0.0)
    for lvl in range(1, HG_LEVELS + 1):
        half = CH >> lvl
        m = pre[lvl * CH:(lvl + 1) * CH]
        low = (row & half) != 0
        dec = jnp.exp(jnp.where(low, b - m, m - b))
        qt = jnp.where(low, q * dec, 0.0)
        kt = jnp.where(low, 0.0, kk * dec)
        same = (ri >> (7 - lvl)) == (ci >> (7 - lvl))
        a = a + jnp.where(same, bbnt(_heads(qt, g), _heads(kt, g)), 0.0)
    o = bbnt(_heads(q * jnp.exp(b), g), st) + bbnn(a, v)
    kd = kk * jnp.exp(b_last - b)
    st_new = st * _heads(jnp.exp(b_last), g) + bbtn(v, _heads(kd, g))
    return o, st_new


def _hg_specs(npair, rev):
    cc = (lambda c: npair - 1 - c) if rev else (lambda c: c)
    ng = HG_H // HG_G
    blk = lambda off: pl.BlockSpec((PAIR, HG_G * HD), lambda h, c: (cc(c), off * ng + h))
    lbs = pl.BlockSpec((2, HG_G * HD), lambda h, c: (0, h))
    state = pl.BlockSpec((1, HG_G, HD, HD), lambda h, c: (cc(c), h, 0, 0))
    return ng, blk, lbs, state


def _hg_fwd(p, lbraw, ecat, pad, name):
    t_pad = p.shape[0]
    npair = t_pad // PAIR
    ng, blk, lbs, state = _hg_specs(npair, False)

    def body(q_ref, f_ref, i_ref, lb_ref, e_ref, et_ref, o_ref, ss_ref, s_ref):
        c = pl.program_id(1)

        @pl.when(c == 0)
        def _():
            s_ref[...] = jnp.zeros_like(s_ref)

        st = s_ref[...]
        ss_ref[0] = st
        for half in (0, 1):
            r = _chunk_rows(half)
            o, st = _hg_chunk(q_ref[r, :], f_ref[r, :], i_ref[r, :], lb_ref[0:1, :], lb_ref[1:2, :], st,
                              _chunk_valid(c, half, pad), (e_ref[...], et_ref[...]))
            _store_heads(o_ref, o, r)
        s_ref[...] = st

    return pl.pallas_call(
        body, name=name, grid=(ng, npair),
        in_specs=[blk(0), blk(1), blk(2), lbs] + [pl.BlockSpec(e.shape, lambda h, c: (0, 0)) for e in ecat],
        out_specs=[blk(0), state],
        out_shape=[jax.ShapeDtypeStruct((t_pad, HG_H * HD), F32), jax.ShapeDtypeStruct((npair, HG_H, HD, HD), F32)],
        scratch_shapes=[pltpu.VMEM((HG_G, HD, HD), F32)],
        compiler_params=_cparams(),
    )(p, p, p, lbraw, *ecat)


def _hg_bwd(p, lbraw, ecat, ssave, do, pad, name, cargo=(), exchange=None):
    t_pad = p.shape[0]
    npair = t_pad // PAIR
    ng, blk, lbs, state = _hg_specs(npair, True)
    n = len(cargo)

    def body(q_ref, f_ref, i_ref, lb_ref, e_ref, et_ref, ss_ref, do_ref, *rest):
        c = pl.program_id(1)
        hg = pl.program_id(0)
        ds_ref = rest[-1]
        (dq_ref, df_ref, di_ref, dlb_ref), end_cargo = _cargo_bounds(
            rest[:-1], n, 4, exchange, (hg == 0) & (c == 0), (hg == ng - 1) & (c == npair - 1))

        @pl.when(c == 0)
        def _():
            ds_ref[...] = jnp.zeros_like(ds_ref)
            dlb_ref[...] = jnp.zeros_like(dlb_ref)

        ra, rb = _chunk_rows(0), _chunk_rows(1)
        va, vb = _chunk_valid(npair - 1 - c, 0, pad), _chunk_valid(npair - 1 - c, 1, pad)
        ecv = (e_ref[...], et_ref[...])

        def pair(qa, fa, ia, qb, fb, ib, r0, r1, st):
            oa, st = _hg_chunk(qa, fa, ia, r0, r1, st, va, ecv)
            ob, st = _hg_chunk(qb, fb, ib, r0, r1, st, vb, ecv)
            return oa, ob, st

        ins = [ref[r, :] for r in (ra, rb) for ref in (q_ref, f_ref, i_ref)]
        _, vjp = jax.vjp(pair, *ins, lb_ref[0:1, :], lb_ref[1:2, :], ss_ref[0])
        g = vjp((_heads(do_ref[ra, :], HG_G), _heads(do_ref[rb, :], HG_G), ds_ref[...]))
        for r, (dq, df, di) in ((ra, g[0:3]), (rb, g[3:6])):
            dq_ref[r, :] = dq.astype(BF16)
            df_ref[r, :] = df.astype(BF16)
            di_ref[r, :] = di.astype(BF16)
        dlb_ref[0:1, :] += g[6]
        dlb_ref[1:2, :] += g[7]
        ds_ref[...] = g[8]
        end_cargo()

    sds = jax.ShapeDtypeStruct((t_pad, HG_H * HD), BF16)
    return pl.pallas_call(
        body, name=name, grid=(ng, npair),
        in_specs=[blk(0), blk(1), blk(2), lbs] + [pl.BlockSpec(e.shape, lambda h, c: (0, 0)) for e in ecat]
        + [state, blk(0)] + [ANY] * n,
        out_specs=[blk(0), blk(0), blk(0), lbs] + [ANY] * n,
        out_shape=[sds, sds, sds, jax.ShapeDtypeStruct((2, HG_H * HD), F32)] + (exchange[1](cargo) if n else []),
        scratch_shapes=(exchange[2](n) if n else []) + [pltpu.VMEM((HG_G, HD, HD), F32)],
        compiler_params=_cparams(),
    )(p, p, p, lbraw, *ecat, ssave, do, *cargo)


def _pad_ab_cols(w):
    z = jnp.zeros((w.shape[0], HD - GDN_H), w.dtype)
    return jnp.concatenate([w[:, :2048], w[:, 2048:2052], z, w[:, 2052:2056], z, w[:, 2056:]], axis=1)


def _unpad_ab_cols(w):
    return jnp.concatenate([w[:, :2048], w[:, 2048:2052], w[:, 2176:2180], w[:, 2304:]], axis=1)


def _lane_pad(v):
    return jnp.pad(v, ((0, 0), (0, HD - v.shape[1])))


def _mlp_fwd(hb, w1, w2, layer):
    a, r = _mm(hb, w1, b_view=("cols", layer), out_dtype=BF16, act=True, name=f"mlp_up_{layer}")
    m = _mm(r, w2, b_view=("rows", layer), name=f"mlp_down_{layer}")
    return a, r, m


def _mlp_bwd(hb, a, r, dmb, w1, w2, layer):
    da = _mm(dmb, w2, tb=True, b_view=("rows", layer), out_dtype=BF16, gate=a, name=f"mlp_down_dx_{layer}")
    dw2 = _mm(r, dmb, ta=True, out_dtype=BF16, name=f"mlp_down_dw_{layer}")
    dh = _mm(da, w1, tb=True, b_view=("cols", layer), name=f"mlp_up_dx_{layer}")
    dw1 = _mm(hb, da, ta=True, out_dtype=BF16, out_split=N_CHIP, name=f"mlp_up_dw_{layer}")
    return dh, dw1, dw2


def _local_step(h0, tgt, w, pad, late=None):
    row = lambda a, i: a[i:i + 1]
    ecat = tuple(jnp.asarray(e, dtype=BF16) for e in _hg_prefix_matrix())
    cw = [w["conv_w"][:, i * 512:(i + 1) * 512] for i in range(3)]
    alog, dtb = _lane_pad(w["a_log"]), _lane_pad(w["dt_bias"])

    h0b = h0.astype(BF16)
    p0 = _mm(h0b, w["ab_w_in"], name="ab_in")
    qn = _conv_fwd(p0, 0, cw[0], "q", pad, "conv_q")
    kn = _conv_fwd(p0, 4, cw[1], "k", pad, "conv_k")
    vn = _conv_fwd(p0, 8, cw[2], "v", pad, "conv_v")
    if late is None:
        oa_raw, ss0 = _gdn_fwd(qn, kn, vn, p0, alog, dtb, pad, "gdn_fwd")
        ob, rtot = _sb_fwd(p0, pad, "sb_fwd")
    else:
        oa_raw, ss0, g_cin, g_cout = _gdn_fwd(qn, kn, vn, p0, alog, dtb, pad, "gdn_fwd",
                                              cargo=[late["c_w_in"], late["c_w_out"]], exchange=GATHER)
        ob, rtot, g_about, g_w1, g_w2 = _sb_fwd(p0, pad, "sb_fwd", exchange=GATHER,
                                                cargo=[late["ab_w_out"], late["mlp_w1"], late["mlp_w2"]])
        w = dict(w, ab_w_out=g_about.reshape(D, D), c_w_in=g_cin, c_w_out=g_cout.reshape(D, D), mlp_w1=g_w1, mlp_w2=g_w2)
    oa = _grms_fwd(oa_raw, p0, 12, w["ab_gnorm_g"], "gdn_gate")
    ycat = jnp.concatenate([oa, ob], axis=1)
    mix0 = _mm(ycat, w["ab_w_out"], name="ab_out")
    h1, h1b = _ln_res_fwd(h0, mix0, row(w["ln_mix_g"], 0), row(w["ln_mix_b"], 0), "ln_mix_0")
    a0, r0, m0 = _mlp_fwd(h1b, w["mlp_w1"], w["mlp_w2"], 0)
    h2, h2b = _ln_res_fwd(h1, m0, row(w["ln_ffn_g"], 0), row(w["ln_ffn_b"], 0), "ln_ffn_0")
    p1 = _mm(h2b, w["c_w_in"], b_view=("cols", 0), name="c_in")
    oc_raw, ss1 = _hg_fwd(p1, w["c_lb_raw"], ecat, pad, "hg_fwd")
    yc = _grms_fwd(oc_raw, p1, 3 * HG_H, w["c_gnorm_g"], "hg_gate")
    mix1 = _mm(yc, w["c_w_out"], name="c_out")
    h3, h3b = _ln_res_fwd(h2, mix1, row(w["ln_mix_g"], 1), row(w["ln_mix_b"], 1), "ln_mix_1")
    a1, r1, m1 = _mlp_fwd(h3b, w["mlp_w1"], w["mlp_w2"], 1)
    h4, _ = _ln_res_fwd(h3, m1, row(w["ln_ffn_g"], 1), row(w["ln_ffn_b"], 1), "ln_ffn_1")
    loss, dh4 = _loss_fwd(h4, tgt, pad + N_META, "loss")

    dh3a, dm1b, dfg1, dfb1 = _ln_res_bwd(h3, m1, row(w["ln_ffn_g"], 1), row(w["ln_ffn_b"], 1), [dh4], "ln_ffn_bwd_1")
    dh3b, dw1_1, dw2_1 = _mlp_bwd(h3b, a1, r1, dm1b, w["mlp_w1"], w["mlp_w2"], 1)
    dh2a, dmix1b, dmg1, dmb1 = _ln_res_bwd(h2, mix1, row(w["ln_mix_g"], 1), row(w["ln_mix_b"], 1), [dh3a, dh3b], "ln_mix_bwd_1")
    dyc = _mm(dmix1b, w["c_w_out"], tb=True, name="c_out_dx")
    dwco = _mm(yc, dmix1b, ta=True, out_dtype=BF16, name="c_out_dw")
    doc, dzc, dcg = _grms_bwd(oc_raw, p1, 3 * HG_H, w["c_gnorm_g"], dyc, 0, "hg_gate_bwd")
    landed = {}
    rows4 = lambda a: a.reshape(N_CHIP, -1, D)
    if late is None:
        dq1, df1, di1, dlb = _hg_bwd(p1, w["c_lb_raw"], ecat, ss1, doc, pad, "hg_bwd")
    else:
        dq1, df1, di1, dlb, landed["w1_1"] = _hg_bwd(
            p1, w["c_lb_raw"], ecat, ss1, doc, pad, "hg_bwd", cargo=[dw1_1], exchange=SCATTER)
    dp1 = [dq1, df1, di1, dzc]
    dh2b = _mm_groups_nt(dp1, w["c_w_in"], "c_in_dx")
    dwc = jnp.stack([_mm(h2b, d, ta=True, out_dtype=BF16, name=f"c_in_dw_{i}") for i, d in enumerate(dp1)])
    dh1a, dm0b, dfg0, dfb0 = _ln_res_bwd(h1, m0, row(w["ln_ffn_g"], 0), row(w["ln_ffn_b"], 0), [dh2a, dh2b], "ln_ffn_bwd_0")
    dh1b, dw1_0, dw2_0 = _mlp_bwd(h1b, a0, r0, dm0b, w["mlp_w1"], w["mlp_w2"], 0)
    dh0a, dmix0b, dmg0, dmb0 = _ln_res_bwd(h0, mix0, row(w["ln_mix_g"], 0), row(w["ln_mix_b"], 0), [dh1a, dh1b], "ln_mix_bwd_0")
    dycat = _mm(dmix0b, w["ab_w_out"], tb=True, name="ab_out_dx")
    dwabo = _mm(ycat, dmix0b, ta=True, out_dtype=BF16, name="ab_out_dw")
    doa, dza, dag = _grms_bwd(oa_raw, p0, 12, w["ab_gnorm_g"], dycat, 0, "gdn_gate_bwd")
    if late is None:
        dqn, dkn, dvn, dbb, daa, dal, ddt = _gdn_bwd(qn, kn, vn, p0, alog, dtb, ss0, doa, pad, "gdn_bwd")
        dqb, dkb, dvb = _sb_bwd(p0, rtot, dycat, 4, pad, "sb_bwd")
    else:
        dqn, dkn, dvn, dbb, daa, dal, ddt, landed["c_w_in"] = _gdn_bwd(
            qn, kn, vn, p0, alog, dtb, ss0, doa, pad, "gdn_bwd", cargo=[dwc], exchange=SCATTER)
        (dqb, dkb, dvb, landed["w1_0"], landed["w2_0"], landed["w2_1"], landed["ab_w_out"],
         landed["c_w_out"]) = _sb_bwd(
            p0, rtot, dycat, 4, pad, "sb_bwd",
            cargo=[dw1_0, rows4(dw2_0), rows4(dw2_1), rows4(dwabo), rows4(dwco)], exchange=SCATTER)
    dpq, dcq = _conv_bwd(p0, 0, cw[0], dqn, "q", pad, "conv_q_bwd")
    dpk, dck = _conv_bwd(p0, 4, cw[1], dkn, "k", pad, "conv_k_bwd")
    dpv, dcv = _conv_bwd(p0, 8, cw[2], dvn, "v", pad, "conv_v_bwd")
    dp0 = _assemble_bf16([dpq, dpk, dpv, dza, dbb, daa, dqb, dkb, dvb], "ab_in_dy")
    dwab = _mm(h0b, dp0, ta=True, out_dtype=BF16, name="ab_in_dw")
    if late is None:
        dh0 = _mm(dp0, w["ab_w_in"], tb=True, plus=dh0a, name="ab_in_dx")
    else:
        dab = jnp.transpose(_unpad_ab_cols(dwab).reshape(D, N_CHIP, AB_TRUE // N_CHIP), (1, 0, 2))
        dh0, landed["ab_w_in"] = _mm(dp0, w["ab_w_in"], tb=True, plus=dh0a, name="ab_in_dx", cargo=[dab],
                                     exchange=SCATTER)

    grads = {
        "ab_w_in": dwab, "conv_w": jnp.concatenate([dcq, dck, dcv], axis=1),
        "a_log": dal[:, :GDN_H], "dt_bias": ddt[:, :GDN_H],
        "ab_gnorm_g": dag, "ab_w_out": dwabo, "c_w_in": dwc, "c_lb_raw": dlb, "c_gnorm_g": dcg, "c_w_out": dwco,
        "ln_mix_g": jnp.concatenate([dmg0, dmg1], 0), "ln_mix_b": jnp.concatenate([dmb0, dmb1], 0),
        "w1_0": dw1_0, "w1_1": dw1_1, "w2_0": dw2_0, "w2_1": dw2_1,
        "ln_ffn_g": jnp.concatenate([dfg0, dfg1], 0), "ln_ffn_b": jnp.concatenate([dfb0, dfb1], 0),
        "landed": landed,
    }
    return loss, dh0, grads


MESH = pl.DeviceIdType.MESH
ANY = pl.BlockSpec(memory_space=pl.ANY)
N_CHIP = 4
N_DEV = 8
CHIP_REL = ((1, 0), (0, 1), (1, 1))
DEV_REL = tuple((dx, dy, dc) for dx in (0, 1) for dy in (0, 1) for dc in (0, 1))[1:]

def _pos():
    return lax.axis_index("x"), lax.axis_index("y"), lax.axis_index("c")


def _flip(a, d):
    return a + d - 2 * a * d


class _Exchange:
    def __init__(self, local, sends, recvs):
        self.local, self.sends, self.recvs = local, sends, recvs

    def start(self):
        for cp in self.local + self.sends:
            cp.start()

    def wait(self):
        for cp in self.recvs:
            cp.wait_recv()
        for cp in self.sends:
            cp.wait_send()
        for cp in self.local:
            cp.wait()


def _gather_sems(n):
    return [pltpu.SemaphoreType.DMA((3 * n,)), pltpu.SemaphoreType.DMA((3 * n,)), pltpu.SemaphoreType.DMA((n,))]


def _gather_copies(x_refs, o_refs, send_sems, recv_sems, local_sems):
    n = len(x_refs)
    x, y, c = _pos()
    local = [pltpu.make_async_copy(x_refs[a], o_refs[a].at[2 * x + y], local_sems.at[a]) for a in range(n)]

    def copy(a, k, sending):
        tx, ty = _flip(x, CHIP_REL[k][0]), _flip(y, CHIP_REL[k][1])
        return pltpu.make_async_remote_copy(
            src_ref=x_refs[a], dst_ref=o_refs[a].at[2 * x + y if sending else 2 * tx + ty],
            send_sem=send_sems.at[3 * a + k], recv_sem=recv_sems.at[3 * a + k], device_id=(tx, ty, c), device_id_type=MESH)

    pairs = [(a, k) for a in range(n) for k in range(3)]
    return _Exchange(local, [copy(a, k, True) for a, k in pairs], [copy(a, k, False) for a, k in pairs])


def _gather_shapes(bufs):
    return [jax.ShapeDtypeStruct((N_CHIP,) + b.shape, b.dtype) for b in bufs]


def _chip_allgather(bufs, name):
    n = len(bufs)

    def body(*refs):
        ex = _gather_copies(refs[:n], refs[n:2 * n], *refs[2 * n:])
        ex.start()
        ex.wait()

    return pl.pallas_call(
        body, name=name, in_specs=[ANY] * n, out_specs=[ANY] * n, out_shape=_gather_shapes(bufs),
        scratch_shapes=_gather_sems(n), compiler_params=pltpu.CompilerParams(has_side_effects=True),
    )(*bufs)


def _scatter_sems(n):
    nr = N_DEV - 1
    return [pltpu.SemaphoreType.DMA((nr * n,)), pltpu.SemaphoreType.DMA((nr * n,)), pltpu.SemaphoreType.DMA((n,))]


def _scatter_copies(g_refs, o_refs, send_sems, recv_sems, local_sems):
    n = len(g_refs)
    nr = N_DEV - 1
    x, y, c = _pos()
    me = 4 * x + 2 * y + c
    local = [pltpu.make_async_copy(g_refs[a].at[2 * x + y], o_refs[a].at[me], local_sems.at[a]) for a in range(n)]

    def copy(a, k, sending):
        dx, dy, dc = DEV_REL[k]
        tx, ty, tc = _flip(x, dx), _flip(y, dy), _flip(c, dc)
        return pltpu.make_async_remote_copy(
            src_ref=g_refs[a].at[2 * tx + ty], dst_ref=o_refs[a].at[me if sending else 4 * tx + 2 * ty + tc],
            send_sem=send_sems.at[nr * a + k], recv_sem=recv_sems.at[nr * a + k],
            device_id=(tx, ty, tc), device_id_type=MESH)

    pairs = [(a, k) for a in range(n) for k in range(nr)]
    return _Exchange(local, [copy(a, k, True) for a, k in pairs], [copy(a, k, False) for a, k in pairs])


def _scatter_shapes(gs):
    return [jax.ShapeDtypeStruct((N_DEV,) + g.shape[1:], g.dtype) for g in gs]


GATHER = (_gather_copies, _gather_shapes, _gather_sems)
SCATTER = (_scatter_copies, _scatter_shapes, _scatter_sems)


def _sum_slots(r, name):
    n, rh, w = r.shape
    tr = _pick(rh, (256, 128, 64, 16))

    def body(r_ref, o_ref):
        acc = r_ref[0].astype(F32)
        for s in range(1, n):
            acc = acc + r_ref[s].astype(F32)
        o_ref[...] = acc

    return pl.pallas_call(
        body, name=name, grid=(rh // tr,), in_specs=[pl.BlockSpec((n, tr, w), lambda i: (0, i, 0))],
        out_specs=pl.BlockSpec((tr, w), lambda i: (i, 0)), out_shape=jax.ShapeDtypeStruct((rh, w), F32),
        compiler_params=_cparams(),
    )(r)


def _small_allreduce(buf, name):
    r, w = buf.shape

    def body(b_ref, o_ref, land_ref, send_sems, recv_sems):
        x, y, c = _pos()
        me = 4 * x + 2 * y + c
        land_ref[me] = b_ref[...]

        def target(k):
            dx, dy, dc = DEV_REL[k]
            return _flip(x, dx), _flip(y, dy), _flip(c, dc)

        sends = []
        for k in range(N_DEV - 1):
            tx, ty, tc = target(k)
            cp = pltpu.make_async_remote_copy(
                src_ref=b_ref, dst_ref=land_ref.at[me], send_sem=send_sems.at[k], recv_sem=recv_sems.at[k],
                device_id=(tx, ty, tc), device_id_type=MESH)
            cp.start()
            sends.append(cp)
        for k in range(N_DEV - 1):
            tx, ty, tc = target(k)
            pltpu.make_async_remote_copy(
                src_ref=b_ref, dst_ref=land_ref.at[4 * tx + 2 * ty + tc], send_sem=send_sems.at[k],
                recv_sem=recv_sems.at[k], device_id=(tx, ty, tc), device_id_type=MESH).wait_recv()
        for cp in sends:
            cp.wait_send()
        acc = land_ref[0]
        for s in range(1, N_DEV):
            acc = acc + land_ref[s]
        o_ref[...] = acc

    vm = pl.BlockSpec(memory_space=pltpu.VMEM)
    return pl.pallas_call(
        body, name=name, in_specs=[vm], out_specs=vm, out_shape=jax.ShapeDtypeStruct((r, w), F32),
        scratch_shapes=[pltpu.VMEM((N_DEV, r, w), F32), pltpu.SemaphoreType.DMA((N_DEV - 1,)),
                        pltpu.SemaphoreType.DMA((N_DEV - 1,))],
        compiler_params=pltpu.CompilerParams(has_side_effects=True),
    )(buf)


def _adamw(w, g, m, v, name):
    r, c = w.shape
    tr = _pick(r, (256, 128, 64, 8)) if r * c > (1 << 18) else r

    def body(w_ref, g_ref, m_ref, v_ref, d_ref, m2_ref, v2_ref):
        gg = g_ref[...]
        m2 = ADAM_B1 * m_ref[...] + (1.0 - ADAM_B1) * gg
        v2 = ADAM_B2 * v_ref[...] + (1.0 - ADAM_B2) * (gg * gg)
        m_hat = m2 / (1.0 - ADAM_B1 ** ADAM_STEP)
        v_hat = v2 / (1.0 - ADAM_B2 ** ADAM_STEP)
        d_ref[...] = -ADAM_LR * (m_hat / (jnp.sqrt(v_hat) + ADAM_EPS) + ADAM_WD * w_ref[...])
        m2_ref[...] = m2
        v2_ref[...] = v2

    blk = pl.BlockSpec((tr, c), lambda i: (i, 0))
    sds = jax.ShapeDtypeStruct((r, c), F32)
    return pl.pallas_call(body, name=name, grid=(r // tr,), in_specs=[blk] * 4, out_specs=[blk] * 3,
                          out_shape=[sds] * 3, compiler_params=_cparams())(w, g, m, v)


BIG = ("ab_w_in", "ab_w_out", "c_w_in", "c_w_out", "mlp_w1", "mlp_w2")
SMALL = ("ln_mix_g", "ln_mix_b", "ln_ffn_g", "ln_ffn_b", "c_lb_raw", "ab_a_log", "ab_dt_bias", "ab_gnorm_g", "c_gnorm_g")
SMALL_ROWS = 16
CONV_ROWS = 8
CONV_W = 3 * GDN_H * HD


def _conv_to_rows(cw):
    return jnp.pad(cw, ((0, 0), (0, 2 * D - CONV_W))).reshape(CONV_ROWS, D)


def _rows_to_conv(rows):
    return rows.reshape(CONV_K, 2 * D)[:, :CONV_W]


def _pack_small(d):
    rows = [jnp.pad(d[n], ((0, 0), (0, D - d[n].shape[1]))) for n in SMALL]
    buf = jnp.concatenate(rows, axis=0)
    return jnp.pad(buf, ((0, SMALL_ROWS - buf.shape[0]), (0, 0)))


def _unpack_small(buf, like):
    out, r = {}, 0
    for n in SMALL:
        nr, nc = like[n].shape
        out[n] = buf[r:r + nr, :nc]
        r += nr
    return out


def kernel(x, meta_tokens, ab_w_in, ab_conv_w, ab_a_log, ab_dt_bias, ab_gnorm_g, ab_w_out, c_w_in, c_lb_raw, c_gnorm_g, c_w_out, ln_mix_g, ln_mix_b, mlp_w1, mlp_w2, ln_ffn_g, ln_ffn_b, loss_target, m_meta_tokens, m_ab_w_in, m_ab_conv_w, m_ab_a_log, m_ab_dt_bias, m_ab_gnorm_g, m_ab_w_out, m_c_w_in, m_c_lb_raw, m_c_gnorm_g, m_c_w_out, m_ln_mix_g, m_ln_mix_b, m_mlp_w1, m_mlp_w2, m_ln_ffn_g, m_ln_ffn_b, v_meta_tokens, v_ab_w_in, v_ab_conv_w, v_ab_a_log, v_ab_dt_bias, v_ab_gnorm_g, v_ab_w_out, v_c_w_in, v_c_lb_raw, v_c_gnorm_g, v_c_w_out, v_ln_mix_g, v_ln_mix_b, v_mlp_w1, v_mlp_w2, v_ln_ffn_g, v_ln_ffn_b):
    names = ("meta_tokens", "ab_w_in", "ab_conv_w", "ab_a_log", "ab_dt_bias", "ab_gnorm_g", "ab_w_out", "c_w_in",
             "c_lb_raw", "c_gnorm_g", "c_w_out", "ln_mix_g", "ln_mix_b", "mlp_w1", "mlp_w2", "ln_ffn_g", "ln_ffn_b")
    wts = dict(zip(names, (meta_tokens, ab_w_in, ab_conv_w, ab_a_log, ab_dt_bias, ab_gnorm_g, ab_w_out, c_w_in, c_lb_raw,
                           c_gnorm_g, c_w_out, ln_mix_g, ln_mix_b, mlp_w1, mlp_w2, ln_ffn_g, ln_ffn_b)))
    mom_m = dict(zip(names, (m_meta_tokens, m_ab_w_in, m_ab_conv_w, m_ab_a_log, m_ab_dt_bias, m_ab_gnorm_g, m_ab_w_out,
                             m_c_w_in, m_c_lb_raw, m_c_gnorm_g, m_c_w_out, m_ln_mix_g, m_ln_mix_b, m_mlp_w1, m_mlp_w2,
                             m_ln_ffn_g, m_ln_ffn_b)))
    mom_v = dict(zip(names, (v_meta_tokens, v_ab_w_in, v_ab_conv_w, v_ab_a_log, v_ab_dt_bias, v_ab_gnorm_g, v_ab_w_out,
                             v_c_w_in, v_c_lb_raw, v_c_gnorm_g, v_c_w_out, v_ln_mix_g, v_ln_mix_b, v_mlp_w1, v_mlp_w2,
                             v_ln_ffn_g, v_ln_ffn_b)))
    seq = x.shape[1]
    pad = (-(N_META + seq)) % QB
    xi, yi, ci = _pos()
    chip = 2 * xi + yi

    gat_ab_in, = _chip_allgather([ab_w_in[0].astype(BF16)], "gather_weights")
    late = {"ab_w_out": ab_w_out[0].astype(BF16), "c_w_in": c_w_in.astype(BF16), "c_w_out": c_w_out[0].astype(BF16),
            "mlp_w1": mlp_w1.astype(BF16), "mlp_w2": mlp_w2.astype(BF16)}
    mcols, ccols = meta_tokens.shape[1], ab_conv_w.shape[2]
    place = jnp.concatenate([
        lax.dynamic_update_slice(jnp.zeros((N_META, D), F32), 0.5 * meta_tokens, (0, chip * mcols)),
        _conv_to_rows(lax.dynamic_update_slice(jnp.zeros((CONV_K, CONV_W), F32), 0.5 * ab_conv_w[0], (0, chip * ccols)))],
        axis=0)
    placed = _small_allreduce(place, "gather_meta")
    meta_full = placed[:N_META]

    w = {
        "ab_w_in": _pad_ab_cols(jnp.transpose(gat_ab_in, (1, 0, 2)).reshape(D, AB_TRUE)),
        "conv_w": _rows_to_conv(placed[N_META:]), "a_log": ab_a_log, "dt_bias": ab_dt_bias,
        "ab_gnorm_g": ab_gnorm_g, "c_lb_raw": c_lb_raw,
        "c_gnorm_g": c_gnorm_g, "ln_mix_g": ln_mix_g, "ln_mix_b": ln_mix_b, "ln_ffn_g": ln_ffn_g, "ln_ffn_b": ln_ffn_b,
    }

    h0 = jnp.concatenate([jnp.zeros((pad, D), F32), meta_full, x[0]], axis=0)
    tgt = jnp.concatenate([jnp.zeros((pad + N_META, D), F32), loss_target[0]], axis=0)
    loss8, dh0, g = _local_step(h0, tgt, w, pad, late)
    loss = lax.psum(loss8[0, 0], ("x", "y", "c"))
    grad_x = dh0[pad + N_META:][None]

    gsmall = {"ln_mix_g": g["ln_mix_g"], "ln_mix_b": g["ln_mix_b"], "ln_ffn_g": g["ln_ffn_g"], "ln_ffn_b": g["ln_ffn_b"],
              "c_lb_raw": g["c_lb_raw"], "ab_a_log": g["a_log"], "ab_dt_bias": g["dt_bias"], "ab_gnorm_g": g["ab_gnorm_g"],
              "c_gnorm_g": g["c_gnorm_g"]}
    sbuf = jnp.concatenate([_pack_small(gsmall), dh0[pad:pad + N_META], _conv_to_rows(g["conv_w"])], axis=0)
    ssum = _small_allreduce(sbuf, "allreduce_small")
    grads = _unpack_small(ssum[:SMALL_ROWS], wts)
    grads["meta_tokens"] = lax.dynamic_slice(ssum[SMALL_ROWS:SMALL_ROWS + N_META], (0, chip * mcols), (N_META, mcols))
    grads["ab_conv_w"] = lax.dynamic_slice(_rows_to_conv(ssum[SMALL_ROWS + N_META:]), (0, chip * ccols), (CONV_K, ccols))[None]

    sums = {k: _sum_slots(v, f"grad_sum_{k}") for k, v in g["landed"].items()}
    for n in ("ab_w_in", "ab_w_out", "c_w_in", "c_w_out"):
        grads[n] = sums[n][None]
    grads["mlp_w1"] = jnp.stack([sums["w1_0"], sums["w1_1"]])
    grads["mlp_w2"] = jnp.stack([sums["w2_0"], sums["w2_1"]])

    delta, new_m, new_v = {}, {}, {}
    for n in ("meta_tokens", "ab_conv_w") + BIG:
        shp = wts[n].shape
        to2 = lambda a: a.reshape(-1, shp[-1])
        d2, m2, v2 = _adamw(to2(wts[n]), to2(grads[n]), to2(mom_m[n]), to2(mom_v[n]), f"adamw_{n}")
        delta[n], new_m[n], new_v[n] = d2.reshape(shp), m2.reshape(shp), v2.reshape(shp)
    d2, m2, v2 = _adamw(_pack_small(wts), ssum[:SMALL_ROWS], _pack_small(mom_m), _pack_small(mom_v), "adamw_small")
    delta.update(_unpack_small(d2, wts))
    new_m.update(_unpack_small(m2, wts))
    new_v.update(_unpack_small(v2, wts))

    return (loss, grad_x, *[grads[n] for n in names], *[delta[n] for n in names], *[new_m[n] for n in names],
            *[new_v[n] for n in names])
```

```python
import functools

import numpy as np
import jax
import jax.numpy as jnp
from jax import lax
from jax.experimental import pallas as pl
from jax.experimental.pallas import tpu as pltpu

F32 = jnp.float32
BF16 = jnp.bfloat16

D = 1024
N_META = 16
DEPTH = 2
GDN_H = 4
SB_H = 8
SB_DH = 64
HG_H = 8
HD = 128
CH = 64
QB = 128
ALPHA = float((2 * DEPTH) ** 0.25)
LN_EPS = 1e-5
RMS_EPS = 1e-6
L2_EPS = 1e-6
NEG = -1e30

ADAM_LR = 0.001
ADAM_B1 = 0.9
ADAM_B2 = 0.999
ADAM_EPS = 1e-08
ADAM_WD = 0.01
ADAM_STEP = 10

AB_TRUE = 3592
V7X_VMEM_BYTES = 64 * 1024 * 1024
VMEM_LIMIT = V7X_VMEM_BYTES - 8 * 1024 * 1024

NN = ((1,), (0,))
NT = ((1,), (1,))
TN = ((0,), (0,))


def _cparams(**kw):
    return pltpu.CompilerParams(vmem_limit_bytes=VMEM_LIMIT, **kw)


def _dg(a, b, dims, mode):
    if mode == "h":
        return lax.dot_general(a, b, dims, precision=lax.Precision.HIGHEST, preferred_element_type=F32)
    if mode == "b":
        return lax.dot_general(a.astype(BF16), b.astype(BF16), dims, preferred_element_type=F32)
    ah, bh = a.astype(BF16), b.astype(BF16)
    al, bl = (a - ah.astype(F32)).astype(BF16), (b - bh.astype(F32)).astype(BF16)
    d = lambda x, y: lax.dot_general(x, y, dims, preferred_element_type=F32)
    return d(ah, bh) + (d(ah, bl) + d(al, bh))


def _make_dots(mode, batched=False):
    if batched:
        nn_d, nt_d, tn_d = (((2,), (1,)), ((0,), (0,))), (((2,), (2,)), ((0,), (0,))), (((1,), (1,)), ((0,), (0,)))
    else:
        nn_d, nt_d, tn_d = (NN, ((), ())), (NT, ((), ())), (TN, ((), ()))

    @jax.custom_vjp
    def nn(a, b):
        return _dg(a, b, nn_d, mode)

    @jax.custom_vjp
    def nt(a, b):
        return _dg(a, b, nt_d, mode)

    @jax.custom_vjp
    def tn(a, b):
        return _dg(a, b, tn_d, mode)

    nn.defvjp(lambda a, b: (nn(a, b), (a, b)), lambda r, g: (nt(g, r[1]), tn(r[0], g)))
    nt.defvjp(lambda a, b: (nt(a, b), (a, b)), lambda r, g: (nn(g, r[1]), tn(g, r[0])))
    tn.defvjp(lambda a, b: (tn(a, b), (a, b)), lambda r, g: (nt(r[1], g), nn(r[0], g)))
    return nn, nt, tn


hnn = _make_dots("h")[0]
bbnn, bbnt, bbtn = _make_dots("b", True)
mbnn, mbnt, mbtn = _make_dots("m", True)
hbnt = _make_dots("h", True)[1]


def _split3(x, axis):
    x1 = x.astype(BF16)
    r1 = x - x1.astype(F32)
    x2 = r1.astype(BF16)
    x3 = (r1 - x2.astype(F32)).astype(BF16)
    return jnp.concatenate([x1, x2, x3], axis=axis)


@jax.custom_vjp
def _mask_dot(e3, x):
    return lax.dot_general(e3[0], _split3(x, 0), (NN, ((), ())), preferred_element_type=F32)


def _mask_dot_bwd(e3, g):
    dx = lax.dot_general(e3[1], _split3(g, 0), (TN, ((), ())), preferred_element_type=F32)
    return (jnp.zeros_like(e3[0]), jnp.zeros_like(e3[1])), dx


_mask_dot.defvjp(lambda e3, x: (_mask_dot(e3, x), e3), _mask_dot_bwd)


def _heads(a, n):
    return jnp.concatenate([a[None, :, h * HD:(h + 1) * HD] for h in range(n)], axis=0)


def _sigmoid(x):
    return jax.nn.sigmoid(x)


def _silu(x):
    return x * jax.nn.sigmoid(x)


def _softplus(x):
    return jnp.maximum(x, 0.0) + jnp.log(1.0 + jnp.exp(-jnp.abs(x)))


def _iota(shape, dim):
    return lax.broadcasted_iota(jnp.int32, shape, dim)


def _pick(n, prefs):
    for p in prefs:
        if n % p == 0:
            return p
    return n


def _mm(a, b, *, ta=False, tb=False, out_dtype=F32, name, b_view=None, out_split=0, act=False, gate=None, plus=None,
        cargo=(), exchange=None):
    if ta:
        k_dim, m_dim = a.shape
    else:
        m_dim, k_dim = a.shape
    if b_view is None:
        w_rows, w_cols = b.shape
    else:
        kind, layer = b_view
        nj, _, blk_r, blk_c = b.shape
        w_rows, w_cols = (blk_r, nj * blk_c) if kind == "cols" else (nj * blk_r, blk_c)
    n_dim = w_rows if tb else w_cols
    assert (w_cols if tb else w_rows) == k_dim
    tm = _pick(m_dim, (1024, 1056, 704, 640, 512, 384, 256, 128))
    tn = _pick(n_dim, (1024, 1056, 704, 640, 512, 384, 256, 128))
    tk = _pick(k_dim, (1024, 1056, 704, 512, 384, 256, 128))
    nk = k_dim // tk
    a_spec = pl.BlockSpec((tk, tm), lambda i, j, k: (k, i)) if ta else pl.BlockSpec((tm, tk), lambda i, j, k: (i, k))
    wb = (tn, tk) if tb else (tk, tn)
    w_idx = (lambda i, j, k: (j, k)) if tb else (lambda i, j, k: (k, j))
    if b_view is None:
        b_spec = pl.BlockSpec(wb, w_idx)
    elif kind == "cols":
        per = blk_c // wb[1]
        b_spec = pl.BlockSpec((None, None) + wb,
                              lambda i, j, k: (w_idx(i, j, k)[1] // per, layer, w_idx(i, j, k)[0], w_idx(i, j, k)[1] % per))
    else:
        per = blk_r // wb[0]
        b_spec = pl.BlockSpec((None, None) + wb,
                              lambda i, j, k: (w_idx(i, j, k)[0] // per, layer, w_idx(i, j, k)[0] % per, w_idx(i, j, k)[1]))
    if out_split:
        per_o = (n_dim // out_split) // tn
        out_spec = pl.BlockSpec((None, tm, tn), lambda i, j, k: (j // per_o, i, j % per_o))
        out_sds = jax.ShapeDtypeStruct((out_split, m_dim, n_dim // out_split), out_dtype)
    else:
        out_spec = pl.BlockSpec((tm, tn), lambda i, j, k: (i, j))
        out_sds = jax.ShapeDtypeStruct((m_dim, n_dim), out_dtype)
    dims = (((0 if ta else 1,), (1 if tb else 0,)), ((), ()))
    assert gate is None or plus is None
    extra = [e for e in (gate, plus) if e is not None]
    n_out = 2 if act else 1

    def finish(acc, refs):
        if act:
            refs[0][...] = acc.astype(refs[0].dtype)
            r = jnp.maximum(acc, 0.0)
            refs[1][...] = (r * r).astype(refs[1].dtype)
        elif gate is not None:
            refs[1][...] = (acc * (2.0 * jnp.maximum(refs[0][...].astype(F32), 0.0))).astype(refs[1].dtype)
        elif plus is not None:
            refs[1][...] = (refs[0][...] + acc).astype(refs[1].dtype)
        else:
            refs[0][...] = acc.astype(refs[0].dtype)

    grid = (m_dim // tm, n_dim // tn, nk)
    nc = len(cargo)

    def body(a_ref, b_ref, *rest):
        acc_ref = rest[-1]
        ids = [pl.program_id(d) for d in range(3)]
        outs, end_cargo = _cargo_bounds(
            rest[len(extra):-1], nc, n_out, exchange, (ids[0] == 0) & (ids[1] == 0) & (ids[2] == 0),
            (ids[0] == grid[0] - 1) & (ids[1] == grid[1] - 1) & (ids[2] == grid[2] - 1))
        refs = tuple(rest[:len(extra)]) + tuple(outs)
        part = lax.dot_general(a_ref[...], b_ref[...], dims, preferred_element_type=F32)
        if nk == 1:
            finish(part, refs)
        else:
            k = ids[2]

            @pl.when(k == 0)
            def _():
                acc_ref[...] = part

            @pl.when(k > 0)
            def _():
                acc_ref[...] += part

            @pl.when(k == nk - 1)
            def _():
                finish(acc_ref[...], refs)
        end_cargo()

    out = pl.pallas_call(
        body, name=name, grid=grid,
        in_specs=[a_spec, b_spec] + [pl.BlockSpec((tm, tn), lambda i, j, k: (i, j))] * len(extra) + [ANY] * nc,
        out_specs=[out_spec] * n_out + [ANY] * nc,
        out_shape=[out_sds] * n_out + (exchange[1](cargo) if nc else []),
        scratch_shapes=(exchange[2](nc) if nc else []) + [pltpu.VMEM((tm, tn) if nk > 1 else (8, 128), F32)],
        compiler_params=_cparams(dimension_semantics=("arbitrary",) * 3 if nc else ("parallel", "parallel", "arbitrary")),
    )(a, b, *extra, *cargo)
    if nc:
        return out
    return out if act else out[0]


def _mm_groups_nt(parts, b, name):
    m_dim, k_dim = parts[0].shape
    ng, _, n_dim, _ = b.shape
    assert len(parts) == ng and b.shape[3] == k_dim
    tm = _pick(m_dim, (1056, 704, 512, 384, 256, 128))

    def body(*refs):
        a_refs, b_ref, o_ref, acc_ref = refs[:ng], refs[ng], refs[ng + 1], refs[ng + 2]
        k = pl.program_id(1)
        for g in range(ng):
            @pl.when(k == g)
            def _(g=g):
                part = lax.dot_general(a_refs[g][...], b_ref[...], (NT, ((), ())), preferred_element_type=F32)
                if g == 0:
                    acc_ref[...] = part
                elif g < ng - 1:
                    acc_ref[...] += part
                else:
                    o_ref[...] = acc_ref[...] + part

    return pl.pallas_call(
        body, name=name, grid=(m_dim // tm, ng),
        in_specs=[pl.BlockSpec((tm, k_dim), lambda i, k: (i, 0))] * ng
        + [pl.BlockSpec((None, None, n_dim, k_dim), lambda i, k: (k, 0, 0, 0))],
        out_specs=pl.BlockSpec((tm, n_dim), lambda i, k: (i, 0)),
        out_shape=jax.ShapeDtypeStruct((m_dim, n_dim), F32),
        scratch_shapes=[pltpu.VMEM((tm, n_dim), F32)],
        compiler_params=_cparams(dimension_semantics=("parallel", "arbitrary")),
    )(*parts, b)


def _row_tile(t_pad, width):
    for tr in (528, 352, 176, 128, 64):
        if t_pad % tr == 0 and tr * width * 4 <= (3 << 19) and tr % 16 == 0:
            return tr
    return 64 if t_pad % 64 == 0 else t_pad


def _ln_res_fn(h, m, g, b):
    x = ALPHA * h + m
    mu = jnp.mean(x, axis=-1, keepdims=True)
    xc = x - mu
    var = jnp.mean(xc * xc, axis=-1, keepdims=True)
    return xc * lax.rsqrt(var + LN_EPS) * g + b


def _ln_res_fwd(h, m, g, b, name):
    t_pad = h.shape[0]
    tr = _row_tile(t_pad, D)

    def body(h_ref, m_ref, g_ref, b_ref, y_ref, yb_ref):
        y = _ln_res_fn(h_ref[...], m_ref[...], g_ref[...], b_ref[...])
        y_ref[...] = y
        yb_ref[...] = y.astype(BF16)

    row = pl.BlockSpec((tr, D), lambda i: (i, 0))
    par = pl.BlockSpec((1, D), lambda i: (0, 0))
    return pl.pallas_call(
        body, name=name, grid=(t_pad // tr,), in_specs=[row, row, par, par], out_specs=[row, row],
        out_shape=[jax.ShapeDtypeStruct((t_pad, D), F32), jax.ShapeDtypeStruct((t_pad, D), BF16)],
        compiler_params=_cparams(),
    )(h, m, g, b)


def _ln_res_bwd(h, m, g, b, dys, name):
    t_pad = h.shape[0]
    tr = _row_tile(t_pad, D)
    nd = len(dys)

    def body(h_ref, m_ref, g_ref, b_ref, *rest):
        d_refs, (dh_ref, dm_ref, dg_ref, db_ref) = rest[:nd], rest[nd:]
        _, vjp = jax.vjp(_ln_res_fn, h_ref[...], m_ref[...], g_ref[...], b_ref[...])
        dy = d_refs[0][...]
        for d_ref in d_refs[1:]:
            dy = dy + d_ref[...]
        dh, dm, dg, db = vjp(dy)
        dh_ref[...] = dh
        dm_ref[...] = dm.astype(BF16)

        @pl.when(pl.program_id(0) == 0)
        def _():
            dg_ref[...] = jnp.zeros_like(dg_ref)
            db_ref[...] = jnp.zeros_like(db_ref)

        dg_ref[...] += dg
        db_ref[...] += db

    row = pl.BlockSpec((tr, D), lambda i: (i, 0))
    par = pl.BlockSpec((1, D), lambda i: (0, 0))
    return pl.pallas_call(
        body, name=name, grid=(t_pad // tr,), in_specs=[row, row, par, par] + [row] * nd,
        out_specs=[row, row, par, par],
        out_shape=[jax.ShapeDtypeStruct((t_pad, D), F32), jax.ShapeDtypeStruct((t_pad, D), BF16),
                   jax.ShapeDtypeStruct((1, D), F32), jax.ShapeDtypeStruct((1, D), F32)],
        compiler_params=_cparams(),
    )(h, m, g, b, *dys)


def _grms_fn(o, z, g):
    y = o * lax.rsqrt(jnp.mean(o * o, axis=-1, keepdims=True) + RMS_EPS) * g
    return y * _silu(z)


def _grms_fwd(o, z_arr, z_blk0, g, name):
    t_pad, w = o.shape
    tr = _row_tile(t_pad, w)
    assert (z_blk0 * HD) % w == 0

    def body(o_ref, z_ref, g_ref, y_ref):
        for h in range(w // HD):
            c = slice(h * HD, (h + 1) * HD)
            y_ref[:, c] = _grms_fn(o_ref[:, c], z_ref[:, c], g_ref[...]).astype(BF16)

    return pl.pallas_call(
        body, name=name, grid=(t_pad // tr,),
        in_specs=[pl.BlockSpec((tr, w), lambda i: (i, 0)), pl.BlockSpec((tr, w), lambda i: (i, z_blk0 * HD // w)),
                  pl.BlockSpec((1, HD), lambda i: (0, 0))],
        out_specs=pl.BlockSpec((tr, w), lambda i: (i, 0)),
        out_shape=jax.ShapeDtypeStruct((t_pad, w), BF16), compiler_params=_cparams(),
    )(o, z_arr, g)


def _grms_bwd(o, z_arr, z_blk0, g, dy_arr, dy_blk0, name):
    t_pad, w = o.shape
    tr = _row_tile(t_pad, w)
    assert (z_blk0 * HD) % w == 0 and (dy_blk0 * HD) % w == 0

    def body(o_ref, z_ref, g_ref, dy_ref, do_ref, dz_ref, dg_ref):
        @pl.when(pl.program_id(0) == 0)
        def _():
            dg_ref[...] = jnp.zeros_like(dg_ref)

        for h in range(w // HD):
            c = slice(h * HD, (h + 1) * HD)
            _, vjp = jax.vjp(_grms_fn, o_ref[:, c], z_ref[:, c], g_ref[...])
            do, dz, dg = vjp(dy_ref[:, c])
            do_ref[:, c] = do
            dz_ref[:, c] = dz.astype(BF16)
            dg_ref[...] += dg

    blk = pl.BlockSpec((tr, w), lambda i: (i, 0))
    return pl.pallas_call(
        body, name=name, grid=(t_pad // tr,),
        in_specs=[blk, pl.BlockSpec((tr, w), lambda i: (i, z_blk0 * HD // w)), pl.BlockSpec((1, HD), lambda i: (0, 0)),
                  pl.BlockSpec((tr, w), lambda i: (i, dy_blk0 * HD // w))],
        out_specs=[blk, blk, pl.BlockSpec((1, HD), lambda i: (0, 0))],
        out_shape=[jax.ShapeDtypeStruct((t_pad, w), F32), jax.ShapeDtypeStruct((t_pad, w), BF16),
                   jax.ShapeDtypeStruct((1, HD), F32)],
        compiler_params=_cparams(),
    )(o, z_arr, g, dy_arr)


def _loss_fwd(y, tgt, first_row, name):
    t_pad = y.shape[0]
    tr = _row_tile(t_pad, D)

    def body(y_ref, t_ref, l_ref, dy_ref):
        rows = pl.program_id(0) * tr + _iota((tr, 1), 0)
        err = jnp.where(rows >= first_row, y_ref[...] - t_ref[...], 0.0)
        dy_ref[...] = err * (1.0 / D)

        @pl.when(pl.program_id(0) == 0)
        def _():
            l_ref[...] = jnp.zeros_like(l_ref)

        part = jnp.sum(jnp.sum(err * err, axis=1, keepdims=True), axis=0, keepdims=True)
        l_ref[...] += jnp.broadcast_to(part * (0.5 / D), l_ref.shape)

    row = pl.BlockSpec((tr, D), lambda i: (i, 0))
    return pl.pallas_call(
        body, name=name, grid=(t_pad // tr,), in_specs=[row, row],
        out_specs=[pl.BlockSpec((8, 128), lambda i: (0, 0)), row],
        out_shape=[jax.ShapeDtypeStruct((8, 128), F32), jax.ShapeDtypeStruct((t_pad, D), F32)],
        compiler_params=_cparams(),
    )(y, tgt)


def _assemble_bf16(parts, name):
    t_pad = parts[0].shape[0]
    widths = [p.shape[1] for p in parts]
    total = sum(widths)
    tr = _row_tile(t_pad, total)

    def body(*refs):
        o_ref = refs[-1]
        off = 0
        for ref, w in zip(refs[:-1], widths):
            o_ref[:, off:off + w] = ref[...].astype(BF16)
            off += w

    return pl.pallas_call(
        body, name=name, grid=(t_pad // tr,), in_specs=[pl.BlockSpec((tr, w), lambda i: (i, 0)) for w in widths],
        out_specs=pl.BlockSpec((tr, total), lambda i: (i, 0)),
        out_shape=jax.ShapeDtypeStruct((t_pad, total), BF16), compiler_params=_cparams(),
    )(*parts)


CONV_K = 4
HALO = 8
RT = 128


def _conv_fwd(p, blk0, w, mode, pad, name):
    t_pad = p.shape[0]
    nt = t_pad // RT
    scale = HD ** -0.5 if mode == "q" else 1.0

    def body(x_ref, w_ref, y_ref, xs_ref):
        xs_ref[0:HALO, :] = jnp.zeros((HALO, HD), F32)
        rows = _iota((t_pad, 1), 0)
        xs_ref[HALO:HALO + t_pad, :] = jnp.where(rows >= pad, x_ref[...], 0.0)
        wv = w_ref[...]

        def tile(i, carry):
            r0 = pl.multiple_of(i * RT, RT)
            ext = xs_ref[pl.ds(r0, RT + HALO), :]
            acc = ext[HALO:, :] * wv[3:4, :]
            for s in (1, 2, 3):
                acc = acc + pltpu.roll(ext, s, 0)[HALO:, :] * wv[3 - s:4 - s, :]
            y = _silu(acc)
            if mode != "v":
                y = y * lax.rsqrt(jnp.sum(y * y, axis=-1, keepdims=True) + L2_EPS) * scale
            y_ref[pl.ds(r0, RT), :] = y
            return carry

        lax.fori_loop(0, nt, tile, 0)

    return pl.pallas_call(
        body, name=name, grid=(GDN_H,),
        in_specs=[pl.BlockSpec((t_pad, HD), lambda h: (0, blk0 + h)), pl.BlockSpec((CONV_K, HD), lambda h: (0, h))],
        out_specs=pl.BlockSpec((t_pad, HD), lambda h: (0, h)),
        out_shape=jax.ShapeDtypeStruct((t_pad, GDN_H * HD), F32),
        scratch_shapes=[pltpu.VMEM((t_pad + HALO, HD), F32)],
        compiler_params=_cparams(),
    )(p, w)


def _conv_bwd(p, blk0, w, dn, mode, pad, name):
    t_pad = p.shape[0]
    nt = t_pad // RT
    scale = HD ** -0.5 if mode == "q" else 1.0

    def body(x_ref, w_ref, dn_ref, dx_ref, dw_ref, xs_ref, ds_ref):
        xs_ref[0:HALO, :] = jnp.zeros((HALO, HD), F32)
        xs_ref[HALO + t_pad:HALO + t_pad + 2 * HALO, :] = jnp.zeros((2 * HALO, HD), F32)
        ds_ref[t_pad:t_pad + HALO, :] = jnp.zeros((HALO, HD), F32)
        rows = _iota((t_pad, 1), 0)
        xs_ref[HALO:HALO + t_pad, :] = jnp.where(rows >= pad, x_ref[...], 0.0)
        ds_ref[0:t_pad, :] = dn_ref[...]
        wv = w_ref[...]

        def tile(i, dw):
            r0 = pl.multiple_of(i * RT, RT)
            ext = xs_ref[pl.ds(r0, RT + 2 * HALO), :]
            dn_e = ds_ref[pl.ds(r0, RT + HALO), :]
            xsh = [ext[HALO:, :]] + [pltpu.roll(ext, s, 0)[HALO:, :] for s in (1, 2, 3)]
            pre = xsh[0] * wv[3:4, :]
            for s in (1, 2, 3):
                pre = pre + xsh[s] * wv[3 - s:4 - s, :]
            sg = _sigmoid(pre)
            y = pre * sg
            if mode != "v":
                ss = jnp.sum(y * y, axis=-1, keepdims=True) + L2_EPS
                r = lax.rsqrt(ss)
                dy = scale * (dn_e * r - y * (r * r * r) * jnp.sum(dn_e * y, axis=-1, keepdims=True))
            else:
                dy = dn_e
            dpre = dy * (sg * (1.0 + pre * (1.0 - sg)))
            dx = dpre[:RT, :] * wv[3:4, :]
            for s in (1, 2, 3):
                dx = dx + pltpu.roll(dpre, RT + HALO - s, 0)[:RT, :] * wv[3 - s:4 - s, :]
            trow = r0 + _iota((RT, 1), 0)
            dx_ref[pl.ds(r0, RT), :] = jnp.where(trow >= pad, dx, 0.0)
            new = []
            for s in (0, 1, 2, 3):
                new.append(dw[s] + jnp.sum(dpre[:RT, :] * xsh[s][:RT, :], axis=0, keepdims=True))
            return tuple(new)

        z = jnp.zeros((1, HD), F32)
        dw = lax.fori_loop(0, nt, tile, (z, z, z, z))
        for s in (0, 1, 2, 3):
            dw_ref[3 - s:4 - s, :] = dw[s]

    return pl.pallas_call(
        body, name=name, grid=(GDN_H,),
        in_specs=[pl.BlockSpec((t_pad, HD), lambda h: (0, blk0 + h)), pl.BlockSpec((CONV_K, HD), lambda h: (0, h)),
                  pl.BlockSpec((t_pad, HD), lambda h: (0, h))],
        out_specs=[pl.BlockSpec((t_pad, HD), lambda h: (0, h)), pl.BlockSpec((CONV_K, HD), lambda h: (0, h))],
        out_shape=[jax.ShapeDtypeStruct((t_pad, GDN_H * HD), F32), jax.ShapeDtypeStruct((CONV_K, GDN_H * HD), F32)],
        scratch_shapes=[pltpu.VMEM((t_pad + 3 * HALO, HD), F32), pltpu.VMEM((t_pad + HALO, HD), F32)],
        compiler_params=_cparams(),
    )(p, w, dn)


@jax.custom_vjp
def _unit_lower_inv(m, bd, eye):
    md = m * bd
    low = m - md
    p2 = mbnn(md, md)
    p4 = mbnn(p2, p2)
    dinv = mbnn(mbnn(eye - md, eye + p2), eye + p4)
    n = mbnn(dinv, low)
    n2 = mbnn(n, n)
    n4 = mbnn(n2, n2)
    return mbnn(mbnn(mbnn(eye - n, eye + n2), eye + n4), dinv)


def _unit_lower_inv_bwd(res, g):
    t, bd, eye = res
    return -mbtn(t, mbnt(g, t)), jnp.zeros_like(bd), jnp.zeros_like(eye)


def _unit_lower_inv_fwd(m, bd, eye):
    t = _unit_lower_inv(m, bd, eye)
    return t, (t, bd, eye)


_unit_lower_inv.defvjp(_unit_lower_inv_fwd, _unit_lower_inv_bwd)


def _gdn_chunks(chunks, alog, dtb, s):
    nh = chunks[0][0].shape[0]
    ri = _iota((1, CH, CH), 1)
    ci = _iota((1, CH, CH), 2)
    causal = ri >= ci
    strict = ri > ci
    eye = (ri == ci).astype(F32)
    bd = ((ri >> 3) == (ci >> 3)).astype(F32)
    ltri = (_iota((CH, CH), 0) >= _iota((CH, CH), 1)).astype(F32)
    sel = (_iota((nh, 1, HD), 2) == _iota((nh, 1, HD), 0)).astype(F32)
    last = _iota((1, CH, 1), 1) == CH - 1

    beta, gc, gc_rows = [], [], []
    for _, _, _, bb, aa, valid in chunks:
        beta_all = jnp.where(valid, _sigmoid(bb), 0.0)
        g_all = jnp.where(valid, -jnp.exp(alog) * _softplus(aa + dtb), 0.0)
        gc_all = hnn(ltri, g_all)
        beta.append(jnp.sum(beta_all[None] * sel, axis=2, keepdims=True))
        gc.append(jnp.sum(gc_all[None] * sel, axis=2, keepdims=True))
        gc_rows.append(hbnt(jnp.broadcast_to(sel, (nh, CH, HD)), jnp.broadcast_to(gc_all[None], (nh, CH, HD))))
    cat = lambda xs: jnp.concatenate(xs, axis=0)
    q, k, v = (cat([c[j] for c in chunks]) for j in range(3))
    beta, gc, gc_rows = cat(beta), cat(gc), cat(gc_rows)
    gc_last = jnp.sum(jnp.where(last, gc, 0.0), axis=1, keepdims=True)
    decay = jnp.exp(jnp.where(causal, gc - gc_rows, NEG))
    egc = jnp.exp(gc)

    kb = k * beta
    m = jnp.where(strict, bbnt(kb, k) * decay, 0.0)
    t_inv = _unit_lower_inv(m, bd, eye)
    u = bbnn(t_inv, v * beta)
    w = bbnn(t_inv, kb * egc)
    a_intra = bbnt(q, k) * decay
    q_dec = q * egc
    k_dec = k * jnp.exp(gc_last - gc)
    g_tot = jnp.exp(gc_last)

    outs = []
    for n in range(len(chunks)):
        part = lambda a: a[n * nh:(n + 1) * nh]
        v_new = part(u) - bbnn(part(w), s)
        outs.append(bbnn(part(q_dec), s) + bbnn(part(a_intra), v_new))
        s = s * part(g_tot) + bbtn(part(k_dec), v_new)
    return outs, s


PAIR = 2 * CH


def _gdn_specs(npair, rev):
    cc = (lambda c: npair - 1 - c) if rev else (lambda c: c)
    wide = pl.BlockSpec((PAIR, GDN_H * HD), lambda c: (cc(c), 0))
    fix = lambda off: pl.BlockSpec((PAIR, HD), lambda c: (cc(c), off))
    par = pl.BlockSpec((1, HD), lambda c: (0, 0))
    state = pl.BlockSpec((1, GDN_H, HD, HD), lambda c: (cc(c), 0, 0, 0))
    return wide, fix, par, state


def _store_heads(ref, a, rows=slice(None)):
    for h in range(a.shape[0]):
        ref[rows, h * HD:(h + 1) * HD] = a[h]


def _chunk_rows(half):
    return slice(half * CH, (half + 1) * CH)


def _chunk_valid(pair, half, pad):
    return ((2 * pair + half) * CH + _iota((CH, 1), 0)) >= pad


def _gdn_fwd(qn, kn, vn, p, alog, dtb, pad, name, cargo=(), exchange=None):
    t_pad = qn.shape[0]
    npair = t_pad // PAIR
    wide, fix, par, state = _gdn_specs(npair, False)
    n = len(cargo)

    def body(q_ref, k_ref, v_ref, bb_ref, aa_ref, al_ref, dt_ref, *rest):
        c = pl.program_id(0)
        s_ref = rest[-1]
        (o_ref, ss_ref), end_cargo = _cargo_bounds(rest[:-1], n, 2, exchange, c == 0, c == npair - 1)

        @pl.when(c == 0)
        def _():
            s_ref[...] = jnp.zeros_like(s_ref)

        s = s_ref[...]
        ss_ref[0] = s
        rows = [_chunk_rows(half) for half in (0, 1)]
        chunks = [(_heads(q_ref[r, :], GDN_H), _heads(k_ref[r, :], GDN_H), _heads(v_ref[r, :], GDN_H),
                   bb_ref[r, :], aa_ref[r, :], _chunk_valid(c, half, pad)) for half, r in enumerate(rows)]
        outs, s = _gdn_chunks(chunks, al_ref[...], dt_ref[...], s)
        for r, o in zip(rows, outs):
            _store_heads(o_ref, o, r)
        s_ref[...] = s
        end_cargo()

    return pl.pallas_call(
        body, name=name, grid=(npair,),
        in_specs=[wide, wide, wide, fix(16), fix(17), par, par] + [ANY] * n,
        out_specs=[wide, state] + [ANY] * n,
        out_shape=[jax.ShapeDtypeStruct((t_pad, GDN_H * HD), F32), jax.ShapeDtypeStruct((npair, GDN_H, HD, HD), F32)]
        + (exchange[1](cargo) if n else []),
        scratch_shapes=(exchange[2](n) if n else []) + [pltpu.VMEM((GDN_H, HD, HD), F32)],
        compiler_params=_cparams(),
    )(qn, kn, vn, p, p, alog, dtb, *cargo)


def _gdn_bwd(qn, kn, vn, p, alog, dtb, ssave, do, pad, name, cargo=(), exchange=None):
    t_pad = qn.shape[0]
    npair = t_pad // PAIR
    wide, fix, par, state = _gdn_specs(npair, True)
    n = len(cargo)

    def body(q_ref, k_ref, v_ref, bb_ref, aa_ref, al_ref, dt_ref, ss_ref, do_ref, *rest):
        c = pl.program_id(0)
        ds_ref = rest[-1]
        (dq_ref, dk_ref, dv_ref, dbb_ref, daa_ref, dal_ref, ddt_ref), end_cargo = _cargo_bounds(
            rest[:-1], n, 7, exchange, c == 0, c == npair - 1)

        @pl.when(c == 0)
        def _():
            ds_ref[...] = jnp.zeros_like(ds_ref)
            dal_ref[...] = jnp.zeros_like(dal_ref)
            ddt_ref[...] = jnp.zeros_like(ddt_ref)

        ra, rb = _chunk_rows(0), _chunk_rows(1)
        va, vb = _chunk_valid(npair - 1 - c, 0, pad), _chunk_valid(npair - 1 - c, 1, pad)

        def pair(qa, ka, va_, ba, aa, qb, kb, vb_, bb, ab, al, dt, s):
            (oa, ob), s = _gdn_chunks([(qa, ka, va_, ba, aa, va), (qb, kb, vb_, bb, ab, vb)], al, dt, s)
            return oa, ob, s

        ins = [f(ref[r, :]) for r in (ra, rb)
               for ref, f in ((q_ref, lambda a: _heads(a, GDN_H)), (k_ref, lambda a: _heads(a, GDN_H)),
                              (v_ref, lambda a: _heads(a, GDN_H)), (bb_ref, lambda a: a), (aa_ref, lambda a: a))]
        _, vjp = jax.vjp(pair, *ins, al_ref[...], dt_ref[...], ss_ref[0])
        g = vjp((_heads(do_ref[ra, :], GDN_H), _heads(do_ref[rb, :], GDN_H), ds_ref[...]))
        for r, (dq, dk, dv, dbb, daa) in ((ra, g[0:5]), (rb, g[5:10])):
            _store_heads(dq_ref, dq, r)
            _store_heads(dk_ref, dk, r)
            _store_heads(dv_ref, dv, r)
            dbb_ref[r, :] = dbb
            daa_ref[r, :] = daa
        dal_ref[...] += g[10]
        ddt_ref[...] += g[11]
        ds_ref[...] = g[12]
        end_cargo()

    sds = jax.ShapeDtypeStruct
    return pl.pallas_call(
        body, name=name, grid=(npair,),
        in_specs=[wide, wide, wide, fix(16), fix(17), par, par, state, wide] + [ANY] * n,
        out_specs=[wide, wide, wide, fix(0), fix(0), par, par] + [ANY] * n,
        out_shape=[sds((t_pad, GDN_H * HD), F32)] * 3 + [sds((t_pad, HD), F32)] * 2 + [sds((1, HD), F32)] * 2
        + (exchange[1](cargo) if n else []),
        scratch_shapes=(exchange[2](n) if n else []) + [pltpu.VMEM((GDN_H, HD, HD), F32)],
        compiler_params=_cparams(),
    )(qn, kn, vn, p, p, alog, dtb, ssave, do, *cargo)


SB_Q0, SB_K0, SB_V0 = 18, 22, 26
SB_SCALE = SB_DH ** -0.5
SB_NB = 8


def _sb_terms(z, allowed):
    e = jnp.exp(-jnp.abs(z))
    den = 1.0 + e
    raw = -jnp.maximum(z, 0.0) - jnp.log(den)
    l1m = raw if allowed is None else jnp.where(allowed, raw, 0.0)
    return l1m, z + raw, jnp.where(z >= 0.0, 1.0, e) / den


def _sb_passes(i, step, carry):
    total = i + 1
    sized = lambda done: [functools.partial(step, done, masked=True, nb=nb) for nb in range(1, SB_NB + 1)]

    def several(c):
        n_mid = (total - SB_NB - 1) // SB_NB
        c = step(0, c, masked=True, nb=SB_NB)
        c = lax.fori_loop(0, n_mid, lambda t, cc: step(SB_NB * (1 + t), cc, masked=False, nb=SB_NB), c)
        done = SB_NB * (1 + n_mid)
        return lax.switch(total - done - 1, sized(done), c)

    return lax.cond(total <= SB_NB, lambda c: lax.switch(total - 1, sized(0), c), several, carry)


def _sb_stack(a, i):
    first = _iota((1, HD), 1) < SB_DH
    a2 = jnp.concatenate([jnp.where(first, a, 0.0), jnp.where(first, 0.0, a)], axis=0).astype(BF16)
    rq = i * QB + _iota((QB, 1), 0)
    return a2, jnp.concatenate([rq, rq], axis=0), first


def _hi_lo(a):
    hi = a.astype(BF16)
    lo = (a - hi.astype(F32)).astype(BF16)
    return jnp.concatenate([hi, lo], axis=1)


def _cargo_bounds(refs, n, n_out, exchange, first, last):
    outs = refs[n:n + n_out]
    if not n:
        return outs, lambda: None
    ex = exchange[0](refs[:n], refs[n + n_out:2 * n + n_out], *refs[2 * n + n_out:])

    @pl.when(first)
    def _():
        ex.start()

    def finish():
        @pl.when(last)
        def _():
            ex.wait()

    return outs, finish


def _sb_fwd(p, pad, name, cargo=(), exchange=None):
    t_pad = p.shape[0]
    nq = t_pad // QB
    n = len(cargo)

    def body(q_ref, k_ref, v_ref, *rest):
        i = pl.program_id(1)
        pr = pl.program_id(0)
        (o_ref, r_ref), end_cargo = _cargo_bounds(rest, n, 2, exchange, (pr == 0) & (i == 0),
                                                  (pr == SB_H // 2 - 1) & (i == nq - 1))
        q2, rowq, first = _sb_stack(q_ref[...] * SB_SCALE, i)
        tri = (_iota((QB, QB), 0) > _iota((QB, QB), 1)).astype(BF16)
        upper2 = jnp.concatenate([tri, tri], axis=0)

        def chain(kb, masked):
            start = pl.multiple_of(kb * QB, QB)
            kblk = k_ref[pl.ds(start, QB), :].astype(BF16)
            vblk = v_ref[pl.ds(start, QB), :].astype(BF16)
            z = lax.dot_general(q2, kblk, (NT, ((), ())), preferred_element_type=F32)
            colk = kb * QB + _iota((1, QB), 1)
            al = ((colk < rowq) & (colk >= pad)) if masked else None
            l1m, ls, _ = _sb_terms(z, al)
            suf = lax.dot_general(_hi_lo(l1m), upper2, (NN, ((), ())), preferred_element_type=F32)
            return al, ls, suf, jnp.sum(l1m, axis=1, keepdims=True), vblk

        def step(done, carry, masked, nb):
            o_acc, run = carry
            ws, vs = [], []
            for n in range(nb):
                al, ls, suf, rs, vblk = chain(i - done - n, masked)
                wgt = jnp.exp(ls + suf + run)
                ws.append((wgt if al is None else jnp.where(al, wgt, 0.0)).astype(BF16))
                vs.append(vblk)
                run = run + rs
            o_acc = o_acc + lax.dot_general(jnp.concatenate(ws, axis=1), jnp.concatenate(vs, axis=0),
                                            (NN, ((), ())), preferred_element_type=F32)
            return o_acc, run

        o_acc, run = _sb_passes(i, step, (jnp.zeros((2 * QB, HD), F32), jnp.zeros((2 * QB, 1), F32)))
        o_ref[...] = jnp.where(first, o_acc[:QB], o_acc[QB:]).astype(BF16)
        r_ref[...] = jnp.where(first, run[:QB], run[QB:])
        end_cargo()

    full = lambda off: pl.BlockSpec((t_pad, HD), lambda pr, i: (0, off + pr))
    blk = pl.BlockSpec((QB, HD), lambda pr, i: (i, pr))
    return pl.pallas_call(
        body, name=name, grid=(SB_H // 2, nq),
        in_specs=[pl.BlockSpec((QB, HD), lambda pr, i: (i, SB_Q0 + pr)), full(SB_K0), full(SB_V0)] + [ANY] * n,
        out_specs=[blk, blk] + [ANY] * n,
        out_shape=[jax.ShapeDtypeStruct((t_pad, SB_H * SB_DH), BF16), jax.ShapeDtypeStruct((t_pad, SB_H * SB_DH), F32)]
        + (exchange[1](cargo) if n else []),
        scratch_shapes=exchange[2](n) if n else [],
        compiler_params=_cparams(),
    )(p, p, p, *cargo)


def _sb_bwd(p, rtot, dy, dy_blk0, pad, name, cargo=(), exchange=None):
    t_pad = p.shape[0]
    nq = t_pad // QB
    n = len(cargo)

    def body(q_ref, k_ref, v_ref, r_ref, do_ref, *rest):
        i = pl.program_id(1)
        pr = pl.program_id(0)
        dkt_ref, dvt_ref = rest[-2:]
        (dq_ref, dk_ref, dv_ref), end_cargo = _cargo_bounds(rest[:-2], n, 3, exchange, (pr == 0) & (i == 0),
                                                            (pr == SB_H // 2 - 1) & (i == nq - 1))

        @pl.when(i == 0)
        def _():
            dkt_ref[...] = jnp.zeros_like(dkt_ref)
            dvt_ref[...] = jnp.zeros_like(dvt_ref)

        q2, rowq, first = _sb_stack(q_ref[...] * SB_SCALE, i)
        do2, _, _ = _sb_stack(do_ref[...], i)
        q2t = jnp.transpose(q2.astype(F32)).astype(BF16)
        do2t = jnp.transpose(do2.astype(F32)).astype(BF16)
        rt = r_ref[...]
        lane = _iota((1, HD), 1)
        rcol = jnp.concatenate([jnp.sum(jnp.where(lane == 0, rt, 0.0), axis=1, keepdims=True),
                                jnp.sum(jnp.where(lane == SB_DH, rt, 0.0), axis=1, keepdims=True)], axis=0)
        rj = _iota((QB, QB), 0)
        cs = _iota((QB, QB), 1)
        tri_u = (rj > cs).astype(BF16)
        tri_l = (rj < cs).astype(BF16)
        upper2 = jnp.concatenate([tri_u, tri_u], axis=0)
        lower2 = jnp.concatenate([tri_l, tri_l], axis=0)

        def chain(kb, masked):
            start = pl.multiple_of(kb * QB, QB)
            kblk = k_ref[pl.ds(start, QB), :].astype(BF16)
            vblk = v_ref[pl.ds(start, QB), :].astype(BF16)
            z = lax.dot_general(q2, kblk, (NT, ((), ())), preferred_element_type=F32)
            colk = kb * QB + _iota((1, QB), 1)
            al = ((colk < rowq) & (colk >= pad)) if masked else None
            l1m, ls, sg = _sb_terms(z, al)
            dwgt = lax.dot_general(do2, vblk, (NT, ((), ())), preferred_element_type=F32)
            suf = lax.dot_general(_hi_lo(l1m), upper2, (NN, ((), ())), preferred_element_type=F32)
            return kb, kblk, al, ls, suf, jnp.sum(l1m, axis=1, keepdims=True), dwgt, sg

        def finish(c, seen, gseen):
            kb, kblk, al, ls, suf, rs, dwgt, sg = c
            wgt = jnp.exp(ls + suf + (rcol - seen - rs))
            if al is not None:
                wgt = jnp.where(al, wgt, 0.0)
            dl = dwgt * wgt
            gpre = gseen + lax.dot_general(_hi_lo(dl), lower2, (NN, ((), ())), preferred_element_type=F32)
            dz = dl - sg * (dl + gpre)
            if al is not None:
                dz = jnp.where(al, dz, 0.0)
            dz = dz.astype(BF16)
            dkt_ref[kb] += lax.dot_general(q2t, dz, (NN, ((), ())), preferred_element_type=F32)
            dvt_ref[kb] += lax.dot_general(do2t, wgt.astype(BF16), (NN, ((), ())), preferred_element_type=F32)
            return dz, seen + rs, gseen + jnp.sum(dl, axis=1, keepdims=True)

        def step(done, carry, masked, nb):
            dq_acc, seen, gseen = carry
            cs_ = [chain(done + n, masked) for n in range(nb)]
            dzs = []
            for c in cs_:
                dz, seen, gseen = finish(c, seen, gseen)
                dzs.append(dz)
            dq_acc = dq_acc + lax.dot_general(jnp.concatenate(dzs, axis=1), jnp.concatenate([c[1] for c in cs_], axis=0),
                                              (NN, ((), ())), preferred_element_type=F32)
            return dq_acc, seen, gseen

        zc = jnp.zeros((2 * QB, 1), F32)
        dq_acc, _, _ = _sb_passes(i, step, (jnp.zeros((2 * QB, HD), F32), zc, zc))
        dq_ref[...] = jnp.where(first, dq_acc[:QB], dq_acc[QB:]) * SB_SCALE

        @pl.when(i == nq - 1)
        def _():
            for kb in range(nq):
                dk_ref[kb * QB:(kb + 1) * QB, :] = jnp.transpose(dkt_ref[kb])
                dv_ref[kb * QB:(kb + 1) * QB, :] = jnp.transpose(dvt_ref[kb])

        end_cargo()

    full_in = lambda off: pl.BlockSpec((t_pad, HD), lambda pr, i: (0, off + pr))
    full_out = pl.BlockSpec((t_pad, HD), lambda pr, i: (0, pr))
    blk = pl.BlockSpec((QB, HD), lambda pr, i: (i, pr))
    sds = jax.ShapeDtypeStruct((t_pad, SB_H * SB_DH), F32)
    return pl.pallas_call(
        body, name=name, grid=(SB_H // 2, nq),
        in_specs=[pl.BlockSpec((QB, HD), lambda pr, i: (i, SB_Q0 + pr)), full_in(SB_K0), full_in(SB_V0), blk,
                  pl.BlockSpec((QB, HD), lambda pr, i: (i, dy_blk0 + pr))] + [ANY] * n,
        out_specs=[blk, full_out, full_out] + [ANY] * n,
        out_shape=[sds, sds, sds] + (exchange[1](cargo) if n else []),
        scratch_shapes=(exchange[2](n) if n else []) + [pltpu.VMEM((nq, HD, QB), F32)] * 2,
        compiler_params=_cparams(),
    )(p, p, p, rtot, dy, *cargo)


HG_LEVELS = 6


def _hg_prefix_matrix():
    t = np.arange(CH)[:, None]
    j = np.arange(CH)[None, :]
    groups = [(j <= t)]
    for lvl in range(1, HG_LEVELS + 1):
        half = CH >> lvl
        e = (t // (2 * half)) * (2 * half) + half - 1
        groups.append(j <= e)
    groups.append(np.ones((8, CH), bool))
    e = np.concatenate(groups, axis=0).astype(np.float32)
    return np.concatenate([e, e, e], axis=1), np.concatenate([e, e, e], axis=0)


HG_G = 4


def _hg_chunk(qr, fr, iv, r0, r1, st, valid, ecat):
    g = st.shape[0]
    mx = jnp.maximum(r0, r1)
    e0 = jnp.exp(r0 - mx)
    e1 = jnp.exp(r1 - mx)
    lb = e1 / (e0 + e1)
    fg = lb + (1.0 - lb) * _sigmoid(fr)
    logf = jnp.where(valid, jnp.log(fg), 0.0)
    kk = jnp.where(valid, 1.0 - fg, 0.0)
    q = jnp.where(valid, _silu(qr), 0.0)
    v = _heads(jnp.where(valid, iv, 0.0), g)

    pre = _mask_dot(ecat, logf)
    b = pre[0:CH]
    b_last = jnp.max(pre[(HG_LEVELS + 1) * CH:], axis=0, keepdims=True)
    row = _iota((CH, 1), 0)
    ri = _iota((1, CH, CH), 1)
    ci = _iota((1, CH, CH), 2)
    a = jnp.where(ri == ci, jnp.sum(_heads(q * kk, g), axis=2, keepdims=True), 0.0)
    for lvl in range(1, HG_LEVELS + 1):
        half = CH >> lvl
        m = pre[lvl * CH:(lvl + 1) * CH]
        low = (row & half) != 0
        dec = jnp.exp(jnp.where(low, b - m, m - b))
        qt = jnp.where(low, q * dec, 0.0)
        kt = jnp.where(low, 0.0, kk * dec)
        same = (ri >> (7 - lvl)) == (ci >> (7 - lvl))
        a = a + jnp.where(same, bbnt(_heads(qt, g), _heads(kt, g)), 0.0)
    o = bbnt(_heads(q * jnp.exp(b), g), st) + bbnn(a, v)
    kd = kk * jnp.exp(b_last - b)
    st_new = st * _heads(jnp.exp(b_last), g) + bbtn(v, _heads(kd, g))
    return o, st_new


def _hg_specs(npair, rev):
    cc = (lambda c: npair - 1 - c) if rev else (lambda c: c)
    ng = HG_H // HG_G
    blk = lambda off: pl.BlockSpec((PAIR, HG_G * HD), lambda h, c: (cc(c), off * ng + h))
    lbs = pl.BlockSpec((2, HG_G * HD), lambda h, c: (0, h))
    state = pl.BlockSpec((1, HG_G, HD, HD), lambda h, c: (cc(c), h, 0, 0))
    return ng, blk, lbs, state


def _hg_fwd(p, lbraw, ecat, pad, name):
    t_pad = p.shape[0]
    npair = t_pad // PAIR
    ng, blk, lbs, state = _hg_specs(npair, False)

    def body(q_ref, f_ref, i_ref, lb_ref, e_ref, et_ref, o_ref, ss_ref, s_ref):
        c = pl.program_id(1)

        @pl.when(c == 0)
        def _():
            s_ref[...] = jnp.zeros_like(s_ref)

        st = s_ref[...]
        ss_ref[0] = st
        for half in (0, 1):
            r = _chunk_rows(half)
            o, st = _hg_chunk(q_ref[r, :], f_ref[r, :], i_ref[r, :], lb_ref[0:1, :], lb_ref[1:2, :], st,
                              _chunk_valid(c, half, pad), (e_ref[...], et_ref[...]))
            _store_heads(o_ref, o, r)
        s_ref[...] = st

    return pl.pallas_call(
        body, name=name, grid=(ng, npair),
        in_specs=[blk(0), blk(1), blk(2), lbs] + [pl.BlockSpec(e.shape, lambda h, c: (0, 0)) for e in ecat],
        out_specs=[blk(0), state],
        out_shape=[jax.ShapeDtypeStruct((t_pad, HG_H * HD), F32), jax.ShapeDtypeStruct((npair, HG_H, HD, HD), F32)],
        scratch_shapes=[pltpu.VMEM((HG_G, HD, HD), F32)],
        compiler_params=_cparams(),
    )(p, p, p, lbraw, *ecat)


def _hg_bwd(p, lbraw, ecat, ssave, do, pad, name, cargo=(), exchange=None):
    t_pad = p.shape[0]
    npair = t_pad // PAIR
    ng, blk, lbs, state = _hg_specs(npair, True)
    n = len(cargo)

    def body(q_ref, f_ref, i_ref, lb_ref, e_ref, et_ref, ss_ref, do_ref, *rest):
        c = pl.program_id(1)
        hg = pl.program_id(0)
        ds_ref = rest[-1]
        (dq_ref, df_ref, di_ref, dlb_ref), end_cargo = _cargo_bounds(
            rest[:-1], n, 4, exchange, (hg == 0) & (c == 0), (hg == ng - 1) & (c == npair - 1))

        @pl.when(c == 0)
        def _():
            ds_ref[...] = jnp.zeros_like(ds_ref)
            dlb_ref[...] = jnp.zeros_like(dlb_ref)

        ra, rb = _chunk_rows(0), _chunk_rows(1)
        va, vb = _chunk_valid(npair - 1 - c, 0, pad), _chunk_valid(npair - 1 - c, 1, pad)
        ecv = (e_ref[...], et_ref[...])

        def pair(qa, fa, ia, qb, fb, ib, r0, r1, st):
            oa, st = _hg_chunk(qa, fa, ia, r0, r1, st, va, ecv)
            ob, st = _hg_chunk(qb, fb, ib, r0, r1, st, vb, ecv)
            return oa, ob, st

        ins = [ref[r, :] for r in (ra, rb) for ref in (q_ref, f_ref, i_ref)]
        _, vjp = jax.vjp(pair, *ins, lb_ref[0:1, :], lb_ref[1:2, :], ss_ref[0])
        g = vjp((_heads(do_ref[ra, :], HG_G), _heads(do_ref[rb, :], HG_G), ds_ref[...]))
        for r, (dq, df, di) in ((ra, g[0:3]), (rb, g[3:6])):
            dq_ref[r, :] = dq.astype(BF16)
            df_ref[r, :] = df.astype(BF16)
            di_ref[r, :] = di.astype(BF16)
        dlb_ref[0:1, :] += g[6]
        dlb_ref[1:2, :] += g[7]
        ds_ref[...] = g[8]
        end_cargo()

    sds = jax.ShapeDtypeStruct((t_pad, HG_H * HD), BF16)
    return pl.pallas_call(
        body, name=name, grid=(ng, npair),
        in_specs=[blk(0), blk(1), blk(2), lbs] + [pl.BlockSpec(e.shape, lambda h, c: (0, 0)) for e in ecat]
        + [state, blk(0)] + [ANY] * n,
        out_specs=[blk(0), blk(0), blk(0), lbs] + [ANY] * n,
        out_shape=[sds, sds, sds, jax.ShapeDtypeStruct((2, HG_H * HD), F32)] + (exchange[1](cargo) if n else []),
        scratch_shapes=(exchange[2](n) if n else []) + [pltpu.VMEM((HG_G, HD, HD), F32)],
        compiler_params=_cparams(),
    )(p, p, p, lbraw, *ecat, ssave, do, *cargo)


def _pad_ab_cols(w):
    z = jnp.zeros((w.shape[0], HD - GDN_H), w.dtype)
    return jnp.concatenate([w[:, :2048], w[:, 2048:2052], z, w[:, 2052:2056], z, w[:, 2056:]], axis=1)


def _unpad_ab_cols(w):
    return jnp.concatenate([w[:, :2048], w[:, 2048:2052], w[:, 2176:2180], w[:, 2304:]], axis=1)


def _lane_pad(v):
    return jnp.pad(v, ((0, 0), (0, HD - v.shape[1])))


def _mlp_fwd(hb, w1, w2, layer):
    a, r = _mm(hb, w1, b_view=("cols", layer), out_dtype=BF16, act=True, name=f"mlp_up_{layer}")
    m = _mm(r, w2, b_view=("rows", layer), name=f"mlp_down_{layer}")
    return a, r, m


def _mlp_bwd(hb, a, r, dmb, w1, w2, layer):
    da = _mm(dmb, w2, tb=True, b_view=("rows", layer), out_dtype=BF16, gate=a, name=f"mlp_down_dx_{layer}")
    dw2 = _mm(r, dmb, ta=True, out_dtype=BF16, name=f"mlp_down_dw_{layer}")
    dh = _mm(da, w1, tb=True, b_view=("cols", layer), name=f"mlp_up_dx_{layer}")
    dw1 = _mm(hb, da, ta=True, out_dtype=BF16, out_split=N_CHIP, name=f"mlp_up_dw_{layer}")
    return dh, dw1, dw2


def _local_step(h0, tgt, w, pad, late=None):
    row = lambda a, i: a[i:i + 1]
    ecat = tuple(jnp.asarray(e, dtype=BF16) for e in _hg_prefix_matrix())
    cw = [w["conv_w"][:, i * 512:(i + 1) * 512] for i in range(3)]
    alog, dtb = _lane_pad(w["a_log"]), _lane_pad(w["dt_bias"])

    h0b = h0.astype(BF16)
    p0 = _mm(h0b, w["ab_w_in"], name="ab_in")
    qn = _conv_fwd(p0, 0, cw[0], "q", pad, "conv_q")
    kn = _conv_fwd(p0, 4, cw[1], "k", pad, "conv_k")
    vn = _conv_fwd(p0, 8, cw[2], "v", pad, "conv_v")
    if late is None:
        oa_raw, ss0 = _gdn_fwd(qn, kn, vn, p0, alog, dtb, pad, "gdn_fwd")
        ob, rtot = _sb_fwd(p0, pad, "sb_fwd")
    else:
        oa_raw, ss0, g_cin, g_cout = _gdn_fwd(qn, kn, vn, p0, alog, dtb, pad, "gdn_fwd",
                                              cargo=[late["c_w_in"], late["c_w_out"]], exchange=GATHER)
        ob, rtot, g_about, g_w1, g_w2 = _sb_fwd(p0, pad, "sb_fwd", exchange=GATHER,
                                                cargo=[late["ab_w_out"], late["mlp_w1"], late["mlp_w2"]])
        w = dict(w, ab_w_out=g_about.reshape(D, D), c_w_in=g_cin, c_w_out=g_cout.reshape(D, D), mlp_w1=g_w1, mlp_w2=g_w2)
    oa = _grms_fwd(oa_raw, p0, 12, w["ab_gnorm_g"], "gdn_gate")
    ycat = jnp.concatenate([oa, ob], axis=1)
    mix0 = _mm(ycat, w["ab_w_out"], name="ab_out")
    h1, h1b = _ln_res_fwd(h0, mix0, row(w["ln_mix_g"], 0), row(w["ln_mix_b"], 0), "ln_mix_0")
    a0, r0, m0 = _mlp_fwd(h1b, w["mlp_w1"], w["mlp_w2"], 0)
    h2, h2b = _ln_res_fwd(h1, m0, row(w["ln_ffn_g"], 0), row(w["ln_ffn_b"], 0), "ln_ffn_0")
    p1 = _mm(h2b, w["c_w_in"], b_view=("cols", 0), name="c_in")
    oc_raw, ss1 = _hg_fwd(p1, w["c_lb_raw"], ecat, pad, "hg_fwd")
    yc = _grms_fwd(oc_raw, p1, 3 * HG_H, w["c_gnorm_g"], "hg_gate")
    mix1 = _mm(yc, w["c_w_out"], name="c_out")
    h3, h3b = _ln_res_fwd(h2, mix1, row(w["ln_mix_g"], 1), row(w["ln_mix_b"], 1), "ln_mix_1")
    a1, r1, m1 = _mlp_fwd(h3b, w["mlp_w1"], w["mlp_w2"], 1)
    h4, _ = _ln_res_fwd(h3, m1, row(w["ln_ffn_g"], 1), row(w["ln_ffn_b"], 1), "ln_ffn_1")
    loss, dh4 = _loss_fwd(h4, tgt, pad + N_META, "loss")

    dh3a, dm1b, dfg1, dfb1 = _ln_res_bwd(h3, m1, row(w["ln_ffn_g"], 1), row(w["ln_ffn_b"], 1), [dh4], "ln_ffn_bwd_1")
    dh3b, dw1_1, dw2_1 = _mlp_bwd(h3b, a1, r1, dm1b, w["mlp_w1"], w["mlp_w2"], 1)
    dh2a, dmix1b, dmg1, dmb1 = _ln_res_bwd(h2, mix1, row(w["ln_mix_g"], 1), row(w["ln_mix_b"], 1), [dh3a, dh3b], "ln_mix_bwd_1")
    dyc = _mm(dmix1b, w["c_w_out"], tb=True, name="c_out_dx")
    dwco = _mm(yc, dmix1b, ta=True, out_dtype=BF16, name="c_out_dw")
    doc, dzc, dcg = _grms_bwd(oc_raw, p1, 3 * HG_H, w["c_gnorm_g"], dyc, 0, "hg_gate_bwd")
    landed = {}
    rows4 = lambda a: a.reshape(N_CHIP, -1, D)
    if late is None:
        dq1, df1, di1, dlb = _hg_bwd(p1, w["c_lb_raw"], ecat, ss1, doc, pad, "hg_bwd")
    else:
        dq1, df1, di1, dlb, landed["w1_1"] = _hg_bwd(
            p1, w["c_lb_raw"], ecat, ss1, doc, pad, "hg_bwd", cargo=[dw1_1], exchange=SCATTER)
    dp1 = [dq1, df1, di1, dzc]
    dh2b = _mm_groups_nt(dp1, w["c_w_in"], "c_in_dx")
    dwc = jnp.stack([_mm(h2b, d, ta=True, out_dtype=BF16, name=f"c_in_dw_{i}") for i, d in enumerate(dp1)])
    dh1a, dm0b, dfg0, dfb0 = _ln_res_bwd(h1, m0, row(w["ln_ffn_g"], 0), row(w["ln_ffn_b"], 0), [dh2a, dh2b], "ln_ffn_bwd_0")
    dh1b, dw1_0, dw2_0 = _mlp_bwd(h1b, a0, r0, dm0b, w["mlp_w1"], w["mlp_w2"], 0)
    dh0a, dmix0b, dmg0, dmb0 = _ln_res_bwd(h0, mix0, row(w["ln_mix_g"], 0), row(w["ln_mix_b"], 0), [dh1a, dh1b], "ln_mix_bwd_0")
    dycat = _mm(dmix0b, w["ab_w_out"], tb=True, name="ab_out_dx")
    dwabo = _mm(ycat, dmix0b, ta=True, out_dtype=BF16, name="ab_out_dw")
    doa, dza, dag = _grms_bwd(oa_raw, p0, 12, w["ab_gnorm_g"], dycat, 0, "gdn_gate_bwd")
    if late is None:
        dqn, dkn, dvn, dbb, daa, dal, ddt = _gdn_bwd(qn, kn, vn, p0, alog, dtb, ss0, doa, pad, "gdn_bwd")
        dqb, dkb, dvb = _sb_bwd(p0, rtot, dycat, 4, pad, "sb_bwd")
    else:
        dqn, dkn, dvn, dbb, daa, dal, ddt, landed["c_w_in"] = _gdn_bwd(
            qn, kn, vn, p0, alog, dtb, ss0, doa, pad, "gdn_bwd", cargo=[dwc], exchange=SCATTER)
        (dqb, dkb, dvb, landed["w1_0"], landed["w2_0"], landed["w2_1"], landed["ab_w_out"],
         landed["c_w_out"]) = _sb_bwd(
            p0, rtot, dycat, 4, pad, "sb_bwd",
            cargo=[dw1_0, rows4(dw2_0), rows4(dw2_1), rows4(dwabo), rows4(dwco)], exchange=SCATTER)
    dpq, dcq = _conv_bwd(p0, 0, cw[0], dqn, "q", pad, "conv_q_bwd")
    dpk, dck = _conv_bwd(p0, 4, cw[1], dkn, "k", pad, "conv_k_bwd")
    dpv, dcv = _conv_bwd(p0, 8, cw[2], dvn, "v", pad, "conv_v_bwd")
    dp0 = _assemble_bf16([dpq, dpk, dpv, dza, dbb, daa, dqb, dkb, dvb], "ab_in_dy")
    dwab = _mm(h0b, dp0, ta=True, out_dtype=BF16, name="ab_in_dw")
    if late is None:
        dh0 = _mm(dp0, w["ab_w_in"], tb=True, plus=dh0a, name="ab_in_dx")
    else:
        dab = jnp.transpose(_unpad_ab_cols(dwab).reshape(D, N_CHIP, AB_TRUE // N_CHIP), (1, 0, 2))
        dh0, landed["ab_w_in"] = _mm(dp0, w["ab_w_in"], tb=True, plus=dh0a, name="ab_in_dx", cargo=[dab],
                                     exchange=SCATTER)

    grads = {
        "ab_w_in": dwab, "conv_w": jnp.concatenate([dcq, dck, dcv], axis=1),
        "a_log": dal[:, :GDN_H], "dt_bias": ddt[:, :GDN_H],
        "ab_gnorm_g": dag, "ab_w_out": dwabo, "c_w_in": dwc, "c_lb_raw": dlb, "c_gnorm_g": dcg, "c_w_out": dwco,
        "ln_mix_g": jnp.concatenate([dmg0, dmg1], 0), "ln_mix_b": jnp.concatenate([dmb0, dmb1], 0),
        "w1_0": dw1_0, "w1_1": dw1_1, "w2_0": dw2_0, "w2_1": dw2_1,
        "ln_ffn_g": jnp.concatenate([dfg0, dfg1], 0), "ln_ffn_b": jnp.concatenate([dfb0, dfb1], 0),
        "landed": landed,
    }
    return loss, dh0, grads


MESH = pl.DeviceIdType.MESH
ANY = pl.BlockSpec(memory_space=pl.ANY)
N_CHIP = 4
N_DEV = 8
CHIP_REL = ((1, 0), (0, 1), (1, 1))
DEV_REL = tuple((dx, dy, dc) for dx in (0, 1) for dy in (0, 1) for dc in (0, 1))[1:]

def _pos():
    return lax.axis_index("x"), lax.axis_index("y"), lax.axis_index("c")


def _flip(a, d):
    return a + d - 2 * a * d


class _Exchange:
    def __init__(self, local, sends, recvs):
        self.local, self.sends, self.recvs = local, sends, recvs

    def start(self):
        for cp in self.local + self.sends:
            cp.start()

    def wait(self):
        for cp in self.recvs:
            cp.wait_recv()
        for cp in self.sends:
            cp.wait_send()
        for cp in self.local:
            cp.wait()


def _gather_sems(n):
    return [pltpu.SemaphoreType.DMA((3 * n,)), pltpu.SemaphoreType.DMA((3 * n,)), pltpu.SemaphoreType.DMA((n,))]


def _gather_copies(x_refs, o_refs, send_sems, recv_sems, local_sems):
    n = len(x_refs)
    x, y, c = _pos()
    local = [pltpu.make_async_copy(x_refs[a], o_refs[a].at[2 * x + y], local_sems.at[a]) for a in range(n)]

    def copy(a, k, sending):
        tx, ty = _flip(x, CHIP_REL[k][0]), _flip(y, CHIP_REL[k][1])
        return pltpu.make_async_remote_copy(
            src_ref=x_refs[a], dst_ref=o_refs[a].at[2 * x + y if sending else 2 * tx + ty],
            send_sem=send_sems.at[3 * a + k], recv_sem=recv_sems.at[3 * a + k], device_id=(tx, ty, c), device_id_type=MESH)

    pairs = [(a, k) for a in range(n) for k in range(3)]
    return _Exchange(local, [copy(a, k, True) for a, k in pairs], [copy(a, k, False) for a, k in pairs])


def _gather_shapes(bufs):
    return [jax.ShapeDtypeStruct((N_CHIP,) + b.shape, b.dtype) for b in bufs]


def _chip_allgather(bufs, name):
    n = len(bufs)

    def body(*refs):
        ex = _gather_copies(refs[:n], refs[n:2 * n], *refs[2 * n:])
        ex.start()
        ex.wait()

    return pl.pallas_call(
        body, name=name, in_specs=[ANY] * n, out_specs=[ANY] * n, out_shape=_gather_shapes(bufs),
        scratch_shapes=_gather_sems(n), compiler_params=pltpu.CompilerParams(has_side_effects=True),
    )(*bufs)


def _scatter_sems(n):
    nr = N_DEV - 1
    return [pltpu.SemaphoreType.DMA((nr * n,)), pltpu.SemaphoreType.DMA((nr * n,)), pltpu.SemaphoreType.DMA((n,))]


def _scatter_copies(g_refs, o_refs, send_sems, recv_sems, local_sems):
    n = len(g_refs)
    nr = N_DEV - 1
    x, y, c = _pos()
    me = 4 * x + 2 * y + c
    local = [pltpu.make_async_copy(g_refs[a].at[2 * x + y], o_refs[a].at[me], local_sems.at[a]) for a in range(n)]

    def copy(a, k, sending):
        dx, dy, dc = DEV_REL[k]
        tx, ty, tc = _flip(x, dx), _flip(y, dy), _flip(c, dc)
        return pltpu.make_async_remote_copy(
            src_ref=g_refs[a].at[2 * tx + ty], dst_ref=o_refs[a].at[me if sending else 4 * tx + 2 * ty + tc],
            send_sem=send_sems.at[nr * a + k], recv_sem=recv_sems.at[nr * a + k],
            device_id=(tx, ty, tc), device_id_type=MESH)

    pairs = [(a, k) for a in range(n) for k in range(nr)]
    return _Exchange(local, [copy(a, k, True) for a, k in pairs], [copy(a, k, False) for a, k in pairs])


def _scatter_shapes(gs):
    return [jax.ShapeDtypeStruct((N_DEV,) + g.shape[1:], g.dtype) for g in gs]


GATHER = (_gather_copies, _gather_shapes, _gather_sems)
SCATTER = (_scatter_copies, _scatter_shapes, _scatter_sems)


def _sum_slots(r, name):
    n, rh, w = r.shape
    tr = _pick(rh, (256, 128, 64, 16))

    def body(r_ref, o_ref):
        acc = r_ref[0].astype(F32)
        for s in range(1, n):
            acc = acc + r_ref[s].astype(F32)
        o_ref[...] = acc

    return pl.pallas_call(
        body, name=name, grid=(rh // tr,), in_specs=[pl.BlockSpec((n, tr, w), lambda i: (0, i, 0))],
        out_specs=pl.BlockSpec((tr, w), lambda i: (i, 0)), out_shape=jax.ShapeDtypeStruct((rh, w), F32),
        compiler_params=_cparams(),
    )(r)


def _small_allreduce(buf, name):
    r, w = buf.shape

    def body(b_ref, o_ref, land_ref, send_sems, recv_sems):
        x, y, c = _pos()
        me = 4 * x + 2 * y + c
        land_ref[me] = b_ref[...]

        def target(k):
            dx, dy, dc = DEV_REL[k]
            return _flip(x, dx), _flip(y, dy), _flip(c, dc)

        sends = []
        for k in range(N_DEV - 1):
            tx, ty, tc = target(k)
            cp = pltpu.make_async_remote_copy(
                src_ref=b_ref, dst_ref=land_ref.at[me], send_sem=send_sems.at[k], recv_sem=recv_sems.at[k],
                device_id=(tx, ty, tc), device_id_type=MESH)
            cp.start()
            sends.append(cp)
        for k in range(N_DEV - 1):
            tx, ty, tc = target(k)
            pltpu.make_async_remote_copy(
                src_ref=b_ref, dst_ref=land_ref.at[4 * tx + 2 * ty + tc], send_sem=send_sems.at[k],
                recv_sem=recv_sems.at[k], device_id=(tx, ty, tc), device_id_type=MESH).wait_recv()
        for cp in sends:
            cp.wait_send()
        acc = land_ref[0]
        for s in range(1, N_DEV):
            acc = acc + land_ref[s]
        o_ref[...] = acc

    vm = pl.BlockSpec(memory_space=pltpu.VMEM)
    return pl.pallas_call(
        body, name=name, in_specs=[vm], out_specs=vm, out_shape=jax.ShapeDtypeStruct((r, w), F32),
        scratch_shapes=[pltpu.VMEM((N_DEV, r, w), F32), pltpu.SemaphoreType.DMA((N_DEV - 1,)),
                        pltpu.SemaphoreType.DMA((N_DEV - 1,))],
        compiler_params=pltpu.CompilerParams(has_side_effects=True),
    )(buf)


def _adamw(w, g, m, v, name):
    r, c = w.shape
    tr = _pick(r, (256, 128, 64, 8)) if r * c > (1 << 18) else r

    def body(w_ref, g_ref, m_ref, v_ref, d_ref, m2_ref, v2_ref):
        gg = g_ref[...]
        m2 = ADAM_B1 * m_ref[...] + (1.0 - ADAM_B1) * gg
        v2 = ADAM_B2 * v_ref[...] + (1.0 - ADAM_B2) * (gg * gg)
        m_hat = m2 / (1.0 - ADAM_B1 ** ADAM_STEP)
        v_hat = v2 / (1.0 - ADAM_B2 ** ADAM_STEP)
        d_ref[...] = -ADAM_LR * (m_hat / (jnp.sqrt(v_hat) + ADAM_EPS) + ADAM_WD * w_ref[...])
        m2_ref[...] = m2
        v2_ref[...] = v2

    blk = pl.BlockSpec((tr, c), lambda i: (i, 0))
    sds = jax.ShapeDtypeStruct((r, c), F32)
    return pl.pallas_call(body, name=name, grid=(r // tr,), in_specs=[blk] * 4, out_specs=[blk] * 3,
                          out_shape=[sds] * 3, compiler_params=_cparams())(w, g, m, v)


BIG = ("ab_w_in", "ab_w_out", "c_w_in", "c_w_out", "mlp_w1", "mlp_w2")
SMALL = ("ln_mix_g", "ln_mix_b", "ln_ffn_g", "ln_ffn_b", "c_lb_raw", "ab_a_log", "ab_dt_bias", "ab_gnorm_g", "c_gnorm_g")
SMALL_ROWS = 16
CONV_ROWS = 8
CONV_W = 3 * GDN_H * HD


def _conv_to_rows(cw):
    return jnp.pad(cw, ((0, 0), (0, 2 * D - CONV_W))).reshape(CONV_ROWS, D)


def _rows_to_conv(rows):
    return rows.reshape(CONV_K, 2 * D)[:, :CONV_W]


def _pack_small(d):
    rows = [jnp.pad(d[n], ((0, 0), (0, D - d[n].shape[1]))) for n in SMALL]
    buf = jnp.concatenate(rows, axis=0)
    return jnp.pad(buf, ((0, SMALL_ROWS - buf.shape[0]), (0, 0)))


def _unpack_small(buf, like):
    out, r = {}, 0
    for n in SMALL:
        nr, nc = like[n].shape
        out[n] = buf[r:r + nr, :nc]
        r += nr
    return out


def kernel(x, meta_tokens, ab_w_in, ab_conv_w, ab_a_log, ab_dt_bias, ab_gnorm_g, ab_w_out, c_w_in, c_lb_raw, c_gnorm_g, c_w_out, ln_mix_g, ln_mix_b, mlp_w1, mlp_w2, ln_ffn_g, ln_ffn_b, loss_target, m_meta_tokens, m_ab_w_in, m_ab_conv_w, m_ab_a_log, m_ab_dt_bias, m_ab_gnorm_g, m_ab_w_out, m_c_w_in, m_c_lb_raw, m_c_gnorm_g, m_c_w_out, m_ln_mix_g, m_ln_mix_b, m_mlp_w1, m_mlp_w2, m_ln_ffn_g, m_ln_ffn_b, v_meta_tokens, v_ab_w_in, v_ab_conv_w, v_ab_a_log, v_ab_dt_bias, v_ab_gnorm_g, v_ab_w_out, v_c_w_in, v_c_lb_raw, v_c_gnorm_g, v_c_w_out, v_ln_mix_g, v_ln_mix_b, v_mlp_w1, v_mlp_w2, v_ln_ffn_g, v_ln_ffn_b):
    names = ("meta_tokens", "ab_w_in", "ab_conv_w", "ab_a_log", "ab_dt_bias", "ab_gnorm_g", "ab_w_out", "c_w_in",
             "c_lb_raw", "c_gnorm_g", "c_w_out", "ln_mix_g", "ln_mix_b", "mlp_w1", "mlp_w2", "ln_ffn_g", "ln_ffn_b")
    wts = dict(zip(names, (meta_tokens, ab_w_in, ab_conv_w, ab_a_log, ab_dt_bias, ab_gnorm_g, ab_w_out, c_w_in, c_lb_raw,
                           c_gnorm_g, c_w_out, ln_mix_g, ln_mix_b, mlp_w1, mlp_w2, ln_ffn_g, ln_ffn_b)))
    mom_m = dict(zip(names, (m_meta_tokens, m_ab_w_in, m_ab_conv_w, m_ab_a_log, m_ab_dt_bias, m_ab_gnorm_g, m_ab_w_out,
                             m_c_w_in, m_c_lb_raw, m_c_gnorm_g, m_c_w_out, m_ln_mix_g, m_ln_mix_b, m_mlp_w1, m_mlp_w2,
                             m_ln_ffn_g, m_ln_ffn_b)))
    mom_v = dict(zip(names, (v_meta_tokens, v_ab_w_in, v_ab_conv_w, v_ab_a_log, v_ab_dt_bias, v_ab_gnorm_g, v_ab_w_out,
                             v_c_w_in, v_c_lb_raw, v_c_gnorm_g, v_c_w_out, v_ln_mix_g, v_ln_mix_b, v_mlp_w1, v_mlp_w2,
                             v_ln_ffn_g, v_ln_ffn_b)))
    seq = x.shape[1]
    pad = (-(N_META + seq)) % QB
    xi, yi, ci = _pos()
    chip = 2 * xi + yi

    gat_ab_in, = _chip_allgather([ab_w_in[0].astype(BF16)], "gather_weights")
    late = {"ab_w_out": ab_w_out[0].astype(BF16), "c_w_in": c_w_in.astype(BF16), "c_w_out": c_w_out[0].astype(BF16),
            "mlp_w1": mlp_w1.astype(BF16), "mlp_w2": mlp_w2.astype(BF16)}
    mcols, ccols = meta_tokens.shape[1], ab_conv_w.shape[2]
    place = jnp.concatenate([
        lax.dynamic_update_slice(jnp.zeros((N_META, D), F32), 0.5 * meta_tokens, (0, chip * mcols)),
        _conv_to_rows(lax.dynamic_update_slice(jnp.zeros((CONV_K, CONV_W), F32), 0.5 * ab_conv_w[0], (0, chip * ccols)))],
        axis=0)
    placed = _small_allreduce(place, "gather_meta")
    meta_full = placed[:N_META]

    w = {
        "ab_w_in": _pad_ab_cols(jnp.transpose(gat_ab_in, (1, 0, 2)).reshape(D, AB_TRUE)),
        "conv_w": _rows_to_conv(placed[N_META:]), "a_log": ab_a_log, "dt_bias": ab_dt_bias,
        "ab_gnorm_g": ab_gnorm_g, "c_lb_raw": c_lb_raw,
        "c_gnorm_g": c_gnorm_g, "ln_mix_g": ln_mix_g, "ln_mix_b": ln_mix_b, "ln_ffn_g": ln_ffn_g, "ln_ffn_b": ln_ffn_b,
    }

    h0 = jnp.concatenate([jnp.zeros((pad, D), F32), meta_full, x[0]], axis=0)
    tgt = jnp.concatenate([jnp.zeros((pad + N_META, D), F32), loss_target[0]], axis=0)
    loss8, dh0, g = _local_step(h0, tgt, w, pad, late)
    loss = lax.psum(loss8[0, 0], ("x", "y", "c"))
    grad_x = dh0[pad + N_META:][None]

    gsmall = {"ln_mix_g": g["ln_mix_g"], "ln_mix_b": g["ln_mix_b"], "ln_ffn_g": g["ln_ffn_g"], "ln_ffn_b": g["ln_ffn_b"],
              "c_lb_raw": g["c_lb_raw"], "ab_a_log": g["a_log"], "ab_dt_bias": g["dt_bias"], "ab_gnorm_g": g["ab_gnorm_g"],
              "c_gnorm_g": g["c_gnorm_g"]}
    sbuf = jnp.concatenate([_pack_small(gsmall), dh0[pad:pad + N_META], _conv_to_rows(g["conv_w"])], axis=0)
    ssum = _small_allreduce(sbuf, "allreduce_small")
    grads = _unpack_small(ssum[:SMALL_ROWS], wts)
    grads["meta_tokens"] = lax.dynamic_slice(ssum[SMALL_ROWS:SMALL_ROWS + N_META], (0, chip * mcols), (N_META, mcols))
    grads["ab_conv_w"] = lax.dynamic_slice(_rows_to_conv(ssum[SMALL_ROWS + N_META:]), (0, chip * ccols), (CONV_K, ccols))[None]

    sums = {k: _sum_slots(v, f"grad_sum_{k}") for k, v in g["landed"].items()}
    for n in ("ab_w_in", "ab_w_out", "c_w_in", "c_w_out"):
        grads[n] = sums[n][None]
    grads["mlp_w1"] = jnp.stack([sums["w1_0"], sums["w1_1"]])
    grads["mlp_w2"] = jnp.stack([sums["w2_0"], sums["w2_1"]])

    delta, new_m, new_v = {}, {}, {}
    for n in ("meta_tokens", "ab_conv_w") + BIG:
        shp = wts[n].shape
        to2 = lambda a: a.reshape(-1, shp[-1])
        d2, m2, v2 = _adamw(to2(wts[n]), to2(grads[n]), to2(mom_m[n]), to2(mom_v[n]), f"adamw_{n}")
        delta[n], new_m[n], new_v[n] = d2.reshape(shp), m2.reshape(shp), v2.reshape(shp)
    d2, m2, v2 = _adamw(_pack_small(wts), ssum[:SMALL_ROWS], _pack_small(mom_m), _pack_small(mom_v), "adamw_small")
    delta.update(_unpack_small(d2, wts))
    new_m.update(_unpack_small(m2, wts))
    new_v.update(_unpack_small(v2, wts))

    return (loss, grad_x, *[grads[n] for n in names], *[delta[n] for n in names], *[new_m[n] for n in names],
            *[new_v[n] for n in names])
```

```python
import functools

import numpy as np
import jax
import jax.numpy as jnp
from jax import lax
from jax.experimental import pallas as pl
from jax.experimental.pallas import tpu as pltpu

F32 = jnp.float32
BF16 = jnp.bfloat16

D = 1024
N_META = 16
DEPTH = 2
GDN_H = 4
SB_H = 8
SB_DH = 64
HG_H = 8
HD = 128
CH = 64
QB = 128
ALPHA = float((2 * DEPTH) ** 0.25)
LN_EPS = 1e-5
RMS_EPS = 1e-6
L2_EPS = 1e-6
NEG = -1e30

ADAM_LR = 0.001
ADAM_B1 = 0.9
ADAM_B2 = 0.999
ADAM_EPS = 1e-08
ADAM_WD = 0.01
ADAM_STEP = 10

AB_TRUE = 3592
V7X_VMEM_BYTES = 64 * 1024 * 1024
VMEM_LIMIT = V7X_VMEM_BYTES - 8 * 1024 * 1024

NN = ((1,), (0,))
NT = ((1,), (1,))
TN = ((0,), (0,))


def _cparams(**kw):
    return pltpu.CompilerParams(vmem_limit_bytes=VMEM_LIMIT, **kw)


def _dg(a, b, dims, mode):
    if mode == "h":
        return lax.dot_general(a, b, dims, precision=lax.Precision.HIGHEST, preferred_element_type=F32)
    if mode == "b":
        return lax.dot_general(a.astype(BF16), b.astype(BF16), dims, preferred_element_type=F32)
    ah, bh = a.astype(BF16), b.astype(BF16)
    al, bl = (a - ah.astype(F32)).astype(BF16), (b - bh.astype(F32)).astype(BF16)
    d = lambda x, y: lax.dot_general(x, y, dims, preferred_element_type=F32)
    return d(ah, bh) + (d(ah, bl) + d(al, bh))


def _make_dots(mode, batched=False):
    if batched:
        nn_d, nt_d, tn_d = (((2,), (1,)), ((0,), (0,))), (((2,), (2,)), ((0,), (0,))), (((1,), (1,)), ((0,), (0,)))
    else:
        nn_d, nt_d, tn_d = (NN, ((), ())), (NT, ((), ())), (TN, ((), ()))

    @jax.custom_vjp
    def nn(a, b):
        return _dg(a, b, nn_d, mode)

    @jax.custom_vjp
    def nt(a, b):
        return _dg(a, b, nt_d, mode)

    @jax.custom_vjp
    def tn(a, b):
        return _dg(a, b, tn_d, mode)

    nn.defvjp(lambda a, b: (nn(a, b), (a, b)), lambda r, g: (nt(g, r[1]), tn(r[0], g)))
    nt.defvjp(lambda a, b: (nt(a, b), (a, b)), lambda r, g: (nn(g, r[1]), tn(g, r[0])))
    tn.defvjp(lambda a, b: (tn(a, b), (a, b)), lambda r, g: (nt(r[1], g), nn(r[0], g)))
    return nn, nt, tn


hnn = _make_dots("h")[0]
bbnn, bbnt, bbtn = _make_dots("b", True)
mbnn, mbnt, mbtn = _make_dots("m", True)
hbnt = _make_dots("h", True)[1]


def _split3(x, axis):
    x1 = x.astype(BF16)
    r1 = x - x1.astype(F32)
    x2 = r1.astype(BF16)
    x3 = (r1 - x2.astype(F32)).astype(BF16)
    return jnp.concatenate([x1, x2, x3], axis=axis)


@jax.custom_vjp
def _mask_dot(e3, x):
    return lax.dot_general(e3[0], _split3(x, 0), (NN, ((), ())), preferred_element_type=F32)


def _mask_dot_bwd(e3, g):
    dx = lax.dot_general(e3[1], _split3(g, 0), (TN, ((), ())), preferred_element_type=F32)
    return (jnp.zeros_like(e3[0]), jnp.zeros_like(e3[1])), dx


_mask_dot.defvjp(lambda e3, x: (_mask_dot(e3, x), e3), _mask_dot_bwd)


def _heads(a, n):
    return jnp.concatenate([a[None, :, h * HD:(h + 1) * HD] for h in range(n)], axis=0)


def _sigmoid(x):
    return jax.nn.sigmoid(x)


def _silu(x):
    return x * jax.nn.sigmoid(x)


def _softplus(x):
    return jnp.maximum(x, 0.0) + jnp.log(1.0 + jnp.exp(-jnp.abs(x)))


def _iota(shape, dim):
    return lax.broadcasted_iota(jnp.int32, shape, dim)


def _pick(n, prefs):
    for p in prefs:
        if n % p == 0:
            return p
    return n


def _mm(a, b, *, ta=False, tb=False, out_dtype=F32, name, b_view=None, out_split=0, act=False, gate=None, plus=None,
        cargo=(), exchange=None):
    if ta:
        k_dim, m_dim = a.shape
    else:
        m_dim, k_dim = a.shape
    if b_view is None:
        w_rows, w_cols = b.shape
    else:
        kind, layer = b_view
        nj, _, blk_r, blk_c = b.shape
        w_rows, w_cols = (blk_r, nj * blk_c) if kind == "cols" else (nj * blk_r, blk_c)
    n_dim = w_rows if tb else w_cols
    assert (w_cols if tb else w_rows) == k_dim
    tm = _pick(m_dim, (1024, 1056, 704, 640, 512, 384, 256, 128))
    tn = _pick(n_dim, (1024, 1056, 704, 640, 512, 384, 256, 128))
    tk = _pick(k_dim, (1024, 1056, 704, 512, 384, 256, 128))
    nk = k_dim // tk
    a_spec = pl.BlockSpec((tk, tm), lambda i, j, k: (k, i)) if ta else pl.BlockSpec((tm, tk), lambda i, j, k: (i, k))
    wb = (tn, tk) if tb else (tk, tn)
    w_idx = (lambda i, j, k: (j, k)) if tb else (lambda i, j, k: (k, j))
    if b_view is None:
        b_spec = pl.BlockSpec(wb, w_idx)
    elif kind == "cols":
        per = blk_c // wb[1]
        b_spec = pl.BlockSpec((None, None) + wb,
                              lambda i, j, k: (w_idx(i, j, k)[1] // per, layer, w_idx(i, j, k)[0], w_idx(i, j, k)[1] % per))
    else:
        per = blk_r // wb[0]
        b_spec = pl.BlockSpec((None, None) + wb,
                              lambda i, j, k: (w_idx(i, j, k)[0] // per, layer, w_idx(i, j, k)[0] % per, w_idx(i, j, k)[1]))
    if out_split:
        per_o = (n_dim // out_split) // tn
        out_spec = pl.BlockSpec((None, tm, tn), lambda i, j, k: (j // per_o, i, j % per_o))
        out_sds = jax.ShapeDtypeStruct((out_split, m_dim, n_dim // out_split), out_dtype)
    else:
        out_spec = pl.BlockSpec((tm, tn), lambda i, j, k: (i, j))
        out_sds = jax.ShapeDtypeStruct((m_dim, n_dim), out_dtype)
    dims = (((0 if ta else 1,), (1 if tb else 0,)), ((), ()))
    assert gate is None or plus is None
    extra = [e for e in (gate, plus) if e is not None]
    n_out = 2 if act else 1

    def finish(acc, refs):
        if act:
            refs[0][...] = acc.astype(refs[0].dtype)
            r = jnp.maximum(acc, 0.0)
            refs[1][...] = (r * r).astype(refs[1].dtype)
        elif gate is not None:
            refs[1][...] = (acc * (2.0 * jnp.maximum(refs[0][...].astype(F32), 0.0))).astype(refs[1].dtype)
        elif plus is not None:
            refs[1][...] = (refs[0][...] + acc).astype(refs[1].dtype)
        else:
            refs[0][...] = acc.astype(refs[0].dtype)

    grid = (m_dim // tm, n_dim // tn, nk)
    nc = len(cargo)

    def body(a_ref, b_ref, *rest):
        acc_ref = rest[-1]
        ids = [pl.program_id(d) for d in range(3)]
        outs, end_cargo = _cargo_bounds(
            rest[len(extra):-1], nc, n_out, exchange, (ids[0] == 0) & (ids[1] == 0) & (ids[2] == 0),
            (ids[0] == grid[0] - 1) & (ids[1] == grid[1] - 1) & (ids[2] == grid[2] - 1))
        refs = tuple(rest[:len(extra)]) + tuple(outs)
        part = lax.dot_general(a_ref[...], b_ref[...], dims, preferred_element_type=F32)
        if nk == 1:
            finish(part, refs)
        else:
            k = ids[2]

            @pl.when(k == 0)
            def _():
                acc_ref[...] = part

            @pl.when(k > 0)
            def _():
                acc_ref[...] += part

            @pl.when(k == nk - 1)
            def _():
                finish(acc_ref[...], refs)
        end_cargo()

    out = pl.pallas_call(
        body, name=name, grid=grid,
        in_specs=[a_spec, b_spec] + [pl.BlockSpec((tm, tn), lambda i, j, k: (i, j))] * len(extra) + [ANY] * nc,
        out_specs=[out_spec] * n_out + [ANY] * nc,
        out_shape=[out_sds] * n_out + (exchange[1](cargo) if nc else []),
        scratch_shapes=(exchange[2](nc) if nc else []) + [pltpu.VMEM((tm, tn) if nk > 1 else (8, 128), F32)],
        compiler_params=_cparams(dimension_semantics=("arbitrary",) * 3 if nc else ("parallel", "parallel", "arbitrary")),
    )(a, b, *extra, *cargo)
    if nc:
        return out
    return out if act else out[0]


def _mm_groups_nt(parts, b, name):
    m_dim, k_dim = parts[0].shape
    ng, _, n_dim, _ = b.shape
    assert len(parts) == ng and b.shape[3] == k_dim
    tm = _pick(m_dim, (1056, 704, 512, 384, 256, 128))

    def body(*refs):
        a_refs, b_ref, o_ref, acc_ref = refs[:ng], refs[ng], refs[ng + 1], refs[ng + 2]
        k = pl.program_id(1)
        for g in range(ng):
            @pl.when(k == g)
            def _(g=g):
                part = lax.dot_general(a_refs[g][...], b_ref[...], (NT, ((), ())), preferred_element_type=F32)
                if g == 0:
                    acc_ref[...] = part
                elif g < ng - 1:
                    acc_ref[...] += part
                else:
                    o_ref[...] = acc_ref[...] + part

    return pl.pallas_call(
        body, name=name, grid=(m_dim // tm, ng),
        in_specs=[pl.BlockSpec((tm, k_dim), lambda i, k: (i, 0))] * ng
        + [pl.BlockSpec((None, None, n_dim, k_dim), lambda i, k: (k, 0, 0, 0))],
        out_specs=pl.BlockSpec((tm, n_dim), lambda i, k: (i, 0)),
        out_shape=jax.ShapeDtypeStruct((m_dim, n_dim), F32),
        scratch_shapes=[pltpu.VMEM((tm, n_dim), F32)],
        compiler_params=_cparams(dimension_semantics=("parallel", "arbitrary")),
    )(*parts, b)


def _row_tile(t_pad, width):
    for tr in (528, 352, 176, 128, 64):
        if t_pad % tr == 0 and tr * width * 4 <= (3 << 19) and tr % 16 == 0:
            return tr
    return 64 if t_pad % 64 == 0 else t_pad


def _ln_res_fn(h, m, g, b):
    x = ALPHA * h + m
    mu = jnp.mean(x, axis=-1, keepdims=True)
    xc = x - mu
    var = jnp.mean(xc * xc, axis=-1, keepdims=True)
    return xc * lax.rsqrt(var + LN_EPS) * g + b


def _ln_res_fwd(h, m, g, b, name):
    t_pad = h.shape[0]
    tr = _row_tile(t_pad, D)

    def body(h_ref, m_ref, g_ref, b_ref, y_ref, yb_ref):
        y = _ln_res_fn(h_ref[...], m_ref[...], g_ref[...], b_ref[...])
        y_ref[...] = y
        yb_ref[...] = y.astype(BF16)

    row = pl.BlockSpec((tr, D), lambda i: (i, 0))
    par = pl.BlockSpec((1, D), lambda i: (0, 0))
    return pl.pallas_call(
        body, name=name, grid=(t_pad // tr,), in_specs=[row, row, par, par], out_specs=[row, row],
        out_shape=[jax.ShapeDtypeStruct((t_pad, D), F32), jax.ShapeDtypeStruct((t_pad, D), BF16)],
        compiler_params=_cparams(),
    )(h, m, g, b)


def _ln_res_bwd(h, m, g, b, dys, name):
    t_pad = h.shape[0]
    tr = _row_tile(t_pad, D)
    nd = len(dys)

    def body(h_ref, m_ref, g_ref, b_ref, *rest):
        d_refs, (dh_ref, dm_ref, dg_ref, db_ref) = rest[:nd], rest[nd:]
        _, vjp = jax.vjp(_ln_res_fn, h_ref[...], m_ref[...], g_ref[...], b_ref[...])
        dy = d_refs[0][...]
        for d_ref in d_refs[1:]:
            dy = dy + d_ref[...]
        dh, dm, dg, db = vjp(dy)
        dh_ref[...] = dh
        dm_ref[...] = dm.astype(BF16)

        @pl.when(pl.program_id(0) == 0)
        def _():
            dg_ref[...] = jnp.zeros_like(dg_ref)
            db_ref[...] = jnp.zeros_like(db_ref)

        dg_ref[...] += dg
        db_ref[...] += db

    row = pl.BlockSpec((tr, D), lambda i: (i, 0))
    par = pl.BlockSpec((1, D), lambda i: (0, 0))
    return pl.pallas_call(
        body, name=name, grid=(t_pad // tr,), in_specs=[row, row, par, par] + [row] * nd,
        out_specs=[row, row, par, par],
        out_shape=[jax.ShapeDtypeStruct((t_pad, D), F32), jax.ShapeDtypeStruct((t_pad, D), BF16),
                   jax.ShapeDtypeStruct((1, D), F32), jax.ShapeDtypeStruct((1, D), F32)],
        compiler_params=_cparams(),
    )(h, m, g, b, *dys)


def _grms_fn(o, z, g):
    y = o * lax.rsqrt(jnp.mean(o * o, axis=-1, keepdims=True) + RMS_EPS) * g
    return y * _silu(z)


def _grms_fwd(o, z_arr, z_blk0, g, name):
    t_pad, w = o.shape
    tr = _row_tile(t_pad, w)
    assert (z_blk0 * HD) % w == 0

    def body(o_ref, z_ref, g_ref, y_ref):
        for h in range(w // HD):
            c = slice(h * HD, (h + 1) * HD)
            y_ref[:, c] = _grms_fn(o_ref[:, c], z_ref[:, c], g_ref[...]).astype(BF16)

    return pl.pallas_call(
        body, name=name, grid=(t_pad // tr,),
        in_specs=[pl.BlockSpec((tr, w), lambda i: (i, 0)), pl.BlockSpec((tr, w), lambda i: (i, z_blk0 * HD // w)),
                  pl.BlockSpec((1, HD), lambda i: (0, 0))],
        out_specs=pl.BlockSpec((tr, w), lambda i: (i, 0)),
        out_shape=jax.ShapeDtypeStruct((t_pad, w), BF16), compiler_params=_cparams(),
    )(o, z_arr, g)


def _grms_bwd(o, z_arr, z_blk0, g, dy_arr, dy_blk0, name):
    t_pad, w = o.shape
    tr = _row_tile(t_pad, w)
    assert (z_blk0 * HD) % w == 0 and (dy_blk0 * HD) % w == 0

    def body(o_ref, z_ref, g_ref, dy_ref, do_ref, dz_ref, dg_ref):
        @pl.when(pl.program_id(0) == 0)
        def _():
            dg_ref[...] = jnp.zeros_like(dg_ref)

        for h in range(w // HD):
            c = slice(h * HD, (h + 1) * HD)
            _, vjp = jax.vjp(_grms_fn, o_ref[:, c], z_ref[:, c], g_ref[...])
            do, dz, dg = vjp(dy_ref[:, c])
            do_ref[:, c] = do
            dz_ref[:, c] = dz.astype(BF16)
            dg_ref[...] += dg

    blk = pl.BlockSpec((tr, w), lambda i: (i, 0))
    return pl.pallas_call(
        body, name=name, grid=(t_pad // tr,),
        in_specs=[blk, pl.BlockSpec((tr, w), lambda i: (i, z_blk0 * HD // w)), pl.BlockSpec((1, HD), lambda i: (0, 0)),
                  pl.BlockSpec((tr, w), lambda i: (i, dy_blk0 * HD // w))],
        out_specs=[blk, blk, pl.BlockSpec((1, HD), lambda i: (0, 0))],
        out_shape=[jax.ShapeDtypeStruct((t_pad, w), F32), jax.ShapeDtypeStruct((t_pad, w), BF16),
                   jax.ShapeDtypeStruct((1, HD), F32)],
        compiler_params=_cparams(),
    )(o, z_arr, g, dy_arr)


def _loss_fwd(y, tgt, first_row, name):
    t_pad = y.shape[0]
    tr = _row_tile(t_pad, D)

    def body(y_ref, t_ref, l_ref, dy_ref):
        rows = pl.program_id(0) * tr + _iota((tr, 1), 0)
        err = jnp.where(rows >= first_row, y_ref[...] - t_ref[...], 0.0)
        dy_ref[...] = err * (1.0 / D)

        @pl.when(pl.program_id(0) == 0)
        def _():
            l_ref[...] = jnp.zeros_like(l_ref)

        part = jnp.sum(jnp.sum(err * err, axis=1, keepdims=True), axis=0, keepdims=True)
        l_ref[...] += jnp.broadcast_to(part * (0.5 / D), l_ref.shape)

    row = pl.BlockSpec((tr, D), lambda i: (i, 0))
    return pl.pallas_call(
        body, name=name, grid=(t_pad // tr,), in_specs=[row, row],
        out_specs=[pl.BlockSpec((8, 128), lambda i: (0, 0)), row],
        out_shape=[jax.ShapeDtypeStruct((8, 128), F32), jax.ShapeDtypeStruct((t_pad, D), F32)],
        compiler_params=_cparams(),
    )(y, tgt)


def _assemble_bf16(parts, name):
    t_pad = parts[0].shape[0]
    widths = [p.shape[1] for p in parts]
    total = sum(widths)
    tr = _row_tile(t_pad, total)

    def body(*refs):
        o_ref = refs[-1]
        off = 0
        for ref, w in zip(refs[:-1], widths):
            o_ref[:, off:off + w] = ref[...].astype(BF16)
            off += w

    return pl.pallas_call(
        body, name=name, grid=(t_pad // tr,), in_specs=[pl.BlockSpec((tr, w), lambda i: (i, 0)) for w in widths],
        out_specs=pl.BlockSpec((tr, total), lambda i: (i, 0)),
        out_shape=jax.ShapeDtypeStruct((t_pad, total), BF16), compiler_params=_cparams(),
    )(*parts)


CONV_K = 4
HALO = 8
RT = 128


def _conv_fwd(p, blk0, w, mode, pad, name):
    t_pad = p.shape[0]
    nt = t_pad // RT
    scale = HD ** -0.5 if mode == "q" else 1.0

    def body(x_ref, w_ref, y_ref, xs_ref):
        xs_ref[0:HALO, :] = jnp.zeros((HALO, HD), F32)
        rows = _iota((t_pad, 1), 0)
        xs_ref[HALO:HALO + t_pad, :] = jnp.where(rows >= pad, x_ref[...], 0.0)
        wv = w_ref[...]

        def tile(i, carry):
            r0 = pl.multiple_of(i * RT, RT)
            ext = xs_ref[pl.ds(r0, RT + HALO), :]
            acc = ext[HALO:, :] * wv[3:4, :]
            for s in (1, 2, 3):
                acc = acc + pltpu.roll(ext, s, 0)[HALO:, :] * wv[3 - s:4 - s, :]
            y = _silu(acc)
            if mode != "v":
                y = y * lax.rsqrt(jnp.sum(y * y, axis=-1, keepdims=True) + L2_EPS) * scale
            y_ref[pl.ds(r0, RT), :] = y
            return carry

        lax.fori_loop(0, nt, tile, 0)

    return pl.pallas_call(
        body, name=name, grid=(GDN_H,),
        in_specs=[pl.BlockSpec((t_pad, HD), lambda h: (0, blk0 + h)), pl.BlockSpec((CONV_K, HD), lambda h: (0, h))],
        out_specs=pl.BlockSpec((t_pad, HD), lambda h: (0, h)),
        out_shape=jax.ShapeDtypeStruct((t_pad, GDN_H * HD), F32),
        scratch_shapes=[pltpu.VMEM((t_pad + HALO, HD), F32)],
        compiler_params=_cparams(),
    )(p, w)


def _conv_bwd(p, blk0, w, dn, mode, pad, name):
    t_pad = p.shape[0]
    nt = t_pad // RT
    scale = HD ** -0.5 if mode == "q" else 1.0

    def body(x_ref, w_ref, dn_ref, dx_ref, dw_ref, xs_ref, ds_ref):
        xs_ref[0:HALO, :] = jnp.zeros((HALO, HD), F32)
        xs_ref[HALO + t_pad:HALO + t_pad + 2 * HALO, :] = jnp.zeros((2 * HALO, HD), F32)
        ds_ref[t_pad:t_pad + HALO, :] = jnp.zeros((HALO, HD), F32)
        rows = _iota((t_pad, 1), 0)
        xs_ref[HALO:HALO + t_pad, :] = jnp.where(rows >= pad, x_ref[...], 0.0)
        ds_ref[0:t_pad, :] = dn_ref[...]
        wv = w_ref[...]

        def tile(i, dw):
            r0 = pl.multiple_of(i * RT, RT)
            ext = xs_ref[pl.ds(r0, RT + 2 * HALO), :]
            dn_e = ds_ref[pl.ds(r0, RT + HALO), :]
            xsh = [ext[HALO:, :]] + [pltpu.roll(ext, s, 0)[HALO:, :] for s in (1, 2, 3)]
            pre = xsh[0] * wv[3:4, :]
            for s in (1, 2, 3):
                pre = pre + xsh[s] * wv[3 - s:4 - s, :]
            sg = _sigmoid(pre)
            y = pre * sg
            if mode != "v":
                ss = jnp.sum(y * y, axis=-1, keepdims=True) + L2_EPS
                r = lax.rsqrt(ss)
                dy = scale * (dn_e * r - y * (r * r * r) * jnp.sum(dn_e * y, axis=-1, keepdims=True))
            else:
                dy = dn_e
            dpre = dy * (sg * (1.0 + pre * (1.0 - sg)))
            dx = dpre[:RT, :] * wv[3:4, :]
            for s in (1, 2, 3):
                dx = dx + pltpu.roll(dpre, RT + HALO - s, 0)[:RT, :] * wv[3 - s:4 - s, :]
            trow = r0 + _iota((RT, 1), 0)
            dx_ref[pl.ds(r0, RT), :] = jnp.where(trow >= pad, dx, 0.0)
            new = []
            for s in (0, 1, 2, 3):
                new.append(dw[s] + jnp.sum(dpre[:RT, :] * xsh[s][:RT, :], axis=0, keepdims=True))
            return tuple(new)

        z = jnp.zeros((1, HD), F32)
        dw = lax.fori_loop(0, nt, tile, (z, z, z, z))
        for s in (0, 1, 2, 3):
            dw_ref[3 - s:4 - s, :] = dw[s]

    return pl.pallas_call(
        body, name=name, grid=(GDN_H,),
        in_specs=[pl.BlockSpec((t_pad, HD), lambda h: (0, blk0 + h)), pl.BlockSpec((CONV_K, HD), lambda h: (0, h)),
                  pl.BlockSpec((t_pad, HD), lambda h: (0, h))],
        out_specs=[pl.BlockSpec((t_pad, HD), lambda h: (0, h)), pl.BlockSpec((CONV_K, HD), lambda h: (0, h))],
        out_shape=[jax.ShapeDtypeStruct((t_pad, GDN_H * HD), F32), jax.ShapeDtypeStruct((CONV_K, GDN_H * HD), F32)],
        scratch_shapes=[pltpu.VMEM((t_pad + 3 * HALO, HD), F32), pltpu.VMEM((t_pad + HALO, HD), F32)],
        compiler_params=_cparams(),
    )(p, w, dn)


@jax.custom_vjp
def _unit_lower_inv(m, bd, eye):
    md = m * bd
    low = m - md
    p2 = mbnn(md, md)
    p4 = mbnn(p2, p2)
    dinv = mbnn(mbnn(eye - md, eye + p2), eye + p4)
    n = mbnn(dinv, low)
    n2 = mbnn(n, n)
    n4 = mbnn(n2, n2)
    return mbnn(mbnn(mbnn(eye - n, eye + n2), eye + n4), dinv)


def _unit_lower_inv_bwd(res, g):
    t, bd, eye = res
    return -mbtn(t, mbnt(g, t)), jnp.zeros_like(bd), jnp.zeros_like(eye)


def _unit_lower_inv_fwd(m, bd, eye):
    t = _unit_lower_inv(m, bd, eye)
    return t, (t, bd, eye)


_unit_lower_inv.defvjp(_unit_lower_inv_fwd, _unit_lower_inv_bwd)


def _gdn_chunks(chunks, alog, dtb, s):
    nh = chunks[0][0].shape[0]
    ri = _iota((1, CH, CH), 1)
    ci = _iota((1, CH, CH), 2)
    causal = ri >= ci
    strict = ri > ci
    eye = (ri == ci).astype(F32)
    bd = ((ri >> 3) == (ci >> 3)).astype(F32)
    ltri = (_iota((CH, CH), 0) >= _iota((CH, CH), 1)).astype(F32)
    sel = (_iota((nh, 1, HD), 2) == _iota((nh, 1, HD), 0)).astype(F32)
    last = _iota((1, CH, 1), 1) == CH - 1

    beta, gc, gc_rows = [], [], []
    for _, _, _, bb, aa, valid in chunks:
        beta_all = jnp.where(valid, _sigmoid(bb), 0.0)
        g_all = jnp.where(valid, -jnp.exp(alog) * _softplus(aa + dtb), 0.0)
        gc_all = hnn(ltri, g_all)
        beta.append(jnp.sum(beta_all[None] * sel, axis=2, keepdims=True))
        gc.append(jnp.sum(gc_all[None] * sel, axis=2, keepdims=True))
        gc_rows.append(hbnt(jnp.broadcast_to(sel, (nh, CH, HD)), jnp.broadcast_to(gc_all[None], (nh, CH, HD))))
    cat = lambda xs: jnp.concatenate(xs, axis=0)
    q, k, v = (cat([c[j] for c in chunks]) for j in range(3))
    beta, gc, gc_rows = cat(beta), cat(gc), cat(gc_rows)
    gc_last = jnp.sum(jnp.where(last, gc, 0.0), axis=1, keepdims=True)
    decay = jnp.exp(jnp.where(causal, gc - gc_rows, NEG))
    egc = jnp.exp(gc)

    kb = k * beta
    m = jnp.where(strict, bbnt(kb, k) * decay, 0.0)
    t_inv = _unit_lower_inv(m, bd, eye)
    u = bbnn(t_inv, v * beta)
    w = bbnn(t_inv, kb * egc)
    a_intra = bbnt(q, k) * decay
    q_dec = q * egc
    k_dec = k * jnp.exp(gc_last - gc)
    g_tot = jnp.exp(gc_last)

    outs = []
    for n in range(len(chunks)):
        part = lambda a: a[n * nh:(n + 1) * nh]
        v_new = part(u) - bbnn(part(w), s)
        outs.append(bbnn(part(q_dec), s) + bbnn(part(a_intra), v_new))
        s = s * part(g_tot) + bbtn(part(k_dec), v_new)
    return outs, s


PAIR = 2 * CH


def _gdn_specs(npair, rev):
    cc = (lambda c: npair - 1 - c) if rev else (lambda c: c)
    wide = pl.BlockSpec((PAIR, GDN_H * HD), lambda c: (cc(c), 0))
    fix = lambda off: pl.BlockSpec((PAIR, HD), lambda c: (cc(c), off))
    par = pl.BlockSpec((1, HD), lambda c: (0, 0))
    state = pl.BlockSpec((1, GDN_H, HD, HD), lambda c: (cc(c), 0, 0, 0))
    return wide, fix, par, state


def _store_heads(ref, a, rows=slice(None)):
    for h in range(a.shape[0]):
        ref[rows, h * HD:(h + 1) * HD] = a[h]


def _chunk_rows(half):
    return slice(half * CH, (half + 1) * CH)


def _chunk_valid(pair, half, pad):
    return ((2 * pair + half) * CH + _iota((CH, 1), 0)) >= pad


def _gdn_fwd(qn, kn, vn, p, alog, dtb, pad, name, cargo=(), exchange=None):
    t_pad = qn.shape[0]
    npair = t_pad // PAIR
    wide, fix, par, state = _gdn_specs(npair, False)
    n = len(cargo)

    def body(q_ref, k_ref, v_ref, bb_ref, aa_ref, al_ref, dt_ref, *rest):
        c = pl.program_id(0)
        s_ref = rest[-1]
        (o_ref, ss_ref), end_cargo = _cargo_bounds(rest[:-1], n, 2, exchange, c == 0, c == npair - 1)

        @pl.when(c == 0)
        def _():
            s_ref[...] = jnp.zeros_like(s_ref)

        s = s_ref[...]
        ss_ref[0] = s
        rows = [_chunk_rows(half) for half in (0, 1)]
        chunks = [(_heads(q_ref[r, :], GDN_H), _heads(k_ref[r, :], GDN_H), _heads(v_ref[r, :], GDN_H),
                   bb_ref[r, :], aa_ref[r, :], _chunk_valid(c, half, pad)) for half, r in enumerate(rows)]
        outs, s = _gdn_chunks(chunks, al_ref[...], dt_ref[...], s)
        for r, o in zip(rows, outs):
            _store_heads(o_ref, o, r)
        s_ref[...] = s
        end_cargo()

    return pl.pallas_call(
        body, name=name, grid=(npair,),
        in_specs=[wide, wide, wide, fix(16), fix(17), par, par] + [ANY] * n,
        out_specs=[wide, state] + [ANY] * n,
        out_shape=[jax.ShapeDtypeStruct((t_pad, GDN_H * HD), F32), jax.ShapeDtypeStruct((npair, GDN_H, HD, HD), F32)]
        + (exchange[1](cargo) if n else []),
        scratch_shapes=(exchange[2](n) if n else []) + [pltpu.VMEM((GDN_H, HD, HD), F32)],
        compiler_params=_cparams(),
    )(qn, kn, vn, p, p, alog, dtb, *cargo)


def _gdn_bwd(qn, kn, vn, p, alog, dtb, ssave, do, pad, name, cargo=(), exchange=None):
    t_pad = qn.shape[0]
    npair = t_pad // PAIR
    wide, fix, par, state = _gdn_specs(npair, True)
    n = len(cargo)

    def body(q_ref, k_ref, v_ref, bb_ref, aa_ref, al_ref, dt_ref, ss_ref, do_ref, *rest):
        c = pl.program_id(0)
        ds_ref = rest[-1]
        (dq_ref, dk_ref, dv_ref, dbb_ref, daa_ref, dal_ref, ddt_ref), end_cargo = _cargo_bounds(
            rest[:-1], n, 7, exchange, c == 0, c == npair - 1)

        @pl.when(c == 0)
        def _():
            ds_ref[...] = jnp.zeros_like(ds_ref)
            dal_ref[...] = jnp.zeros_like(dal_ref)
            ddt_ref[...] = jnp.zeros_like(ddt_ref)

        ra, rb = _chunk_rows(0), _chunk_rows(1)
        va, vb = _chunk_valid(npair - 1 - c, 0, pad), _chunk_valid(npair - 1 - c, 1, pad)

        def pair(qa, ka, va_, ba, aa, qb, kb, vb_, bb, ab, al, dt, s):
            (oa, ob), s = _gdn_chunks([(qa, ka, va_, ba, aa, va), (qb, kb, vb_, bb, ab, vb)], al, dt, s)
            return oa, ob, s

        ins = [f(ref[r, :]) for r in (ra, rb)
               for ref, f in ((q_ref, lambda a: _heads(a, GDN_H)), (k_ref, lambda a: _heads(a, GDN_H)),
                              (v_ref, lambda a: _heads(a, GDN_H)), (bb_ref, lambda a: a), (aa_ref, lambda a: a))]
        _, vjp = jax.vjp(pair, *ins, al_ref[...], dt_ref[...], ss_ref[0])
        g = vjp((_heads(do_ref[ra, :], GDN_H), _heads(do_ref[rb, :], GDN_H), ds_ref[...]))
        for r, (dq, dk, dv, dbb, daa) in ((ra, g[0:5]), (rb, g[5:10])):
            _store_heads(dq_ref, dq, r)
            _store_heads(dk_ref, dk, r)
            _store_heads(dv_ref, dv, r)
            dbb_ref[r, :] = dbb
            daa_ref[r, :] = daa
        dal_ref[...] += g[10]
        ddt_ref[...] += g[11]
        ds_ref[...] = g[12]
        end_cargo()

    sds = jax.ShapeDtypeStruct
    return pl.pallas_call(
        body, name=name, grid=(npair,),
        in_specs=[wide, wide, wide, fix(16), fix(17), par, par, state, wide] + [ANY] * n,
        out_specs=[wide, wide, wide, fix(0), fix(0), par, par] + [ANY] * n,
        out_shape=[sds((t_pad, GDN_H * HD), F32)] * 3 + [sds((t_pad, HD), F32)] * 2 + [sds((1, HD), F32)] * 2
        + (exchange[1](cargo) if n else []),
        scratch_shapes=(exchange[2](n) if n else []) + [pltpu.VMEM((GDN_H, HD, HD), F32)],
        compiler_params=_cparams(),
    )(qn, kn, vn, p, p, alog, dtb, ssave, do, *cargo)


SB_Q0, SB_K0, SB_V0 = 18, 22, 26
SB_SCALE = SB_DH ** -0.5
SB_NB = 8


def _sb_terms(z, allowed):
    e = jnp.exp(-jnp.abs(z))
    den = 1.0 + e
    raw = -jnp.maximum(z, 0.0) - jnp.log(den)
    l1m = raw if allowed is None else jnp.where(allowed, raw, 0.0)
    return l1m, z + raw, jnp.where(z >= 0.0, 1.0, e) / den


def _sb_passes(i, step, carry):
    total = i + 1
    sized = lambda done: [functools.partial(step, done, masked=True, nb=nb) for nb in range(1, SB_NB + 1)]

    def several(c):
        n_mid = (total - SB_NB - 1) // SB_NB
        c = step(0, c, masked=True, nb=SB_NB)
        c = lax.fori_loop(0, n_mid, lambda t, cc: step(SB_NB * (1 + t), cc, masked=False, nb=SB_NB), c)
        done = SB_NB * (1 + n_mid)
        return lax.switch(total - done - 1, sized(done), c)

    return lax.cond(total <= SB_NB, lambda c: lax.switch(total - 1, sized(0), c), several, carry)


def _sb_stack(a, i):
    first = _iota((1, HD), 1) < SB_DH
    a2 = jnp.concatenate([jnp.where(first, a, 0.0), jnp.where(first, 0.0, a)], axis=0).astype(BF16)
    rq = i * QB + _iota((QB, 1), 0)
    return a2, jnp.concatenate([rq, rq], axis=0), first


def _hi_lo(a):
    hi = a.astype(BF16)
    lo = (a - hi.astype(F32)).astype(BF16)
    return jnp.concatenate([hi, lo], axis=1)


def _cargo_bounds(refs, n, n_out, exchange, first, last):
    outs = refs[n:n + n_out]
    if not n:
        return outs, lambda: None
    ex = exchange[0](refs[:n], refs[n + n_out:2 * n + n_out], *refs[2 * n + n_out:])

    @pl.when(first)
    def _():
        ex.start()

    def finish():
        @pl.when(last)
        def _():
            ex.wait()

    return outs, finish


def _sb_fwd(p, pad, name, cargo=(), exchange=None):
    t_pad = p.shape[0]
    nq = t_pad // QB
    n = len(cargo)

    def body(q_ref, k_ref, v_ref, *rest):
        i = pl.program_id(1)
        pr = pl.program_id(0)
        (o_ref, r_ref), end_cargo = _cargo_bounds(rest, n, 2, exchange, (pr == 0) & (i == 0),
                                                  (pr == SB_H // 2 - 1) & (i == nq - 1))
        q2, rowq, first = _sb_stack(q_ref[...] * SB_SCALE, i)
        tri = (_iota((QB, QB), 0) > _iota((QB, QB), 1)).astype(BF16)
        upper2 = jnp.concatenate([jnp.concatenate([tri, tri], axis=0), jnp.ones((2 * QB, QB), BF16)], axis=1)

        def chain(kb, masked):
            start = pl.multiple_of(kb * QB, QB)
            kblk = k_ref[pl.ds(start, QB), :].astype(BF16)
            vblk = v_ref[pl.ds(start, QB), :].astype(BF16)
            z = lax.dot_general(q2, kblk, (NT, ((), ())), preferred_element_type=F32)
            colk = kb * QB + _iota((1, QB), 1)
            al = ((colk < rowq) & (colk >= pad)) if masked else None
            l1m, ls, _ = _sb_terms(z, al)
            sums = lax.dot_general(_hi_lo(l1m), upper2, (NN, ((), ())), preferred_element_type=F32)
            return al, ls, sums[:, :QB], sums[:, QB:], vblk

        def step(done, carry, masked, nb):
            o_acc, run = carry
            ws, vs = [], []
            for n in range(nb):
                al, ls, suf, rs, vblk = chain(i - done - n, masked)
                wgt = jnp.exp(ls + suf + run)
                ws.append((wgt if al is None else jnp.where(al, wgt, 0.0)).astype(BF16))
                vs.append(vblk)
                run = run + rs
            o_acc = o_acc + lax.dot_general(jnp.concatenate(ws, axis=1), jnp.concatenate(vs, axis=0),
                                            (NN, ((), ())), preferred_element_type=F32)
            return o_acc, run

        o_acc, run = _sb_passes(i, step, (jnp.zeros((2 * QB, HD), F32), jnp.zeros((2 * QB, QB), F32)))
        o_ref[...] = jnp.where(first, o_acc[:QB], o_acc[QB:]).astype(BF16)
        r_ref[...] = jnp.where(first, run[:QB], run[QB:])
        end_cargo()

    full = lambda off: pl.BlockSpec((t_pad, HD), lambda pr, i: (0, off + pr))
    blk = pl.BlockSpec((QB, HD), lambda pr, i: (i, pr))
    return pl.pallas_call(
        body, name=name, grid=(SB_H // 2, nq),
        in_specs=[pl.BlockSpec((QB, HD), lambda pr, i: (i, SB_Q0 + pr)), full(SB_K0), full(SB_V0)] + [ANY] * n,
        out_specs=[blk, blk] + [ANY] * n,
        out_shape=[jax.ShapeDtypeStruct((t_pad, SB_H * SB_DH), BF16), jax.ShapeDtypeStruct((t_pad, SB_H * SB_DH), F32)]
        + (exchange[1](cargo) if n else []),
        scratch_shapes=exchange[2](n) if n else [],
        compiler_params=_cparams(),
    )(p, p, p, *cargo)


def _sb_bwd(p, rtot, dy, dy_blk0, pad, name, cargo=(), exchange=None):
    t_pad = p.shape[0]
    nq = t_pad // QB
    n = len(cargo)

    def body(q_ref, k_ref, v_ref, r_ref, do_ref, *rest):
        i = pl.program_id(1)
        pr = pl.program_id(0)
        dkt_ref, dvt_ref = rest[-2:]
        (dq_ref, dk_ref, dv_ref), end_cargo = _cargo_bounds(rest[:-2], n, 3, exchange, (pr == 0) & (i == 0),
                                                            (pr == SB_H // 2 - 1) & (i == nq - 1))

        @pl.when(i == 0)
        def _():
            dkt_ref[...] = jnp.zeros_like(dkt_ref)
            dvt_ref[...] = jnp.zeros_like(dvt_ref)

        q2, rowq, first = _sb_stack(q_ref[...] * SB_SCALE, i)
        do2, _, _ = _sb_stack(do_ref[...], i)
        q2t = jnp.transpose(q2.astype(F32)).astype(BF16)
        do2t = jnp.transpose(do2.astype(F32)).astype(BF16)
        rt = r_ref[...]
        lane = _iota((1, HD), 1)
        rcol = jnp.concatenate([jnp.sum(jnp.where(lane == 0, rt, 0.0), axis=1, keepdims=True),
                                jnp.sum(jnp.where(lane == SB_DH, rt, 0.0), axis=1, keepdims=True)], axis=0)
        rj = _iota((QB, QB), 0)
        cs = _iota((QB, QB), 1)
        tri_u = (rj > cs).astype(BF16)
        tri_l = (rj < cs).astype(BF16)
        ones2 = jnp.ones((2 * QB, QB), BF16)
        upper2 = jnp.concatenate([jnp.concatenate([tri_u, tri_u], axis=0), ones2], axis=1)
        lower2 = jnp.concatenate([jnp.concatenate([tri_l, tri_l], axis=0), ones2], axis=1)
        rcol = jnp.broadcast_to(rcol, (2 * QB, QB))

        def chain(kb, masked):
            start = pl.multiple_of(kb * QB, QB)
            kblk = k_ref[pl.ds(start, QB), :].astype(BF16)
            vblk = v_ref[pl.ds(start, QB), :].astype(BF16)
            z = lax.dot_general(q2, kblk, (NT, ((), ())), preferred_element_type=F32)
            colk = kb * QB + _iota((1, QB), 1)
            al = ((colk < rowq) & (colk >= pad)) if masked else None
            l1m, ls, sg = _sb_terms(z, al)
            dwgt = lax.dot_general(do2, vblk, (NT, ((), ())), preferred_element_type=F32)
            sums = lax.dot_general(_hi_lo(l1m), upper2, (NN, ((), ())), preferred_element_type=F32)
            return kb, kblk, al, ls, sums[:, :QB], sums[:, QB:], dwgt, sg

        def finish(c, seen, gseen):
            kb, kblk, al, ls, suf, rs, dwgt, sg = c
            wgt = jnp.exp(ls + suf + (rcol - seen - rs))
            if al is not None:
                wgt = jnp.where(al, wgt, 0.0)
            dl = dwgt * wgt
            sums = lax.dot_general(_hi_lo(dl), lower2, (NN, ((), ())), preferred_element_type=F32)
            gpre = gseen + sums[:, :QB]
            dz = dl - sg * (dl + gpre)
            if al is not None:
                dz = jnp.where(al, dz, 0.0)
            dz = dz.astype(BF16)
            dkt_ref[kb] += lax.dot_general(q2t, dz, (NN, ((), ())), preferred_element_type=F32)
            dvt_ref[kb] += lax.dot_general(do2t, wgt.astype(BF16), (NN, ((), ())), preferred_element_type=F32)
            return dz, seen + rs, gseen + sums[:, QB:]

        def step(done, carry, masked, nb):
            dq_acc, seen, gseen = carry
            cs_ = [chain(done + n, masked) for n in range(nb)]
            dzs = []
            for c in cs_:
                dz, seen, gseen = finish(c, seen, gseen)
                dzs.append(dz)
            dq_acc = dq_acc + lax.dot_general(jnp.concatenate(dzs, axis=1), jnp.concatenate([c[1] for c in cs_], axis=0),
                                              (NN, ((), ())), preferred_element_type=F32)
            return dq_acc, seen, gseen

        zc = jnp.zeros((2 * QB, QB), F32)
        dq_acc, _, _ = _sb_passes(i, step, (jnp.zeros((2 * QB, HD), F32), zc, zc))
        dq_ref[...] = jnp.where(first, dq_acc[:QB], dq_acc[QB:]) * SB_SCALE

        @pl.when(i == nq - 1)
        def _():
            for kb in range(nq):
                dk_ref[kb * QB:(kb + 1) * QB, :] = jnp.transpose(dkt_ref[kb])
                dv_ref[kb * QB:(kb + 1) * QB, :] = jnp.transpose(dvt_ref[kb])

        end_cargo()

    full_in = lambda off: pl.BlockSpec((t_pad, HD), lambda pr, i: (0, off + pr))
    full_out = pl.BlockSpec((t_pad, HD), lambda pr, i: (0, pr))
    blk = pl.BlockSpec((QB, HD), lambda pr, i: (i, pr))
    sds = jax.ShapeDtypeStruct((t_pad, SB_H * SB_DH), F32)
    return pl.pallas_call(
        body, name=name, grid=(SB_H // 2, nq),
        in_specs=[pl.BlockSpec((QB, HD), lambda pr, i: (i, SB_Q0 + pr)), full_in(SB_K0), full_in(SB_V0), blk,
                  pl.BlockSpec((QB, HD), lambda pr, i: (i, dy_blk0 + pr))] + [ANY] * n,
        out_specs=[blk, full_out, full_out] + [ANY] * n,
        out_shape=[sds, sds, sds] + (exchange[1](cargo) if n else []),
        scratch_shapes=(exchange[2](n) if n else []) + [pltpu.VMEM((nq, HD, QB), F32)] * 2,
        compiler_params=_cparams(),
    )(p, p, p, rtot, dy, *cargo)


HG_LEVELS = 6


def _hg_prefix_matrix():
    t = np.arange(CH)[:, None]
    j = np.arange(CH)[None, :]
    groups = [(j <= t)]
    for lvl in range(1, HG_LEVELS + 1):
        half = CH >> lvl
        e = (t // (2 * half)) * (2 * half) + half - 1
        groups.append(j <= e)
    groups.append(np.ones((8, CH), bool))
    e = np.concatenate(groups, axis=0).astype(np.float32)
    return np.concatenate([e, e, e], axis=1), np.concatenate([e, e, e], axis=0)


HG_G = 4


def _hg_chunk(qr, fr, iv, r0, r1, st, valid, ecat):
    g = st.shape[0]
    mx = jnp.maximum(r0, r1)
    e0 = jnp.exp(r0 - mx)
    e1 = jnp.exp(r1 - mx)
    lb = e1 / (e0 + e1)
    fg = lb + (1.0 - lb) * _sigmoid(fr)
    logf = jnp.where(valid, jnp.log(fg), 0.0)
    kk = jnp.where(valid, 1.0 - fg, 0.0)
    q = jnp.where(valid, _silu(qr), 0.0)
    v = _heads(jnp.where(valid, iv, 0.0), g)

    pre = _mask_dot(ecat, logf)
    b = pre[0:CH]
    b_last = jnp.max(pre[(HG_LEVELS + 1) * CH:], axis=0, keepdims=True)
    row = _iota((CH, 1), 0)
    ri = _iota((1, CH, CH), 1)
    ci = _iota((1, CH, CH), 2)
    a = jnp.where(ri == ci, jnp.sum(_heads(q * kk, g), axis=2, keepdims=True), 0.0)
    for lvl in range(1, HG_LEVELS + 1):
        half = CH >> lvl
        m = pre[lvl * CH:(lvl + 1) * CH]
        low = (row & half) != 0
        dec = jnp.exp(jnp.where(low, b - m, m - b))
        qt = jnp.where(low, q * dec, 0.0)
        kt = jnp.where(low, 0.0, kk * dec)
        same = (ri >> (7 - lvl)) == (ci >> (7 - lvl))
        a = a + jnp.where(same, bbnt(_heads(qt, g), _heads(kt, g)), 0.0)
    o = bbnt(_heads(q * jnp.exp(b), g), st) + bbnn(a, v)
    kd = kk * jnp.exp(b_last - b)
    st_new = st * _heads(jnp.exp(b_last), g) + bbtn(v, _heads(kd, g))
    return o, st_new


def _hg_specs(npair, rev):
    cc = (lambda c: npair - 1 - c) if rev else (lambda c: c)
    ng = HG_H // HG_G
    blk = lambda off: pl.BlockSpec((PAIR, HG_G * HD), lambda h, c: (cc(c), off * ng + h))
    lbs = pl.BlockSpec((2, HG_G * HD), lambda h, c: (0, h))
    state = pl.BlockSpec((1, HG_G, HD, HD), lambda h, c: (cc(c), h, 0, 0))
    return ng, blk, lbs, state


def _hg_fwd(p, lbraw, ecat, pad, name):
    t_pad = p.shape[0]
    npair = t_pad // PAIR
    ng, blk, lbs, state = _hg_specs(npair, False)

    def body(q_ref, f_ref, i_ref, lb_ref, e_ref, et_ref, o_ref, ss_ref, s_ref):
        c = pl.program_id(1)

        @pl.when(c == 0)
        def _():
            s_ref[...] = jnp.zeros_like(s_ref)

        st = s_ref[...]
        ss_ref[0] = st
        for half in (0, 1):
            r = _chunk_rows(half)
            o, st = _hg_chunk(q_ref[r, :], f_ref[r, :], i_ref[r, :], lb_ref[0:1, :], lb_ref[1:2, :], st,
                              _chunk_valid(c, half, pad), (e_ref[...], et_ref[...]))
            _store_heads(o_ref, o, r)
        s_ref[...] = st

    return pl.pallas_call(
        body, name=name, grid=(ng, npair),
        in_specs=[blk(0), blk(1), blk(2), lbs] + [pl.BlockSpec(e.shape, lambda h, c: (0, 0)) for e in ecat],
        out_specs=[blk(0), state],
        out_shape=[jax.ShapeDtypeStruct((t_pad, HG_H * HD), F32), jax.ShapeDtypeStruct((npair, HG_H, HD, HD), F32)],
        scratch_shapes=[pltpu.VMEM((HG_G, HD, HD), F32)],
        compiler_params=_cparams(),
    )(p, p, p, lbraw, *ecat)


def _hg_bwd(p, lbraw, ecat, ssave, do, pad, name, cargo=(), exchange=None):
    t_pad = p.shape[0]
    npair = t_pad // PAIR
    ng, blk, lbs, state = _hg_specs(npair, True)
    n = len(cargo)

    def body(q_ref, f_ref, i_ref, lb_ref, e_ref, et_ref, ss_ref, do_ref, *rest):
        c = pl.program_id(1)
        hg = pl.program_id(0)
        ds_ref = rest[-1]
        (dq_ref, df_ref, di_ref, dlb_ref), end_cargo = _cargo_bounds(
            rest[:-1], n, 4, exchange, (hg == 0) & (c == 0), (hg == ng - 1) & (c == npair - 1))

        @pl.when(c == 0)
        def _():
            ds_ref[...] = jnp.zeros_like(ds_ref)
            dlb_ref[...] = jnp.zeros_like(dlb_ref)

        ra, rb = _chunk_rows(0), _chunk_rows(1)
        va, vb = _chunk_valid(npair - 1 - c, 0, pad), _chunk_valid(npair - 1 - c, 1, pad)
        ecv = (e_ref[...], et_ref[...])

        def pair(qa, fa, ia, qb, fb, ib, r0, r1, st):
            oa, st = _hg_chunk(qa, fa, ia, r0, r1, st, va, ecv)
            ob, st = _hg_chunk(qb, fb, ib, r0, r1, st, vb, ecv)
            return oa, ob, st

        ins = [ref[r, :] for r in (ra, rb) for ref in (q_ref, f_ref, i_ref)]
        _, vjp = jax.vjp(pair, *ins, lb_ref[0:1, :], lb_ref[1:2, :], ss_ref[0])
        g = vjp((_heads(do_ref[ra, :], HG_G), _heads(do_ref[rb, :], HG_G), ds_ref[...]))
        for r, (dq, df, di) in ((ra, g[0:3]), (rb, g[3:6])):
            dq_ref[r, :] = dq.astype(BF16)
            df_ref[r, :] = df.astype(BF16)
            di_ref[r, :] = di.astype(BF16)
        dlb_ref[0:1, :] += g[6]
        dlb_ref[1:2, :] += g[7]
        ds_ref[...] = g[8]
        end_cargo()

    sds = jax.ShapeDtypeStruct((t_pad, HG_H * HD), BF16)
    return pl.pallas_call(
        body, name=name, grid=(ng, npair),
        in_specs=[blk(0), blk(1), blk(2), lbs] + [pl.BlockSpec(e.shape, lambda h, c: (0, 0)) for e in ecat]
        + [state, blk(0)] + [ANY] * n,
        out_specs=[blk(0), blk(0), blk(0), lbs] + [ANY] * n,
        out_shape=[sds, sds, sds, jax.ShapeDtypeStruct((2, HG_H * HD), F32)] + (exchange[1](cargo) if n else []),
        scratch_shapes=(exchange[2](n) if n else []) + [pltpu.VMEM((HG_G, HD, HD), F32)],
        compiler_params=_cparams(),
    )(p, p, p, lbraw, *ecat, ssave, do, *cargo)


def _pad_ab_cols(w):
    z = jnp.zeros((w.shape[0], HD - GDN_H), w.dtype)
    return jnp.concatenate([w[:, :2048], w[:, 2048:2052], z, w[:, 2052:2056], z, w[:, 2056:]], axis=1)


def _unpad_ab_cols(w):
    return jnp.concatenate([w[:, :2048], w[:, 2048:2052], w[:, 2176:2180], w[:, 2304:]], axis=1)


def _lane_pad(v):
    return jnp.pad(v, ((0, 0), (0, HD - v.shape[1])))


def _mlp_fwd(hb, w1, w2, layer):
    a, r = _mm(hb, w1, b_view=("cols", layer), out_dtype=BF16, act=True, name=f"mlp_up_{layer}")
    m = _mm(r, w2, b_view=("rows", layer), name=f"mlp_down_{layer}")
    return a, r, m


def _mlp_bwd(hb, a, r, dmb, w1, w2, layer):
    da = _mm(dmb, w2, tb=True, b_view=("rows", layer), out_dtype=BF16, gate=a, name=f"mlp_down_dx_{layer}")
    dw2 = _mm(r, dmb, ta=True, out_dtype=BF16, name=f"mlp_down_dw_{layer}")
    dh = _mm(da, w1, tb=True, b_view=("cols", layer), name=f"mlp_up_dx_{layer}")
    dw1 = _mm(hb, da, ta=True, out_dtype=BF16, out_split=N_CHIP, name=f"mlp_up_dw_{layer}")
    return dh, dw1, dw2


def _local_step(h0, tgt, w, pad, late=None):
    row = lambda a, i: a[i:i + 1]
    ecat = tuple(jnp.asarray(e, dtype=BF16) for e in _hg_prefix_matrix())
    cw = [w["conv_w"][:, i * 512:(i + 1) * 512] for i in range(3)]
    alog, dtb = _lane_pad(w["a_log"]), _lane_pad(w["dt_bias"])

    h0b = h0.astype(BF16)
    p0 = _mm(h0b, w["ab_w_in"], name="ab_in")
    qn = _conv_fwd(p0, 0, cw[0], "q", pad, "conv_q")
    kn = _conv_fwd(p0, 4, cw[1], "k", pad, "conv_k")
    vn = _conv_fwd(p0, 8, cw[2], "v", pad, "conv_v")
    if late is None:
        oa_raw, ss0 = _gdn_fwd(qn, kn, vn, p0, alog, dtb, pad, "gdn_fwd")
        ob, rtot = _sb_fwd(p0, pad, "sb_fwd")
    else:
        oa_raw, ss0, g_cin, g_cout = _gdn_fwd(qn, kn, vn, p0, alog, dtb, pad, "gdn_fwd",
                                              cargo=[late["c_w_in"], late["c_w_out"]], exchange=GATHER)
        ob, rtot, g_about, g_w1, g_w2 = _sb_fwd(p0, pad, "sb_fwd", exchange=GATHER,
                                                cargo=[late["ab_w_out"], late["mlp_w1"], late["mlp_w2"]])
        w = dict(w, ab_w_out=g_about.reshape(D, D), c_w_in=g_cin, c_w_out=g_cout.reshape(D, D), mlp_w1=g_w1, mlp_w2=g_w2)
    oa = _grms_fwd(oa_raw, p0, 12, w["ab_gnorm_g"], "gdn_gate")
    ycat = jnp.concatenate([oa, ob], axis=1)
    mix0 = _mm(ycat, w["ab_w_out"], name="ab_out")
    h1, h1b = _ln_res_fwd(h0, mix0, row(w["ln_mix_g"], 0), row(w["ln_mix_b"], 0), "ln_mix_0")
    a0, r0, m0 = _mlp_fwd(h1b, w["mlp_w1"], w["mlp_w2"], 0)
    h2, h2b = _ln_res_fwd(h1, m0, row(w["ln_ffn_g"], 0), row(w["ln_ffn_b"], 0), "ln_ffn_0")
    p1 = _mm(h2b, w["c_w_in"], b_view=("cols", 0), name="c_in")
    oc_raw, ss1 = _hg_fwd(p1, w["c_lb_raw"], ecat, pad, "hg_fwd")
    yc = _grms_fwd(oc_raw, p1, 3 * HG_H, w["c_gnorm_g"], "hg_gate")
    mix1 = _mm(yc, w["c_w_out"], name="c_out")
    h3, h3b = _ln_res_fwd(h2, mix1, row(w["ln_mix_g"], 1), row(w["ln_mix_b"], 1), "ln_mix_1")
    a1, r1, m1 = _mlp_fwd(h3b, w["mlp_w1"], w["mlp_w2"], 1)
    h4, _ = _ln_res_fwd(h3, m1, row(w["ln_ffn_g"], 1), row(w["ln_ffn_b"], 1), "ln_ffn_1")
    loss, dh4 = _loss_fwd(h4, tgt, pad + N_META, "loss")

    dh3a, dm1b, dfg1, dfb1 = _ln_res_bwd(h3, m1, row(w["ln_ffn_g"], 1), row(w["ln_ffn_b"], 1), [dh4], "ln_ffn_bwd_1")
    dh3b, dw1_1, dw2_1 = _mlp_bwd(h3b, a1, r1, dm1b, w["mlp_w1"], w["mlp_w2"], 1)
    dh2a, dmix1b, dmg1, dmb1 = _ln_res_bwd(h2, mix1, row(w["ln_mix_g"], 1), row(w["ln_mix_b"], 1), [dh3a, dh3b], "ln_mix_bwd_1")
    dyc = _mm(dmix1b, w["c_w_out"], tb=True, name="c_out_dx")
    dwco = _mm(yc, dmix1b, ta=True, out_dtype=BF16, name="c_out_dw")
    doc, dzc, dcg = _grms_bwd(oc_raw, p1, 3 * HG_H, w["c_gnorm_g"], dyc, 0, "hg_gate_bwd")
    landed = {}
    rows4 = lambda a: a.reshape(N_CHIP, -1, D)
    if late is None:
        dq1, df1, di1, dlb = _hg_bwd(p1, w["c_lb_raw"], ecat, ss1, doc, pad, "hg_bwd")
    else:
        dq1, df1, di1, dlb, landed["w1_1"] = _hg_bwd(
            p1, w["c_lb_raw"], ecat, ss1, doc, pad, "hg_bwd", cargo=[dw1_1], exchange=SCATTER)
    dp1 = [dq1, df1, di1, dzc]
    dh2b = _mm_groups_nt(dp1, w["c_w_in"], "c_in_dx")
    dwc = jnp.stack([_mm(h2b, d, ta=True, out_dtype=BF16, name=f"c_in_dw_{i}") for i, d in enumerate(dp1)])
    dh1a, dm0b, dfg0, dfb0 = _ln_res_bwd(h1, m0, row(w["ln_ffn_g"], 0), row(w["ln_ffn_b"], 0), [dh2a, dh2b], "ln_ffn_bwd_0")
    dh1b, dw1_0, dw2_0 = _mlp_bwd(h1b, a0, r0, dm0b, w["mlp_w1"], w["mlp_w2"], 0)
    dh0a, dmix0b, dmg0, dmb0 = _ln_res_bwd(h0, mix0, row(w["ln_mix_g"], 0), row(w["ln_mix_b"], 0), [dh1a, dh1b], "ln_mix_bwd_0")
    dycat = _mm(dmix0b, w["ab_w_out"], tb=True, name="ab_out_dx")
    dwabo = _mm(ycat, dmix0b, ta=True, out_dtype=BF16, name="ab_out_dw")
    doa, dza, dag = _grms_bwd(oa_raw, p0, 12, w["ab_gnorm_g"], dycat, 0, "gdn_gate_bwd")
    if late is None:
        dqn, dkn, dvn, dbb, daa, dal, ddt = _gdn_bwd(qn, kn, vn, p0, alog, dtb, ss0, doa, pad, "gdn_bwd")
        dqb, dkb, dvb = _sb_bwd(p0, rtot, dycat, 4, pad, "sb_bwd")
    else:
        dqn, dkn, dvn, dbb, daa, dal, ddt, landed["c_w_in"] = _gdn_bwd(
            qn, kn, vn, p0, alog, dtb, ss0, doa, pad, "gdn_bwd", cargo=[dwc], exchange=SCATTER)
        (dqb, dkb, dvb, landed["w1_0"], landed["w2_0"], landed["w2_1"], landed["ab_w_out"],
         landed["c_w_out"]) = _sb_bwd(
            p0, rtot, dycat, 4, pad, "sb_bwd",
            cargo=[dw1_0, rows4(dw2_0), rows4(dw2_1), rows4(dwabo), rows4(dwco)], exchange=SCATTER)
    dpq, dcq = _conv_bwd(p0, 0, cw[0], dqn, "q", pad, "conv_q_bwd")
    dpk, dck = _conv_bwd(p0, 4, cw[1], dkn, "k", pad, "conv_k_bwd")
    dpv, dcv = _conv_bwd(p0, 8, cw[2], dvn, "v", pad, "conv_v_bwd")
    dp0 = _assemble_bf16([dpq, dpk, dpv, dza, dbb, daa, dqb, dkb, dvb], "ab_in_dy")
    dwab = _mm(h0b, dp0, ta=True, out_dtype=BF16, name="ab_in_dw")
    if late is None:
        dh0 = _mm(dp0, w["ab_w_in"], tb=True, plus=dh0a, name="ab_in_dx")
    else:
        dab = jnp.transpose(_unpad_ab_cols(dwab).reshape(D, N_CHIP, AB_TRUE // N_CHIP), (1, 0, 2))
        dh0, landed["ab_w_in"] = _mm(dp0, w["ab_w_in"], tb=True, plus=dh0a, name="ab_in_dx", cargo=[dab],
                                     exchange=SCATTER)

    grads = {
        "ab_w_in": dwab, "conv_w": jnp.concatenate([dcq, dck, dcv], axis=1),
        "a_log": dal[:, :GDN_H], "dt_bias": ddt[:, :GDN_H],
        "ab_gnorm_g": dag, "ab_w_out": dwabo, "c_w_in": dwc, "c_lb_raw": dlb, "c_gnorm_g": dcg, "c_w_out": dwco,
        "ln_mix_g": jnp.concatenate([dmg0, dmg1], 0), "ln_mix_b": jnp.concatenate([dmb0, dmb1], 0),
        "w1_0": dw1_0, "w1_1": dw1_1, "w2_0": dw2_0, "w2_1": dw2_1,
        "ln_ffn_g": jnp.concatenate([dfg0, dfg1], 0), "ln_ffn_b": jnp.concatenate([dfb0, dfb1], 0),
        "landed": landed,
    }
    return loss, dh0, grads


MESH = pl.DeviceIdType.MESH
ANY = pl.BlockSpec(memory_space=pl.ANY)
N_CHIP = 4
N_DEV = 8
CHIP_REL = ((1, 0), (0, 1), (1, 1))
DEV_REL = tuple((dx, dy, dc) for dx in (0, 1) for dy in (0, 1) for dc in (0, 1))[1:]

def _pos():
    return lax.axis_index("x"), lax.axis_index("y"), lax.axis_index("c")


def _flip(a, d):
    return a + d - 2 * a * d


class _Exchange:
    def __init__(self, local, sends, recvs):
        self.local, self.sends, self.recvs = local, sends, recvs

    def start(self):
        for cp in self.local + self.sends:
            cp.start()

    def wait(self):
        for cp in self.recvs:
            cp.wait_recv()
        for cp in self.sends:
            cp.wait_send()
        for cp in self.local:
            cp.wait()


def _gather_sems(n):
    return [pltpu.SemaphoreType.DMA((3 * n,)), pltpu.SemaphoreType.DMA((3 * n,)), pltpu.SemaphoreType.DMA((n,))]


def _gather_copies(x_refs, o_refs, send_sems, recv_sems, local_sems):
    n = len(x_refs)
    x, y, c = _pos()
    local = [pltpu.make_async_copy(x_refs[a], o_refs[a].at[2 * x + y], local_sems.at[a]) for a in range(n)]

    def copy(a, k, sending):
        tx, ty = _flip(x, CHIP_REL[k][0]), _flip(y, CHIP_REL[k][1])
        return pltpu.make_async_remote_copy(
            src_ref=x_refs[a], dst_ref=o_refs[a].at[2 * x + y if sending else 2 * tx + ty],
            send_sem=send_sems.at[3 * a + k], recv_sem=recv_sems.at[3 * a + k], device_id=(tx, ty, c), device_id_type=MESH)

    pairs = [(a, k) for a in range(n) for k in range(3)]
    return _Exchange(local, [copy(a, k, True) for a, k in pairs], [copy(a, k, False) for a, k in pairs])


def _gather_shapes(bufs):
    return [jax.ShapeDtypeStruct((N_CHIP,) + b.shape, b.dtype) for b in bufs]


def _chip_allgather(bufs, name):
    n = len(bufs)

    def body(*refs):
        ex = _gather_copies(refs[:n], refs[n:2 * n], *refs[2 * n:])
        ex.start()
        ex.wait()

    return pl.pallas_call(
        body, name=name, in_specs=[ANY] * n, out_specs=[ANY] * n, out_shape=_gather_shapes(bufs),
        scratch_shapes=_gather_sems(n), compiler_params=pltpu.CompilerParams(has_side_effects=True),
    )(*bufs)


def _scatter_sems(n):
    nr = N_DEV - 1
    return [pltpu.SemaphoreType.DMA((nr * n,)), pltpu.SemaphoreType.DMA((nr * n,)), pltpu.SemaphoreType.DMA((n,))]


def _scatter_copies(g_refs, o_refs, send_sems, recv_sems, local_sems):
    n = len(g_refs)
    nr = N_DEV - 1
    x, y, c = _pos()
    me = 4 * x + 2 * y + c
    local = [pltpu.make_async_copy(g_refs[a].at[2 * x + y], o_refs[a].at[me], local_sems.at[a]) for a in range(n)]

    def copy(a, k, sending):
        dx, dy, dc = DEV_REL[k]
        tx, ty, tc = _flip(x, dx), _flip(y, dy), _flip(c, dc)
        return pltpu.make_async_remote_copy(
            src_ref=g_refs[a].at[2 * tx + ty], dst_ref=o_refs[a].at[me if sending else 4 * tx + 2 * ty + tc],
            send_sem=send_sems.at[nr * a + k], recv_sem=recv_sems.at[nr * a + k],
            device_id=(tx, ty, tc), device_id_type=MESH)

    pairs = [(a, k) for a in range(n) for k in range(nr)]
    return _Exchange(local, [copy(a, k, True) for a, k in pairs], [copy(a, k, False) for a, k in pairs])


def _scatter_shapes(gs):
    return [jax.ShapeDtypeStruct((N_DEV,) + g.shape[1:], g.dtype) for g in gs]


GATHER = (_gather_copies, _gather_shapes, _gather_sems)
SCATTER = (_scatter_copies, _scatter_shapes, _scatter_sems)


def _sum_slots(r, name):
    n, rh, w = r.shape
    tr = _pick(rh, (256, 128, 64, 16))

    def body(r_ref, o_ref):
        acc = r_ref[0].astype(F32)
        for s in range(1, n):
            acc = acc + r_ref[s].astype(F32)
        o_ref[...] = acc

    return pl.pallas_call(
        body, name=name, grid=(rh // tr,), in_specs=[pl.BlockSpec((n, tr, w), lambda i: (0, i, 0))],
        out_specs=pl.BlockSpec((tr, w), lambda i: (i, 0)), out_shape=jax.ShapeDtypeStruct((rh, w), F32),
        compiler_params=_cparams(),
    )(r)


def _small_allreduce(buf, name):
    r, w = buf.shape

    def body(b_ref, o_ref, land_ref, send_sems, recv_sems):
        x, y, c = _pos()
        me = 4 * x + 2 * y + c
        land_ref[me] = b_ref[...]

        def target(k):
            dx, dy, dc = DEV_REL[k]
            return _flip(x, dx), _flip(y, dy), _flip(c, dc)

        sends = []
        for k in range(N_DEV - 1):
            tx, ty, tc = target(k)
            cp = pltpu.make_async_remote_copy(
                src_ref=b_ref, dst_ref=land_ref.at[me], send_sem=send_sems.at[k], recv_sem=recv_sems.at[k],
                device_id=(tx, ty, tc), device_id_type=MESH)
            cp.start()
            sends.append(cp)
        for k in range(N_DEV - 1):
            tx, ty, tc = target(k)
            pltpu.make_async_remote_copy(
                src_ref=b_ref, dst_ref=land_ref.at[4 * tx + 2 * ty + tc], send_sem=send_sems.at[k],
                recv_sem=recv_sems.at[k], device_id=(tx, ty, tc), device_id_type=MESH).wait_recv()
        for cp in sends:
            cp.wait_send()
        acc = land_ref[0]
        for s in range(1, N_DEV):
            acc = acc + land_ref[s]
        o_ref[...] = acc

    vm = pl.BlockSpec(memory_space=pltpu.VMEM)
    return pl.pallas_call(
        body, name=name, in_specs=[vm], out_specs=vm, out_shape=jax.ShapeDtypeStruct((r, w), F32),
        scratch_shapes=[pltpu.VMEM((N_DEV, r, w), F32), pltpu.SemaphoreType.DMA((N_DEV - 1,)),
                        pltpu.SemaphoreType.DMA((N_DEV - 1,))],
        compiler_params=pltpu.CompilerParams(has_side_effects=True),
    )(buf)


def _adamw(w, g, m, v, name):
    r, c = w.shape
    tr = _pick(r, (256, 128, 64, 8)) if r * c > (1 << 18) else r

    def body(w_ref, g_ref, m_ref, v_ref, d_ref, m2_ref, v2_ref):
        gg = g_ref[...]
        m2 = ADAM_B1 * m_ref[...] + (1.0 - ADAM_B1) * gg
        v2 = ADAM_B2 * v_ref[...] + (1.0 - ADAM_B2) * (gg * gg)
        m_hat = m2 / (1.0 - ADAM_B1 ** ADAM_STEP)
        v_hat = v2 / (1.0 - ADAM_B2 ** ADAM_STEP)
        d_ref[...] = -ADAM_LR * (m_hat / (jnp.sqrt(v_hat) + ADAM_EPS) + ADAM_WD * w_ref[...])
        m2_ref[...] = m2
        v2_ref[...] = v2

    blk = pl.BlockSpec((tr, c), lambda i: (i, 0))
    sds = jax.ShapeDtypeStruct((r, c), F32)
    return pl.pallas_call(body, name=name, grid=(r // tr,), in_specs=[blk] * 4, out_specs=[blk] * 3,
                          out_shape=[sds] * 3, compiler_params=_cparams())(w, g, m, v)


BIG = ("ab_w_in", "ab_w_out", "c_w_in", "c_w_out", "mlp_w1", "mlp_w2")
SMALL = ("ln_mix_g", "ln_mix_b", "ln_ffn_g", "ln_ffn_b", "c_lb_raw", "ab_a_log", "ab_dt_bias", "ab_gnorm_g", "c_gnorm_g")
SMALL_ROWS = 16
CONV_ROWS = 8
CONV_W = 3 * GDN_H * HD


def _conv_to_rows(cw):
    return jnp.pad(cw, ((0, 0), (0, 2 * D - CONV_W))).reshape(CONV_ROWS, D)


def _rows_to_conv(rows):
    return rows.reshape(CONV_K, 2 * D)[:, :CONV_W]


def _pack_small(d):
    rows = [jnp.pad(d[n], ((0, 0), (0, D - d[n].shape[1]))) for n in SMALL]
    buf = jnp.concatenate(rows, axis=0)
    return jnp.pad(buf, ((0, SMALL_ROWS - buf.shape[0]), (0, 0)))


def _unpack_small(buf, like):
    out, r = {}, 0
    for n in SMALL:
        nr, nc = like[n].shape
        out[n] = buf[r:r + nr, :nc]
        r += nr
    return out


def kernel(x, meta_tokens, ab_w_in, ab_conv_w, ab_a_log, ab_dt_bias, ab_gnorm_g, ab_w_out, c_w_in, c_lb_raw, c_gnorm_g, c_w_out, ln_mix_g, ln_mix_b, mlp_w1, mlp_w2, ln_ffn_g, ln_ffn_b, loss_target, m_meta_tokens, m_ab_w_in, m_ab_conv_w, m_ab_a_log, m_ab_dt_bias, m_ab_gnorm_g, m_ab_w_out, m_c_w_in, m_c_lb_raw, m_c_gnorm_g, m_c_w_out, m_ln_mix_g, m_ln_mix_b, m_mlp_w1, m_mlp_w2, m_ln_ffn_g, m_ln_ffn_b, v_meta_tokens, v_ab_w_in, v_ab_conv_w, v_ab_a_log, v_ab_dt_bias, v_ab_gnorm_g, v_ab_w_out, v_c_w_in, v_c_lb_raw, v_c_gnorm_g, v_c_w_out, v_ln_mix_g, v_ln_mix_b, v_mlp_w1, v_mlp_w2, v_ln_ffn_g, v_ln_ffn_b):
    names = ("meta_tokens", "ab_w_in", "ab_conv_w", "ab_a_log", "ab_dt_bias", "ab_gnorm_g", "ab_w_out", "c_w_in",
             "c_lb_raw", "c_gnorm_g", "c_w_out", "ln_mix_g", "ln_mix_b", "mlp_w1", "mlp_w2", "ln_ffn_g", "ln_ffn_b")
    wts = dict(zip(names, (meta_tokens, ab_w_in, ab_conv_w, ab_a_log, ab_dt_bias, ab_gnorm_g, ab_w_out, c_w_in, c_lb_raw,
                           c_gnorm_g, c_w_out, ln_mix_g, ln_mix_b, mlp_w1, mlp_w2, ln_ffn_g, ln_ffn_b)))
    mom_m = dict(zip(names, (m_meta_tokens, m_ab_w_in, m_ab_conv_w, m_ab_a_log, m_ab_dt_bias, m_ab_gnorm_g, m_ab_w_out,
                             m_c_w_in, m_c_lb_raw, m_c_gnorm_g, m_c_w_out, m_ln_mix_g, m_ln_mix_b, m_mlp_w1, m_mlp_w2,
                             m_ln_ffn_g, m_ln_ffn_b)))
    mom_v = dict(zip(names, (v_meta_tokens, v_ab_w_in, v_ab_conv_w, v_ab_a_log, v_ab_dt_bias, v_ab_gnorm_g, v_ab_w_out,
                             v_c_w_in, v_c_lb_raw, v_c_gnorm_g, v_c_w_out, v_ln_mix_g, v_ln_mix_b, v_mlp_w1, v_mlp_w2,
                             v_ln_ffn_g, v_ln_ffn_b)))
    seq = x.shape[1]
    pad = (-(N_META + seq)) % QB
    xi, yi, ci = _pos()
    chip = 2 * xi + yi

    gat_ab_in, = _chip_allgather([ab_w_in[0].astype(BF16)], "gather_weights")
    late = {"ab_w_out": ab_w_out[0].astype(BF16), "c_w_in": c_w_in.astype(BF16), "c_w_out": c_w_out[0].astype(BF16),
            "mlp_w1": mlp_w1.astype(BF16), "mlp_w2": mlp_w2.astype(BF16)}
    mcols, ccols = meta_tokens.shape[1], ab_conv_w.shape[2]
    place = jnp.concatenate([
        lax.dynamic_update_slice(jnp.zeros((N_META, D), F32), 0.5 * meta_tokens, (0, chip * mcols)),
        _conv_to_rows(lax.dynamic_update_slice(jnp.zeros((CONV_K, CONV_W), F32), 0.5 * ab_conv_w[0], (0, chip * ccols)))],
        axis=0)
    placed = _small_allreduce(place, "gather_meta")
    meta_full = placed[:N_META]

    w = {
        "ab_w_in": _pad_ab_cols(jnp.transpose(gat_ab_in, (1, 0, 2)).reshape(D, AB_TRUE)),
        "conv_w": _rows_to_conv(placed[N_META:]), "a_log": ab_a_log, "dt_bias": ab_dt_bias,
        "ab_gnorm_g": ab_gnorm_g, "c_lb_raw": c_lb_raw,
        "c_gnorm_g": c_gnorm_g, "ln_mix_g": ln_mix_g, "ln_mix_b": ln_mix_b, "ln_ffn_g": ln_ffn_g, "ln_ffn_b": ln_ffn_b,
    }

    h0 = jnp.concatenate([jnp.zeros((pad, D), F32), meta_full, x[0]], axis=0)
    tgt = jnp.concatenate([jnp.zeros((pad + N_META, D), F32), loss_target[0]], axis=0)
    loss8, dh0, g = _local_step(h0, tgt, w, pad, late)
    loss = lax.psum(loss8[0, 0], ("x", "y", "c"))
    grad_x = dh0[pad + N_META:][None]

    gsmall = {"ln_mix_g": g["ln_mix_g"], "ln_mix_b": g["ln_mix_b"], "ln_ffn_g": g["ln_ffn_g"], "ln_ffn_b": g["ln_ffn_b"],
              "c_lb_raw": g["c_lb_raw"], "ab_a_log": g["a_log"], "ab_dt_bias": g["dt_bias"], "ab_gnorm_g": g["ab_gnorm_g"],
              "c_gnorm_g": g["c_gnorm_g"]}
    sbuf = jnp.concatenate([_pack_small(gsmall), dh0[pad:pad + N_META], _conv_to_rows(g["conv_w"])], axis=0)
    ssum = _small_allreduce(sbuf, "allreduce_small")
    grads = _unpack_small(ssum[:SMALL_ROWS], wts)
    grads["meta_tokens"] = lax.dynamic_slice(ssum[SMALL_ROWS:SMALL_ROWS + N_META], (0, chip * mcols), (N_META, mcols))
    grads["ab_conv_w"] = lax.dynamic_slice(_rows_to_conv(ssum[SMALL_ROWS + N_META:]), (0, chip * ccols), (CONV_K, ccols))[None]

    sums = {k: _sum_slots(v, f"grad_sum_{k}") for k, v in g["landed"].items()}
    for n in ("ab_w_in", "ab_w_out", "c_w_in", "c_w_out"):
        grads[n] = sums[n][None]
    grads["mlp_w1"] = jnp.stack([sums["w1_0"], sums["w1_1"]])
    grads["mlp_w2"] = jnp.stack([sums["w2_0"], sums["w2_1"]])

    delta, new_m, new_v = {}, {}, {}
    for n in ("meta_tokens", "ab_conv_w") + BIG:
        shp = wts[n].shape
        to2 = lambda a: a.reshape(-1, shp[-1])
        d2, m2, v2 = _adamw(to2(wts[n]), to2(grads[n]), to2(mom_m[n]), to2(mom_v[n]), f"adamw_{n}")
        delta[n], new_m[n], new_v[n] = d2.reshape(shp), m2.reshape(shp), v2.reshape(shp)
    d2, m2, v2 = _adamw(_pack_small(wts), ssum[:SMALL_ROWS], _pack_small(mom_m), _pack_small(mom_v), "adamw_small")
    delta.update(_unpack_small(d2, wts))
    new_m.update(_unpack_small(m2, wts))
    new_v.update(_unpack_small(v2, wts))

    return (loss, grad_x, *[grads[n] for n in names], *[delta[n] for n in names], *[new_m[n] for n in names],
            *[new_v[n] for n in names])
```

```python
import functools

import numpy as np
import jax
import jax.numpy as jnp
from jax import lax
from jax.experimental import pallas as pl
from jax.experimental.pallas import tpu as pltpu

F32 = jnp.float32
BF16 = jnp.bfloat16

D = 1024
N_META = 16
DEPTH = 2
GDN_H = 4
SB_H = 8
SB_DH = 64
HG_H = 8
HD = 128
CH = 64
QB = 128
ALPHA = float((2 * DEPTH) ** 0.25)
LN_EPS = 1e-5
RMS_EPS = 1e-6
L2_EPS = 1e-6
NEG = -1e30

ADAM_LR = 0.001
ADAM_B1 = 0.9
ADAM_B2 = 0.999
ADAM_EPS = 1e-08
ADAM_WD = 0.01
ADAM_STEP = 10

AB_TRUE = 3592
V7X_VMEM_BYTES = 64 * 1024 * 1024
VMEM_LIMIT = V7X_VMEM_BYTES - 8 * 1024 * 1024

NN = ((1,), (0,))
NT = ((1,), (1,))
TN = ((0,), (0,))


def _cparams(**kw):
    return pltpu.CompilerParams(vmem_limit_bytes=VMEM_LIMIT, **kw)


def _dg(a, b, dims, mode):
    if mode == "h":
        return lax.dot_general(a, b, dims, precision=lax.Precision.HIGHEST, preferred_element_type=F32)
    if mode == "b":
        return lax.dot_general(a.astype(BF16), b.astype(BF16), dims, preferred_element_type=F32)
    ah, bh = a.astype(BF16), b.astype(BF16)
    al, bl = (a - ah.astype(F32)).astype(BF16), (b - bh.astype(F32)).astype(BF16)
    d = lambda x, y: lax.dot_general(x, y, dims, preferred_element_type=F32)
    return d(ah, bh) + (d(ah, bl) + d(al, bh))


def _make_dots(mode, batched=False):
    if batched:
        nn_d, nt_d, tn_d = (((2,), (1,)), ((0,), (0,))), (((2,), (2,)), ((0,), (0,))), (((1,), (1,)), ((0,), (0,)))
    else:
        nn_d, nt_d, tn_d = (NN, ((), ())), (NT, ((), ())), (TN, ((), ()))

    @jax.custom_vjp
    def nn(a, b):
        return _dg(a, b, nn_d, mode)

    @jax.custom_vjp
    def nt(a, b):
        return _dg(a, b, nt_d, mode)

    @jax.custom_vjp
    def tn(a, b):
        return _dg(a, b, tn_d, mode)

    nn.defvjp(lambda a, b: (nn(a, b), (a, b)), lambda r, g: (nt(g, r[1]), tn(r[0], g)))
    nt.defvjp(lambda a, b: (nt(a, b), (a, b)), lambda r, g: (nn(g, r[1]), tn(g, r[0])))
    tn.defvjp(lambda a, b: (tn(a, b), (a, b)), lambda r, g: (nt(r[1], g), nn(r[0], g)))
    return nn, nt, tn


hnn = _make_dots("h")[0]
bbnn, bbnt, bbtn = _make_dots("b", True)
mbnn, mbnt, mbtn = _make_dots("m", True)
hbnt = _make_dots("h", True)[1]


def _split3(x, axis):
    x1 = x.astype(BF16)
    r1 = x - x1.astype(F32)
    x2 = r1.astype(BF16)
    x3 = (r1 - x2.astype(F32)).astype(BF16)
    return jnp.concatenate([x1, x2, x3], axis=axis)


@jax.custom_vjp
def _mask_dot(e3, x):
    return lax.dot_general(e3[0], _split3(x, 0), (NN, ((), ())), preferred_element_type=F32)


def _mask_dot_bwd(e3, g):
    dx = lax.dot_general(e3[1], _split3(g, 0), (TN, ((), ())), preferred_element_type=F32)
    return (jnp.zeros_like(e3[0]), jnp.zeros_like(e3[1])), dx


_mask_dot.defvjp(lambda e3, x: (_mask_dot(e3, x), e3), _mask_dot_bwd)


def _heads(a, n):
    return jnp.concatenate([a[None, :, h * HD:(h + 1) * HD] for h in range(n)], axis=0)


def _sigmoid(x):
    return jax.nn.sigmoid(x)


def _silu(x):
    return x * jax.nn.sigmoid(x)


def _softplus(x):
    return jnp.maximum(x, 0.0) + jnp.log(1.0 + jnp.exp(-jnp.abs(x)))


def _iota(shape, dim):
    return lax.broadcasted_iota(jnp.int32, shape, dim)


def _pick(n, prefs):
    for p in prefs:
        if n % p == 0:
            return p
    return n


def _mm(a, b, *, ta=False, tb=False, out_dtype=F32, name, b_view=None, out_split=0, act=False, gate=None, plus=None,
        cargo=(), exchange=None):
    if ta:
        k_dim, m_dim = a.shape
    else:
        m_dim, k_dim = a.shape
    if b_view is None:
        w_rows, w_cols = b.shape
    else:
        kind, layer = b_view
        nj, _, blk_r, blk_c = b.shape
        w_rows, w_cols = (blk_r, nj * blk_c) if kind == "cols" else (nj * blk_r, blk_c)
    n_dim = w_rows if tb else w_cols
    assert (w_cols if tb else w_rows) == k_dim
    tm = _pick(m_dim, (1024, 1056, 704, 640, 512, 384, 256, 128))
    tn = _pick(n_dim, (1024, 1056, 704, 640, 512, 384, 256, 128))
    tk = _pick(k_dim, (1024, 1056, 704, 512, 384, 256, 128))
    nk = k_dim // tk
    a_spec = pl.BlockSpec((tk, tm), lambda i, j, k: (k, i)) if ta else pl.BlockSpec((tm, tk), lambda i, j, k: (i, k))
    wb = (tn, tk) if tb else (tk, tn)
    w_idx = (lambda i, j, k: (j, k)) if tb else (lambda i, j, k: (k, j))
    if b_view is None:
        b_spec = pl.BlockSpec(wb, w_idx)
    elif kind == "cols":
        per = blk_c // wb[1]
        b_spec = pl.BlockSpec((None, None) + wb,
                              lambda i, j, k: (w_idx(i, j, k)[1] // per, layer, w_idx(i, j, k)[0], w_idx(i, j, k)[1] % per))
    else:
        per = blk_r // wb[0]
        b_spec = pl.BlockSpec((None, None) + wb,
                              lambda i, j, k: (w_idx(i, j, k)[0] // per, layer, w_idx(i, j, k)[0] % per, w_idx(i, j, k)[1]))
    if out_split:
        per_o = (n_dim // out_split) // tn
        out_spec = pl.BlockSpec((None, tm, tn), lambda i, j, k: (j // per_o, i, j % per_o))
        out_sds = jax.ShapeDtypeStruct((out_split, m_dim, n_dim // out_split), out_dtype)
    else:
        out_spec = pl.BlockSpec((tm, tn), lambda i, j, k: (i, j))
        out_sds = jax.ShapeDtypeStruct((m_dim, n_dim), out_dtype)
    dims = (((0 if ta else 1,), (1 if tb else 0,)), ((), ()))
    assert gate is None or plus is None
    extra = [e for e in (gate, plus) if e is not None]
    n_out = 2 if act else 1

    def finish(acc, refs):
        if act:
            refs[0][...] = acc.astype(refs[0].dtype)
            r = jnp.maximum(acc, 0.0)
            refs[1][...] = (r * r).astype(refs[1].dtype)
        elif gate is not None:
            refs[1][...] = (acc * (2.0 * jnp.maximum(refs[0][...].astype(F32), 0.0))).astype(refs[1].dtype)
        elif plus is not None:
            refs[1][...] = (refs[0][...] + acc).astype(refs[1].dtype)
        else:
            refs[0][...] = acc.astype(refs[0].dtype)

    grid = (m_dim // tm, n_dim // tn, nk)
    nc = len(cargo)

    def body(a_ref, b_ref, *rest):
        acc_ref = rest[-1]
        ids = [pl.program_id(d) for d in range(3)]
        outs, end_cargo = _cargo_bounds(
            rest[len(extra):-1], nc, n_out, exchange, (ids[0] == 0) & (ids[1] == 0) & (ids[2] == 0),
            (ids[0] == grid[0] - 1) & (ids[1] == grid[1] - 1) & (ids[2] == grid[2] - 1))
        refs = tuple(rest[:len(extra)]) + tuple(outs)
        part = lax.dot_general(a_ref[...], b_ref[...], dims, preferred_element_type=F32)
        if nk == 1:
            finish(part, refs)
        else:
            k = ids[2]

            @pl.when(k == 0)
            def _():
                acc_ref[...] = part

            @pl.when(k > 0)
            def _():
                acc_ref[...] += part

            @pl.when(k == nk - 1)
            def _():
                finish(acc_ref[...], refs)
        end_cargo()

    out = pl.pallas_call(
        body, name=name, grid=grid,
        in_specs=[a_spec, b_spec] + [pl.BlockSpec((tm, tn), lambda i, j, k: (i, j))] * len(extra) + [ANY] * nc,
        out_specs=[out_spec] * n_out + [ANY] * nc,
        out_shape=[out_sds] * n_out + (exchange[1](cargo) if nc else []),
        scratch_shapes=(exchange[2](nc) if nc else []) + [pltpu.VMEM((tm, tn) if nk > 1 else (8, 128), F32)],
        compiler_params=_cparams(dimension_semantics=("arbitrary",) * 3 if nc else ("parallel", "parallel", "arbitrary")),
    )(a, b, *extra, *cargo)
    if nc:
        return out
    return out if act else out[0]


def _mm_groups_nt(parts, b, name):
    m_dim, k_dim = parts[0].shape
    ng, _, n_dim, _ = b.shape
    assert len(parts) == ng and b.shape[3] == k_dim
    tm = _pick(m_dim, (1056, 704, 512, 384, 256, 128))

    def body(*refs):
        a_refs, b_ref, o_ref, acc_ref = refs[:ng], refs[ng], refs[ng + 1], refs[ng + 2]
        k = pl.program_id(1)
        for g in range(ng):
            @pl.when(k == g)
            def _(g=g):
                part = lax.dot_general(a_refs[g][...], b_ref[...], (NT, ((), ())), preferred_element_type=F32)
                if g == 0:
                    acc_ref[...] = part
                elif g < ng - 1:
                    acc_ref[...] += part
                else:
                    o_ref[...] = acc_ref[...] + part

    return pl.pallas_call(
        body, name=name, grid=(m_dim // tm, ng),
        in_specs=[pl.BlockSpec((tm, k_dim), lambda i, k: (i, 0))] * ng
        + [pl.BlockSpec((None, None, n_dim, k_dim), lambda i, k: (k, 0, 0, 0))],
        out_specs=pl.BlockSpec((tm, n_dim), lambda i, k: (i, 0)),
        out_shape=jax.ShapeDtypeStruct((m_dim, n_dim), F32),
        scratch_shapes=[pltpu.VMEM((tm, n_dim), F32)],
        compiler_params=_cparams(dimension_semantics=("parallel", "arbitrary")),
    )(*parts, b)


def _row_tile(t_pad, width):
    for tr in (528, 352, 176, 128, 64):
        if t_pad % tr == 0 and tr * width * 4 <= (3 << 19) and tr % 16 == 0:
            return tr
    return 64 if t_pad % 64 == 0 else t_pad


def _ln_res_fn(h, m, g, b):
    x = ALPHA * h + m
    mu = jnp.mean(x, axis=-1, keepdims=True)
    xc = x - mu
    var = jnp.mean(xc * xc, axis=-1, keepdims=True)
    return xc * lax.rsqrt(var + LN_EPS) * g + b


def _ln_res_fwd(h, m, g, b, name):
    t_pad = h.shape[0]
    tr = _row_tile(t_pad, D)

    def body(h_ref, m_ref, g_ref, b_ref, y_ref, yb_ref):
        y = _ln_res_fn(h_ref[...], m_ref[...], g_ref[...], b_ref[...])
        y_ref[...] = y
        yb_ref[...] = y.astype(BF16)

    row = pl.BlockSpec((tr, D), lambda i: (i, 0))
    par = pl.BlockSpec((1, D), lambda i: (0, 0))
    return pl.pallas_call(
        body, name=name, grid=(t_pad // tr,), in_specs=[row, row, par, par], out_specs=[row, row],
        out_shape=[jax.ShapeDtypeStruct((t_pad, D), F32), jax.ShapeDtypeStruct((t_pad, D), BF16)],
        compiler_params=_cparams(),
    )(h, m, g, b)


def _ln_res_bwd(h, m, g, b, dys, name):
    t_pad = h.shape[0]
    tr = _row_tile(t_pad, D)
    nd = len(dys)

    def body(h_ref, m_ref, g_ref, b_ref, *rest):
        d_refs, (dh_ref, dm_ref, dg_ref, db_ref) = rest[:nd], rest[nd:]
        _, vjp = jax.vjp(_ln_res_fn, h_ref[...], m_ref[...], g_ref[...], b_ref[...])
        dy = d_refs[0][...]
        for d_ref in d_refs[1:]:
            dy = dy + d_ref[...]
        dh, dm, dg, db = vjp(dy)
        dh_ref[...] = dh
        dm_ref[...] = dm.astype(BF16)

        @pl.when(pl.program_id(0) == 0)
        def _():
            dg_ref[...] = jnp.zeros_like(dg_ref)
            db_ref[...] = jnp.zeros_like(db_ref)

        dg_ref[...] += dg
        db_ref[...] += db

    row = pl.BlockSpec((tr, D), lambda i: (i, 0))
    par = pl.BlockSpec((1, D), lambda i: (0, 0))
    return pl.pallas_call(
        body, name=name, grid=(t_pad // tr,), in_specs=[row, row, par, par] + [row] * nd,
        out_specs=[row, row, par, par],
        out_shape=[jax.ShapeDtypeStruct((t_pad, D), F32), jax.ShapeDtypeStruct((t_pad, D), BF16),
                   jax.ShapeDtypeStruct((1, D), F32), jax.ShapeDtypeStruct((1, D), F32)],
        compiler_params=_cparams(),
    )(h, m, g, b, *dys)


def _grms_fn(o, z, g):
    y = o * lax.rsqrt(jnp.mean(o * o, axis=-1, keepdims=True) + RMS_EPS) * g
    return y * _silu(z)


def _grms_fwd(o, z_arr, z_blk0, g, name):
    t_pad, w = o.shape
    tr = _row_tile(t_pad, w)
    assert (z_blk0 * HD) % w == 0

    def body(o_ref, z_ref, g_ref, y_ref):
        for h in range(w // HD):
            c = slice(h * HD, (h + 1) * HD)
            y_ref[:, c] = _grms_fn(o_ref[:, c], z_ref[:, c], g_ref[...]).astype(BF16)

    return pl.pallas_call(
        body, name=name, grid=(t_pad // tr,),
        in_specs=[pl.BlockSpec((tr, w), lambda i: (i, 0)), pl.BlockSpec((tr, w), lambda i: (i, z_blk0 * HD // w)),
                  pl.BlockSpec((1, HD), lambda i: (0, 0))],
        out_specs=pl.BlockSpec((tr, w), lambda i: (i, 0)),
        out_shape=jax.ShapeDtypeStruct((t_pad, w), BF16), compiler_params=_cparams(),
    )(o, z_arr, g)


def _grms_bwd(o, z_arr, z_blk0, g, dy_arr, dy_blk0, name):
    t_pad, w = o.shape
    tr = _row_tile(t_pad, w)
    assert (z_blk0 * HD) % w == 0 and (dy_blk0 * HD) % w == 0

    def body(o_ref, z_ref, g_ref, dy_ref, do_ref, dz_ref, dg_ref):
        @pl.when(pl.program_id(0) == 0)
        def _():
            dg_ref[...] = jnp.zeros_like(dg_ref)

        for h in range(w // HD):
            c = slice(h * HD, (h + 1) * HD)
            _, vjp = jax.vjp(_grms_fn, o_ref[:, c], z_ref[:, c], g_ref[...])
            do, dz, dg = vjp(dy_ref[:, c])
            do_ref[:, c] = do
            dz_ref[:, c] = dz.astype(BF16)
            dg_ref[...] += dg

    blk = pl.BlockSpec((tr, w), lambda i: (i, 0))
    return pl.pallas_call(
        body, name=name, grid=(t_pad // tr,),
        in_specs=[blk, pl.BlockSpec((tr, w), lambda i: (i, z_blk0 * HD // w)), pl.BlockSpec((1, HD), lambda i: (0, 0)),
                  pl.BlockSpec((tr, w), lambda i: (i, dy_blk0 * HD // w))],
        out_specs=[blk, blk, pl.BlockSpec((1, HD), lambda i: (0, 0))],
        out_shape=[jax.ShapeDtypeStruct((t_pad, w), F32), jax.ShapeDtypeStruct((t_pad, w), BF16),
                   jax.ShapeDtypeStruct((1, HD), F32)],
        compiler_params=_cparams(),
    )(o, z_arr, g, dy_arr)


def _loss_fwd(y, tgt, first_row, name):
    t_pad = y.shape[0]
    tr = _row_tile(t_pad, D)

    def body(y_ref, t_ref, l_ref, dy_ref):
        rows = pl.program_id(0) * tr + _iota((tr, 1), 0)
        err = jnp.where(rows >= first_row, y_ref[...] - t_ref[...], 0.0)
        dy_ref[...] = err * (1.0 / D)

        @pl.when(pl.program_id(0) == 0)
        def _():
            l_ref[...] = jnp.zeros_like(l_ref)

        part = jnp.sum(jnp.sum(err * err, axis=1, keepdims=True), axis=0, keepdims=True)
        l_ref[...] += jnp.broadcast_to(part * (0.5 / D), l_ref.shape)

    row = pl.BlockSpec((tr, D), lambda i: (i, 0))
    return pl.pallas_call(
        body, name=name, grid=(t_pad // tr,), in_specs=[row, row],
        out_specs=[pl.BlockSpec((8, 128), lambda i: (0, 0)), row],
        out_shape=[jax.ShapeDtypeStruct((8, 128), F32), jax.ShapeDtypeStruct((t_pad, D), F32)],
        compiler_params=_cparams(),
    )(y, tgt)


def _assemble_bf16(parts, name):
    t_pad = parts[0].shape[0]
    widths = [p.shape[1] for p in parts]
    total = sum(widths)
    tr = _row_tile(t_pad, total)

    def body(*refs):
        o_ref = refs[-1]
        off = 0
        for ref, w in zip(refs[:-1], widths):
            o_ref[:, off:off + w] = ref[...].astype(BF16)
            off += w

    return pl.pallas_call(
        body, name=name, grid=(t_pad // tr,), in_specs=[pl.BlockSpec((tr, w), lambda i: (i, 0)) for w in widths],
        out_specs=pl.BlockSpec((tr, total), lambda i: (i, 0)),
        out_shape=jax.ShapeDtypeStruct((t_pad, total), BF16), compiler_params=_cparams(),
    )(*parts)


CONV_K = 4
HALO = 8
RT = 128


def _conv_fwd(p, blk0, w, mode, pad, name):
    t_pad = p.shape[0]
    nt = t_pad // RT
    scale = HD ** -0.5 if mode == "q" else 1.0

    def body(x_ref, w_ref, y_ref, xs_ref):
        xs_ref[0:HALO, :] = jnp.zeros((HALO, HD), F32)
        rows = _iota((t_pad, 1), 0)
        xs_ref[HALO:HALO + t_pad, :] = jnp.where(rows >= pad, x_ref[...], 0.0)
        wv = w_ref[...]

        def tile(i, carry):
            r0 = pl.multiple_of(i * RT, RT)
            ext = xs_ref[pl.ds(r0, RT + HALO), :]
            acc = ext[HALO:, :] * wv[3:4, :]
            for s in (1, 2, 3):
                acc = acc + pltpu.roll(ext, s, 0)[HALO:, :] * wv[3 - s:4 - s, :]
            y = _silu(acc)
            if mode != "v":
                y = y * lax.rsqrt(jnp.sum(y * y, axis=-1, keepdims=True) + L2_EPS) * scale
            y_ref[pl.ds(r0, RT), :] = y
            return carry

        lax.fori_loop(0, nt, tile, 0)

    return pl.pallas_call(
        body, name=name, grid=(GDN_H,),
        in_specs=[pl.BlockSpec((t_pad, HD), lambda h: (0, blk0 + h)), pl.BlockSpec((CONV_K, HD), lambda h: (0, h))],
        out_specs=pl.BlockSpec((t_pad, HD), lambda h: (0, h)),
        out_shape=jax.ShapeDtypeStruct((t_pad, GDN_H * HD), F32),
        scratch_shapes=[pltpu.VMEM((t_pad + HALO, HD), F32)],
        compiler_params=_cparams(),
    )(p, w)


def _conv_bwd(p, blk0, w, dn, mode, pad, name):
    t_pad = p.shape[0]
    nt = t_pad // RT
    scale = HD ** -0.5 if mode == "q" else 1.0

    def body(x_ref, w_ref, dn_ref, dx_ref, dw_ref, xs_ref, ds_ref):
        xs_ref[0:HALO, :] = jnp.zeros((HALO, HD), F32)
        xs_ref[HALO + t_pad:HALO + t_pad + 2 * HALO, :] = jnp.zeros((2 * HALO, HD), F32)
        ds_ref[t_pad:t_pad + HALO, :] = jnp.zeros((HALO, HD), F32)
        rows = _iota((t_pad, 1), 0)
        xs_ref[HALO:HALO + t_pad, :] = jnp.where(rows >= pad, x_ref[...], 0.0)
        ds_ref[0:t_pad, :] = dn_ref[...]
        wv = w_ref[...]

        def tile(i, dw):
            r0 = pl.multiple_of(i * RT, RT)
            ext = xs_ref[pl.ds(r0, RT + 2 * HALO), :]
            dn_e = ds_ref[pl.ds(r0, RT + HALO), :]
            xsh = [ext[HALO:, :]] + [pltpu.roll(ext, s, 0)[HALO:, :] for s in (1, 2, 3)]
            pre = xsh[0] * wv[3:4, :]
            for s in (1, 2, 3):
                pre = pre + xsh[s] * wv[3 - s:4 - s, :]
            sg = _sigmoid(pre)
            y = pre * sg
            if mode != "v":
                ss = jnp.sum(y * y, axis=-1, keepdims=True) + L2_EPS
                r = lax.rsqrt(ss)
                dy = scale * (dn_e * r - y * (r * r * r) * jnp.sum(dn_e * y, axis=-1, keepdims=True))
            else:
                dy = dn_e
            dpre = dy * (sg * (1.0 + pre * (1.0 - sg)))
            dx = dpre[:RT, :] * wv[3:4, :]
            for s in (1, 2, 3):
                dx = dx + pltpu.roll(dpre, RT + HALO - s, 0)[:RT, :] * wv[3 - s:4 - s, :]
            trow = r0 + _iota((RT, 1), 0)
            dx_ref[pl.ds(r0, RT), :] = jnp.where(trow >= pad, dx, 0.0)
            new = []
            for s in (0, 1, 2, 3):
                new.append(dw[s] + jnp.sum(dpre[:RT, :] * xsh[s][:RT, :], axis=0, keepdims=True))
            return tuple(new)

        z = jnp.zeros((1, HD), F32)
        dw = lax.fori_loop(0, nt, tile, (z, z, z, z))
        for s in (0, 1, 2, 3):
            dw_ref[3 - s:4 - s, :] = dw[s]

    return pl.pallas_call(
        body, name=name, grid=(GDN_H,),
        in_specs=[pl.BlockSpec((t_pad, HD), lambda h: (0, blk0 + h)), pl.BlockSpec((CONV_K, HD), lambda h: (0, h)),
                  pl.BlockSpec((t_pad, HD), lambda h: (0, h))],
        out_specs=[pl.BlockSpec((t_pad, HD), lambda h: (0, h)), pl.BlockSpec((CONV_K, HD), lambda h: (0, h))],
        out_shape=[jax.ShapeDtypeStruct((t_pad, GDN_H * HD), F32), jax.ShapeDtypeStruct((CONV_K, GDN_H * HD), F32)],
        scratch_shapes=[pltpu.VMEM((t_pad + 3 * HALO, HD), F32), pltpu.VMEM((t_pad + HALO, HD), F32)],
        compiler_params=_cparams(),
    )(p, w, dn)


@jax.custom_vjp
def _unit_lower_inv(m, bd, eye):
    md = m * bd
    low = m - md
    p2 = mbnn(md, md)
    p4 = mbnn(p2, p2)
    dinv = mbnn(mbnn(eye - md, eye + p2), eye + p4)
    n = mbnn(dinv, low)
    n2 = mbnn(n, n)
    n4 = mbnn(n2, n2)
    return mbnn(mbnn(mbnn(eye - n, eye + n2), eye + n4), dinv)


def _unit_lower_inv_bwd(res, g):
    t, bd, eye = res
    return -mbtn(t, mbnt(g, t)), jnp.zeros_like(bd), jnp.zeros_like(eye)


def _unit_lower_inv_fwd(m, bd, eye):
    t = _unit_lower_inv(m, bd, eye)
    return t, (t, bd, eye)


_unit_lower_inv.defvjp(_unit_lower_inv_fwd, _unit_lower_inv_bwd)


def _gdn_chunks(chunks, alog, dtb, s):
    nh = chunks[0][0].shape[0]
    ri = _iota((1, CH, CH), 1)
    ci = _iota((1, CH, CH), 2)
    causal = ri >= ci
    strict = ri > ci
    eye = (ri == ci).astype(F32)
    bd = ((ri >> 3) == (ci >> 3)).astype(F32)
    ltri = (_iota((CH, CH), 0) >= _iota((CH, CH), 1)).astype(F32)
    sel = (_iota((nh, 1, HD), 2) == _iota((nh, 1, HD), 0)).astype(F32)
    last = _iota((1, CH, 1), 1) == CH - 1

    beta, gc, gc_rows = [], [], []
    for _, _, _, bb, aa, valid in chunks:
        beta_all = jnp.where(valid, _sigmoid(bb), 0.0)
        g_all = jnp.where(valid, -jnp.exp(alog) * _softplus(aa + dtb), 0.0)
        gc_all = hnn(ltri, g_all)
        beta.append(jnp.sum(beta_all[None] * sel, axis=2, keepdims=True))
        gc.append(jnp.sum(gc_all[None] * sel, axis=2, keepdims=True))
        gc_rows.append(hbnt(jnp.broadcast_to(sel, (nh, CH, HD)), jnp.broadcast_to(gc_all[None], (nh, CH, HD))))
    cat = lambda xs: jnp.concatenate(xs, axis=0)
    q, k, v = (cat([c[j] for c in chunks]) for j in range(3))
    beta, gc, gc_rows = cat(beta), cat(gc), cat(gc_rows)
    gc_last = jnp.sum(jnp.where(last, gc, 0.0), axis=1, keepdims=True)
    decay = jnp.exp(jnp.where(causal, gc - gc_rows, NEG))
    egc = jnp.exp(gc)

    kb = k * beta
    m = jnp.where(strict, bbnt(kb, k) * decay, 0.0)
    t_inv = _unit_lower_inv(m, bd, eye)
    u = bbnn(t_inv, v * beta)
    w = bbnn(t_inv, kb * egc)
    a_intra = bbnt(q, k) * decay
    q_dec = q * egc
    k_dec = k * jnp.exp(gc_last - gc)
    g_tot = jnp.exp(gc_last)

    outs = []
    for n in range(len(chunks)):
        part = lambda a: a[n * nh:(n + 1) * nh]
        v_new = part(u) - bbnn(part(w), s)
        outs.append(bbnn(part(q_dec), s) + bbnn(part(a_intra), v_new))
        s = s * part(g_tot) + bbtn(part(k_dec), v_new)
    return outs, s


PAIR = 2 * CH


def _gdn_specs(npair, rev):
    cc = (lambda c: npair - 1 - c) if rev else (lambda c: c)
    wide = pl.BlockSpec((PAIR, GDN_H * HD), lambda c: (cc(c), 0))
    fix = lambda off: pl.BlockSpec((PAIR, HD), lambda c: (cc(c), off))
    par = pl.BlockSpec((1, HD), lambda c: (0, 0))
    state = pl.BlockSpec((1, GDN_H, HD, HD), lambda c: (cc(c), 0, 0, 0))
    return wide, fix, par, state


def _store_heads(ref, a, rows=slice(None)):
    for h in range(a.shape[0]):
        ref[rows, h * HD:(h + 1) * HD] = a[h]


def _chunk_rows(half):
    return slice(half * CH, (half + 1) * CH)


def _chunk_valid(pair, half, pad):
    return ((2 * pair + half) * CH + _iota((CH, 1), 0)) >= pad


def _gdn_fwd(qn, kn, vn, p, alog, dtb, pad, name, cargo=(), exchange=None):
    t_pad = qn.shape[0]
    npair = t_pad // PAIR
    wide, fix, par, state = _gdn_specs(npair, False)
    n = len(cargo)

    def body(q_ref, k_ref, v_ref, bb_ref, aa_ref, al_ref, dt_ref, *rest):
        c = pl.program_id(0)
        s_ref = rest[-1]
        (o_ref, ss_ref), end_cargo = _cargo_bounds(rest[:-1], n, 2, exchange, c == 0, c == npair - 1)

        @pl.when(c == 0)
        def _():
            s_ref[...] = jnp.zeros_like(s_ref)

        s = s_ref[...]
        ss_ref[0] = s
        rows = [_chunk_rows(half) for half in (0, 1)]
        chunks = [(_heads(q_ref[r, :], GDN_H), _heads(k_ref[r, :], GDN_H), _heads(v_ref[r, :], GDN_H),
                   bb_ref[r, :], aa_ref[r, :], _chunk_valid(c, half, pad)) for half, r in enumerate(rows)]
        outs, s = _gdn_chunks(chunks, al_ref[...], dt_ref[...], s)
        for r, o in zip(rows, outs):
            _store_heads(o_ref, o, r)
        s_ref[...] = s
        end_cargo()

    return pl.pallas_call(
        body, name=name, grid=(npair,),
        in_specs=[wide, wide, wide, fix(16), fix(17), par, par] + [ANY] * n,
        out_specs=[wide, state] + [ANY] * n,
        out_shape=[jax.ShapeDtypeStruct((t_pad, GDN_H * HD), F32), jax.ShapeDtypeStruct((npair, GDN_H, HD, HD), F32)]
        + (exchange[1](cargo) if n else []),
        scratch_shapes=(exchange[2](n) if n else []) + [pltpu.VMEM((GDN_H, HD, HD), F32)],
        compiler_params=_cparams(),
    )(qn, kn, vn, p, p, alog, dtb, *cargo)


def _gdn_bwd(qn, kn, vn, p, alog, dtb, ssave, do, pad, name, cargo=(), exchange=None):
    t_pad = qn.shape[0]
    npair = t_pad // PAIR
    wide, fix, par, state = _gdn_specs(npair, True)
    n = len(cargo)

    def body(q_ref, k_ref, v_ref, bb_ref, aa_ref, al_ref, dt_ref, ss_ref, do_ref, *rest):
        c = pl.program_id(0)
        ds_ref = rest[-1]
        (dq_ref, dk_ref, dv_ref, dbb_ref, daa_ref, dal_ref, ddt_ref), end_cargo = _cargo_bounds(
            rest[:-1], n, 7, exchange, c == 0, c == npair - 1)

        @pl.when(c == 0)
        def _():
            ds_ref[...] = jnp.zeros_like(ds_ref)
            dal_ref[...] = jnp.zeros_like(dal_ref)
            ddt_ref[...] = jnp.zeros_like(ddt_ref)

        ra, rb = _chunk_rows(0), _chunk_rows(1)
        va, vb = _chunk_valid(npair - 1 - c, 0, pad), _chunk_valid(npair - 1 - c, 1, pad)

        def pair(qa, ka, va_, ba, aa, qb, kb, vb_, bb, ab, al, dt, s):
            (oa, ob), s = _gdn_chunks([(qa, ka, va_, ba, aa, va), (qb, kb, vb_, bb, ab, vb)], al, dt, s)
            return oa, ob, s

        ins = [f(ref[r, :]) for r in (ra, rb)
               for ref, f in ((q_ref, lambda a: _heads(a, GDN_H)), (k_ref, lambda a: _heads(a, GDN_H)),
                              (v_ref, lambda a: _heads(a, GDN_H)), (bb_ref, lambda a: a), (aa_ref, lambda a: a))]
        _, vjp = jax.vjp(pair, *ins, al_ref[...], dt_ref[...], ss_ref[0])
        g = vjp((_heads(do_ref[ra, :], GDN_H), _heads(do_ref[rb, :], GDN_H), ds_ref[...]))
        for r, (dq, dk, dv, dbb, daa) in ((ra, g[0:5]), (rb, g[5:10])):
            _store_heads(dq_ref, dq, r)
            _store_heads(dk_ref, dk, r)
            _store_heads(dv_ref, dv, r)
            dbb_ref[r, :] = dbb
            daa_ref[r, :] = daa
        dal_ref[...] += g[10]
        ddt_ref[...] += g[11]
        ds_ref[...] = g[12]
        end_cargo()

    sds = jax.ShapeDtypeStruct
    return pl.pallas_call(
        body, name=name, grid=(npair,),
        in_specs=[wide, wide, wide, fix(16), fix(17), par, par, state, wide] + [ANY] * n,
        out_specs=[wide, wide, wide, fix(0), fix(0), par, par] + [ANY] * n,
        out_shape=[sds((t_pad, GDN_H * HD), F32)] * 3 + [sds((t_pad, HD), F32)] * 2 + [sds((1, HD), F32)] * 2
        + (exchange[1](cargo) if n else []),
        scratch_shapes=(exchange[2](n) if n else []) + [pltpu.VMEM((GDN_H, HD, HD), F32)],
        compiler_params=_cparams(),
    )(qn, kn, vn, p, p, alog, dtb, ssave, do, *cargo)


SB_Q0, SB_K0, SB_V0 = 18, 22, 26
SB_SCALE = SB_DH ** -0.5
SB_NB = 8


def _sb_terms(z, allowed):
    e = jnp.exp(-jnp.abs(z))
    den = 1.0 + e
    raw = -jnp.maximum(z, 0.0) - jnp.log(den)
    l1m = raw if allowed is None else jnp.where(allowed, raw, 0.0)
    ls = z + raw
    return l1m, ls, jnp.exp(ls)


def _sb_passes(i, step, carry):
    total = i + 1
    sized = lambda done: [functools.partial(step, done, masked=True, nb=nb) for nb in range(1, SB_NB + 1)]

    def several(c):
        n_mid = (total - SB_NB - 1) // SB_NB
        c = step(0, c, masked=True, nb=SB_NB)
        c = lax.fori_loop(0, n_mid, lambda t, cc: step(SB_NB * (1 + t), cc, masked=False, nb=SB_NB), c)
        done = SB_NB * (1 + n_mid)
        return lax.switch(total - done - 1, sized(done), c)

    return lax.cond(total <= SB_NB, lambda c: lax.switch(total - 1, sized(0), c), several, carry)


def _sb_stack(a, i):
    first = _iota((1, HD), 1) < SB_DH
    a2 = jnp.concatenate([jnp.where(first, a, 0.0), jnp.where(first, 0.0, a)], axis=0).astype(BF16)
    rq = i * QB + _iota((QB, 1), 0)
    return a2, jnp.concatenate([rq, rq], axis=0), first


def _hi_lo(a):
    hi = a.astype(BF16)
    lo = (a - hi.astype(F32)).astype(BF16)
    return jnp.concatenate([hi, lo], axis=1)


def _cargo_bounds(refs, n, n_out, exchange, first, last):
    outs = refs[n:n + n_out]
    if not n:
        return outs, lambda: None
    ex = exchange[0](refs[:n], refs[n + n_out:2 * n + n_out], *refs[2 * n + n_out:])

    @pl.when(first)
    def _():
        ex.start()

    def finish():
        @pl.when(last)
        def _():
            ex.wait()

    return outs, finish


def _sb_fwd(p, pad, name, cargo=(), exchange=None):
    t_pad = p.shape[0]
    nq = t_pad // QB
    n = len(cargo)

    def body(q_ref, k_ref, v_ref, *rest):
        i = pl.program_id(1)
        pr = pl.program_id(0)
        (o_ref, r_ref), end_cargo = _cargo_bounds(rest, n, 2, exchange, (pr == 0) & (i == 0),
                                                  (pr == SB_H // 2 - 1) & (i == nq - 1))
        q2, rowq, first = _sb_stack(q_ref[...] * SB_SCALE, i)
        tri = (_iota((QB, QB), 0) > _iota((QB, QB), 1)).astype(BF16)
        upper2 = jnp.concatenate([jnp.concatenate([tri, tri], axis=0), jnp.ones((2 * QB, QB), BF16)], axis=1)

        def chain(kb, masked):
            start = pl.multiple_of(kb * QB, QB)
            kblk = k_ref[pl.ds(start, QB), :].astype(BF16)
            vblk = v_ref[pl.ds(start, QB), :].astype(BF16)
            z = lax.dot_general(q2, kblk, (NT, ((), ())), preferred_element_type=F32)
            colk = kb * QB + _iota((1, QB), 1)
            al = ((colk < rowq) & (colk >= pad)) if masked else None
            l1m, ls, _ = _sb_terms(z, al)
            sums = lax.dot_general(_hi_lo(l1m), upper2, (NN, ((), ())), preferred_element_type=F32)
            return al, ls, sums[:, :QB], sums[:, QB:], vblk

        def step(done, carry, masked, nb):
            o_acc, run = carry
            ws, vs = [], []
            for n in range(nb):
                al, ls, suf, rs, vblk = chain(i - done - n, masked)
                wgt = jnp.exp(ls + suf + run)
                ws.append((wgt if al is None else jnp.where(al, wgt, 0.0)).astype(BF16))
                vs.append(vblk)
                run = run + rs
            o_acc = o_acc + lax.dot_general(jnp.concatenate(ws, axis=1), jnp.concatenate(vs, axis=0),
                                            (NN, ((), ())), preferred_element_type=F32)
            return o_acc, run

        o_acc, run = _sb_passes(i, step, (jnp.zeros((2 * QB, HD), F32), jnp.zeros((2 * QB, QB), F32)))
        o_ref[...] = jnp.where(first, o_acc[:QB], o_acc[QB:]).astype(BF16)
        r_ref[...] = jnp.where(first, run[:QB], run[QB:])
        end_cargo()

    full = lambda off: pl.BlockSpec((t_pad, HD), lambda pr, i: (0, off + pr))
    blk = pl.BlockSpec((QB, HD), lambda pr, i: (i, pr))
    return pl.pallas_call(
        body, name=name, grid=(SB_H // 2, nq),
        in_specs=[pl.BlockSpec((QB, HD), lambda pr, i: (i, SB_Q0 + pr)), full(SB_K0), full(SB_V0)] + [ANY] * n,
        out_specs=[blk, blk] + [ANY] * n,
        out_shape=[jax.ShapeDtypeStruct((t_pad, SB_H * SB_DH), BF16), jax.ShapeDtypeStruct((t_pad, SB_H * SB_DH), F32)]
        + (exchange[1](cargo) if n else []),
        scratch_shapes=exchange[2](n) if n else [],
        compiler_params=_cparams(),
    )(p, p, p, *cargo)


def _sb_bwd(p, rtot, dy, dy_blk0, pad, name, cargo=(), exchange=None):
    t_pad = p.shape[0]
    nq = t_pad // QB
    n = len(cargo)

    def body(q_ref, k_ref, v_ref, r_ref, do_ref, *rest):
        i = pl.program_id(1)
        pr = pl.program_id(0)
        dkt_ref, dvt_ref = rest[-2:]
        (dq_ref, dk_ref, dv_ref), end_cargo = _cargo_bounds(rest[:-2], n, 3, exchange, (pr == 0) & (i == 0),
                                                            (pr == SB_H // 2 - 1) & (i == nq - 1))

        @pl.when(i == 0)
        def _():
            dkt_ref[...] = jnp.zeros_like(dkt_ref)
            dvt_ref[...] = jnp.zeros_like(dvt_ref)

        q2, rowq, first = _sb_stack(q_ref[...] * SB_SCALE, i)
        do2, _, _ = _sb_stack(do_ref[...], i)
        q2t = jnp.transpose(q2.astype(F32)).astype(BF16)
        do2t = jnp.transpose(do2.astype(F32)).astype(BF16)
        rt = r_ref[...]
        lane = _iota((1, HD), 1)
        rcol = jnp.concatenate([jnp.sum(jnp.where(lane == 0, rt, 0.0), axis=1, keepdims=True),
                                jnp.sum(jnp.where(lane == SB_DH, rt, 0.0), axis=1, keepdims=True)], axis=0)
        rj = _iota((QB, QB), 0)
        cs = _iota((QB, QB), 1)
        tri_u = (rj > cs).astype(BF16)
        tri_l = (rj < cs).astype(BF16)
        ones2 = jnp.ones((2 * QB, QB), BF16)
        upper2 = jnp.concatenate([jnp.concatenate([tri_u, tri_u], axis=0), ones2], axis=1)
        lower2 = jnp.concatenate([jnp.concatenate([tri_l, tri_l], axis=0), ones2], axis=1)
        rcol = jnp.broadcast_to(rcol, (2 * QB, QB))

        def chain(kb, masked):
            start = pl.multiple_of(kb * QB, QB)
            kblk = k_ref[pl.ds(start, QB), :].astype(BF16)
            vblk = v_ref[pl.ds(start, QB), :].astype(BF16)
            z = lax.dot_general(q2, kblk, (NT, ((), ())), preferred_element_type=F32)
            colk = kb * QB + _iota((1, QB), 1)
            al = ((colk < rowq) & (colk >= pad)) if masked else None
            l1m, ls, sg = _sb_terms(z, al)
            dwgt = lax.dot_general(do2, vblk, (NT, ((), ())), preferred_element_type=F32)
            sums = lax.dot_general(_hi_lo(l1m), upper2, (NN, ((), ())), preferred_element_type=F32)
            return kb, kblk, al, ls, sums[:, :QB], sums[:, QB:], dwgt, sg

        def finish(c, seen, gseen):
            kb, kblk, al, ls, suf, rs, dwgt, sg = c
            wgt = jnp.exp(ls + suf + (rcol - seen - rs))
            if al is not None:
                wgt = jnp.where(al, wgt, 0.0)
            dl = dwgt * wgt
            sums = lax.dot_general(_hi_lo(dl), lower2, (NN, ((), ())), preferred_element_type=F32)
            gpre = gseen + sums[:, :QB]
            dz = dl - sg * (dl + gpre)
            if al is not None:
                dz = jnp.where(al, dz, 0.0)
            dz = dz.astype(BF16)
            dkt_ref[kb] += lax.dot_general(q2t, dz, (NN, ((), ())), preferred_element_type=F32)
            dvt_ref[kb] += lax.dot_general(do2t, wgt.astype(BF16), (NN, ((), ())), preferred_element_type=F32)
            return dz, seen + rs, gseen + sums[:, QB:]

        def step(done, carry, masked, nb):
            dq_acc, seen, gseen = carry
            cs_ = [chain(done + n, masked) for n in range(nb)]
            dzs = []
            for c in cs_:
                dz, seen, gseen = finish(c, seen, gseen)
                dzs.append(dz)
            dq_acc = dq_acc + lax.dot_general(jnp.concatenate(dzs, axis=1), jnp.concatenate([c[1] for c in cs_], axis=0),
                                              (NN, ((), ())), preferred_element_type=F32)
            return dq_acc, seen, gseen

        zc = jnp.zeros((2 * QB, QB), F32)
        dq_acc, _, _ = _sb_passes(i, step, (jnp.zeros((2 * QB, HD), F32), zc, zc))
        dq_ref[...] = jnp.where(first, dq_acc[:QB], dq_acc[QB:]) * SB_SCALE

        @pl.when(i == nq - 1)
        def _():
            for kb in range(nq):
                dk_ref[kb * QB:(kb + 1) * QB, :] = jnp.transpose(dkt_ref[kb])
                dv_ref[kb * QB:(kb + 1) * QB, :] = jnp.transpose(dvt_ref[kb])

        end_cargo()

    full_in = lambda off: pl.BlockSpec((t_pad, HD), lambda pr, i: (0, off + pr))
    full_out = pl.BlockSpec((t_pad, HD), lambda pr, i: (0, pr))
    blk = pl.BlockSpec((QB, HD), lambda pr, i: (i, pr))
    sds = jax.ShapeDtypeStruct((t_pad, SB_H * SB_DH), F32)
    return pl.pallas_call(
        body, name=name, grid=(SB_H // 2, nq),
        in_specs=[pl.BlockSpec((QB, HD), lambda pr, i: (i, SB_Q0 + pr)), full_in(SB_K0), full_in(SB_V0), blk,
                  pl.BlockSpec((QB, HD), lambda pr, i: (i, dy_blk0 + pr))] + [ANY] * n,
        out_specs=[blk, full_out, full_out] + [ANY] * n,
        out_shape=[sds, sds, sds] + (exchange[1](cargo) if n else []),
        scratch_shapes=(exchange[2](n) if n else []) + [pltpu.VMEM((nq, HD, QB), F32)] * 2,
        compiler_params=_cparams(),
    )(p, p, p, rtot, dy, *cargo)


HG_LEVELS = 6


def _hg_prefix_matrix():
    t = np.arange(CH)[:, None]
    j = np.arange(CH)[None, :]
    groups = [(j <= t)]
    for lvl in range(1, HG_LEVELS + 1):
        half = CH >> lvl
        e = (t // (2 * half)) * (2 * half) + half - 1
        groups.append(j <= e)
    groups.append(np.ones((8, CH), bool))
    e = np.concatenate(groups, axis=0).astype(np.float32)
    return np.concatenate([e, e, e], axis=1), np.concatenate([e, e, e], axis=0)


HG_G = 4


def _hg_chunk(qr, fr, iv, r0, r1, st, valid, ecat):
    g = st.shape[0]
    mx = jnp.maximum(r0, r1)
    e0 = jnp.exp(r0 - mx)
    e1 = jnp.exp(r1 - mx)
    lb = e1 / (e0 + e1)
    fg = lb + (1.0 - lb) * _sigmoid(fr)
    logf = jnp.where(valid, jnp.log(fg), 0.0)
    kk = jnp.where(valid, 1.0 - fg, 0.0)
    q = jnp.where(valid, _silu(qr), 0.0)
    v = _heads(jnp.where(valid, iv, 0.0), g)

    pre = _mask_dot(ecat, logf)
    b = pre[0:CH]
    b_last = jnp.max(pre[(HG_LEVELS + 1) * CH:], axis=0, keepdims=True)
    row = _iota((CH, 1), 0)
    ri = _iota((1, CH, CH), 1)
    ci = _iota((1, CH, CH), 2)
    a = jnp.where(ri == ci, jnp.sum(_heads(q * kk, g), axis=2, keepdims=True), 0.0)
    for lvl in range(1, HG_LEVELS + 1):
        half = CH >> lvl
        m = pre[lvl * CH:(lvl + 1) * CH]
        low = (row & half) != 0
        dec = jnp.exp(jnp.where(low, b - m, m - b))
        qt = jnp.where(low, q * dec, 0.0)
        kt = jnp.where(low, 0.0, kk * dec)
        same = (ri >> (7 - lvl)) == (ci >> (7 - lvl))
        a = a + jnp.where(same, bbnt(_heads(qt, g), _heads(kt, g)), 0.0)
    o = bbnt(_heads(q * jnp.exp(b), g), st) + bbnn(a, v)
    kd = kk * jnp.exp(b_last - b)
    st_new = st * _heads(jnp.exp(b_last), g) + bbtn(v, _heads(kd, g))
    return o, st_new


def _hg_specs(npair, rev):
    cc = (lambda c: npair - 1 - c) if rev else (lambda c: c)
    ng = HG_H // HG_G
    blk = lambda off: pl.BlockSpec((PAIR, HG_G * HD), lambda h, c: (cc(c), off * ng + h))
    lbs = pl.BlockSpec((2, HG_G * HD), lambda h, c: (0, h))
    state = pl.BlockSpec((1, HG_G, HD, HD), lambda h, c: (cc(c), h, 0, 0))
    return ng, blk, lbs, state


def _hg_fwd(p, lbraw, ecat, pad, name):
    t_pad = p.shape[0]
    npair = t_pad // PAIR
    ng, blk, lbs, state = _hg_specs(npair, False)

    def body(q_ref, f_ref, i_ref, lb_ref, e_ref, et_ref, o_ref, ss_ref, s_ref):
        c = pl.program_id(1)

        @pl.when(c == 0)
        def _():
            s_ref[...] = jnp.zeros_like(s_ref)

        st = s_ref[...]
        ss_ref[0] = st
        for half in (0, 1):
            r = _chunk_rows(half)
            o, st = _hg_chunk(q_ref[r, :], f_ref[r, :], i_ref[r, :], lb_ref[0:1, :], lb_ref[1:2, :], st,
                              _chunk_valid(c, half, pad), (e_ref[...], et_ref[...]))
            _store_heads(o_ref, o, r)
        s_ref[...] = st

    return pl.pallas_call(
        body, name=name, grid=(ng, npair),
        in_specs=[blk(0), blk(1), blk(2), lbs] + [pl.BlockSpec(e.shape, lambda h, c: (0, 0)) for e in ecat],
        out_specs=[blk(0), state],
        out_shape=[jax.ShapeDtypeStruct((t_pad, HG_H * HD), F32), jax.ShapeDtypeStruct((npair, HG_H, HD, HD), F32)],
        scratch_shapes=[pltpu.VMEM((HG_G, HD, HD), F32)],
        compiler_params=_cparams(),
    )(p, p, p, lbraw, *ecat)


def _hg_bwd(p, lbraw, ecat, ssave, do, pad, name, cargo=(), exchange=None):
    t_pad = p.shape[0]
    npair = t_pad // PAIR
    ng, blk, lbs, state = _hg_specs(npair, True)
    n = len(cargo)

    def body(q_ref, f_ref, i_ref, lb_ref, e_ref, et_ref, ss_ref, do_ref, *rest):
        c = pl.program_id(1)
        hg = pl.program_id(0)
        ds_ref = rest[-1]
        (dq_ref, df_ref, di_ref, dlb_ref), end_cargo = _cargo_bounds(
            rest[:-1], n, 4, exchange, (hg == 0) & (c == 0), (hg == ng - 1) & (c == npair - 1))

        @pl.when(c == 0)
        def _():
            ds_ref[...] = jnp.zeros_like(ds_ref)
            dlb_ref[...] = jnp.zeros_like(dlb_ref)

        ra, rb = _chunk_rows(0), _chunk_rows(1)
        va, vb = _chunk_valid(npair - 1 - c, 0, pad), _chunk_valid(npair - 1 - c, 1, pad)
        ecv = (e_ref[...], et_ref[...])

        def pair(qa, fa, ia, qb, fb, ib, r0, r1, st):
            oa, st = _hg_chunk(qa, fa, ia, r0, r1, st, va, ecv)
            ob, st = _hg_chunk(qb, fb, ib, r0, r1, st, vb, ecv)
            return oa, ob, st

        ins = [ref[r, :] for r in (ra, rb) for ref in (q_ref, f_ref, i_ref)]
        _, vjp = jax.vjp(pair, *ins, lb_ref[0:1, :], lb_ref[1:2, :], ss_ref[0])
        g = vjp((_heads(do_ref[ra, :], HG_G), _heads(do_ref[rb, :], HG_G), ds_ref[...]))
        for r, (dq, df, di) in ((ra, g[0:3]), (rb, g[3:6])):
            dq_ref[r, :] = dq.astype(BF16)
            df_ref[r, :] = df.astype(BF16)
            di_ref[r, :] = di.astype(BF16)
        dlb_ref[0:1, :] += g[6]
        dlb_ref[1:2, :] += g[7]
        ds_ref[...] = g[8]
        end_cargo()

    sds = jax.ShapeDtypeStruct((t_pad, HG_H * HD), BF16)
    return pl.pallas_call(
        body, name=name, grid=(ng, npair),
        in_specs=[blk(0), blk(1), blk(2), lbs] + [pl.BlockSpec(e.shape, lambda h, c: (0, 0)) for e in ecat]
        + [state, blk(0)] + [ANY] * n,
        out_specs=[blk(0), blk(0), blk(0), lbs] + [ANY] * n,
        out_shape=[sds, sds, sds, jax.ShapeDtypeStruct((2, HG_H * HD), F32)] + (exchange[1](cargo) if n else []),
        scratch_shapes=(exchange[2](n) if n else []) + [pltpu.VMEM((HG_G, HD, HD), F32)],
        compiler_params=_cparams(),
    )(p, p, p, lbraw, *ecat, ssave, do, *cargo)


def _pad_ab_cols(w):
    z = jnp.zeros((w.shape[0], HD - GDN_H), w.dtype)
    return jnp.concatenate([w[:, :2048], w[:, 2048:2052], z, w[:, 2052:2056], z, w[:, 2056:]], axis=1)


def _unpad_ab_cols(w):
    return jnp.concatenate([w[:, :2048], w[:, 2048:2052], w[:, 2176:2180], w[:, 2304:]], axis=1)


def _lane_pad(v):
    return jnp.pad(v, ((0, 0), (0, HD - v.shape[1])))


def _mlp_fwd(hb, w1, w2, layer):
    a, r = _mm(hb, w1, b_view=("cols", layer), out_dtype=BF16, act=True, name=f"mlp_up_{layer}")
    m = _mm(r, w2, b_view=("rows", layer), name=f"mlp_down_{layer}")
    return a, r, m


def _mlp_bwd(hb, a, r, dmb, w1, w2, layer):
    da = _mm(dmb, w2, tb=True, b_view=("rows", layer), out_dtype=BF16, gate=a, name=f"mlp_down_dx_{layer}")
    dw2 = _mm(r, dmb, ta=True, out_dtype=BF16, name=f"mlp_down_dw_{layer}")
    dh = _mm(da, w1, tb=True, b_view=("cols", layer), name=f"mlp_up_dx_{layer}")
    dw1 = _mm(hb, da, ta=True, out_dtype=BF16, out_split=N_CHIP, name=f"mlp_up_dw_{layer}")
    return dh, dw1, dw2


def _local_step(h0, tgt, w, pad, late=None):
    row = lambda a, i: a[i:i + 1]
    ecat = tuple(jnp.asarray(e, dtype=BF16) for e in _hg_prefix_matrix())
    cw = [w["conv_w"][:, i * 512:(i + 1) * 512] for i in range(3)]
    alog, dtb = _lane_pad(w["a_log"]), _lane_pad(w["dt_bias"])

    h0b = h0.astype(BF16)
    p0 = _mm(h0b, w["ab_w_in"], name="ab_in")
    qn = _conv_fwd(p0, 0, cw[0], "q", pad, "conv_q")
    kn = _conv_fwd(p0, 4, cw[1], "k", pad, "conv_k")
    vn = _conv_fwd(p0, 8, cw[2], "v", pad, "conv_v")
    if late is None:
        oa_raw, ss0 = _gdn_fwd(qn, kn, vn, p0, alog, dtb, pad, "gdn_fwd")
        ob, rtot = _sb_fwd(p0, pad, "sb_fwd")
    else:
        oa_raw, ss0, g_cin, g_cout = _gdn_fwd(qn, kn, vn, p0, alog, dtb, pad, "gdn_fwd",
                                              cargo=[late["c_w_in"], late["c_w_out"]], exchange=GATHER)
        ob, rtot, g_about, g_w1, g_w2 = _sb_fwd(p0, pad, "sb_fwd", exchange=GATHER,
                                                cargo=[late["ab_w_out"], late["mlp_w1"], late["mlp_w2"]])
        w = dict(w, ab_w_out=g_about.reshape(D, D), c_w_in=g_cin, c_w_out=g_cout.reshape(D, D), mlp_w1=g_w1, mlp_w2=g_w2)
    oa = _grms_fwd(oa_raw, p0, 12, w["ab_gnorm_g"], "gdn_gate")
    ycat = jnp.concatenate([oa, ob], axis=1)
    mix0 = _mm(ycat, w["ab_w_out"], name="ab_out")
    h1, h1b = _ln_res_fwd(h0, mix0, row(w["ln_mix_g"], 0), row(w["ln_mix_b"], 0), "ln_mix_0")
    a0, r0, m0 = _mlp_fwd(h1b, w["mlp_w1"], w["mlp_w2"], 0)
    h2, h2b = _ln_res_fwd(h1, m0, row(w["ln_ffn_g"], 0), row(w["ln_ffn_b"], 0), "ln_ffn_0")
    p1 = _mm(h2b, w["c_w_in"], b_view=("cols", 0), name="c_in")
    oc_raw, ss1 = _hg_fwd(p1, w["c_lb_raw"], ecat, pad, "hg_fwd")
    yc = _grms_fwd(oc_raw, p1, 3 * HG_H, w["c_gnorm_g"], "hg_gate")
    mix1 = _mm(yc, w["c_w_out"], name="c_out")
    h3, h3b = _ln_res_fwd(h2, mix1, row(w["ln_mix_g"], 1), row(w["ln_mix_b"], 1), "ln_mix_1")
    a1, r1, m1 = _mlp_fwd(h3b, w["mlp_w1"], w["mlp_w2"], 1)
    h4, _ = _ln_res_fwd(h3, m1, row(w["ln_ffn_g"], 1), row(w["ln_ffn_b"], 1), "ln_ffn_1")
    loss, dh4 = _loss_fwd(h4, tgt, pad + N_META, "loss")

    dh3a, dm1b, dfg1, dfb1 = _ln_res_bwd(h3, m1, row(w["ln_ffn_g"], 1), row(w["ln_ffn_b"], 1), [dh4], "ln_ffn_bwd_1")
    dh3b, dw1_1, dw2_1 = _mlp_bwd(h3b, a1, r1, dm1b, w["mlp_w1"], w["mlp_w2"], 1)
    dh2a, dmix1b, dmg1, dmb1 = _ln_res_bwd(h2, mix1, row(w["ln_mix_g"], 1), row(w["ln_mix_b"], 1), [dh3a, dh3b], "ln_mix_bwd_1")
    dyc = _mm(dmix1b, w["c_w_out"], tb=True, name="c_out_dx")
    dwco = _mm(yc, dmix1b, ta=True, out_dtype=BF16, name="c_out_dw")
    doc, dzc, dcg = _grms_bwd(oc_raw, p1, 3 * HG_H, w["c_gnorm_g"], dyc, 0, "hg_gate_bwd")
    landed = {}
    rows4 = lambda a: a.reshape(N_CHIP, -1, D)
    if late is None:
        dq1, df1, di1, dlb = _hg_bwd(p1, w["c_lb_raw"], ecat, ss1, doc, pad, "hg_bwd")
    else:
        dq1, df1, di1, dlb, landed["w1_1"] = _hg_bwd(
            p1, w["c_lb_raw"], ecat, ss1, doc, pad, "hg_bwd", cargo=[dw1_1], exchange=SCATTER)
    dp1 = [dq1, df1, di1, dzc]
    dh2b = _mm_groups_nt(dp1, w["c_w_in"], "c_in_dx")
    dwc = jnp.stack([_mm(h2b, d, ta=True, out_dtype=BF16, name=f"c_in_dw_{i}") for i, d in enumerate(dp1)])
    dh1a, dm0b, dfg0, dfb0 = _ln_res_bwd(h1, m0, row(w["ln_ffn_g"], 0), row(w["ln_ffn_b"], 0), [dh2a, dh2b], "ln_ffn_bwd_0")
    dh1b, dw1_0, dw2_0 = _mlp_bwd(h1b, a0, r0, dm0b, w["mlp_w1"], w["mlp_w2"], 0)
    dh0a, dmix0b, dmg0, dmb0 = _ln_res_bwd(h0, mix0, row(w["ln_mix_g"], 0), row(w["ln_mix_b"], 0), [dh1a, dh1b], "ln_mix_bwd_0")
    dycat = _mm(dmix0b, w["ab_w_out"], tb=True, name="ab_out_dx")
    dwabo = _mm(ycat, dmix0b, ta=True, out_dtype=BF16, name="ab_out_dw")
    doa, dza, dag = _grms_bwd(oa_raw, p0, 12, w["ab_gnorm_g"], dycat, 0, "gdn_gate_bwd")
    if late is None:
        dqn, dkn, dvn, dbb, daa, dal, ddt = _gdn_bwd(qn, kn, vn, p0, alog, dtb, ss0, doa, pad, "gdn_bwd")
        dqb, dkb, dvb = _sb_bwd(p0, rtot, dycat, 4, pad, "sb_bwd")
    else:
        dqn, dkn, dvn, dbb, daa, dal, ddt, landed["c_w_in"] = _gdn_bwd(
            qn, kn, vn, p0, alog, dtb, ss0, doa, pad, "gdn_bwd", cargo=[dwc], exchange=SCATTER)
        (dqb, dkb, dvb, landed["w1_0"], landed["w2_0"], landed["w2_1"], landed["ab_w_out"],
         landed["c_w_out"]) = _sb_bwd(
            p0, rtot, dycat, 4, pad, "sb_bwd",
            cargo=[dw1_0, rows4(dw2_0), rows4(dw2_1), rows4(dwabo), rows4(dwco)], exchange=SCATTER)
    dpq, dcq = _conv_bwd(p0, 0, cw[0], dqn, "q", pad, "conv_q_bwd")
    dpk, dck = _conv_bwd(p0, 4, cw[1], dkn, "k", pad, "conv_k_bwd")
    dpv, dcv = _conv_bwd(p0, 8, cw[2], dvn, "v", pad, "conv_v_bwd")
    dp0 = _assemble_bf16([dpq, dpk, dpv, dza, dbb, daa, dqb, dkb, dvb], "ab_in_dy")
    dwab = _mm(h0b, dp0, ta=True, out_dtype=BF16, name="ab_in_dw")
    if late is None:
        dh0 = _mm(dp0, w["ab_w_in"], tb=True, plus=dh0a, name="ab_in_dx")
    else:
        dab = jnp.transpose(_unpad_ab_cols(dwab).reshape(D, N_CHIP, AB_TRUE // N_CHIP), (1, 0, 2))
        dh0, landed["ab_w_in"] = _mm(dp0, w["ab_w_in"], tb=True, plus=dh0a, name="ab_in_dx", cargo=[dab],
                                     exchange=SCATTER)

    grads = {
        "ab_w_in": dwab, "conv_w": jnp.concatenate([dcq, dck, dcv], axis=1),
        "a_log": dal[:, :GDN_H], "dt_bias": ddt[:, :GDN_H],
        "ab_gnorm_g": dag, "ab_w_out": dwabo, "c_w_in": dwc, "c_lb_raw": dlb, "c_gnorm_g": dcg, "c_w_out": dwco,
        "ln_mix_g": jnp.concatenate([dmg0, dmg1], 0), "ln_mix_b": jnp.concatenate([dmb0, dmb1], 0),
        "w1_0": dw1_0, "w1_1": dw1_1, "w2_0": dw2_0, "w2_1": dw2_1,
        "ln_ffn_g": jnp.concatenate([dfg0, dfg1], 0), "ln_ffn_b": jnp.concatenate([dfb0, dfb1], 0),
        "landed": landed,
    }
    return loss, dh0, grads


MESH = pl.DeviceIdType.MESH
ANY = pl.BlockSpec(memory_space=pl.ANY)
N_CHIP = 4
N_DEV = 8
CHIP_REL = ((1, 0), (0, 1), (1, 1))
DEV_REL = tuple((dx, dy, dc) for dx in (0, 1) for dy in (0, 1) for dc in (0, 1))[1:]

def _pos():
    return lax.axis_index("x"), lax.axis_index("y"), lax.axis_index("c")


def _flip(a, d):
    return a + d - 2 * a * d


class _Exchange:
    def __init__(self, local, sends, recvs):
        self.local, self.sends, self.recvs = local, sends, recvs

    def start(self):
        for cp in self.local + self.sends:
            cp.start()

    def wait(self):
        for cp in self.recvs:
            cp.wait_recv()
        for cp in self.sends:
            cp.wait_send()
        for cp in self.local:
            cp.wait()


def _gather_sems(n):
    return [pltpu.SemaphoreType.DMA((3 * n,)), pltpu.SemaphoreType.DMA((3 * n,)), pltpu.SemaphoreType.DMA((n,))]


def _gather_copies(x_refs, o_refs, send_sems, recv_sems, local_sems):
    n = len(x_refs)
    x, y, c = _pos()
    local = [pltpu.make_async_copy(x_refs[a], o_refs[a].at[2 * x + y], local_sems.at[a]) for a in range(n)]

    def copy(a, k, sending):
        tx, ty = _flip(x, CHIP_REL[k][0]), _flip(y, CHIP_REL[k][1])
        return pltpu.make_async_remote_copy(
            src_ref=x_refs[a], dst_ref=o_refs[a].at[2 * x + y if sending else 2 * tx + ty],
            send_sem=send_sems.at[3 * a + k], recv_sem=recv_sems.at[3 * a + k], device_id=(tx, ty, c), device_id_type=MESH)

    pairs = [(a, k) for a in range(n) for k in range(3)]
    return _Exchange(local, [copy(a, k, True) for a, k in pairs], [copy(a, k, False) for a, k in pairs])


def _gather_shapes(bufs):
    return [jax.ShapeDtypeStruct((N_CHIP,) + b.shape, b.dtype) for b in bufs]


def _chip_allgather(bufs, name):
    n = len(bufs)

    def body(*refs):
        ex = _gather_copies(refs[:n], refs[n:2 * n], *refs[2 * n:])
        ex.start()
        ex.wait()

    return pl.pallas_call(
        body, name=name, in_specs=[ANY] * n, out_specs=[ANY] * n, out_shape=_gather_shapes(bufs),
        scratch_shapes=_gather_sems(n), compiler_params=pltpu.CompilerParams(has_side_effects=True),
    )(*bufs)


def _scatter_sems(n):
    nr = N_DEV - 1
    return [pltpu.SemaphoreType.DMA((nr * n,)), pltpu.SemaphoreType.DMA((nr * n,)), pltpu.SemaphoreType.DMA((n,))]


def _scatter_copies(g_refs, o_refs, send_sems, recv_sems, local_sems):
    n = len(g_refs)
    nr = N_DEV - 1
    x, y, c = _pos()
    me = 4 * x + 2 * y + c
    local = [pltpu.make_async_copy(g_refs[a].at[2 * x + y], o_refs[a].at[me], local_sems.at[a]) for a in range(n)]

    def copy(a, k, sending):
        dx, dy, dc = DEV_REL[k]
        tx, ty, tc = _flip(x, dx), _flip(y, dy), _flip(c, dc)
        return pltpu.make_async_remote_copy(
            src_ref=g_refs[a].at[2 * tx + ty], dst_ref=o_refs[a].at[me if sending else 4 * tx + 2 * ty + tc],
            send_sem=send_sems.at[nr * a + k], recv_sem=recv_sems.at[nr * a + k],
            device_id=(tx, ty, tc), device_id_type=MESH)

    pairs = [(a, k) for a in range(n) for k in range(nr)]
    return _Exchange(local, [copy(a, k, True) for a, k in pairs], [copy(a, k, False) for a, k in pairs])


def _scatter_shapes(gs):
    return [jax.ShapeDtypeStruct((N_DEV,) + g.shape[1:], g.dtype) for g in gs]


GATHER = (_gather_copies, _gather_shapes, _gather_sems)
SCATTER = (_scatter_copies, _scatter_shapes, _scatter_sems)


def _sum_slots(rs, name):
    n, rh, w = rs[0].shape
    tr = _pick(rh, (256, 128, 64, 16))

    def body(*refs):
        o_ref = refs[-1]
        for layer, r_ref in enumerate(refs[:-1]):
            acc = r_ref[0].astype(F32)
            for s in range(1, n):
                acc = acc + r_ref[s].astype(F32)
            o_ref[layer] = acc

    return pl.pallas_call(
        body, name=name, grid=(rh // tr,), in_specs=[pl.BlockSpec((n, tr, w), lambda i: (0, i, 0))] * len(rs),
        out_specs=pl.BlockSpec((len(rs), tr, w), lambda i: (0, i, 0)),
        out_shape=jax.ShapeDtypeStruct((len(rs), rh, w), F32), compiler_params=_cparams(),
    )(*rs)


def _small_allreduce(buf, name):
    r, w = buf.shape

    def body(b_ref, o_ref, land_ref, send_sems, recv_sems):
        x, y, c = _pos()
        me = 4 * x + 2 * y + c
        land_ref[me] = b_ref[...]

        def target(k):
            dx, dy, dc = DEV_REL[k]
            return _flip(x, dx), _flip(y, dy), _flip(c, dc)

        sends = []
        for k in range(N_DEV - 1):
            tx, ty, tc = target(k)
            cp = pltpu.make_async_remote_copy(
                src_ref=b_ref, dst_ref=land_ref.at[me], send_sem=send_sems.at[k], recv_sem=recv_sems.at[k],
                device_id=(tx, ty, tc), device_id_type=MESH)
            cp.start()
            sends.append(cp)
        for k in range(N_DEV - 1):
            tx, ty, tc = target(k)
            pltpu.make_async_remote_copy(
                src_ref=b_ref, dst_ref=land_ref.at[4 * tx + 2 * ty + tc], send_sem=send_sems.at[k],
                recv_sem=recv_sems.at[k], device_id=(tx, ty, tc), device_id_type=MESH).wait_recv()
        for cp in sends:
            cp.wait_send()
        acc = land_ref[0]
        for s in range(1, N_DEV):
            acc = acc + land_ref[s]
        o_ref[...] = acc

    vm = pl.BlockSpec(memory_space=pltpu.VMEM)
    return pl.pallas_call(
        body, name=name, in_specs=[vm], out_specs=vm, out_shape=jax.ShapeDtypeStruct((r, w), F32),
        scratch_shapes=[pltpu.VMEM((N_DEV, r, w), F32), pltpu.SemaphoreType.DMA((N_DEV - 1,)),
                        pltpu.SemaphoreType.DMA((N_DEV - 1,))],
        compiler_params=pltpu.CompilerParams(has_side_effects=True),
    )(buf)


def _adamw(w, g, m, v, name):
    r, c = w.shape
    tr = _pick(r, (256, 128, 64, 8)) if r * c > (1 << 18) else r

    def body(w_ref, g_ref, m_ref, v_ref, d_ref, m2_ref, v2_ref):
        gg = g_ref[...]
        m2 = ADAM_B1 * m_ref[...] + (1.0 - ADAM_B1) * gg
        v2 = ADAM_B2 * v_ref[...] + (1.0 - ADAM_B2) * (gg * gg)
        m_hat = m2 / (1.0 - ADAM_B1 ** ADAM_STEP)
        v_hat = v2 / (1.0 - ADAM_B2 ** ADAM_STEP)
        d_ref[...] = -ADAM_LR * (m_hat / (jnp.sqrt(v_hat) + ADAM_EPS) + ADAM_WD * w_ref[...])
        m2_ref[...] = m2
        v2_ref[...] = v2

    blk = pl.BlockSpec((tr, c), lambda i: (i, 0))
    sds = jax.ShapeDtypeStruct((r, c), F32)
    return pl.pallas_call(body, name=name, grid=(r // tr,), in_specs=[blk] * 4, out_specs=[blk] * 3,
                          out_shape=[sds] * 3, compiler_params=_cparams())(w, g, m, v)


BIG = ("ab_w_in", "ab_w_out", "c_w_in", "c_w_out", "mlp_w1", "mlp_w2")
SMALL = ("ln_mix_g", "ln_mix_b", "ln_ffn_g", "ln_ffn_b", "c_lb_raw", "ab_a_log", "ab_dt_bias", "ab_gnorm_g", "c_gnorm_g")
SMALL_ROWS = 16
CONV_ROWS = 8
CONV_W = 3 * GDN_H * HD


def _conv_to_rows(cw):
    return jnp.pad(cw, ((0, 0), (0, 2 * D - CONV_W))).reshape(CONV_ROWS, D)


def _rows_to_conv(rows):
    return rows.reshape(CONV_K, 2 * D)[:, :CONV_W]


def _pack_small(d):
    rows = [jnp.pad(d[n], ((0, 0), (0, D - d[n].shape[1]))) for n in SMALL]
    buf = jnp.concatenate(rows, axis=0)
    return jnp.pad(buf, ((0, SMALL_ROWS - buf.shape[0]), (0, 0)))


def _unpack_small(buf, like):
    out, r = {}, 0
    for n in SMALL:
        nr, nc = like[n].shape
        out[n] = buf[r:r + nr, :nc]
        r += nr
    return out


def kernel(x, meta_tokens, ab_w_in, ab_conv_w, ab_a_log, ab_dt_bias, ab_gnorm_g, ab_w_out, c_w_in, c_lb_raw, c_gnorm_g, c_w_out, ln_mix_g, ln_mix_b, mlp_w1, mlp_w2, ln_ffn_g, ln_ffn_b, loss_target, m_meta_tokens, m_ab_w_in, m_ab_conv_w, m_ab_a_log, m_ab_dt_bias, m_ab_gnorm_g, m_ab_w_out, m_c_w_in, m_c_lb_raw, m_c_gnorm_g, m_c_w_out, m_ln_mix_g, m_ln_mix_b, m_mlp_w1, m_mlp_w2, m_ln_ffn_g, m_ln_ffn_b, v_meta_tokens, v_ab_w_in, v_ab_conv_w, v_ab_a_log, v_ab_dt_bias, v_ab_gnorm_g, v_ab_w_out, v_c_w_in, v_c_lb_raw, v_c_gnorm_g, v_c_w_out, v_ln_mix_g, v_ln_mix_b, v_mlp_w1, v_mlp_w2, v_ln_ffn_g, v_ln_ffn_b):
    names = ("meta_tokens", "ab_w_in", "ab_conv_w", "ab_a_log", "ab_dt_bias", "ab_gnorm_g", "ab_w_out", "c_w_in",
             "c_lb_raw", "c_gnorm_g", "c_w_out", "ln_mix_g", "ln_mix_b", "mlp_w1", "mlp_w2", "ln_ffn_g", "ln_ffn_b")
    wts = dict(zip(names, (meta_tokens, ab_w_in, ab_conv_w, ab_a_log, ab_dt_bias, ab_gnorm_g, ab_w_out, c_w_in, c_lb_raw,
                           c_gnorm_g, c_w_out, ln_mix_g, ln_mix_b, mlp_w1, mlp_w2, ln_ffn_g, ln_ffn_b)))
    mom_m = dict(zip(names, (m_meta_tokens, m_ab_w_in, m_ab_conv_w, m_ab_a_log, m_ab_dt_bias, m_ab_gnorm_g, m_ab_w_out,
                             m_c_w_in, m_c_lb_raw, m_c_gnorm_g, m_c_w_out, m_ln_mix_g, m_ln_mix_b, m_mlp_w1, m_mlp_w2,
                             m_ln_ffn_g, m_ln_ffn_b)))
    mom_v = dict(zip(names, (v_meta_tokens, v_ab_w_in, v_ab_conv_w, v_ab_a_log, v_ab_dt_bias, v_ab_gnorm_g, v_ab_w_out,
                             v_c_w_in, v_c_lb_raw, v_c_gnorm_g, v_c_w_out, v_ln_mix_g, v_ln_mix_b, v_mlp_w1, v_mlp_w2,
                             v_ln_ffn_g, v_ln_ffn_b)))
    seq = x.shape[1]
    pad = (-(N_META + seq)) % QB
    xi, yi, ci = _pos()
    chip = 2 * xi + yi

    gat_ab_in, = _chip_allgather([ab_w_in[0].astype(BF16)], "gather_weights")
    late = {"ab_w_out": ab_w_out[0].astype(BF16), "c_w_in": c_w_in.astype(BF16), "c_w_out": c_w_out[0].astype(BF16),
            "mlp_w1": mlp_w1.astype(BF16), "mlp_w2": mlp_w2.astype(BF16)}
    mcols, ccols = meta_tokens.shape[1], ab_conv_w.shape[2]
    place = jnp.concatenate([
        lax.dynamic_update_slice(jnp.zeros((N_META, D), F32), 0.5 * meta_tokens, (0, chip * mcols)),
        _conv_to_rows(lax.dynamic_update_slice(jnp.zeros((CONV_K, CONV_W), F32), 0.5 * ab_conv_w[0], (0, chip * ccols)))],
        axis=0)
    placed = _small_allreduce(place, "gather_meta")
    meta_full = placed[:N_META]

    w = {
        "ab_w_in": _pad_ab_cols(jnp.transpose(gat_ab_in, (1, 0, 2)).reshape(D, AB_TRUE)),
        "conv_w": _rows_to_conv(placed[N_META:]), "a_log": ab_a_log, "dt_bias": ab_dt_bias,
        "ab_gnorm_g": ab_gnorm_g, "c_lb_raw": c_lb_raw,
        "c_gnorm_g": c_gnorm_g, "ln_mix_g": ln_mix_g, "ln_mix_b": ln_mix_b, "ln_ffn_g": ln_ffn_g, "ln_ffn_b": ln_ffn_b,
    }

    h0 = jnp.concatenate([jnp.zeros((pad, D), F32), meta_full, x[0]], axis=0)
    tgt = jnp.concatenate([jnp.zeros((pad + N_META, D), F32), loss_target[0]], axis=0)
    loss8, dh0, g = _local_step(h0, tgt, w, pad, late)
    loss = lax.psum(loss8[0, 0], ("x", "y", "c"))
    grad_x = dh0[pad + N_META:][None]

    gsmall = {"ln_mix_g": g["ln_mix_g"], "ln_mix_b": g["ln_mix_b"], "ln_ffn_g": g["ln_ffn_g"], "ln_ffn_b": g["ln_ffn_b"],
              "c_lb_raw": g["c_lb_raw"], "ab_a_log": g["a_log"], "ab_dt_bias": g["dt_bias"], "ab_gnorm_g": g["ab_gnorm_g"],
              "c_gnorm_g": g["c_gnorm_g"]}
    sbuf = jnp.concatenate([_pack_small(gsmall), dh0[pad:pad + N_META], _conv_to_rows(g["conv_w"])], axis=0)
    ssum = _small_allreduce(sbuf, "allreduce_small")
    grads = _unpack_small(ssum[:SMALL_ROWS], wts)
    grads["meta_tokens"] = lax.dynamic_slice(ssum[SMALL_ROWS:SMALL_ROWS + N_META], (0, chip * mcols), (N_META, mcols))
    grads["ab_conv_w"] = lax.dynamic_slice(_rows_to_conv(ssum[SMALL_ROWS + N_META:]), (0, chip * ccols), (CONV_K, ccols))[None]

    landed = g["landed"]
    for n in ("ab_w_in", "ab_w_out", "c_w_in", "c_w_out"):
        grads[n] = _sum_slots([landed[n]], f"grad_sum_{n}")
    grads["mlp_w1"] = _sum_slots([landed["w1_0"], landed["w1_1"]], "grad_sum_mlp_w1")
    grads["mlp_w2"] = _sum_slots([landed["w2_0"], landed["w2_1"]], "grad_sum_mlp_w2")

    delta, new_m, new_v = {}, {}, {}
    for n in ("meta_tokens", "ab_conv_w") + BIG:
        shp = wts[n].shape
        to2 = lambda a: a.reshape(-1, shp[-1])
        d2, m2, v2 = _adamw(to2(wts[n]), to2(grads[n]), to2(mom_m[n]), to2(mom_v[n]), f"adamw_{n}")
        delta[n], new_m[n], new_v[n] = d2.reshape(shp), m2.reshape(shp), v2.reshape(shp)
    d2, m2, v2 = _adamw(_pack_small(wts), ssum[:SMALL_ROWS], _pack_small(mom_m), _pack_small(mom_v), "adamw_small")
    delta.update(_unpack_small(d2, wts))
    new_m.update(_unpack_small(m2, wts))
    new_v.update(_unpack_small(v2, wts))

    return (loss, grad_x, *[grads[n] for n in names], *[delta[n] for n in names], *[new_m[n] for n in names],
            *[new_v[n] for n in names])
```

```python
import functools

import numpy as np
import jax
import jax.numpy as jnp
from jax import lax
from jax.experimental import pallas as pl
from jax.experimental.pallas import tpu as pltpu

F32 = jnp.float32
BF16 = jnp.bfloat16

D = 1024
N_META = 16
DEPTH = 2
GDN_H = 4
SB_H = 8
SB_DH = 64
HG_H = 8
HD = 128
CH = 64
QB = 128
ALPHA = float((2 * DEPTH) ** 0.25)
LN_EPS = 1e-5
RMS_EPS = 1e-6
L2_EPS = 1e-6
NEG = -1e30

ADAM_LR = 0.001
ADAM_B1 = 0.9
ADAM_B2 = 0.999
ADAM_EPS = 1e-08
ADAM_WD = 0.01
ADAM_STEP = 10

AB_TRUE = 3592
V7X_VMEM_BYTES = 64 * 1024 * 1024
VMEM_LIMIT = V7X_VMEM_BYTES - 8 * 1024 * 1024

NN = ((1,), (0,))
NT = ((1,), (1,))
TN = ((0,), (0,))


def _cparams(**kw):
    return pltpu.CompilerParams(vmem_limit_bytes=VMEM_LIMIT, **kw)


def _dg(a, b, dims, mode):
    if mode == "h":
        return lax.dot_general(a, b, dims, precision=lax.Precision.HIGHEST, preferred_element_type=F32)
    if mode == "b":
        return lax.dot_general(a.astype(BF16), b.astype(BF16), dims, preferred_element_type=F32)
    ah, bh = a.astype(BF16), b.astype(BF16)
    al, bl = (a - ah.astype(F32)).astype(BF16), (b - bh.astype(F32)).astype(BF16)
    d = lambda x, y: lax.dot_general(x, y, dims, preferred_element_type=F32)
    return d(ah, bh) + (d(ah, bl) + d(al, bh))


def _make_dots(mode, batched=False):
    if batched:
        nn_d, nt_d, tn_d = (((2,), (1,)), ((0,), (0,))), (((2,), (2,)), ((0,), (0,))), (((1,), (1,)), ((0,), (0,)))
    else:
        nn_d, nt_d, tn_d = (NN, ((), ())), (NT, ((), ())), (TN, ((), ()))

    @jax.custom_vjp
    def nn(a, b):
        return _dg(a, b, nn_d, mode)

    @jax.custom_vjp
    def nt(a, b):
        return _dg(a, b, nt_d, mode)

    @jax.custom_vjp
    def tn(a, b):
        return _dg(a, b, tn_d, mode)

    nn.defvjp(lambda a, b: (nn(a, b), (a, b)), lambda r, g: (nt(g, r[1]), tn(r[0], g)))
    nt.defvjp(lambda a, b: (nt(a, b), (a, b)), lambda r, g: (nn(g, r[1]), tn(g, r[0])))
    tn.defvjp(lambda a, b: (tn(a, b), (a, b)), lambda r, g: (nt(r[1], g), nn(r[0], g)))
    return nn, nt, tn


hnn = _make_dots("h")[0]
bbnn, bbnt, bbtn = _make_dots("b", True)
mbnn, mbnt, mbtn = _make_dots("m", True)
hbnt = _make_dots("h", True)[1]


def _split3(x, axis):
    x1 = x.astype(BF16)
    r1 = x - x1.astype(F32)
    x2 = r1.astype(BF16)
    x3 = (r1 - x2.astype(F32)).astype(BF16)
    return jnp.concatenate([x1, x2, x3], axis=axis)


@jax.custom_vjp
def _mask_dot(e3, x):
    return lax.dot_general(e3[0], _split3(x, 0), (NN, ((), ())), preferred_element_type=F32)


def _mask_dot_bwd(e3, g):
    dx = lax.dot_general(e3[1], _split3(g, 0), (TN, ((), ())), preferred_element_type=F32)
    return (jnp.zeros_like(e3[0]), jnp.zeros_like(e3[1])), dx


_mask_dot.defvjp(lambda e3, x: (_mask_dot(e3, x), e3), _mask_dot_bwd)


def _heads(a, n):
    return jnp.concatenate([a[None, :, h * HD:(h + 1) * HD] for h in range(n)], axis=0)


def _sigmoid(x):
    return jax.nn.sigmoid(x)


def _silu(x):
    return x * jax.nn.sigmoid(x)


def _softplus(x):
    return jnp.maximum(x, 0.0) + jnp.log(1.0 + jnp.exp(-jnp.abs(x)))


def _iota(shape, dim):
    return lax.broadcasted_iota(jnp.int32, shape, dim)


def _pick(n, prefs):
    for p in prefs:
        if n % p == 0:
            return p
    return n


def _mm(a, b, *, ta=False, tb=False, out_dtype=F32, name, b_view=None, out_split=0, act=False, gate=None, plus=None,
        cargo=(), exchange=None):
    if ta:
        k_dim, m_dim = a.shape
    else:
        m_dim, k_dim = a.shape
    if b_view is None:
        w_rows, w_cols = b.shape
    else:
        kind, layer = b_view
        nj, _, blk_r, blk_c = b.shape
        w_rows, w_cols = (blk_r, nj * blk_c) if kind == "cols" else (nj * blk_r, blk_c)
    n_dim = w_rows if tb else w_cols
    assert (w_cols if tb else w_rows) == k_dim
    tm = _pick(m_dim, (1024, 1056, 704, 640, 512, 384, 256, 128))
    tn = _pick(n_dim, (1024, 1056, 704, 640, 512, 384, 256, 128))
    tk = _pick(k_dim, (1024, 1056, 704, 512, 384, 256, 128))
    nk = k_dim // tk
    a_spec = pl.BlockSpec((tk, tm), lambda i, j, k: (k, i)) if ta else pl.BlockSpec((tm, tk), lambda i, j, k: (i, k))
    wb = (tn, tk) if tb else (tk, tn)
    w_idx = (lambda i, j, k: (j, k)) if tb else (lambda i, j, k: (k, j))
    if b_view is None:
        b_spec = pl.BlockSpec(wb, w_idx)
    elif kind == "cols":
        per = blk_c // wb[1]
        b_spec = pl.BlockSpec((None, None) + wb,
                              lambda i, j, k: (w_idx(i, j, k)[1] // per, layer, w_idx(i, j, k)[0], w_idx(i, j, k)[1] % per))
    else:
        per = blk_r // wb[0]
        b_spec = pl.BlockSpec((None, None) + wb,
                              lambda i, j, k: (w_idx(i, j, k)[0] // per, layer, w_idx(i, j, k)[0] % per, w_idx(i, j, k)[1]))
    if out_split:
        per_o = (n_dim // out_split) // tn
        out_spec = pl.BlockSpec((None, tm, tn), lambda i, j, k: (j // per_o, i, j % per_o))
        out_sds = jax.ShapeDtypeStruct((out_split, m_dim, n_dim // out_split), out_dtype)
    else:
        out_spec = pl.BlockSpec((tm, tn), lambda i, j, k: (i, j))
        out_sds = jax.ShapeDtypeStruct((m_dim, n_dim), out_dtype)
    dims = (((0 if ta else 1,), (1 if tb else 0,)), ((), ()))
    assert gate is None or plus is None
    extra = [e for e in (gate, plus) if e is not None]
    n_out = 2 if act else 1

    def finish(acc, refs):
        if act:
            refs[0][...] = acc.astype(refs[0].dtype)
            r = jnp.maximum(acc, 0.0)
            refs[1][...] = (r * r).astype(refs[1].dtype)
        elif gate is not None:
            refs[1][...] = (acc * (2.0 * jnp.maximum(refs[0][...].astype(F32), 0.0))).astype(refs[1].dtype)
        elif plus is not None:
            refs[1][...] = (refs[0][...] + acc).astype(refs[1].dtype)
        else:
            refs[0][...] = acc.astype(refs[0].dtype)

    grid = (m_dim // tm, n_dim // tn, nk)
    nc = len(cargo)

    def body(a_ref, b_ref, *rest):
        acc_ref = rest[-1]
        ids = [pl.program_id(d) for d in range(3)]
        outs, end_cargo = _cargo_bounds(
            rest[len(extra):-1], nc, n_out, exchange, (ids[0] == 0) & (ids[1] == 0) & (ids[2] == 0),
            (ids[0] == grid[0] - 1) & (ids[1] == grid[1] - 1) & (ids[2] == grid[2] - 1))
        refs = tuple(rest[:len(extra)]) + tuple(outs)
        part = lax.dot_general(a_ref[...], b_ref[...], dims, preferred_element_type=F32)
        if nk == 1:
            finish(part, refs)
        else:
            k = ids[2]

            @pl.when(k == 0)
            def _():
                acc_ref[...] = part

            @pl.when(k > 0)
            def _():
                acc_ref[...] += part

            @pl.when(k == nk - 1)
            def _():
                finish(acc_ref[...], refs)
        end_cargo()

    out = pl.pallas_call(
        body, name=name, grid=grid,
        in_specs=[a_spec, b_spec] + [pl.BlockSpec((tm, tn), lambda i, j, k: (i, j))] * len(extra) + [ANY] * nc,
        out_specs=[out_spec] * n_out + [ANY] * nc,
        out_shape=[out_sds] * n_out + (exchange[1](cargo) if nc else []),
        scratch_shapes=(exchange[2](nc) if nc else []) + [pltpu.VMEM((tm, tn) if nk > 1 else (8, 128), F32)],
        compiler_params=_cparams(dimension_semantics=("arbitrary",) * 3 if nc else ("parallel", "parallel", "arbitrary")),
    )(a, b, *extra, *cargo)
    if nc:
        return out
    return out if act else out[0]


def _mm_groups_nt(parts, b, name):
    m_dim, k_dim = parts[0].shape
    ng, _, n_dim, _ = b.shape
    assert len(parts) == ng and b.shape[3] == k_dim
    tm = _pick(m_dim, (1056, 704, 512, 384, 256, 128))

    def body(*refs):
        a_refs, b_ref, o_ref, acc_ref = refs[:ng], refs[ng], refs[ng + 1], refs[ng + 2]
        k = pl.program_id(1)
        for g in range(ng):
            @pl.when(k == g)
            def _(g=g):
                part = lax.dot_general(a_refs[g][...], b_ref[...], (NT, ((), ())), preferred_element_type=F32)
                if g == 0:
                    acc_ref[...] = part
                elif g < ng - 1:
                    acc_ref[...] += part
                else:
                    o_ref[...] = acc_ref[...] + part

    return pl.pallas_call(
        body, name=name, grid=(m_dim // tm, ng),
        in_specs=[pl.BlockSpec((tm, k_dim), lambda i, k: (i, 0))] * ng
        + [pl.BlockSpec((None, None, n_dim, k_dim), lambda i, k: (k, 0, 0, 0))],
        out_specs=pl.BlockSpec((tm, n_dim), lambda i, k: (i, 0)),
        out_shape=jax.ShapeDtypeStruct((m_dim, n_dim), F32),
        scratch_shapes=[pltpu.VMEM((tm, n_dim), F32)],
        compiler_params=_cparams(dimension_semantics=("parallel", "arbitrary")),
    )(*parts, b)


def _row_tile(t_pad, width):
    for tr in (528, 352, 176, 128, 64):
        if t_pad % tr == 0 and tr * width * 4 <= (3 << 19) and tr % 16 == 0:
            return tr
    return 64 if t_pad % 64 == 0 else t_pad


def _ln_res_fn(h, m, g, b):
    x = ALPHA * h + m
    mu = jnp.mean(x, axis=-1, keepdims=True)
    xc = x - mu
    var = jnp.mean(xc * xc, axis=-1, keepdims=True)
    return xc * lax.rsqrt(var + LN_EPS) * g + b


def _ln_res_fwd(h, m, g, b, name):
    t_pad = h.shape[0]
    tr = _row_tile(t_pad, D)

    def body(h_ref, m_ref, g_ref, b_ref, y_ref, yb_ref):
        y = _ln_res_fn(h_ref[...], m_ref[...], g_ref[...], b_ref[...])
        y_ref[...] = y
        yb_ref[...] = y.astype(BF16)

    row = pl.BlockSpec((tr, D), lambda i: (i, 0))
    par = pl.BlockSpec((1, D), lambda i: (0, 0))
    return pl.pallas_call(
        body, name=name, grid=(t_pad // tr,), in_specs=[row, row, par, par], out_specs=[row, row],
        out_shape=[jax.ShapeDtypeStruct((t_pad, D), F32), jax.ShapeDtypeStruct((t_pad, D), BF16)],
        compiler_params=_cparams(),
    )(h, m, g, b)


def _ln_res_bwd(h, m, g, b, dys, name):
    t_pad = h.shape[0]
    tr = _row_tile(t_pad, D)
    nd = len(dys)

    def body(h_ref, m_ref, g_ref, b_ref, *rest):
        d_refs, (dh_ref, dm_ref, dg_ref, db_ref) = rest[:nd], rest[nd:]
        _, vjp = jax.vjp(_ln_res_fn, h_ref[...], m_ref[...], g_ref[...], b_ref[...])
        dy = d_refs[0][...]
        for d_ref in d_refs[1:]:
            dy = dy + d_ref[...]
        dh, dm, dg, db = vjp(dy)
        dh_ref[...] = dh
        dm_ref[...] = dm.astype(BF16)

        @pl.when(pl.program_id(0) == 0)
        def _():
            dg_ref[...] = jnp.zeros_like(dg_ref)
            db_ref[...] = jnp.zeros_like(db_ref)

        dg_ref[...] += dg
        db_ref[...] += db

    row = pl.BlockSpec((tr, D), lambda i: (i, 0))
    par = pl.BlockSpec((1, D), lambda i: (0, 0))
    return pl.pallas_call(
        body, name=name, grid=(t_pad // tr,), in_specs=[row, row, par, par] + [row] * nd,
        out_specs=[row, row, par, par],
        out_shape=[jax.ShapeDtypeStruct((t_pad, D), F32), jax.ShapeDtypeStruct((t_pad, D), BF16),
                   jax.ShapeDtypeStruct((1, D), F32), jax.ShapeDtypeStruct((1, D), F32)],
        compiler_params=_cparams(),
    )(h, m, g, b, *dys)


def _grms_fn(o, z, g):
    y = o * lax.rsqrt(jnp.mean(o * o, axis=-1, keepdims=True) + RMS_EPS) * g
    return y * _silu(z)


def _grms_fwd(o, z_arr, z_blk0, g, name):
    t_pad, w = o.shape
    tr = _row_tile(t_pad, w)
    assert (z_blk0 * HD) % w == 0

    def body(o_ref, z_ref, g_ref, y_ref):
        for h in range(w // HD):
            c = slice(h * HD, (h + 1) * HD)
            y_ref[:, c] = _grms_fn(o_ref[:, c], z_ref[:, c], g_ref[...]).astype(BF16)

    return pl.pallas_call(
        body, name=name, grid=(t_pad // tr,),
        in_specs=[pl.BlockSpec((tr, w), lambda i: (i, 0)), pl.BlockSpec((tr, w), lambda i: (i, z_blk0 * HD // w)),
                  pl.BlockSpec((1, HD), lambda i: (0, 0))],
        out_specs=pl.BlockSpec((tr, w), lambda i: (i, 0)),
        out_shape=jax.ShapeDtypeStruct((t_pad, w), BF16), compiler_params=_cparams(),
    )(o, z_arr, g)


def _grms_bwd(o, z_arr, z_blk0, g, dy_arr, dy_blk0, name):
    t_pad, w = o.shape
    tr = _row_tile(t_pad, w)
    assert (z_blk0 * HD) % w == 0 and (dy_blk0 * HD) % w == 0

    def body(o_ref, z_ref, g_ref, dy_ref, do_ref, dz_ref, dg_ref):
        @pl.when(pl.program_id(0) == 0)
        def _():
            dg_ref[...] = jnp.zeros_like(dg_ref)

        for h in range(w // HD):
            c = slice(h * HD, (h + 1) * HD)
            _, vjp = jax.vjp(_grms_fn, o_ref[:, c], z_ref[:, c], g_ref[...])
            do, dz, dg = vjp(dy_ref[:, c])
            do_ref[:, c] = do
            dz_ref[:, c] = dz.astype(BF16)
            dg_ref[...] += dg

    blk = pl.BlockSpec((tr, w), lambda i: (i, 0))
    return pl.pallas_call(
        body, name=name, grid=(t_pad // tr,),
        in_specs=[blk, pl.BlockSpec((tr, w), lambda i: (i, z_blk0 * HD // w)), pl.BlockSpec((1, HD), lambda i: (0, 0)),
                  pl.BlockSpec((tr, w), lambda i: (i, dy_blk0 * HD // w))],
        out_specs=[blk, blk, pl.BlockSpec((1, HD), lambda i: (0, 0))],
        out_shape=[jax.ShapeDtypeStruct((t_pad, w), F32), jax.ShapeDtypeStruct((t_pad, w), BF16),
                   jax.ShapeDtypeStruct((1, HD), F32)],
        compiler_params=_cparams(),
    )(o, z_arr, g, dy_arr)


def _loss_fwd(y, tgt, first_row, name):
    t_pad = y.shape[0]
    tr = _row_tile(t_pad, D)

    def body(y_ref, t_ref, l_ref, dy_ref):
        rows = pl.program_id(0) * tr + _iota((tr, 1), 0)
        err = jnp.where(rows >= first_row, y_ref[...] - t_ref[...], 0.0)
        dy_ref[...] = err * (1.0 / D)

        @pl.when(pl.program_id(0) == 0)
        def _():
            l_ref[...] = jnp.zeros_like(l_ref)

        part = jnp.sum(jnp.sum(err * err, axis=1, keepdims=True), axis=0, keepdims=True)
        l_ref[...] += jnp.broadcast_to(part * (0.5 / D), l_ref.shape)

    row = pl.BlockSpec((tr, D), lambda i: (i, 0))
    return pl.pallas_call(
        body, name=name, grid=(t_pad // tr,), in_specs=[row, row],
        out_specs=[pl.BlockSpec((8, 128), lambda i: (0, 0)), row],
        out_shape=[jax.ShapeDtypeStruct((8, 128), F32), jax.ShapeDtypeStruct((t_pad, D), F32)],
        compiler_params=_cparams(),
    )(y, tgt)


def _assemble_bf16(parts, name):
    t_pad = parts[0].shape[0]
    widths = [p.shape[1] for p in parts]
    total = sum(widths)
    tr = _row_tile(t_pad, total)

    def body(*refs):
        o_ref = refs[-1]
        off = 0
        for ref, w in zip(refs[:-1], widths):
            o_ref[:, off:off + w] = ref[...].astype(BF16)
            off += w

    return pl.pallas_call(
        body, name=name, grid=(t_pad // tr,), in_specs=[pl.BlockSpec((tr, w), lambda i: (i, 0)) for w in widths],
        out_specs=pl.BlockSpec((tr, total), lambda i: (i, 0)),
        out_shape=jax.ShapeDtypeStruct((t_pad, total), BF16), compiler_params=_cparams(),
    )(*parts)


CONV_K = 4
HALO = 8
RT = 128


def _conv_fwd(p, blk0, w, mode, pad, name):
    t_pad = p.shape[0]
    nt = t_pad // RT
    scale = HD ** -0.5 if mode == "q" else 1.0

    def body(x_ref, w_ref, y_ref, xs_ref):
        xs_ref[0:HALO, :] = jnp.zeros((HALO, HD), F32)
        rows = _iota((t_pad, 1), 0)
        xs_ref[HALO:HALO + t_pad, :] = jnp.where(rows >= pad, x_ref[...], 0.0)
        wv = w_ref[...]

        def tile(i, carry):
            r0 = pl.multiple_of(i * RT, RT)
            ext = xs_ref[pl.ds(r0, RT + HALO), :]
            acc = ext[HALO:, :] * wv[3:4, :]
            for s in (1, 2, 3):
                acc = acc + pltpu.roll(ext, s, 0)[HALO:, :] * wv[3 - s:4 - s, :]
            y = _silu(acc)
            if mode != "v":
                y = y * lax.rsqrt(jnp.sum(y * y, axis=-1, keepdims=True) + L2_EPS) * scale
            y_ref[pl.ds(r0, RT), :] = y
            return carry

        lax.fori_loop(0, nt, tile, 0)

    return pl.pallas_call(
        body, name=name, grid=(GDN_H,),
        in_specs=[pl.BlockSpec((t_pad, HD), lambda h: (0, blk0 + h)), pl.BlockSpec((CONV_K, HD), lambda h: (0, h))],
        out_specs=pl.BlockSpec((t_pad, HD), lambda h: (0, h)),
        out_shape=jax.ShapeDtypeStruct((t_pad, GDN_H * HD), F32),
        scratch_shapes=[pltpu.VMEM((t_pad + HALO, HD), F32)],
        compiler_params=_cparams(),
    )(p, w)


def _conv_bwd(p, blk0, w, dn, mode, pad, name):
    t_pad = p.shape[0]
    nt = t_pad // RT
    scale = HD ** -0.5 if mode == "q" else 1.0

    def body(x_ref, w_ref, dn_ref, dx_ref, dw_ref, xs_ref, ds_ref):
        xs_ref[0:HALO, :] = jnp.zeros((HALO, HD), F32)
        xs_ref[HALO + t_pad:HALO + t_pad + 2 * HALO, :] = jnp.zeros((2 * HALO, HD), F32)
        ds_ref[t_pad:t_pad + HALO, :] = jnp.zeros((HALO, HD), F32)
        rows = _iota((t_pad, 1), 0)
        xs_ref[HALO:HALO + t_pad, :] = jnp.where(rows >= pad, x_ref[...], 0.0)
        ds_ref[0:t_pad, :] = dn_ref[...]
        wv = w_ref[...]

        def tile(i, dw):
            r0 = pl.multiple_of(i * RT, RT)
            ext = xs_ref[pl.ds(r0, RT + 2 * HALO), :]
            dn_e = ds_ref[pl.ds(r0, RT + HALO), :]
            xsh = [ext[HALO:, :]] + [pltpu.roll(ext, s, 0)[HALO:, :] for s in (1, 2, 3)]
            pre = xsh[0] * wv[3:4, :]
            for s in (1, 2, 3):
                pre = pre + xsh[s] * wv[3 - s:4 - s, :]
            sg = _sigmoid(pre)
            y = pre * sg
            if mode != "v":
                ss = jnp.sum(y * y, axis=-1, keepdims=True) + L2_EPS
                r = lax.rsqrt(ss)
                dy = scale * (dn_e * r - y * (r * r * r) * jnp.sum(dn_e * y, axis=-1, keepdims=True))
            else:
                dy = dn_e
            dpre = dy * (sg * (1.0 + pre * (1.0 - sg)))
            dx = dpre[:RT, :] * wv[3:4, :]
            for s in (1, 2, 3):
                dx = dx + pltpu.roll(dpre, RT + HALO - s, 0)[:RT, :] * wv[3 - s:4 - s, :]
            trow = r0 + _iota((RT, 1), 0)
            dx_ref[pl.ds(r0, RT), :] = jnp.where(trow >= pad, dx, 0.0)
            new = []
            for s in (0, 1, 2, 3):
                new.append(dw[s] + jnp.sum(dpre[:RT, :] * xsh[s][:RT, :], axis=0, keepdims=True))
            return tuple(new)

        z = jnp.zeros((1, HD), F32)
        dw = lax.fori_loop(0, nt, tile, (z, z, z, z))
        for s in (0, 1, 2, 3):
            dw_ref[3 - s:4 - s, :] = dw[s]

    return pl.pallas_call(
        body, name=name, grid=(GDN_H,),
        in_specs=[pl.BlockSpec((t_pad, HD), lambda h: (0, blk0 + h)), pl.BlockSpec((CONV_K, HD), lambda h: (0, h)),
                  pl.BlockSpec((t_pad, HD), lambda h: (0, h))],
        out_specs=[pl.BlockSpec((t_pad, HD), lambda h: (0, h)), pl.BlockSpec((CONV_K, HD), lambda h: (0, h))],
        out_shape=[jax.ShapeDtypeStruct((t_pad, GDN_H * HD), F32), jax.ShapeDtypeStruct((CONV_K, GDN_H * HD), F32)],
        scratch_shapes=[pltpu.VMEM((t_pad + 3 * HALO, HD), F32), pltpu.VMEM((t_pad + HALO, HD), F32)],
        compiler_params=_cparams(),
    )(p, w, dn)


@jax.custom_vjp
def _unit_lower_inv(m, bd, eye):
    md = m * bd
    low = m - md
    p2 = mbnn(md, md)
    p4 = mbnn(p2, p2)
    dinv = mbnn(mbnn(eye - md, eye + p2), eye + p4)
    n = mbnn(dinv, low)
    n2 = mbnn(n, n)
    n4 = mbnn(n2, n2)
    return mbnn(mbnn(mbnn(eye - n, eye + n2), eye + n4), dinv)


def _unit_lower_inv_bwd(res, g):
    t, bd, eye = res
    return -mbtn(t, mbnt(g, t)), jnp.zeros_like(bd), jnp.zeros_like(eye)


def _unit_lower_inv_fwd(m, bd, eye):
    t = _unit_lower_inv(m, bd, eye)
    return t, (t, bd, eye)


_unit_lower_inv.defvjp(_unit_lower_inv_fwd, _unit_lower_inv_bwd)


def _gdn_chunks(chunks, alog, dtb, s):
    nh = chunks[0][0].shape[0]
    ri = _iota((1, CH, CH), 1)
    ci = _iota((1, CH, CH), 2)
    causal = ri >= ci
    strict = ri > ci
    eye = (ri == ci).astype(F32)
    bd = ((ri >> 3) == (ci >> 3)).astype(F32)
    ltri = (_iota((CH, CH), 0) >= _iota((CH, CH), 1)).astype(F32)
    sel = (_iota((nh, 1, HD), 2) == _iota((nh, 1, HD), 0)).astype(F32)
    last = _iota((1, CH, 1), 1) == CH - 1

    beta, gc, gc_rows = [], [], []
    for _, _, _, bb, aa, valid in chunks:
        beta_all = jnp.where(valid, _sigmoid(bb), 0.0)
        g_all = jnp.where(valid, -jnp.exp(alog) * _softplus(aa + dtb), 0.0)
        gc_all = hnn(ltri, g_all)
        beta.append(jnp.sum(beta_all[None] * sel, axis=2, keepdims=True))
        gc.append(jnp.sum(gc_all[None] * sel, axis=2, keepdims=True))
        gc_rows.append(hbnt(jnp.broadcast_to(sel, (nh, CH, HD)), jnp.broadcast_to(gc_all[None], (nh, CH, HD))))
    cat = lambda xs: jnp.concatenate(xs, axis=0)
    q, k, v = (cat([c[j] for c in chunks]) for j in range(3))
    beta, gc, gc_rows = cat(beta), cat(gc), cat(gc_rows)
    gc_last = jnp.sum(jnp.where(last, gc, 0.0), axis=1, keepdims=True)
    decay = jnp.exp(jnp.where(causal, gc - gc_rows, NEG))
    egc = jnp.exp(gc)

    kb = k * beta
    m = jnp.where(strict, bbnt(kb, k) * decay, 0.0)
    t_inv = _unit_lower_inv(m, bd, eye)
    u = bbnn(t_inv, v * beta)
    w = bbnn(t_inv, kb * egc)
    a_intra = bbnt(q, k) * decay
    q_dec = q * egc
    k_dec = k * jnp.exp(gc_last - gc)
    g_tot = jnp.exp(gc_last)

    outs = []
    for n in range(len(chunks)):
        part = lambda a: a[n * nh:(n + 1) * nh]
        v_new = part(u) - bbnn(part(w), s)
        outs.append(bbnn(part(q_dec), s) + bbnn(part(a_intra), v_new))
        s = s * part(g_tot) + bbtn(part(k_dec), v_new)
    return outs, s


PAIR = 2 * CH


def _gdn_specs(npair, rev):
    cc = (lambda c: npair - 1 - c) if rev else (lambda c: c)
    wide = pl.BlockSpec((PAIR, GDN_H * HD), lambda c: (cc(c), 0))
    fix = lambda off: pl.BlockSpec((PAIR, HD), lambda c: (cc(c), off))
    par = pl.BlockSpec((1, HD), lambda c: (0, 0))
    state = pl.BlockSpec((1, GDN_H, HD, HD), lambda c: (cc(c), 0, 0, 0))
    return wide, fix, par, state


def _store_heads(ref, a, rows=slice(None)):
    for h in range(a.shape[0]):
        ref[rows, h * HD:(h + 1) * HD] = a[h]


def _chunk_rows(half):
    return slice(half * CH, (half + 1) * CH)


def _chunk_valid(pair, half, pad):
    return ((2 * pair + half) * CH + _iota((CH, 1), 0)) >= pad


def _gdn_fwd(qn, kn, vn, p, alog, dtb, pad, name, cargo=(), exchange=None):
    t_pad = qn.shape[0]
    npair = t_pad // PAIR
    wide, fix, par, state = _gdn_specs(npair, False)
    n = len(cargo)

    def body(q_ref, k_ref, v_ref, bb_ref, aa_ref, al_ref, dt_ref, *rest):
        c = pl.program_id(0)
        s_ref = rest[-1]
        (o_ref, ss_ref), end_cargo = _cargo_bounds(rest[:-1], n, 2, exchange, c == 0, c == npair - 1)

        @pl.when(c == 0)
        def _():
            s_ref[...] = jnp.zeros_like(s_ref)

        s = s_ref[...]
        ss_ref[0] = s
        rows = [_chunk_rows(half) for half in (0, 1)]
        chunks = [(_heads(q_ref[r, :], GDN_H), _heads(k_ref[r, :], GDN_H), _heads(v_ref[r, :], GDN_H),
                   bb_ref[r, :], aa_ref[r, :], _chunk_valid(c, half, pad)) for half, r in enumerate(rows)]
        outs, s = _gdn_chunks(chunks, al_ref[...], dt_ref[...], s)
        for r, o in zip(rows, outs):
            _store_heads(o_ref, o, r)
        s_ref[...] = s
        end_cargo()

    return pl.pallas_call(
        body, name=name, grid=(npair,),
        in_specs=[wide, wide, wide, fix(16), fix(17), par, par] + [ANY] * n,
        out_specs=[wide, state] + [ANY] * n,
        out_shape=[jax.ShapeDtypeStruct((t_pad, GDN_H * HD), F32), jax.ShapeDtypeStruct((npair, GDN_H, HD, HD), F32)]
        + (exchange[1](cargo) if n else []),
        scratch_shapes=(exchange[2](n) if n else []) + [pltpu.VMEM((GDN_H, HD, HD), F32)],
        compiler_params=_cparams(),
    )(qn, kn, vn, p, p, alog, dtb, *cargo)


def _gdn_bwd(qn, kn, vn, p, alog, dtb, ssave, do, pad, name, cargo=(), exchange=None):
    t_pad = qn.shape[0]
    npair = t_pad // PAIR
    wide, fix, par, state = _gdn_specs(npair, True)
    n = len(cargo)

    def body(q_ref, k_ref, v_ref, bb_ref, aa_ref, al_ref, dt_ref, ss_ref, do_ref, *rest):
        c = pl.program_id(0)
        ds_ref = rest[-1]
        (dq_ref, dk_ref, dv_ref, dbb_ref, daa_ref, dal_ref, ddt_ref), end_cargo = _cargo_bounds(
            rest[:-1], n, 7, exchange, c == 0, c == npair - 1)

        @pl.when(c == 0)
        def _():
            ds_ref[...] = jnp.zeros_like(ds_ref)
            dal_ref[...] = jnp.zeros_like(dal_ref)
            ddt_ref[...] = jnp.zeros_like(ddt_ref)

        ra, rb = _chunk_rows(0), _chunk_rows(1)
        va, vb = _chunk_valid(npair - 1 - c, 0, pad), _chunk_valid(npair - 1 - c, 1, pad)

        def pair(qa, ka, va_, ba, aa, qb, kb, vb_, bb, ab, al, dt, s):
            (oa, ob), s = _gdn_chunks([(qa, ka, va_, ba, aa, va), (qb, kb, vb_, bb, ab, vb)], al, dt, s)
            return oa, ob, s

        ins = [f(ref[r, :]) for r in (ra, rb)
               for ref, f in ((q_ref, lambda a: _heads(a, GDN_H)), (k_ref, lambda a: _heads(a, GDN_H)),
                              (v_ref, lambda a: _heads(a, GDN_H)), (bb_ref, lambda a: a), (aa_ref, lambda a: a))]
        _, vjp = jax.vjp(pair, *ins, al_ref[...], dt_ref[...], ss_ref[0])
        g = vjp((_heads(do_ref[ra, :], GDN_H), _heads(do_ref[rb, :], GDN_H), ds_ref[...]))
        for r, (dq, dk, dv, dbb, daa) in ((ra, g[0:5]), (rb, g[5:10])):
            _store_heads(dq_ref, dq, r)
            _store_heads(dk_ref, dk, r)
            _store_heads(dv_ref, dv, r)
            dbb_ref[r, :] = dbb
            daa_ref[r, :] = daa
        dal_ref[...] += g[10]
        ddt_ref[...] += g[11]
        ds_ref[...] = g[12]
        end_cargo()

    sds = jax.ShapeDtypeStruct
    return pl.pallas_call(
        body, name=name, grid=(npair,),
        in_specs=[wide, wide, wide, fix(16), fix(17), par, par, state, wide] + [ANY] * n,
        out_specs=[wide, wide, wide, fix(0), fix(0), par, par] + [ANY] * n,
        out_shape=[sds((t_pad, GDN_H * HD), F32)] * 3 + [sds((t_pad, HD), F32)] * 2 + [sds((1, HD), F32)] * 2
        + (exchange[1](cargo) if n else []),
        scratch_shapes=(exchange[2](n) if n else []) + [pltpu.VMEM((GDN_H, HD, HD), F32)],
        compiler_params=_cparams(),
    )(qn, kn, vn, p, p, alog, dtb, ssave, do, *cargo)


SB_Q0, SB_K0, SB_V0 = 18, 22, 26
SB_SCALE = SB_DH ** -0.5
SB_NB = 8


def _sb_terms(z, allowed):
    e = jnp.exp(-jnp.abs(z))
    den = 1.0 + e
    raw = -jnp.maximum(z, 0.0) - jnp.log(den)
    l1m = raw if allowed is None else jnp.where(allowed, raw, 0.0)
    ls = z + raw
    return l1m, ls, jnp.exp(ls)


def _sb_passes(i, step, carry):
    total = i + 1
    sized = lambda done: [functools.partial(step, done, masked=True, nb=nb) for nb in range(1, SB_NB + 1)]

    def several(c):
        n_mid = (total - SB_NB - 1) // SB_NB
        c = step(0, c, masked=True, nb=SB_NB)
        c = lax.fori_loop(0, n_mid, lambda t, cc: step(SB_NB * (1 + t), cc, masked=False, nb=SB_NB), c)
        done = SB_NB * (1 + n_mid)
        return lax.switch(total - done - 1, sized(done), c)

    return lax.cond(total <= SB_NB, lambda c: lax.switch(total - 1, sized(0), c), several, carry)


def _sb_stack(a, i):
    first = _iota((1, HD), 1) < SB_DH
    a2 = jnp.concatenate([jnp.where(first, a, 0.0), jnp.where(first, 0.0, a)], axis=0).astype(BF16)
    rq = i * QB + _iota((QB, 1), 0)
    return a2, jnp.concatenate([rq, rq], axis=0), first


def _hi_lo(a):
    hi = a.astype(BF16)
    lo = (a - hi.astype(F32)).astype(BF16)
    return jnp.concatenate([hi, lo], axis=1)


def _cargo_bounds(refs, n, n_out, exchange, first, last):
    outs = refs[n:n + n_out]
    if not n:
        return outs, lambda: None
    ex = exchange[0](refs[:n], refs[n + n_out:2 * n + n_out], *refs[2 * n + n_out:])

    @pl.when(first)
    def _():
        ex.start()

    def finish():
        @pl.when(last)
        def _():
            ex.wait()

    return outs, finish


def _sb_fwd(p, pad, name, cargo=(), exchange=None):
    t_pad = p.shape[0]
    nq = t_pad // QB
    n = len(cargo)

    def body(q_ref, k_ref, v_ref, *rest):
        i = pl.program_id(1)
        pr = pl.program_id(0)
        (o_ref, r_ref), end_cargo = _cargo_bounds(rest, n, 2, exchange, (pr == 0) & (i == 0),
                                                  (pr == SB_H // 2 - 1) & (i == nq - 1))
        q2, rowq, first = _sb_stack(q_ref[...] * SB_SCALE, i)
        tri = (_iota((QB, QB), 0) > _iota((QB, QB), 1)).astype(BF16)
        upper2 = jnp.concatenate([jnp.concatenate([tri, tri], axis=0), jnp.ones((2 * QB, QB), BF16)], axis=1)

        def chain(kb, masked):
            start = pl.multiple_of(kb * QB, QB)
            kblk = k_ref[pl.ds(start, QB), :].astype(BF16)
            vblk = v_ref[pl.ds(start, QB), :].astype(BF16)
            z = lax.dot_general(q2, kblk, (NT, ((), ())), preferred_element_type=F32)
            colk = kb * QB + _iota((1, QB), 1)
            al = ((colk < rowq) & (colk >= pad)) if masked else None
            l1m, ls, _ = _sb_terms(z, al)
            sums = lax.dot_general(_hi_lo(l1m), upper2, (NN, ((), ())), preferred_element_type=F32)
            return al, ls, sums[:, :QB], sums[:, QB:], vblk

        def step(done, carry, masked, nb):
            o_acc, run = carry
            ws, vs = [], []
            for n in range(nb):
                al, ls, suf, rs, vblk = chain(i - done - n, masked)
                wgt = jnp.exp(ls + suf + run)
                ws.append((wgt if al is None else jnp.where(al, wgt, 0.0)).astype(BF16))
                vs.append(vblk)
                run = run + rs
            o_acc = o_acc + lax.dot_general(jnp.concatenate(ws, axis=1), jnp.concatenate(vs, axis=0),
                                            (NN, ((), ())), preferred_element_type=F32)
            return o_acc, run

        o_acc, run = _sb_passes(i, step, (jnp.zeros((2 * QB, HD), F32), jnp.zeros((2 * QB, QB), F32)))
        o_ref[...] = jnp.where(first, o_acc[:QB], o_acc[QB:]).astype(BF16)
        r_ref[...] = jnp.where(first, run[:QB], run[QB:])
        end_cargo()

    full = lambda off: pl.BlockSpec((t_pad, HD), lambda pr, i: (0, off + pr))
    blk = pl.BlockSpec((QB, HD), lambda pr, i: (i, pr))
    return pl.pallas_call(
        body, name=name, grid=(SB_H // 2, nq),
        in_specs=[pl.BlockSpec((QB, HD), lambda pr, i: (i, SB_Q0 + pr)), full(SB_K0), full(SB_V0)] + [ANY] * n,
        out_specs=[blk, blk] + [ANY] * n,
        out_shape=[jax.ShapeDtypeStruct((t_pad, SB_H * SB_DH), BF16), jax.ShapeDtypeStruct((t_pad, SB_H * SB_DH), F32)]
        + (exchange[1](cargo) if n else []),
        scratch_shapes=exchange[2](n) if n else [],
        compiler_params=_cparams(),
    )(p, p, p, *cargo)


def _sb_bwd(p, rtot, dy, dy_blk0, pad, name, cargo=(), exchange=None):
    t_pad = p.shape[0]
    nq = t_pad // QB
    n = len(cargo)

    def body(q_ref, k_ref, v_ref, r_ref, do_ref, *rest):
        i = pl.program_id(1)
        pr = pl.program_id(0)
        dkt_ref, dvt_ref = rest[-2:]
        (dq_ref, dk_ref, dv_ref), end_cargo = _cargo_bounds(rest[:-2], n, 3, exchange, (pr == 0) & (i == 0),
                                                            (pr == SB_H // 2 - 1) & (i == nq - 1))

        @pl.when(i == 0)
        def _():
            dkt_ref[...] = jnp.zeros_like(dkt_ref)
            dvt_ref[...] = jnp.zeros_like(dvt_ref)

        q2, rowq, first = _sb_stack(q_ref[...] * SB_SCALE, i)
        do2, _, _ = _sb_stack(do_ref[...], i)
        q2t = jnp.transpose(q2.astype(F32)).astype(BF16)
        do2t = jnp.transpose(do2.astype(F32)).astype(BF16)
        rt = r_ref[...]
        lane = _iota((1, HD), 1)
        rcol = jnp.concatenate([jnp.sum(jnp.where(lane == 0, rt, 0.0), axis=1, keepdims=True),
                                jnp.sum(jnp.where(lane == SB_DH, rt, 0.0), axis=1, keepdims=True)], axis=0)
        rj = _iota((QB, QB), 0)
        cs = _iota((QB, QB), 1)
        tri_u = (rj > cs).astype(BF16)
        tri_l = (rj < cs).astype(BF16)
        ones2 = jnp.ones((2 * QB, QB), BF16)
        upper2 = jnp.concatenate([jnp.concatenate([tri_u, tri_u], axis=0), ones2], axis=1)
        lower2 = jnp.concatenate([jnp.concatenate([tri_l, tri_l], axis=0), ones2], axis=1)
        rcol = jnp.broadcast_to(rcol, (2 * QB, QB))

        def chain(kb, masked):
            start = pl.multiple_of(kb * QB, QB)
            kblk = k_ref[pl.ds(start, QB), :].astype(BF16)
            vblk = v_ref[pl.ds(start, QB), :].astype(BF16)
            z = lax.dot_general(q2, kblk, (NT, ((), ())), preferred_element_type=F32)
            colk = kb * QB + _iota((1, QB), 1)
            al = ((colk < rowq) & (colk >= pad)) if masked else None
            l1m, ls, sg = _sb_terms(z, al)
            dwgt = lax.dot_general(do2, vblk, (NT, ((), ())), preferred_element_type=F32)
            sums = lax.dot_general(_hi_lo(l1m), upper2, (NN, ((), ())), preferred_element_type=F32)
            return kb, kblk, al, ls, sums[:, :QB], sums[:, QB:], dwgt, sg

        def finish(c, seen, gseen):
            kb, kblk, al, ls, suf, rs, dwgt, sg = c
            wgt = jnp.exp(ls + suf + (rcol - seen - rs))
            if al is not None:
                wgt = jnp.where(al, wgt, 0.0)
            dl = dwgt * wgt
            sums = lax.dot_general(_hi_lo(dl), lower2, (NN, ((), ())), preferred_element_type=F32)
            gpre = gseen + sums[:, :QB]
            dz = dl - sg * (dl + gpre)
            if al is not None:
                dz = jnp.where(al, dz, 0.0)
            dz = dz.astype(BF16)
            dkt_ref[kb] += lax.dot_general(q2t, dz, (NN, ((), ())), preferred_element_type=F32)
            dvt_ref[kb] += lax.dot_general(do2t, wgt.astype(BF16), (NN, ((), ())), preferred_element_type=F32)
            return dz, seen + rs, gseen + sums[:, QB:]

        def step(done, carry, masked, nb):
            dq_acc, seen, gseen = carry
            cs_ = [chain(done + n, masked) for n in range(nb)]
            dzs = []
            for c in cs_:
                dz, seen, gseen = finish(c, seen, gseen)
                dzs.append(dz)
            dq_acc = dq_acc + lax.dot_general(jnp.concatenate(dzs, axis=1), jnp.concatenate([c[1] for c in cs_], axis=0),
                                              (NN, ((), ())), preferred_element_type=F32)
            return dq_acc, seen, gseen

        zc = jnp.zeros((2 * QB, QB), F32)
        dq_acc, _, _ = _sb_passes(i, step, (jnp.zeros((2 * QB, HD), F32), zc, zc))
        dq_ref[...] = jnp.where(first, dq_acc[:QB], dq_acc[QB:]) * SB_SCALE

        @pl.when(i == nq - 1)
        def _():
            for kb in range(nq):
                dk_ref[kb * QB:(kb + 1) * QB, :] = jnp.transpose(dkt_ref[kb])
                dv_ref[kb * QB:(kb + 1) * QB, :] = jnp.transpose(dvt_ref[kb])

        end_cargo()

    full_in = lambda off: pl.BlockSpec((t_pad, HD), lambda pr, i: (0, off + pr))
    full_out = pl.BlockSpec((t_pad, HD), lambda pr, i: (0, pr))
    blk = pl.BlockSpec((QB, HD), lambda pr, i: (i, pr))
    sds = jax.ShapeDtypeStruct((t_pad, SB_H * SB_DH), F32)
    return pl.pallas_call(
        body, name=name, grid=(SB_H // 2, nq),
        in_specs=[pl.BlockSpec((QB, HD), lambda pr, i: (i, SB_Q0 + pr)), full_in(SB_K0), full_in(SB_V0), blk,
                  pl.BlockSpec((QB, HD), lambda pr, i: (i, dy_blk0 + pr))] + [ANY] * n,
        out_specs=[blk, full_out, full_out] + [ANY] * n,
        out_shape=[sds, sds, sds] + (exchange[1](cargo) if n else []),
        scratch_shapes=(exchange[2](n) if n else []) + [pltpu.VMEM((nq, HD, QB), F32)] * 2,
        compiler_params=_cparams(),
    )(p, p, p, rtot, dy, *cargo)


HG_LEVELS = 6


def _hg_prefix_matrix():
    t = np.arange(CH)[:, None]
    j = np.arange(CH)[None, :]
    groups = [(j <= t)]
    for lvl in range(1, HG_LEVELS + 1):
        half = CH >> lvl
        e = (t // (2 * half)) * (2 * half) + half - 1
        groups.append(j <= e)
    groups.append(np.ones((8, CH), bool))
    e = np.concatenate(groups, axis=0).astype(np.float32)
    return np.concatenate([e, e, e], axis=1), np.concatenate([e, e, e], axis=0)


HG_G = 4


def _hg_chunk(qr, fr, iv, r0, r1, st, valid, ecat):
    g = st.shape[0]
    mx = jnp.maximum(r0, r1)
    e0 = jnp.exp(r0 - mx)
    e1 = jnp.exp(r1 - mx)
    lb = e1 / (e0 + e1)
    fg = lb + (1.0 - lb) * _sigmoid(fr)
    logf = jnp.where(valid, jnp.log(fg), 0.0)
    kk = jnp.where(valid, 1.0 - fg, 0.0)
    q = jnp.where(valid, _silu(qr), 0.0)
    v = _heads(jnp.where(valid, iv, 0.0), g)

    pre = _mask_dot(ecat, logf)
    b = pre[0:CH]
    b_last = jnp.max(pre[(HG_LEVELS + 1) * CH:], axis=0, keepdims=True)
    row = _iota((CH, 1), 0)
    ri = _iota((1, CH, CH), 1)
    ci = _iota((1, CH, CH), 2)
    a = jnp.where(ri == ci, jnp.sum(_heads(q * kk, g), axis=2, keepdims=True), 0.0)
    for lvl in range(1, HG_LEVELS + 1):
        half = CH >> lvl
        m = pre[lvl * CH:(lvl + 1) * CH]
        low = (row & half) != 0
        dec = jnp.exp(jnp.where(low, b - m, m - b))
        qt = jnp.where(low, q * dec, 0.0)
        kt = jnp.where(low, 0.0, kk * dec)
        same = (ri >> (7 - lvl)) == (ci >> (7 - lvl))
        a = a + jnp.where(same, bbnt(_heads(qt, g), _heads(kt, g)), 0.0)
    o = bbnt(_heads(q * jnp.exp(b), g), st) + bbnn(a, v)
    kd = kk * jnp.exp(b_last - b)
    st_new = st * _heads(jnp.exp(b_last), g) + bbtn(v, _heads(kd, g))
    return o, st_new


def _hg_specs(npair, rev):
    cc = (lambda c: npair - 1 - c) if rev else (lambda c: c)
    ng = HG_H // HG_G
    blk = lambda off: pl.BlockSpec((PAIR, HG_G * HD), lambda h, c: (cc(c), off * ng + h))
    lbs = pl.BlockSpec((2, HG_G * HD), lambda h, c: (0, h))
    state = pl.BlockSpec((1, HG_G, HD, HD), lambda h, c: (cc(c), h, 0, 0))
    return ng, blk, lbs, state


def _hg_fwd(p, lbraw, ecat, pad, name):
    t_pad = p.shape[0]
    npair = t_pad // PAIR
    ng, blk, lbs, state = _hg_specs(npair, False)

    def body(q_ref, f_ref, i_ref, lb_ref, e_ref, et_ref, o_ref, ss_ref, s_ref):
        c = pl.program_id(1)

        @pl.when(c == 0)
        def _():
            s_ref[...] = jnp.zeros_like(s_ref)

        st = s_ref[...]
        ss_ref[0] = st
        for half in (0, 1):
            r = _chunk_rows(half)
            o, st = _hg_chunk(q_ref[r, :], f_ref[r, :], i_ref[r, :], lb_ref[0:1, :], lb_ref[1:2, :], st,
                              _chunk_valid(c, half, pad), (e_ref[...], et_ref[...]))
            _store_heads(o_ref, o, r)
        s_ref[...] = st

    return pl.pallas_call(
        body, name=name, grid=(ng, npair),
        in_specs=[blk(0), blk(1), blk(2), lbs] + [pl.BlockSpec(e.shape, lambda h, c: (0, 0)) for e in ecat],
        out_specs=[blk(0), state],
        out_shape=[jax.ShapeDtypeStruct((t_pad, HG_H * HD), F32), jax.ShapeDtypeStruct((npair, HG_H, HD, HD), F32)],
        scratch_shapes=[pltpu.VMEM((HG_G, HD, HD), F32)],
        compiler_params=_cparams(),
    )(p, p, p, lbraw, *ecat)


def _hg_bwd(p, lbraw, ecat, ssave, do, pad, name, cargo=(), exchange=None):
    t_pad = p.shape[0]
    npair = t_pad // PAIR
    ng, blk, lbs, state = _hg_specs(npair, True)
    n = len(cargo)

    def body(q_ref, f_ref, i_ref, lb_ref, e_ref, et_ref, ss_ref, do_ref, *rest):
        c = pl.program_id(1)
        hg = pl.program_id(0)
        ds_ref = rest[-1]
        (dq_ref, df_ref, di_ref, dlb_ref), end_cargo = _cargo_bounds(
            rest[:-1], n, 4, exchange, (hg == 0) & (c == 0), (hg == ng - 1) & (c == npair - 1))

        @pl.when(c == 0)
        def _():
            ds_ref[...] = jnp.zeros_like(ds_ref)
            dlb_ref[...] = jnp.zeros_like(dlb_ref)

        ra, rb = _chunk_rows(0), _chunk_rows(1)
        va, vb = _chunk_valid(npair - 1 - c, 0, pad), _chunk_valid(npair - 1 - c, 1, pad)
        ecv = (e_ref[...], et_ref[...])

        def pair(qa, fa, ia, qb, fb, ib, r0, r1, st):
            oa, st = _hg_chunk(qa, fa, ia, r0, r1, st, va, ecv)
            ob, st = _hg_chunk(qb, fb, ib, r0, r1, st, vb, ecv)
            return oa, ob, st

        ins = [ref[r, :] for r in (ra, rb) for ref in (q_ref, f_ref, i_ref)]
        _, vjp = jax.vjp(pair, *ins, lb_ref[0:1, :], lb_ref[1:2, :], ss_ref[0])
        g = vjp((_heads(do_ref[ra, :], HG_G), _heads(do_ref[rb, :], HG_G), ds_ref[...]))
        for r, (dq, df, di) in ((ra, g[0:3]), (rb, g[3:6])):
            dq_ref[r, :] = dq.astype(BF16)
            df_ref[r, :] = df.astype(BF16)
            di_ref[r, :] = di.astype(BF16)
        dlb_ref[0:1, :] += g[6]
        dlb_ref[1:2, :] += g[7]
        ds_ref[...] = g[8]
        end_cargo()

    sds = jax.ShapeDtypeStruct((t_pad, HG_H * HD), BF16)
    return pl.pallas_call(
        body, name=name, grid=(ng, npair),
        in_specs=[blk(0), blk(1), blk(2), lbs] + [pl.BlockSpec(e.shape, lambda h, c: (0, 0)) for e in ecat]
        + [state, blk(0)] + [ANY] * n,
        out_specs=[blk(0), blk(0), blk(0), lbs] + [ANY] * n,
        out_shape=[sds, sds, sds, jax.ShapeDtypeStruct((2, HG_H * HD), F32)] + (exchange[1](cargo) if n else []),
        scratch_shapes=(exchange[2](n) if n else []) + [pltpu.VMEM((HG_G, HD, HD), F32)],
        compiler_params=_cparams(),
    )(p, p, p, lbraw, *ecat, ssave, do, *cargo)


def _pad_ab_cols(w):
    z = jnp.zeros((w.shape[0], HD - GDN_H), w.dtype)
    return jnp.concatenate([w[:, :2048], w[:, 2048:2052], z, w[:, 2052:2056], z, w[:, 2056:]], axis=1)


def _unpad_ab_cols(w):
    return jnp.concatenate([w[:, :2048], w[:, 2048:2052], w[:, 2176:2180], w[:, 2304:]], axis=1)


def _lane_pad(v):
    return jnp.pad(v, ((0, 0), (0, HD - v.shape[1])))


def _mlp_fwd(hb, w1, w2, layer):
    a, r = _mm(hb, w1, b_view=("cols", layer), out_dtype=BF16, act=True, name=f"mlp_up_{layer}")
    m = _mm(r, w2, b_view=("rows", layer), name=f"mlp_down_{layer}")
    return a, r, m


def _mlp_bwd(hb, a, r, dmb, w1, w2, layer):
    da = _mm(dmb, w2, tb=True, b_view=("rows", layer), out_dtype=BF16, gate=a, name=f"mlp_down_dx_{layer}")
    dw2 = _mm(r, dmb, ta=True, out_dtype=BF16, name=f"mlp_down_dw_{layer}")
    dh = _mm(da, w1, tb=True, b_view=("cols", layer), name=f"mlp_up_dx_{layer}")
    dw1 = _mm(hb, da, ta=True, out_dtype=BF16, out_split=N_CHIP, name=f"mlp_up_dw_{layer}")
    return dh, dw1, dw2


def _local_step(h0, tgt, w, pad, late=None):
    row = lambda a, i: a[i:i + 1]
    ecat = tuple(jnp.asarray(e, dtype=BF16) for e in _hg_prefix_matrix())
    cw = [w["conv_w"][:, i * 512:(i + 1) * 512] for i in range(3)]
    alog, dtb = _lane_pad(w["a_log"]), _lane_pad(w["dt_bias"])

    h0b = h0.astype(BF16)
    p0 = _mm(h0b, w["ab_w_in"], name="ab_in")
    qn = _conv_fwd(p0, 0, cw[0], "q", pad, "conv_q")
    kn = _conv_fwd(p0, 4, cw[1], "k", pad, "conv_k")
    vn = _conv_fwd(p0, 8, cw[2], "v", pad, "conv_v")
    if late is None:
        oa_raw, ss0 = _gdn_fwd(qn, kn, vn, p0, alog, dtb, pad, "gdn_fwd")
        ob, rtot = _sb_fwd(p0, pad, "sb_fwd")
    else:
        oa_raw, ss0, g_cin, g_cout = _gdn_fwd(qn, kn, vn, p0, alog, dtb, pad, "gdn_fwd",
                                              cargo=[late["c_w_in"], late["c_w_out"]], exchange=GATHER)
        ob, rtot, g_about, g_w1, g_w2 = _sb_fwd(p0, pad, "sb_fwd", exchange=GATHER,
                                                cargo=[late["ab_w_out"], late["mlp_w1"], late["mlp_w2"]])
        w = dict(w, ab_w_out=g_about.reshape(D, D), c_w_in=g_cin, c_w_out=g_cout.reshape(D, D), mlp_w1=g_w1, mlp_w2=g_w2)
    oa = _grms_fwd(oa_raw, p0, 12, w["ab_gnorm_g"], "gdn_gate")
    ycat = jnp.concatenate([oa, ob], axis=1)
    mix0 = _mm(ycat, w["ab_w_out"], name="ab_out")
    h1, h1b = _ln_res_fwd(h0, mix0, row(w["ln_mix_g"], 0), row(w["ln_mix_b"], 0), "ln_mix_0")
    a0, r0, m0 = _mlp_fwd(h1b, w["mlp_w1"], w["mlp_w2"], 0)
    h2, h2b = _ln_res_fwd(h1, m0, row(w["ln_ffn_g"], 0), row(w["ln_ffn_b"], 0), "ln_ffn_0")
    p1 = _mm(h2b, w["c_w_in"], b_view=("cols", 0), name="c_in")
    oc_raw, ss1 = _hg_fwd(p1, w["c_lb_raw"], ecat, pad, "hg_fwd")
    yc = _grms_fwd(oc_raw, p1, 3 * HG_H, w["c_gnorm_g"], "hg_gate")
    mix1 = _mm(yc, w["c_w_out"], name="c_out")
    h3, h3b = _ln_res_fwd(h2, mix1, row(w["ln_mix_g"], 1), row(w["ln_mix_b"], 1), "ln_mix_1")
    a1, r1, m1 = _mlp_fwd(h3b, w["mlp_w1"], w["mlp_w2"], 1)
    h4, _ = _ln_res_fwd(h3, m1, row(w["ln_ffn_g"], 1), row(w["ln_ffn_b"], 1), "ln_ffn_1")
    loss, dh4 = _loss_fwd(h4, tgt, pad + N_META, "loss")

    dh3a, dm1b, dfg1, dfb1 = _ln_res_bwd(h3, m1, row(w["ln_ffn_g"], 1), row(w["ln_ffn_b"], 1), [dh4], "ln_ffn_bwd_1")
    dh3b, dw1_1, dw2_1 = _mlp_bwd(h3b, a1, r1, dm1b, w["mlp_w1"], w["mlp_w2"], 1)
    dh2a, dmix1b, dmg1, dmb1 = _ln_res_bwd(h2, mix1, row(w["ln_mix_g"], 1), row(w["ln_mix_b"], 1), [dh3a, dh3b], "ln_mix_bwd_1")
    dyc = _mm(dmix1b, w["c_w_out"], tb=True, name="c_out_dx")
    dwco = _mm(yc, dmix1b, ta=True, out_dtype=BF16, name="c_out_dw")
    doc, dzc, dcg = _grms_bwd(oc_raw, p1, 3 * HG_H, w["c_gnorm_g"], dyc, 0, "hg_gate_bwd")
    landed = {}
    rows4 = lambda a: a.reshape(N_CHIP, -1, D)
    if late is None:
        dq1, df1, di1, dlb = _hg_bwd(p1, w["c_lb_raw"], ecat, ss1, doc, pad, "hg_bwd")
    else:
        dq1, df1, di1, dlb, landed["w1_1"] = _hg_bwd(
            p1, w["c_lb_raw"], ecat, ss1, doc, pad, "hg_bwd", cargo=[dw1_1], exchange=SCATTER)
    dp1 = [dq1, df1, di1, dzc]
    dh2b = _mm_groups_nt(dp1, w["c_w_in"], "c_in_dx")
    dwc = jnp.stack([_mm(h2b, d, ta=True, out_dtype=BF16, name=f"c_in_dw_{i}") for i, d in enumerate(dp1)])
    dh1a, dm0b, dfg0, dfb0 = _ln_res_bwd(h1, m0, row(w["ln_ffn_g"], 0), row(w["ln_ffn_b"], 0), [dh2a, dh2b], "ln_ffn_bwd_0")
    dh1b, dw1_0, dw2_0 = _mlp_bwd(h1b, a0, r0, dm0b, w["mlp_w1"], w["mlp_w2"], 0)
    dh0a, dmix0b, dmg0, dmb0 = _ln_res_bwd(h0, mix0, row(w["ln_mix_g"], 0), row(w["ln_mix_b"], 0), [dh1a, dh1b], "ln_mix_bwd_0")
    dycat = _mm(dmix0b, w["ab_w_out"], tb=True, name="ab_out_dx")
    dwabo = _mm(ycat, dmix0b, ta=True, out_dtype=BF16, name="ab_out_dw")
    doa, dza, dag = _grms_bwd(oa_raw, p0, 12, w["ab_gnorm_g"], dycat, 0, "gdn_gate_bwd")
    if late is None:
        dqn, dkn, dvn, dbb, daa, dal, ddt = _gdn_bwd(qn, kn, vn, p0, alog, dtb, ss0, doa, pad, "gdn_bwd")
        dqb, dkb, dvb = _sb_bwd(p0, rtot, dycat, 4, pad, "sb_bwd")
    else:
        dqn, dkn, dvn, dbb, daa, dal, ddt, landed["c_w_in"] = _gdn_bwd(
            qn, kn, vn, p0, alog, dtb, ss0, doa, pad, "gdn_bwd", cargo=[dwc], exchange=SCATTER)
        (dqb, dkb, dvb, landed["w1_0"], landed["w2_0"], landed["w2_1"], landed["ab_w_out"],
         landed["c_w_out"]) = _sb_bwd(
            p0, rtot, dycat, 4, pad, "sb_bwd",
            cargo=[dw1_0, rows4(dw2_0), rows4(dw2_1), rows4(dwabo), rows4(dwco)], exchange=SCATTER)
    dpq, dcq = _conv_bwd(p0, 0, cw[0], dqn, "q", pad, "conv_q_bwd")
    dpk, dck = _conv_bwd(p0, 4, cw[1], dkn, "k", pad, "conv_k_bwd")
    dpv, dcv = _conv_bwd(p0, 8, cw[2], dvn, "v", pad, "conv_v_bwd")
    dp0 = _assemble_bf16([dpq, dpk, dpv, dza, dbb, daa, dqb, dkb, dvb], "ab_in_dy")
    dwab = _mm(h0b, dp0, ta=True, out_dtype=BF16, name="ab_in_dw")
    if late is None:
        dh0 = _mm(dp0, w["ab_w_in"], tb=True, plus=dh0a, name="ab_in_dx")
    else:
        dab = jnp.transpose(_unpad_ab_cols(dwab).reshape(D, N_CHIP, AB_TRUE // N_CHIP), (1, 0, 2))
        dh0, landed["ab_w_in"] = _mm(dp0, w["ab_w_in"], tb=True, plus=dh0a, name="ab_in_dx", cargo=[dab],
                                     exchange=SCATTER)

    grads = {
        "ab_w_in": dwab, "conv_w": jnp.concatenate([dcq, dck, dcv], axis=1),
        "a_log": dal[:, :GDN_H], "dt_bias": ddt[:, :GDN_H],
        "ab_gnorm_g": dag, "ab_w_out": dwabo, "c_w_in": dwc, "c_lb_raw": dlb, "c_gnorm_g": dcg, "c_w_out": dwco,
        "ln_mix_g": jnp.concatenate([dmg0, dmg1], 0), "ln_mix_b": jnp.concatenate([dmb0, dmb1], 0),
        "w1_0": dw1_0, "w1_1": dw1_1, "w2_0": dw2_0, "w2_1": dw2_1,
        "ln_ffn_g": jnp.concatenate([dfg0, dfg1], 0), "ln_ffn_b": jnp.concatenate([dfb0, dfb1], 0),
        "landed": landed,
    }
    return loss, dh0, grads


MESH = pl.DeviceIdType.MESH
ANY = pl.BlockSpec(memory_space=pl.ANY)
N_CHIP = 4
N_DEV = 8
CHIP_REL = ((1, 0), (0, 1), (1, 1))
DEV_REL = tuple((dx, dy, dc) for dx in (0, 1) for dy in (0, 1) for dc in (0, 1))[1:]

def _pos():
    return lax.axis_index("x"), lax.axis_index("y"), lax.axis_index("c")


def _flip(a, d):
    return a + d - 2 * a * d


class _Exchange:
    def __init__(self, local, sends, recvs):
        self.local, self.sends, self.recvs = local, sends, recvs

    def start(self):
        for cp in self.local + self.sends:
            cp.start()

    def wait(self):
        for cp in self.recvs:
            cp.wait_recv()
        for cp in self.sends:
            cp.wait_send()
        for cp in self.local:
            cp.wait()


def _gather_sems(n):
    return [pltpu.SemaphoreType.DMA((3 * n,)), pltpu.SemaphoreType.DMA((3 * n,)), pltpu.SemaphoreType.DMA((n,))]


def _gather_copies(x_refs, o_refs, send_sems, recv_sems, local_sems):
    n = len(x_refs)
    x, y, c = _pos()
    local = [pltpu.make_async_copy(x_refs[a], o_refs[a].at[2 * x + y], local_sems.at[a]) for a in range(n)]

    def copy(a, k, sending):
        tx, ty = _flip(x, CHIP_REL[k][0]), _flip(y, CHIP_REL[k][1])
        return pltpu.make_async_remote_copy(
            src_ref=x_refs[a], dst_ref=o_refs[a].at[2 * x + y if sending else 2 * tx + ty],
            send_sem=send_sems.at[3 * a + k], recv_sem=recv_sems.at[3 * a + k], device_id=(tx, ty, c), device_id_type=MESH)

    pairs = [(a, k) for a in range(n) for k in range(3)]
    return _Exchange(local, [copy(a, k, True) for a, k in pairs], [copy(a, k, False) for a, k in pairs])


def _gather_shapes(bufs):
    return [jax.ShapeDtypeStruct((N_CHIP,) + b.shape, b.dtype) for b in bufs]


def _chip_allgather(bufs, name):
    n = len(bufs)

    def body(*refs):
        ex = _gather_copies(refs[:n], refs[n:2 * n], *refs[2 * n:])
        ex.start()
        ex.wait()

    return pl.pallas_call(
        body, name=name, in_specs=[ANY] * n, out_specs=[ANY] * n, out_shape=_gather_shapes(bufs),
        scratch_shapes=_gather_sems(n), compiler_params=pltpu.CompilerParams(has_side_effects=True),
    )(*bufs)


def _scatter_sems(n):
    nr = N_DEV - 1
    return [pltpu.SemaphoreType.DMA((nr * n,)), pltpu.SemaphoreType.DMA((nr * n,)), pltpu.SemaphoreType.DMA((n,))]


def _scatter_copies(g_refs, o_refs, send_sems, recv_sems, local_sems):
    n = len(g_refs)
    nr = N_DEV - 1
    x, y, c = _pos()
    me = 4 * x + 2 * y + c
    local = [pltpu.make_async_copy(g_refs[a].at[2 * x + y], o_refs[a].at[me], local_sems.at[a]) for a in range(n)]

    def copy(a, k, sending):
        dx, dy, dc = DEV_REL[k]
        tx, ty, tc = _flip(x, dx), _flip(y, dy), _flip(c, dc)
        return pltpu.make_async_remote_copy(
            src_ref=g_refs[a].at[2 * tx + ty], dst_ref=o_refs[a].at[me if sending else 4 * tx + 2 * ty + tc],
            send_sem=send_sems.at[nr * a + k], recv_sem=recv_sems.at[nr * a + k],
            device_id=(tx, ty, tc), device_id_type=MESH)

    pairs = [(a, k) for a in range(n) for k in range(nr)]
    return _Exchange(local, [copy(a, k, True) for a, k in pairs], [copy(a, k, False) for a, k in pairs])


def _scatter_shapes(gs):
    return [jax.ShapeDtypeStruct((N_DEV,) + g.shape[1:], g.dtype) for g in gs]


GATHER = (_gather_copies, _gather_shapes, _gather_sems)
SCATTER = (_scatter_copies, _scatter_shapes, _scatter_sems)


def _sum_slots(rs, name):
    n, rh, w = rs[0].shape
    tr = _pick(rh, (256, 128, 64, 16))

    def body(*refs):
        o_ref = refs[-1]
        for layer, r_ref in enumerate(refs[:-1]):
            acc = r_ref[0].astype(F32)
            for s in range(1, n):
                acc = acc + r_ref[s].astype(F32)
            o_ref[layer] = acc

    return pl.pallas_call(
        body, name=name, grid=(rh // tr,), in_specs=[pl.BlockSpec((n, tr, w), lambda i: (0, i, 0))] * len(rs),
        out_specs=pl.BlockSpec((len(rs), tr, w), lambda i: (0, i, 0)),
        out_shape=jax.ShapeDtypeStruct((len(rs), rh, w), F32), compiler_params=_cparams(),
    )(*rs)


def _small_allreduce(buf, name):
    r, w = buf.shape

    def body(b_ref, o_ref, land_ref, send_sems, recv_sems):
        x, y, c = _pos()
        me = 4 * x + 2 * y + c
        land_ref[me] = b_ref[...]

        def target(k):
            dx, dy, dc = DEV_REL[k]
            return _flip(x, dx), _flip(y, dy), _flip(c, dc)

        sends = []
        for k in range(N_DEV - 1):
            tx, ty, tc = target(k)
            cp = pltpu.make_async_remote_copy(
                src_ref=b_ref, dst_ref=land_ref.at[me], send_sem=send_sems.at[k], recv_sem=recv_sems.at[k],
                device_id=(tx, ty, tc), device_id_type=MESH)
            cp.start()
            sends.append(cp)
        for k in range(N_DEV - 1):
            tx, ty, tc = target(k)
            pltpu.make_async_remote_copy(
                src_ref=b_ref, dst_ref=land_ref.at[4 * tx + 2 * ty + tc], send_sem=send_sems.at[k],
                recv_sem=recv_sems.at[k], device_id=(tx, ty, tc), device_id_type=MESH).wait_recv()
        for cp in sends:
            cp.wait_send()
        acc = land_ref[0]
        for s in range(1, N_DEV):
            acc = acc + land_ref[s]
        o_ref[...] = acc

    vm = pl.BlockSpec(memory_space=pltpu.VMEM)
    return pl.pallas_call(
        body, name=name, in_specs=[vm], out_specs=vm, out_shape=jax.ShapeDtypeStruct((r, w), F32),
        scratch_shapes=[pltpu.VMEM((N_DEV, r, w), F32), pltpu.SemaphoreType.DMA((N_DEV - 1,)),
                        pltpu.SemaphoreType.DMA((N_DEV - 1,))],
        compiler_params=pltpu.CompilerParams(has_side_effects=True),
    )(buf)


def _adamw(w, g, m, v, name):
    r, c = w.shape
    tr = _pick(r, (256, 128, 64, 8)) if r * c > (1 << 18) else r

    def body(w_ref, g_ref, m_ref, v_ref, d_ref, m2_ref, v2_ref):
        gg = g_ref[...]
        m2 = ADAM_B1 * m_ref[...] + (1.0 - ADAM_B1) * gg
        v2 = ADAM_B2 * v_ref[...] + (1.0 - ADAM_B2) * (gg * gg)
        m_hat = m2 / (1.0 - ADAM_B1 ** ADAM_STEP)
        v_hat = v2 / (1.0 - ADAM_B2 ** ADAM_STEP)
        d_ref[...] = -ADAM_LR * (m_hat / (jnp.sqrt(v_hat) + ADAM_EPS) + ADAM_WD * w_ref[...])
        m2_ref[...] = m2
        v2_ref[...] = v2

    blk = pl.BlockSpec((tr, c), lambda i: (i, 0))
    sds = jax.ShapeDtypeStruct((r, c), F32)
    return pl.pallas_call(body, name=name, grid=(r // tr,), in_specs=[blk] * 4, out_specs=[blk] * 3,
                          out_shape=[sds] * 3, compiler_params=_cparams())(w, g, m, v)


BIG = ("ab_w_in", "ab_w_out", "c_w_in", "c_w_out", "mlp_w1", "mlp_w2")
SMALL = ("ln_mix_g", "ln_mix_b", "ln_ffn_g", "ln_ffn_b", "c_lb_raw", "ab_a_log", "ab_dt_bias", "ab_gnorm_g", "c_gnorm_g")
SMALL_ROWS = 16
CONV_ROWS = 8
CONV_W = 3 * GDN_H * HD


def _conv_to_rows(cw):
    return jnp.pad(cw, ((0, 0), (0, 2 * D - CONV_W))).reshape(CONV_ROWS, D)


def _rows_to_conv(rows):
    return rows.reshape(CONV_K, 2 * D)[:, :CONV_W]


def _pack_small(d):
    rows = [jnp.pad(d[n], ((0, 0), (0, D - d[n].shape[1]))) for n in SMALL]
    buf = jnp.concatenate(rows, axis=0)
    return jnp.pad(buf, ((0, SMALL_ROWS - buf.shape[0]), (0, 0)))


def _unpack_small(buf, like):
    out, r = {}, 0
    for n in SMALL:
        nr, nc = like[n].shape
        out[n] = buf[r:r + nr, :nc]
        r += nr
    return out


def kernel(x, meta_tokens, ab_w_in, ab_conv_w, ab_a_log, ab_dt_bias, ab_gnorm_g, ab_w_out, c_w_in, c_lb_raw, c_gnorm_g, c_w_out, ln_mix_g, ln_mix_b, mlp_w1, mlp_w2, ln_ffn_g, ln_ffn_b, loss_target, m_meta_tokens, m_ab_w_in, m_ab_conv_w, m_ab_a_log, m_ab_dt_bias, m_ab_gnorm_g, m_ab_w_out, m_c_w_in, m_c_lb_raw, m_c_gnorm_g, m_c_w_out, m_ln_mix_g, m_ln_mix_b, m_mlp_w1, m_mlp_w2, m_ln_ffn_g, m_ln_ffn_b, v_meta_tokens, v_ab_w_in, v_ab_conv_w, v_ab_a_log, v_ab_dt_bias, v_ab_gnorm_g, v_ab_w_out, v_c_w_in, v_c_lb_raw, v_c_gnorm_g, v_c_w_out, v_ln_mix_g, v_ln_mix_b, v_mlp_w1, v_mlp_w2, v_ln_ffn_g, v_ln_ffn_b):
    names = ("meta_tokens", "ab_w_in", "ab_conv_w", "ab_a_log", "ab_dt_bias", "ab_gnorm_g", "ab_w_out", "c_w_in",
             "c_lb_raw", "c_gnorm_g", "c_w_out", "ln_mix_g", "ln_mix_b", "mlp_w1", "mlp_w2", "ln_ffn_g", "ln_ffn_b")
    wts = dict(zip(names, (meta_tokens, ab_w_in, ab_conv_w, ab_a_log, ab_dt_bias, ab_gnorm_g, ab_w_out, c_w_in, c_lb_raw,
                           c_gnorm_g, c_w_out, ln_mix_g, ln_mix_b, mlp_w1, mlp_w2, ln_ffn_g, ln_ffn_b)))
    mom_m = dict(zip(names, (m_meta_tokens, m_ab_w_in, m_ab_conv_w, m_ab_a_log, m_ab_dt_bias, m_ab_gnorm_g, m_ab_w_out,
                             m_c_w_in, m_c_lb_raw, m_c_gnorm_g, m_c_w_out, m_ln_mix_g, m_ln_mix_b, m_mlp_w1, m_mlp_w2,
                             m_ln_ffn_g, m_ln_ffn_b)))
    mom_v = dict(zip(names, (v_meta_tokens, v_ab_w_in, v_ab_conv_w, v_ab_a_log, v_ab_dt_bias, v_ab_gnorm_g, v_ab_w_out,
                             v_c_w_in, v_c_lb_raw, v_c_gnorm_g, v_c_w_out, v_ln_mix_g, v_ln_mix_b, v_mlp_w1, v_mlp_w2,
                             v_ln_ffn_g, v_ln_ffn_b)))
    seq = x.shape[1]
    pad = (-(N_META + seq)) % QB
    xi, yi, ci = _pos()
    chip = 2 * xi + yi

    gat_ab_in, = _chip_allgather([ab_w_in[0].astype(BF16)], "gather_weights")
    late = {"ab_w_out": ab_w_out[0].astype(BF16), "c_w_in": c_w_in.astype(BF16), "c_w_out": c_w_out[0].astype(BF16),
            "mlp_w1": mlp_w1.astype(BF16), "mlp_w2": mlp_w2.astype(BF16)}
    mcols, ccols = meta_tokens.shape[1], ab_conv_w.shape[2]
    place = jnp.concatenate([
        lax.dynamic_update_slice(jnp.zeros((N_META, D), F32), 0.5 * meta_tokens, (0, chip * mcols)),
        _conv_to_rows(lax.dynamic_update_slice(jnp.zeros((CONV_K, CONV_W), F32), 0.5 * ab_conv_w[0], (0, chip * ccols)))],
        axis=0)
    placed = _small_allreduce(place, "gather_meta")
    meta_full = placed[:N_META]

    w = {
        "ab_w_in": _pad_ab_cols(jnp.transpose(gat_ab_in, (1, 0, 2)).reshape(D, AB_TRUE)),
        "conv_w": _rows_to_conv(placed[N_META:]), "a_log": ab_a_log, "dt_bias": ab_dt_bias,
        "ab_gnorm_g": ab_gnorm_g, "c_lb_raw": c_lb_raw,
        "c_gnorm_g": c_gnorm_g, "ln_mix_g": ln_mix_g, "ln_mix_b": ln_mix_b, "ln_ffn_g": ln_ffn_g, "ln_ffn_b": ln_ffn_b,
    }

    h0 = jnp.concatenate([jnp.zeros((pad, D), F32), meta_full, x[0]], axis=0)
    tgt = jnp.concatenate([jnp.zeros((pad + N_META, D), F32), loss_target[0]], axis=0)
    loss8, dh0, g = _local_step(h0, tgt, w, pad, late)
    grad_x = dh0[pad + N_META:][None]

    gsmall = {"ln_mix_g": g["ln_mix_g"], "ln_mix_b": g["ln_mix_b"], "ln_ffn_g": g["ln_ffn_g"], "ln_ffn_b": g["ln_ffn_b"],
              "c_lb_raw": g["c_lb_raw"], "ab_a_log": g["a_log"], "ab_dt_bias": g["dt_bias"], "ab_gnorm_g": g["ab_gnorm_g"],
              "c_gnorm_g": g["c_gnorm_g"]}
    packed = _pack_small(gsmall).at[SMALL_ROWS - 1, :loss8.shape[1]].set(loss8[0])
    sbuf = jnp.concatenate([packed, dh0[pad:pad + N_META], _conv_to_rows(g["conv_w"])], axis=0)
    ssum = _small_allreduce(sbuf, "allreduce_small")
    loss = ssum[SMALL_ROWS - 1, 0]
    grads = _unpack_small(ssum[:SMALL_ROWS], wts)
    grads["meta_tokens"] = lax.dynamic_slice(ssum[SMALL_ROWS:SMALL_ROWS + N_META], (0, chip * mcols), (N_META, mcols))
    grads["ab_conv_w"] = lax.dynamic_slice(_rows_to_conv(ssum[SMALL_ROWS + N_META:]), (0, chip * ccols), (CONV_K, ccols))[None]

    landed = g["landed"]
    for n in ("ab_w_in", "ab_w_out", "c_w_in", "c_w_out"):
        grads[n] = _sum_slots([landed[n]], f"grad_sum_{n}")
    grads["mlp_w1"] = _sum_slots([landed["w1_0"], landed["w1_1"]], "grad_sum_mlp_w1")
    grads["mlp_w2"] = _sum_slots([landed["w2_0"], landed["w2_1"]], "grad_sum_mlp_w2")

    delta, new_m, new_v = {}, {}, {}
    for n in ("meta_tokens", "ab_conv_w") + BIG:
        shp = wts[n].shape
        to2 = lambda a: a.reshape(-1, shp[-1])
        d2, m2, v2 = _adamw(to2(wts[n]), to2(grads[n]), to2(mom_m[n]), to2(mom_v[n]), f"adamw_{n}")
        delta[n], new_m[n], new_v[n] = d2.reshape(shp), m2.reshape(shp), v2.reshape(shp)
    d2, m2, v2 = _adamw(_pack_small(wts), ssum[:SMALL_ROWS], _pack_small(mom_m), _pack_small(mom_v), "adamw_small")
    delta.update(_unpack_small(d2, wts))
    new_m.update(_unpack_small(m2, wts))
    new_v.update(_unpack_small(v2, wts))

    return (loss, grad_x, *[grads[n] for n in names], *[delta[n] for n in names], *[new_m[n] for n in names],
            *[new_v[n] for n in names])
```

```python
import functools

import numpy as np
import jax
import jax.numpy as jnp
from jax import lax
from jax.experimental import pallas as pl
from jax.experimental.pallas import tpu as pltpu

F32 = jnp.float32
BF16 = jnp.bfloat16

D = 1024
N_META = 16
DEPTH = 2
GDN_H = 4
SB_H = 8
SB_DH = 64
HG_H = 8
HD = 128
CH = 64
QB = 128
ALPHA = float((2 * DEPTH) ** 0.25)
LN_EPS = 1e-5
RMS_EPS = 1e-6
L2_EPS = 1e-6
NEG = -1e30

ADAM_LR = 0.001
ADAM_B1 = 0.9
ADAM_B2 = 0.999
ADAM_EPS = 1e-08
ADAM_WD = 0.01
ADAM_STEP = 10

AB_TRUE = 3592
V7X_VMEM_BYTES = 64 * 1024 * 1024
VMEM_LIMIT = V7X_VMEM_BYTES - 8 * 1024 * 1024

NN = ((1,), (0,))
NT = ((1,), (1,))
TN = ((0,), (0,))


def _cparams(**kw):
    return pltpu.CompilerParams(vmem_limit_bytes=VMEM_LIMIT, **kw)


def _dg(a, b, dims, mode):
    if mode == "h":
        return lax.dot_general(a, b, dims, precision=lax.Precision.HIGHEST, preferred_element_type=F32)
    if mode == "b":
        return lax.dot_general(a.astype(BF16), b.astype(BF16), dims, preferred_element_type=F32)
    ah, bh = a.astype(BF16), b.astype(BF16)
    al, bl = (a - ah.astype(F32)).astype(BF16), (b - bh.astype(F32)).astype(BF16)
    d = lambda x, y: lax.dot_general(x, y, dims, preferred_element_type=F32)
    return d(ah, bh) + (d(ah, bl) + d(al, bh))


def _make_dots(mode, batched=False):
    if batched:
        nn_d, nt_d, tn_d = (((2,), (1,)), ((0,), (0,))), (((2,), (2,)), ((0,), (0,))), (((1,), (1,)), ((0,), (0,)))
    else:
        nn_d, nt_d, tn_d = (NN, ((), ())), (NT, ((), ())), (TN, ((), ()))

    @jax.custom_vjp
    def nn(a, b):
        return _dg(a, b, nn_d, mode)

    @jax.custom_vjp
    def nt(a, b):
        return _dg(a, b, nt_d, mode)

    @jax.custom_vjp
    def tn(a, b):
        return _dg(a, b, tn_d, mode)

    nn.defvjp(lambda a, b: (nn(a, b), (a, b)), lambda r, g: (nt(g, r[1]), tn(r[0], g)))
    nt.defvjp(lambda a, b: (nt(a, b), (a, b)), lambda r, g: (nn(g, r[1]), tn(g, r[0])))
    tn.defvjp(lambda a, b: (tn(a, b), (a, b)), lambda r, g: (nt(r[1], g), nn(r[0], g)))
    return nn, nt, tn


hnn = _make_dots("h")[0]
bbnn, bbnt, bbtn = _make_dots("b", True)
mbnn, mbnt, mbtn = _make_dots("m", True)
hbnt = _make_dots("h", True)[1]


def _split3(x, axis):
    x1 = x.astype(BF16)
    r1 = x - x1.astype(F32)
    x2 = r1.astype(BF16)
    x3 = (r1 - x2.astype(F32)).astype(BF16)
    return jnp.concatenate([x1, x2, x3], axis=axis)


@jax.custom_vjp
def _mask_dot(e3, x):
    return lax.dot_general(e3[0], _split3(x, 0), (NN, ((), ())), preferred_element_type=F32)


def _mask_dot_bwd(e3, g):
    dx = lax.dot_general(e3[1], _split3(g, 0), (TN, ((), ())), preferred_element_type=F32)
    return (jnp.zeros_like(e3[0]), jnp.zeros_like(e3[1])), dx


_mask_dot.defvjp(lambda e3, x: (_mask_dot(e3, x), e3), _mask_dot_bwd)


def _heads(a, n):
    return jnp.concatenate([a[None, :, h * HD:(h + 1) * HD] for h in range(n)], axis=0)


def _sigmoid(x):
    return jax.nn.sigmoid(x)


def _silu(x):
    return x * jax.nn.sigmoid(x)


def _softplus(x):
    return jnp.maximum(x, 0.0) + jnp.log(1.0 + jnp.exp(-jnp.abs(x)))


def _iota(shape, dim):
    return lax.broadcasted_iota(jnp.int32, shape, dim)


def _pick(n, prefs):
    for p in prefs:
        if n % p == 0:
            return p
    return n


def _mm(a, b, *, ta=False, tb=False, out_dtype=F32, name, b_view=None, out_split=0, act=False, gate=None, plus=None,
        ln=None, cargo=(), exchange=None):
    if ta:
        k_dim, m_dim = a.shape
    else:
        m_dim, k_dim = a.shape
    if b_view is None:
        w_rows, w_cols = b.shape
    else:
        kind, layer = b_view
        nj, _, blk_r, blk_c = b.shape
        w_rows, w_cols = (blk_r, nj * blk_c) if kind == "cols" else (nj * blk_r, blk_c)
    n_dim = w_rows if tb else w_cols
    assert (w_cols if tb else w_rows) == k_dim
    tm = _pick(m_dim, (1024, 1056, 704, 640, 512, 384, 256, 128))
    tn = _pick(n_dim, (1024, 1056, 704, 640, 512, 384, 256, 128))
    tk = _pick(k_dim, (1024, 1056, 704, 512, 384, 256, 128))
    nk = k_dim // tk
    a_spec = pl.BlockSpec((tk, tm), lambda i, j, k: (k, i)) if ta else pl.BlockSpec((tm, tk), lambda i, j, k: (i, k))
    wb = (tn, tk) if tb else (tk, tn)
    w_idx = (lambda i, j, k: (j, k)) if tb else (lambda i, j, k: (k, j))
    if b_view is None:
        b_spec = pl.BlockSpec(wb, w_idx)
    elif kind == "cols":
        per = blk_c // wb[1]
        b_spec = pl.BlockSpec((None, None) + wb,
                              lambda i, j, k: (w_idx(i, j, k)[1] // per, layer, w_idx(i, j, k)[0], w_idx(i, j, k)[1] % per))
    else:
        per = blk_r // wb[0]
        b_spec = pl.BlockSpec((None, None) + wb,
                              lambda i, j, k: (w_idx(i, j, k)[0] // per, layer, w_idx(i, j, k)[0] % per, w_idx(i, j, k)[1]))
    if out_split:
        per_o = (n_dim // out_split) // tn
        out_spec = pl.BlockSpec((None, tm, tn), lambda i, j, k: (j // per_o, i, j % per_o))
        out_sds = jax.ShapeDtypeStruct((out_split, m_dim, n_dim // out_split), out_dtype)
    else:
        out_spec = pl.BlockSpec((tm, tn), lambda i, j, k: (i, j))
        out_sds = jax.ShapeDtypeStruct((m_dim, n_dim), out_dtype)
    dims = (((0 if ta else 1,), (1 if tb else 0,)), ((), ()))
    assert sum(e is not None for e in (gate, plus, ln)) <= 1
    extra = [e for e in (gate, plus) if e is not None] + list(ln or ())
    n_out = 2 if act else 3 if ln else 1
    assert ln is None or (tn == n_dim and not out_split)

    def finish(acc, refs):
        if ln:
            y = _ln_res_fn(refs[0][...], acc, refs[1][...], refs[2][...])
            refs[3][...] = acc
            refs[4][...] = y
            refs[5][...] = y.astype(BF16)
        elif act:
            refs[0][...] = acc.astype(refs[0].dtype)
            r = jnp.maximum(acc, 0.0)
            refs[1][...] = (r * r).astype(refs[1].dtype)
        elif gate is not None:
            refs[1][...] = (acc * (2.0 * jnp.maximum(refs[0][...].astype(F32), 0.0))).astype(refs[1].dtype)
        elif plus is not None:
            refs[1][...] = (refs[0][...] + acc).astype(refs[1].dtype)
        else:
            refs[0][...] = acc.astype(refs[0].dtype)

    grid = (m_dim // tm, n_dim // tn, nk)
    nc = len(cargo)

    def body(a_ref, b_ref, *rest):
        acc_ref = rest[-1]
        ids = [pl.program_id(d) for d in range(3)]
        outs, end_cargo = _cargo_bounds(
            rest[len(extra):-1], nc, n_out, exchange, (ids[0] == 0) & (ids[1] == 0) & (ids[2] == 0),
            (ids[0] == grid[0] - 1) & (ids[1] == grid[1] - 1) & (ids[2] == grid[2] - 1))
        refs = tuple(rest[:len(extra)]) + tuple(outs)
        part = lax.dot_general(a_ref[...], b_ref[...], dims, preferred_element_type=F32)
        if nk == 1:
            finish(part, refs)
        else:
            k = ids[2]

            @pl.when(k == 0)
            def _():
                acc_ref[...] = part

            @pl.when(k > 0)
            def _():
                acc_ref[...] += part

            @pl.when(k == nk - 1)
            def _():
                finish(acc_ref[...], refs)
        end_cargo()

    tile = pl.BlockSpec((tm, tn), lambda i, j, k: (i, j))
    rowv = pl.BlockSpec((1, tn), lambda i, j, k: (0, j))
    out_sdss = [out_sds] * n_out
    if ln:
        out_sdss = [jax.ShapeDtypeStruct((m_dim, n_dim), dt) for dt in (F32, F32, BF16)]
    out = pl.pallas_call(
        body, name=name, grid=grid,
        in_specs=[a_spec, b_spec] + ([tile, rowv, rowv] if ln else [tile] * len(extra)) + [ANY] * nc,
        out_specs=[out_spec] * n_out + [ANY] * nc,
        out_shape=out_sdss + (exchange[1](cargo) if nc else []),
        scratch_shapes=(exchange[2](nc) if nc else []) + [pltpu.VMEM((tm, tn) if nk > 1 else (8, 128), F32)],
        compiler_params=_cparams(dimension_semantics=("arbitrary",) * 3 if nc else ("parallel", "parallel", "arbitrary")),
    )(a, b, *extra, *cargo)
    if nc:
        return out
    return out if (act or ln) else out[0]


def _mm_groups_nt(parts, b, name):
    m_dim, k_dim = parts[0].shape
    ng, _, n_dim, _ = b.shape
    assert len(parts) == ng and b.shape[3] == k_dim
    tm = _pick(m_dim, (1056, 704, 512, 384, 256, 128))

    def body(*refs):
        a_refs, b_ref, o_ref, acc_ref = refs[:ng], refs[ng], refs[ng + 1], refs[ng + 2]
        k = pl.program_id(1)
        for g in range(ng):
            @pl.when(k == g)
            def _(g=g):
                part = lax.dot_general(a_refs[g][...], b_ref[...], (NT, ((), ())), preferred_element_type=F32)
                if g == 0:
                    acc_ref[...] = part
                elif g < ng - 1:
                    acc_ref[...] += part
                else:
                    o_ref[...] = acc_ref[...] + part

    return pl.pallas_call(
        body, name=name, grid=(m_dim // tm, ng),
        in_specs=[pl.BlockSpec((tm, k_dim), lambda i, k: (i, 0))] * ng
        + [pl.BlockSpec((None, None, n_dim, k_dim), lambda i, k: (k, 0, 0, 0))],
        out_specs=pl.BlockSpec((tm, n_dim), lambda i, k: (i, 0)),
        out_shape=jax.ShapeDtypeStruct((m_dim, n_dim), F32),
        scratch_shapes=[pltpu.VMEM((tm, n_dim), F32)],
        compiler_params=_cparams(dimension_semantics=("parallel", "arbitrary")),
    )(*parts, b)


def _row_tile(t_pad, width):
    for tr in (528, 352, 176, 128, 64):
        if t_pad % tr == 0 and tr * width * 4 <= (3 << 19) and tr % 16 == 0:
            return tr
    return 64 if t_pad % 64 == 0 else t_pad


def _ln_res_fn(h, m, g, b):
    x = ALPHA * h + m
    mu = jnp.mean(x, axis=-1, keepdims=True)
    xc = x - mu
    var = jnp.mean(xc * xc, axis=-1, keepdims=True)
    return xc * lax.rsqrt(var + LN_EPS) * g + b


def _ln_res_bwd(h, m, g, b, dys, name):
    t_pad = h.shape[0]
    tr = _row_tile(t_pad, D)
    nd = len(dys)

    def body(h_ref, m_ref, g_ref, b_ref, *rest):
        d_refs, (dh_ref, dm_ref, dg_ref, db_ref) = rest[:nd], rest[nd:]
        _, vjp = jax.vjp(_ln_res_fn, h_ref[...], m_ref[...], g_ref[...], b_ref[...])
        dy = d_refs[0][...]
        for d_ref in d_refs[1:]:
            dy = dy + d_ref[...]
        dh, dm, dg, db = vjp(dy)
        dh_ref[...] = dh
        dm_ref[...] = dm.astype(BF16)

        @pl.when(pl.program_id(0) == 0)
        def _():
            dg_ref[...] = jnp.zeros_like(dg_ref)
            db_ref[...] = jnp.zeros_like(db_ref)

        dg_ref[...] += dg
        db_ref[...] += db

    row = pl.BlockSpec((tr, D), lambda i: (i, 0))
    par = pl.BlockSpec((1, D), lambda i: (0, 0))
    return pl.pallas_call(
        body, name=name, grid=(t_pad // tr,), in_specs=[row, row, par, par] + [row] * nd,
        out_specs=[row, row, par, par],
        out_shape=[jax.ShapeDtypeStruct((t_pad, D), F32), jax.ShapeDtypeStruct((t_pad, D), BF16),
                   jax.ShapeDtypeStruct((1, D), F32), jax.ShapeDtypeStruct((1, D), F32)],
        compiler_params=_cparams(),
    )(h, m, g, b, *dys)


def _grms_fn(o, z, g):
    y = o * lax.rsqrt(jnp.mean(o * o, axis=-1, keepdims=True) + RMS_EPS) * g
    return y * _silu(z)


def _grms_fwd(o, z_arr, z_blk0, g, name):
    t_pad, w = o.shape
    tr = _row_tile(t_pad, w)
    assert (z_blk0 * HD) % w == 0

    def body(o_ref, z_ref, g_ref, y_ref):
        for h in range(w // HD):
            c = slice(h * HD, (h + 1) * HD)
            y_ref[:, c] = _grms_fn(o_ref[:, c], z_ref[:, c], g_ref[...]).astype(BF16)

    return pl.pallas_call(
        body, name=name, grid=(t_pad // tr,),
        in_specs=[pl.BlockSpec((tr, w), lambda i: (i, 0)), pl.BlockSpec((tr, w), lambda i: (i, z_blk0 * HD // w)),
                  pl.BlockSpec((1, HD), lambda i: (0, 0))],
        out_specs=pl.BlockSpec((tr, w), lambda i: (i, 0)),
        out_shape=jax.ShapeDtypeStruct((t_pad, w), BF16), compiler_params=_cparams(),
    )(o, z_arr, g)


def _grms_bwd(o, z_arr, z_blk0, g, dy_arr, dy_blk0, name):
    t_pad, w = o.shape
    tr = _row_tile(t_pad, w)
    assert (z_blk0 * HD) % w == 0 and (dy_blk0 * HD) % w == 0

    def body(o_ref, z_ref, g_ref, dy_ref, do_ref, dz_ref, dg_ref):
        @pl.when(pl.program_id(0) == 0)
        def _():
            dg_ref[...] = jnp.zeros_like(dg_ref)

        for h in range(w // HD):
            c = slice(h * HD, (h + 1) * HD)
            _, vjp = jax.vjp(_grms_fn, o_ref[:, c], z_ref[:, c], g_ref[...])
            do, dz, dg = vjp(dy_ref[:, c])
            do_ref[:, c] = do
            dz_ref[:, c] = dz.astype(BF16)
            dg_ref[...] += dg

    blk = pl.BlockSpec((tr, w), lambda i: (i, 0))
    return pl.pallas_call(
        body, name=name, grid=(t_pad // tr,),
        in_specs=[blk, pl.BlockSpec((tr, w), lambda i: (i, z_blk0 * HD // w)), pl.BlockSpec((1, HD), lambda i: (0, 0)),
                  pl.BlockSpec((tr, w), lambda i: (i, dy_blk0 * HD // w))],
        out_specs=[blk, blk, pl.BlockSpec((1, HD), lambda i: (0, 0))],
        out_shape=[jax.ShapeDtypeStruct((t_pad, w), F32), jax.ShapeDtypeStruct((t_pad, w), BF16),
                   jax.ShapeDtypeStruct((1, HD), F32)],
        compiler_params=_cparams(),
    )(o, z_arr, g, dy_arr)


def _loss_fwd(y, tgt, first_row, name):
    t_pad = y.shape[0]
    tr = _row_tile(t_pad, D)

    def body(y_ref, t_ref, l_ref, dy_ref):
        rows = pl.program_id(0) * tr + _iota((tr, 1), 0)
        err = jnp.where(rows >= first_row, y_ref[...] - t_ref[...], 0.0)
        dy_ref[...] = err * (1.0 / D)

        @pl.when(pl.program_id(0) == 0)
        def _():
            l_ref[...] = jnp.zeros_like(l_ref)

        part = jnp.sum(jnp.sum(err * err, axis=1, keepdims=True), axis=0, keepdims=True)
        l_ref[...] += jnp.broadcast_to(part * (0.5 / D), l_ref.shape)

    row = pl.BlockSpec((tr, D), lambda i: (i, 0))
    return pl.pallas_call(
        body, name=name, grid=(t_pad // tr,), in_specs=[row, row],
        out_specs=[pl.BlockSpec((8, 128), lambda i: (0, 0)), row],
        out_shape=[jax.ShapeDtypeStruct((8, 128), F32), jax.ShapeDtypeStruct((t_pad, D), F32)],
        compiler_params=_cparams(),
    )(y, tgt)


def _assemble_bf16(parts, name):
    t_pad = parts[0].shape[0]
    widths = [p.shape[1] for p in parts]
    total = sum(widths)
    tr = _row_tile(t_pad, total)

    def body(*refs):
        o_ref = refs[-1]
        off = 0
        for ref, w in zip(refs[:-1], widths):
            o_ref[:, off:off + w] = ref[...].astype(BF16)
            off += w

    return pl.pallas_call(
        body, name=name, grid=(t_pad // tr,), in_specs=[pl.BlockSpec((tr, w), lambda i: (i, 0)) for w in widths],
        out_specs=pl.BlockSpec((tr, total), lambda i: (i, 0)),
        out_shape=jax.ShapeDtypeStruct((t_pad, total), BF16), compiler_params=_cparams(),
    )(*parts)


CONV_K = 4
HALO = 8
RT = 128


def _conv_fwd(p, blk0, w, mode, pad, name):
    t_pad = p.shape[0]
    nt = t_pad // RT
    scale = HD ** -0.5 if mode == "q" else 1.0

    def body(x_ref, w_ref, y_ref, xs_ref):
        xs_ref[0:HALO, :] = jnp.zeros((HALO, HD), F32)
        rows = _iota((t_pad, 1), 0)
        xs_ref[HALO:HALO + t_pad, :] = jnp.where(rows >= pad, x_ref[...], 0.0)
        wv = w_ref[...]

        def tile(i, carry):
            r0 = pl.multiple_of(i * RT, RT)
            ext = xs_ref[pl.ds(r0, RT + HALO), :]
            acc = ext[HALO:, :] * wv[3:4, :]
            for s in (1, 2, 3):
                acc = acc + pltpu.roll(ext, s, 0)[HALO:, :] * wv[3 - s:4 - s, :]
            y = _silu(acc)
            if mode != "v":
                y = y * lax.rsqrt(jnp.sum(y * y, axis=-1, keepdims=True) + L2_EPS) * scale
            y_ref[pl.ds(r0, RT), :] = y
            return carry

        lax.fori_loop(0, nt, tile, 0)

    return pl.pallas_call(
        body, name=name, grid=(GDN_H,),
        in_specs=[pl.BlockSpec((t_pad, HD), lambda h: (0, blk0 + h)), pl.BlockSpec((CONV_K, HD), lambda h: (0, h))],
        out_specs=pl.BlockSpec((t_pad, HD), lambda h: (0, h)),
        out_shape=jax.ShapeDtypeStruct((t_pad, GDN_H * HD), F32),
        scratch_shapes=[pltpu.VMEM((t_pad + HALO, HD), F32)],
        compiler_params=_cparams(),
    )(p, w)


def _conv_bwd(p, blk0, w, dn, mode, pad, name):
    t_pad = p.shape[0]
    nt = t_pad // RT
    scale = HD ** -0.5 if mode == "q" else 1.0

    def body(x_ref, w_ref, dn_ref, dx_ref, dw_ref, xs_ref, ds_ref):
        xs_ref[0:HALO, :] = jnp.zeros((HALO, HD), F32)
        xs_ref[HALO + t_pad:HALO + t_pad + 2 * HALO, :] = jnp.zeros((2 * HALO, HD), F32)
        ds_ref[t_pad:t_pad + HALO, :] = jnp.zeros((HALO, HD), F32)
        rows = _iota((t_pad, 1), 0)
        xs_ref[HALO:HALO + t_pad, :] = jnp.where(rows >= pad, x_ref[...], 0.0)
        ds_ref[0:t_pad, :] = dn_ref[...]
        wv = w_ref[...]

        def tile(i, dw):
            r0 = pl.multiple_of(i * RT, RT)
            ext = xs_ref[pl.ds(r0, RT + 2 * HALO), :]
            dn_e = ds_ref[pl.ds(r0, RT + HALO), :]
            xsh = [ext[HALO:, :]] + [pltpu.roll(ext, s, 0)[HALO:, :] for s in (1, 2, 3)]
            pre = xsh[0] * wv[3:4, :]
            for s in (1, 2, 3):
                pre = pre + xsh[s] * wv[3 - s:4 - s, :]
            sg = _sigmoid(pre)
            y = pre * sg
            if mode != "v":
                ss = jnp.sum(y * y, axis=-1, keepdims=True) + L2_EPS
                r = lax.rsqrt(ss)
                dy = scale * (dn_e * r - y * (r * r * r) * jnp.sum(dn_e * y, axis=-1, keepdims=True))
            else:
                dy = dn_e
            dpre = dy * (sg * (1.0 + pre * (1.0 - sg)))
            dx = dpre[:RT, :] * wv[3:4, :]
            for s in (1, 2, 3):
                dx = dx + pltpu.roll(dpre, RT + HALO - s, 0)[:RT, :] * wv[3 - s:4 - s, :]
            trow = r0 + _iota((RT, 1), 0)
            dx_ref[pl.ds(r0, RT), :] = jnp.where(trow >= pad, dx, 0.0)
            new = []
            for s in (0, 1, 2, 3):
                new.append(dw[s] + jnp.sum(dpre[:RT, :] * xsh[s][:RT, :], axis=0, keepdims=True))
            return tuple(new)

        z = jnp.zeros((1, HD), F32)
        dw = lax.fori_loop(0, nt, tile, (z, z, z, z))
        for s in (0, 1, 2, 3):
            dw_ref[3 - s:4 - s, :] = dw[s]

    return pl.pallas_call(
        body, name=name, grid=(GDN_H,),
        in_specs=[pl.BlockSpec((t_pad, HD), lambda h: (0, blk0 + h)), pl.BlockSpec((CONV_K, HD), lambda h: (0, h)),
                  pl.BlockSpec((t_pad, HD), lambda h: (0, h))],
        out_specs=[pl.BlockSpec((t_pad, HD), lambda h: (0, h)), pl.BlockSpec((CONV_K, HD), lambda h: (0, h))],
        out_shape=[jax.ShapeDtypeStruct((t_pad, GDN_H * HD), F32), jax.ShapeDtypeStruct((CONV_K, GDN_H * HD), F32)],
        scratch_shapes=[pltpu.VMEM((t_pad + 3 * HALO, HD), F32), pltpu.VMEM((t_pad + HALO, HD), F32)],
        compiler_params=_cparams(),
    )(p, w, dn)


@jax.custom_vjp
def _unit_lower_inv(m, bd, eye):
    md = m * bd
    low = m - md
    p2 = mbnn(md, md)
    p4 = mbnn(p2, p2)
    dinv = mbnn(mbnn(eye - md, eye + p2), eye + p4)
    n = mbnn(dinv, low)
    n2 = mbnn(n, n)
    n4 = mbnn(n2, n2)
    return mbnn(mbnn(mbnn(eye - n, eye + n2), eye + n4), dinv)


def _unit_lower_inv_bwd(res, g):
    t, bd, eye = res
    return -mbtn(t, mbnt(g, t)), jnp.zeros_like(bd), jnp.zeros_like(eye)


def _unit_lower_inv_fwd(m, bd, eye):
    t = _unit_lower_inv(m, bd, eye)
    return t, (t, bd, eye)


_unit_lower_inv.defvjp(_unit_lower_inv_fwd, _unit_lower_inv_bwd)


def _gdn_chunks(chunks, alog, dtb, s):
    nh = chunks[0][0].shape[0]
    ri = _iota((1, CH, CH), 1)
    ci = _iota((1, CH, CH), 2)
    causal = ri >= ci
    strict = ri > ci
    eye = (ri == ci).astype(F32)
    bd = ((ri >> 3) == (ci >> 3)).astype(F32)
    ltri = (_iota((CH, CH), 0) >= _iota((CH, CH), 1)).astype(F32)
    sel = (_iota((nh, 1, HD), 2) == _iota((nh, 1, HD), 0)).astype(F32)
    last = _iota((1, CH, 1), 1) == CH - 1

    beta, gc, gc_rows = [], [], []
    for _, _, _, bb, aa, valid in chunks:
        beta_all = jnp.where(valid, _sigmoid(bb), 0.0)
        g_all = jnp.where(valid, -jnp.exp(alog) * _softplus(aa + dtb), 0.0)
        gc_all = hnn(ltri, g_all)
        beta.append(jnp.sum(beta_all[None] * sel, axis=2, keepdims=True))
        gc.append(jnp.sum(gc_all[None] * sel, axis=2, keepdims=True))
        gc_rows.append(hbnt(jnp.broadcast_to(sel, (nh, CH, HD)), jnp.broadcast_to(gc_all[None], (nh, CH, HD))))
    cat = lambda xs: jnp.concatenate(xs, axis=0)
    q, k, v = (cat([c[j] for c in chunks]) for j in range(3))
    beta, gc, gc_rows = cat(beta), cat(gc), cat(gc_rows)
    gc_last = jnp.sum(jnp.where(last, gc, 0.0), axis=1, keepdims=True)
    decay = jnp.exp(jnp.where(causal, gc - gc_rows, NEG))
    egc = jnp.exp(gc)

    kb = k * beta
    m = jnp.where(strict, bbnt(kb, k) * decay, 0.0)
    t_inv = _unit_lower_inv(m, bd, eye)
    u = bbnn(t_inv, v * beta)
    w = bbnn(t_inv, kb * egc)
    a_intra = bbnt(q, k) * decay
    q_dec = q * egc
    k_dec = k * jnp.exp(gc_last - gc)
    g_tot = jnp.exp(gc_last)

    outs = []
    for n in range(len(chunks)):
        part = lambda a: a[n * nh:(n + 1) * nh]
        v_new = part(u) - bbnn(part(w), s)
        outs.append(bbnn(part(q_dec), s) + bbnn(part(a_intra), v_new))
        s = s * part(g_tot) + bbtn(part(k_dec), v_new)
    return outs, s


PAIR = 2 * CH


def _gdn_specs(npair, rev):
    cc = (lambda c: npair - 1 - c) if rev else (lambda c: c)
    wide = pl.BlockSpec((PAIR, GDN_H * HD), lambda c: (cc(c), 0))
    fix = lambda off: pl.BlockSpec((PAIR, HD), lambda c: (cc(c), off))
    par = pl.BlockSpec((1, HD), lambda c: (0, 0))
    state = pl.BlockSpec((1, GDN_H, HD, HD), lambda c: (cc(c), 0, 0, 0))
    return wide, fix, par, state


def _store_heads(ref, a, rows=slice(None)):
    for h in range(a.shape[0]):
        ref[rows, h * HD:(h + 1) * HD] = a[h]


def _chunk_rows(half):
    return slice(half * CH, (half + 1) * CH)


def _chunk_valid(pair, half, pad):
    return ((2 * pair + half) * CH + _iota((CH, 1), 0)) >= pad


def _gdn_fwd(qn, kn, vn, p, alog, dtb, pad, name, cargo=(), exchange=None):
    t_pad = qn.shape[0]
    npair = t_pad // PAIR
    wide, fix, par, state = _gdn_specs(npair, False)
    n = len(cargo)

    def body(q_ref, k_ref, v_ref, bb_ref, aa_ref, al_ref, dt_ref, *rest):
        c = pl.program_id(0)
        s_ref = rest[-1]
        (o_ref, ss_ref), end_cargo = _cargo_bounds(rest[:-1], n, 2, exchange, c == 0, c == npair - 1)

        @pl.when(c == 0)
        def _():
            s_ref[...] = jnp.zeros_like(s_ref)

        s = s_ref[...]
        ss_ref[0] = s
        rows = [_chunk_rows(half) for half in (0, 1)]
        chunks = [(_heads(q_ref[r, :], GDN_H), _heads(k_ref[r, :], GDN_H), _heads(v_ref[r, :], GDN_H),
                   bb_ref[r, :], aa_ref[r, :], _chunk_valid(c, half, pad)) for half, r in enumerate(rows)]
        outs, s = _gdn_chunks(chunks, al_ref[...], dt_ref[...], s)
        for r, o in zip(rows, outs):
            _store_heads(o_ref, o, r)
        s_ref[...] = s
        end_cargo()

    return pl.pallas_call(
        body, name=name, grid=(npair,),
        in_specs=[wide, wide, wide, fix(16), fix(17), par, par] + [ANY] * n,
        out_specs=[wide, state] + [ANY] * n,
        out_shape=[jax.ShapeDtypeStruct((t_pad, GDN_H * HD), F32), jax.ShapeDtypeStruct((npair, GDN_H, HD, HD), F32)]
        + (exchange[1](cargo) if n else []),
        scratch_shapes=(exchange[2](n) if n else []) + [pltpu.VMEM((GDN_H, HD, HD), F32)],
        compiler_params=_cparams(),
    )(qn, kn, vn, p, p, alog, dtb, *cargo)


def _gdn_bwd(qn, kn, vn, p, alog, dtb, ssave, do, pad, name, cargo=(), exchange=None):
    t_pad = qn.shape[0]
    npair = t_pad // PAIR
    wide, fix, par, state = _gdn_specs(npair, True)
    n = len(cargo)

    def body(q_ref, k_ref, v_ref, bb_ref, aa_ref, al_ref, dt_ref, ss_ref, do_ref, *rest):
        c = pl.program_id(0)
        ds_ref = rest[-1]
        (dq_ref, dk_ref, dv_ref, dbb_ref, daa_ref, dal_ref, ddt_ref), end_cargo = _cargo_bounds(
            rest[:-1], n, 7, exchange, c == 0, c == npair - 1)

        @pl.when(c == 0)
        def _():
            ds_ref[...] = jnp.zeros_like(ds_ref)
            dal_ref[...] = jnp.zeros_like(dal_ref)
            ddt_ref[...] = jnp.zeros_like(ddt_ref)

        ra, rb = _chunk_rows(0), _chunk_rows(1)
        va, vb = _chunk_valid(npair - 1 - c, 0, pad), _chunk_valid(npair - 1 - c, 1, pad)

        def pair(qa, ka, va_, ba, aa, qb, kb, vb_, bb, ab, al, dt, s):
            (oa, ob), s = _gdn_chunks([(qa, ka, va_, ba, aa, va), (qb, kb, vb_, bb, ab, vb)], al, dt, s)
            return oa, ob, s

        ins = [f(ref[r, :]) for r in (ra, rb)
               for ref, f in ((q_ref, lambda a: _heads(a, GDN_H)), (k_ref, lambda a: _heads(a, GDN_H)),
                              (v_ref, lambda a: _heads(a, GDN_H)), (bb_ref, lambda a: a), (aa_ref, lambda a: a))]
        _, vjp = jax.vjp(pair, *ins, al_ref[...], dt_ref[...], ss_ref[0])
        g = vjp((_heads(do_ref[ra, :], GDN_H), _heads(do_ref[rb, :], GDN_H), ds_ref[...]))
        for r, (dq, dk, dv, dbb, daa) in ((ra, g[0:5]), (rb, g[5:10])):
            _store_heads(dq_ref, dq, r)
            _store_heads(dk_ref, dk, r)
            _store_heads(dv_ref, dv, r)
            dbb_ref[r, :] = dbb
            daa_ref[r, :] = daa
        dal_ref[...] += g[10]
        ddt_ref[...] += g[11]
        ds_ref[...] = g[12]
        end_cargo()

    sds = jax.ShapeDtypeStruct
    return pl.pallas_call(
        body, name=name, grid=(npair,),
        in_specs=[wide, wide, wide, fix(16), fix(17), par, par, state, wide] + [ANY] * n,
        out_specs=[wide, wide, wide, fix(0), fix(0), par, par] + [ANY] * n,
        out_shape=[sds((t_pad, GDN_H * HD), F32)] * 3 + [sds((t_pad, HD), F32)] * 2 + [sds((1, HD), F32)] * 2
        + (exchange[1](cargo) if n else []),
        scratch_shapes=(exchange[2](n) if n else []) + [pltpu.VMEM((GDN_H, HD, HD), F32)],
        compiler_params=_cparams(),
    )(qn, kn, vn, p, p, alog, dtb, ssave, do, *cargo)


SB_Q0, SB_K0, SB_V0 = 18, 22, 26
SB_SCALE = SB_DH ** -0.5
SB_NB = 8


def _sb_terms(z, allowed):
    e = jnp.exp(-jnp.abs(z))
    den = 1.0 + e
    raw = -jnp.maximum(z, 0.0) - jnp.log(den)
    l1m = raw if allowed is None else jnp.where(allowed, raw, 0.0)
    ls = z + raw
    return l1m, ls, jnp.exp(ls)


def _sb_passes(i, step, carry):
    total = i + 1
    sized = lambda done: [functools.partial(step, done, masked=True, nb=nb) for nb in range(1, SB_NB + 1)]

    def several(c):
        n_mid = (total - SB_NB - 1) // SB_NB
        c = step(0, c, masked=True, nb=SB_NB)
        c = lax.fori_loop(0, n_mid, lambda t, cc: step(SB_NB * (1 + t), cc, masked=False, nb=SB_NB), c)
        done = SB_NB * (1 + n_mid)
        return lax.switch(total - done - 1, sized(done), c)

    return lax.cond(total <= SB_NB, lambda c: lax.switch(total - 1, sized(0), c), several, carry)


def _sb_stack(a, i):
    first = _iota((1, HD), 1) < SB_DH
    a2 = jnp.concatenate([jnp.where(first, a, 0.0), jnp.where(first, 0.0, a)], axis=0).astype(BF16)
    rq = i * QB + _iota((QB, 1), 0)
    return a2, jnp.concatenate([rq, rq], axis=0), first


def _hi_lo(a):
    hi = a.astype(BF16)
    lo = (a - hi.astype(F32)).astype(BF16)
    return jnp.concatenate([hi, lo], axis=1)


def _cargo_bounds(refs, n, n_out, exchange, first, last):
    outs = refs[n:n + n_out]
    if not n:
        return outs, lambda: None
    ex = exchange[0](refs[:n], refs[n + n_out:2 * n + n_out], *refs[2 * n + n_out:])

    @pl.when(first)
    def _():
        ex.start()

    def finish():
        @pl.when(last)
        def _():
            ex.wait()

    return outs, finish


def _sb_fwd(p, pad, name, cargo=(), exchange=None):
    t_pad = p.shape[0]
    nq = t_pad // QB
    n = len(cargo)

    def body(q_ref, k_ref, v_ref, *rest):
        i = pl.program_id(1)
        pr = pl.program_id(0)
        (o_ref, r_ref), end_cargo = _cargo_bounds(rest, n, 2, exchange, (pr == 0) & (i == 0),
                                                  (pr == SB_H // 2 - 1) & (i == nq - 1))
        q2, rowq, first = _sb_stack(q_ref[...] * SB_SCALE, i)
        tri = (_iota((QB, QB), 0) > _iota((QB, QB), 1)).astype(BF16)
        upper2 = jnp.concatenate([jnp.concatenate([tri, tri], axis=0), jnp.ones((2 * QB, QB), BF16)], axis=1)

        def chain(kb, masked):
            start = pl.multiple_of(kb * QB, QB)
            kblk = k_ref[pl.ds(start, QB), :].astype(BF16)
            vblk = v_ref[pl.ds(start, QB), :].astype(BF16)
            z = lax.dot_general(q2, kblk, (NT, ((), ())), preferred_element_type=F32)
            colk = kb * QB + _iota((1, QB), 1)
            al = ((colk < rowq) & (colk >= pad)) if masked else None
            l1m, ls, _ = _sb_terms(z, al)
            sums = lax.dot_general(_hi_lo(l1m), upper2, (NN, ((), ())), preferred_element_type=F32)
            return al, ls, sums[:, :QB], sums[:, QB:], vblk

        def step(done, carry, masked, nb):
            o_acc, run = carry
            ws, vs = [], []
            for n in range(nb):
                al, ls, suf, rs, vblk = chain(i - done - n, masked)
                wgt = jnp.exp(ls + suf + run)
                ws.append((wgt if al is None else jnp.where(al, wgt, 0.0)).astype(BF16))
                vs.append(vblk)
                run = run + rs
            o_acc = o_acc + lax.dot_general(jnp.concatenate(ws, axis=1), jnp.concatenate(vs, axis=0),
                                            (NN, ((), ())), preferred_element_type=F32)
            return o_acc, run

        o_acc, run = _sb_passes(i, step, (jnp.zeros((2 * QB, HD), F32), jnp.zeros((2 * QB, QB), F32)))
        o_ref[...] = jnp.where(first, o_acc[:QB], o_acc[QB:]).astype(BF16)
        r_ref[...] = jnp.where(first, run[:QB], run[QB:])
        end_cargo()

    full = lambda off: pl.BlockSpec((t_pad, HD), lambda pr, i: (0, off + pr))
    blk = pl.BlockSpec((QB, HD), lambda pr, i: (i, pr))
    return pl.pallas_call(
        body, name=name, grid=(SB_H // 2, nq),
        in_specs=[pl.BlockSpec((QB, HD), lambda pr, i: (i, SB_Q0 + pr)), full(SB_K0), full(SB_V0)] + [ANY] * n,
        out_specs=[blk, blk] + [ANY] * n,
        out_shape=[jax.ShapeDtypeStruct((t_pad, SB_H * SB_DH), BF16), jax.ShapeDtypeStruct((t_pad, SB_H * SB_DH), F32)]
        + (exchange[1](cargo) if n else []),
        scratch_shapes=exchange[2](n) if n else [],
        compiler_params=_cparams(),
    )(p, p, p, *cargo)


def _sb_bwd(p, rtot, dy, dy_blk0, pad, name, cargo=(), exchange=None):
    t_pad = p.shape[0]
    nq = t_pad // QB
    n = len(cargo)

    def body(q_ref, k_ref, v_ref, r_ref, do_ref, *rest):
        i = pl.program_id(1)
        pr = pl.program_id(0)
        dkt_ref, dvt_ref = rest[-2:]
        (dq_ref, dk_ref, dv_ref), end_cargo = _cargo_bounds(rest[:-2], n, 3, exchange, (pr == 0) & (i == 0),
                                                            (pr == SB_H // 2 - 1) & (i == nq - 1))

        @pl.when(i == 0)
        def _():
            dkt_ref[...] = jnp.zeros_like(dkt_ref)
            dvt_ref[...] = jnp.zeros_like(dvt_ref)

        q2, rowq, first = _sb_stack(q_ref[...] * SB_SCALE, i)
        do2, _, _ = _sb_stack(do_ref[...], i)
        q2t = jnp.transpose(q2.astype(F32)).astype(BF16)
        do2t = jnp.transpose(do2.astype(F32)).astype(BF16)
        rt = r_ref[...]
        lane = _iota((1, HD), 1)
        rcol = jnp.concatenate([jnp.sum(jnp.where(lane == 0, rt, 0.0), axis=1, keepdims=True),
                                jnp.sum(jnp.where(lane == SB_DH, rt, 0.0), axis=1, keepdims=True)], axis=0)
        rj = _iota((QB, QB), 0)
        cs = _iota((QB, QB), 1)
        tri_u = (rj > cs).astype(BF16)
        tri_l = (rj < cs).astype(BF16)
        ones2 = jnp.ones((2 * QB, QB), BF16)
        upper2 = jnp.concatenate([jnp.concatenate([tri_u, tri_u], axis=0), ones2], axis=1)
        lower2 = jnp.concatenate([jnp.concatenate([tri_l, tri_l], axis=0), ones2], axis=1)
        rcol = jnp.broadcast_to(rcol, (2 * QB, QB))

        def chain(kb, masked):
            start = pl.multiple_of(kb * QB, QB)
            kblk = k_ref[pl.ds(start, QB), :].astype(BF16)
            vblk = v_ref[pl.ds(start, QB), :].astype(BF16)
            z = lax.dot_general(q2, kblk, (NT, ((), ())), preferred_element_type=F32)
            colk = kb * QB + _iota((1, QB), 1)
            al = ((colk < rowq) & (colk >= pad)) if masked else None
            l1m, ls, sg = _sb_terms(z, al)
            dwgt = lax.dot_general(do2, vblk, (NT, ((), ())), preferred_element_type=F32)
            sums = lax.dot_general(_hi_lo(l1m), upper2, (NN, ((), ())), preferred_element_type=F32)
            return kb, kblk, al, ls, sums[:, :QB], sums[:, QB:], dwgt, sg

        def finish(c, seen, gseen):
            kb, kblk, al, ls, suf, rs, dwgt, sg = c
            wgt = jnp.exp(ls + suf + (rcol - seen - rs))
            if al is not None:
                wgt = jnp.where(al, wgt, 0.0)
            dl = dwgt * wgt
            sums = lax.dot_general(_hi_lo(dl), lower2, (NN, ((), ())), preferred_element_type=F32)
            gpre = gseen + sums[:, :QB]
            dz = dl - sg * (dl + gpre)
            if al is not None:
                dz = jnp.where(al, dz, 0.0)
            dz = dz.astype(BF16)
            dkt_ref[kb] += lax.dot_general(q2t, dz, (NN, ((), ())), preferred_element_type=F32)
            dvt_ref[kb] += lax.dot_general(do2t, wgt.astype(BF16), (NN, ((), ())), preferred_element_type=F32)
            return dz, seen + rs, gseen + sums[:, QB:]

        def step(done, carry, masked, nb):
            dq_acc, seen, gseen = carry
            cs_ = [chain(done + n, masked) for n in range(nb)]
            dzs = []
            for c in cs_:
                dz, seen, gseen = finish(c, seen, gseen)
                dzs.append(dz)
            dq_acc = dq_acc + lax.dot_general(jnp.concatenate(dzs, axis=1), jnp.concatenate([c[1] for c in cs_], axis=0),
                                              (NN, ((), ())), preferred_element_type=F32)
            return dq_acc, seen, gseen

        zc = jnp.zeros((2 * QB, QB), F32)
        dq_acc, _, _ = _sb_passes(i, step, (jnp.zeros((2 * QB, HD), F32), zc, zc))
        dq_ref[...] = jnp.where(first, dq_acc[:QB], dq_acc[QB:]) * SB_SCALE

        @pl.when(i == nq - 1)
        def _():
            for kb in range(nq):
                dk_ref[kb * QB:(kb + 1) * QB, :] = jnp.transpose(dkt_ref[kb])
                dv_ref[kb * QB:(kb + 1) * QB, :] = jnp.transpose(dvt_ref[kb])

        end_cargo()

    full_in = lambda off: pl.BlockSpec((t_pad, HD), lambda pr, i: (0, off + pr))
    full_out = pl.BlockSpec((t_pad, HD), lambda pr, i: (0, pr))
    blk = pl.BlockSpec((QB, HD), lambda pr, i: (i, pr))
    sds = jax.ShapeDtypeStruct((t_pad, SB_H * SB_DH), F32)
    return pl.pallas_call(
        body, name=name, grid=(SB_H // 2, nq),
        in_specs=[pl.BlockSpec((QB, HD), lambda pr, i: (i, SB_Q0 + pr)), full_in(SB_K0), full_in(SB_V0), blk,
                  pl.BlockSpec((QB, HD), lambda pr, i: (i, dy_blk0 + pr))] + [ANY] * n,
        out_specs=[blk, full_out, full_out] + [ANY] * n,
        out_shape=[sds, sds, sds] + (exchange[1](cargo) if n else []),
        scratch_shapes=(exchange[2](n) if n else []) + [pltpu.VMEM((nq, HD, QB), F32)] * 2,
        compiler_params=_cparams(),
    )(p, p, p, rtot, dy, *cargo)


HG_LEVELS = 6


def _hg_prefix_matrix():
    t = np.arange(CH)[:, None]
    j = np.arange(CH)[None, :]
    groups = [(j <= t)]
    for lvl in range(1, HG_LEVELS + 1):
        half = CH >> lvl
        e = (t // (2 * half)) * (2 * half) + half - 1
        groups.append(j <= e)
    groups.append(np.ones((8, CH), bool))
    e = np.concatenate(groups, axis=0).astype(np.float32)
    return np.concatenate([e, e, e], axis=1), np.concatenate([e, e, e], axis=0)


HG_G = 4


def _hg_chunk(qr, fr, iv, r0, r1, st, valid, ecat):
    g = st.shape[0]
    mx = jnp.maximum(r0, r1)
    e0 = jnp.exp(r0 - mx)
    e1 = jnp.exp(r1 - mx)
    lb = e1 / (e0 + e1)
    fg = lb + (1.0 - lb) * _sigmoid(fr)
    logf = jnp.where(valid, jnp.log(fg), 0.0)
    kk = jnp.where(valid, 1.0 - fg, 0.0)
    q = jnp.where(valid, _silu(qr), 0.0)
    v = _heads(jnp.where(valid, iv, 0.0), g)

    pre = _mask_dot(ecat, logf)
    b = pre[0:CH]
    b_last = jnp.max(pre[(HG_LEVELS + 1) * CH:], axis=0, keepdims=True)
    row = _iota((CH, 1), 0)
    ri = _iota((1, CH, CH), 1)
    ci = _iota((1, CH, CH), 2)
    a = jnp.where(ri == ci, jnp.sum(_heads(q * kk, g), axis=2, keepdims=True), 0.0)
    for lvl in range(1, HG_LEVELS + 1):
        half = CH >> lvl
        m = pre[lvl * CH:(lvl + 1) * CH]
        low = (row & half) != 0
        dec = jnp.exp(jnp.where(low, b - m, m - b))
        qt = jnp.where(low, q * dec, 0.0)
        kt = jnp.where(low, 0.0, kk * dec)
        same = (ri >> (7 - lvl)) == (ci >> (7 - lvl))
        a = a + jnp.where(same, bbnt(_heads(qt, g), _heads(kt, g)), 0.0)
    o = bbnt(_heads(q * jnp.exp(b), g), st) + bbnn(a, v)
    kd = kk * jnp.exp(b_last - b)
    st_new = st * _heads(jnp.exp(b_last), g) + bbtn(v, _heads(kd, g))
    return o, st_new


def _hg_specs(npair, rev):
    cc = (lambda c: npair - 1 - c) if rev else (lambda c: c)
    ng = HG_H // HG_G
    blk = lambda off: pl.BlockSpec((PAIR, HG_G * HD), lambda h, c: (cc(c), off * ng + h))
    lbs = pl.BlockSpec((2, HG_G * HD), lambda h, c: (0, h))
    state = pl.BlockSpec((1, HG_G, HD, HD), lambda h, c: (cc(c), h, 0, 0))
    return ng, blk, lbs, state


def _hg_fwd(p, lbraw, ecat, pad, name):
    t_pad = p.shape[0]
    npair = t_pad // PAIR
    ng, blk, lbs, state = _hg_specs(npair, False)

    def body(q_ref, f_ref, i_ref, lb_ref, e_ref, et_ref, o_ref, ss_ref, s_ref):
        c = pl.program_id(1)

        @pl.when(c == 0)
        def _():
            s_ref[...] = jnp.zeros_like(s_ref)

        st = s_ref[...]
        ss_ref[0] = st
        for half in (0, 1):
            r = _chunk_rows(half)
            o, st = _hg_chunk(q_ref[r, :], f_ref[r, :], i_ref[r, :], lb_ref[0:1, :], lb_ref[1:2, :], st,
                              _chunk_valid(c, half, pad), (e_ref[...], et_ref[...]))
            _store_heads(o_ref, o, r)
        s_ref[...] = st

    return pl.pallas_call(
        body, name=name, grid=(ng, npair),
        in_specs=[blk(0), blk(1), blk(2), lbs] + [pl.BlockSpec(e.shape, lambda h, c: (0, 0)) for e in ecat],
        out_specs=[blk(0), state],
        out_shape=[jax.ShapeDtypeStruct((t_pad, HG_H * HD), F32), jax.ShapeDtypeStruct((npair, HG_H, HD, HD), F32)],
        scratch_shapes=[pltpu.VMEM((HG_G, HD, HD), F32)],
        compiler_params=_cparams(),
    )(p, p, p, lbraw, *ecat)


def _hg_bwd(p, lbraw, ecat, ssave, do, pad, name, cargo=(), exchange=None):
    t_pad = p.shape[0]
    npair = t_pad // PAIR
    ng, blk, lbs, state = _hg_specs(npair, True)
    n = len(cargo)

    def body(q_ref, f_ref, i_ref, lb_ref, e_ref, et_ref, ss_ref, do_ref, *rest):
        c = pl.program_id(1)
        hg = pl.program_id(0)
        ds_ref = rest[-1]
        (dq_ref, df_ref, di_ref, dlb_ref), end_cargo = _cargo_bounds(
            rest[:-1], n, 4, exchange, (hg == 0) & (c == 0), (hg == ng - 1) & (c == npair - 1))

        @pl.when(c == 0)
        def _():
            ds_ref[...] = jnp.zeros_like(ds_ref)
            dlb_ref[...] = jnp.zeros_like(dlb_ref)

        ra, rb = _chunk_rows(0), _chunk_rows(1)
        va, vb = _chunk_valid(npair - 1 - c, 0, pad), _chunk_valid(npair - 1 - c, 1, pad)
        ecv = (e_ref[...], et_ref[...])

        def pair(qa, fa, ia, qb, fb, ib, r0, r1, st):
            oa, st = _hg_chunk(qa, fa, ia, r0, r1, st, va, ecv)
            ob, st = _hg_chunk(qb, fb, ib, r0, r1, st, vb, ecv)
            return oa, ob, st

        ins = [ref[r, :] for r in (ra, rb) for ref in (q_ref, f_ref, i_ref)]
        _, vjp = jax.vjp(pair, *ins, lb_ref[0:1, :], lb_ref[1:2, :], ss_ref[0])
        g = vjp((_heads(do_ref[ra, :], HG_G), _heads(do_ref[rb, :], HG_G), ds_ref[...]))
        for r, (dq, df, di) in ((ra, g[0:3]), (rb, g[3:6])):
            dq_ref[r, :] = dq.astype(BF16)
            df_ref[r, :] = df.astype(BF16)
            di_ref[r, :] = di.astype(BF16)
        dlb_ref[0:1, :] += g[6]
        dlb_ref[1:2, :] += g[7]
        ds_ref[...] = g[8]
        end_cargo()

    sds = jax.ShapeDtypeStruct((t_pad, HG_H * HD), BF16)
    return pl.pallas_call(
        body, name=name, grid=(ng, npair),
        in_specs=[blk(0), blk(1), blk(2), lbs] + [pl.BlockSpec(e.shape, lambda h, c: (0, 0)) for e in ecat]
        + [state, blk(0)] + [ANY] * n,
        out_specs=[blk(0), blk(0), blk(0), lbs] + [ANY] * n,
        out_shape=[sds, sds, sds, jax.ShapeDtypeStruct((2, HG_H * HD), F32)] + (exchange[1](cargo) if n else []),
        scratch_shapes=(exchange[2](n) if n else []) + [pltpu.VMEM((HG_G, HD, HD), F32)],
        compiler_params=_cparams(),
    )(p, p, p, lbraw, *ecat, ssave, do, *cargo)


def _pad_ab_cols(w):
    z = jnp.zeros((w.shape[0], HD - GDN_H), w.dtype)
    return jnp.concatenate([w[:, :2048], w[:, 2048:2052], z, w[:, 2052:2056], z, w[:, 2056:]], axis=1)


def _unpad_ab_cols(w):
    return jnp.concatenate([w[:, :2048], w[:, 2048:2052], w[:, 2176:2180], w[:, 2304:]], axis=1)


def _lane_pad(v):
    return jnp.pad(v, ((0, 0), (0, HD - v.shape[1])))


def _mlp_fwd(h, hb, w1, w2, layer, g, b):
    a, r = _mm(hb, w1, b_view=("cols", layer), out_dtype=BF16, act=True, name=f"mlp_up_{layer}")
    m, y, yb = _mm(r, w2, b_view=("rows", layer), ln=(h, g, b), name=f"mlp_down_{layer}")
    return a, r, m, y, yb


def _mlp_bwd(hb, a, r, dmb, w1, w2, layer):
    da = _mm(dmb, w2, tb=True, b_view=("rows", layer), out_dtype=BF16, gate=a, name=f"mlp_down_dx_{layer}")
    dw2 = _mm(r, dmb, ta=True, out_dtype=BF16, name=f"mlp_down_dw_{layer}")
    dh = _mm(da, w1, tb=True, b_view=("cols", layer), name=f"mlp_up_dx_{layer}")
    dw1 = _mm(hb, da, ta=True, out_dtype=BF16, out_split=N_CHIP, name=f"mlp_up_dw_{layer}")
    return dh, dw1, dw2


def _local_step(h0, tgt, w, pad, late=None):
    row = lambda a, i: a[i:i + 1]
    ecat = tuple(jnp.asarray(e, dtype=BF16) for e in _hg_prefix_matrix())
    cw = [w["conv_w"][:, i * 512:(i + 1) * 512] for i in range(3)]
    alog, dtb = _lane_pad(w["a_log"]), _lane_pad(w["dt_bias"])

    h0b = h0.astype(BF16)
    p0 = _mm(h0b, w["ab_w_in"], name="ab_in")
    qn = _conv_fwd(p0, 0, cw[0], "q", pad, "conv_q")
    kn = _conv_fwd(p0, 4, cw[1], "k", pad, "conv_k")
    vn = _conv_fwd(p0, 8, cw[2], "v", pad, "conv_v")
    if late is None:
        oa_raw, ss0 = _gdn_fwd(qn, kn, vn, p0, alog, dtb, pad, "gdn_fwd")
        ob, rtot = _sb_fwd(p0, pad, "sb_fwd")
    else:
        oa_raw, ss0, g_cin, g_cout = _gdn_fwd(qn, kn, vn, p0, alog, dtb, pad, "gdn_fwd",
                                              cargo=[late["c_w_in"], late["c_w_out"]], exchange=GATHER)
        ob, rtot, g_about, g_w1, g_w2 = _sb_fwd(p0, pad, "sb_fwd", exchange=GATHER,
                                                cargo=[late["ab_w_out"], late["mlp_w1"], late["mlp_w2"]])
        w = dict(w, ab_w_out=g_about.reshape(D, D), c_w_in=g_cin, c_w_out=g_cout.reshape(D, D), mlp_w1=g_w1, mlp_w2=g_w2)
    oa = _grms_fwd(oa_raw, p0, 12, w["ab_gnorm_g"], "gdn_gate")
    ycat = jnp.concatenate([oa, ob], axis=1)
    mix0, h1, h1b = _mm(ycat, w["ab_w_out"], name="ab_out", ln=(h0, row(w["ln_mix_g"], 0), row(w["ln_mix_b"], 0)))
    a0, r0, m0, h2, h2b = _mlp_fwd(h1, h1b, w["mlp_w1"], w["mlp_w2"], 0, row(w["ln_ffn_g"], 0), row(w["ln_ffn_b"], 0))
    p1 = _mm(h2b, w["c_w_in"], b_view=("cols", 0), name="c_in")
    oc_raw, ss1 = _hg_fwd(p1, w["c_lb_raw"], ecat, pad, "hg_fwd")
    yc = _grms_fwd(oc_raw, p1, 3 * HG_H, w["c_gnorm_g"], "hg_gate")
    mix1, h3, h3b = _mm(yc, w["c_w_out"], name="c_out", ln=(h2, row(w["ln_mix_g"], 1), row(w["ln_mix_b"], 1)))
    a1, r1, m1, h4, _ = _mlp_fwd(h3, h3b, w["mlp_w1"], w["mlp_w2"], 1, row(w["ln_ffn_g"], 1), row(w["ln_ffn_b"], 1))
    loss, dh4 = _loss_fwd(h4, tgt, pad + N_META, "loss")

    dh3a, dm1b, dfg1, dfb1 = _ln_res_bwd(h3, m1, row(w["ln_ffn_g"], 1), row(w["ln_ffn_b"], 1), [dh4], "ln_ffn_bwd_1")
    dh3b, dw1_1, dw2_1 = _mlp_bwd(h3b, a1, r1, dm1b, w["mlp_w1"], w["mlp_w2"], 1)
    dh2a, dmix1b, dmg1, dmb1 = _ln_res_bwd(h2, mix1, row(w["ln_mix_g"], 1), row(w["ln_mix_b"], 1), [dh3a, dh3b], "ln_mix_bwd_1")
    dyc = _mm(dmix1b, w["c_w_out"], tb=True, name="c_out_dx")
    dwco = _mm(yc, dmix1b, ta=True, out_dtype=BF16, name="c_out_dw")
    doc, dzc, dcg = _grms_bwd(oc_raw, p1, 3 * HG_H, w["c_gnorm_g"], dyc, 0, "hg_gate_bwd")
    landed = {}
    rows4 = lambda a: a.reshape(N_CHIP, -1, D)
    if late is None:
        dq1, df1, di1, dlb = _hg_bwd(p1, w["c_lb_raw"], ecat, ss1, doc, pad, "hg_bwd")
    else:
        dq1, df1, di1, dlb, landed["w1_1"] = _hg_bwd(
            p1, w["c_lb_raw"], ecat, ss1, doc, pad, "hg_bwd", cargo=[dw1_1], exchange=SCATTER)
    dp1 = [dq1, df1, di1, dzc]
    dh2b = _mm_groups_nt(dp1, w["c_w_in"], "c_in_dx")
    dwc = jnp.stack([_mm(h2b, d, ta=True, out_dtype=BF16, name=f"c_in_dw_{i}") for i, d in enumerate(dp1)])
    dh1a, dm0b, dfg0, dfb0 = _ln_res_bwd(h1, m0, row(w["ln_ffn_g"], 0), row(w["ln_ffn_b"], 0), [dh2a, dh2b], "ln_ffn_bwd_0")
    dh1b, dw1_0, dw2_0 = _mlp_bwd(h1b, a0, r0, dm0b, w["mlp_w1"], w["mlp_w2"], 0)
    dh0a, dmix0b, dmg0, dmb0 = _ln_res_bwd(h0, mix0, row(w["ln_mix_g"], 0), row(w["ln_mix_b"], 0), [dh1a, dh1b], "ln_mix_bwd_0")
    dycat = _mm(dmix0b, w["ab_w_out"], tb=True, name="ab_out_dx")
    dwabo = _mm(ycat, dmix0b, ta=True, out_dtype=BF16, name="ab_out_dw")
    doa, dza, dag = _grms_bwd(oa_raw, p0, 12, w["ab_gnorm_g"], dycat, 0, "gdn_gate_bwd")
    if late is None:
        dqn, dkn, dvn, dbb, daa, dal, ddt = _gdn_bwd(qn, kn, vn, p0, alog, dtb, ss0, doa, pad, "gdn_bwd")
        dqb, dkb, dvb = _sb_bwd(p0, rtot, dycat, 4, pad, "sb_bwd")
    else:
        dqn, dkn, dvn, dbb, daa, dal, ddt, landed["c_w_in"] = _gdn_bwd(
            qn, kn, vn, p0, alog, dtb, ss0, doa, pad, "gdn_bwd", cargo=[dwc], exchange=SCATTER)
        (dqb, dkb, dvb, landed["w1_0"], landed["w2_0"], landed["w2_1"], landed["ab_w_out"],
         landed["c_w_out"]) = _sb_bwd(
            p0, rtot, dycat, 4, pad, "sb_bwd",
            cargo=[dw1_0, rows4(dw2_0), rows4(dw2_1), rows4(dwabo), rows4(dwco)], exchange=SCATTER)
    dpq, dcq = _conv_bwd(p0, 0, cw[0], dqn, "q", pad, "conv_q_bwd")
    dpk, dck = _conv_bwd(p0, 4, cw[1], dkn, "k", pad, "conv_k_bwd")
    dpv, dcv = _conv_bwd(p0, 8, cw[2], dvn, "v", pad, "conv_v_bwd")
    dp0 = _assemble_bf16([dpq, dpk, dpv, dza, dbb, daa, dqb, dkb, dvb], "ab_in_dy")
    dwab = _mm(h0b, dp0, ta=True, out_dtype=BF16, name="ab_in_dw")
    if late is None:
        dh0 = _mm(dp0, w["ab_w_in"], tb=True, plus=dh0a, name="ab_in_dx")
    else:
        dab = jnp.transpose(_unpad_ab_cols(dwab).reshape(D, N_CHIP, AB_TRUE // N_CHIP), (1, 0, 2))
        dh0, landed["ab_w_in"] = _mm(dp0, w["ab_w_in"], tb=True, plus=dh0a, name="ab_in_dx", cargo=[dab],
                                     exchange=SCATTER)

    grads = {
        "ab_w_in": dwab, "conv_w": jnp.concatenate([dcq, dck, dcv], axis=1),
        "a_log": dal[:, :GDN_H], "dt_bias": ddt[:, :GDN_H],
        "ab_gnorm_g": dag, "ab_w_out": dwabo, "c_w_in": dwc, "c_lb_raw": dlb, "c_gnorm_g": dcg, "c_w_out": dwco,
        "ln_mix_g": jnp.concatenate([dmg0, dmg1], 0), "ln_mix_b": jnp.concatenate([dmb0, dmb1], 0),
        "w1_0": dw1_0, "w1_1": dw1_1, "w2_0": dw2_0, "w2_1": dw2_1,
        "ln_ffn_g": jnp.concatenate([dfg0, dfg1], 0), "ln_ffn_b": jnp.concatenate([dfb0, dfb1], 0),
        "landed": landed,
    }
    return loss, dh0, grads


MESH = pl.DeviceIdType.MESH
ANY = pl.BlockSpec(memory_space=pl.ANY)
N_CHIP = 4
N_DEV = 8
CHIP_REL = ((1, 0), (0, 1), (1, 1))
DEV_REL = tuple((dx, dy, dc) for dx in (0, 1) for dy in (0, 1) for dc in (0, 1))[1:]

def _pos():
    return lax.axis_index("x"), lax.axis_index("y"), lax.axis_index("c")


def _flip(a, d):
    return a + d - 2 * a * d


class _Exchange:
    def __init__(self, local, sends, recvs):
        self.local, self.sends, self.recvs = local, sends, recvs

    def start(self):
        for cp in self.local + self.sends:
            cp.start()

    def wait(self):
        for cp in self.recvs:
            cp.wait_recv()
        for cp in self.sends:
            cp.wait_send()
        for cp in self.local:
            cp.wait()


def _gather_sems(n):
    return [pltpu.SemaphoreType.DMA((3 * n,)), pltpu.SemaphoreType.DMA((3 * n,)), pltpu.SemaphoreType.DMA((n,))]


def _gather_copies(x_refs, o_refs, send_sems, recv_sems, local_sems):
    n = len(x_refs)
    x, y, c = _pos()
    local = [pltpu.make_async_copy(x_refs[a], o_refs[a].at[2 * x + y], local_sems.at[a]) for a in range(n)]

    def copy(a, k, sending):
        tx, ty = _flip(x, CHIP_REL[k][0]), _flip(y, CHIP_REL[k][1])
        return pltpu.make_async_remote_copy(
            src_ref=x_refs[a], dst_ref=o_refs[a].at[2 * x + y if sending else 2 * tx + ty],
            send_sem=send_sems.at[3 * a + k], recv_sem=recv_sems.at[3 * a + k], device_id=(tx, ty, c), device_id_type=MESH)

    pairs = [(a, k) for a in range(n) for k in range(3)]
    return _Exchange(local, [copy(a, k, True) for a, k in pairs], [copy(a, k, False) for a, k in pairs])


def _gather_shapes(bufs):
    return [jax.ShapeDtypeStruct((N_CHIP,) + b.shape, b.dtype) for b in bufs]


def _chip_allgather(bufs, name):
    n = len(bufs)

    def body(*refs):
        ex = _gather_copies(refs[:n], refs[n:2 * n], *refs[2 * n:])
        ex.start()
        ex.wait()

    return pl.pallas_call(
        body, name=name, in_specs=[ANY] * n, out_specs=[ANY] * n, out_shape=_gather_shapes(bufs),
        scratch_shapes=_gather_sems(n), compiler_params=pltpu.CompilerParams(has_side_effects=True),
    )(*bufs)


def _scatter_sems(n):
    nr = N_DEV - 1
    return [pltpu.SemaphoreType.DMA((nr * n,)), pltpu.SemaphoreType.DMA((nr * n,)), pltpu.SemaphoreType.DMA((n,))]


def _scatter_copies(g_refs, o_refs, send_sems, recv_sems, local_sems):
    n = len(g_refs)
    nr = N_DEV - 1
    x, y, c = _pos()
    me = 4 * x + 2 * y + c
    local = [pltpu.make_async_copy(g_refs[a].at[2 * x + y], o_refs[a].at[me], local_sems.at[a]) for a in range(n)]

    def copy(a, k, sending):
        dx, dy, dc = DEV_REL[k]
        tx, ty, tc = _flip(x, dx), _flip(y, dy), _flip(c, dc)
        return pltpu.make_async_remote_copy(
            src_ref=g_refs[a].at[2 * tx + ty], dst_ref=o_refs[a].at[me if sending else 4 * tx + 2 * ty + tc],
            send_sem=send_sems.at[nr * a + k], recv_sem=recv_sems.at[nr * a + k],
            device_id=(tx, ty, tc), device_id_type=MESH)

    pairs = [(a, k) for a in range(n) for k in range(nr)]
    return _Exchange(local, [copy(a, k, True) for a, k in pairs], [copy(a, k, False) for a, k in pairs])


def _scatter_shapes(gs):
    return [jax.ShapeDtypeStruct((N_DEV,) + g.shape[1:], g.dtype) for g in gs]


GATHER = (_gather_copies, _gather_shapes, _gather_sems)
SCATTER = (_scatter_copies, _scatter_shapes, _scatter_sems)


def _sum_slots(rs, name):
    n, rh, w = rs[0].shape
    tr = _pick(rh, (256, 128, 64, 16))

    def body(*refs):
        o_ref = refs[-1]
        for layer, r_ref in enumerate(refs[:-1]):
            acc = r_ref[0].astype(F32)
            for s in range(1, n):
                acc = acc + r_ref[s].astype(F32)
            o_ref[layer] = acc

    return pl.pallas_call(
        body, name=name, grid=(rh // tr,), in_specs=[pl.BlockSpec((n, tr, w), lambda i: (0, i, 0))] * len(rs),
        out_specs=pl.BlockSpec((len(rs), tr, w), lambda i: (0, i, 0)),
        out_shape=jax.ShapeDtypeStruct((len(rs), rh, w), F32), compiler_params=_cparams(),
    )(*rs)


def _small_allreduce(buf, name):
    r, w = buf.shape

    def body(b_ref, o_ref, land_ref, send_sems, recv_sems):
        x, y, c = _pos()
        me = 4 * x + 2 * y + c
        land_ref[me] = b_ref[...]

        def target(k):
            dx, dy, dc = DEV_REL[k]
            return _flip(x, dx), _flip(y, dy), _flip(c, dc)

        sends = []
        for k in range(N_DEV - 1):
            tx, ty, tc = target(k)
            cp = pltpu.make_async_remote_copy(
                src_ref=b_ref, dst_ref=land_ref.at[me], send_sem=send_sems.at[k], recv_sem=recv_sems.at[k],
                device_id=(tx, ty, tc), device_id_type=MESH)
            cp.start()
            sends.append(cp)
        for k in range(N_DEV - 1):
            tx, ty, tc = target(k)
            pltpu.make_async_remote_copy(
                src_ref=b_ref, dst_ref=land_ref.at[4 * tx + 2 * ty + tc], send_sem=send_sems.at[k],
                recv_sem=recv_sems.at[k], device_id=(tx, ty, tc), device_id_type=MESH).wait_recv()
        for cp in sends:
            cp.wait_send()
        acc = land_ref[0]
        for s in range(1, N_DEV):
            acc = acc + land_ref[s]
        o_ref[...] = acc

    vm = pl.BlockSpec(memory_space=pltpu.VMEM)
    return pl.pallas_call(
        body, name=name, in_specs=[vm], out_specs=vm, out_shape=jax.ShapeDtypeStruct((r, w), F32),
        scratch_shapes=[pltpu.VMEM((N_DEV, r, w), F32), pltpu.SemaphoreType.DMA((N_DEV - 1,)),
                        pltpu.SemaphoreType.DMA((N_DEV - 1,))],
        compiler_params=pltpu.CompilerParams(has_side_effects=True),
    )(buf)


def _adamw(w, g, m, v, name):
    r, c = w.shape
    tr = _pick(r, (256, 128, 64, 8)) if r * c > (1 << 18) else r

    def body(w_ref, g_ref, m_ref, v_ref, d_ref, m2_ref, v2_ref):
        gg = g_ref[...]
        m2 = ADAM_B1 * m_ref[...] + (1.0 - ADAM_B1) * gg
        v2 = ADAM_B2 * v_ref[...] + (1.0 - ADAM_B2) * (gg * gg)
        m_hat = m2 / (1.0 - ADAM_B1 ** ADAM_STEP)
        v_hat = v2 / (1.0 - ADAM_B2 ** ADAM_STEP)
        d_ref[...] = -ADAM_LR * (m_hat / (jnp.sqrt(v_hat) + ADAM_EPS) + ADAM_WD * w_ref[...])
        m2_ref[...] = m2
        v2_ref[...] = v2

    blk = pl.BlockSpec((tr, c), lambda i: (i, 0))
    sds = jax.ShapeDtypeStruct((r, c), F32)
    return pl.pallas_call(body, name=name, grid=(r // tr,), in_specs=[blk] * 4, out_specs=[blk] * 3,
                          out_shape=[sds] * 3, compiler_params=_cparams())(w, g, m, v)


BIG = ("ab_w_in", "ab_w_out", "c_w_in", "c_w_out", "mlp_w1", "mlp_w2")
SMALL = ("ln_mix_g", "ln_mix_b", "ln_ffn_g", "ln_ffn_b", "c_lb_raw", "ab_a_log", "ab_dt_bias", "ab_gnorm_g", "c_gnorm_g")
SMALL_ROWS = 16
CONV_ROWS = 8
CONV_W = 3 * GDN_H * HD


def _conv_to_rows(cw):
    return jnp.pad(cw, ((0, 0), (0, 2 * D - CONV_W))).reshape(CONV_ROWS, D)


def _rows_to_conv(rows):
    return rows.reshape(CONV_K, 2 * D)[:, :CONV_W]


def _pack_small(d):
    rows = [jnp.pad(d[n], ((0, 0), (0, D - d[n].shape[1]))) for n in SMALL]
    buf = jnp.concatenate(rows, axis=0)
    return jnp.pad(buf, ((0, SMALL_ROWS - buf.shape[0]), (0, 0)))


def _unpack_small(buf, like):
    out, r = {}, 0
    for n in SMALL:
        nr, nc = like[n].shape
        out[n] = buf[r:r + nr, :nc]
        r += nr
    return out


def kernel(x, meta_tokens, ab_w_in, ab_conv_w, ab_a_log, ab_dt_bias, ab_gnorm_g, ab_w_out, c_w_in, c_lb_raw, c_gnorm_g, c_w_out, ln_mix_g, ln_mix_b, mlp_w1, mlp_w2, ln_ffn_g, ln_ffn_b, loss_target, m_meta_tokens, m_ab_w_in, m_ab_conv_w, m_ab_a_log, m_ab_dt_bias, m_ab_gnorm_g, m_ab_w_out, m_c_w_in, m_c_lb_raw, m_c_gnorm_g, m_c_w_out, m_ln_mix_g, m_ln_mix_b, m_mlp_w1, m_mlp_w2, m_ln_ffn_g, m_ln_ffn_b, v_meta_tokens, v_ab_w_in, v_ab_conv_w, v_ab_a_log, v_ab_dt_bias, v_ab_gnorm_g, v_ab_w_out, v_c_w_in, v_c_lb_raw, v_c_gnorm_g, v_c_w_out, v_ln_mix_g, v_ln_mix_b, v_mlp_w1, v_mlp_w2, v_ln_ffn_g, v_ln_ffn_b):
    names = ("meta_tokens", "ab_w_in", "ab_conv_w", "ab_a_log", "ab_dt_bias", "ab_gnorm_g", "ab_w_out", "c_w_in",
             "c_lb_raw", "c_gnorm_g", "c_w_out", "ln_mix_g", "ln_mix_b", "mlp_w1", "mlp_w2", "ln_ffn_g", "ln_ffn_b")
    wts = dict(zip(names, (meta_tokens, ab_w_in, ab_conv_w, ab_a_log, ab_dt_bias, ab_gnorm_g, ab_w_out, c_w_in, c_lb_raw,
                           c_gnorm_g, c_w_out, ln_mix_g, ln_mix_b, mlp_w1, mlp_w2, ln_ffn_g, ln_ffn_b)))
    mom_m = dict(zip(names, (m_meta_tokens, m_ab_w_in, m_ab_conv_w, m_ab_a_log, m_ab_dt_bias, m_ab_gnorm_g, m_ab_w_out,
                             m_c_w_in, m_c_lb_raw, m_c_gnorm_g, m_c_w_out, m_ln_mix_g, m_ln_mix_b, m_mlp_w1, m_mlp_w2,
                             m_ln_ffn_g, m_ln_ffn_b)))
    mom_v = dict(zip(names, (v_meta_tokens, v_ab_w_in, v_ab_conv_w, v_ab_a_log, v_ab_dt_bias, v_ab_gnorm_g, v_ab_w_out,
                             v_c_w_in, v_c_lb_raw, v_c_gnorm_g, v_c_w_out, v_ln_mix_g, v_ln_mix_b, v_mlp_w1, v_mlp_w2,
                             v_ln_ffn_g, v_ln_ffn_b)))
    seq = x.shape[1]
    pad = (-(N_META + seq)) % QB
    xi, yi, ci = _pos()
    chip = 2 * xi + yi

    gat_ab_in, = _chip_allgather([ab_w_in[0].astype(BF16)], "gather_weights")
    late = {"ab_w_out": ab_w_out[0].astype(BF16), "c_w_in": c_w_in.astype(BF16), "c_w_out": c_w_out[0].astype(BF16),
            "mlp_w1": mlp_w1.astype(BF16), "mlp_w2": mlp_w2.astype(BF16)}
    mcols, ccols = meta_tokens.shape[1], ab_conv_w.shape[2]
    place = jnp.concatenate([
        lax.dynamic_update_slice(jnp.zeros((N_META, D), F32), 0.5 * meta_tokens, (0, chip * mcols)),
        _conv_to_rows(lax.dynamic_update_slice(jnp.zeros((CONV_K, CONV_W), F32), 0.5 * ab_conv_w[0], (0, chip * ccols)))],
        axis=0)
    placed = _small_allreduce(place, "gather_meta")
    meta_full = placed[:N_META]

    w = {
        "ab_w_in": _pad_ab_cols(jnp.transpose(gat_ab_in, (1, 0, 2)).reshape(D, AB_TRUE)),
        "conv_w": _rows_to_conv(placed[N_META:]), "a_log": ab_a_log, "dt_bias": ab_dt_bias,
        "ab_gnorm_g": ab_gnorm_g, "c_lb_raw": c_lb_raw,
        "c_gnorm_g": c_gnorm_g, "ln_mix_g": ln_mix_g, "ln_mix_b": ln_mix_b, "ln_ffn_g": ln_ffn_g, "ln_ffn_b": ln_ffn_b,
    }

    h0 = jnp.concatenate([jnp.zeros((pad, D), F32), meta_full, x[0]], axis=0)
    tgt = jnp.concatenate([jnp.zeros((pad + N_META, D), F32), loss_target[0]], axis=0)
    loss8, dh0, g = _local_step(h0, tgt, w, pad, late)
    grad_x = dh0[pad + N_META:][None]

    gsmall = {"ln_mix_g": g["ln_mix_g"], "ln_mix_b": g["ln_mix_b"], "ln_ffn_g": g["ln_ffn_g"], "ln_ffn_b": g["ln_ffn_b"],
              "c_lb_raw": g["c_lb_raw"], "ab_a_log": g["a_log"], "ab_dt_bias": g["dt_bias"], "ab_gnorm_g": g["ab_gnorm_g"],
              "c_gnorm_g": g["c_gnorm_g"]}
    packed = _pack_small(gsmall).at[SMALL_ROWS - 1, :loss8.shape[1]].set(loss8[0])
    sbuf = jnp.concatenate([packed, dh0[pad:pad + N_META], _conv_to_rows(g["conv_w"])], axis=0)
    ssum = _small_allreduce(sbuf, "allreduce_small")
    loss = ssum[SMALL_ROWS - 1, 0]
    grads = _unpack_small(ssum[:SMALL_ROWS], wts)
    grads["meta_tokens"] = lax.dynamic_slice(ssum[SMALL_ROWS:SMALL_ROWS + N_META], (0, chip * mcols), (N_META, mcols))
    grads["ab_conv_w"] = lax.dynamic_slice(_rows_to_conv(ssum[SMALL_ROWS + N_META:]), (0, chip * ccols), (CONV_K, ccols))[None]

    landed = g["landed"]
    for n in ("ab_w_in", "ab_w_out", "c_w_in", "c_w_out"):
        grads[n] = _sum_slots([landed[n]], f"grad_sum_{n}")
    grads["mlp_w1"] = _sum_slots([landed["w1_0"], landed["w1_1"]], "grad_sum_mlp_w1")
    grads["mlp_w2"] = _sum_slots([landed["w2_0"], landed["w2_1"]], "grad_sum_mlp_w2")

    delta, new_m, new_v = {}, {}, {}
    for n in ("meta_tokens", "ab_conv_w") + BIG:
        shp = wts[n].shape
        to2 = lambda a: a.reshape(-1, shp[-1])
        d2, m2, v2 = _adamw(to2(wts[n]), to2(grads[n]), to2(mom_m[n]), to2(mom_v[n]), f"adamw_{n}")
        delta[n], new_m[n], new_v[n] = d2.reshape(shp), m2.reshape(shp), v2.reshape(shp)
    d2, m2, v2 = _adamw(_pack_small(wts), ssum[:SMALL_ROWS], _pack_small(mom_m), _pack_small(mom_v), "adamw_small")
    delta.update(_unpack_small(d2, wts))
    new_m.update(_unpack_small(m2, wts))
    new_v.update(_unpack_small(v2, wts))

    return (loss, grad_x, *[grads[n] for n in names], *[delta[n] for n in names], *[new_m[n] for n in names],
            *[new_v[n] for n in names])
```

```python
import functools

import numpy as np
import jax
import jax.numpy as jnp
from jax import lax
from jax.experimental import pallas as pl
from jax.experimental.pallas import tpu as pltpu

F32 = jnp.float32
BF16 = jnp.bfloat16

D = 1024
N_META = 16
DEPTH = 2
GDN_H = 4
SB_H = 8
SB_DH = 64
HG_H = 8
HD = 128
CH = 64
QB = 128
ALPHA = float((2 * DEPTH) ** 0.25)
LN_EPS = 1e-5
RMS_EPS = 1e-6
L2_EPS = 1e-6
NEG = -1e30

ADAM_LR = 0.001
ADAM_B1 = 0.9
ADAM_B2 = 0.999
ADAM_EPS = 1e-08
ADAM_WD = 0.01
ADAM_STEP = 10

AB_TRUE = 3592
V7X_VMEM_BYTES = 64 * 1024 * 1024
VMEM_LIMIT = V7X_VMEM_BYTES - 8 * 1024 * 1024

NN = ((1,), (0,))
NT = ((1,), (1,))
TN = ((0,), (0,))


def _cparams(**kw):
    return pltpu.CompilerParams(vmem_limit_bytes=VMEM_LIMIT, **kw)


def _dg(a, b, dims, mode):
    if mode == "h":
        return lax.dot_general(a, b, dims, precision=lax.Precision.HIGHEST, preferred_element_type=F32)
    if mode == "b":
        return lax.dot_general(a.astype(BF16), b.astype(BF16), dims, preferred_element_type=F32)
    ah, bh = a.astype(BF16), b.astype(BF16)
    al, bl = (a - ah.astype(F32)).astype(BF16), (b - bh.astype(F32)).astype(BF16)
    d = lambda x, y: lax.dot_general(x, y, dims, preferred_element_type=F32)
    return d(ah, bh) + (d(ah, bl) + d(al, bh))


def _make_dots(mode, batched=False):
    if batched:
        nn_d, nt_d, tn_d = (((2,), (1,)), ((0,), (0,))), (((2,), (2,)), ((0,), (0,))), (((1,), (1,)), ((0,), (0,)))
    else:
        nn_d, nt_d, tn_d = (NN, ((), ())), (NT, ((), ())), (TN, ((), ()))

    @jax.custom_vjp
    def nn(a, b):
        return _dg(a, b, nn_d, mode)

    @jax.custom_vjp
    def nt(a, b):
        return _dg(a, b, nt_d, mode)

    @jax.custom_vjp
    def tn(a, b):
        return _dg(a, b, tn_d, mode)

    nn.defvjp(lambda a, b: (nn(a, b), (a, b)), lambda r, g: (nt(g, r[1]), tn(r[0], g)))
    nt.defvjp(lambda a, b: (nt(a, b), (a, b)), lambda r, g: (nn(g, r[1]), tn(g, r[0])))
    tn.defvjp(lambda a, b: (tn(a, b), (a, b)), lambda r, g: (nt(r[1], g), nn(r[0], g)))
    return nn, nt, tn


hnn = _make_dots("h")[0]
bbnn, bbnt, bbtn = _make_dots("b", True)
mbnn, mbnt, mbtn = _make_dots("m", True)
hbnt = _make_dots("h", True)[1]


def _split3(x, axis):
    x1 = x.astype(BF16)
    r1 = x - x1.astype(F32)
    x2 = r1.astype(BF16)
    x3 = (r1 - x2.astype(F32)).astype(BF16)
    return jnp.concatenate([x1, x2, x3], axis=axis)


@jax.custom_vjp
def _mask_dot(e3, x):
    return lax.dot_general(e3[0], _split3(x, 0), (NN, ((), ())), preferred_element_type=F32)


def _mask_dot_bwd(e3, g):
    dx = lax.dot_general(e3[1], _split3(g, 0), (TN, ((), ())), preferred_element_type=F32)
    return (jnp.zeros_like(e3[0]), jnp.zeros_like(e3[1])), dx


_mask_dot.defvjp(lambda e3, x: (_mask_dot(e3, x), e3), _mask_dot_bwd)


def _heads(a, n):
    return jnp.concatenate([a[None, :, h * HD:(h + 1) * HD] for h in range(n)], axis=0)


def _sigmoid(x):
    return jax.nn.sigmoid(x)


def _silu(x):
    return x * jax.nn.sigmoid(x)


def _softplus(x):
    return jnp.maximum(x, 0.0) + jnp.log(1.0 + jnp.exp(-jnp.abs(x)))


def _iota(shape, dim):
    return lax.broadcasted_iota(jnp.int32, shape, dim)


def _pick(n, prefs):
    for p in prefs:
        if n % p == 0:
            return p
    return n


def _mm(a, b, *, ta=False, tb=False, out_dtype=F32, name, b_view=None, out_split=0, act=False, gate=None, plus=None,
        ln=None, cargo=(), exchange=None):
    if ta:
        k_dim, m_dim = a.shape
    else:
        m_dim, k_dim = a.shape
    if b_view is None:
        w_rows, w_cols = b.shape
    else:
        kind, layer = b_view
        nj, _, blk_r, blk_c = b.shape
        w_rows, w_cols = (blk_r, nj * blk_c) if kind == "cols" else (nj * blk_r, blk_c)
    n_dim = w_rows if tb else w_cols
    assert (w_cols if tb else w_rows) == k_dim
    tm = _pick(m_dim, (1024, 1056, 704, 640, 512, 384, 256, 128))
    tn = _pick(n_dim, (1024, 1056, 704, 640, 512, 384, 256, 128))
    tk = _pick(k_dim, (1024, 1056, 704, 512, 384, 256, 128))
    nk = k_dim // tk
    a_spec = pl.BlockSpec((tk, tm), lambda i, j, k: (k, i)) if ta else pl.BlockSpec((tm, tk), lambda i, j, k: (i, k))
    wb = (tn, tk) if tb else (tk, tn)
    w_idx = (lambda i, j, k: (j, k)) if tb else (lambda i, j, k: (k, j))
    if b_view is None:
        b_spec = pl.BlockSpec(wb, w_idx)
    elif kind == "cols":
        per = blk_c // wb[1]
        b_spec = pl.BlockSpec((None, None) + wb,
                              lambda i, j, k: (w_idx(i, j, k)[1] // per, layer, w_idx(i, j, k)[0], w_idx(i, j, k)[1] % per))
    else:
        per = blk_r // wb[0]
        b_spec = pl.BlockSpec((None, None) + wb,
                              lambda i, j, k: (w_idx(i, j, k)[0] // per, layer, w_idx(i, j, k)[0] % per, w_idx(i, j, k)[1]))
    if out_split:
        per_o = (n_dim // out_split) // tn
        out_spec = pl.BlockSpec((None, tm, tn), lambda i, j, k: (j // per_o, i, j % per_o))
        out_sds = jax.ShapeDtypeStruct((out_split, m_dim, n_dim // out_split), out_dtype)
    else:
        out_spec = pl.BlockSpec((tm, tn), lambda i, j, k: (i, j))
        out_sds = jax.ShapeDtypeStruct((m_dim, n_dim), out_dtype)
    dims = (((0 if ta else 1,), (1 if tb else 0,)), ((), ()))
    assert sum(e is not None for e in (gate, plus, ln)) <= 1
    extra = [e for e in (gate, plus) if e is not None] + list(ln or ())
    n_out = 2 if act else 3 if ln else 1
    assert ln is None or (tn == n_dim and not out_split)

    def finish(acc, refs):
        if ln:
            y = _ln_res_fn(refs[0][...], acc, refs[1][...], refs[2][...])
            refs[3][...] = acc
            refs[4][...] = y
            refs[5][...] = y.astype(BF16)
        elif act:
            refs[0][...] = acc.astype(refs[0].dtype)
            r = jnp.maximum(acc, 0.0)
            refs[1][...] = (r * r).astype(refs[1].dtype)
        elif gate is not None:
            refs[1][...] = (acc * (2.0 * jnp.maximum(refs[0][...].astype(F32), 0.0))).astype(refs[1].dtype)
        elif plus is not None:
            refs[1][...] = (refs[0][...] + acc).astype(refs[1].dtype)
        else:
            refs[0][...] = acc.astype(refs[0].dtype)

    grid = (m_dim // tm, n_dim // tn, nk)
    nc = len(cargo)

    def body(a_ref, b_ref, *rest):
        acc_ref = rest[-1]
        ids = [pl.program_id(d) for d in range(3)]
        outs, end_cargo = _cargo_bounds(
            rest[len(extra):-1], nc, n_out, exchange, (ids[0] == 0) & (ids[1] == 0) & (ids[2] == 0),
            (ids[0] == grid[0] - 1) & (ids[1] == grid[1] - 1) & (ids[2] == grid[2] - 1))
        refs = tuple(rest[:len(extra)]) + tuple(outs)
        part = lax.dot_general(a_ref[...], b_ref[...], dims, preferred_element_type=F32)
        if nk == 1:
            finish(part, refs)
        else:
            k = ids[2]

            @pl.when(k == 0)
            def _():
                acc_ref[...] = part

            @pl.when(k > 0)
            def _():
                acc_ref[...] += part

            @pl.when(k == nk - 1)
            def _():
                finish(acc_ref[...], refs)
        end_cargo()

    tile = pl.BlockSpec((tm, tn), lambda i, j, k: (i, j))
    rowv = pl.BlockSpec((1, tn), lambda i, j, k: (0, j))
    out_sdss = [out_sds] * n_out
    if ln:
        out_sdss = [jax.ShapeDtypeStruct((m_dim, n_dim), dt) for dt in (F32, F32, BF16)]
    out = pl.pallas_call(
        body, name=name, grid=grid,
        in_specs=[a_spec, b_spec] + ([tile, rowv, rowv] if ln else [tile] * len(extra)) + [ANY] * nc,
        out_specs=[out_spec] * n_out + [ANY] * nc,
        out_shape=out_sdss + (exchange[1](cargo) if nc else []),
        scratch_shapes=(exchange[2](nc) if nc else []) + [pltpu.VMEM((tm, tn) if nk > 1 else (8, 128), F32)],
        compiler_params=_cparams(dimension_semantics=("arbitrary",) * 3 if nc else ("parallel", "parallel", "arbitrary")),
    )(a, b, *extra, *cargo)
    if nc:
        return out
    return out if (act or ln) else out[0]


def _mm_groups_nt(parts, b, name):
    m_dim, k_dim = parts[0].shape
    ng, _, n_dim, _ = b.shape
    assert len(parts) == ng and b.shape[3] == k_dim
    tm = _pick(m_dim, (1056, 704, 512, 384, 256, 128))

    def body(*refs):
        a_refs, b_ref, o_ref, acc_ref = refs[:ng], refs[ng], refs[ng + 1], refs[ng + 2]
        k = pl.program_id(1)
        for g in range(ng):
            @pl.when(k == g)
            def _(g=g):
                part = lax.dot_general(a_refs[g][...], b_ref[...], (NT, ((), ())), preferred_element_type=F32)
                if g == 0:
                    acc_ref[...] = part
                elif g < ng - 1:
                    acc_ref[...] += part
                else:
                    o_ref[...] = acc_ref[...] + part

    return pl.pallas_call(
        body, name=name, grid=(m_dim // tm, ng),
        in_specs=[pl.BlockSpec((tm, k_dim), lambda i, k: (i, 0))] * ng
        + [pl.BlockSpec((None, None, n_dim, k_dim), lambda i, k: (k, 0, 0, 0))],
        out_specs=pl.BlockSpec((tm, n_dim), lambda i, k: (i, 0)),
        out_shape=jax.ShapeDtypeStruct((m_dim, n_dim), F32),
        scratch_shapes=[pltpu.VMEM((tm, n_dim), F32)],
        compiler_params=_cparams(dimension_semantics=("parallel", "arbitrary")),
    )(*parts, b)


def _row_tile(t_pad, width):
    for tr in (528, 352, 176, 128, 64):
        if t_pad % tr == 0 and tr * width * 4 <= (3 << 19) and tr % 16 == 0:
            return tr
    return 64 if t_pad % 64 == 0 else t_pad


def _ln_res_fn(h, m, g, b):
    x = ALPHA * h + m
    mu = jnp.mean(x, axis=-1, keepdims=True)
    xc = x - mu
    var = jnp.mean(xc * xc, axis=-1, keepdims=True)
    return xc * lax.rsqrt(var + LN_EPS) * g + b


def _ln_res_bwd(h, m, g, b, dys, name):
    t_pad = h.shape[0]
    tr = _row_tile(t_pad, D)
    nd = len(dys)

    def body(h_ref, m_ref, g_ref, b_ref, *rest):
        d_refs, (dh_ref, dm_ref, dg_ref, db_ref) = rest[:nd], rest[nd:]
        _, vjp = jax.vjp(_ln_res_fn, h_ref[...], m_ref[...], g_ref[...], b_ref[...])
        dy = d_refs[0][...]
        for d_ref in d_refs[1:]:
            dy = dy + d_ref[...]
        dh, dm, dg, db = vjp(dy)
        dh_ref[...] = dh
        dm_ref[...] = dm.astype(BF16)

        @pl.when(pl.program_id(0) == 0)
        def _():
            dg_ref[...] = jnp.zeros_like(dg_ref)
            db_ref[...] = jnp.zeros_like(db_ref)

        dg_ref[...] += dg
        db_ref[...] += db

    row = pl.BlockSpec((tr, D), lambda i: (i, 0))
    par = pl.BlockSpec((1, D), lambda i: (0, 0))
    return pl.pallas_call(
        body, name=name, grid=(t_pad // tr,), in_specs=[row, row, par, par] + [row] * nd,
        out_specs=[row, row, par, par],
        out_shape=[jax.ShapeDtypeStruct((t_pad, D), F32), jax.ShapeDtypeStruct((t_pad, D), BF16),
                   jax.ShapeDtypeStruct((1, D), F32), jax.ShapeDtypeStruct((1, D), F32)],
        compiler_params=_cparams(),
    )(h, m, g, b, *dys)


def _grms_fn(o, z, g):
    y = o * lax.rsqrt(jnp.mean(o * o, axis=-1, keepdims=True) + RMS_EPS) * g
    return y * _silu(z)


def _grms_fwd(o, z_arr, z_blk0, g, name):
    t_pad, w = o.shape
    tr = _row_tile(t_pad, w)
    assert (z_blk0 * HD) % w == 0

    def body(o_ref, z_ref, g_ref, y_ref):
        for h in range(w // HD):
            c = slice(h * HD, (h + 1) * HD)
            y_ref[:, c] = _grms_fn(o_ref[:, c], z_ref[:, c], g_ref[...]).astype(BF16)

    return pl.pallas_call(
        body, name=name, grid=(t_pad // tr,),
        in_specs=[pl.BlockSpec((tr, w), lambda i: (i, 0)), pl.BlockSpec((tr, w), lambda i: (i, z_blk0 * HD // w)),
                  pl.BlockSpec((1, HD), lambda i: (0, 0))],
        out_specs=pl.BlockSpec((tr, w), lambda i: (i, 0)),
        out_shape=jax.ShapeDtypeStruct((t_pad, w), BF16), compiler_params=_cparams(),
    )(o, z_arr, g)


def _grms_bwd(o, z_arr, z_blk0, g, dy_arr, dy_blk0, name):
    t_pad, w = o.shape
    tr = _row_tile(t_pad, w)
    assert (z_blk0 * HD) % w == 0 and (dy_blk0 * HD) % w == 0

    def body(o_ref, z_ref, g_ref, dy_ref, do_ref, dz_ref, dg_ref):
        @pl.when(pl.program_id(0) == 0)
        def _():
            dg_ref[...] = jnp.zeros_like(dg_ref)

        for h in range(w // HD):
            c = slice(h * HD, (h + 1) * HD)
            _, vjp = jax.vjp(_grms_fn, o_ref[:, c], z_ref[:, c], g_ref[...])
            do, dz, dg = vjp(dy_ref[:, c])
            do_ref[:, c] = do
            dz_ref[:, c] = dz.astype(BF16)
            dg_ref[...] += dg

    blk = pl.BlockSpec((tr, w), lambda i: (i, 0))
    return pl.pallas_call(
        body, name=name, grid=(t_pad // tr,),
        in_specs=[blk, pl.BlockSpec((tr, w), lambda i: (i, z_blk0 * HD // w)), pl.BlockSpec((1, HD), lambda i: (0, 0)),
                  pl.BlockSpec((tr, w), lambda i: (i, dy_blk0 * HD // w))],
        out_specs=[blk, blk, pl.BlockSpec((1, HD), lambda i: (0, 0))],
        out_shape=[jax.ShapeDtypeStruct((t_pad, w), F32), jax.ShapeDtypeStruct((t_pad, w), BF16),
                   jax.ShapeDtypeStruct((1, HD), F32)],
        compiler_params=_cparams(),
    )(o, z_arr, g, dy_arr)


def _loss_fwd(y, tgt, first_row, name):
    t_pad = y.shape[0]
    tr = _row_tile(t_pad, D)

    def body(y_ref, t_ref, l_ref, dy_ref):
        rows = pl.program_id(0) * tr + _iota((tr, 1), 0)
        err = jnp.where(rows >= first_row, y_ref[...] - t_ref[...], 0.0)
        dy_ref[...] = err * (1.0 / D)

        @pl.when(pl.program_id(0) == 0)
        def _():
            l_ref[...] = jnp.zeros_like(l_ref)

        part = jnp.sum(jnp.sum(err * err, axis=1, keepdims=True), axis=0, keepdims=True)
        l_ref[...] += jnp.broadcast_to(part * (0.5 / D), l_ref.shape)

    row = pl.BlockSpec((tr, D), lambda i: (i, 0))
    return pl.pallas_call(
        body, name=name, grid=(t_pad // tr,), in_specs=[row, row],
        out_specs=[pl.BlockSpec((8, 128), lambda i: (0, 0)), row],
        out_shape=[jax.ShapeDtypeStruct((8, 128), F32), jax.ShapeDtypeStruct((t_pad, D), F32)],
        compiler_params=_cparams(),
    )(y, tgt)


def _assemble_bf16(parts, name):
    t_pad = parts[0].shape[0]
    widths = [p.shape[1] for p in parts]
    total = sum(widths)
    tr = _row_tile(t_pad, total)

    def body(*refs):
        o_ref = refs[-1]
        off = 0
        for ref, w in zip(refs[:-1], widths):
            o_ref[:, off:off + w] = ref[...].astype(BF16)
            off += w

    return pl.pallas_call(
        body, name=name, grid=(t_pad // tr,), in_specs=[pl.BlockSpec((tr, w), lambda i: (i, 0)) for w in widths],
        out_specs=pl.BlockSpec((tr, total), lambda i: (i, 0)),
        out_shape=jax.ShapeDtypeStruct((t_pad, total), BF16), compiler_params=_cparams(),
    )(*parts)


CONV_K = 4
HALO = 8
RT = 128


def _conv_fwd(p, blk0, w, mode, pad, name):
    t_pad = p.shape[0]
    nt = t_pad // RT
    scale = HD ** -0.5 if mode == "q" else 1.0

    def body(x_ref, w_ref, y_ref, xs_ref):
        xs_ref[0:HALO, :] = jnp.zeros((HALO, HD), F32)
        rows = _iota((t_pad, 1), 0)
        xs_ref[HALO:HALO + t_pad, :] = jnp.where(rows >= pad, x_ref[...], 0.0)
        wv = w_ref[...]

        def tile(i, carry):
            r0 = pl.multiple_of(i * RT, RT)
            ext = xs_ref[pl.ds(r0, RT + HALO), :]
            acc = ext[HALO:, :] * wv[3:4, :]
            for s in (1, 2, 3):
                acc = acc + pltpu.roll(ext, s, 0)[HALO:, :] * wv[3 - s:4 - s, :]
            y = _silu(acc)
            if mode != "v":
                y = y * lax.rsqrt(jnp.sum(y * y, axis=-1, keepdims=True) + L2_EPS) * scale
            y_ref[pl.ds(r0, RT), :] = y
            return carry

        lax.fori_loop(0, nt, tile, 0)

    return pl.pallas_call(
        body, name=name, grid=(GDN_H,),
        in_specs=[pl.BlockSpec((t_pad, HD), lambda h: (0, blk0 + h)), pl.BlockSpec((CONV_K, HD), lambda h: (0, h))],
        out_specs=pl.BlockSpec((t_pad, HD), lambda h: (0, h)),
        out_shape=jax.ShapeDtypeStruct((t_pad, GDN_H * HD), F32),
        scratch_shapes=[pltpu.VMEM((t_pad + HALO, HD), F32)],
        compiler_params=_cparams(),
    )(p, w)


def _conv_bwd(p, blk0, w, dn, mode, pad, name):
    t_pad = p.shape[0]
    nt = t_pad // RT
    scale = HD ** -0.5 if mode == "q" else 1.0

    def body(x_ref, w_ref, dn_ref, dx_ref, dw_ref, xs_ref, ds_ref):
        xs_ref[0:HALO, :] = jnp.zeros((HALO, HD), F32)
        xs_ref[HALO + t_pad:HALO + t_pad + 2 * HALO, :] = jnp.zeros((2 * HALO, HD), F32)
        ds_ref[t_pad:t_pad + HALO, :] = jnp.zeros((HALO, HD), F32)
        rows = _iota((t_pad, 1), 0)
        xs_ref[HALO:HALO + t_pad, :] = jnp.where(rows >= pad, x_ref[...], 0.0)
        ds_ref[0:t_pad, :] = dn_ref[...]
        wv = w_ref[...]

        def tile(i, dw):
            r0 = pl.multiple_of(i * RT, RT)
            ext = xs_ref[pl.ds(r0, RT + 2 * HALO), :]
            dn_e = ds_ref[pl.ds(r0, RT + HALO), :]
            xsh = [ext[HALO:, :]] + [pltpu.roll(ext, s, 0)[HALO:, :] for s in (1, 2, 3)]
            pre = xsh[0] * wv[3:4, :]
            for s in (1, 2, 3):
                pre = pre + xsh[s] * wv[3 - s:4 - s, :]
            sg = _sigmoid(pre)
            y = pre * sg
            if mode != "v":
                ss = jnp.sum(y * y, axis=-1, keepdims=True) + L2_EPS
                r = lax.rsqrt(ss)
                dy = scale * (dn_e * r - y * (r * r * r) * jnp.sum(dn_e * y, axis=-1, keepdims=True))
            else:
                dy = dn_e
            dpre = dy * (sg * (1.0 + pre * (1.0 - sg)))
            dx = dpre[:RT, :] * wv[3:4, :]
            for s in (1, 2, 3):
                dx = dx + pltpu.roll(dpre, RT + HALO - s, 0)[:RT, :] * wv[3 - s:4 - s, :]
            trow = r0 + _iota((RT, 1), 0)
            dx_ref[pl.ds(r0, RT), :] = jnp.where(trow >= pad, dx, 0.0)
            new = []
            for s in (0, 1, 2, 3):
                new.append(dw[s] + jnp.sum(dpre[:RT, :] * xsh[s][:RT, :], axis=0, keepdims=True))
            return tuple(new)

        z = jnp.zeros((1, HD), F32)
        dw = lax.fori_loop(0, nt, tile, (z, z, z, z))
        for s in (0, 1, 2, 3):
            dw_ref[3 - s:4 - s, :] = dw[s]

    return pl.pallas_call(
        body, name=name, grid=(GDN_H,),
        in_specs=[pl.BlockSpec((t_pad, HD), lambda h: (0, blk0 + h)), pl.BlockSpec((CONV_K, HD), lambda h: (0, h)),
                  pl.BlockSpec((t_pad, HD), lambda h: (0, h))],
        out_specs=[pl.BlockSpec((t_pad, HD), lambda h: (0, h)), pl.BlockSpec((CONV_K, HD), lambda h: (0, h))],
        out_shape=[jax.ShapeDtypeStruct((t_pad, GDN_H * HD), F32), jax.ShapeDtypeStruct((CONV_K, GDN_H * HD), F32)],
        scratch_shapes=[pltpu.VMEM((t_pad + 3 * HALO, HD), F32), pltpu.VMEM((t_pad + HALO, HD), F32)],
        compiler_params=_cparams(),
    )(p, w, dn)


@jax.custom_vjp
def _unit_lower_inv(m, bd, eye):
    md = m * bd
    low = m - md
    p2 = mbnn(md, md)
    p4 = mbnn(p2, p2)
    dinv = mbnn(mbnn(eye - md, eye + p2), eye + p4)
    n = mbnn(dinv, low)
    n2 = mbnn(n, n)
    n4 = mbnn(n2, n2)
    return mbnn(mbnn(mbnn(eye - n, eye + n2), eye + n4), dinv)


def _unit_lower_inv_bwd(res, g):
    t, bd, eye = res
    return -mbtn(t, mbnt(g, t)), jnp.zeros_like(bd), jnp.zeros_like(eye)


def _unit_lower_inv_fwd(m, bd, eye):
    t = _unit_lower_inv(m, bd, eye)
    return t, (t, bd, eye)


_unit_lower_inv.defvjp(_unit_lower_inv_fwd, _unit_lower_inv_bwd)


def _gdn_chunks(chunks, alog, dtb, s):
    nh = chunks[0][0].shape[0]
    ri = _iota((1, CH, CH), 1)
    ci = _iota((1, CH, CH), 2)
    causal = ri >= ci
    strict = ri > ci
    eye = (ri == ci).astype(F32)
    bd = ((ri >> 3) == (ci >> 3)).astype(F32)
    ltri = (_iota((CH, CH), 0) >= _iota((CH, CH), 1)).astype(F32)
    sel = (_iota((nh, 1, HD), 2) == _iota((nh, 1, HD), 0)).astype(F32)
    last = _iota((1, CH, 1), 1) == CH - 1

    beta, gc, gc_rows = [], [], []
    for _, _, _, bb, aa, valid in chunks:
        beta_all = jnp.where(valid, _sigmoid(bb), 0.0)
        g_all = jnp.where(valid, -jnp.exp(alog) * _softplus(aa + dtb), 0.0)
        gc_all = hnn(ltri, g_all)
        beta.append(jnp.sum(beta_all[None] * sel, axis=2, keepdims=True))
        gc.append(jnp.sum(gc_all[None] * sel, axis=2, keepdims=True))
        gc_rows.append(hbnt(jnp.broadcast_to(sel, (nh, CH, HD)), jnp.broadcast_to(gc_all[None], (nh, CH, HD))))
    cat = lambda xs: jnp.concatenate(xs, axis=0)
    q, k, v = (cat([c[j] for c in chunks]) for j in range(3))
    beta, gc, gc_rows = cat(beta), cat(gc), cat(gc_rows)
    gc_last = jnp.sum(jnp.where(last, gc, 0.0), axis=1, keepdims=True)
    decay = jnp.exp(jnp.where(causal, gc - gc_rows, NEG))
    egc = jnp.exp(gc)

    kb = k * beta
    m = jnp.where(strict, bbnt(kb, k) * decay, 0.0)
    t_inv = _unit_lower_inv(m, bd, eye)
    u = bbnn(t_inv, v * beta)
    w = bbnn(t_inv, kb * egc)
    a_intra = bbnt(q, k) * decay
    q_dec = q * egc
    k_dec = k * jnp.exp(gc_last - gc)
    g_tot = jnp.exp(gc_last)

    outs = []
    for n in range(len(chunks)):
        part = lambda a: a[n * nh:(n + 1) * nh]
        v_new = part(u) - bbnn(part(w), s)
        outs.append(bbnn(part(q_dec), s) + bbnn(part(a_intra), v_new))
        s = s * part(g_tot) + bbtn(part(k_dec), v_new)
    return outs, s


PAIR = 2 * CH


def _gdn_specs(npair, rev):
    cc = (lambda c: npair - 1 - c) if rev else (lambda c: c)
    wide = pl.BlockSpec((PAIR, GDN_H * HD), lambda c: (cc(c), 0))
    fix = lambda off: pl.BlockSpec((PAIR, HD), lambda c: (cc(c), off))
    par = pl.BlockSpec((1, HD), lambda c: (0, 0))
    state = pl.BlockSpec((1, GDN_H, HD, HD), lambda c: (cc(c), 0, 0, 0))
    return wide, fix, par, state


def _store_heads(ref, a, rows=slice(None)):
    for h in range(a.shape[0]):
        ref[rows, h * HD:(h + 1) * HD] = a[h]


def _chunk_rows(half):
    return slice(half * CH, (half + 1) * CH)


def _chunk_valid(pair, half, pad):
    return ((2 * pair + half) * CH + _iota((CH, 1), 0)) >= pad


def _gdn_fwd(qn, kn, vn, p, alog, dtb, pad, name, cargo=(), exchange=None):
    t_pad = qn.shape[0]
    npair = t_pad // PAIR
    wide, fix, par, state = _gdn_specs(npair, False)
    n = len(cargo)

    def body(q_ref, k_ref, v_ref, bb_ref, aa_ref, al_ref, dt_ref, *rest):
        c = pl.program_id(0)
        s_ref = rest[-1]
        (o_ref, ss_ref), end_cargo = _cargo_bounds(rest[:-1], n, 2, exchange, c == 0, c == npair - 1)

        @pl.when(c == 0)
        def _():
            s_ref[...] = jnp.zeros_like(s_ref)

        s = s_ref[...]
        ss_ref[0] = s
        rows = [_chunk_rows(half) for half in (0, 1)]
        chunks = [(_heads(q_ref[r, :], GDN_H), _heads(k_ref[r, :], GDN_H), _heads(v_ref[r, :], GDN_H),
                   bb_ref[r, :], aa_ref[r, :], _chunk_valid(c, half, pad)) for half, r in enumerate(rows)]
        outs, s = _gdn_chunks(chunks, al_ref[...], dt_ref[...], s)
        for r, o in zip(rows, outs):
            _store_heads(o_ref, o, r)
        s_ref[...] = s
        end_cargo()

    return pl.pallas_call(
        body, name=name, grid=(npair,),
        in_specs=[wide, wide, wide, fix(16), fix(17), par, par] + [ANY] * n,
        out_specs=[wide, state] + [ANY] * n,
        out_shape=[jax.ShapeDtypeStruct((t_pad, GDN_H * HD), F32), jax.ShapeDtypeStruct((npair, GDN_H, HD, HD), F32)]
        + (exchange[1](cargo) if n else []),
        scratch_shapes=(exchange[2](n) if n else []) + [pltpu.VMEM((GDN_H, HD, HD), F32)],
        compiler_params=_cparams(),
    )(qn, kn, vn, p, p, alog, dtb, *cargo)


def _gdn_bwd(qn, kn, vn, p, alog, dtb, ssave, do, pad, name, cargo=(), exchange=None):
    t_pad = qn.shape[0]
    npair = t_pad // PAIR
    wide, fix, par, state = _gdn_specs(npair, True)
    n = len(cargo)

    def body(q_ref, k_ref, v_ref, bb_ref, aa_ref, al_ref, dt_ref, ss_ref, do_ref, *rest):
        c = pl.program_id(0)
        ds_ref = rest[-1]
        (dq_ref, dk_ref, dv_ref, dbb_ref, daa_ref, dal_ref, ddt_ref), end_cargo = _cargo_bounds(
            rest[:-1], n, 7, exchange, c == 0, c == npair - 1)

        @pl.when(c == 0)
        def _():
            ds_ref[...] = jnp.zeros_like(ds_ref)
            dal_ref[...] = jnp.zeros_like(dal_ref)
            ddt_ref[...] = jnp.zeros_like(ddt_ref)

        ra, rb = _chunk_rows(0), _chunk_rows(1)
        va, vb = _chunk_valid(npair - 1 - c, 0, pad), _chunk_valid(npair - 1 - c, 1, pad)

        def pair(qa, ka, va_, ba, aa, qb, kb, vb_, bb, ab, al, dt, s):
            (oa, ob), s = _gdn_chunks([(qa, ka, va_, ba, aa, va), (qb, kb, vb_, bb, ab, vb)], al, dt, s)
            return oa, ob, s

        ins = [f(ref[r, :]) for r in (ra, rb)
               for ref, f in ((q_ref, lambda a: _heads(a, GDN_H)), (k_ref, lambda a: _heads(a, GDN_H)),
                              (v_ref, lambda a: _heads(a, GDN_H)), (bb_ref, lambda a: a), (aa_ref, lambda a: a))]
        _, vjp = jax.vjp(pair, *ins, al_ref[...], dt_ref[...], ss_ref[0])
        g = vjp((_heads(do_ref[ra, :], GDN_H), _heads(do_ref[rb, :], GDN_H), ds_ref[...]))
        for r, (dq, dk, dv, dbb, daa) in ((ra, g[0:5]), (rb, g[5:10])):
            _store_heads(dq_ref, dq, r)
            _store_heads(dk_ref, dk, r)
            _store_heads(dv_ref, dv, r)
            dbb_ref[r, :] = dbb
            daa_ref[r, :] = daa
        dal_ref[...] += g[10]
        ddt_ref[...] += g[11]
        ds_ref[...] = g[12]
        end_cargo()

    sds = jax.ShapeDtypeStruct
    return pl.pallas_call(
        body, name=name, grid=(npair,),
        in_specs=[wide, wide, wide, fix(16), fix(17), par, par, state, wide] + [ANY] * n,
        out_specs=[wide, wide, wide, fix(0), fix(0), par, par] + [ANY] * n,
        out_shape=[sds((t_pad, GDN_H * HD), F32)] * 3 + [sds((t_pad, HD), F32)] * 2 + [sds((1, HD), F32)] * 2
        + (exchange[1](cargo) if n else []),
        scratch_shapes=(exchange[2](n) if n else []) + [pltpu.VMEM((GDN_H, HD, HD), F32)],
        compiler_params=_cparams(),
    )(qn, kn, vn, p, p, alog, dtb, ssave, do, *cargo)


SB_Q0, SB_K0, SB_V0 = 18, 22, 26
SB_SCALE = SB_DH ** -0.5
SB_NB = 8


def _sb_terms(z, allowed):
    nz = -z
    raw = jnp.minimum(nz, 0.0) - jnp.log(1.0 + jnp.exp(jnp.minimum(z, nz)))
    l1m = raw if allowed is None else jnp.where(allowed, raw, 0.0)
    ls = z + raw
    return l1m, ls, jnp.exp(ls)


def _sb_passes(i, step, carry):
    total = i + 1
    sized = lambda done: [functools.partial(step, done, masked=True, nb=nb) for nb in range(1, SB_NB + 1)]

    def several(c):
        n_mid = (total - SB_NB - 1) // SB_NB
        c = step(0, c, masked=True, nb=SB_NB)
        c = lax.fori_loop(0, n_mid, lambda t, cc: step(SB_NB * (1 + t), cc, masked=False, nb=SB_NB), c)
        done = SB_NB * (1 + n_mid)
        return lax.switch(total - done - 1, sized(done), c)

    return lax.cond(total <= SB_NB, lambda c: lax.switch(total - 1, sized(0), c), several, carry)


def _sb_stack(a, i):
    first = _iota((1, HD), 1) < SB_DH
    a2 = jnp.concatenate([jnp.where(first, a, 0.0), jnp.where(first, 0.0, a)], axis=0).astype(BF16)
    rq = i * QB + _iota((QB, 1), 0)
    return a2, jnp.concatenate([rq, rq], axis=0), first


def _hi_lo(a):
    hi = a.astype(BF16)
    lo = (a - hi.astype(F32)).astype(BF16)
    return jnp.concatenate([hi, lo], axis=1)


def _cargo_bounds(refs, n, n_out, exchange, first, last):
    outs = refs[n:n + n_out]
    if not n:
        return outs, lambda: None
    ex = exchange[0](refs[:n], refs[n + n_out:2 * n + n_out], *refs[2 * n + n_out:])

    @pl.when(first)
    def _():
        ex.start()

    def finish():
        @pl.when(last)
        def _():
            ex.wait()

    return outs, finish


def _sb_fwd(p, pad, name, cargo=(), exchange=None):
    t_pad = p.shape[0]
    nq = t_pad // QB
    n = len(cargo)

    def body(q_ref, k_ref, v_ref, *rest):
        i = pl.program_id(1)
        pr = pl.program_id(0)
        (o_ref, r_ref), end_cargo = _cargo_bounds(rest, n, 2, exchange, (pr == 0) & (i == 0),
                                                  (pr == SB_H // 2 - 1) & (i == nq - 1))
        q2, rowq, first = _sb_stack(q_ref[...] * SB_SCALE, i)
        tri = (_iota((QB, QB), 0) > _iota((QB, QB), 1)).astype(BF16)
        upper2 = jnp.concatenate([jnp.concatenate([tri, tri], axis=0), jnp.ones((2 * QB, QB), BF16)], axis=1)

        def chain(kb, masked):
            start = pl.multiple_of(kb * QB, QB)
            kblk = k_ref[pl.ds(start, QB), :].astype(BF16)
            vblk = v_ref[pl.ds(start, QB), :].astype(BF16)
            z = lax.dot_general(q2, kblk, (NT, ((), ())), preferred_element_type=F32)
            colk = kb * QB + _iota((1, QB), 1)
            al = ((colk < rowq) & (colk >= pad)) if masked else None
            l1m, ls, _ = _sb_terms(z, al)
            sums = lax.dot_general(_hi_lo(l1m), upper2, (NN, ((), ())), preferred_element_type=F32)
            return al, ls, sums[:, :QB], sums[:, QB:], vblk

        def step(done, carry, masked, nb):
            o_acc, run = carry
            ws, vs = [], []
            for n in range(nb):
                al, ls, suf, rs, vblk = chain(i - done - n, masked)
                wgt = jnp.exp(ls + suf + run)
                ws.append((wgt if al is None else jnp.where(al, wgt, 0.0)).astype(BF16))
                vs.append(vblk)
                run = run + rs
            o_acc = o_acc + lax.dot_general(jnp.concatenate(ws, axis=1), jnp.concatenate(vs, axis=0),
                                            (NN, ((), ())), preferred_element_type=F32)
            return o_acc, run

        o_acc, run = _sb_passes(i, step, (jnp.zeros((2 * QB, HD), F32), jnp.zeros((2 * QB, QB), F32)))
        o_ref[...] = jnp.where(first, o_acc[:QB], o_acc[QB:]).astype(BF16)
        r_ref[...] = jnp.where(first, run[:QB], run[QB:])
        end_cargo()

    full = lambda off: pl.BlockSpec((t_pad, HD), lambda pr, i: (0, off + pr))
    blk = pl.BlockSpec((QB, HD), lambda pr, i: (i, pr))
    return pl.pallas_call(
        body, name=name, grid=(SB_H // 2, nq),
        in_specs=[pl.BlockSpec((QB, HD), lambda pr, i: (i, SB_Q0 + pr)), full(SB_K0), full(SB_V0)] + [ANY] * n,
        out_specs=[blk, blk] + [ANY] * n,
        out_shape=[jax.ShapeDtypeStruct((t_pad, SB_H * SB_DH), BF16), jax.ShapeDtypeStruct((t_pad, SB_H * SB_DH), F32)]
        + (exchange[1](cargo) if n else []),
        scratch_shapes=exchange[2](n) if n else [],
        compiler_params=_cparams(),
    )(p, p, p, *cargo)


def _sb_bwd(p, rtot, dy, dy_blk0, pad, name, cargo=(), exchange=None):
    t_pad = p.shape[0]
    nq = t_pad // QB
    n = len(cargo)

    def body(q_ref, k_ref, v_ref, r_ref, do_ref, *rest):
        i = pl.program_id(1)
        pr = pl.program_id(0)
        dkt_ref, dvt_ref = rest[-2:]
        (dq_ref, dk_ref, dv_ref), end_cargo = _cargo_bounds(rest[:-2], n, 3, exchange, (pr == 0) & (i == 0),
                                                            (pr == SB_H // 2 - 1) & (i == nq - 1))

        @pl.when(i == 0)
        def _():
            dkt_ref[...] = jnp.zeros_like(dkt_ref)
            dvt_ref[...] = jnp.zeros_like(dvt_ref)

        q2, rowq, first = _sb_stack(q_ref[...] * SB_SCALE, i)
        do2, _, _ = _sb_stack(do_ref[...], i)
        q2t = jnp.transpose(q2.astype(F32)).astype(BF16)
        do2t = jnp.transpose(do2.astype(F32)).astype(BF16)
        rt = r_ref[...]
        lane = _iota((1, HD), 1)
        rcol = jnp.concatenate([jnp.sum(jnp.where(lane == 0, rt, 0.0), axis=1, keepdims=True),
                                jnp.sum(jnp.where(lane == SB_DH, rt, 0.0), axis=1, keepdims=True)], axis=0)
        rj = _iota((QB, QB), 0)
        cs = _iota((QB, QB), 1)
        tri_u = (rj > cs).astype(BF16)
        tri_l = (rj < cs).astype(BF16)
        ones2 = jnp.ones((2 * QB, QB), BF16)
        upper2 = jnp.concatenate([jnp.concatenate([tri_u, tri_u], axis=0), ones2], axis=1)
        lower2 = jnp.concatenate([jnp.concatenate([tri_l, tri_l], axis=0), ones2], axis=1)
        rcol = jnp.broadcast_to(rcol, (2 * QB, QB))

        def chain(kb, masked):
            start = pl.multiple_of(kb * QB, QB)
            kblk = k_ref[pl.ds(start, QB), :].astype(BF16)
            vblk = v_ref[pl.ds(start, QB), :].astype(BF16)
            z = lax.dot_general(q2, kblk, (NT, ((), ())), preferred_element_type=F32)
            colk = kb * QB + _iota((1, QB), 1)
            al = ((colk < rowq) & (colk >= pad)) if masked else None
            l1m, ls, sg = _sb_terms(z, al)
            dwgt = lax.dot_general(do2, vblk, (NT, ((), ())), preferred_element_type=F32)
            sums = lax.dot_general(_hi_lo(l1m), upper2, (NN, ((), ())), preferred_element_type=F32)
            return kb, kblk, al, ls, sums[:, :QB], sums[:, QB:], dwgt, sg

        def finish(c, left, gseen):
            kb, kblk, al, ls, suf, rs, dwgt, sg = c
            left = left - rs
            wgt = jnp.exp(ls + suf + left)
            if al is not None:
                wgt = jnp.where(al, wgt, 0.0)
            dl = dwgt * wgt
            sums = lax.dot_general(_hi_lo(dl), lower2, (NN, ((), ())), preferred_element_type=F32)
            gpre = gseen + sums[:, :QB]
            dz = dl - sg * (dl + gpre)
            if al is not None:
                dz = jnp.where(al, dz, 0.0)
            dz = dz.astype(BF16)
            dkt_ref[kb] += lax.dot_general(q2t, dz, (NN, ((), ())), preferred_element_type=F32)
            dvt_ref[kb] += lax.dot_general(do2t, wgt.astype(BF16), (NN, ((), ())), preferred_element_type=F32)
            return dz, left, gseen + sums[:, QB:]

        def step(done, carry, masked, nb):
            dq_acc, left, gseen = carry
            cs_ = [chain(done + n, masked) for n in range(nb)]
            dzs = []
            for c in cs_:
                dz, left, gseen = finish(c, left, gseen)
                dzs.append(dz)
            dq_acc = dq_acc + lax.dot_general(jnp.concatenate(dzs, axis=1), jnp.concatenate([c[1] for c in cs_], axis=0),
                                              (NN, ((), ())), preferred_element_type=F32)
            return dq_acc, left, gseen

        dq_acc, _, _ = _sb_passes(i, step, (jnp.zeros((2 * QB, HD), F32), rcol, jnp.zeros((2 * QB, QB), F32)))
        dq_ref[...] = jnp.where(first, dq_acc[:QB], dq_acc[QB:]) * SB_SCALE

        @pl.when(i == nq - 1)
        def _():
            for kb in range(nq):
                dk_ref[kb * QB:(kb + 1) * QB, :] = jnp.transpose(dkt_ref[kb])
                dv_ref[kb * QB:(kb + 1) * QB, :] = jnp.transpose(dvt_ref[kb])

        end_cargo()

    full_in = lambda off: pl.BlockSpec((t_pad, HD), lambda pr, i: (0, off + pr))
    full_out = pl.BlockSpec((t_pad, HD), lambda pr, i: (0, pr))
    blk = pl.BlockSpec((QB, HD), lambda pr, i: (i, pr))
    sds = jax.ShapeDtypeStruct((t_pad, SB_H * SB_DH), F32)
    return pl.pallas_call(
        body, name=name, grid=(SB_H // 2, nq),
        in_specs=[pl.BlockSpec((QB, HD), lambda pr, i: (i, SB_Q0 + pr)), full_in(SB_K0), full_in(SB_V0), blk,
                  pl.BlockSpec((QB, HD), lambda pr, i: (i, dy_blk0 + pr))] + [ANY] * n,
        out_specs=[blk, full_out, full_out] + [ANY] * n,
        out_shape=[sds, sds, sds] + (exchange[1](cargo) if n else []),
        scratch_shapes=(exchange[2](n) if n else []) + [pltpu.VMEM((nq, HD, QB), F32)] * 2,
        compiler_params=_cparams(),
    )(p, p, p, rtot, dy, *cargo)


HG_LEVELS = 6


def _hg_prefix_matrix():
    t = np.arange(CH)[:, None]
    j = np.arange(CH)[None, :]
    groups = [(j <= t)]
    for lvl in range(1, HG_LEVELS + 1):
        half = CH >> lvl
        e = (t // (2 * half)) * (2 * half) + half - 1
        groups.append(j <= e)
    groups.append(np.ones((8, CH), bool))
    e = np.concatenate(groups, axis=0).astype(np.float32)
    return np.concatenate([e, e, e], axis=1), np.concatenate([e, e, e], axis=0)


HG_G = 4


def _hg_chunk(qr, fr, iv, r0, r1, st, valid, ecat):
    g = st.shape[0]
    mx = jnp.maximum(r0, r1)
    e0 = jnp.exp(r0 - mx)
    e1 = jnp.exp(r1 - mx)
    lb = e1 / (e0 + e1)
    fg = lb + (1.0 - lb) * _sigmoid(fr)
    logf = jnp.where(valid, jnp.log(fg), 0.0)
    kk = jnp.where(valid, 1.0 - fg, 0.0)
    q = jnp.where(valid, _silu(qr), 0.0)
    v = _heads(jnp.where(valid, iv, 0.0), g)

    pre = _mask_dot(ecat, logf)
    b = pre[0:CH]
    b_last = jnp.max(pre[(HG_LEVELS + 1) * CH:], axis=0, keepdims=True)
    row = _iota((CH, 1), 0)
    ri = _iota((1, CH, CH), 1)
    ci = _iota((1, CH, CH), 2)
    a = jnp.where(ri == ci, jnp.sum(_heads(q * kk, g), axis=2, keepdims=True), 0.0)
    for lvl in range(1, HG_LEVELS + 1):
        half = CH >> lvl
        m = pre[lvl * CH:(lvl + 1) * CH]
        low = (row & half) != 0
        dec = jnp.exp(jnp.where(low, b - m, m - b))
        qt = jnp.where(low, q * dec, 0.0)
        kt = jnp.where(low, 0.0, kk * dec)
        same = (ri >> (7 - lvl)) == (ci >> (7 - lvl))
        a = a + jnp.where(same, bbnt(_heads(qt, g), _heads(kt, g)), 0.0)
    o = bbnt(_heads(q * jnp.exp(b), g), st) + bbnn(a, v)
    kd = kk * jnp.exp(b_last - b)
    st_new = st * _heads(jnp.exp(b_last), g) + bbtn(v, _heads(kd, g))
    return o, st_new


def _hg_specs(npair, rev):
    cc = (lambda c: npair - 1 - c) if rev else (lambda c: c)
    ng = HG_H // HG_G
    blk = lambda off: pl.BlockSpec((PAIR, HG_G * HD), lambda h, c: (cc(c), off * ng + h))
    lbs = pl.BlockSpec((2, HG_G * HD), lambda h, c: (0, h))
    state = pl.BlockSpec((1, HG_G, HD, HD), lambda h, c: (cc(c), h, 0, 0))
    return ng, blk, lbs, state


def _hg_fwd(p, lbraw, ecat, pad, name):
    t_pad = p.shape[0]
    npair = t_pad // PAIR
    ng, blk, lbs, state = _hg_specs(npair, False)

    def body(q_ref, f_ref, i_ref, lb_ref, e_ref, et_ref, o_ref, ss_ref, s_ref):
        c = pl.program_id(1)

        @pl.when(c == 0)
        def _():
            s_ref[...] = jnp.zeros_like(s_ref)

        st = s_ref[...]
        ss_ref[0] = st
        for half in (0, 1):
            r = _chunk_rows(half)
            o, st = _hg_chunk(q_ref[r, :], f_ref[r, :], i_ref[r, :], lb_ref[0:1, :], lb_ref[1:2, :], st,
                              _chunk_valid(c, half, pad), (e_ref[...], et_ref[...]))
            _store_heads(o_ref, o, r)
        s_ref[...] = st

    return pl.pallas_call(
        body, name=name, grid=(ng, npair),
        in_specs=[blk(0), blk(1), blk(2), lbs] + [pl.BlockSpec(e.shape, lambda h, c: (0, 0)) for e in ecat],
        out_specs=[blk(0), state],
        out_shape=[jax.ShapeDtypeStruct((t_pad, HG_H * HD), F32), jax.ShapeDtypeStruct((npair, HG_H, HD, HD), F32)],
        scratch_shapes=[pltpu.VMEM((HG_G, HD, HD), F32)],
        compiler_params=_cparams(),
    )(p, p, p, lbraw, *ecat)


def _hg_bwd(p, lbraw, ecat, ssave, do, pad, name, cargo=(), exchange=None):
    t_pad = p.shape[0]
    npair = t_pad // PAIR
    ng, blk, lbs, state = _hg_specs(npair, True)
    n = len(cargo)

    def body(q_ref, f_ref, i_ref, lb_ref, e_ref, et_ref, ss_ref, do_ref, *rest):
        c = pl.program_id(1)
        hg = pl.program_id(0)
        ds_ref = rest[-1]
        (dq_ref, df_ref, di_ref, dlb_ref), end_cargo = _cargo_bounds(
            rest[:-1], n, 4, exchange, (hg == 0) & (c == 0), (hg == ng - 1) & (c == npair - 1))

        @pl.when(c == 0)
        def _():
            ds_ref[...] = jnp.zeros_like(ds_ref)
            dlb_ref[...] = jnp.zeros_like(dlb_ref)

        ra, rb = _chunk_rows(0), _chunk_rows(1)
        va, vb = _chunk_valid(npair - 1 - c, 0, pad), _chunk_valid(npair - 1 - c, 1, pad)
        ecv = (e_ref[...], et_ref[...])

        def pair(qa, fa, ia, qb, fb, ib, r0, r1, st):
            oa, st = _hg_chunk(qa, fa, ia, r0, r1, st, va, ecv)
            ob, st = _hg_chunk(qb, fb, ib, r0, r1, st, vb, ecv)
            return oa, ob, st

        ins = [ref[r, :] for r in (ra, rb) for ref in (q_ref, f_ref, i_ref)]
        _, vjp = jax.vjp(pair, *ins, lb_ref[0:1, :], lb_ref[1:2, :], ss_ref[0])
        g = vjp((_heads(do_ref[ra, :], HG_G), _heads(do_ref[rb, :], HG_G), ds_ref[...]))
        for r, (dq, df, di) in ((ra, g[0:3]), (rb, g[3:6])):
            dq_ref[r, :] = dq.astype(BF16)
            df_ref[r, :] = df.astype(BF16)
            di_ref[r, :] = di.astype(BF16)
        dlb_ref[0:1, :] += g[6]
        dlb_ref[1:2, :] += g[7]
        ds_ref[...] = g[8]
        end_cargo()

    sds = jax.ShapeDtypeStruct((t_pad, HG_H * HD), BF16)
    return pl.pallas_call(
        body, name=name, grid=(ng, npair),
        in_specs=[blk(0), blk(1), blk(2), lbs] + [pl.BlockSpec(e.shape, lambda h, c: (0, 0)) for e in ecat]
        + [state, blk(0)] + [ANY] * n,
        out_specs=[blk(0), blk(0), blk(0), lbs] + [ANY] * n,
        out_shape=[sds, sds, sds, jax.ShapeDtypeStruct((2, HG_H * HD), F32)] + (exchange[1](cargo) if n else []),
        scratch_shapes=(exchange[2](n) if n else []) + [pltpu.VMEM((HG_G, HD, HD), F32)],
        compiler_params=_cparams(),
    )(p, p, p, lbraw, *ecat, ssave, do, *cargo)


def _pad_ab_cols(w):
    z = jnp.zeros((w.shape[0], HD - GDN_H), w.dtype)
    return jnp.concatenate([w[:, :2048], w[:, 2048:2052], z, w[:, 2052:2056], z, w[:, 2056:]], axis=1)


def _unpad_ab_cols(w):
    return jnp.concatenate([w[:, :2048], w[:, 2048:2052], w[:, 2176:2180], w[:, 2304:]], axis=1)


def _lane_pad(v):
    return jnp.pad(v, ((0, 0), (0, HD - v.shape[1])))


def _mlp_fwd(h, hb, w1, w2, layer, g, b):
    a, r = _mm(hb, w1, b_view=("cols", layer), out_dtype=BF16, act=True, name=f"mlp_up_{layer}")
    m, y, yb = _mm(r, w2, b_view=("rows", layer), ln=(h, g, b), name=f"mlp_down_{layer}")
    return a, r, m, y, yb


def _mlp_bwd(hb, a, r, dmb, w1, w2, layer):
    da = _mm(dmb, w2, tb=True, b_view=("rows", layer), out_dtype=BF16, gate=a, name=f"mlp_down_dx_{layer}")
    dw2 = _mm(r, dmb, ta=True, out_dtype=BF16, name=f"mlp_down_dw_{layer}")
    dh = _mm(da, w1, tb=True, b_view=("cols", layer), name=f"mlp_up_dx_{layer}")
    dw1 = _mm(hb, da, ta=True, out_dtype=BF16, out_split=N_CHIP, name=f"mlp_up_dw_{layer}")
    return dh, dw1, dw2


def _local_step(h0, tgt, w, pad, late=None):
    row = lambda a, i: a[i:i + 1]
    ecat = tuple(jnp.asarray(e, dtype=BF16) for e in _hg_prefix_matrix())
    cw = [w["conv_w"][:, i * 512:(i + 1) * 512] for i in range(3)]
    alog, dtb = _lane_pad(w["a_log"]), _lane_pad(w["dt_bias"])

    h0b = h0.astype(BF16)
    p0 = _mm(h0b, w["ab_w_in"], name="ab_in")
    qn = _conv_fwd(p0, 0, cw[0], "q", pad, "conv_q")
    kn = _conv_fwd(p0, 4, cw[1], "k", pad, "conv_k")
    vn = _conv_fwd(p0, 8, cw[2], "v", pad, "conv_v")
    if late is None:
        oa_raw, ss0 = _gdn_fwd(qn, kn, vn, p0, alog, dtb, pad, "gdn_fwd")
        ob, rtot = _sb_fwd(p0, pad, "sb_fwd")
    else:
        oa_raw, ss0, g_cin, g_cout = _gdn_fwd(qn, kn, vn, p0, alog, dtb, pad, "gdn_fwd",
                                              cargo=[late["c_w_in"], late["c_w_out"]], exchange=GATHER)
        ob, rtot, g_about, g_w1, g_w2 = _sb_fwd(p0, pad, "sb_fwd", exchange=GATHER,
                                                cargo=[late["ab_w_out"], late["mlp_w1"], late["mlp_w2"]])
        w = dict(w, ab_w_out=g_about.reshape(D, D), c_w_in=g_cin, c_w_out=g_cout.reshape(D, D), mlp_w1=g_w1, mlp_w2=g_w2)
    oa = _grms_fwd(oa_raw, p0, 12, w["ab_gnorm_g"], "gdn_gate")
    ycat = jnp.concatenate([oa, ob], axis=1)
    mix0, h1, h1b = _mm(ycat, w["ab_w_out"], name="ab_out", ln=(h0, row(w["ln_mix_g"], 0), row(w["ln_mix_b"], 0)))
    a0, r0, m0, h2, h2b = _mlp_fwd(h1, h1b, w["mlp_w1"], w["mlp_w2"], 0, row(w["ln_ffn_g"], 0), row(w["ln_ffn_b"], 0))
    p1 = _mm(h2b, w["c_w_in"], b_view=("cols", 0), name="c_in")
    oc_raw, ss1 = _hg_fwd(p1, w["c_lb_raw"], ecat, pad, "hg_fwd")
    yc = _grms_fwd(oc_raw, p1, 3 * HG_H, w["c_gnorm_g"], "hg_gate")
    mix1, h3, h3b = _mm(yc, w["c_w_out"], name="c_out", ln=(h2, row(w["ln_mix_g"], 1), row(w["ln_mix_b"], 1)))
    a1, r1, m1, h4, _ = _mlp_fwd(h3, h3b, w["mlp_w1"], w["mlp_w2"], 1, row(w["ln_ffn_g"], 1), row(w["ln_ffn_b"], 1))
    loss, dh4 = _loss_fwd(h4, tgt, pad + N_META, "loss")

    dh3a, dm1b, dfg1, dfb1 = _ln_res_bwd(h3, m1, row(w["ln_ffn_g"], 1), row(w["ln_ffn_b"], 1), [dh4], "ln_ffn_bwd_1")
    dh3b, dw1_1, dw2_1 = _mlp_bwd(h3b, a1, r1, dm1b, w["mlp_w1"], w["mlp_w2"], 1)
    dh2a, dmix1b, dmg1, dmb1 = _ln_res_bwd(h2, mix1, row(w["ln_mix_g"], 1), row(w["ln_mix_b"], 1), [dh3a, dh3b], "ln_mix_bwd_1")
    dyc = _mm(dmix1b, w["c_w_out"], tb=True, name="c_out_dx")
    dwco = _mm(yc, dmix1b, ta=True, out_dtype=BF16, name="c_out_dw")
    doc, dzc, dcg = _grms_bwd(oc_raw, p1, 3 * HG_H, w["c_gnorm_g"], dyc, 0, "hg_gate_bwd")
    landed = {}
    rows4 = lambda a: a.reshape(N_CHIP, -1, D)
    if late is None:
        dq1, df1, di1, dlb = _hg_bwd(p1, w["c_lb_raw"], ecat, ss1, doc, pad, "hg_bwd")
    else:
        dq1, df1, di1, dlb, landed["w1_1"] = _hg_bwd(
            p1, w["c_lb_raw"], ecat, ss1, doc, pad, "hg_bwd", cargo=[dw1_1], exchange=SCATTER)
    dp1 = [dq1, df1, di1, dzc]
    dh2b = _mm_groups_nt(dp1, w["c_w_in"], "c_in_dx")
    dwc = jnp.stack([_mm(h2b, d, ta=True, out_dtype=BF16, name=f"c_in_dw_{i}") for i, d in enumerate(dp1)])
    dh1a, dm0b, dfg0, dfb0 = _ln_res_bwd(h1, m0, row(w["ln_ffn_g"], 0), row(w["ln_ffn_b"], 0), [dh2a, dh2b], "ln_ffn_bwd_0")
    dh1b, dw1_0, dw2_0 = _mlp_bwd(h1b, a0, r0, dm0b, w["mlp_w1"], w["mlp_w2"], 0)
    dh0a, dmix0b, dmg0, dmb0 = _ln_res_bwd(h0, mix0, row(w["ln_mix_g"], 0), row(w["ln_mix_b"], 0), [dh1a, dh1b], "ln_mix_bwd_0")
    dycat = _mm(dmix0b, w["ab_w_out"], tb=True, name="ab_out_dx")
    dwabo = _mm(ycat, dmix0b, ta=True, out_dtype=BF16, name="ab_out_dw")
    doa, dza, dag = _grms_bwd(oa_raw, p0, 12, w["ab_gnorm_g"], dycat, 0, "gdn_gate_bwd")
    if late is None:
        dqn, dkn, dvn, dbb, daa, dal, ddt = _gdn_bwd(qn, kn, vn, p0, alog, dtb, ss0, doa, pad, "gdn_bwd")
        dqb, dkb, dvb = _sb_bwd(p0, rtot, dycat, 4, pad, "sb_bwd")
    else:
        dqn, dkn, dvn, dbb, daa, dal, ddt, landed["c_w_in"] = _gdn_bwd(
            qn, kn, vn, p0, alog, dtb, ss0, doa, pad, "gdn_bwd", cargo=[dwc], exchange=SCATTER)
        (dqb, dkb, dvb, landed["w1_0"], landed["w2_0"], landed["w2_1"], landed["ab_w_out"],
         landed["c_w_out"]) = _sb_bwd(
            p0, rtot, dycat, 4, pad, "sb_bwd",
            cargo=[dw1_0, rows4(dw2_0), rows4(dw2_1), rows4(dwabo), rows4(dwco)], exchange=SCATTER)
    dpq, dcq = _conv_bwd(p0, 0, cw[0], dqn, "q", pad, "conv_q_bwd")
    dpk, dck = _conv_bwd(p0, 4, cw[1], dkn, "k", pad, "conv_k_bwd")
    dpv, dcv = _conv_bwd(p0, 8, cw[2], dvn, "v", pad, "conv_v_bwd")
    dp0 = _assemble_bf16([dpq, dpk, dpv, dza, dbb, daa, dqb, dkb, dvb], "ab_in_dy")
    dwab = _mm(h0b, dp0, ta=True, out_dtype=BF16, name="ab_in_dw")
    if late is None:
        dh0 = _mm(dp0, w["ab_w_in"], tb=True, plus=dh0a, name="ab_in_dx")
    else:
        dab = jnp.transpose(_unpad_ab_cols(dwab).reshape(D, N_CHIP, AB_TRUE // N_CHIP), (1, 0, 2))
        dh0, landed["ab_w_in"] = _mm(dp0, w["ab_w_in"], tb=True, plus=dh0a, name="ab_in_dx", cargo=[dab],
                                     exchange=SCATTER)

    grads = {
        "ab_w_in": dwab, "conv_w": jnp.concatenate([dcq, dck, dcv], axis=1),
        "a_log": dal[:, :GDN_H], "dt_bias": ddt[:, :GDN_H],
        "ab_gnorm_g": dag, "ab_w_out": dwabo, "c_w_in": dwc, "c_lb_raw": dlb, "c_gnorm_g": dcg, "c_w_out": dwco,
        "ln_mix_g": jnp.concatenate([dmg0, dmg1], 0), "ln_mix_b": jnp.concatenate([dmb0, dmb1], 0),
        "w1_0": dw1_0, "w1_1": dw1_1, "w2_0": dw2_0, "w2_1": dw2_1,
        "ln_ffn_g": jnp.concatenate([dfg0, dfg1], 0), "ln_ffn_b": jnp.concatenate([dfb0, dfb1], 0),
        "landed": landed,
    }
    return loss, dh0, grads


MESH = pl.DeviceIdType.MESH
ANY = pl.BlockSpec(memory_space=pl.ANY)
N_CHIP = 4
N_DEV = 8
CHIP_REL = ((1, 0), (0, 1), (1, 1))
DEV_REL = tuple((dx, dy, dc) for dx in (0, 1) for dy in (0, 1) for dc in (0, 1))[1:]

def _pos():
    return lax.axis_index("x"), lax.axis_index("y"), lax.axis_index("c")


def _flip(a, d):
    return a + d - 2 * a * d


class _Exchange:
    def __init__(self, local, sends, recvs):
        self.local, self.sends, self.recvs = local, sends, recvs

    def start(self):
        for cp in self.local + self.sends:
            cp.start()

    def wait(self):
        for cp in self.recvs:
            cp.wait_recv()
        for cp in self.sends:
            cp.wait_send()
        for cp in self.local:
            cp.wait()


def _gather_sems(n):
    return [pltpu.SemaphoreType.DMA((3 * n,)), pltpu.SemaphoreType.DMA((3 * n,)), pltpu.SemaphoreType.DMA((n,))]


def _gather_copies(x_refs, o_refs, send_sems, recv_sems, local_sems):
    n = len(x_refs)
    x, y, c = _pos()
    local = [pltpu.make_async_copy(x_refs[a], o_refs[a].at[2 * x + y], local_sems.at[a]) for a in range(n)]

    def copy(a, k, sending):
        tx, ty = _flip(x, CHIP_REL[k][0]), _flip(y, CHIP_REL[k][1])
        return pltpu.make_async_remote_copy(
            src_ref=x_refs[a], dst_ref=o_refs[a].at[2 * x + y if sending else 2 * tx + ty],
            send_sem=send_sems.at[3 * a + k], recv_sem=recv_sems.at[3 * a + k], device_id=(tx, ty, c), device_id_type=MESH)

    pairs = [(a, k) for a in range(n) for k in range(3)]
    return _Exchange(local, [copy(a, k, True) for a, k in pairs], [copy(a, k, False) for a, k in pairs])


def _gather_shapes(bufs):
    return [jax.ShapeDtypeStruct((N_CHIP,) + b.shape, b.dtype) for b in bufs]


def _chip_allgather(bufs, name):
    n = len(bufs)

    def body(*refs):
        ex = _gather_copies(refs[:n], refs[n:2 * n], *refs[2 * n:])
        ex.start()
        ex.wait()

    return pl.pallas_call(
        body, name=name, in_specs=[ANY] * n, out_specs=[ANY] * n, out_shape=_gather_shapes(bufs),
        scratch_shapes=_gather_sems(n), compiler_params=pltpu.CompilerParams(has_side_effects=True),
    )(*bufs)


def _scatter_sems(n):
    nr = N_DEV - 1
    return [pltpu.SemaphoreType.DMA((nr * n,)), pltpu.SemaphoreType.DMA((nr * n,)), pltpu.SemaphoreType.DMA((n,))]


def _scatter_copies(g_refs, o_refs, send_sems, recv_sems, local_sems):
    n = len(g_refs)
    nr = N_DEV - 1
    x, y, c = _pos()
    me = 4 * x + 2 * y + c
    local = [pltpu.make_async_copy(g_refs[a].at[2 * x + y], o_refs[a].at[me], local_sems.at[a]) for a in range(n)]

    def copy(a, k, sending):
        dx, dy, dc = DEV_REL[k]
        tx, ty, tc = _flip(x, dx), _flip(y, dy), _flip(c, dc)
        return pltpu.make_async_remote_copy(
            src_ref=g_refs[a].at[2 * tx + ty], dst_ref=o_refs[a].at[me if sending else 4 * tx + 2 * ty + tc],
            send_sem=send_sems.at[nr * a + k], recv_sem=recv_sems.at[nr * a + k],
            device_id=(tx, ty, tc), device_id_type=MESH)

    pairs = [(a, k) for a in range(n) for k in range(nr)]
    return _Exchange(local, [copy(a, k, True) for a, k in pairs], [copy(a, k, False) for a, k in pairs])


def _scatter_shapes(gs):
    return [jax.ShapeDtypeStruct((N_DEV,) + g.shape[1:], g.dtype) for g in gs]


GATHER = (_gather_copies, _gather_shapes, _gather_sems)
SCATTER = (_scatter_copies, _scatter_shapes, _scatter_sems)


def _sum_slots(rs, name):
    n, rh, w = rs[0].shape
    tr = _pick(rh, (256, 128, 64, 16))

    def body(*refs):
        o_ref = refs[-1]
        for layer, r_ref in enumerate(refs[:-1]):
            acc = r_ref[0].astype(F32)
            for s in range(1, n):
                acc = acc + r_ref[s].astype(F32)
            o_ref[layer] = acc

    return pl.pallas_call(
        body, name=name, grid=(rh // tr,), in_specs=[pl.BlockSpec((n, tr, w), lambda i: (0, i, 0))] * len(rs),
        out_specs=pl.BlockSpec((len(rs), tr, w), lambda i: (0, i, 0)),
        out_shape=jax.ShapeDtypeStruct((len(rs), rh, w), F32), compiler_params=_cparams(),
    )(*rs)


def _small_allreduce(buf, name):
    r, w = buf.shape

    def body(b_ref, o_ref, land_ref, send_sems, recv_sems):
        x, y, c = _pos()
        me = 4 * x + 2 * y + c
        land_ref[me] = b_ref[...]

        def target(k):
            dx, dy, dc = DEV_REL[k]
            return _flip(x, dx), _flip(y, dy), _flip(c, dc)

        sends = []
        for k in range(N_DEV - 1):
            tx, ty, tc = target(k)
            cp = pltpu.make_async_remote_copy(
                src_ref=b_ref, dst_ref=land_ref.at[me], send_sem=send_sems.at[k], recv_sem=recv_sems.at[k],
                device_id=(tx, ty, tc), device_id_type=MESH)
            cp.start()
            sends.append(cp)
        for k in range(N_DEV - 1):
            tx, ty, tc = target(k)
            pltpu.make_async_remote_copy(
                src_ref=b_ref, dst_ref=land_ref.at[4 * tx + 2 * ty + tc], send_sem=send_sems.at[k],
                recv_sem=recv_sems.at[k], device_id=(tx, ty, tc), device_id_type=MESH).wait_recv()
        for cp in sends:
            cp.wait_send()
        acc = land_ref[0]
        for s in range(1, N_DEV):
            acc = acc + land_ref[s]
        o_ref[...] = acc

    vm = pl.BlockSpec(memory_space=pltpu.VMEM)
    return pl.pallas_call(
        body, name=name, in_specs=[vm], out_specs=vm, out_shape=jax.ShapeDtypeStruct((r, w), F32),
        scratch_shapes=[pltpu.VMEM((N_DEV, r, w), F32), pltpu.SemaphoreType.DMA((N_DEV - 1,)),
                        pltpu.SemaphoreType.DMA((N_DEV - 1,))],
        compiler_params=pltpu.CompilerParams(has_side_effects=True),
    )(buf)


def _adamw(w, g, m, v, name):
    r, c = w.shape
    tr = _pick(r, (256, 128, 64, 8)) if r * c > (1 << 18) else r

    def body(w_ref, g_ref, m_ref, v_ref, d_ref, m2_ref, v2_ref):
        gg = g_ref[...]
        m2 = ADAM_B1 * m_ref[...] + (1.0 - ADAM_B1) * gg
        v2 = ADAM_B2 * v_ref[...] + (1.0 - ADAM_B2) * (gg * gg)
        m_hat = m2 / (1.0 - ADAM_B1 ** ADAM_STEP)
        v_hat = v2 / (1.0 - ADAM_B2 ** ADAM_STEP)
        d_ref[...] = -ADAM_LR * (m_hat / (jnp.sqrt(v_hat) + ADAM_EPS) + ADAM_WD * w_ref[...])
        m2_ref[...] = m2
        v2_ref[...] = v2

    blk = pl.BlockSpec((tr, c), lambda i: (i, 0))
    sds = jax.ShapeDtypeStruct((r, c), F32)
    return pl.pallas_call(body, name=name, grid=(r // tr,), in_specs=[blk] * 4, out_specs=[blk] * 3,
                          out_shape=[sds] * 3, compiler_params=_cparams())(w, g, m, v)


BIG = ("ab_w_in", "ab_w_out", "c_w_in", "c_w_out", "mlp_w1", "mlp_w2")
SMALL = ("ln_mix_g", "ln_mix_b", "ln_ffn_g", "ln_ffn_b", "c_lb_raw", "ab_a_log", "ab_dt_bias", "ab_gnorm_g", "c_gnorm_g")
SMALL_ROWS = 16
CONV_ROWS = 8
CONV_W = 3 * GDN_H * HD


def _conv_to_rows(cw):
    return jnp.pad(cw, ((0, 0), (0, 2 * D - CONV_W))).reshape(CONV_ROWS, D)


def _rows_to_conv(rows):
    return rows.reshape(CONV_K, 2 * D)[:, :CONV_W]


def _pack_small(d):
    rows = [jnp.pad(d[n], ((0, 0), (0, D - d[n].shape[1]))) for n in SMALL]
    buf = jnp.concatenate(rows, axis=0)
    return jnp.pad(buf, ((0, SMALL_ROWS - buf.shape[0]), (0, 0)))


def _unpack_small(buf, like):
    out, r = {}, 0
    for n in SMALL:
        nr, nc = like[n].shape
        out[n] = buf[r:r + nr, :nc]
        r += nr
    return out


def kernel(x, meta_tokens, ab_w_in, ab_conv_w, ab_a_log, ab_dt_bias, ab_gnorm_g, ab_w_out, c_w_in, c_lb_raw, c_gnorm_g, c_w_out, ln_mix_g, ln_mix_b, mlp_w1, mlp_w2, ln_ffn_g, ln_ffn_b, loss_target, m_meta_tokens, m_ab_w_in, m_ab_conv_w, m_ab_a_log, m_ab_dt_bias, m_ab_gnorm_g, m_ab_w_out, m_c_w_in, m_c_lb_raw, m_c_gnorm_g, m_c_w_out, m_ln_mix_g, m_ln_mix_b, m_mlp_w1, m_mlp_w2, m_ln_ffn_g, m_ln_ffn_b, v_meta_tokens, v_ab_w_in, v_ab_conv_w, v_ab_a_log, v_ab_dt_bias, v_ab_gnorm_g, v_ab_w_out, v_c_w_in, v_c_lb_raw, v_c_gnorm_g, v_c_w_out, v_ln_mix_g, v_ln_mix_b, v_mlp_w1, v_mlp_w2, v_ln_ffn_g, v_ln_ffn_b):
    names = ("meta_tokens", "ab_w_in", "ab_conv_w", "ab_a_log", "ab_dt_bias", "ab_gnorm_g", "ab_w_out", "c_w_in",
             "c_lb_raw", "c_gnorm_g", "c_w_out", "ln_mix_g", "ln_mix_b", "mlp_w1", "mlp_w2", "ln_ffn_g", "ln_ffn_b")
    wts = dict(zip(names, (meta_tokens, ab_w_in, ab_conv_w, ab_a_log, ab_dt_bias, ab_gnorm_g, ab_w_out, c_w_in, c_lb_raw,
                           c_gnorm_g, c_w_out, ln_mix_g, ln_mix_b, mlp_w1, mlp_w2, ln_ffn_g, ln_ffn_b)))
    mom_m = dict(zip(names, (m_meta_tokens, m_ab_w_in, m_ab_conv_w, m_ab_a_log, m_ab_dt_bias, m_ab_gnorm_g, m_ab_w_out,
                             m_c_w_in, m_c_lb_raw, m_c_gnorm_g, m_c_w_out, m_ln_mix_g, m_ln_mix_b, m_mlp_w1, m_mlp_w2,
                             m_ln_ffn_g, m_ln_ffn_b)))
    mom_v = dict(zip(names, (v_meta_tokens, v_ab_w_in, v_ab_conv_w, v_ab_a_log, v_ab_dt_bias, v_ab_gnorm_g, v_ab_w_out,
                             v_c_w_in, v_c_lb_raw, v_c_gnorm_g, v_c_w_out, v_ln_mix_g, v_ln_mix_b, v_mlp_w1, v_mlp_w2,
                             v_ln_ffn_g, v_ln_ffn_b)))
    seq = x.shape[1]
    pad = (-(N_META + seq)) % QB
    xi, yi, ci = _pos()
    chip = 2 * xi + yi

    gat_ab_in, = _chip_allgather([ab_w_in[0].astype(BF16)], "gather_weights")
    late = {"ab_w_out": ab_w_out[0].astype(BF16), "c_w_in": c_w_in.astype(BF16), "c_w_out": c_w_out[0].astype(BF16),
            "mlp_w1": mlp_w1.astype(BF16), "mlp_w2": mlp_w2.astype(BF16)}
    mcols, ccols = meta_tokens.shape[1], ab_conv_w.shape[2]
    place = jnp.concatenate([
        lax.dynamic_update_slice(jnp.zeros((N_META, D), F32), 0.5 * meta_tokens, (0, chip * mcols)),
        _conv_to_rows(lax.dynamic_update_slice(jnp.zeros((CONV_K, CONV_W), F32), 0.5 * ab_conv_w[0], (0, chip * ccols)))],
        axis=0)
    placed = _small_allreduce(place, "gather_meta")
    meta_full = placed[:N_META]

    w = {
        "ab_w_in": _pad_ab_cols(jnp.transpose(gat_ab_in, (1, 0, 2)).reshape(D, AB_TRUE)),
        "conv_w": _rows_to_conv(placed[N_META:]), "a_log": ab_a_log, "dt_bias": ab_dt_bias,
        "ab_gnorm_g": ab_gnorm_g, "c_lb_raw": c_lb_raw,
        "c_gnorm_g": c_gnorm_g, "ln_mix_g": ln_mix_g, "ln_mix_b": ln_mix_b, "ln_ffn_g": ln_ffn_g, "ln_ffn_b": ln_ffn_b,
    }

    h0 = jnp.concatenate([jnp.zeros((pad, D), F32), meta_full, x[0]], axis=0)
    tgt = jnp.concatenate([jnp.zeros((pad + N_META, D), F32), loss_target[0]], axis=0)
    loss8, dh0, g = _local_step(h0, tgt, w, pad, late)
    grad_x = dh0[pad + N_META:][None]

    gsmall = {"ln_mix_g": g["ln_mix_g"], "ln_mix_b": g["ln_mix_b"], "ln_ffn_g": g["ln_ffn_g"], "ln_ffn_b": g["ln_ffn_b"],
              "c_lb_raw": g["c_lb_raw"], "ab_a_log": g["a_log"], "ab_dt_bias": g["dt_bias"], "ab_gnorm_g": g["ab_gnorm_g"],
              "c_gnorm_g": g["c_gnorm_g"]}
    packed = _pack_small(gsmall).at[SMALL_ROWS - 1, :loss8.shape[1]].set(loss8[0])
    sbuf = jnp.concatenate([packed, dh0[pad:pad + N_META], _conv_to_rows(g["conv_w"])], axis=0)
    ssum = _small_allreduce(sbuf, "allreduce_small")
    loss = ssum[SMALL_ROWS - 1, 0]
    grads = _unpack_small(ssum[:SMALL_ROWS], wts)
    grads["meta_tokens"] = lax.dynamic_slice(ssum[SMALL_ROWS:SMALL_ROWS + N_META], (0, chip * mcols), (N_META, mcols))
    grads["ab_conv_w"] = lax.dynamic_slice(_rows_to_conv(ssum[SMALL_ROWS + N_META:]), (0, chip * ccols), (CONV_K, ccols))[None]

    landed = g["landed"]
    for n in ("ab_w_in", "ab_w_out", "c_w_in", "c_w_out"):
        grads[n] = _sum_slots([landed[n]], f"grad_sum_{n}")
    grads["mlp_w1"] = _sum_slots([landed["w1_0"], landed["w1_1"]], "grad_sum_mlp_w1")
    grads["mlp_w2"] = _sum_slots([landed["w2_0"], landed["w2_1"]], "grad_sum_mlp_w2")

    delta, new_m, new_v = {}, {}, {}
    for n in ("meta_tokens", "ab_conv_w") + BIG:
        shp = wts[n].shape
        to2 = lambda a: a.reshape(-1, shp[-1])
        d2, m2, v2 = _adamw(to2(wts[n]), to2(grads[n]), to2(mom_m[n]), to2(mom_v[n]), f"adamw_{n}")
        delta[n], new_m[n], new_v[n] = d2.reshape(shp), m2.reshape(shp), v2.reshape(shp)
    d2, m2, v2 = _adamw(_pack_small(wts), ssum[:SMALL_ROWS], _pack_small(mom_m), _pack_small(mom_v), "adamw_small")
    delta.update(_unpack_small(d2, wts))
    new_m.update(_unpack_small(m2, wts))
    new_v.update(_unpack_small(v2, wts))

    return (loss, grad_x, *[grads[n] for n in names], *[delta[n] for n in names], *[new_m[n] for n in names],
            *[new_v[n] for n in names])
```

```python
import functools

import numpy as np
import jax
import jax.numpy as jnp
from jax import lax
from jax.experimental import pallas as pl
from jax.experimental.pallas import tpu as pltpu

F32 = jnp.float32
BF16 = jnp.bfloat16

D = 1024
N_META = 16
DEPTH = 2
GDN_H = 4
SB_H = 8
SB_DH = 64
HG_H = 8
HD = 128
CH = 64
QB = 128
ALPHA = float((2 * DEPTH) ** 0.25)
LN_EPS = 1e-5
RMS_EPS = 1e-6
L2_EPS = 1e-6
NEG = -1e30

ADAM_LR = 0.001
ADAM_B1 = 0.9
ADAM_B2 = 0.999
ADAM_EPS = 1e-08
ADAM_WD = 0.01
ADAM_STEP = 10

AB_TRUE = 3592
V7X_VMEM_BYTES = 64 * 1024 * 1024
VMEM_LIMIT = V7X_VMEM_BYTES - 8 * 1024 * 1024

NN = ((1,), (0,))
NT = ((1,), (1,))
TN = ((0,), (0,))


def _cparams(**kw):
    return pltpu.CompilerParams(vmem_limit_bytes=VMEM_LIMIT, **kw)


def _dg(a, b, dims, mode):
    if mode == "h":
        return lax.dot_general(a, b, dims, precision=lax.Precision.HIGHEST, preferred_element_type=F32)
    if mode == "b":
        return lax.dot_general(a.astype(BF16), b.astype(BF16), dims, preferred_element_type=F32)
    ah, bh = a.astype(BF16), b.astype(BF16)
    al, bl = (a - ah.astype(F32)).astype(BF16), (b - bh.astype(F32)).astype(BF16)
    d = lambda x, y: lax.dot_general(x, y, dims, preferred_element_type=F32)
    return d(ah, bh) + (d(ah, bl) + d(al, bh))


def _make_dots(mode, batched=False):
    if batched:
        nn_d, nt_d, tn_d = (((2,), (1,)), ((0,), (0,))), (((2,), (2,)), ((0,), (0,))), (((1,), (1,)), ((0,), (0,)))
    else:
        nn_d, nt_d, tn_d = (NN, ((), ())), (NT, ((), ())), (TN, ((), ()))

    @jax.custom_vjp
    def nn(a, b):
        return _dg(a, b, nn_d, mode)

    @jax.custom_vjp
    def nt(a, b):
        return _dg(a, b, nt_d, mode)

    @jax.custom_vjp
    def tn(a, b):
        return _dg(a, b, tn_d, mode)

    nn.defvjp(lambda a, b: (nn(a, b), (a, b)), lambda r, g: (nt(g, r[1]), tn(r[0], g)))
    nt.defvjp(lambda a, b: (nt(a, b), (a, b)), lambda r, g: (nn(g, r[1]), tn(g, r[0])))
    tn.defvjp(lambda a, b: (tn(a, b), (a, b)), lambda r, g: (nt(r[1], g), nn(r[0], g)))
    return nn, nt, tn


hnn = _make_dots("h")[0]
bbnn, bbnt, bbtn = _make_dots("b", True)
mbnn, mbnt, mbtn = _make_dots("m", True)
hbnt = _make_dots("h", True)[1]


def _split3(x, axis):
    x1 = x.astype(BF16)
    r1 = x - x1.astype(F32)
    x2 = r1.astype(BF16)
    x3 = (r1 - x2.astype(F32)).astype(BF16)
    return jnp.concatenate([x1, x2, x3], axis=axis)


@jax.custom_vjp
def _mask_dot(e3, x):
    return lax.dot_general(e3[0], _split3(x, 0), (NN, ((), ())), preferred_element_type=F32)


def _mask_dot_bwd(e3, g):
    dx = lax.dot_general(e3[1], _split3(g, 0), (TN, ((), ())), preferred_element_type=F32)
    return (jnp.zeros_like(e3[0]), jnp.zeros_like(e3[1])), dx


_mask_dot.defvjp(lambda e3, x: (_mask_dot(e3, x), e3), _mask_dot_bwd)


def _heads(a, n):
    return jnp.concatenate([a[None, :, h * HD:(h + 1) * HD] for h in range(n)], axis=0)


def _sigmoid(x):
    return jax.nn.sigmoid(x)


def _silu(x):
    return x * jax.nn.sigmoid(x)


def _softplus(x):
    return jnp.maximum(x, 0.0) + jnp.log(1.0 + jnp.exp(-jnp.abs(x)))


def _iota(shape, dim):
    return lax.broadcasted_iota(jnp.int32, shape, dim)


def _pick(n, prefs):
    for p in prefs:
        if n % p == 0:
            return p
    return n


def _mm(a, b, *, ta=False, tb=False, out_dtype=F32, name, b_view=None, out_split=0, act=False, gate=None, plus=None,
        ln=None, cargo=(), exchange=None):
    if ta:
        k_dim, m_dim = a.shape
    else:
        m_dim, k_dim = a.shape
    if b_view is None:
        w_rows, w_cols = b.shape
    else:
        kind, layer = b_view
        nj, _, blk_r, blk_c = b.shape
        w_rows, w_cols = (blk_r, nj * blk_c) if kind == "cols" else (nj * blk_r, blk_c)
    n_dim = w_rows if tb else w_cols
    assert (w_cols if tb else w_rows) == k_dim
    tm = _pick(m_dim, (1024, 1056, 704, 640, 512, 384, 256, 128))
    tn = _pick(n_dim, (1024, 1056, 704, 640, 512, 384, 256, 128))
    tk = _pick(k_dim, (1024, 1056, 704, 512, 384, 256, 128))
    nk = k_dim // tk
    a_spec = pl.BlockSpec((tk, tm), lambda i, j, k: (k, i)) if ta else pl.BlockSpec((tm, tk), lambda i, j, k: (i, k))
    wb = (tn, tk) if tb else (tk, tn)
    w_idx = (lambda i, j, k: (j, k)) if tb else (lambda i, j, k: (k, j))
    if b_view is None:
        b_spec = pl.BlockSpec(wb, w_idx)
    elif kind == "cols":
        per = blk_c // wb[1]
        b_spec = pl.BlockSpec((None, None) + wb,
                              lambda i, j, k: (w_idx(i, j, k)[1] // per, layer, w_idx(i, j, k)[0], w_idx(i, j, k)[1] % per))
    else:
        per = blk_r // wb[0]
        b_spec = pl.BlockSpec((None, None) + wb,
                              lambda i, j, k: (w_idx(i, j, k)[0] // per, layer, w_idx(i, j, k)[0] % per, w_idx(i, j, k)[1]))
    if out_split:
        per_o = (n_dim // out_split) // tn
        out_spec = pl.BlockSpec((None, tm, tn), lambda i, j, k: (j // per_o, i, j % per_o))
        out_sds = jax.ShapeDtypeStruct((out_split, m_dim, n_dim // out_split), out_dtype)
    else:
        out_spec = pl.BlockSpec((tm, tn), lambda i, j, k: (i, j))
        out_sds = jax.ShapeDtypeStruct((m_dim, n_dim), out_dtype)
    dims = (((0 if ta else 1,), (1 if tb else 0,)), ((), ()))
    assert sum(e is not None for e in (gate, plus, ln)) <= 1
    extra = [e for e in (gate, plus) if e is not None] + list(ln or ())
    n_out = 2 if act else 3 if ln else 1
    assert ln is None or (tn == n_dim and not out_split)

    def finish(acc, refs):
        if ln:
            y = _ln_res_fn(refs[0][...], acc, refs[1][...], refs[2][...])
            refs[3][...] = acc
            refs[4][...] = y
            refs[5][...] = y.astype(BF16)
        elif act:
            refs[0][...] = acc.astype(refs[0].dtype)
            r = jnp.maximum(acc, 0.0)
            refs[1][...] = (r * r).astype(refs[1].dtype)
        elif gate is not None:
            refs[1][...] = (acc * (2.0 * jnp.maximum(refs[0][...].astype(F32), 0.0))).astype(refs[1].dtype)
        elif plus is not None:
            refs[1][...] = (refs[0][...] + acc).astype(refs[1].dtype)
        else:
            refs[0][...] = acc.astype(refs[0].dtype)

    grid = (m_dim // tm, n_dim // tn, nk)
    nc = len(cargo)

    def body(a_ref, b_ref, *rest):
        acc_ref = rest[-1]
        ids = [pl.program_id(d) for d in range(3)]
        outs, end_cargo = _cargo_bounds(
            rest[len(extra):-1], nc, n_out, exchange, (ids[0] == 0) & (ids[1] == 0) & (ids[2] == 0),
            (ids[0] == grid[0] - 1) & (ids[1] == grid[1] - 1) & (ids[2] == grid[2] - 1))
        refs = tuple(rest[:len(extra)]) + tuple(outs)
        part = lax.dot_general(a_ref[...], b_ref[...], dims, preferred_element_type=F32)
        if nk == 1:
            finish(part, refs)
        else:
            k = ids[2]

            @pl.when(k == 0)
            def _():
                acc_ref[...] = part

            @pl.when(k > 0)
            def _():
                acc_ref[...] += part

            @pl.when(k == nk - 1)
            def _():
                finish(acc_ref[...], refs)
        end_cargo()

    tile = pl.BlockSpec((tm, tn), lambda i, j, k: (i, j))
    rowv = pl.BlockSpec((1, tn), lambda i, j, k: (0, j))
    out_sdss = [out_sds] * n_out
    if ln:
        out_sdss = [jax.ShapeDtypeStruct((m_dim, n_dim), dt) for dt in (F32, F32, BF16)]
    out = pl.pallas_call(
        body, name=name, grid=grid,
        in_specs=[a_spec, b_spec] + ([tile, rowv, rowv] if ln else [tile] * len(extra)) + [ANY] * nc,
        out_specs=[out_spec] * n_out + [ANY] * nc,
        out_shape=out_sdss + (exchange[1](cargo) if nc else []),
        scratch_shapes=(exchange[2](nc) if nc else []) + [pltpu.VMEM((tm, tn) if nk > 1 else (8, 128), F32)],
        compiler_params=_cparams(dimension_semantics=("arbitrary",) * 3 if nc else ("parallel", "parallel", "arbitrary")),
    )(a, b, *extra, *cargo)
    if nc:
        return out
    return out if (act or ln) else out[0]


def _mm_groups_nt(parts, b, name):
    m_dim, k_dim = parts[0].shape
    ng, _, n_dim, _ = b.shape
    assert len(parts) == ng and b.shape[3] == k_dim
    tm = _pick(m_dim, (1056, 704, 512, 384, 256, 128))

    def body(*refs):
        a_refs, b_ref, o_ref, acc_ref = refs[:ng], refs[ng], refs[ng + 1], refs[ng + 2]
        k = pl.program_id(1)
        for g in range(ng):
            @pl.when(k == g)
            def _(g=g):
                part = lax.dot_general(a_refs[g][...], b_ref[...], (NT, ((), ())), preferred_element_type=F32)
                if g == 0:
                    acc_ref[...] = part
                elif g < ng - 1:
                    acc_ref[...] += part
                else:
                    o_ref[...] = acc_ref[...] + part

    return pl.pallas_call(
        body, name=name, grid=(m_dim // tm, ng),
        in_specs=[pl.BlockSpec((tm, k_dim), lambda i, k: (i, 0))] * ng
        + [pl.BlockSpec((None, None, n_dim, k_dim), lambda i, k: (k, 0, 0, 0))],
        out_specs=pl.BlockSpec((tm, n_dim), lambda i, k: (i, 0)),
        out_shape=jax.ShapeDtypeStruct((m_dim, n_dim), F32),
        scratch_shapes=[pltpu.VMEM((tm, n_dim), F32)],
        compiler_params=_cparams(dimension_semantics=("parallel", "arbitrary")),
    )(*parts, b)


def _row_tile(t_pad, width):
    for tr in (528, 352, 176, 128, 64):
        if t_pad % tr == 0 and tr * width * 4 <= (3 << 19) and tr % 16 == 0:
            return tr
    return 64 if t_pad % 64 == 0 else t_pad


def _ln_res_fn(h, m, g, b):
    x = ALPHA * h + m
    mu = jnp.mean(x, axis=-1, keepdims=True)
    xc = x - mu
    var = jnp.mean(xc * xc, axis=-1, keepdims=True)
    return xc * lax.rsqrt(var + LN_EPS) * g + b


def _ln_res_bwd(h, m, g, b, dys, name):
    t_pad = h.shape[0]
    tr = _row_tile(t_pad, D)
    nd = len(dys)

    def body(h_ref, m_ref, g_ref, b_ref, *rest):
        d_refs, (dh_ref, dm_ref, dg_ref, db_ref) = rest[:nd], rest[nd:]
        _, vjp = jax.vjp(_ln_res_fn, h_ref[...], m_ref[...], g_ref[...], b_ref[...])
        dy = d_refs[0][...]
        for d_ref in d_refs[1:]:
            dy = dy + d_ref[...]
        dh, dm, dg, db = vjp(dy)
        dh_ref[...] = dh
        dm_ref[...] = dm.astype(BF16)

        @pl.when(pl.program_id(0) == 0)
        def _():
            dg_ref[...] = jnp.zeros_like(dg_ref)
            db_ref[...] = jnp.zeros_like(db_ref)

        dg_ref[...] += dg
        db_ref[...] += db

    row = pl.BlockSpec((tr, D), lambda i: (i, 0))
    par = pl.BlockSpec((1, D), lambda i: (0, 0))
    return pl.pallas_call(
        body, name=name, grid=(t_pad // tr,), in_specs=[row, row, par, par] + [row] * nd,
        out_specs=[row, row, par, par],
        out_shape=[jax.ShapeDtypeStruct((t_pad, D), F32), jax.ShapeDtypeStruct((t_pad, D), BF16),
                   jax.ShapeDtypeStruct((1, D), F32), jax.ShapeDtypeStruct((1, D), F32)],
        compiler_params=_cparams(),
    )(h, m, g, b, *dys)


def _grms_fn(o, z, g):
    y = o * lax.rsqrt(jnp.mean(o * o, axis=-1, keepdims=True) + RMS_EPS) * g
    return y * _silu(z)


def _grms_fwd(o, z_arr, z_blk0, g, name):
    t_pad, w = o.shape
    tr = _row_tile(t_pad, w)
    assert (z_blk0 * HD) % w == 0

    def body(o_ref, z_ref, g_ref, y_ref):
        for h in range(w // HD):
            c = slice(h * HD, (h + 1) * HD)
            y_ref[:, c] = _grms_fn(o_ref[:, c], z_ref[:, c], g_ref[...]).astype(BF16)

    return pl.pallas_call(
        body, name=name, grid=(t_pad // tr,),
        in_specs=[pl.BlockSpec((tr, w), lambda i: (i, 0)), pl.BlockSpec((tr, w), lambda i: (i, z_blk0 * HD // w)),
                  pl.BlockSpec((1, HD), lambda i: (0, 0))],
        out_specs=pl.BlockSpec((tr, w), lambda i: (i, 0)),
        out_shape=jax.ShapeDtypeStruct((t_pad, w), BF16), compiler_params=_cparams(),
    )(o, z_arr, g)


def _grms_bwd(o, z_arr, z_blk0, g, dy_arr, dy_blk0, name):
    t_pad, w = o.shape
    tr = _row_tile(t_pad, w)
    assert (z_blk0 * HD) % w == 0 and (dy_blk0 * HD) % w == 0

    def body(o_ref, z_ref, g_ref, dy_ref, do_ref, dz_ref, dg_ref):
        @pl.when(pl.program_id(0) == 0)
        def _():
            dg_ref[...] = jnp.zeros_like(dg_ref)

        for h in range(w // HD):
            c = slice(h * HD, (h + 1) * HD)
            _, vjp = jax.vjp(_grms_fn, o_ref[:, c], z_ref[:, c], g_ref[...])
            do, dz, dg = vjp(dy_ref[:, c])
            do_ref[:, c] = do
            dz_ref[:, c] = dz.astype(BF16)
            dg_ref[...] += dg

    blk = pl.BlockSpec((tr, w), lambda i: (i, 0))
    return pl.pallas_call(
        body, name=name, grid=(t_pad // tr,),
        in_specs=[blk, pl.BlockSpec((tr, w), lambda i: (i, z_blk0 * HD // w)), pl.BlockSpec((1, HD), lambda i: (0, 0)),
                  pl.BlockSpec((tr, w), lambda i: (i, dy_blk0 * HD // w))],
        out_specs=[blk, blk, pl.BlockSpec((1, HD), lambda i: (0, 0))],
        out_shape=[jax.ShapeDtypeStruct((t_pad, w), F32), jax.ShapeDtypeStruct((t_pad, w), BF16),
                   jax.ShapeDtypeStruct((1, HD), F32)],
        compiler_params=_cparams(),
    )(o, z_arr, g, dy_arr)


def _loss_fwd(y, tgt, first_row, name):
    t_pad = y.shape[0]
    tr = _row_tile(t_pad, D)

    def body(y_ref, t_ref, l_ref, dy_ref):
        rows = pl.program_id(0) * tr + _iota((tr, 1), 0)
        err = jnp.where(rows >= first_row, y_ref[...] - t_ref[...], 0.0)
        dy_ref[...] = err * (1.0 / D)

        @pl.when(pl.program_id(0) == 0)
        def _():
            l_ref[...] = jnp.zeros_like(l_ref)

        part = jnp.sum(jnp.sum(err * err, axis=1, keepdims=True), axis=0, keepdims=True)
        l_ref[...] += jnp.broadcast_to(part * (0.5 / D), l_ref.shape)

    row = pl.BlockSpec((tr, D), lambda i: (i, 0))
    return pl.pallas_call(
        body, name=name, grid=(t_pad // tr,), in_specs=[row, row],
        out_specs=[pl.BlockSpec((8, 128), lambda i: (0, 0)), row],
        out_shape=[jax.ShapeDtypeStruct((8, 128), F32), jax.ShapeDtypeStruct((t_pad, D), F32)],
        compiler_params=_cparams(),
    )(y, tgt)


def _assemble_bf16(parts, name):
    t_pad = parts[0].shape[0]
    widths = [p.shape[1] for p in parts]
    total = sum(widths)
    tr = _row_tile(t_pad, total)

    def body(*refs):
        o_ref = refs[-1]
        off = 0
        for ref, w in zip(refs[:-1], widths):
            o_ref[:, off:off + w] = ref[...].astype(BF16)
            off += w

    return pl.pallas_call(
        body, name=name, grid=(t_pad // tr,), in_specs=[pl.BlockSpec((tr, w), lambda i: (i, 0)) for w in widths],
        out_specs=pl.BlockSpec((tr, total), lambda i: (i, 0)),
        out_shape=jax.ShapeDtypeStruct((t_pad, total), BF16), compiler_params=_cparams(),
    )(*parts)


CONV_K = 4
HALO = 8
RT = 128


def _conv_fwd(p, blk0, w, mode, pad, name):
    t_pad = p.shape[0]
    nt = t_pad // RT
    scale = HD ** -0.5 if mode == "q" else 1.0

    def body(x_ref, w_ref, y_ref, xs_ref):
        xs_ref[0:HALO, :] = jnp.zeros((HALO, HD), F32)
        rows = _iota((t_pad, 1), 0)
        xs_ref[HALO:HALO + t_pad, :] = jnp.where(rows >= pad, x_ref[...], 0.0)
        wv = w_ref[...]

        def tile(i, carry):
            r0 = pl.multiple_of(i * RT, RT)
            ext = xs_ref[pl.ds(r0, RT + HALO), :]
            acc = ext[HALO:, :] * wv[3:4, :]
            for s in (1, 2, 3):
                acc = acc + pltpu.roll(ext, s, 0)[HALO:, :] * wv[3 - s:4 - s, :]
            y = _silu(acc)
            if mode != "v":
                y = y * lax.rsqrt(jnp.sum(y * y, axis=-1, keepdims=True) + L2_EPS) * scale
            y_ref[pl.ds(r0, RT), :] = y
            return carry

        lax.fori_loop(0, nt, tile, 0)

    return pl.pallas_call(
        body, name=name, grid=(GDN_H,),
        in_specs=[pl.BlockSpec((t_pad, HD), lambda h: (0, blk0 + h)), pl.BlockSpec((CONV_K, HD), lambda h: (0, h))],
        out_specs=pl.BlockSpec((t_pad, HD), lambda h: (0, h)),
        out_shape=jax.ShapeDtypeStruct((t_pad, GDN_H * HD), F32),
        scratch_shapes=[pltpu.VMEM((t_pad + HALO, HD), F32)],
        compiler_params=_cparams(),
    )(p, w)


def _conv_bwd(p, blk0, w, dn, mode, pad, name):
    t_pad = p.shape[0]
    nt = t_pad // RT
    scale = HD ** -0.5 if mode == "q" else 1.0

    def body(x_ref, w_ref, dn_ref, dx_ref, dw_ref, xs_ref, ds_ref):
        xs_ref[0:HALO, :] = jnp.zeros((HALO, HD), F32)
        xs_ref[HALO + t_pad:HALO + t_pad + 2 * HALO, :] = jnp.zeros((2 * HALO, HD), F32)
        ds_ref[t_pad:t_pad + HALO, :] = jnp.zeros((HALO, HD), F32)
        rows = _iota((t_pad, 1), 0)
        xs_ref[HALO:HALO + t_pad, :] = jnp.where(rows >= pad, x_ref[...], 0.0)
        ds_ref[0:t_pad, :] = dn_ref[...]
        wv = w_ref[...]

        def tile(i, dw):
            r0 = pl.multiple_of(i * RT, RT)
            ext = xs_ref[pl.ds(r0, RT + 2 * HALO), :]
            dn_e = ds_ref[pl.ds(r0, RT + HALO), :]
            xsh = [ext[HALO:, :]] + [pltpu.roll(ext, s, 0)[HALO:, :] for s in (1, 2, 3)]
            pre = xsh[0] * wv[3:4, :]
            for s in (1, 2, 3):
                pre = pre + xsh[s] * wv[3 - s:4 - s, :]
            sg = _sigmoid(pre)
            y = pre * sg
            if mode != "v":
                ss = jnp.sum(y * y, axis=-1, keepdims=True) + L2_EPS
                r = lax.rsqrt(ss)
                dy = scale * (dn_e * r - y * (r * r * r) * jnp.sum(dn_e * y, axis=-1, keepdims=True))
            else:
                dy = dn_e
            dpre = dy * (sg * (1.0 + pre * (1.0 - sg)))
            dx = dpre[:RT, :] * wv[3:4, :]
            for s in (1, 2, 3):
                dx = dx + pltpu.roll(dpre, RT + HALO - s, 0)[:RT, :] * wv[3 - s:4 - s, :]
            trow = r0 + _iota((RT, 1), 0)
            dx_ref[pl.ds(r0, RT), :] = jnp.where(trow >= pad, dx, 0.0)
            new = []
            for s in (0, 1, 2, 3):
                new.append(dw[s] + jnp.sum(dpre[:RT, :] * xsh[s][:RT, :], axis=0, keepdims=True))
            return tuple(new)

        z = jnp.zeros((1, HD), F32)
        dw = lax.fori_loop(0, nt, tile, (z, z, z, z))
        for s in (0, 1, 2, 3):
            dw_ref[3 - s:4 - s, :] = dw[s]

    return pl.pallas_call(
        body, name=name, grid=(GDN_H,),
        in_specs=[pl.BlockSpec((t_pad, HD), lambda h: (0, blk0 + h)), pl.BlockSpec((CONV_K, HD), lambda h: (0, h)),
                  pl.BlockSpec((t_pad, HD), lambda h: (0, h))],
        out_specs=[pl.BlockSpec((t_pad, HD), lambda h: (0, h)), pl.BlockSpec((CONV_K, HD), lambda h: (0, h))],
        out_shape=[jax.ShapeDtypeStruct((t_pad, GDN_H * HD), F32), jax.ShapeDtypeStruct((CONV_K, GDN_H * HD), F32)],
        scratch_shapes=[pltpu.VMEM((t_pad + 3 * HALO, HD), F32), pltpu.VMEM((t_pad + HALO, HD), F32)],
        compiler_params=_cparams(),
    )(p, w, dn)


@jax.custom_vjp
def _unit_lower_inv(m, bd, eye):
    md = m * bd
    low = m - md
    p2 = mbnn(md, md)
    p4 = mbnn(p2, p2)
    dinv = mbnn(mbnn(eye - md, eye + p2), eye + p4)
    n = mbnn(dinv, low)
    n2 = mbnn(n, n)
    n4 = mbnn(n2, n2)
    return mbnn(mbnn(mbnn(eye - n, eye + n2), eye + n4), dinv)


def _unit_lower_inv_bwd(res, g):
    t, bd, eye = res
    return -mbtn(t, mbnt(g, t)), jnp.zeros_like(bd), jnp.zeros_like(eye)


def _unit_lower_inv_fwd(m, bd, eye):
    t = _unit_lower_inv(m, bd, eye)
    return t, (t, bd, eye)


_unit_lower_inv.defvjp(_unit_lower_inv_fwd, _unit_lower_inv_bwd)


def _gdn_chunks(chunks, alog, dtb, s):
    nh = chunks[0][0].shape[0]
    ri = _iota((1, CH, CH), 1)
    ci = _iota((1, CH, CH), 2)
    causal = ri >= ci
    strict = ri > ci
    eye = (ri == ci).astype(F32)
    bd = ((ri >> 3) == (ci >> 3)).astype(F32)
    ltri = (_iota((CH, CH), 0) >= _iota((CH, CH), 1)).astype(F32)
    sel = (_iota((nh, 1, HD), 2) == _iota((nh, 1, HD), 0)).astype(F32)
    last = _iota((1, CH, 1), 1) == CH - 1

    beta, gc, gc_rows = [], [], []
    for _, _, _, bb, aa, valid in chunks:
        beta_all = jnp.where(valid, _sigmoid(bb), 0.0)
        g_all = jnp.where(valid, -jnp.exp(alog) * _softplus(aa + dtb), 0.0)
        gc_all = hnn(ltri, g_all)
        beta.append(jnp.sum(beta_all[None] * sel, axis=2, keepdims=True))
        gc.append(jnp.sum(gc_all[None] * sel, axis=2, keepdims=True))
        gc_rows.append(hbnt(jnp.broadcast_to(sel, (nh, CH, HD)), jnp.broadcast_to(gc_all[None], (nh, CH, HD))))
    cat = lambda xs: jnp.concatenate(xs, axis=0)
    q, k, v = (cat([c[j] for c in chunks]) for j in range(3))
    beta, gc, gc_rows = cat(beta), cat(gc), cat(gc_rows)
    gc_last = jnp.sum(jnp.where(last, gc, 0.0), axis=1, keepdims=True)
    decay = jnp.exp(jnp.where(causal, gc - gc_rows, NEG))
    egc = jnp.exp(gc)

    kb = k * beta
    m = jnp.where(strict, bbnt(kb, k) * decay, 0.0)
    t_inv = _unit_lower_inv(m, bd, eye)
    u = bbnn(t_inv, v * beta)
    w = bbnn(t_inv, kb * egc)
    a_intra = bbnt(q, k) * decay
    q_dec = q * egc
    k_dec = k * jnp.exp(gc_last - gc)
    g_tot = jnp.exp(gc_last)

    outs = []
    for n in range(len(chunks)):
        part = lambda a: a[n * nh:(n + 1) * nh]
        v_new = part(u) - bbnn(part(w), s)
        outs.append(bbnn(part(q_dec), s) + bbnn(part(a_intra), v_new))
        s = s * part(g_tot) + bbtn(part(k_dec), v_new)
    return outs, s


PAIR = 2 * CH


def _gdn_specs(npair, rev):
    cc = (lambda c: npair - 1 - c) if rev else (lambda c: c)
    wide = pl.BlockSpec((PAIR, GDN_H * HD), lambda c: (cc(c), 0))
    fix = lambda off: pl.BlockSpec((PAIR, HD), lambda c: (cc(c), off))
    par = pl.BlockSpec((1, HD), lambda c: (0, 0))
    state = pl.BlockSpec((1, GDN_H, HD, HD), lambda c: (cc(c), 0, 0, 0))
    return wide, fix, par, state


def _store_heads(ref, a, rows=slice(None)):
    for h in range(a.shape[0]):
        ref[rows, h * HD:(h + 1) * HD] = a[h]


def _chunk_rows(half):
    return slice(half * CH, (half + 1) * CH)


def _chunk_valid(pair, half, pad):
    return ((2 * pair + half) * CH + _iota((CH, 1), 0)) >= pad


def _gdn_fwd(qn, kn, vn, p, alog, dtb, pad, name, cargo=(), exchange=None):
    t_pad = qn.shape[0]
    npair = t_pad // PAIR
    wide, fix, par, state = _gdn_specs(npair, False)
    n = len(cargo)

    def body(q_ref, k_ref, v_ref, bb_ref, aa_ref, al_ref, dt_ref, *rest):
        c = pl.program_id(0)
        s_ref = rest[-1]
        (o_ref, ss_ref), end_cargo = _cargo_bounds(rest[:-1], n, 2, exchange, c == 0, c == npair - 1)

        @pl.when(c == 0)
        def _():
            s_ref[...] = jnp.zeros_like(s_ref)

        s = s_ref[...]
        ss_ref[0] = s
        rows = [_chunk_rows(half) for half in (0, 1)]
        chunks = [(_heads(q_ref[r, :], GDN_H), _heads(k_ref[r, :], GDN_H), _heads(v_ref[r, :], GDN_H),
                   bb_ref[r, :], aa_ref[r, :], _chunk_valid(c, half, pad)) for half, r in enumerate(rows)]
        outs, s = _gdn_chunks(chunks, al_ref[...], dt_ref[...], s)
        for r, o in zip(rows, outs):
            _store_heads(o_ref, o, r)
        s_ref[...] = s
        end_cargo()

    return pl.pallas_call(
        body, name=name, grid=(npair,),
        in_specs=[wide, wide, wide, fix(16), fix(17), par, par] + [ANY] * n,
        out_specs=[wide, state] + [ANY] * n,
        out_shape=[jax.ShapeDtypeStruct((t_pad, GDN_H * HD), F32), jax.ShapeDtypeStruct((npair, GDN_H, HD, HD), F32)]
        + (exchange[1](cargo) if n else []),
        scratch_shapes=(exchange[2](n) if n else []) + [pltpu.VMEM((GDN_H, HD, HD), F32)],
        compiler_params=_cparams(),
    )(qn, kn, vn, p, p, alog, dtb, *cargo)


def _gdn_bwd(qn, kn, vn, p, alog, dtb, ssave, do, pad, name, cargo=(), exchange=None):
    t_pad = qn.shape[0]
    npair = t_pad // PAIR
    wide, fix, par, state = _gdn_specs(npair, True)
    n = len(cargo)

    def body(q_ref, k_ref, v_ref, bb_ref, aa_ref, al_ref, dt_ref, ss_ref, do_ref, *rest):
        c = pl.program_id(0)
        ds_ref = rest[-1]
        (dq_ref, dk_ref, dv_ref, dbb_ref, daa_ref, dal_ref, ddt_ref), end_cargo = _cargo_bounds(
            rest[:-1], n, 7, exchange, c == 0, c == npair - 1)

        @pl.when(c == 0)
        def _():
            ds_ref[...] = jnp.zeros_like(ds_ref)
            dal_ref[...] = jnp.zeros_like(dal_ref)
            ddt_ref[...] = jnp.zeros_like(ddt_ref)

        ra, rb = _chunk_rows(0), _chunk_rows(1)
        va, vb = _chunk_valid(npair - 1 - c, 0, pad), _chunk_valid(npair - 1 - c, 1, pad)

        def pair(qa, ka, va_, ba, aa, qb, kb, vb_, bb, ab, al, dt, s):
            (oa, ob), s = _gdn_chunks([(qa, ka, va_, ba, aa, va), (qb, kb, vb_, bb, ab, vb)], al, dt, s)
            return oa, ob, s

        ins = [f(ref[r, :]) for r in (ra, rb)
               for ref, f in ((q_ref, lambda a: _heads(a, GDN_H)), (k_ref, lambda a: _heads(a, GDN_H)),
                              (v_ref, lambda a: _heads(a, GDN_H)), (bb_ref, lambda a: a), (aa_ref, lambda a: a))]
        _, vjp = jax.vjp(pair, *ins, al_ref[...], dt_ref[...], ss_ref[0])
        g = vjp((_heads(do_ref[ra, :], GDN_H), _heads(do_ref[rb, :], GDN_H), ds_ref[...]))
        for r, (dq, dk, dv, dbb, daa) in ((ra, g[0:5]), (rb, g[5:10])):
            _store_heads(dq_ref, dq, r)
            _store_heads(dk_ref, dk, r)
            _store_heads(dv_ref, dv, r)
            dbb_ref[r, :] = dbb
            daa_ref[r, :] = daa
        dal_ref[...] += g[10]
        ddt_ref[...] += g[11]
        ds_ref[...] = g[12]
        end_cargo()

    sds = jax.ShapeDtypeStruct
    return pl.pallas_call(
        body, name=name, grid=(npair,),
        in_specs=[wide, wide, wide, fix(16), fix(17), par, par, state, wide] + [ANY] * n,
        out_specs=[wide, wide, wide, fix(0), fix(0), par, par] + [ANY] * n,
        out_shape=[sds((t_pad, GDN_H * HD), F32)] * 3 + [sds((t_pad, HD), F32)] * 2 + [sds((1, HD), F32)] * 2
        + (exchange[1](cargo) if n else []),
        scratch_shapes=(exchange[2](n) if n else []) + [pltpu.VMEM((GDN_H, HD, HD), F32)],
        compiler_params=_cparams(),
    )(qn, kn, vn, p, p, alog, dtb, ssave, do, *cargo)


SB_Q0, SB_K0, SB_V0 = 18, 22, 26
SB_SCALE = SB_DH ** -0.5
SB_NB = 8


def _sb_terms(z, allowed):
    nz = -z
    raw = jnp.minimum(nz, 0.0) - jnp.log(1.0 + jnp.exp(jnp.minimum(z, nz)))
    l1m = raw if allowed is None else jnp.where(allowed, raw, 0.0)
    ls = z + raw
    return l1m, ls, jnp.exp(ls)


def _sb_passes(i, step, carry):
    total = i + 1

    def sized(done, first):
        return [functools.partial(step, done, masked=({0} if first else set()) | {nb - 1}, nb=nb)
                for nb in range(1, SB_NB + 1)]

    def several(c):
        n_mid = (total - SB_NB - 1) // SB_NB
        c = step(0, c, masked={0}, nb=SB_NB)
        c = lax.fori_loop(0, n_mid, lambda t, cc: step(SB_NB * (1 + t), cc, masked=set(), nb=SB_NB), c)
        done = SB_NB * (1 + n_mid)
        return lax.switch(total - done - 1, sized(done, False), c)

    return lax.cond(total <= SB_NB, lambda c: lax.switch(total - 1, sized(0, True), c), several, carry)


def _sb_stack(a, i):
    first = _iota((1, HD), 1) < SB_DH
    a2 = jnp.concatenate([jnp.where(first, a, 0.0), jnp.where(first, 0.0, a)], axis=0).astype(BF16)
    rq = i * QB + _iota((QB, 1), 0)
    return a2, jnp.concatenate([rq, rq], axis=0), first


def _hi_lo(a):
    hi = a.astype(BF16)
    lo = (a - hi.astype(F32)).astype(BF16)
    return jnp.concatenate([hi, lo], axis=1)


def _cargo_bounds(refs, n, n_out, exchange, first, last):
    outs = refs[n:n + n_out]
    if not n:
        return outs, lambda: None
    ex = exchange[0](refs[:n], refs[n + n_out:2 * n + n_out], *refs[2 * n + n_out:])

    @pl.when(first)
    def _():
        ex.start()

    def finish():
        @pl.when(last)
        def _():
            ex.wait()

    return outs, finish


def _sb_fwd(p, pad, name, cargo=(), exchange=None):
    t_pad = p.shape[0]
    nq = t_pad // QB
    n = len(cargo)

    def body(q_ref, k_ref, v_ref, *rest):
        i = pl.program_id(1)
        pr = pl.program_id(0)
        (o_ref, r_ref), end_cargo = _cargo_bounds(rest, n, 2, exchange, (pr == 0) & (i == 0),
                                                  (pr == SB_H // 2 - 1) & (i == nq - 1))
        q2, rowq, first = _sb_stack(q_ref[...] * SB_SCALE, i)
        tri = (_iota((QB, QB), 0) > _iota((QB, QB), 1)).astype(BF16)
        upper2 = jnp.concatenate([jnp.concatenate([tri, tri], axis=0), jnp.ones((2 * QB, QB), BF16)], axis=1)

        def chain(kb, masked):
            start = pl.multiple_of(kb * QB, QB)
            kblk = k_ref[pl.ds(start, QB), :].astype(BF16)
            vblk = v_ref[pl.ds(start, QB), :].astype(BF16)
            z = lax.dot_general(q2, kblk, (NT, ((), ())), preferred_element_type=F32)
            colk = kb * QB + _iota((1, QB), 1)
            al = ((colk < rowq) & (colk >= pad)) if masked else None
            l1m, ls, _ = _sb_terms(z, al)
            sums = lax.dot_general(_hi_lo(l1m), upper2, (NN, ((), ())), preferred_element_type=F32)
            return al, ls, sums[:, :QB], sums[:, QB:], vblk

        def step(done, carry, masked, nb):
            o_acc, run = carry
            ws, vs = [], []
            for n in range(nb):
                al, ls, suf, rs, vblk = chain(i - done - n, n in masked)
                wgt = jnp.exp(ls + suf + run)
                ws.append((wgt if al is None else jnp.where(al, wgt, 0.0)).astype(BF16))
                vs.append(vblk)
                run = run + rs
            o_acc = o_acc + lax.dot_general(jnp.concatenate(ws, axis=1), jnp.concatenate(vs, axis=0),
                                            (NN, ((), ())), preferred_element_type=F32)
            return o_acc, run

        o_acc, run = _sb_passes(i, step, (jnp.zeros((2 * QB, HD), F32), jnp.zeros((2 * QB, QB), F32)))
        o_ref[...] = jnp.where(first, o_acc[:QB], o_acc[QB:]).astype(BF16)
        r_ref[...] = jnp.where(first, run[:QB], run[QB:])
        end_cargo()

    full = lambda off: pl.BlockSpec((t_pad, HD), lambda pr, i: (0, off + pr))
    blk = pl.BlockSpec((QB, HD), lambda pr, i: (i, pr))
    return pl.pallas_call(
        body, name=name, grid=(SB_H // 2, nq),
        in_specs=[pl.BlockSpec((QB, HD), lambda pr, i: (i, SB_Q0 + pr)), full(SB_K0), full(SB_V0)] + [ANY] * n,
        out_specs=[blk, blk] + [ANY] * n,
        out_shape=[jax.ShapeDtypeStruct((t_pad, SB_H * SB_DH), BF16), jax.ShapeDtypeStruct((t_pad, SB_H * SB_DH), F32)]
        + (exchange[1](cargo) if n else []),
        scratch_shapes=exchange[2](n) if n else [],
        compiler_params=_cparams(),
    )(p, p, p, *cargo)


def _sb_bwd(p, rtot, dy, dy_blk0, pad, name, cargo=(), exchange=None):
    t_pad = p.shape[0]
    nq = t_pad // QB
    n = len(cargo)

    def body(q_ref, k_ref, v_ref, r_ref, do_ref, *rest):
        i = pl.program_id(1)
        pr = pl.program_id(0)
        dkt_ref, dvt_ref = rest[-2:]
        (dq_ref, dk_ref, dv_ref), end_cargo = _cargo_bounds(rest[:-2], n, 3, exchange, (pr == 0) & (i == 0),
                                                            (pr == SB_H // 2 - 1) & (i == nq - 1))

        @pl.when(i == 0)
        def _():
            dkt_ref[...] = jnp.zeros_like(dkt_ref)
            dvt_ref[...] = jnp.zeros_like(dvt_ref)

        q2, rowq, first = _sb_stack(q_ref[...] * SB_SCALE, i)
        do2, _, _ = _sb_stack(do_ref[...], i)
        q2t = jnp.transpose(q2.astype(F32)).astype(BF16)
        do2t = jnp.transpose(do2.astype(F32)).astype(BF16)
        rt = r_ref[...]
        lane = _iota((1, HD), 1)
        rcol = jnp.concatenate([jnp.sum(jnp.where(lane == 0, rt, 0.0), axis=1, keepdims=True),
                                jnp.sum(jnp.where(lane == SB_DH, rt, 0.0), axis=1, keepdims=True)], axis=0)
        rj = _iota((QB, QB), 0)
        cs = _iota((QB, QB), 1)
        tri_u = (rj > cs).astype(BF16)
        tri_l = (rj < cs).astype(BF16)
        ones2 = jnp.ones((2 * QB, QB), BF16)
        upper2 = jnp.concatenate([jnp.concatenate([tri_u, tri_u], axis=0), ones2], axis=1)
        lower2 = jnp.concatenate([jnp.concatenate([tri_l, tri_l], axis=0), ones2], axis=1)
        rcol = jnp.broadcast_to(rcol, (2 * QB, QB))

        def chain(kb, masked):
            start = pl.multiple_of(kb * QB, QB)
            kblk = k_ref[pl.ds(start, QB), :].astype(BF16)
            vblk = v_ref[pl.ds(start, QB), :].astype(BF16)
            z = lax.dot_general(q2, kblk, (NT, ((), ())), preferred_element_type=F32)
            colk = kb * QB + _iota((1, QB), 1)
            al = ((colk < rowq) & (colk >= pad)) if masked else None
            l1m, ls, sg = _sb_terms(z, al)
            dwgt = lax.dot_general(do2, vblk, (NT, ((), ())), preferred_element_type=F32)
            sums = lax.dot_general(_hi_lo(l1m), upper2, (NN, ((), ())), preferred_element_type=F32)
            return kb, kblk, al, ls, sums[:, :QB], sums[:, QB:], dwgt, sg

        def finish(c, left, gseen):
            kb, kblk, al, ls, suf, rs, dwgt, sg = c
            left = left - rs
            wgt = jnp.exp(ls + suf + left)
            if al is not None:
                wgt = jnp.where(al, wgt, 0.0)
            dl = dwgt * wgt
            sums = lax.dot_general(_hi_lo(dl), lower2, (NN, ((), ())), preferred_element_type=F32)
            gpre = gseen + sums[:, :QB]
            dz = dl - sg * (dl + gpre)
            if al is not None:
                dz = jnp.where(al, dz, 0.0)
            dz = dz.astype(BF16)
            dkt_ref[kb] += lax.dot_general(q2t, dz, (NN, ((), ())), preferred_element_type=F32)
            dvt_ref[kb] += lax.dot_general(do2t, wgt.astype(BF16), (NN, ((), ())), preferred_element_type=F32)
            return dz, left, gseen + sums[:, QB:]

        def step(done, carry, masked, nb):
            dq_acc, left, gseen = carry
            cs_ = [chain(done + n, n in masked) for n in range(nb)]
            dzs = []
            for c in cs_:
                dz, left, gseen = finish(c, left, gseen)
                dzs.append(dz)
            dq_acc = dq_acc + lax.dot_general(jnp.concatenate(dzs, axis=1), jnp.concatenate([c[1] for c in cs_], axis=0),
                                              (NN, ((), ())), preferred_element_type=F32)
            return dq_acc, left, gseen

        dq_acc, _, _ = _sb_passes(i, step, (jnp.zeros((2 * QB, HD), F32), rcol, jnp.zeros((2 * QB, QB), F32)))
        dq_ref[...] = jnp.where(first, dq_acc[:QB], dq_acc[QB:]) * SB_SCALE

        @pl.when(i == nq - 1)
        def _():
            for kb in range(nq):
                dk_ref[kb * QB:(kb + 1) * QB, :] = jnp.transpose(dkt_ref[kb])
                dv_ref[kb * QB:(kb + 1) * QB, :] = jnp.transpose(dvt_ref[kb])

        end_cargo()

    full_in = lambda off: pl.BlockSpec((t_pad, HD), lambda pr, i: (0, off + pr))
    full_out = pl.BlockSpec((t_pad, HD), lambda pr, i: (0, pr))
    blk = pl.BlockSpec((QB, HD), lambda pr, i: (i, pr))
    sds = jax.ShapeDtypeStruct((t_pad, SB_H * SB_DH), F32)
    return pl.pallas_call(
        body, name=name, grid=(SB_H // 2, nq),
        in_specs=[pl.BlockSpec((QB, HD), lambda pr, i: (i, SB_Q0 + pr)), full_in(SB_K0), full_in(SB_V0), blk,
                  pl.BlockSpec((QB, HD), lambda pr, i: (i, dy_blk0 + pr))] + [ANY] * n,
        out_specs=[blk, full_out, full_out] + [ANY] * n,
        out_shape=[sds, sds, sds] + (exchange[1](cargo) if n else []),
        scratch_shapes=(exchange[2](n) if n else []) + [pltpu.VMEM((nq, HD, QB), F32)] * 2,
        compiler_params=_cparams(),
    )(p, p, p, rtot, dy, *cargo)


HG_LEVELS = 6


def _hg_prefix_matrix():
    t = np.arange(CH)[:, None]
    j = np.arange(CH)[None, :]
    groups = [(j <= t)]
    for lvl in range(1, HG_LEVELS + 1):
        half = CH >> lvl
        e = (t // (2 * half)) * (2 * half) + half - 1
        groups.append(j <= e)
    groups.append(np.ones((8, CH), bool))
    e = np.concatenate(groups, axis=0).astype(np.float32)
    return np.concatenate([e, e, e], axis=1), np.concatenate([e, e, e], axis=0)


HG_G = 4


def _hg_chunk(qr, fr, iv, r0, r1, st, valid, ecat):
    g = st.shape[0]
    mx = jnp.maximum(r0, r1)
    e0 = jnp.exp(r0 - mx)
    e1 = jnp.exp(r1 - mx)
    lb = e1 / (e0 + e1)
    fg = lb + (1.0 - lb) * _sigmoid(fr)
    logf = jnp.where(valid, jnp.log(fg), 0.0)
    kk = jnp.where(valid, 1.0 - fg, 0.0)
    q = jnp.where(valid, _silu(qr), 0.0)
    v = _heads(jnp.where(valid, iv, 0.0), g)

    pre = _mask_dot(ecat, logf)
    b = pre[0:CH]
    b_last = jnp.max(pre[(HG_LEVELS + 1) * CH:], axis=0, keepdims=True)
    row = _iota((CH, 1), 0)
    ri = _iota((1, CH, CH), 1)
    ci = _iota((1, CH, CH), 2)
    a = jnp.where(ri == ci, jnp.sum(_heads(q * kk, g), axis=2, keepdims=True), 0.0)
    for lvl in range(1, HG_LEVELS + 1):
        half = CH >> lvl
        m = pre[lvl * CH:(lvl + 1) * CH]
        low = (row & half) != 0
        dec = jnp.exp(jnp.where(low, b - m, m - b))
        qt = jnp.where(low, q * dec, 0.0)
        kt = jnp.where(low, 0.0, kk * dec)
        same = (ri >> (7 - lvl)) == (ci >> (7 - lvl))
        a = a + jnp.where(same, bbnt(_heads(qt, g), _heads(kt, g)), 0.0)
    o = bbnt(_heads(q * jnp.exp(b), g), st) + bbnn(a, v)
    kd = kk * jnp.exp(b_last - b)
    st_new = st * _heads(jnp.exp(b_last), g) + bbtn(v, _heads(kd, g))
    return o, st_new


def _hg_specs(npair, rev):
    cc = (lambda c: npair - 1 - c) if rev else (lambda c: c)
    ng = HG_H // HG_G
    blk = lambda off: pl.BlockSpec((PAIR, HG_G * HD), lambda h, c: (cc(c), off * ng + h))
    lbs = pl.BlockSpec((2, HG_G * HD), lambda h, c: (0, h))
    state = pl.BlockSpec((1, HG_G, HD, HD), lambda h, c: (cc(c), h, 0, 0))
    return ng, blk, lbs, state


def _hg_fwd(p, lbraw, ecat, pad, name):
    t_pad = p.shape[0]
    npair = t_pad // PAIR
    ng, blk, lbs, state = _hg_specs(npair, False)

    def body(q_ref, f_ref, i_ref, lb_ref, e_ref, et_ref, o_ref, ss_ref, s_ref):
        c = pl.program_id(1)

        @pl.when(c == 0)
        def _():
            s_ref[...] = jnp.zeros_like(s_ref)

        st = s_ref[...]
        ss_ref[0] = st
        for half in (0, 1):
            r = _chunk_rows(half)
            o, st = _hg_chunk(q_ref[r, :], f_ref[r, :], i_ref[r, :], lb_ref[0:1, :], lb_ref[1:2, :], st,
                              _chunk_valid(c, half, pad), (e_ref[...], et_ref[...]))
            _store_heads(o_ref, o, r)
        s_ref[...] = st

    return pl.pallas_call(
        body, name=name, grid=(ng, npair),
        in_specs=[blk(0), blk(1), blk(2), lbs] + [pl.BlockSpec(e.shape, lambda h, c: (0, 0)) for e in ecat],
        out_specs=[blk(0), state],
        out_shape=[jax.ShapeDtypeStruct((t_pad, HG_H * HD), F32), jax.ShapeDtypeStruct((npair, HG_H, HD, HD), F32)],
        scratch_shapes=[pltpu.VMEM((HG_G, HD, HD), F32)],
        compiler_params=_cparams(),
    )(p, p, p, lbraw, *ecat)


def _hg_bwd(p, lbraw, ecat, ssave, do, pad, name, cargo=(), exchange=None):
    t_pad = p.shape[0]
    npair = t_pad // PAIR
    ng, blk, lbs, state = _hg_specs(npair, True)
    n = len(cargo)

    def body(q_ref, f_ref, i_ref, lb_ref, e_ref, et_ref, ss_ref, do_ref, *rest):
        c = pl.program_id(1)
        hg = pl.program_id(0)
        ds_ref = rest[-1]
        (dq_ref, df_ref, di_ref, dlb_ref), end_cargo = _cargo_bounds(
            rest[:-1], n, 4, exchange, (hg == 0) & (c == 0), (hg == ng - 1) & (c == npair - 1))

        @pl.when(c == 0)
        def _():
            ds_ref[...] = jnp.zeros_like(ds_ref)
            dlb_ref[...] = jnp.zeros_like(dlb_ref)

        ra, rb = _chunk_rows(0), _chunk_rows(1)
        va, vb = _chunk_valid(npair - 1 - c, 0, pad), _chunk_valid(npair - 1 - c, 1, pad)
        ecv = (e_ref[...], et_ref[...])

        def pair(qa, fa, ia, qb, fb, ib, r0, r1, st):
            oa, st = _hg_chunk(qa, fa, ia, r0, r1, st, va, ecv)
            ob, st = _hg_chunk(qb, fb, ib, r0, r1, st, vb, ecv)
            return oa, ob, st

        ins = [ref[r, :] for r in (ra, rb) for ref in (q_ref, f_ref, i_ref)]
        _, vjp = jax.vjp(pair, *ins, lb_ref[0:1, :], lb_ref[1:2, :], ss_ref[0])
        g = vjp((_heads(do_ref[ra, :], HG_G), _heads(do_ref[rb, :], HG_G), ds_ref[...]))
        for r, (dq, df, di) in ((ra, g[0:3]), (rb, g[3:6])):
            dq_ref[r, :] = dq.astype(BF16)
            df_ref[r, :] = df.astype(BF16)
            di_ref[r, :] = di.astype(BF16)
        dlb_ref[0:1, :] += g[6]
        dlb_ref[1:2, :] += g[7]
        ds_ref[...] = g[8]
        end_cargo()

    sds = jax.ShapeDtypeStruct((t_pad, HG_H * HD), BF16)
    return pl.pallas_call(
        body, name=name, grid=(ng, npair),
        in_specs=[blk(0), blk(1), blk(2), lbs] + [pl.BlockSpec(e.shape, lambda h, c: (0, 0)) for e in ecat]
        + [state, blk(0)] + [ANY] * n,
        out_specs=[blk(0), blk(0), blk(0), lbs] + [ANY] * n,
        out_shape=[sds, sds, sds, jax.ShapeDtypeStruct((2, HG_H * HD), F32)] + (exchange[1](cargo) if n else []),
        scratch_shapes=(exchange[2](n) if n else []) + [pltpu.VMEM((HG_G, HD, HD), F32)],
        compiler_params=_cparams(),
    )(p, p, p, lbraw, *ecat, ssave, do, *cargo)


def _pad_ab_cols(w):
    z = jnp.zeros((w.shape[0], HD - GDN_H), w.dtype)
    return jnp.concatenate([w[:, :2048], w[:, 2048:2052], z, w[:, 2052:2056], z, w[:, 2056:]], axis=1)


def _unpad_ab_cols(w):
    return jnp.concatenate([w[:, :2048], w[:, 2048:2052], w[:, 2176:2180], w[:, 2304:]], axis=1)


def _lane_pad(v):
    return jnp.pad(v, ((0, 0), (0, HD - v.shape[1])))


def _mlp_fwd(h, hb, w1, w2, layer, g, b):
    a, r = _mm(hb, w1, b_view=("cols", layer), out_dtype=BF16, act=True, name=f"mlp_up_{layer}")
    m, y, yb = _mm(r, w2, b_view=("rows", layer), ln=(h, g, b), name=f"mlp_down_{layer}")
    return a, r, m, y, yb


def _mlp_bwd(hb, a, r, dmb, w1, w2, layer):
    da = _mm(dmb, w2, tb=True, b_view=("rows", layer), out_dtype=BF16, gate=a, name=f"mlp_down_dx_{layer}")
    dw2 = _mm(r, dmb, ta=True, out_dtype=BF16, name=f"mlp_down_dw_{layer}")
    dh = _mm(da, w1, tb=True, b_view=("cols", layer), name=f"mlp_up_dx_{layer}")
    dw1 = _mm(hb, da, ta=True, out_dtype=BF16, out_split=N_CHIP, name=f"mlp_up_dw_{layer}")
    return dh, dw1, dw2


def _local_step(h0, tgt, w, pad, late=None):
    row = lambda a, i: a[i:i + 1]
    ecat = tuple(jnp.asarray(e, dtype=BF16) for e in _hg_prefix_matrix())
    cw = [w["conv_w"][:, i * 512:(i + 1) * 512] for i in range(3)]
    alog, dtb = _lane_pad(w["a_log"]), _lane_pad(w["dt_bias"])

    h0b = h0.astype(BF16)
    p0 = _mm(h0b, w["ab_w_in"], name="ab_in")
    qn = _conv_fwd(p0, 0, cw[0], "q", pad, "conv_q")
    kn = _conv_fwd(p0, 4, cw[1], "k", pad, "conv_k")
    vn = _conv_fwd(p0, 8, cw[2], "v", pad, "conv_v")
    if late is None:
        oa_raw, ss0 = _gdn_fwd(qn, kn, vn, p0, alog, dtb, pad, "gdn_fwd")
        ob, rtot = _sb_fwd(p0, pad, "sb_fwd")
    else:
        oa_raw, ss0, g_cin, g_cout = _gdn_fwd(qn, kn, vn, p0, alog, dtb, pad, "gdn_fwd",
                                              cargo=[late["c_w_in"], late["c_w_out"]], exchange=GATHER)
        ob, rtot, g_about, g_w1, g_w2 = _sb_fwd(p0, pad, "sb_fwd", exchange=GATHER,
                                                cargo=[late["ab_w_out"], late["mlp_w1"], late["mlp_w2"]])
        w = dict(w, ab_w_out=g_about.reshape(D, D), c_w_in=g_cin, c_w_out=g_cout.reshape(D, D), mlp_w1=g_w1, mlp_w2=g_w2)
    oa = _grms_fwd(oa_raw, p0, 12, w["ab_gnorm_g"], "gdn_gate")
    ycat = jnp.concatenate([oa, ob], axis=1)
    mix0, h1, h1b = _mm(ycat, w["ab_w_out"], name="ab_out", ln=(h0, row(w["ln_mix_g"], 0), row(w["ln_mix_b"], 0)))
    a0, r0, m0, h2, h2b = _mlp_fwd(h1, h1b, w["mlp_w1"], w["mlp_w2"], 0, row(w["ln_ffn_g"], 0), row(w["ln_ffn_b"], 0))
    p1 = _mm(h2b, w["c_w_in"], b_view=("cols", 0), name="c_in")
    oc_raw, ss1 = _hg_fwd(p1, w["c_lb_raw"], ecat, pad, "hg_fwd")
    yc = _grms_fwd(oc_raw, p1, 3 * HG_H, w["c_gnorm_g"], "hg_gate")
    mix1, h3, h3b = _mm(yc, w["c_w_out"], name="c_out", ln=(h2, row(w["ln_mix_g"], 1), row(w["ln_mix_b"], 1)))
    a1, r1, m1, h4, _ = _mlp_fwd(h3, h3b, w["mlp_w1"], w["mlp_w2"], 1, row(w["ln_ffn_g"], 1), row(w["ln_ffn_b"], 1))
    loss, dh4 = _loss_fwd(h4, tgt, pad + N_META, "loss")

    dh3a, dm1b, dfg1, dfb1 = _ln_res_bwd(h3, m1, row(w["ln_ffn_g"], 1), row(w["ln_ffn_b"], 1), [dh4], "ln_ffn_bwd_1")
    dh3b, dw1_1, dw2_1 = _mlp_bwd(h3b, a1, r1, dm1b, w["mlp_w1"], w["mlp_w2"], 1)
    dh2a, dmix1b, dmg1, dmb1 = _ln_res_bwd(h2, mix1, row(w["ln_mix_g"], 1), row(w["ln_mix_b"], 1), [dh3a, dh3b], "ln_mix_bwd_1")
    dyc = _mm(dmix1b, w["c_w_out"], tb=True, name="c_out_dx")
    dwco = _mm(yc, dmix1b, ta=True, out_dtype=BF16, name="c_out_dw")
    doc, dzc, dcg = _grms_bwd(oc_raw, p1, 3 * HG_H, w["c_gnorm_g"], dyc, 0, "hg_gate_bwd")
    landed = {}
    rows4 = lambda a: a.reshape(N_CHIP, -1, D)
    if late is None:
        dq1, df1, di1, dlb = _hg_bwd(p1, w["c_lb_raw"], ecat, ss1, doc, pad, "hg_bwd")
    else:
        dq1, df1, di1, dlb, landed["w1_1"] = _hg_bwd(
            p1, w["c_lb_raw"], ecat, ss1, doc, pad, "hg_bwd", cargo=[dw1_1], exchange=SCATTER)
    dp1 = [dq1, df1, di1, dzc]
    dh2b = _mm_groups_nt(dp1, w["c_w_in"], "c_in_dx")
    dwc = jnp.stack([_mm(h2b, d, ta=True, out_dtype=BF16, name=f"c_in_dw_{i}") for i, d in enumerate(dp1)])
    dh1a, dm0b, dfg0, dfb0 = _ln_res_bwd(h1, m0, row(w["ln_ffn_g"], 0), row(w["ln_ffn_b"], 0), [dh2a, dh2b], "ln_ffn_bwd_0")
    dh1b, dw1_0, dw2_0 = _mlp_bwd(h1b, a0, r0, dm0b, w["mlp_w1"], w["mlp_w2"], 0)
    dh0a, dmix0b, dmg0, dmb0 = _ln_res_bwd(h0, mix0, row(w["ln_mix_g"], 0), row(w["ln_mix_b"], 0), [dh1a, dh1b], "ln_mix_bwd_0")
    dycat = _mm(dmix0b, w["ab_w_out"], tb=True, name="ab_out_dx")
    dwabo = _mm(ycat, dmix0b, ta=True, out_dtype=BF16, name="ab_out_dw")
    doa, dza, dag = _grms_bwd(oa_raw, p0, 12, w["ab_gnorm_g"], dycat, 0, "gdn_gate_bwd")
    if late is None:
        dqn, dkn, dvn, dbb, daa, dal, ddt = _gdn_bwd(qn, kn, vn, p0, alog, dtb, ss0, doa, pad, "gdn_bwd")
        dqb, dkb, dvb = _sb_bwd(p0, rtot, dycat, 4, pad, "sb_bwd")
    else:
        dqn, dkn, dvn, dbb, daa, dal, ddt, landed["c_w_in"] = _gdn_bwd(
            qn, kn, vn, p0, alog, dtb, ss0, doa, pad, "gdn_bwd", cargo=[dwc], exchange=SCATTER)
        (dqb, dkb, dvb, landed["w1_0"], landed["w2_0"], landed["w2_1"], landed["ab_w_out"],
         landed["c_w_out"]) = _sb_bwd(
            p0, rtot, dycat, 4, pad, "sb_bwd",
            cargo=[dw1_0, rows4(dw2_0), rows4(dw2_1), rows4(dwabo), rows4(dwco)], exchange=SCATTER)
    dpq, dcq = _conv_bwd(p0, 0, cw[0], dqn, "q", pad, "conv_q_bwd")
    dpk, dck = _conv_bwd(p0, 4, cw[1], dkn, "k", pad, "conv_k_bwd")
    dpv, dcv = _conv_bwd(p0, 8, cw[2], dvn, "v", pad, "conv_v_bwd")
    dp0 = _assemble_bf16([dpq, dpk, dpv, dza, dbb, daa, dqb, dkb, dvb], "ab_in_dy")
    dwab = _mm(h0b, dp0, ta=True, out_dtype=BF16, name="ab_in_dw")
    if late is None:
        dh0 = _mm(dp0, w["ab_w_in"], tb=True, plus=dh0a, name="ab_in_dx")
    else:
        dab = jnp.transpose(_unpad_ab_cols(dwab).reshape(D, N_CHIP, AB_TRUE // N_CHIP), (1, 0, 2))
        dh0, landed["ab_w_in"] = _mm(dp0, w["ab_w_in"], tb=True, plus=dh0a, name="ab_in_dx", cargo=[dab],
                                     exchange=SCATTER)

    grads = {
        "ab_w_in": dwab, "conv_w": jnp.concatenate([dcq, dck, dcv], axis=1),
        "a_log": dal[:, :GDN_H], "dt_bias": ddt[:, :GDN_H],
        "ab_gnorm_g": dag, "ab_w_out": dwabo, "c_w_in": dwc, "c_lb_raw": dlb, "c_gnorm_g": dcg, "c_w_out": dwco,
        "ln_mix_g": jnp.concatenate([dmg0, dmg1], 0), "ln_mix_b": jnp.concatenate([dmb0, dmb1], 0),
        "w1_0": dw1_0, "w1_1": dw1_1, "w2_0": dw2_0, "w2_1": dw2_1,
        "ln_ffn_g": jnp.concatenate([dfg0, dfg1], 0), "ln_ffn_b": jnp.concatenate([dfb0, dfb1], 0),
        "landed": landed,
    }
    return loss, dh0, grads


MESH = pl.DeviceIdType.MESH
ANY = pl.BlockSpec(memory_space=pl.ANY)
N_CHIP = 4
N_DEV = 8
CHIP_REL = ((1, 0), (0, 1), (1, 1))
DEV_REL = tuple((dx, dy, dc) for dx in (0, 1) for dy in (0, 1) for dc in (0, 1))[1:]

def _pos():
    return lax.axis_index("x"), lax.axis_index("y"), lax.axis_index("c")


def _flip(a, d):
    return a + d - 2 * a * d


class _Exchange:
    def __init__(self, local, sends, recvs):
        self.local, self.sends, self.recvs = local, sends, recvs

    def start(self):
        for cp in self.local + self.sends:
            cp.start()

    def wait(self):
        for cp in self.recvs:
            cp.wait_recv()
        for cp in self.sends:
            cp.wait_send()
        for cp in self.local:
            cp.wait()


def _gather_sems(n):
    return [pltpu.SemaphoreType.DMA((3 * n,)), pltpu.SemaphoreType.DMA((3 * n,)), pltpu.SemaphoreType.DMA((n,))]


def _gather_copies(x_refs, o_refs, send_sems, recv_sems, local_sems):
    n = len(x_refs)
    x, y, c = _pos()
    local = [pltpu.make_async_copy(x_refs[a], o_refs[a].at[2 * x + y], local_sems.at[a]) for a in range(n)]

    def copy(a, k, sending):
        tx, ty = _flip(x, CHIP_REL[k][0]), _flip(y, CHIP_REL[k][1])
        return pltpu.make_async_remote_copy(
            src_ref=x_refs[a], dst_ref=o_refs[a].at[2 * x + y if sending else 2 * tx + ty],
            send_sem=send_sems.at[3 * a + k], recv_sem=recv_sems.at[3 * a + k], device_id=(tx, ty, c), device_id_type=MESH)

    pairs = [(a, k) for a in range(n) for k in range(3)]
    return _Exchange(local, [copy(a, k, True) for a, k in pairs], [copy(a, k, False) for a, k in pairs])


def _gather_shapes(bufs):
    return [jax.ShapeDtypeStruct((N_CHIP,) + b.shape, b.dtype) for b in bufs]


def _chip_allgather(bufs, name):
    n = len(bufs)

    def body(*refs):
        ex = _gather_copies(refs[:n], refs[n:2 * n], *refs[2 * n:])
        ex.start()
        ex.wait()

    return pl.pallas_call(
        body, name=name, in_specs=[ANY] * n, out_specs=[ANY] * n, out_shape=_gather_shapes(bufs),
        scratch_shapes=_gather_sems(n), compiler_params=pltpu.CompilerParams(has_side_effects=True),
    )(*bufs)


def _scatter_sems(n):
    nr = N_DEV - 1
    return [pltpu.SemaphoreType.DMA((nr * n,)), pltpu.SemaphoreType.DMA((nr * n,)), pltpu.SemaphoreType.DMA((n,))]


def _scatter_copies(g_refs, o_refs, send_sems, recv_sems, local_sems):
    n = len(g_refs)
    nr = N_DEV - 1
    x, y, c = _pos()
    me = 4 * x + 2 * y + c
    local = [pltpu.make_async_copy(g_refs[a].at[2 * x + y], o_refs[a].at[me], local_sems.at[a]) for a in range(n)]

    def copy(a, k, sending):
        dx, dy, dc = DEV_REL[k]
        tx, ty, tc = _flip(x, dx), _flip(y, dy), _flip(c, dc)
        return pltpu.make_async_remote_copy(
            src_ref=g_refs[a].at[2 * tx + ty], dst_ref=o_refs[a].at[me if sending else 4 * tx + 2 * ty + tc],
            send_sem=send_sems.at[nr * a + k], recv_sem=recv_sems.at[nr * a + k],
            device_id=(tx, ty, tc), device_id_type=MESH)

    pairs = [(a, k) for a in range(n) for k in range(nr)]
    return _Exchange(local, [copy(a, k, True) for a, k in pairs], [copy(a, k, False) for a, k in pairs])


def _scatter_shapes(gs):
    return [jax.ShapeDtypeStruct((N_DEV,) + g.shape[1:], g.dtype) for g in gs]


GATHER = (_gather_copies, _gather_shapes, _gather_sems)
SCATTER = (_scatter_copies, _scatter_shapes, _scatter_sems)


def _sum_slots(rs, name):
    n, rh, w = rs[0].shape
    tr = _pick(rh, (256, 128, 64, 16))

    def body(*refs):
        o_ref = refs[-1]
        for layer, r_ref in enumerate(refs[:-1]):
            acc = r_ref[0].astype(F32)
            for s in range(1, n):
                acc = acc + r_ref[s].astype(F32)
            o_ref[layer] = acc

    return pl.pallas_call(
        body, name=name, grid=(rh // tr,), in_specs=[pl.BlockSpec((n, tr, w), lambda i: (0, i, 0))] * len(rs),
        out_specs=pl.BlockSpec((len(rs), tr, w), lambda i: (0, i, 0)),
        out_shape=jax.ShapeDtypeStruct((len(rs), rh, w), F32), compiler_params=_cparams(),
    )(*rs)


def _small_allreduce(buf, name):
    r, w = buf.shape

    def body(b_ref, o_ref, land_ref, send_sems, recv_sems):
        x, y, c = _pos()
        me = 4 * x + 2 * y + c
        land_ref[me] = b_ref[...]

        def target(k):
            dx, dy, dc = DEV_REL[k]
            return _flip(x, dx), _flip(y, dy), _flip(c, dc)

        sends = []
        for k in range(N_DEV - 1):
            tx, ty, tc = target(k)
            cp = pltpu.make_async_remote_copy(
                src_ref=b_ref, dst_ref=land_ref.at[me], send_sem=send_sems.at[k], recv_sem=recv_sems.at[k],
                device_id=(tx, ty, tc), device_id_type=MESH)
            cp.start()
            sends.append(cp)
        for k in range(N_DEV - 1):
            tx, ty, tc = target(k)
            pltpu.make_async_remote_copy(
                src_ref=b_ref, dst_ref=land_ref.at[4 * tx + 2 * ty + tc], send_sem=send_sems.at[k],
                recv_sem=recv_sems.at[k], device_id=(tx, ty, tc), device_id_type=MESH).wait_recv()
        for cp in sends:
            cp.wait_send()
        acc = land_ref[0]
        for s in range(1, N_DEV):
            acc = acc + land_ref[s]
        o_ref[...] = acc

    vm = pl.BlockSpec(memory_space=pltpu.VMEM)
    return pl.pallas_call(
        body, name=name, in_specs=[vm], out_specs=vm, out_shape=jax.ShapeDtypeStruct((r, w), F32),
        scratch_shapes=[pltpu.VMEM((N_DEV, r, w), F32), pltpu.SemaphoreType.DMA((N_DEV - 1,)),
                        pltpu.SemaphoreType.DMA((N_DEV - 1,))],
        compiler_params=pltpu.CompilerParams(has_side_effects=True),
    )(buf)


def _adamw(w, g, m, v, name):
    r, c = w.shape
    tr = _pick(r, (256, 128, 64, 8)) if r * c > (1 << 18) else r

    def body(w_ref, g_ref, m_ref, v_ref, d_ref, m2_ref, v2_ref):
        gg = g_ref[...]
        m2 = ADAM_B1 * m_ref[...] + (1.0 - ADAM_B1) * gg
        v2 = ADAM_B2 * v_ref[...] + (1.0 - ADAM_B2) * (gg * gg)
        m_hat = m2 / (1.0 - ADAM_B1 ** ADAM_STEP)
        v_hat = v2 / (1.0 - ADAM_B2 ** ADAM_STEP)
        d_ref[...] = -ADAM_LR * (m_hat / (jnp.sqrt(v_hat) + ADAM_EPS) + ADAM_WD * w_ref[...])
        m2_ref[...] = m2
        v2_ref[...] = v2

    blk = pl.BlockSpec((tr, c), lambda i: (i, 0))
    sds = jax.ShapeDtypeStruct((r, c), F32)
    return pl.pallas_call(body, name=name, grid=(r // tr,), in_specs=[blk] * 4, out_specs=[blk] * 3,
                          out_shape=[sds] * 3, compiler_params=_cparams())(w, g, m, v)


BIG = ("ab_w_in", "ab_w_out", "c_w_in", "c_w_out", "mlp_w1", "mlp_w2")
SMALL = ("ln_mix_g", "ln_mix_b", "ln_ffn_g", "ln_ffn_b", "c_lb_raw", "ab_a_log", "ab_dt_bias", "ab_gnorm_g", "c_gnorm_g")
SMALL_ROWS = 16
CONV_ROWS = 8
CONV_W = 3 * GDN_H * HD


def _conv_to_rows(cw):
    return jnp.pad(cw, ((0, 0), (0, 2 * D - CONV_W))).reshape(CONV_ROWS, D)


def _rows_to_conv(rows):
    return rows.reshape(CONV_K, 2 * D)[:, :CONV_W]


def _pack_small(d):
    rows = [jnp.pad(d[n], ((0, 0), (0, D - d[n].shape[1]))) for n in SMALL]
    buf = jnp.concatenate(rows, axis=0)
    return jnp.pad(buf, ((0, SMALL_ROWS - buf.shape[0]), (0, 0)))


def _unpack_small(buf, like):
    out, r = {}, 0
    for n in SMALL:
        nr, nc = like[n].shape
        out[n] = buf[r:r + nr, :nc]
        r += nr
    return out


def kernel(x, meta_tokens, ab_w_in, ab_conv_w, ab_a_log, ab_dt_bias, ab_gnorm_g, ab_w_out, c_w_in, c_lb_raw, c_gnorm_g, c_w_out, ln_mix_g, ln_mix_b, mlp_w1, mlp_w2, ln_ffn_g, ln_ffn_b, loss_target, m_meta_tokens, m_ab_w_in, m_ab_conv_w, m_ab_a_log, m_ab_dt_bias, m_ab_gnorm_g, m_ab_w_out, m_c_w_in, m_c_lb_raw, m_c_gnorm_g, m_c_w_out, m_ln_mix_g, m_ln_mix_b, m_mlp_w1, m_mlp_w2, m_ln_ffn_g, m_ln_ffn_b, v_meta_tokens, v_ab_w_in, v_ab_conv_w, v_ab_a_log, v_ab_dt_bias, v_ab_gnorm_g, v_ab_w_out, v_c_w_in, v_c_lb_raw, v_c_gnorm_g, v_c_w_out, v_ln_mix_g, v_ln_mix_b, v_mlp_w1, v_mlp_w2, v_ln_ffn_g, v_ln_ffn_b):
    names = ("meta_tokens", "ab_w_in", "ab_conv_w", "ab_a_log", "ab_dt_bias", "ab_gnorm_g", "ab_w_out", "c_w_in",
             "c_lb_raw", "c_gnorm_g", "c_w_out", "ln_mix_g", "ln_mix_b", "mlp_w1", "mlp_w2", "ln_ffn_g", "ln_ffn_b")
    wts = dict(zip(names, (meta_tokens, ab_w_in, ab_conv_w, ab_a_log, ab_dt_bias, ab_gnorm_g, ab_w_out, c_w_in, c_lb_raw,
                           c_gnorm_g, c_w_out, ln_mix_g, ln_mix_b, mlp_w1, mlp_w2, ln_ffn_g, ln_ffn_b)))
    mom_m = dict(zip(names, (m_meta_tokens, m_ab_w_in, m_ab_conv_w, m_ab_a_log, m_ab_dt_bias, m_ab_gnorm_g, m_ab_w_out,
                             m_c_w_in, m_c_lb_raw, m_c_gnorm_g, m_c_w_out, m_ln_mix_g, m_ln_mix_b, m_mlp_w1, m_mlp_w2,
                             m_ln_ffn_g, m_ln_ffn_b)))
    mom_v = dict(zip(names, (v_meta_tokens, v_ab_w_in, v_ab_conv_w, v_ab_a_log, v_ab_dt_bias, v_ab_gnorm_g, v_ab_w_out,
                             v_c_w_in, v_c_lb_raw, v_c_gnorm_g, v_c_w_out, v_ln_mix_g, v_ln_mix_b, v_mlp_w1, v_mlp_w2,
                             v_ln_ffn_g, v_ln_ffn_b)))
    seq = x.shape[1]
    pad = (-(N_META + seq)) % QB
    xi, yi, ci = _pos()
    chip = 2 * xi + yi

    gat_ab_in, = _chip_allgather([ab_w_in[0].astype(BF16)], "gather_weights")
    late = {"ab_w_out": ab_w_out[0].astype(BF16), "c_w_in": c_w_in.astype(BF16), "c_w_out": c_w_out[0].astype(BF16),
            "mlp_w1": mlp_w1.astype(BF16), "mlp_w2": mlp_w2.astype(BF16)}
    mcols, ccols = meta_tokens.shape[1], ab_conv_w.shape[2]
    place = jnp.concatenate([
        lax.dynamic_update_slice(jnp.zeros((N_META, D), F32), 0.5 * meta_tokens, (0, chip * mcols)),
        _conv_to_rows(lax.dynamic_update_slice(jnp.zeros((CONV_K, CONV_W), F32), 0.5 * ab_conv_w[0], (0, chip * ccols)))],
        axis=0)
    placed = _small_allreduce(place, "gather_meta")
    meta_full = placed[:N_META]

    w = {
        "ab_w_in": _pad_ab_cols(jnp.transpose(gat_ab_in, (1, 0, 2)).reshape(D, AB_TRUE)),
        "conv_w": _rows_to_conv(placed[N_META:]), "a_log": ab_a_log, "dt_bias": ab_dt_bias,
        "ab_gnorm_g": ab_gnorm_g, "c_lb_raw": c_lb_raw,
        "c_gnorm_g": c_gnorm_g, "ln_mix_g": ln_mix_g, "ln_mix_b": ln_mix_b, "ln_ffn_g": ln_ffn_g, "ln_ffn_b": ln_ffn_b,
    }

    h0 = jnp.concatenate([jnp.zeros((pad, D), F32), meta_full, x[0]], axis=0)
    tgt = jnp.concatenate([jnp.zeros((pad + N_META, D), F32), loss_target[0]], axis=0)
    loss8, dh0, g = _local_step(h0, tgt, w, pad, late)
    grad_x = dh0[pad + N_META:][None]

    gsmall = {"ln_mix_g": g["ln_mix_g"], "ln_mix_b": g["ln_mix_b"], "ln_ffn_g": g["ln_ffn_g"], "ln_ffn_b": g["ln_ffn_b"],
              "c_lb_raw": g["c_lb_raw"], "ab_a_log": g["a_log"], "ab_dt_bias": g["dt_bias"], "ab_gnorm_g": g["ab_gnorm_g"],
              "c_gnorm_g": g["c_gnorm_g"]}
    packed = _pack_small(gsmall).at[SMALL_ROWS - 1, :loss8.shape[1]].set(loss8[0])
    sbuf = jnp.concatenate([packed, dh0[pad:pad + N_META], _conv_to_rows(g["conv_w"])], axis=0)
    ssum = _small_allreduce(sbuf, "allreduce_small")
    loss = ssum[SMALL_ROWS - 1, 0]
    grads = _unpack_small(ssum[:SMALL_ROWS], wts)
    grads["meta_tokens"] = lax.dynamic_slice(ssum[SMALL_ROWS:SMALL_ROWS + N_META], (0, chip * mcols), (N_META, mcols))
    grads["ab_conv_w"] = lax.dynamic_slice(_rows_to_conv(ssum[SMALL_ROWS + N_META:]), (0, chip * ccols), (CONV_K, ccols))[None]

    landed = g["landed"]
    for n in ("ab_w_in", "ab_w_out", "c_w_in", "c_w_out"):
        grads[n] = _sum_slots([landed[n]], f"grad_sum_{n}")
    grads["mlp_w1"] = _sum_slots([landed["w1_0"], landed["w1_1"]], "grad_sum_mlp_w1")
    grads["mlp_w2"] = _sum_slots([landed["w2_0"], landed["w2_1"]], "grad_sum_mlp_w2")

    delta, new_m, new_v = {}, {}, {}
    for n in ("meta_tokens", "ab_conv_w") + BIG:
        shp = wts[n].shape
        to2 = lambda a: a.reshape(-1, shp[-1])
        d2, m2, v2 = _adamw(to2(wts[n]), to2(grads[n]), to2(mom_m[n]), to2(mom_v[n]), f"adamw_{n}")
        delta[n], new_m[n], new_v[n] = d2.reshape(shp), m2.reshape(shp), v2.reshape(shp)
    d2, m2, v2 = _adamw(_pack_small(wts), ssum[:SMALL_ROWS], _pack_small(mom_m), _pack_small(mom_v), "adamw_small")
    delta.update(_unpack_small(d2, wts))
    new_m.update(_unpack_small(m2, wts))
    new_v.update(_unpack_small(v2, wts))

    return (loss, grad_x, *[grads[n] for n in names], *[delta[n] for n in names], *[new_m[n] for n in names],
            *[new_v[n] for n in names])
```

```python
import functools

import numpy as np
import jax
import jax.numpy as jnp
from jax import lax
from jax.experimental import pallas as pl
from jax.experimental.pallas import tpu as pltpu

F32 = jnp.float32
BF16 = jnp.bfloat16

D = 1024
N_META = 16
DEPTH = 2
GDN_H = 4
SB_H = 8
SB_DH = 64
HG_H = 8
HD = 128
CH = 64
QB = 128
ALPHA = float((2 * DEPTH) ** 0.25)
LN_EPS = 1e-5
RMS_EPS = 1e-6
L2_EPS = 1e-6
NEG = -1e30

ADAM_LR = 0.001
ADAM_B1 = 0.9
ADAM_B2 = 0.999
ADAM_EPS = 1e-08
ADAM_WD = 0.01
ADAM_STEP = 10

AB_TRUE = 3592
V7X_VMEM_BYTES = 64 * 1024 * 1024
VMEM_LIMIT = V7X_VMEM_BYTES - 8 * 1024 * 1024

NN = ((1,), (0,))
NT = ((1,), (1,))
TN = ((0,), (0,))


def _cparams(**kw):
    return pltpu.CompilerParams(vmem_limit_bytes=VMEM_LIMIT, **kw)


def _dg(a, b, dims, mode):
    if mode == "h":
        return lax.dot_general(a, b, dims, precision=lax.Precision.HIGHEST, preferred_element_type=F32)
    if mode == "b":
        return lax.dot_general(a.astype(BF16), b.astype(BF16), dims, preferred_element_type=F32)
    ah, bh = a.astype(BF16), b.astype(BF16)
    al, bl = (a - ah.astype(F32)).astype(BF16), (b - bh.astype(F32)).astype(BF16)
    d = lambda x, y: lax.dot_general(x, y, dims, preferred_element_type=F32)
    return d(ah, bh) + (d(ah, bl) + d(al, bh))


def _make_dots(mode, batched=False):
    if batched:
        nn_d, nt_d, tn_d = (((2,), (1,)), ((0,), (0,))), (((2,), (2,)), ((0,), (0,))), (((1,), (1,)), ((0,), (0,)))
    else:
        nn_d, nt_d, tn_d = (NN, ((), ())), (NT, ((), ())), (TN, ((), ()))

    @jax.custom_vjp
    def nn(a, b):
        return _dg(a, b, nn_d, mode)

    @jax.custom_vjp
    def nt(a, b):
        return _dg(a, b, nt_d, mode)

    @jax.custom_vjp
    def tn(a, b):
        return _dg(a, b, tn_d, mode)

    nn.defvjp(lambda a, b: (nn(a, b), (a, b)), lambda r, g: (nt(g, r[1]), tn(r[0], g)))
    nt.defvjp(lambda a, b: (nt(a, b), (a, b)), lambda r, g: (nn(g, r[1]), tn(g, r[0])))
    tn.defvjp(lambda a, b: (tn(a, b), (a, b)), lambda r, g: (nt(r[1], g), nn(r[0], g)))
    return nn, nt, tn


hnn = _make_dots("h")[0]
bbnn, bbnt, bbtn = _make_dots("b", True)
mbnn, mbnt, mbtn = _make_dots("m", True)
hbnt = _make_dots("h", True)[1]


def _split3(x, axis):
    x1 = x.astype(BF16)
    r1 = x - x1.astype(F32)
    x2 = r1.astype(BF16)
    x3 = (r1 - x2.astype(F32)).astype(BF16)
    return jnp.concatenate([x1, x2, x3], axis=axis)


@jax.custom_vjp
def _mask_dot(e3, x):
    return lax.dot_general(e3[0], _split3(x, 0), (NN, ((), ())), preferred_element_type=F32)


def _mask_dot_bwd(e3, g):
    dx = lax.dot_general(e3[1], _split3(g, 0), (TN, ((), ())), preferred_element_type=F32)
    return (jnp.zeros_like(e3[0]), jnp.zeros_like(e3[1])), dx


_mask_dot.defvjp(lambda e3, x: (_mask_dot(e3, x), e3), _mask_dot_bwd)


def _heads(a, n):
    return jnp.concatenate([a[None, :, h * HD:(h + 1) * HD] for h in range(n)], axis=0)


def _sigmoid(x):
    return jax.nn.sigmoid(x)


def _silu(x):
    return x * jax.nn.sigmoid(x)


def _softplus(x):
    return jnp.maximum(x, 0.0) + jnp.log(1.0 + jnp.exp(-jnp.abs(x)))


def _iota(shape, dim):
    return lax.broadcasted_iota(jnp.int32, shape, dim)


def _pick(n, prefs):
    for p in prefs:
        if n % p == 0:
            return p
    return n


def _mm(a, b, *, ta=False, tb=False, out_dtype=F32, name, b_view=None, out_split=0, act=False, gate=None, plus=None,
        ln=None, cargo=(), exchange=None):
    if ta:
        k_dim, m_dim = a.shape
    else:
        m_dim, k_dim = a.shape
    if b_view is None:
        w_rows, w_cols = b.shape
    else:
        kind, layer = b_view
        nj, _, blk_r, blk_c = b.shape
        w_rows, w_cols = (blk_r, nj * blk_c) if kind == "cols" else (nj * blk_r, blk_c)
    n_dim = w_rows if tb else w_cols
    assert (w_cols if tb else w_rows) == k_dim
    tm = _pick(m_dim, (1024, 1056, 704, 640, 512, 384, 256, 128))
    tn = _pick(n_dim, (1024, 1056, 704, 640, 512, 384, 256, 128))
    tk = _pick(k_dim, (1024, 1056, 704, 512, 384, 256, 128))
    nk = k_dim // tk
    a_spec = pl.BlockSpec((tk, tm), lambda i, j, k: (k, i)) if ta else pl.BlockSpec((tm, tk), lambda i, j, k: (i, k))
    wb = (tn, tk) if tb else (tk, tn)
    w_idx = (lambda i, j, k: (j, k)) if tb else (lambda i, j, k: (k, j))
    if b_view is None:
        b_spec = pl.BlockSpec(wb, w_idx)
    elif kind == "cols":
        per = blk_c // wb[1]
        b_spec = pl.BlockSpec((None, None) + wb,
                              lambda i, j, k: (w_idx(i, j, k)[1] // per, layer, w_idx(i, j, k)[0], w_idx(i, j, k)[1] % per))
    else:
        per = blk_r // wb[0]
        b_spec = pl.BlockSpec((None, None) + wb,
                              lambda i, j, k: (w_idx(i, j, k)[0] // per, layer, w_idx(i, j, k)[0] % per, w_idx(i, j, k)[1]))
    if out_split:
        per_o = (n_dim // out_split) // tn
        out_spec = pl.BlockSpec((None, tm, tn), lambda i, j, k: (j // per_o, i, j % per_o))
        out_sds = jax.ShapeDtypeStruct((out_split, m_dim, n_dim // out_split), out_dtype)
    else:
        out_spec = pl.BlockSpec((tm, tn), lambda i, j, k: (i, j))
        out_sds = jax.ShapeDtypeStruct((m_dim, n_dim), out_dtype)
    dims = (((0 if ta else 1,), (1 if tb else 0,)), ((), ()))
    assert sum(e is not None for e in (gate, plus, ln)) <= 1
    extra = [e for e in (gate, plus) if e is not None] + list(ln or ())
    n_out = 2 if act else 3 if ln else 1
    assert ln is None or (tn == n_dim and not out_split)

    def finish(acc, refs):
        if ln:
            y = _ln_res_fn(refs[0][...], acc, refs[1][...], refs[2][...])
            refs[3][...] = acc
            refs[4][...] = y
            refs[5][...] = y.astype(BF16)
        elif act:
            refs[0][...] = acc.astype(refs[0].dtype)
            r = jnp.maximum(acc, 0.0)
            refs[1][...] = (r * r).astype(refs[1].dtype)
        elif gate is not None:
            refs[1][...] = (acc * (2.0 * jnp.maximum(refs[0][...].astype(F32), 0.0))).astype(refs[1].dtype)
        elif plus is not None:
            refs[1][...] = (refs[0][...] + acc).astype(refs[1].dtype)
        else:
            refs[0][...] = acc.astype(refs[0].dtype)

    grid = (m_dim // tm, n_dim // tn, nk)
    nc = len(cargo)

    def body(a_ref, b_ref, *rest):
        acc_ref = rest[-1]
        ids = [pl.program_id(d) for d in range(3)]
        outs, end_cargo = _cargo_bounds(
            rest[len(extra):-1], nc, n_out, exchange, (ids[0] == 0) & (ids[1] == 0) & (ids[2] == 0),
            (ids[0] == grid[0] - 1) & (ids[1] == grid[1] - 1) & (ids[2] == grid[2] - 1))
        refs = tuple(rest[:len(extra)]) + tuple(outs)
        part = lax.dot_general(a_ref[...], b_ref[...], dims, preferred_element_type=F32)
        if nk == 1:
            finish(part, refs)
        else:
            k = ids[2]

            @pl.when(k == 0)
            def _():
                acc_ref[...] = part

            @pl.when(k > 0)
            def _():
                acc_ref[...] += part

            @pl.when(k == nk - 1)
            def _():
                finish(acc_ref[...], refs)
        end_cargo()

    tile = pl.BlockSpec((tm, tn), lambda i, j, k: (i, j))
    rowv = pl.BlockSpec((1, tn), lambda i, j, k: (0, j))
    out_sdss = [out_sds] * n_out
    if ln:
        out_sdss = [jax.ShapeDtypeStruct((m_dim, n_dim), dt) for dt in (F32, F32, BF16)]
    out = pl.pallas_call(
        body, name=name, grid=grid,
        in_specs=[a_spec, b_spec] + ([tile, rowv, rowv] if ln else [tile] * len(extra)) + [ANY] * nc,
        out_specs=[out_spec] * n_out + [ANY] * nc,
        out_shape=out_sdss + (exchange[1](cargo) if nc else []),
        scratch_shapes=(exchange[2](nc) if nc else []) + [pltpu.VMEM((tm, tn) if nk > 1 else (8, 128), F32)],
        compiler_params=_cparams(dimension_semantics=("arbitrary",) * 3 if nc else ("parallel", "parallel", "arbitrary")),
    )(a, b, *extra, *cargo)
    if nc:
        return out
    return out if (act or ln) else out[0]


def _mm_groups_nt(parts, b, name):
    m_dim, k_dim = parts[0].shape
    ng, _, n_dim, _ = b.shape
    assert len(parts) == ng and b.shape[3] == k_dim
    tm = _pick(m_dim, (1056, 704, 512, 384, 256, 128))

    def body(*refs):
        a_refs, b_ref, o_ref, acc_ref = refs[:ng], refs[ng], refs[ng + 1], refs[ng + 2]
        k = pl.program_id(1)
        for g in range(ng):
            @pl.when(k == g)
            def _(g=g):
                part = lax.dot_general(a_refs[g][...], b_ref[...], (NT, ((), ())), preferred_element_type=F32)
                if g == 0:
                    acc_ref[...] = part
                elif g < ng - 1:
                    acc_ref[...] += part
                else:
                    o_ref[...] = acc_ref[...] + part

    return pl.pallas_call(
        body, name=name, grid=(m_dim // tm, ng),
        in_specs=[pl.BlockSpec((tm, k_dim), lambda i, k: (i, 0))] * ng
        + [pl.BlockSpec((None, None, n_dim, k_dim), lambda i, k: (k, 0, 0, 0))],
        out_specs=pl.BlockSpec((tm, n_dim), lambda i, k: (i, 0)),
        out_shape=jax.ShapeDtypeStruct((m_dim, n_dim), F32),
        scratch_shapes=[pltpu.VMEM((tm, n_dim), F32)],
        compiler_params=_cparams(dimension_semantics=("parallel", "arbitrary")),
    )(*parts, b)


def _row_tile(t_pad, width):
    for tr in (528, 352, 176, 128, 64):
        if t_pad % tr == 0 and tr * width * 4 <= (3 << 19) and tr % 16 == 0:
            return tr
    return 64 if t_pad % 64 == 0 else t_pad


def _ln_res_fn(h, m, g, b):
    x = ALPHA * h + m
    mu = jnp.mean(x, axis=-1, keepdims=True)
    xc = x - mu
    var = jnp.mean(xc * xc, axis=-1, keepdims=True)
    return xc * lax.rsqrt(var + LN_EPS) * g + b


def _ln_res_bwd(h, m, g, b, dys, name):
    t_pad = h.shape[0]
    tr = _row_tile(t_pad, D)
    nd = len(dys)

    def body(h_ref, m_ref, g_ref, b_ref, *rest):
        d_refs, (dh_ref, dm_ref, dg_ref, db_ref) = rest[:nd], rest[nd:]
        _, vjp = jax.vjp(_ln_res_fn, h_ref[...], m_ref[...], g_ref[...], b_ref[...])
        dy = d_refs[0][...]
        for d_ref in d_refs[1:]:
            dy = dy + d_ref[...]
        dh, dm, dg, db = vjp(dy)
        dh_ref[...] = dh
        dm_ref[...] = dm.astype(BF16)

        @pl.when(pl.program_id(0) == 0)
        def _():
            dg_ref[...] = jnp.zeros_like(dg_ref)
            db_ref[...] = jnp.zeros_like(db_ref)

        dg_ref[...] += dg
        db_ref[...] += db

    row = pl.BlockSpec((tr, D), lambda i: (i, 0))
    par = pl.BlockSpec((1, D), lambda i: (0, 0))
    return pl.pallas_call(
        body, name=name, grid=(t_pad // tr,), in_specs=[row, row, par, par] + [row] * nd,
        out_specs=[row, row, par, par],
        out_shape=[jax.ShapeDtypeStruct((t_pad, D), F32), jax.ShapeDtypeStruct((t_pad, D), BF16),
                   jax.ShapeDtypeStruct((1, D), F32), jax.ShapeDtypeStruct((1, D), F32)],
        compiler_params=_cparams(),
    )(h, m, g, b, *dys)


def _grms_fn(o, z, g):
    y = o * lax.rsqrt(jnp.mean(o * o, axis=-1, keepdims=True) + RMS_EPS) * g
    return y * _silu(z)


def _grms_fwd(o, z_arr, z_blk0, g, name):
    t_pad, w = o.shape
    tr = _row_tile(t_pad, w)
    assert (z_blk0 * HD) % w == 0

    def body(o_ref, z_ref, g_ref, y_ref):
        for h in range(w // HD):
            c = slice(h * HD, (h + 1) * HD)
            y_ref[:, c] = _grms_fn(o_ref[:, c], z_ref[:, c], g_ref[...]).astype(BF16)

    return pl.pallas_call(
        body, name=name, grid=(t_pad // tr,),
        in_specs=[pl.BlockSpec((tr, w), lambda i: (i, 0)), pl.BlockSpec((tr, w), lambda i: (i, z_blk0 * HD // w)),
                  pl.BlockSpec((1, HD), lambda i: (0, 0))],
        out_specs=pl.BlockSpec((tr, w), lambda i: (i, 0)),
        out_shape=jax.ShapeDtypeStruct((t_pad, w), BF16), compiler_params=_cparams(),
    )(o, z_arr, g)


def _grms_bwd(o, z_arr, z_blk0, g, dy_arr, dy_blk0, name):
    t_pad, w = o.shape
    tr = _row_tile(t_pad, w)
    assert (z_blk0 * HD) % w == 0 and (dy_blk0 * HD) % w == 0

    def body(o_ref, z_ref, g_ref, dy_ref, do_ref, dz_ref, dg_ref):
        @pl.when(pl.program_id(0) == 0)
        def _():
            dg_ref[...] = jnp.zeros_like(dg_ref)

        for h in range(w // HD):
            c = slice(h * HD, (h + 1) * HD)
            _, vjp = jax.vjp(_grms_fn, o_ref[:, c], z_ref[:, c], g_ref[...])
            do, dz, dg = vjp(dy_ref[:, c])
            do_ref[:, c] = do
            dz_ref[:, c] = dz.astype(BF16)
            dg_ref[...] += dg

    blk = pl.BlockSpec((tr, w), lambda i: (i, 0))
    return pl.pallas_call(
        body, name=name, grid=(t_pad // tr,),
        in_specs=[blk, pl.BlockSpec((tr, w), lambda i: (i, z_blk0 * HD // w)), pl.BlockSpec((1, HD), lambda i: (0, 0)),
                  pl.BlockSpec((tr, w), lambda i: (i, dy_blk0 * HD // w))],
        out_specs=[blk, blk, pl.BlockSpec((1, HD), lambda i: (0, 0))],
        out_shape=[jax.ShapeDtypeStruct((t_pad, w), F32), jax.ShapeDtypeStruct((t_pad, w), BF16),
                   jax.ShapeDtypeStruct((1, HD), F32)],
        compiler_params=_cparams(),
    )(o, z_arr, g, dy_arr)


def _loss_fwd(y, tgt, first_row, name):
    t_pad = y.shape[0]
    tr = _row_tile(t_pad, D)

    def body(y_ref, t_ref, l_ref, dy_ref):
        rows = pl.program_id(0) * tr + _iota((tr, 1), 0)
        err = jnp.where(rows >= first_row, y_ref[...] - t_ref[...], 0.0)
        dy_ref[...] = err * (1.0 / D)

        @pl.when(pl.program_id(0) == 0)
        def _():
            l_ref[...] = jnp.zeros_like(l_ref)

        part = jnp.sum(jnp.sum(err * err, axis=1, keepdims=True), axis=0, keepdims=True)
        l_ref[...] += jnp.broadcast_to(part * (0.5 / D), l_ref.shape)

    row = pl.BlockSpec((tr, D), lambda i: (i, 0))
    return pl.pallas_call(
        body, name=name, grid=(t_pad // tr,), in_specs=[row, row],
        out_specs=[pl.BlockSpec((8, 128), lambda i: (0, 0)), row],
        out_shape=[jax.ShapeDtypeStruct((8, 128), F32), jax.ShapeDtypeStruct((t_pad, D), F32)],
        compiler_params=_cparams(),
    )(y, tgt)


def _assemble_bf16(parts, name):
    t_pad = parts[0].shape[0]
    widths = [p.shape[1] for p in parts]
    total = sum(widths)
    tr = _row_tile(t_pad, total)

    def body(*refs):
        o_ref = refs[-1]
        off = 0
        for ref, w in zip(refs[:-1], widths):
            o_ref[:, off:off + w] = ref[...].astype(BF16)
            off += w

    return pl.pallas_call(
        body, name=name, grid=(t_pad // tr,), in_specs=[pl.BlockSpec((tr, w), lambda i: (i, 0)) for w in widths],
        out_specs=pl.BlockSpec((tr, total), lambda i: (i, 0)),
        out_shape=jax.ShapeDtypeStruct((t_pad, total), BF16), compiler_params=_cparams(),
    )(*parts)


CONV_K = 4
HALO = 8
RT = 128


def _conv_fwd(p, blk0, w, mode, pad, name):
    t_pad = p.shape[0]
    nt = t_pad // RT
    scale = HD ** -0.5 if mode == "q" else 1.0

    def body(x_ref, w_ref, y_ref, xs_ref):
        xs_ref[0:HALO, :] = jnp.zeros((HALO, HD), F32)
        rows = _iota((t_pad, 1), 0)
        xs_ref[HALO:HALO + t_pad, :] = jnp.where(rows >= pad, x_ref[...], 0.0)
        wv = w_ref[...]

        def tile(i, carry):
            r0 = pl.multiple_of(i * RT, RT)
            ext = xs_ref[pl.ds(r0, RT + HALO), :]
            acc = ext[HALO:, :] * wv[3:4, :]
            for s in (1, 2, 3):
                acc = acc + pltpu.roll(ext, s, 0)[HALO:, :] * wv[3 - s:4 - s, :]
            y = _silu(acc)
            if mode != "v":
                y = y * lax.rsqrt(jnp.sum(y * y, axis=-1, keepdims=True) + L2_EPS) * scale
            y_ref[pl.ds(r0, RT), :] = y
            return carry

        lax.fori_loop(0, nt, tile, 0)

    return pl.pallas_call(
        body, name=name, grid=(GDN_H,),
        in_specs=[pl.BlockSpec((t_pad, HD), lambda h: (0, blk0 + h)), pl.BlockSpec((CONV_K, HD), lambda h: (0, h))],
        out_specs=pl.BlockSpec((t_pad, HD), lambda h: (0, h)),
        out_shape=jax.ShapeDtypeStruct((t_pad, GDN_H * HD), F32),
        scratch_shapes=[pltpu.VMEM((t_pad + HALO, HD), F32)],
        compiler_params=_cparams(),
    )(p, w)


def _conv_bwd(p, blk0, w, dn, mode, pad, name):
    t_pad = p.shape[0]
    nt = t_pad // RT
    scale = HD ** -0.5 if mode == "q" else 1.0

    def body(x_ref, w_ref, dn_ref, dx_ref, dw_ref, xs_ref, ds_ref):
        xs_ref[0:HALO, :] = jnp.zeros((HALO, HD), F32)
        xs_ref[HALO + t_pad:HALO + t_pad + 2 * HALO, :] = jnp.zeros((2 * HALO, HD), F32)
        ds_ref[t_pad:t_pad + HALO, :] = jnp.zeros((HALO, HD), F32)
        rows = _iota((t_pad, 1), 0)
        xs_ref[HALO:HALO + t_pad, :] = jnp.where(rows >= pad, x_ref[...], 0.0)
        ds_ref[0:t_pad, :] = dn_ref[...]
        wv = w_ref[...]

        def tile(i, dw):
            r0 = pl.multiple_of(i * RT, RT)
            ext = xs_ref[pl.ds(r0, RT + 2 * HALO), :]
            dn_e = ds_ref[pl.ds(r0, RT + HALO), :]
            xsh = [ext[HALO:, :]] + [pltpu.roll(ext, s, 0)[HALO:, :] for s in (1, 2, 3)]
            pre = xsh[0] * wv[3:4, :]
            for s in (1, 2, 3):
                pre = pre + xsh[s] * wv[3 - s:4 - s, :]
            sg = _sigmoid(pre)
            y = pre * sg
            if mode != "v":
                ss = jnp.sum(y * y, axis=-1, keepdims=True) + L2_EPS
                r = lax.rsqrt(ss)
                dy = scale * (dn_e * r - y * (r * r * r) * jnp.sum(dn_e * y, axis=-1, keepdims=True))
            else:
                dy = dn_e
            dpre = dy * (sg * (1.0 + pre * (1.0 - sg)))
            dx = dpre[:RT, :] * wv[3:4, :]
            for s in (1, 2, 3):
                dx = dx + pltpu.roll(dpre, RT + HALO - s, 0)[:RT, :] * wv[3 - s:4 - s, :]
            trow = r0 + _iota((RT, 1), 0)
            dx_ref[pl.ds(r0, RT), :] = jnp.where(trow >= pad, dx, 0.0)
            new = []
            for s in (0, 1, 2, 3):
                new.append(dw[s] + jnp.sum(dpre[:RT, :] * xsh[s][:RT, :], axis=0, keepdims=True))
            return tuple(new)

        z = jnp.zeros((1, HD), F32)
        dw = lax.fori_loop(0, nt, tile, (z, z, z, z))
        for s in (0, 1, 2, 3):
            dw_ref[3 - s:4 - s, :] = dw[s]

    return pl.pallas_call(
        body, name=name, grid=(GDN_H,),
        in_specs=[pl.BlockSpec((t_pad, HD), lambda h: (0, blk0 + h)), pl.BlockSpec((CONV_K, HD), lambda h: (0, h)),
                  pl.BlockSpec((t_pad, HD), lambda h: (0, h))],
        out_specs=[pl.BlockSpec((t_pad, HD), lambda h: (0, h)), pl.BlockSpec((CONV_K, HD), lambda h: (0, h))],
        out_shape=[jax.ShapeDtypeStruct((t_pad, GDN_H * HD), F32), jax.ShapeDtypeStruct((CONV_K, GDN_H * HD), F32)],
        scratch_shapes=[pltpu.VMEM((t_pad + 3 * HALO, HD), F32), pltpu.VMEM((t_pad + HALO, HD), F32)],
        compiler_params=_cparams(),
    )(p, w, dn)


@jax.custom_vjp
def _unit_lower_inv(m, bd, eye):
    md = m * bd
    low = m - md
    p2 = mbnn(md, md)
    p4 = mbnn(p2, p2)
    dinv = mbnn(mbnn(eye - md, eye + p2), eye + p4)
    n = mbnn(dinv, low)
    n2 = mbnn(n, n)
    n4 = mbnn(n2, n2)
    return mbnn(mbnn(mbnn(eye - n, eye + n2), eye + n4), dinv)


def _unit_lower_inv_bwd(res, g):
    t, bd, eye = res
    return -mbtn(t, mbnt(g, t)), jnp.zeros_like(bd), jnp.zeros_like(eye)


def _unit_lower_inv_fwd(m, bd, eye):
    t = _unit_lower_inv(m, bd, eye)
    return t, (t, bd, eye)


_unit_lower_inv.defvjp(_unit_lower_inv_fwd, _unit_lower_inv_bwd)


def _gdn_chunks(chunks, alog, dtb, s):
    nh = chunks[0][0].shape[0]
    ri = _iota((1, CH, CH), 1)
    ci = _iota((1, CH, CH), 2)
    causal = ri >= ci
    strict = ri > ci
    eye = (ri == ci).astype(F32)
    bd = ((ri >> 3) == (ci >> 3)).astype(F32)
    ltri = (_iota((CH, CH), 0) >= _iota((CH, CH), 1)).astype(F32)
    sel = (_iota((nh, 1, HD), 2) == _iota((nh, 1, HD), 0)).astype(F32)
    last = _iota((1, CH, 1), 1) == CH - 1

    beta, gc, gc_rows = [], [], []
    for _, _, _, bb, aa, valid in chunks:
        beta_all = jnp.where(valid, _sigmoid(bb), 0.0)
        g_all = jnp.where(valid, -jnp.exp(alog) * _softplus(aa + dtb), 0.0)
        gc_all = hnn(ltri, g_all)
        beta.append(jnp.sum(beta_all[None] * sel, axis=2, keepdims=True))
        gc.append(jnp.sum(gc_all[None] * sel, axis=2, keepdims=True))
        gc_rows.append(hbnt(jnp.broadcast_to(sel, (nh, CH, HD)), jnp.broadcast_to(gc_all[None], (nh, CH, HD))))
    cat = lambda xs: jnp.concatenate(xs, axis=0)
    q, k, v = (cat([c[j] for c in chunks]) for j in range(3))
    beta, gc, gc_rows = cat(beta), cat(gc), cat(gc_rows)
    gc_last = jnp.sum(jnp.where(last, gc, 0.0), axis=1, keepdims=True)
    decay = jnp.exp(jnp.where(causal, gc - gc_rows, NEG))
    egc = jnp.exp(gc)

    kb = k * beta
    m = jnp.where(strict, bbnt(kb, k) * decay, 0.0)
    t_inv = _unit_lower_inv(m, bd, eye)
    u = bbnn(t_inv, v * beta)
    w = bbnn(t_inv, kb * egc)
    a_intra = bbnt(q, k) * decay
    q_dec = q * egc
    k_dec = k * jnp.exp(gc_last - gc)
    g_tot = jnp.exp(gc_last)

    outs = []
    for n in range(len(chunks)):
        part = lambda a: a[n * nh:(n + 1) * nh]
        v_new = part(u) - bbnn(part(w), s)
        outs.append(bbnn(part(q_dec), s) + bbnn(part(a_intra), v_new))
        s = s * part(g_tot) + bbtn(part(k_dec), v_new)
    return outs, s


PAIR = 2 * CH


def _gdn_specs(npair, rev):
    cc = (lambda c: npair - 1 - c) if rev else (lambda c: c)
    wide = pl.BlockSpec((PAIR, GDN_H * HD), lambda c: (cc(c), 0))
    fix = lambda off: pl.BlockSpec((PAIR, HD), lambda c: (cc(c), off))
    par = pl.BlockSpec((1, HD), lambda c: (0, 0))
    state = pl.BlockSpec((1, GDN_H, HD, HD), lambda c: (cc(c), 0, 0, 0))
    return wide, fix, par, state


def _store_heads(ref, a, rows=slice(None)):
    for h in range(a.shape[0]):
        ref[rows, h * HD:(h + 1) * HD] = a[h]


def _chunk_rows(half):
    return slice(half * CH, (half + 1) * CH)


def _chunk_valid(pair, half, pad):
    return ((2 * pair + half) * CH + _iota((CH, 1), 0)) >= pad


def _gdn_fwd(qn, kn, vn, p, alog, dtb, pad, name, cargo=(), exchange=None):
    t_pad = qn.shape[0]
    npair = t_pad // PAIR
    wide, fix, par, state = _gdn_specs(npair, False)
    n = len(cargo)

    def body(q_ref, k_ref, v_ref, bb_ref, aa_ref, al_ref, dt_ref, *rest):
        c = pl.program_id(0)
        s_ref = rest[-1]
        (o_ref, ss_ref), end_cargo = _cargo_bounds(rest[:-1], n, 2, exchange, c == 0, c == npair - 1)

        @pl.when(c == 0)
        def _():
            s_ref[...] = jnp.zeros_like(s_ref)

        s = s_ref[...]
        ss_ref[0] = s
        rows = [_chunk_rows(half) for half in (0, 1)]
        chunks = [(_heads(q_ref[r, :], GDN_H), _heads(k_ref[r, :], GDN_H), _heads(v_ref[r, :], GDN_H),
                   bb_ref[r, :], aa_ref[r, :], _chunk_valid(c, half, pad)) for half, r in enumerate(rows)]
        outs, s = _gdn_chunks(chunks, al_ref[...], dt_ref[...], s)
        for r, o in zip(rows, outs):
            _store_heads(o_ref, o, r)
        s_ref[...] = s
        end_cargo()

    return pl.pallas_call(
        body, name=name, grid=(npair,),
        in_specs=[wide, wide, wide, fix(16), fix(17), par, par] + [ANY] * n,
        out_specs=[wide, state] + [ANY] * n,
        out_shape=[jax.ShapeDtypeStruct((t_pad, GDN_H * HD), F32), jax.ShapeDtypeStruct((npair, GDN_H, HD, HD), F32)]
        + (exchange[1](cargo) if n else []),
        scratch_shapes=(exchange[2](n) if n else []) + [pltpu.VMEM((GDN_H, HD, HD), F32)],
        compiler_params=_cparams(),
    )(qn, kn, vn, p, p, alog, dtb, *cargo)


def _gdn_bwd(qn, kn, vn, p, alog, dtb, ssave, do, pad, name, cargo=(), exchange=None):
    t_pad = qn.shape[0]
    npair = t_pad // PAIR
    wide, fix, par, state = _gdn_specs(npair, True)
    n = len(cargo)

    def body(q_ref, k_ref, v_ref, bb_ref, aa_ref, al_ref, dt_ref, ss_ref, do_ref, *rest):
        c = pl.program_id(0)
        ds_ref = rest[-1]
        (dq_ref, dk_ref, dv_ref, dbb_ref, daa_ref, dal_ref, ddt_ref), end_cargo = _cargo_bounds(
            rest[:-1], n, 7, exchange, c == 0, c == npair - 1)

        @pl.when(c == 0)
        def _():
            ds_ref[...] = jnp.zeros_like(ds_ref)
            dal_ref[...] = jnp.zeros_like(dal_ref)
            ddt_ref[...] = jnp.zeros_like(ddt_ref)

        ra, rb = _chunk_rows(0), _chunk_rows(1)
        va, vb = _chunk_valid(npair - 1 - c, 0, pad), _chunk_valid(npair - 1 - c, 1, pad)

        def pair(qa, ka, va_, ba, aa, qb, kb, vb_, bb, ab, al, dt, s):
            (oa, ob), s = _gdn_chunks([(qa, ka, va_, ba, aa, va), (qb, kb, vb_, bb, ab, vb)], al, dt, s)
            return oa, ob, s

        ins = [f(ref[r, :]) for r in (ra, rb)
               for ref, f in ((q_ref, lambda a: _heads(a, GDN_H)), (k_ref, lambda a: _heads(a, GDN_H)),
                              (v_ref, lambda a: _heads(a, GDN_H)), (bb_ref, lambda a: a), (aa_ref, lambda a: a))]
        _, vjp = jax.vjp(pair, *ins, al_ref[...], dt_ref[...], ss_ref[0])
        g = vjp((_heads(do_ref[ra, :], GDN_H), _heads(do_ref[rb, :], GDN_H), ds_ref[...]))
        for r, (dq, dk, dv, dbb, daa) in ((ra, g[0:5]), (rb, g[5:10])):
            _store_heads(dq_ref, dq, r)
            _store_heads(dk_ref, dk, r)
            _store_heads(dv_ref, dv, r)
            dbb_ref[r, :] = dbb
            daa_ref[r, :] = daa
        dal_ref[...] += g[10]
        ddt_ref[...] += g[11]
        ds_ref[...] = g[12]
        end_cargo()

    sds = jax.ShapeDtypeStruct
    return pl.pallas_call(
        body, name=name, grid=(npair,),
        in_specs=[wide, wide, wide, fix(16), fix(17), par, par, state, wide] + [ANY] * n,
        out_specs=[wide, wide, wide, fix(0), fix(0), par, par] + [ANY] * n,
        out_shape=[sds((t_pad, GDN_H * HD), F32)] * 3 + [sds((t_pad, HD), F32)] * 2 + [sds((1, HD), F32)] * 2
        + (exchange[1](cargo) if n else []),
        scratch_shapes=(exchange[2](n) if n else []) + [pltpu.VMEM((GDN_H, HD, HD), F32)],
        compiler_params=_cparams(),
    )(qn, kn, vn, p, p, alog, dtb, ssave, do, *cargo)


SB_Q0, SB_K0, SB_V0 = 18, 22, 26
SB_SCALE = SB_DH ** -0.5
SB_NB_FWD, SB_NB_BWD = 11, 8


def _sb_terms(z, allowed):
    nz = -z
    raw = jnp.minimum(nz, 0.0) - jnp.log(1.0 + jnp.exp(jnp.minimum(z, nz)))
    l1m = raw if allowed is None else jnp.where(allowed, raw, 0.0)
    ls = z + raw
    return l1m, ls, jnp.exp(ls)


def _sb_passes(i, step, carry, per):
    total = i + 1

    def sized(done, first):
        return [functools.partial(step, done, masked=({0} if first else set()) | {nb - 1}, nb=nb)
                for nb in range(1, per + 1)]

    def several(c):
        n_mid = (total - per - 1) // per
        c = step(0, c, masked={0}, nb=per)
        c = lax.fori_loop(0, n_mid, lambda t, cc: step(per * (1 + t), cc, masked=set(), nb=per), c)
        done = per * (1 + n_mid)
        return lax.switch(total - done - 1, sized(done, False), c)

    return lax.cond(total <= per, lambda c: lax.switch(total - 1, sized(0, True), c), several, carry)


def _sb_stack(a, i):
    first = _iota((1, HD), 1) < SB_DH
    a2 = jnp.concatenate([jnp.where(first, a, 0.0), jnp.where(first, 0.0, a)], axis=0).astype(BF16)
    rq = i * QB + _iota((QB, 1), 0)
    return a2, jnp.concatenate([rq, rq], axis=0), first


def _hi_lo(a):
    hi = a.astype(BF16)
    lo = (a - hi.astype(F32)).astype(BF16)
    return jnp.concatenate([hi, lo], axis=1)


def _cargo_bounds(refs, n, n_out, exchange, first, last):
    outs = refs[n:n + n_out]
    if not n:
        return outs, lambda: None
    ex = exchange[0](refs[:n], refs[n + n_out:2 * n + n_out], *refs[2 * n + n_out:])

    @pl.when(first)
    def _():
        ex.start()

    def finish():
        @pl.when(last)
        def _():
            ex.wait()

    return outs, finish


def _sb_fwd(p, pad, name, cargo=(), exchange=None):
    t_pad = p.shape[0]
    nq = t_pad // QB
    n = len(cargo)

    def body(q_ref, k_ref, v_ref, *rest):
        i = pl.program_id(1)
        pr = pl.program_id(0)
        (o_ref, r_ref), end_cargo = _cargo_bounds(rest, n, 2, exchange, (pr == 0) & (i == 0),
                                                  (pr == SB_H // 2 - 1) & (i == nq - 1))
        q2, rowq, first = _sb_stack(q_ref[...] * SB_SCALE, i)
        tri = (_iota((QB, QB), 0) > _iota((QB, QB), 1)).astype(BF16)
        upper2 = jnp.concatenate([jnp.concatenate([tri, tri], axis=0), jnp.ones((2 * QB, QB), BF16)], axis=1)

        def chain(kb, masked):
            start = pl.multiple_of(kb * QB, QB)
            kblk = k_ref[pl.ds(start, QB), :].astype(BF16)
            vblk = v_ref[pl.ds(start, QB), :].astype(BF16)
            z = lax.dot_general(q2, kblk, (NT, ((), ())), preferred_element_type=F32)
            colk = kb * QB + _iota((1, QB), 1)
            al = ((colk < rowq) & (colk >= pad)) if masked else None
            l1m, ls, _ = _sb_terms(z, al)
            sums = lax.dot_general(_hi_lo(l1m), upper2, (NN, ((), ())), preferred_element_type=F32)
            return al, ls, sums[:, :QB], sums[:, QB:], vblk

        def step(done, carry, masked, nb):
            o_acc, run = carry
            ws, vs = [], []
            for n in range(nb):
                al, ls, suf, rs, vblk = chain(i - done - n, n in masked)
                wgt = jnp.exp(ls + suf + run)
                ws.append((wgt if al is None else jnp.where(al, wgt, 0.0)).astype(BF16))
                vs.append(vblk)
                run = run + rs
            o_acc = o_acc + lax.dot_general(jnp.concatenate(ws, axis=1), jnp.concatenate(vs, axis=0),
                                            (NN, ((), ())), preferred_element_type=F32)
            return o_acc, run

        o_acc, run = _sb_passes(i, step, (jnp.zeros((2 * QB, HD), F32), jnp.zeros((2 * QB, QB), F32)), SB_NB_FWD)
        o_ref[...] = jnp.where(first, o_acc[:QB], o_acc[QB:]).astype(BF16)
        r_ref[...] = jnp.where(first, run[:QB], run[QB:])
        end_cargo()

    full = lambda off: pl.BlockSpec((t_pad, HD), lambda pr, i: (0, off + pr))
    blk = pl.BlockSpec((QB, HD), lambda pr, i: (i, pr))
    return pl.pallas_call(
        body, name=name, grid=(SB_H // 2, nq),
        in_specs=[pl.BlockSpec((QB, HD), lambda pr, i: (i, SB_Q0 + pr)), full(SB_K0), full(SB_V0)] + [ANY] * n,
        out_specs=[blk, blk] + [ANY] * n,
        out_shape=[jax.ShapeDtypeStruct((t_pad, SB_H * SB_DH), BF16), jax.ShapeDtypeStruct((t_pad, SB_H * SB_DH), F32)]
        + (exchange[1](cargo) if n else []),
        scratch_shapes=exchange[2](n) if n else [],
        compiler_params=_cparams(),
    )(p, p, p, *cargo)


def _sb_bwd(p, rtot, dy, dy_blk0, pad, name, cargo=(), exchange=None):
    t_pad = p.shape[0]
    nq = t_pad // QB
    n = len(cargo)

    def body(q_ref, k_ref, v_ref, r_ref, do_ref, *rest):
        i = pl.program_id(1)
        pr = pl.program_id(0)
        dkt_ref, dvt_ref = rest[-2:]
        (dq_ref, dk_ref, dv_ref), end_cargo = _cargo_bounds(rest[:-2], n, 3, exchange, (pr == 0) & (i == 0),
                                                            (pr == SB_H // 2 - 1) & (i == nq - 1))

        @pl.when(i == 0)
        def _():
            dkt_ref[...] = jnp.zeros_like(dkt_ref)
            dvt_ref[...] = jnp.zeros_like(dvt_ref)

        q2, rowq, first = _sb_stack(q_ref[...] * SB_SCALE, i)
        do2, _, _ = _sb_stack(do_ref[...], i)
        q2t = jnp.transpose(q2.astype(F32)).astype(BF16)
        do2t = jnp.transpose(do2.astype(F32)).astype(BF16)
        rt = r_ref[...]
        lane = _iota((1, HD), 1)
        rcol = jnp.concatenate([jnp.sum(jnp.where(lane == 0, rt, 0.0), axis=1, keepdims=True),
                                jnp.sum(jnp.where(lane == SB_DH, rt, 0.0), axis=1, keepdims=True)], axis=0)
        rj = _iota((QB, QB), 0)
        cs = _iota((QB, QB), 1)
        tri_u = (rj > cs).astype(BF16)
        tri_l = (rj < cs).astype(BF16)
        ones2 = jnp.ones((2 * QB, QB), BF16)
        upper2 = jnp.concatenate([jnp.concatenate([tri_u, tri_u], axis=0), ones2], axis=1)
        lower2 = jnp.concatenate([jnp.concatenate([tri_l, tri_l], axis=0), ones2], axis=1)
        rcol = jnp.broadcast_to(rcol, (2 * QB, QB))

        def chain(kb, masked):
            start = pl.multiple_of(kb * QB, QB)
            kblk = k_ref[pl.ds(start, QB), :].astype(BF16)
            vblk = v_ref[pl.ds(start, QB), :].astype(BF16)
            z = lax.dot_general(q2, kblk, (NT, ((), ())), preferred_element_type=F32)
            colk = kb * QB + _iota((1, QB), 1)
            al = ((colk < rowq) & (colk >= pad)) if masked else None
            l1m, ls, sg = _sb_terms(z, al)
            dwgt = lax.dot_general(do2, vblk, (NT, ((), ())), preferred_element_type=F32)
            sums = lax.dot_general(_hi_lo(l1m), upper2, (NN, ((), ())), preferred_element_type=F32)
            return kb, kblk, al, ls, sums[:, :QB], sums[:, QB:], dwgt, sg

        def finish(c, left, gseen):
            kb, kblk, al, ls, suf, rs, dwgt, sg = c
            left = left - rs
            wgt = jnp.exp(ls + suf + left)
            if al is not None:
                wgt = jnp.where(al, wgt, 0.0)
            dl = dwgt * wgt
            sums = lax.dot_general(_hi_lo(dl), lower2, (NN, ((), ())), preferred_element_type=F32)
            gpre = gseen + sums[:, :QB]
            dz = dl - sg * (dl + gpre)
            if al is not None:
                dz = jnp.where(al, dz, 0.0)
            dz = dz.astype(BF16)
            dkt_ref[kb] += lax.dot_general(q2t, dz, (NN, ((), ())), preferred_element_type=F32)
            dvt_ref[kb] += lax.dot_general(do2t, wgt.astype(BF16), (NN, ((), ())), preferred_element_type=F32)
            return dz, left, gseen + sums[:, QB:]

        def step(done, carry, masked, nb):
            dq_acc, left, gseen = carry
            cs_ = [chain(done + n, n in masked) for n in range(nb)]
            dzs = []
            for c in cs_:
                dz, left, gseen = finish(c, left, gseen)
                dzs.append(dz)
            dq_acc = dq_acc + lax.dot_general(jnp.concatenate(dzs, axis=1), jnp.concatenate([c[1] for c in cs_], axis=0),
                                              (NN, ((), ())), preferred_element_type=F32)
            return dq_acc, left, gseen

        dq_acc, _, _ = _sb_passes(i, step, (jnp.zeros((2 * QB, HD), F32), rcol, jnp.zeros((2 * QB, QB), F32)),
                                  SB_NB_BWD)
        dq_ref[...] = jnp.where(first, dq_acc[:QB], dq_acc[QB:]) * SB_SCALE

        @pl.when(i == nq - 1)
        def _():
            for kb in range(nq):
                dk_ref[kb * QB:(kb + 1) * QB, :] = jnp.transpose(dkt_ref[kb])
                dv_ref[kb * QB:(kb + 1) * QB, :] = jnp.transpose(dvt_ref[kb])

        end_cargo()

    full_in = lambda off: pl.BlockSpec((t_pad, HD), lambda pr, i: (0, off + pr))
    full_out = pl.BlockSpec((t_pad, HD), lambda pr, i: (0, pr))
    blk = pl.BlockSpec((QB, HD), lambda pr, i: (i, pr))
    sds = jax.ShapeDtypeStruct((t_pad, SB_H * SB_DH), F32)
    return pl.pallas_call(
        body, name=name, grid=(SB_H // 2, nq),
        in_specs=[pl.BlockSpec((QB, HD), lambda pr, i: (i, SB_Q0 + pr)), full_in(SB_K0), full_in(SB_V0), blk,
                  pl.BlockSpec((QB, HD), lambda pr, i: (i, dy_blk0 + pr))] + [ANY] * n,
        out_specs=[blk, full_out, full_out] + [ANY] * n,
        out_shape=[sds, sds, sds] + (exchange[1](cargo) if n else []),
        scratch_shapes=(exchange[2](n) if n else []) + [pltpu.VMEM((nq, HD, QB), F32)] * 2,
        compiler_params=_cparams(),
    )(p, p, p, rtot, dy, *cargo)


HG_LEVELS = 6


def _hg_prefix_matrix():
    t = np.arange(CH)[:, None]
    j = np.arange(CH)[None, :]
    groups = [(j <= t)]
    for lvl in range(1, HG_LEVELS + 1):
        half = CH >> lvl
        e = (t // (2 * half)) * (2 * half) + half - 1
        groups.append(j <= e)
    groups.append(np.ones((8, CH), bool))
    e = np.concatenate(groups, axis=0).astype(np.float32)
    return np.concatenate([e, e, e], axis=1), np.concatenate([e, e, e], axis=0)


HG_G = 4


def _hg_chunk(qr, fr, iv, r0, r1, st, valid, ecat):
    g = st.shape[0]
    mx = jnp.maximum(r0, r1)
    e0 = jnp.exp(r0 - mx)
    e1 = jnp.exp(r1 - mx)
    lb = e1 / (e0 + e1)
    fg = lb + (1.0 - lb) * _sigmoid(fr)
    logf = jnp.where(valid, jnp.log(fg), 0.0)
    kk = jnp.where(valid, 1.0 - fg, 0.0)
    q = jnp.where(valid, _silu(qr), 0.0)
    v = _heads(jnp.where(valid, iv, 0.0), g)

    pre = _mask_dot(ecat, logf)
    b = pre[0:CH]
    b_last = jnp.max(pre[(HG_LEVELS + 1) * CH:], axis=0, keepdims=True)
    row = _iota((CH, 1), 0)
    ri = _iota((1, CH, CH), 1)
    ci = _iota((1, CH, CH), 2)
    a = jnp.where(ri == ci, jnp.sum(_heads(q * kk, g), axis=2, keepdims=True), 0.0)
    for lvl in range(1, HG_LEVELS + 1):
        half = CH >> lvl
        m = pre[lvl * CH:(lvl + 1) * CH]
        low = (row & half) != 0
        dec = jnp.exp(jnp.where(low, b - m, m - b))
        qt = jnp.where(low, q * dec, 0.0)
        kt = jnp.where(low, 0.0, kk * dec)
        same = (ri >> (7 - lvl)) == (ci >> (7 - lvl))
        a = a + jnp.where(same, bbnt(_heads(qt, g), _heads(kt, g)), 0.0)
    o = bbnt(_heads(q * jnp.exp(b), g), st) + bbnn(a, v)
    kd = kk * jnp.exp(b_last - b)
    st_new = st * _heads(jnp.exp(b_last), g) + bbtn(v, _heads(kd, g))
    return o, st_new


def _hg_specs(npair, rev):
    cc = (lambda c: npair - 1 - c) if rev else (lambda c: c)
    ng = HG_H // HG_G
    blk = lambda off: pl.BlockSpec((PAIR, HG_G * HD), lambda h, c: (cc(c), off * ng + h))
    lbs = pl.BlockSpec((2, HG_G * HD), lambda h, c: (0, h))
    state = pl.BlockSpec((1, HG_G, HD, HD), lambda h, c: (cc(c), h, 0, 0))
    return ng, blk, lbs, state


def _hg_fwd(p, lbraw, ecat, pad, name):
    t_pad = p.shape[0]
    npair = t_pad // PAIR
    ng, blk, lbs, state = _hg_specs(npair, False)

    def body(q_ref, f_ref, i_ref, lb_ref, e_ref, et_ref, o_ref, ss_ref, s_ref):
        c = pl.program_id(1)

        @pl.when(c == 0)
        def _():
            s_ref[...] = jnp.zeros_like(s_ref)

        st = s_ref[...]
        ss_ref[0] = st
        for half in (0, 1):
            r = _chunk_rows(half)
            o, st = _hg_chunk(q_ref[r, :], f_ref[r, :], i_ref[r, :], lb_ref[0:1, :], lb_ref[1:2, :], st,
                              _chunk_valid(c, half, pad), (e_ref[...], et_ref[...]))
            _store_heads(o_ref, o, r)
        s_ref[...] = st

    return pl.pallas_call(
        body, name=name, grid=(ng, npair),
        in_specs=[blk(0), blk(1), blk(2), lbs] + [pl.BlockSpec(e.shape, lambda h, c: (0, 0)) for e in ecat],
        out_specs=[blk(0), state],
        out_shape=[jax.ShapeDtypeStruct((t_pad, HG_H * HD), F32), jax.ShapeDtypeStruct((npair, HG_H, HD, HD), F32)],
        scratch_shapes=[pltpu.VMEM((HG_G, HD, HD), F32)],
        compiler_params=_cparams(),
    )(p, p, p, lbraw, *ecat)


def _hg_bwd(p, lbraw, ecat, ssave, do, pad, name, cargo=(), exchange=None):
    t_pad = p.shape[0]
    npair = t_pad // PAIR
    ng, blk, lbs, state = _hg_specs(npair, True)
    n = len(cargo)

    def body(q_ref, f_ref, i_ref, lb_ref, e_ref, et_ref, ss_ref, do_ref, *rest):
        c = pl.program_id(1)
        hg = pl.program_id(0)
        ds_ref = rest[-1]
        (dq_ref, df_ref, di_ref, dlb_ref), end_cargo = _cargo_bounds(
            rest[:-1], n, 4, exchange, (hg == 0) & (c == 0), (hg == ng - 1) & (c == npair - 1))

        @pl.when(c == 0)
        def _():
            ds_ref[...] = jnp.zeros_like(ds_ref)
            dlb_ref[...] = jnp.zeros_like(dlb_ref)

        ra, rb = _chunk_rows(0), _chunk_rows(1)
        va, vb = _chunk_valid(npair - 1 - c, 0, pad), _chunk_valid(npair - 1 - c, 1, pad)
        ecv = (e_ref[...], et_ref[...])

        def pair(qa, fa, ia, qb, fb, ib, r0, r1, st):
            oa, st = _hg_chunk(qa, fa, ia, r0, r1, st, va, ecv)
            ob, st = _hg_chunk(qb, fb, ib, r0, r1, st, vb, ecv)
            return oa, ob, st

        ins = [ref[r, :] for r in (ra, rb) for ref in (q_ref, f_ref, i_ref)]
        _, vjp = jax.vjp(pair, *ins, lb_ref[0:1, :], lb_ref[1:2, :], ss_ref[0])
        g = vjp((_heads(do_ref[ra, :], HG_G), _heads(do_ref[rb, :], HG_G), ds_ref[...]))
        for r, (dq, df, di) in ((ra, g[0:3]), (rb, g[3:6])):
            dq_ref[r, :] = dq.astype(BF16)
            df_ref[r, :] = df.astype(BF16)
            di_ref[r, :] = di.astype(BF16)
        dlb_ref[0:1, :] += g[6]
        dlb_ref[1:2, :] += g[7]
        ds_ref[...] = g[8]
        end_cargo()

    sds = jax.ShapeDtypeStruct((t_pad, HG_H * HD), BF16)
    return pl.pallas_call(
        body, name=name, grid=(ng, npair),
        in_specs=[blk(0), blk(1), blk(2), lbs] + [pl.BlockSpec(e.shape, lambda h, c: (0, 0)) for e in ecat]
        + [state, blk(0)] + [ANY] * n,
        out_specs=[blk(0), blk(0), blk(0), lbs] + [ANY] * n,
        out_shape=[sds, sds, sds, jax.ShapeDtypeStruct((2, HG_H * HD), F32)] + (exchange[1](cargo) if n else []),
        scratch_shapes=(exchange[2](n) if n else []) + [pltpu.VMEM((HG_G, HD, HD), F32)],
        compiler_params=_cparams(),
    )(p, p, p, lbraw, *ecat, ssave, do, *cargo)


def _pad_ab_cols(w):
    z = jnp.zeros((w.shape[0], HD - GDN_H), w.dtype)
    return jnp.concatenate([w[:, :2048], w[:, 2048:2052], z, w[:, 2052:2056], z, w[:, 2056:]], axis=1)


def _unpad_ab_cols(w):
    return jnp.concatenate([w[:, :2048], w[:, 2048:2052], w[:, 2176:2180], w[:, 2304:]], axis=1)


def _lane_pad(v):
    return jnp.pad(v, ((0, 0), (0, HD - v.shape[1])))


def _mlp_fwd(h, hb, w1, w2, layer, g, b):
    a, r = _mm(hb, w1, b_view=("cols", layer), out_dtype=BF16, act=True, name=f"mlp_up_{layer}")
    m, y, yb = _mm(r, w2, b_view=("rows", layer), ln=(h, g, b), name=f"mlp_down_{layer}")
    return a, r, m, y, yb


def _mlp_bwd(hb, a, r, dmb, w1, w2, layer):
    da = _mm(dmb, w2, tb=True, b_view=("rows", layer), out_dtype=BF16, gate=a, name=f"mlp_down_dx_{layer}")
    dw2 = _mm(r, dmb, ta=True, out_dtype=BF16, name=f"mlp_down_dw_{layer}")
    dh = _mm(da, w1, tb=True, b_view=("cols", layer), name=f"mlp_up_dx_{layer}")
    dw1 = _mm(hb, da, ta=True, out_dtype=BF16, out_split=N_CHIP, name=f"mlp_up_dw_{layer}")
    return dh, dw1, dw2


def _local_step(h0, tgt, w, pad, late=None):
    row = lambda a, i: a[i:i + 1]
    ecat = tuple(jnp.asarray(e, dtype=BF16) for e in _hg_prefix_matrix())
    cw = [w["conv_w"][:, i * 512:(i + 1) * 512] for i in range(3)]
    alog, dtb = _lane_pad(w["a_log"]), _lane_pad(w["dt_bias"])

    h0b = h0.astype(BF16)
    p0 = _mm(h0b, w["ab_w_in"], name="ab_in")
    qn = _conv_fwd(p0, 0, cw[0], "q", pad, "conv_q")
    kn = _conv_fwd(p0, 4, cw[1], "k", pad, "conv_k")
    vn = _conv_fwd(p0, 8, cw[2], "v", pad, "conv_v")
    if late is None:
        oa_raw, ss0 = _gdn_fwd(qn, kn, vn, p0, alog, dtb, pad, "gdn_fwd")
        ob, rtot = _sb_fwd(p0, pad, "sb_fwd")
    else:
        oa_raw, ss0, g_cin, g_cout = _gdn_fwd(qn, kn, vn, p0, alog, dtb, pad, "gdn_fwd",
                                              cargo=[late["c_w_in"], late["c_w_out"]], exchange=GATHER)
        ob, rtot, g_about, g_w1, g_w2 = _sb_fwd(p0, pad, "sb_fwd", exchange=GATHER,
                                                cargo=[late["ab_w_out"], late["mlp_w1"], late["mlp_w2"]])
        w = dict(w, ab_w_out=g_about.reshape(D, D), c_w_in=g_cin, c_w_out=g_cout.reshape(D, D), mlp_w1=g_w1, mlp_w2=g_w2)
    oa = _grms_fwd(oa_raw, p0, 12, w["ab_gnorm_g"], "gdn_gate")
    ycat = jnp.concatenate([oa, ob], axis=1)
    mix0, h1, h1b = _mm(ycat, w["ab_w_out"], name="ab_out", ln=(h0, row(w["ln_mix_g"], 0), row(w["ln_mix_b"], 0)))
    a0, r0, m0, h2, h2b = _mlp_fwd(h1, h1b, w["mlp_w1"], w["mlp_w2"], 0, row(w["ln_ffn_g"], 0), row(w["ln_ffn_b"], 0))
    p1 = _mm(h2b, w["c_w_in"], b_view=("cols", 0), name="c_in")
    oc_raw, ss1 = _hg_fwd(p1, w["c_lb_raw"], ecat, pad, "hg_fwd")
    yc = _grms_fwd(oc_raw, p1, 3 * HG_H, w["c_gnorm_g"], "hg_gate")
    mix1, h3, h3b = _mm(yc, w["c_w_out"], name="c_out", ln=(h2, row(w["ln_mix_g"], 1), row(w["ln_mix_b"], 1)))
    a1, r1, m1, h4, _ = _mlp_fwd(h3, h3b, w["mlp_w1"], w["mlp_w2"], 1, row(w["ln_ffn_g"], 1), row(w["ln_ffn_b"], 1))
    loss, dh4 = _loss_fwd(h4, tgt, pad + N_META, "loss")

    dh3a, dm1b, dfg1, dfb1 = _ln_res_bwd(h3, m1, row(w["ln_ffn_g"], 1), row(w["ln_ffn_b"], 1), [dh4], "ln_ffn_bwd_1")
    dh3b, dw1_1, dw2_1 = _mlp_bwd(h3b, a1, r1, dm1b, w["mlp_w1"], w["mlp_w2"], 1)
    dh2a, dmix1b, dmg1, dmb1 = _ln_res_bwd(h2, mix1, row(w["ln_mix_g"], 1), row(w["ln_mix_b"], 1), [dh3a, dh3b], "ln_mix_bwd_1")
    dyc = _mm(dmix1b, w["c_w_out"], tb=True, name="c_out_dx")
    dwco = _mm(yc, dmix1b, ta=True, out_dtype=BF16, name="c_out_dw")
    doc, dzc, dcg = _grms_bwd(oc_raw, p1, 3 * HG_H, w["c_gnorm_g"], dyc, 0, "hg_gate_bwd")
    landed = {}
    rows4 = lambda a: a.reshape(N_CHIP, -1, D)
    if late is None:
        dq1, df1, di1, dlb = _hg_bwd(p1, w["c_lb_raw"], ecat, ss1, doc, pad, "hg_bwd")
    else:
        dq1, df1, di1, dlb, landed["w1_1"] = _hg_bwd(
            p1, w["c_lb_raw"], ecat, ss1, doc, pad, "hg_bwd", cargo=[dw1_1], exchange=SCATTER)
    dp1 = [dq1, df1, di1, dzc]
    dh2b = _mm_groups_nt(dp1, w["c_w_in"], "c_in_dx")
    dwc = jnp.stack([_mm(h2b, d, ta=True, out_dtype=BF16, name=f"c_in_dw_{i}") for i, d in enumerate(dp1)])
    dh1a, dm0b, dfg0, dfb0 = _ln_res_bwd(h1, m0, row(w["ln_ffn_g"], 0), row(w["ln_ffn_b"], 0), [dh2a, dh2b], "ln_ffn_bwd_0")
    dh1b, dw1_0, dw2_0 = _mlp_bwd(h1b, a0, r0, dm0b, w["mlp_w1"], w["mlp_w2"], 0)
    dh0a, dmix0b, dmg0, dmb0 = _ln_res_bwd(h0, mix0, row(w["ln_mix_g"], 0), row(w["ln_mix_b"], 0), [dh1a, dh1b], "ln_mix_bwd_0")
    dycat = _mm(dmix0b, w["ab_w_out"], tb=True, name="ab_out_dx")
    dwabo = _mm(ycat, dmix0b, ta=True, out_dtype=BF16, name="ab_out_dw")
    doa, dza, dag = _grms_bwd(oa_raw, p0, 12, w["ab_gnorm_g"], dycat, 0, "gdn_gate_bwd")
    if late is None:
        dqn, dkn, dvn, dbb, daa, dal, ddt = _gdn_bwd(qn, kn, vn, p0, alog, dtb, ss0, doa, pad, "gdn_bwd")
        dqb, dkb, dvb = _sb_bwd(p0, rtot, dycat, 4, pad, "sb_bwd")
    else:
        dqn, dkn, dvn, dbb, daa, dal, ddt, landed["c_w_in"] = _gdn_bwd(
            qn, kn, vn, p0, alog, dtb, ss0, doa, pad, "gdn_bwd", cargo=[dwc], exchange=SCATTER)
        (dqb, dkb, dvb, landed["w1_0"], landed["w2_0"], landed["w2_1"], landed["ab_w_out"],
         landed["c_w_out"]) = _sb_bwd(
            p0, rtot, dycat, 4, pad, "sb_bwd",
            cargo=[dw1_0, rows4(dw2_0), rows4(dw2_1), rows4(dwabo), rows4(dwco)], exchange=SCATTER)
    dpq, dcq = _conv_bwd(p0, 0, cw[0], dqn, "q", pad, "conv_q_bwd")
    dpk, dck = _conv_bwd(p0, 4, cw[1], dkn, "k", pad, "conv_k_bwd")
    dpv, dcv = _conv_bwd(p0, 8, cw[2], dvn, "v", pad, "conv_v_bwd")
    dp0 = _assemble_bf16([dpq, dpk, dpv, dza, dbb, daa, dqb, dkb, dvb], "ab_in_dy")
    dwab = _mm(h0b, dp0, ta=True, out_dtype=BF16, name="ab_in_dw")
    if late is None:
        dh0 = _mm(dp0, w["ab_w_in"], tb=True, plus=dh0a, name="ab_in_dx")
    else:
        dab = jnp.transpose(_unpad_ab_cols(dwab).reshape(D, N_CHIP, AB_TRUE // N_CHIP), (1, 0, 2))
        dh0, landed["ab_w_in"] = _mm(dp0, w["ab_w_in"], tb=True, plus=dh0a, name="ab_in_dx", cargo=[dab],
                                     exchange=SCATTER)

    grads = {
        "ab_w_in": dwab, "conv_w": jnp.concatenate([dcq, dck, dcv], axis=1),
        "a_log": dal[:, :GDN_H], "dt_bias": ddt[:, :GDN_H],
        "ab_gnorm_g": dag, "ab_w_out": dwabo, "c_w_in": dwc, "c_lb_raw": dlb, "c_gnorm_g": dcg, "c_w_out": dwco,
        "ln_mix_g": jnp.concatenate([dmg0, dmg1], 0), "ln_mix_b": jnp.concatenate([dmb0, dmb1], 0),
        "w1_0": dw1_0, "w1_1": dw1_1, "w2_0": dw2_0, "w2_1": dw2_1,
        "ln_ffn_g": jnp.concatenate([dfg0, dfg1], 0), "ln_ffn_b": jnp.concatenate([dfb0, dfb1], 0),
        "landed": landed,
    }
    return loss, dh0, grads


MESH = pl.DeviceIdType.MESH
ANY = pl.BlockSpec(memory_space=pl.ANY)
N_CHIP = 4
N_DEV = 8
CHIP_REL = ((1, 0), (0, 1), (1, 1))
DEV_REL = tuple((dx, dy, dc) for dx in (0, 1) for dy in (0, 1) for dc in (0, 1))[1:]

def _pos():
    return lax.axis_index("x"), lax.axis_index("y"), lax.axis_index("c")


def _flip(a, d):
    return a + d - 2 * a * d


class _Exchange:
    def __init__(self, local, sends, recvs):
        self.local, self.sends, self.recvs = local, sends, recvs

    def start(self):
        for cp in self.local + self.sends:
            cp.start()

    def wait(self):
        for cp in self.recvs:
            cp.wait_recv()
        for cp in self.sends:
            cp.wait_send()
        for cp in self.local:
            cp.wait()


def _gather_sems(n):
    return [pltpu.SemaphoreType.DMA((3 * n,)), pltpu.SemaphoreType.DMA((3 * n,)), pltpu.SemaphoreType.DMA((n,))]


def _gather_copies(x_refs, o_refs, send_sems, recv_sems, local_sems):
    n = len(x_refs)
    x, y, c = _pos()
    local = [pltpu.make_async_copy(x_refs[a], o_refs[a].at[2 * x + y], local_sems.at[a]) for a in range(n)]

    def copy(a, k, sending):
        tx, ty = _flip(x, CHIP_REL[k][0]), _flip(y, CHIP_REL[k][1])
        return pltpu.make_async_remote_copy(
            src_ref=x_refs[a], dst_ref=o_refs[a].at[2 * x + y if sending else 2 * tx + ty],
            send_sem=send_sems.at[3 * a + k], recv_sem=recv_sems.at[3 * a + k], device_id=(tx, ty, c), device_id_type=MESH)

    pairs = [(a, k) for a in range(n) for k in range(3)]
    return _Exchange(local, [copy(a, k, True) for a, k in pairs], [copy(a, k, False) for a, k in pairs])


def _gather_shapes(bufs):
    return [jax.ShapeDtypeStruct((N_CHIP,) + b.shape, b.dtype) for b in bufs]


def _chip_allgather(bufs, name):
    n = len(bufs)

    def body(*refs):
        ex = _gather_copies(refs[:n], refs[n:2 * n], *refs[2 * n:])
        ex.start()
        ex.wait()

    return pl.pallas_call(
        body, name=name, in_specs=[ANY] * n, out_specs=[ANY] * n, out_shape=_gather_shapes(bufs),
        scratch_shapes=_gather_sems(n), compiler_params=pltpu.CompilerParams(has_side_effects=True),
    )(*bufs)


def _scatter_sems(n):
    nr = N_DEV - 1
    return [pltpu.SemaphoreType.DMA((nr * n,)), pltpu.SemaphoreType.DMA((nr * n,)), pltpu.SemaphoreType.DMA((n,))]


def _scatter_copies(g_refs, o_refs, send_sems, recv_sems, local_sems):
    n = len(g_refs)
    nr = N_DEV - 1
    x, y, c = _pos()
    me = 4 * x + 2 * y + c
    local = [pltpu.make_async_copy(g_refs[a].at[2 * x + y], o_refs[a].at[me], local_sems.at[a]) for a in range(n)]

    def copy(a, k, sending):
        dx, dy, dc = DEV_REL[k]
        tx, ty, tc = _flip(x, dx), _flip(y, dy), _flip(c, dc)
        return pltpu.make_async_remote_copy(
            src_ref=g_refs[a].at[2 * tx + ty], dst_ref=o_refs[a].at[me if sending else 4 * tx + 2 * ty + tc],
            send_sem=send_sems.at[nr * a + k], recv_sem=recv_sems.at[nr * a + k],
            device_id=(tx, ty, tc), device_id_type=MESH)

    pairs = [(a, k) for a in range(n) for k in range(nr)]
    return _Exchange(local, [copy(a, k, True) for a, k in pairs], [copy(a, k, False) for a, k in pairs])


def _scatter_shapes(gs):
    return [jax.ShapeDtypeStruct((N_DEV,) + g.shape[1:], g.dtype) for g in gs]


GATHER = (_gather_copies, _gather_shapes, _gather_sems)
SCATTER = (_scatter_copies, _scatter_shapes, _scatter_sems)


def _sum_slots(rs, name):
    n, rh, w = rs[0].shape
    tr = _pick(rh, (256, 128, 64, 16))

    def body(*refs):
        o_ref = refs[-1]
        for layer, r_ref in enumerate(refs[:-1]):
            acc = r_ref[0].astype(F32)
            for s in range(1, n):
                acc = acc + r_ref[s].astype(F32)
            o_ref[layer] = acc

    return pl.pallas_call(
        body, name=name, grid=(rh // tr,), in_specs=[pl.BlockSpec((n, tr, w), lambda i: (0, i, 0))] * len(rs),
        out_specs=pl.BlockSpec((len(rs), tr, w), lambda i: (0, i, 0)),
        out_shape=jax.ShapeDtypeStruct((len(rs), rh, w), F32), compiler_params=_cparams(),
    )(*rs)


def _small_allreduce(buf, name):
    r, w = buf.shape

    def body(b_ref, o_ref, land_ref, send_sems, recv_sems):
        x, y, c = _pos()
        me = 4 * x + 2 * y + c
        land_ref[me] = b_ref[...]

        def target(k):
            dx, dy, dc = DEV_REL[k]
            return _flip(x, dx), _flip(y, dy), _flip(c, dc)

        sends = []
        for k in range(N_DEV - 1):
            tx, ty, tc = target(k)
            cp = pltpu.make_async_remote_copy(
                src_ref=b_ref, dst_ref=land_ref.at[me], send_sem=send_sems.at[k], recv_sem=recv_sems.at[k],
                device_id=(tx, ty, tc), device_id_type=MESH)
            cp.start()
            sends.append(cp)
        for k in range(N_DEV - 1):
            tx, ty, tc = target(k)
            pltpu.make_async_remote_copy(
                src_ref=b_ref, dst_ref=land_ref.at[4 * tx + 2 * ty + tc], send_sem=send_sems.at[k],
                recv_sem=recv_sems.at[k], device_id=(tx, ty, tc), device_id_type=MESH).wait_recv()
        for cp in sends:
            cp.wait_send()
        acc = land_ref[0]
        for s in range(1, N_DEV):
            acc = acc + land_ref[s]
        o_ref[...] = acc

    vm = pl.BlockSpec(memory_space=pltpu.VMEM)
    return pl.pallas_call(
        body, name=name, in_specs=[vm], out_specs=vm, out_shape=jax.ShapeDtypeStruct((r, w), F32),
        scratch_shapes=[pltpu.VMEM((N_DEV, r, w), F32), pltpu.SemaphoreType.DMA((N_DEV - 1,)),
                        pltpu.SemaphoreType.DMA((N_DEV - 1,))],
        compiler_params=pltpu.CompilerParams(has_side_effects=True),
    )(buf)


def _adamw(w, g, m, v, name):
    r, c = w.shape
    tr = _pick(r, (256, 128, 64, 8)) if r * c > (1 << 18) else r

    def body(w_ref, g_ref, m_ref, v_ref, d_ref, m2_ref, v2_ref):
        gg = g_ref[...]
        m2 = ADAM_B1 * m_ref[...] + (1.0 - ADAM_B1) * gg
        v2 = ADAM_B2 * v_ref[...] + (1.0 - ADAM_B2) * (gg * gg)
        m_hat = m2 / (1.0 - ADAM_B1 ** ADAM_STEP)
        v_hat = v2 / (1.0 - ADAM_B2 ** ADAM_STEP)
        d_ref[...] = -ADAM_LR * (m_hat / (jnp.sqrt(v_hat) + ADAM_EPS) + ADAM_WD * w_ref[...])
        m2_ref[...] = m2
        v2_ref[...] = v2

    blk = pl.BlockSpec((tr, c), lambda i: (i, 0))
    sds = jax.ShapeDtypeStruct((r, c), F32)
    return pl.pallas_call(body, name=name, grid=(r // tr,), in_specs=[blk] * 4, out_specs=[blk] * 3,
                          out_shape=[sds] * 3, compiler_params=_cparams())(w, g, m, v)


BIG = ("ab_w_in", "ab_w_out", "c_w_in", "c_w_out", "mlp_w1", "mlp_w2")
SMALL = ("ln_mix_g", "ln_mix_b", "ln_ffn_g", "ln_ffn_b", "c_lb_raw", "ab_a_log", "ab_dt_bias", "ab_gnorm_g", "c_gnorm_g")
SMALL_ROWS = 16
CONV_ROWS = 8
CONV_W = 3 * GDN_H * HD


def _conv_to_rows(cw):
    return jnp.pad(cw, ((0, 0), (0, 2 * D - CONV_W))).reshape(CONV_ROWS, D)


def _rows_to_conv(rows):
    return rows.reshape(CONV_K, 2 * D)[:, :CONV_W]


def _pack_small(d):
    rows = [jnp.pad(d[n], ((0, 0), (0, D - d[n].shape[1]))) for n in SMALL]
    buf = jnp.concatenate(rows, axis=0)
    return jnp.pad(buf, ((0, SMALL_ROWS - buf.shape[0]), (0, 0)))


def _unpack_small(buf, like):
    out, r = {}, 0
    for n in SMALL:
        nr, nc = like[n].shape
        out[n] = buf[r:r + nr, :nc]
        r += nr
    return out


def kernel(x, meta_tokens, ab_w_in, ab_conv_w, ab_a_log, ab_dt_bias, ab_gnorm_g, ab_w_out, c_w_in, c_lb_raw, c_gnorm_g, c_w_out, ln_mix_g, ln_mix_b, mlp_w1, mlp_w2, ln_ffn_g, ln_ffn_b, loss_target, m_meta_tokens, m_ab_w_in, m_ab_conv_w, m_ab_a_log, m_ab_dt_bias, m_ab_gnorm_g, m_ab_w_out, m_c_w_in, m_c_lb_raw, m_c_gnorm_g, m_c_w_out, m_ln_mix_g, m_ln_mix_b, m_mlp_w1, m_mlp_w2, m_ln_ffn_g, m_ln_ffn_b, v_meta_tokens, v_ab_w_in, v_ab_conv_w, v_ab_a_log, v_ab_dt_bias, v_ab_gnorm_g, v_ab_w_out, v_c_w_in, v_c_lb_raw, v_c_gnorm_g, v_c_w_out, v_ln_mix_g, v_ln_mix_b, v_mlp_w1, v_mlp_w2, v_ln_ffn_g, v_ln_ffn_b):
    names = ("meta_tokens", "ab_w_in", "ab_conv_w", "ab_a_log", "ab_dt_bias", "ab_gnorm_g", "ab_w_out", "c_w_in",
             "c_lb_raw", "c_gnorm_g", "c_w_out", "ln_mix_g", "ln_mix_b", "mlp_w1", "mlp_w2", "ln_ffn_g", "ln_ffn_b")
    wts = dict(zip(names, (meta_tokens, ab_w_in, ab_conv_w, ab_a_log, ab_dt_bias, ab_gnorm_g, ab_w_out, c_w_in, c_lb_raw,
                           c_gnorm_g, c_w_out, ln_mix_g, ln_mix_b, mlp_w1, mlp_w2, ln_ffn_g, ln_ffn_b)))
    mom_m = dict(zip(names, (m_meta_tokens, m_ab_w_in, m_ab_conv_w, m_ab_a_log, m_ab_dt_bias, m_ab_gnorm_g, m_ab_w_out,
                             m_c_w_in, m_c_lb_raw, m_c_gnorm_g, m_c_w_out, m_ln_mix_g, m_ln_mix_b, m_mlp_w1, m_mlp_w2,
                             m_ln_ffn_g, m_ln_ffn_b)))
    mom_v = dict(zip(names, (v_meta_tokens, v_ab_w_in, v_ab_conv_w, v_ab_a_log, v_ab_dt_bias, v_ab_gnorm_g, v_ab_w_out,
                             v_c_w_in, v_c_lb_raw, v_c_gnorm_g, v_c_w_out, v_ln_mix_g, v_ln_mix_b, v_mlp_w1, v_mlp_w2,
                             v_ln_ffn_g, v_ln_ffn_b)))
    seq = x.shape[1]
    pad = (-(N_META + seq)) % QB
    xi, yi, ci = _pos()
    chip = 2 * xi + yi

    gat_ab_in, = _chip_allgather([ab_w_in[0].astype(BF16)], "gather_weights")
    late = {"ab_w_out": ab_w_out[0].astype(BF16), "c_w_in": c_w_in.astype(BF16), "c_w_out": c_w_out[0].astype(BF16),
            "mlp_w1": mlp_w1.astype(BF16), "mlp_w2": mlp_w2.astype(BF16)}
    mcols, ccols = meta_tokens.shape[1], ab_conv_w.shape[2]
    place = jnp.concatenate([
        lax.dynamic_update_slice(jnp.zeros((N_META, D), F32), 0.5 * meta_tokens, (0, chip * mcols)),
        _conv_to_rows(lax.dynamic_update_slice(jnp.zeros((CONV_K, CONV_W), F32), 0.5 * ab_conv_w[0], (0, chip * ccols)))],
        axis=0)
    placed = _small_allreduce(place, "gather_meta")
    meta_full = placed[:N_META]

    w = {
        "ab_w_in": _pad_ab_cols(jnp.transpose(gat_ab_in, (1, 0, 2)).reshape(D, AB_TRUE)),
        "conv_w": _rows_to_conv(placed[N_META:]), "a_log": ab_a_log, "dt_bias": ab_dt_bias,
        "ab_gnorm_g": ab_gnorm_g, "c_lb_raw": c_lb_raw,
        "c_gnorm_g": c_gnorm_g, "ln_mix_g": ln_mix_g, "ln_mix_b": ln_mix_b, "ln_ffn_g": ln_ffn_g, "ln_ffn_b": ln_ffn_b,
    }

    h0 = jnp.concatenate([jnp.zeros((pad, D), F32), meta_full, x[0]], axis=0)
    tgt = jnp.concatenate([jnp.zeros((pad + N_META, D), F32), loss_target[0]], axis=0)
    loss8, dh0, g = _local_step(h0, tgt, w, pad, late)
    grad_x = dh0[pad + N_META:][None]

    gsmall = {"ln_mix_g": g["ln_mix_g"], "ln_mix_b": g["ln_mix_b"], "ln_ffn_g": g["ln_ffn_g"], "ln_ffn_b": g["ln_ffn_b"],
              "c_lb_raw": g["c_lb_raw"], "ab_a_log": g["a_log"], "ab_dt_bias": g["dt_bias"], "ab_gnorm_g": g["ab_gnorm_g"],
              "c_gnorm_g": g["c_gnorm_g"]}
    packed = _pack_small(gsmall).at[SMALL_ROWS - 1, :loss8.shape[1]].set(loss8[0])
    sbuf = jnp.concatenate([packed, dh0[pad:pad + N_META], _conv_to_rows(g["conv_w"])], axis=0)
    ssum = _small_allreduce(sbuf, "allreduce_small")
    loss = ssum[SMALL_ROWS - 1, 0]
    grads = _unpack_small(ssum[:SMALL_ROWS], wts)
    grads["meta_tokens"] = lax.dynamic_slice(ssum[SMALL_ROWS:SMALL_ROWS + N_META], (0, chip * mcols), (N_META, mcols))
    grads["ab_conv_w"] = lax.dynamic_slice(_rows_to_conv(ssum[SMALL_ROWS + N_META:]), (0, chip * ccols), (CONV_K, ccols))[None]

    landed = g["landed"]
    for n in ("ab_w_in", "ab_w_out", "c_w_in", "c_w_out"):
        grads[n] = _sum_slots([landed[n]], f"grad_sum_{n}")
    grads["mlp_w1"] = _sum_slots([landed["w1_0"], landed["w1_1"]], "grad_sum_mlp_w1")
    grads["mlp_w2"] = _sum_slots([landed["w2_0"], landed["w2_1"]], "grad_sum_mlp_w2")

    delta, new_m, new_v = {}, {}, {}
    for n in ("meta_tokens", "ab_conv_w") + BIG:
        shp = wts[n].shape
        to2 = lambda a: a.reshape(-1, shp[-1])
        d2, m2, v2 = _adamw(to2(wts[n]), to2(grads[n]), to2(mom_m[n]), to2(mom_v[n]), f"adamw_{n}")
        delta[n], new_m[n], new_v[n] = d2.reshape(shp), m2.reshape(shp), v2.reshape(shp)
    d2, m2, v2 = _adamw(_pack_small(wts), ssum[:SMALL_ROWS], _pack_small(mom_m), _pack_small(mom_v), "adamw_small")
    delta.update(_unpack_small(d2, wts))
    new_m.update(_unpack_small(m2, wts))
    new_v.update(_unpack_small(v2, wts))

    return (loss, grad_x, *[grads[n] for n in names], *[delta[n] for n in names], *[new_m[n] for n in names],
            *[new_v[n] for n in names])
```

```python
import functools

import numpy as np
import jax
import jax.numpy as jnp
from jax import lax
from jax.experimental import pallas as pl
from jax.experimental.pallas import tpu as pltpu

F32 = jnp.float32
BF16 = jnp.bfloat16

D = 1024
N_META = 16
DEPTH = 2
GDN_H = 4
SB_H = 8
SB_DH = 64
HG_H = 8
HD = 128
CH = 64
QB = 128
ALPHA = float((2 * DEPTH) ** 0.25)
LN_EPS = 1e-5
RMS_EPS = 1e-6
L2_EPS = 1e-6
NEG = -1e30

ADAM_LR = 0.001
ADAM_B1 = 0.9
ADAM_B2 = 0.999
ADAM_EPS = 1e-08
ADAM_WD = 0.01
ADAM_STEP = 10

AB_TRUE = 3592
V7X_VMEM_BYTES = 64 * 1024 * 1024
VMEM_LIMIT = V7X_VMEM_BYTES - 8 * 1024 * 1024

NN = ((1,), (0,))
NT = ((1,), (1,))
TN = ((0,), (0,))


def _cparams(**kw):
    return pltpu.CompilerParams(vmem_limit_bytes=VMEM_LIMIT, **kw)


def _dg(a, b, dims, mode):
    if mode == "h":
        return lax.dot_general(a, b, dims, precision=lax.Precision.HIGHEST, preferred_element_type=F32)
    if mode == "b":
        return lax.dot_general(a.astype(BF16), b.astype(BF16), dims, preferred_element_type=F32)
    ah, bh = a.astype(BF16), b.astype(BF16)
    al, bl = (a - ah.astype(F32)).astype(BF16), (b - bh.astype(F32)).astype(BF16)
    d = lambda x, y: lax.dot_general(x, y, dims, preferred_element_type=F32)
    return d(ah, bh) + (d(ah, bl) + d(al, bh))


def _make_dots(mode, batched=False):
    if batched:
        nn_d, nt_d, tn_d = (((2,), (1,)), ((0,), (0,))), (((2,), (2,)), ((0,), (0,))), (((1,), (1,)), ((0,), (0,)))
    else:
        nn_d, nt_d, tn_d = (NN, ((), ())), (NT, ((), ())), (TN, ((), ()))

    @jax.custom_vjp
    def nn(a, b):
        return _dg(a, b, nn_d, mode)

    @jax.custom_vjp
    def nt(a, b):
        return _dg(a, b, nt_d, mode)

    @jax.custom_vjp
    def tn(a, b):
        return _dg(a, b, tn_d, mode)

    nn.defvjp(lambda a, b: (nn(a, b), (a, b)), lambda r, g: (nt(g, r[1]), tn(r[0], g)))
    nt.defvjp(lambda a, b: (nt(a, b), (a, b)), lambda r, g: (nn(g, r[1]), tn(g, r[0])))
    tn.defvjp(lambda a, b: (tn(a, b), (a, b)), lambda r, g: (nt(r[1], g), nn(r[0], g)))
    return nn, nt, tn


hnn = _make_dots("h")[0]
bbnn, bbnt, bbtn = _make_dots("b", True)
mbnn, mbnt, mbtn = _make_dots("m", True)
hbnt = _make_dots("h", True)[1]


def _split3(x, axis):
    x1 = x.astype(BF16)
    r1 = x - x1.astype(F32)
    x2 = r1.astype(BF16)
    x3 = (r1 - x2.astype(F32)).astype(BF16)
    return jnp.concatenate([x1, x2, x3], axis=axis)


@jax.custom_vjp
def _mask_dot(e3, x):
    return lax.dot_general(e3[0], _split3(x, 0), (NN, ((), ())), preferred_element_type=F32)


def _mask_dot_bwd(e3, g):
    dx = lax.dot_general(e3[1], _split3(g, 0), (TN, ((), ())), preferred_element_type=F32)
    return (jnp.zeros_like(e3[0]), jnp.zeros_like(e3[1])), dx


_mask_dot.defvjp(lambda e3, x: (_mask_dot(e3, x), e3), _mask_dot_bwd)


def _heads(a, n):
    return jnp.concatenate([a[None, :, h * HD:(h + 1) * HD] for h in range(n)], axis=0)


def _sigmoid(x):
    return jax.nn.sigmoid(x)


def _silu(x):
    return x * jax.nn.sigmoid(x)


def _softplus(x):
    return jnp.maximum(x, 0.0) + jnp.log(1.0 + jnp.exp(-jnp.abs(x)))


def _iota(shape, dim):
    return lax.broadcasted_iota(jnp.int32, shape, dim)


def _pick(n, prefs):
    for p in prefs:
        if n % p == 0:
            return p
    return n


def _mm(a, b, *, ta=False, tb=False, out_dtype=F32, name, b_view=None, out_split=0, act=False, gate=None, plus=None,
        ln=None, cargo=(), exchange=None):
    if ta:
        k_dim, m_dim = a.shape
    else:
        m_dim, k_dim = a.shape
    if b_view is None:
        w_rows, w_cols = b.shape
    else:
        kind, layer = b_view
        nj, _, blk_r, blk_c = b.shape
        w_rows, w_cols = (blk_r, nj * blk_c) if kind == "cols" else (nj * blk_r, blk_c)
    n_dim = w_rows if tb else w_cols
    assert (w_cols if tb else w_rows) == k_dim
    tm = _pick(m_dim, (1024, 1056, 704, 640, 512, 384, 256, 128))
    tn = _pick(n_dim, (1024, 1056, 704, 640, 512, 384, 256, 128))
    tk = _pick(k_dim, (1024, 1056, 704, 512, 384, 256, 128))
    nk = k_dim // tk
    a_spec = pl.BlockSpec((tk, tm), lambda i, j, k: (k, i)) if ta else pl.BlockSpec((tm, tk), lambda i, j, k: (i, k))
    wb = (tn, tk) if tb else (tk, tn)
    w_idx = (lambda i, j, k: (j, k)) if tb else (lambda i, j, k: (k, j))
    if b_view is None:
        b_spec = pl.BlockSpec(wb, w_idx)
    elif kind == "cols":
        per = blk_c // wb[1]
        b_spec = pl.BlockSpec((None, None) + wb,
                              lambda i, j, k: (w_idx(i, j, k)[1] // per, layer, w_idx(i, j, k)[0], w_idx(i, j, k)[1] % per))
    else:
        per = blk_r // wb[0]
        b_spec = pl.BlockSpec((None, None) + wb,
                              lambda i, j, k: (w_idx(i, j, k)[0] // per, layer, w_idx(i, j, k)[0] % per, w_idx(i, j, k)[1]))
    if out_split:
        per_o = (n_dim // out_split) // tn
        out_spec = pl.BlockSpec((None, tm, tn), lambda i, j, k: (j // per_o, i, j % per_o))
        out_sds = jax.ShapeDtypeStruct((out_split, m_dim, n_dim // out_split), out_dtype)
    else:
        out_spec = pl.BlockSpec((tm, tn), lambda i, j, k: (i, j))
        out_sds = jax.ShapeDtypeStruct((m_dim, n_dim), out_dtype)
    dims = (((0 if ta else 1,), (1 if tb else 0,)), ((), ()))
    assert sum(e is not None for e in (gate, plus, ln)) <= 1
    extra = [e for e in (gate, plus) if e is not None] + list(ln or ())
    n_out = 2 if act else 3 if ln else 1
    assert ln is None or (tn == n_dim and not out_split)

    def finish(acc, refs):
        if ln:
            y = _ln_res_fn(refs[0][...], acc, refs[1][...], refs[2][...])
            refs[3][...] = acc
            refs[4][...] = y
            refs[5][...] = y.astype(BF16)
        elif act:
            refs[0][...] = acc.astype(refs[0].dtype)
            r = jnp.maximum(acc, 0.0)
            refs[1][...] = (r * r).astype(refs[1].dtype)
        elif gate is not None:
            refs[1][...] = (acc * (2.0 * jnp.maximum(refs[0][...].astype(F32), 0.0))).astype(refs[1].dtype)
        elif plus is not None:
            refs[1][...] = (refs[0][...] + acc).astype(refs[1].dtype)
        else:
            refs[0][...] = acc.astype(refs[0].dtype)

    grid = (m_dim // tm, n_dim // tn, nk)
    nc = len(cargo)

    def body(a_ref, b_ref, *rest):
        acc_ref = rest[-1]
        ids = [pl.program_id(d) for d in range(3)]
        outs, end_cargo = _cargo_bounds(
            rest[len(extra):-1], nc, n_out, exchange, (ids[0] == 0) & (ids[1] == 0) & (ids[2] == 0),
            (ids[0] == grid[0] - 1) & (ids[1] == grid[1] - 1) & (ids[2] == grid[2] - 1))
        refs = tuple(rest[:len(extra)]) + tuple(outs)
        part = lax.dot_general(a_ref[...], b_ref[...], dims, preferred_element_type=F32)
        if nk == 1:
            finish(part, refs)
        else:
            k = ids[2]

            @pl.when(k == 0)
            def _():
                acc_ref[...] = part

            @pl.when(k > 0)
            def _():
                acc_ref[...] += part

            @pl.when(k == nk - 1)
            def _():
                finish(acc_ref[...], refs)
        end_cargo()

    tile = pl.BlockSpec((tm, tn), lambda i, j, k: (i, j))
    rowv = pl.BlockSpec((1, tn), lambda i, j, k: (0, j))
    out_sdss = [out_sds] * n_out
    if ln:
        out_sdss = [jax.ShapeDtypeStruct((m_dim, n_dim), dt) for dt in (F32, F32, BF16)]
    out = pl.pallas_call(
        body, name=name, grid=grid,
        in_specs=[a_spec, b_spec] + ([tile, rowv, rowv] if ln else [tile] * len(extra)) + [ANY] * nc,
        out_specs=[out_spec] * n_out + [ANY] * nc,
        out_shape=out_sdss + (exchange[1](cargo) if nc else []),
        scratch_shapes=(exchange[2](nc) if nc else []) + [pltpu.VMEM((tm, tn) if nk > 1 else (8, 128), F32)],
        compiler_params=_cparams(dimension_semantics=("arbitrary",) * 3 if nc else ("parallel", "parallel", "arbitrary")),
    )(a, b, *extra, *cargo)
    if nc:
        return out
    return out if (act or ln) else out[0]


def _mm_groups_nt(parts, b, name):
    m_dim, k_dim = parts[0].shape
    ng, _, n_dim, _ = b.shape
    assert len(parts) == ng and b.shape[3] == k_dim
    tm = _pick(m_dim, (1056, 704, 512, 384, 256, 128))

    def body(*refs):
        a_refs, b_ref, o_ref, acc_ref = refs[:ng], refs[ng], refs[ng + 1], refs[ng + 2]
        k = pl.program_id(1)
        for g in range(ng):
            @pl.when(k == g)
            def _(g=g):
                part = lax.dot_general(a_refs[g][...], b_ref[...], (NT, ((), ())), preferred_element_type=F32)
                if g == 0:
                    acc_ref[...] = part
                elif g < ng - 1:
                    acc_ref[...] += part
                else:
                    o_ref[...] = acc_ref[...] + part

    return pl.pallas_call(
        body, name=name, grid=(m_dim // tm, ng),
        in_specs=[pl.BlockSpec((tm, k_dim), lambda i, k: (i, 0))] * ng
        + [pl.BlockSpec((None, None, n_dim, k_dim), lambda i, k: (k, 0, 0, 0))],
        out_specs=pl.BlockSpec((tm, n_dim), lambda i, k: (i, 0)),
        out_shape=jax.ShapeDtypeStruct((m_dim, n_dim), F32),
        scratch_shapes=[pltpu.VMEM((tm, n_dim), F32)],
        compiler_params=_cparams(dimension_semantics=("parallel", "arbitrary")),
    )(*parts, b)


def _row_tile(t_pad, width):
    for tr in (528, 352, 176, 128, 64):
        if t_pad % tr == 0 and tr * width * 4 <= (3 << 19) and tr % 16 == 0:
            return tr
    return 64 if t_pad % 64 == 0 else t_pad


def _ln_res_fn(h, m, g, b):
    x = ALPHA * h + m
    mu = jnp.mean(x, axis=-1, keepdims=True)
    xc = x - mu
    var = jnp.mean(xc * xc, axis=-1, keepdims=True)
    return xc * lax.rsqrt(var + LN_EPS) * g + b


def _ln_res_bwd(h, m, g, b, dys, name):
    t_pad = h.shape[0]
    tr = _row_tile(t_pad, D)
    nd = len(dys)

    def body(h_ref, m_ref, g_ref, b_ref, *rest):
        d_refs, (dh_ref, dm_ref, dg_ref, db_ref) = rest[:nd], rest[nd:]
        _, vjp = jax.vjp(_ln_res_fn, h_ref[...], m_ref[...], g_ref[...], b_ref[...])
        dy = d_refs[0][...]
        for d_ref in d_refs[1:]:
            dy = dy + d_ref[...]
        dh, dm, dg, db = vjp(dy)
        dh_ref[...] = dh
        dm_ref[...] = dm.astype(BF16)

        @pl.when(pl.program_id(0) == 0)
        def _():
            dg_ref[...] = jnp.zeros_like(dg_ref)
            db_ref[...] = jnp.zeros_like(db_ref)

        dg_ref[...] += dg
        db_ref[...] += db

    row = pl.BlockSpec((tr, D), lambda i: (i, 0))
    par = pl.BlockSpec((1, D), lambda i: (0, 0))
    return pl.pallas_call(
        body, name=name, grid=(t_pad // tr,), in_specs=[row, row, par, par] + [row] * nd,
        out_specs=[row, row, par, par],
        out_shape=[jax.ShapeDtypeStruct((t_pad, D), F32), jax.ShapeDtypeStruct((t_pad, D), BF16),
                   jax.ShapeDtypeStruct((1, D), F32), jax.ShapeDtypeStruct((1, D), F32)],
        compiler_params=_cparams(),
    )(h, m, g, b, *dys)


def _grms_fn(o, z, g):
    y = o * lax.rsqrt(jnp.mean(o * o, axis=-1, keepdims=True) + RMS_EPS) * g
    return y * _silu(z)


def _grms_fwd(o, z_arr, z_blk0, g, name):
    t_pad, w = o.shape
    tr = _row_tile(t_pad, w)
    assert (z_blk0 * HD) % w == 0

    def body(o_ref, z_ref, g_ref, y_ref):
        for h in range(w // HD):
            c = slice(h * HD, (h + 1) * HD)
            y_ref[:, c] = _grms_fn(o_ref[:, c], z_ref[:, c], g_ref[...]).astype(BF16)

    return pl.pallas_call(
        body, name=name, grid=(t_pad // tr,),
        in_specs=[pl.BlockSpec((tr, w), lambda i: (i, 0)), pl.BlockSpec((tr, w), lambda i: (i, z_blk0 * HD // w)),
                  pl.BlockSpec((1, HD), lambda i: (0, 0))],
        out_specs=pl.BlockSpec((tr, w), lambda i: (i, 0)),
        out_shape=jax.ShapeDtypeStruct((t_pad, w), BF16), compiler_params=_cparams(),
    )(o, z_arr, g)


def _grms_bwd(o, z_arr, z_blk0, g, dy_arr, dy_blk0, name):
    t_pad, w = o.shape
    tr = _row_tile(t_pad, w)
    assert (z_blk0 * HD) % w == 0 and (dy_blk0 * HD) % w == 0

    def body(o_ref, z_ref, g_ref, dy_ref, do_ref, dz_ref, dg_ref):
        @pl.when(pl.program_id(0) == 0)
        def _():
            dg_ref[...] = jnp.zeros_like(dg_ref)

        for h in range(w // HD):
            c = slice(h * HD, (h + 1) * HD)
            _, vjp = jax.vjp(_grms_fn, o_ref[:, c], z_ref[:, c], g_ref[...])
            do, dz, dg = vjp(dy_ref[:, c])
            do_ref[:, c] = do
            dz_ref[:, c] = dz.astype(BF16)
            dg_ref[...] += dg

    blk = pl.BlockSpec((tr, w), lambda i: (i, 0))
    return pl.pallas_call(
        body, name=name, grid=(t_pad // tr,),
        in_specs=[blk, pl.BlockSpec((tr, w), lambda i: (i, z_blk0 * HD // w)), pl.BlockSpec((1, HD), lambda i: (0, 0)),
                  pl.BlockSpec((tr, w), lambda i: (i, dy_blk0 * HD // w))],
        out_specs=[blk, blk, pl.BlockSpec((1, HD), lambda i: (0, 0))],
        out_shape=[jax.ShapeDtypeStruct((t_pad, w), F32), jax.ShapeDtypeStruct((t_pad, w), BF16),
                   jax.ShapeDtypeStruct((1, HD), F32)],
        compiler_params=_cparams(),
    )(o, z_arr, g, dy_arr)


def _loss_fwd(y, tgt, first_row, name):
    t_pad = y.shape[0]
    tr = _row_tile(t_pad, D)

    def body(y_ref, t_ref, l_ref, dy_ref):
        rows = pl.program_id(0) * tr + _iota((tr, 1), 0)
        err = jnp.where(rows >= first_row, y_ref[...] - t_ref[...], 0.0)
        dy_ref[...] = err * (1.0 / D)

        @pl.when(pl.program_id(0) == 0)
        def _():
            l_ref[...] = jnp.zeros_like(l_ref)

        part = jnp.sum(jnp.sum(err * err, axis=1, keepdims=True), axis=0, keepdims=True)
        l_ref[...] += jnp.broadcast_to(part * (0.5 / D), l_ref.shape)

    row = pl.BlockSpec((tr, D), lambda i: (i, 0))
    return pl.pallas_call(
        body, name=name, grid=(t_pad // tr,), in_specs=[row, row],
        out_specs=[pl.BlockSpec((8, 128), lambda i: (0, 0)), row],
        out_shape=[jax.ShapeDtypeStruct((8, 128), F32), jax.ShapeDtypeStruct((t_pad, D), F32)],
        compiler_params=_cparams(),
    )(y, tgt)


def _assemble_bf16(parts, name):
    t_pad = parts[0].shape[0]
    widths = [p.shape[1] for p in parts]
    total = sum(widths)
    tr = _row_tile(t_pad, total)

    def body(*refs):
        o_ref = refs[-1]
        off = 0
        for ref, w in zip(refs[:-1], widths):
            o_ref[:, off:off + w] = ref[...].astype(BF16)
            off += w

    return pl.pallas_call(
        body, name=name, grid=(t_pad // tr,), in_specs=[pl.BlockSpec((tr, w), lambda i: (i, 0)) for w in widths],
        out_specs=pl.BlockSpec((tr, total), lambda i: (i, 0)),
        out_shape=jax.ShapeDtypeStruct((t_pad, total), BF16), compiler_params=_cparams(),
    )(*parts)


CONV_K = 4
HALO = 8
RT = 128


def _conv_fwd(p, blk0, w, mode, pad, name):
    t_pad = p.shape[0]
    nt = t_pad // RT
    scale = HD ** -0.5 if mode == "q" else 1.0

    def body(x_ref, w_ref, y_ref, xs_ref):
        xs_ref[0:HALO, :] = jnp.zeros((HALO, HD), F32)
        rows = _iota((t_pad, 1), 0)
        xs_ref[HALO:HALO + t_pad, :] = jnp.where(rows >= pad, x_ref[...], 0.0)
        wv = w_ref[...]

        def tile(i, carry):
            r0 = pl.multiple_of(i * RT, RT)
            ext = xs_ref[pl.ds(r0, RT + HALO), :]
            acc = ext[HALO:, :] * wv[3:4, :]
            for s in (1, 2, 3):
                acc = acc + pltpu.roll(ext, s, 0)[HALO:, :] * wv[3 - s:4 - s, :]
            y = _silu(acc)
            if mode != "v":
                y = y * lax.rsqrt(jnp.sum(y * y, axis=-1, keepdims=True) + L2_EPS) * scale
            y_ref[pl.ds(r0, RT), :] = y
            return carry

        lax.fori_loop(0, nt, tile, 0)

    return pl.pallas_call(
        body, name=name, grid=(GDN_H,),
        in_specs=[pl.BlockSpec((t_pad, HD), lambda h: (0, blk0 + h)), pl.BlockSpec((CONV_K, HD), lambda h: (0, h))],
        out_specs=pl.BlockSpec((t_pad, HD), lambda h: (0, h)),
        out_shape=jax.ShapeDtypeStruct((t_pad, GDN_H * HD), F32),
        scratch_shapes=[pltpu.VMEM((t_pad + HALO, HD), F32)],
        compiler_params=_cparams(),
    )(p, w)


def _conv_bwd(p, blk0, w, dn, mode, pad, name):
    t_pad = p.shape[0]
    nt = t_pad // RT
    scale = HD ** -0.5 if mode == "q" else 1.0

    def body(x_ref, w_ref, dn_ref, dx_ref, dw_ref, xs_ref, ds_ref):
        xs_ref[0:HALO, :] = jnp.zeros((HALO, HD), F32)
        xs_ref[HALO + t_pad:HALO + t_pad + 2 * HALO, :] = jnp.zeros((2 * HALO, HD), F32)
        ds_ref[t_pad:t_pad + HALO, :] = jnp.zeros((HALO, HD), F32)
        rows = _iota((t_pad, 1), 0)
        xs_ref[HALO:HALO + t_pad, :] = jnp.where(rows >= pad, x_ref[...], 0.0)
        ds_ref[0:t_pad, :] = dn_ref[...]
        wv = w_ref[...]

        def tile(i, dw):
            r0 = pl.multiple_of(i * RT, RT)
            ext = xs_ref[pl.ds(r0, RT + 2 * HALO), :]
            dn_e = ds_ref[pl.ds(r0, RT + HALO), :]
            xsh = [ext[HALO:, :]] + [pltpu.roll(ext, s, 0)[HALO:, :] for s in (1, 2, 3)]
            pre = xsh[0] * wv[3:4, :]
            for s in (1, 2, 3):
                pre = pre + xsh[s] * wv[3 - s:4 - s, :]
            sg = _sigmoid(pre)
            y = pre * sg
            if mode != "v":
                ss = jnp.sum(y * y, axis=-1, keepdims=True) + L2_EPS
                r = lax.rsqrt(ss)
                dy = scale * (dn_e * r - y * (r * r * r) * jnp.sum(dn_e * y, axis=-1, keepdims=True))
            else:
                dy = dn_e
            dpre = dy * (sg * (1.0 + pre * (1.0 - sg)))
            dx = dpre[:RT, :] * wv[3:4, :]
            for s in (1, 2, 3):
                dx = dx + pltpu.roll(dpre, RT + HALO - s, 0)[:RT, :] * wv[3 - s:4 - s, :]
            trow = r0 + _iota((RT, 1), 0)
            dx_ref[pl.ds(r0, RT), :] = jnp.where(trow >= pad, dx, 0.0)
            new = []
            for s in (0, 1, 2, 3):
                new.append(dw[s] + jnp.sum(dpre[:RT, :] * xsh[s][:RT, :], axis=0, keepdims=True))
            return tuple(new)

        z = jnp.zeros((1, HD), F32)
        dw = lax.fori_loop(0, nt, tile, (z, z, z, z))
        for s in (0, 1, 2, 3):
            dw_ref[3 - s:4 - s, :] = dw[s]

    return pl.pallas_call(
        body, name=name, grid=(GDN_H,),
        in_specs=[pl.BlockSpec((t_pad, HD), lambda h: (0, blk0 + h)), pl.BlockSpec((CONV_K, HD), lambda h: (0, h)),
                  pl.BlockSpec((t_pad, HD), lambda h: (0, h))],
        out_specs=[pl.BlockSpec((t_pad, HD), lambda h: (0, h)), pl.BlockSpec((CONV_K, HD), lambda h: (0, h))],
        out_shape=[jax.ShapeDtypeStruct((t_pad, GDN_H * HD), F32), jax.ShapeDtypeStruct((CONV_K, GDN_H * HD), F32)],
        scratch_shapes=[pltpu.VMEM((t_pad + 3 * HALO, HD), F32), pltpu.VMEM((t_pad + HALO, HD), F32)],
        compiler_params=_cparams(),
    )(p, w, dn)


@jax.custom_vjp
def _unit_lower_inv(m, bd, eye):
    md = m * bd
    low = m - md
    p2 = mbnn(md, md)
    p4 = mbnn(p2, p2)
    dinv = mbnn(mbnn(eye - md, eye + p2), eye + p4)
    n = mbnn(dinv, low)
    n2 = mbnn(n, n)
    n4 = mbnn(n2, n2)
    return mbnn(mbnn(mbnn(eye - n, eye + n2), eye + n4), dinv)


def _unit_lower_inv_bwd(res, g):
    t, bd, eye = res
    return -mbtn(t, mbnt(g, t)), jnp.zeros_like(bd), jnp.zeros_like(eye)


def _unit_lower_inv_fwd(m, bd, eye):
    t = _unit_lower_inv(m, bd, eye)
    return t, (t, bd, eye)


_unit_lower_inv.defvjp(_unit_lower_inv_fwd, _unit_lower_inv_bwd)


def _gdn_chunks(chunks, alog, dtb, s):
    nh = chunks[0][0].shape[0]
    ri = _iota((1, CH, CH), 1)
    ci = _iota((1, CH, CH), 2)
    causal = ri >= ci
    strict = ri > ci
    eye = (ri == ci).astype(F32)
    bd = ((ri >> 3) == (ci >> 3)).astype(F32)
    ltri = (_iota((CH, CH), 0) >= _iota((CH, CH), 1)).astype(F32)
    sel = (_iota((nh, 1, HD), 2) == _iota((nh, 1, HD), 0)).astype(F32)
    last = _iota((1, CH, 1), 1) == CH - 1

    beta, gc, gc_rows = [], [], []
    for _, _, _, bb, aa, valid in chunks:
        beta_all = jnp.where(valid, _sigmoid(bb), 0.0)
        g_all = jnp.where(valid, -jnp.exp(alog) * _softplus(aa + dtb), 0.0)
        gc_all = hnn(ltri, g_all)
        beta.append(jnp.sum(beta_all[None] * sel, axis=2, keepdims=True))
        gc.append(jnp.sum(gc_all[None] * sel, axis=2, keepdims=True))
        gc_rows.append(hbnt(jnp.broadcast_to(sel, (nh, CH, HD)), jnp.broadcast_to(gc_all[None], (nh, CH, HD))))
    cat = lambda xs: jnp.concatenate(xs, axis=0)
    q, k, v = (cat([c[j] for c in chunks]) for j in range(3))
    beta, gc, gc_rows = cat(beta), cat(gc), cat(gc_rows)
    gc_last = jnp.sum(jnp.where(last, gc, 0.0), axis=1, keepdims=True)
    decay = jnp.exp(jnp.where(causal, gc - gc_rows, NEG))
    egc = jnp.exp(gc)

    kb = k * beta
    m = jnp.where(strict, bbnt(kb, k) * decay, 0.0)
    t_inv = _unit_lower_inv(m, bd, eye)
    u = bbnn(t_inv, v * beta)
    w = bbnn(t_inv, kb * egc)
    a_intra = bbnt(q, k) * decay
    q_dec = q * egc
    k_dec = k * jnp.exp(gc_last - gc)
    g_tot = jnp.exp(gc_last)

    outs = []
    for n in range(len(chunks)):
        part = lambda a: a[n * nh:(n + 1) * nh]
        v_new = part(u) - bbnn(part(w), s)
        outs.append(bbnn(part(q_dec), s) + bbnn(part(a_intra), v_new))
        s = s * part(g_tot) + bbtn(part(k_dec), v_new)
    return outs, s


PAIR = 2 * CH


def _gdn_specs(npair, rev):
    cc = (lambda c: npair - 1 - c) if rev else (lambda c: c)
    wide = pl.BlockSpec((PAIR, GDN_H * HD), lambda c: (cc(c), 0))
    fix = lambda off: pl.BlockSpec((PAIR, HD), lambda c: (cc(c), off))
    par = pl.BlockSpec((1, HD), lambda c: (0, 0))
    state = pl.BlockSpec((1, GDN_H, HD, HD), lambda c: (cc(c), 0, 0, 0))
    return wide, fix, par, state


def _store_heads(ref, a, rows=slice(None)):
    for h in range(a.shape[0]):
        ref[rows, h * HD:(h + 1) * HD] = a[h]


def _chunk_rows(half):
    return slice(half * CH, (half + 1) * CH)


def _chunk_valid(pair, half, pad):
    return ((2 * pair + half) * CH + _iota((CH, 1), 0)) >= pad


def _gdn_fwd(qn, kn, vn, p, alog, dtb, pad, name, cargo=(), exchange=None):
    t_pad = qn.shape[0]
    npair = t_pad // PAIR
    wide, fix, par, state = _gdn_specs(npair, False)
    n = len(cargo)

    def body(q_ref, k_ref, v_ref, bb_ref, aa_ref, al_ref, dt_ref, *rest):
        c = pl.program_id(0)
        s_ref = rest[-1]
        (o_ref, ss_ref), end_cargo = _cargo_bounds(rest[:-1], n, 2, exchange, c == 0, c == npair - 1)

        @pl.when(c == 0)
        def _():
            s_ref[...] = jnp.zeros_like(s_ref)

        s = s_ref[...]
        ss_ref[0] = s
        rows = [_chunk_rows(half) for half in (0, 1)]
        chunks = [(_heads(q_ref[r, :], GDN_H), _heads(k_ref[r, :], GDN_H), _heads(v_ref[r, :], GDN_H),
                   bb_ref[r, :], aa_ref[r, :], _chunk_valid(c, half, pad)) for half, r in enumerate(rows)]
        outs, s = _gdn_chunks(chunks, al_ref[...], dt_ref[...], s)
        for r, o in zip(rows, outs):
            _store_heads(o_ref, o, r)
        s_ref[...] = s
        end_cargo()

    return pl.pallas_call(
        body, name=name, grid=(npair,),
        in_specs=[wide, wide, wide, fix(16), fix(17), par, par] + [ANY] * n,
        out_specs=[wide, state] + [ANY] * n,
        out_shape=[jax.ShapeDtypeStruct((t_pad, GDN_H * HD), F32), jax.ShapeDtypeStruct((npair, GDN_H, HD, HD), F32)]
        + (exchange[1](cargo) if n else []),
        scratch_shapes=(exchange[2](n) if n else []) + [pltpu.VMEM((GDN_H, HD, HD), F32)],
        compiler_params=_cparams(),
    )(qn, kn, vn, p, p, alog, dtb, *cargo)


def _gdn_bwd(qn, kn, vn, p, alog, dtb, ssave, do, pad, name, cargo=(), exchange=None):
    t_pad = qn.shape[0]
    npair = t_pad // PAIR
    wide, fix, par, state = _gdn_specs(npair, True)
    n = len(cargo)

    def body(q_ref, k_ref, v_ref, bb_ref, aa_ref, al_ref, dt_ref, ss_ref, do_ref, *rest):
        c = pl.program_id(0)
        ds_ref = rest[-1]
        (dq_ref, dk_ref, dv_ref, dbb_ref, daa_ref, dal_ref, ddt_ref), end_cargo = _cargo_bounds(
            rest[:-1], n, 7, exchange, c == 0, c == npair - 1)

        @pl.when(c == 0)
        def _():
            ds_ref[...] = jnp.zeros_like(ds_ref)
            dal_ref[...] = jnp.zeros_like(dal_ref)
            ddt_ref[...] = jnp.zeros_like(ddt_ref)

        ra, rb = _chunk_rows(0), _chunk_rows(1)
        va, vb = _chunk_valid(npair - 1 - c, 0, pad), _chunk_valid(npair - 1 - c, 1, pad)

        def pair(qa, ka, va_, ba, aa, qb, kb, vb_, bb, ab, al, dt, s):
            (oa, ob), s = _gdn_chunks([(qa, ka, va_, ba, aa, va), (qb, kb, vb_, bb, ab, vb)], al, dt, s)
            return oa, ob, s

        ins = [f(ref[r, :]) for r in (ra, rb)
               for ref, f in ((q_ref, lambda a: _heads(a, GDN_H)), (k_ref, lambda a: _heads(a, GDN_H)),
                              (v_ref, lambda a: _heads(a, GDN_H)), (bb_ref, lambda a: a), (aa_ref, lambda a: a))]
        _, vjp = jax.vjp(pair, *ins, al_ref[...], dt_ref[...], ss_ref[0])
        g = vjp((_heads(do_ref[ra, :], GDN_H), _heads(do_ref[rb, :], GDN_H), ds_ref[...]))
        for r, (dq, dk, dv, dbb, daa) in ((ra, g[0:5]), (rb, g[5:10])):
            _store_heads(dq_ref, dq, r)
            _store_heads(dk_ref, dk, r)
            _store_heads(dv_ref, dv, r)
            dbb_ref[r, :] = dbb
            daa_ref[r, :] = daa
        dal_ref[...] += g[10]
        ddt_ref[...] += g[11]
        ds_ref[...] = g[12]
        end_cargo()

    sds = jax.ShapeDtypeStruct
    return pl.pallas_call(
        body, name=name, grid=(npair,),
        in_specs=[wide, wide, wide, fix(16), fix(17), par, par, state, wide] + [ANY] * n,
        out_specs=[wide, wide, wide, fix(0), fix(0), par, par] + [ANY] * n,
        out_shape=[sds((t_pad, GDN_H * HD), F32)] * 3 + [sds((t_pad, HD), F32)] * 2 + [sds((1, HD), F32)] * 2
        + (exchange[1](cargo) if n else []),
        scratch_shapes=(exchange[2](n) if n else []) + [pltpu.VMEM((GDN_H, HD, HD), F32)],
        compiler_params=_cparams(),
    )(qn, kn, vn, p, p, alog, dtb, ssave, do, *cargo)


SB_Q0, SB_K0, SB_V0 = 18, 22, 26
SB_SCALE = SB_DH ** -0.5
SB_NB_FWD, SB_NB_BWD = 11, 8


def _sb_terms(z, allowed):
    nz = -z
    raw = jnp.minimum(nz, 0.0) - jnp.log(1.0 + jnp.exp(jnp.minimum(z, nz)))
    l1m = raw if allowed is None else jnp.where(allowed, raw, 0.0)
    ls = z + raw
    return l1m, ls, jnp.exp(ls)


def _sb_passes(i, step, carry, per):
    total = i + 1

    def sized(done, first):
        return [functools.partial(step, done, masked=({0} if first else set()) | {nb - 1}, nb=nb)
                for nb in range(1, per + 1)]

    def several(c):
        n_mid = (total - per - 1) // per
        c = step(0, c, masked={0}, nb=per)
        c = lax.fori_loop(0, n_mid, lambda t, cc: step(per * (1 + t), cc, masked=set(), nb=per), c)
        done = per * (1 + n_mid)
        return lax.switch(total - done - 1, sized(done, False), c)

    return lax.cond(total <= per, lambda c: lax.switch(total - 1, sized(0, True), c), several, carry)


def _sb_stack(a, i):
    first = _iota((1, HD), 1) < SB_DH
    a2 = jnp.concatenate([jnp.where(first, a, 0.0), jnp.where(first, 0.0, a)], axis=0).astype(BF16)
    rq = i * QB + _iota((QB, 1), 0)
    return a2, jnp.concatenate([rq, rq], axis=0), first


def _hi_lo(a):
    hi = a.astype(BF16)
    lo = (a - hi.astype(F32)).astype(BF16)
    return jnp.concatenate([hi, lo], axis=1)


def _cargo_bounds(refs, n, n_out, exchange, first, last):
    outs = refs[n:n + n_out]
    if not n:
        return outs, lambda: None
    ex = exchange[0](refs[:n], refs[n + n_out:2 * n + n_out], *refs[2 * n + n_out:])

    @pl.when(first)
    def _():
        ex.start()

    def finish():
        @pl.when(last)
        def _():
            ex.wait()

    return outs, finish


def _sb_fwd(p, pad, name, cargo=(), exchange=None):
    t_pad = p.shape[0]
    nq = t_pad // QB
    n = len(cargo)

    def body(q_ref, k_ref, v_ref, *rest):
        i = pl.program_id(1)
        pr = pl.program_id(0)
        (o_ref, r_ref), end_cargo = _cargo_bounds(rest, n, 2, exchange, (pr == 0) & (i == 0),
                                                  (pr == SB_H // 2 - 1) & (i == nq - 1))
        q2, rowq, first = _sb_stack(q_ref[...] * SB_SCALE, i)
        tri = (_iota((QB, QB), 0) > _iota((QB, QB), 1)).astype(BF16)
        upper2 = jnp.concatenate([jnp.concatenate([tri, tri], axis=0), jnp.ones((2 * QB, QB), BF16)], axis=1)

        def chain(kb, masked):
            start = pl.multiple_of(kb * QB, QB)
            kblk = k_ref[pl.ds(start, QB), :].astype(BF16)
            vblk = v_ref[pl.ds(start, QB), :].astype(BF16)
            z = lax.dot_general(q2, kblk, (NT, ((), ())), preferred_element_type=F32)
            colk = kb * QB + _iota((1, QB), 1)
            al = ((colk < rowq) & (colk >= pad)) if masked else None
            l1m, ls, _ = _sb_terms(z, al)
            sums = lax.dot_general(_hi_lo(l1m), upper2, (NN, ((), ())), preferred_element_type=F32)
            return al, ls, sums[:, :QB], sums[:, QB:], vblk

        def step(done, carry, masked, nb):
            o_acc, run = carry
            ws, vs = [], []
            for n in range(nb):
                al, ls, suf, rs, vblk = chain(i - done - n, n in masked)
                wgt = jnp.exp(ls + suf + run)
                ws.append((wgt if al is None else jnp.where(al, wgt, 0.0)).astype(BF16))
                vs.append(vblk)
                run = run + rs
            o_acc = o_acc + lax.dot_general(jnp.concatenate(ws, axis=1), jnp.concatenate(vs, axis=0),
                                            (NN, ((), ())), preferred_element_type=F32)
            return o_acc, run

        o_acc, run = _sb_passes(i, step, (jnp.zeros((2 * QB, HD), F32), jnp.zeros((2 * QB, QB), F32)), SB_NB_FWD)
        o_ref[...] = jnp.where(first, o_acc[:QB], o_acc[QB:]).astype(BF16)
        r_ref[...] = jnp.where(first, run[:QB], run[QB:])
        end_cargo()

    full = lambda off: pl.BlockSpec((t_pad, HD), lambda pr, i: (0, off + pr))
    blk = pl.BlockSpec((QB, HD), lambda pr, i: (i, pr))
    return pl.pallas_call(
        body, name=name, grid=(SB_H // 2, nq),
        in_specs=[pl.BlockSpec((QB, HD), lambda pr, i: (i, SB_Q0 + pr)), full(SB_K0), full(SB_V0)] + [ANY] * n,
        out_specs=[blk, blk] + [ANY] * n,
        out_shape=[jax.ShapeDtypeStruct((t_pad, SB_H * SB_DH), BF16), jax.ShapeDtypeStruct((t_pad, SB_H * SB_DH), F32)]
        + (exchange[1](cargo) if n else []),
        scratch_shapes=exchange[2](n) if n else [],
        compiler_params=_cparams(),
    )(p, p, p, *cargo)


def _sb_bwd(p, rtot, dy, dy_blk0, pad, name, cargo=(), exchange=None):
    t_pad = p.shape[0]
    nq = t_pad // QB
    n = len(cargo)

    def body(q_ref, k_ref, v_ref, r_ref, do_ref, *rest):
        i = pl.program_id(1)
        pr = pl.program_id(0)
        dkt_ref, dvt_ref = rest[-2:]
        (dq_ref, dk_ref, dv_ref), end_cargo = _cargo_bounds(rest[:-2], n, 3, exchange, (pr == 0) & (i == 0),
                                                            (pr == SB_H // 2 - 1) & (i == nq - 1))

        @pl.when(i == 0)
        def _():
            dkt_ref[...] = jnp.zeros_like(dkt_ref)
            dvt_ref[...] = jnp.zeros_like(dvt_ref)

        q2, rowq, first = _sb_stack(q_ref[...] * SB_SCALE, i)
        do2, _, _ = _sb_stack(do_ref[...], i)
        q2t = jnp.transpose(q2.astype(F32)).astype(BF16)
        do2t = jnp.transpose(do2.astype(F32)).astype(BF16)
        rt = r_ref[...]
        lane = _iota((1, HD), 1)
        rcol = jnp.concatenate([jnp.sum(jnp.where(lane == 0, rt, 0.0), axis=1, keepdims=True),
                                jnp.sum(jnp.where(lane == SB_DH, rt, 0.0), axis=1, keepdims=True)], axis=0)
        rj = _iota((QB, QB), 0)
        cs = _iota((QB, QB), 1)
        tri_u = (rj > cs).astype(BF16)
        tri_l = (rj < cs).astype(BF16)
        ones2 = jnp.ones((2 * QB, QB), BF16)
        upper2 = jnp.concatenate([jnp.concatenate([tri_u, tri_u], axis=0), ones2], axis=1)
        lower2 = jnp.concatenate([jnp.concatenate([tri_l, tri_l], axis=0), ones2], axis=1)
        rcol = jnp.broadcast_to(rcol, (2 * QB, QB))

        def chain(kb, masked):
            start = pl.multiple_of(kb * QB, QB)
            kblk = k_ref[pl.ds(start, QB), :].astype(BF16)
            vblk = v_ref[pl.ds(start, QB), :].astype(BF16)
            z = lax.dot_general(q2, kblk, (NT, ((), ())), preferred_element_type=F32)
            colk = kb * QB + _iota((1, QB), 1)
            al = ((colk < rowq) & (colk >= pad)) if masked else None
            l1m, ls, sg = _sb_terms(z, al)
            dwgt = lax.dot_general(do2, vblk, (NT, ((), ())), preferred_element_type=F32)
            sums = lax.dot_general(_hi_lo(l1m), upper2, (NN, ((), ())), preferred_element_type=F32)
            return kb, kblk, al, ls, sums[:, :QB], sums[:, QB:], dwgt, sg

        def finish(c, left, gseen):
            kb, kblk, al, ls, suf, rs, dwgt, sg = c
            left = left - rs
            wgt = jnp.exp(ls + suf + left)
            if al is not None:
                wgt = jnp.where(al, wgt, 0.0)
            dl = dwgt * wgt
            sums = lax.dot_general(_hi_lo(dl), lower2, (NN, ((), ())), preferred_element_type=F32)
            gpre = gseen + sums[:, :QB]
            dz = dl - sg * (dl + gpre)
            if al is not None:
                dz = jnp.where(al, dz, 0.0)
            dz = dz.astype(BF16)
            dkt_ref[kb] += lax.dot_general(q2t, dz, (NN, ((), ())), preferred_element_type=F32)
            dvt_ref[kb] += lax.dot_general(do2t, wgt.astype(BF16), (NN, ((), ())), preferred_element_type=F32)
            return dz, left, gseen + sums[:, QB:]

        def step(done, carry, masked, nb):
            dq_acc, left, gseen = carry
            cs_ = [chain(done + n, n in masked) for n in range(nb)]
            dzs = []
            for c in cs_:
                dz, left, gseen = finish(c, left, gseen)
                dzs.append(dz)
            dq_acc = dq_acc + lax.dot_general(jnp.concatenate(dzs, axis=1), jnp.concatenate([c[1] for c in cs_], axis=0),
                                              (NN, ((), ())), preferred_element_type=F32)
            return dq_acc, left, gseen

        dq_acc, _, _ = _sb_passes(i, step, (jnp.zeros((2 * QB, HD), F32), rcol, jnp.zeros((2 * QB, QB), F32)),
                                  SB_NB_BWD)
        dq_ref[...] = jnp.where(first, dq_acc[:QB], dq_acc[QB:]) * SB_SCALE

        @pl.when(i == nq - 1)
        def _():
            for kb in range(nq):
                dk_ref[kb * QB:(kb + 1) * QB, :] = jnp.transpose(dkt_ref[kb])
                dv_ref[kb * QB:(kb + 1) * QB, :] = jnp.transpose(dvt_ref[kb])

        end_cargo()

    full_in = lambda off: pl.BlockSpec((t_pad, HD), lambda pr, i: (0, off + pr))
    full_out = pl.BlockSpec((t_pad, HD), lambda pr, i: (0, pr))
    blk = pl.BlockSpec((QB, HD), lambda pr, i: (i, pr))
    sds = jax.ShapeDtypeStruct((t_pad, SB_H * SB_DH), F32)
    return pl.pallas_call(
        body, name=name, grid=(SB_H // 2, nq),
        in_specs=[pl.BlockSpec((QB, HD), lambda pr, i: (i, SB_Q0 + pr)), full_in(SB_K0), full_in(SB_V0), blk,
                  pl.BlockSpec((QB, HD), lambda pr, i: (i, dy_blk0 + pr))] + [ANY] * n,
        out_specs=[blk, full_out, full_out] + [ANY] * n,
        out_shape=[sds, sds, sds] + (exchange[1](cargo) if n else []),
        scratch_shapes=(exchange[2](n) if n else []) + [pltpu.VMEM((nq, HD, QB), F32)] * 2,
        compiler_params=_cparams(),
    )(p, p, p, rtot, dy, *cargo)


HG_LEVELS = 6


def _hg_prefix_matrix():
    t = np.arange(CH)[:, None]
    j = np.arange(CH)[None, :]
    groups = [(j <= t)]
    for lvl in range(1, HG_LEVELS + 1):
        half = CH >> lvl
        e = (t // (2 * half)) * (2 * half) + half - 1
        groups.append(j <= e)
    groups.append(np.ones((8, CH), bool))
    e = np.concatenate(groups, axis=0).astype(np.float32)
    return np.concatenate([e, e, e], axis=1), np.concatenate([e, e, e], axis=0)


HG_G = 8


def _hg_chunk(qr, fr, iv, r0, r1, st, valid, ecat):
    g = st.shape[0]
    mx = jnp.maximum(r0, r1)
    e0 = jnp.exp(r0 - mx)
    e1 = jnp.exp(r1 - mx)
    lb = e1 / (e0 + e1)
    fg = lb + (1.0 - lb) * _sigmoid(fr)
    logf = jnp.where(valid, jnp.log(fg), 0.0)
    kk = jnp.where(valid, 1.0 - fg, 0.0)
    q = jnp.where(valid, _silu(qr), 0.0)
    v = _heads(jnp.where(valid, iv, 0.0), g)

    pre = _mask_dot(ecat, logf)
    b = pre[0:CH]
    b_last = jnp.max(pre[(HG_LEVELS + 1) * CH:], axis=0, keepdims=True)
    row = _iota((CH, 1), 0)
    ri = _iota((1, CH, CH), 1)
    ci = _iota((1, CH, CH), 2)
    a = jnp.where(ri == ci, jnp.sum(_heads(q * kk, g), axis=2, keepdims=True), 0.0)
    for lvl in range(1, HG_LEVELS + 1):
        half = CH >> lvl
        m = pre[lvl * CH:(lvl + 1) * CH]
        low = (row & half) != 0
        dec = jnp.exp(jnp.where(low, b - m, m - b))
        qt = jnp.where(low, q * dec, 0.0)
        kt = jnp.where(low, 0.0, kk * dec)
        same = (ri >> (7 - lvl)) == (ci >> (7 - lvl))
        a = a + jnp.where(same, bbnt(_heads(qt, g), _heads(kt, g)), 0.0)
    o = bbnt(_heads(q * jnp.exp(b), g), st) + bbnn(a, v)
    kd = kk * jnp.exp(b_last - b)
    st_new = st * _heads(jnp.exp(b_last), g) + bbtn(v, _heads(kd, g))
    return o, st_new


def _hg_specs(npair, rev):
    cc = (lambda c: npair - 1 - c) if rev else (lambda c: c)
    ng = HG_H // HG_G
    blk = lambda off: pl.BlockSpec((PAIR, HG_G * HD), lambda h, c: (cc(c), off * ng + h))
    lbs = pl.BlockSpec((2, HG_G * HD), lambda h, c: (0, h))
    state = pl.BlockSpec((1, HG_G, HD, HD), lambda h, c: (cc(c), h, 0, 0))
    return ng, blk, lbs, state


def _hg_fwd(p, lbraw, ecat, pad, name):
    t_pad = p.shape[0]
    npair = t_pad // PAIR
    ng, blk, lbs, state = _hg_specs(npair, False)

    def body(q_ref, f_ref, i_ref, lb_ref, e_ref, et_ref, o_ref, ss_ref, s_ref):
        c = pl.program_id(1)

        @pl.when(c == 0)
        def _():
            s_ref[...] = jnp.zeros_like(s_ref)

        st = s_ref[...]
        ss_ref[0] = st
        for half in (0, 1):
            r = _chunk_rows(half)
            o, st = _hg_chunk(q_ref[r, :], f_ref[r, :], i_ref[r, :], lb_ref[0:1, :], lb_ref[1:2, :], st,
                              _chunk_valid(c, half, pad), (e_ref[...], et_ref[...]))
            _store_heads(o_ref, o, r)
        s_ref[...] = st

    return pl.pallas_call(
        body, name=name, grid=(ng, npair),
        in_specs=[blk(0), blk(1), blk(2), lbs] + [pl.BlockSpec(e.shape, lambda h, c: (0, 0)) for e in ecat],
        out_specs=[blk(0), state],
        out_shape=[jax.ShapeDtypeStruct((t_pad, HG_H * HD), F32), jax.ShapeDtypeStruct((npair, HG_H, HD, HD), F32)],
        scratch_shapes=[pltpu.VMEM((HG_G, HD, HD), F32)],
        compiler_params=_cparams(),
    )(p, p, p, lbraw, *ecat)


def _hg_bwd(p, lbraw, ecat, ssave, do, pad, name, cargo=(), exchange=None):
    t_pad = p.shape[0]
    npair = t_pad // PAIR
    ng, blk, lbs, state = _hg_specs(npair, True)
    n = len(cargo)

    def body(q_ref, f_ref, i_ref, lb_ref, e_ref, et_ref, ss_ref, do_ref, *rest):
        c = pl.program_id(1)
        hg = pl.program_id(0)
        ds_ref = rest[-1]
        (dq_ref, df_ref, di_ref, dlb_ref), end_cargo = _cargo_bounds(
            rest[:-1], n, 4, exchange, (hg == 0) & (c == 0), (hg == ng - 1) & (c == npair - 1))

        @pl.when(c == 0)
        def _():
            ds_ref[...] = jnp.zeros_like(ds_ref)
            dlb_ref[...] = jnp.zeros_like(dlb_ref)

        ra, rb = _chunk_rows(0), _chunk_rows(1)
        va, vb = _chunk_valid(npair - 1 - c, 0, pad), _chunk_valid(npair - 1 - c, 1, pad)
        ecv = (e_ref[...], et_ref[...])

        def pair(qa, fa, ia, qb, fb, ib, r0, r1, st):
            oa, st = _hg_chunk(qa, fa, ia, r0, r1, st, va, ecv)
            ob, st = _hg_chunk(qb, fb, ib, r0, r1, st, vb, ecv)
            return oa, ob, st

        ins = [ref[r, :] for r in (ra, rb) for ref in (q_ref, f_ref, i_ref)]
        _, vjp = jax.vjp(pair, *ins, lb_ref[0:1, :], lb_ref[1:2, :], ss_ref[0])
        g = vjp((_heads(do_ref[ra, :], HG_G), _heads(do_ref[rb, :], HG_G), ds_ref[...]))
        for r, (dq, df, di) in ((ra, g[0:3]), (rb, g[3:6])):
            dq_ref[r, :] = dq.astype(BF16)
            df_ref[r, :] = df.astype(BF16)
            di_ref[r, :] = di.astype(BF16)
        dlb_ref[0:1, :] += g[6]
        dlb_ref[1:2, :] += g[7]
        ds_ref[...] = g[8]
        end_cargo()

    sds = jax.ShapeDtypeStruct((t_pad, HG_H * HD), BF16)
    return pl.pallas_call(
        body, name=name, grid=(ng, npair),
        in_specs=[blk(0), blk(1), blk(2), lbs] + [pl.BlockSpec(e.shape, lambda h, c: (0, 0)) for e in ecat]
        + [state, blk(0)] + [ANY] * n,
        out_specs=[blk(0), blk(0), blk(0), lbs] + [ANY] * n,
        out_shape=[sds, sds, sds, jax.ShapeDtypeStruct((2, HG_H * HD), F32)] + (exchange[1](cargo) if n else []),
        scratch_shapes=(exchange[2](n) if n else []) + [pltpu.VMEM((HG_G, HD, HD), F32)],
        compiler_params=_cparams(),
    )(p, p, p, lbraw, *ecat, ssave, do, *cargo)


def _pad_ab_cols(w):
    z = jnp.zeros((w.shape[0], HD - GDN_H), w.dtype)
    return jnp.concatenate([w[:, :2048], w[:, 2048:2052], z, w[:, 2052:2056], z, w[:, 2056:]], axis=1)


def _unpad_ab_cols(w):
    return jnp.concatenate([w[:, :2048], w[:, 2048:2052], w[:, 2176:2180], w[:, 2304:]], axis=1)


def _lane_pad(v):
    return jnp.pad(v, ((0, 0), (0, HD - v.shape[1])))


def _mlp_fwd(h, hb, w1, w2, layer, g, b):
    a, r = _mm(hb, w1, b_view=("cols", layer), out_dtype=BF16, act=True, name=f"mlp_up_{layer}")
    m, y, yb = _mm(r, w2, b_view=("rows", layer), ln=(h, g, b), name=f"mlp_down_{layer}")
    return a, r, m, y, yb


def _mlp_bwd(hb, a, r, dmb, w1, w2, layer):
    da = _mm(dmb, w2, tb=True, b_view=("rows", layer), out_dtype=BF16, gate=a, name=f"mlp_down_dx_{layer}")
    dw2 = _mm(r, dmb, ta=True, out_dtype=BF16, name=f"mlp_down_dw_{layer}")
    dh = _mm(da, w1, tb=True, b_view=("cols", layer), name=f"mlp_up_dx_{layer}")
    dw1 = _mm(hb, da, ta=True, out_dtype=BF16, out_split=N_CHIP, name=f"mlp_up_dw_{layer}")
    return dh, dw1, dw2


def _local_step(h0, tgt, w, pad, late=None):
    row = lambda a, i: a[i:i + 1]
    ecat = tuple(jnp.asarray(e, dtype=BF16) for e in _hg_prefix_matrix())
    cw = [w["conv_w"][:, i * 512:(i + 1) * 512] for i in range(3)]
    alog, dtb = _lane_pad(w["a_log"]), _lane_pad(w["dt_bias"])

    h0b = h0.astype(BF16)
    p0 = _mm(h0b, w["ab_w_in"], name="ab_in")
    qn = _conv_fwd(p0, 0, cw[0], "q", pad, "conv_q")
    kn = _conv_fwd(p0, 4, cw[1], "k", pad, "conv_k")
    vn = _conv_fwd(p0, 8, cw[2], "v", pad, "conv_v")
    if late is None:
        oa_raw, ss0 = _gdn_fwd(qn, kn, vn, p0, alog, dtb, pad, "gdn_fwd")
        ob, rtot = _sb_fwd(p0, pad, "sb_fwd")
    else:
        oa_raw, ss0, g_cin, g_cout = _gdn_fwd(qn, kn, vn, p0, alog, dtb, pad, "gdn_fwd",
                                              cargo=[late["c_w_in"], late["c_w_out"]], exchange=GATHER)
        ob, rtot, g_about, g_w1, g_w2 = _sb_fwd(p0, pad, "sb_fwd", exchange=GATHER,
                                                cargo=[late["ab_w_out"], late["mlp_w1"], late["mlp_w2"]])
        w = dict(w, ab_w_out=g_about.reshape(D, D), c_w_in=g_cin, c_w_out=g_cout.reshape(D, D), mlp_w1=g_w1, mlp_w2=g_w2)
    oa = _grms_fwd(oa_raw, p0, 12, w["ab_gnorm_g"], "gdn_gate")
    ycat = jnp.concatenate([oa, ob], axis=1)
    mix0, h1, h1b = _mm(ycat, w["ab_w_out"], name="ab_out", ln=(h0, row(w["ln_mix_g"], 0), row(w["ln_mix_b"], 0)))
    a0, r0, m0, h2, h2b = _mlp_fwd(h1, h1b, w["mlp_w1"], w["mlp_w2"], 0, row(w["ln_ffn_g"], 0), row(w["ln_ffn_b"], 0))
    p1 = _mm(h2b, w["c_w_in"], b_view=("cols", 0), name="c_in")
    oc_raw, ss1 = _hg_fwd(p1, w["c_lb_raw"], ecat, pad, "hg_fwd")
    yc = _grms_fwd(oc_raw, p1, 3 * HG_H, w["c_gnorm_g"], "hg_gate")
    mix1, h3, h3b = _mm(yc, w["c_w_out"], name="c_out", ln=(h2, row(w["ln_mix_g"], 1), row(w["ln_mix_b"], 1)))
    a1, r1, m1, h4, _ = _mlp_fwd(h3, h3b, w["mlp_w1"], w["mlp_w2"], 1, row(w["ln_ffn_g"], 1), row(w["ln_ffn_b"], 1))
    loss, dh4 = _loss_fwd(h4, tgt, pad + N_META, "loss")

    dh3a, dm1b, dfg1, dfb1 = _ln_res_bwd(h3, m1, row(w["ln_ffn_g"], 1), row(w["ln_ffn_b"], 1), [dh4], "ln_ffn_bwd_1")
    dh3b, dw1_1, dw2_1 = _mlp_bwd(h3b, a1, r1, dm1b, w["mlp_w1"], w["mlp_w2"], 1)
    dh2a, dmix1b, dmg1, dmb1 = _ln_res_bwd(h2, mix1, row(w["ln_mix_g"], 1), row(w["ln_mix_b"], 1), [dh3a, dh3b], "ln_mix_bwd_1")
    dyc = _mm(dmix1b, w["c_w_out"], tb=True, name="c_out_dx")
    dwco = _mm(yc, dmix1b, ta=True, out_dtype=BF16, name="c_out_dw")
    doc, dzc, dcg = _grms_bwd(oc_raw, p1, 3 * HG_H, w["c_gnorm_g"], dyc, 0, "hg_gate_bwd")
    landed = {}
    rows4 = lambda a: a.reshape(N_CHIP, -1, D)
    if late is None:
        dq1, df1, di1, dlb = _hg_bwd(p1, w["c_lb_raw"], ecat, ss1, doc, pad, "hg_bwd")
    else:
        dq1, df1, di1, dlb, landed["w1_1"] = _hg_bwd(
            p1, w["c_lb_raw"], ecat, ss1, doc, pad, "hg_bwd", cargo=[dw1_1], exchange=SCATTER)
    dp1 = [dq1, df1, di1, dzc]
    dh2b = _mm_groups_nt(dp1, w["c_w_in"], "c_in_dx")
    dwc = jnp.stack([_mm(h2b, d, ta=True, out_dtype=BF16, name=f"c_in_dw_{i}") for i, d in enumerate(dp1)])
    dh1a, dm0b, dfg0, dfb0 = _ln_res_bwd(h1, m0, row(w["ln_ffn_g"], 0), row(w["ln_ffn_b"], 0), [dh2a, dh2b], "ln_ffn_bwd_0")
    dh1b, dw1_0, dw2_0 = _mlp_bwd(h1b, a0, r0, dm0b, w["mlp_w1"], w["mlp_w2"], 0)
    dh0a, dmix0b, dmg0, dmb0 = _ln_res_bwd(h0, mix0, row(w["ln_mix_g"], 0), row(w["ln_mix_b"], 0), [dh1a, dh1b], "ln_mix_bwd_0")
    dycat = _mm(dmix0b, w["ab_w_out"], tb=True, name="ab_out_dx")
    dwabo = _mm(ycat, dmix0b, ta=True, out_dtype=BF16, name="ab_out_dw")
    doa, dza, dag = _grms_bwd(oa_raw, p0, 12, w["ab_gnorm_g"], dycat, 0, "gdn_gate_bwd")
    if late is None:
        dqn, dkn, dvn, dbb, daa, dal, ddt = _gdn_bwd(qn, kn, vn, p0, alog, dtb, ss0, doa, pad, "gdn_bwd")
        dqb, dkb, dvb = _sb_bwd(p0, rtot, dycat, 4, pad, "sb_bwd")
    else:
        dqn, dkn, dvn, dbb, daa, dal, ddt, landed["c_w_in"] = _gdn_bwd(
            qn, kn, vn, p0, alog, dtb, ss0, doa, pad, "gdn_bwd", cargo=[dwc], exchange=SCATTER)
        (dqb, dkb, dvb, landed["w1_0"], landed["w2_0"], landed["w2_1"], landed["ab_w_out"],
         landed["c_w_out"]) = _sb_bwd(
            p0, rtot, dycat, 4, pad, "sb_bwd",
            cargo=[dw1_0, rows4(dw2_0), rows4(dw2_1), rows4(dwabo), rows4(dwco)], exchange=SCATTER)
    dpq, dcq = _conv_bwd(p0, 0, cw[0], dqn, "q", pad, "conv_q_bwd")
    dpk, dck = _conv_bwd(p0, 4, cw[1], dkn, "k", pad, "conv_k_bwd")
    dpv, dcv = _conv_bwd(p0, 8, cw[2], dvn, "v", pad, "conv_v_bwd")
    dp0 = _assemble_bf16([dpq, dpk, dpv, dza, dbb, daa, dqb, dkb, dvb], "ab_in_dy")
    dwab = _mm(h0b, dp0, ta=True, out_dtype=BF16, name="ab_in_dw")
    if late is None:
        dh0 = _mm(dp0, w["ab_w_in"], tb=True, plus=dh0a, name="ab_in_dx")
    else:
        dab = jnp.transpose(_unpad_ab_cols(dwab).reshape(D, N_CHIP, AB_TRUE // N_CHIP), (1, 0, 2))
        dh0, landed["ab_w_in"] = _mm(dp0, w["ab_w_in"], tb=True, plus=dh0a, name="ab_in_dx", cargo=[dab],
                                     exchange=SCATTER)

    grads = {
        "ab_w_in": dwab, "conv_w": jnp.concatenate([dcq, dck, dcv], axis=1),
        "a_log": dal[:, :GDN_H], "dt_bias": ddt[:, :GDN_H],
        "ab_gnorm_g": dag, "ab_w_out": dwabo, "c_w_in": dwc, "c_lb_raw": dlb, "c_gnorm_g": dcg, "c_w_out": dwco,
        "ln_mix_g": jnp.concatenate([dmg0, dmg1], 0), "ln_mix_b": jnp.concatenate([dmb0, dmb1], 0),
        "w1_0": dw1_0, "w1_1": dw1_1, "w2_0": dw2_0, "w2_1": dw2_1,
        "ln_ffn_g": jnp.concatenate([dfg0, dfg1], 0), "ln_ffn_b": jnp.concatenate([dfb0, dfb1], 0),
        "landed": landed,
    }
    return loss, dh0, grads


MESH = pl.DeviceIdType.MESH
ANY = pl.BlockSpec(memory_space=pl.ANY)
N_CHIP = 4
N_DEV = 8
CHIP_REL = ((1, 0), (0, 1), (1, 1))
DEV_REL = tuple((dx, dy, dc) for dx in (0, 1) for dy in (0, 1) for dc in (0, 1))[1:]

def _pos():
    return lax.axis_index("x"), lax.axis_index("y"), lax.axis_index("c")


def _flip(a, d):
    return a + d - 2 * a * d


class _Exchange:
    def __init__(self, local, sends, recvs):
        self.local, self.sends, self.recvs = local, sends, recvs

    def start(self):
        for cp in self.local + self.sends:
            cp.start()

    def wait(self):
        for cp in self.recvs:
            cp.wait_recv()
        for cp in self.sends:
            cp.wait_send()
        for cp in self.local:
            cp.wait()


def _gather_sems(n):
    return [pltpu.SemaphoreType.DMA((3 * n,)), pltpu.SemaphoreType.DMA((3 * n,)), pltpu.SemaphoreType.DMA((n,))]


def _gather_copies(x_refs, o_refs, send_sems, recv_sems, local_sems):
    n = len(x_refs)
    x, y, c = _pos()
    local = [pltpu.make_async_copy(x_refs[a], o_refs[a].at[2 * x + y], local_sems.at[a]) for a in range(n)]

    def copy(a, k, sending):
        tx, ty = _flip(x, CHIP_REL[k][0]), _flip(y, CHIP_REL[k][1])
        return pltpu.make_async_remote_copy(
            src_ref=x_refs[a], dst_ref=o_refs[a].at[2 * x + y if sending else 2 * tx + ty],
            send_sem=send_sems.at[3 * a + k], recv_sem=recv_sems.at[3 * a + k], device_id=(tx, ty, c), device_id_type=MESH)

    pairs = [(a, k) for a in range(n) for k in range(3)]
    return _Exchange(local, [copy(a, k, True) for a, k in pairs], [copy(a, k, False) for a, k in pairs])


def _gather_shapes(bufs):
    return [jax.ShapeDtypeStruct((N_CHIP,) + b.shape, b.dtype) for b in bufs]


def _chip_allgather(bufs, name):
    n = len(bufs)

    def body(*refs):
        ex = _gather_copies(refs[:n], refs[n:2 * n], *refs[2 * n:])
        ex.start()
        ex.wait()

    return pl.pallas_call(
        body, name=name, in_specs=[ANY] * n, out_specs=[ANY] * n, out_shape=_gather_shapes(bufs),
        scratch_shapes=_gather_sems(n), compiler_params=pltpu.CompilerParams(has_side_effects=True),
    )(*bufs)


def _scatter_sems(n):
    nr = N_DEV - 1
    return [pltpu.SemaphoreType.DMA((nr * n,)), pltpu.SemaphoreType.DMA((nr * n,)), pltpu.SemaphoreType.DMA((n,))]


def _scatter_copies(g_refs, o_refs, send_sems, recv_sems, local_sems):
    n = len(g_refs)
    nr = N_DEV - 1
    x, y, c = _pos()
    me = 4 * x + 2 * y + c
    local = [pltpu.make_async_copy(g_refs[a].at[2 * x + y], o_refs[a].at[me], local_sems.at[a]) for a in range(n)]

    def copy(a, k, sending):
        dx, dy, dc = DEV_REL[k]
        tx, ty, tc = _flip(x, dx), _flip(y, dy), _flip(c, dc)
        return pltpu.make_async_remote_copy(
            src_ref=g_refs[a].at[2 * tx + ty], dst_ref=o_refs[a].at[me if sending else 4 * tx + 2 * ty + tc],
            send_sem=send_sems.at[nr * a + k], recv_sem=recv_sems.at[nr * a + k],
            device_id=(tx, ty, tc), device_id_type=MESH)

    pairs = [(a, k) for a in range(n) for k in range(nr)]
    return _Exchange(local, [copy(a, k, True) for a, k in pairs], [copy(a, k, False) for a, k in pairs])


def _scatter_shapes(gs):
    return [jax.ShapeDtypeStruct((N_DEV,) + g.shape[1:], g.dtype) for g in gs]


GATHER = (_gather_copies, _gather_shapes, _gather_sems)
SCATTER = (_scatter_copies, _scatter_shapes, _scatter_sems)


def _sum_slots(rs, name):
    n, rh, w = rs[0].shape
    tr = _pick(rh, (256, 128, 64, 16))

    def body(*refs):
        o_ref = refs[-1]
        for layer, r_ref in enumerate(refs[:-1]):
            acc = r_ref[0].astype(F32)
            for s in range(1, n):
                acc = acc + r_ref[s].astype(F32)
            o_ref[layer] = acc

    return pl.pallas_call(
        body, name=name, grid=(rh // tr,), in_specs=[pl.BlockSpec((n, tr, w), lambda i: (0, i, 0))] * len(rs),
        out_specs=pl.BlockSpec((len(rs), tr, w), lambda i: (0, i, 0)),
        out_shape=jax.ShapeDtypeStruct((len(rs), rh, w), F32), compiler_params=_cparams(),
    )(*rs)


def _small_allreduce(buf, name):
    r, w = buf.shape

    def body(b_ref, o_ref, land_ref, send_sems, recv_sems):
        x, y, c = _pos()
        me = 4 * x + 2 * y + c
        land_ref[me] = b_ref[...]

        def target(k):
            dx, dy, dc = DEV_REL[k]
            return _flip(x, dx), _flip(y, dy), _flip(c, dc)

        sends = []
        for k in range(N_DEV - 1):
            tx, ty, tc = target(k)
            cp = pltpu.make_async_remote_copy(
                src_ref=b_ref, dst_ref=land_ref.at[me], send_sem=send_sems.at[k], recv_sem=recv_sems.at[k],
                device_id=(tx, ty, tc), device_id_type=MESH)
            cp.start()
            sends.append(cp)
        for k in range(N_DEV - 1):
            tx, ty, tc = target(k)
            pltpu.make_async_remote_copy(
                src_ref=b_ref, dst_ref=land_ref.at[4 * tx + 2 * ty + tc], send_sem=send_sems.at[k],
                recv_sem=recv_sems.at[k], device_id=(tx, ty, tc), device_id_type=MESH).wait_recv()
        for cp in sends:
            cp.wait_send()
        acc = land_ref[0]
        for s in range(1, N_DEV):
            acc = acc + land_ref[s]
        o_ref[...] = acc

    vm = pl.BlockSpec(memory_space=pltpu.VMEM)
    return pl.pallas_call(
        body, name=name, in_specs=[vm], out_specs=vm, out_shape=jax.ShapeDtypeStruct((r, w), F32),
        scratch_shapes=[pltpu.VMEM((N_DEV, r, w), F32), pltpu.SemaphoreType.DMA((N_DEV - 1,)),
                        pltpu.SemaphoreType.DMA((N_DEV - 1,))],
        compiler_params=pltpu.CompilerParams(has_side_effects=True),
    )(buf)


def _adamw(w, g, m, v, name):
    r, c = w.shape
    tr = _pick(r, (256, 128, 64, 8)) if r * c > (1 << 18) else r

    def body(w_ref, g_ref, m_ref, v_ref, d_ref, m2_ref, v2_ref):
        gg = g_ref[...]
        m2 = ADAM_B1 * m_ref[...] + (1.0 - ADAM_B1) * gg
        v2 = ADAM_B2 * v_ref[...] + (1.0 - ADAM_B2) * (gg * gg)
        m_hat = m2 / (1.0 - ADAM_B1 ** ADAM_STEP)
        v_hat = v2 / (1.0 - ADAM_B2 ** ADAM_STEP)
        d_ref[...] = -ADAM_LR * (m_hat / (jnp.sqrt(v_hat) + ADAM_EPS) + ADAM_WD * w_ref[...])
        m2_ref[...] = m2
        v2_ref[...] = v2

    blk = pl.BlockSpec((tr, c), lambda i: (i, 0))
    sds = jax.ShapeDtypeStruct((r, c), F32)
    return pl.pallas_call(body, name=name, grid=(r // tr,), in_specs=[blk] * 4, out_specs=[blk] * 3,
                          out_shape=[sds] * 3, compiler_params=_cparams())(w, g, m, v)


BIG = ("ab_w_in", "ab_w_out", "c_w_in", "c_w_out", "mlp_w1", "mlp_w2")
SMALL = ("ln_mix_g", "ln_mix_b", "ln_ffn_g", "ln_ffn_b", "c_lb_raw", "ab_a_log", "ab_dt_bias", "ab_gnorm_g", "c_gnorm_g")
SMALL_ROWS = 16
CONV_ROWS = 8
CONV_W = 3 * GDN_H * HD


def _conv_to_rows(cw):
    return jnp.pad(cw, ((0, 0), (0, 2 * D - CONV_W))).reshape(CONV_ROWS, D)


def _rows_to_conv(rows):
    return rows.reshape(CONV_K, 2 * D)[:, :CONV_W]


def _pack_small(d):
    rows = [jnp.pad(d[n], ((0, 0), (0, D - d[n].shape[1]))) for n in SMALL]
    buf = jnp.concatenate(rows, axis=0)
    return jnp.pad(buf, ((0, SMALL_ROWS - buf.shape[0]), (0, 0)))


def _unpack_small(buf, like):
    out, r = {}, 0
    for n in SMALL:
        nr, nc = like[n].shape
        out[n] = buf[r:r + nr, :nc]
        r += nr
    return out


def kernel(x, meta_tokens, ab_w_in, ab_conv_w, ab_a_log, ab_dt_bias, ab_gnorm_g, ab_w_out, c_w_in, c_lb_raw, c_gnorm_g, c_w_out, ln_mix_g, ln_mix_b, mlp_w1, mlp_w2, ln_ffn_g, ln_ffn_b, loss_target, m_meta_tokens, m_ab_w_in, m_ab_conv_w, m_ab_a_log, m_ab_dt_bias, m_ab_gnorm_g, m_ab_w_out, m_c_w_in, m_c_lb_raw, m_c_gnorm_g, m_c_w_out, m_ln_mix_g, m_ln_mix_b, m_mlp_w1, m_mlp_w2, m_ln_ffn_g, m_ln_ffn_b, v_meta_tokens, v_ab_w_in, v_ab_conv_w, v_ab_a_log, v_ab_dt_bias, v_ab_gnorm_g, v_ab_w_out, v_c_w_in, v_c_lb_raw, v_c_gnorm_g, v_c_w_out, v_ln_mix_g, v_ln_mix_b, v_mlp_w1, v_mlp_w2, v_ln_ffn_g, v_ln_ffn_b):
    names = ("meta_tokens", "ab_w_in", "ab_conv_w", "ab_a_log", "ab_dt_bias", "ab_gnorm_g", "ab_w_out", "c_w_in",
             "c_lb_raw", "c_gnorm_g", "c_w_out", "ln_mix_g", "ln_mix_b", "mlp_w1", "mlp_w2", "ln_ffn_g", "ln_ffn_b")
    wts = dict(zip(names, (meta_tokens, ab_w_in, ab_conv_w, ab_a_log, ab_dt_bias, ab_gnorm_g, ab_w_out, c_w_in, c_lb_raw,
                           c_gnorm_g, c_w_out, ln_mix_g, ln_mix_b, mlp_w1, mlp_w2, ln_ffn_g, ln_ffn_b)))
    mom_m = dict(zip(names, (m_meta_tokens, m_ab_w_in, m_ab_conv_w, m_ab_a_log, m_ab_dt_bias, m_ab_gnorm_g, m_ab_w_out,
                             m_c_w_in, m_c_lb_raw, m_c_gnorm_g, m_c_w_out, m_ln_mix_g, m_ln_mix_b, m_mlp_w1, m_mlp_w2,
                             m_ln_ffn_g, m_ln_ffn_b)))
    mom_v = dict(zip(names, (v_meta_tokens, v_ab_w_in, v_ab_conv_w, v_ab_a_log, v_ab_dt_bias, v_ab_gnorm_g, v_ab_w_out,
                             v_c_w_in, v_c_lb_raw, v_c_gnorm_g, v_c_w_out, v_ln_mix_g, v_ln_mix_b, v_mlp_w1, v_mlp_w2,
                             v_ln_ffn_g, v_ln_ffn_b)))
    seq = x.shape[1]
    pad = (-(N_META + seq)) % QB
    xi, yi, ci = _pos()
    chip = 2 * xi + yi

    gat_ab_in, = _chip_allgather([ab_w_in[0].astype(BF16)], "gather_weights")
    late = {"ab_w_out": ab_w_out[0].astype(BF16), "c_w_in": c_w_in.astype(BF16), "c_w_out": c_w_out[0].astype(BF16),
            "mlp_w1": mlp_w1.astype(BF16), "mlp_w2": mlp_w2.astype(BF16)}
    mcols, ccols = meta_tokens.shape[1], ab_conv_w.shape[2]
    place = jnp.concatenate([
        lax.dynamic_update_slice(jnp.zeros((N_META, D), F32), 0.5 * meta_tokens, (0, chip * mcols)),
        _conv_to_rows(lax.dynamic_update_slice(jnp.zeros((CONV_K, CONV_W), F32), 0.5 * ab_conv_w[0], (0, chip * ccols)))],
        axis=0)
    placed = _small_allreduce(place, "gather_meta")
    meta_full = placed[:N_META]

    w = {
        "ab_w_in": _pad_ab_cols(jnp.transpose(gat_ab_in, (1, 0, 2)).reshape(D, AB_TRUE)),
        "conv_w": _rows_to_conv(placed[N_META:]), "a_log": ab_a_log, "dt_bias": ab_dt_bias,
        "ab_gnorm_g": ab_gnorm_g, "c_lb_raw": c_lb_raw,
        "c_gnorm_g": c_gnorm_g, "ln_mix_g": ln_mix_g, "ln_mix_b": ln_mix_b, "ln_ffn_g": ln_ffn_g, "ln_ffn_b": ln_ffn_b,
    }

    h0 = jnp.concatenate([jnp.zeros((pad, D), F32), meta_full, x[0]], axis=0)
    tgt = jnp.concatenate([jnp.zeros((pad + N_META, D), F32), loss_target[0]], axis=0)
    loss8, dh0, g = _local_step(h0, tgt, w, pad, late)
    grad_x = dh0[pad + N_META:][None]

    gsmall = {"ln_mix_g": g["ln_mix_g"], "ln_mix_b": g["ln_mix_b"], "ln_ffn_g": g["ln_ffn_g"], "ln_ffn_b": g["ln_ffn_b"],
              "c_lb_raw": g["c_lb_raw"], "ab_a_log": g["a_log"], "ab_dt_bias": g["dt_bias"], "ab_gnorm_g": g["ab_gnorm_g"],
              "c_gnorm_g": g["c_gnorm_g"]}
    packed = _pack_small(gsmall).at[SMALL_ROWS - 1, :loss8.shape[1]].set(loss8[0])
    sbuf = jnp.concatenate([packed, dh0[pad:pad + N_META], _conv_to_rows(g["conv_w"])], axis=0)
    ssum = _small_allreduce(sbuf, "allreduce_small")
    loss = ssum[SMALL_ROWS - 1, 0]
    grads = _unpack_small(ssum[:SMALL_ROWS], wts)
    grads["meta_tokens"] = lax.dynamic_slice(ssum[SMALL_ROWS:SMALL_ROWS + N_META], (0, chip * mcols), (N_META, mcols))
    grads["ab_conv_w"] = lax.dynamic_slice(_rows_to_conv(ssum[SMALL_ROWS + N_META:]), (0, chip * ccols), (CONV_K, ccols))[None]

    landed = g["landed"]
    for n in ("ab_w_in", "ab_w_out", "c_w_in", "c_w_out"):
        grads[n] = _sum_slots([landed[n]], f"grad_sum_{n}")
    grads["mlp_w1"] = _sum_slots([landed["w1_0"], landed["w1_1"]], "grad_sum_mlp_w1")
    grads["mlp_w2"] = _sum_slots([landed["w2_0"], landed["w2_1"]], "grad_sum_mlp_w2")

    delta, new_m, new_v = {}, {}, {}
    for n in ("meta_tokens", "ab_conv_w") + BIG:
        shp = wts[n].shape
        to2 = lambda a: a.reshape(-1, shp[-1])
        d2, m2, v2 = _adamw(to2(wts[n]), to2(grads[n]), to2(mom_m[n]), to2(mom_v[n]), f"adamw_{n}")
        delta[n], new_m[n], new_v[n] = d2.reshape(shp), m2.reshape(shp), v2.reshape(shp)
    d2, m2, v2 = _adamw(_pack_small(wts), ssum[:SMALL_ROWS], _pack_small(mom_m), _pack_small(mom_v), "adamw_small")
    delta.update(_unpack_small(d2, wts))
    new_m.update(_unpack_small(m2, wts))
    new_v.update(_unpack_small(v2, wts))

    return (loss, grad_x, *[grads[n] for n in names], *[delta[n] for n in names], *[new_m[n] for n in names],
            *[new_v[n] for n in names])
```

```python
import functools

import numpy as np
import jax
import jax.numpy as jnp
from jax import lax
from jax.experimental import pallas as pl
from jax.experimental.pallas import tpu as pltpu

F32 = jnp.float32
BF16 = jnp.bfloat16

D = 1024
N_META = 16
DEPTH = 2
GDN_H = 4
SB_H = 8
SB_DH = 64
HG_H = 8
HD = 128
CH = 64
QB = 128
ALPHA = float((2 * DEPTH) ** 0.25)
LN_EPS = 1e-5
RMS_EPS = 1e-6
L2_EPS = 1e-6
NEG = -1e30

ADAM_LR = 0.001
ADAM_B1 = 0.9
ADAM_B2 = 0.999
ADAM_EPS = 1e-08
ADAM_WD = 0.01
ADAM_STEP = 10

AB_TRUE = 3592
V7X_VMEM_BYTES = 64 * 1024 * 1024
VMEM_LIMIT = V7X_VMEM_BYTES - 8 * 1024 * 1024

NN = ((1,), (0,))
NT = ((1,), (1,))
TN = ((0,), (0,))


def _cparams(**kw):
    return pltpu.CompilerParams(vmem_limit_bytes=VMEM_LIMIT, **kw)


def _dg(a, b, dims, mode):
    if mode == "h":
        return lax.dot_general(a, b, dims, precision=lax.Precision.HIGHEST, preferred_element_type=F32)
    if mode == "b":
        return lax.dot_general(a.astype(BF16), b.astype(BF16), dims, preferred_element_type=F32)
    ah, bh = a.astype(BF16), b.astype(BF16)
    al, bl = (a - ah.astype(F32)).astype(BF16), (b - bh.astype(F32)).astype(BF16)
    d = lambda x, y: lax.dot_general(x, y, dims, preferred_element_type=F32)
    return d(ah, bh) + (d(ah, bl) + d(al, bh))


def _make_dots(mode, batched=False):
    if batched:
        nn_d, nt_d, tn_d = (((2,), (1,)), ((0,), (0,))), (((2,), (2,)), ((0,), (0,))), (((1,), (1,)), ((0,), (0,)))
    else:
        nn_d, nt_d, tn_d = (NN, ((), ())), (NT, ((), ())), (TN, ((), ()))

    @jax.custom_vjp
    def nn(a, b):
        return _dg(a, b, nn_d, mode)

    @jax.custom_vjp
    def nt(a, b):
        return _dg(a, b, nt_d, mode)

    @jax.custom_vjp
    def tn(a, b):
        return _dg(a, b, tn_d, mode)

    nn.defvjp(lambda a, b: (nn(a, b), (a, b)), lambda r, g: (nt(g, r[1]), tn(r[0], g)))
    nt.defvjp(lambda a, b: (nt(a, b), (a, b)), lambda r, g: (nn(g, r[1]), tn(g, r[0])))
    tn.defvjp(lambda a, b: (tn(a, b), (a, b)), lambda r, g: (nt(r[1], g), nn(r[0], g)))
    return nn, nt, tn


hnn = _make_dots("h")[0]
bbnn, bbnt, bbtn = _make_dots("b", True)
mbnn, mbnt, mbtn = _make_dots("m", True)
hbnt = _make_dots("h", True)[1]


def _split3(x, axis):
    x1 = x.astype(BF16)
    r1 = x - x1.astype(F32)
    x2 = r1.astype(BF16)
    x3 = (r1 - x2.astype(F32)).astype(BF16)
    return jnp.concatenate([x1, x2, x3], axis=axis)


@jax.custom_vjp
def _mask_dot(e3, x):
    return lax.dot_general(e3[0], _split3(x, 0), (NN, ((), ())), preferred_element_type=F32)


def _mask_dot_bwd(e3, g):
    dx = lax.dot_general(e3[1], _split3(g, 0), (TN, ((), ())), preferred_element_type=F32)
    return (jnp.zeros_like(e3[0]), jnp.zeros_like(e3[1])), dx


_mask_dot.defvjp(lambda e3, x: (_mask_dot(e3, x), e3), _mask_dot_bwd)


def _heads(a, n):
    return jnp.concatenate([a[None, :, h * HD:(h + 1) * HD] for h in range(n)], axis=0)


def _sigmoid(x):
    return jax.nn.sigmoid(x)


def _silu(x):
    return x * jax.nn.sigmoid(x)


def _softplus(x):
    return jnp.maximum(x, 0.0) + jnp.log(1.0 + jnp.exp(-jnp.abs(x)))


def _iota(shape, dim):
    return lax.broadcasted_iota(jnp.int32, shape, dim)


def _pick(n, prefs):
    for p in prefs:
        if n % p == 0:
            return p
    return n


def _mm(a, b, *, ta=False, tb=False, out_dtype=F32, name, b_view=None, out_split=0, act=False, gate=None, plus=None,
        ln=None, cargo=(), exchange=None):
    if ta:
        k_dim, m_dim = a.shape
    else:
        m_dim, k_dim = a.shape
    if b_view is None:
        w_rows, w_cols = b.shape
    else:
        kind, layer = b_view
        nj, _, blk_r, blk_c = b.shape
        w_rows, w_cols = (blk_r, nj * blk_c) if kind == "cols" else (nj * blk_r, blk_c)
    n_dim = w_rows if tb else w_cols
    assert (w_cols if tb else w_rows) == k_dim
    tm = _pick(m_dim, (1024, 1056, 704, 640, 512, 384, 256, 128))
    tn = _pick(n_dim, (1024, 1056, 704, 640, 512, 384, 256, 128))
    tk = _pick(k_dim, (1024, 1056, 704, 512, 384, 256, 128))
    nk = k_dim // tk
    a_spec = pl.BlockSpec((tk, tm), lambda i, j, k: (k, i)) if ta else pl.BlockSpec((tm, tk), lambda i, j, k: (i, k))
    wb = (tn, tk) if tb else (tk, tn)
    w_idx = (lambda i, j, k: (j, k)) if tb else (lambda i, j, k: (k, j))
    if b_view is None:
        b_spec = pl.BlockSpec(wb, w_idx)
    elif kind == "cols":
        per = blk_c // wb[1]
        b_spec = pl.BlockSpec((None, None) + wb,
                              lambda i, j, k: (w_idx(i, j, k)[1] // per, layer, w_idx(i, j, k)[0], w_idx(i, j, k)[1] % per))
    else:
        per = blk_r // wb[0]
        b_spec = pl.BlockSpec((None, None) + wb,
                              lambda i, j, k: (w_idx(i, j, k)[0] // per, layer, w_idx(i, j, k)[0] % per, w_idx(i, j, k)[1]))
    if out_split:
        per_o = (n_dim // out_split) // tn
        out_spec = pl.BlockSpec((None, tm, tn), lambda i, j, k: (j // per_o, i, j % per_o))
        out_sds = jax.ShapeDtypeStruct((out_split, m_dim, n_dim // out_split), out_dtype)
    else:
        out_spec = pl.BlockSpec((tm, tn), lambda i, j, k: (i, j))
        out_sds = jax.ShapeDtypeStruct((m_dim, n_dim), out_dtype)
    dims = (((0 if ta else 1,), (1 if tb else 0,)), ((), ()))
    assert sum(e is not None for e in (gate, plus, ln)) <= 1
    extra = [e for e in (gate, plus) if e is not None] + list(ln or ())
    n_out = 2 if act else 3 if ln else 1
    assert ln is None or (tn == n_dim and not out_split)

    def finish(acc, refs):
        if ln:
            y = _ln_res_fn(refs[0][...], acc, refs[1][...], refs[2][...])
            refs[3][...] = acc
            refs[4][...] = y
            refs[5][...] = y.astype(BF16)
        elif act:
            refs[0][...] = acc.astype(refs[0].dtype)
            r = jnp.maximum(acc, 0.0)
            refs[1][...] = (r * r).astype(refs[1].dtype)
        elif gate is not None:
            refs[1][...] = (acc * (2.0 * jnp.maximum(refs[0][...].astype(F32), 0.0))).astype(refs[1].dtype)
        elif plus is not None:
            refs[1][...] = (refs[0][...] + acc).astype(refs[1].dtype)
        else:
            refs[0][...] = acc.astype(refs[0].dtype)

    grid = (m_dim // tm, n_dim // tn, nk)
    nc = len(cargo)

    def body(a_ref, b_ref, *rest):
        acc_ref = rest[-1]
        ids = [pl.program_id(d) for d in range(3)]
        outs, end_cargo = _cargo_bounds(
            rest[len(extra):-1], nc, n_out, exchange, (ids[0] == 0) & (ids[1] == 0) & (ids[2] == 0),
            (ids[0] == grid[0] - 1) & (ids[1] == grid[1] - 1) & (ids[2] == grid[2] - 1))
        refs = tuple(rest[:len(extra)]) + tuple(outs)
        part = lax.dot_general(a_ref[...], b_ref[...], dims, preferred_element_type=F32)
        if nk == 1:
            finish(part, refs)
        else:
            k = ids[2]

            @pl.when(k == 0)
            def _():
                acc_ref[...] = part

            @pl.when(k > 0)
            def _():
                acc_ref[...] += part

            @pl.when(k == nk - 1)
            def _():
                finish(acc_ref[...], refs)
        end_cargo()

    tile = pl.BlockSpec((tm, tn), lambda i, j, k: (i, j))
    rowv = pl.BlockSpec((1, tn), lambda i, j, k: (0, j))
    out_sdss = [out_sds] * n_out
    if ln:
        out_sdss = [jax.ShapeDtypeStruct((m_dim, n_dim), dt) for dt in (F32, F32, BF16)]
    out = pl.pallas_call(
        body, name=name, grid=grid,
        in_specs=[a_spec, b_spec] + ([tile, rowv, rowv] if ln else [tile] * len(extra)) + [ANY] * nc,
        out_specs=[out_spec] * n_out + [ANY] * nc,
        out_shape=out_sdss + (exchange[1](cargo) if nc else []),
        scratch_shapes=(exchange[2](nc) if nc else []) + [pltpu.VMEM((tm, tn) if nk > 1 else (8, 128), F32)],
        compiler_params=_cparams(dimension_semantics=("arbitrary",) * 3 if nc else ("parallel", "parallel", "arbitrary")),
    )(a, b, *extra, *cargo)
    if nc:
        return out
    return out if (act or ln) else out[0]


def _mm_groups_nt(parts, b, name):
    m_dim, k_dim = parts[0].shape
    ng, _, n_dim, _ = b.shape
    assert len(parts) == ng and b.shape[3] == k_dim
    tm = _pick(m_dim, (1056, 704, 512, 384, 256, 128))

    def body(*refs):
        a_refs, b_ref, o_ref, acc_ref = refs[:ng], refs[ng], refs[ng + 1], refs[ng + 2]
        k = pl.program_id(1)
        for g in range(ng):
            @pl.when(k == g)
            def _(g=g):
                part = lax.dot_general(a_refs[g][...], b_ref[...], (NT, ((), ())), preferred_element_type=F32)
                if g == 0:
                    acc_ref[...] = part
                elif g < ng - 1:
                    acc_ref[...] += part
                else:
                    o_ref[...] = acc_ref[...] + part

    return pl.pallas_call(
        body, name=name, grid=(m_dim // tm, ng),
        in_specs=[pl.BlockSpec((tm, k_dim), lambda i, k: (i, 0))] * ng
        + [pl.BlockSpec((None, None, n_dim, k_dim), lambda i, k: (k, 0, 0, 0))],
        out_specs=pl.BlockSpec((tm, n_dim), lambda i, k: (i, 0)),
        out_shape=jax.ShapeDtypeStruct((m_dim, n_dim), F32),
        scratch_shapes=[pltpu.VMEM((tm, n_dim), F32)],
        compiler_params=_cparams(dimension_semantics=("parallel", "arbitrary")),
    )(*parts, b)


def _row_tile(t_pad, width):
    for tr in (528, 352, 176, 128, 64):
        if t_pad % tr == 0 and tr * width * 4 <= (3 << 19) and tr % 16 == 0:
            return tr
    return 64 if t_pad % 64 == 0 else t_pad


def _ln_res_fn(h, m, g, b):
    x = ALPHA * h + m
    mu = jnp.mean(x, axis=-1, keepdims=True)
    xc = x - mu
    var = jnp.mean(xc * xc, axis=-1, keepdims=True)
    return xc * lax.rsqrt(var + LN_EPS) * g + b


def _ln_res_bwd(h, m, g, b, dys, name):
    t_pad = h.shape[0]
    tr = _row_tile(t_pad, D)
    nd = len(dys)

    def body(h_ref, m_ref, g_ref, b_ref, *rest):
        d_refs, (dh_ref, dm_ref, dg_ref, db_ref) = rest[:nd], rest[nd:]
        _, vjp = jax.vjp(_ln_res_fn, h_ref[...], m_ref[...], g_ref[...], b_ref[...])
        dy = d_refs[0][...]
        for d_ref in d_refs[1:]:
            dy = dy + d_ref[...]
        dh, dm, dg, db = vjp(dy)
        dh_ref[...] = dh
        dm_ref[...] = dm.astype(BF16)

        @pl.when(pl.program_id(0) == 0)
        def _():
            dg_ref[...] = jnp.zeros_like(dg_ref)
            db_ref[...] = jnp.zeros_like(db_ref)

        dg_ref[...] += dg
        db_ref[...] += db

    row = pl.BlockSpec((tr, D), lambda i: (i, 0))
    par = pl.BlockSpec((1, D), lambda i: (0, 0))
    return pl.pallas_call(
        body, name=name, grid=(t_pad // tr,), in_specs=[row, row, par, par] + [row] * nd,
        out_specs=[row, row, par, par],
        out_shape=[jax.ShapeDtypeStruct((t_pad, D), F32), jax.ShapeDtypeStruct((t_pad, D), BF16),
                   jax.ShapeDtypeStruct((1, D), F32), jax.ShapeDtypeStruct((1, D), F32)],
        compiler_params=_cparams(),
    )(h, m, g, b, *dys)


def _grms_fn(o, z, g):
    y = o * lax.rsqrt(jnp.mean(o * o, axis=-1, keepdims=True) + RMS_EPS) * g
    return y * _silu(z)


def _grms_fwd(o, z_arr, z_blk0, g, name):
    t_pad, w = o.shape
    tr = _row_tile(t_pad, w)
    assert (z_blk0 * HD) % w == 0

    def body(o_ref, z_ref, g_ref, y_ref):
        for h in range(w // HD):
            c = slice(h * HD, (h + 1) * HD)
            y_ref[:, c] = _grms_fn(o_ref[:, c], z_ref[:, c], g_ref[...]).astype(BF16)

    return pl.pallas_call(
        body, name=name, grid=(t_pad // tr,),
        in_specs=[pl.BlockSpec((tr, w), lambda i: (i, 0)), pl.BlockSpec((tr, w), lambda i: (i, z_blk0 * HD // w)),
                  pl.BlockSpec((1, HD), lambda i: (0, 0))],
        out_specs=pl.BlockSpec((tr, w), lambda i: (i, 0)),
        out_shape=jax.ShapeDtypeStruct((t_pad, w), BF16), compiler_params=_cparams(),
    )(o, z_arr, g)


def _grms_bwd(o, z_arr, z_blk0, g, dy_arr, dy_blk0, name):
    t_pad, w = o.shape
    tr = _row_tile(t_pad, w)
    assert (z_blk0 * HD) % w == 0 and (dy_blk0 * HD) % w == 0

    def body(o_ref, z_ref, g_ref, dy_ref, do_ref, dz_ref, dg_ref):
        @pl.when(pl.program_id(0) == 0)
        def _():
            dg_ref[...] = jnp.zeros_like(dg_ref)

        for h in range(w // HD):
            c = slice(h * HD, (h + 1) * HD)
            _, vjp = jax.vjp(_grms_fn, o_ref[:, c], z_ref[:, c], g_ref[...])
            do, dz, dg = vjp(dy_ref[:, c])
            do_ref[:, c] = do
            dz_ref[:, c] = dz.astype(BF16)
            dg_ref[...] += dg

    blk = pl.BlockSpec((tr, w), lambda i: (i, 0))
    return pl.pallas_call(
        body, name=name, grid=(t_pad // tr,),
        in_specs=[blk, pl.BlockSpec((tr, w), lambda i: (i, z_blk0 * HD // w)), pl.BlockSpec((1, HD), lambda i: (0, 0)),
                  pl.BlockSpec((tr, w), lambda i: (i, dy_blk0 * HD // w))],
        out_specs=[blk, blk, pl.BlockSpec((1, HD), lambda i: (0, 0))],
        out_shape=[jax.ShapeDtypeStruct((t_pad, w), F32), jax.ShapeDtypeStruct((t_pad, w), BF16),
                   jax.ShapeDtypeStruct((1, HD), F32)],
        compiler_params=_cparams(),
    )(o, z_arr, g, dy_arr)


def _loss_fwd(y, tgt, first_row, name):
    t_pad = y.shape[0]
    tr = _row_tile(t_pad, D)

    def body(y_ref, t_ref, l_ref, dy_ref):
        rows = pl.program_id(0) * tr + _iota((tr, 1), 0)
        err = jnp.where(rows >= first_row, y_ref[...] - t_ref[...], 0.0)
        dy_ref[...] = err * (1.0 / D)

        @pl.when(pl.program_id(0) == 0)
        def _():
            l_ref[...] = jnp.zeros_like(l_ref)

        part = jnp.sum(jnp.sum(err * err, axis=1, keepdims=True), axis=0, keepdims=True)
        l_ref[...] += jnp.broadcast_to(part * (0.5 / D), l_ref.shape)

    row = pl.BlockSpec((tr, D), lambda i: (i, 0))
    return pl.pallas_call(
        body, name=name, grid=(t_pad // tr,), in_specs=[row, row],
        out_specs=[pl.BlockSpec((8, 128), lambda i: (0, 0)), row],
        out_shape=[jax.ShapeDtypeStruct((8, 128), F32), jax.ShapeDtypeStruct((t_pad, D), F32)],
        compiler_params=_cparams(),
    )(y, tgt)


def _assemble_bf16(parts, name):
    t_pad = parts[0].shape[0]
    widths = [p.shape[1] for p in parts]
    total = sum(widths)
    tr = _row_tile(t_pad, total)

    def body(*refs):
        o_ref = refs[-1]
        off = 0
        for ref, w in zip(refs[:-1], widths):
            o_ref[:, off:off + w] = ref[...].astype(BF16)
            off += w

    return pl.pallas_call(
        body, name=name, grid=(t_pad // tr,), in_specs=[pl.BlockSpec((tr, w), lambda i: (i, 0)) for w in widths],
        out_specs=pl.BlockSpec((tr, total), lambda i: (i, 0)),
        out_shape=jax.ShapeDtypeStruct((t_pad, total), BF16), compiler_params=_cparams(),
    )(*parts)


CONV_K = 4
HALO = 8
RT = 192


def _conv_fwd(p, blk0, w, mode, pad, name):
    t_pad = p.shape[0]
    nt = t_pad // RT
    scale = HD ** -0.5 if mode == "q" else 1.0

    def body(x_ref, w_ref, y_ref, xs_ref):
        xs_ref[0:HALO, :] = jnp.zeros((HALO, HD), F32)
        rows = _iota((t_pad, 1), 0)
        xs_ref[HALO:HALO + t_pad, :] = jnp.where(rows >= pad, x_ref[...], 0.0)
        wv = w_ref[...]

        def tile(i, carry):
            r0 = pl.multiple_of(i * RT, RT)
            ext = xs_ref[pl.ds(r0, RT + HALO), :]
            acc = ext[HALO:, :] * wv[3:4, :]
            for s in (1, 2, 3):
                acc = acc + pltpu.roll(ext, s, 0)[HALO:, :] * wv[3 - s:4 - s, :]
            y = _silu(acc)
            if mode != "v":
                y = y * lax.rsqrt(jnp.sum(y * y, axis=-1, keepdims=True) + L2_EPS) * scale
            y_ref[pl.ds(r0, RT), :] = y
            return carry

        lax.fori_loop(0, nt, tile, 0)

    return pl.pallas_call(
        body, name=name, grid=(GDN_H,),
        in_specs=[pl.BlockSpec((t_pad, HD), lambda h: (0, blk0 + h)), pl.BlockSpec((CONV_K, HD), lambda h: (0, h))],
        out_specs=pl.BlockSpec((t_pad, HD), lambda h: (0, h)),
        out_shape=jax.ShapeDtypeStruct((t_pad, GDN_H * HD), F32),
        scratch_shapes=[pltpu.VMEM((t_pad + HALO, HD), F32)],
        compiler_params=_cparams(),
    )(p, w)


def _conv_bwd(p, blk0, w, dn, mode, pad, name):
    t_pad = p.shape[0]
    nt = t_pad // RT
    scale = HD ** -0.5 if mode == "q" else 1.0

    def body(x_ref, w_ref, dn_ref, dx_ref, dw_ref, xs_ref, ds_ref):
        xs_ref[0:HALO, :] = jnp.zeros((HALO, HD), F32)
        xs_ref[HALO + t_pad:HALO + t_pad + 2 * HALO, :] = jnp.zeros((2 * HALO, HD), F32)
        ds_ref[t_pad:t_pad + HALO, :] = jnp.zeros((HALO, HD), F32)
        rows = _iota((t_pad, 1), 0)
        xs_ref[HALO:HALO + t_pad, :] = jnp.where(rows >= pad, x_ref[...], 0.0)
        ds_ref[0:t_pad, :] = dn_ref[...]
        wv = w_ref[...]

        def tile(i, dw):
            r0 = pl.multiple_of(i * RT, RT)
            ext = xs_ref[pl.ds(r0, RT + 2 * HALO), :]
            dn_e = ds_ref[pl.ds(r0, RT + HALO), :]
            xsh = [ext[HALO:, :]] + [pltpu.roll(ext, s, 0)[HALO:, :] for s in (1, 2, 3)]
            pre = xsh[0] * wv[3:4, :]
            for s in (1, 2, 3):
                pre = pre + xsh[s] * wv[3 - s:4 - s, :]
            sg = _sigmoid(pre)
            y = pre * sg
            if mode != "v":
                ss = jnp.sum(y * y, axis=-1, keepdims=True) + L2_EPS
                r = lax.rsqrt(ss)
                dy = scale * (dn_e * r - y * (r * r * r) * jnp.sum(dn_e * y, axis=-1, keepdims=True))
            else:
                dy = dn_e
            dpre = dy * (sg * (1.0 + pre * (1.0 - sg)))
            dx = dpre[:RT, :] * wv[3:4, :]
            for s in (1, 2, 3):
                dx = dx + pltpu.roll(dpre, RT + HALO - s, 0)[:RT, :] * wv[3 - s:4 - s, :]
            trow = r0 + _iota((RT, 1), 0)
            dx_ref[pl.ds(r0, RT), :] = jnp.where(trow >= pad, dx, 0.0)
            new = []
            for s in (0, 1, 2, 3):
                new.append(dw[s] + jnp.sum(dpre[:RT, :] * xsh[s][:RT, :], axis=0, keepdims=True))
            return tuple(new)

        z = jnp.zeros((1, HD), F32)
        dw = lax.fori_loop(0, nt, tile, (z, z, z, z))
        for s in (0, 1, 2, 3):
            dw_ref[3 - s:4 - s, :] = dw[s]

    return pl.pallas_call(
        body, name=name, grid=(GDN_H,),
        in_specs=[pl.BlockSpec((t_pad, HD), lambda h: (0, blk0 + h)), pl.BlockSpec((CONV_K, HD), lambda h: (0, h)),
                  pl.BlockSpec((t_pad, HD), lambda h: (0, h))],
        out_specs=[pl.BlockSpec((t_pad, HD), lambda h: (0, h)), pl.BlockSpec((CONV_K, HD), lambda h: (0, h))],
        out_shape=[jax.ShapeDtypeStruct((t_pad, GDN_H * HD), F32), jax.ShapeDtypeStruct((CONV_K, GDN_H * HD), F32)],
        scratch_shapes=[pltpu.VMEM((t_pad + 3 * HALO, HD), F32), pltpu.VMEM((t_pad + HALO, HD), F32)],
        compiler_params=_cparams(),
    )(p, w, dn)


@jax.custom_vjp
def _unit_lower_inv(m, bd, eye):
    md = m * bd
    low = m - md
    p2 = mbnn(md, md)
    p4 = mbnn(p2, p2)
    dinv = mbnn(mbnn(eye - md, eye + p2), eye + p4)
    n = mbnn(dinv, low)
    n2 = mbnn(n, n)
    n4 = mbnn(n2, n2)
    return mbnn(mbnn(mbnn(eye - n, eye + n2), eye + n4), dinv)


def _unit_lower_inv_bwd(res, g):
    t, bd, eye = res
    return -mbtn(t, mbnt(g, t)), jnp.zeros_like(bd), jnp.zeros_like(eye)


def _unit_lower_inv_fwd(m, bd, eye):
    t = _unit_lower_inv(m, bd, eye)
    return t, (t, bd, eye)


_unit_lower_inv.defvjp(_unit_lower_inv_fwd, _unit_lower_inv_bwd)


def _gdn_chunks(chunks, alog, dtb, s):
    nh = chunks[0][0].shape[0]
    ri = _iota((1, CH, CH), 1)
    ci = _iota((1, CH, CH), 2)
    causal = ri >= ci
    strict = ri > ci
    eye = (ri == ci).astype(F32)
    bd = ((ri >> 3) == (ci >> 3)).astype(F32)
    ltri = (_iota((CH, CH), 0) >= _iota((CH, CH), 1)).astype(F32)
    sel = (_iota((nh, 1, HD), 2) == _iota((nh, 1, HD), 0)).astype(F32)
    last = _iota((1, CH, 1), 1) == CH - 1

    beta, gc, gc_rows = [], [], []
    for _, _, _, bb, aa, valid in chunks:
        beta_all = jnp.where(valid, _sigmoid(bb), 0.0)
        g_all = jnp.where(valid, -jnp.exp(alog) * _softplus(aa + dtb), 0.0)
        gc_all = hnn(ltri, g_all)
        beta.append(jnp.sum(beta_all[None] * sel, axis=2, keepdims=True))
        gc.append(jnp.sum(gc_all[None] * sel, axis=2, keepdims=True))
        gc_rows.append(hbnt(jnp.broadcast_to(sel, (nh, CH, HD)), jnp.broadcast_to(gc_all[None], (nh, CH, HD))))
    cat = lambda xs: jnp.concatenate(xs, axis=0)
    q, k, v = (cat([c[j] for c in chunks]) for j in range(3))
    beta, gc, gc_rows = cat(beta), cat(gc), cat(gc_rows)
    gc_last = jnp.sum(jnp.where(last, gc, 0.0), axis=1, keepdims=True)
    decay = jnp.exp(jnp.where(causal, gc - gc_rows, NEG))
    egc = jnp.exp(gc)

    kb = k * beta
    m = jnp.where(strict, bbnt(kb, k) * decay, 0.0)
    t_inv = _unit_lower_inv(m, bd, eye)
    u = bbnn(t_inv, v * beta)
    w = bbnn(t_inv, kb * egc)
    a_intra = bbnt(q, k) * decay
    q_dec = q * egc
    k_dec = k * jnp.exp(gc_last - gc)
    g_tot = jnp.exp(gc_last)

    outs = []
    for n in range(len(chunks)):
        part = lambda a: a[n * nh:(n + 1) * nh]
        v_new = part(u) - bbnn(part(w), s)
        outs.append(bbnn(part(q_dec), s) + bbnn(part(a_intra), v_new))
        s = s * part(g_tot) + bbtn(part(k_dec), v_new)
    return outs, s


PAIR = 2 * CH


def _gdn_specs(npair, rev):
    cc = (lambda c: npair - 1 - c) if rev else (lambda c: c)
    wide = pl.BlockSpec((PAIR, GDN_H * HD), lambda c: (cc(c), 0))
    fix = lambda off: pl.BlockSpec((PAIR, HD), lambda c: (cc(c), off))
    par = pl.BlockSpec((1, HD), lambda c: (0, 0))
    state = pl.BlockSpec((1, GDN_H, HD, HD), lambda c: (cc(c), 0, 0, 0))
    return wide, fix, par, state


def _store_heads(ref, a, rows=slice(None)):
    for h in range(a.shape[0]):
        ref[rows, h * HD:(h + 1) * HD] = a[h]


def _chunk_rows(half):
    return slice(half * CH, (half + 1) * CH)


def _chunk_valid(pair, half, pad):
    return ((2 * pair + half) * CH + _iota((CH, 1), 0)) >= pad


def _gdn_fwd(qn, kn, vn, p, alog, dtb, pad, name, cargo=(), exchange=None):
    t_pad = qn.shape[0]
    npair = t_pad // PAIR
    wide, fix, par, state = _gdn_specs(npair, False)
    n = len(cargo)

    def body(q_ref, k_ref, v_ref, bb_ref, aa_ref, al_ref, dt_ref, *rest):
        c = pl.program_id(0)
        s_ref = rest[-1]
        (o_ref, ss_ref), end_cargo = _cargo_bounds(rest[:-1], n, 2, exchange, c == 0, c == npair - 1)

        @pl.when(c == 0)
        def _():
            s_ref[...] = jnp.zeros_like(s_ref)

        s = s_ref[...]
        ss_ref[0] = s
        rows = [_chunk_rows(half) for half in (0, 1)]
        chunks = [(_heads(q_ref[r, :], GDN_H), _heads(k_ref[r, :], GDN_H), _heads(v_ref[r, :], GDN_H),
                   bb_ref[r, :], aa_ref[r, :], _chunk_valid(c, half, pad)) for half, r in enumerate(rows)]
        outs, s = _gdn_chunks(chunks, al_ref[...], dt_ref[...], s)
        for r, o in zip(rows, outs):
            _store_heads(o_ref, o, r)
        s_ref[...] = s
        end_cargo()

    return pl.pallas_call(
        body, name=name, grid=(npair,),
        in_specs=[wide, wide, wide, fix(16), fix(17), par, par] + [ANY] * n,
        out_specs=[wide, state] + [ANY] * n,
        out_shape=[jax.ShapeDtypeStruct((t_pad, GDN_H * HD), F32), jax.ShapeDtypeStruct((npair, GDN_H, HD, HD), F32)]
        + (exchange[1](cargo) if n else []),
        scratch_shapes=(exchange[2](n) if n else []) + [pltpu.VMEM((GDN_H, HD, HD), F32)],
        compiler_params=_cparams(),
    )(qn, kn, vn, p, p, alog, dtb, *cargo)


def _gdn_bwd(qn, kn, vn, p, alog, dtb, ssave, do, pad, name, cargo=(), exchange=None):
    t_pad = qn.shape[0]
    npair = t_pad // PAIR
    wide, fix, par, state = _gdn_specs(npair, True)
    n = len(cargo)

    def body(q_ref, k_ref, v_ref, bb_ref, aa_ref, al_ref, dt_ref, ss_ref, do_ref, *rest):
        c = pl.program_id(0)
        ds_ref = rest[-1]
        (dq_ref, dk_ref, dv_ref, dbb_ref, daa_ref, dal_ref, ddt_ref), end_cargo = _cargo_bounds(
            rest[:-1], n, 7, exchange, c == 0, c == npair - 1)

        @pl.when(c == 0)
        def _():
            ds_ref[...] = jnp.zeros_like(ds_ref)
            dal_ref[...] = jnp.zeros_like(dal_ref)
            ddt_ref[...] = jnp.zeros_like(ddt_ref)

        ra, rb = _chunk_rows(0), _chunk_rows(1)
        va, vb = _chunk_valid(npair - 1 - c, 0, pad), _chunk_valid(npair - 1 - c, 1, pad)

        def pair(qa, ka, va_, ba, aa, qb, kb, vb_, bb, ab, al, dt, s):
            (oa, ob), s = _gdn_chunks([(qa, ka, va_, ba, aa, va), (qb, kb, vb_, bb, ab, vb)], al, dt, s)
            return oa, ob, s

        ins = [f(ref[r, :]) for r in (ra, rb)
               for ref, f in ((q_ref, lambda a: _heads(a, GDN_H)), (k_ref, lambda a: _heads(a, GDN_H)),
                              (v_ref, lambda a: _heads(a, GDN_H)), (bb_ref, lambda a: a), (aa_ref, lambda a: a))]
        _, vjp = jax.vjp(pair, *ins, al_ref[...], dt_ref[...], ss_ref[0])
        g = vjp((_heads(do_ref[ra, :], GDN_H), _heads(do_ref[rb, :], GDN_H), ds_ref[...]))
        for r, (dq, dk, dv, dbb, daa) in ((ra, g[0:5]), (rb, g[5:10])):
            _store_heads(dq_ref, dq, r)
            _store_heads(dk_ref, dk, r)
            _store_heads(dv_ref, dv, r)
            dbb_ref[r, :] = dbb
            daa_ref[r, :] = daa
        dal_ref[...] += g[10]
        ddt_ref[...] += g[11]
        ds_ref[...] = g[12]
        end_cargo()

    sds = jax.ShapeDtypeStruct
    return pl.pallas_call(
        body, name=name, grid=(npair,),
        in_specs=[wide, wide, wide, fix(16), fix(17), par, par, state, wide] + [ANY] * n,
        out_specs=[wide, wide, wide, fix(0), fix(0), par, par] + [ANY] * n,
        out_shape=[sds((t_pad, GDN_H * HD), F32)] * 3 + [sds((t_pad, HD), F32)] * 2 + [sds((1, HD), F32)] * 2
        + (exchange[1](cargo) if n else []),
        scratch_shapes=(exchange[2](n) if n else []) + [pltpu.VMEM((GDN_H, HD, HD), F32)],
        compiler_params=_cparams(),
    )(qn, kn, vn, p, p, alog, dtb, ssave, do, *cargo)


SB_Q0, SB_K0, SB_V0 = 18, 22, 26
SB_SCALE = SB_DH ** -0.5
SB_NB_FWD, SB_NB_BWD = 11, 8


def _sb_terms(z, allowed):
    nz = -z
    raw = jnp.minimum(nz, 0.0) - jnp.log(1.0 + jnp.exp(jnp.minimum(z, nz)))
    l1m = raw if allowed is None else jnp.where(allowed, raw, 0.0)
    ls = z + raw
    return l1m, ls, jnp.exp(ls)


def _sb_passes(i, step, carry, per):
    total = i + 1

    def sized(done, first):
        return [functools.partial(step, done, masked=({0} if first else set()) | {nb - 1}, nb=nb)
                for nb in range(1, per + 1)]

    def several(c):
        n_mid = (total - per - 1) // per
        c = step(0, c, masked={0}, nb=per)
        c = lax.fori_loop(0, n_mid, lambda t, cc: step(per * (1 + t), cc, masked=set(), nb=per), c)
        done = per * (1 + n_mid)
        return lax.switch(total - done - 1, sized(done, False), c)

    return lax.cond(total <= per, lambda c: lax.switch(total - 1, sized(0, True), c), several, carry)


def _sb_stack(a, i):
    first = _iota((1, HD), 1) < SB_DH
    a2 = jnp.concatenate([jnp.where(first, a, 0.0), jnp.where(first, 0.0, a)], axis=0).astype(BF16)
    rq = i * QB + _iota((QB, 1), 0)
    return a2, jnp.concatenate([rq, rq], axis=0), first


def _hi_lo(a):
    hi = a.astype(BF16)
    lo = (a - hi.astype(F32)).astype(BF16)
    return jnp.concatenate([hi, lo], axis=1)


def _cargo_bounds(refs, n, n_out, exchange, first, last):
    outs = refs[n:n + n_out]
    if not n:
        return outs, lambda: None
    ex = exchange[0](refs[:n], refs[n + n_out:2 * n + n_out], *refs[2 * n + n_out:])

    @pl.when(first)
    def _():
        ex.start()

    def finish():
        @pl.when(last)
        def _():
            ex.wait()

    return outs, finish


def _sb_fwd(p, pad, name, cargo=(), exchange=None):
    t_pad = p.shape[0]
    nq = t_pad // QB
    n = len(cargo)

    def body(q_ref, k_ref, v_ref, *rest):
        i = pl.program_id(1)
        pr = pl.program_id(0)
        (o_ref, r_ref), end_cargo = _cargo_bounds(rest, n, 2, exchange, (pr == 0) & (i == 0),
                                                  (pr == SB_H // 2 - 1) & (i == nq - 1))
        q2, rowq, first = _sb_stack(q_ref[...] * SB_SCALE, i)
        tri = (_iota((QB, QB), 0) > _iota((QB, QB), 1)).astype(BF16)
        upper2 = jnp.concatenate([jnp.concatenate([tri, tri], axis=0), jnp.ones((2 * QB, QB), BF16)], axis=1)

        def chain(kb, masked):
            start = pl.multiple_of(kb * QB, QB)
            kblk = k_ref[pl.ds(start, QB), :].astype(BF16)
            vblk = v_ref[pl.ds(start, QB), :].astype(BF16)
            z = lax.dot_general(q2, kblk, (NT, ((), ())), preferred_element_type=F32)
            colk = kb * QB + _iota((1, QB), 1)
            al = ((colk < rowq) & (colk >= pad)) if masked else None
            l1m, ls, _ = _sb_terms(z, al)
            sums = lax.dot_general(_hi_lo(l1m), upper2, (NN, ((), ())), preferred_element_type=F32)
            return al, ls, sums[:, :QB], sums[:, QB:], vblk

        def step(done, carry, masked, nb):
            o_acc, run = carry
            ws, vs = [], []
            for n in range(nb):
                al, ls, suf, rs, vblk = chain(i - done - n, n in masked)
                wgt = jnp.exp(ls + suf + run)
                ws.append((wgt if al is None else jnp.where(al, wgt, 0.0)).astype(BF16))
                vs.append(vblk)
                run = run + rs
            o_acc = o_acc + lax.dot_general(jnp.concatenate(ws, axis=1), jnp.concatenate(vs, axis=0),
                                            (NN, ((), ())), preferred_element_type=F32)
            return o_acc, run

        o_acc, run = _sb_passes(i, step, (jnp.zeros((2 * QB, HD), F32), jnp.zeros((2 * QB, QB), F32)), SB_NB_FWD)
        o_ref[...] = jnp.where(first, o_acc[:QB], o_acc[QB:]).astype(BF16)
        r_ref[...] = jnp.where(first, run[:QB], run[QB:])
        end_cargo()

    full = lambda off: pl.BlockSpec((t_pad, HD), lambda pr, i: (0, off + pr))
    blk = pl.BlockSpec((QB, HD), lambda pr, i: (i, pr))
    return pl.pallas_call(
        body, name=name, grid=(SB_H // 2, nq),
        in_specs=[pl.BlockSpec((QB, HD), lambda pr, i: (i, SB_Q0 + pr)), full(SB_K0), full(SB_V0)] + [ANY] * n,
        out_specs=[blk, blk] + [ANY] * n,
        out_shape=[jax.ShapeDtypeStruct((t_pad, SB_H * SB_DH), BF16), jax.ShapeDtypeStruct((t_pad, SB_H * SB_DH), F32)]
        + (exchange[1](cargo) if n else []),
        scratch_shapes=exchange[2](n) if n else [],
        compiler_params=_cparams(),
    )(p, p, p, *cargo)


def _sb_bwd(p, rtot, dy, dy_blk0, pad, name, cargo=(), exchange=None):
    t_pad = p.shape[0]
    nq = t_pad // QB
    n = len(cargo)

    def body(q_ref, k_ref, v_ref, r_ref, do_ref, *rest):
        i = pl.program_id(1)
        pr = pl.program_id(0)
        dkt_ref, dvt_ref = rest[-2:]
        (dq_ref, dk_ref, dv_ref), end_cargo = _cargo_bounds(rest[:-2], n, 3, exchange, (pr == 0) & (i == 0),
                                                            (pr == SB_H // 2 - 1) & (i == nq - 1))

        @pl.when(i == 0)
        def _():
            dkt_ref[...] = jnp.zeros_like(dkt_ref)
            dvt_ref[...] = jnp.zeros_like(dvt_ref)

        q2, rowq, first = _sb_stack(q_ref[...] * SB_SCALE, i)
        do2, _, _ = _sb_stack(do_ref[...], i)
        q2t = jnp.transpose(q2.astype(F32)).astype(BF16)
        do2t = jnp.transpose(do2.astype(F32)).astype(BF16)
        rt = r_ref[...]
        lane = _iota((1, HD), 1)
        rcol = jnp.concatenate([jnp.sum(jnp.where(lane == 0, rt, 0.0), axis=1, keepdims=True),
                                jnp.sum(jnp.where(lane == SB_DH, rt, 0.0), axis=1, keepdims=True)], axis=0)
        rj = _iota((QB, QB), 0)
        cs = _iota((QB, QB), 1)
        tri_u = (rj > cs).astype(BF16)
        tri_l = (rj < cs).astype(BF16)
        ones2 = jnp.ones((2 * QB, QB), BF16)
        upper2 = jnp.concatenate([jnp.concatenate([tri_u, tri_u], axis=0), ones2], axis=1)
        lower2 = jnp.concatenate([jnp.concatenate([tri_l, tri_l], axis=0), ones2], axis=1)
        rcol = jnp.broadcast_to(rcol, (2 * QB, QB))

        def chain(kb, masked):
            start = pl.multiple_of(kb * QB, QB)
            kblk = k_ref[pl.ds(start, QB), :].astype(BF16)
            vblk = v_ref[pl.ds(start, QB), :].astype(BF16)
            z = lax.dot_general(q2, kblk, (NT, ((), ())), preferred_element_type=F32)
            colk = kb * QB + _iota((1, QB), 1)
            al = ((colk < rowq) & (colk >= pad)) if masked else None
            l1m, ls, sg = _sb_terms(z, al)
            dwgt = lax.dot_general(do2, vblk, (NT, ((), ())), preferred_element_type=F32)
            sums = lax.dot_general(_hi_lo(l1m), upper2, (NN, ((), ())), preferred_element_type=F32)
            return kb, kblk, al, ls, sums[:, :QB], sums[:, QB:], dwgt, sg

        def finish(c, left, gseen):
            kb, kblk, al, ls, suf, rs, dwgt, sg = c
            left = left - rs
            wgt = jnp.exp(ls + suf + left)
            if al is not None:
                wgt = jnp.where(al, wgt, 0.0)
            dl = dwgt * wgt
            sums = lax.dot_general(_hi_lo(dl), lower2, (NN, ((), ())), preferred_element_type=F32)
            gpre = gseen + sums[:, :QB]
            dz = dl - sg * (dl + gpre)
            if al is not None:
                dz = jnp.where(al, dz, 0.0)
            dz = dz.astype(BF16)
            dkt_ref[kb] += lax.dot_general(q2t, dz, (NN, ((), ())), preferred_element_type=F32)
            dvt_ref[kb] += lax.dot_general(do2t, wgt.astype(BF16), (NN, ((), ())), preferred_element_type=F32)
            return dz, left, gseen + sums[:, QB:]

        def step(done, carry, masked, nb):
            dq_acc, left, gseen = carry
            cs_ = [chain(done + n, n in masked) for n in range(nb)]
            dzs = []
            for c in cs_:
                dz, left, gseen = finish(c, left, gseen)
                dzs.append(dz)
            dq_acc = dq_acc + lax.dot_general(jnp.concatenate(dzs, axis=1), jnp.concatenate([c[1] for c in cs_], axis=0),
                                              (NN, ((), ())), preferred_element_type=F32)
            return dq_acc, left, gseen

        dq_acc, _, _ = _sb_passes(i, step, (jnp.zeros((2 * QB, HD), F32), rcol, jnp.zeros((2 * QB, QB), F32)),
                                  SB_NB_BWD)
        dq_ref[...] = jnp.where(first, dq_acc[:QB], dq_acc[QB:]) * SB_SCALE

        @pl.when(i == nq - 1)
        def _():
            for kb in range(nq):
                dk_ref[kb * QB:(kb + 1) * QB, :] = jnp.transpose(dkt_ref[kb])
                dv_ref[kb * QB:(kb + 1) * QB, :] = jnp.transpose(dvt_ref[kb])

        end_cargo()

    full_in = lambda off: pl.BlockSpec((t_pad, HD), lambda pr, i: (0, off + pr))
    full_out = pl.BlockSpec((t_pad, HD), lambda pr, i: (0, pr))
    blk = pl.BlockSpec((QB, HD), lambda pr, i: (i, pr))
    sds = jax.ShapeDtypeStruct((t_pad, SB_H * SB_DH), F32)
    return pl.pallas_call(
        body, name=name, grid=(SB_H // 2, nq),
        in_specs=[pl.BlockSpec((QB, HD), lambda pr, i: (i, SB_Q0 + pr)), full_in(SB_K0), full_in(SB_V0), blk,
                  pl.BlockSpec((QB, HD), lambda pr, i: (i, dy_blk0 + pr))] + [ANY] * n,
        out_specs=[blk, full_out, full_out] + [ANY] * n,
        out_shape=[sds, sds, sds] + (exchange[1](cargo) if n else []),
        scratch_shapes=(exchange[2](n) if n else []) + [pltpu.VMEM((nq, HD, QB), F32)] * 2,
        compiler_params=_cparams(),
    )(p, p, p, rtot, dy, *cargo)


HG_LEVELS = 6


def _hg_prefix_matrix():
    t = np.arange(CH)[:, None]
    j = np.arange(CH)[None, :]
    groups = [(j <= t)]
    for lvl in range(1, HG_LEVELS + 1):
        half = CH >> lvl
        e = (t // (2 * half)) * (2 * half) + half - 1
        groups.append(j <= e)
    groups.append(np.ones((8, CH), bool))
    e = np.concatenate(groups, axis=0).astype(np.float32)
    return np.concatenate([e, e, e], axis=1), np.concatenate([e, e, e], axis=0)


HG_G = 8


def _hg_chunk(qr, fr, iv, r0, r1, st, valid, ecat):
    g = st.shape[0]
    mx = jnp.maximum(r0, r1)
    e0 = jnp.exp(r0 - mx)
    e1 = jnp.exp(r1 - mx)
    lb = e1 / (e0 + e1)
    fg = lb + (1.0 - lb) * _sigmoid(fr)
    logf = jnp.where(valid, jnp.log(fg), 0.0)
    kk = jnp.where(valid, 1.0 - fg, 0.0)
    q = jnp.where(valid, _silu(qr), 0.0)
    v = _heads(jnp.where(valid, iv, 0.0), g)

    pre = _mask_dot(ecat, logf)
    b = pre[0:CH]
    b_last = jnp.max(pre[(HG_LEVELS + 1) * CH:], axis=0, keepdims=True)
    row = _iota((CH, 1), 0)
    ri = _iota((1, CH, CH), 1)
    ci = _iota((1, CH, CH), 2)
    a = jnp.where(ri == ci, jnp.sum(_heads(q * kk, g), axis=2, keepdims=True), 0.0)
    for lvl in range(1, HG_LEVELS + 1):
        half = CH >> lvl
        m = pre[lvl * CH:(lvl + 1) * CH]
        low = (row & half) != 0
        dec = jnp.exp(jnp.where(low, b - m, m - b))
        qt = jnp.where(low, q * dec, 0.0)
        kt = jnp.where(low, 0.0, kk * dec)
        same = (ri >> (7 - lvl)) == (ci >> (7 - lvl))
        a = a + jnp.where(same, bbnt(_heads(qt, g), _heads(kt, g)), 0.0)
    o = bbnt(_heads(q * jnp.exp(b), g), st) + bbnn(a, v)
    kd = kk * jnp.exp(b_last - b)
    st_new = st * _heads(jnp.exp(b_last), g) + bbtn(v, _heads(kd, g))
    return o, st_new


def _hg_specs(npair, rev):
    cc = (lambda c: npair - 1 - c) if rev else (lambda c: c)
    ng = HG_H // HG_G
    blk = lambda off: pl.BlockSpec((PAIR, HG_G * HD), lambda h, c: (cc(c), off * ng + h))
    lbs = pl.BlockSpec((2, HG_G * HD), lambda h, c: (0, h))
    state = pl.BlockSpec((1, HG_G, HD, HD), lambda h, c: (cc(c), h, 0, 0))
    return ng, blk, lbs, state


def _hg_fwd(p, lbraw, ecat, pad, name):
    t_pad = p.shape[0]
    npair = t_pad // PAIR
    ng, blk, lbs, state = _hg_specs(npair, False)

    def body(q_ref, f_ref, i_ref, lb_ref, e_ref, et_ref, o_ref, ss_ref, s_ref):
        c = pl.program_id(1)

        @pl.when(c == 0)
        def _():
            s_ref[...] = jnp.zeros_like(s_ref)

        st = s_ref[...]
        ss_ref[0] = st
        for half in (0, 1):
            r = _chunk_rows(half)
            o, st = _hg_chunk(q_ref[r, :], f_ref[r, :], i_ref[r, :], lb_ref[0:1, :], lb_ref[1:2, :], st,
                              _chunk_valid(c, half, pad), (e_ref[...], et_ref[...]))
            _store_heads(o_ref, o, r)
        s_ref[...] = st

    return pl.pallas_call(
        body, name=name, grid=(ng, npair),
        in_specs=[blk(0), blk(1), blk(2), lbs] + [pl.BlockSpec(e.shape, lambda h, c: (0, 0)) for e in ecat],
        out_specs=[blk(0), state],
        out_shape=[jax.ShapeDtypeStruct((t_pad, HG_H * HD), F32), jax.ShapeDtypeStruct((npair, HG_H, HD, HD), F32)],
        scratch_shapes=[pltpu.VMEM((HG_G, HD, HD), F32)],
        compiler_params=_cparams(),
    )(p, p, p, lbraw, *ecat)


def _hg_bwd(p, lbraw, ecat, ssave, do, pad, name, cargo=(), exchange=None):
    t_pad = p.shape[0]
    npair = t_pad // PAIR
    ng, blk, lbs, state = _hg_specs(npair, True)
    n = len(cargo)

    def body(q_ref, f_ref, i_ref, lb_ref, e_ref, et_ref, ss_ref, do_ref, *rest):
        c = pl.program_id(1)
        hg = pl.program_id(0)
        ds_ref = rest[-1]
        (dq_ref, df_ref, di_ref, dlb_ref), end_cargo = _cargo_bounds(
            rest[:-1], n, 4, exchange, (hg == 0) & (c == 0), (hg == ng - 1) & (c == npair - 1))

        @pl.when(c == 0)
        def _():
            ds_ref[...] = jnp.zeros_like(ds_ref)
            dlb_ref[...] = jnp.zeros_like(dlb_ref)

        ra, rb = _chunk_rows(0), _chunk_rows(1)
        va, vb = _chunk_valid(npair - 1 - c, 0, pad), _chunk_valid(npair - 1 - c, 1, pad)
        ecv = (e_ref[...], et_ref[...])

        def pair(qa, fa, ia, qb, fb, ib, r0, r1, st):
            oa, st = _hg_chunk(qa, fa, ia, r0, r1, st, va, ecv)
            ob, st = _hg_chunk(qb, fb, ib, r0, r1, st, vb, ecv)
            return oa, ob, st

        ins = [ref[r, :] for r in (ra, rb) for ref in (q_ref, f_ref, i_ref)]
        _, vjp = jax.vjp(pair, *ins, lb_ref[0:1, :], lb_ref[1:2, :], ss_ref[0])
        g = vjp((_heads(do_ref[ra, :], HG_G), _heads(do_ref[rb, :], HG_G), ds_ref[...]))
        for r, (dq, df, di) in ((ra, g[0:3]), (rb, g[3:6])):
            dq_ref[r, :] = dq.astype(BF16)
            df_ref[r, :] = df.astype(BF16)
            di_ref[r, :] = di.astype(BF16)
        dlb_ref[0:1, :] += g[6]
        dlb_ref[1:2, :] += g[7]
        ds_ref[...] = g[8]
        end_cargo()

    sds = jax.ShapeDtypeStruct((t_pad, HG_H * HD), BF16)
    return pl.pallas_call(
        body, name=name, grid=(ng, npair),
        in_specs=[blk(0), blk(1), blk(2), lbs] + [pl.BlockSpec(e.shape, lambda h, c: (0, 0)) for e in ecat]
        + [state, blk(0)] + [ANY] * n,
        out_specs=[blk(0), blk(0), blk(0), lbs] + [ANY] * n,
        out_shape=[sds, sds, sds, jax.ShapeDtypeStruct((2, HG_H * HD), F32)] + (exchange[1](cargo) if n else []),
        scratch_shapes=(exchange[2](n) if n else []) + [pltpu.VMEM((HG_G, HD, HD), F32)],
        compiler_params=_cparams(),
    )(p, p, p, lbraw, *ecat, ssave, do, *cargo)


def _pad_ab_cols(w):
    z = jnp.zeros((w.shape[0], HD - GDN_H), w.dtype)
    return jnp.concatenate([w[:, :2048], w[:, 2048:2052], z, w[:, 2052:2056], z, w[:, 2056:]], axis=1)


def _unpad_ab_cols(w):
    return jnp.concatenate([w[:, :2048], w[:, 2048:2052], w[:, 2176:2180], w[:, 2304:]], axis=1)


def _lane_pad(v):
    return jnp.pad(v, ((0, 0), (0, HD - v.shape[1])))


def _mlp_fwd(h, hb, w1, w2, layer, g, b):
    a, r = _mm(hb, w1, b_view=("cols", layer), out_dtype=BF16, act=True, name=f"mlp_up_{layer}")
    m, y, yb = _mm(r, w2, b_view=("rows", layer), ln=(h, g, b), name=f"mlp_down_{layer}")
    return a, r, m, y, yb


def _mlp_bwd(hb, a, r, dmb, w1, w2, layer):
    da = _mm(dmb, w2, tb=True, b_view=("rows", layer), out_dtype=BF16, gate=a, name=f"mlp_down_dx_{layer}")
    dw2 = _mm(r, dmb, ta=True, out_dtype=BF16, name=f"mlp_down_dw_{layer}")
    dh = _mm(da, w1, tb=True, b_view=("cols", layer), name=f"mlp_up_dx_{layer}")
    dw1 = _mm(hb, da, ta=True, out_dtype=BF16, out_split=N_CHIP, name=f"mlp_up_dw_{layer}")
    return dh, dw1, dw2


def _local_step(h0, tgt, w, pad, late=None):
    row = lambda a, i: a[i:i + 1]
    ecat = tuple(jnp.asarray(e, dtype=BF16) for e in _hg_prefix_matrix())
    cw = [w["conv_w"][:, i * 512:(i + 1) * 512] for i in range(3)]
    alog, dtb = _lane_pad(w["a_log"]), _lane_pad(w["dt_bias"])

    h0b = h0.astype(BF16)
    p0 = _mm(h0b, w["ab_w_in"], name="ab_in")
    qn = _conv_fwd(p0, 0, cw[0], "q", pad, "conv_q")
    kn = _conv_fwd(p0, 4, cw[1], "k", pad, "conv_k")
    vn = _conv_fwd(p0, 8, cw[2], "v", pad, "conv_v")
    if late is None:
        oa_raw, ss0 = _gdn_fwd(qn, kn, vn, p0, alog, dtb, pad, "gdn_fwd")
        ob, rtot = _sb_fwd(p0, pad, "sb_fwd")
    else:
        oa_raw, ss0, g_cin, g_cout = _gdn_fwd(qn, kn, vn, p0, alog, dtb, pad, "gdn_fwd",
                                              cargo=[late["c_w_in"], late["c_w_out"]], exchange=GATHER)
        ob, rtot, g_about, g_w1, g_w2 = _sb_fwd(p0, pad, "sb_fwd", exchange=GATHER,
                                                cargo=[late["ab_w_out"], late["mlp_w1"], late["mlp_w2"]])
        w = dict(w, ab_w_out=g_about.reshape(D, D), c_w_in=g_cin, c_w_out=g_cout.reshape(D, D), mlp_w1=g_w1, mlp_w2=g_w2)
    oa = _grms_fwd(oa_raw, p0, 12, w["ab_gnorm_g"], "gdn_gate")
    ycat = jnp.concatenate([oa, ob], axis=1)
    mix0, h1, h1b = _mm(ycat, w["ab_w_out"], name="ab_out", ln=(h0, row(w["ln_mix_g"], 0), row(w["ln_mix_b"], 0)))
    a0, r0, m0, h2, h2b = _mlp_fwd(h1, h1b, w["mlp_w1"], w["mlp_w2"], 0, row(w["ln_ffn_g"], 0), row(w["ln_ffn_b"], 0))
    p1 = _mm(h2b, w["c_w_in"], b_view=("cols", 0), name="c_in")
    oc_raw, ss1 = _hg_fwd(p1, w["c_lb_raw"], ecat, pad, "hg_fwd")
    yc = _grms_fwd(oc_raw, p1, 3 * HG_H, w["c_gnorm_g"], "hg_gate")
    mix1, h3, h3b = _mm(yc, w["c_w_out"], name="c_out", ln=(h2, row(w["ln_mix_g"], 1), row(w["ln_mix_b"], 1)))
    a1, r1, m1, h4, _ = _mlp_fwd(h3, h3b, w["mlp_w1"], w["mlp_w2"], 1, row(w["ln_ffn_g"], 1), row(w["ln_ffn_b"], 1))
    loss, dh4 = _loss_fwd(h4, tgt, pad + N_META, "loss")

    dh3a, dm1b, dfg1, dfb1 = _ln_res_bwd(h3, m1, row(w["ln_ffn_g"], 1), row(w["ln_ffn_b"], 1), [dh4], "ln_ffn_bwd_1")
    dh3b, dw1_1, dw2_1 = _mlp_bwd(h3b, a1, r1, dm1b, w["mlp_w1"], w["mlp_w2"], 1)
    dh2a, dmix1b, dmg1, dmb1 = _ln_res_bwd(h2, mix1, row(w["ln_mix_g"], 1), row(w["ln_mix_b"], 1), [dh3a, dh3b], "ln_mix_bwd_1")
    dyc = _mm(dmix1b, w["c_w_out"], tb=True, name="c_out_dx")
    dwco = _mm(yc, dmix1b, ta=True, out_dtype=BF16, name="c_out_dw")
    doc, dzc, dcg = _grms_bwd(oc_raw, p1, 3 * HG_H, w["c_gnorm_g"], dyc, 0, "hg_gate_bwd")
    landed = {}
    rows4 = lambda a: a.reshape(N_CHIP, -1, D)
    if late is None:
        dq1, df1, di1, dlb = _hg_bwd(p1, w["c_lb_raw"], ecat, ss1, doc, pad, "hg_bwd")
    else:
        dq1, df1, di1, dlb, landed["w1_1"] = _hg_bwd(
            p1, w["c_lb_raw"], ecat, ss1, doc, pad, "hg_bwd", cargo=[dw1_1], exchange=SCATTER)
    dp1 = [dq1, df1, di1, dzc]
    dh2b = _mm_groups_nt(dp1, w["c_w_in"], "c_in_dx")
    dwc = jnp.stack([_mm(h2b, d, ta=True, out_dtype=BF16, name=f"c_in_dw_{i}") for i, d in enumerate(dp1)])
    dh1a, dm0b, dfg0, dfb0 = _ln_res_bwd(h1, m0, row(w["ln_ffn_g"], 0), row(w["ln_ffn_b"], 0), [dh2a, dh2b], "ln_ffn_bwd_0")
    dh1b, dw1_0, dw2_0 = _mlp_bwd(h1b, a0, r0, dm0b, w["mlp_w1"], w["mlp_w2"], 0)
    dh0a, dmix0b, dmg0, dmb0 = _ln_res_bwd(h0, mix0, row(w["ln_mix_g"], 0), row(w["ln_mix_b"], 0), [dh1a, dh1b], "ln_mix_bwd_0")
    dycat = _mm(dmix0b, w["ab_w_out"], tb=True, name="ab_out_dx")
    dwabo = _mm(ycat, dmix0b, ta=True, out_dtype=BF16, name="ab_out_dw")
    doa, dza, dag = _grms_bwd(oa_raw, p0, 12, w["ab_gnorm_g"], dycat, 0, "gdn_gate_bwd")
    if late is None:
        dqn, dkn, dvn, dbb, daa, dal, ddt = _gdn_bwd(qn, kn, vn, p0, alog, dtb, ss0, doa, pad, "gdn_bwd")
        dqb, dkb, dvb = _sb_bwd(p0, rtot, dycat, 4, pad, "sb_bwd")
    else:
        dqn, dkn, dvn, dbb, daa, dal, ddt, landed["c_w_in"] = _gdn_bwd(
            qn, kn, vn, p0, alog, dtb, ss0, doa, pad, "gdn_bwd", cargo=[dwc], exchange=SCATTER)
        (dqb, dkb, dvb, landed["w1_0"], landed["w2_0"], landed["w2_1"], landed["ab_w_out"],
         landed["c_w_out"]) = _sb_bwd(
            p0, rtot, dycat, 4, pad, "sb_bwd",
            cargo=[dw1_0, rows4(dw2_0), rows4(dw2_1), rows4(dwabo), rows4(dwco)], exchange=SCATTER)
    dpq, dcq = _conv_bwd(p0, 0, cw[0], dqn, "q", pad, "conv_q_bwd")
    dpk, dck = _conv_bwd(p0, 4, cw[1], dkn, "k", pad, "conv_k_bwd")
    dpv, dcv = _conv_bwd(p0, 8, cw[2], dvn, "v", pad, "conv_v_bwd")
    dp0 = _assemble_bf16([dpq, dpk, dpv, dza, dbb, daa, dqb, dkb, dvb], "ab_in_dy")
    dwab = _mm(h0b, dp0, ta=True, out_dtype=BF16, name="ab_in_dw")
    if late is None:
        dh0 = _mm(dp0, w["ab_w_in"], tb=True, plus=dh0a, name="ab_in_dx")
    else:
        dab = jnp.transpose(_unpad_ab_cols(dwab).reshape(D, N_CHIP, AB_TRUE // N_CHIP), (1, 0, 2))
        dh0, landed["ab_w_in"] = _mm(dp0, w["ab_w_in"], tb=True, plus=dh0a, name="ab_in_dx", cargo=[dab],
                                     exchange=SCATTER)

    grads = {
        "ab_w_in": dwab, "conv_w": jnp.concatenate([dcq, dck, dcv], axis=1),
        "a_log": dal[:, :GDN_H], "dt_bias": ddt[:, :GDN_H],
        "ab_gnorm_g": dag, "ab_w_out": dwabo, "c_w_in": dwc, "c_lb_raw": dlb, "c_gnorm_g": dcg, "c_w_out": dwco,
        "ln_mix_g": jnp.concatenate([dmg0, dmg1], 0), "ln_mix_b": jnp.concatenate([dmb0, dmb1], 0),
        "w1_0": dw1_0, "w1_1": dw1_1, "w2_0": dw2_0, "w2_1": dw2_1,
        "ln_ffn_g": jnp.concatenate([dfg0, dfg1], 0), "ln_ffn_b": jnp.concatenate([dfb0, dfb1], 0),
        "landed": landed,
    }
    return loss, dh0, grads


MESH = pl.DeviceIdType.MESH
ANY = pl.BlockSpec(memory_space=pl.ANY)
N_CHIP = 4
N_DEV = 8
CHIP_REL = ((1, 0), (0, 1), (1, 1))
DEV_REL = tuple((dx, dy, dc) for dx in (0, 1) for dy in (0, 1) for dc in (0, 1))[1:]

def _pos():
    return lax.axis_index("x"), lax.axis_index("y"), lax.axis_index("c")


def _flip(a, d):
    return a + d - 2 * a * d


class _Exchange:
    def __init__(self, local, sends, recvs):
        self.local, self.sends, self.recvs = local, sends, recvs

    def start(self):
        for cp in self.local + self.sends:
            cp.start()

    def wait(self):
        for cp in self.recvs:
            cp.wait_recv()
        for cp in self.sends:
            cp.wait_send()
        for cp in self.local:
            cp.wait()


def _gather_sems(n):
    return [pltpu.SemaphoreType.DMA((3 * n,)), pltpu.SemaphoreType.DMA((3 * n,)), pltpu.SemaphoreType.DMA((n,))]


def _gather_copies(x_refs, o_refs, send_sems, recv_sems, local_sems):
    n = len(x_refs)
    x, y, c = _pos()
    local = [pltpu.make_async_copy(x_refs[a], o_refs[a].at[2 * x + y], local_sems.at[a]) for a in range(n)]

    def copy(a, k, sending):
        tx, ty = _flip(x, CHIP_REL[k][0]), _flip(y, CHIP_REL[k][1])
        return pltpu.make_async_remote_copy(
            src_ref=x_refs[a], dst_ref=o_refs[a].at[2 * x + y if sending else 2 * tx + ty],
            send_sem=send_sems.at[3 * a + k], recv_sem=recv_sems.at[3 * a + k], device_id=(tx, ty, c), device_id_type=MESH)

    pairs = [(a, k) for a in range(n) for k in range(3)]
    return _Exchange(local, [copy(a, k, True) for a, k in pairs], [copy(a, k, False) for a, k in pairs])


def _gather_shapes(bufs):
    return [jax.ShapeDtypeStruct((N_CHIP,) + b.shape, b.dtype) for b in bufs]


def _chip_allgather(bufs, name):
    n = len(bufs)

    def body(*refs):
        ex = _gather_copies(refs[:n], refs[n:2 * n], *refs[2 * n:])
        ex.start()
        ex.wait()

    return pl.pallas_call(
        body, name=name, in_specs=[ANY] * n, out_specs=[ANY] * n, out_shape=_gather_shapes(bufs),
        scratch_shapes=_gather_sems(n), compiler_params=pltpu.CompilerParams(has_side_effects=True),
    )(*bufs)


def _scatter_sems(n):
    nr = N_DEV - 1
    return [pltpu.SemaphoreType.DMA((nr * n,)), pltpu.SemaphoreType.DMA((nr * n,)), pltpu.SemaphoreType.DMA((n,))]


def _scatter_copies(g_refs, o_refs, send_sems, recv_sems, local_sems):
    n = len(g_refs)
    nr = N_DEV - 1
    x, y, c = _pos()
    me = 4 * x + 2 * y + c
    local = [pltpu.make_async_copy(g_refs[a].at[2 * x + y], o_refs[a].at[me], local_sems.at[a]) for a in range(n)]

    def copy(a, k, sending):
        dx, dy, dc = DEV_REL[k]
        tx, ty, tc = _flip(x, dx), _flip(y, dy), _flip(c, dc)
        return pltpu.make_async_remote_copy(
            src_ref=g_refs[a].at[2 * tx + ty], dst_ref=o_refs[a].at[me if sending else 4 * tx + 2 * ty + tc],
            send_sem=send_sems.at[nr * a + k], recv_sem=recv_sems.at[nr * a + k],
            device_id=(tx, ty, tc), device_id_type=MESH)

    pairs = [(a, k) for a in range(n) for k in range(nr)]
    return _Exchange(local, [copy(a, k, True) for a, k in pairs], [copy(a, k, False) for a, k in pairs])


def _scatter_shapes(gs):
    return [jax.ShapeDtypeStruct((N_DEV,) + g.shape[1:], g.dtype) for g in gs]


GATHER = (_gather_copies, _gather_shapes, _gather_sems)
SCATTER = (_scatter_copies, _scatter_shapes, _scatter_sems)


def _sum_slots(rs, name):
    n, rh, w = rs[0].shape
    tr = _pick(rh, (256, 128, 64, 16))

    def body(*refs):
        o_ref = refs[-1]
        for layer, r_ref in enumerate(refs[:-1]):
            acc = r_ref[0].astype(F32)
            for s in range(1, n):
                acc = acc + r_ref[s].astype(F32)
            o_ref[layer] = acc

    return pl.pallas_call(
        body, name=name, grid=(rh // tr,), in_specs=[pl.BlockSpec((n, tr, w), lambda i: (0, i, 0))] * len(rs),
        out_specs=pl.BlockSpec((len(rs), tr, w), lambda i: (0, i, 0)),
        out_shape=jax.ShapeDtypeStruct((len(rs), rh, w), F32), compiler_params=_cparams(),
    )(*rs)


def _small_allreduce(buf, name):
    r, w = buf.shape

    def body(b_ref, o_ref, land_ref, send_sems, recv_sems):
        x, y, c = _pos()
        me = 4 * x + 2 * y + c
        land_ref[me] = b_ref[...]

        def target(k):
            dx, dy, dc = DEV_REL[k]
            return _flip(x, dx), _flip(y, dy), _flip(c, dc)

        sends = []
        for k in range(N_DEV - 1):
            tx, ty, tc = target(k)
            cp = pltpu.make_async_remote_copy(
                src_ref=b_ref, dst_ref=land_ref.at[me], send_sem=send_sems.at[k], recv_sem=recv_sems.at[k],
                device_id=(tx, ty, tc), device_id_type=MESH)
            cp.start()
            sends.append(cp)
        for k in range(N_DEV - 1):
            tx, ty, tc = target(k)
            pltpu.make_async_remote_copy(
                src_ref=b_ref, dst_ref=land_ref.at[4 * tx + 2 * ty + tc], send_sem=send_sems.at[k],
                recv_sem=recv_sems.at[k], device_id=(tx, ty, tc), device_id_type=MESH).wait_recv()
        for cp in sends:
            cp.wait_send()
        acc = land_ref[0]
        for s in range(1, N_DEV):
            acc = acc + land_ref[s]
        o_ref[...] = acc

    vm = pl.BlockSpec(memory_space=pltpu.VMEM)
    return pl.pallas_call(
        body, name=name, in_specs=[vm], out_specs=vm, out_shape=jax.ShapeDtypeStruct((r, w), F32),
        scratch_shapes=[pltpu.VMEM((N_DEV, r, w), F32), pltpu.SemaphoreType.DMA((N_DEV - 1,)),
                        pltpu.SemaphoreType.DMA((N_DEV - 1,))],
        compiler_params=pltpu.CompilerParams(has_side_effects=True),
    )(buf)


def _adamw(w, g, m, v, name):
    r, c = w.shape
    tr = _pick(r, (256, 128, 64, 8)) if r * c > (1 << 18) else r

    def body(w_ref, g_ref, m_ref, v_ref, d_ref, m2_ref, v2_ref):
        gg = g_ref[...]
        m2 = ADAM_B1 * m_ref[...] + (1.0 - ADAM_B1) * gg
        v2 = ADAM_B2 * v_ref[...] + (1.0 - ADAM_B2) * (gg * gg)
        m_hat = m2 / (1.0 - ADAM_B1 ** ADAM_STEP)
        v_hat = v2 / (1.0 - ADAM_B2 ** ADAM_STEP)
        d_ref[...] = -ADAM_LR * (m_hat / (jnp.sqrt(v_hat) + ADAM_EPS) + ADAM_WD * w_ref[...])
        m2_ref[...] = m2
        v2_ref[...] = v2

    blk = pl.BlockSpec((tr, c), lambda i: (i, 0))
    sds = jax.ShapeDtypeStruct((r, c), F32)
    return pl.pallas_call(body, name=name, grid=(r // tr,), in_specs=[blk] * 4, out_specs=[blk] * 3,
                          out_shape=[sds] * 3, compiler_params=_cparams())(w, g, m, v)


BIG = ("ab_w_in", "ab_w_out", "c_w_in", "c_w_out", "mlp_w1", "mlp_w2")
SMALL = ("ln_mix_g", "ln_mix_b", "ln_ffn_g", "ln_ffn_b", "c_lb_raw", "ab_a_log", "ab_dt_bias", "ab_gnorm_g", "c_gnorm_g")
SMALL_ROWS = 16
CONV_ROWS = 8
CONV_W = 3 * GDN_H * HD


def _conv_to_rows(cw):
    return jnp.pad(cw, ((0, 0), (0, 2 * D - CONV_W))).reshape(CONV_ROWS, D)


def _rows_to_conv(rows):
    return rows.reshape(CONV_K, 2 * D)[:, :CONV_W]


def _pack_small(d):
    rows = [jnp.pad(d[n], ((0, 0), (0, D - d[n].shape[1]))) for n in SMALL]
    buf = jnp.concatenate(rows, axis=0)
    return jnp.pad(buf, ((0, SMALL_ROWS - buf.shape[0]), (0, 0)))


def _unpack_small(buf, like):
    out, r = {}, 0
    for n in SMALL:
        nr, nc = like[n].shape
        out[n] = buf[r:r + nr, :nc]
        r += nr
    return out


def kernel(x, meta_tokens, ab_w_in, ab_conv_w, ab_a_log, ab_dt_bias, ab_gnorm_g, ab_w_out, c_w_in, c_lb_raw, c_gnorm_g, c_w_out, ln_mix_g, ln_mix_b, mlp_w1, mlp_w2, ln_ffn_g, ln_ffn_b, loss_target, m_meta_tokens, m_ab_w_in, m_ab_conv_w, m_ab_a_log, m_ab_dt_bias, m_ab_gnorm_g, m_ab_w_out, m_c_w_in, m_c_lb_raw, m_c_gnorm_g, m_c_w_out, m_ln_mix_g, m_ln_mix_b, m_mlp_w1, m_mlp_w2, m_ln_ffn_g, m_ln_ffn_b, v_meta_tokens, v_ab_w_in, v_ab_conv_w, v_ab_a_log, v_ab_dt_bias, v_ab_gnorm_g, v_ab_w_out, v_c_w_in, v_c_lb_raw, v_c_gnorm_g, v_c_w_out, v_ln_mix_g, v_ln_mix_b, v_mlp_w1, v_mlp_w2, v_ln_ffn_g, v_ln_ffn_b):
    names = ("meta_tokens", "ab_w_in", "ab_conv_w", "ab_a_log", "ab_dt_bias", "ab_gnorm_g", "ab_w_out", "c_w_in",
             "c_lb_raw", "c_gnorm_g", "c_w_out", "ln_mix_g", "ln_mix_b", "mlp_w1", "mlp_w2", "ln_ffn_g", "ln_ffn_b")
    wts = dict(zip(names, (meta_tokens, ab_w_in, ab_conv_w, ab_a_log, ab_dt_bias, ab_gnorm_g, ab_w_out, c_w_in, c_lb_raw,
                           c_gnorm_g, c_w_out, ln_mix_g, ln_mix_b, mlp_w1, mlp_w2, ln_ffn_g, ln_ffn_b)))
    mom_m = dict(zip(names, (m_meta_tokens, m_ab_w_in, m_ab_conv_w, m_ab_a_log, m_ab_dt_bias, m_ab_gnorm_g, m_ab_w_out,
                             m_c_w_in, m_c_lb_raw, m_c_gnorm_g, m_c_w_out, m_ln_mix_g, m_ln_mix_b, m_mlp_w1, m_mlp_w2,
                             m_ln_ffn_g, m_ln_ffn_b)))
    mom_v = dict(zip(names, (v_meta_tokens, v_ab_w_in, v_ab_conv_w, v_ab_a_log, v_ab_dt_bias, v_ab_gnorm_g, v_ab_w_out,
                             v_c_w_in, v_c_lb_raw, v_c_gnorm_g, v_c_w_out, v_ln_mix_g, v_ln_mix_b, v_mlp_w1, v_mlp_w2,
                             v_ln_ffn_g, v_ln_ffn_b)))
    seq = x.shape[1]
    pad = (-(N_META + seq)) % QB
    xi, yi, ci = _pos()
    chip = 2 * xi + yi

    gat_ab_in, = _chip_allgather([ab_w_in[0].astype(BF16)], "gather_weights")
    late = {"ab_w_out": ab_w_out[0].astype(BF16), "c_w_in": c_w_in.astype(BF16), "c_w_out": c_w_out[0].astype(BF16),
            "mlp_w1": mlp_w1.astype(BF16), "mlp_w2": mlp_w2.astype(BF16)}
    mcols, ccols = meta_tokens.shape[1], ab_conv_w.shape[2]
    place = jnp.concatenate([
        lax.dynamic_update_slice(jnp.zeros((N_META, D), F32), 0.5 * meta_tokens, (0, chip * mcols)),
        _conv_to_rows(lax.dynamic_update_slice(jnp.zeros((CONV_K, CONV_W), F32), 0.5 * ab_conv_w[0], (0, chip * ccols)))],
        axis=0)
    placed = _small_allreduce(place, "gather_meta")
    meta_full = placed[:N_META]

    w = {
        "ab_w_in": _pad_ab_cols(jnp.transpose(gat_ab_in, (1, 0, 2)).reshape(D, AB_TRUE)),
        "conv_w": _rows_to_conv(placed[N_META:]), "a_log": ab_a_log, "dt_bias": ab_dt_bias,
        "ab_gnorm_g": ab_gnorm_g, "c_lb_raw": c_lb_raw,
        "c_gnorm_g": c_gnorm_g, "ln_mix_g": ln_mix_g, "ln_mix_b": ln_mix_b, "ln_ffn_g": ln_ffn_g, "ln_ffn_b": ln_ffn_b,
    }

    h0 = jnp.concatenate([jnp.zeros((pad, D), F32), meta_full, x[0]], axis=0)
    tgt = jnp.concatenate([jnp.zeros((pad + N_META, D), F32), loss_target[0]], axis=0)
    loss8, dh0, g = _local_step(h0, tgt, w, pad, late)
    grad_x = dh0[pad + N_META:][None]

    gsmall = {"ln_mix_g": g["ln_mix_g"], "ln_mix_b": g["ln_mix_b"], "ln_ffn_g": g["ln_ffn_g"], "ln_ffn_b": g["ln_ffn_b"],
              "c_lb_raw": g["c_lb_raw"], "ab_a_log": g["a_log"], "ab_dt_bias": g["dt_bias"], "ab_gnorm_g": g["ab_gnorm_g"],
              "c_gnorm_g": g["c_gnorm_g"]}
    packed = _pack_small(gsmall).at[SMALL_ROWS - 1, :loss8.shape[1]].set(loss8[0])
    sbuf = jnp.concatenate([packed, dh0[pad:pad + N_META], _conv_to_rows(g["conv_w"])], axis=0)
    ssum = _small_allreduce(sbuf, "allreduce_small")
    loss = ssum[SMALL_ROWS - 1, 0]
    grads = _unpack_small(ssum[:SMALL_ROWS], wts)
    grads["meta_tokens"] = lax.dynamic_slice(ssum[SMALL_ROWS:SMALL_ROWS + N_META], (0, chip * mcols), (N_META, mcols))
    grads["ab_conv_w"] = lax.dynamic_slice(_rows_to_conv(ssum[SMALL_ROWS + N_META:]), (0, chip * ccols), (CONV_K, ccols))[None]

    landed = g["landed"]
    for n in ("ab_w_in", "ab_w_out", "c_w_in", "c_w_out"):
        grads[n] = _sum_slots([landed[n]], f"grad_sum_{n}")
    grads["mlp_w1"] = _sum_slots([landed["w1_0"], landed["w1_1"]], "grad_sum_mlp_w1")
    grads["mlp_w2"] = _sum_slots([landed["w2_0"], landed["w2_1"]], "grad_sum_mlp_w2")

    delta, new_m, new_v = {}, {}, {}
    for n in ("meta_tokens", "ab_conv_w") + BIG:
        shp = wts[n].shape
        to2 = lambda a: a.reshape(-1, shp[-1])
        d2, m2, v2 = _adamw(to2(wts[n]), to2(grads[n]), to2(mom_m[n]), to2(mom_v[n]), f"adamw_{n}")
        delta[n], new_m[n], new_v[n] = d2.reshape(shp), m2.reshape(shp), v2.reshape(shp)
    d2, m2, v2 = _adamw(_pack_small(wts), ssum[:SMALL_ROWS], _pack_small(mom_m), _pack_small(mom_v), "adamw_small")
    delta.update(_unpack_small(d2, wts))
    new_m.update(_unpack_small(m2, wts))
    new_v.update(_unpack_small(v2, wts))

    return (loss, grad_x, *[grads[n] for n in names], *[delta[n] for n in names], *[new_m[n] for n in names],
            *[new_v[n] for n in names])
```

```python
import functools

import numpy as np
import jax
import jax.numpy as jnp
from jax import lax
from jax.experimental import pallas as pl
from jax.experimental.pallas import tpu as pltpu

F32 = jnp.float32
BF16 = jnp.bfloat16

D = 1024
N_META = 16
DEPTH = 2
GDN_H = 4
SB_H = 8
SB_DH = 64
HG_H = 8
HD = 128
CH = 64
QB = 128
ALPHA = float((2 * DEPTH) ** 0.25)
LN_EPS = 1e-5
RMS_EPS = 1e-6
L2_EPS = 1e-6
NEG = -1e30

ADAM_LR = 0.001
ADAM_B1 = 0.9
ADAM_B2 = 0.999
ADAM_EPS = 1e-08
ADAM_WD = 0.01
ADAM_STEP = 10

AB_TRUE = 3592
V7X_VMEM_BYTES = 64 * 1024 * 1024
VMEM_LIMIT = V7X_VMEM_BYTES - 8 * 1024 * 1024

NN = ((1,), (0,))
NT = ((1,), (1,))
TN = ((0,), (0,))


def _cparams(**kw):
    return pltpu.CompilerParams(vmem_limit_bytes=VMEM_LIMIT, **kw)


def _dg(a, b, dims, mode):
    if mode == "h":
        return lax.dot_general(a, b, dims, precision=lax.Precision.HIGHEST, preferred_element_type=F32)
    if mode == "b":
        return lax.dot_general(a.astype(BF16), b.astype(BF16), dims, preferred_element_type=F32)
    ah, bh = a.astype(BF16), b.astype(BF16)
    al, bl = (a - ah.astype(F32)).astype(BF16), (b - bh.astype(F32)).astype(BF16)
    d = lambda x, y: lax.dot_general(x, y, dims, preferred_element_type=F32)
    return d(ah, bh) + (d(ah, bl) + d(al, bh))


def _make_dots(mode, batched=False):
    if batched:
        nn_d, nt_d, tn_d = (((2,), (1,)), ((0,), (0,))), (((2,), (2,)), ((0,), (0,))), (((1,), (1,)), ((0,), (0,)))
    else:
        nn_d, nt_d, tn_d = (NN, ((), ())), (NT, ((), ())), (TN, ((), ()))

    @jax.custom_vjp
    def nn(a, b):
        return _dg(a, b, nn_d, mode)

    @jax.custom_vjp
    def nt(a, b):
        return _dg(a, b, nt_d, mode)

    @jax.custom_vjp
    def tn(a, b):
        return _dg(a, b, tn_d, mode)

    nn.defvjp(lambda a, b: (nn(a, b), (a, b)), lambda r, g: (nt(g, r[1]), tn(r[0], g)))
    nt.defvjp(lambda a, b: (nt(a, b), (a, b)), lambda r, g: (nn(g, r[1]), tn(g, r[0])))
    tn.defvjp(lambda a, b: (tn(a, b), (a, b)), lambda r, g: (nt(r[1], g), nn(r[0], g)))
    return nn, nt, tn


hnn = _make_dots("h")[0]
bbnn, bbnt, bbtn = _make_dots("b", True)
mbnn, mbnt, mbtn = _make_dots("m", True)
hbnt = _make_dots("h", True)[1]


def _split3(x, axis):
    x1 = x.astype(BF16)
    r1 = x - x1.astype(F32)
    x2 = r1.astype(BF16)
    x3 = (r1 - x2.astype(F32)).astype(BF16)
    return jnp.concatenate([x1, x2, x3], axis=axis)


@jax.custom_vjp
def _mask_dot(e3, x):
    return lax.dot_general(e3[0], _split3(x, 0), (NN, ((), ())), preferred_element_type=F32)


def _mask_dot_bwd(e3, g):
    dx = lax.dot_general(e3[1], _split3(g, 0), (TN, ((), ())), preferred_element_type=F32)
    return (jnp.zeros_like(e3[0]), jnp.zeros_like(e3[1])), dx


_mask_dot.defvjp(lambda e3, x: (_mask_dot(e3, x), e3), _mask_dot_bwd)


def _heads(a, n):
    return jnp.concatenate([a[None, :, h * HD:(h + 1) * HD] for h in range(n)], axis=0)


def _sigmoid(x):
    return jax.nn.sigmoid(x)


def _silu(x):
    return x * jax.nn.sigmoid(x)


def _softplus(x):
    return jnp.maximum(x, 0.0) + jnp.log(1.0 + jnp.exp(-jnp.abs(x)))


def _iota(shape, dim):
    return lax.broadcasted_iota(jnp.int32, shape, dim)


def _pick(n, prefs):
    for p in prefs:
        if n % p == 0:
            return p
    return n


def _mm(a, b, *, ta=False, tb=False, out_dtype=F32, name, b_view=None, out_split=0, act=False, gate=None, plus=None,
        ln=None, cargo=(), exchange=None):
    if ta:
        k_dim, m_dim = a.shape
    else:
        m_dim, k_dim = a.shape
    if b_view is None:
        w_rows, w_cols = b.shape
    else:
        kind, layer = b_view
        nj, _, blk_r, blk_c = b.shape
        w_rows, w_cols = (blk_r, nj * blk_c) if kind == "cols" else (nj * blk_r, blk_c)
    n_dim = w_rows if tb else w_cols
    assert (w_cols if tb else w_rows) == k_dim
    tm = _pick(m_dim, (1024, 1056, 704, 640, 512, 384, 256, 128))
    tn = _pick(n_dim, (1024, 1056, 704, 640, 512, 384, 256, 128))
    tk = _pick(k_dim, (1024, 1056, 704, 512, 384, 256, 128))
    nk = k_dim // tk
    a_spec = pl.BlockSpec((tk, tm), lambda i, j, k: (k, i)) if ta else pl.BlockSpec((tm, tk), lambda i, j, k: (i, k))
    wb = (tn, tk) if tb else (tk, tn)
    w_idx = (lambda i, j, k: (j, k)) if tb else (lambda i, j, k: (k, j))
    if b_view is None:
        b_spec = pl.BlockSpec(wb, w_idx)
    elif kind == "cols":
        per = blk_c // wb[1]
        b_spec = pl.BlockSpec((None, None) + wb,
                              lambda i, j, k: (w_idx(i, j, k)[1] // per, layer, w_idx(i, j, k)[0], w_idx(i, j, k)[1] % per))
    else:
        per = blk_r // wb[0]
        b_spec = pl.BlockSpec((None, None) + wb,
                              lambda i, j, k: (w_idx(i, j, k)[0] // per, layer, w_idx(i, j, k)[0] % per, w_idx(i, j, k)[1]))
    if out_split:
        per_o = (n_dim // out_split) // tn
        out_spec = pl.BlockSpec((None, tm, tn), lambda i, j, k: (j // per_o, i, j % per_o))
        out_sds = jax.ShapeDtypeStruct((out_split, m_dim, n_dim // out_split), out_dtype)
    else:
        out_spec = pl.BlockSpec((tm, tn), lambda i, j, k: (i, j))
        out_sds = jax.ShapeDtypeStruct((m_dim, n_dim), out_dtype)
    dims = (((0 if ta else 1,), (1 if tb else 0,)), ((), ()))
    assert sum(e is not None for e in (gate, plus, ln)) <= 1
    extra = [e for e in (gate, plus) if e is not None] + list(ln or ())
    n_out = 2 if act else 3 if ln else 1
    assert ln is None or (tn == n_dim and not out_split)

    def finish(acc, refs):
        if ln:
            y = _ln_res_fn(refs[0][...], acc, refs[1][...], refs[2][...])
            refs[3][...] = acc
            refs[4][...] = y
            refs[5][...] = y.astype(BF16)
        elif act:
            refs[0][...] = acc.astype(refs[0].dtype)
            r = jnp.maximum(acc, 0.0)
            refs[1][...] = (r * r).astype(refs[1].dtype)
        elif gate is not None:
            refs[1][...] = (acc * (2.0 * jnp.maximum(refs[0][...].astype(F32), 0.0))).astype(refs[1].dtype)
        elif plus is not None:
            refs[1][...] = (refs[0][...] + acc).astype(refs[1].dtype)
        else:
            refs[0][...] = acc.astype(refs[0].dtype)

    grid = (m_dim // tm, n_dim // tn, nk)
    nc = len(cargo)

    def body(a_ref, b_ref, *rest):
        acc_ref = rest[-1]
        ids = [pl.program_id(d) for d in range(3)]
        outs, end_cargo = _cargo_bounds(
            rest[len(extra):-1], nc, n_out, exchange, (ids[0] == 0) & (ids[1] == 0) & (ids[2] == 0),
            (ids[0] == grid[0] - 1) & (ids[1] == grid[1] - 1) & (ids[2] == grid[2] - 1))
        refs = tuple(rest[:len(extra)]) + tuple(outs)
        part = lax.dot_general(a_ref[...], b_ref[...], dims, preferred_element_type=F32)
        if nk == 1:
            finish(part, refs)
        else:
            k = ids[2]

            @pl.when(k == 0)
            def _():
                acc_ref[...] = part

            @pl.when(k > 0)
            def _():
                acc_ref[...] += part

            @pl.when(k == nk - 1)
            def _():
                finish(acc_ref[...], refs)
        end_cargo()

    tile = pl.BlockSpec((tm, tn), lambda i, j, k: (i, j))
    rowv = pl.BlockSpec((1, tn), lambda i, j, k: (0, j))
    out_sdss = [out_sds] * n_out
    if ln:
        out_sdss = [jax.ShapeDtypeStruct((m_dim, n_dim), dt) for dt in (F32, F32, BF16)]
    out = pl.pallas_call(
        body, name=name, grid=grid,
        in_specs=[a_spec, b_spec] + ([tile, rowv, rowv] if ln else [tile] * len(extra)) + [ANY] * nc,
        out_specs=[out_spec] * n_out + [ANY] * nc,
        out_shape=out_sdss + (exchange[1](cargo) if nc else []),
        scratch_shapes=(exchange[2](nc) if nc else []) + [pltpu.VMEM((tm, tn) if nk > 1 else (8, 128), F32)],
        compiler_params=_cparams(dimension_semantics=("arbitrary",) * 3 if nc else ("parallel", "parallel", "arbitrary")),
    )(a, b, *extra, *cargo)
    if nc:
        return out
    return out if (act or ln) else out[0]


def _mm_groups_nt(parts, b, name):
    m_dim, k_dim = parts[0].shape
    ng, _, n_dim, _ = b.shape
    assert len(parts) == ng and b.shape[3] == k_dim
    tm = _pick(m_dim, (1056, 704, 512, 384, 256, 128))

    def body(*refs):
        a_refs, b_ref, o_ref, acc_ref = refs[:ng], refs[ng], refs[ng + 1], refs[ng + 2]
        k = pl.program_id(1)
        for g in range(ng):
            @pl.when(k == g)
            def _(g=g):
                part = lax.dot_general(a_refs[g][...], b_ref[...], (NT, ((), ())), preferred_element_type=F32)
                if g == 0:
                    acc_ref[...] = part
                elif g < ng - 1:
                    acc_ref[...] += part
                else:
                    o_ref[...] = acc_ref[...] + part

    return pl.pallas_call(
        body, name=name, grid=(m_dim // tm, ng),
        in_specs=[pl.BlockSpec((tm, k_dim), lambda i, k: (i, 0))] * ng
        + [pl.BlockSpec((None, None, n_dim, k_dim), lambda i, k: (k, 0, 0, 0))],
        out_specs=pl.BlockSpec((tm, n_dim), lambda i, k: (i, 0)),
        out_shape=jax.ShapeDtypeStruct((m_dim, n_dim), F32),
        scratch_shapes=[pltpu.VMEM((tm, n_dim), F32)],
        compiler_params=_cparams(dimension_semantics=("parallel", "arbitrary")),
    )(*parts, b)


def _row_tile(t_pad, width):
    for tr in (528, 352, 176, 128, 64):
        if t_pad % tr == 0 and tr * width * 4 <= (3 << 19) and tr % 16 == 0:
            return tr
    return 64 if t_pad % 64 == 0 else t_pad


def _ln_res_fn(h, m, g, b):
    x = ALPHA * h + m
    mu = jnp.mean(x, axis=-1, keepdims=True)
    xc = x - mu
    var = jnp.mean(xc * xc, axis=-1, keepdims=True)
    return xc * lax.rsqrt(var + LN_EPS) * g + b


def _ln_res_bwd(h, m, g, b, dys, name):
    t_pad = h.shape[0]
    tr = _row_tile(t_pad, D)
    nd = len(dys)

    def body(h_ref, m_ref, g_ref, b_ref, *rest):
        d_refs, (dh_ref, dm_ref, dg_ref, db_ref) = rest[:nd], rest[nd:]
        _, vjp = jax.vjp(_ln_res_fn, h_ref[...], m_ref[...], g_ref[...], b_ref[...])
        dy = d_refs[0][...]
        for d_ref in d_refs[1:]:
            dy = dy + d_ref[...]
        dh, dm, dg, db = vjp(dy)
        dh_ref[...] = dh
        dm_ref[...] = dm.astype(BF16)

        @pl.when(pl.program_id(0) == 0)
        def _():
            dg_ref[...] = jnp.zeros_like(dg_ref)
            db_ref[...] = jnp.zeros_like(db_ref)

        dg_ref[...] += dg
        db_ref[...] += db

    row = pl.BlockSpec((tr, D), lambda i: (i, 0))
    par = pl.BlockSpec((1, D), lambda i: (0, 0))
    return pl.pallas_call(
        body, name=name, grid=(t_pad // tr,), in_specs=[row, row, par, par] + [row] * nd,
        out_specs=[row, row, par, par],
        out_shape=[jax.ShapeDtypeStruct((t_pad, D), F32), jax.ShapeDtypeStruct((t_pad, D), BF16),
                   jax.ShapeDtypeStruct((1, D), F32), jax.ShapeDtypeStruct((1, D), F32)],
        compiler_params=_cparams(),
    )(h, m, g, b, *dys)


def _grms_fn(o, z, g):
    y = o * lax.rsqrt(jnp.mean(o * o, axis=-1, keepdims=True) + RMS_EPS) * g
    return y * _silu(z)


def _grms_fwd(o, z_arr, z_blk0, g, name):
    t_pad, w = o.shape
    tr = _row_tile(t_pad, w)
    assert (z_blk0 * HD) % w == 0

    def body(o_ref, z_ref, g_ref, y_ref):
        for h in range(w // HD):
            c = slice(h * HD, (h + 1) * HD)
            y_ref[:, c] = _grms_fn(o_ref[:, c], z_ref[:, c], g_ref[...]).astype(BF16)

    return pl.pallas_call(
        body, name=name, grid=(t_pad // tr,),
        in_specs=[pl.BlockSpec((tr, w), lambda i: (i, 0)), pl.BlockSpec((tr, w), lambda i: (i, z_blk0 * HD // w)),
                  pl.BlockSpec((1, HD), lambda i: (0, 0))],
        out_specs=pl.BlockSpec((tr, w), lambda i: (i, 0)),
        out_shape=jax.ShapeDtypeStruct((t_pad, w), BF16), compiler_params=_cparams(),
    )(o, z_arr, g)


def _grms_bwd(o, z_arr, z_blk0, g, dy_arr, dy_blk0, name):
    t_pad, w = o.shape
    tr = _row_tile(t_pad, w)
    assert (z_blk0 * HD) % w == 0 and (dy_blk0 * HD) % w == 0

    def body(o_ref, z_ref, g_ref, dy_ref, do_ref, dz_ref, dg_ref):
        @pl.when(pl.program_id(0) == 0)
        def _():
            dg_ref[...] = jnp.zeros_like(dg_ref)

        for h in range(w // HD):
            c = slice(h * HD, (h + 1) * HD)
            _, vjp = jax.vjp(_grms_fn, o_ref[:, c], z_ref[:, c], g_ref[...])
            do, dz, dg = vjp(dy_ref[:, c])
            do_ref[:, c] = do
            dz_ref[:, c] = dz.astype(BF16)
            dg_ref[...] += dg

    blk = pl.BlockSpec((tr, w), lambda i: (i, 0))
    return pl.pallas_call(
        body, name=name, grid=(t_pad // tr,),
        in_specs=[blk, pl.BlockSpec((tr, w), lambda i: (i, z_blk0 * HD // w)), pl.BlockSpec((1, HD), lambda i: (0, 0)),
                  pl.BlockSpec((tr, w), lambda i: (i, dy_blk0 * HD // w))],
        out_specs=[blk, blk, pl.BlockSpec((1, HD), lambda i: (0, 0))],
        out_shape=[jax.ShapeDtypeStruct((t_pad, w), F32), jax.ShapeDtypeStruct((t_pad, w), BF16),
                   jax.ShapeDtypeStruct((1, HD), F32)],
        compiler_params=_cparams(),
    )(o, z_arr, g, dy_arr)


def _loss_fwd(y, tgt, first_row, name):
    t_pad = y.shape[0]
    tr = _row_tile(t_pad, D)

    def body(y_ref, t_ref, l_ref, dy_ref):
        rows = pl.program_id(0) * tr + _iota((tr, 1), 0)
        err = jnp.where(rows >= first_row, y_ref[...] - t_ref[...], 0.0)
        dy_ref[...] = err * (1.0 / D)

        @pl.when(pl.program_id(0) == 0)
        def _():
            l_ref[...] = jnp.zeros_like(l_ref)

        part = jnp.sum(jnp.sum(err * err, axis=1, keepdims=True), axis=0, keepdims=True)
        l_ref[...] += jnp.broadcast_to(part * (0.5 / D), l_ref.shape)

    row = pl.BlockSpec((tr, D), lambda i: (i, 0))
    return pl.pallas_call(
        body, name=name, grid=(t_pad // tr,), in_specs=[row, row],
        out_specs=[pl.BlockSpec((8, 128), lambda i: (0, 0)), row],
        out_shape=[jax.ShapeDtypeStruct((8, 128), F32), jax.ShapeDtypeStruct((t_pad, D), F32)],
        compiler_params=_cparams(),
    )(y, tgt)


def _assemble_bf16(parts, name):
    t_pad = parts[0].shape[0]
    widths = [p.shape[1] for p in parts]
    total = sum(widths)
    tr = _row_tile(t_pad, total)

    def body(*refs):
        o_ref = refs[-1]
        off = 0
        for ref, w in zip(refs[:-1], widths):
            o_ref[:, off:off + w] = ref[...].astype(BF16)
            off += w

    return pl.pallas_call(
        body, name=name, grid=(t_pad // tr,), in_specs=[pl.BlockSpec((tr, w), lambda i: (i, 0)) for w in widths],
        out_specs=pl.BlockSpec((tr, total), lambda i: (i, 0)),
        out_shape=jax.ShapeDtypeStruct((t_pad, total), BF16), compiler_params=_cparams(),
    )(*parts)


CONV_K = 4
HALO = 8
RT = 384


def _conv_fwd(p, blk0, w, mode, pad, name):
    t_pad = p.shape[0]
    nt = t_pad // RT
    scale = HD ** -0.5 if mode == "q" else 1.0

    def body(x_ref, w_ref, y_ref, xs_ref):
        xs_ref[0:HALO, :] = jnp.zeros((HALO, HD), F32)
        rows = _iota((t_pad, 1), 0)
        xs_ref[HALO:HALO + t_pad, :] = jnp.where(rows >= pad, x_ref[...], 0.0)
        wv = w_ref[...]

        def tile(i, carry):
            r0 = pl.multiple_of(i * RT, RT)
            ext = xs_ref[pl.ds(r0, RT + HALO), :]
            acc = ext[HALO:, :] * wv[3:4, :]
            for s in (1, 2, 3):
                acc = acc + pltpu.roll(ext, s, 0)[HALO:, :] * wv[3 - s:4 - s, :]
            y = _silu(acc)
            if mode != "v":
                y = y * lax.rsqrt(jnp.sum(y * y, axis=-1, keepdims=True) + L2_EPS) * scale
            y_ref[pl.ds(r0, RT), :] = y
            return carry

        lax.fori_loop(0, nt, tile, 0)

    return pl.pallas_call(
        body, name=name, grid=(GDN_H,),
        in_specs=[pl.BlockSpec((t_pad, HD), lambda h: (0, blk0 + h)), pl.BlockSpec((CONV_K, HD), lambda h: (0, h))],
        out_specs=pl.BlockSpec((t_pad, HD), lambda h: (0, h)),
        out_shape=jax.ShapeDtypeStruct((t_pad, GDN_H * HD), F32),
        scratch_shapes=[pltpu.VMEM((t_pad + HALO, HD), F32)],
        compiler_params=_cparams(),
    )(p, w)


def _conv_bwd(p, blk0, w, dn, mode, pad, name):
    t_pad = p.shape[0]
    nt = t_pad // RT
    scale = HD ** -0.5 if mode == "q" else 1.0

    def body(x_ref, w_ref, dn_ref, dx_ref, dw_ref, xs_ref, ds_ref):
        xs_ref[0:HALO, :] = jnp.zeros((HALO, HD), F32)
        xs_ref[HALO + t_pad:HALO + t_pad + 2 * HALO, :] = jnp.zeros((2 * HALO, HD), F32)
        ds_ref[t_pad:t_pad + HALO, :] = jnp.zeros((HALO, HD), F32)
        rows = _iota((t_pad, 1), 0)
        xs_ref[HALO:HALO + t_pad, :] = jnp.where(rows >= pad, x_ref[...], 0.0)
        ds_ref[0:t_pad, :] = dn_ref[...]
        wv = w_ref[...]

        def tile(i, dw):
            r0 = pl.multiple_of(i * RT, RT)
            ext = xs_ref[pl.ds(r0, RT + 2 * HALO), :]
            dn_e = ds_ref[pl.ds(r0, RT + HALO), :]
            xsh = [ext[HALO:, :]] + [pltpu.roll(ext, s, 0)[HALO:, :] for s in (1, 2, 3)]
            pre = xsh[0] * wv[3:4, :]
            for s in (1, 2, 3):
                pre = pre + xsh[s] * wv[3 - s:4 - s, :]
            sg = _sigmoid(pre)
            y = pre * sg
            if mode != "v":
                ss = jnp.sum(y * y, axis=-1, keepdims=True) + L2_EPS
                r = lax.rsqrt(ss)
                dy = scale * (dn_e * r - y * (r * r * r) * jnp.sum(dn_e * y, axis=-1, keepdims=True))
            else:
                dy = dn_e
            dpre = dy * (sg * (1.0 + pre * (1.0 - sg)))
            dx = dpre[:RT, :] * wv[3:4, :]
            for s in (1, 2, 3):
                dx = dx + pltpu.roll(dpre, RT + HALO - s, 0)[:RT, :] * wv[3 - s:4 - s, :]
            trow = r0 + _iota((RT, 1), 0)
            dx_ref[pl.ds(r0, RT), :] = jnp.where(trow >= pad, dx, 0.0)
            new = []
            for s in (0, 1, 2, 3):
                new.append(dw[s] + jnp.sum(dpre[:RT, :] * xsh[s][:RT, :], axis=0, keepdims=True))
            return tuple(new)

        z = jnp.zeros((1, HD), F32)
        dw = lax.fori_loop(0, nt, tile, (z, z, z, z))
        for s in (0, 1, 2, 3):
            dw_ref[3 - s:4 - s, :] = dw[s]

    return pl.pallas_call(
        body, name=name, grid=(GDN_H,),
        in_specs=[pl.BlockSpec((t_pad, HD), lambda h: (0, blk0 + h)), pl.BlockSpec((CONV_K, HD), lambda h: (0, h)),
                  pl.BlockSpec((t_pad, HD), lambda h: (0, h))],
        out_specs=[pl.BlockSpec((t_pad, HD), lambda h: (0, h)), pl.BlockSpec((CONV_K, HD), lambda h: (0, h))],
        out_shape=[jax.ShapeDtypeStruct((t_pad, GDN_H * HD), F32), jax.ShapeDtypeStruct((CONV_K, GDN_H * HD), F32)],
        scratch_shapes=[pltpu.VMEM((t_pad + 3 * HALO, HD), F32), pltpu.VMEM((t_pad + HALO, HD), F32)],
        compiler_params=_cparams(),
    )(p, w, dn)


@jax.custom_vjp
def _unit_lower_inv(m, bd, eye):
    md = m * bd
    low = m - md
    p2 = mbnn(md, md)
    p4 = mbnn(p2, p2)
    dinv = mbnn(mbnn(eye - md, eye + p2), eye + p4)
    n = mbnn(dinv, low)
    n2 = mbnn(n, n)
    n4 = mbnn(n2, n2)
    return mbnn(mbnn(mbnn(eye - n, eye + n2), eye + n4), dinv)


def _unit_lower_inv_bwd(res, g):
    t, bd, eye = res
    return -mbtn(t, mbnt(g, t)), jnp.zeros_like(bd), jnp.zeros_like(eye)


def _unit_lower_inv_fwd(m, bd, eye):
    t = _unit_lower_inv(m, bd, eye)
    return t, (t, bd, eye)


_unit_lower_inv.defvjp(_unit_lower_inv_fwd, _unit_lower_inv_bwd)


def _gdn_chunks(chunks, alog, dtb, s):
    nh = chunks[0][0].shape[0]
    ri = _iota((1, CH, CH), 1)
    ci = _iota((1, CH, CH), 2)
    causal = ri >= ci
    strict = ri > ci
    eye = (ri == ci).astype(F32)
    bd = ((ri >> 3) == (ci >> 3)).astype(F32)
    ltri = (_iota((CH, CH), 0) >= _iota((CH, CH), 1)).astype(F32)
    sel = (_iota((nh, 1, HD), 2) == _iota((nh, 1, HD), 0)).astype(F32)
    last = _iota((1, CH, 1), 1) == CH - 1

    beta, gc, gc_rows = [], [], []
    for _, _, _, bb, aa, valid in chunks:
        beta_all = jnp.where(valid, _sigmoid(bb), 0.0)
        g_all = jnp.where(valid, -jnp.exp(alog) * _softplus(aa + dtb), 0.0)
        gc_all = hnn(ltri, g_all)
        beta.append(jnp.sum(beta_all[None] * sel, axis=2, keepdims=True))
        gc.append(jnp.sum(gc_all[None] * sel, axis=2, keepdims=True))
        gc_rows.append(hbnt(jnp.broadcast_to(sel, (nh, CH, HD)), jnp.broadcast_to(gc_all[None], (nh, CH, HD))))
    cat = lambda xs: jnp.concatenate(xs, axis=0)
    q, k, v = (cat([c[j] for c in chunks]) for j in range(3))
    beta, gc, gc_rows = cat(beta), cat(gc), cat(gc_rows)
    gc_last = jnp.sum(jnp.where(last, gc, 0.0), axis=1, keepdims=True)
    decay = jnp.exp(jnp.where(causal, gc - gc_rows, NEG))
    egc = jnp.exp(gc)

    kb = k * beta
    m = jnp.where(strict, bbnt(kb, k) * decay, 0.0)
    t_inv = _unit_lower_inv(m, bd, eye)
    u = bbnn(t_inv, v * beta)
    w = bbnn(t_inv, kb * egc)
    a_intra = bbnt(q, k) * decay
    q_dec = q * egc
    k_dec = k * jnp.exp(gc_last - gc)
    g_tot = jnp.exp(gc_last)

    outs = []
    for n in range(len(chunks)):
        part = lambda a: a[n * nh:(n + 1) * nh]
        v_new = part(u) - bbnn(part(w), s)
        outs.append(bbnn(part(q_dec), s) + bbnn(part(a_intra), v_new))
        s = s * part(g_tot) + bbtn(part(k_dec), v_new)
    return outs, s


PAIR = 2 * CH


def _gdn_specs(npair, rev):
    cc = (lambda c: npair - 1 - c) if rev else (lambda c: c)
    wide = pl.BlockSpec((PAIR, GDN_H * HD), lambda c: (cc(c), 0))
    fix = lambda off: pl.BlockSpec((PAIR, HD), lambda c: (cc(c), off))
    par = pl.BlockSpec((1, HD), lambda c: (0, 0))
    state = pl.BlockSpec((1, GDN_H, HD, HD), lambda c: (cc(c), 0, 0, 0))
    return wide, fix, par, state


def _store_heads(ref, a, rows=slice(None)):
    for h in range(a.shape[0]):
        ref[rows, h * HD:(h + 1) * HD] = a[h]


def _chunk_rows(half):
    return slice(half * CH, (half + 1) * CH)


def _chunk_valid(pair, half, pad):
    return ((2 * pair + half) * CH + _iota((CH, 1), 0)) >= pad


def _gdn_fwd(qn, kn, vn, p, alog, dtb, pad, name, cargo=(), exchange=None):
    t_pad = qn.shape[0]
    npair = t_pad // PAIR
    wide, fix, par, state = _gdn_specs(npair, False)
    n = len(cargo)

    def body(q_ref, k_ref, v_ref, bb_ref, aa_ref, al_ref, dt_ref, *rest):
        c = pl.program_id(0)
        s_ref = rest[-1]
        (o_ref, ss_ref), end_cargo = _cargo_bounds(rest[:-1], n, 2, exchange, c == 0, c == npair - 1)

        @pl.when(c == 0)
        def _():
            s_ref[...] = jnp.zeros_like(s_ref)

        s = s_ref[...]
        ss_ref[0] = s
        rows = [_chunk_rows(half) for half in (0, 1)]
        chunks = [(_heads(q_ref[r, :], GDN_H), _heads(k_ref[r, :], GDN_H), _heads(v_ref[r, :], GDN_H),
                   bb_ref[r, :], aa_ref[r, :], _chunk_valid(c, half, pad)) for half, r in enumerate(rows)]
        outs, s = _gdn_chunks(chunks, al_ref[...], dt_ref[...], s)
        for r, o in zip(rows, outs):
            _store_heads(o_ref, o, r)
        s_ref[...] = s
        end_cargo()

    return pl.pallas_call(
        body, name=name, grid=(npair,),
        in_specs=[wide, wide, wide, fix(16), fix(17), par, par] + [ANY] * n,
        out_specs=[wide, state] + [ANY] * n,
        out_shape=[jax.ShapeDtypeStruct((t_pad, GDN_H * HD), F32), jax.ShapeDtypeStruct((npair, GDN_H, HD, HD), F32)]
        + (exchange[1](cargo) if n else []),
        scratch_shapes=(exchange[2](n) if n else []) + [pltpu.VMEM((GDN_H, HD, HD), F32)],
        compiler_params=_cparams(),
    )(qn, kn, vn, p, p, alog, dtb, *cargo)


def _gdn_bwd(qn, kn, vn, p, alog, dtb, ssave, do, pad, name, cargo=(), exchange=None):
    t_pad = qn.shape[0]
    npair = t_pad // PAIR
    wide, fix, par, state = _gdn_specs(npair, True)
    n = len(cargo)

    def body(q_ref, k_ref, v_ref, bb_ref, aa_ref, al_ref, dt_ref, ss_ref, do_ref, *rest):
        c = pl.program_id(0)
        ds_ref = rest[-1]
        (dq_ref, dk_ref, dv_ref, dbb_ref, daa_ref, dal_ref, ddt_ref), end_cargo = _cargo_bounds(
            rest[:-1], n, 7, exchange, c == 0, c == npair - 1)

        @pl.when(c == 0)
        def _():
            ds_ref[...] = jnp.zeros_like(ds_ref)
            dal_ref[...] = jnp.zeros_like(dal_ref)
            ddt_ref[...] = jnp.zeros_like(ddt_ref)

        ra, rb = _chunk_rows(0), _chunk_rows(1)
        va, vb = _chunk_valid(npair - 1 - c, 0, pad), _chunk_valid(npair - 1 - c, 1, pad)

        def pair(qa, ka, va_, ba, aa, qb, kb, vb_, bb, ab, al, dt, s):
            (oa, ob), s = _gdn_chunks([(qa, ka, va_, ba, aa, va), (qb, kb, vb_, bb, ab, vb)], al, dt, s)
            return oa, ob, s

        ins = [f(ref[r, :]) for r in (ra, rb)
               for ref, f in ((q_ref, lambda a: _heads(a, GDN_H)), (k_ref, lambda a: _heads(a, GDN_H)),
                              (v_ref, lambda a: _heads(a, GDN_H)), (bb_ref, lambda a: a), (aa_ref, lambda a: a))]
        _, vjp = jax.vjp(pair, *ins, al_ref[...], dt_ref[...], ss_ref[0])
        g = vjp((_heads(do_ref[ra, :], GDN_H), _heads(do_ref[rb, :], GDN_H), ds_ref[...]))
        for r, (dq, dk, dv, dbb, daa) in ((ra, g[0:5]), (rb, g[5:10])):
            _store_heads(dq_ref, dq, r)
            _store_heads(dk_ref, dk, r)
            _store_heads(dv_ref, dv, r)
            dbb_ref[r, :] = dbb
            daa_ref[r, :] = daa
        dal_ref[...] += g[10]
        ddt_ref[...] += g[11]
        ds_ref[...] = g[12]
        end_cargo()

    sds = jax.ShapeDtypeStruct
    return pl.pallas_call(
        body, name=name, grid=(npair,),
        in_specs=[wide, wide, wide, fix(16), fix(17), par, par, state, wide] + [ANY] * n,
        out_specs=[wide, wide, wide, fix(0), fix(0), par, par] + [ANY] * n,
        out_shape=[sds((t_pad, GDN_H * HD), F32)] * 3 + [sds((t_pad, HD), F32)] * 2 + [sds((1, HD), F32)] * 2
        + (exchange[1](cargo) if n else []),
        scratch_shapes=(exchange[2](n) if n else []) + [pltpu.VMEM((GDN_H, HD, HD), F32)],
        compiler_params=_cparams(),
    )(qn, kn, vn, p, p, alog, dtb, ssave, do, *cargo)


SB_Q0, SB_K0, SB_V0 = 18, 22, 26
SB_SCALE = SB_DH ** -0.5
SB_NB_FWD, SB_NB_BWD = 11, 8


def _sb_terms(z, allowed):
    nz = -z
    raw = jnp.minimum(nz, 0.0) - jnp.log(1.0 + jnp.exp(jnp.minimum(z, nz)))
    l1m = raw if allowed is None else jnp.where(allowed, raw, 0.0)
    ls = z + raw
    return l1m, ls, jnp.exp(ls)


def _sb_passes(i, step, carry, per):
    total = i + 1

    def sized(done, first):
        return [functools.partial(step, done, masked=({0} if first else set()) | {nb - 1}, nb=nb)
                for nb in range(1, per + 1)]

    def several(c):
        n_mid = (total - per - 1) // per
        c = step(0, c, masked={0}, nb=per)
        c = lax.fori_loop(0, n_mid, lambda t, cc: step(per * (1 + t), cc, masked=set(), nb=per), c)
        done = per * (1 + n_mid)
        return lax.switch(total - done - 1, sized(done, False), c)

    return lax.cond(total <= per, lambda c: lax.switch(total - 1, sized(0, True), c), several, carry)


def _sb_stack(a, i):
    first = _iota((1, HD), 1) < SB_DH
    a2 = jnp.concatenate([jnp.where(first, a, 0.0), jnp.where(first, 0.0, a)], axis=0).astype(BF16)
    rq = i * QB + _iota((QB, 1), 0)
    return a2, jnp.concatenate([rq, rq], axis=0), first


def _hi_lo(a):
    hi = a.astype(BF16)
    lo = (a - hi.astype(F32)).astype(BF16)
    return jnp.concatenate([hi, lo], axis=1)


def _cargo_bounds(refs, n, n_out, exchange, first, last):
    outs = refs[n:n + n_out]
    if not n:
        return outs, lambda: None
    ex = exchange[0](refs[:n], refs[n + n_out:2 * n + n_out], *refs[2 * n + n_out:])

    @pl.when(first)
    def _():
        ex.start()

    def finish():
        @pl.when(last)
        def _():
            ex.wait()

    return outs, finish


def _sb_fwd(p, pad, name, cargo=(), exchange=None):
    t_pad = p.shape[0]
    nq = t_pad // QB
    n = len(cargo)

    def body(q_ref, k_ref, v_ref, *rest):
        i = pl.program_id(1)
        pr = pl.program_id(0)
        (o_ref, r_ref), end_cargo = _cargo_bounds(rest, n, 2, exchange, (pr == 0) & (i == 0),
                                                  (pr == SB_H // 2 - 1) & (i == nq - 1))
        q2, rowq, first = _sb_stack(q_ref[...] * SB_SCALE, i)
        tri = (_iota((QB, QB), 0) > _iota((QB, QB), 1)).astype(BF16)
        upper2 = jnp.concatenate([jnp.concatenate([tri, tri], axis=0), jnp.ones((2 * QB, QB), BF16)], axis=1)

        def chain(kb, masked):
            start = pl.multiple_of(kb * QB, QB)
            kblk = k_ref[pl.ds(start, QB), :].astype(BF16)
            vblk = v_ref[pl.ds(start, QB), :].astype(BF16)
            z = lax.dot_general(q2, kblk, (NT, ((), ())), preferred_element_type=F32)
            colk = kb * QB + _iota((1, QB), 1)
            al = ((colk < rowq) & (colk >= pad)) if masked else None
            l1m, ls, _ = _sb_terms(z, al)
            sums = lax.dot_general(_hi_lo(l1m), upper2, (NN, ((), ())), preferred_element_type=F32)
            return al, ls, sums[:, :QB], sums[:, QB:], vblk

        def step(done, carry, masked, nb):
            o_acc, run = carry
            ws, vs = [], []
            for n in range(nb):
                al, ls, suf, rs, vblk = chain(i - done - n, n in masked)
                wgt = jnp.exp(ls + suf + run)
                ws.append((wgt if al is None else jnp.where(al, wgt, 0.0)).astype(BF16))
                vs.append(vblk)
                run = run + rs
            o_acc = o_acc + lax.dot_general(jnp.concatenate(ws, axis=1), jnp.concatenate(vs, axis=0),
                                            (NN, ((), ())), preferred_element_type=F32)
            return o_acc, run

        o_acc, run = _sb_passes(i, step, (jnp.zeros((2 * QB, HD), F32), jnp.zeros((2 * QB, QB), F32)), SB_NB_FWD)
        o_ref[...] = jnp.where(first, o_acc[:QB], o_acc[QB:]).astype(BF16)
        r_ref[...] = jnp.where(first, run[:QB], run[QB:])
        end_cargo()

    full = lambda off: pl.BlockSpec((t_pad, HD), lambda pr, i: (0, off + pr))
    blk = pl.BlockSpec((QB, HD), lambda pr, i: (i, pr))
    return pl.pallas_call(
        body, name=name, grid=(SB_H // 2, nq),
        in_specs=[pl.BlockSpec((QB, HD), lambda pr, i: (i, SB_Q0 + pr)), full(SB_K0), full(SB_V0)] + [ANY] * n,
        out_specs=[blk, blk] + [ANY] * n,
        out_shape=[jax.ShapeDtypeStruct((t_pad, SB_H * SB_DH), BF16), jax.ShapeDtypeStruct((t_pad, SB_H * SB_DH), F32)]
        + (exchange[1](cargo) if n else []),
        scratch_shapes=exchange[2](n) if n else [],
        compiler_params=_cparams(),
    )(p, p, p, *cargo)


def _sb_bwd(p, rtot, dy, dy_blk0, pad, name, cargo=(), exchange=None):
    t_pad = p.shape[0]
    nq = t_pad // QB
    n = len(cargo)

    def body(q_ref, k_ref, v_ref, r_ref, do_ref, *rest):
        i = pl.program_id(1)
        pr = pl.program_id(0)
        dkt_ref, dvt_ref = rest[-2:]
        (dq_ref, dk_ref, dv_ref), end_cargo = _cargo_bounds(rest[:-2], n, 3, exchange, (pr == 0) & (i == 0),
                                                            (pr == SB_H // 2 - 1) & (i == nq - 1))

        @pl.when(i == 0)
        def _():
            dkt_ref[...] = jnp.zeros_like(dkt_ref)
            dvt_ref[...] = jnp.zeros_like(dvt_ref)

        q2, rowq, first = _sb_stack(q_ref[...] * SB_SCALE, i)
        do2, _, _ = _sb_stack(do_ref[...], i)
        q2t = jnp.transpose(q2.astype(F32)).astype(BF16)
        do2t = jnp.transpose(do2.astype(F32)).astype(BF16)
        rt = r_ref[...]
        lane = _iota((1, HD), 1)
        rcol = jnp.concatenate([jnp.sum(jnp.where(lane == 0, rt, 0.0), axis=1, keepdims=True),
                                jnp.sum(jnp.where(lane == SB_DH, rt, 0.0), axis=1, keepdims=True)], axis=0)
        rj = _iota((QB, QB), 0)
        cs = _iota((QB, QB), 1)
        tri_u = (rj > cs).astype(BF16)
        tri_l = (rj < cs).astype(BF16)
        ones2 = jnp.ones((2 * QB, QB), BF16)
        upper2 = jnp.concatenate([jnp.concatenate([tri_u, tri_u], axis=0), ones2], axis=1)
        lower2 = jnp.concatenate([jnp.concatenate([tri_l, tri_l], axis=0), ones2], axis=1)
        rcol = jnp.broadcast_to(rcol, (2 * QB, QB))

        def chain(kb, masked):
            start = pl.multiple_of(kb * QB, QB)
            kblk = k_ref[pl.ds(start, QB), :].astype(BF16)
            vblk = v_ref[pl.ds(start, QB), :].astype(BF16)
            z = lax.dot_general(q2, kblk, (NT, ((), ())), preferred_element_type=F32)
            colk = kb * QB + _iota((1, QB), 1)
            al = ((colk < rowq) & (colk >= pad)) if masked else None
            l1m, ls, sg = _sb_terms(z, al)
            dwgt = lax.dot_general(do2, vblk, (NT, ((), ())), preferred_element_type=F32)
            sums = lax.dot_general(_hi_lo(l1m), upper2, (NN, ((), ())), preferred_element_type=F32)
            return kb, kblk, al, ls, sums[:, :QB], sums[:, QB:], dwgt, sg

        def finish(c, left, gseen):
            kb, kblk, al, ls, suf, rs, dwgt, sg = c
            left = left - rs
            wgt = jnp.exp(ls + suf + left)
            if al is not None:
                wgt = jnp.where(al, wgt, 0.0)
            dl = dwgt * wgt
            sums = lax.dot_general(_hi_lo(dl), lower2, (NN, ((), ())), preferred_element_type=F32)
            gpre = gseen + sums[:, :QB]
            dz = dl - sg * (dl + gpre)
            if al is not None:
                dz = jnp.where(al, dz, 0.0)
            dz = dz.astype(BF16)
            dkt_ref[kb] += lax.dot_general(q2t, dz, (NN, ((), ())), preferred_element_type=F32)
            dvt_ref[kb] += lax.dot_general(do2t, wgt.astype(BF16), (NN, ((), ())), preferred_element_type=F32)
            return dz, left, gseen + sums[:, QB:]

        def step(done, carry, masked, nb):
            dq_acc, left, gseen = carry
            cs_ = [chain(done + n, n in masked) for n in range(nb)]
            dzs = []
            for c in cs_:
                dz, left, gseen = finish(c, left, gseen)
                dzs.append(dz)
            dq_acc = dq_acc + lax.dot_general(jnp.concatenate(dzs, axis=1), jnp.concatenate([c[1] for c in cs_], axis=0),
                                              (NN, ((), ())), preferred_element_type=F32)
            return dq_acc, left, gseen

        dq_acc, _, _ = _sb_passes(i, step, (jnp.zeros((2 * QB, HD), F32), rcol, jnp.zeros((2 * QB, QB), F32)),
                                  SB_NB_BWD)
        dq_ref[...] = jnp.where(first, dq_acc[:QB], dq_acc[QB:]) * SB_SCALE

        @pl.when(i == nq - 1)
        def _():
            for kb in range(nq):
                dk_ref[kb * QB:(kb + 1) * QB, :] = jnp.transpose(dkt_ref[kb])
                dv_ref[kb * QB:(kb + 1) * QB, :] = jnp.transpose(dvt_ref[kb])

        end_cargo()

    full_in = lambda off: pl.BlockSpec((t_pad, HD), lambda pr, i: (0, off + pr))
    full_out = pl.BlockSpec((t_pad, HD), lambda pr, i: (0, pr))
    blk = pl.BlockSpec((QB, HD), lambda pr, i: (i, pr))
    sds = jax.ShapeDtypeStruct((t_pad, SB_H * SB_DH), F32)
    return pl.pallas_call(
        body, name=name, grid=(SB_H // 2, nq),
        in_specs=[pl.BlockSpec((QB, HD), lambda pr, i: (i, SB_Q0 + pr)), full_in(SB_K0), full_in(SB_V0), blk,
                  pl.BlockSpec((QB, HD), lambda pr, i: (i, dy_blk0 + pr))] + [ANY] * n,
        out_specs=[blk, full_out, full_out] + [ANY] * n,
        out_shape=[sds, sds, sds] + (exchange[1](cargo) if n else []),
        scratch_shapes=(exchange[2](n) if n else []) + [pltpu.VMEM((nq, HD, QB), F32)] * 2,
        compiler_params=_cparams(),
    )(p, p, p, rtot, dy, *cargo)


HG_LEVELS = 6


def _hg_prefix_matrix():
    t = np.arange(CH)[:, None]
    j = np.arange(CH)[None, :]
    groups = [(j <= t)]
    for lvl in range(1, HG_LEVELS + 1):
        half = CH >> lvl
        e = (t // (2 * half)) * (2 * half) + half - 1
        groups.append(j <= e)
    groups.append(np.ones((8, CH), bool))
    e = np.concatenate(groups, axis=0).astype(np.float32)
    return np.concatenate([e, e, e], axis=1), np.concatenate([e, e, e], axis=0)


HG_G = 8


def _hg_chunk(qr, fr, iv, r0, r1, st, valid, ecat):
    g = st.shape[0]
    mx = jnp.maximum(r0, r1)
    e0 = jnp.exp(r0 - mx)
    e1 = jnp.exp(r1 - mx)
    lb = e1 / (e0 + e1)
    fg = lb + (1.0 - lb) * _sigmoid(fr)
    logf = jnp.where(valid, jnp.log(fg), 0.0)
    kk = jnp.where(valid, 1.0 - fg, 0.0)
    q = jnp.where(valid, _silu(qr), 0.0)
    v = _heads(jnp.where(valid, iv, 0.0), g)

    pre = _mask_dot(ecat, logf)
    b = pre[0:CH]
    b_last = jnp.max(pre[(HG_LEVELS + 1) * CH:], axis=0, keepdims=True)
    row = _iota((CH, 1), 0)
    ri = _iota((1, CH, CH), 1)
    ci = _iota((1, CH, CH), 2)
    a = jnp.where(ri == ci, jnp.sum(_heads(q * kk, g), axis=2, keepdims=True), 0.0)
    for lvl in range(1, HG_LEVELS + 1):
        half = CH >> lvl
        m = pre[lvl * CH:(lvl + 1) * CH]
        low = (row & half) != 0
        dec = jnp.exp(jnp.where(low, b - m, m - b))
        qt = jnp.where(low, q * dec, 0.0)
        kt = jnp.where(low, 0.0, kk * dec)
        same = (ri >> (7 - lvl)) == (ci >> (7 - lvl))
        a = a + jnp.where(same, bbnt(_heads(qt, g), _heads(kt, g)), 0.0)
    o = bbnt(_heads(q * jnp.exp(b), g), st) + bbnn(a, v)
    kd = kk * jnp.exp(b_last - b)
    st_new = st * _heads(jnp.exp(b_last), g) + bbtn(v, _heads(kd, g))
    return o, st_new


def _hg_specs(npair, rev):
    cc = (lambda c: npair - 1 - c) if rev else (lambda c: c)
    ng = HG_H // HG_G
    blk = lambda off: pl.BlockSpec((PAIR, HG_G * HD), lambda h, c: (cc(c), off * ng + h))
    lbs = pl.BlockSpec((2, HG_G * HD), lambda h, c: (0, h))
    state = pl.BlockSpec((1, HG_G, HD, HD), lambda h, c: (cc(c), h, 0, 0))
    return ng, blk, lbs, state


def _hg_fwd(p, lbraw, ecat, pad, name):
    t_pad = p.shape[0]
    npair = t_pad // PAIR
    ng, blk, lbs, state = _hg_specs(npair, False)

    def body(q_ref, f_ref, i_ref, lb_ref, e_ref, et_ref, o_ref, ss_ref, s_ref):
        c = pl.program_id(1)

        @pl.when(c == 0)
        def _():
            s_ref[...] = jnp.zeros_like(s_ref)

        st = s_ref[...]
        ss_ref[0] = st
        for half in (0, 1):
            r = _chunk_rows(half)
            o, st = _hg_chunk(q_ref[r, :], f_ref[r, :], i_ref[r, :], lb_ref[0:1, :], lb_ref[1:2, :], st,
                              _chunk_valid(c, half, pad), (e_ref[...], et_ref[...]))
            _store_heads(o_ref, o, r)
        s_ref[...] = st

    return pl.pallas_call(
        body, name=name, grid=(ng, npair),
        in_specs=[blk(0), blk(1), blk(2), lbs] + [pl.BlockSpec(e.shape, lambda h, c: (0, 0)) for e in ecat],
        out_specs=[blk(0), state],
        out_shape=[jax.ShapeDtypeStruct((t_pad, HG_H * HD), F32), jax.ShapeDtypeStruct((npair, HG_H, HD, HD), F32)],
        scratch_shapes=[pltpu.VMEM((HG_G, HD, HD), F32)],
        compiler_params=_cparams(),
    )(p, p, p, lbraw, *ecat)


def _hg_bwd(p, lbraw, ecat, ssave, do, pad, name, cargo=(), exchange=None):
    t_pad = p.shape[0]
    npair = t_pad // PAIR
    ng, blk, lbs, state = _hg_specs(npair, True)
    n = len(cargo)

    def body(q_ref, f_ref, i_ref, lb_ref, e_ref, et_ref, ss_ref, do_ref, *rest):
        c = pl.program_id(1)
        hg = pl.program_id(0)
        ds_ref = rest[-1]
        (dq_ref, df_ref, di_ref, dlb_ref), end_cargo = _cargo_bounds(
            rest[:-1], n, 4, exchange, (hg == 0) & (c == 0), (hg == ng - 1) & (c == npair - 1))

        @pl.when(c == 0)
        def _():
            ds_ref[...] = jnp.zeros_like(ds_ref)
            dlb_ref[...] = jnp.zeros_like(dlb_ref)

        ra, rb = _chunk_rows(0), _chunk_rows(1)
        va, vb = _chunk_valid(npair - 1 - c, 0, pad), _chunk_valid(npair - 1 - c, 1, pad)
        ecv = (e_ref[...], et_ref[...])

        def pair(qa, fa, ia, qb, fb, ib, r0, r1, st):
            oa, st = _hg_chunk(qa, fa, ia, r0, r1, st, va, ecv)
            ob, st = _hg_chunk(qb, fb, ib, r0, r1, st, vb, ecv)
            return oa, ob, st

        ins = [ref[r, :] for r in (ra, rb) for ref in (q_ref, f_ref, i_ref)]
        _, vjp = jax.vjp(pair, *ins, lb_ref[0:1, :], lb_ref[1:2, :], ss_ref[0])
        g = vjp((_heads(do_ref[ra, :], HG_G), _heads(do_ref[rb, :], HG_G), ds_ref[...]))
        for r, (dq, df, di) in ((ra, g[0:3]), (rb, g[3:6])):
            dq_ref[r, :] = dq.astype(BF16)
            df_ref[r, :] = df.astype(BF16)
            di_ref[r, :] = di.astype(BF16)
        dlb_ref[0:1, :] += g[6]
        dlb_ref[1:2, :] += g[7]
        ds_ref[...] = g[8]
        end_cargo()

    sds = jax.ShapeDtypeStruct((t_pad, HG_H * HD), BF16)
    return pl.pallas_call(
        body, name=name, grid=(ng, npair),
        in_specs=[blk(0), blk(1), blk(2), lbs] + [pl.BlockSpec(e.shape, lambda h, c: (0, 0)) for e in ecat]
        + [state, blk(0)] + [ANY] * n,
        out_specs=[blk(0), blk(0), blk(0), lbs] + [ANY] * n,
        out_shape=[sds, sds, sds, jax.ShapeDtypeStruct((2, HG_H * HD), F32)] + (exchange[1](cargo) if n else []),
        scratch_shapes=(exchange[2](n) if n else []) + [pltpu.VMEM((HG_G, HD, HD), F32)],
        compiler_params=_cparams(),
    )(p, p, p, lbraw, *ecat, ssave, do, *cargo)


def _pad_ab_cols(w):
    z = jnp.zeros((w.shape[0], HD - GDN_H), w.dtype)
    return jnp.concatenate([w[:, :2048], w[:, 2048:2052], z, w[:, 2052:2056], z, w[:, 2056:]], axis=1)


def _unpad_ab_cols(w):
    return jnp.concatenate([w[:, :2048], w[:, 2048:2052], w[:, 2176:2180], w[:, 2304:]], axis=1)


def _lane_pad(v):
    return jnp.pad(v, ((0, 0), (0, HD - v.shape[1])))


def _mlp_fwd(h, hb, w1, w2, layer, g, b):
    a, r = _mm(hb, w1, b_view=("cols", layer), out_dtype=BF16, act=True, name=f"mlp_up_{layer}")
    m, y, yb = _mm(r, w2, b_view=("rows", layer), ln=(h, g, b), name=f"mlp_down_{layer}")
    return a, r, m, y, yb


def _mlp_bwd(hb, a, r, dmb, w1, w2, layer):
    da = _mm(dmb, w2, tb=True, b_view=("rows", layer), out_dtype=BF16, gate=a, name=f"mlp_down_dx_{layer}")
    dw2 = _mm(r, dmb, ta=True, out_dtype=BF16, name=f"mlp_down_dw_{layer}")
    dh = _mm(da, w1, tb=True, b_view=("cols", layer), name=f"mlp_up_dx_{layer}")
    dw1 = _mm(hb, da, ta=True, out_dtype=BF16, out_split=N_CHIP, name=f"mlp_up_dw_{layer}")
    return dh, dw1, dw2


def _local_step(h0, tgt, w, pad, late=None):
    row = lambda a, i: a[i:i + 1]
    ecat = tuple(jnp.asarray(e, dtype=BF16) for e in _hg_prefix_matrix())
    cw = [w["conv_w"][:, i * 512:(i + 1) * 512] for i in range(3)]
    alog, dtb = _lane_pad(w["a_log"]), _lane_pad(w["dt_bias"])

    h0b = h0.astype(BF16)
    p0 = _mm(h0b, w["ab_w_in"], name="ab_in")
    qn = _conv_fwd(p0, 0, cw[0], "q", pad, "conv_q")
    kn = _conv_fwd(p0, 4, cw[1], "k", pad, "conv_k")
    vn = _conv_fwd(p0, 8, cw[2], "v", pad, "conv_v")
    if late is None:
        oa_raw, ss0 = _gdn_fwd(qn, kn, vn, p0, alog, dtb, pad, "gdn_fwd")
        ob, rtot = _sb_fwd(p0, pad, "sb_fwd")
    else:
        oa_raw, ss0, g_cin, g_cout = _gdn_fwd(qn, kn, vn, p0, alog, dtb, pad, "gdn_fwd",
                                              cargo=[late["c_w_in"], late["c_w_out"]], exchange=GATHER)
        ob, rtot, g_about, g_w1, g_w2 = _sb_fwd(p0, pad, "sb_fwd", exchange=GATHER,
                                                cargo=[late["ab_w_out"], late["mlp_w1"], late["mlp_w2"]])
        w = dict(w, ab_w_out=g_about.reshape(D, D), c_w_in=g_cin, c_w_out=g_cout.reshape(D, D), mlp_w1=g_w1, mlp_w2=g_w2)
    oa = _grms_fwd(oa_raw, p0, 12, w["ab_gnorm_g"], "gdn_gate")
    ycat = jnp.concatenate([oa, ob], axis=1)
    mix0, h1, h1b = _mm(ycat, w["ab_w_out"], name="ab_out", ln=(h0, row(w["ln_mix_g"], 0), row(w["ln_mix_b"], 0)))
    a0, r0, m0, h2, h2b = _mlp_fwd(h1, h1b, w["mlp_w1"], w["mlp_w2"], 0, row(w["ln_ffn_g"], 0), row(w["ln_ffn_b"], 0))
    p1 = _mm(h2b, w["c_w_in"], b_view=("cols", 0), name="c_in")
    oc_raw, ss1 = _hg_fwd(p1, w["c_lb_raw"], ecat, pad, "hg_fwd")
    yc = _grms_fwd(oc_raw, p1, 3 * HG_H, w["c_gnorm_g"], "hg_gate")
    mix1, h3, h3b = _mm(yc, w["c_w_out"], name="c_out", ln=(h2, row(w["ln_mix_g"], 1), row(w["ln_mix_b"], 1)))
    a1, r1, m1, h4, _ = _mlp_fwd(h3, h3b, w["mlp_w1"], w["mlp_w2"], 1, row(w["ln_ffn_g"], 1), row(w["ln_ffn_b"], 1))
    loss, dh4 = _loss_fwd(h4, tgt, pad + N_META, "loss")

    dh3a, dm1b, dfg1, dfb1 = _ln_res_bwd(h3, m1, row(w["ln_ffn_g"], 1), row(w["ln_ffn_b"], 1), [dh4], "ln_ffn_bwd_1")
    dh3b, dw1_1, dw2_1 = _mlp_bwd(h3b, a1, r1, dm1b, w["mlp_w1"], w["mlp_w2"], 1)
    dh2a, dmix1b, dmg1, dmb1 = _ln_res_bwd(h2, mix1, row(w["ln_mix_g"], 1), row(w["ln_mix_b"], 1), [dh3a, dh3b], "ln_mix_bwd_1")
    dyc = _mm(dmix1b, w["c_w_out"], tb=True, name="c_out_dx")
    dwco = _mm(yc, dmix1b, ta=True, out_dtype=BF16, name="c_out_dw")
    doc, dzc, dcg = _grms_bwd(oc_raw, p1, 3 * HG_H, w["c_gnorm_g"], dyc, 0, "hg_gate_bwd")
    landed = {}
    rows4 = lambda a: a.reshape(N_CHIP, -1, D)
    if late is None:
        dq1, df1, di1, dlb = _hg_bwd(p1, w["c_lb_raw"], ecat, ss1, doc, pad, "hg_bwd")
    else:
        dq1, df1, di1, dlb, landed["w1_1"] = _hg_bwd(
            p1, w["c_lb_raw"], ecat, ss1, doc, pad, "hg_bwd", cargo=[dw1_1], exchange=SCATTER)
    dp1 = [dq1, df1, di1, dzc]
    dh2b = _mm_groups_nt(dp1, w["c_w_in"], "c_in_dx")
    dwc = jnp.stack([_mm(h2b, d, ta=True, out_dtype=BF16, name=f"c_in_dw_{i}") for i, d in enumerate(dp1)])
    dh1a, dm0b, dfg0, dfb0 = _ln_res_bwd(h1, m0, row(w["ln_ffn_g"], 0), row(w["ln_ffn_b"], 0), [dh2a, dh2b], "ln_ffn_bwd_0")
    dh1b, dw1_0, dw2_0 = _mlp_bwd(h1b, a0, r0, dm0b, w["mlp_w1"], w["mlp_w2"], 0)
    dh0a, dmix0b, dmg0, dmb0 = _ln_res_bwd(h0, mix0, row(w["ln_mix_g"], 0), row(w["ln_mix_b"], 0), [dh1a, dh1b], "ln_mix_bwd_0")
    dycat = _mm(dmix0b, w["ab_w_out"], tb=True, name="ab_out_dx")
    dwabo = _mm(ycat, dmix0b, ta=True, out_dtype=BF16, name="ab_out_dw")
    doa, dza, dag = _grms_bwd(oa_raw, p0, 12, w["ab_gnorm_g"], dycat, 0, "gdn_gate_bwd")
    if late is None:
        dqn, dkn, dvn, dbb, daa, dal, ddt = _gdn_bwd(qn, kn, vn, p0, alog, dtb, ss0, doa, pad, "gdn_bwd")
        dqb, dkb, dvb = _sb_bwd(p0, rtot, dycat, 4, pad, "sb_bwd")
    else:
        dqn, dkn, dvn, dbb, daa, dal, ddt, landed["c_w_in"] = _gdn_bwd(
            qn, kn, vn, p0, alog, dtb, ss0, doa, pad, "gdn_bwd", cargo=[dwc], exchange=SCATTER)
        (dqb, dkb, dvb, landed["w1_0"], landed["w2_0"], landed["w2_1"], landed["ab_w_out"],
         landed["c_w_out"]) = _sb_bwd(
            p0, rtot, dycat, 4, pad, "sb_bwd",
            cargo=[dw1_0, rows4(dw2_0), rows4(dw2_1), rows4(dwabo), rows4(dwco)], exchange=SCATTER)
    dpq, dcq = _conv_bwd(p0, 0, cw[0], dqn, "q", pad, "conv_q_bwd")
    dpk, dck = _conv_bwd(p0, 4, cw[1], dkn, "k", pad, "conv_k_bwd")
    dpv, dcv = _conv_bwd(p0, 8, cw[2], dvn, "v", pad, "conv_v_bwd")
    dp0 = _assemble_bf16([dpq, dpk, dpv, dza, dbb, daa, dqb, dkb, dvb], "ab_in_dy")
    dwab = _mm(h0b, dp0, ta=True, out_dtype=BF16, name="ab_in_dw")
    if late is None:
        dh0 = _mm(dp0, w["ab_w_in"], tb=True, plus=dh0a, name="ab_in_dx")
    else:
        dab = jnp.transpose(_unpad_ab_cols(dwab).reshape(D, N_CHIP, AB_TRUE // N_CHIP), (1, 0, 2))
        dh0, landed["ab_w_in"] = _mm(dp0, w["ab_w_in"], tb=True, plus=dh0a, name="ab_in_dx", cargo=[dab],
                                     exchange=SCATTER)

    grads = {
        "ab_w_in": dwab, "conv_w": jnp.concatenate([dcq, dck, dcv], axis=1),
        "a_log": dal[:, :GDN_H], "dt_bias": ddt[:, :GDN_H],
        "ab_gnorm_g": dag, "ab_w_out": dwabo, "c_w_in": dwc, "c_lb_raw": dlb, "c_gnorm_g": dcg, "c_w_out": dwco,
        "ln_mix_g": jnp.concatenate([dmg0, dmg1], 0), "ln_mix_b": jnp.concatenate([dmb0, dmb1], 0),
        "w1_0": dw1_0, "w1_1": dw1_1, "w2_0": dw2_0, "w2_1": dw2_1,
        "ln_ffn_g": jnp.concatenate([dfg0, dfg1], 0), "ln_ffn_b": jnp.concatenate([dfb0, dfb1], 0),
        "landed": landed,
    }
    return loss, dh0, grads


MESH = pl.DeviceIdType.MESH
ANY = pl.BlockSpec(memory_space=pl.ANY)
N_CHIP = 4
N_DEV = 8
CHIP_REL = ((1, 0), (0, 1), (1, 1))
DEV_REL = tuple((dx, dy, dc) for dx in (0, 1) for dy in (0, 1) for dc in (0, 1))[1:]

def _pos():
    return lax.axis_index("x"), lax.axis_index("y"), lax.axis_index("c")


def _flip(a, d):
    return a + d - 2 * a * d


class _Exchange:
    def __init__(self, local, sends, recvs):
        self.local, self.sends, self.recvs = local, sends, recvs

    def start(self):
        for cp in self.local + self.sends:
            cp.start()

    def wait(self):
        for cp in self.recvs:
            cp.wait_recv()
        for cp in self.sends:
            cp.wait_send()
        for cp in self.local:
            cp.wait()


def _gather_sems(n):
    return [pltpu.SemaphoreType.DMA((3 * n,)), pltpu.SemaphoreType.DMA((3 * n,)), pltpu.SemaphoreType.DMA((n,))]


def _gather_copies(x_refs, o_refs, send_sems, recv_sems, local_sems):
    n = len(x_refs)
    x, y, c = _pos()
    local = [pltpu.make_async_copy(x_refs[a], o_refs[a].at[2 * x + y], local_sems.at[a]) for a in range(n)]

    def copy(a, k, sending):
        tx, ty = _flip(x, CHIP_REL[k][0]), _flip(y, CHIP_REL[k][1])
        return pltpu.make_async_remote_copy(
            src_ref=x_refs[a], dst_ref=o_refs[a].at[2 * x + y if sending else 2 * tx + ty],
            send_sem=send_sems.at[3 * a + k], recv_sem=recv_sems.at[3 * a + k], device_id=(tx, ty, c), device_id_type=MESH)

    pairs = [(a, k) for a in range(n) for k in range(3)]
    return _Exchange(local, [copy(a, k, True) for a, k in pairs], [copy(a, k, False) for a, k in pairs])


def _gather_shapes(bufs):
    return [jax.ShapeDtypeStruct((N_CHIP,) + b.shape, b.dtype) for b in bufs]


def _chip_allgather(bufs, name):
    n = len(bufs)

    def body(*refs):
        ex = _gather_copies(refs[:n], refs[n:2 * n], *refs[2 * n:])
        ex.start()
        ex.wait()

    return pl.pallas_call(
        body, name=name, in_specs=[ANY] * n, out_specs=[ANY] * n, out_shape=_gather_shapes(bufs),
        scratch_shapes=_gather_sems(n), compiler_params=pltpu.CompilerParams(has_side_effects=True),
    )(*bufs)


def _scatter_sems(n):
    nr = N_DEV - 1
    return [pltpu.SemaphoreType.DMA((nr * n,)), pltpu.SemaphoreType.DMA((nr * n,)), pltpu.SemaphoreType.DMA((n,))]


def _scatter_copies(g_refs, o_refs, send_sems, recv_sems, local_sems):
    n = len(g_refs)
    nr = N_DEV - 1
    x, y, c = _pos()
    me = 4 * x + 2 * y + c
    local = [pltpu.make_async_copy(g_refs[a].at[2 * x + y], o_refs[a].at[me], local_sems.at[a]) for a in range(n)]

    def copy(a, k, sending):
        dx, dy, dc = DEV_REL[k]
        tx, ty, tc = _flip(x, dx), _flip(y, dy), _flip(c, dc)
        return pltpu.make_async_remote_copy(
            src_ref=g_refs[a].at[2 * tx + ty], dst_ref=o_refs[a].at[me if sending else 4 * tx + 2 * ty + tc],
            send_sem=send_sems.at[nr * a + k], recv_sem=recv_sems.at[nr * a + k],
            device_id=(tx, ty, tc), device_id_type=MESH)

    pairs = [(a, k) for a in range(n) for k in range(nr)]
    return _Exchange(local, [copy(a, k, True) for a, k in pairs], [copy(a, k, False) for a, k in pairs])


def _scatter_shapes(gs):
    return [jax.ShapeDtypeStruct((N_DEV,) + g.shape[1:], g.dtype) for g in gs]


GATHER = (_gather_copies, _gather_shapes, _gather_sems)
SCATTER = (_scatter_copies, _scatter_shapes, _scatter_sems)


def _sum_slots(rs, name):
    n, rh, w = rs[0].shape
    tr = _pick(rh, (256, 128, 64, 16))

    def body(*refs):
        o_ref = refs[-1]
        for layer, r_ref in enumerate(refs[:-1]):
            acc = r_ref[0].astype(F32)
            for s in range(1, n):
                acc = acc + r_ref[s].astype(F32)
            o_ref[layer] = acc

    return pl.pallas_call(
        body, name=name, grid=(rh // tr,), in_specs=[pl.BlockSpec((n, tr, w), lambda i: (0, i, 0))] * len(rs),
        out_specs=pl.BlockSpec((len(rs), tr, w), lambda i: (0, i, 0)),
        out_shape=jax.ShapeDtypeStruct((len(rs), rh, w), F32), compiler_params=_cparams(),
    )(*rs)


def _small_allreduce(buf, name):
    r, w = buf.shape

    def body(b_ref, o_ref, land_ref, send_sems, recv_sems):
        x, y, c = _pos()
        me = 4 * x + 2 * y + c
        land_ref[me] = b_ref[...]

        def target(k):
            dx, dy, dc = DEV_REL[k]
            return _flip(x, dx), _flip(y, dy), _flip(c, dc)

        sends = []
        for k in range(N_DEV - 1):
            tx, ty, tc = target(k)
            cp = pltpu.make_async_remote_copy(
                src_ref=b_ref, dst_ref=land_ref.at[me], send_sem=send_sems.at[k], recv_sem=recv_sems.at[k],
                device_id=(tx, ty, tc), device_id_type=MESH)
            cp.start()
            sends.append(cp)
        for k in range(N_DEV - 1):
            tx, ty, tc = target(k)
            pltpu.make_async_remote_copy(
                src_ref=b_ref, dst_ref=land_ref.at[4 * tx + 2 * ty + tc], send_sem=send_sems.at[k],
                recv_sem=recv_sems.at[k], device_id=(tx, ty, tc), device_id_type=MESH).wait_recv()
        for cp in sends:
            cp.wait_send()
        acc = land_ref[0]
        for s in range(1, N_DEV):
            acc = acc + land_ref[s]
        o_ref[...] = acc

    vm = pl.BlockSpec(memory_space=pltpu.VMEM)
    return pl.pallas_call(
        body, name=name, in_specs=[vm], out_specs=vm, out_shape=jax.ShapeDtypeStruct((r, w), F32),
        scratch_shapes=[pltpu.VMEM((N_DEV, r, w), F32), pltpu.SemaphoreType.DMA((N_DEV - 1,)),
                        pltpu.SemaphoreType.DMA((N_DEV - 1,))],
        compiler_params=pltpu.CompilerParams(has_side_effects=True),
    )(buf)


def _adamw(w, g, m, v, name):
    r, c = w.shape
    tr = _pick(r, (256, 128, 64, 8)) if r * c > (1 << 18) else r

    def body(w_ref, g_ref, m_ref, v_ref, d_ref, m2_ref, v2_ref):
        gg = g_ref[...]
        m2 = ADAM_B1 * m_ref[...] + (1.0 - ADAM_B1) * gg
        v2 = ADAM_B2 * v_ref[...] + (1.0 - ADAM_B2) * (gg * gg)
        m_hat = m2 / (1.0 - ADAM_B1 ** ADAM_STEP)
        v_hat = v2 / (1.0 - ADAM_B2 ** ADAM_STEP)
        d_ref[...] = -ADAM_LR * (m_hat / (jnp.sqrt(v_hat) + ADAM_EPS) + ADAM_WD * w_ref[...])
        m2_ref[...] = m2
        v2_ref[...] = v2

    blk = pl.BlockSpec((tr, c), lambda i: (i, 0))
    sds = jax.ShapeDtypeStruct((r, c), F32)
    return pl.pallas_call(body, name=name, grid=(r // tr,), in_specs=[blk] * 4, out_specs=[blk] * 3,
                          out_shape=[sds] * 3, compiler_params=_cparams())(w, g, m, v)


BIG = ("ab_w_in", "ab_w_out", "c_w_in", "c_w_out", "mlp_w1", "mlp_w2")
SMALL = ("ln_mix_g", "ln_mix_b", "ln_ffn_g", "ln_ffn_b", "c_lb_raw", "ab_a_log", "ab_dt_bias", "ab_gnorm_g", "c_gnorm_g")
SMALL_ROWS = 16
CONV_ROWS = 8
CONV_W = 3 * GDN_H * HD


def _conv_to_rows(cw):
    return jnp.pad(cw, ((0, 0), (0, 2 * D - CONV_W))).reshape(CONV_ROWS, D)


def _rows_to_conv(rows):
    return rows.reshape(CONV_K, 2 * D)[:, :CONV_W]


def _pack_small(d):
    rows = [jnp.pad(d[n], ((0, 0), (0, D - d[n].shape[1]))) for n in SMALL]
    buf = jnp.concatenate(rows, axis=0)
    return jnp.pad(buf, ((0, SMALL_ROWS - buf.shape[0]), (0, 0)))


def _unpack_small(buf, like):
    out, r = {}, 0
    for n in SMALL:
        nr, nc = like[n].shape
        out[n] = buf[r:r + nr, :nc]
        r += nr
    return out


def kernel(x, meta_tokens, ab_w_in, ab_conv_w, ab_a_log, ab_dt_bias, ab_gnorm_g, ab_w_out, c_w_in, c_lb_raw, c_gnorm_g, c_w_out, ln_mix_g, ln_mix_b, mlp_w1, mlp_w2, ln_ffn_g, ln_ffn_b, loss_target, m_meta_tokens, m_ab_w_in, m_ab_conv_w, m_ab_a_log, m_ab_dt_bias, m_ab_gnorm_g, m_ab_w_out, m_c_w_in, m_c_lb_raw, m_c_gnorm_g, m_c_w_out, m_ln_mix_g, m_ln_mix_b, m_mlp_w1, m_mlp_w2, m_ln_ffn_g, m_ln_ffn_b, v_meta_tokens, v_ab_w_in, v_ab_conv_w, v_ab_a_log, v_ab_dt_bias, v_ab_gnorm_g, v_ab_w_out, v_c_w_in, v_c_lb_raw, v_c_gnorm_g, v_c_w_out, v_ln_mix_g, v_ln_mix_b, v_mlp_w1, v_mlp_w2, v_ln_ffn_g, v_ln_ffn_b):
    names = ("meta_tokens", "ab_w_in", "ab_conv_w", "ab_a_log", "ab_dt_bias", "ab_gnorm_g", "ab_w_out", "c_w_in",
             "c_lb_raw", "c_gnorm_g", "c_w_out", "ln_mix_g", "ln_mix_b", "mlp_w1", "mlp_w2", "ln_ffn_g", "ln_ffn_b")
    wts = dict(zip(names, (meta_tokens, ab_w_in, ab_conv_w, ab_a_log, ab_dt_bias, ab_gnorm_g, ab_w_out, c_w_in, c_lb_raw,
                           c_gnorm_g, c_w_out, ln_mix_g, ln_mix_b, mlp_w1, mlp_w2, ln_ffn_g, ln_ffn_b)))
    mom_m = dict(zip(names, (m_meta_tokens, m_ab_w_in, m_ab_conv_w, m_ab_a_log, m_ab_dt_bias, m_ab_gnorm_g, m_ab_w_out,
                             m_c_w_in, m_c_lb_raw, m_c_gnorm_g, m_c_w_out, m_ln_mix_g, m_ln_mix_b, m_mlp_w1, m_mlp_w2,
                             m_ln_ffn_g, m_ln_ffn_b)))
    mom_v = dict(zip(names, (v_meta_tokens, v_ab_w_in, v_ab_conv_w, v_ab_a_log, v_ab_dt_bias, v_ab_gnorm_g, v_ab_w_out,
                             v_c_w_in, v_c_lb_raw, v_c_gnorm_g, v_c_w_out, v_ln_mix_g, v_ln_mix_b, v_mlp_w1, v_mlp_w2,
                             v_ln_ffn_g, v_ln_ffn_b)))
    seq = x.shape[1]
    pad = (-(N_META + seq)) % QB
    xi, yi, ci = _pos()
    chip = 2 * xi + yi

    gat_ab_in, = _chip_allgather([ab_w_in[0].astype(BF16)], "gather_weights")
    late = {"ab_w_out": ab_w_out[0].astype(BF16), "c_w_in": c_w_in.astype(BF16), "c_w_out": c_w_out[0].astype(BF16),
            "mlp_w1": mlp_w1.astype(BF16), "mlp_w2": mlp_w2.astype(BF16)}
    mcols, ccols = meta_tokens.shape[1], ab_conv_w.shape[2]
    place = jnp.concatenate([
        lax.dynamic_update_slice(jnp.zeros((N_META, D), F32), 0.5 * meta_tokens, (0, chip * mcols)),
        _conv_to_rows(lax.dynamic_update_slice(jnp.zeros((CONV_K, CONV_W), F32), 0.5 * ab_conv_w[0], (0, chip * ccols)))],
        axis=0)
    placed = _small_allreduce(place, "gather_meta")
    meta_full = placed[:N_META]

    w = {
        "ab_w_in": _pad_ab_cols(jnp.transpose(gat_ab_in, (1, 0, 2)).reshape(D, AB_TRUE)),
        "conv_w": _rows_to_conv(placed[N_META:]), "a_log": ab_a_log, "dt_bias": ab_dt_bias,
        "ab_gnorm_g": ab_gnorm_g, "c_lb_raw": c_lb_raw,
        "c_gnorm_g": c_gnorm_g, "ln_mix_g": ln_mix_g, "ln_mix_b": ln_mix_b, "ln_ffn_g": ln_ffn_g, "ln_ffn_b": ln_ffn_b,
    }

    h0 = jnp.concatenate([jnp.zeros((pad, D), F32), meta_full, x[0]], axis=0)
    tgt = jnp.concatenate([jnp.zeros((pad + N_META, D), F32), loss_target[0]], axis=0)
    loss8, dh0, g = _local_step(h0, tgt, w, pad, late)
    grad_x = dh0[pad + N_META:][None]

    gsmall = {"ln_mix_g": g["ln_mix_g"], "ln_mix_b": g["ln_mix_b"], "ln_ffn_g": g["ln_ffn_g"], "ln_ffn_b": g["ln_ffn_b"],
              "c_lb_raw": g["c_lb_raw"], "ab_a_log": g["a_log"], "ab_dt_bias": g["dt_bias"], "ab_gnorm_g": g["ab_gnorm_g"],
              "c_gnorm_g": g["c_gnorm_g"]}
    packed = _pack_small(gsmall).at[SMALL_ROWS - 1, :loss8.shape[1]].set(loss8[0])
    sbuf = jnp.concatenate([packed, dh0[pad:pad + N_META], _conv_to_rows(g["conv_w"])], axis=0)
    ssum = _small_allreduce(sbuf, "allreduce_small")
    loss = ssum[SMALL_ROWS - 1, 0]
    grads = _unpack_small(ssum[:SMALL_ROWS], wts)
    grads["meta_tokens"] = lax.dynamic_slice(ssum[SMALL_ROWS:SMALL_ROWS + N_META], (0, chip * mcols), (N_META, mcols))
    grads["ab_conv_w"] = lax.dynamic_slice(_rows_to_conv(ssum[SMALL_ROWS + N_META:]), (0, chip * ccols), (CONV_K, ccols))[None]

    landed = g["landed"]
    for n in ("ab_w_in", "ab_w_out", "c_w_in", "c_w_out"):
        grads[n] = _sum_slots([landed[n]], f"grad_sum_{n}")
    grads["mlp_w1"] = _sum_slots([landed["w1_0"], landed["w1_1"]], "grad_sum_mlp_w1")
    grads["mlp_w2"] = _sum_slots([landed["w2_0"], landed["w2_1"]], "grad_sum_mlp_w2")

    delta, new_m, new_v = {}, {}, {}
    for n in ("meta_tokens", "ab_conv_w") + BIG:
        shp = wts[n].shape
        to2 = lambda a: a.reshape(-1, shp[-1])
        d2, m2, v2 = _adamw(to2(wts[n]), to2(grads[n]), to2(mom_m[n]), to2(mom_v[n]), f"adamw_{n}")
        delta[n], new_m[n], new_v[n] = d2.reshape(shp), m2.reshape(shp), v2.reshape(shp)
    d2, m2, v2 = _adamw(_pack_small(wts), ssum[:SMALL_ROWS], _pack_small(mom_m), _pack_small(mom_v), "adamw_small")
    delta.update(_unpack_small(d2, wts))
    new_m.update(_unpack_small(m2, wts))
    new_v.update(_unpack_small(v2, wts))

    return (loss, grad_x, *[grads[n] for n in names], *[delta[n] for n in names], *[new_m[n] for n in names],
            *[new_v[n] for n in names])
```

```python
import functools

import numpy as np
import jax
import jax.numpy as jnp
from jax import lax
from jax.experimental import pallas as pl
from jax.experimental.pallas import tpu as pltpu

F32 = jnp.float32
BF16 = jnp.bfloat16

D = 1024
N_META = 16
DEPTH = 2
GDN_H = 4
SB_H = 8
SB_DH = 64
HG_H = 8
HD = 128
CH = 64
QB = 128
ALPHA = float((2 * DEPTH) ** 0.25)
LN_EPS = 1e-5
RMS_EPS = 1e-6
L2_EPS = 1e-6
NEG = -1e30

ADAM_LR = 0.001
ADAM_B1 = 0.9
ADAM_B2 = 0.999
ADAM_EPS = 1e-08
ADAM_WD = 0.01
ADAM_STEP = 10

AB_TRUE = 3592
V7X_VMEM_BYTES = 64 * 1024 * 1024
VMEM_LIMIT = V7X_VMEM_BYTES - 8 * 1024 * 1024

NN = ((1,), (0,))
NT = ((1,), (1,))
TN = ((0,), (0,))


def _cparams(**kw):
    return pltpu.CompilerParams(vmem_limit_bytes=VMEM_LIMIT, **kw)


def _dg(a, b, dims, mode):
    if mode == "h":
        return lax.dot_general(a, b, dims, precision=lax.Precision.HIGHEST, preferred_element_type=F32)
    if mode == "b":
        return lax.dot_general(a.astype(BF16), b.astype(BF16), dims, preferred_element_type=F32)
    ah, bh = a.astype(BF16), b.astype(BF16)
    al, bl = (a - ah.astype(F32)).astype(BF16), (b - bh.astype(F32)).astype(BF16)
    d = lambda x, y: lax.dot_general(x, y, dims, preferred_element_type=F32)
    return d(ah, bh) + (d(ah, bl) + d(al, bh))


def _make_dots(mode, batched=False):
    if batched:
        nn_d, nt_d, tn_d = (((2,), (1,)), ((0,), (0,))), (((2,), (2,)), ((0,), (0,))), (((1,), (1,)), ((0,), (0,)))
    else:
        nn_d, nt_d, tn_d = (NN, ((), ())), (NT, ((), ())), (TN, ((), ()))

    @jax.custom_vjp
    def nn(a, b):
        return _dg(a, b, nn_d, mode)

    @jax.custom_vjp
    def nt(a, b):
        return _dg(a, b, nt_d, mode)

    @jax.custom_vjp
    def tn(a, b):
        return _dg(a, b, tn_d, mode)

    nn.defvjp(lambda a, b: (nn(a, b), (a, b)), lambda r, g: (nt(g, r[1]), tn(r[0], g)))
    nt.defvjp(lambda a, b: (nt(a, b), (a, b)), lambda r, g: (nn(g, r[1]), tn(g, r[0])))
    tn.defvjp(lambda a, b: (tn(a, b), (a, b)), lambda r, g: (nt(r[1], g), nn(r[0], g)))
    return nn, nt, tn


hnn = _make_dots("h")[0]
bbnn, bbnt, bbtn = _make_dots("b", True)
mbnn, mbnt, mbtn = _make_dots("m", True)
hbnt = _make_dots("h", True)[1]


def _split3(x, axis):
    x1 = x.astype(BF16)
    r1 = x - x1.astype(F32)
    x2 = r1.astype(BF16)
    x3 = (r1 - x2.astype(F32)).astype(BF16)
    return jnp.concatenate([x1, x2, x3], axis=axis)


@jax.custom_vjp
def _mask_dot(e3, x):
    return lax.dot_general(e3[0], _split3(x, 0), (NN, ((), ())), preferred_element_type=F32)


def _mask_dot_bwd(e3, g):
    dx = lax.dot_general(e3[1], _split3(g, 0), (TN, ((), ())), preferred_element_type=F32)
    return (jnp.zeros_like(e3[0]), jnp.zeros_like(e3[1])), dx


_mask_dot.defvjp(lambda e3, x: (_mask_dot(e3, x), e3), _mask_dot_bwd)


def _heads(a, n):
    return jnp.concatenate([a[None, :, h * HD:(h + 1) * HD] for h in range(n)], axis=0)


def _sigmoid(x):
    return jax.nn.sigmoid(x)


def _silu(x):
    return x * jax.nn.sigmoid(x)


def _softplus(x):
    return jnp.maximum(x, 0.0) + jnp.log(1.0 + jnp.exp(-jnp.abs(x)))


def _iota(shape, dim):
    return lax.broadcasted_iota(jnp.int32, shape, dim)


def _pick(n, prefs):
    for p in prefs:
        if n % p == 0:
            return p
    return n


def _mm(a, b, *, ta=False, tb=False, out_dtype=F32, name, b_view=None, out_split=0, act=False, gate=None, plus=None,
        ln=None, cargo=(), exchange=None):
    if ta:
        k_dim, m_dim = a.shape
    else:
        m_dim, k_dim = a.shape
    if b_view is None:
        w_rows, w_cols = b.shape
    else:
        kind, layer = b_view
        nj, _, blk_r, blk_c = b.shape
        w_rows, w_cols = (blk_r, nj * blk_c) if kind == "cols" else (nj * blk_r, blk_c)
    n_dim = w_rows if tb else w_cols
    assert (w_cols if tb else w_rows) == k_dim
    tm = _pick(m_dim, (1024, 1056, 704, 640, 512, 384, 256, 128))
    tn = _pick(n_dim, (1024, 1056, 704, 640, 512, 384, 256, 128))
    tk = _pick(k_dim, (1024, 1056, 704, 512, 384, 256, 128))
    nk = k_dim // tk
    a_spec = pl.BlockSpec((tk, tm), lambda i, j, k: (k, i)) if ta else pl.BlockSpec((tm, tk), lambda i, j, k: (i, k))
    wb = (tn, tk) if tb else (tk, tn)
    w_idx = (lambda i, j, k: (j, k)) if tb else (lambda i, j, k: (k, j))
    if b_view is None:
        b_spec = pl.BlockSpec(wb, w_idx)
    elif kind == "cols":
        per = blk_c // wb[1]
        b_spec = pl.BlockSpec((None, None) + wb,
                              lambda i, j, k: (w_idx(i, j, k)[1] // per, layer, w_idx(i, j, k)[0], w_idx(i, j, k)[1] % per))
    else:
        per = blk_r // wb[0]
        b_spec = pl.BlockSpec((None, None) + wb,
                              lambda i, j, k: (w_idx(i, j, k)[0] // per, layer, w_idx(i, j, k)[0] % per, w_idx(i, j, k)[1]))
    if out_split:
        per_o = (n_dim // out_split) // tn
        out_spec = pl.BlockSpec((None, tm, tn), lambda i, j, k: (j // per_o, i, j % per_o))
        out_sds = jax.ShapeDtypeStruct((out_split, m_dim, n_dim // out_split), out_dtype)
    else:
        out_spec = pl.BlockSpec((tm, tn), lambda i, j, k: (i, j))
        out_sds = jax.ShapeDtypeStruct((m_dim, n_dim), out_dtype)
    dims = (((0 if ta else 1,), (1 if tb else 0,)), ((), ()))
    assert sum(e is not None for e in (gate, plus, ln)) <= 1
    extra = [e for e in (gate, plus) if e is not None] + list(ln or ())
    n_out = 2 if act else 3 if ln else 1
    assert ln is None or (tn == n_dim and not out_split)

    def finish(acc, refs):
        if ln:
            y = _ln_res_fn(refs[0][...], acc, refs[1][...], refs[2][...])
            refs[3][...] = acc
            refs[4][...] = y
            refs[5][...] = y.astype(BF16)
        elif act:
            refs[0][...] = acc.astype(refs[0].dtype)
            r = jnp.maximum(acc, 0.0)
            refs[1][...] = (r * r).astype(refs[1].dtype)
        elif gate is not None:
            refs[1][...] = (acc * (2.0 * jnp.maximum(refs[0][...].astype(F32), 0.0))).astype(refs[1].dtype)
        elif plus is not None:
            refs[1][...] = (refs[0][...] + acc).astype(refs[1].dtype)
        else:
            refs[0][...] = acc.astype(refs[0].dtype)

    grid = (m_dim // tm, n_dim // tn, nk)
    nc = len(cargo)

    def body(a_ref, b_ref, *rest):
        acc_ref = rest[-1]
        ids = [pl.program_id(d) for d in range(3)]
        outs, end_cargo = _cargo_bounds(
            rest[len(extra):-1], nc, n_out, exchange, (ids[0] == 0) & (ids[1] == 0) & (ids[2] == 0),
            (ids[0] == grid[0] - 1) & (ids[1] == grid[1] - 1) & (ids[2] == grid[2] - 1))
        refs = tuple(rest[:len(extra)]) + tuple(outs)
        part = lax.dot_general(a_ref[...], b_ref[...], dims, preferred_element_type=F32)
        if nk == 1:
            finish(part, refs)
        else:
            k = ids[2]

            @pl.when(k == 0)
            def _():
                acc_ref[...] = part

            @pl.when(k > 0)
            def _():
                acc_ref[...] += part

            @pl.when(k == nk - 1)
            def _():
                finish(acc_ref[...], refs)
        end_cargo()

    tile = pl.BlockSpec((tm, tn), lambda i, j, k: (i, j))
    rowv = pl.BlockSpec((1, tn), lambda i, j, k: (0, j))
    out_sdss = [out_sds] * n_out
    if ln:
        out_sdss = [jax.ShapeDtypeStruct((m_dim, n_dim), dt) for dt in (F32, F32, BF16)]
    out = pl.pallas_call(
        body, name=name, grid=grid,
        in_specs=[a_spec, b_spec] + ([tile, rowv, rowv] if ln else [tile] * len(extra)) + [ANY] * nc,
        out_specs=[out_spec] * n_out + [ANY] * nc,
        out_shape=out_sdss + (exchange[1](cargo) if nc else []),
        scratch_shapes=(exchange[2](nc) if nc else []) + [pltpu.VMEM((tm, tn) if nk > 1 else (8, 128), F32)],
        compiler_params=_cparams(dimension_semantics=("arbitrary",) * 3 if nc else ("parallel", "parallel", "arbitrary")),
    )(a, b, *extra, *cargo)
    if nc:
        return out
    return out if (act or ln) else out[0]


def _mm_groups_nt(parts, b, name):
    m_dim, k_dim = parts[0].shape
    ng, _, n_dim, _ = b.shape
    assert len(parts) == ng and b.shape[3] == k_dim
    tm = _pick(m_dim, (1056, 704, 512, 384, 256, 128))

    def body(*refs):
        a_refs, b_ref, o_ref, acc_ref = refs[:ng], refs[ng], refs[ng + 1], refs[ng + 2]
        k = pl.program_id(1)
        for g in range(ng):
            @pl.when(k == g)
            def _(g=g):
                part = lax.dot_general(a_refs[g][...], b_ref[...], (NT, ((), ())), preferred_element_type=F32)
                if g == 0:
                    acc_ref[...] = part
                elif g < ng - 1:
                    acc_ref[...] += part
                else:
                    o_ref[...] = acc_ref[...] + part

    return pl.pallas_call(
        body, name=name, grid=(m_dim // tm, ng),
        in_specs=[pl.BlockSpec((tm, k_dim), lambda i, k: (i, 0))] * ng
        + [pl.BlockSpec((None, None, n_dim, k_dim), lambda i, k: (k, 0, 0, 0))],
        out_specs=pl.BlockSpec((tm, n_dim), lambda i, k: (i, 0)),
        out_shape=jax.ShapeDtypeStruct((m_dim, n_dim), F32),
        scratch_shapes=[pltpu.VMEM((tm, n_dim), F32)],
        compiler_params=_cparams(dimension_semantics=("parallel", "arbitrary")),
    )(*parts, b)


def _row_tile(t_pad, width):
    for tr in (528, 352, 176, 128, 64):
        if t_pad % tr == 0 and tr * width * 4 <= (3 << 19) and tr % 16 == 0:
            return tr
    return 64 if t_pad % 64 == 0 else t_pad


def _ln_res_fn(h, m, g, b):
    x = ALPHA * h + m
    mu = jnp.mean(x, axis=-1, keepdims=True)
    xc = x - mu
    var = jnp.mean(xc * xc, axis=-1, keepdims=True)
    return xc * lax.rsqrt(var + LN_EPS) * g + b


def _ln_res_bwd(h, m, g, b, dys, name):
    t_pad = h.shape[0]
    tr = _row_tile(t_pad, D)
    nd = len(dys)

    def body(h_ref, m_ref, g_ref, b_ref, *rest):
        d_refs, (dh_ref, dm_ref, dg_ref, db_ref) = rest[:nd], rest[nd:]
        _, vjp = jax.vjp(_ln_res_fn, h_ref[...], m_ref[...], g_ref[...], b_ref[...])
        dy = d_refs[0][...]
        for d_ref in d_refs[1:]:
            dy = dy + d_ref[...]
        dh, dm, dg, db = vjp(dy)
        dh_ref[...] = dh
        dm_ref[...] = dm.astype(BF16)

        @pl.when(pl.program_id(0) == 0)
        def _():
            dg_ref[...] = jnp.zeros_like(dg_ref)
            db_ref[...] = jnp.zeros_like(db_ref)

        dg_ref[...] += dg
        db_ref[...] += db

    row = pl.BlockSpec((tr, D), lambda i: (i, 0))
    par = pl.BlockSpec((1, D), lambda i: (0, 0))
    return pl.pallas_call(
        body, name=name, grid=(t_pad // tr,), in_specs=[row, row, par, par] + [row] * nd,
        out_specs=[row, row, par, par],
        out_shape=[jax.ShapeDtypeStruct((t_pad, D), F32), jax.ShapeDtypeStruct((t_pad, D), BF16),
                   jax.ShapeDtypeStruct((1, D), F32), jax.ShapeDtypeStruct((1, D), F32)],
        compiler_params=_cparams(),
    )(h, m, g, b, *dys)


def _grms_fn(o, z, g):
    y = o * lax.rsqrt(jnp.mean(o * o, axis=-1, keepdims=True) + RMS_EPS) * g
    return y * _silu(z)


def _grms_fwd(o, z_arr, z_blk0, g, name):
    t_pad, w = o.shape
    tr = _row_tile(t_pad, w)
    assert (z_blk0 * HD) % w == 0

    def body(o_ref, z_ref, g_ref, y_ref):
        for h in range(w // HD):
            c = slice(h * HD, (h + 1) * HD)
            y_ref[:, c] = _grms_fn(o_ref[:, c], z_ref[:, c], g_ref[...]).astype(BF16)

    return pl.pallas_call(
        body, name=name, grid=(t_pad // tr,),
        in_specs=[pl.BlockSpec((tr, w), lambda i: (i, 0)), pl.BlockSpec((tr, w), lambda i: (i, z_blk0 * HD // w)),
                  pl.BlockSpec((1, HD), lambda i: (0, 0))],
        out_specs=pl.BlockSpec((tr, w), lambda i: (i, 0)),
        out_shape=jax.ShapeDtypeStruct((t_pad, w), BF16), compiler_params=_cparams(),
    )(o, z_arr, g)


def _grms_bwd(o, z_arr, z_blk0, g, dy_arr, dy_blk0, name):
    t_pad, w = o.shape
    tr = _row_tile(t_pad, w)
    assert (z_blk0 * HD) % w == 0 and (dy_blk0 * HD) % w == 0

    def body(o_ref, z_ref, g_ref, dy_ref, do_ref, dz_ref, dg_ref):
        @pl.when(pl.program_id(0) == 0)
        def _():
            dg_ref[...] = jnp.zeros_like(dg_ref)

        for h in range(w // HD):
            c = slice(h * HD, (h + 1) * HD)
            _, vjp = jax.vjp(_grms_fn, o_ref[:, c], z_ref[:, c], g_ref[...])
            do, dz, dg = vjp(dy_ref[:, c])
            do_ref[:, c] = do
            dz_ref[:, c] = dz.astype(BF16)
            dg_ref[...] += dg

    blk = pl.BlockSpec((tr, w), lambda i: (i, 0))
    return pl.pallas_call(
        body, name=name, grid=(t_pad // tr,),
        in_specs=[blk, pl.BlockSpec((tr, w), lambda i: (i, z_blk0 * HD // w)), pl.BlockSpec((1, HD), lambda i: (0, 0)),
                  pl.BlockSpec((tr, w), lambda i: (i, dy_blk0 * HD // w))],
        out_specs=[blk, blk, pl.BlockSpec((1, HD), lambda i: (0, 0))],
        out_shape=[jax.ShapeDtypeStruct((t_pad, w), F32), jax.ShapeDtypeStruct((t_pad, w), BF16),
                   jax.ShapeDtypeStruct((1, HD), F32)],
        compiler_params=_cparams(),
    )(o, z_arr, g, dy_arr)


def _loss_fwd(y, tgt, first_row, name):
    t_pad = y.shape[0]
    tr = _row_tile(t_pad, D)

    def body(y_ref, t_ref, l_ref, dy_ref):
        rows = pl.program_id(0) * tr + _iota((tr, 1), 0)
        err = jnp.where(rows >= first_row, y_ref[...] - t_ref[...], 0.0)
        dy_ref[...] = err * (1.0 / D)

        @pl.when(pl.program_id(0) == 0)
        def _():
            l_ref[...] = jnp.zeros_like(l_ref)

        part = jnp.sum(jnp.sum(err * err, axis=1, keepdims=True), axis=0, keepdims=True)
        l_ref[...] += jnp.broadcast_to(part * (0.5 / D), l_ref.shape)

    row = pl.BlockSpec((tr, D), lambda i: (i, 0))
    return pl.pallas_call(
        body, name=name, grid=(t_pad // tr,), in_specs=[row, row],
        out_specs=[pl.BlockSpec((8, 128), lambda i: (0, 0)), row],
        out_shape=[jax.ShapeDtypeStruct((8, 128), F32), jax.ShapeDtypeStruct((t_pad, D), F32)],
        compiler_params=_cparams(),
    )(y, tgt)


def _assemble_bf16(parts, name):
    t_pad = parts[0].shape[0]
    widths = [p.shape[1] for p in parts]
    total = sum(widths)
    tr = _row_tile(t_pad, total)

    def body(*refs):
        o_ref = refs[-1]
        off = 0
        for ref, w in zip(refs[:-1], widths):
            o_ref[:, off:off + w] = ref[...].astype(BF16)
            off += w

    return pl.pallas_call(
        body, name=name, grid=(t_pad // tr,), in_specs=[pl.BlockSpec((tr, w), lambda i: (i, 0)) for w in widths],
        out_specs=pl.BlockSpec((tr, total), lambda i: (i, 0)),
        out_shape=jax.ShapeDtypeStruct((t_pad, total), BF16), compiler_params=_cparams(),
    )(*parts)


CONV_K = 4
HALO = 8
RT = 384


def _conv_fwd(p, blk0, w, mode, pad, name):
    t_pad = p.shape[0]
    nt = t_pad // RT
    scale = HD ** -0.5 if mode == "q" else 1.0

    def body(x_ref, w_ref, y_ref, xs_ref):
        xs_ref[0:HALO, :] = jnp.zeros((HALO, HD), F32)
        rows = _iota((t_pad, 1), 0)
        xs_ref[HALO:HALO + t_pad, :] = jnp.where(rows >= pad, x_ref[...], 0.0)
        wv = w_ref[...]

        def tile(i, carry):
            r0 = pl.multiple_of(i * RT, RT)
            ext = xs_ref[pl.ds(r0, RT + HALO), :]
            acc = ext[HALO:, :] * wv[3:4, :]
            for s in (1, 2, 3):
                acc = acc + pltpu.roll(ext, s, 0)[HALO:, :] * wv[3 - s:4 - s, :]
            y = _silu(acc)
            if mode != "v":
                y = y * lax.rsqrt(jnp.sum(y * y, axis=-1, keepdims=True) + L2_EPS) * scale
            y_ref[pl.ds(r0, RT), :] = y
            return carry

        lax.fori_loop(0, nt, tile, 0)

    return pl.pallas_call(
        body, name=name, grid=(GDN_H,),
        in_specs=[pl.BlockSpec((t_pad, HD), lambda h: (0, blk0 + h)), pl.BlockSpec((CONV_K, HD), lambda h: (0, h))],
        out_specs=pl.BlockSpec((t_pad, HD), lambda h: (0, h)),
        out_shape=jax.ShapeDtypeStruct((t_pad, GDN_H * HD), F32),
        scratch_shapes=[pltpu.VMEM((t_pad + HALO, HD), F32)],
        compiler_params=_cparams(),
    )(p, w)


def _conv_bwd(p, blk0, w, dn, mode, pad, name):
    t_pad = p.shape[0]
    nt = t_pad // RT
    scale = HD ** -0.5 if mode == "q" else 1.0

    def body(x_ref, w_ref, dn_ref, dx_ref, dw_ref, xs_ref, ds_ref):
        xs_ref[0:HALO, :] = jnp.zeros((HALO, HD), F32)
        xs_ref[HALO + t_pad:HALO + t_pad + 2 * HALO, :] = jnp.zeros((2 * HALO, HD), F32)
        ds_ref[t_pad:t_pad + HALO, :] = jnp.zeros((HALO, HD), F32)
        rows = _iota((t_pad, 1), 0)
        xs_ref[HALO:HALO + t_pad, :] = jnp.where(rows >= pad, x_ref[...], 0.0)
        ds_ref[0:t_pad, :] = dn_ref[...]
        wv = w_ref[...]

        def tile(i, dw):
            r0 = pl.multiple_of(i * RT, RT)
            ext = xs_ref[pl.ds(r0, RT + 2 * HALO), :]
            dn_e = ds_ref[pl.ds(r0, RT + HALO), :]
            xsh = [ext[HALO:, :]] + [pltpu.roll(ext, s, 0)[HALO:, :] for s in (1, 2, 3)]
            pre = xsh[0] * wv[3:4, :]
            for s in (1, 2, 3):
                pre = pre + xsh[s] * wv[3 - s:4 - s, :]
            sg = _sigmoid(pre)
            y = pre * sg
            if mode != "v":
                ss = jnp.sum(y * y, axis=-1, keepdims=True) + L2_EPS
                r = lax.rsqrt(ss)
                dy = scale * (dn_e * r - y * (r * r * r) * jnp.sum(dn_e * y, axis=-1, keepdims=True))
            else:
                dy = dn_e
            dpre = dy * (sg * (1.0 + pre * (1.0 - sg)))
            dx = dpre[:RT, :] * wv[3:4, :]
            for s in (1, 2, 3):
                dx = dx + pltpu.roll(dpre, RT + HALO - s, 0)[:RT, :] * wv[3 - s:4 - s, :]
            trow = r0 + _iota((RT, 1), 0)
            dx_ref[pl.ds(r0, RT), :] = jnp.where(trow >= pad, dx, 0.0)
            new = []
            for s in (0, 1, 2, 3):
                new.append(dw[s] + jnp.sum(dpre[:RT, :] * xsh[s][:RT, :], axis=0, keepdims=True))
            return tuple(new)

        z = jnp.zeros((1, HD), F32)
        dw = lax.fori_loop(0, nt, tile, (z, z, z, z))
        for s in (0, 1, 2, 3):
            dw_ref[3 - s:4 - s, :] = dw[s]

    return pl.pallas_call(
        body, name=name, grid=(GDN_H,),
        in_specs=[pl.BlockSpec((t_pad, HD), lambda h: (0, blk0 + h)), pl.BlockSpec((CONV_K, HD), lambda h: (0, h)),
                  pl.BlockSpec((t_pad, HD), lambda h: (0, h))],
        out_specs=[pl.BlockSpec((t_pad, HD), lambda h: (0, h)), pl.BlockSpec((CONV_K, HD), lambda h: (0, h))],
        out_shape=[jax.ShapeDtypeStruct((t_pad, GDN_H * HD), F32), jax.ShapeDtypeStruct((CONV_K, GDN_H * HD), F32)],
        scratch_shapes=[pltpu.VMEM((t_pad + 3 * HALO, HD), F32), pltpu.VMEM((t_pad + HALO, HD), F32)],
        compiler_params=_cparams(),
    )(p, w, dn)


@jax.custom_vjp
def _unit_lower_inv(m, bd, eye):
    md = m * bd
    low = m - md
    p2 = mbnn(md, md)
    p4 = mbnn(p2, p2)
    dinv = mbnn(mbnn(eye - md, eye + p2), eye + p4)
    n = mbnn(dinv, low)
    n2 = mbnn(n, n)
    n4 = mbnn(n2, n2)
    return mbnn(mbnn(mbnn(eye - n, eye + n2), eye + n4), dinv)


def _unit_lower_inv_bwd(res, g):
    t, bd, eye = res
    return -mbtn(t, mbnt(g, t)), jnp.zeros_like(bd), jnp.zeros_like(eye)


def _unit_lower_inv_fwd(m, bd, eye):
    t = _unit_lower_inv(m, bd, eye)
    return t, (t, bd, eye)


_unit_lower_inv.defvjp(_unit_lower_inv_fwd, _unit_lower_inv_bwd)


def _gdn_chunks(chunks, alog, dtb, s):
    nh = chunks[0][0].shape[0]
    ri = _iota((1, CH, CH), 1)
    ci = _iota((1, CH, CH), 2)
    causal = ri >= ci
    strict = ri > ci
    eye = (ri == ci).astype(F32)
    bd = ((ri >> 3) == (ci >> 3)).astype(F32)
    ltri = (_iota((CH, CH), 0) >= _iota((CH, CH), 1)).astype(F32)
    sel = (_iota((nh, 1, HD), 2) == _iota((nh, 1, HD), 0)).astype(F32)
    last = _iota((1, CH, 1), 1) == CH - 1

    beta, gc, gc_rows = [], [], []
    for _, _, _, bb, aa, valid in chunks:
        beta_all = jnp.where(valid, _sigmoid(bb), 0.0)
        g_all = jnp.where(valid, -jnp.exp(alog) * _softplus(aa + dtb), 0.0)
        gc_all = hnn(ltri, g_all)
        beta.append(jnp.sum(beta_all[None] * sel, axis=2, keepdims=True))
        gc.append(jnp.sum(gc_all[None] * sel, axis=2, keepdims=True))
        gc_rows.append(hbnt(jnp.broadcast_to(sel, (nh, CH, HD)), jnp.broadcast_to(gc_all[None], (nh, CH, HD))))
    cat = lambda xs: jnp.concatenate(xs, axis=0)
    q, k, v = (cat([c[j] for c in chunks]) for j in range(3))
    beta, gc, gc_rows = cat(beta), cat(gc), cat(gc_rows)
    gc_last = jnp.sum(jnp.where(last, gc, 0.0), axis=1, keepdims=True)
    decay = jnp.exp(jnp.where(causal, gc - gc_rows, NEG))
    egc = jnp.exp(gc)

    kb = k * beta
    m = jnp.where(strict, bbnt(kb, k) * decay, 0.0)
    t_inv = _unit_lower_inv(m, bd, eye)
    u = bbnn(t_inv, v * beta)
    w = bbnn(t_inv, kb * egc)
    a_intra = bbnt(q, k) * decay
    q_dec = q * egc
    k_dec = k * jnp.exp(gc_last - gc)
    g_tot = jnp.exp(gc_last)

    outs = []
    for n in range(len(chunks)):
        part = lambda a: a[n * nh:(n + 1) * nh]
        v_new = part(u) - bbnn(part(w), s)
        outs.append(bbnn(part(q_dec), s) + bbnn(part(a_intra), v_new))
        s = s * part(g_tot) + bbtn(part(k_dec), v_new)
    return outs, s


PAIR = 2 * CH


def _gdn_specs(npair, rev):
    cc = (lambda c: npair - 1 - c) if rev else (lambda c: c)
    wide = pl.BlockSpec((PAIR, GDN_H * HD), lambda c: (cc(c), 0))
    fix = lambda off: pl.BlockSpec((PAIR, HD), lambda c: (cc(c), off))
    par = pl.BlockSpec((1, HD), lambda c: (0, 0))
    state = pl.BlockSpec((1, GDN_H, HD, HD), lambda c: (cc(c), 0, 0, 0))
    return wide, fix, par, state


def _store_heads(ref, a, rows=slice(None)):
    for h in range(a.shape[0]):
        ref[rows, h * HD:(h + 1) * HD] = a[h]


def _chunk_rows(half):
    return slice(half * CH, (half + 1) * CH)


def _chunk_valid(pair, half, pad):
    return ((2 * pair + half) * CH + _iota((CH, 1), 0)) >= pad


def _gdn_fwd(qn, kn, vn, p, alog, dtb, pad, name, cargo=(), exchange=None):
    t_pad = qn.shape[0]
    npair = t_pad // PAIR
    wide, fix, par, state = _gdn_specs(npair, False)
    n = len(cargo)

    def body(q_ref, k_ref, v_ref, bb_ref, aa_ref, al_ref, dt_ref, *rest):
        c = pl.program_id(0)
        s_ref = rest[-1]
        (o_ref, ss_ref), end_cargo = _cargo_bounds(rest[:-1], n, 2, exchange, c == 0, c == npair - 1)

        @pl.when(c == 0)
        def _():
            s_ref[...] = jnp.zeros_like(s_ref)

        s = s_ref[...]
        ss_ref[0] = s
        rows = [_chunk_rows(half) for half in (0, 1)]
        chunks = [(_heads(q_ref[r, :], GDN_H), _heads(k_ref[r, :], GDN_H), _heads(v_ref[r, :], GDN_H),
                   bb_ref[r, :], aa_ref[r, :], _chunk_valid(c, half, pad)) for half, r in enumerate(rows)]
        outs, s = _gdn_chunks(chunks, al_ref[...], dt_ref[...], s)
        for r, o in zip(rows, outs):
            _store_heads(o_ref, o, r)
        s_ref[...] = s
        end_cargo()

    return pl.pallas_call(
        body, name=name, grid=(npair,),
        in_specs=[wide, wide, wide, fix(16), fix(17), par, par] + [ANY] * n,
        out_specs=[wide, state] + [ANY] * n,
        out_shape=[jax.ShapeDtypeStruct((t_pad, GDN_H * HD), F32), jax.ShapeDtypeStruct((npair, GDN_H, HD, HD), F32)]
        + (exchange[1](cargo) if n else []),
        scratch_shapes=(exchange[2](n) if n else []) + [pltpu.VMEM((GDN_H, HD, HD), F32)],
        compiler_params=_cparams(),
    )(qn, kn, vn, p, p, alog, dtb, *cargo)


def _gdn_bwd(qn, kn, vn, p, alog, dtb, ssave, do, pad, name, cargo=(), exchange=None):
    t_pad = qn.shape[0]
    npair = t_pad // PAIR
    wide, fix, par, state = _gdn_specs(npair, True)
    n = len(cargo)

    def body(q_ref, k_ref, v_ref, bb_ref, aa_ref, al_ref, dt_ref, ss_ref, do_ref, *rest):
        c = pl.program_id(0)
        ds_ref = rest[-1]
        (dq_ref, dk_ref, dv_ref, dbb_ref, daa_ref, dal_ref, ddt_ref), end_cargo = _cargo_bounds(
            rest[:-1], n, 7, exchange, c == 0, c == npair - 1)

        @pl.when(c == 0)
        def _():
            ds_ref[...] = jnp.zeros_like(ds_ref)
            dal_ref[...] = jnp.zeros_like(dal_ref)
            ddt_ref[...] = jnp.zeros_like(ddt_ref)

        ra, rb = _chunk_rows(0), _chunk_rows(1)
        va, vb = _chunk_valid(npair - 1 - c, 0, pad), _chunk_valid(npair - 1 - c, 1, pad)

        def pair(qa, ka, va_, ba, aa, qb, kb, vb_, bb, ab, al, dt, s):
            (oa, ob), s = _gdn_chunks([(qa, ka, va_, ba, aa, va), (qb, kb, vb_, bb, ab, vb)], al, dt, s)
            return oa, ob, s

        ins = [f(ref[r, :]) for r in (ra, rb)
               for ref, f in ((q_ref, lambda a: _heads(a, GDN_H)), (k_ref, lambda a: _heads(a, GDN_H)),
                              (v_ref, lambda a: _heads(a, GDN_H)), (bb_ref, lambda a: a), (aa_ref, lambda a: a))]
        _, vjp = jax.vjp(pair, *ins, al_ref[...], dt_ref[...], ss_ref[0])
        g = vjp((_heads(do_ref[ra, :], GDN_H), _heads(do_ref[rb, :], GDN_H), ds_ref[...]))
        for r, (dq, dk, dv, dbb, daa) in ((ra, g[0:5]), (rb, g[5:10])):
            _store_heads(dq_ref, dq, r)
            _store_heads(dk_ref, dk, r)
            _store_heads(dv_ref, dv, r)
            dbb_ref[r, :] = dbb
            daa_ref[r, :] = daa
        dal_ref[...] += g[10]
        ddt_ref[...] += g[11]
        ds_ref[...] = g[12]
        end_cargo()

    sds = jax.ShapeDtypeStruct
    return pl.pallas_call(
        body, name=name, grid=(npair,),
        in_specs=[wide, wide, wide, fix(16), fix(17), par, par, state, wide] + [ANY] * n,
        out_specs=[wide, wide, wide, fix(0), fix(0), par, par] + [ANY] * n,
        out_shape=[sds((t_pad, GDN_H * HD), F32)] * 3 + [sds((t_pad, HD), F32)] * 2 + [sds((1, HD), F32)] * 2
        + (exchange[1](cargo) if n else []),
        scratch_shapes=(exchange[2](n) if n else []) + [pltpu.VMEM((GDN_H, HD, HD), F32)],
        compiler_params=_cparams(),
    )(qn, kn, vn, p, p, alog, dtb, ssave, do, *cargo)


SB_Q0, SB_K0, SB_V0 = 18, 22, 26
SB_SCALE = SB_DH ** -0.5
SB_NB_FWD, SB_NB_BWD = 11, 8


def _sb_terms(z, allowed):
    nz = -z
    raw = jnp.minimum(nz, 0.0) - jnp.log(1.0 + jnp.exp(jnp.minimum(z, nz)))
    l1m = raw if allowed is None else jnp.where(allowed, raw, 0.0)
    ls = z + raw
    return l1m, ls, jnp.exp(ls)


def _sb_passes(i, step, carry, per):
    total = i + 1

    def sized(done, first):
        return [functools.partial(step, done, masked=({0} if first else set()) | {nb - 1}, nb=nb)
                for nb in range(1, per + 1)]

    def several(c):
        n_mid = (total - per - 1) // per
        c = step(0, c, masked={0}, nb=per)
        c = lax.fori_loop(0, n_mid, lambda t, cc: step(per * (1 + t), cc, masked=set(), nb=per), c)
        done = per * (1 + n_mid)
        return lax.switch(total - done - 1, sized(done, False), c)

    return lax.cond(total <= per, lambda c: lax.switch(total - 1, sized(0, True), c), several, carry)


def _sb_stack(a, i):
    first = _iota((1, HD), 1) < SB_DH
    a2 = jnp.concatenate([jnp.where(first, a, 0.0), jnp.where(first, 0.0, a)], axis=0).astype(BF16)
    rq = i * QB + _iota((QB, 1), 0)
    return a2, jnp.concatenate([rq, rq], axis=0), first


def _hi_lo(a):
    hi = a.astype(BF16)
    lo = (a - hi.astype(F32)).astype(BF16)
    return jnp.concatenate([hi, lo], axis=1)


def _cargo_bounds(refs, n, n_out, exchange, first, last):
    outs = refs[n:n + n_out]
    if not n:
        return outs, lambda: None
    ex = exchange[0](refs[:n], refs[n + n_out:2 * n + n_out], *refs[2 * n + n_out:])

    @pl.when(first)
    def _():
        ex.start()

    def finish():
        @pl.when(last)
        def _():
            ex.wait()

    return outs, finish


def _sb_fwd(p, pad, name, cargo=(), exchange=None):
    t_pad = p.shape[0]
    nq = t_pad // QB
    n = len(cargo)

    def body(q_ref, k_ref, v_ref, *rest):
        i = pl.program_id(1)
        pr = pl.program_id(0)
        (o_ref, r_ref), end_cargo = _cargo_bounds(rest, n, 2, exchange, (pr == 0) & (i == 0),
                                                  (pr == SB_H // 2 - 1) & (i == nq - 1))
        q2, rowq, first = _sb_stack(q_ref[...] * SB_SCALE, i)
        tri = (_iota((QB, QB), 0) > _iota((QB, QB), 1)).astype(BF16)
        upper2 = jnp.concatenate([jnp.concatenate([tri, tri], axis=0), jnp.ones((2 * QB, QB), BF16)], axis=1)

        def chain(kb, masked):
            start = pl.multiple_of(kb * QB, QB)
            kblk = k_ref[pl.ds(start, QB), :].astype(BF16)
            vblk = v_ref[pl.ds(start, QB), :].astype(BF16)
            z = lax.dot_general(q2, kblk, (NT, ((), ())), preferred_element_type=F32)
            colk = kb * QB + _iota((1, QB), 1)
            al = ((colk < rowq) & (colk >= pad)) if masked else None
            l1m, ls, _ = _sb_terms(z, al)
            sums = lax.dot_general(_hi_lo(l1m), upper2, (NN, ((), ())), preferred_element_type=F32)
            return al, ls, sums[:, :QB], sums[:, QB:], vblk

        def step(done, carry, masked, nb):
            o_acc, run = carry
            ws, vs = [], []
            for n in range(nb):
                al, ls, suf, rs, vblk = chain(i - done - n, n in masked)
                wgt = jnp.exp(ls + suf + run)
                ws.append((wgt if al is None else jnp.where(al, wgt, 0.0)).astype(BF16))
                vs.append(vblk)
                run = run + rs
            o_acc = o_acc + lax.dot_general(jnp.concatenate(ws, axis=1), jnp.concatenate(vs, axis=0),
                                            (NN, ((), ())), preferred_element_type=F32)
            return o_acc, run

        o_acc, run = _sb_passes(i, step, (jnp.zeros((2 * QB, HD), F32), jnp.zeros((2 * QB, QB), F32)), SB_NB_FWD)
        o_ref[...] = jnp.where(first, o_acc[:QB], o_acc[QB:]).astype(BF16)
        r_ref[...] = jnp.where(first, run[:QB], run[QB:])
        end_cargo()

    full = lambda off: pl.BlockSpec((t_pad, HD), lambda pr, i: (0, off + pr))
    blk = pl.BlockSpec((QB, HD), lambda pr, i: (i, pr))
    return pl.pallas_call(
        body, name=name, grid=(SB_H // 2, nq),
        in_specs=[pl.BlockSpec((QB, HD), lambda pr, i: (i, SB_Q0 + pr)), full(SB_K0), full(SB_V0)] + [ANY] * n,
        out_specs=[blk, blk] + [ANY] * n,
        out_shape=[jax.ShapeDtypeStruct((t_pad, SB_H * SB_DH), BF16), jax.ShapeDtypeStruct((t_pad, SB_H * SB_DH), F32)]
        + (exchange[1](cargo) if n else []),
        scratch_shapes=exchange[2](n) if n else [],
        compiler_params=_cparams(),
    )(p, p, p, *cargo)


def _sb_bwd(p, rtot, dy, dy_blk0, pad, name, cargo=(), exchange=None):
    t_pad = p.shape[0]
    nq = t_pad // QB
    n = len(cargo)

    def body(q_ref, k_ref, v_ref, r_ref, do_ref, *rest):
        i = pl.program_id(1)
        pr = pl.program_id(0)
        dkt_ref, dvt_ref = rest[-2:]
        (dq_ref, dk_ref, dv_ref), end_cargo = _cargo_bounds(rest[:-2], n, 3, exchange, (pr == 0) & (i == 0),
                                                            (pr == SB_H // 2 - 1) & (i == nq - 1))

        @pl.when(i == 0)
        def _():
            dkt_ref[...] = jnp.zeros_like(dkt_ref)
            dvt_ref[...] = jnp.zeros_like(dvt_ref)

        q2, rowq, first = _sb_stack(q_ref[...] * SB_SCALE, i)
        do2, _, _ = _sb_stack(do_ref[...], i)
        q2t = jnp.transpose(q2.astype(F32)).astype(BF16)
        do2t = jnp.transpose(do2.astype(F32)).astype(BF16)
        rt = r_ref[...]
        lane = _iota((1, HD), 1)
        rcol = jnp.concatenate([jnp.sum(jnp.where(lane == 0, rt, 0.0), axis=1, keepdims=True),
                                jnp.sum(jnp.where(lane == SB_DH, rt, 0.0), axis=1, keepdims=True)], axis=0)
        rj = _iota((QB, QB), 0)
        cs = _iota((QB, QB), 1)
        tri_u = (rj > cs).astype(BF16)
        tri_l = (rj < cs).astype(BF16)
        ones2 = jnp.ones((2 * QB, QB), BF16)
        upper2 = jnp.concatenate([jnp.concatenate([tri_u, tri_u], axis=0), ones2], axis=1)
        lower2 = jnp.concatenate([jnp.concatenate([tri_l, tri_l], axis=0), ones2], axis=1)
        rcol = jnp.broadcast_to(rcol, (2 * QB, QB))

        def chain(kb, masked):
            start = pl.multiple_of(kb * QB, QB)
            kblk = k_ref[pl.ds(start, QB), :].astype(BF16)
            vblk = v_ref[pl.ds(start, QB), :].astype(BF16)
            z = lax.dot_general(q2, kblk, (NT, ((), ())), preferred_element_type=F32)
            colk = kb * QB + _iota((1, QB), 1)
            al = ((colk < rowq) & (colk >= pad)) if masked else None
            l1m, ls, sg = _sb_terms(z, al)
            dwgt = lax.dot_general(do2, vblk, (NT, ((), ())), preferred_element_type=F32)
            sums = lax.dot_general(_hi_lo(l1m), upper2, (NN, ((), ())), preferred_element_type=F32)
            return kb, kblk, al, ls, sums[:, :QB], sums[:, QB:], dwgt, sg

        def finish(c, left, gseen):
            kb, kblk, al, ls, suf, rs, dwgt, sg = c
            left = left - rs
            wgt = jnp.exp(ls + suf + left)
            if al is not None:
                wgt = jnp.where(al, wgt, 0.0)
            dl = dwgt * wgt
            sums = lax.dot_general(_hi_lo(dl), lower2, (NN, ((), ())), preferred_element_type=F32)
            gpre = gseen + sums[:, :QB]
            dz = dl - sg * (dl + gpre)
            if al is not None:
                dz = jnp.where(al, dz, 0.0)
            dz = dz.astype(BF16)
            dkt_ref[kb] += lax.dot_general(q2t, dz, (NN, ((), ())), preferred_element_type=F32)
            dvt_ref[kb] += lax.dot_general(do2t, wgt.astype(BF16), (NN, ((), ())), preferred_element_type=F32)
            return dz, left, gseen + sums[:, QB:]

        def step(done, carry, masked, nb):
            dq_acc, left, gseen = carry
            cs_ = [chain(done + n, n in masked) for n in range(nb)]
            dzs = []
            for c in cs_:
                dz, left, gseen = finish(c, left, gseen)
                dzs.append(dz)
            dq_acc = dq_acc + lax.dot_general(jnp.concatenate(dzs, axis=1), jnp.concatenate([c[1] for c in cs_], axis=0),
                                              (NN, ((), ())), preferred_element_type=F32)
            return dq_acc, left, gseen

        dq_acc, _, _ = _sb_passes(i, step, (jnp.zeros((2 * QB, HD), F32), rcol, jnp.zeros((2 * QB, QB), F32)),
                                  SB_NB_BWD)
        dq_ref[...] = jnp.where(first, dq_acc[:QB], dq_acc[QB:]) * SB_SCALE

        @pl.when(i == nq - 1)
        def _():
            for kb in range(nq):
                dk_ref[kb * QB:(kb + 1) * QB, :] = jnp.transpose(dkt_ref[kb])
                dv_ref[kb * QB:(kb + 1) * QB, :] = jnp.transpose(dvt_ref[kb])

        end_cargo()

    full_in = lambda off: pl.BlockSpec((t_pad, HD), lambda pr, i: (0, off + pr))
    full_out = pl.BlockSpec((t_pad, HD), lambda pr, i: (0, pr))
    blk = pl.BlockSpec((QB, HD), lambda pr, i: (i, pr))
    sds = jax.ShapeDtypeStruct((t_pad, SB_H * SB_DH), F32)
    return pl.pallas_call(
        body, name=name, grid=(SB_H // 2, nq),
        in_specs=[pl.BlockSpec((QB, HD), lambda pr, i: (i, SB_Q0 + pr)), full_in(SB_K0), full_in(SB_V0), blk,
                  pl.BlockSpec((QB, HD), lambda pr, i: (i, dy_blk0 + pr))] + [ANY] * n,
        out_specs=[blk, full_out, full_out] + [ANY] * n,
        out_shape=[sds, sds, sds] + (exchange[1](cargo) if n else []),
        scratch_shapes=(exchange[2](n) if n else []) + [pltpu.VMEM((nq, HD, QB), F32)] * 2,
        compiler_params=_cparams(),
    )(p, p, p, rtot, dy, *cargo)


HG_LEVELS = 6


def _hg_prefix_matrix():
    t = np.arange(CH)[:, None]
    j = np.arange(CH)[None, :]
    groups = [(j <= t)]
    for lvl in range(1, HG_LEVELS + 1):
        half = CH >> lvl
        e = (t // (2 * half)) * (2 * half) + half - 1
        groups.append(j <= e)
    groups.append(np.ones((8, CH), bool))
    e = np.concatenate(groups, axis=0).astype(np.float32)
    return np.concatenate([e, e, e], axis=1), np.concatenate([e, e, e], axis=0)


HG_G = 8


def _hg_chunk(qr, fr, iv, r0, r1, st, valid, ecat):
    g = st.shape[0]
    mx = jnp.maximum(r0, r1)
    e0 = jnp.exp(r0 - mx)
    e1 = jnp.exp(r1 - mx)
    lb = e1 / (e0 + e1)
    fg = lb + (1.0 - lb) * _sigmoid(fr)
    logf = jnp.where(valid, jnp.log(fg), 0.0)
    kk = jnp.where(valid, 1.0 - fg, 0.0)
    q = jnp.where(valid, _silu(qr), 0.0)
    v = _heads(jnp.where(valid, iv, 0.0), g)

    pre = _mask_dot(ecat, logf)
    b = pre[0:CH]
    b_last = jnp.max(pre[(HG_LEVELS + 1) * CH:], axis=0, keepdims=True)
    row = _iota((CH, 1), 0)
    ri = _iota((1, CH, CH), 1)
    ci = _iota((1, CH, CH), 2)
    a = jnp.where(ri == ci, jnp.sum(_heads(q * kk, g), axis=2, keepdims=True), 0.0)
    for lvl in range(1, HG_LEVELS + 1):
        half = CH >> lvl
        m = pre[lvl * CH:(lvl + 1) * CH]
        low = (row & half) != 0
        dec = jnp.exp(jnp.where(low, b - m, m - b))
        qt = jnp.where(low, q * dec, 0.0)
        kt = jnp.where(low, 0.0, kk * dec)
        same = (ri >> (7 - lvl)) == (ci >> (7 - lvl))
        a = a + jnp.where(same, bbnt(_heads(qt, g), _heads(kt, g)), 0.0)
    o = bbnt(_heads(q * jnp.exp(b), g), st) + bbnn(a, v)
    kd = kk * jnp.exp(b_last - b)
    st_new = st * _heads(jnp.exp(b_last), g) + bbtn(v, _heads(kd, g))
    return o, st_new


def _hg_specs(npair, rev):
    cc = (lambda c: npair - 1 - c) if rev else (lambda c: c)
    ng = HG_H // HG_G
    blk = lambda off: pl.BlockSpec((PAIR, HG_G * HD), lambda h, c: (cc(c), off * ng + h))
    lbs = pl.BlockSpec((2, HG_G * HD), lambda h, c: (0, h))
    state = pl.BlockSpec((1, HG_G, HD, HD), lambda h, c: (cc(c), h, 0, 0))
    return ng, blk, lbs, state


def _hg_fwd(p, lbraw, ecat, pad, name):
    t_pad = p.shape[0]
    npair = t_pad // PAIR
    ng, blk, lbs, state = _hg_specs(npair, False)

    def body(q_ref, f_ref, i_ref, lb_ref, e_ref, et_ref, o_ref, ss_ref, s_ref):
        c = pl.program_id(1)

        @pl.when(c == 0)
        def _():
            s_ref[...] = jnp.zeros_like(s_ref)

        st = s_ref[...]
        ss_ref[0] = st
        for half in (0, 1):
            r = _chunk_rows(half)
            o, st = _hg_chunk(q_ref[r, :], f_ref[r, :], i_ref[r, :], lb_ref[0:1, :], lb_ref[1:2, :], st,
                              _chunk_valid(c, half, pad), (e_ref[...], et_ref[...]))
            _store_heads(o_ref, o, r)
        s_ref[...] = st

    return pl.pallas_call(
        body, name=name, grid=(ng, npair),
        in_specs=[blk(0), blk(1), blk(2), lbs] + [pl.BlockSpec(e.shape, lambda h, c: (0, 0)) for e in ecat],
        out_specs=[blk(0), state],
        out_shape=[jax.ShapeDtypeStruct((t_pad, HG_H * HD), F32), jax.ShapeDtypeStruct((npair, HG_H, HD, HD), F32)],
        scratch_shapes=[pltpu.VMEM((HG_G, HD, HD), F32)],
        compiler_params=_cparams(),
    )(p, p, p, lbraw, *ecat)


def _hg_bwd(p, lbraw, ecat, ssave, do, pad, name, cargo=(), exchange=None):
    t_pad = p.shape[0]
    npair = t_pad // PAIR
    ng, blk, lbs, state = _hg_specs(npair, True)
    n = len(cargo)

    def body(q_ref, f_ref, i_ref, lb_ref, e_ref, et_ref, ss_ref, do_ref, *rest):
        c = pl.program_id(1)
        hg = pl.program_id(0)
        ds_ref = rest[-1]
        (dq_ref, df_ref, di_ref, dlb_ref), end_cargo = _cargo_bounds(
            rest[:-1], n, 4, exchange, (hg == 0) & (c == 0), (hg == ng - 1) & (c == npair - 1))

        @pl.when(c == 0)
        def _():
            ds_ref[...] = jnp.zeros_like(ds_ref)
            dlb_ref[...] = jnp.zeros_like(dlb_ref)

        ra, rb = _chunk_rows(0), _chunk_rows(1)
        va, vb = _chunk_valid(npair - 1 - c, 0, pad), _chunk_valid(npair - 1 - c, 1, pad)
        ecv = (e_ref[...], et_ref[...])

        def pair(qa, fa, ia, qb, fb, ib, r0, r1, st):
            oa, st = _hg_chunk(qa, fa, ia, r0, r1, st, va, ecv)
            ob, st = _hg_chunk(qb, fb, ib, r0, r1, st, vb, ecv)
            return oa, ob, st

        ins = [ref[r, :] for r in (ra, rb) for ref in (q_ref, f_ref, i_ref)]
        _, vjp = jax.vjp(pair, *ins, lb_ref[0:1, :], lb_ref[1:2, :], ss_ref[0])
        g = vjp((_heads(do_ref[ra, :], HG_G), _heads(do_ref[rb, :], HG_G), ds_ref[...]))
        for r, (dq, df, di) in ((ra, g[0:3]), (rb, g[3:6])):
            dq_ref[r, :] = dq.astype(BF16)
            df_ref[r, :] = df.astype(BF16)
            di_ref[r, :] = di.astype(BF16)
        dlb_ref[0:1, :] += g[6]
        dlb_ref[1:2, :] += g[7]
        ds_ref[...] = g[8]
        end_cargo()

    sds = jax.ShapeDtypeStruct((t_pad, HG_H * HD), BF16)
    return pl.pallas_call(
        body, name=name, grid=(ng, npair),
        in_specs=[blk(0), blk(1), blk(2), lbs] + [pl.BlockSpec(e.shape, lambda h, c: (0, 0)) for e in ecat]
        + [state, blk(0)] + [ANY] * n,
        out_specs=[blk(0), blk(0), blk(0), lbs] + [ANY] * n,
        out_shape=[sds, sds, sds, jax.ShapeDtypeStruct((2, HG_H * HD), F32)] + (exchange[1](cargo) if n else []),
        scratch_shapes=(exchange[2](n) if n else []) + [pltpu.VMEM((HG_G, HD, HD), F32)],
        compiler_params=_cparams(),
    )(p, p, p, lbraw, *ecat, ssave, do, *cargo)


def _pad_ab_cols(w):
    z = jnp.zeros((w.shape[0], HD - GDN_H), w.dtype)
    return jnp.concatenate([w[:, :2048], w[:, 2048:2052], z, w[:, 2052:2056], z, w[:, 2056:]], axis=1)


def _unpad_ab_cols(w):
    return jnp.concatenate([w[:, :2048], w[:, 2048:2052], w[:, 2176:2180], w[:, 2304:]], axis=1)


def _lane_pad(v):
    return jnp.pad(v, ((0, 0), (0, HD - v.shape[1])))


def _mlp_fwd(h, hb, w1, w2, layer, g, b):
    a, r = _mm(hb, w1, b_view=("cols", layer), out_dtype=BF16, act=True, name=f"mlp_up_{layer}")
    m, y, yb = _mm(r, w2, b_view=("rows", layer), ln=(h, g, b), name=f"mlp_down_{layer}")
    return a, r, m, y, yb


def _mlp_bwd(hb, a, r, dmb, w1, w2, layer):
    da = _mm(dmb, w2, tb=True, b_view=("rows", layer), out_dtype=BF16, gate=a, name=f"mlp_down_dx_{layer}")
    dw2 = _mm(r, dmb, ta=True, out_dtype=BF16, name=f"mlp_down_dw_{layer}")
    dh = _mm(da, w1, tb=True, b_view=("cols", layer), name=f"mlp_up_dx_{layer}")
    dw1 = _mm(hb, da, ta=True, out_dtype=BF16, out_split=N_CHIP, name=f"mlp_up_dw_{layer}")
    return dh, dw1, dw2


def _local_step(h0, tgt, w, pad, late=None):
    row = lambda a, i: a[i:i + 1]
    ecat = tuple(jnp.asarray(e, dtype=BF16) for e in _hg_prefix_matrix())
    cw = [w["conv_w"][:, i * 512:(i + 1) * 512] for i in range(3)]
    alog, dtb = _lane_pad(w["a_log"]), _lane_pad(w["dt_bias"])

    h0b = h0.astype(BF16)
    p0 = _mm(h0b, w["ab_w_in"], name="ab_in")
    qn = _conv_fwd(p0, 0, cw[0], "q", pad, "conv_q")
    kn = _conv_fwd(p0, 4, cw[1], "k", pad, "conv_k")
    vn = _conv_fwd(p0, 8, cw[2], "v", pad, "conv_v")
    if late is None:
        oa_raw, ss0 = _gdn_fwd(qn, kn, vn, p0, alog, dtb, pad, "gdn_fwd")
        ob, rtot = _sb_fwd(p0, pad, "sb_fwd")
    else:
        oa_raw, ss0, g_cin, g_cout = _gdn_fwd(qn, kn, vn, p0, alog, dtb, pad, "gdn_fwd",
                                              cargo=[late["c_w_in"], late["c_w_out"]], exchange=GATHER)
        ob, rtot, g_about, g_w1, g_w2 = _sb_fwd(p0, pad, "sb_fwd", exchange=GATHER,
                                                cargo=[late["ab_w_out"], late["mlp_w1"], late["mlp_w2"]])
        w = dict(w, ab_w_out=g_about.reshape(D, D), c_w_in=g_cin, c_w_out=g_cout.reshape(D, D), mlp_w1=g_w1, mlp_w2=g_w2)
    oa = _grms_fwd(oa_raw, p0, 12, w["ab_gnorm_g"], "gdn_gate")
    ycat = jnp.concatenate([oa, ob], axis=1)
    mix0, h1, h1b = _mm(ycat, w["ab_w_out"], name="ab_out", ln=(h0, row(w["ln_mix_g"], 0), row(w["ln_mix_b"], 0)))
    a0, r0, m0, h2, h2b = _mlp_fwd(h1, h1b, w["mlp_w1"], w["mlp_w2"], 0, row(w["ln_ffn_g"], 0), row(w["ln_ffn_b"], 0))
    p1 = _mm(h2b, w["c_w_in"], b_view=("cols", 0), name="c_in")
    oc_raw, ss1 = _hg_fwd(p1, w["c_lb_raw"], ecat, pad, "hg_fwd")
    yc = _grms_fwd(oc_raw, p1, 3 * HG_H, w["c_gnorm_g"], "hg_gate")
    mix1, h3, h3b = _mm(yc, w["c_w_out"], name="c_out", ln=(h2, row(w["ln_mix_g"], 1), row(w["ln_mix_b"], 1)))
    a1, r1, m1, h4, _ = _mlp_fwd(h3, h3b, w["mlp_w1"], w["mlp_w2"], 1, row(w["ln_ffn_g"], 1), row(w["ln_ffn_b"], 1))
    loss, dh4 = _loss_fwd(h4, tgt, pad + N_META, "loss")

    dh3a, dm1b, dfg1, dfb1 = _ln_res_bwd(h3, m1, row(w["ln_ffn_g"], 1), row(w["ln_ffn_b"], 1), [dh4], "ln_ffn_bwd_1")
    dh3b, dw1_1, dw2_1 = _mlp_bwd(h3b, a1, r1, dm1b, w["mlp_w1"], w["mlp_w2"], 1)
    dh2a, dmix1b, dmg1, dmb1 = _ln_res_bwd(h2, mix1, row(w["ln_mix_g"], 1), row(w["ln_mix_b"], 1), [dh3a, dh3b], "ln_mix_bwd_1")
    dyc = _mm(dmix1b, w["c_w_out"], tb=True, name="c_out_dx")
    dwco = _mm(yc, dmix1b, ta=True, out_dtype=BF16, name="c_out_dw")
    doc, dzc, dcg = _grms_bwd(oc_raw, p1, 3 * HG_H, w["c_gnorm_g"], dyc, 0, "hg_gate_bwd")
    landed = {}
    rows4 = lambda a: a.reshape(N_CHIP, -1, D)
    if late is None:
        dq1, df1, di1, dlb = _hg_bwd(p1, w["c_lb_raw"], ecat, ss1, doc, pad, "hg_bwd")
    else:
        dq1, df1, di1, dlb, landed["w1_1"] = _hg_bwd(
            p1, w["c_lb_raw"], ecat, ss1, doc, pad, "hg_bwd", cargo=[dw1_1], exchange=SCATTER)
    dp1 = [dq1, df1, di1, dzc]
    dh2b = _mm_groups_nt(dp1, w["c_w_in"], "c_in_dx")
    dwc = jnp.stack([_mm(h2b, d, ta=True, out_dtype=BF16, name=f"c_in_dw_{i}") for i, d in enumerate(dp1)])
    dh1a, dm0b, dfg0, dfb0 = _ln_res_bwd(h1, m0, row(w["ln_ffn_g"], 0), row(w["ln_ffn_b"], 0), [dh2a, dh2b], "ln_ffn_bwd_0")
    dh1b, dw1_0, dw2_0 = _mlp_bwd(h1b, a0, r0, dm0b, w["mlp_w1"], w["mlp_w2"], 0)
    dh0a, dmix0b, dmg0, dmb0 = _ln_res_bwd(h0, mix0, row(w["ln_mix_g"], 0), row(w["ln_mix_b"], 0), [dh1a, dh1b], "ln_mix_bwd_0")
    dycat = _mm(dmix0b, w["ab_w_out"], tb=True, name="ab_out_dx")
    dwabo = _mm(ycat, dmix0b, ta=True, out_dtype=BF16, name="ab_out_dw")
    doa, dza, dag = _grms_bwd(oa_raw, p0, 12, w["ab_gnorm_g"], dycat, 0, "gdn_gate_bwd")
    if late is None:
        dqn, dkn, dvn, dbb, daa, dal, ddt = _gdn_bwd(qn, kn, vn, p0, alog, dtb, ss0, doa, pad, "gdn_bwd")
        dqb, dkb, dvb = _sb_bwd(p0, rtot, dycat, 4, pad, "sb_bwd")
    else:
        dqn, dkn, dvn, dbb, daa, dal, ddt, landed["c_w_in"] = _gdn_bwd(
            qn, kn, vn, p0, alog, dtb, ss0, doa, pad, "gdn_bwd", cargo=[dwc], exchange=SCATTER)
        (dqb, dkb, dvb, landed["w1_0"], landed["w2_0"], landed["w2_1"], landed["ab_w_out"],
         landed["c_w_out"]) = _sb_bwd(
            p0, rtot, dycat, 4, pad, "sb_bwd",
            cargo=[dw1_0, rows4(dw2_0), rows4(dw2_1), rows4(dwabo), rows4(dwco)], exchange=SCATTER)
    dpq, dcq = _conv_bwd(p0, 0, cw[0], dqn, "q", pad, "conv_q_bwd")
    dpk, dck = _conv_bwd(p0, 4, cw[1], dkn, "k", pad, "conv_k_bwd")
    dpv, dcv = _conv_bwd(p0, 8, cw[2], dvn, "v", pad, "conv_v_bwd")
    dp0 = _assemble_bf16([dpq, dpk, dpv, dza, dbb, daa, dqb, dkb, dvb], "ab_in_dy")
    dwab = _mm(h0b, dp0, ta=True, out_dtype=BF16, name="ab_in_dw")
    if late is None:
        dh0 = _mm(dp0, w["ab_w_in"], tb=True, plus=dh0a, name="ab_in_dx")
    else:
        dab = jnp.transpose(_unpad_ab_cols(dwab).reshape(D, N_CHIP, AB_TRUE // N_CHIP), (1, 0, 2))
        dh0, landed["ab_w_in"] = _mm(dp0, w["ab_w_in"], tb=True, plus=dh0a, name="ab_in_dx", cargo=[dab],
                                     exchange=SCATTER)

    grads = {
        "ab_w_in": dwab, "conv_w": jnp.concatenate([dcq, dck, dcv], axis=1),
        "a_log": dal[:, :GDN_H], "dt_bias": ddt[:, :GDN_H],
        "ab_gnorm_g": dag, "ab_w_out": dwabo, "c_w_in": dwc, "c_lb_raw": dlb, "c_gnorm_g": dcg, "c_w_out": dwco,
        "ln_mix_g": jnp.concatenate([dmg0, dmg1], 0), "ln_mix_b": jnp.concatenate([dmb0, dmb1], 0),
        "w1_0": dw1_0, "w1_1": dw1_1, "w2_0": dw2_0, "w2_1": dw2_1,
        "ln_ffn_g": jnp.concatenate([dfg0, dfg1], 0), "ln_ffn_b": jnp.concatenate([dfb0, dfb1], 0),
        "landed": landed,
    }
    return loss, dh0, grads


MESH = pl.DeviceIdType.MESH
ANY = pl.BlockSpec(memory_space=pl.ANY)
N_CHIP = 4
N_DEV = 8
CHIP_REL = ((1, 0), (0, 1), (1, 1))
DEV_REL = tuple((dx, dy, dc) for dx in (0, 1) for dy in (0, 1) for dc in (0, 1))[1:]

def _pos():
    return lax.axis_index("x"), lax.axis_index("y"), lax.axis_index("c")


def _flip(a, d):
    return a + d - 2 * a * d


class _Exchange:
    def __init__(self, local, sends, recvs):
        self.local, self.sends, self.recvs = local, sends, recvs

    def start(self):
        for cp in self.local + self.sends:
            cp.start()

    def wait(self):
        for cp in self.recvs:
            cp.wait_recv()
        for cp in self.sends:
            cp.wait_send()
        for cp in self.local:
            cp.wait()


def _gather_sems(n):
    return [pltpu.SemaphoreType.DMA((3 * n,)), pltpu.SemaphoreType.DMA((3 * n,)), pltpu.SemaphoreType.DMA((n,))]


def _gather_copies(x_refs, o_refs, send_sems, recv_sems, local_sems):
    n = len(x_refs)
    x, y, c = _pos()
    local = [pltpu.make_async_copy(x_refs[a], o_refs[a].at[2 * x + y], local_sems.at[a]) for a in range(n)]

    def copy(a, k, sending):
        tx, ty = _flip(x, CHIP_REL[k][0]), _flip(y, CHIP_REL[k][1])
        return pltpu.make_async_remote_copy(
            src_ref=x_refs[a], dst_ref=o_refs[a].at[2 * x + y if sending else 2 * tx + ty],
            send_sem=send_sems.at[3 * a + k], recv_sem=recv_sems.at[3 * a + k], device_id=(tx, ty, c), device_id_type=MESH)

    pairs = [(a, k) for a in range(n) for k in range(3)]
    return _Exchange(local, [copy(a, k, True) for a, k in pairs], [copy(a, k, False) for a, k in pairs])


def _gather_shapes(bufs):
    return [jax.ShapeDtypeStruct((N_CHIP,) + b.shape, b.dtype) for b in bufs]


def _chip_allgather(bufs, name):
    n = len(bufs)

    def body(*refs):
        ex = _gather_copies(refs[:n], refs[n:2 * n], *refs[2 * n:])
        ex.start()
        ex.wait()

    return pl.pallas_call(
        body, name=name, in_specs=[ANY] * n, out_specs=[ANY] * n, out_shape=_gather_shapes(bufs),
        scratch_shapes=_gather_sems(n), compiler_params=pltpu.CompilerParams(has_side_effects=True),
    )(*bufs)


def _scatter_sems(n):
    nr = N_DEV - 1
    return [pltpu.SemaphoreType.DMA((nr * n,)), pltpu.SemaphoreType.DMA((nr * n,)), pltpu.SemaphoreType.DMA((n,))]


def _scatter_copies(g_refs, o_refs, send_sems, recv_sems, local_sems):
    n = len(g_refs)
    nr = N_DEV - 1
    x, y, c = _pos()
    me = 4 * x + 2 * y + c
    local = [pltpu.make_async_copy(g_refs[a].at[2 * x + y], o_refs[a].at[me], local_sems.at[a]) for a in range(n)]

    def copy(a, k, sending):
        dx, dy, dc = DEV_REL[k]
        tx, ty, tc = _flip(x, dx), _flip(y, dy), _flip(c, dc)
        return pltpu.make_async_remote_copy(
            src_ref=g_refs[a].at[2 * tx + ty], dst_ref=o_refs[a].at[me if sending else 4 * tx + 2 * ty + tc],
            send_sem=send_sems.at[nr * a + k], recv_sem=recv_sems.at[nr * a + k],
            device_id=(tx, ty, tc), device_id_type=MESH)

    pairs = [(a, k) for a in range(n) for k in range(nr)]
    return _Exchange(local, [copy(a, k, True) for a, k in pairs], [copy(a, k, False) for a, k in pairs])


def _scatter_shapes(gs):
    return [jax.ShapeDtypeStruct((N_DEV,) + g.shape[1:], g.dtype) for g in gs]


GATHER = (_gather_copies, _gather_shapes, _gather_sems)
SCATTER = (_scatter_copies, _scatter_shapes, _scatter_sems)


def _sum_slots(rs, name):
    n, rh, w = rs[0].shape
    tr = _pick(rh, (256, 128, 64, 16))

    def body(*refs):
        o_ref = refs[-1]
        for layer, r_ref in enumerate(refs[:-1]):
            acc = r_ref[0].astype(F32)
            for s in range(1, n):
                acc = acc + r_ref[s].astype(F32)
            o_ref[layer] = acc

    return pl.pallas_call(
        body, name=name, grid=(rh // tr,), in_specs=[pl.BlockSpec((n, tr, w), lambda i: (0, i, 0))] * len(rs),
        out_specs=pl.BlockSpec((len(rs), tr, w), lambda i: (0, i, 0)),
        out_shape=jax.ShapeDtypeStruct((len(rs), rh, w), F32), compiler_params=_cparams(),
    )(*rs)


def _small_allreduce(buf, name):
    r, w = buf.shape

    def body(b_ref, o_ref, land_ref, send_sems, recv_sems):
        _allreduce_steps(b_ref, o_ref, land_ref, send_sems, recv_sems)

    vm = pl.BlockSpec(memory_space=pltpu.VMEM)
    return pl.pallas_call(
        body, name=name, in_specs=[vm], out_specs=vm, out_shape=jax.ShapeDtypeStruct((r, w), F32),
        scratch_shapes=[pltpu.VMEM((N_DEV, r, w), F32), pltpu.SemaphoreType.DMA((N_DEV - 1,)),
                        pltpu.SemaphoreType.DMA((N_DEV - 1,))],
        compiler_params=pltpu.CompilerParams(has_side_effects=True),
    )(buf)


def _front_exchange(bufs, small, name):
    n = len(bufs)
    r, w = small.shape

    def body(*refs):
        x_refs, b_ref = refs[:n], refs[n]
        o_refs, os_ref = refs[n + 1:2 * n + 1], refs[2 * n + 1]
        gsend, grecv, glocal, land_ref, send_sems, recv_sems = refs[2 * n + 2:]
        ex = _gather_copies(x_refs, o_refs, gsend, grecv, glocal)
        ex.start()
        _allreduce_steps(b_ref, os_ref, land_ref, send_sems, recv_sems)
        ex.wait()

    vm = pl.BlockSpec(memory_space=pltpu.VMEM)
    return pl.pallas_call(
        body, name=name, in_specs=[ANY] * n + [vm], out_specs=[ANY] * n + [vm],
        out_shape=_gather_shapes(bufs) + [jax.ShapeDtypeStruct((r, w), F32)],
        scratch_shapes=_gather_sems(n) + [pltpu.VMEM((N_DEV, r, w), F32), pltpu.SemaphoreType.DMA((N_DEV - 1,)),
                                          pltpu.SemaphoreType.DMA((N_DEV - 1,))],
        compiler_params=pltpu.CompilerParams(has_side_effects=True),
    )(*bufs, small)


def _allreduce_steps(b_ref, o_ref, land_ref, send_sems, recv_sems):
    if True:
        x, y, c = _pos()
        me = 4 * x + 2 * y + c
        land_ref[me] = b_ref[...]

        def target(k):
            dx, dy, dc = DEV_REL[k]
            return _flip(x, dx), _flip(y, dy), _flip(c, dc)

        sends = []
        for k in range(N_DEV - 1):
            tx, ty, tc = target(k)
            cp = pltpu.make_async_remote_copy(
                src_ref=b_ref, dst_ref=land_ref.at[me], send_sem=send_sems.at[k], recv_sem=recv_sems.at[k],
                device_id=(tx, ty, tc), device_id_type=MESH)
            cp.start()
            sends.append(cp)
        for k in range(N_DEV - 1):
            tx, ty, tc = target(k)
            pltpu.make_async_remote_copy(
                src_ref=b_ref, dst_ref=land_ref.at[4 * tx + 2 * ty + tc], send_sem=send_sems.at[k],
                recv_sem=recv_sems.at[k], device_id=(tx, ty, tc), device_id_type=MESH).wait_recv()
        for cp in sends:
            cp.wait_send()
        acc = land_ref[0]
        for s in range(1, N_DEV):
            acc = acc + land_ref[s]
        o_ref[...] = acc


def _adamw(w, g, m, v, name):
    r, c = w.shape
    tr = _pick(r, (256, 128, 64, 8)) if r * c > (1 << 18) else r

    def body(w_ref, g_ref, m_ref, v_ref, d_ref, m2_ref, v2_ref):
        gg = g_ref[...]
        m2 = ADAM_B1 * m_ref[...] + (1.0 - ADAM_B1) * gg
        v2 = ADAM_B2 * v_ref[...] + (1.0 - ADAM_B2) * (gg * gg)
        m_hat = m2 / (1.0 - ADAM_B1 ** ADAM_STEP)
        v_hat = v2 / (1.0 - ADAM_B2 ** ADAM_STEP)
        d_ref[...] = -ADAM_LR * (m_hat / (jnp.sqrt(v_hat) + ADAM_EPS) + ADAM_WD * w_ref[...])
        m2_ref[...] = m2
        v2_ref[...] = v2

    blk = pl.BlockSpec((tr, c), lambda i: (i, 0))
    sds = jax.ShapeDtypeStruct((r, c), F32)
    return pl.pallas_call(body, name=name, grid=(r // tr,), in_specs=[blk] * 4, out_specs=[blk] * 3,
                          out_shape=[sds] * 3, compiler_params=_cparams())(w, g, m, v)


BIG = ("ab_w_in", "ab_w_out", "c_w_in", "c_w_out", "mlp_w1", "mlp_w2")
SMALL = ("ln_mix_g", "ln_mix_b", "ln_ffn_g", "ln_ffn_b", "c_lb_raw", "ab_a_log", "ab_dt_bias", "ab_gnorm_g", "c_gnorm_g")
SMALL_ROWS = 16
CONV_ROWS = 8
CONV_W = 3 * GDN_H * HD


def _conv_to_rows(cw):
    return jnp.pad(cw, ((0, 0), (0, 2 * D - CONV_W))).reshape(CONV_ROWS, D)


def _rows_to_conv(rows):
    return rows.reshape(CONV_K, 2 * D)[:, :CONV_W]


def _pack_small(d):
    rows = [jnp.pad(d[n], ((0, 0), (0, D - d[n].shape[1]))) for n in SMALL]
    buf = jnp.concatenate(rows, axis=0)
    return jnp.pad(buf, ((0, SMALL_ROWS - buf.shape[0]), (0, 0)))


def _unpack_small(buf, like):
    out, r = {}, 0
    for n in SMALL:
        nr, nc = like[n].shape
        out[n] = buf[r:r + nr, :nc]
        r += nr
    return out


def kernel(x, meta_tokens, ab_w_in, ab_conv_w, ab_a_log, ab_dt_bias, ab_gnorm_g, ab_w_out, c_w_in, c_lb_raw, c_gnorm_g, c_w_out, ln_mix_g, ln_mix_b, mlp_w1, mlp_w2, ln_ffn_g, ln_ffn_b, loss_target, m_meta_tokens, m_ab_w_in, m_ab_conv_w, m_ab_a_log, m_ab_dt_bias, m_ab_gnorm_g, m_ab_w_out, m_c_w_in, m_c_lb_raw, m_c_gnorm_g, m_c_w_out, m_ln_mix_g, m_ln_mix_b, m_mlp_w1, m_mlp_w2, m_ln_ffn_g, m_ln_ffn_b, v_meta_tokens, v_ab_w_in, v_ab_conv_w, v_ab_a_log, v_ab_dt_bias, v_ab_gnorm_g, v_ab_w_out, v_c_w_in, v_c_lb_raw, v_c_gnorm_g, v_c_w_out, v_ln_mix_g, v_ln_mix_b, v_mlp_w1, v_mlp_w2, v_ln_ffn_g, v_ln_ffn_b):
    names = ("meta_tokens", "ab_w_in", "ab_conv_w", "ab_a_log", "ab_dt_bias", "ab_gnorm_g", "ab_w_out", "c_w_in",
             "c_lb_raw", "c_gnorm_g", "c_w_out", "ln_mix_g", "ln_mix_b", "mlp_w1", "mlp_w2", "ln_ffn_g", "ln_ffn_b")
    wts = dict(zip(names, (meta_tokens, ab_w_in, ab_conv_w, ab_a_log, ab_dt_bias, ab_gnorm_g, ab_w_out, c_w_in, c_lb_raw,
                           c_gnorm_g, c_w_out, ln_mix_g, ln_mix_b, mlp_w1, mlp_w2, ln_ffn_g, ln_ffn_b)))
    mom_m = dict(zip(names, (m_meta_tokens, m_ab_w_in, m_ab_conv_w, m_ab_a_log, m_ab_dt_bias, m_ab_gnorm_g, m_ab_w_out,
                             m_c_w_in, m_c_lb_raw, m_c_gnorm_g, m_c_w_out, m_ln_mix_g, m_ln_mix_b, m_mlp_w1, m_mlp_w2,
                             m_ln_ffn_g, m_ln_ffn_b)))
    mom_v = dict(zip(names, (v_meta_tokens, v_ab_w_in, v_ab_conv_w, v_ab_a_log, v_ab_dt_bias, v_ab_gnorm_g, v_ab_w_out,
                             v_c_w_in, v_c_lb_raw, v_c_gnorm_g, v_c_w_out, v_ln_mix_g, v_ln_mix_b, v_mlp_w1, v_mlp_w2,
                             v_ln_ffn_g, v_ln_ffn_b)))
    seq = x.shape[1]
    pad = (-(N_META + seq)) % QB
    xi, yi, ci = _pos()
    chip = 2 * xi + yi

    late ={"ab_w_out": ab_w_out[0].astype(BF16), "c_w_in": c_w_in.astype(BF16), "c_w_out": c_w_out[0].astype(BF16),
            "mlp_w1": mlp_w1.astype(BF16), "mlp_w2": mlp_w2.astype(BF16)}
    mcols, ccols = meta_tokens.shape[1], ab_conv_w.shape[2]
    place = jnp.concatenate([
        lax.dynamic_update_slice(jnp.zeros((N_META, D), F32), 0.5 * meta_tokens, (0, chip * mcols)),
        _conv_to_rows(lax.dynamic_update_slice(jnp.zeros((CONV_K, CONV_W), F32), 0.5 * ab_conv_w[0], (0, chip * ccols)))],
        axis=0)
    gat_ab_in, placed = _front_exchange([ab_w_in[0].astype(BF16)], place, "gather_front")
    meta_full = placed[:N_META]

    w = {
        "ab_w_in": _pad_ab_cols(jnp.transpose(gat_ab_in, (1, 0, 2)).reshape(D, AB_TRUE)),
        "conv_w": _rows_to_conv(placed[N_META:]), "a_log": ab_a_log, "dt_bias": ab_dt_bias,
        "ab_gnorm_g": ab_gnorm_g, "c_lb_raw": c_lb_raw,
        "c_gnorm_g": c_gnorm_g, "ln_mix_g": ln_mix_g, "ln_mix_b": ln_mix_b, "ln_ffn_g": ln_ffn_g, "ln_ffn_b": ln_ffn_b,
    }

    h0 = jnp.concatenate([jnp.zeros((pad, D), F32), meta_full, x[0]], axis=0)
    tgt = jnp.concatenate([jnp.zeros((pad + N_META, D), F32), loss_target[0]], axis=0)
    loss8, dh0, g = _local_step(h0, tgt, w, pad, late)
    grad_x = dh0[pad + N_META:][None]

    gsmall = {"ln_mix_g": g["ln_mix_g"], "ln_mix_b": g["ln_mix_b"], "ln_ffn_g": g["ln_ffn_g"], "ln_ffn_b": g["ln_ffn_b"],
              "c_lb_raw": g["c_lb_raw"], "ab_a_log": g["a_log"], "ab_dt_bias": g["dt_bias"], "ab_gnorm_g": g["ab_gnorm_g"],
              "c_gnorm_g": g["c_gnorm_g"]}
    packed = _pack_small(gsmall).at[SMALL_ROWS - 1, :loss8.shape[1]].set(loss8[0])
    sbuf = jnp.concatenate([packed, dh0[pad:pad + N_META], _conv_to_rows(g["conv_w"])], axis=0)
    ssum = _small_allreduce(sbuf, "allreduce_small")
    loss = ssum[SMALL_ROWS - 1, 0]
    grads = _unpack_small(ssum[:SMALL_ROWS], wts)
    grads["meta_tokens"] = lax.dynamic_slice(ssum[SMALL_ROWS:SMALL_ROWS + N_META], (0, chip * mcols), (N_META, mcols))
    grads["ab_conv_w"] = lax.dynamic_slice(_rows_to_conv(ssum[SMALL_ROWS + N_META:]), (0, chip * ccols), (CONV_K, ccols))[None]

    landed = g["landed"]
    for n in ("ab_w_in", "ab_w_out", "c_w_in", "c_w_out"):
        grads[n] = _sum_slots([landed[n]], f"grad_sum_{n}")
    grads["mlp_w1"] = _sum_slots([landed["w1_0"], landed["w1_1"]], "grad_sum_mlp_w1")
    grads["mlp_w2"] = _sum_slots([landed["w2_0"], landed["w2_1"]], "grad_sum_mlp_w2")

    delta, new_m, new_v = {}, {}, {}
    for n in ("meta_tokens", "ab_conv_w") + BIG:
        shp = wts[n].shape
        to2 = lambda a: a.reshape(-1, shp[-1])
        d2, m2, v2 = _adamw(to2(wts[n]), to2(grads[n]), to2(mom_m[n]), to2(mom_v[n]), f"adamw_{n}")
        delta[n], new_m[n], new_v[n] = d2.reshape(shp), m2.reshape(shp), v2.reshape(shp)
    d2, m2, v2 = _adamw(_pack_small(wts), ssum[:SMALL_ROWS], _pack_small(mom_m), _pack_small(mom_v), "adamw_small")
    delta.update(_unpack_small(d2, wts))
    new_m.update(_unpack_small(m2, wts))
    new_v.update(_unpack_small(v2, wts))

    return (loss, grad_x, *[grads[n] for n in names], *[delta[n] for n in names], *[new_m[n] for n in names],
            *[new_v[n] for n in names])
```

```python
import functools

import numpy as np
import jax
import jax.numpy as jnp
from jax import lax
from jax.experimental import pallas as pl
from jax.experimental.pallas import tpu as pltpu

F32 = jnp.float32
BF16 = jnp.bfloat16

D = 1024
N_META = 16
DEPTH = 2
GDN_H = 4
SB_H = 8
SB_DH = 64
HG_H = 8
HD = 128
CH = 64
QB = 128
ALPHA = float((2 * DEPTH) ** 0.25)
LN_EPS = 1e-5
RMS_EPS = 1e-6
L2_EPS = 1e-6
NEG = -1e30

ADAM_LR = 0.001
ADAM_B1 = 0.9
ADAM_B2 = 0.999
ADAM_EPS = 1e-08
ADAM_WD = 0.01
ADAM_STEP = 10

AB_TRUE = 3592
V7X_VMEM_BYTES = 64 * 1024 * 1024
VMEM_LIMIT = V7X_VMEM_BYTES - 8 * 1024 * 1024

NN = ((1,), (0,))
NT = ((1,), (1,))
TN = ((0,), (0,))


def _cparams(**kw):
    return pltpu.CompilerParams(vmem_limit_bytes=VMEM_LIMIT, **kw)


def _dg(a, b, dims, mode):
    if mode == "h":
        return lax.dot_general(a, b, dims, precision=lax.Precision.HIGHEST, preferred_element_type=F32)
    if mode == "b":
        return lax.dot_general(a.astype(BF16), b.astype(BF16), dims, preferred_element_type=F32)
    ah, bh = a.astype(BF16), b.astype(BF16)
    al, bl = (a - ah.astype(F32)).astype(BF16), (b - bh.astype(F32)).astype(BF16)
    d = lambda x, y: lax.dot_general(x, y, dims, preferred_element_type=F32)
    return d(ah, bh) + (d(ah, bl) + d(al, bh))


def _make_dots(mode, batched=False):
    if batched:
        nn_d, nt_d, tn_d = (((2,), (1,)), ((0,), (0,))), (((2,), (2,)), ((0,), (0,))), (((1,), (1,)), ((0,), (0,)))
    else:
        nn_d, nt_d, tn_d = (NN, ((), ())), (NT, ((), ())), (TN, ((), ()))

    @jax.custom_vjp
    def nn(a, b):
        return _dg(a, b, nn_d, mode)

    @jax.custom_vjp
    def nt(a, b):
        return _dg(a, b, nt_d, mode)

    @jax.custom_vjp
    def tn(a, b):
        return _dg(a, b, tn_d, mode)

    nn.defvjp(lambda a, b: (nn(a, b), (a, b)), lambda r, g: (nt(g, r[1]), tn(r[0], g)))
    nt.defvjp(lambda a, b: (nt(a, b), (a, b)), lambda r, g: (nn(g, r[1]), tn(g, r[0])))
    tn.defvjp(lambda a, b: (tn(a, b), (a, b)), lambda r, g: (nt(r[1], g), nn(r[0], g)))
    return nn, nt, tn


hnn = _make_dots("h")[0]
bbnn, bbnt, bbtn = _make_dots("b", True)
mbnn, mbnt, mbtn = _make_dots("m", True)
hbnt = _make_dots("h", True)[1]


def _split3(x, axis):
    x1 = x.astype(BF16)
    r1 = x - x1.astype(F32)
    x2 = r1.astype(BF16)
    x3 = (r1 - x2.astype(F32)).astype(BF16)
    return jnp.concatenate([x1, x2, x3], axis=axis)


@jax.custom_vjp
def _mask_dot(e3, x):
    return lax.dot_general(e3[0], _split3(x, 0), (NN, ((), ())), preferred_element_type=F32)


def _mask_dot_bwd(e3, g):
    dx = lax.dot_general(e3[1], _split3(g, 0), (TN, ((), ())), preferred_element_type=F32)
    return (jnp.zeros_like(e3[0]), jnp.zeros_like(e3[1])), dx


_mask_dot.defvjp(lambda e3, x: (_mask_dot(e3, x), e3), _mask_dot_bwd)


def _heads(a, n):
    return jnp.concatenate([a[None, :, h * HD:(h + 1) * HD] for h in range(n)], axis=0)


def _sigmoid(x):
    return jax.nn.sigmoid(x)


def _silu(x):
    return x * jax.nn.sigmoid(x)


def _softplus(x):
    return jnp.maximum(x, 0.0) + jnp.log(1.0 + jnp.exp(-jnp.abs(x)))


def _iota(shape, dim):
    return lax.broadcasted_iota(jnp.int32, shape, dim)


def _pick(n, prefs):
    for p in prefs:
        if n % p == 0:
            return p
    return n


def _mm(a, b, *, ta=False, tb=False, out_dtype=F32, name, b_view=None, out_split=0, act=False, gate=None, plus=None,
        ln=None, cargo=(), exchange=None):
    if ta:
        k_dim, m_dim = a.shape
    else:
        m_dim, k_dim = a.shape
    if b_view is None:
        w_rows, w_cols = b.shape
    else:
        kind, layer = b_view
        nj, _, blk_r, blk_c = b.shape
        w_rows, w_cols = (blk_r, nj * blk_c) if kind == "cols" else (nj * blk_r, blk_c)
    n_dim = w_rows if tb else w_cols
    assert (w_cols if tb else w_rows) == k_dim
    tm = _pick(m_dim, (1024, 1056, 704, 640, 512, 384, 256, 128))
    tn = _pick(n_dim, (1024, 1056, 704, 640, 512, 384, 256, 128))
    tk = _pick(k_dim, (1024, 1056, 704, 512, 384, 256, 128))
    nk = k_dim // tk
    a_spec = pl.BlockSpec((tk, tm), lambda i, j, k: (k, i)) if ta else pl.BlockSpec((tm, tk), lambda i, j, k: (i, k))
    wb = (tn, tk) if tb else (tk, tn)
    w_idx = (lambda i, j, k: (j, k)) if tb else (lambda i, j, k: (k, j))
    if b_view is None:
        b_spec = pl.BlockSpec(wb, w_idx)
    elif kind == "cols":
        per = blk_c // wb[1]
        b_spec = pl.BlockSpec((None, None) + wb,
                              lambda i, j, k: (w_idx(i, j, k)[1] // per, layer, w_idx(i, j, k)[0], w_idx(i, j, k)[1] % per))
    else:
        per = blk_r // wb[0]
        b_spec = pl.BlockSpec((None, None) + wb,
                              lambda i, j, k: (w_idx(i, j, k)[0] // per, layer, w_idx(i, j, k)[0] % per, w_idx(i, j, k)[1]))
    if out_split:
        per_o = (n_dim // out_split) // tn
        out_spec = pl.BlockSpec((None, tm, tn), lambda i, j, k: (j // per_o, i, j % per_o))
        out_sds = jax.ShapeDtypeStruct((out_split, m_dim, n_dim // out_split), out_dtype)
    else:
        out_spec = pl.BlockSpec((tm, tn), lambda i, j, k: (i, j))
        out_sds = jax.ShapeDtypeStruct((m_dim, n_dim), out_dtype)
    dims = (((0 if ta else 1,), (1 if tb else 0,)), ((), ()))
    assert sum(e is not None for e in (gate, plus, ln)) <= 1
    extra = [e for e in (gate, plus) if e is not None] + list(ln or ())
    n_out = 2 if act else 3 if ln else 1
    assert ln is None or (tn == n_dim and not out_split)

    def finish(acc, refs):
        if ln:
            y = _ln_res_fn(refs[0][...], acc, refs[1][...], refs[2][...])
            refs[3][...] = acc
            refs[4][...] = y
            refs[5][...] = y.astype(BF16)
        elif act:
            refs[0][...] = acc.astype(refs[0].dtype)
            r = jnp.maximum(acc, 0.0)
            refs[1][...] = (r * r).astype(refs[1].dtype)
        elif gate is not None:
            refs[1][...] = (acc * (2.0 * jnp.maximum(refs[0][...].astype(F32), 0.0))).astype(refs[1].dtype)
        elif plus is not None:
            refs[1][...] = (refs[0][...] + acc).astype(refs[1].dtype)
        else:
            refs[0][...] = acc.astype(refs[0].dtype)

    grid = (m_dim // tm, n_dim // tn, nk)
    nc = len(cargo)

    def body(a_ref, b_ref, *rest):
        acc_ref = rest[-1]
        ids = [pl.program_id(d) for d in range(3)]
        outs, end_cargo = _cargo_bounds(
            rest[len(extra):-1], nc, n_out, exchange, (ids[0] == 0) & (ids[1] == 0) & (ids[2] == 0),
            (ids[0] == grid[0] - 1) & (ids[1] == grid[1] - 1) & (ids[2] == grid[2] - 1))
        refs = tuple(rest[:len(extra)]) + tuple(outs)
        part = lax.dot_general(a_ref[...], b_ref[...], dims, preferred_element_type=F32)
        if nk == 1:
            finish(part, refs)
        else:
            k = ids[2]

            @pl.when(k == 0)
            def _():
                acc_ref[...] = part

            @pl.when(k > 0)
            def _():
                acc_ref[...] += part

            @pl.when(k == nk - 1)
            def _():
                finish(acc_ref[...], refs)
        end_cargo()

    tile = pl.BlockSpec((tm, tn), lambda i, j, k: (i, j))
    rowv = pl.BlockSpec((1, tn), lambda i, j, k: (0, j))
    out_sdss = [out_sds] * n_out
    if ln:
        out_sdss = [jax.ShapeDtypeStruct((m_dim, n_dim), dt) for dt in (F32, F32, BF16)]
    out = pl.pallas_call(
        body, name=name, grid=grid,
        in_specs=[a_spec, b_spec] + ([tile, rowv, rowv] if ln else [tile] * len(extra)) + [ANY] * nc,
        out_specs=[out_spec] * n_out + [ANY] * nc,
        out_shape=out_sdss + (exchange[1](cargo) if nc else []),
        scratch_shapes=(exchange[2](nc) if nc else []) + [pltpu.VMEM((tm, tn) if nk > 1 else (8, 128), F32)],
        compiler_params=_cparams(dimension_semantics=("arbitrary",) * 3 if nc else ("parallel", "parallel", "arbitrary")),
    )(a, b, *extra, *cargo)
    if nc:
        return out
    return out if (act or ln) else out[0]


def _mm_groups_nt(parts, b, name):
    m_dim, k_dim = parts[0].shape
    ng, _, n_dim, _ = b.shape
    assert len(parts) == ng and b.shape[3] == k_dim
    tm = _pick(m_dim, (1056, 704, 512, 384, 256, 128))

    def body(*refs):
        a_refs, b_ref, o_ref, acc_ref = refs[:ng], refs[ng], refs[ng + 1], refs[ng + 2]
        k = pl.program_id(1)
        for g in range(ng):
            @pl.when(k == g)
            def _(g=g):
                part = lax.dot_general(a_refs[g][...], b_ref[...], (NT, ((), ())), preferred_element_type=F32)
                if g == 0:
                    acc_ref[...] = part
                elif g < ng - 1:
                    acc_ref[...] += part
                else:
                    o_ref[...] = acc_ref[...] + part

    return pl.pallas_call(
        body, name=name, grid=(m_dim // tm, ng),
        in_specs=[pl.BlockSpec((tm, k_dim), lambda i, k: (i, 0))] * ng
        + [pl.BlockSpec((None, None, n_dim, k_dim), lambda i, k: (k, 0, 0, 0))],
        out_specs=pl.BlockSpec((tm, n_dim), lambda i, k: (i, 0)),
        out_shape=jax.ShapeDtypeStruct((m_dim, n_dim), F32),
        scratch_shapes=[pltpu.VMEM((tm, n_dim), F32)],
        compiler_params=_cparams(dimension_semantics=("parallel", "arbitrary")),
    )(*parts, b)


def _row_tile(t_pad, width):
    for tr in (528, 352, 176, 128, 64):
        if t_pad % tr == 0 and tr * width * 4 <= (3 << 19) and tr % 16 == 0:
            return tr
    return 64 if t_pad % 64 == 0 else t_pad


def _ln_res_fn(h, m, g, b):
    x = ALPHA * h + m
    mu = jnp.mean(x, axis=-1, keepdims=True)
    xc = x - mu
    var = jnp.mean(xc * xc, axis=-1, keepdims=True)
    return xc * lax.rsqrt(var + LN_EPS) * g + b


def _ln_res_bwd(h, m, g, b, dys, name):
    t_pad = h.shape[0]
    tr = _row_tile(t_pad, D)
    nd = len(dys)

    def body(h_ref, m_ref, g_ref, b_ref, *rest):
        d_refs, (dh_ref, dm_ref, dg_ref, db_ref) = rest[:nd], rest[nd:]
        _, vjp = jax.vjp(_ln_res_fn, h_ref[...], m_ref[...], g_ref[...], b_ref[...])
        dy = d_refs[0][...]
        for d_ref in d_refs[1:]:
            dy = dy + d_ref[...]
        dh, dm, dg, db = vjp(dy)
        dh_ref[...] = dh
        dm_ref[...] = dm.astype(BF16)

        @pl.when(pl.program_id(0) == 0)
        def _():
            dg_ref[...] = jnp.zeros_like(dg_ref)
            db_ref[...] = jnp.zeros_like(db_ref)

        dg_ref[...] += dg
        db_ref[...] += db

    row = pl.BlockSpec((tr, D), lambda i: (i, 0))
    par = pl.BlockSpec((1, D), lambda i: (0, 0))
    return pl.pallas_call(
        body, name=name, grid=(t_pad // tr,), in_specs=[row, row, par, par] + [row] * nd,
        out_specs=[row, row, par, par],
        out_shape=[jax.ShapeDtypeStruct((t_pad, D), F32), jax.ShapeDtypeStruct((t_pad, D), BF16),
                   jax.ShapeDtypeStruct((1, D), F32), jax.ShapeDtypeStruct((1, D), F32)],
        compiler_params=_cparams(),
    )(h, m, g, b, *dys)


def _grms_fn(o, z, g):
    y = o * lax.rsqrt(jnp.mean(o * o, axis=-1, keepdims=True) + RMS_EPS) * g
    return y * _silu(z)


def _grms_fwd(o, z_arr, z_blk0, g, name):
    t_pad, w = o.shape
    tr = _row_tile(t_pad, w)
    assert (z_blk0 * HD) % w == 0

    def body(o_ref, z_ref, g_ref, y_ref):
        for h in range(w // HD):
            c = slice(h * HD, (h + 1) * HD)
            y_ref[:, c] = _grms_fn(o_ref[:, c], z_ref[:, c], g_ref[...]).astype(BF16)

    return pl.pallas_call(
        body, name=name, grid=(t_pad // tr,),
        in_specs=[pl.BlockSpec((tr, w), lambda i: (i, 0)), pl.BlockSpec((tr, w), lambda i: (i, z_blk0 * HD // w)),
                  pl.BlockSpec((1, HD), lambda i: (0, 0))],
        out_specs=pl.BlockSpec((tr, w), lambda i: (i, 0)),
        out_shape=jax.ShapeDtypeStruct((t_pad, w), BF16), compiler_params=_cparams(),
    )(o, z_arr, g)


def _grms_bwd(o, z_arr, z_blk0, g, dy_arr, dy_blk0, name):
    t_pad, w = o.shape
    tr = _row_tile(t_pad, w)
    assert (z_blk0 * HD) % w == 0 and (dy_blk0 * HD) % w == 0

    def body(o_ref, z_ref, g_ref, dy_ref, do_ref, dz_ref, dg_ref):
        @pl.when(pl.program_id(0) == 0)
        def _():
            dg_ref[...] = jnp.zeros_like(dg_ref)

        for h in range(w // HD):
            c = slice(h * HD, (h + 1) * HD)
            _, vjp = jax.vjp(_grms_fn, o_ref[:, c], z_ref[:, c], g_ref[...])
            do, dz, dg = vjp(dy_ref[:, c])
            do_ref[:, c] = do
            dz_ref[:, c] = dz.astype(BF16)
            dg_ref[...] += dg

    blk = pl.BlockSpec((tr, w), lambda i: (i, 0))
    return pl.pallas_call(
        body, name=name, grid=(t_pad // tr,),
        in_specs=[blk, pl.BlockSpec((tr, w), lambda i: (i, z_blk0 * HD // w)), pl.BlockSpec((1, HD), lambda i: (0, 0)),
                  pl.BlockSpec((tr, w), lambda i: (i, dy_blk0 * HD // w))],
        out_specs=[blk, blk, pl.BlockSpec((1, HD), lambda i: (0, 0))],
        out_shape=[jax.ShapeDtypeStruct((t_pad, w), F32), jax.ShapeDtypeStruct((t_pad, w), BF16),
                   jax.ShapeDtypeStruct((1, HD), F32)],
        compiler_params=_cparams(),
    )(o, z_arr, g, dy_arr)


def _loss_fwd(y, tgt, first_row, name):
    t_pad = y.shape[0]
    tr = _row_tile(t_pad, D)

    def body(y_ref, t_ref, l_ref, dy_ref):
        rows = pl.program_id(0) * tr + _iota((tr, 1), 0)
        err = jnp.where(rows >= first_row, y_ref[...] - t_ref[...], 0.0)
        dy_ref[...] = err * (1.0 / D)

        @pl.when(pl.program_id(0) == 0)
        def _():
            l_ref[...] = jnp.zeros_like(l_ref)

        part = jnp.sum(jnp.sum(err * err, axis=1, keepdims=True), axis=0, keepdims=True)
        l_ref[...] += jnp.broadcast_to(part * (0.5 / D), l_ref.shape)

    row = pl.BlockSpec((tr, D), lambda i: (i, 0))
    return pl.pallas_call(
        body, name=name, grid=(t_pad // tr,), in_specs=[row, row],
        out_specs=[pl.BlockSpec((8, 128), lambda i: (0, 0)), row],
        out_shape=[jax.ShapeDtypeStruct((8, 128), F32), jax.ShapeDtypeStruct((t_pad, D), F32)],
        compiler_params=_cparams(),
    )(y, tgt)


def _assemble_bf16(parts, name):
    t_pad = parts[0].shape[0]
    widths = [p.shape[1] for p in parts]
    total = sum(widths)
    tr = _row_tile(t_pad, total)

    def body(*refs):
        o_ref = refs[-1]
        off = 0
        for ref, w in zip(refs[:-1], widths):
            o_ref[:, off:off + w] = ref[...].astype(BF16)
            off += w

    return pl.pallas_call(
        body, name=name, grid=(t_pad // tr,), in_specs=[pl.BlockSpec((tr, w), lambda i: (i, 0)) for w in widths],
        out_specs=pl.BlockSpec((tr, total), lambda i: (i, 0)),
        out_shape=jax.ShapeDtypeStruct((t_pad, total), BF16), compiler_params=_cparams(),
    )(*parts)


CONV_K = 4
HALO = 8
RT = 384


def _conv_fwd(p, blk0, w, mode, pad, name):
    t_pad = p.shape[0]
    nt = t_pad // RT
    scale = HD ** -0.5 if mode == "q" else 1.0

    def body(x_ref, w_ref, y_ref, xs_ref):
        xs_ref[0:HALO, :] = jnp.zeros((HALO, HD), F32)
        rows = _iota((t_pad, 1), 0)
        xs_ref[HALO:HALO + t_pad, :] = jnp.where(rows >= pad, x_ref[...], 0.0)
        wv = w_ref[...]

        def tile(i, carry):
            r0 = pl.multiple_of(i * RT, RT)
            ext = xs_ref[pl.ds(r0, RT + HALO), :]
            acc = ext[HALO:, :] * wv[3:4, :]
            for s in (1, 2, 3):
                acc = acc + pltpu.roll(ext, s, 0)[HALO:, :] * wv[3 - s:4 - s, :]
            y = _silu(acc)
            if mode != "v":
                y = y * lax.rsqrt(jnp.sum(y * y, axis=-1, keepdims=True) + L2_EPS) * scale
            y_ref[pl.ds(r0, RT), :] = y
            return carry

        lax.fori_loop(0, nt, tile, 0)

    return pl.pallas_call(
        body, name=name, grid=(GDN_H,),
        in_specs=[pl.BlockSpec((t_pad, HD), lambda h: (0, blk0 + h)), pl.BlockSpec((CONV_K, HD), lambda h: (0, h))],
        out_specs=pl.BlockSpec((t_pad, HD), lambda h: (0, h)),
        out_shape=jax.ShapeDtypeStruct((t_pad, GDN_H * HD), F32),
        scratch_shapes=[pltpu.VMEM((t_pad + HALO, HD), F32)],
        compiler_params=_cparams(),
    )(p, w)


def _conv_bwd(p, blk0, w, dn, mode, pad, name):
    t_pad = p.shape[0]
    nt = t_pad // RT
    scale = HD ** -0.5 if mode == "q" else 1.0

    def body(x_ref, w_ref, dn_ref, dx_ref, dw_ref, xs_ref, ds_ref):
        xs_ref[0:HALO, :] = jnp.zeros((HALO, HD), F32)
        xs_ref[HALO + t_pad:HALO + t_pad + 2 * HALO, :] = jnp.zeros((2 * HALO, HD), F32)
        ds_ref[t_pad:t_pad + HALO, :] = jnp.zeros((HALO, HD), F32)
        rows = _iota((t_pad, 1), 0)
        xs_ref[HALO:HALO + t_pad, :] = jnp.where(rows >= pad, x_ref[...], 0.0)
        ds_ref[0:t_pad, :] = dn_ref[...]
        wv = w_ref[...]

        def tile(i, dw):
            r0 = pl.multiple_of(i * RT, RT)
            ext = xs_ref[pl.ds(r0, RT + 2 * HALO), :]
            dn_e = ds_ref[pl.ds(r0, RT + HALO), :]
            xsh = [ext[HALO:, :]] + [pltpu.roll(ext, s, 0)[HALO:, :] for s in (1, 2, 3)]
            pre = xsh[0] * wv[3:4, :]
            for s in (1, 2, 3):
                pre = pre + xsh[s] * wv[3 - s:4 - s, :]
            sg = _sigmoid(pre)
            y = pre * sg
            if mode != "v":
                ss = jnp.sum(y * y, axis=-1, keepdims=True) + L2_EPS
                r = lax.rsqrt(ss)
                dy = scale * (dn_e * r - y * (r * r * r) * jnp.sum(dn_e * y, axis=-1, keepdims=True))
            else:
                dy = dn_e
            dpre = dy * (sg * (1.0 + pre * (1.0 - sg)))
            dx = dpre[:RT, :] * wv[3:4, :]
            for s in (1, 2, 3):
                dx = dx + pltpu.roll(dpre, RT + HALO - s, 0)[:RT, :] * wv[3 - s:4 - s, :]
            trow = r0 + _iota((RT, 1), 0)
            dx_ref[pl.ds(r0, RT), :] = jnp.where(trow >= pad, dx, 0.0)
            new = []
            for s in (0, 1, 2, 3):
                new.append(dw[s] + jnp.sum(dpre[:RT, :] * xsh[s][:RT, :], axis=0, keepdims=True))
            return tuple(new)

        z = jnp.zeros((1, HD), F32)
        dw = lax.fori_loop(0, nt, tile, (z, z, z, z))
        for s in (0, 1, 2, 3):
            dw_ref[3 - s:4 - s, :] = dw[s]

    return pl.pallas_call(
        body, name=name, grid=(GDN_H,),
        in_specs=[pl.BlockSpec((t_pad, HD), lambda h: (0, blk0 + h)), pl.BlockSpec((CONV_K, HD), lambda h: (0, h)),
                  pl.BlockSpec((t_pad, HD), lambda h: (0, h))],
        out_specs=[pl.BlockSpec((t_pad, HD), lambda h: (0, h)), pl.BlockSpec((CONV_K, HD), lambda h: (0, h))],
        out_shape=[jax.ShapeDtypeStruct((t_pad, GDN_H * HD), F32), jax.ShapeDtypeStruct((CONV_K, GDN_H * HD), F32)],
        scratch_shapes=[pltpu.VMEM((t_pad + 3 * HALO, HD), F32), pltpu.VMEM((t_pad + HALO, HD), F32)],
        compiler_params=_cparams(),
    )(p, w, dn)


@jax.custom_vjp
def _unit_lower_inv(m, bd, eye):
    md = m * bd
    low = m - md
    p2 = mbnn(md, md)
    p4 = mbnn(p2, p2)
    dinv = mbnn(mbnn(eye - md, eye + p2), eye + p4)
    n = mbnn(dinv, low)
    n2 = mbnn(n, n)
    n4 = mbnn(n2, n2)
    return mbnn(mbnn(mbnn(eye - n, eye + n2), eye + n4), dinv)


def _unit_lower_inv_bwd(res, g):
    t, bd, eye = res
    return -mbtn(t, mbnt(g, t)), jnp.zeros_like(bd), jnp.zeros_like(eye)


def _unit_lower_inv_fwd(m, bd, eye):
    t = _unit_lower_inv(m, bd, eye)
    return t, (t, bd, eye)


_unit_lower_inv.defvjp(_unit_lower_inv_fwd, _unit_lower_inv_bwd)


def _gdn_chunks(chunks, alog, dtb, s):
    nh = chunks[0][0].shape[0]
    ri = _iota((1, CH, CH), 1)
    ci = _iota((1, CH, CH), 2)
    causal = ri >= ci
    strict = ri > ci
    eye = (ri == ci).astype(F32)
    bd = ((ri >> 3) == (ci >> 3)).astype(F32)
    ltri = (_iota((CH, CH), 0) >= _iota((CH, CH), 1)).astype(F32)
    sel = (_iota((nh, 1, HD), 2) == _iota((nh, 1, HD), 0)).astype(F32)
    last = _iota((1, CH, 1), 1) == CH - 1

    beta, gc, gc_rows = [], [], []
    for _, _, _, bb, aa, valid in chunks:
        beta_all = jnp.where(valid, _sigmoid(bb), 0.0)
        g_all = jnp.where(valid, -jnp.exp(alog) * _softplus(aa + dtb), 0.0)
        gc_all = hnn(ltri, g_all)
        beta.append(jnp.sum(beta_all[None] * sel, axis=2, keepdims=True))
        gc.append(jnp.sum(gc_all[None] * sel, axis=2, keepdims=True))
        gc_rows.append(hbnt(jnp.broadcast_to(sel, (nh, CH, HD)), jnp.broadcast_to(gc_all[None], (nh, CH, HD))))
    cat = lambda xs: jnp.concatenate(xs, axis=0)
    q, k, v = (cat([c[j] for c in chunks]) for j in range(3))
    beta, gc, gc_rows = cat(beta), cat(gc), cat(gc_rows)
    gc_last = jnp.sum(jnp.where(last, gc, 0.0), axis=1, keepdims=True)
    decay = jnp.exp(jnp.where(causal, gc - gc_rows, NEG))
    egc = jnp.exp(gc)

    kb = k * beta
    m = jnp.where(strict, bbnt(kb, k) * decay, 0.0)
    t_inv = _unit_lower_inv(m, bd, eye)
    u = bbnn(t_inv, v * beta)
    w = bbnn(t_inv, kb * egc)
    a_intra = bbnt(q, k) * decay
    q_dec = q * egc
    k_dec = k * jnp.exp(gc_last - gc)
    g_tot = jnp.exp(gc_last)

    outs = []
    for n in range(len(chunks)):
        part = lambda a: a[n * nh:(n + 1) * nh]
        v_new = part(u) - bbnn(part(w), s)
        outs.append(bbnn(part(q_dec), s) + bbnn(part(a_intra), v_new))
        s = s * part(g_tot) + bbtn(part(k_dec), v_new)
    return outs, s


PAIR = 2 * CH


def _gdn_specs(npair, rev):
    cc = (lambda c: npair - 1 - c) if rev else (lambda c: c)
    wide = pl.BlockSpec((PAIR, GDN_H * HD), lambda c: (cc(c), 0))
    fix = lambda off: pl.BlockSpec((PAIR, HD), lambda c: (cc(c), off))
    par = pl.BlockSpec((1, HD), lambda c: (0, 0))
    state = pl.BlockSpec((1, GDN_H, HD, HD), lambda c: (cc(c), 0, 0, 0))
    return wide, fix, par, state


def _store_heads(ref, a, rows=slice(None)):
    for h in range(a.shape[0]):
        ref[rows, h * HD:(h + 1) * HD] = a[h]


def _chunk_rows(half):
    return slice(half * CH, (half + 1) * CH)


def _chunk_valid(pair, half, pad):
    return ((2 * pair + half) * CH + _iota((CH, 1), 0)) >= pad


def _gdn_fwd(qn, kn, vn, p, alog, dtb, pad, name, cargo=(), exchange=None):
    t_pad = qn.shape[0]
    npair = t_pad // PAIR
    wide, fix, par, state = _gdn_specs(npair, False)
    n = len(cargo)

    def body(q_ref, k_ref, v_ref, bb_ref, aa_ref, al_ref, dt_ref, *rest):
        c = pl.program_id(0)
        s_ref = rest[-1]
        (o_ref, ss_ref), end_cargo = _cargo_bounds(rest[:-1], n, 2, exchange, c == 0, c == npair - 1)

        @pl.when(c == 0)
        def _():
            s_ref[...] = jnp.zeros_like(s_ref)

        s = s_ref[...]
        ss_ref[0] = s
        rows = [_chunk_rows(half) for half in (0, 1)]
        chunks = [(_heads(q_ref[r, :], GDN_H), _heads(k_ref[r, :], GDN_H), _heads(v_ref[r, :], GDN_H),
                   bb_ref[r, :], aa_ref[r, :], _chunk_valid(c, half, pad)) for half, r in enumerate(rows)]
        outs, s = _gdn_chunks(chunks, al_ref[...], dt_ref[...], s)
        for r, o in zip(rows, outs):
            _store_heads(o_ref, o, r)
        s_ref[...] = s
        end_cargo()

    return pl.pallas_call(
        body, name=name, grid=(npair,),
        in_specs=[wide, wide, wide, fix(16), fix(17), par, par] + [ANY] * n,
        out_specs=[wide, state] + [ANY] * n,
        out_shape=[jax.ShapeDtypeStruct((t_pad, GDN_H * HD), F32), jax.ShapeDtypeStruct((npair, GDN_H, HD, HD), F32)]
        + (exchange[1](cargo) if n else []),
        scratch_shapes=(exchange[2](n) if n else []) + [pltpu.VMEM((GDN_H, HD, HD), F32)],
        compiler_params=_cparams(),
    )(qn, kn, vn, p, p, alog, dtb, *cargo)


def _gdn_bwd(qn, kn, vn, p, alog, dtb, ssave, do, pad, name, cargo=(), exchange=None):
    t_pad = qn.shape[0]
    npair = t_pad // PAIR
    wide, fix, par, state = _gdn_specs(npair, True)
    n = len(cargo)

    def body(q_ref, k_ref, v_ref, bb_ref, aa_ref, al_ref, dt_ref, ss_ref, do_ref, *rest):
        c = pl.program_id(0)
        ds_ref = rest[-1]
        (dq_ref, dk_ref, dv_ref, dbb_ref, daa_ref, dal_ref, ddt_ref), end_cargo = _cargo_bounds(
            rest[:-1], n, 7, exchange, c == 0, c == npair - 1)

        @pl.when(c == 0)
        def _():
            ds_ref[...] = jnp.zeros_like(ds_ref)
            dal_ref[...] = jnp.zeros_like(dal_ref)
            ddt_ref[...] = jnp.zeros_like(ddt_ref)

        ra, rb = _chunk_rows(0), _chunk_rows(1)
        va, vb = _chunk_valid(npair - 1 - c, 0, pad), _chunk_valid(npair - 1 - c, 1, pad)

        def pair(qa, ka, va_, ba, aa, qb, kb, vb_, bb, ab, al, dt, s):
            (oa, ob), s = _gdn_chunks([(qa, ka, va_, ba, aa, va), (qb, kb, vb_, bb, ab, vb)], al, dt, s)
            return oa, ob, s

        ins = [f(ref[r, :]) for r in (ra, rb)
               for ref, f in ((q_ref, lambda a: _heads(a, GDN_H)), (k_ref, lambda a: _heads(a, GDN_H)),
                              (v_ref, lambda a: _heads(a, GDN_H)), (bb_ref, lambda a: a), (aa_ref, lambda a: a))]
        _, vjp = jax.vjp(pair, *ins, al_ref[...], dt_ref[...], ss_ref[0])
        g = vjp((_heads(do_ref[ra, :], GDN_H), _heads(do_ref[rb, :], GDN_H), ds_ref[...]))
        for r, (dq, dk, dv, dbb, daa) in ((ra, g[0:5]), (rb, g[5:10])):
            _store_heads(dq_ref, dq, r)
            _store_heads(dk_ref, dk, r)
            _store_heads(dv_ref, dv, r)
            dbb_ref[r, :] = dbb
            daa_ref[r, :] = daa
        dal_ref[...] += g[10]
        ddt_ref[...] += g[11]
        ds_ref[...] = g[12]
        end_cargo()

    sds = jax.ShapeDtypeStruct
    return pl.pallas_call(
        body, name=name, grid=(npair,),
        in_specs=[wide, wide, wide, fix(16), fix(17), par, par, state, wide] + [ANY] * n,
        out_specs=[wide, wide, wide, fix(0), fix(0), par, par] + [ANY] * n,
        out_shape=[sds((t_pad, GDN_H * HD), F32)] * 3 + [sds((t_pad, HD), F32)] * 2 + [sds((1, HD), F32)] * 2
        + (exchange[1](cargo) if n else []),
        scratch_shapes=(exchange[2](n) if n else []) + [pltpu.VMEM((GDN_H, HD, HD), F32)],
        compiler_params=_cparams(),
    )(qn, kn, vn, p, p, alog, dtb, ssave, do, *cargo)


SB_Q0, SB_K0, SB_V0 = 18, 22, 26
SB_SCALE = SB_DH ** -0.5
SB_NB_FWD, SB_NB_BWD = 11, 8


def _sb_terms(z, allowed):
    nz = -z
    raw = jnp.minimum(nz, 0.0) - jnp.log(1.0 + jnp.exp(jnp.minimum(z, nz)))
    l1m = raw if allowed is None else jnp.where(allowed, raw, 0.0)
    ls = z + raw
    return l1m, ls, jnp.exp(ls)


def _sb_passes(i, step, carry, per):
    total = i + 1

    def sized(done, first):
        return [functools.partial(step, done, masked=({0} if first else set()) | {nb - 1}, nb=nb)
                for nb in range(1, per + 1)]

    def several(c):
        n_mid = (total - per - 1) // per
        c = step(0, c, masked={0}, nb=per)
        c = lax.fori_loop(0, n_mid, lambda t, cc: step(per * (1 + t), cc, masked=set(), nb=per), c)
        done = per * (1 + n_mid)
        return lax.switch(total - done - 1, sized(done, False), c)

    return lax.cond(total <= per, lambda c: lax.switch(total - 1, sized(0, True), c), several, carry)


def _sb_stack(a, i):
    first = _iota((1, HD), 1) < SB_DH
    a2 = jnp.concatenate([jnp.where(first, a, 0.0), jnp.where(first, 0.0, a)], axis=0).astype(BF16)
    rq = i * QB + _iota((QB, 1), 0)
    return a2, jnp.concatenate([rq, rq], axis=0), first


def _hi_lo(a):
    hi = a.astype(BF16)
    lo = (a - hi.astype(F32)).astype(BF16)
    return jnp.concatenate([hi, lo], axis=1)


def _cargo_bounds(refs, n, n_out, exchange, first, last):
    outs = refs[n:n + n_out]
    if not n:
        return outs, lambda: None
    ex = exchange[0](refs[:n], refs[n + n_out:2 * n + n_out], *refs[2 * n + n_out:])

    @pl.when(first)
    def _():
        ex.start()

    def finish():
        @pl.when(last)
        def _():
            ex.wait()

    return outs, finish


def _sb_fwd(p, pad, name, cargo=(), exchange=None):
    t_pad = p.shape[0]
    nq = t_pad // QB
    n = len(cargo)

    def body(q_ref, k_ref, v_ref, *rest):
        i = pl.program_id(1)
        pr = pl.program_id(0)
        (o_ref, r_ref), end_cargo = _cargo_bounds(rest, n, 2, exchange, (pr == 0) & (i == 0),
                                                  (pr == SB_H // 2 - 1) & (i == nq - 1))
        q2, rowq, first = _sb_stack(q_ref[...] * SB_SCALE, i)
        tri = (_iota((QB, QB), 0) > _iota((QB, QB), 1)).astype(BF16)
        upper2 = jnp.concatenate([jnp.concatenate([tri, tri], axis=0), jnp.ones((2 * QB, QB), BF16)], axis=1)

        def chain(kb, masked):
            start = pl.multiple_of(kb * QB, QB)
            kblk = k_ref[pl.ds(start, QB), :].astype(BF16)
            vblk = v_ref[pl.ds(start, QB), :].astype(BF16)
            z = lax.dot_general(q2, kblk, (NT, ((), ())), preferred_element_type=F32)
            colk = kb * QB + _iota((1, QB), 1)
            al = ((colk < rowq) & (colk >= pad)) if masked else None
            l1m, ls, _ = _sb_terms(z, al)
            sums = lax.dot_general(_hi_lo(l1m), upper2, (NN, ((), ())), preferred_element_type=F32)
            return al, ls, sums[:, :QB], sums[:, QB:], vblk

        def step(done, carry, masked, nb):
            o_acc, run = carry
            ws, vs = [], []
            for n in range(nb):
                al, ls, suf, rs, vblk = chain(i - done - n, n in masked)
                wgt = jnp.exp(ls + suf + run)
                ws.append((wgt if al is None else jnp.where(al, wgt, 0.0)).astype(BF16))
                vs.append(vblk)
                run = run + rs
            o_acc = o_acc + lax.dot_general(jnp.concatenate(ws, axis=1), jnp.concatenate(vs, axis=0),
                                            (NN, ((), ())), preferred_element_type=F32)
            return o_acc, run

        o_acc, run = _sb_passes(i, step, (jnp.zeros((2 * QB, HD), F32), jnp.zeros((2 * QB, QB), F32)), SB_NB_FWD)
        o_ref[...] = jnp.where(first, o_acc[:QB], o_acc[QB:]).astype(BF16)
        r_ref[...] = jnp.where(first, run[:QB], run[QB:])
        end_cargo()

    full = lambda off: pl.BlockSpec((t_pad, HD), lambda pr, i: (0, off + pr))
    blk = pl.BlockSpec((QB, HD), lambda pr, i: (i, pr))
    return pl.pallas_call(
        body, name=name, grid=(SB_H // 2, nq),
        in_specs=[pl.BlockSpec((QB, HD), lambda pr, i: (i, SB_Q0 + pr)), full(SB_K0), full(SB_V0)] + [ANY] * n,
        out_specs=[blk, blk] + [ANY] * n,
        out_shape=[jax.ShapeDtypeStruct((t_pad, SB_H * SB_DH), BF16), jax.ShapeDtypeStruct((t_pad, SB_H * SB_DH), F32)]
        + (exchange[1](cargo) if n else []),
        scratch_shapes=exchange[2](n) if n else [],
        compiler_params=_cparams(),
    )(p, p, p, *cargo)


def _sb_bwd(p, rtot, dy, dy_blk0, pad, name, cargo=(), exchange=None):
    t_pad = p.shape[0]
    nq = t_pad // QB
    n = len(cargo)

    def body(q_ref, k_ref, v_ref, r_ref, do_ref, *rest):
        i = pl.program_id(1)
        pr = pl.program_id(0)
        dkt_ref, dvt_ref = rest[-2:]
        (dq_ref, dk_ref, dv_ref), end_cargo = _cargo_bounds(rest[:-2], n, 3, exchange, (pr == 0) & (i == 0),
                                                            (pr == SB_H // 2 - 1) & (i == nq - 1))

        @pl.when(i == 0)
        def _():
            dkt_ref[...] = jnp.zeros_like(dkt_ref)
            dvt_ref[...] = jnp.zeros_like(dvt_ref)

        q2, rowq, first = _sb_stack(q_ref[...] * SB_SCALE, i)
        do2, _, _ = _sb_stack(do_ref[...], i)
        q2t = jnp.transpose(q2.astype(F32)).astype(BF16)
        do2t = jnp.transpose(do2.astype(F32)).astype(BF16)
        rt = r_ref[...]
        lane = _iota((1, HD), 1)
        rcol = jnp.concatenate([jnp.sum(jnp.where(lane == 0, rt, 0.0), axis=1, keepdims=True),
                                jnp.sum(jnp.where(lane == SB_DH, rt, 0.0), axis=1, keepdims=True)], axis=0)
        rj = _iota((QB, QB), 0)
        cs = _iota((QB, QB), 1)
        tri_u = (rj > cs).astype(BF16)
        tri_l = (rj < cs).astype(BF16)
        ones2 = jnp.ones((2 * QB, QB), BF16)
        upper2 = jnp.concatenate([jnp.concatenate([tri_u, tri_u], axis=0), ones2], axis=1)
        lower2 = jnp.concatenate([jnp.concatenate([tri_l, tri_l], axis=0), ones2], axis=1)
        rcol = jnp.broadcast_to(rcol, (2 * QB, QB))

        def chain(kb, masked):
            start = pl.multiple_of(kb * QB, QB)
            kblk = k_ref[pl.ds(start, QB), :].astype(BF16)
            vblk = v_ref[pl.ds(start, QB), :].astype(BF16)
            z = lax.dot_general(q2, kblk, (NT, ((), ())), preferred_element_type=F32)
            colk = kb * QB + _iota((1, QB), 1)
            al = ((colk < rowq) & (colk >= pad)) if masked else None
            l1m, ls, sg = _sb_terms(z, al)
            dwgt = lax.dot_general(do2, vblk, (NT, ((), ())), preferred_element_type=F32)
            sums = lax.dot_general(_hi_lo(l1m), upper2, (NN, ((), ())), preferred_element_type=F32)
            return kb, kblk, al, ls, sums[:, :QB], sums[:, QB:], dwgt, sg

        def finish(c, left, gseen):
            kb, kblk, al, ls, suf, rs, dwgt, sg = c
            left = left - rs
            wgt = jnp.exp(ls + suf + left)
            if al is not None:
                wgt = jnp.where(al, wgt, 0.0)
            dl = dwgt * wgt
            sums = lax.dot_general(_hi_lo(dl), lower2, (NN, ((), ())), preferred_element_type=F32)
            gpre = gseen + sums[:, :QB]
            dz = dl - sg * (dl + gpre)
            if al is not None:
                dz = jnp.where(al, dz, 0.0)
            dz = dz.astype(BF16)
            dkt_ref[kb] += lax.dot_general(q2t, dz, (NN, ((), ())), preferred_element_type=F32)
            dvt_ref[kb] += lax.dot_general(do2t, wgt.astype(BF16), (NN, ((), ())), preferred_element_type=F32)
            return dz, left, gseen + sums[:, QB:]

        def step(done, carry, masked, nb):
            dq_acc, left, gseen = carry
            cs_ = [chain(done + n, n in masked) for n in range(nb)]
            dzs = []
            for c in cs_:
                dz, left, gseen = finish(c, left, gseen)
                dzs.append(dz)
            dq_acc = dq_acc + lax.dot_general(jnp.concatenate(dzs, axis=1), jnp.concatenate([c[1] for c in cs_], axis=0),
                                              (NN, ((), ())), preferred_element_type=F32)
            return dq_acc, left, gseen

        dq_acc, _, _ = _sb_passes(i, step, (jnp.zeros((2 * QB, HD), F32), rcol, jnp.zeros((2 * QB, QB), F32)),
                                  SB_NB_BWD)
        dq_ref[...] = jnp.where(first, dq_acc[:QB], dq_acc[QB:]) * SB_SCALE

        @pl.when(i == nq - 1)
        def _():
            for kb in range(nq):
                dk_ref[kb * QB:(kb + 1) * QB, :] = jnp.transpose(dkt_ref[kb])
                dv_ref[kb * QB:(kb + 1) * QB, :] = jnp.transpose(dvt_ref[kb])

        end_cargo()

    full_in = lambda off: pl.BlockSpec((t_pad, HD), lambda pr, i: (0, off + pr))
    full_out = pl.BlockSpec((t_pad, HD), lambda pr, i: (0, pr))
    blk = pl.BlockSpec((QB, HD), lambda pr, i: (i, pr))
    sds = jax.ShapeDtypeStruct((t_pad, SB_H * SB_DH), F32)
    return pl.pallas_call(
        body, name=name, grid=(SB_H // 2, nq),
        in_specs=[pl.BlockSpec((QB, HD), lambda pr, i: (i, SB_Q0 + pr)), full_in(SB_K0), full_in(SB_V0), blk,
                  pl.BlockSpec((QB, HD), lambda pr, i: (i, dy_blk0 + pr))] + [ANY] * n,
        out_specs=[blk, full_out, full_out] + [ANY] * n,
        out_shape=[sds, sds, sds] + (exchange[1](cargo) if n else []),
        scratch_shapes=(exchange[2](n) if n else []) + [pltpu.VMEM((nq, HD, QB), F32)] * 2,
        compiler_params=_cparams(),
    )(p, p, p, rtot, dy, *cargo)


HG_LEVELS = 6


def _hg_prefix_matrix():
    t = np.arange(CH)[:, None]
    j = np.arange(CH)[None, :]
    groups = [(j <= t)]
    for lvl in range(1, HG_LEVELS + 1):
        half = CH >> lvl
        e = (t // (2 * half)) * (2 * half) + half - 1
        groups.append(j <= e)
    groups.append(np.ones((8, CH), bool))
    e = np.concatenate(groups, axis=0).astype(np.float32)
    return np.concatenate([e, e, e], axis=1), np.concatenate([e, e, e], axis=0)


HG_G = 8


def _hg_chunk(qr, fr, iv, r0, r1, st, valid, ecat):
    g = st.shape[0]
    mx = jnp.maximum(r0, r1)
    e0 = jnp.exp(r0 - mx)
    e1 = jnp.exp(r1 - mx)
    lb = e1 / (e0 + e1)
    fg = lb + (1.0 - lb) * _sigmoid(fr)
    logf = jnp.where(valid, jnp.log(fg), 0.0)
    kk = jnp.where(valid, 1.0 - fg, 0.0)
    q = jnp.where(valid, _silu(qr), 0.0)
    v = _heads(jnp.where(valid, iv, 0.0), g)

    pre = _mask_dot(ecat, logf)
    b = pre[0:CH]
    b_last = jnp.max(pre[(HG_LEVELS + 1) * CH:], axis=0, keepdims=True)
    row = _iota((CH, 1), 0)
    ri = _iota((1, CH, CH), 1)
    ci = _iota((1, CH, CH), 2)
    a = jnp.where(ri == ci, jnp.sum(_heads(q * kk, g), axis=2, keepdims=True), 0.0)
    for lvl in range(1, HG_LEVELS + 1):
        half = CH >> lvl
        m = pre[lvl * CH:(lvl + 1) * CH]
        low = (row & half) != 0
        dec = jnp.exp(jnp.where(low, b - m, m - b))
        qt = jnp.where(low, q * dec, 0.0)
        kt = jnp.where(low, 0.0, kk * dec)
        same = (ri >> (7 - lvl)) == (ci >> (7 - lvl))
        a = a + jnp.where(same, bbnt(_heads(qt, g), _heads(kt, g)), 0.0)
    o = bbnt(_heads(q * jnp.exp(b), g), st) + bbnn(a, v)
    kd = kk * jnp.exp(b_last - b)
    st_new = st * _heads(jnp.exp(b_last), g) + bbtn(v, _heads(kd, g))
    return o, st_new


def _hg_specs(npair, rev):
    cc = (lambda c: npair - 1 - c) if rev else (lambda c: c)
    ng = HG_H // HG_G
    blk = lambda off: pl.BlockSpec((PAIR, HG_G * HD), lambda h, c: (cc(c), off * ng + h))
    lbs = pl.BlockSpec((2, HG_G * HD), lambda h, c: (0, h))
    state = pl.BlockSpec((1, HG_G, HD, HD), lambda h, c: (cc(c), h, 0, 0))
    return ng, blk, lbs, state


def _hg_fwd(p, lbraw, ecat, pad, name):
    t_pad = p.shape[0]
    npair = t_pad // PAIR
    ng, blk, lbs, state = _hg_specs(npair, False)

    def body(q_ref, f_ref, i_ref, lb_ref, e_ref, et_ref, o_ref, ss_ref, s_ref):
        c = pl.program_id(1)

        @pl.when(c == 0)
        def _():
            s_ref[...] = jnp.zeros_like(s_ref)

        st = s_ref[...]
        ss_ref[0] = st
        for half in (0, 1):
            r = _chunk_rows(half)
            o, st = _hg_chunk(q_ref[r, :], f_ref[r, :], i_ref[r, :], lb_ref[0:1, :], lb_ref[1:2, :], st,
                              _chunk_valid(c, half, pad), (e_ref[...], et_ref[...]))
            _store_heads(o_ref, o, r)
        s_ref[...] = st

    return pl.pallas_call(
        body, name=name, grid=(ng, npair),
        in_specs=[blk(0), blk(1), blk(2), lbs] + [pl.BlockSpec(e.shape, lambda h, c: (0, 0)) for e in ecat],
        out_specs=[blk(0), state],
        out_shape=[jax.ShapeDtypeStruct((t_pad, HG_H * HD), F32), jax.ShapeDtypeStruct((npair, HG_H, HD, HD), F32)],
        scratch_shapes=[pltpu.VMEM((HG_G, HD, HD), F32)],
        compiler_params=_cparams(),
    )(p, p, p, lbraw, *ecat)


def _hg_bwd(p, lbraw, ecat, ssave, do, pad, name, cargo=(), exchange=None):
    t_pad = p.shape[0]
    npair = t_pad // PAIR
    ng, blk, lbs, state = _hg_specs(npair, True)
    n = len(cargo)

    def body(q_ref, f_ref, i_ref, lb_ref, e_ref, et_ref, ss_ref, do_ref, *rest):
        c = pl.program_id(1)
        hg = pl.program_id(0)
        ds_ref = rest[-1]
        (dq_ref, df_ref, di_ref, dlb_ref), end_cargo = _cargo_bounds(
            rest[:-1], n, 4, exchange, (hg == 0) & (c == 0), (hg == ng - 1) & (c == npair - 1))

        @pl.when(c == 0)
        def _():
            ds_ref[...] = jnp.zeros_like(ds_ref)
            dlb_ref[...] = jnp.zeros_like(dlb_ref)

        ra, rb = _chunk_rows(0), _chunk_rows(1)
        va, vb = _chunk_valid(npair - 1 - c, 0, pad), _chunk_valid(npair - 1 - c, 1, pad)
        ecv = (e_ref[...], et_ref[...])

        def pair(qa, fa, ia, qb, fb, ib, r0, r1, st):
            oa, st = _hg_chunk(qa, fa, ia, r0, r1, st, va, ecv)
            ob, st = _hg_chunk(qb, fb, ib, r0, r1, st, vb, ecv)
            return oa, ob, st

        ins = [ref[r, :] for r in (ra, rb) for ref in (q_ref, f_ref, i_ref)]
        _, vjp = jax.vjp(pair, *ins, lb_ref[0:1, :], lb_ref[1:2, :], ss_ref[0])
        g = vjp((_heads(do_ref[ra, :], HG_G), _heads(do_ref[rb, :], HG_G), ds_ref[...]))
        for r, (dq, df, di) in ((ra, g[0:3]), (rb, g[3:6])):
            dq_ref[r, :] = dq.astype(BF16)
            df_ref[r, :] = df.astype(BF16)
            di_ref[r, :] = di.astype(BF16)
        dlb_ref[0:1, :] += g[6]
        dlb_ref[1:2, :] += g[7]
        ds_ref[...] = g[8]
        end_cargo()

    sds = jax.ShapeDtypeStruct((t_pad, HG_H * HD), BF16)
    return pl.pallas_call(
        body, name=name, grid=(ng, npair),
        in_specs=[blk(0), blk(1), blk(2), lbs] + [pl.BlockSpec(e.shape, lambda h, c: (0, 0)) for e in ecat]
        + [state, blk(0)] + [ANY] * n,
        out_specs=[blk(0), blk(0), blk(0), lbs] + [ANY] * n,
        out_shape=[sds, sds, sds, jax.ShapeDtypeStruct((2, HG_H * HD), F32)] + (exchange[1](cargo) if n else []),
        scratch_shapes=(exchange[2](n) if n else []) + [pltpu.VMEM((HG_G, HD, HD), F32)],
        compiler_params=_cparams(),
    )(p, p, p, lbraw, *ecat, ssave, do, *cargo)


def _pad_ab_cols(w):
    z = jnp.zeros((w.shape[0], HD - GDN_H), w.dtype)
    return jnp.concatenate([w[:, :2048], w[:, 2048:2052], z, w[:, 2052:2056], z, w[:, 2056:]], axis=1)


def _unpad_ab_cols(w):
    return jnp.concatenate([w[:, :2048], w[:, 2048:2052], w[:, 2176:2180], w[:, 2304:]], axis=1)


def _lane_pad(v):
    return jnp.pad(v, ((0, 0), (0, HD - v.shape[1])))


def _mlp_fwd(h, hb, w1, w2, layer, g, b):
    a, r = _mm(hb, w1, b_view=("cols", layer), out_dtype=BF16, act=True, name=f"mlp_up_{layer}")
    m, y, yb = _mm(r, w2, b_view=("rows", layer), ln=(h, g, b), name=f"mlp_down_{layer}")
    return a, r, m, y, yb


def _mlp_bwd(hb, a, r, dmb, w1, w2, layer):
    da = _mm(dmb, w2, tb=True, b_view=("rows", layer), out_dtype=BF16, gate=a, name=f"mlp_down_dx_{layer}")
    dw2 = _mm(r, dmb, ta=True, out_dtype=BF16, name=f"mlp_down_dw_{layer}")
    dh = _mm(da, w1, tb=True, b_view=("cols", layer), name=f"mlp_up_dx_{layer}")
    dw1 = _mm(hb, da, ta=True, out_dtype=BF16, out_split=N_CHIP, name=f"mlp_up_dw_{layer}")
    return dh, dw1, dw2


def _local_step(h0, tgt, w, pad, late=None):
    row = lambda a, i: a[i:i + 1]
    ecat = tuple(jnp.asarray(e, dtype=BF16) for e in _hg_prefix_matrix())
    cw = [w["conv_w"][:, i * 512:(i + 1) * 512] for i in range(3)]
    alog, dtb = _lane_pad(w["a_log"]), _lane_pad(w["dt_bias"])

    h0b = h0.astype(BF16)
    p0 = _mm(h0b, w["ab_w_in"], name="ab_in")
    qn = _conv_fwd(p0, 0, cw[0], "q", pad, "conv_q")
    kn = _conv_fwd(p0, 4, cw[1], "k", pad, "conv_k")
    vn = _conv_fwd(p0, 8, cw[2], "v", pad, "conv_v")
    if late is None:
        oa_raw, ss0 = _gdn_fwd(qn, kn, vn, p0, alog, dtb, pad, "gdn_fwd")
        ob, rtot = _sb_fwd(p0, pad, "sb_fwd")
    else:
        oa_raw, ss0, g_cin, g_cout = _gdn_fwd(qn, kn, vn, p0, alog, dtb, pad, "gdn_fwd",
                                              cargo=[late["c_w_in"], late["c_w_out"]], exchange=GATHER)
        ob, rtot, g_about, g_w1, g_w2 = _sb_fwd(p0, pad, "sb_fwd", exchange=GATHER,
                                                cargo=[late["ab_w_out"], late["mlp_w1"], late["mlp_w2"]])
        w = dict(w, ab_w_out=g_about.reshape(D, D), c_w_in=g_cin, c_w_out=g_cout.reshape(D, D), mlp_w1=g_w1, mlp_w2=g_w2)
    oa = _grms_fwd(oa_raw, p0, 12, w["ab_gnorm_g"], "gdn_gate")
    ycat = jnp.concatenate([oa, ob], axis=1)
    mix0, h1, h1b = _mm(ycat, w["ab_w_out"], name="ab_out", ln=(h0, row(w["ln_mix_g"], 0), row(w["ln_mix_b"], 0)))
    a0, r0, m0, h2, h2b = _mlp_fwd(h1, h1b, w["mlp_w1"], w["mlp_w2"], 0, row(w["ln_ffn_g"], 0), row(w["ln_ffn_b"], 0))
    p1 = _mm(h2b, w["c_w_in"], b_view=("cols", 0), name="c_in")
    oc_raw, ss1 = _hg_fwd(p1, w["c_lb_raw"], ecat, pad, "hg_fwd")
    yc = _grms_fwd(oc_raw, p1, 3 * HG_H, w["c_gnorm_g"], "hg_gate")
    mix1, h3, h3b = _mm(yc, w["c_w_out"], name="c_out", ln=(h2, row(w["ln_mix_g"], 1), row(w["ln_mix_b"], 1)))
    a1, r1, m1, h4, _ = _mlp_fwd(h3, h3b, w["mlp_w1"], w["mlp_w2"], 1, row(w["ln_ffn_g"], 1), row(w["ln_ffn_b"], 1))
    loss, dh4 = _loss_fwd(h4, tgt, pad + N_META, "loss")

    dh3a, dm1b, dfg1, dfb1 = _ln_res_bwd(h3, m1, row(w["ln_ffn_g"], 1), row(w["ln_ffn_b"], 1), [dh4], "ln_ffn_bwd_1")
    dh3b, dw1_1, dw2_1 = _mlp_bwd(h3b, a1, r1, dm1b, w["mlp_w1"], w["mlp_w2"], 1)
    dh2a, dmix1b, dmg1, dmb1 = _ln_res_bwd(h2, mix1, row(w["ln_mix_g"], 1), row(w["ln_mix_b"], 1), [dh3a, dh3b], "ln_mix_bwd_1")
    dyc = _mm(dmix1b, w["c_w_out"], tb=True, name="c_out_dx")
    dwco = _mm(yc, dmix1b, ta=True, out_dtype=BF16, name="c_out_dw")
    doc, dzc, dcg = _grms_bwd(oc_raw, p1, 3 * HG_H, w["c_gnorm_g"], dyc, 0, "hg_gate_bwd")
    landed = {}
    rows4 = lambda a: a.reshape(N_CHIP, -1, D)
    if late is None:
        dq1, df1, di1, dlb = _hg_bwd(p1, w["c_lb_raw"], ecat, ss1, doc, pad, "hg_bwd")
    else:
        dq1, df1, di1, dlb, landed["w1_1"] = _hg_bwd(
            p1, w["c_lb_raw"], ecat, ss1, doc, pad, "hg_bwd", cargo=[dw1_1], exchange=SCATTER)
    dp1 = [dq1, df1, di1, dzc]
    dh2b = _mm_groups_nt(dp1, w["c_w_in"], "c_in_dx")
    dwc = jnp.stack([_mm(h2b, d, ta=True, out_dtype=BF16, name=f"c_in_dw_{i}") for i, d in enumerate(dp1)])
    dh1a, dm0b, dfg0, dfb0 = _ln_res_bwd(h1, m0, row(w["ln_ffn_g"], 0), row(w["ln_ffn_b"], 0), [dh2a, dh2b], "ln_ffn_bwd_0")
    dh1b, dw1_0, dw2_0 = _mlp_bwd(h1b, a0, r0, dm0b, w["mlp_w1"], w["mlp_w2"], 0)
    dh0a, dmix0b, dmg0, dmb0 = _ln_res_bwd(h0, mix0, row(w["ln_mix_g"], 0), row(w["ln_mix_b"], 0), [dh1a, dh1b], "ln_mix_bwd_0")
    dycat = _mm(dmix0b, w["ab_w_out"], tb=True, name="ab_out_dx")
    dwabo = _mm(ycat, dmix0b, ta=True, out_dtype=BF16, name="ab_out_dw")
    doa, dza, dag = _grms_bwd(oa_raw, p0, 12, w["ab_gnorm_g"], dycat, 0, "gdn_gate_bwd")
    if late is None:
        dqn, dkn, dvn, dbb, daa, dal, ddt = _gdn_bwd(qn, kn, vn, p0, alog, dtb, ss0, doa, pad, "gdn_bwd")
        dqb, dkb, dvb = _sb_bwd(p0, rtot, dycat, 4, pad, "sb_bwd")
    else:
        dqn, dkn, dvn, dbb, daa, dal, ddt, landed["c_w_in"] = _gdn_bwd(
            qn, kn, vn, p0, alog, dtb, ss0, doa, pad, "gdn_bwd", cargo=[dwc], exchange=SCATTER)
        (dqb, dkb, dvb, landed["w1_0"], landed["w2_0"], landed["w2_1"], landed["ab_w_out"],
         landed["c_w_out"]) = _sb_bwd(
            p0, rtot, dycat, 4, pad, "sb_bwd",
            cargo=[dw1_0, rows4(dw2_0), rows4(dw2_1), rows4(dwabo), rows4(dwco)], exchange=SCATTER)
    dpq, dcq = _conv_bwd(p0, 0, cw[0], dqn, "q", pad, "conv_q_bwd")
    dpk, dck = _conv_bwd(p0, 4, cw[1], dkn, "k", pad, "conv_k_bwd")
    dpv, dcv = _conv_bwd(p0, 8, cw[2], dvn, "v", pad, "conv_v_bwd")
    dp0 = _assemble_bf16([dpq, dpk, dpv, dza, dbb, daa, dqb, dkb, dvb], "ab_in_dy")
    dwab = _mm(h0b, dp0, ta=True, out_dtype=BF16, name="ab_in_dw")
    if late is None:
        dh0 = _mm(dp0, w["ab_w_in"], tb=True, plus=dh0a, name="ab_in_dx")
    else:
        dab = jnp.transpose(_unpad_ab_cols(dwab).reshape(D, N_CHIP, AB_TRUE // N_CHIP), (1, 0, 2))
        dh0, landed["ab_w_in"] = _mm(dp0, w["ab_w_in"], tb=True, plus=dh0a, name="ab_in_dx", cargo=[dab],
                                     exchange=SCATTER)

    grads = {
        "ab_w_in": dwab, "conv_w": jnp.concatenate([dcq, dck, dcv], axis=1),
        "a_log": dal[:, :GDN_H], "dt_bias": ddt[:, :GDN_H],
        "ab_gnorm_g": dag, "ab_w_out": dwabo, "c_w_in": dwc, "c_lb_raw": dlb, "c_gnorm_g": dcg, "c_w_out": dwco,
        "ln_mix_g": jnp.concatenate([dmg0, dmg1], 0), "ln_mix_b": jnp.concatenate([dmb0, dmb1], 0),
        "w1_0": dw1_0, "w1_1": dw1_1, "w2_0": dw2_0, "w2_1": dw2_1,
        "ln_ffn_g": jnp.concatenate([dfg0, dfg1], 0), "ln_ffn_b": jnp.concatenate([dfb0, dfb1], 0),
        "landed": landed,
    }
    return loss, dh0, grads


MESH = pl.DeviceIdType.MESH
ANY = pl.BlockSpec(memory_space=pl.ANY)
N_CHIP = 4
N_DEV = 8
CHIP_REL = ((1, 0), (0, 1), (1, 1))
DEV_REL = tuple((dx, dy, dc) for dx in (0, 1) for dy in (0, 1) for dc in (0, 1))[1:]

def _pos():
    return lax.axis_index("x"), lax.axis_index("y"), lax.axis_index("c")


def _flip(a, d):
    return a + d - 2 * a * d


class _Exchange:
    def __init__(self, local, sends, recvs):
        self.local, self.sends, self.recvs = local, sends, recvs

    def start(self):
        for cp in self.local + self.sends:
            cp.start()

    def wait(self):
        for cp in self.recvs:
            cp.wait_recv()
        for cp in self.sends:
            cp.wait_send()
        for cp in self.local:
            cp.wait()


def _gather_sems(n):
    return [pltpu.SemaphoreType.DMA((3 * n,)), pltpu.SemaphoreType.DMA((3 * n,)), pltpu.SemaphoreType.DMA((n,))]


def _gather_copies(x_refs, o_refs, send_sems, recv_sems, local_sems):
    n = len(x_refs)
    x, y, c = _pos()
    local = [pltpu.make_async_copy(x_refs[a], o_refs[a].at[2 * x + y], local_sems.at[a]) for a in range(n)]

    def copy(a, k, sending):
        tx, ty = _flip(x, CHIP_REL[k][0]), _flip(y, CHIP_REL[k][1])
        return pltpu.make_async_remote_copy(
            src_ref=x_refs[a], dst_ref=o_refs[a].at[2 * x + y if sending else 2 * tx + ty],
            send_sem=send_sems.at[3 * a + k], recv_sem=recv_sems.at[3 * a + k], device_id=(tx, ty, c), device_id_type=MESH)

    pairs = [(a, k) for a in range(n) for k in range(3)]
    return _Exchange(local, [copy(a, k, True) for a, k in pairs], [copy(a, k, False) for a, k in pairs])


def _gather_shapes(bufs):
    return [jax.ShapeDtypeStruct((N_CHIP,) + b.shape, b.dtype) for b in bufs]


def _chip_allgather(bufs, name):
    n = len(bufs)

    def body(*refs):
        ex = _gather_copies(refs[:n], refs[n:2 * n], *refs[2 * n:])
        ex.start()
        ex.wait()

    return pl.pallas_call(
        body, name=name, in_specs=[ANY] * n, out_specs=[ANY] * n, out_shape=_gather_shapes(bufs),
        scratch_shapes=_gather_sems(n), compiler_params=pltpu.CompilerParams(has_side_effects=True),
    )(*bufs)


def _scatter_sems(n):
    nr = N_DEV - 1
    return [pltpu.SemaphoreType.DMA((nr * n,)), pltpu.SemaphoreType.DMA((nr * n,)), pltpu.SemaphoreType.DMA((n,))]


def _scatter_copies(g_refs, o_refs, send_sems, recv_sems, local_sems):
    n = len(g_refs)
    nr = N_DEV - 1
    x, y, c = _pos()
    me = 4 * x + 2 * y + c
    local = [pltpu.make_async_copy(g_refs[a].at[2 * x + y], o_refs[a].at[me], local_sems.at[a]) for a in range(n)]

    def copy(a, k, sending):
        dx, dy, dc = DEV_REL[k]
        tx, ty, tc = _flip(x, dx), _flip(y, dy), _flip(c, dc)
        return pltpu.make_async_remote_copy(
            src_ref=g_refs[a].at[2 * tx + ty], dst_ref=o_refs[a].at[me if sending else 4 * tx + 2 * ty + tc],
            send_sem=send_sems.at[nr * a + k], recv_sem=recv_sems.at[nr * a + k],
            device_id=(tx, ty, tc), device_id_type=MESH)

    pairs = [(a, k) for a in range(n) for k in range(nr)]
    return _Exchange(local, [copy(a, k, True) for a, k in pairs], [copy(a, k, False) for a, k in pairs])


def _scatter_shapes(gs):
    return [jax.ShapeDtypeStruct((N_DEV,) + g.shape[1:], g.dtype) for g in gs]


GATHER = (_gather_copies, _gather_shapes, _gather_sems)
SCATTER = (_scatter_copies, _scatter_shapes, _scatter_sems)


def _sum_slots(rs, name, small=None):
    n, rh, w = rs[0].shape
    tr = _pick(rh, (256, 128, 64, 16))

    if small is not None:
        nl = len(rs)
        sr, sw = small.shape
        nsteps = rh // tr

        def body_with_rider(*refs):
            b_ref, o_ref, os_ref, land_ref, send_sems, recv_sems = refs[nl:]
            i = pl.program_id(0)

            @pl.when(i == 0)
            def _():
                _allreduce_steps(b_ref, os_ref, land_ref, send_sems, recv_sems, finish=False)

            for layer, r_ref in enumerate(refs[:nl]):
                acc = r_ref[0].astype(F32)
                for s in range(1, n):
                    acc = acc + r_ref[s].astype(F32)
                o_ref[layer] = acc

            @pl.when(i == nsteps - 1)
            def _():
                _allreduce_steps(b_ref, os_ref, land_ref, send_sems, recv_sems, start=False)

        whole = pl.BlockSpec((sr, sw), lambda i: (0, 0))
        return pl.pallas_call(
            body_with_rider, name=name, grid=(nsteps,),
            in_specs=[pl.BlockSpec((n, tr, w), lambda i: (0, i, 0))] * nl + [whole],
            out_specs=[pl.BlockSpec((nl, tr, w), lambda i: (0, i, 0)), whole],
            out_shape=[jax.ShapeDtypeStruct((nl, rh, w), F32), jax.ShapeDtypeStruct((sr, sw), F32)],
            scratch_shapes=[pltpu.VMEM((N_DEV, sr, sw), F32), pltpu.SemaphoreType.DMA((N_DEV - 1,)),
                            pltpu.SemaphoreType.DMA((N_DEV - 1,))],
            compiler_params=_cparams(has_side_effects=True),
        )(*rs, small)

    def body(*refs):
        o_ref = refs[-1]
        for layer, r_ref in enumerate(refs[:-1]):
            acc = r_ref[0].astype(F32)
            for s in range(1, n):
                acc = acc + r_ref[s].astype(F32)
            o_ref[layer] = acc

    return pl.pallas_call(
        body, name=name, grid=(rh // tr,), in_specs=[pl.BlockSpec((n, tr, w), lambda i: (0, i, 0))] * len(rs),
        out_specs=pl.BlockSpec((len(rs), tr, w), lambda i: (0, i, 0)),
        out_shape=jax.ShapeDtypeStruct((len(rs), rh, w), F32), compiler_params=_cparams(),
    )(*rs)


def _small_allreduce(buf, name):
    r, w = buf.shape

    def body(b_ref, o_ref, land_ref, send_sems, recv_sems):
        _allreduce_steps(b_ref, o_ref, land_ref, send_sems, recv_sems)

    vm = pl.BlockSpec(memory_space=pltpu.VMEM)
    return pl.pallas_call(
        body, name=name, in_specs=[vm], out_specs=vm, out_shape=jax.ShapeDtypeStruct((r, w), F32),
        scratch_shapes=[pltpu.VMEM((N_DEV, r, w), F32), pltpu.SemaphoreType.DMA((N_DEV - 1,)),
                        pltpu.SemaphoreType.DMA((N_DEV - 1,))],
        compiler_params=pltpu.CompilerParams(has_side_effects=True),
    )(buf)


def _front_exchange(bufs, small, name):
    n = len(bufs)
    r, w = small.shape

    def body(*refs):
        x_refs, b_ref = refs[:n], refs[n]
        o_refs, os_ref = refs[n + 1:2 * n + 1], refs[2 * n + 1]
        gsend, grecv, glocal, land_ref, send_sems, recv_sems = refs[2 * n + 2:]
        ex = _gather_copies(x_refs, o_refs, gsend, grecv, glocal)
        ex.start()
        _allreduce_steps(b_ref, os_ref, land_ref, send_sems, recv_sems)
        ex.wait()

    vm = pl.BlockSpec(memory_space=pltpu.VMEM)
    return pl.pallas_call(
        body, name=name, in_specs=[ANY] * n + [vm], out_specs=[ANY] * n + [vm],
        out_shape=_gather_shapes(bufs) + [jax.ShapeDtypeStruct((r, w), F32)],
        scratch_shapes=_gather_sems(n) + [pltpu.VMEM((N_DEV, r, w), F32), pltpu.SemaphoreType.DMA((N_DEV - 1,)),
                                          pltpu.SemaphoreType.DMA((N_DEV - 1,))],
        compiler_params=pltpu.CompilerParams(has_side_effects=True),
    )(*bufs, small)


def _allreduce_steps(b_ref, o_ref, land_ref, send_sems, recv_sems, start=True, finish=True):
    x, y, c = _pos()
    me = 4 * x + 2 * y + c

    def target(k):
        dx, dy, dc = DEV_REL[k]
        return _flip(x, dx), _flip(y, dy), _flip(c, dc)

    sends = []
    for k in range(N_DEV - 1):
        tx, ty, tc = target(k)
        sends.append(pltpu.make_async_remote_copy(
            src_ref=b_ref, dst_ref=land_ref.at[me], send_sem=send_sems.at[k], recv_sem=recv_sems.at[k],
            device_id=(tx, ty, tc), device_id_type=MESH))
    if start:
        land_ref[me] = b_ref[...]
        for cp in sends:
            cp.start()
    if finish:
        for k in range(N_DEV - 1):
            tx, ty, tc = target(k)
            pltpu.make_async_remote_copy(
                src_ref=b_ref, dst_ref=land_ref.at[4 * tx + 2 * ty + tc], send_sem=send_sems.at[k],
                recv_sem=recv_sems.at[k], device_id=(tx, ty, tc), device_id_type=MESH).wait_recv()
        for cp in sends:
            cp.wait_send()
        acc = land_ref[0]
        for s in range(1, N_DEV):
            acc = acc + land_ref[s]
        o_ref[...] = acc


def _adamw(w, g, m, v, name):
    r, c = w.shape
    tr = _pick(r, (256, 128, 64, 8)) if r * c > (1 << 18) else r

    def body(w_ref, g_ref, m_ref, v_ref, d_ref, m2_ref, v2_ref):
        gg = g_ref[...]
        m2 = ADAM_B1 * m_ref[...] + (1.0 - ADAM_B1) * gg
        v2 = ADAM_B2 * v_ref[...] + (1.0 - ADAM_B2) * (gg * gg)
        m_hat = m2 / (1.0 - ADAM_B1 ** ADAM_STEP)
        v_hat = v2 / (1.0 - ADAM_B2 ** ADAM_STEP)
        d_ref[...] = -ADAM_LR * (m_hat / (jnp.sqrt(v_hat) + ADAM_EPS) + ADAM_WD * w_ref[...])
        m2_ref[...] = m2
        v2_ref[...] = v2

    blk = pl.BlockSpec((tr, c), lambda i: (i, 0))
    sds = jax.ShapeDtypeStruct((r, c), F32)
    return pl.pallas_call(body, name=name, grid=(r // tr,), in_specs=[blk] * 4, out_specs=[blk] * 3,
                          out_shape=[sds] * 3, compiler_params=_cparams())(w, g, m, v)


BIG = ("ab_w_in", "ab_w_out", "c_w_in", "c_w_out", "mlp_w1", "mlp_w2")
SMALL = ("ln_mix_g", "ln_mix_b", "ln_ffn_g", "ln_ffn_b", "c_lb_raw", "ab_a_log", "ab_dt_bias", "ab_gnorm_g", "c_gnorm_g")
SMALL_ROWS = 16
CONV_ROWS = 8
CONV_W = 3 * GDN_H * HD


def _conv_to_rows(cw):
    return jnp.pad(cw, ((0, 0), (0, 2 * D - CONV_W))).reshape(CONV_ROWS, D)


def _rows_to_conv(rows):
    return rows.reshape(CONV_K, 2 * D)[:, :CONV_W]


def _pack_small(d):
    rows = [jnp.pad(d[n], ((0, 0), (0, D - d[n].shape[1]))) for n in SMALL]
    buf = jnp.concatenate(rows, axis=0)
    return jnp.pad(buf, ((0, SMALL_ROWS - buf.shape[0]), (0, 0)))


def _unpack_small(buf, like):
    out, r = {}, 0
    for n in SMALL:
        nr, nc = like[n].shape
        out[n] = buf[r:r + nr, :nc]
        r += nr
    return out


def kernel(x, meta_tokens, ab_w_in, ab_conv_w, ab_a_log, ab_dt_bias, ab_gnorm_g, ab_w_out, c_w_in, c_lb_raw, c_gnorm_g, c_w_out, ln_mix_g, ln_mix_b, mlp_w1, mlp_w2, ln_ffn_g, ln_ffn_b, loss_target, m_meta_tokens, m_ab_w_in, m_ab_conv_w, m_ab_a_log, m_ab_dt_bias, m_ab_gnorm_g, m_ab_w_out, m_c_w_in, m_c_lb_raw, m_c_gnorm_g, m_c_w_out, m_ln_mix_g, m_ln_mix_b, m_mlp_w1, m_mlp_w2, m_ln_ffn_g, m_ln_ffn_b, v_meta_tokens, v_ab_w_in, v_ab_conv_w, v_ab_a_log, v_ab_dt_bias, v_ab_gnorm_g, v_ab_w_out, v_c_w_in, v_c_lb_raw, v_c_gnorm_g, v_c_w_out, v_ln_mix_g, v_ln_mix_b, v_mlp_w1, v_mlp_w2, v_ln_ffn_g, v_ln_ffn_b):
    names = ("meta_tokens", "ab_w_in", "ab_conv_w", "ab_a_log", "ab_dt_bias", "ab_gnorm_g", "ab_w_out", "c_w_in",
             "c_lb_raw", "c_gnorm_g", "c_w_out", "ln_mix_g", "ln_mix_b", "mlp_w1", "mlp_w2", "ln_ffn_g", "ln_ffn_b")
    wts = dict(zip(names, (meta_tokens, ab_w_in, ab_conv_w, ab_a_log, ab_dt_bias, ab_gnorm_g, ab_w_out, c_w_in, c_lb_raw,
                           c_gnorm_g, c_w_out, ln_mix_g, ln_mix_b, mlp_w1, mlp_w2, ln_ffn_g, ln_ffn_b)))
    mom_m = dict(zip(names, (m_meta_tokens, m_ab_w_in, m_ab_conv_w, m_ab_a_log, m_ab_dt_bias, m_ab_gnorm_g, m_ab_w_out,
                             m_c_w_in, m_c_lb_raw, m_c_gnorm_g, m_c_w_out, m_ln_mix_g, m_ln_mix_b, m_mlp_w1, m_mlp_w2,
                             m_ln_ffn_g, m_ln_ffn_b)))
    mom_v = dict(zip(names, (v_meta_tokens, v_ab_w_in, v_ab_conv_w, v_ab_a_log, v_ab_dt_bias, v_ab_gnorm_g, v_ab_w_out,
                             v_c_w_in, v_c_lb_raw, v_c_gnorm_g, v_c_w_out, v_ln_mix_g, v_ln_mix_b, v_mlp_w1, v_mlp_w2,
                             v_ln_ffn_g, v_ln_ffn_b)))
    seq = x.shape[1]
    pad = (-(N_META + seq)) % QB
    xi, yi, ci = _pos()
    chip = 2 * xi + yi

    late ={"ab_w_out": ab_w_out[0].astype(BF16), "c_w_in": c_w_in.astype(BF16), "c_w_out": c_w_out[0].astype(BF16),
            "mlp_w1": mlp_w1.astype(BF16), "mlp_w2": mlp_w2.astype(BF16)}
    mcols, ccols = meta_tokens.shape[1], ab_conv_w.shape[2]
    place = jnp.concatenate([
        lax.dynamic_update_slice(jnp.zeros((N_META, D), F32), 0.5 * meta_tokens, (0, chip * mcols)),
        _conv_to_rows(lax.dynamic_update_slice(jnp.zeros((CONV_K, CONV_W), F32), 0.5 * ab_conv_w[0], (0, chip * ccols)))],
        axis=0)
    gat_ab_in, placed = _front_exchange([ab_w_in[0].astype(BF16)], place, "gather_front")
    meta_full = placed[:N_META]

    w = {
        "ab_w_in": _pad_ab_cols(jnp.transpose(gat_ab_in, (1, 0, 2)).reshape(D, AB_TRUE)),
        "conv_w": _rows_to_conv(placed[N_META:]), "a_log": ab_a_log, "dt_bias": ab_dt_bias,
        "ab_gnorm_g": ab_gnorm_g, "c_lb_raw": c_lb_raw,
        "c_gnorm_g": c_gnorm_g, "ln_mix_g": ln_mix_g, "ln_mix_b": ln_mix_b, "ln_ffn_g": ln_ffn_g, "ln_ffn_b": ln_ffn_b,
    }

    h0 = jnp.concatenate([jnp.zeros((pad, D), F32), meta_full, x[0]], axis=0)
    tgt = jnp.concatenate([jnp.zeros((pad + N_META, D), F32), loss_target[0]], axis=0)
    loss8, dh0, g = _local_step(h0, tgt, w, pad, late)
    grad_x = dh0[pad + N_META:][None]

    gsmall = {"ln_mix_g": g["ln_mix_g"], "ln_mix_b": g["ln_mix_b"], "ln_ffn_g": g["ln_ffn_g"], "ln_ffn_b": g["ln_ffn_b"],
              "c_lb_raw": g["c_lb_raw"], "ab_a_log": g["a_log"], "ab_dt_bias": g["dt_bias"], "ab_gnorm_g": g["ab_gnorm_g"],
              "c_gnorm_g": g["c_gnorm_g"]}
    packed = _pack_small(gsmall).at[SMALL_ROWS - 1, :loss8.shape[1]].set(loss8[0])
    sbuf = jnp.concatenate([packed, dh0[pad:pad + N_META], _conv_to_rows(g["conv_w"])], axis=0)
    grad_w1, ssum = _sum_slots([g["landed"]["w1_0"], g["landed"]["w1_1"]], "grad_sum_mlp_w1", small=sbuf)
    loss = ssum[SMALL_ROWS - 1, 0]
    grads = _unpack_small(ssum[:SMALL_ROWS], wts)
    grads["mlp_w1"] = grad_w1
    grads["meta_tokens"] = lax.dynamic_slice(ssum[SMALL_ROWS:SMALL_ROWS + N_META], (0, chip * mcols), (N_META, mcols))
    grads["ab_conv_w"] = lax.dynamic_slice(_rows_to_conv(ssum[SMALL_ROWS + N_META:]), (0, chip * ccols), (CONV_K, ccols))[None]

    landed = g["landed"]
    for n in ("ab_w_in", "ab_w_out", "c_w_in", "c_w_out"):
        grads[n] = _sum_slots([landed[n]], f"grad_sum_{n}")
    grads["mlp_w2"] = _sum_slots([landed["w2_0"], landed["w2_1"]], "grad_sum_mlp_w2")

    delta, new_m, new_v = {}, {}, {}
    for n in ("meta_tokens", "ab_conv_w") + BIG:
        shp = wts[n].shape
        to2 = lambda a: a.reshape(-1, shp[-1])
        d2, m2, v2 = _adamw(to2(wts[n]), to2(grads[n]), to2(mom_m[n]), to2(mom_v[n]), f"adamw_{n}")
        delta[n], new_m[n], new_v[n] = d2.reshape(shp), m2.reshape(shp), v2.reshape(shp)
    d2, m2, v2 = _adamw(_pack_small(wts), ssum[:SMALL_ROWS], _pack_small(mom_m), _pack_small(mom_v), "adamw_small")
    delta.update(_unpack_small(d2, wts))
    new_m.update(_unpack_small(m2, wts))
    new_v.update(_unpack_small(v2, wts))

    return (loss, grad_x, *[grads[n] for n in names], *[delta[n] for n in names], *[new_m[n] for n in names],
            *[new_v[n] for n in names])
```
